```python
import math
import jax, jax.numpy as jnp
from jax import lax
import numpy as np

D_MODEL = 1024
BATCH = 8
SEQ = 4096
DEPTH = 1

D_DN = 512
DN_HEADS = 4
DN_HEAD_DIM = 128
CONV_WIDTH = 4
CHUNK = 64
D_ATT = 512
ATT_HEADS = 8
ATT_HEAD_DIM = 64
DILATED_PATTERNS = ((128, 1), (512, 4), (2048, 16))
N_BUCKETS = 32
MAX_DISTANCE = 2048
D_MIX = D_DN + D_ATT
D_IN = 4 * D_DN + 2 * DN_HEADS + 4 * D_ATT
EPS = 1e-6

kernel_name = "hybrid_deltanet_dilated_attention_layer"


def rms_norm(x, w):
    xf = x.astype(jnp.float32)
    return xf * lax.rsqrt(jnp.mean(xf * xf, axis=-1, keepdims=True) + EPS) * w.astype(jnp.float32)


def l2_norm(x):
    return x * lax.rsqrt(jnp.sum(x * x, axis=-1, keepdims=True) + EPS)


def split_heads(t, n_heads):
    b, s, _ = t.shape
    return t.reshape(b, s, n_heads, -1).transpose(0, 2, 1, 3)


def causal_depthwise_conv(u, w):
    k_width = w.shape[0]
    s = u.shape[1]
    up = jnp.pad(u, ((0, 0), (k_width - 1, 0), (0, 0)))
    y = up[:, 0:s] * w[0]
    for i in range(1, k_width):
        y = y + up[:, i:i + s] * w[i]
    return y


def gated_delta_rule(q, k, v, g, beta):
    b, h, s, dk = q.shape
    dv = v.shape[-1]
    n = s // CHUNK
    q = q * (dk ** -0.5)
    qc = q.reshape(b, h, n, CHUNK, dk)
    kc = k.reshape(b, h, n, CHUNK, dk)
    vc = v.reshape(b, h, n, CHUNK, dv)
    bc = beta.reshape(b, h, n, CHUNK)
    gc = jnp.cumsum(g.reshape(b, h, n, CHUNK), axis=-1)
    tril_incl = np.tril(np.ones((CHUNK, CHUNK), dtype=bool))
    tril_strict = np.tril(np.ones((CHUNK, CHUNK), dtype=bool), -1)
    diff = gc[..., :, None] - gc[..., None, :]
    decay = jnp.exp(jnp.where(tril_incl, diff, -jnp.inf))
    kb = kc * bc[..., None]
    a_mat = jnp.where(tril_strict, jnp.einsum('bhnid,bhnjd->bhnij', kb, kc) * decay, 0.0)
    l_mat = a_mat + jnp.eye(CHUNK, dtype=a_mat.dtype)
    u = lax.linalg.triangular_solve(l_mat, vc * bc[..., None], left_side=True, lower=True, unit_diagonal=True)
    w = lax.linalg.triangular_solve(l_mat, kb * jnp.exp(gc)[..., None], left_side=True, lower=True, unit_diagonal=True)
    attn_intra = jnp.einsum('bhnid,bhnjd->bhnij', qc, kc) * decay
    q_dec = qc * jnp.exp(gc)[..., None]
    k_tail = kc * jnp.exp(gc[..., -1:] - gc)[..., None]
    g_last = jnp.exp(gc[..., -1])
    xs = tuple(jnp.moveaxis(t, 2, 0) for t in (attn_intra, q_dec, k_tail, u, w, g_last))

    def step(state, inp):
        a_i, qd_i, kt_i, u_i, w_i, gl_i = inp
        v_new = u_i - jnp.einsum('bhck,bhkv->bhcv', w_i, state)
        o_i = jnp.einsum('bhck,bhkv->bhcv', qd_i, state) + jnp.einsum('bhij,bhjv->bhiv', a_i, v_new)
        state = state * gl_i[..., None, None] + jnp.einsum('bhck,bhcv->bhkv', kt_i, v_new)
        return state, o_i

    s0 = jnp.zeros((b, h, dk, dv), jnp.float32)
    _, o = lax.scan(step, s0, xs)
    return jnp.moveaxis(o, 0, 2).reshape(b, h, s, dv)


def t5_bucket(dist):
    max_exact = N_BUCKETS // 2
    d = np.maximum(dist, 1).astype(np.float64)
    large = max_exact + (np.log(d / max_exact) / math.log(MAX_DISTANCE / max_exact)
                         * (N_BUCKETS - max_exact)).astype(np.int32)
    large = np.minimum(large, N_BUCKETS - 1)
    return np.where(dist < max_exact, dist, large).astype(np.int32)


def dilated_pattern(q, k, v, rel_bias, window, dilation):
    b, h, s, hd = q.shape
    r = dilation
    l_sub = s // r
    w_steps = window // r
    blk = w_steps
    n_blk = -(-l_sub // blk)
    l_pad = n_blk * blk

    def to_blocks(t):
        t = t.reshape(b, h, l_sub, r, hd).transpose(0, 1, 3, 2, 4)
        t = jnp.pad(t, ((0, 0), (0, 0), (0, 0), (0, l_pad - l_sub), (0, 0)))
        return t.reshape(b, h, r, n_blk, blk, hd)

    def with_prev(t):
        prev = jnp.pad(t, ((0, 0), (0, 0), (0, 0), (1, 0), (0, 0), (0, 0)))[:, :, :, :-1]
        return jnp.concatenate([prev, t], axis=4)

    qb = to_blocks(q)
    kw = with_prev(to_blocks(k))
    vw = with_prev(to_blocks(v))
    qi = np.arange(blk)[:, None]
    kj = np.arange(2 * blk)[None, :]
    step = qi - kj + blk
    band = (step >= 0) & (step <= w_steps)
    key_idx = np.arange(n_blk)[:, None, None] * blk + kj[None] - blk
    mask = band[None] & (key_idx >= 0)
    buckets = t5_bucket(np.clip(step, 0, None) * r)
    bias = rel_bias.astype(jnp.float32)[:, buckets]
    scores = jnp.einsum('bhrnqd,bhrnkd->bhrnqk', qb, kw).astype(jnp.float32) + bias[:, None, None]
    scores = jnp.where(mask, scores, -jnp.inf)
    lse = jax.nn.logsumexp(scores, axis=-1)
    p = jnp.exp(scores - lse[..., None])
    o = jnp.einsum('bhrnqk,bhrnkd->bhrnqd', p, vw.astype(jnp.float32))
    o = o.reshape(b, h, r, l_pad, hd)[:, :, :, :l_sub].transpose(0, 1, 3, 2, 4).reshape(b, h, s, hd)
    lse = lse.reshape(b, h, r, l_pad)[:, :, :, :l_sub].transpose(0, 1, 3, 2).reshape(b, h, s)
    return o, lse


def deltanet_branch(qkv, z, b_proj, a_proj, conv_w, a_log, dt_bias, dn_norm_w):
    bsz, s, _ = qkv.shape
    qkv = jax.nn.silu(causal_depthwise_conv(qkv.astype(jnp.float32), conv_w.astype(jnp.float32)))
    q, k, v = jnp.split(qkv, 3, axis=-1)
    q = l2_norm(split_heads(q, DN_HEADS))
    k = l2_norm(split_heads(k, DN_HEADS))
    v = split_heads(v, DN_HEADS)
    beta = jax.nn.sigmoid(b_proj.astype(jnp.float32)).transpose(0, 2, 1)
    g = -jnp.exp(a_log.astype(jnp.float32)) * jax.nn.softplus(a_proj.astype(jnp.float32) + dt_bias.astype(jnp.float32))
    g = g.transpose(0, 2, 1)
    o = gated_delta_rule(q, k, v, g, beta)
    o = rms_norm(o, dn_norm_w).transpose(0, 2, 1, 3).reshape(bsz, s, D_DN)
    return o * jax.nn.silu(z.astype(jnp.float32))


def dilated_attention_branch(qkv, gate, q_norm_w, k_norm_w, rel_bias):
    bsz, s, _ = qkv.shape
    q, k, v = jnp.split(qkv, 3, axis=-1)
    q = rms_norm(split_heads(q, ATT_HEADS), q_norm_w) * (ATT_HEAD_DIM ** -0.5)
    k = rms_norm(split_heads(k, ATT_HEADS), k_norm_w)
    v = split_heads(v, ATT_HEADS).astype(jnp.float32)
    outs, lses = [], []
    for window, dilation in DILATED_PATTERNS:
        o_p, lse_p = dilated_pattern(q, k, v, rel_bias, window, dilation)
        outs.append(o_p)
        lses.append(lse_p)
    wts = jax.nn.softmax(jnp.stack(lses), axis=0)
    o = jnp.sum(wts[..., None] * jnp.stack(outs), axis=0)
    o = o.transpose(0, 2, 1, 3).reshape(bsz, s, D_ATT)
    return o * jax.nn.silu(gate.astype(jnp.float32))


def hybrid_layer(x, norm_w, w_in, conv_w, a_log, dt_bias, dn_norm_w, q_norm_w, k_norm_w, rel_bias, w_out):
    h = rms_norm(x, norm_w).astype(x.dtype)
    proj = h @ w_in
    cuts = [3 * D_DN, 4 * D_DN, 4 * D_DN + DN_HEADS, 4 * D_DN + 2 * DN_HEADS,
            4 * D_DN + 2 * DN_HEADS + 3 * D_ATT]
    qkv_dn, z_dn, b_dn, a_dn, qkv_att, gate_att = jnp.split(proj, cuts, axis=-1)
    y_dn = deltanet_branch(qkv_dn, z_dn, b_dn, a_dn, conv_w, a_log, dt_bias, dn_norm_w)
    y_att = dilated_attention_branch(qkv_att, gate_att, q_norm_w, k_norm_w, rel_bias)
    mixed = jnp.concatenate([y_dn, y_att], axis=-1).astype(x.dtype)
    return x + mixed @ w_out


def _fwd_setup_inputs(seed: int = 0) -> dict:
    key = jax.random.key(seed)
    ks = jax.random.split(key, 11)
    x = jax.random.normal(ks[0], (BATCH, SEQ, D_MODEL), jnp.float32)
    norm_w = 1.0 + 0.1 * jax.random.normal(ks[1], (DEPTH, D_MODEL), jnp.float32)
    w_in = jax.random.normal(ks[2], (DEPTH, D_MODEL, D_IN), jnp.float32) * (D_MODEL ** -0.5)
    conv_w = jax.random.normal(ks[3], (DEPTH, CONV_WIDTH, 3 * D_DN), jnp.float32) * (CONV_WIDTH ** -0.5)
    a_log = jnp.log(jax.random.uniform(ks[4], (DEPTH, DN_HEADS), jnp.float32, minval=1.0, maxval=16.0))
    dt = jnp.exp(jax.random.uniform(ks[5], (DEPTH, DN_HEADS), jnp.float32,
                                    minval=math.log(1e-3), maxval=math.log(1e-1)))
    dt_bias = dt + jnp.log(-jnp.expm1(-dt))
    dn_norm_w = 1.0 + 0.1 * jax.random.normal(ks[6], (DEPTH, DN_HEAD_DIM), jnp.float32)
    q_norm_w = 1.0 + 0.1 * jax.random.normal(ks[7], (DEPTH, ATT_HEAD_DIM), jnp.float32)
    k_norm_w = 1.0 + 0.1 * jax.random.normal(ks[8], (DEPTH, ATT_HEAD_DIM), jnp.float32)
    rel_bias = 0.5 * jax.random.normal(ks[9], (ATT_HEADS, N_BUCKETS), jnp.float32)
    w_out = jax.random.normal(ks[10], (DEPTH, D_MIX, D_MODEL), jnp.float32) * (D_MIX ** -0.5)
    return {"x": x, "norm_w": norm_w, "w_in": w_in, "conv_w": conv_w, "a_log": a_log,
            "dt_bias": dt_bias, "dn_norm_w": dn_norm_w, "q_norm_w": q_norm_w,
            "k_norm_w": k_norm_w, "rel_bias": rel_bias, "w_out": w_out}


def _fwd_reference(x, norm_w, w_in, conv_w, a_log, dt_bias, dn_norm_w, q_norm_w, k_norm_w, rel_bias, w_out):
    for layer in range(DEPTH):
        x = hybrid_layer(x, norm_w[layer], w_in[layer], conv_w[layer], a_log[layer], dt_bias[layer],
                         dn_norm_w[layer], q_norm_w[layer], k_norm_w[layer], rel_bias, w_out[layer])
    return x


import jax as _jax
import jax.numpy as _jnp

TWIN_FORMAT = 'train_step'
FWD_PARAMS = ['x', 'norm_w', 'w_in', 'conv_w', 'a_log', 'dt_bias', 'dn_norm_w', 'q_norm_w', 'k_norm_w', 'rel_bias', 'w_out']
TWIN_WEIGHTS = ['norm_w', 'w_in', 'conv_w', 'a_log', 'dt_bias', 'dn_norm_w', 'q_norm_w', 'k_norm_w', 'rel_bias', 'w_out']
TWIN_DIFF_INPUT = 'x'
TWIN_INPUTS = ['x', 'norm_w', 'w_in', 'conv_w', 'a_log', 'dt_bias', 'dn_norm_w', 'q_norm_w', 'k_norm_w', 'rel_bias', 'w_out', 'loss_target', 'm_norm_w', 'm_w_in', 'm_conv_w', 'm_a_log', 'm_dt_bias', 'm_dn_norm_w', 'm_q_norm_w', 'm_k_norm_w', 'm_rel_bias', 'm_w_out', 'v_norm_w', 'v_w_in', 'v_conv_w', 'v_a_log', 'v_dt_bias', 'v_dn_norm_w', 'v_q_norm_w', 'v_k_norm_w', 'v_rel_bias', 'v_w_out']
TWIN_OUTPUTS = ['loss', 'grad_x', 'grad_norm_w', 'grad_w_in', 'grad_conv_w', 'grad_a_log', 'grad_dt_bias', 'grad_dn_norm_w', 'grad_q_norm_w', 'grad_k_norm_w', 'grad_rel_bias', 'grad_w_out', 'delta_norm_w', 'delta_w_in', 'delta_conv_w', 'delta_a_log', 'delta_dt_bias', 'delta_dn_norm_w', 'delta_q_norm_w', 'delta_k_norm_w', 'delta_rel_bias', 'delta_w_out', 'new_m_norm_w', 'new_m_w_in', 'new_m_conv_w', 'new_m_a_log', 'new_m_dt_bias', 'new_m_dn_norm_w', 'new_m_q_norm_w', 'new_m_k_norm_w', 'new_m_rel_bias', 'new_m_w_out', 'new_v_norm_w', 'new_v_w_in', 'new_v_conv_w', 'new_v_a_log', 'new_v_dt_bias', 'new_v_dn_norm_w', 'new_v_q_norm_w', 'new_v_k_norm_w', 'new_v_rel_bias', 'new_v_w_out']
TWIN_LEAF_KINDS = {'loss': 'loss', 'grad_x': 'grad_x', 'grad_norm_w': 'grad_w', 'grad_w_in': 'grad_w', 'grad_conv_w': 'grad_w', 'grad_a_log': 'grad_w', 'grad_dt_bias': 'grad_w', 'grad_dn_norm_w': 'grad_w', 'grad_q_norm_w': 'grad_w', 'grad_k_norm_w': 'grad_w', 'grad_rel_bias': 'grad_w', 'grad_w_out': 'grad_w', 'delta_norm_w': 'delta_w', 'delta_w_in': 'delta_w', 'delta_conv_w': 'delta_w', 'delta_a_log': 'delta_w', 'delta_dt_bias': 'delta_w', 'delta_dn_norm_w': 'delta_w', 'delta_q_norm_w': 'delta_w', 'delta_k_norm_w': 'delta_w', 'delta_rel_bias': 'delta_w', 'delta_w_out': 'delta_w', 'new_m_norm_w': 'new_m', 'new_m_w_in': 'new_m', 'new_m_conv_w': 'new_m', 'new_m_a_log': 'new_m', 'new_m_dt_bias': 'new_m', 'new_m_dn_norm_w': 'new_m', 'new_m_q_norm_w': 'new_m', 'new_m_k_norm_w': 'new_m', 'new_m_rel_bias': 'new_m', 'new_m_w_out': 'new_m', 'new_v_norm_w': 'new_v', 'new_v_w_in': 'new_v', 'new_v_conv_w': 'new_v', 'new_v_a_log': 'new_v', 'new_v_dt_bias': 'new_v', 'new_v_dn_norm_w': 'new_v', 'new_v_q_norm_w': 'new_v', 'new_v_k_norm_w': 'new_v', 'new_v_rel_bias': 'new_v', 'new_v_w_out': 'new_v'}


def _forward(args):
    return _fwd_reference(*[args[k] for k in FWD_PARAMS])


def _output_shape():
    out = _jax.eval_shape(lambda: _forward(_fwd_setup_inputs(0)))
    return out.shape, out.dtype

N_MICROBATCH = 1
ADAM_LR = 0.001
ADAM_B1 = 0.9
ADAM_B2 = 0.999
ADAM_EPS = 1e-08
ADAM_WD = 0.01
ADAM_STEP = 10
PER_EXAMPLE_BATCH_AXIS = {'x': 0, 'loss_target': 0}
SHARED_INPUTS = []
_WEIGHT_DTYPES = {'norm_w': _jnp.float32, 'w_in': _jnp.float32, 'conv_w': _jnp.float32, 'a_log': _jnp.float32, 'dt_bias': _jnp.float32, 'dn_norm_w': _jnp.float32, 'q_norm_w': _jnp.float32, 'k_norm_w': _jnp.float32, 'rel_bias': _jnp.float32, 'w_out': _jnp.float32}
MOMENT_SCALE = {'norm_w': 6.391838e+00, 'w_in': 2.229179e-01, 'conv_w': 5.970725e-01, 'a_log': 4.196334e+01, 'dt_bias': 3.948620e+01, 'dn_norm_w': 4.762099e+01, 'q_norm_w': 7.079349e-01, 'k_norm_w': 7.147470e-01, 'rel_bias': 1.504346e-01, 'w_out': 3.195314e-01}


def _to_microbatches(a, axis):
    t = _jnp.moveaxis(a, axis, 0)
    t = t.reshape((N_MICROBATCH, t.shape[0] // N_MICROBATCH) + t.shape[1:])
    return _jnp.moveaxis(t, 1, axis + 1)


def setup_inputs(seed: int = 0) -> dict:
    inp = _fwd_setup_inputs(seed)
    key = _jax.random.fold_in(_jax.random.key(seed), 7919)
    shape, _ = _output_shape()
    out = dict(inp)
    out["loss_target"] = _jax.random.normal(_jax.random.fold_in(key, 0), shape, _jnp.float32)
    for i, name in enumerate(TWIN_WEIGHTS):
        w = inp[name].astype(_jnp.float32)
        if MOMENT_SCALE is None:
            s = _jnp.sqrt(_jnp.mean(_jnp.square(w)) + 1e-30)
        else:
            s = MOMENT_SCALE[name]
        km, kv = _jax.random.split(_jax.random.fold_in(key, i + 1))
        out[name] = w
        out["m_" + name] = s * _jax.random.normal(km, w.shape, _jnp.float32)
        out["v_" + name] = (s * s) * _jax.random.uniform(kv, w.shape, _jnp.float32, 0.5, 1.5)
    if N_MICROBATCH > 1:
        for name, axis in PER_EXAMPLE_BATCH_AXIS.items():
            out[name] = _to_microbatches(out[name], axis)
    return {'x': out['x'], 'norm_w': out['norm_w'], 'w_in': out['w_in'], 'conv_w': out['conv_w'], 'a_log': out['a_log'], 'dt_bias': out['dt_bias'], 'dn_norm_w': out['dn_norm_w'], 'q_norm_w': out['q_norm_w'], 'k_norm_w': out['k_norm_w'], 'rel_bias': out['rel_bias'], 'w_out': out['w_out'], 'loss_target': out['loss_target'], 'm_norm_w': out['m_norm_w'], 'm_w_in': out['m_w_in'], 'm_conv_w': out['m_conv_w'], 'm_a_log': out['m_a_log'], 'm_dt_bias': out['m_dt_bias'], 'm_dn_norm_w': out['m_dn_norm_w'], 'm_q_norm_w': out['m_q_norm_w'], 'm_k_norm_w': out['m_k_norm_w'], 'm_rel_bias': out['m_rel_bias'], 'm_w_out': out['m_w_out'], 'v_norm_w': out['v_norm_w'], 'v_w_in': out['v_w_in'], 'v_conv_w': out['v_conv_w'], 'v_a_log': out['v_a_log'], 'v_dt_bias': out['v_dt_bias'], 'v_dn_norm_w': out['v_dn_norm_w'], 'v_q_norm_w': out['v_q_norm_w'], 'v_k_norm_w': out['v_k_norm_w'], 'v_rel_bias': out['v_rel_bias'], 'v_w_out': out['v_w_out']}


def _loss(weights, diff, rest, loss_target):
    with _jax.named_scope("forward"):
        args = {**rest, TWIN_DIFF_INPUT: diff, **{k: w.astype(_WEIGHT_DTYPES[k]) for k, w in weights.items()}}
        y = _forward(args)
    with _jax.named_scope("loss_head"):
        err = _jnp.square(y.astype(_jnp.float32) - loss_target)
        return 0.5 * _jnp.sum(_jnp.mean(err, axis=-1)) if err.ndim else 0.5 * err


def _adamw(w, g, m, v):
    m = ADAM_B1 * m + (1.0 - ADAM_B1) * g
    v = ADAM_B2 * v + (1.0 - ADAM_B2) * _jnp.square(g)
    m_hat = m / (1.0 - ADAM_B1 ** ADAM_STEP)
    v_hat = v / (1.0 - ADAM_B2 ** ADAM_STEP)
    delta = -ADAM_LR * (m_hat / (_jnp.sqrt(v_hat) + ADAM_EPS) + ADAM_WD * w)
    return delta, m, v


def reference(x, norm_w, w_in, conv_w, a_log, dt_bias, dn_norm_w, q_norm_w, k_norm_w, rel_bias, w_out, loss_target, m_norm_w, m_w_in, m_conv_w, m_a_log, m_dt_bias, m_dn_norm_w, m_q_norm_w, m_k_norm_w, m_rel_bias, m_w_out, v_norm_w, v_w_in, v_conv_w, v_a_log, v_dt_bias, v_dn_norm_w, v_q_norm_w, v_k_norm_w, v_rel_bias, v_w_out):
    given = dict(x=x, norm_w=norm_w, w_in=w_in, conv_w=conv_w, a_log=a_log, dt_bias=dt_bias, dn_norm_w=dn_norm_w, q_norm_w=q_norm_w, k_norm_w=k_norm_w, rel_bias=rel_bias, w_out=w_out, loss_target=loss_target, m_norm_w=m_norm_w, m_w_in=m_w_in, m_conv_w=m_conv_w, m_a_log=m_a_log, m_dt_bias=m_dt_bias, m_dn_norm_w=m_dn_norm_w, m_q_norm_w=m_q_norm_w, m_k_norm_w=m_k_norm_w, m_rel_bias=m_rel_bias, m_w_out=m_w_out, v_norm_w=v_norm_w, v_w_in=v_w_in, v_conv_w=v_conv_w, v_a_log=v_a_log, v_dt_bias=v_dt_bias, v_dn_norm_w=v_dn_norm_w, v_q_norm_w=v_q_norm_w, v_k_norm_w=v_k_norm_w, v_rel_bias=v_rel_bias, v_w_out=v_w_out)
    weights = {n: given[n] for n in TWIN_WEIGHTS}
    shared = {n: given[n] for n in SHARED_INPUTS}
    per_example = {n: given[n] for n in ['x']}
    grad_fn = _jax.value_and_grad(_loss, argnums=(0, 1))

    def one_microbatch(ex, loss_target):
        ex = dict(ex)
        diff = ex.pop(TWIN_DIFF_INPUT)
        return grad_fn(weights, diff, {**shared, **ex}, loss_target)

    if N_MICROBATCH == 1:
        loss, (grad_w, grad_x) = one_microbatch(per_example, given["loss_target"])
    else:
        def body(carry, xs):
            loss_sum, grad_sum = carry
            l_k, (gw_k, gx_k) = one_microbatch(xs[0], xs[1])
            with _jax.named_scope("update"):
                return (loss_sum + l_k, _jax.tree.map(_jnp.add, grad_sum, gw_k)), gx_k

        init = (_jnp.zeros((), _jnp.float32), _jax.tree.map(_jnp.zeros_like, weights))
        (loss, grad_w), grad_x = _jax.lax.scan(body, init, (per_example, given["loss_target"]))
    with _jax.named_scope("update"):
        delta_w, new_m, new_v = {}, {}, {}
        for n in TWIN_WEIGHTS:
            delta_w[n], new_m[n], new_v[n] = _adamw(weights[n], grad_w[n], given["m_" + n], given["v_" + n])
    return (loss, grad_x, *[grad_w[n] for n in TWIN_WEIGHTS], *[delta_w[n] for n in TWIN_WEIGHTS],
            *[new_m[n] for n in TWIN_WEIGHTS], *[new_v[n] for n in TWIN_WEIGHTS])
```

```python
import functools
import math

import numpy as np
import jax
import jax.numpy as jnp
from jax import lax
from jax.experimental import pallas as pl
from jax.experimental.pallas import tpu as pltpu

F32 = jnp.float32
BF16 = jnp.bfloat16
HI = lax.Precision.HIGHEST

D_MODEL = 1024
D_DN = 512
DN_HEADS = 4
DN_HD = 128
CONV_W = 4
CHUNK = 64
D_ATT = 512
ATT_HEADS = 8
ATT_HD = 64
PATTERNS = ((128, 1), (512, 4), (2048, 16))
N_BUCKETS = 32
MAX_DISTANCE = 2048
D_IN = 4 * D_DN + 2 * DN_HEADS + 4 * D_ATT
D_IN_PAD = 4224
EPS = 1e-6
BLK = 128
NEG = -1e30
N_CHIPS = 4

ADAM_LR = 0.001
ADAM_B1 = 0.9
ADAM_B2 = 0.999
ADAM_EPS = 1e-08
ADAM_WD = 0.01
ADAM_STEP = 10

VMEM_LIMIT = 56 * 1024 * 1024

COL_Z = 3
COL_ATT_Q = 4
COL_ATT_K = 5
COL_ATT_V = 6
COL_GATE = 7
COL_BA_128 = 32


def _cp(sem=None):
    if sem is None:
        return pltpu.CompilerParams(vmem_limit_bytes=VMEM_LIMIT)
    return pltpu.CompilerParams(dimension_semantics=sem, vmem_limit_bytes=VMEM_LIMIT)


def _sds(shape, dtype=F32):
    return jax.ShapeDtypeStruct(shape, dtype)


def _mm(a, b):
    return jnp.dot(a.astype(BF16), b.astype(BF16), preferred_element_type=F32)


def _mm_nt(a, b):
    return lax.dot_general(a.astype(BF16), b.astype(BF16), (((1,), (1,)), ((), ())),
                           preferred_element_type=F32)


def _mm_tn(a, b):
    return lax.dot_general(a.astype(BF16), b.astype(BF16), (((0,), (0,)), ((), ())),
                           preferred_element_type=F32)


def _mmx(a, b):
    return jnp.dot(a, b, precision=HI, preferred_element_type=F32)


def _mmx_nt(a, b):
    return lax.dot_general(a, b, (((1,), (1,)), ((), ())), precision=HI, preferred_element_type=F32)


def _mmx_tn(a, b):
    return lax.dot_general(a, b, (((0,), (0,)), ((), ())), precision=HI, preferred_element_type=F32)


def _segsum(x, bd):
    hi = x.astype(BF16)
    r1 = x - hi.astype(F32)
    mid = r1.astype(BF16)
    lo = (r1 - mid.astype(F32)).astype(BF16)
    return (jnp.dot(hi, bd, preferred_element_type=F32) + jnp.dot(mid, bd, preferred_element_type=F32)
            + jnp.dot(lo, bd, preferred_element_type=F32))


def _sigmoid(x):
    return 1.0 / (1.0 + jnp.exp(-x))


def _silu_grad(x, s):
    return s * (1.0 + x * (1.0 - s))


def _block_ones(n, seg):
    i = np.arange(n)
    return jnp.asarray((i[:, None] // seg == i[None, :] // seg).astype(np.float32), dtype=BF16)


def _tri_incl():
    i = np.arange(CHUNK)
    return jnp.asarray((i[:, None] >= i[None, :]).astype(np.float32))


def _t5_bucket(dist):
    max_exact = N_BUCKETS // 2
    d = np.maximum(dist, 1).astype(np.float64)
    large = max_exact + (np.log(d / max_exact) / math.log(MAX_DISTANCE / max_exact)
                         * (N_BUCKETS - max_exact)).astype(np.int32)
    large = np.minimum(large, N_BUCKETS - 1)
    return np.where(dist < max_exact, dist, large).astype(np.int32)


def _bucket_tables():
    qi = np.arange(BLK)[:, None]
    kj = np.arange(2 * BLK)[None, :]
    step = qi - kj + BLK
    return jnp.asarray(np.stack([_t5_bucket(np.clip(step, 0, None) * r) for _, r in PATTERNS]))


def _in_proj(x, norm_w, w_bf):
    s = x.shape[0]
    tm = 256

    def body(x_ref, nw_ref, w_ref, o_ref):
        xv = x_ref[...]
        rstd = lax.rsqrt(jnp.mean(xv * xv, axis=-1, keepdims=True) + EPS)
        h = (xv * rstd * nw_ref[...]).astype(BF16)
        o_ref[...] = jnp.dot(h, w_ref[...], preferred_element_type=F32)

    return pl.pallas_call(
        body, grid=(s // tm,),
        in_specs=[pl.BlockSpec((tm, D_MODEL), lambda i: (i, 0)),
                  pl.BlockSpec((1, D_MODEL), lambda i: (0, 0)),
                  pl.BlockSpec((D_MODEL, D_IN_PAD), lambda i: (0, 0))],
        out_specs=pl.BlockSpec((tm, D_IN_PAD), lambda i: (i, 0)),
        out_shape=_sds((s, D_IN_PAD)), compiler_params=_cp(("parallel",)), name="in_proj")(x, norm_w, w_bf)


def _conv_group(cur, halo, w_ref, c):
    rows = cur.shape[0]
    lanes = slice(128 * c, 128 * c + 128)
    xcat = jnp.concatenate([halo, cur], axis=0)
    y = cur * w_ref[CONV_W - 1:CONV_W, lanes]
    for k in range(1, CONV_W):
        sh = pltpu.roll(xcat, k, 0)[8:8 + rows]
        y = y + sh * w_ref[CONV_W - 1 - k:CONV_W - k, lanes]
    return y


def _beta_g(ba, alog_l, dtb_l):
    lane = lax.broadcasted_iota(jnp.int32, ba.shape, 1)
    sig_b = _sigmoid(ba)
    t = ba + dtb_l
    softplus = jnp.maximum(t, 0.0) + jnp.log(1.0 + jnp.exp(-jnp.abs(t)))
    nega = -jnp.exp(alog_l)
    g = nega * softplus
    out = jnp.where(lane < DN_HEADS, sig_b, jnp.where(lane < 2 * DN_HEADS, g, 0.0))
    return out, lane, sig_b, t, nega, g


def _dn_pre(proj, conv_w8, alog_l, dtb_l):
    s = proj.shape[0]
    tr = 512
    nh = tr // 8

    def body(u_ref, halo_ref, ba_ref, w_ref, al_ref, dt_ref, q_ref, k_ref, v_ref, bg_ref):
        i = pl.program_id(0)
        keep = (i > 0).astype(F32)
        for c in range(12):
            lanes = slice(128 * c, 128 * c + 128)
            y = _conv_group(u_ref[:, lanes], halo_ref[:, lanes] * keep, w_ref, c)
            sv = y * _sigmoid(y)
            if c < 8:
                rs = lax.rsqrt(jnp.sum(sv * sv, axis=1, keepdims=True) + EPS)
                n = sv * rs
                if c < 4:
                    q_ref[:, lanes] = n * (DN_HD ** -0.5)
                else:
                    k_ref[:, slice(128 * (c - 4), 128 * (c - 3))] = n
            else:
                v_ref[:, slice(128 * (c - 8), 128 * (c - 7))] = sv
        bg_ref[...] = _beta_g(ba_ref[...], al_ref[...], dt_ref[...])[0]

    return pl.pallas_call(
        body, grid=(s // tr,),
        in_specs=[pl.BlockSpec((tr, 1536), lambda i: (i, 0)),
                  pl.BlockSpec((8, 1536), lambda i: (jnp.maximum(i * nh - 1, 0), 0)),
                  pl.BlockSpec((tr, 128), lambda i: (i, COL_BA_128)),
                  pl.BlockSpec((8, 1536), lambda i: (0, 0)),
                  pl.BlockSpec((1, 128), lambda i: (0, 0)),
                  pl.BlockSpec((1, 128), lambda i: (0, 0))],
        out_specs=[pl.BlockSpec((tr, 512), lambda i: (i, 0))] * 3 + [pl.BlockSpec((tr, 128), lambda i: (i, 0))],
        out_shape=[_sds((s, 512))] * 3 + [_sds((s, 128))],
        compiler_params=_cp(("parallel",)), name="dn_pre")(proj, proj, proj, conv_w8, alog_l, dtb_l)


def _chunk_common(bg, bgt, tri, h):
    gc_col = _mmx(tri, bg)
    gc_row = _mmx_nt(bgt, tri)
    gcc = gc_col[:, DN_HEADS + h:DN_HEADS + h + 1]
    gcr = gc_row[DN_HEADS + h:DN_HEADS + h + 1, :]
    beta = bg[:, h:h + 1]
    ii = lax.broadcasted_iota(jnp.int32, (CHUNK, CHUNK), 0)
    jj = lax.broadcasted_iota(jnp.int32, (CHUNK, CHUNK), 1)
    incl = ii >= jj
    strict = ii > jj
    decay = jnp.exp(jnp.where(incl, gcc - gcr, NEG))
    gl = gcc[CHUNK - 1:CHUNK, :]
    return gcc, beta, incl, strict, decay, gl


def _dn_prep(qn, kn, v, bg, bgt, tri):
    s = qn.shape[0]
    nc = s // CHUNK

    def body(q_ref, k_ref, v_ref, bg_ref, bgt_ref, tri_ref,
             u_ref, w_ref, qd_ref, kt_ref, attn_ref, t_ref, gl_ref):
        tri_v = tri_ref[...]
        bgv = bg_ref[...]
        bgtv = bgt_ref[0]
        ii = lax.broadcasted_iota(jnp.int32, (CHUNK, CHUNK), 0)
        jj = lax.broadcasted_iota(jnp.int32, (CHUNK, CHUNK), 1)
        eye = (ii == jj).astype(F32)
        for h in range(DN_HEADS):
            lanes = slice(128 * h, 128 * h + 128)
            gcc, beta, incl, strict, decay, gl = _chunk_common(bgv, bgtv, tri_v, h)
            q = q_ref[:, lanes]
            k = k_ref[:, lanes]
            vv = v_ref[:, lanes]
            kb = k * beta
            a_mat = jnp.where(strict, _mm_nt(kb, k) * decay, 0.0)
            xm = -a_mat
            t = eye + xm
            p = xm
            for _ in range(5):
                p = _mmx(p, p)
                t = t + _mmx(t, p)
            egc = jnp.exp(gcc)
            u_ref[:, lanes] = _mmx(t, vv * beta)
            w_ref[:, lanes] = _mmx(t, kb * egc)
            attn_ref[0, h] = jnp.where(incl, _mm_nt(q, k) * decay, 0.0)
            t_ref[0, h] = t
            qd_ref[:, lanes] = q * egc
            kt_ref[:, lanes] = k * jnp.exp(gl - gcc)
            gl_ref[0, h] = jnp.broadcast_to(jnp.exp(gl), (1, 128))

    big = pl.BlockSpec((CHUNK, 512), lambda n: (n, 0))
    sq = pl.BlockSpec((1, DN_HEADS, CHUNK, CHUNK), lambda n: (n, 0, 0, 0))
    return pl.pallas_call(
        body, grid=(nc,),
        in_specs=[big, big, big, pl.BlockSpec((CHUNK, 128), lambda n: (n, 0)),
                  pl.BlockSpec((1, 8, CHUNK), lambda n: (n, 0, 0)),
                  pl.BlockSpec((CHUNK, CHUNK), lambda n: (0, 0))],
        out_specs=[big, big, big, big, sq, sq, pl.BlockSpec((1, DN_HEADS, 1, 128), lambda n: (n, 0, 0, 0))],
        out_shape=[_sds((s, 512))] * 4 + [_sds((nc, DN_HEADS, CHUNK, CHUNK))] * 2 + [_sds((nc, DN_HEADS, 1, 128))],
        compiler_params=_cp(("parallel",)), name="dn_prep")(qn, kn, v, bg, bgt, tri)


def _dn_scan(u, w, qd, kt, attn, gl):
    s = u.shape[0]
    nc = s // CHUNK

    def body(u_ref, w_ref, qd_ref, kt_ref, attn_ref, gl_ref, o_ref, vn_ref, sp_ref, st_ref):
        n = pl.program_id(0)

        @pl.when(n == 0)
        def _():
            st_ref[...] = jnp.zeros_like(st_ref)

        for h in range(DN_HEADS):
            lanes = slice(128 * h, 128 * h + 128)
            st = st_ref[h]
            sp_ref[0, h] = st
            vn = u_ref[:, lanes] - _mm(w_ref[:, lanes], st)
            vn_ref[:, lanes] = vn
            o_ref[:, lanes] = _mm(qd_ref[:, lanes], st) + _mm(attn_ref[0, h], vn)
            st_ref[h] = st * gl_ref[0, h] + _mm_tn(kt_ref[:, lanes], vn)

    big = pl.BlockSpec((CHUNK, 512), lambda n: (n, 0))
    return pl.pallas_call(
        body, grid=(nc,),
        in_specs=[big, big, big, big,
                  pl.BlockSpec((1, DN_HEADS, CHUNK, CHUNK), lambda n: (n, 0, 0, 0)),
                  pl.BlockSpec((1, DN_HEADS, 1, 128), lambda n: (n, 0, 0, 0))],
        out_specs=[big, big, pl.BlockSpec((1, DN_HEADS, DN_HD, DN_HD), lambda n: (n, 0, 0, 0))],
        out_shape=[_sds((s, 512)), _sds((s, 512)), _sds((nc, DN_HEADS, DN_HD, DN_HD))],
        scratch_shapes=[pltpu.VMEM((DN_HEADS, DN_HD, DN_HD), F32)],
        compiler_params=_cp(("arbitrary",)), name="dn_scan")(u, w, qd, kt, attn, gl)


def _att_pre(proj, qw_t, kw_t, bd64):
    s = proj.shape[0]
    tm = 256

    def body(q_ref, k_ref, qw_ref, kw_ref, bd_ref, qn_ref, kn_ref):
        bd = bd_ref[...]
        q = q_ref[...]
        k = k_ref[...]
        qn_ref[...] = q * lax.rsqrt(_segsum(q * q, bd) * (1.0 / ATT_HD) + EPS) * qw_ref[...] * (ATT_HD ** -0.5)
        kn_ref[...] = k * lax.rsqrt(_segsum(k * k, bd) * (1.0 / ATT_HD) + EPS) * kw_ref[...]

    row = pl.BlockSpec((1, 512), lambda i: (0, 0))
    return pl.pallas_call(
        body, grid=(s // tm,),
        in_specs=[pl.BlockSpec((tm, 512), lambda i: (i, COL_ATT_Q)),
                  pl.BlockSpec((tm, 512), lambda i: (i, COL_ATT_K)),
                  row, row, pl.BlockSpec((512, 512), lambda i: (0, 0))],
        out_specs=[pl.BlockSpec((tm, 512), lambda i: (i, 0))] * 2,
        out_shape=[_sds((s, 512))] * 2,
        compiler_params=_cp(("parallel",)), name="att_pre")(proj, proj, qw_t, kw_t, bd64)


def _bias_fwd(rel_bias, buckets):
    def body(rb_ref, bk_ref, o_ref):
        for p in range(len(PATTERNS)):
            bk = bk_ref[p]
            for h in range(ATT_HEADS):
                acc = jnp.zeros((BLK, 2 * BLK), F32)
                for b in range(N_BUCKETS):
                    acc = jnp.where(bk == b, rb_ref[h, b], acc)
                o_ref[p, h] = acc

    return pl.pallas_call(
        body,
        in_specs=[pl.BlockSpec(memory_space=pltpu.SMEM), pl.BlockSpec(memory_space=pltpu.VMEM)],
        out_specs=pl.BlockSpec(memory_space=pltpu.VMEM),
        out_shape=_sds((len(PATTERNS), ATT_HEADS, BLK, 2 * BLK)),
        compiler_params=_cp(), name="bias_fwd")(rel_bias, buckets)


def _bias_bwd(ds_acc, buckets):
    def body(ds_ref, bk_ref, o_ref):
        for h in range(ATT_HEADS):
            for b in range(N_BUCKETS):
                tot = jnp.zeros((), F32)
                for p in range(len(PATTERNS)):
                    tot = tot + jnp.sum(jnp.where(bk_ref[p] == b, ds_ref[p, h], 0.0))
                o_ref[h, b] = tot

    return pl.pallas_call(
        body,
        in_specs=[pl.BlockSpec(memory_space=pltpu.VMEM), pl.BlockSpec(memory_space=pltpu.VMEM)],
        out_specs=pl.BlockSpec(memory_space=pltpu.SMEM),
        out_shape=_sds((ATT_HEADS, N_BUCKETS)),
        compiler_params=_cp(), name="bias_bwd")(ds_acc, buckets)


def _blocks_per_seq(p, nblk):
    r1 = PATTERNS[1][1]
    r2 = PATTERNS[2][1]
    return jnp.where(p == 0, nblk, jnp.where(p == 1, nblk // r1, nblk // r2))


def _att_scores(q, kp, kc, bias_h, has_prev):
    qi = lax.broadcasted_iota(jnp.int32, (BLK, BLK), 0)
    kj = lax.broadcasted_iota(jnp.int32, (BLK, BLK), 1)
    s_prev = _mm_nt(q, kp) + bias_h[:, :BLK]
    s_cur = _mm_nt(q, kc) + bias_h[:, BLK:]
    s_prev = jnp.where(jnp.logical_and(kj >= qi, has_prev), s_prev, NEG)
    s_cur = jnp.where(kj <= qi, s_cur, NEG)
    return s_prev, s_cur


def _att_fwd(q3, k3, v3, bias):
    npat, s, _ = q3.shape
    nblk = s // BLK

    def body(q_ref, kp_ref, kc_ref, vp_ref, vc_ref, b_ref, o_ref, lse_ref):
        p = pl.program_id(0)
        j = pl.program_id(1)
        has_prev = (j % _blocks_per_seq(p, nblk)) != 0
        for h in range(ATT_HEADS):
            lanes = slice(ATT_HD * h, ATT_HD * h + ATT_HD)
            q = q_ref[0, :, lanes]
            s_prev, s_cur = _att_scores(q, kp_ref[0, :, lanes], kc_ref[0, :, lanes], b_ref[0, h], has_prev)
            m = jnp.maximum(jnp.max(s_prev, axis=1, keepdims=True), jnp.max(s_cur, axis=1, keepdims=True))
            p_prev = jnp.exp(s_prev - m)
            p_cur = jnp.exp(s_cur - m)
            l = jnp.sum(p_prev, axis=1, keepdims=True) + jnp.sum(p_cur, axis=1, keepdims=True)
            o = _mm(p_prev, vp_ref[0, :, lanes]) + _mm(p_cur, vc_ref[0, :, lanes])
            o_ref[0, :, lanes] = o / l
            lse_ref[0, :, lanes] = jnp.broadcast_to(m + jnp.log(l), (BLK, ATT_HD))

    cur = pl.BlockSpec((1, BLK, 512), lambda p, j: (p, j, 0))
    prev = pl.BlockSpec((1, BLK, 512), lambda p, j: (p, jnp.maximum(j - 1, 0), 0))
    return pl.pallas_call(
        body, grid=(npat, nblk),
        in_specs=[cur, prev, cur, prev, cur,
                  pl.BlockSpec((1, ATT_HEADS, BLK, 2 * BLK), lambda p, j: (p, 0, 0, 0))],
        out_specs=[cur, cur],
        out_shape=[_sds((npat, s, 512))] * 2,
        compiler_params=_cp(("parallel", "parallel")), name="att_fwd")(q3, k3, k3, v3, v3, bias)


def _post_fwd(o_dn, proj, o3, lse3, dnw_t, bd128):
    s = o_dn.shape[0]
    tm = 256
    npat = o3.shape[0]

    def body(o_ref, z_ref, gate_ref, o3_ref, lse_ref, wn_ref, bd_ref, mixed_ref, oatt_ref, l_ref):
        o = o_ref[...]
        z = z_ref[...]
        rstd = lax.rsqrt(_segsum(o * o, bd_ref[...]) * (1.0 / DN_HD) + EPS)
        y_dn = o * rstd * wn_ref[...] * (z * _sigmoid(z))
        mixed_ref[:, 0:512] = y_dn.astype(BF16)
        m = lse_ref[0]
        for p in range(1, npat):
            m = jnp.maximum(m, lse_ref[p])
        tot = jnp.zeros_like(m)
        for p in range(npat):
            tot = tot + jnp.exp(lse_ref[p] - m)
        big_l = m + jnp.log(tot)
        acc = jnp.zeros_like(m)
        for p in range(npat):
            acc = acc + jnp.exp(lse_ref[p] - big_l) * o3_ref[p]
        gate = gate_ref[...]
        oatt_ref[...] = acc
        l_ref[...] = big_l
        mixed_ref[:, 512:1024] = (acc * (gate * _sigmoid(gate))).astype(BF16)

    blk = pl.BlockSpec((tm, 512), lambda i: (i, 0))
    blk3 = pl.BlockSpec((npat, tm, 512), lambda i: (0, i, 0))
    return pl.pallas_call(
        body, grid=(s // tm,),
        in_specs=[blk, pl.BlockSpec((tm, 512), lambda i: (i, COL_Z)),
                  pl.BlockSpec((tm, 512), lambda i: (i, COL_GATE)), blk3, blk3,
                  pl.BlockSpec((1, 512), lambda i: (0, 0)), pl.BlockSpec((512, 512), lambda i: (0, 0))],
        out_specs=[pl.BlockSpec((tm, D_MODEL), lambda i: (i, 0)), blk, blk],
        out_shape=[_sds((s, D_MODEL), BF16), _sds((s, 512)), _sds((s, 512))],
        compiler_params=_cp(("parallel",)), name="post_fwd")(o_dn, proj, proj, o3, lse3, dnw_t, bd128)


def _out_fwd(x, mixed, w_out_bf, tgt):
    s = x.shape[0]
    tm = 256

    def body(x_ref, m_ref, w_ref, t_ref, dy_ref, loss_ref):
        i = pl.program_id(0)

        @pl.when(i == 0)
        def _():
            loss_ref[...] = jnp.zeros_like(loss_ref)

        y = x_ref[...] + jnp.dot(m_ref[...], w_ref[...], preferred_element_type=F32)
        err = y - t_ref[...]
        dy_ref[...] = err * (1.0 / D_MODEL)
        part = 0.5 * jnp.sum(jnp.mean(err * err, axis=-1, keepdims=True), axis=0, keepdims=True)
        loss_ref[...] = loss_ref[...] + part

    blk = pl.BlockSpec((tm, D_MODEL), lambda i: (i, 0))
    return pl.pallas_call(
        body, grid=(s // tm,),
        in_specs=[blk, blk, pl.BlockSpec((D_MODEL, D_MODEL), lambda i: (0, 0)), blk],
        out_specs=[blk, pl.BlockSpec((8, 128), lambda i: (0, 0))],
        out_shape=[_sds((s, D_MODEL)), _sds((8, 128))],
        compiler_params=_cp(("arbitrary",)), name="out_fwd")(x, mixed, w_out_bf, tgt)


def _out_bwd(dy, mixed, w_out_bf):
    s = dy.shape[0]
    tm = 256

    def body(dy_ref, m_ref, w_ref, dm_ref, dw_ref):
        i = pl.program_id(0)

        @pl.when(i == 0)
        def _():
            dw_ref[...] = jnp.zeros_like(dw_ref)

        dyb = dy_ref[...].astype(BF16)
        dm_ref[...] = lax.dot_general(dyb, w_ref[...], (((1,), (1,)), ((), ())), preferred_element_type=F32)
        dw_ref[...] = dw_ref[...] + lax.dot_general(m_ref[...], dyb, (((0,), (0,)), ((), ())),
                                                    preferred_element_type=F32)

    blk = pl.BlockSpec((tm, D_MODEL), lambda i: (i, 0))
    full = pl.BlockSpec((D_MODEL, D_MODEL), lambda i: (0, 0))
    return pl.pallas_call(
        body, grid=(s // tm,), in_specs=[blk, blk, full], out_specs=[blk, full],
        out_shape=[_sds((s, D_MODEL)), _sds((D_MODEL, D_MODEL))],
        compiler_params=_cp(("arbitrary",)), name="out_bwd")(dy, mixed, w_out_bf)


def _post_bwd(dmixed, o_dn, proj, o_att, dnw_t, bd128, bd64):
    s = o_dn.shape[0]
    tm = 256

    def body(ddn_ref, datt_ref, o_ref, z_ref, gate_ref, oatt_ref, wn_ref, bd128_ref, bd64_ref,
             do_ref, dz_ref, doatt_ref, dgate_ref, delta_ref, dnw_ref):
        i = pl.program_id(0)

        @pl.when(i == 0)
        def _():
            dnw_ref[...] = jnp.zeros_like(dnw_ref)

        bd128v = bd128_ref[...]
        o = o_ref[...]
        z = z_ref[...]
        wn = wn_ref[...]
        dy = ddn_ref[...]
        rstd = lax.rsqrt(_segsum(o * o, bd128v) * (1.0 / DN_HD) + EPS)
        nrm = o * rstd
        sz = _sigmoid(z)
        dz_ref[...] = dy * nrm * wn * _silu_grad(z, sz)
        dn = dy * z * sz
        gw = dn * wn
        do_ref[...] = rstd * (gw - nrm * (_segsum(gw * nrm, bd128v) * (1.0 / DN_HD)))
        colsum = jnp.sum(dn * nrm, axis=0, keepdims=True)
        fold = colsum[:, 0:128] + colsum[:, 128:256] + colsum[:, 256:384] + colsum[:, 384:512]
        dnw_ref[...] = dnw_ref[...] + fold
        dya = datt_ref[...]
        gate = gate_ref[...]
        oatt = oatt_ref[...]
        sg = _sigmoid(gate)
        dgate_ref[...] = dya * oatt * _silu_grad(gate, sg)
        doa = dya * gate * sg
        doatt_ref[...] = doa
        delta_ref[...] = _segsum(doa * oatt, bd64_ref[...])

    blk = pl.BlockSpec((tm, 512), lambda i: (i, 0))
    cst = pl.BlockSpec((512, 512), lambda i: (0, 0))
    return pl.pallas_call(
        body, grid=(s // tm,),
        in_specs=[blk, pl.BlockSpec((tm, 512), lambda i: (i, 1)), blk,
                  pl.BlockSpec((tm, 512), lambda i: (i, COL_Z)), pl.BlockSpec((tm, 512), lambda i: (i, COL_GATE)),
                  blk, pl.BlockSpec((1, 512), lambda i: (0, 0)), cst, cst],
        out_specs=[blk] * 5 + [pl.BlockSpec((8, 128), lambda i: (0, 0))],
        out_shape=[_sds((s, 512))] * 5 + [_sds((8, 128))],
        compiler_params=_cp(("arbitrary",)), name="post_bwd")(dmixed, dmixed, o_dn, proj, proj, o_att, dnw_t,
                                                               bd128, bd64)


def _att_bwd(q3, k3, v3, do3, l3, delta3, bias):
    npat, s, _ = q3.shape
    nblk = s // BLK

    def body(q_ref, kp_ref, kc_ref, vp_ref, vc_ref, do_ref, l_ref, dl_ref, b_ref,
             dq_ref, dk_ref, dv_ref, ds_ref, dkc_ref, dvc_ref):
        p = pl.program_id(0)
        j = pl.program_id(1)
        has_prev = (j % _blocks_per_seq(p, nblk)) != 0

        @pl.when(j == 0)
        def _():
            dkc_ref[...] = jnp.zeros_like(dkc_ref)
            dvc_ref[...] = jnp.zeros_like(dvc_ref)
            ds_ref[...] = jnp.zeros_like(ds_ref)

        @pl.when(j < nblk)
        def _():
            for h in range(ATT_HEADS):
                lanes = slice(ATT_HD * h, ATT_HD * h + ATT_HD)
                q = q_ref[0, :, lanes]
                kp = kp_ref[0, :, lanes]
                kc = kc_ref[0, :, lanes]
                do = do_ref[0, :, lanes]
                big_l = l_ref[0, :, ATT_HD * h:ATT_HD * h + 1]
                delta = dl_ref[0, :, ATT_HD * h:ATT_HD * h + 1]
                s_prev, s_cur = _att_scores(q, kp, kc, b_ref[0, h], has_prev)
                p_prev = jnp.exp(s_prev - big_l)
                p_cur = jnp.exp(s_cur - big_l)
                ds_prev = p_prev * (_mm_nt(do, vp_ref[0, :, lanes]) - delta)
                ds_cur = p_cur * (_mm_nt(do, vc_ref[0, :, lanes]) - delta)
                dq_ref[0, :, lanes] = _mm(ds_prev, kp) + _mm(ds_cur, kc)
                dk_ref[0, :, lanes] = dkc_ref[:, lanes] + _mm_tn(ds_prev, q)
                dv_ref[0, :, lanes] = dvc_ref[:, lanes] + _mm_tn(p_prev, do)
                dkc_ref[:, lanes] = _mm_tn(ds_cur, q)
                dvc_ref[:, lanes] = _mm_tn(p_cur, do)
                ds_ref[0, h, :, 0:BLK] = ds_ref[0, h, :, 0:BLK] + ds_prev
                ds_ref[0, h, :, BLK:2 * BLK] = ds_ref[0, h, :, BLK:2 * BLK] + ds_cur

        @pl.when(j == nblk)
        def _():
            dk_ref[0] = dkc_ref[...]
            dv_ref[0] = dvc_ref[...]

    last = nblk - 1
    cur = pl.BlockSpec((1, BLK, 512), lambda p, j: (p, jnp.minimum(j, last), 0))
    prev = pl.BlockSpec((1, BLK, 512), lambda p, j: (p, jnp.clip(j - 1, 0, last), 0))
    return pl.pallas_call(
        body, grid=(npat, nblk + 1),
        in_specs=[cur, prev, cur, prev, cur, cur, cur, cur,
                  pl.BlockSpec((1, ATT_HEADS, BLK, 2 * BLK), lambda p, j: (p, 0, 0, 0))],
        out_specs=[cur, prev, prev, pl.BlockSpec((1, ATT_HEADS, BLK, 2 * BLK), lambda p, j: (p, 0, 0, 0))],
        out_shape=[_sds((npat, s, 512))] * 3 + [_sds((npat, ATT_HEADS, BLK, 2 * BLK))],
        scratch_shapes=[pltpu.VMEM((BLK, 512), F32), pltpu.VMEM((BLK, 512), F32)],
        compiler_params=_cp(("arbitrary", "arbitrary")), name="att_bwd")(q3, k3, k3, v3, v3, do3, l3, delta3, bias)


def _att_pre_bwd(dq3, dk3, dv3, proj, qw_t, kw_t, bd64):
    npat, s, _ = dq3.shape
    tm = 256

    def body(dq_ref, dk_ref, dv_ref, q_ref, k_ref, qw_ref, kw_ref, bd_ref,
             dqr_ref, dkr_ref, dvr_ref, dqw_ref, dkw_ref):
        i = pl.program_id(0)

        @pl.when(i == 0)
        def _():
            dqw_ref[...] = jnp.zeros_like(dqw_ref)
            dkw_ref[...] = jnp.zeros_like(dkw_ref)

        bd = bd_ref[...]

        def one(d_ref, x_ref, w_ref, scale, dx_ref, dw_ref):
            dy = d_ref[0]
            for p in range(1, npat):
                dy = dy + d_ref[p]
            dy = dy * scale
            x = x_ref[...]
            rstd = lax.rsqrt(_segsum(x * x, bd) * (1.0 / ATT_HD) + EPS)
            nrm = x * rstd
            dw_ref[...] = dw_ref[...] + jnp.sum(dy * nrm, axis=0, keepdims=True)
            g = dy * w_ref[...]
            dx_ref[...] = rstd * (g - nrm * (_segsum(g * nrm, bd) * (1.0 / ATT_HD)))

        one(dq_ref, q_ref, qw_ref, ATT_HD ** -0.5, dqr_ref, dqw_ref)
        one(dk_ref, k_ref, kw_ref, 1.0, dkr_ref, dkw_ref)
        dv = dv_ref[0]
        for p in range(1, npat):
            dv = dv + dv_ref[p]
        dvr_ref[...] = dv

    blk = pl.BlockSpec((tm, 512), lambda i: (i, 0))
    blk3 = pl.BlockSpec((npat, tm, 512), lambda i: (0, i, 0))
    row = pl.BlockSpec((1, 512), lambda i: (0, 0))
    acc = pl.BlockSpec((8, 512), lambda i: (0, 0))
    return pl.pallas_call(
        body, grid=(s // tm,),
        in_specs=[blk3, blk3, blk3, pl.BlockSpec((tm, 512), lambda i: (i, COL_ATT_Q)),
                  pl.BlockSpec((tm, 512), lambda i: (i, COL_ATT_K)), row, row,
                  pl.BlockSpec((512, 512), lambda i: (0, 0))],
        out_specs=[blk, blk, blk, acc, acc],
        out_shape=[_sds((s, 512))] * 3 + [_sds((8, 512))] * 2,
        compiler_params=_cp(("arbitrary",)), name="att_pre_bwd")(dq3, dk3, dv3, proj, proj, qw_t, kw_t, bd64)


def _dn_scan_bwd(do, sp, qd, kt, w, vn, attn, gl):
    s = do.shape[0]
    nc = s // CHUNK

    def body(do_ref, sp_ref, qd_ref, kt_ref, w_ref, vn_ref, attn_ref, gl_ref,
             du_ref, dqd_ref, dkt_ref, dw_ref, dattn_ref, dgl_ref, ds_ref):
        n = pl.program_id(0)

        @pl.when(n == 0)
        def _():
            ds_ref[...] = jnp.zeros_like(ds_ref)

        for h in range(DN_HEADS):
            lanes = slice(128 * h, 128 * h + 128)
            dsn = ds_ref[h]
            st = sp_ref[0, h]
            dov = do_ref[:, lanes]
            vnv = vn_ref[:, lanes]
            dvn = _mm_tn(attn_ref[0, h], dov) + _mm(kt_ref[:, lanes], dsn)
            du_ref[:, lanes] = dvn
            dqd_ref[:, lanes] = _mm_nt(dov, st)
            dattn_ref[0, h] = _mm_nt(dov, vnv)
            dkt_ref[:, lanes] = _mm_nt(vnv, dsn)
            tot = jnp.sum(jnp.sum(st * dsn, axis=1, keepdims=True), axis=0, keepdims=True)
            dgl_ref[0, h] = jnp.broadcast_to(tot, (1, 128))
            dw_ref[:, lanes] = -_mm_nt(dvn, st)
            ds_ref[h] = _mm_tn(qd_ref[:, lanes], dov) + dsn * gl_ref[0, h] - _mm_tn(w_ref[:, lanes], dvn)

    big = pl.BlockSpec((CHUNK, 512), lambda n: (nc - 1 - n, 0))
    sq = pl.BlockSpec((1, DN_HEADS, CHUNK, CHUNK), lambda n: (nc - 1 - n, 0, 0, 0))
    glb = pl.BlockSpec((1, DN_HEADS, 1, 128), lambda n: (nc - 1 - n, 0, 0, 0))
    return pl.pallas_call(
        body, grid=(nc,),
        in_specs=[big, pl.BlockSpec((1, DN_HEADS, DN_HD, DN_HD), lambda n: (nc - 1 - n, 0, 0, 0)),
                  big, big, big, big, sq, glb],
        out_specs=[big, big, big, big, sq, glb],
        out_shape=[_sds((s, 512))] * 4 + [_sds((nc, DN_HEADS, CHUNK, CHUNK)), _sds((nc, DN_HEADS, 1, 128))],
        scratch_shapes=[pltpu.VMEM((DN_HEADS, DN_HD, DN_HD), F32)],
        compiler_params=_cp(("arbitrary",)), name="dn_scan_bwd")(do, sp, qd, kt, w, vn, attn, gl)


def _dn_prep_bwd(qn, kn, v, bg, bgt, tri, t_inv, attn, u, w, du, dw, dqd, dkt, dattn, dgl):
    s = qn.shape[0]
    nc = s // CHUNK

    def body(q_ref, k_ref, v_ref, bg_ref, bgt_ref, tri_ref, t_ref, attn_ref, u_ref, w_ref,
             du_ref, dw_ref, dqd_ref, dkt_ref, dattn_ref, dgl_ref,
             dq_ref, dk_ref, dv_ref, dbg_ref):
        tri_v = tri_ref[...]
        bgv = bg_ref[...]
        bgtv = bgt_ref[0]
        lane = lax.broadcasted_iota(jnp.int32, (CHUNK, 128), 1)
        rowi = lax.broadcasted_iota(jnp.int32, (CHUNK, 1), 0)
        ones = jnp.ones((CHUNK, 128), F32)
        dgc_mat = jnp.zeros((CHUNK, 128), F32)
        dbeta_mat = jnp.zeros((CHUNK, 128), F32)
        for h in range(DN_HEADS):
            lanes = slice(128 * h, 128 * h + 128)
            gcc, beta, incl, strict, decay, gl = _chunk_common(bgv, bgtv, tri_v, h)
            q = q_ref[:, lanes]
            k = k_ref[:, lanes]
            vv = v_ref[:, lanes]
            tv = t_ref[0, h]
            attn_v = attn_ref[0, h]
            uv = u_ref[:, lanes]
            wv = w_ref[:, lanes]
            egc = jnp.exp(gcc)
            kb = k * beta
            a_mat = jnp.where(strict, _mm_nt(kb, k) * decay, 0.0)
            dvb = _mmx_tn(tv, du_ref[:, lanes])
            dkbg = _mmx_tn(tv, dw_ref[:, lanes])
            d_a = jnp.where(strict, -(_mm_nt(dvb, uv) + _mm_nt(dkbg, wv)), 0.0)
            d_m = d_a * decay
            dkb = _mm(d_m, k)
            dk = _mm_tn(d_m, kb)
            dattn_m = jnp.where(incl, dattn_ref[0, h], 0.0)
            dqk = dattn_m * decay
            dq = _mm(dqk, k)
            dk = dk + _mm_tn(dqk, q)
            e_mat = d_a * a_mat + dattn_m * attn_v
            dgc = jnp.sum(e_mat, axis=1, keepdims=True) - _mmx_tn(e_mat, ones)[:, 0:1]
            dqd = dqd_ref[:, lanes]
            dq = dq + dqd * egc
            dgc = dgc + jnp.sum(dqd * q, axis=1, keepdims=True) * egc
            tail = jnp.exp(gl - gcc)
            dkt = dkt_ref[:, lanes]
            dk = dk + dkt * tail
            r = jnp.sum(dkt * k, axis=1, keepdims=True) * tail
            dgc = dgc - r
            dgl_tot = jnp.sum(r, axis=0, keepdims=True) + dgl_ref[0, h][:, 0:1] * jnp.exp(gl)
            rk = jnp.sum(dkbg * k, axis=1, keepdims=True)
            dk = dk + dkbg * (beta * egc) + dkb * beta
            dbeta = rk * egc + jnp.sum(dkb * k, axis=1, keepdims=True) + jnp.sum(dvb * vv, axis=1, keepdims=True)
            dgc = dgc + rk * beta * egc
            dgc = dgc + jnp.where(rowi == CHUNK - 1, dgl_tot, 0.0)
            dq_ref[:, lanes] = dq
            dk_ref[:, lanes] = dk
            dv_ref[:, lanes] = dvb * beta
            dgc_mat = dgc_mat + jnp.where(lane == DN_HEADS + h, dgc, 0.0)
            dbeta_mat = dbeta_mat + jnp.where(lane == h, dbeta, 0.0)
        dbg_ref[...] = _mmx_tn(tri_v, dgc_mat) + dbeta_mat

    big = pl.BlockSpec((CHUNK, 512), lambda n: (n, 0))
    sq = pl.BlockSpec((1, DN_HEADS, CHUNK, CHUNK), lambda n: (n, 0, 0, 0))
    glb = pl.BlockSpec((1, DN_HEADS, 1, 128), lambda n: (n, 0, 0, 0))
    small = pl.BlockSpec((CHUNK, 128), lambda n: (n, 0))
    return pl.pallas_call(
        body, grid=(nc,),
        in_specs=[big, big, big, small, pl.BlockSpec((1, 8, CHUNK), lambda n: (n, 0, 0)),
                  pl.BlockSpec((CHUNK, CHUNK), lambda n: (0, 0)), sq, sq, big, big,
                  big, big, big, big, sq, glb],
        out_specs=[big, big, big, small],
        out_shape=[_sds((s, 512))] * 3 + [_sds((s, 128))],
        compiler_params=_cp(("parallel",)), name="dn_prep_bwd")(qn, kn, v, bg, bgt, tri, t_inv, attn, u, w,
                                                                  du, dw, dqd, dkt, dattn, dgl)


def _dn_pre_bwd(dqn, dkn, dv, dbg, proj, conv_w8, alog_l, dtb_l):
    s = proj.shape[0]
    tr = 512
    nh = tr // 8

    def body(dq_ref, dk_ref, dv_ref, dbg_ref, u_ref, halo_ref, ba_ref, w_ref, al_ref, dt_ref,
             dy_ref, dba_ref, dsm_ref):
        i = pl.program_id(0)

        @pl.when(i == 0)
        def _():
            dsm_ref[...] = jnp.zeros_like(dsm_ref)

        keep = (i > 0).astype(F32)
        for c in range(12):
            lanes = slice(128 * c, 128 * c + 128)
            y = _conv_group(u_ref[:, lanes], halo_ref[:, lanes] * keep, w_ref, c)
            sg = _sigmoid(y)
            sv = y * sg
            if c < 8:
                rs = lax.rsqrt(jnp.sum(sv * sv, axis=1, keepdims=True) + EPS)
                n = sv * rs
                if c < 4:
                    dn = dq_ref[:, lanes] * (DN_HD ** -0.5)
                else:
                    dn = dk_ref[:, slice(128 * (c - 4), 128 * (c - 3))]
                dsv = rs * (dn - n * jnp.sum(dn * n, axis=1, keepdims=True))
            else:
                dsv = dv_ref[:, slice(128 * (c - 8), 128 * (c - 7))]
            dy_ref[:, lanes] = dsv * _silu_grad(y, sg)
        _, lane, sig_b, t, nega, g = _beta_g(ba_ref[...], al_ref[...], dt_ref[...])
        dbg = dbg_ref[...]
        da = dbg * nega * _sigmoid(t)
        is_b = lane < DN_HEADS
        is_a = jnp.logical_and(lane >= DN_HEADS, lane < 2 * DN_HEADS)
        dba_ref[...] = jnp.where(is_b, dbg * sig_b * (1.0 - sig_b), jnp.where(is_a, da, 0.0))
        d_alog = jnp.sum(jnp.where(is_a, dbg * g, 0.0), axis=0, keepdims=True)
        d_dtb = jnp.sum(jnp.where(is_a, da, 0.0), axis=0, keepdims=True)
        row = lax.broadcasted_iota(jnp.int32, (8, 128), 0)
        dsm_ref[...] = dsm_ref[...] + jnp.where(row == 0, d_alog, jnp.where(row == 1, d_dtb, 0.0))

    blk = pl.BlockSpec((tr, 512), lambda i: (i, 0))
    return pl.pallas_call(
        body, grid=(s // tr,),
        in_specs=[blk, blk, blk, pl.BlockSpec((tr, 128), lambda i: (i, 0)),
                  pl.BlockSpec((tr, 1536), lambda i: (i, 0)),
                  pl.BlockSpec((8, 1536), lambda i: (jnp.maximum(i * nh - 1, 0), 0)),
                  pl.BlockSpec((tr, 128), lambda i: (i, COL_BA_128)),
                  pl.BlockSpec((8, 1536), lambda i: (0, 0)),
                  pl.BlockSpec((1, 128), lambda i: (0, 0)), pl.BlockSpec((1, 128), lambda i: (0, 0))],
        out_specs=[pl.BlockSpec((tr, 1536), lambda i: (i, 0)), pl.BlockSpec((tr, 128), lambda i: (i, 0)),
                   pl.BlockSpec((8, 128), lambda i: (0, 0))],
        out_shape=[_sds((s, 1536)), _sds((s, 128)), _sds((8, 128))],
        compiler_params=_cp(("arbitrary",)), name="dn_pre_bwd")(dqn, dkn, dv, dbg, proj, proj, proj, conv_w8,
                                                                 alog_l, dtb_l)


def _conv_bwd(dy, proj, conv_w8):
    s = dy.shape[0]
    tr = 512
    nh = tr // 8
    nblk = s // tr

    def body(dy_ref, dyn_ref, u_ref, halo_ref, w_ref, du_ref, dw_ref):
        i = pl.program_id(0)

        @pl.when(i == 0)
        def _():
            dw_ref[...] = jnp.zeros_like(dw_ref)

        keep_prev = (i > 0).astype(F32)
        keep_next = (i < nblk - 1).astype(F32)
        row = lax.broadcasted_iota(jnp.int32, (8, 128), 0)
        for c in range(12):
            lanes = slice(128 * c, 128 * c + 128)
            dyc = dy_ref[:, lanes]
            dcat = jnp.concatenate([dyc, dyn_ref[:, lanes] * keep_next], axis=0)
            xcat = jnp.concatenate([halo_ref[:, lanes] * keep_prev, u_ref[:, lanes]], axis=0)
            du = dyc * w_ref[CONV_W - 1:CONV_W, lanes]
            dwc = jnp.where(row == CONV_W - 1, jnp.sum(dyc * u_ref[:, lanes], axis=0, keepdims=True), 0.0)
            for k in range(1, CONV_W):
                du = du + pltpu.roll(dcat, tr + 8 - k, 0)[0:tr] * w_ref[CONV_W - 1 - k:CONV_W - k, lanes]
                ush = pltpu.roll(xcat, k, 0)[8:8 + tr]
                dwc = dwc + jnp.where(row == CONV_W - 1 - k, jnp.sum(dyc * ush, axis=0, keepdims=True), 0.0)
            du_ref[:, lanes] = du
            dw_ref[:, lanes] = dw_ref[:, lanes] + dwc

    return pl.pallas_call(
        body, grid=(nblk,),
        in_specs=[pl.BlockSpec((tr, 1536), lambda i: (i, 0)),
                  pl.BlockSpec((8, 1536), lambda i: (jnp.minimum((i + 1) * nh, s // 8 - 1), 0)),
                  pl.BlockSpec((tr, 1536), lambda i: (i, 0)),
                  pl.BlockSpec((8, 1536), lambda i: (jnp.maximum(i * nh - 1, 0), 0)),
                  pl.BlockSpec((8, 1536), lambda i: (0, 0))],
        out_specs=[pl.BlockSpec((tr, 1536), lambda i: (i, 0)), pl.BlockSpec((8, 1536), lambda i: (0, 0))],
        out_shape=[_sds((s, 1536)), _sds((8, 1536))],
        compiler_params=_cp(("arbitrary",)), name="conv_bwd")(dy, dy, proj, proj, conv_w8)


def _in_bwd_dx(d_qkv_dn, dz, dq_att, dk_att, dv_att, dgate, dba, w_bf, x, dy, norm_w):
    s = x.shape[0]
    tm = 256

    def body(a_ref, b_ref, c_ref, d_ref, e_ref, f_ref, g_ref, w_ref, x_ref, dy_ref, nw_ref,
             dx_ref, dp_ref, dnw_ref):
        i = pl.program_id(0)

        @pl.when(i == 0)
        def _():
            dnw_ref[...] = jnp.zeros_like(dnw_ref)

        dp = jnp.concatenate([r[...].astype(BF16) for r in (a_ref, b_ref, c_ref, d_ref, e_ref, f_ref, g_ref)],
                             axis=1)
        dp_ref[...] = dp
        dh = lax.dot_general(dp, w_ref[...], (((1,), (1,)), ((), ())), preferred_element_type=F32)
        xv = x_ref[...]
        rstd = lax.rsqrt(jnp.mean(xv * xv, axis=-1, keepdims=True) + EPS)
        xh = xv * rstd
        dnw_ref[...] = dnw_ref[...] + jnp.sum(dh * xh, axis=0, keepdims=True)
        g = dh * nw_ref[...]
        dx_ref[...] = rstd * (g - xh * jnp.mean(g * xh, axis=-1, keepdims=True)) + dy_ref[...]

    def blk(n):
        return pl.BlockSpec((tm, n), lambda i: (i, 0))

    return pl.pallas_call(
        body, grid=(s // tm,),
        in_specs=[blk(1536), blk(512), blk(512), blk(512), blk(512), blk(512), blk(128),
                  pl.BlockSpec((D_MODEL, D_IN_PAD), lambda i: (0, 0)), blk(D_MODEL), blk(D_MODEL),
                  pl.BlockSpec((1, D_MODEL), lambda i: (0, 0))],
        out_specs=[blk(D_MODEL), blk(D_IN_PAD), pl.BlockSpec((8, D_MODEL), lambda i: (0, 0))],
        out_shape=[_sds((s, D_MODEL)), _sds((s, D_IN_PAD), BF16), _sds((8, D_MODEL))],
        compiler_params=_cp(("arbitrary",)), name="in_bwd_dx")(d_qkv_dn, dz, dq_att, dk_att, dv_att, dgate, dba,
                                                                w_bf, x, dy, norm_w)


def _in_bwd_dw(x, norm_w, dp):
    s = x.shape[0]
    tm = 256
    tn = D_IN_PAD // 3

    def body(x_ref, nw_ref, dp_ref, dw_ref):
        i = pl.program_id(1)

        @pl.when(i == 0)
        def _():
            dw_ref[...] = jnp.zeros_like(dw_ref)

        xv = x_ref[...]
        rstd = lax.rsqrt(jnp.mean(xv * xv, axis=-1, keepdims=True) + EPS)
        h = (xv * rstd * nw_ref[...]).astype(BF16)
        dw_ref[...] = dw_ref[...] + lax.dot_general(h, dp_ref[...], (((0,), (0,)), ((), ())),
                                                    preferred_element_type=F32)

    return pl.pallas_call(
        body, grid=(3, s // tm),
        in_specs=[pl.BlockSpec((tm, D_MODEL), lambda j, i: (i, 0)), pl.BlockSpec((1, D_MODEL), lambda j, i: (0, 0)),
                  pl.BlockSpec((tm, tn), lambda j, i: (i, j))],
        out_specs=pl.BlockSpec((D_MODEL, tn), lambda j, i: (0, j)),
        out_shape=_sds((D_MODEL, D_IN_PAD)),
        compiler_params=_cp(("parallel", "arbitrary")), name="in_bwd_dw")(x, norm_w, dp)


def _perm(a, r):
    s, n = a.shape
    return a.reshape(s // r, r, n).transpose(1, 0, 2).reshape(s, n)


def _unperm(a, r):
    s, n = a.shape
    return a.reshape(r, s // r, n).transpose(1, 0, 2).reshape(s, n)


def _stack_perm(a):
    return jnp.stack([a if r == 1 else _perm(a, r) for _, r in PATTERNS])


def _stack_unperm(a3):
    return jnp.stack([a3[i] if r == 1 else _unperm(a3[i], r) for i, (_, r) in enumerate(PATTERNS)])


def _pack_w_in(w):
    pad = jnp.zeros((w.shape[0], D_IN_PAD - D_IN), w.dtype)
    return jnp.concatenate([w[:, 0:2048], w[:, 2056:4104], w[:, 2048:2056], pad], axis=1)


def _unpack_dw_in(dwp):
    return jnp.concatenate([dwp[:, 0:2048], dwp[:, 4096:4104], dwp[:, 2048:4096]], axis=1)


def _lane_row(vec, offset):
    return jnp.pad(vec.reshape(1, -1), ((0, 0), (offset, 128 - offset - vec.shape[0])))


def _local_step(x, tgt, norm_w, w_in, conv_w, a_log, dt_bias, dn_norm_w, q_norm_w, k_norm_w, rel_bias, w_out):
    s = x.shape[0]
    nc = s // CHUNK
    w_bf = _pack_w_in(w_in).astype(BF16)
    w_out_bf = w_out.astype(BF16)
    conv_w8 = jnp.pad(conv_w, ((0, 8 - CONV_W), (0, 0)))
    alog_l = _lane_row(a_log.reshape(-1), DN_HEADS)
    dtb_l = _lane_row(dt_bias.reshape(-1), DN_HEADS)
    dnw_t = jnp.tile(dn_norm_w.reshape(1, DN_HD), (1, DN_HEADS))
    qw_t = jnp.tile(q_norm_w.reshape(1, ATT_HD), (1, ATT_HEADS))
    kw_t = jnp.tile(k_norm_w.reshape(1, ATT_HD), (1, ATT_HEADS))
    bd128 = _block_ones(512, DN_HD)
    bd64 = _block_ones(512, ATT_HD)
    tri = _tri_incl()
    buckets = _bucket_tables()

    proj = _in_proj(x, norm_w, w_bf)
    qn, kn, v_dn, bg = _dn_pre(proj, conv_w8, alog_l, dtb_l)
    bgt = bg[:, 0:8].reshape(nc, CHUNK, 8).transpose(0, 2, 1)
    u, w, qd, kt, attn, t_inv, gl = _dn_prep(qn, kn, v_dn, bg, bgt, tri)
    o_dn, vn, sp = _dn_scan(u, w, qd, kt, attn, gl)
    qa, ka = _att_pre(proj, qw_t, kw_t, bd64)
    va = proj[:, 512 * COL_ATT_V:512 * COL_ATT_V + 512]
    q3, k3, v3 = _stack_perm(qa), _stack_perm(ka), _stack_perm(va)
    bias = _bias_fwd(rel_bias, buckets)
    o3, lse3 = _att_fwd(q3, k3, v3, bias)
    mixed, o_att, big_l = _post_fwd(o_dn, proj, _stack_unperm(o3), _stack_unperm(lse3), dnw_t, bd128)
    dy, loss_blk = _out_fwd(x, mixed, w_out_bf, tgt)

    dmixed, d_w_out = _out_bwd(dy, mixed, w_out_bf)
    do_dn, dz, do_att, dgate, delta, d_dnw = _post_bwd(dmixed, o_dn, proj, o_att, dnw_t, bd128, bd64)
    dq3, dk3, dv3, ds_acc = _att_bwd(q3, k3, v3, _stack_perm(do_att), _stack_perm(big_l), _stack_perm(delta), bias)
    d_rel_bias = _bias_bwd(ds_acc, buckets)
    dq_att, dk_att, dv_att, d_qw, d_kw = _att_pre_bwd(_stack_unperm(dq3), _stack_unperm(dk3), _stack_unperm(dv3),
                                                     proj, qw_t, kw_t, bd64)
    du, dqd, dkt, dw, dattn, dgl = _dn_scan_bwd(do_dn, sp, qd, kt, w, vn, attn, gl)
    dqn, dkn, dv_dn, dbg = _dn_prep_bwd(qn, kn, v_dn, bg, bgt, tri, t_inv, attn, u, w, du, dw, dqd, dkt, dattn, dgl)
    dyc, dba, dsm = _dn_pre_bwd(dqn, dkn, dv_dn, dbg, proj, conv_w8, alog_l, dtb_l)
    d_qkv_dn, d_conv8 = _conv_bwd(dyc, proj, conv_w8)
    grad_x, dp, d_nw8 = _in_bwd_dx(d_qkv_dn, dz, dq_att, dk_att, dv_att, dgate, dba, w_bf, x, dy, norm_w)
    d_w_in = _unpack_dw_in(_in_bwd_dw(x, norm_w, dp))

    grads = dict(
        norm_w=d_nw8[0:1, :],
        w_in=d_w_in,
        conv_w=d_conv8[0:CONV_W, :],
        a_log=dsm[0:1, DN_HEADS:2 * DN_HEADS],
        dt_bias=dsm[1:2, DN_HEADS:2 * DN_HEADS],
        dn_norm_w=d_dnw[0:1, :],
        q_norm_w=d_qw[0:1, :].reshape(ATT_HEADS, ATT_HD),
        k_norm_w=d_kw[0:1, :].reshape(ATT_HEADS, ATT_HD),
        rel_bias=d_rel_bias,
        w_out=d_w_out,
    )
    return loss_blk[0, 0], grad_x, grads


MESH_ID = pl.DeviceIdType.MESH
ANY = pl.BlockSpec(memory_space=pl.ANY)


def _position():
    return lax.axis_index("x"), lax.axis_index("y"), lax.axis_index("c")


def _other_chips(x, y):
    return [(1 - x, y), (x, 1 - y), (1 - x, 1 - y)]


def _gather_weights(w_in_s, w_out_s, conv_s):
    srcs = (w_in_s, w_out_s, conv_s)
    n_arr = len(srcs)

    def body(a_ref, b_ref, c_ref, ga_ref, gb_ref, gc_ref, send_sems, recv_sems, loc_sems):
        x, y, c = _position()
        me = 2 * x + y
        pairs = ((a_ref, ga_ref), (b_ref, gb_ref), (c_ref, gc_ref))
        local = [pltpu.make_async_copy(src, dst.at[me], loc_sems.at[t]) for t, (src, dst) in enumerate(pairs)]
        for cp in local:
            cp.start()
        sends = []
        for j, (px, py) in enumerate(_other_chips(x, y)):
            for t, (src, dst) in enumerate(pairs):
                k = n_arr * j + t
                sends.append(pltpu.make_async_remote_copy(
                    src_ref=src, dst_ref=dst.at[me], send_sem=send_sems.at[k], recv_sem=recv_sems.at[k],
                    device_id=(px, py, c), device_id_type=MESH_ID))
        for cp in sends:
            cp.start()
        for j, (px, py) in enumerate(_other_chips(x, y)):
            for t, (src, dst) in enumerate(pairs):
                k = n_arr * j + t
                pltpu.make_async_remote_copy(
                    src_ref=src, dst_ref=dst.at[2 * px + py], send_sem=send_sems.at[k], recv_sem=recv_sems.at[k],
                    device_id=(px, py, c), device_id_type=MESH_ID).wait_recv()
        for cp in sends:
            cp.wait_send()
        for cp in local:
            cp.wait()

    return pl.pallas_call(
        body, in_specs=[ANY] * n_arr, out_specs=[ANY] * n_arr,
        out_shape=[_sds((N_CHIPS,) + a.shape, a.dtype) for a in srcs],
        scratch_shapes=[pltpu.SemaphoreType.DMA((3 * n_arr,)), pltpu.SemaphoreType.DMA((3 * n_arr,)),
                        pltpu.SemaphoreType.DMA((n_arr,))],
        name="gather_weights")(*srcs)


def _exchange_grads(gw_in4, gw_out4, small):
    n_big = 2
    n_dev = 8

    def body(a_ref, b_ref, s_ref, ra_ref, rb_ref, rs_ref, send_sems, recv_sems, loc_sem):
        x, y, c = _position()
        dev = 4 * x + 2 * y + c
        pairs = ((a_ref, ra_ref), (b_ref, rb_ref))
        local = pltpu.make_async_copy(s_ref, rs_ref.at[dev], loc_sem)
        local.start()
        sends = []
        for j, (px, py) in enumerate(_other_chips(x, y)):
            for t, (src, dst) in enumerate(pairs):
                k = n_big * j + t
                sends.append(pltpu.make_async_remote_copy(
                    src_ref=src.at[2 * px + py], dst_ref=dst.at[j], send_sem=send_sems.at[k],
                    recv_sem=recv_sems.at[k], device_id=(px, py, c), device_id_type=MESH_ID))
        flips = [(dx, dy, dc) for dx in (0, 1) for dy in (0, 1) for dc in (0, 1)][1:]
        for f, (dx, dy, dc) in enumerate(flips):
            k = 3 * n_big + f
            peer = (x ^ dx, y ^ dy, c ^ dc)
            sends.append(pltpu.make_async_remote_copy(
                src_ref=s_ref, dst_ref=rs_ref.at[dev], send_sem=send_sems.at[k], recv_sem=recv_sems.at[k],
                device_id=peer, device_id_type=MESH_ID))
        for cp in sends:
            cp.start()
        for j, (px, py) in enumerate(_other_chips(x, y)):
            for t, (src, dst) in enumerate(pairs):
                k = n_big * j + t
                pltpu.make_async_remote_copy(
                    src_ref=src.at[0], dst_ref=dst.at[j], send_sem=send_sems.at[k], recv_sem=recv_sems.at[k],
                    device_id=(px, py, c), device_id_type=MESH_ID).wait_recv()
        for f, (dx, dy, dc) in enumerate(flips):
            k = 3 * n_big + f
            peer = (x ^ dx, y ^ dy, c ^ dc)
            pltpu.make_async_remote_copy(
                src_ref=s_ref, dst_ref=rs_ref.at[4 * peer[0] + 2 * peer[1] + peer[2]], send_sem=send_sems.at[k],
                recv_sem=recv_sems.at[k], device_id=peer, device_id_type=MESH_ID).wait_recv()
        for cp in sends:
            cp.wait_send()
        local.wait()

    n_sem = 3 * n_big + n_dev - 1
    return pl.pallas_call(
        body, in_specs=[ANY] * 3, out_specs=[ANY] * 3,
        out_shape=[_sds((3,) + gw_in4.shape[1:], gw_in4.dtype), _sds((3,) + gw_out4.shape[1:], gw_out4.dtype),
                   _sds((n_dev,) + small.shape, small.dtype)],
        scratch_shapes=[pltpu.SemaphoreType.DMA((n_sem,)), pltpu.SemaphoreType.DMA((n_sem,)),
                        pltpu.SemaphoreType.DMA],
        name="exchange_grads")(gw_in4, gw_out4, small)


def _plane_sum(own, recv, name):
    rows, cols = own.shape
    tr = 128

    def body(o_ref, r_ref, out_ref):
        acc = o_ref[...]
        for j in range(3):
            acc = acc + r_ref[j].astype(F32)
        out_ref[...] = acc

    return pl.pallas_call(
        body, grid=(rows // tr,),
        in_specs=[pl.BlockSpec((tr, cols), lambda i: (i, 0)), pl.BlockSpec((3, tr, cols), lambda i: (0, i, 0))],
        out_specs=pl.BlockSpec((tr, cols), lambda i: (i, 0)), out_shape=_sds((rows, cols)),
        compiler_params=_cp(("parallel",)), name=name)(own, recv)


def _swap_with_sibling(p_in, p_out):
    def body(a_ref, b_ref, ra_ref, rb_ref, send_sems, recv_sems):
        x, y, c = _position()
        sib = (x, y, 1 - c)
        cps = [pltpu.make_async_remote_copy(src_ref=src, dst_ref=dst, send_sem=send_sems.at[t],
                                            recv_sem=recv_sems.at[t], device_id=sib, device_id_type=MESH_ID)
               for t, (src, dst) in enumerate(((a_ref, ra_ref), (b_ref, rb_ref)))]
        for cp in cps:
            cp.start()
        for cp in cps:
            cp.wait_recv()
        for cp in cps:
            cp.wait_send()

    return pl.pallas_call(
        body, in_specs=[ANY] * 2, out_specs=[ANY] * 2,
        out_shape=[_sds(p_in.shape), _sds(p_out.shape)],
        scratch_shapes=[pltpu.SemaphoreType.DMA((2,)), pltpu.SemaphoreType.DMA((2,))],
        name="swap_with_sibling")(p_in, p_out)


SMALL_LAYOUT = (("norm_w", 1024), ("conv_w", 6144), ("a_log", 128), ("dt_bias", 128), ("dn_norm_w", 128),
                ("q_norm_w", 512), ("k_norm_w", 512), ("rel_bias", 256))
SMALL_TOTAL = sum(n for _, n in SMALL_LAYOUT)


def _small_offset(name):
    off = 0
    for n, size in SMALL_LAYOUT:
        if n == name:
            return off
        off += size
    raise KeyError(name)


def _pack_small(grads):
    parts = []
    for name, size in SMALL_LAYOUT:
        flat = grads[name].reshape(1, -1)
        parts.append(jnp.pad(flat, ((0, 0), (0, size - flat.shape[1]))))
    return jnp.concatenate(parts, axis=1)


def _sum_small(rows):
    n_dev = rows.shape[0]
    q_off = _small_offset("q_norm_w")
    k_off = _small_offset("k_norm_w")

    def body(r_ref, tot_ref, qk_ref):
        tot = r_ref[0:1, :]
        for d in range(1, n_dev):
            tot = tot + r_ref[d:d + 1, :]
        tot_ref[...] = tot
        for row, off in ((0, q_off), (1, k_off)):
            s4 = tot[:, off:off + 128] + tot[:, off + 128:off + 256] + tot[:, off + 256:off + 384] \
                + tot[:, off + 384:off + 512]
            qk_ref[row:row + 1, :] = s4 + pltpu.roll(s4, ATT_HD, 1)

    return pl.pallas_call(
        body, in_specs=[pl.BlockSpec(memory_space=pltpu.VMEM)],
        out_specs=[pl.BlockSpec(memory_space=pltpu.VMEM)] * 2,
        out_shape=[_sds((1, SMALL_TOTAL)), _sds((2, 128))],
        compiler_params=_cp(), name="sum_small")(rows)


def _adamw_math(w, g, m, v):
    m = ADAM_B1 * m + (1.0 - ADAM_B1) * g
    v = ADAM_B2 * v + (1.0 - ADAM_B2) * (g * g)
    m_hat = m / (1.0 - ADAM_B1 ** ADAM_STEP)
    v_hat = v / (1.0 - ADAM_B2 ** ADAM_STEP)
    delta = -ADAM_LR * (m_hat / (jnp.sqrt(v_hat) + ADAM_EPS) + ADAM_WD * w)
    return delta, m, v


def _adamw_big(plane_a, plane_b, w, m, v, name):
    rows, cols = w.shape
    tr = 128

    def body(a_ref, b_ref, w_ref, m_ref, v_ref, g_ref, d_ref, nm_ref, nv_ref):
        g = a_ref[...] + b_ref[...]
        g_ref[...] = g
        d_ref[...], nm_ref[...], nv_ref[...] = _adamw_math(w_ref[...], g, m_ref[...], v_ref[...])

    blk = pl.BlockSpec((tr, cols), lambda i: (i, 0))
    return pl.pallas_call(
        body, grid=(rows // tr,), in_specs=[blk] * 5, out_specs=[blk] * 4, out_shape=[_sds((rows, cols))] * 4,
        compiler_params=_cp(("parallel",)), name=name)(plane_a, plane_b, w, m, v)


def _adamw_small(w, g, m, v, name):
    def body(w_ref, g_ref, m_ref, v_ref, d_ref, nm_ref, nv_ref):
        d_ref[...], nm_ref[...], nv_ref[...] = _adamw_math(w_ref[...], g_ref[...], m_ref[...], v_ref[...])

    vm = pl.BlockSpec(memory_space=pltpu.VMEM)
    return pl.pallas_call(body, in_specs=[vm] * 4, out_specs=[vm] * 3, out_shape=[_sds(w.shape)] * 3,
                          compiler_params=_cp(), name=name)(w, g, m, v)


WEIGHTS = ("norm_w", "w_in", "conv_w", "a_log", "dt_bias", "dn_norm_w", "q_norm_w", "k_norm_w", "rel_bias", "w_out")


def kernel(x, norm_w, w_in, conv_w, a_log, dt_bias, dn_norm_w, q_norm_w, k_norm_w, rel_bias, w_out, loss_target, m_norm_w, m_w_in, m_conv_w, m_a_log, m_dt_bias, m_dn_norm_w, m_q_norm_w, m_k_norm_w, m_rel_bias, m_w_out, v_norm_w, v_w_in, v_conv_w, v_a_log, v_dt_bias, v_dn_norm_w, v_q_norm_w, v_k_norm_w, v_rel_bias, v_w_out):
    xi, yi, _ = _position()
    chip = 2 * xi + yi
    w_loc = dict(norm_w=norm_w, w_in=w_in[0], conv_w=conv_w[0], a_log=a_log, dt_bias=dt_bias, dn_norm_w=dn_norm_w,
                 q_norm_w=q_norm_w, k_norm_w=k_norm_w, rel_bias=rel_bias, w_out=w_out[0])
    m_loc = dict(norm_w=m_norm_w, w_in=m_w_in[0], conv_w=m_conv_w[0], a_log=m_a_log, dt_bias=m_dt_bias,
                 dn_norm_w=m_dn_norm_w, q_norm_w=m_q_norm_w, k_norm_w=m_k_norm_w, rel_bias=m_rel_bias,
                 w_out=m_w_out[0])
    v_loc = dict(norm_w=v_norm_w, w_in=v_w_in[0], conv_w=v_conv_w[0], a_log=v_a_log, dt_bias=v_dt_bias,
                 dn_norm_w=v_dn_norm_w, q_norm_w=v_q_norm_w, k_norm_w=v_k_norm_w, rel_bias=v_rel_bias,
                 w_out=v_w_out[0])

    g_in, g_out, g_conv = _gather_weights(w_loc["w_in"].astype(BF16), w_loc["w_out"].astype(BF16), w_loc["conv_w"])
    d_shard = D_IN // N_CHIPS
    w_in_full = g_in.transpose(1, 0, 2).reshape(D_MODEL, D_IN)
    w_out_full = g_out.reshape(D_MODEL, D_MODEL)
    conv_full = g_conv.transpose(1, 0, 2).reshape(CONV_W, 3 * D_DN)

    loss_local, grad_x, grads = _local_step(x[0], loss_target[0], norm_w, w_in_full, conv_full, a_log, dt_bias,
                                            dn_norm_w, q_norm_w, k_norm_w, rel_bias, w_out_full)
    loss = lax.psum(loss_local, ("x", "y", "c"))

    gw_in4 = grads["w_in"].reshape(D_MODEL, N_CHIPS, d_shard).transpose(1, 0, 2)
    gw_out4 = grads["w_out"].reshape(N_CHIPS, D_MODEL // N_CHIPS, D_MODEL)
    small = _pack_small(grads)
    r_in, r_out, r_small = _exchange_grads(gw_in4.astype(BF16), gw_out4.astype(BF16), small)
    own_in = lax.dynamic_index_in_dim(gw_in4, chip, 0, keepdims=False)
    own_out = lax.dynamic_index_in_dim(gw_out4, chip, 0, keepdims=False)
    p_in = _plane_sum(own_in, r_in, "plane_sum_w_in")
    p_out = _plane_sum(own_out, r_out, "plane_sum_w_out")
    s_in, s_out = _swap_with_sibling(p_in, p_out)
    tot_small, qk = _sum_small(r_small.reshape(8, SMALL_TOTAL))

    def small_grad(name, n):
        off = _small_offset(name)
        return tot_small[:, off:off + n]

    conv_all = small_grad("conv_w", CONV_W * 3 * D_DN).reshape(CONV_W, 3 * D_DN)
    g_small = dict(
        norm_w=small_grad("norm_w", D_MODEL),
        conv_w=lax.dynamic_slice_in_dim(conv_all, chip * (3 * D_DN // N_CHIPS), 3 * D_DN // N_CHIPS, axis=1),
        a_log=small_grad("a_log", DN_HEADS),
        dt_bias=small_grad("dt_bias", DN_HEADS),
        dn_norm_w=small_grad("dn_norm_w", DN_HD),
        q_norm_w=qk[0:1, 0:ATT_HD],
        k_norm_w=qk[1:2, 0:ATT_HD],
        rel_bias=small_grad("rel_bias", ATT_HEADS * N_BUCKETS).reshape(ATT_HEADS, N_BUCKETS),
    )

    out_g, out_d, out_m, out_v = {}, {}, {}, {}
    out_g["w_in"], out_d["w_in"], out_m["w_in"], out_v["w_in"] = _adamw_big(
        p_in, s_in, w_loc["w_in"], m_loc["w_in"], v_loc["w_in"], "adamw_w_in")
    out_g["w_out"], out_d["w_out"], out_m["w_out"], out_v["w_out"] = _adamw_big(
        p_out, s_out, w_loc["w_out"], m_loc["w_out"], v_loc["w_out"], "adamw_w_out")
    for name in g_small:
        out_g[name] = g_small[name]
        out_d[name], out_m[name], out_v[name] = _adamw_small(w_loc[name], g_small[name], m_loc[name], v_loc[name],
                                                             "adamw_" + name)
    for d in (out_g, out_d, out_m, out_v):
        for name in ("w_in", "conv_w", "w_out"):
            d[name] = d[name][None]
    return (loss, grad_x[None], *[out_g[n] for n in WEIGHTS], *[out_d[n] for n in WEIGHTS],
            *[out_m[n] for n in WEIGHTS], *[out_v[n] for n in WEIGHTS])
```

```python
import functools
import math

import numpy as np
import jax
import jax.numpy as jnp
from jax import lax
from jax.experimental import pallas as pl
from jax.experimental.pallas import tpu as pltpu

F32 = jnp.float32
BF16 = jnp.bfloat16
HI = lax.Precision.HIGHEST

D_MODEL = 1024
D_DN = 512
DN_HEADS = 4
DN_HD = 128
CONV_W = 4
CHUNK = 64
D_ATT = 512
ATT_HEADS = 8
ATT_HD = 64
PATTERNS = ((128, 1), (512, 4), (2048, 16))
N_BUCKETS = 32
MAX_DISTANCE = 2048
D_IN = 4 * D_DN + 2 * DN_HEADS + 4 * D_ATT
D_IN_PAD = 4224
EPS = 1e-6
BLK = 128
NEG = -1e30
N_CHIPS = 4

ADAM_LR = 0.001
ADAM_B1 = 0.9
ADAM_B2 = 0.999
ADAM_EPS = 1e-08
ADAM_WD = 0.01
ADAM_STEP = 10

VMEM_LIMIT = 56 * 1024 * 1024

COL_Z = 3
COL_ATT_Q = 4
COL_ATT_K = 5
COL_ATT_V = 6
COL_GATE = 7
COL_BA_128 = 32


def _cp(sem=None):
    if sem is None:
        return pltpu.CompilerParams(vmem_limit_bytes=VMEM_LIMIT)
    return pltpu.CompilerParams(dimension_semantics=sem, vmem_limit_bytes=VMEM_LIMIT)


def _sds(shape, dtype=F32):
    return jax.ShapeDtypeStruct(shape, dtype)


def _mm(a, b):
    return jnp.dot(a.astype(BF16), b.astype(BF16), preferred_element_type=F32)


def _mm_nt(a, b):
    return lax.dot_general(a.astype(BF16), b.astype(BF16), (((1,), (1,)), ((), ())),
                           preferred_element_type=F32)


def _mm_tn(a, b):
    return lax.dot_general(a.astype(BF16), b.astype(BF16), (((0,), (0,)), ((), ())),
                           preferred_element_type=F32)


def _mmx(a, b):
    return jnp.dot(a, b, precision=HI, preferred_element_type=F32)


def _mmx_nt(a, b):
    return lax.dot_general(a, b, (((1,), (1,)), ((), ())), precision=HI, preferred_element_type=F32)


def _mmx_tn(a, b):
    return lax.dot_general(a, b, (((0,), (0,)), ((), ())), precision=HI, preferred_element_type=F32)


def _dot(a, b):
    return jnp.dot(a, b, preferred_element_type=F32)


def _dot_nt(a, b):
    return lax.dot_general(a, b, (((1,), (1,)), ((), ())), preferred_element_type=F32)


def _dot_tn(a, b):
    return lax.dot_general(a, b, (((0,), (0,)), ((), ())), preferred_element_type=F32)


def _split(a):
    hi = a.astype(BF16)
    return hi, (a - hi.astype(F32)).astype(BF16)


def _mm3(a_s, b_s):
    return _dot(a_s[0], b_s[0]) + _dot(a_s[0], b_s[1]) + _dot(a_s[1], b_s[0])


def _mm3_tn(a_s, b_s):
    return _dot_tn(a_s[0], b_s[0]) + _dot_tn(a_s[0], b_s[1]) + _dot_tn(a_s[1], b_s[0])


def _interleave(gens):
    live = list(gens)
    while live:
        nxt = []
        for g in live:
            try:
                next(g)
                nxt.append(g)
            except StopIteration:
                pass
        live = nxt


def _segsum(x, bd):
    hi = x.astype(BF16)
    r1 = x - hi.astype(F32)
    mid = r1.astype(BF16)
    lo = (r1 - mid.astype(F32)).astype(BF16)
    return (jnp.dot(hi, bd, preferred_element_type=F32) + jnp.dot(mid, bd, preferred_element_type=F32)
            + jnp.dot(lo, bd, preferred_element_type=F32))


def _sigmoid(x):
    return 1.0 / (1.0 + jnp.exp(-x))


def _silu_grad(x, s):
    return s * (1.0 + x * (1.0 - s))


def _block_ones(n, seg):
    i = np.arange(n)
    return jnp.asarray((i[:, None] // seg == i[None, :] // seg).astype(np.float32), dtype=BF16)


def _tri_incl():
    i = np.arange(CHUNK)
    return jnp.asarray((i[:, None] >= i[None, :]).astype(np.float32))


def _t5_bucket(dist):
    max_exact = N_BUCKETS // 2
    d = np.maximum(dist, 1).astype(np.float64)
    large = max_exact + (np.log(d / max_exact) / math.log(MAX_DISTANCE / max_exact)
                         * (N_BUCKETS - max_exact)).astype(np.int32)
    large = np.minimum(large, N_BUCKETS - 1)
    return np.where(dist < max_exact, dist, large).astype(np.int32)


def _bucket_tables():
    qi = np.arange(BLK)[:, None]
    kj = np.arange(2 * BLK)[None, :]
    step = qi - kj + BLK
    return jnp.asarray(np.stack([_t5_bucket(np.clip(step, 0, None) * r) for _, r in PATTERNS]))


def _in_proj(x, norm_w, w_bf):
    s = x.shape[0]
    tm = 256

    def body(x_ref, nw_ref, w_ref, o_ref):
        xv = x_ref[...]
        rstd = lax.rsqrt(jnp.mean(xv * xv, axis=-1, keepdims=True) + EPS)
        h = (xv * rstd * nw_ref[...]).astype(BF16)
        o_ref[...] = jnp.dot(h, w_ref[...], preferred_element_type=F32)

    return pl.pallas_call(
        body, grid=(s // tm,),
        in_specs=[pl.BlockSpec((tm, D_MODEL), lambda i: (i, 0)),
                  pl.BlockSpec((1, D_MODEL), lambda i: (0, 0)),
                  pl.BlockSpec((D_MODEL, D_IN_PAD), lambda i: (0, 0))],
        out_specs=pl.BlockSpec((tm, D_IN_PAD), lambda i: (i, 0)),
        out_shape=_sds((s, D_IN_PAD)), compiler_params=_cp(("parallel",)), name="in_proj")(x, norm_w, w_bf)


def _conv_group(cur, halo, w_ref, c):
    rows = cur.shape[0]
    lanes = slice(128 * c, 128 * c + 128)
    xcat = jnp.concatenate([halo, cur], axis=0)
    y = cur * w_ref[CONV_W - 1:CONV_W, lanes]
    for k in range(1, CONV_W):
        sh = pltpu.roll(xcat, k, 0)[8:8 + rows]
        y = y + sh * w_ref[CONV_W - 1 - k:CONV_W - k, lanes]
    return y


def _beta_g(ba, alog_l, dtb_l):
    lane = lax.broadcasted_iota(jnp.int32, ba.shape, 1)
    sig_b = _sigmoid(ba)
    t = ba + dtb_l
    softplus = jnp.maximum(t, 0.0) + jnp.log(1.0 + jnp.exp(-jnp.abs(t)))
    nega = -jnp.exp(alog_l)
    g = nega * softplus
    out = jnp.where(lane < DN_HEADS, sig_b, jnp.where(lane < 2 * DN_HEADS, g, 0.0))
    return out, lane, sig_b, t, nega, g


def _dn_pre(proj, conv_w8, alog_l, dtb_l):
    s = proj.shape[0]
    tr = 512
    nh = tr // 8

    def body(u_ref, halo_ref, ba_ref, w_ref, al_ref, dt_ref, q_ref, k_ref, v_ref, bg_ref):
        i = pl.program_id(0)
        keep = (i > 0).astype(F32)
        for c in range(12):
            lanes = slice(128 * c, 128 * c + 128)
            y = _conv_group(u_ref[:, lanes], halo_ref[:, lanes] * keep, w_ref, c)
            sv = y * _sigmoid(y)
            if c < 8:
                rs = lax.rsqrt(jnp.sum(sv * sv, axis=1, keepdims=True) + EPS)
                n = sv * rs
                if c < 4:
                    q_ref[:, lanes] = n * (DN_HD ** -0.5)
                else:
                    k_ref[:, slice(128 * (c - 4), 128 * (c - 3))] = n
            else:
                v_ref[:, slice(128 * (c - 8), 128 * (c - 7))] = sv
        bg_ref[...] = _beta_g(ba_ref[...], al_ref[...], dt_ref[...])[0]

    return pl.pallas_call(
        body, grid=(s // tr,),
        in_specs=[pl.BlockSpec((tr, 1536), lambda i: (i, 0)),
                  pl.BlockSpec((8, 1536), lambda i: (jnp.maximum(i * nh - 1, 0), 0)),
                  pl.BlockSpec((tr, 128), lambda i: (i, COL_BA_128)),
                  pl.BlockSpec((8, 1536), lambda i: (0, 0)),
                  pl.BlockSpec((1, 128), lambda i: (0, 0)),
                  pl.BlockSpec((1, 128), lambda i: (0, 0))],
        out_specs=[pl.BlockSpec((tr, 512), lambda i: (i, 0))] * 3 + [pl.BlockSpec((tr, 128), lambda i: (i, 0))],
        out_shape=[_sds((s, 512))] * 3 + [_sds((s, 128))],
        compiler_params=_cp(("parallel",)), name="dn_pre")(proj, proj, proj, conv_w8, alog_l, dtb_l)


CPS = 2


def _chunk_cumsum(bg, bgt, tri):
    return _mmx(tri, bg), _mmx_nt(bgt, tri)


def _chunk_common(bg, gc_col, gc_row, h):
    gcc = gc_col[:, DN_HEADS + h:DN_HEADS + h + 1]
    gcr = gc_row[DN_HEADS + h:DN_HEADS + h + 1, :]
    beta = bg[:, h:h + 1]
    ii = lax.broadcasted_iota(jnp.int32, (CHUNK, CHUNK), 0)
    jj = lax.broadcasted_iota(jnp.int32, (CHUNK, CHUNK), 1)
    incl = ii >= jj
    strict = ii > jj
    decay = jnp.exp(jnp.where(incl, gcc - gcr, NEG))
    gl = gcc[CHUNK - 1:CHUNK, :]
    return gcc, beta, incl, strict, decay, gl


def _dn_prep(qn, kn, v, bg, bgt, tri):
    s = qn.shape[0]
    nc = s // CHUNK

    def body(q_ref, k_ref, v_ref, bg_ref, bgt_ref, tri_ref,
             u_ref, w_ref, qd_ref, kt_ref, attn_ref, t_ref, gl_ref):
        tri_v = tri_ref[...]
        ii = lax.broadcasted_iota(jnp.int32, (CHUNK, CHUNK), 0)
        jj = lax.broadcasted_iota(jnp.int32, (CHUNK, CHUNK), 1)
        eye = (ii == jj).astype(F32)

        def chain(cc, h, bgv, gc_col, gc_row):
            rows = slice(CHUNK * cc, CHUNK * cc + CHUNK)
            lanes = slice(128 * h, 128 * h + 128)
            gcc, beta, incl, strict, decay, gl = _chunk_common(bgv, gc_col, gc_row, h)
            q = q_ref[rows, lanes]
            k = k_ref[rows, lanes]
            vv = v_ref[rows, lanes]
            kb = k * beta
            egc = jnp.exp(gcc)
            a_mat = jnp.where(strict, _mm_nt(kb, k) * decay, 0.0)
            attn_ref[cc, h] = jnp.where(incl, _mm_nt(q, k) * decay, 0.0)
            qd_ref[rows, lanes] = q * egc
            kt_ref[rows, lanes] = k * jnp.exp(gl - gcc)
            gl_ref[cc, h] = jnp.broadcast_to(jnp.exp(gl), (1, 128))
            yield
            p = -a_mat
            t = eye + p
            for _ in range(5):
                ps = _split(p)
                p = _mm3(ps, ps)
                yield
                t = t + _mm3(_split(t), _split(p))
                yield
            t_ref[cc, h] = t
            ts = _split(t)
            u_ref[rows, lanes] = _mm3(ts, _split(vv * beta))
            w_ref[rows, lanes] = _mm3(ts, _split(kb * egc))

        gens = []
        for cc in range(CPS):
            bgv = bg_ref[CHUNK * cc:CHUNK * cc + CHUNK, :]
            gc_col, gc_row = _chunk_cumsum(bgv, bgt_ref[cc], tri_v)
            gens += [chain(cc, h, bgv, gc_col, gc_row) for h in range(DN_HEADS)]
        _interleave(gens)

    rows_step = CPS * CHUNK
    big = pl.BlockSpec((rows_step, 512), lambda n: (n, 0))
    sq = pl.BlockSpec((CPS, DN_HEADS, CHUNK, CHUNK), lambda n: (n, 0, 0, 0))
    return pl.pallas_call(
        body, grid=(nc // CPS,),
        in_specs=[big, big, big, pl.BlockSpec((rows_step, 128), lambda n: (n, 0)),
                  pl.BlockSpec((CPS, 8, CHUNK), lambda n: (n, 0, 0)),
                  pl.BlockSpec((CHUNK, CHUNK), lambda n: (0, 0))],
        out_specs=[big, big, big, big, sq, sq, pl.BlockSpec((CPS, DN_HEADS, 1, 128), lambda n: (n, 0, 0, 0))],
        out_shape=[_sds((s, 512))] * 4 + [_sds((nc, DN_HEADS, CHUNK, CHUNK))] * 2 + [_sds((nc, DN_HEADS, 1, 128))],
        compiler_params=_cp(("parallel",)), name="dn_prep")(qn, kn, v, bg, bgt, tri)


def _dn_scan(u, w, qd, kt, attn, gl):
    s = u.shape[0]
    nc = s // CHUNK

    def body(u_ref, w_ref, qd_ref, kt_ref, attn_ref, gl_ref, o_ref, vn_ref, sp_ref, st_ref):
        n = pl.program_id(0)

        @pl.when(n == 0)
        def _():
            st_ref[...] = jnp.zeros_like(st_ref)

        def chain(cc, h):
            rows = slice(CHUNK * cc, CHUNK * cc + CHUNK)
            lanes = slice(128 * h, 128 * h + 128)
            st = st_ref[h]
            sp_ref[cc, h] = st
            stb = st.astype(BF16)
            ws = _dot(w_ref[rows, lanes].astype(BF16), stb)
            qs = _dot(qd_ref[rows, lanes].astype(BF16), stb)
            yield
            vn = u_ref[rows, lanes] - ws
            vn_ref[rows, lanes] = vn
            vnb = vn.astype(BF16)
            o_ref[rows, lanes] = qs + _dot(attn_ref[cc, h].astype(BF16), vnb)
            st_ref[h] = st * gl_ref[cc, h] + _dot_tn(kt_ref[rows, lanes].astype(BF16), vnb)

        for cc in range(CPS):
            _interleave([chain(cc, h) for h in range(DN_HEADS)])

    big = pl.BlockSpec((CPS * CHUNK, 512), lambda n: (n, 0))
    return pl.pallas_call(
        body, grid=(nc // CPS,),
        in_specs=[big, big, big, big,
                  pl.BlockSpec((CPS, DN_HEADS, CHUNK, CHUNK), lambda n: (n, 0, 0, 0)),
                  pl.BlockSpec((CPS, DN_HEADS, 1, 128), lambda n: (n, 0, 0, 0))],
        out_specs=[big, big, pl.BlockSpec((CPS, DN_HEADS, DN_HD, DN_HD), lambda n: (n, 0, 0, 0))],
        out_shape=[_sds((s, 512)), _sds((s, 512)), _sds((nc, DN_HEADS, DN_HD, DN_HD))],
        scratch_shapes=[pltpu.VMEM((DN_HEADS, DN_HD, DN_HD), F32)],
        compiler_params=_cp(("arbitrary",)), name="dn_scan")(u, w, qd, kt, attn, gl)


R4 = PATTERNS[1][1]
R16 = PATTERNS[2][1]
TM = 256


def _pattern_spec(r):
    return pl.BlockSpec((r, TM // r, 512), lambda i: (0, i, 0))


def _pattern_shape(s, r):
    return _sds((r, s // r, 512))


SLABS = pltpu.VMEM((4, TM, 128), F32)


def _to_patterns(val, dsts, scr):
    for c in range(4):
        lanes = slice(128 * c, 128 * c + 128)
        scr[c] = val[:, lanes]
        for dst_ref, r in dsts:
            for a in range(r):
                dst_ref[a, :, lanes] = scr[c, pl.ds(a, TM // r, stride=r), :]


def _from_pattern(src_ref, r, scr):
    for c in range(4):
        for a in range(r):
            scr[c, pl.ds(a, TM // r, stride=r), :] = src_ref[a, :, 128 * c:128 * c + 128]
    return jnp.concatenate([scr[c] for c in range(4)], axis=1)


def _att_pre(proj, qw_t, kw_t, bd64):
    s = proj.shape[0]

    def body(q_ref, k_ref, v_ref, qw_ref, kw_ref, bd_ref,
             q1_ref, k1_ref, q4_ref, k4_ref, v4_ref, q16_ref, k16_ref, v16_ref, scr):
        bd = bd_ref[...]
        q = q_ref[...]
        k = k_ref[...]
        qn = q * lax.rsqrt(_segsum(q * q, bd) * (1.0 / ATT_HD) + EPS) * qw_ref[...] * (ATT_HD ** -0.5)
        kn = k * lax.rsqrt(_segsum(k * k, bd) * (1.0 / ATT_HD) + EPS) * kw_ref[...]
        q1_ref[...] = qn
        k1_ref[...] = kn
        _to_patterns(qn, ((q4_ref, R4), (q16_ref, R16)), scr)
        _to_patterns(kn, ((k4_ref, R4), (k16_ref, R16)), scr)
        _to_patterns(v_ref[...], ((v4_ref, R4), (v16_ref, R16)), scr)

    row = pl.BlockSpec((1, 512), lambda i: (0, 0))
    tok = pl.BlockSpec((TM, 512), lambda i: (i, 0))
    return pl.pallas_call(
        body, grid=(s // TM,),
        in_specs=[pl.BlockSpec((TM, 512), lambda i: (i, COL_ATT_Q)),
                  pl.BlockSpec((TM, 512), lambda i: (i, COL_ATT_K)),
                  pl.BlockSpec((TM, 512), lambda i: (i, COL_ATT_V)),
                  row, row, pl.BlockSpec((512, 512), lambda i: (0, 0))],
        out_specs=[tok, tok] + [_pattern_spec(R4)] * 3 + [_pattern_spec(R16)] * 3,
        out_shape=[_sds((s, 512))] * 2 + [_pattern_shape(s, R4)] * 3 + [_pattern_shape(s, R16)] * 3,
        scratch_shapes=[SLABS],
        compiler_params=_cp(("parallel",)), name="att_pre")(proj, proj, proj, qw_t, kw_t, bd64)


def _bias_fwd(rel_bias, buckets):
    def body(rb_ref, bk_ref, o_ref):
        for p in range(len(PATTERNS)):
            bk = bk_ref[p]
            for h in range(ATT_HEADS):
                acc = jnp.zeros((BLK, 2 * BLK), F32)
                for b in range(N_BUCKETS):
                    acc = jnp.where(bk == b, rb_ref[h, b], acc)
                o_ref[p, h] = acc

    return pl.pallas_call(
        body,
        in_specs=[pl.BlockSpec(memory_space=pltpu.SMEM), pl.BlockSpec(memory_space=pltpu.VMEM)],
        out_specs=pl.BlockSpec(memory_space=pltpu.VMEM),
        out_shape=_sds((len(PATTERNS), ATT_HEADS, BLK, 2 * BLK)),
        compiler_params=_cp(), name="bias_fwd")(rel_bias, buckets)


def _bias_bwd(ds_accs, buckets):
    def body(ds0_ref, ds1_ref, ds2_ref, bk_ref, o_ref):
        for h in range(ATT_HEADS):
            for b in range(N_BUCKETS):
                tot = jnp.zeros((), F32)
                for p, ds_ref in enumerate((ds0_ref, ds1_ref, ds2_ref)):
                    tot = tot + jnp.sum(jnp.where(bk_ref[p] == b, ds_ref[h], 0.0))
                o_ref[h, b] = tot

    return pl.pallas_call(
        body,
        in_specs=[pl.BlockSpec(memory_space=pltpu.VMEM)] * 4,
        out_specs=pl.BlockSpec(memory_space=pltpu.SMEM),
        out_shape=_sds((ATT_HEADS, N_BUCKETS)),
        compiler_params=_cp(), name="bias_bwd")(*ds_accs, buckets)


def _att_masks(has_prev):
    qi = lax.broadcasted_iota(jnp.int32, (BLK, BLK), 0)
    kj = lax.broadcasted_iota(jnp.int32, (BLK, BLK), 1)
    lane = lax.broadcasted_iota(jnp.int32, (BLK, 2 * ATT_HD), 1)
    return jnp.logical_and(kj >= qi, has_prev), kj <= qi, lane < ATT_HD


def _head_lanes(h):
    half = h % 2
    return slice(ATT_HD * h, ATT_HD * h + ATT_HD), slice(ATT_HD * half, ATT_HD * half + ATT_HD)


def _att_scores(qm, kp2, kc2, bias_h, mask_prev, mask_cur):
    s_prev = jnp.where(mask_prev, _dot_nt(qm, kp2) + bias_h[:, :BLK], NEG)
    s_cur = jnp.where(mask_cur, _dot_nt(qm, kc2) + bias_h[:, BLK:], NEG)
    return s_prev, s_cur


def _att_fwd(q, k, v, v_col, bias, p_idx, r, name):
    s = q.shape[0]
    nblk = s // BLK
    nseq = nblk // r

    def body(q_ref, kp_ref, kc_ref, vp_ref, vc_ref, b_ref, o_ref, lse_ref):
        j = pl.program_id(0)
        mask_prev, mask_cur, lo_half = _att_masks((j % nseq) != 0)

        def head(h, q2, kp2, kc2, vp2, vc2):
            out_l, pair_l = _head_lanes(h)
            sel = lo_half if h % 2 == 0 else jnp.logical_not(lo_half)
            qm = jnp.where(sel, q2, 0.0).astype(BF16)
            s_prev, s_cur = _att_scores(qm, kp2, kc2, b_ref[0, h], mask_prev, mask_cur)
            yield
            m = jnp.maximum(jnp.max(s_prev, axis=1, keepdims=True), jnp.max(s_cur, axis=1, keepdims=True))
            p_prev = jnp.exp(s_prev - m)
            p_cur = jnp.exp(s_cur - m)
            l = jnp.sum(p_prev, axis=1, keepdims=True) + jnp.sum(p_cur, axis=1, keepdims=True)
            yield
            o2 = _dot(p_prev.astype(BF16), vp2) + _dot(p_cur.astype(BF16), vc2)
            o_ref[:, out_l] = (o2 * (1.0 / l))[:, pair_l]
            lse_ref[:, out_l] = jnp.broadcast_to(m + jnp.log(l), (BLK, ATT_HD))

        gens = []
        for pp in range(ATT_HEADS // 2):
            lanes = slice(128 * pp, 128 * pp + 128)
            slabs = (q_ref[:, lanes], kp_ref[:, lanes].astype(BF16), kc_ref[:, lanes].astype(BF16),
                     vp_ref[:, lanes].astype(BF16), vc_ref[:, lanes].astype(BF16))
            gens += [head(2 * pp, *slabs), head(2 * pp + 1, *slabs)]
        _interleave(gens)

    cur = pl.BlockSpec((BLK, 512), lambda j: (j, 0))
    prev = pl.BlockSpec((BLK, 512), lambda j: (jnp.maximum(j - 1, 0), 0))
    vcur = pl.BlockSpec((BLK, 512), lambda j: (j, v_col))
    vprev = pl.BlockSpec((BLK, 512), lambda j: (jnp.maximum(j - 1, 0), v_col))
    return pl.pallas_call(
        body, grid=(nblk,),
        in_specs=[cur, prev, cur, vprev, vcur,
                  pl.BlockSpec((1, ATT_HEADS, BLK, 2 * BLK), lambda j: (p_idx, 0, 0, 0))],
        out_specs=[cur, cur],
        out_shape=[_sds((s, 512))] * 2,
        compiler_params=_cp(("parallel",)), name=name)(q, k, k, v, v, bias)


def _post_fwd(o_dn, proj, o_pats, lse_pats, dnw_t, bd128):
    s = o_dn.shape[0]

    def body(o_ref, z_ref, gate_ref, o1_ref, o4_ref, o16_ref, s1_ref, s4_ref, s16_ref, wn_ref, bd_ref,
             mixed_ref, oatt_ref, l1_ref, l4_ref, l16_ref, scr_a, scr_b, scr_c, scr_d):
        o = o_ref[...]
        z = z_ref[...]
        rstd = lax.rsqrt(_segsum(o * o, bd_ref[...]) * (1.0 / DN_HD) + EPS)
        y_dn = o * rstd * wn_ref[...] * (z * _sigmoid(z))
        mixed_ref[:, 0:512] = y_dn.astype(BF16)
        lses = (s1_ref[...], _from_pattern(s4_ref, R4, scr_a), _from_pattern(s16_ref, R16, scr_b))
        outs = (o1_ref[...], _from_pattern(o4_ref, R4, scr_c), _from_pattern(o16_ref, R16, scr_d))
        m = jnp.maximum(jnp.maximum(lses[0], lses[1]), lses[2])
        tot = jnp.exp(lses[0] - m) + jnp.exp(lses[1] - m) + jnp.exp(lses[2] - m)
        big_l = m + jnp.log(tot)
        acc = jnp.zeros_like(m)
        for lse_p, o_p in zip(lses, outs):
            acc = acc + jnp.exp(lse_p - big_l) * o_p
        gate = gate_ref[...]
        oatt_ref[...] = acc
        l1_ref[...] = big_l
        mixed_ref[:, 512:1024] = (acc * (gate * _sigmoid(gate))).astype(BF16)
        _to_patterns(big_l, ((l4_ref, R4), (l16_ref, R16)), scr_a)

    blk = pl.BlockSpec((TM, 512), lambda i: (i, 0))
    p4, p16 = _pattern_spec(R4), _pattern_spec(R16)
    return pl.pallas_call(
        body, grid=(s // TM,),
        in_specs=[blk, pl.BlockSpec((TM, 512), lambda i: (i, COL_Z)),
                  pl.BlockSpec((TM, 512), lambda i: (i, COL_GATE)), blk, p4, p16, blk, p4, p16,
                  pl.BlockSpec((1, 512), lambda i: (0, 0)), pl.BlockSpec((512, 512), lambda i: (0, 0))],
        out_specs=[pl.BlockSpec((TM, D_MODEL), lambda i: (i, 0)), blk, blk, p4, p16],
        out_shape=[_sds((s, D_MODEL), BF16), _sds((s, 512)), _sds((s, 512)), _pattern_shape(s, R4),
                   _pattern_shape(s, R16)],
        scratch_shapes=[SLABS] * 4,
        compiler_params=_cp(("parallel",)), name="post_fwd")(o_dn, proj, proj, *o_pats, *lse_pats, dnw_t, bd128)


def _out_fwd(x, mixed, w_out_bf, tgt):
    s = x.shape[0]
    tm = 256

    def body(x_ref, m_ref, w_ref, t_ref, dy_ref, loss_ref):
        i = pl.program_id(0)

        @pl.when(i == 0)
        def _():
            loss_ref[...] = jnp.zeros_like(loss_ref)

        y = x_ref[...] + jnp.dot(m_ref[...], w_ref[...], preferred_element_type=F32)
        err = y - t_ref[...]
        dy_ref[...] = err * (1.0 / D_MODEL)
        part = 0.5 * jnp.sum(jnp.mean(err * err, axis=-1, keepdims=True), axis=0, keepdims=True)
        loss_ref[...] = loss_ref[...] + part

    blk = pl.BlockSpec((tm, D_MODEL), lambda i: (i, 0))
    return pl.pallas_call(
        body, grid=(s // tm,),
        in_specs=[blk, blk, pl.BlockSpec((D_MODEL, D_MODEL), lambda i: (0, 0)), blk],
        out_specs=[blk, pl.BlockSpec((8, 128), lambda i: (0, 0))],
        out_shape=[_sds((s, D_MODEL)), _sds((8, 128))],
        compiler_params=_cp(("arbitrary",)), name="out_fwd")(x, mixed, w_out_bf, tgt)


def _out_bwd(dy, mixed, w_out_bf):
    s = dy.shape[0]
    tm = 256

    def body(dy_ref, m_ref, w_ref, dm_ref, dw_ref):
        i = pl.program_id(0)

        @pl.when(i == 0)
        def _():
            dw_ref[...] = jnp.zeros_like(dw_ref)

        dyb = dy_ref[...].astype(BF16)
        dm_ref[...] = lax.dot_general(dyb, w_ref[...], (((1,), (1,)), ((), ())), preferred_element_type=F32)
        dw_ref[...] = dw_ref[...] + lax.dot_general(m_ref[...], dyb, (((0,), (0,)), ((), ())),
                                                    preferred_element_type=F32)

    blk = pl.BlockSpec((tm, D_MODEL), lambda i: (i, 0))
    full = pl.BlockSpec((D_MODEL, D_MODEL), lambda i: (0, 0))
    return pl.pallas_call(
        body, grid=(s // tm,), in_specs=[blk, blk, full], out_specs=[blk, full],
        out_shape=[_sds((s, D_MODEL)), _sds((D_MODEL, D_MODEL))],
        compiler_params=_cp(("arbitrary",)), name="out_bwd")(dy, mixed, w_out_bf)


def _post_bwd(dmixed, o_dn, proj, o_att, dnw_t, bd128, bd64):
    s = o_dn.shape[0]
    tm = 256

    def body(ddn_ref, datt_ref, o_ref, z_ref, gate_ref, oatt_ref, wn_ref, bd128_ref, bd64_ref,
             do_ref, dz_ref, dgate_ref, doatt_ref, do4_ref, do16_ref, delta_ref, dl4_ref, dl16_ref, dnw_ref, scr):
        i = pl.program_id(0)

        @pl.when(i == 0)
        def _():
            dnw_ref[...] = jnp.zeros_like(dnw_ref)

        bd128v = bd128_ref[...]
        o = o_ref[...]
        z = z_ref[...]
        wn = wn_ref[...]
        dy = ddn_ref[...]
        rstd = lax.rsqrt(_segsum(o * o, bd128v) * (1.0 / DN_HD) + EPS)
        nrm = o * rstd
        sz = _sigmoid(z)
        dz_ref[...] = dy * nrm * wn * _silu_grad(z, sz)
        dn = dy * z * sz
        gw = dn * wn
        do_ref[...] = rstd * (gw - nrm * (_segsum(gw * nrm, bd128v) * (1.0 / DN_HD)))
        colsum = jnp.sum(dn * nrm, axis=0, keepdims=True)
        fold = colsum[:, 0:128] + colsum[:, 128:256] + colsum[:, 256:384] + colsum[:, 384:512]
        dnw_ref[...] = dnw_ref[...] + fold
        dya = datt_ref[...]
        gate = gate_ref[...]
        oatt = oatt_ref[...]
        sg = _sigmoid(gate)
        dgate_ref[...] = dya * oatt * _silu_grad(gate, sg)
        doa = dya * gate * sg
        doatt_ref[...] = doa
        delta = _segsum(doa * oatt, bd64_ref[...])
        delta_ref[...] = delta
        _to_patterns(doa, ((do4_ref, R4), (do16_ref, R16)), scr)
        _to_patterns(delta, ((dl4_ref, R4), (dl16_ref, R16)), scr)

    blk = pl.BlockSpec((tm, 512), lambda i: (i, 0))
    cst = pl.BlockSpec((512, 512), lambda i: (0, 0))
    p4, p16 = _pattern_spec(R4), _pattern_spec(R16)
    return pl.pallas_call(
        body, grid=(s // tm,),
        in_specs=[blk, pl.BlockSpec((tm, 512), lambda i: (i, 1)), blk,
                  pl.BlockSpec((tm, 512), lambda i: (i, COL_Z)), pl.BlockSpec((tm, 512), lambda i: (i, COL_GATE)),
                  blk, pl.BlockSpec((1, 512), lambda i: (0, 0)), cst, cst],
        out_specs=[blk, blk, blk, blk, p4, p16, blk, p4, p16, pl.BlockSpec((8, 128), lambda i: (0, 0))],
        out_shape=[_sds((s, 512))] * 4 + [_pattern_shape(s, R4), _pattern_shape(s, R16), _sds((s, 512)),
                                          _pattern_shape(s, R4), _pattern_shape(s, R16), _sds((8, 128))],
        scratch_shapes=[SLABS],
        compiler_params=_cp(("arbitrary",)), name="post_bwd")(dmixed, dmixed, o_dn, proj, proj, o_att, dnw_t,
                                                               bd128, bd64)


def _att_bwd(q, k, v, v_col, do, big_l, delta, bias, p_idx, r, name):
    s = q.shape[0]
    nblk = s // BLK
    nseq = nblk // r

    def body(q_ref, kp_ref, kc_ref, vp_ref, vc_ref, do_ref, l_ref, dl_ref, b_ref,
             dq_ref, dk_ref, dv_ref, ds_ref, dkc_ref, dvc_ref):
        j = pl.program_id(0)
        mask_prev, mask_cur, lo_half = _att_masks((j % nseq) != 0)

        @pl.when(j == 0)
        def _():
            dkc_ref[...] = jnp.zeros_like(dkc_ref)
            dvc_ref[...] = jnp.zeros_like(dvc_ref)
            ds_ref[...] = jnp.zeros_like(ds_ref)

        @pl.when(j < nblk)
        def _():
            def head(h, q2, do2, kp2, kc2, vp2, vc2):
                out_l, pair_l = _head_lanes(h)
                sel = lo_half if h % 2 == 0 else jnp.logical_not(lo_half)
                qm = jnp.where(sel, q2, 0.0).astype(BF16)
                dom = jnp.where(sel, do2, 0.0).astype(BF16)
                s_prev, s_cur = _att_scores(qm, kp2, kc2, b_ref[0, h], mask_prev, mask_cur)
                dp_prev = _dot_nt(dom, vp2)
                dp_cur = _dot_nt(dom, vc2)
                yield
                lh = l_ref[:, ATT_HD * h:ATT_HD * h + 1]
                dh = dl_ref[:, ATT_HD * h:ATT_HD * h + 1]
                p_prev = jnp.exp(s_prev - lh)
                p_cur = jnp.exp(s_cur - lh)
                ds_prev = p_prev * (dp_prev - dh)
                ds_cur = p_cur * (dp_cur - dh)
                ds_ref[h, :, 0:BLK] = ds_ref[h, :, 0:BLK] + ds_prev
                ds_ref[h, :, BLK:2 * BLK] = ds_ref[h, :, BLK:2 * BLK] + ds_cur
                dsb_prev, dsb_cur = ds_prev.astype(BF16), ds_cur.astype(BF16)
                pb_prev, pb_cur = p_prev.astype(BF16), p_cur.astype(BF16)
                yield
                qb, dob = q2.astype(BF16), do2.astype(BF16)
                dq_ref[:, out_l] = (_dot(dsb_prev, kp2) + _dot(dsb_cur, kc2))[:, pair_l]
                dk_ref[:, out_l] = dkc_ref[:, out_l] + _dot_tn(dsb_prev, qb)[:, pair_l]
                dv_ref[:, out_l] = dvc_ref[:, out_l] + _dot_tn(pb_prev, dob)[:, pair_l]
                yield
                dkc_ref[:, out_l] = _dot_tn(dsb_cur, qb)[:, pair_l]
                dvc_ref[:, out_l] = _dot_tn(pb_cur, dob)[:, pair_l]

            gens = []
            for pp in range(ATT_HEADS // 2):
                lanes = slice(128 * pp, 128 * pp + 128)
                slabs = (q_ref[:, lanes], do_ref[:, lanes], kp_ref[:, lanes].astype(BF16),
                         kc_ref[:, lanes].astype(BF16), vp_ref[:, lanes].astype(BF16),
                         vc_ref[:, lanes].astype(BF16))
                gens += [head(2 * pp, *slabs), head(2 * pp + 1, *slabs)]
            _interleave(gens)

        @pl.when(j == nblk)
        def _():
            dk_ref[...] = dkc_ref[...]
            dv_ref[...] = dvc_ref[...]

    last = nblk - 1
    cur = pl.BlockSpec((BLK, 512), lambda j: (jnp.minimum(j, last), 0))
    prev = pl.BlockSpec((BLK, 512), lambda j: (jnp.clip(j - 1, 0, last), 0))
    vcur = pl.BlockSpec((BLK, 512), lambda j: (jnp.minimum(j, last), v_col))
    vprev = pl.BlockSpec((BLK, 512), lambda j: (jnp.clip(j - 1, 0, last), v_col))
    return pl.pallas_call(
        body, grid=(nblk + 1,),
        in_specs=[cur, prev, cur, vprev, vcur, cur, cur, cur,
                  pl.BlockSpec((1, ATT_HEADS, BLK, 2 * BLK), lambda j: (p_idx, 0, 0, 0))],
        out_specs=[cur, prev, prev, pl.BlockSpec((ATT_HEADS, BLK, 2 * BLK), lambda j: (0, 0, 0))],
        out_shape=[_sds((s, 512))] * 3 + [_sds((ATT_HEADS, BLK, 2 * BLK))],
        scratch_shapes=[pltpu.VMEM((BLK, 512), F32), pltpu.VMEM((BLK, 512), F32)],
        compiler_params=_cp(("arbitrary",)), name=name)(q, k, k, v, v, do, big_l, delta, bias)


def _att_pre_bwd(dq_pats, dk_pats, dv_pats, proj, qw_t, kw_t, bd64):
    s = proj.shape[0]
    tm = TM

    def body(dq1_ref, dq4_ref, dq16_ref, dk1_ref, dk4_ref, dk16_ref, dv1_ref, dv4_ref, dv16_ref,
             q_ref, k_ref, qw_ref, kw_ref, bd_ref,
             dqr_ref, dkr_ref, dvr_ref, dqw_ref, dkw_ref, scr4, scr16):
        i = pl.program_id(0)

        @pl.when(i == 0)
        def _():
            dqw_ref[...] = jnp.zeros_like(dqw_ref)
            dkw_ref[...] = jnp.zeros_like(dkw_ref)

        bd = bd_ref[...]

        def total(d1_ref, d4_ref, d16_ref):
            return d1_ref[...] + _from_pattern(d4_ref, R4, scr4) + _from_pattern(d16_ref, R16, scr16)

        def one(d_refs, x_ref, w_ref, scale, dx_ref, dw_ref):
            dy = total(*d_refs) * scale
            x = x_ref[...]
            rstd = lax.rsqrt(_segsum(x * x, bd) * (1.0 / ATT_HD) + EPS)
            nrm = x * rstd
            dw_ref[...] = dw_ref[...] + jnp.sum(dy * nrm, axis=0, keepdims=True)
            g = dy * w_ref[...]
            dx_ref[...] = rstd * (g - nrm * (_segsum(g * nrm, bd) * (1.0 / ATT_HD)))

        one((dq1_ref, dq4_ref, dq16_ref), q_ref, qw_ref, ATT_HD ** -0.5, dqr_ref, dqw_ref)
        one((dk1_ref, dk4_ref, dk16_ref), k_ref, kw_ref, 1.0, dkr_ref, dkw_ref)
        dvr_ref[...] = total(dv1_ref, dv4_ref, dv16_ref)

    blk = pl.BlockSpec((tm, 512), lambda i: (i, 0))
    pats = [blk, _pattern_spec(R4), _pattern_spec(R16)]
    row = pl.BlockSpec((1, 512), lambda i: (0, 0))
    acc = pl.BlockSpec((8, 512), lambda i: (0, 0))
    return pl.pallas_call(
        body, grid=(s // tm,),
        in_specs=pats * 3 + [pl.BlockSpec((tm, 512), lambda i: (i, COL_ATT_Q)),
                             pl.BlockSpec((tm, 512), lambda i: (i, COL_ATT_K)), row, row,
                             pl.BlockSpec((512, 512), lambda i: (0, 0))],
        out_specs=[blk, blk, blk, acc, acc],
        out_shape=[_sds((s, 512))] * 3 + [_sds((8, 512))] * 2,
        scratch_shapes=[SLABS] * 2,
        compiler_params=_cp(("arbitrary",)), name="att_pre_bwd")(*dq_pats, *dk_pats, *dv_pats, proj, proj,
                                                                  qw_t, kw_t, bd64)


def _dn_scan_bwd(do, sp, qd, kt, w, vn, attn, gl):
    s = do.shape[0]
    nc = s // CHUNK

    def body(do_ref, sp_ref, qd_ref, kt_ref, w_ref, vn_ref, attn_ref, gl_ref,
             du_ref, dqd_ref, dkt_ref, dw_ref, dattn_ref, dgl_ref, ds_ref):
        n = pl.program_id(0)

        @pl.when(n == 0)
        def _():
            ds_ref[...] = jnp.zeros_like(ds_ref)

        def chain(cc, h):
            rows = slice(CHUNK * cc, CHUNK * cc + CHUNK)
            lanes = slice(128 * h, 128 * h + 128)
            dsn = ds_ref[h]
            st = sp_ref[cc, h]
            dsb, stb = dsn.astype(BF16), st.astype(BF16)
            dob = do_ref[rows, lanes].astype(BF16)
            vnb = vn_ref[rows, lanes].astype(BF16)
            dvn = _dot_tn(attn_ref[cc, h].astype(BF16), dob) + _dot(kt_ref[rows, lanes].astype(BF16), dsb)
            du_ref[rows, lanes] = dvn
            dqd_ref[rows, lanes] = _dot_nt(dob, stb)
            dattn_ref[cc, h] = _dot_nt(dob, vnb)
            dkt_ref[rows, lanes] = _dot_nt(vnb, dsb)
            tot = jnp.sum(jnp.sum(st * dsn, axis=1, keepdims=True), axis=0, keepdims=True)
            dgl_ref[cc, h] = jnp.broadcast_to(tot, (1, 128))
            qdo = _dot_tn(qd_ref[rows, lanes].astype(BF16), dob)
            yield
            dvb = dvn.astype(BF16)
            dw_ref[rows, lanes] = -_dot_nt(dvb, stb)
            ds_ref[h] = qdo + dsn * gl_ref[cc, h] - _dot_tn(w_ref[rows, lanes].astype(BF16), dvb)

        for cc in reversed(range(CPS)):
            _interleave([chain(cc, h) for h in range(DN_HEADS)])

    nsteps = nc // CPS
    big = pl.BlockSpec((CPS * CHUNK, 512), lambda n: (nsteps - 1 - n, 0))
    sq = pl.BlockSpec((CPS, DN_HEADS, CHUNK, CHUNK), lambda n: (nsteps - 1 - n, 0, 0, 0))
    glb = pl.BlockSpec((CPS, DN_HEADS, 1, 128), lambda n: (nsteps - 1 - n, 0, 0, 0))
    return pl.pallas_call(
        body, grid=(nsteps,),
        in_specs=[big, pl.BlockSpec((CPS, DN_HEADS, DN_HD, DN_HD), lambda n: (nsteps - 1 - n, 0, 0, 0)),
                  big, big, big, big, sq, glb],
        out_specs=[big, big, big, big, sq, glb],
        out_shape=[_sds((s, 512))] * 4 + [_sds((nc, DN_HEADS, CHUNK, CHUNK)), _sds((nc, DN_HEADS, 1, 128))],
        scratch_shapes=[pltpu.VMEM((DN_HEADS, DN_HD, DN_HD), F32)],
        compiler_params=_cp(("arbitrary",)), name="dn_scan_bwd")(do, sp, qd, kt, w, vn, attn, gl)


def _dn_prep_bwd(qn, kn, v, bg, bgt, tri, t_inv, attn, u, w, du, dw, dqd, dkt, dattn, dgl):
    s = qn.shape[0]
    nc = s // CHUNK

    def body(q_ref, k_ref, v_ref, bg_ref, bgt_ref, tri_ref, t_ref, attn_ref, u_ref, w_ref,
             du_ref, dw_ref, dqd_ref, dkt_ref, dattn_ref, dgl_ref,
             dq_ref, dk_ref, dv_ref, dbg_ref):
        tri_v = tri_ref[...]
        lane = lax.broadcasted_iota(jnp.int32, (CHUNK, 128), 1)
        rowi = lax.broadcasted_iota(jnp.int32, (CHUNK, 1), 0)
        ones_b = jnp.ones((CHUNK, 128), BF16)
        parts = [[] for _ in range(CPS)]

        def chain(cc, h, bgv, gc_col, gc_row):
            rows = slice(CHUNK * cc, CHUNK * cc + CHUNK)
            lanes = slice(128 * h, 128 * h + 128)
            gcc, beta, incl, strict, decay, gl = _chunk_common(bgv, gc_col, gc_row, h)
            q = q_ref[rows, lanes]
            k = k_ref[rows, lanes]
            vv = v_ref[rows, lanes]
            ts = _split(t_ref[cc, h])
            egc = jnp.exp(gcc)
            kb = k * beta
            a_mat = jnp.where(strict, _mm_nt(kb, k) * decay, 0.0)
            dvb = _mm3_tn(ts, _split(du_ref[rows, lanes]))
            dkbg = _mm3_tn(ts, _split(dw_ref[rows, lanes]))
            yield
            d_a = jnp.where(strict, -(_mm_nt(dvb, u_ref[rows, lanes]) + _mm_nt(dkbg, w_ref[rows, lanes])), 0.0)
            d_m = d_a * decay
            dattn_m = jnp.where(incl, dattn_ref[cc, h], 0.0)
            dqk = dattn_m * decay
            e_hi, e_lo = _split(d_a * a_mat + dattn_m * attn_ref[cc, h])
            yield
            dkb = _mm(d_m, k)
            dk = _mm_tn(d_m, kb) + _mm_tn(dqk, q)
            dq = _mm(dqk, k)
            e_colsum = (_dot_tn(e_hi, ones_b) + _dot_tn(e_lo, ones_b))[:, 0:1]
            e_rowsum = jnp.sum(e_hi.astype(F32) + e_lo.astype(F32), axis=1, keepdims=True)
            yield
            dgc = e_rowsum - e_colsum
            dqd = dqd_ref[rows, lanes]
            dq = dq + dqd * egc
            dgc = dgc + jnp.sum(dqd * q, axis=1, keepdims=True) * egc
            tail = jnp.exp(gl - gcc)
            dkt = dkt_ref[rows, lanes]
            dk = dk + dkt * tail
            r = jnp.sum(dkt * k, axis=1, keepdims=True) * tail
            dgc = dgc - r
            dgl_tot = jnp.sum(r, axis=0, keepdims=True) + dgl_ref[cc, h][:, 0:1] * jnp.exp(gl)
            rk = jnp.sum(dkbg * k, axis=1, keepdims=True)
            dk = dk + dkbg * (beta * egc) + dkb * beta
            dbeta = rk * egc + jnp.sum(dkb * k, axis=1, keepdims=True) + jnp.sum(dvb * vv, axis=1, keepdims=True)
            dgc = dgc + rk * beta * egc
            dgc = dgc + jnp.where(rowi == CHUNK - 1, dgl_tot, 0.0)
            dq_ref[rows, lanes] = dq
            dk_ref[rows, lanes] = dk
            dv_ref[rows, lanes] = dvb * beta
            parts[cc].append((h, dgc, dbeta))

        gens = []
        for cc in range(CPS):
            bgv = bg_ref[CHUNK * cc:CHUNK * cc + CHUNK, :]
            gc_col, gc_row = _chunk_cumsum(bgv, bgt_ref[cc], tri_v)
            gens += [chain(cc, h, bgv, gc_col, gc_row) for h in range(DN_HEADS)]
        _interleave(gens)
        for cc in range(CPS):
            dgc_mat = jnp.zeros((CHUNK, 128), F32)
            dbeta_mat = jnp.zeros((CHUNK, 128), F32)
            for h, dgc, dbeta in parts[cc]:
                dgc_mat = dgc_mat + jnp.where(lane == DN_HEADS + h, dgc, 0.0)
                dbeta_mat = dbeta_mat + jnp.where(lane == h, dbeta, 0.0)
            dbg_ref[CHUNK * cc:CHUNK * cc + CHUNK, :] = _mmx_tn(tri_v, dgc_mat) + dbeta_mat

    big = pl.BlockSpec((CPS * CHUNK, 512), lambda n: (n, 0))
    sq = pl.BlockSpec((CPS, DN_HEADS, CHUNK, CHUNK), lambda n: (n, 0, 0, 0))
    glb = pl.BlockSpec((CPS, DN_HEADS, 1, 128), lambda n: (n, 0, 0, 0))
    small = pl.BlockSpec((CPS * CHUNK, 128), lambda n: (n, 0))
    return pl.pallas_call(
        body, grid=(nc // CPS,),
        in_specs=[big, big, big, small, pl.BlockSpec((CPS, 8, CHUNK), lambda n: (n, 0, 0)),
                  pl.BlockSpec((CHUNK, CHUNK), lambda n: (0, 0)), sq, sq, big, big,
                  big, big, big, big, sq, glb],
        out_specs=[big, big, big, small],
        out_shape=[_sds((s, 512))] * 3 + [_sds((s, 128))],
        compiler_params=_cp(("parallel",)), name="dn_prep_bwd")(qn, kn, v, bg, bgt, tri, t_inv, attn, u, w,
                                                                  du, dw, dqd, dkt, dattn, dgl)


def _dn_pre_bwd(dqn, dkn, dv, dbg, proj, conv_w8, alog_l, dtb_l):
    s = proj.shape[0]
    tr = 512
    nh = tr // 8

    def body(dq_ref, dk_ref, dv_ref, dbg_ref, u_ref, halo_ref, ba_ref, w_ref, al_ref, dt_ref,
             dy_ref, dba_ref, dsm_ref):
        i = pl.program_id(0)

        @pl.when(i == 0)
        def _():
            dsm_ref[...] = jnp.zeros_like(dsm_ref)

        keep = (i > 0).astype(F32)
        for c in range(12):
            lanes = slice(128 * c, 128 * c + 128)
            y = _conv_group(u_ref[:, lanes], halo_ref[:, lanes] * keep, w_ref, c)
            sg = _sigmoid(y)
            sv = y * sg
            if c < 8:
                rs = lax.rsqrt(jnp.sum(sv * sv, axis=1, keepdims=True) + EPS)
                n = sv * rs
                if c < 4:
                    dn = dq_ref[:, lanes] * (DN_HD ** -0.5)
                else:
                    dn = dk_ref[:, slice(128 * (c - 4), 128 * (c - 3))]
                dsv = rs * (dn - n * jnp.sum(dn * n, axis=1, keepdims=True))
            else:
                dsv = dv_ref[:, slice(128 * (c - 8), 128 * (c - 7))]
            dy_ref[:, lanes] = dsv * _silu_grad(y, sg)
        _, lane, sig_b, t, nega, g = _beta_g(ba_ref[...], al_ref[...], dt_ref[...])
        dbg = dbg_ref[...]
        da = dbg * nega * _sigmoid(t)
        is_b = lane < DN_HEADS
        is_a = jnp.logical_and(lane >= DN_HEADS, lane < 2 * DN_HEADS)
        dba_ref[...] = jnp.where(is_b, dbg * sig_b * (1.0 - sig_b), jnp.where(is_a, da, 0.0))
        d_alog = jnp.sum(jnp.where(is_a, dbg * g, 0.0), axis=0, keepdims=True)
        d_dtb = jnp.sum(jnp.where(is_a, da, 0.0), axis=0, keepdims=True)
        row = lax.broadcasted_iota(jnp.int32, (8, 128), 0)
        dsm_ref[...] = dsm_ref[...] + jnp.where(row == 0, d_alog, jnp.where(row == 1, d_dtb, 0.0))

    blk = pl.BlockSpec((tr, 512), lambda i: (i, 0))
    return pl.pallas_call(
        body, grid=(s // tr,),
        in_specs=[blk, blk, blk, pl.BlockSpec((tr, 128), lambda i: (i, 0)),
                  pl.BlockSpec((tr, 1536), lambda i: (i, 0)),
                  pl.BlockSpec((8, 1536), lambda i: (jnp.maximum(i * nh - 1, 0), 0)),
                  pl.BlockSpec((tr, 128), lambda i: (i, COL_BA_128)),
                  pl.BlockSpec((8, 1536), lambda i: (0, 0)),
                  pl.BlockSpec((1, 128), lambda i: (0, 0)), pl.BlockSpec((1, 128), lambda i: (0, 0))],
        out_specs=[pl.BlockSpec((tr, 1536), lambda i: (i, 0)), pl.BlockSpec((tr, 128), lambda i: (i, 0)),
                   pl.BlockSpec((8, 128), lambda i: (0, 0))],
        out_shape=[_sds((s, 1536)), _sds((s, 128)), _sds((8, 128))],
        compiler_params=_cp(("arbitrary",)), name="dn_pre_bwd")(dqn, dkn, dv, dbg, proj, proj, proj, conv_w8,
                                                                 alog_l, dtb_l)


def _conv_bwd(dy, proj, conv_w8):
    s = dy.shape[0]
    tr = 512
    nh = tr // 8
    nblk = s // tr

    def body(dy_ref, dyn_ref, u_ref, halo_ref, w_ref, du_ref, dw_ref):
        i = pl.program_id(0)

        @pl.when(i == 0)
        def _():
            dw_ref[...] = jnp.zeros_like(dw_ref)

        keep_prev = (i > 0).astype(F32)
        keep_next = (i < nblk - 1).astype(F32)
        row = lax.broadcasted_iota(jnp.int32, (8, 128), 0)
        for c in range(12):
            lanes = slice(128 * c, 128 * c + 128)
            dyc = dy_ref[:, lanes]
            dcat = jnp.concatenate([dyc, dyn_ref[:, lanes] * keep_next], axis=0)
            xcat = jnp.concatenate([halo_ref[:, lanes] * keep_prev, u_ref[:, lanes]], axis=0)
            du = dyc * w_ref[CONV_W - 1:CONV_W, lanes]
            dwc = jnp.where(row == CONV_W - 1, jnp.sum(dyc * u_ref[:, lanes], axis=0, keepdims=True), 0.0)
            for k in range(1, CONV_W):
                du = du + pltpu.roll(dcat, tr + 8 - k, 0)[0:tr] * w_ref[CONV_W - 1 - k:CONV_W - k, lanes]
                ush = pltpu.roll(xcat, k, 0)[8:8 + tr]
                dwc = dwc + jnp.where(row == CONV_W - 1 - k, jnp.sum(dyc * ush, axis=0, keepdims=True), 0.0)
            du_ref[:, lanes] = du
            dw_ref[:, lanes] = dw_ref[:, lanes] + dwc

    return pl.pallas_call(
        body, grid=(nblk,),
        in_specs=[pl.BlockSpec((tr, 1536), lambda i: (i, 0)),
                  pl.BlockSpec((8, 1536), lambda i: (jnp.minimum((i + 1) * nh, s // 8 - 1), 0)),
                  pl.BlockSpec((tr, 1536), lambda i: (i, 0)),
                  pl.BlockSpec((8, 1536), lambda i: (jnp.maximum(i * nh - 1, 0), 0)),
                  pl.BlockSpec((8, 1536), lambda i: (0, 0))],
        out_specs=[pl.BlockSpec((tr, 1536), lambda i: (i, 0)), pl.BlockSpec((8, 1536), lambda i: (0, 0))],
        out_shape=[_sds((s, 1536)), _sds((8, 1536))],
        compiler_params=_cp(("arbitrary",)), name="conv_bwd")(dy, dy, proj, proj, conv_w8)


def _in_bwd_dx(d_qkv_dn, dz, dq_att, dk_att, dv_att, dgate, dba, w_bf, x, dy, norm_w):
    s = x.shape[0]
    tm = 256

    def body(a_ref, b_ref, c_ref, d_ref, e_ref, f_ref, g_ref, w_ref, x_ref, dy_ref, nw_ref,
             dx_ref, dp_ref, dnw_ref):
        i = pl.program_id(0)

        @pl.when(i == 0)
        def _():
            dnw_ref[...] = jnp.zeros_like(dnw_ref)

        dp = jnp.concatenate([r[...].astype(BF16) for r in (a_ref, b_ref, c_ref, d_ref, e_ref, f_ref, g_ref)],
                             axis=1)
        dp_ref[...] = dp
        dh = lax.dot_general(dp, w_ref[...], (((1,), (1,)), ((), ())), preferred_element_type=F32)
        xv = x_ref[...]
        rstd = lax.rsqrt(jnp.mean(xv * xv, axis=-1, keepdims=True) + EPS)
        xh = xv * rstd
        dnw_ref[...] = dnw_ref[...] + jnp.sum(dh * xh, axis=0, keepdims=True)
        g = dh * nw_ref[...]
        dx_ref[...] = rstd * (g - xh * jnp.mean(g * xh, axis=-1, keepdims=True)) + dy_ref[...]

    def blk(n):
        return pl.BlockSpec((tm, n), lambda i: (i, 0))

    return pl.pallas_call(
        body, grid=(s // tm,),
        in_specs=[blk(1536), blk(512), blk(512), blk(512), blk(512), blk(512), blk(128),
                  pl.BlockSpec((D_MODEL, D_IN_PAD), lambda i: (0, 0)), blk(D_MODEL), blk(D_MODEL),
                  pl.BlockSpec((1, D_MODEL), lambda i: (0, 0))],
        out_specs=[blk(D_MODEL), blk(D_IN_PAD), pl.BlockSpec((8, D_MODEL), lambda i: (0, 0))],
        out_shape=[_sds((s, D_MODEL)), _sds((s, D_IN_PAD), BF16), _sds((8, D_MODEL))],
        compiler_params=_cp(("arbitrary",)), name="in_bwd_dx")(d_qkv_dn, dz, dq_att, dk_att, dv_att, dgate, dba,
                                                                w_bf, x, dy, norm_w)


def _in_bwd_dw(x, norm_w, dp):
    s = x.shape[0]
    tm = 256
    tn = D_IN_PAD // 3

    def body(x_ref, nw_ref, dp_ref, dw_ref):
        i = pl.program_id(1)

        @pl.when(i == 0)
        def _():
            dw_ref[...] = jnp.zeros_like(dw_ref)

        xv = x_ref[...]
        rstd = lax.rsqrt(jnp.mean(xv * xv, axis=-1, keepdims=True) + EPS)
        h = (xv * rstd * nw_ref[...]).astype(BF16)
        dw_ref[...] = dw_ref[...] + lax.dot_general(h, dp_ref[...], (((0,), (0,)), ((), ())),
                                                    preferred_element_type=F32)

    return pl.pallas_call(
        body, grid=(3, s // tm),
        in_specs=[pl.BlockSpec((tm, D_MODEL), lambda j, i: (i, 0)), pl.BlockSpec((1, D_MODEL), lambda j, i: (0, 0)),
                  pl.BlockSpec((tm, tn), lambda j, i: (i, j))],
        out_specs=pl.BlockSpec((D_MODEL, tn), lambda j, i: (0, j)),
        out_shape=_sds((D_MODEL, D_IN_PAD)),
        compiler_params=_cp(("parallel", "arbitrary")), name="in_bwd_dw")(x, norm_w, dp)


def _flat(a):
    return a.reshape(-1, a.shape[-1])


def _as_pattern(a, r):
    return a if r == 1 else a.reshape(r, a.shape[0] // r, a.shape[1])


def _pack_w_in(w):
    pad = jnp.zeros((w.shape[0], D_IN_PAD - D_IN), w.dtype)
    return jnp.concatenate([w[:, 0:2048], w[:, 2056:4104], w[:, 2048:2056], pad], axis=1)


def _unpack_dw_in(dwp):
    return jnp.concatenate([dwp[:, 0:2048], dwp[:, 4096:4104], dwp[:, 2048:4096]], axis=1)


def _lane_row(vec, offset):
    return jnp.pad(vec.reshape(1, -1), ((0, 0), (offset, 128 - offset - vec.shape[0])))


def _local_step(x, tgt, norm_w, w_in, conv_w, a_log, dt_bias, dn_norm_w, q_norm_w, k_norm_w, rel_bias, w_out):
    s = x.shape[0]
    nc = s // CHUNK
    w_bf = _pack_w_in(w_in).astype(BF16)
    w_out_bf = w_out.astype(BF16)
    conv_w8 = jnp.pad(conv_w, ((0, 8 - CONV_W), (0, 0)))
    alog_l = _lane_row(a_log.reshape(-1), DN_HEADS)
    dtb_l = _lane_row(dt_bias.reshape(-1), DN_HEADS)
    dnw_t = jnp.tile(dn_norm_w.reshape(1, DN_HD), (1, DN_HEADS))
    qw_t = jnp.tile(q_norm_w.reshape(1, ATT_HD), (1, ATT_HEADS))
    kw_t = jnp.tile(k_norm_w.reshape(1, ATT_HD), (1, ATT_HEADS))
    bd128 = _block_ones(512, DN_HD)
    bd64 = _block_ones(512, ATT_HD)
    tri = _tri_incl()
    buckets = _bucket_tables()

    proj = _in_proj(x, norm_w, w_bf)
    qn, kn, v_dn, bg = _dn_pre(proj, conv_w8, alog_l, dtb_l)
    bgt = bg[:, 0:8].reshape(nc, CHUNK, 8).transpose(0, 2, 1)
    u, w, qd, kt, attn, t_inv, gl = _dn_prep(qn, kn, v_dn, bg, bgt, tri)
    o_dn, vn, sp = _dn_scan(u, w, qd, kt, attn, gl)
    q1, k1, q4, k4, v4, q16, k16, v16 = _att_pre(proj, qw_t, kw_t, bd64)
    rs = [r for _, r in PATTERNS]
    qkv = [(q1, k1, proj, COL_ATT_V), (_flat(q4), _flat(k4), _flat(v4), 0), (_flat(q16), _flat(k16), _flat(v16), 0)]
    bias = _bias_fwd(rel_bias, buckets)
    o_pats, lse_pats = [], []
    for p, r in enumerate(rs):
        o_p, lse_p = _att_fwd(*qkv[p], bias, p, r, "att_fwd_r%d" % r)
        o_pats.append(_as_pattern(o_p, r))
        lse_pats.append(_as_pattern(lse_p, r))
    mixed, o_att, l1, l4, l16 = _post_fwd(o_dn, proj, o_pats, lse_pats, dnw_t, bd128)
    dy, loss_blk = _out_fwd(x, mixed, w_out_bf, tgt)

    dmixed, d_w_out = _out_bwd(dy, mixed, w_out_bf)
    do_dn, dz, dgate, do1, do4, do16, dl1, dl4, dl16, d_dnw = _post_bwd(dmixed, o_dn, proj, o_att, dnw_t, bd128, bd64)
    side = [(do1, l1, dl1), (_flat(do4), _flat(l4), _flat(dl4)), (_flat(do16), _flat(l16), _flat(dl16))]
    dq_pats, dk_pats, dv_pats, ds_accs = [], [], [], []
    for p, r in enumerate(rs):
        dq_p, dk_p, dv_p, ds_p = _att_bwd(*qkv[p], *side[p], bias, p, r, "att_bwd_r%d" % r)
        dq_pats.append(_as_pattern(dq_p, r))
        dk_pats.append(_as_pattern(dk_p, r))
        dv_pats.append(_as_pattern(dv_p, r))
        ds_accs.append(ds_p)
    d_rel_bias = _bias_bwd(ds_accs, buckets)
    dq_att, dk_att, dv_att, d_qw, d_kw = _att_pre_bwd(dq_pats, dk_pats, dv_pats, proj, qw_t, kw_t, bd64)
    du, dqd, dkt, dw, dattn, dgl = _dn_scan_bwd(do_dn, sp, qd, kt, w, vn, attn, gl)
    dqn, dkn, dv_dn, dbg = _dn_prep_bwd(qn, kn, v_dn, bg, bgt, tri, t_inv, attn, u, w, du, dw, dqd, dkt, dattn, dgl)
    dyc, dba, dsm = _dn_pre_bwd(dqn, dkn, dv_dn, dbg, proj, conv_w8, alog_l, dtb_l)
    d_qkv_dn, d_conv8 = _conv_bwd(dyc, proj, conv_w8)
    grad_x, dp, d_nw8 = _in_bwd_dx(d_qkv_dn, dz, dq_att, dk_att, dv_att, dgate, dba, w_bf, x, dy, norm_w)
    d_w_in = _unpack_dw_in(_in_bwd_dw(x, norm_w, dp))

    grads = dict(
        norm_w=d_nw8[0:1, :],
        w_in=d_w_in,
        conv_w=d_conv8[0:CONV_W, :],
        a_log=dsm[0:1, DN_HEADS:2 * DN_HEADS],
        dt_bias=dsm[1:2, DN_HEADS:2 * DN_HEADS],
        dn_norm_w=d_dnw[0:1, :],
        q_norm_w=d_qw[0:1, :].reshape(ATT_HEADS, ATT_HD),
        k_norm_w=d_kw[0:1, :].reshape(ATT_HEADS, ATT_HD),
        rel_bias=d_rel_bias,
        w_out=d_w_out,
    )
    return loss_blk[0, 0], grad_x, grads


MESH_ID = pl.DeviceIdType.MESH
ANY = pl.BlockSpec(memory_space=pl.ANY)


def _position():
    return lax.axis_index("x"), lax.axis_index("y"), lax.axis_index("c")


def _other_chips(x, y):
    return [(1 - x, y), (x, 1 - y), (1 - x, 1 - y)]


def _gather_weights(w_in_s, w_out_s, conv_s):
    srcs = (w_in_s, w_out_s, conv_s)
    n_arr = len(srcs)

    def body(a_ref, b_ref, c_ref, ga_ref, gb_ref, gc_ref, send_sems, recv_sems, loc_sems):
        x, y, c = _position()
        me = 2 * x + y
        pairs = ((a_ref, ga_ref), (b_ref, gb_ref), (c_ref, gc_ref))
        local = [pltpu.make_async_copy(src, dst.at[me], loc_sems.at[t]) for t, (src, dst) in enumerate(pairs)]
        for cp in local:
            cp.start()
        sends = []
        for j, (px, py) in enumerate(_other_chips(x, y)):
            for t, (src, dst) in enumerate(pairs):
                k = n_arr * j + t
                sends.append(pltpu.make_async_remote_copy(
                    src_ref=src, dst_ref=dst.at[me], send_sem=send_sems.at[k], recv_sem=recv_sems.at[k],
                    device_id=(px, py, c), device_id_type=MESH_ID))
        for cp in sends:
            cp.start()
        for j, (px, py) in enumerate(_other_chips(x, y)):
            for t, (src, dst) in enumerate(pairs):
                k = n_arr * j + t
                pltpu.make_async_remote_copy(
                    src_ref=src, dst_ref=dst.at[2 * px + py], send_sem=send_sems.at[k], recv_sem=recv_sems.at[k],
                    device_id=(px, py, c), device_id_type=MESH_ID).wait_recv()
        for cp in sends:
            cp.wait_send()
        for cp in local:
            cp.wait()

    return pl.pallas_call(
        body, in_specs=[ANY] * n_arr, out_specs=[ANY] * n_arr,
        out_shape=[_sds((N_CHIPS,) + a.shape, a.dtype) for a in srcs],
        scratch_shapes=[pltpu.SemaphoreType.DMA((3 * n_arr,)), pltpu.SemaphoreType.DMA((3 * n_arr,)),
                        pltpu.SemaphoreType.DMA((n_arr,))],
        name="gather_weights")(*srcs)


def _exchange_grads(gw_in4, gw_out4, small):
    n_big = 2
    n_dev = 8

    def body(a_ref, b_ref, s_ref, ra_ref, rb_ref, rs_ref, send_sems, recv_sems, loc_sem):
        x, y, c = _position()
        dev = 4 * x + 2 * y + c
        pairs = ((a_ref, ra_ref), (b_ref, rb_ref))
        local = pltpu.make_async_copy(s_ref, rs_ref.at[dev], loc_sem)
        local.start()
        sends = []
        for j, (px, py) in enumerate(_other_chips(x, y)):
            for t, (src, dst) in enumerate(pairs):
                k = n_big * j + t
                sends.append(pltpu.make_async_remote_copy(
                    src_ref=src.at[2 * px + py], dst_ref=dst.at[j], send_sem=send_sems.at[k],
                    recv_sem=recv_sems.at[k], device_id=(px, py, c), device_id_type=MESH_ID))
        flips = [(dx, dy, dc) for dx in (0, 1) for dy in (0, 1) for dc in (0, 1)][1:]
        for f, (dx, dy, dc) in enumerate(flips):
            k = 3 * n_big + f
            peer = (x ^ dx, y ^ dy, c ^ dc)
            sends.append(pltpu.make_async_remote_copy(
                src_ref=s_ref, dst_ref=rs_ref.at[dev], send_sem=send_sems.at[k], recv_sem=recv_sems.at[k],
                device_id=peer, device_id_type=MESH_ID))
        for cp in sends:
            cp.start()
        for j, (px, py) in enumerate(_other_chips(x, y)):
            for t, (src, dst) in enumerate(pairs):
                k = n_big * j + t
                pltpu.make_async_remote_copy(
                    src_ref=src.at[0], dst_ref=dst.at[j], send_sem=send_sems.at[k], recv_sem=recv_sems.at[k],
                    device_id=(px, py, c), device_id_type=MESH_ID).wait_recv()
        for f, (dx, dy, dc) in enumerate(flips):
            k = 3 * n_big + f
            peer = (x ^ dx, y ^ dy, c ^ dc)
            pltpu.make_async_remote_copy(
                src_ref=s_ref, dst_ref=rs_ref.at[4 * peer[0] + 2 * peer[1] + peer[2]], send_sem=send_sems.at[k],
                recv_sem=recv_sems.at[k], device_id=peer, device_id_type=MESH_ID).wait_recv()
        for cp in sends:
            cp.wait_send()
        local.wait()

    n_sem = 3 * n_big + n_dev - 1
    return pl.pallas_call(
        body, in_specs=[ANY] * 3, out_specs=[ANY] * 3,
        out_shape=[_sds((3,) + gw_in4.shape[1:], gw_in4.dtype), _sds((3,) + gw_out4.shape[1:], gw_out4.dtype),
                   _sds((n_dev,) + small.shape, small.dtype)],
        scratch_shapes=[pltpu.SemaphoreType.DMA((n_sem,)), pltpu.SemaphoreType.DMA((n_sem,)),
                        pltpu.SemaphoreType.DMA],
        name="exchange_grads")(gw_in4, gw_out4, small)


def _plane_sum(own, recv, name):
    rows, cols = own.shape
    tr = 128

    def body(o_ref, r_ref, out_ref):
        acc = o_ref[...]
        for j in range(3):
            acc = acc + r_ref[j].astype(F32)
        out_ref[...] = acc

    return pl.pallas_call(
        body, grid=(rows // tr,),
        in_specs=[pl.BlockSpec((tr, cols), lambda i: (i, 0)), pl.BlockSpec((3, tr, cols), lambda i: (0, i, 0))],
        out_specs=pl.BlockSpec((tr, cols), lambda i: (i, 0)), out_shape=_sds((rows, cols)),
        compiler_params=_cp(("parallel",)), name=name)(own, recv)


def _swap_with_sibling(p_in, p_out):
    def body(a_ref, b_ref, ra_ref, rb_ref, send_sems, recv_sems):
        x, y, c = _position()
        sib = (x, y, 1 - c)
        cps = [pltpu.make_async_remote_copy(src_ref=src, dst_ref=dst, send_sem=send_sems.at[t],
                                            recv_sem=recv_sems.at[t], device_id=sib, device_id_type=MESH_ID)
               for t, (src, dst) in enumerate(((a_ref, ra_ref), (b_ref, rb_ref)))]
        for cp in cps:
            cp.start()
        for cp in cps:
            cp.wait_recv()
        for cp in cps:
            cp.wait_send()

    return pl.pallas_call(
        body, in_specs=[ANY] * 2, out_specs=[ANY] * 2,
        out_shape=[_sds(p_in.shape), _sds(p_out.shape)],
        scratch_shapes=[pltpu.SemaphoreType.DMA((2,)), pltpu.SemaphoreType.DMA((2,))],
        name="swap_with_sibling")(p_in, p_out)


SMALL_LAYOUT = (("norm_w", 1024), ("conv_w", 6144), ("a_log", 128), ("dt_bias", 128), ("dn_norm_w", 128),
                ("q_norm_w", 512), ("k_norm_w", 512), ("rel_bias", 256))
SMALL_TOTAL = sum(n for _, n in SMALL_LAYOUT)


def _small_offset(name):
    off = 0
    for n, size in SMALL_LAYOUT:
        if n == name:
            return off
        off += size
    raise KeyError(name)


def _pack_small(grads):
    parts = []
    for name, size in SMALL_LAYOUT:
        flat = grads[name].reshape(1, -1)
        parts.append(jnp.pad(flat, ((0, 0), (0, size - flat.shape[1]))))
    return jnp.concatenate(parts, axis=1)


def _sum_small(rows):
    n_dev = rows.shape[0]
    q_off = _small_offset("q_norm_w")
    k_off = _small_offset("k_norm_w")

    def body(r_ref, tot_ref, qk_ref):
        tot = r_ref[0:1, :]
        for d in range(1, n_dev):
            tot = tot + r_ref[d:d + 1, :]
        tot_ref[...] = tot
        for row, off in ((0, q_off), (1, k_off)):
            s4 = tot[:, off:off + 128] + tot[:, off + 128:off + 256] + tot[:, off + 256:off + 384] \
                + tot[:, off + 384:off + 512]
            qk_ref[row:row + 1, :] = s4 + pltpu.roll(s4, ATT_HD, 1)

    return pl.pallas_call(
        body, in_specs=[pl.BlockSpec(memory_space=pltpu.VMEM)],
        out_specs=[pl.BlockSpec(memory_space=pltpu.VMEM)] * 2,
        out_shape=[_sds((1, SMALL_TOTAL)), _sds((2, 128))],
        compiler_params=_cp(), name="sum_small")(rows)


def _adamw_math(w, g, m, v):
    m = ADAM_B1 * m + (1.0 - ADAM_B1) * g
    v = ADAM_B2 * v + (1.0 - ADAM_B2) * (g * g)
    m_hat = m / (1.0 - ADAM_B1 ** ADAM_STEP)
    v_hat = v / (1.0 - ADAM_B2 ** ADAM_STEP)
    delta = -ADAM_LR * (m_hat / (jnp.sqrt(v_hat) + ADAM_EPS) + ADAM_WD * w)
    return delta, m, v


def _adamw_big(plane_a, plane_b, w, m, v, name):
    rows, cols = w.shape
    tr = 128

    def body(a_ref, b_ref, w_ref, m_ref, v_ref, g_ref, d_ref, nm_ref, nv_ref):
        g = a_ref[...] + b_ref[...]
        g_ref[...] = g
        d_ref[...], nm_ref[...], nv_ref[...] = _adamw_math(w_ref[...], g, m_ref[...], v_ref[...])

    blk = pl.BlockSpec((tr, cols), lambda i: (i, 0))
    return pl.pallas_call(
        body, grid=(rows // tr,), in_specs=[blk] * 5, out_specs=[blk] * 4, out_shape=[_sds((rows, cols))] * 4,
        compiler_params=_cp(("parallel",)), name=name)(plane_a, plane_b, w, m, v)


def _adamw_small(w, g, m, v, name):
    def body(w_ref, g_ref, m_ref, v_ref, d_ref, nm_ref, nv_ref):
        d_ref[...], nm_ref[...], nv_ref[...] = _adamw_math(w_ref[...], g_ref[...], m_ref[...], v_ref[...])

    vm = pl.BlockSpec(memory_space=pltpu.VMEM)
    return pl.pallas_call(body, in_specs=[vm] * 4, out_specs=[vm] * 3, out_shape=[_sds(w.shape)] * 3,
                          compiler_params=_cp(), name=name)(w, g, m, v)


WEIGHTS = ("norm_w", "w_in", "conv_w", "a_log", "dt_bias", "dn_norm_w", "q_norm_w", "k_norm_w", "rel_bias", "w_out")


def kernel(x, norm_w, w_in, conv_w, a_log, dt_bias, dn_norm_w, q_norm_w, k_norm_w, rel_bias, w_out, loss_target, m_norm_w, m_w_in, m_conv_w, m_a_log, m_dt_bias, m_dn_norm_w, m_q_norm_w, m_k_norm_w, m_rel_bias, m_w_out, v_norm_w, v_w_in, v_conv_w, v_a_log, v_dt_bias, v_dn_norm_w, v_q_norm_w, v_k_norm_w, v_rel_bias, v_w_out):
    xi, yi, _ = _position()
    chip = 2 * xi + yi
    w_loc = dict(norm_w=norm_w, w_in=w_in[0], conv_w=conv_w[0], a_log=a_log, dt_bias=dt_bias, dn_norm_w=dn_norm_w,
                 q_norm_w=q_norm_w, k_norm_w=k_norm_w, rel_bias=rel_bias, w_out=w_out[0])
    m_loc = dict(norm_w=m_norm_w, w_in=m_w_in[0], conv_w=m_conv_w[0], a_log=m_a_log, dt_bias=m_dt_bias,
                 dn_norm_w=m_dn_norm_w, q_norm_w=m_q_norm_w, k_norm_w=m_k_norm_w, rel_bias=m_rel_bias,
                 w_out=m_w_out[0])
    v_loc = dict(norm_w=v_norm_w, w_in=v_w_in[0], conv_w=v_conv_w[0], a_log=v_a_log, dt_bias=v_dt_bias,
                 dn_norm_w=v_dn_norm_w, q_norm_w=v_q_norm_w, k_norm_w=v_k_norm_w, rel_bias=v_rel_bias,
                 w_out=v_w_out[0])

    g_in, g_out, g_conv = _gather_weights(w_loc["w_in"].astype(BF16), w_loc["w_out"].astype(BF16), w_loc["conv_w"])
    d_shard = D_IN // N_CHIPS
    w_in_full = g_in.transpose(1, 0, 2).reshape(D_MODEL, D_IN)
    w_out_full = g_out.reshape(D_MODEL, D_MODEL)
    conv_full = g_conv.transpose(1, 0, 2).reshape(CONV_W, 3 * D_DN)

    loss_local, grad_x, grads = _local_step(x[0], loss_target[0], norm_w, w_in_full, conv_full, a_log, dt_bias,
                                            dn_norm_w, q_norm_w, k_norm_w, rel_bias, w_out_full)
    loss = lax.psum(loss_local, ("x", "y", "c"))

    gw_in4 = grads["w_in"].reshape(D_MODEL, N_CHIPS, d_shard).transpose(1, 0, 2)
    gw_out4 = grads["w_out"].reshape(N_CHIPS, D_MODEL // N_CHIPS, D_MODEL)
    small = _pack_small(grads)
    r_in, r_out, r_small = _exchange_grads(gw_in4.astype(BF16), gw_out4.astype(BF16), small)
    own_in = lax.dynamic_index_in_dim(gw_in4, chip, 0, keepdims=False)
    own_out = lax.dynamic_index_in_dim(gw_out4, chip, 0, keepdims=False)
    p_in = _plane_sum(own_in, r_in, "plane_sum_w_in")
    p_out = _plane_sum(own_out, r_out, "plane_sum_w_out")
    s_in, s_out = _swap_with_sibling(p_in, p_out)
    tot_small, qk = _sum_small(r_small.reshape(8, SMALL_TOTAL))

    def small_grad(name, n):
        off = _small_offset(name)
        return tot_small[:, off:off + n]

    conv_all = small_grad("conv_w", CONV_W * 3 * D_DN).reshape(CONV_W, 3 * D_DN)
    g_small = dict(
        norm_w=small_grad("norm_w", D_MODEL),
        conv_w=lax.dynamic_slice_in_dim(conv_all, chip * (3 * D_DN // N_CHIPS), 3 * D_DN // N_CHIPS, axis=1),
        a_log=small_grad("a_log", DN_HEADS),
        dt_bias=small_grad("dt_bias", DN_HEADS),
        dn_norm_w=small_grad("dn_norm_w", DN_HD),
        q_norm_w=qk[0:1, 0:ATT_HD],
        k_norm_w=qk[1:2, 0:ATT_HD],
        rel_bias=small_grad("rel_bias", ATT_HEADS * N_BUCKETS).reshape(ATT_HEADS, N_BUCKETS),
    )

    out_g, out_d, out_m, out_v = {}, {}, {}, {}
    out_g["w_in"], out_d["w_in"], out_m["w_in"], out_v["w_in"] = _adamw_big(
        p_in, s_in, w_loc["w_in"], m_loc["w_in"], v_loc["w_in"], "adamw_w_in")
    out_g["w_out"], out_d["w_out"], out_m["w_out"], out_v["w_out"] = _adamw_big(
        p_out, s_out, w_loc["w_out"], m_loc["w_out"], v_loc["w_out"], "adamw_w_out")
    for name in g_small:
        out_g[name] = g_small[name]
        out_d[name], out_m[name], out_v[name] = _adamw_small(w_loc[name], g_small[name], m_loc[name], v_loc[name],
                                                             "adamw_" + name)
    for d in (out_g, out_d, out_m, out_v):
        for name in ("w_in", "conv_w", "w_out"):
            d[name] = d[name][None]
    return (loss, grad_x[None], *[out_g[n] for n in WEIGHTS], *[out_d[n] for n in WEIGHTS],
            *[out_m[n] for n in WEIGHTS], *[out_v[n] for n in WEIGHTS])
```

```python
import functools
import math

import numpy as np
import jax
import jax.numpy as jnp
from jax import lax
from jax.experimental import pallas as pl
from jax.experimental.pallas import tpu as pltpu

F32 = jnp.float32
BF16 = jnp.bfloat16
HI = lax.Precision.HIGHEST

D_MODEL = 1024
D_DN = 512
DN_HEADS = 4
DN_HD = 128
CONV_W = 4
CHUNK = 64
D_ATT = 512
ATT_HEADS = 8
ATT_HD = 64
PATTERNS = ((128, 1), (512, 4), (2048, 16))
N_BUCKETS = 32
MAX_DISTANCE = 2048
D_IN = 4 * D_DN + 2 * DN_HEADS + 4 * D_ATT
D_IN_PAD = 4224
EPS = 1e-6
BLK = 128
NEG = -1e30
N_CHIPS = 4

ADAM_LR = 0.001
ADAM_B1 = 0.9
ADAM_B2 = 0.999
ADAM_EPS = 1e-08
ADAM_WD = 0.01
ADAM_STEP = 10

VMEM_LIMIT = 56 * 1024 * 1024

COL_Z = 3
COL_ATT_Q = 4
COL_ATT_K = 5
COL_ATT_V = 6
COL_GATE = 7
COL_BA_128 = 32


def _cp(sem=None):
    if sem is None:
        return pltpu.CompilerParams(vmem_limit_bytes=VMEM_LIMIT)
    return pltpu.CompilerParams(dimension_semantics=sem, vmem_limit_bytes=VMEM_LIMIT)


def _sds(shape, dtype=F32):
    return jax.ShapeDtypeStruct(shape, dtype)


def _mm(a, b):
    return jnp.dot(a.astype(BF16), b.astype(BF16), preferred_element_type=F32)


def _mm_nt(a, b):
    return lax.dot_general(a.astype(BF16), b.astype(BF16), (((1,), (1,)), ((), ())),
                           preferred_element_type=F32)


def _mm_tn(a, b):
    return lax.dot_general(a.astype(BF16), b.astype(BF16), (((0,), (0,)), ((), ())),
                           preferred_element_type=F32)


def _mmx(a, b):
    return jnp.dot(a, b, precision=HI, preferred_element_type=F32)


def _mmx_nt(a, b):
    return lax.dot_general(a, b, (((1,), (1,)), ((), ())), precision=HI, preferred_element_type=F32)


def _mmx_tn(a, b):
    return lax.dot_general(a, b, (((0,), (0,)), ((), ())), precision=HI, preferred_element_type=F32)


def _dot(a, b):
    return jnp.dot(a, b, preferred_element_type=F32)


def _dot_nt(a, b):
    return lax.dot_general(a, b, (((1,), (1,)), ((), ())), preferred_element_type=F32)


def _dot_tn(a, b):
    return lax.dot_general(a, b, (((0,), (0,)), ((), ())), preferred_element_type=F32)


def _split(a):
    hi = a.astype(BF16)
    return hi, (a - hi.astype(F32)).astype(BF16)


def _mm3(a_s, b_s):
    return _dot(a_s[0], b_s[0]) + _dot(a_s[0], b_s[1]) + _dot(a_s[1], b_s[0])


def _mm3_tn(a_s, b_s):
    return _dot_tn(a_s[0], b_s[0]) + _dot_tn(a_s[0], b_s[1]) + _dot_tn(a_s[1], b_s[0])


def _interleave(gens):
    live = list(gens)
    while live:
        nxt = []
        for g in live:
            try:
                next(g)
                nxt.append(g)
            except StopIteration:
                pass
        live = nxt


def _segsum(x, bd):
    hi = x.astype(BF16)
    r1 = x - hi.astype(F32)
    mid = r1.astype(BF16)
    lo = (r1 - mid.astype(F32)).astype(BF16)
    return (jnp.dot(hi, bd, preferred_element_type=F32) + jnp.dot(mid, bd, preferred_element_type=F32)
            + jnp.dot(lo, bd, preferred_element_type=F32))


def _sigmoid(x):
    return 1.0 / (1.0 + jnp.exp(-x))


def _silu_grad(x, s):
    return s * (1.0 + x * (1.0 - s))


def _block_ones(n, seg):
    i = np.arange(n)
    return jnp.asarray((i[:, None] // seg == i[None, :] // seg).astype(np.float32), dtype=BF16)


def _tri_incl():
    i = np.arange(CHUNK)
    return jnp.asarray((i[:, None] >= i[None, :]).astype(np.float32))


def _t5_bucket(dist):
    max_exact = N_BUCKETS // 2
    d = np.maximum(dist, 1).astype(np.float64)
    large = max_exact + (np.log(d / max_exact) / math.log(MAX_DISTANCE / max_exact)
                         * (N_BUCKETS - max_exact)).astype(np.int32)
    large = np.minimum(large, N_BUCKETS - 1)
    return np.where(dist < max_exact, dist, large).astype(np.int32)


def _bucket_tables():
    qi = np.arange(BLK)[:, None]
    kj = np.arange(2 * BLK)[None, :]
    step = qi - kj + BLK
    return jnp.asarray(np.stack([_t5_bucket(np.clip(step, 0, None) * r) for _, r in PATTERNS]))


def _in_proj(x, norm_w, wt_bf):
    s = x.shape[0]
    tm = 256

    def body(x_ref, nw_ref, w_ref, o_ref):
        xv = x_ref[...]
        rstd = lax.rsqrt(jnp.mean(xv * xv, axis=-1, keepdims=True) + EPS)
        h = (xv * rstd * nw_ref[...]).astype(BF16)
        o_ref[...] = _dot_nt(h, w_ref[...])

    return pl.pallas_call(
        body, grid=(s // tm,),
        in_specs=[pl.BlockSpec((tm, D_MODEL), lambda i: (i, 0)),
                  pl.BlockSpec((1, D_MODEL), lambda i: (0, 0)),
                  pl.BlockSpec((D_IN_PAD, D_MODEL), lambda i: (0, 0))],
        out_specs=pl.BlockSpec((tm, D_IN_PAD), lambda i: (i, 0)),
        out_shape=_sds((s, D_IN_PAD)), compiler_params=_cp(("parallel",)), name="in_proj")(x, norm_w, wt_bf)


def _conv_group(cur, halo, w_ref, c):
    rows = cur.shape[0]
    lanes = slice(128 * c, 128 * c + 128)
    xcat = jnp.concatenate([halo, cur], axis=0)
    y = cur * w_ref[CONV_W - 1:CONV_W, lanes]
    for k in range(1, CONV_W):
        sh = pltpu.roll(xcat, k, 0)[8:8 + rows]
        y = y + sh * w_ref[CONV_W - 1 - k:CONV_W - k, lanes]
    return y


def _beta_g(ba, alog_l, dtb_l):
    lane = lax.broadcasted_iota(jnp.int32, ba.shape, 1)
    sig_b = _sigmoid(ba)
    t = ba + dtb_l
    softplus = jnp.maximum(t, 0.0) + jnp.log(1.0 + jnp.exp(-jnp.abs(t)))
    nega = -jnp.exp(alog_l)
    g = nega * softplus
    out = jnp.where(lane < DN_HEADS, sig_b, jnp.where(lane < 2 * DN_HEADS, g, 0.0))
    return out, lane, sig_b, t, nega, g


def _dn_pre(proj, conv_w8, alog_l, dtb_l):
    s = proj.shape[0]
    tr = 512
    nh = tr // 8

    def body(u_ref, halo_ref, ba_ref, w_ref, al_ref, dt_ref, q_ref, k_ref, v_ref, bg_ref):
        i = pl.program_id(0)
        keep = (i > 0).astype(F32)
        for c in range(12):
            lanes = slice(128 * c, 128 * c + 128)
            y = _conv_group(u_ref[:, lanes], halo_ref[:, lanes] * keep, w_ref, c)
            sv = y * _sigmoid(y)
            if c < 8:
                rs = lax.rsqrt(jnp.sum(sv * sv, axis=1, keepdims=True) + EPS)
                n = sv * rs
                if c < 4:
                    q_ref[:, lanes] = n * (DN_HD ** -0.5)
                else:
                    k_ref[:, slice(128 * (c - 4), 128 * (c - 3))] = n
            else:
                v_ref[:, slice(128 * (c - 8), 128 * (c - 7))] = sv
        bg_ref[...] = _beta_g(ba_ref[...], al_ref[...], dt_ref[...])[0]

    return pl.pallas_call(
        body, grid=(s // tr,),
        in_specs=[pl.BlockSpec((tr, 1536), lambda i: (i, 0)),
                  pl.BlockSpec((8, 1536), lambda i: (jnp.maximum(i * nh - 1, 0), 0)),
                  pl.BlockSpec((tr, 128), lambda i: (i, COL_BA_128)),
                  pl.BlockSpec((8, 1536), lambda i: (0, 0)),
                  pl.BlockSpec((1, 128), lambda i: (0, 0)),
                  pl.BlockSpec((1, 128), lambda i: (0, 0))],
        out_specs=[pl.BlockSpec((tr, 512), lambda i: (i, 0))] * 3 + [pl.BlockSpec((tr, 128), lambda i: (i, 0))],
        out_shape=[_sds((s, 512))] * 3 + [_sds((s, 128))],
        compiler_params=_cp(("parallel",)), name="dn_pre")(proj, proj, proj, conv_w8, alog_l, dtb_l)


CPS = 2


def _chunk_cumsum(bg, bgt, tri):
    return _mmx(tri, bg), _mmx_nt(bgt, tri)


def _chunk_common(bg, gc_col, gc_row, h):
    gcc = gc_col[:, DN_HEADS + h:DN_HEADS + h + 1]
    gcr = gc_row[DN_HEADS + h:DN_HEADS + h + 1, :]
    beta = bg[:, h:h + 1]
    ii = lax.broadcasted_iota(jnp.int32, (CHUNK, CHUNK), 0)
    jj = lax.broadcasted_iota(jnp.int32, (CHUNK, CHUNK), 1)
    incl = ii >= jj
    strict = ii > jj
    decay = jnp.exp(jnp.where(incl, gcc - gcr, NEG))
    gl = gcc[CHUNK - 1:CHUNK, :]
    return gcc, beta, incl, strict, decay, gl


def _dn_prep(qn, kn, v, bg, bgt, tri):
    s = qn.shape[0]
    nc = s // CHUNK

    def body(q_ref, k_ref, v_ref, bg_ref, bgt_ref, tri_ref,
             u_ref, w_ref, qd_ref, kt_ref, attn_ref, t_ref, gl_ref):
        tri_v = tri_ref[...]
        ii = lax.broadcasted_iota(jnp.int32, (CHUNK, CHUNK), 0)
        jj = lax.broadcasted_iota(jnp.int32, (CHUNK, CHUNK), 1)
        eye = (ii == jj).astype(F32)

        def chain(cc, h, bgv, gc_col, gc_row):
            rows = slice(CHUNK * cc, CHUNK * cc + CHUNK)
            lanes = slice(128 * h, 128 * h + 128)
            gcc, beta, incl, strict, decay, gl = _chunk_common(bgv, gc_col, gc_row, h)
            q = q_ref[rows, lanes]
            k = k_ref[rows, lanes]
            vv = v_ref[rows, lanes]
            kb = k * beta
            egc = jnp.exp(gcc)
            a_mat = jnp.where(strict, _mm_nt(kb, k) * decay, 0.0)
            attn_ref[cc, h] = jnp.where(incl, _mm_nt(q, k) * decay, 0.0)
            qd_ref[rows, lanes] = q * egc
            kt_ref[rows, lanes] = k * jnp.exp(gl - gcc)
            gl_ref[cc, h] = jnp.broadcast_to(jnp.exp(gl), (1, 128))
            yield
            p = -a_mat
            t = eye + p
            for _ in range(5):
                ps = _split(p)
                p = _mm3(ps, ps)
                yield
                t = t + _mm3(_split(t), _split(p))
                yield
            t_ref[cc, h] = t
            ts = _split(t)
            u_ref[rows, lanes] = _mm3(ts, _split(vv * beta))
            w_ref[rows, lanes] = _mm3(ts, _split(kb * egc))

        gens = []
        for cc in range(CPS):
            bgv = bg_ref[CHUNK * cc:CHUNK * cc + CHUNK, :]
            gc_col, gc_row = _chunk_cumsum(bgv, bgt_ref[cc], tri_v)
            gens += [chain(cc, h, bgv, gc_col, gc_row) for h in range(DN_HEADS)]
        _interleave(gens)

    rows_step = CPS * CHUNK
    big = pl.BlockSpec((rows_step, 512), lambda n: (n, 0))
    sq = pl.BlockSpec((CPS, DN_HEADS, CHUNK, CHUNK), lambda n: (n, 0, 0, 0))
    return pl.pallas_call(
        body, grid=(nc // CPS,),
        in_specs=[big, big, big, pl.BlockSpec((rows_step, 128), lambda n: (n, 0)),
                  pl.BlockSpec((CPS, 8, CHUNK), lambda n: (n, 0, 0)),
                  pl.BlockSpec((CHUNK, CHUNK), lambda n: (0, 0))],
        out_specs=[big, big, big, big, sq, sq, pl.BlockSpec((CPS, DN_HEADS, 1, 128), lambda n: (n, 0, 0, 0))],
        out_shape=[_sds((s, 512))] * 4 + [_sds((nc, DN_HEADS, CHUNK, CHUNK))] * 2 + [_sds((nc, DN_HEADS, 1, 128))],
        compiler_params=_cp(("parallel",)), name="dn_prep")(qn, kn, v, bg, bgt, tri)


def _dn_scan(u, w, qd, kt, attn, gl):
    s = u.shape[0]
    nc = s // CHUNK

    def body(u_ref, w_ref, qd_ref, kt_ref, attn_ref, gl_ref, o_ref, vn_ref, sp_ref, st_ref):
        n = pl.program_id(0)

        @pl.when(n == 0)
        def _():
            st_ref[...] = jnp.zeros_like(st_ref)

        def chain(cc, h):
            rows = slice(CHUNK * cc, CHUNK * cc + CHUNK)
            lanes = slice(128 * h, 128 * h + 128)
            st = st_ref[h]
            sp_ref[cc, h] = st
            stb = st.astype(BF16)
            ws = _dot(w_ref[rows, lanes].astype(BF16), stb)
            qs = _dot(qd_ref[rows, lanes].astype(BF16), stb)
            yield
            vn = u_ref[rows, lanes] - ws
            vn_ref[rows, lanes] = vn
            vnb = vn.astype(BF16)
            o_ref[rows, lanes] = qs + _dot(attn_ref[cc, h].astype(BF16), vnb)
            st_ref[h] = st * gl_ref[cc, h] + _dot_tn(kt_ref[rows, lanes].astype(BF16), vnb)

        for cc in range(CPS):
            _interleave([chain(cc, h) for h in range(DN_HEADS)])

    big = pl.BlockSpec((CPS * CHUNK, 512), lambda n: (n, 0))
    return pl.pallas_call(
        body, grid=(nc // CPS,),
        in_specs=[big, big, big, big,
                  pl.BlockSpec((CPS, DN_HEADS, CHUNK, CHUNK), lambda n: (n, 0, 0, 0)),
                  pl.BlockSpec((CPS, DN_HEADS, 1, 128), lambda n: (n, 0, 0, 0))],
        out_specs=[big, big, pl.BlockSpec((CPS, DN_HEADS, DN_HD, DN_HD), lambda n: (n, 0, 0, 0))],
        out_shape=[_sds((s, 512)), _sds((s, 512)), _sds((nc, DN_HEADS, DN_HD, DN_HD))],
        scratch_shapes=[pltpu.VMEM((DN_HEADS, DN_HD, DN_HD), F32)],
        compiler_params=_cp(("arbitrary",)), name="dn_scan")(u, w, qd, kt, attn, gl)


R4 = PATTERNS[1][1]
R16 = PATTERNS[2][1]
TM = 256


def _pattern_spec(r):
    return pl.BlockSpec((r, TM // r, 512), lambda i: (0, i, 0))


def _pattern_shape(s, r):
    return _sds((r, s // r, 512))


SLABS = pltpu.VMEM((4, TM, 128), F32)


def _to_patterns(val, dsts, scr):
    for c in range(4):
        lanes = slice(128 * c, 128 * c + 128)
        scr[c] = val[:, lanes]
        for dst_ref, r in dsts:
            for a in range(r):
                dst_ref[a, :, lanes] = scr[c, pl.ds(a, TM // r, stride=r), :]


def _from_pattern(src_ref, r, scr):
    for c in range(4):
        for a in range(r):
            scr[c, pl.ds(a, TM // r, stride=r), :] = src_ref[a, :, 128 * c:128 * c + 128]
    return jnp.concatenate([scr[c] for c in range(4)], axis=1)


def _att_pre(proj, qw_t, kw_t, bd64):
    s = proj.shape[0]

    def body(q_ref, k_ref, v_ref, qw_ref, kw_ref, bd_ref,
             q1_ref, k1_ref, q4_ref, k4_ref, v4_ref, q16_ref, k16_ref, v16_ref, scr):
        bd = bd_ref[...]
        q = q_ref[...]
        k = k_ref[...]
        qn = q * lax.rsqrt(_segsum(q * q, bd) * (1.0 / ATT_HD) + EPS) * qw_ref[...] * (ATT_HD ** -0.5)
        kn = k * lax.rsqrt(_segsum(k * k, bd) * (1.0 / ATT_HD) + EPS) * kw_ref[...]
        q1_ref[...] = qn
        k1_ref[...] = kn
        _to_patterns(qn, ((q4_ref, R4), (q16_ref, R16)), scr)
        _to_patterns(kn, ((k4_ref, R4), (k16_ref, R16)), scr)
        _to_patterns(v_ref[...], ((v4_ref, R4), (v16_ref, R16)), scr)

    row = pl.BlockSpec((1, 512), lambda i: (0, 0))
    tok = pl.BlockSpec((TM, 512), lambda i: (i, 0))
    return pl.pallas_call(
        body, grid=(s // TM,),
        in_specs=[pl.BlockSpec((TM, 512), lambda i: (i, COL_ATT_Q)),
                  pl.BlockSpec((TM, 512), lambda i: (i, COL_ATT_K)),
                  pl.BlockSpec((TM, 512), lambda i: (i, COL_ATT_V)),
                  row, row, pl.BlockSpec((512, 512), lambda i: (0, 0))],
        out_specs=[tok, tok] + [_pattern_spec(R4)] * 3 + [_pattern_spec(R16)] * 3,
        out_shape=[_sds((s, 512))] * 2 + [_pattern_shape(s, R4)] * 3 + [_pattern_shape(s, R16)] * 3,
        scratch_shapes=[SLABS],
        compiler_params=_cp(("parallel",)), name="att_pre")(proj, proj, proj, qw_t, kw_t, bd64)


def _bias_fwd(rel_bias, buckets):
    def body(rb_ref, bk_ref, o_ref):
        for p in range(len(PATTERNS)):
            bk = bk_ref[p]
            for h in range(ATT_HEADS):
                acc = jnp.zeros((BLK, 2 * BLK), F32)
                for b in range(N_BUCKETS):
                    acc = jnp.where(bk == b, rb_ref[h, b], acc)
                o_ref[p, h] = acc

    return pl.pallas_call(
        body,
        in_specs=[pl.BlockSpec(memory_space=pltpu.SMEM), pl.BlockSpec(memory_space=pltpu.VMEM)],
        out_specs=pl.BlockSpec(memory_space=pltpu.VMEM),
        out_shape=_sds((len(PATTERNS), ATT_HEADS, BLK, 2 * BLK)),
        compiler_params=_cp(), name="bias_fwd")(rel_bias, buckets)


def _bias_bwd(ds_accs, buckets):
    def body(ds0_ref, ds1_ref, ds2_ref, bk_ref, o_ref):
        for h in range(ATT_HEADS):
            for b in range(N_BUCKETS):
                tot = jnp.zeros((), F32)
                for p, ds_ref in enumerate((ds0_ref, ds1_ref, ds2_ref)):
                    tot = tot + jnp.sum(jnp.where(bk_ref[p] == b, ds_ref[h], 0.0))
                o_ref[h, b] = tot

    return pl.pallas_call(
        body,
        in_specs=[pl.BlockSpec(memory_space=pltpu.VMEM)] * 4,
        out_specs=pl.BlockSpec(memory_space=pltpu.SMEM),
        out_shape=_sds((ATT_HEADS, N_BUCKETS)),
        compiler_params=_cp(), name="bias_bwd")(*ds_accs, buckets)


def _att_masks(has_prev):
    qi = lax.broadcasted_iota(jnp.int32, (BLK, BLK), 0)
    kj = lax.broadcasted_iota(jnp.int32, (BLK, BLK), 1)
    lane = lax.broadcasted_iota(jnp.int32, (BLK, 2 * ATT_HD), 1)
    return jnp.logical_and(kj >= qi, has_prev), kj <= qi, lane < ATT_HD


def _head_lanes(h):
    half = h % 2
    return slice(ATT_HD * h, ATT_HD * h + ATT_HD), slice(ATT_HD * half, ATT_HD * half + ATT_HD)


def _att_scores(qm, kp2, kc2, bias_h, mask_prev, mask_cur):
    s_prev = jnp.where(mask_prev, _dot_nt(qm, kp2) + bias_h[:, :BLK], NEG)
    s_cur = jnp.where(mask_cur, _dot_nt(qm, kc2) + bias_h[:, BLK:], NEG)
    return s_prev, s_cur


def _att_fwd(q, k, v, v_col, bias, p_idx, r, name):
    s = q.shape[0]
    nblk = s // BLK
    nseq = nblk // r

    def body(q_ref, kp_ref, kc_ref, vp_ref, vc_ref, b_ref, o_ref, lse_ref):
        j = pl.program_id(0)
        mask_prev, mask_cur, lo_half = _att_masks((j % nseq) != 0)

        def head(h, q2, kp2, kc2, vp2, vc2):
            out_l, pair_l = _head_lanes(h)
            sel = lo_half if h % 2 == 0 else jnp.logical_not(lo_half)
            qm = jnp.where(sel, q2, 0.0).astype(BF16)
            s_prev, s_cur = _att_scores(qm, kp2, kc2, b_ref[0, h], mask_prev, mask_cur)
            yield
            m = jnp.maximum(jnp.max(s_prev, axis=1, keepdims=True), jnp.max(s_cur, axis=1, keepdims=True))
            p_prev = jnp.exp(s_prev - m)
            p_cur = jnp.exp(s_cur - m)
            l = jnp.sum(p_prev, axis=1, keepdims=True) + jnp.sum(p_cur, axis=1, keepdims=True)
            yield
            o2 = _dot(p_prev.astype(BF16), vp2) + _dot(p_cur.astype(BF16), vc2)
            o_ref[:, out_l] = (o2 * (1.0 / l))[:, pair_l]
            lse_ref[:, out_l] = jnp.broadcast_to(m + jnp.log(l), (BLK, ATT_HD))

        gens = []
        for pp in range(ATT_HEADS // 2):
            lanes = slice(128 * pp, 128 * pp + 128)
            slabs = (q_ref[:, lanes], kp_ref[:, lanes].astype(BF16), kc_ref[:, lanes].astype(BF16),
                     vp_ref[:, lanes].astype(BF16), vc_ref[:, lanes].astype(BF16))
            gens += [head(2 * pp, *slabs), head(2 * pp + 1, *slabs)]
        _interleave(gens)

    cur = pl.BlockSpec((BLK, 512), lambda j: (j, 0))
    prev = pl.BlockSpec((BLK, 512), lambda j: (jnp.maximum(j - 1, 0), 0))
    vcur = pl.BlockSpec((BLK, 512), lambda j: (j, v_col))
    vprev = pl.BlockSpec((BLK, 512), lambda j: (jnp.maximum(j - 1, 0), v_col))
    return pl.pallas_call(
        body, grid=(nblk,),
        in_specs=[cur, prev, cur, vprev, vcur,
                  pl.BlockSpec((1, ATT_HEADS, BLK, 2 * BLK), lambda j: (p_idx, 0, 0, 0))],
        out_specs=[cur, cur],
        out_shape=[_sds((s, 512))] * 2,
        compiler_params=_cp(("parallel",)), name=name)(q, k, k, v, v, bias)


def _post_fwd(o_dn, proj, o_pats, lse_pats, dnw_t, bd128):
    s = o_dn.shape[0]

    def body(o_ref, z_ref, gate_ref, o1_ref, o4_ref, o16_ref, s1_ref, s4_ref, s16_ref, wn_ref, bd_ref,
             mixed_ref, oatt_ref, l1_ref, l4_ref, l16_ref, scr_a, scr_b, scr_c, scr_d):
        o = o_ref[...]
        z = z_ref[...]
        rstd = lax.rsqrt(_segsum(o * o, bd_ref[...]) * (1.0 / DN_HD) + EPS)
        y_dn = o * rstd * wn_ref[...] * (z * _sigmoid(z))
        mixed_ref[:, 0:512] = y_dn.astype(BF16)
        lses = (s1_ref[...], _from_pattern(s4_ref, R4, scr_a), _from_pattern(s16_ref, R16, scr_b))
        outs = (o1_ref[...], _from_pattern(o4_ref, R4, scr_c), _from_pattern(o16_ref, R16, scr_d))
        m = jnp.maximum(jnp.maximum(lses[0], lses[1]), lses[2])
        tot = jnp.exp(lses[0] - m) + jnp.exp(lses[1] - m) + jnp.exp(lses[2] - m)
        big_l = m + jnp.log(tot)
        acc = jnp.zeros_like(m)
        for lse_p, o_p in zip(lses, outs):
            acc = acc + jnp.exp(lse_p - big_l) * o_p
        gate = gate_ref[...]
        oatt_ref[...] = acc
        l1_ref[...] = big_l
        mixed_ref[:, 512:1024] = (acc * (gate * _sigmoid(gate))).astype(BF16)
        _to_patterns(big_l, ((l4_ref, R4), (l16_ref, R16)), scr_a)

    blk = pl.BlockSpec((TM, 512), lambda i: (i, 0))
    p4, p16 = _pattern_spec(R4), _pattern_spec(R16)
    return pl.pallas_call(
        body, grid=(s // TM,),
        in_specs=[blk, pl.BlockSpec((TM, 512), lambda i: (i, COL_Z)),
                  pl.BlockSpec((TM, 512), lambda i: (i, COL_GATE)), blk, p4, p16, blk, p4, p16,
                  pl.BlockSpec((1, 512), lambda i: (0, 0)), pl.BlockSpec((512, 512), lambda i: (0, 0))],
        out_specs=[pl.BlockSpec((TM, D_MODEL), lambda i: (i, 0)), blk, blk, p4, p16],
        out_shape=[_sds((s, D_MODEL), BF16), _sds((s, 512)), _sds((s, 512)), _pattern_shape(s, R4),
                   _pattern_shape(s, R16)],
        scratch_shapes=[SLABS] * 4,
        compiler_params=_cp(("parallel",)), name="post_fwd")(o_dn, proj, proj, *o_pats, *lse_pats, dnw_t, bd128)


def _out_fwd(x, mixed, w_out_bf, tgt):
    s = x.shape[0]
    tm = 256

    def body(x_ref, m_ref, w_ref, t_ref, dy_ref, loss_ref):
        i = pl.program_id(0)

        @pl.when(i == 0)
        def _():
            loss_ref[...] = jnp.zeros_like(loss_ref)

        y = x_ref[...] + jnp.dot(m_ref[...], w_ref[...], preferred_element_type=F32)
        err = y - t_ref[...]
        dy_ref[...] = err * (1.0 / D_MODEL)
        part = 0.5 * jnp.sum(jnp.mean(err * err, axis=-1, keepdims=True), axis=0, keepdims=True)
        loss_ref[...] = loss_ref[...] + part

    blk = pl.BlockSpec((tm, D_MODEL), lambda i: (i, 0))
    return pl.pallas_call(
        body, grid=(s // tm,),
        in_specs=[blk, blk, pl.BlockSpec((D_MODEL, D_MODEL), lambda i: (0, 0)), blk],
        out_specs=[blk, pl.BlockSpec((8, 128), lambda i: (0, 0))],
        out_shape=[_sds((s, D_MODEL)), _sds((8, 128))],
        compiler_params=_cp(("arbitrary",)), name="out_fwd")(x, mixed, w_out_bf, tgt)


def _out_bwd(dy, mixed, w_out_bf):
    s = dy.shape[0]
    tm = 256

    def body(dy_ref, m_ref, w_ref, dm_ref, dw_ref):
        i = pl.program_id(0)

        @pl.when(i == 0)
        def _():
            dw_ref[...] = jnp.zeros_like(dw_ref)

        dyb = dy_ref[...].astype(BF16)
        dm_ref[...] = lax.dot_general(dyb, w_ref[...], (((1,), (1,)), ((), ())), preferred_element_type=F32)
        dw_ref[...] = dw_ref[...] + lax.dot_general(m_ref[...], dyb, (((0,), (0,)), ((), ())),
                                                    preferred_element_type=F32)

    blk = pl.BlockSpec((tm, D_MODEL), lambda i: (i, 0))
    full = pl.BlockSpec((D_MODEL, D_MODEL), lambda i: (0, 0))
    return pl.pallas_call(
        body, grid=(s // tm,), in_specs=[blk, blk, full], out_specs=[blk, full],
        out_shape=[_sds((s, D_MODEL)), _sds((D_MODEL, D_MODEL))],
        compiler_params=_cp(("arbitrary",)), name="out_bwd")(dy, mixed, w_out_bf)


def _post_bwd(dmixed, o_dn, proj, o_att, dnw_t, bd128, bd64):
    s = o_dn.shape[0]
    tm = 256

    def body(ddn_ref, datt_ref, o_ref, z_ref, gate_ref, oatt_ref, wn_ref, bd128_ref, bd64_ref,
             do_ref, dz_ref, dgate_ref, doatt_ref, do4_ref, do16_ref, delta_ref, dl4_ref, dl16_ref, dnw_ref, scr):
        i = pl.program_id(0)

        @pl.when(i == 0)
        def _():
            dnw_ref[...] = jnp.zeros_like(dnw_ref)

        bd128v = bd128_ref[...]
        o = o_ref[...]
        z = z_ref[...]
        wn = wn_ref[...]
        dy = ddn_ref[...]
        rstd = lax.rsqrt(_segsum(o * o, bd128v) * (1.0 / DN_HD) + EPS)
        nrm = o * rstd
        sz = _sigmoid(z)
        dz_ref[...] = dy * nrm * wn * _silu_grad(z, sz)
        dn = dy * z * sz
        gw = dn * wn
        do_ref[...] = rstd * (gw - nrm * (_segsum(gw * nrm, bd128v) * (1.0 / DN_HD)))
        colsum = jnp.sum(dn * nrm, axis=0, keepdims=True)
        fold = colsum[:, 0:128] + colsum[:, 128:256] + colsum[:, 256:384] + colsum[:, 384:512]
        dnw_ref[...] = dnw_ref[...] + fold
        dya = datt_ref[...]
        gate = gate_ref[...]
        oatt = oatt_ref[...]
        sg = _sigmoid(gate)
        dgate_ref[...] = dya * oatt * _silu_grad(gate, sg)
        doa = dya * gate * sg
        doatt_ref[...] = doa
        delta = _segsum(doa * oatt, bd64_ref[...])
        delta_ref[...] = delta
        _to_patterns(doa, ((do4_ref, R4), (do16_ref, R16)), scr)
        _to_patterns(delta, ((dl4_ref, R4), (dl16_ref, R16)), scr)

    blk = pl.BlockSpec((tm, 512), lambda i: (i, 0))
    cst = pl.BlockSpec((512, 512), lambda i: (0, 0))
    p4, p16 = _pattern_spec(R4), _pattern_spec(R16)
    return pl.pallas_call(
        body, grid=(s // tm,),
        in_specs=[blk, pl.BlockSpec((tm, 512), lambda i: (i, 1)), blk,
                  pl.BlockSpec((tm, 512), lambda i: (i, COL_Z)), pl.BlockSpec((tm, 512), lambda i: (i, COL_GATE)),
                  blk, pl.BlockSpec((1, 512), lambda i: (0, 0)), cst, cst],
        out_specs=[blk, blk, blk, blk, p4, p16, blk, p4, p16, pl.BlockSpec((8, 128), lambda i: (0, 0))],
        out_shape=[_sds((s, 512))] * 4 + [_pattern_shape(s, R4), _pattern_shape(s, R16), _sds((s, 512)),
                                          _pattern_shape(s, R4), _pattern_shape(s, R16), _sds((8, 128))],
        scratch_shapes=[SLABS],
        compiler_params=_cp(("arbitrary",)), name="post_bwd")(dmixed, dmixed, o_dn, proj, proj, o_att, dnw_t,
                                                               bd128, bd64)


def _att_bwd(q, k, v, v_col, do, big_l, delta, bias, p_idx, r, name):
    s = q.shape[0]
    nblk = s // BLK
    nseq = nblk // r

    def body(q_ref, kp_ref, kc_ref, vp_ref, vc_ref, do_ref, l_ref, dl_ref, b_ref,
             dq_ref, dk_ref, dv_ref, ds_ref, dkc_ref, dvc_ref):
        j = pl.program_id(0)
        mask_prev, mask_cur, lo_half = _att_masks((j % nseq) != 0)

        @pl.when(j == 0)
        def _():
            dkc_ref[...] = jnp.zeros_like(dkc_ref)
            dvc_ref[...] = jnp.zeros_like(dvc_ref)
            ds_ref[...] = jnp.zeros_like(ds_ref)

        @pl.when(j < nblk)
        def _():
            def head(h, q2, do2, kp2, kc2, vp2, vc2):
                out_l, pair_l = _head_lanes(h)
                sel = lo_half if h % 2 == 0 else jnp.logical_not(lo_half)
                qm = jnp.where(sel, q2, 0.0).astype(BF16)
                dom = jnp.where(sel, do2, 0.0).astype(BF16)
                s_prev, s_cur = _att_scores(qm, kp2, kc2, b_ref[0, h], mask_prev, mask_cur)
                dp_prev = _dot_nt(dom, vp2)
                dp_cur = _dot_nt(dom, vc2)
                yield
                lh = l_ref[:, ATT_HD * h:ATT_HD * h + 1]
                dh = dl_ref[:, ATT_HD * h:ATT_HD * h + 1]
                p_prev = jnp.exp(s_prev - lh)
                p_cur = jnp.exp(s_cur - lh)
                ds_prev = p_prev * (dp_prev - dh)
                ds_cur = p_cur * (dp_cur - dh)
                ds_ref[h, :, 0:BLK] = ds_ref[h, :, 0:BLK] + ds_prev
                ds_ref[h, :, BLK:2 * BLK] = ds_ref[h, :, BLK:2 * BLK] + ds_cur
                dsb_prev, dsb_cur = ds_prev.astype(BF16), ds_cur.astype(BF16)
                pb_prev, pb_cur = p_prev.astype(BF16), p_cur.astype(BF16)
                yield
                qb, dob = q2.astype(BF16), do2.astype(BF16)
                dq_ref[:, out_l] = (_dot(dsb_prev, kp2) + _dot(dsb_cur, kc2))[:, pair_l]
                dk_ref[:, out_l] = dkc_ref[:, out_l] + _dot_tn(dsb_prev, qb)[:, pair_l]
                dv_ref[:, out_l] = dvc_ref[:, out_l] + _dot_tn(pb_prev, dob)[:, pair_l]
                yield
                dkc_ref[:, out_l] = _dot_tn(dsb_cur, qb)[:, pair_l]
                dvc_ref[:, out_l] = _dot_tn(pb_cur, dob)[:, pair_l]

            gens = []
            for pp in range(ATT_HEADS // 2):
                lanes = slice(128 * pp, 128 * pp + 128)
                slabs = (q_ref[:, lanes], do_ref[:, lanes], kp_ref[:, lanes].astype(BF16),
                         kc_ref[:, lanes].astype(BF16), vp_ref[:, lanes].astype(BF16),
                         vc_ref[:, lanes].astype(BF16))
                gens += [head(2 * pp, *slabs), head(2 * pp + 1, *slabs)]
            _interleave(gens)

        @pl.when(j == nblk)
        def _():
            dk_ref[...] = dkc_ref[...]
            dv_ref[...] = dvc_ref[...]

    last = nblk - 1
    cur = pl.BlockSpec((BLK, 512), lambda j: (jnp.minimum(j, last), 0))
    prev = pl.BlockSpec((BLK, 512), lambda j: (jnp.clip(j - 1, 0, last), 0))
    vcur = pl.BlockSpec((BLK, 512), lambda j: (jnp.minimum(j, last), v_col))
    vprev = pl.BlockSpec((BLK, 512), lambda j: (jnp.clip(j - 1, 0, last), v_col))
    return pl.pallas_call(
        body, grid=(nblk + 1,),
        in_specs=[cur, prev, cur, vprev, vcur, cur, cur, cur,
                  pl.BlockSpec((1, ATT_HEADS, BLK, 2 * BLK), lambda j: (p_idx, 0, 0, 0))],
        out_specs=[cur, prev, prev, pl.BlockSpec((ATT_HEADS, BLK, 2 * BLK), lambda j: (0, 0, 0))],
        out_shape=[_sds((s, 512))] * 3 + [_sds((ATT_HEADS, BLK, 2 * BLK))],
        scratch_shapes=[pltpu.VMEM((BLK, 512), F32), pltpu.VMEM((BLK, 512), F32)],
        compiler_params=_cp(("arbitrary",)), name=name)(q, k, k, v, v, do, big_l, delta, bias)


def _att_pre_bwd(dq_pats, dk_pats, dv_pats, proj, qw_t, kw_t, bd64):
    s = proj.shape[0]
    tm = TM

    def body(dq1_ref, dq4_ref, dq16_ref, dk1_ref, dk4_ref, dk16_ref, dv1_ref, dv4_ref, dv16_ref,
             q_ref, k_ref, qw_ref, kw_ref, bd_ref,
             dqr_ref, dkr_ref, dvr_ref, dqw_ref, dkw_ref, scr4, scr16):
        i = pl.program_id(0)

        @pl.when(i == 0)
        def _():
            dqw_ref[...] = jnp.zeros_like(dqw_ref)
            dkw_ref[...] = jnp.zeros_like(dkw_ref)

        bd = bd_ref[...]

        def total(d1_ref, d4_ref, d16_ref):
            return d1_ref[...] + _from_pattern(d4_ref, R4, scr4) + _from_pattern(d16_ref, R16, scr16)

        def one(d_refs, x_ref, w_ref, scale, dx_ref, dw_ref):
            dy = total(*d_refs) * scale
            x = x_ref[...]
            rstd = lax.rsqrt(_segsum(x * x, bd) * (1.0 / ATT_HD) + EPS)
            nrm = x * rstd
            dw_ref[...] = dw_ref[...] + jnp.sum(dy * nrm, axis=0, keepdims=True)
            g = dy * w_ref[...]
            dx_ref[...] = rstd * (g - nrm * (_segsum(g * nrm, bd) * (1.0 / ATT_HD)))

        one((dq1_ref, dq4_ref, dq16_ref), q_ref, qw_ref, ATT_HD ** -0.5, dqr_ref, dqw_ref)
        one((dk1_ref, dk4_ref, dk16_ref), k_ref, kw_ref, 1.0, dkr_ref, dkw_ref)
        dvr_ref[...] = total(dv1_ref, dv4_ref, dv16_ref)

    blk = pl.BlockSpec((tm, 512), lambda i: (i, 0))
    pats = [blk, _pattern_spec(R4), _pattern_spec(R16)]
    row = pl.BlockSpec((1, 512), lambda i: (0, 0))
    acc = pl.BlockSpec((8, 512), lambda i: (0, 0))
    return pl.pallas_call(
        body, grid=(s // tm,),
        in_specs=pats * 3 + [pl.BlockSpec((tm, 512), lambda i: (i, COL_ATT_Q)),
                             pl.BlockSpec((tm, 512), lambda i: (i, COL_ATT_K)), row, row,
                             pl.BlockSpec((512, 512), lambda i: (0, 0))],
        out_specs=[blk, blk, blk, acc, acc],
        out_shape=[_sds((s, 512))] * 3 + [_sds((8, 512))] * 2,
        scratch_shapes=[SLABS] * 2,
        compiler_params=_cp(("arbitrary",)), name="att_pre_bwd")(*dq_pats, *dk_pats, *dv_pats, proj, proj,
                                                                  qw_t, kw_t, bd64)


def _dn_scan_bwd(do, sp, qd, kt, w, vn, attn, gl):
    s = do.shape[0]
    nc = s // CHUNK

    def body(do_ref, sp_ref, qd_ref, kt_ref, w_ref, vn_ref, attn_ref, gl_ref,
             du_ref, dqd_ref, dkt_ref, dw_ref, dattn_ref, dgl_ref, ds_ref):
        n = pl.program_id(0)

        @pl.when(n == 0)
        def _():
            ds_ref[...] = jnp.zeros_like(ds_ref)

        def chain(cc, h):
            rows = slice(CHUNK * cc, CHUNK * cc + CHUNK)
            lanes = slice(128 * h, 128 * h + 128)
            dsn = ds_ref[h]
            st = sp_ref[cc, h]
            dsb, stb = dsn.astype(BF16), st.astype(BF16)
            dob = do_ref[rows, lanes].astype(BF16)
            vnb = vn_ref[rows, lanes].astype(BF16)
            dvn = _dot_tn(attn_ref[cc, h].astype(BF16), dob) + _dot(kt_ref[rows, lanes].astype(BF16), dsb)
            du_ref[rows, lanes] = dvn
            dqd_ref[rows, lanes] = _dot_nt(dob, stb)
            dattn_ref[cc, h] = _dot_nt(dob, vnb)
            dkt_ref[rows, lanes] = _dot_nt(vnb, dsb)
            tot = jnp.sum(jnp.sum(st * dsn, axis=1, keepdims=True), axis=0, keepdims=True)
            dgl_ref[cc, h] = jnp.broadcast_to(tot, (1, 128))
            qdo = _dot_tn(qd_ref[rows, lanes].astype(BF16), dob)
            yield
            dvb = dvn.astype(BF16)
            dw_ref[rows, lanes] = -_dot_nt(dvb, stb)
            ds_ref[h] = qdo + dsn * gl_ref[cc, h] - _dot_tn(w_ref[rows, lanes].astype(BF16), dvb)

        for cc in reversed(range(CPS)):
            _interleave([chain(cc, h) for h in range(DN_HEADS)])

    nsteps = nc // CPS
    big = pl.BlockSpec((CPS * CHUNK, 512), lambda n: (nsteps - 1 - n, 0))
    sq = pl.BlockSpec((CPS, DN_HEADS, CHUNK, CHUNK), lambda n: (nsteps - 1 - n, 0, 0, 0))
    glb = pl.BlockSpec((CPS, DN_HEADS, 1, 128), lambda n: (nsteps - 1 - n, 0, 0, 0))
    return pl.pallas_call(
        body, grid=(nsteps,),
        in_specs=[big, pl.BlockSpec((CPS, DN_HEADS, DN_HD, DN_HD), lambda n: (nsteps - 1 - n, 0, 0, 0)),
                  big, big, big, big, sq, glb],
        out_specs=[big, big, big, big, sq, glb],
        out_shape=[_sds((s, 512))] * 4 + [_sds((nc, DN_HEADS, CHUNK, CHUNK)), _sds((nc, DN_HEADS, 1, 128))],
        scratch_shapes=[pltpu.VMEM((DN_HEADS, DN_HD, DN_HD), F32)],
        compiler_params=_cp(("arbitrary",)), name="dn_scan_bwd")(do, sp, qd, kt, w, vn, attn, gl)


def _dn_prep_bwd(qn, kn, v, bg, bgt, tri, t_inv, attn, u, w, du, dw, dqd, dkt, dattn, dgl):
    s = qn.shape[0]
    nc = s // CHUNK

    def body(q_ref, k_ref, v_ref, bg_ref, bgt_ref, tri_ref, t_ref, attn_ref, u_ref, w_ref,
             du_ref, dw_ref, dqd_ref, dkt_ref, dattn_ref, dgl_ref,
             dq_ref, dk_ref, dv_ref, dbg_ref):
        tri_v = tri_ref[...]
        lane = lax.broadcasted_iota(jnp.int32, (CHUNK, 128), 1)
        rowi = lax.broadcasted_iota(jnp.int32, (CHUNK, 1), 0)
        ones_b = jnp.ones((CHUNK, 128), BF16)
        parts = [[] for _ in range(CPS)]

        def chain(cc, h, bgv, gc_col, gc_row):
            rows = slice(CHUNK * cc, CHUNK * cc + CHUNK)
            lanes = slice(128 * h, 128 * h + 128)
            gcc, beta, incl, strict, decay, gl = _chunk_common(bgv, gc_col, gc_row, h)
            q = q_ref[rows, lanes]
            k = k_ref[rows, lanes]
            vv = v_ref[rows, lanes]
            ts = _split(t_ref[cc, h])
            egc = jnp.exp(gcc)
            kb = k * beta
            a_mat = jnp.where(strict, _mm_nt(kb, k) * decay, 0.0)
            dvb = _mm3_tn(ts, _split(du_ref[rows, lanes]))
            dkbg = _mm3_tn(ts, _split(dw_ref[rows, lanes]))
            yield
            d_a = jnp.where(strict, -(_mm_nt(dvb, u_ref[rows, lanes]) + _mm_nt(dkbg, w_ref[rows, lanes])), 0.0)
            d_m = d_a * decay
            dattn_m = jnp.where(incl, dattn_ref[cc, h], 0.0)
            dqk = dattn_m * decay
            e_hi, e_lo = _split(d_a * a_mat + dattn_m * attn_ref[cc, h])
            yield
            dkb = _mm(d_m, k)
            dk = _mm_tn(d_m, kb) + _mm_tn(dqk, q)
            dq = _mm(dqk, k)
            e_colsum = (_dot_tn(e_hi, ones_b) + _dot_tn(e_lo, ones_b))[:, 0:1]
            e_rowsum = jnp.sum(e_hi.astype(F32) + e_lo.astype(F32), axis=1, keepdims=True)
            yield
            dgc = e_rowsum - e_colsum
            dqd = dqd_ref[rows, lanes]
            dq = dq + dqd * egc
            dgc = dgc + jnp.sum(dqd * q, axis=1, keepdims=True) * egc
            tail = jnp.exp(gl - gcc)
            dkt = dkt_ref[rows, lanes]
            dk = dk + dkt * tail
            r = jnp.sum(dkt * k, axis=1, keepdims=True) * tail
            dgc = dgc - r
            dgl_tot = jnp.sum(r, axis=0, keepdims=True) + dgl_ref[cc, h][:, 0:1] * jnp.exp(gl)
            rk = jnp.sum(dkbg * k, axis=1, keepdims=True)
            dk = dk + dkbg * (beta * egc) + dkb * beta
            dbeta = rk * egc + jnp.sum(dkb * k, axis=1, keepdims=True) + jnp.sum(dvb * vv, axis=1, keepdims=True)
            dgc = dgc + rk * beta * egc
            dgc = dgc + jnp.where(rowi == CHUNK - 1, dgl_tot, 0.0)
            dq_ref[rows, lanes] = dq
            dk_ref[rows, lanes] = dk
            dv_ref[rows, lanes] = dvb * beta
            parts[cc].append((h, dgc, dbeta))

        gens = []
        for cc in range(CPS):
            bgv = bg_ref[CHUNK * cc:CHUNK * cc + CHUNK, :]
            gc_col, gc_row = _chunk_cumsum(bgv, bgt_ref[cc], tri_v)
            gens += [chain(cc, h, bgv, gc_col, gc_row) for h in range(DN_HEADS)]
        _interleave(gens)
        for cc in range(CPS):
            dgc_mat = jnp.zeros((CHUNK, 128), F32)
            dbeta_mat = jnp.zeros((CHUNK, 128), F32)
            for h, dgc, dbeta in parts[cc]:
                dgc_mat = dgc_mat + jnp.where(lane == DN_HEADS + h, dgc, 0.0)
                dbeta_mat = dbeta_mat + jnp.where(lane == h, dbeta, 0.0)
            dbg_ref[CHUNK * cc:CHUNK * cc + CHUNK, :] = _mmx_tn(tri_v, dgc_mat) + dbeta_mat

    big = pl.BlockSpec((CPS * CHUNK, 512), lambda n: (n, 0))
    sq = pl.BlockSpec((CPS, DN_HEADS, CHUNK, CHUNK), lambda n: (n, 0, 0, 0))
    glb = pl.BlockSpec((CPS, DN_HEADS, 1, 128), lambda n: (n, 0, 0, 0))
    small = pl.BlockSpec((CPS * CHUNK, 128), lambda n: (n, 0))
    return pl.pallas_call(
        body, grid=(nc // CPS,),
        in_specs=[big, big, big, small, pl.BlockSpec((CPS, 8, CHUNK), lambda n: (n, 0, 0)),
                  pl.BlockSpec((CHUNK, CHUNK), lambda n: (0, 0)), sq, sq, big, big,
                  big, big, big, big, sq, glb],
        out_specs=[big, big, big, small],
        out_shape=[_sds((s, 512))] * 3 + [_sds((s, 128))],
        compiler_params=_cp(("parallel",)), name="dn_prep_bwd")(qn, kn, v, bg, bgt, tri, t_inv, attn, u, w,
                                                                  du, dw, dqd, dkt, dattn, dgl)


def _dn_pre_bwd(dqn, dkn, dv, dbg, proj, conv_w8, alog_l, dtb_l):
    s = proj.shape[0]
    tr = 512
    nh = tr // 8

    def body(dq_ref, dk_ref, dv_ref, dbg_ref, u_ref, halo_ref, ba_ref, w_ref, al_ref, dt_ref,
             dy_ref, dba_ref, dsm_ref):
        i = pl.program_id(0)

        @pl.when(i == 0)
        def _():
            dsm_ref[...] = jnp.zeros_like(dsm_ref)

        keep = (i > 0).astype(F32)
        for c in range(12):
            lanes = slice(128 * c, 128 * c + 128)
            y = _conv_group(u_ref[:, lanes], halo_ref[:, lanes] * keep, w_ref, c)
            sg = _sigmoid(y)
            sv = y * sg
            if c < 8:
                rs = lax.rsqrt(jnp.sum(sv * sv, axis=1, keepdims=True) + EPS)
                n = sv * rs
                if c < 4:
                    dn = dq_ref[:, lanes] * (DN_HD ** -0.5)
                else:
                    dn = dk_ref[:, slice(128 * (c - 4), 128 * (c - 3))]
                dsv = rs * (dn - n * jnp.sum(dn * n, axis=1, keepdims=True))
            else:
                dsv = dv_ref[:, slice(128 * (c - 8), 128 * (c - 7))]
            dy_ref[:, lanes] = dsv * _silu_grad(y, sg)
        _, lane, sig_b, t, nega, g = _beta_g(ba_ref[...], al_ref[...], dt_ref[...])
        dbg = dbg_ref[...]
        da = dbg * nega * _sigmoid(t)
        is_b = lane < DN_HEADS
        is_a = jnp.logical_and(lane >= DN_HEADS, lane < 2 * DN_HEADS)
        dba_ref[...] = jnp.where(is_b, dbg * sig_b * (1.0 - sig_b), jnp.where(is_a, da, 0.0))
        d_alog = jnp.sum(jnp.where(is_a, dbg * g, 0.0), axis=0, keepdims=True)
        d_dtb = jnp.sum(jnp.where(is_a, da, 0.0), axis=0, keepdims=True)
        row = lax.broadcasted_iota(jnp.int32, (8, 128), 0)
        dsm_ref[...] = dsm_ref[...] + jnp.where(row == 0, d_alog, jnp.where(row == 1, d_dtb, 0.0))

    blk = pl.BlockSpec((tr, 512), lambda i: (i, 0))
    return pl.pallas_call(
        body, grid=(s // tr,),
        in_specs=[blk, blk, blk, pl.BlockSpec((tr, 128), lambda i: (i, 0)),
                  pl.BlockSpec((tr, 1536), lambda i: (i, 0)),
                  pl.BlockSpec((8, 1536), lambda i: (jnp.maximum(i * nh - 1, 0), 0)),
                  pl.BlockSpec((tr, 128), lambda i: (i, COL_BA_128)),
                  pl.BlockSpec((8, 1536), lambda i: (0, 0)),
                  pl.BlockSpec((1, 128), lambda i: (0, 0)), pl.BlockSpec((1, 128), lambda i: (0, 0))],
        out_specs=[pl.BlockSpec((tr, 1536), lambda i: (i, 0)), pl.BlockSpec((tr, 128), lambda i: (i, 0)),
                   pl.BlockSpec((8, 128), lambda i: (0, 0))],
        out_shape=[_sds((s, 1536)), _sds((s, 128)), _sds((8, 128))],
        compiler_params=_cp(("arbitrary",)), name="dn_pre_bwd")(dqn, dkn, dv, dbg, proj, proj, proj, conv_w8,
                                                                 alog_l, dtb_l)


def _conv_bwd(dy, proj, conv_w8):
    s = dy.shape[0]
    tr = 512
    nh = tr // 8
    nblk = s // tr

    def body(dy_ref, dyn_ref, u_ref, halo_ref, w_ref, du_ref, dw_ref):
        i = pl.program_id(0)

        @pl.when(i == 0)
        def _():
            dw_ref[...] = jnp.zeros_like(dw_ref)

        keep_prev = (i > 0).astype(F32)
        keep_next = (i < nblk - 1).astype(F32)
        row = lax.broadcasted_iota(jnp.int32, (8, 128), 0)
        for c in range(12):
            lanes = slice(128 * c, 128 * c + 128)
            dyc = dy_ref[:, lanes]
            dcat = jnp.concatenate([dyc, dyn_ref[:, lanes] * keep_next], axis=0)
            xcat = jnp.concatenate([halo_ref[:, lanes] * keep_prev, u_ref[:, lanes]], axis=0)
            du = dyc * w_ref[CONV_W - 1:CONV_W, lanes]
            dwc = jnp.where(row == CONV_W - 1, jnp.sum(dyc * u_ref[:, lanes], axis=0, keepdims=True), 0.0)
            for k in range(1, CONV_W):
                du = du + pltpu.roll(dcat, tr + 8 - k, 0)[0:tr] * w_ref[CONV_W - 1 - k:CONV_W - k, lanes]
                ush = pltpu.roll(xcat, k, 0)[8:8 + tr]
                dwc = dwc + jnp.where(row == CONV_W - 1 - k, jnp.sum(dyc * ush, axis=0, keepdims=True), 0.0)
            du_ref[:, lanes] = du
            dw_ref[:, lanes] = dw_ref[:, lanes] + dwc

    return pl.pallas_call(
        body, grid=(nblk,),
        in_specs=[pl.BlockSpec((tr, 1536), lambda i: (i, 0)),
                  pl.BlockSpec((8, 1536), lambda i: (jnp.minimum((i + 1) * nh, s // 8 - 1), 0)),
                  pl.BlockSpec((tr, 1536), lambda i: (i, 0)),
                  pl.BlockSpec((8, 1536), lambda i: (jnp.maximum(i * nh - 1, 0), 0)),
                  pl.BlockSpec((8, 1536), lambda i: (0, 0))],
        out_specs=[pl.BlockSpec((tr, 1536), lambda i: (i, 0)), pl.BlockSpec((8, 1536), lambda i: (0, 0))],
        out_shape=[_sds((s, 1536)), _sds((8, 1536))],
        compiler_params=_cp(("arbitrary",)), name="conv_bwd")(dy, dy, proj, proj, conv_w8)


def _in_bwd_dx(d_qkv_dn, dz, dq_att, dk_att, dv_att, dgate, dba, w_bf, x, dy, norm_w):
    s = x.shape[0]
    tm = 256

    def body(a_ref, b_ref, c_ref, d_ref, e_ref, f_ref, g_ref, w_ref, x_ref, dy_ref, nw_ref,
             dx_ref, dp_ref, dnw_ref):
        i = pl.program_id(0)

        @pl.when(i == 0)
        def _():
            dnw_ref[...] = jnp.zeros_like(dnw_ref)

        dp = jnp.concatenate([r[...].astype(BF16) for r in (a_ref, b_ref, c_ref, d_ref, e_ref, f_ref, g_ref)],
                             axis=1)
        dp_ref[...] = dp
        dh = _dot(dp, w_ref[...])
        xv = x_ref[...]
        rstd = lax.rsqrt(jnp.mean(xv * xv, axis=-1, keepdims=True) + EPS)
        xh = xv * rstd
        dnw_ref[...] = dnw_ref[...] + jnp.sum(dh * xh, axis=0, keepdims=True)
        g = dh * nw_ref[...]
        dx_ref[...] = rstd * (g - xh * jnp.mean(g * xh, axis=-1, keepdims=True)) + dy_ref[...]

    def blk(n):
        return pl.BlockSpec((tm, n), lambda i: (i, 0))

    return pl.pallas_call(
        body, grid=(s // tm,),
        in_specs=[blk(1536), blk(512), blk(512), blk(512), blk(512), blk(512), blk(128),
                  pl.BlockSpec((D_IN_PAD, D_MODEL), lambda i: (0, 0)), blk(D_MODEL), blk(D_MODEL),
                  pl.BlockSpec((1, D_MODEL), lambda i: (0, 0))],
        out_specs=[blk(D_MODEL), blk(D_IN_PAD), pl.BlockSpec((8, D_MODEL), lambda i: (0, 0))],
        out_shape=[_sds((s, D_MODEL)), _sds((s, D_IN_PAD), BF16), _sds((8, D_MODEL))],
        compiler_params=_cp(("arbitrary",)), name="in_bwd_dx")(d_qkv_dn, dz, dq_att, dk_att, dv_att, dgate, dba,
                                                                w_bf, x, dy, norm_w)


def _in_bwd_dw(x, norm_w, dp):
    s = x.shape[0]
    tm = 256
    tn = D_IN_PAD // 3

    def body(x_ref, nw_ref, dp_ref, dw_ref):
        i = pl.program_id(1)

        @pl.when(i == 0)
        def _():
            dw_ref[...] = jnp.zeros_like(dw_ref)

        xv = x_ref[...]
        rstd = lax.rsqrt(jnp.mean(xv * xv, axis=-1, keepdims=True) + EPS)
        h = (xv * rstd * nw_ref[...]).astype(BF16)
        dw_ref[...] = dw_ref[...] + _dot_tn(dp_ref[...], h)

    return pl.pallas_call(
        body, grid=(3, s // tm),
        in_specs=[pl.BlockSpec((tm, D_MODEL), lambda j, i: (i, 0)), pl.BlockSpec((1, D_MODEL), lambda j, i: (0, 0)),
                  pl.BlockSpec((tm, tn), lambda j, i: (i, j))],
        out_specs=pl.BlockSpec((tn, D_MODEL), lambda j, i: (j, 0)),
        out_shape=_sds((D_IN_PAD, D_MODEL)),
        compiler_params=_cp(("parallel", "arbitrary")), name="in_bwd_dw")(x, norm_w, dp)


def _flat(a):
    return a.reshape(-1, a.shape[-1])


def _as_pattern(a, r):
    return a if r == 1 else a.reshape(r, a.shape[0] // r, a.shape[1])


D_SHARD = D_IN // N_CHIPS
BA_START = 4 * D_DN
BA_PACKED = 4096
S1_HEAD = BA_START - D_SHARD
S1_BA = 2 * D_SHARD - BA_START


def _pack_rows(g):
    pad = jnp.zeros((D_IN_PAD - D_IN, g.shape[2]), g.dtype)
    return jnp.concatenate([g[0], g[1][:S1_HEAD], g[2][2 * DN_HEADS - S1_BA:], g[3], g[1][S1_HEAD:],
                            g[2][:2 * DN_HEADS - S1_BA], pad], axis=0)


def _unpack_rows(p):
    mid = BA_PACKED + S1_BA
    return jnp.stack([p[0:D_SHARD],
                      jnp.concatenate([p[D_SHARD:BA_START], p[BA_PACKED:mid]], axis=0),
                      jnp.concatenate([p[mid:BA_PACKED + 2 * DN_HEADS], p[BA_START:BA_START + S1_HEAD]], axis=0),
                      p[BA_START + S1_HEAD:BA_PACKED]])


def _lane_row(vec, offset):
    return jnp.pad(vec.reshape(1, -1), ((0, 0), (offset, 128 - offset - vec.shape[0])))


def _local_step(x, tgt, norm_w, w_bf, conv_w, a_log, dt_bias, dn_norm_w, q_norm_w, k_norm_w, rel_bias, w_out_bf):
    s = x.shape[0]
    nc = s // CHUNK
    conv_w8 = jnp.pad(conv_w, ((0, 8 - CONV_W), (0, 0)))
    alog_l = _lane_row(a_log.reshape(-1), DN_HEADS)
    dtb_l = _lane_row(dt_bias.reshape(-1), DN_HEADS)
    dnw_t = jnp.tile(dn_norm_w.reshape(1, DN_HD), (1, DN_HEADS))
    qw_t = jnp.tile(q_norm_w.reshape(1, ATT_HD), (1, ATT_HEADS))
    kw_t = jnp.tile(k_norm_w.reshape(1, ATT_HD), (1, ATT_HEADS))
    bd128 = _block_ones(512, DN_HD)
    bd64 = _block_ones(512, ATT_HD)
    tri = _tri_incl()
    buckets = _bucket_tables()

    proj = _in_proj(x, norm_w, w_bf)
    qn, kn, v_dn, bg = _dn_pre(proj, conv_w8, alog_l, dtb_l)
    bgt = bg[:, 0:8].reshape(nc, CHUNK, 8).transpose(0, 2, 1)
    u, w, qd, kt, attn, t_inv, gl = _dn_prep(qn, kn, v_dn, bg, bgt, tri)
    o_dn, vn, sp = _dn_scan(u, w, qd, kt, attn, gl)
    q1, k1, q4, k4, v4, q16, k16, v16 = _att_pre(proj, qw_t, kw_t, bd64)
    rs = [r for _, r in PATTERNS]
    qkv = [(q1, k1, proj, COL_ATT_V), (_flat(q4), _flat(k4), _flat(v4), 0), (_flat(q16), _flat(k16), _flat(v16), 0)]
    bias = _bias_fwd(rel_bias, buckets)
    o_pats, lse_pats = [], []
    for p, r in enumerate(rs):
        o_p, lse_p = _att_fwd(*qkv[p], bias, p, r, "att_fwd_r%d" % r)
        o_pats.append(_as_pattern(o_p, r))
        lse_pats.append(_as_pattern(lse_p, r))
    mixed, o_att, l1, l4, l16 = _post_fwd(o_dn, proj, o_pats, lse_pats, dnw_t, bd128)
    dy, loss_blk = _out_fwd(x, mixed, w_out_bf, tgt)

    dmixed, d_w_out = _out_bwd(dy, mixed, w_out_bf)
    do_dn, dz, dgate, do1, do4, do16, dl1, dl4, dl16, d_dnw = _post_bwd(dmixed, o_dn, proj, o_att, dnw_t, bd128, bd64)
    side = [(do1, l1, dl1), (_flat(do4), _flat(l4), _flat(dl4)), (_flat(do16), _flat(l16), _flat(dl16))]
    dq_pats, dk_pats, dv_pats, ds_accs = [], [], [], []
    for p, r in enumerate(rs):
        dq_p, dk_p, dv_p, ds_p = _att_bwd(*qkv[p], *side[p], bias, p, r, "att_bwd_r%d" % r)
        dq_pats.append(_as_pattern(dq_p, r))
        dk_pats.append(_as_pattern(dk_p, r))
        dv_pats.append(_as_pattern(dv_p, r))
        ds_accs.append(ds_p)
    d_rel_bias = _bias_bwd(ds_accs, buckets)
    dq_att, dk_att, dv_att, d_qw, d_kw = _att_pre_bwd(dq_pats, dk_pats, dv_pats, proj, qw_t, kw_t, bd64)
    du, dqd, dkt, dw, dattn, dgl = _dn_scan_bwd(do_dn, sp, qd, kt, w, vn, attn, gl)
    dqn, dkn, dv_dn, dbg = _dn_prep_bwd(qn, kn, v_dn, bg, bgt, tri, t_inv, attn, u, w, du, dw, dqd, dkt, dattn, dgl)
    dyc, dba, dsm = _dn_pre_bwd(dqn, dkn, dv_dn, dbg, proj, conv_w8, alog_l, dtb_l)
    d_qkv_dn, d_conv8 = _conv_bwd(dyc, proj, conv_w8)
    grad_x, dp, d_nw8 = _in_bwd_dx(d_qkv_dn, dz, dq_att, dk_att, dv_att, dgate, dba, w_bf, x, dy, norm_w)
    d_w_in_t = _in_bwd_dw(x, norm_w, dp)

    grads = dict(
        norm_w=d_nw8[0:1, :],
        w_in_t=d_w_in_t,
        conv_w=d_conv8[0:CONV_W, :],
        a_log=dsm[0:1, DN_HEADS:2 * DN_HEADS],
        dt_bias=dsm[1:2, DN_HEADS:2 * DN_HEADS],
        dn_norm_w=d_dnw[0:1, :],
        q_norm_w=d_qw[0:1, :].reshape(ATT_HEADS, ATT_HD),
        k_norm_w=d_kw[0:1, :].reshape(ATT_HEADS, ATT_HD),
        rel_bias=d_rel_bias,
        w_out=d_w_out,
    )
    return loss_blk[0, 0], grad_x, grads


MESH_ID = pl.DeviceIdType.MESH
ANY = pl.BlockSpec(memory_space=pl.ANY)


def _position():
    return lax.axis_index("x"), lax.axis_index("y"), lax.axis_index("c")


def _other_chips(x, y):
    return [(1 - x, y), (x, 1 - y), (1 - x, 1 - y)]


HALF = D_MODEL // 2


def _half(ref, c):
    return ref.at[:, pl.ds(pl.multiple_of(c * HALF, HALF), HALF)]


def _gather_weights(wt_s, w_out_s, conv_s):
    def body(a_ref, b_ref, c_ref, ga_ref, gb_ref, gc_ref, send_sems, recv_sems, loc_sems):
        x, y, c = _position()
        me = 2 * x + y
        sib = (x, y, 1 - c)
        big = ((a_ref, ga_ref), (b_ref, gb_ref))
        local = [pltpu.make_async_copy(src, dst.at[me], loc_sems.at[t])
                 for t, (src, dst) in enumerate(big + ((c_ref, gc_ref),))]
        for cp in local:
            cp.start()
        others = _other_chips(x, y)
        sends = []
        for j, (px, py) in enumerate(others):
            for t, (src, dst) in enumerate(big):
                k = 3 * j + t
                sends.append(pltpu.make_async_remote_copy(
                    src_ref=_half(src, c), dst_ref=_half(dst.at[me], c), send_sem=send_sems.at[k],
                    recv_sem=recv_sems.at[k], device_id=(px, py, c), device_id_type=MESH_ID))
            sends.append(pltpu.make_async_remote_copy(
                src_ref=c_ref, dst_ref=gc_ref.at[me], send_sem=send_sems.at[3 * j + 2],
                recv_sem=recv_sems.at[3 * j + 2], device_id=(px, py, c), device_id_type=MESH_ID))
        for cp in sends:
            cp.start()
        passed = []
        for j, (px, py) in enumerate(others):
            src_chip = 2 * px + py
            for t, (src, dst) in enumerate(big):
                landed = _half(dst.at[src_chip], c)
                pltpu.make_async_remote_copy(
                    src_ref=_half(src, c), dst_ref=landed, send_sem=send_sems.at[3 * j + t],
                    recv_sem=recv_sems.at[3 * j + t], device_id=(px, py, c), device_id_type=MESH_ID).wait_recv()
                k = 9 + 2 * j + t
                fwd = pltpu.make_async_remote_copy(
                    src_ref=landed, dst_ref=landed, send_sem=send_sems.at[k], recv_sem=recv_sems.at[k],
                    device_id=sib, device_id_type=MESH_ID)
                fwd.start()
                passed.append(fwd)
            pltpu.make_async_remote_copy(
                src_ref=c_ref, dst_ref=gc_ref.at[src_chip], send_sem=send_sems.at[3 * j + 2],
                recv_sem=recv_sems.at[3 * j + 2], device_id=(px, py, c), device_id_type=MESH_ID).wait_recv()
        for j, (px, py) in enumerate(others):
            src_chip = 2 * px + py
            for t, (src, dst) in enumerate(big):
                k = 9 + 2 * j + t
                theirs = _half(dst.at[src_chip], 1 - c)
                pltpu.make_async_remote_copy(
                    src_ref=theirs, dst_ref=theirs, send_sem=send_sems.at[k], recv_sem=recv_sems.at[k],
                    device_id=sib, device_id_type=MESH_ID).wait_recv()
        for cp in sends + passed:
            cp.wait_send()
        for cp in local:
            cp.wait()

    srcs = (wt_s, w_out_s, conv_s)
    n_sem = 9 + 6
    return pl.pallas_call(
        body, in_specs=[ANY] * 3, out_specs=[ANY] * 3,
        out_shape=[_sds((N_CHIPS,) + a.shape, a.dtype) for a in srcs],
        scratch_shapes=[pltpu.SemaphoreType.DMA((n_sem,)), pltpu.SemaphoreType.DMA((n_sem,)),
                        pltpu.SemaphoreType.DMA((3,))],
        name="gather_weights")(*srcs)


def _exchange_grads(gw_in4, gw_out4, small):
    n_big = 2
    n_dev = 8

    def body(a_ref, b_ref, s_ref, ra_ref, rb_ref, rs_ref, send_sems, recv_sems, loc_sem):
        x, y, c = _position()
        dev = 4 * x + 2 * y + c
        pairs = ((a_ref, ra_ref), (b_ref, rb_ref))
        local = pltpu.make_async_copy(s_ref, rs_ref.at[dev], loc_sem)
        local.start()
        sends = []
        for j, (px, py) in enumerate(_other_chips(x, y)):
            for t, (src, dst) in enumerate(pairs):
                k = n_big * j + t
                sends.append(pltpu.make_async_remote_copy(
                    src_ref=src.at[2 * px + py], dst_ref=dst.at[j], send_sem=send_sems.at[k],
                    recv_sem=recv_sems.at[k], device_id=(px, py, c), device_id_type=MESH_ID))
        flips = [(dx, dy, dc) for dx in (0, 1) for dy in (0, 1) for dc in (0, 1)][1:]
        for f, (dx, dy, dc) in enumerate(flips):
            k = 3 * n_big + f
            peer = (x ^ dx, y ^ dy, c ^ dc)
            sends.append(pltpu.make_async_remote_copy(
                src_ref=s_ref, dst_ref=rs_ref.at[dev], send_sem=send_sems.at[k], recv_sem=recv_sems.at[k],
                device_id=peer, device_id_type=MESH_ID))
        for cp in sends:
            cp.start()
        for j, (px, py) in enumerate(_other_chips(x, y)):
            for t, (src, dst) in enumerate(pairs):
                k = n_big * j + t
                pltpu.make_async_remote_copy(
                    src_ref=src.at[0], dst_ref=dst.at[j], send_sem=send_sems.at[k], recv_sem=recv_sems.at[k],
                    device_id=(px, py, c), device_id_type=MESH_ID).wait_recv()
        for f, (dx, dy, dc) in enumerate(flips):
            k = 3 * n_big + f
            peer = (x ^ dx, y ^ dy, c ^ dc)
            pltpu.make_async_remote_copy(
                src_ref=s_ref, dst_ref=rs_ref.at[4 * peer[0] + 2 * peer[1] + peer[2]], send_sem=send_sems.at[k],
                recv_sem=recv_sems.at[k], device_id=peer, device_id_type=MESH_ID).wait_recv()
        for cp in sends:
            cp.wait_send()
        local.wait()

    n_sem = 3 * n_big + n_dev - 1
    return pl.pallas_call(
        body, in_specs=[ANY] * 3, out_specs=[ANY] * 3,
        out_shape=[_sds((3,) + gw_in4.shape[1:], gw_in4.dtype), _sds((3,) + gw_out4.shape[1:], gw_out4.dtype),
                   _sds((n_dev,) + small.shape, small.dtype)],
        scratch_shapes=[pltpu.SemaphoreType.DMA((n_sem,)), pltpu.SemaphoreType.DMA((n_sem,)),
                        pltpu.SemaphoreType.DMA],
        name="exchange_grads")(gw_in4, gw_out4, small)


def _plane_sum(own, recv, name):
    rows, cols = own.shape
    tr = 128

    def body(o_ref, r_ref, out_ref):
        acc = o_ref[...]
        for j in range(3):
            acc = acc + r_ref[j].astype(F32)
        out_ref[...] = acc

    return pl.pallas_call(
        body, grid=(pl.cdiv(rows, tr),),
        in_specs=[pl.BlockSpec((tr, cols), lambda i: (i, 0)), pl.BlockSpec((3, tr, cols), lambda i: (0, i, 0))],
        out_specs=pl.BlockSpec((tr, cols), lambda i: (i, 0)), out_shape=_sds((rows, cols)),
        compiler_params=_cp(("parallel",)), name=name)(own, recv)


def _sum_pair(a, b, name):
    n, rows, cols = a.shape
    tr = 128

    def body(a_ref, b_ref, s_ref, sb_ref):
        tot = a_ref[...] + b_ref[...]
        s_ref[...] = tot
        sb_ref[...] = tot.astype(BF16)

    blk = pl.BlockSpec((1, tr, cols), lambda k, i: (k, i, 0))
    return pl.pallas_call(
        body, grid=(n, pl.cdiv(rows, tr)), in_specs=[blk, blk], out_specs=[blk, blk],
        out_shape=[_sds(a.shape), _sds(a.shape, BF16)],
        compiler_params=_cp(("parallel", "parallel")), name=name)(a, b)


def _fill_halves(f_in, f_out):
    def body(a_ref, b_ref, fa_ref, fb_ref, send_sems, recv_sems, loc_sems):
        x, y, c = _position()
        sib = (x, y, 1 - c)
        pairs = ((a_ref, fa_ref), (b_ref, fb_ref))
        local = [pltpu.make_async_copy(src, _half(dst, c), loc_sems.at[t]) for t, (src, dst) in enumerate(pairs)]
        sends = [pltpu.make_async_remote_copy(src_ref=src, dst_ref=_half(dst, c), send_sem=send_sems.at[t],
                                              recv_sem=recv_sems.at[t], device_id=sib, device_id_type=MESH_ID)
                 for t, (src, dst) in enumerate(pairs)]
        for cp in local + sends:
            cp.start()
        for t, (src, dst) in enumerate(pairs):
            pltpu.make_async_remote_copy(src_ref=src, dst_ref=_half(dst, 1 - c), send_sem=send_sems.at[t],
                                         recv_sem=recv_sems.at[t], device_id=sib, device_id_type=MESH_ID).wait_recv()
        for cp in sends:
            cp.wait_send()
        for cp in local:
            cp.wait()

    return pl.pallas_call(
        body, in_specs=[ANY] * 2, out_specs=[ANY] * 2,
        out_shape=[_sds((f_in.shape[0], D_MODEL)), _sds((f_out.shape[0], D_MODEL))],
        scratch_shapes=[pltpu.SemaphoreType.DMA((2,)), pltpu.SemaphoreType.DMA((2,)), pltpu.SemaphoreType.DMA((2,))],
        name="fill_halves")(f_in, f_out)


def _swap_with_sibling(p_in, p_out):
    def body(a_ref, b_ref, ra_ref, rb_ref, send_sems, recv_sems):
        x, y, c = _position()
        sib = (x, y, 1 - c)
        cps = [pltpu.make_async_remote_copy(src_ref=src, dst_ref=dst, send_sem=send_sems.at[t],
                                            recv_sem=recv_sems.at[t], device_id=sib, device_id_type=MESH_ID)
               for t, (src, dst) in enumerate(((a_ref, ra_ref), (b_ref, rb_ref)))]
        for cp in cps:
            cp.start()
        for cp in cps:
            cp.wait_recv()
        for cp in cps:
            cp.wait_send()

    return pl.pallas_call(
        body, in_specs=[ANY] * 2, out_specs=[ANY] * 2,
        out_shape=[_sds(p_in.shape), _sds(p_out.shape)],
        scratch_shapes=[pltpu.SemaphoreType.DMA((2,)), pltpu.SemaphoreType.DMA((2,))],
        name="swap_with_sibling")(p_in, p_out)


SMALL_LAYOUT = (("norm_w", 1024), ("conv_w", 6144), ("a_log", 128), ("dt_bias", 128), ("dn_norm_w", 128),
                ("q_norm_w", 512), ("k_norm_w", 512), ("rel_bias", 256))
SMALL_TOTAL = sum(n for _, n in SMALL_LAYOUT)


def _small_offset(name):
    off = 0
    for n, size in SMALL_LAYOUT:
        if n == name:
            return off
        off += size
    raise KeyError(name)


def _pack_small(grads):
    parts = []
    for name, size in SMALL_LAYOUT:
        flat = grads[name].reshape(1, -1)
        parts.append(jnp.pad(flat, ((0, 0), (0, size - flat.shape[1]))))
    return jnp.concatenate(parts, axis=1)


def _sum_small(rows):
    n_dev = rows.shape[0]
    q_off = _small_offset("q_norm_w")
    k_off = _small_offset("k_norm_w")

    def body(r_ref, tot_ref, qk_ref):
        tot = r_ref[0:1, :]
        for d in range(1, n_dev):
            tot = tot + r_ref[d:d + 1, :]
        tot_ref[...] = tot
        for row, off in ((0, q_off), (1, k_off)):
            s4 = tot[:, off:off + 128] + tot[:, off + 128:off + 256] + tot[:, off + 256:off + 384] \
                + tot[:, off + 384:off + 512]
            qk_ref[row:row + 1, :] = s4 + pltpu.roll(s4, ATT_HD, 1)

    return pl.pallas_call(
        body, in_specs=[pl.BlockSpec(memory_space=pltpu.VMEM)],
        out_specs=[pl.BlockSpec(memory_space=pltpu.VMEM)] * 2,
        out_shape=[_sds((1, SMALL_TOTAL)), _sds((2, 128))],
        compiler_params=_cp(), name="sum_small")(rows)


def _adamw_math(w, g, m, v):
    m = ADAM_B1 * m + (1.0 - ADAM_B1) * g
    v = ADAM_B2 * v + (1.0 - ADAM_B2) * (g * g)
    m_hat = m / (1.0 - ADAM_B1 ** ADAM_STEP)
    v_hat = v / (1.0 - ADAM_B2 ** ADAM_STEP)
    delta = -ADAM_LR * (m_hat / (jnp.sqrt(v_hat) + ADAM_EPS) + ADAM_WD * w)
    return delta, m, v


def _adamw_big(g, w, m, v, name):
    rows, cols = w.shape
    tr = 128

    def body(g_ref, w_ref, m_ref, v_ref, go_ref, d_ref, nm_ref, nv_ref):
        g = g_ref[...]
        go_ref[...] = g
        d_ref[...], nm_ref[...], nv_ref[...] = _adamw_math(w_ref[...], g, m_ref[...], v_ref[...])

    blk = pl.BlockSpec((tr, cols), lambda i: (i, 0))
    return pl.pallas_call(
        body, grid=(pl.cdiv(rows, tr),), in_specs=[blk] * 4, out_specs=[blk] * 4,
        out_shape=[_sds((rows, cols))] * 4, compiler_params=_cp(("parallel",)), name=name)(g, w, m, v)


def _adamw_small(w, g, m, v, name):
    def body(w_ref, g_ref, m_ref, v_ref, d_ref, nm_ref, nv_ref):
        d_ref[...], nm_ref[...], nv_ref[...] = _adamw_math(w_ref[...], g_ref[...], m_ref[...], v_ref[...])

    vm = pl.BlockSpec(memory_space=pltpu.VMEM)
    return pl.pallas_call(body, in_specs=[vm] * 4, out_specs=[vm] * 3, out_shape=[_sds(w.shape)] * 3,
                          compiler_params=_cp(), name=name)(w, g, m, v)


WEIGHTS = ("norm_w", "w_in", "conv_w", "a_log", "dt_bias", "dn_norm_w", "q_norm_w", "k_norm_w", "rel_bias", "w_out")


def kernel(x, norm_w, w_in, conv_w, a_log, dt_bias, dn_norm_w, q_norm_w, k_norm_w, rel_bias, w_out, loss_target, m_norm_w, m_w_in, m_conv_w, m_a_log, m_dt_bias, m_dn_norm_w, m_q_norm_w, m_k_norm_w, m_rel_bias, m_w_out, v_norm_w, v_w_in, v_conv_w, v_a_log, v_dt_bias, v_dn_norm_w, v_q_norm_w, v_k_norm_w, v_rel_bias, v_w_out):
    xi, yi, ci = _position()
    chip = 2 * xi + yi
    w_loc = dict(norm_w=norm_w, w_in=w_in[0].T, conv_w=conv_w[0], a_log=a_log, dt_bias=dt_bias, dn_norm_w=dn_norm_w,
                 q_norm_w=q_norm_w, k_norm_w=k_norm_w, rel_bias=rel_bias, w_out=w_out[0])
    m_loc = dict(norm_w=m_norm_w, w_in=m_w_in[0].T, conv_w=m_conv_w[0], a_log=m_a_log, dt_bias=m_dt_bias,
                 dn_norm_w=m_dn_norm_w, q_norm_w=m_q_norm_w, k_norm_w=m_k_norm_w, rel_bias=m_rel_bias,
                 w_out=m_w_out[0])
    v_loc = dict(norm_w=v_norm_w, w_in=v_w_in[0].T, conv_w=v_conv_w[0], a_log=v_a_log, dt_bias=v_dt_bias,
                 dn_norm_w=v_dn_norm_w, q_norm_w=v_q_norm_w, k_norm_w=v_k_norm_w, rel_bias=v_rel_bias,
                 w_out=v_w_out[0])

    g_in, g_out, g_conv = _gather_weights(w_loc["w_in"].astype(BF16), w_loc["w_out"].astype(BF16), w_loc["conv_w"])
    wt_full = _pack_rows(g_in)
    w_out_full = g_out.reshape(D_MODEL, D_MODEL)
    conv_full = g_conv.transpose(1, 0, 2).reshape(CONV_W, 3 * D_DN)

    loss_local, grad_x, grads = _local_step(x[0], loss_target[0], norm_w, wt_full, conv_full, a_log, dt_bias,
                                            dn_norm_w, q_norm_w, k_norm_w, rel_bias, w_out_full)
    loss = lax.psum(loss_local, ("x", "y", "c"))

    g_out4 = grads["w_out"].reshape(N_CHIPS, D_MODEL // N_CHIPS, D_MODEL)

    def halves(full, packed):
        mine = lax.dynamic_slice_in_dim(full, ci * HALF, HALF, axis=full.ndim - 1)
        other = lax.dynamic_slice_in_dim(full, (1 - ci) * HALF, HALF, axis=full.ndim - 1)
        return (_unpack_rows(mine), _unpack_rows(other)) if packed else (mine, other)

    in_mine, in_other = halves(grads["w_in_t"], True)
    out_mine, out_other = halves(g_out4, False)
    sib_in, sib_out = _swap_with_sibling(in_other, out_other)
    h_in, hb_in = _sum_pair(in_mine, sib_in, "chip_sum_w_in")
    h_out, hb_out = _sum_pair(out_mine, sib_out, "chip_sum_w_out")
    small = _pack_small(grads)
    r_in, r_out, r_small = _exchange_grads(hb_in, hb_out, small)
    f_in = _plane_sum(lax.dynamic_index_in_dim(h_in, chip, 0, keepdims=False), r_in, "shard_sum_w_in")
    f_out = _plane_sum(lax.dynamic_index_in_dim(h_out, chip, 0, keepdims=False), r_out, "shard_sum_w_out")
    full_in, full_out = _fill_halves(f_in, f_out)
    tot_small, qk = _sum_small(r_small.reshape(8, SMALL_TOTAL))

    def small_grad(name, n):
        off = _small_offset(name)
        return tot_small[:, off:off + n]

    conv_all = small_grad("conv_w", CONV_W * 3 * D_DN).reshape(CONV_W, 3 * D_DN)
    g_small = dict(
        norm_w=small_grad("norm_w", D_MODEL),
        conv_w=lax.dynamic_slice_in_dim(conv_all, chip * (3 * D_DN // N_CHIPS), 3 * D_DN // N_CHIPS, axis=1),
        a_log=small_grad("a_log", DN_HEADS),
        dt_bias=small_grad("dt_bias", DN_HEADS),
        dn_norm_w=small_grad("dn_norm_w", DN_HD),
        q_norm_w=qk[0:1, 0:ATT_HD],
        k_norm_w=qk[1:2, 0:ATT_HD],
        rel_bias=small_grad("rel_bias", ATT_HEADS * N_BUCKETS).reshape(ATT_HEADS, N_BUCKETS),
    )

    out_g, out_d, out_m, out_v = {}, {}, {}, {}
    out_g["w_in"], out_d["w_in"], out_m["w_in"], out_v["w_in"] = _adamw_big(
        full_in, w_loc["w_in"], m_loc["w_in"], v_loc["w_in"], "adamw_w_in")
    out_g["w_out"], out_d["w_out"], out_m["w_out"], out_v["w_out"] = _adamw_big(
        full_out, w_loc["w_out"], m_loc["w_out"], v_loc["w_out"], "adamw_w_out")
    for name in g_small:
        out_g[name] = g_small[name]
        out_d[name], out_m[name], out_v[name] = _adamw_small(w_loc[name], g_small[name], m_loc[name], v_loc[name],
                                                             "adamw_" + name)
    for d in (out_g, out_d, out_m, out_v):
        d["w_in"] = d["w_in"].T
        for name in ("w_in", "conv_w", "w_out"):
            d[name] = d[name][None]
    return (loss, grad_x[None], *[out_g[n] for n in WEIGHTS], *[out_d[n] for n in WEIGHTS],
            *[out_m[n] for n in WEIGHTS], *[out_v[n] for n in WEIGHTS])
```

```python
import functools
import math

import numpy as np
import jax
import jax.numpy as jnp
from jax import lax
from jax.experimental import pallas as pl
from jax.experimental.pallas import tpu as pltpu

F32 = jnp.float32
BF16 = jnp.bfloat16
HI = lax.Precision.HIGHEST

D_MODEL = 1024
D_DN = 512
DN_HEADS = 4
DN_HD = 128
CONV_W = 4
CHUNK = 64
D_ATT = 512
ATT_HEADS = 8
ATT_HD = 64
PATTERNS = ((128, 1), (512, 4), (2048, 16))
N_BUCKETS = 32
MAX_DISTANCE = 2048
D_IN = 4 * D_DN + 2 * DN_HEADS + 4 * D_ATT
D_IN_PAD = 4224
EPS = 1e-6
BLK = 128
NEG = -1e30
N_CHIPS = 4

ADAM_LR = 0.001
ADAM_B1 = 0.9
ADAM_B2 = 0.999
ADAM_EPS = 1e-08
ADAM_WD = 0.01
ADAM_STEP = 10

VMEM_LIMIT = 56 * 1024 * 1024

COL_Z = 3
COL_ATT_Q = 4
COL_ATT_K = 5
COL_ATT_V = 6
COL_GATE = 7
COL_BA_128 = 32


def _cp(sem=None):
    if sem is None:
        return pltpu.CompilerParams(vmem_limit_bytes=VMEM_LIMIT)
    return pltpu.CompilerParams(dimension_semantics=sem, vmem_limit_bytes=VMEM_LIMIT)


def _sds(shape, dtype=F32):
    return jax.ShapeDtypeStruct(shape, dtype)


def _mm(a, b):
    return jnp.dot(a.astype(BF16), b.astype(BF16), preferred_element_type=F32)


def _mm_nt(a, b):
    return lax.dot_general(a.astype(BF16), b.astype(BF16), (((1,), (1,)), ((), ())),
                           preferred_element_type=F32)


def _mm_tn(a, b):
    return lax.dot_general(a.astype(BF16), b.astype(BF16), (((0,), (0,)), ((), ())),
                           preferred_element_type=F32)


def _mmx(a, b):
    return jnp.dot(a, b, precision=HI, preferred_element_type=F32)


def _mmx_nt(a, b):
    return lax.dot_general(a, b, (((1,), (1,)), ((), ())), precision=HI, preferred_element_type=F32)


def _mmx_tn(a, b):
    return lax.dot_general(a, b, (((0,), (0,)), ((), ())), precision=HI, preferred_element_type=F32)


def _dot(a, b):
    return jnp.dot(a, b, preferred_element_type=F32)


def _dot_nt(a, b):
    return lax.dot_general(a, b, (((1,), (1,)), ((), ())), preferred_element_type=F32)


def _dot_tn(a, b):
    return lax.dot_general(a, b, (((0,), (0,)), ((), ())), preferred_element_type=F32)


def _split(a):
    hi = a.astype(BF16)
    return hi, (a - hi.astype(F32)).astype(BF16)


def _mm3(a_s, b_s):
    return _dot(a_s[0], b_s[0]) + _dot(a_s[0], b_s[1]) + _dot(a_s[1], b_s[0])


def _mm3_tn(a_s, b_s):
    return _dot_tn(a_s[0], b_s[0]) + _dot_tn(a_s[0], b_s[1]) + _dot_tn(a_s[1], b_s[0])


def _interleave(gens):
    live = list(gens)
    while live:
        nxt = []
        for g in live:
            try:
                next(g)
                nxt.append(g)
            except StopIteration:
                pass
        live = nxt


def _segsum(x, bd):
    hi = x.astype(BF16)
    r1 = x - hi.astype(F32)
    mid = r1.astype(BF16)
    lo = (r1 - mid.astype(F32)).astype(BF16)
    return (jnp.dot(hi, bd, preferred_element_type=F32) + jnp.dot(mid, bd, preferred_element_type=F32)
            + jnp.dot(lo, bd, preferred_element_type=F32))


def _sigmoid(x):
    return 1.0 / (1.0 + jnp.exp(-x))


def _silu_grad(x, s):
    return s * (1.0 + x * (1.0 - s))


def _block_ones(n, seg):
    i = np.arange(n)
    return jnp.asarray((i[:, None] // seg == i[None, :] // seg).astype(np.float32), dtype=BF16)


def _tri_incl():
    i = np.arange(CHUNK)
    return jnp.asarray((i[:, None] >= i[None, :]).astype(np.float32))


def _t5_bucket(dist):
    max_exact = N_BUCKETS // 2
    d = np.maximum(dist, 1).astype(np.float64)
    large = max_exact + (np.log(d / max_exact) / math.log(MAX_DISTANCE / max_exact)
                         * (N_BUCKETS - max_exact)).astype(np.int32)
    large = np.minimum(large, N_BUCKETS - 1)
    return np.where(dist < max_exact, dist, large).astype(np.int32)


def _bucket_tables():
    qi = np.arange(BLK)[:, None]
    kj = np.arange(2 * BLK)[None, :]
    step = qi - kj + BLK
    return jnp.asarray(np.stack([_t5_bucket(np.clip(step, 0, None) * r) for _, r in PATTERNS]))


def _in_proj(x, norm_w, wt_bf):
    s = x.shape[0]
    tm = 256

    def body(x_ref, nw_ref, w_ref, o_ref):
        xv = x_ref[...]
        rstd = lax.rsqrt(jnp.mean(xv * xv, axis=-1, keepdims=True) + EPS)
        h = (xv * rstd * nw_ref[...]).astype(BF16)
        o_ref[...] = _dot_nt(h, w_ref[...])

    return pl.pallas_call(
        body, grid=(s // tm,),
        in_specs=[pl.BlockSpec((tm, D_MODEL), lambda i: (i, 0)),
                  pl.BlockSpec((1, D_MODEL), lambda i: (0, 0)),
                  pl.BlockSpec((D_IN_PAD, D_MODEL), lambda i: (0, 0))],
        out_specs=pl.BlockSpec((tm, D_IN_PAD), lambda i: (i, 0)),
        out_shape=_sds((s, D_IN_PAD)), compiler_params=_cp(("parallel",)), name="in_proj")(x, norm_w, wt_bf)


def _conv_group(cur, halo, w_ref, c):
    rows = cur.shape[0]
    lanes = slice(128 * c, 128 * c + 128)
    xcat = jnp.concatenate([halo, cur], axis=0)
    y = cur * w_ref[CONV_W - 1:CONV_W, lanes]
    for k in range(1, CONV_W):
        sh = pltpu.roll(xcat, k, 0)[8:8 + rows]
        y = y + sh * w_ref[CONV_W - 1 - k:CONV_W - k, lanes]
    return y


def _beta_g(ba, alog_l, dtb_l):
    lane = lax.broadcasted_iota(jnp.int32, ba.shape, 1)
    sig_b = _sigmoid(ba)
    t = ba + dtb_l
    softplus = jnp.maximum(t, 0.0) + jnp.log(1.0 + jnp.exp(-jnp.abs(t)))
    nega = -jnp.exp(alog_l)
    g = nega * softplus
    out = jnp.where(lane < DN_HEADS, sig_b, jnp.where(lane < 2 * DN_HEADS, g, 0.0))
    return out, lane, sig_b, t, nega, g


def _dn_pre(proj, conv_w8, alog_l, dtb_l):
    s = proj.shape[0]
    tr = 512
    nh = tr // 8

    def body(u_ref, halo_ref, ba_ref, w_ref, al_ref, dt_ref, q_ref, k_ref, v_ref, bg_ref):
        i = pl.program_id(0)
        keep = (i > 0).astype(F32)
        for c in range(12):
            lanes = slice(128 * c, 128 * c + 128)
            y = _conv_group(u_ref[:, lanes], halo_ref[:, lanes] * keep, w_ref, c)
            sv = y * _sigmoid(y)
            if c < 8:
                rs = lax.rsqrt(jnp.sum(sv * sv, axis=1, keepdims=True) + EPS)
                n = sv * rs
                if c < 4:
                    q_ref[:, lanes] = n * (DN_HD ** -0.5)
                else:
                    k_ref[:, slice(128 * (c - 4), 128 * (c - 3))] = n
            else:
                v_ref[:, slice(128 * (c - 8), 128 * (c - 7))] = sv
        bg_ref[...] = _beta_g(ba_ref[...], al_ref[...], dt_ref[...])[0]

    return pl.pallas_call(
        body, grid=(s // tr,),
        in_specs=[pl.BlockSpec((tr, 1536), lambda i: (i, 0)),
                  pl.BlockSpec((8, 1536), lambda i: (jnp.maximum(i * nh - 1, 0), 0)),
                  pl.BlockSpec((tr, 128), lambda i: (i, COL_BA_128)),
                  pl.BlockSpec((8, 1536), lambda i: (0, 0)),
                  pl.BlockSpec((1, 128), lambda i: (0, 0)),
                  pl.BlockSpec((1, 128), lambda i: (0, 0))],
        out_specs=[pl.BlockSpec((tr, 512), lambda i: (i, 0))] * 3 + [pl.BlockSpec((tr, 128), lambda i: (i, 0))],
        out_shape=[_sds((s, 512))] * 3 + [_sds((s, 128))],
        compiler_params=_cp(("parallel",)), name="dn_pre")(proj, proj, proj, conv_w8, alog_l, dtb_l)


CPS = 2


def _chunk_cumsum(bg, bgt, tri):
    return _mmx(tri, bg), _mmx_nt(bgt, tri)


def _chunk_common(bg, gc_col, gc_row, h):
    gcc = gc_col[:, DN_HEADS + h:DN_HEADS + h + 1]
    gcr = gc_row[DN_HEADS + h:DN_HEADS + h + 1, :]
    beta = bg[:, h:h + 1]
    ii = lax.broadcasted_iota(jnp.int32, (CHUNK, CHUNK), 0)
    jj = lax.broadcasted_iota(jnp.int32, (CHUNK, CHUNK), 1)
    incl = ii >= jj
    strict = ii > jj
    decay = jnp.exp(jnp.where(incl, gcc - gcr, NEG))
    gl = gcc[CHUNK - 1:CHUNK, :]
    return gcc, beta, incl, strict, decay, gl


def _dn_prep(qn, kn, v, bg, bgt, tri):
    s = qn.shape[0]
    nc = s // CHUNK

    def body(q_ref, k_ref, v_ref, bg_ref, bgt_ref, tri_ref,
             u_ref, w_ref, qd_ref, kt_ref, attn_ref, t_ref, gl_ref):
        tri_v = tri_ref[...]
        ii = lax.broadcasted_iota(jnp.int32, (CHUNK, CHUNK), 0)
        jj = lax.broadcasted_iota(jnp.int32, (CHUNK, CHUNK), 1)
        eye = (ii == jj).astype(F32)

        def chain(cc, h, bgv, gc_col, gc_row):
            rows = slice(CHUNK * cc, CHUNK * cc + CHUNK)
            lanes = slice(128 * h, 128 * h + 128)
            gcc, beta, incl, strict, decay, gl = _chunk_common(bgv, gc_col, gc_row, h)
            q = q_ref[rows, lanes]
            k = k_ref[rows, lanes]
            vv = v_ref[rows, lanes]
            kb = k * beta
            egc = jnp.exp(gcc)
            a_mat = jnp.where(strict, _mm_nt(kb, k) * decay, 0.0)
            attn_ref[cc, h] = jnp.where(incl, _mm_nt(q, k) * decay, 0.0)
            qd_ref[rows, lanes] = q * egc
            kt_ref[rows, lanes] = k * jnp.exp(gl - gcc)
            gl_ref[cc, h] = jnp.broadcast_to(jnp.exp(gl), (1, 128))
            yield
            p = -a_mat
            t = eye + p
            for _ in range(5):
                ps = _split(p)
                p = _mm3(ps, ps)
                yield
                t = t + _mm3(_split(t), _split(p))
                yield
            t_ref[cc, h] = t
            ts = _split(t)
            u_ref[rows, lanes] = _mm3(ts, _split(vv * beta))
            w_ref[rows, lanes] = _mm3(ts, _split(kb * egc))

        gens = []
        for cc in range(CPS):
            bgv = bg_ref[CHUNK * cc:CHUNK * cc + CHUNK, :]
            gc_col, gc_row = _chunk_cumsum(bgv, bgt_ref[cc], tri_v)
            gens += [chain(cc, h, bgv, gc_col, gc_row) for h in range(DN_HEADS)]
        _interleave(gens)

    rows_step = CPS * CHUNK
    big = pl.BlockSpec((rows_step, 512), lambda n: (n, 0))
    sq = pl.BlockSpec((CPS, DN_HEADS, CHUNK, CHUNK), lambda n: (n, 0, 0, 0))
    return pl.pallas_call(
        body, grid=(nc // CPS,),
        in_specs=[big, big, big, pl.BlockSpec((rows_step, 128), lambda n: (n, 0)),
                  pl.BlockSpec((CPS, 8, CHUNK), lambda n: (n, 0, 0)),
                  pl.BlockSpec((CHUNK, CHUNK), lambda n: (0, 0))],
        out_specs=[big, big, big, big, sq, sq, pl.BlockSpec((CPS, DN_HEADS, 1, 128), lambda n: (n, 0, 0, 0))],
        out_shape=[_sds((s, 512))] * 4 + [_sds((nc, DN_HEADS, CHUNK, CHUNK))] * 2 + [_sds((nc, DN_HEADS, 1, 128))],
        compiler_params=_cp(("parallel",)), name="dn_prep")(qn, kn, v, bg, bgt, tri)


def _dn_scan(u, w, qd, kt, attn, gl):
    s = u.shape[0]
    nc = s // CHUNK

    def body(u_ref, w_ref, qd_ref, kt_ref, attn_ref, gl_ref, o_ref, vn_ref, sp_ref, st_ref):
        n = pl.program_id(0)

        @pl.when(n == 0)
        def _():
            st_ref[...] = jnp.zeros_like(st_ref)

        def chain(cc, h):
            rows = slice(CHUNK * cc, CHUNK * cc + CHUNK)
            lanes = slice(128 * h, 128 * h + 128)
            st = st_ref[h]
            sp_ref[cc, h] = st
            stb = st.astype(BF16)
            ws = _dot(w_ref[rows, lanes].astype(BF16), stb)
            qs = _dot(qd_ref[rows, lanes].astype(BF16), stb)
            yield
            vn = u_ref[rows, lanes] - ws
            vn_ref[rows, lanes] = vn
            vnb = vn.astype(BF16)
            o_ref[rows, lanes] = qs + _dot(attn_ref[cc, h].astype(BF16), vnb)
            st_ref[h] = st * gl_ref[cc, h] + _dot_tn(kt_ref[rows, lanes].astype(BF16), vnb)

        for cc in range(CPS):
            _interleave([chain(cc, h) for h in range(DN_HEADS)])

    big = pl.BlockSpec((CPS * CHUNK, 512), lambda n: (n, 0))
    return pl.pallas_call(
        body, grid=(nc // CPS,),
        in_specs=[big, big, big, big,
                  pl.BlockSpec((CPS, DN_HEADS, CHUNK, CHUNK), lambda n: (n, 0, 0, 0)),
                  pl.BlockSpec((CPS, DN_HEADS, 1, 128), lambda n: (n, 0, 0, 0))],
        out_specs=[big, big, pl.BlockSpec((CPS, DN_HEADS, DN_HD, DN_HD), lambda n: (n, 0, 0, 0))],
        out_shape=[_sds((s, 512)), _sds((s, 512)), _sds((nc, DN_HEADS, DN_HD, DN_HD))],
        scratch_shapes=[pltpu.VMEM((DN_HEADS, DN_HD, DN_HD), F32)],
        compiler_params=_cp(("arbitrary",)), name="dn_scan")(u, w, qd, kt, attn, gl)


R4 = PATTERNS[1][1]
R16 = PATTERNS[2][1]
TM = 256


def _pattern_spec(r):
    return pl.BlockSpec((r, TM // r, 512), lambda i: (0, i, 0))


def _pattern_shape(s, r):
    return _sds((r, s // r, 512))


SLABS = pltpu.VMEM((4, TM, 128), F32)


def _to_patterns(val, dsts, scr):
    for c in range(4):
        lanes = slice(128 * c, 128 * c + 128)
        scr[c] = val[:, lanes]
        for dst_ref, r in dsts:
            for a in range(r):
                dst_ref[a, :, lanes] = scr[c, pl.ds(a, TM // r, stride=r), :]


def _from_pattern(src_ref, r, scr):
    for c in range(4):
        for a in range(r):
            scr[c, pl.ds(a, TM // r, stride=r), :] = src_ref[a, :, 128 * c:128 * c + 128]
    return jnp.concatenate([scr[c] for c in range(4)], axis=1)


def _att_pre(proj, qw_t, kw_t, bd64):
    s = proj.shape[0]

    def body(q_ref, k_ref, v_ref, qw_ref, kw_ref, bd_ref,
             q1_ref, k1_ref, q4_ref, k4_ref, v4_ref, q16_ref, k16_ref, v16_ref, scr):
        bd = bd_ref[...]
        q = q_ref[...]
        k = k_ref[...]
        qn = q * lax.rsqrt(_segsum(q * q, bd) * (1.0 / ATT_HD) + EPS) * qw_ref[...] * (ATT_HD ** -0.5)
        kn = k * lax.rsqrt(_segsum(k * k, bd) * (1.0 / ATT_HD) + EPS) * kw_ref[...]
        q1_ref[...] = qn
        k1_ref[...] = kn
        _to_patterns(qn, ((q4_ref, R4), (q16_ref, R16)), scr)
        _to_patterns(kn, ((k4_ref, R4), (k16_ref, R16)), scr)
        _to_patterns(v_ref[...], ((v4_ref, R4), (v16_ref, R16)), scr)

    row = pl.BlockSpec((1, 512), lambda i: (0, 0))
    tok = pl.BlockSpec((TM, 512), lambda i: (i, 0))
    return pl.pallas_call(
        body, grid=(s // TM,),
        in_specs=[pl.BlockSpec((TM, 512), lambda i: (i, COL_ATT_Q)),
                  pl.BlockSpec((TM, 512), lambda i: (i, COL_ATT_K)),
                  pl.BlockSpec((TM, 512), lambda i: (i, COL_ATT_V)),
                  row, row, pl.BlockSpec((512, 512), lambda i: (0, 0))],
        out_specs=[tok, tok] + [_pattern_spec(R4)] * 3 + [_pattern_spec(R16)] * 3,
        out_shape=[_sds((s, 512))] * 2 + [_pattern_shape(s, R4)] * 3 + [_pattern_shape(s, R16)] * 3,
        scratch_shapes=[SLABS],
        compiler_params=_cp(("parallel",)), name="att_pre")(proj, proj, proj, qw_t, kw_t, bd64)


def _bias_fwd(rel_bias, buckets):
    def body(rb_ref, bk_ref, o_ref):
        for p in range(len(PATTERNS)):
            bk = bk_ref[p]
            for h in range(ATT_HEADS):
                acc = jnp.zeros((BLK, 2 * BLK), F32)
                for b in range(N_BUCKETS):
                    acc = jnp.where(bk == b, rb_ref[h, b], acc)
                o_ref[p, h] = acc

    return pl.pallas_call(
        body,
        in_specs=[pl.BlockSpec(memory_space=pltpu.SMEM), pl.BlockSpec(memory_space=pltpu.VMEM)],
        out_specs=pl.BlockSpec(memory_space=pltpu.VMEM),
        out_shape=_sds((len(PATTERNS), ATT_HEADS, BLK, 2 * BLK)),
        compiler_params=_cp(), name="bias_fwd")(rel_bias, buckets)


def _bias_bwd(ds_accs, buckets):
    def body(ds0_ref, ds1_ref, ds2_ref, bk_ref, o_ref):
        for h in range(ATT_HEADS):
            for b in range(N_BUCKETS):
                tot = jnp.zeros((), F32)
                for p, ds_ref in enumerate((ds0_ref, ds1_ref, ds2_ref)):
                    tot = tot + jnp.sum(jnp.where(bk_ref[p] == b, ds_ref[h], 0.0))
                o_ref[h, b] = tot

    return pl.pallas_call(
        body,
        in_specs=[pl.BlockSpec(memory_space=pltpu.VMEM)] * 4,
        out_specs=pl.BlockSpec(memory_space=pltpu.SMEM),
        out_shape=_sds((ATT_HEADS, N_BUCKETS)),
        compiler_params=_cp(), name="bias_bwd")(*ds_accs, buckets)


def _att_masks(has_prev):
    qi = lax.broadcasted_iota(jnp.int32, (BLK, BLK), 0)
    kj = lax.broadcasted_iota(jnp.int32, (BLK, BLK), 1)
    lane = lax.broadcasted_iota(jnp.int32, (BLK, 2 * ATT_HD), 1)
    return jnp.logical_and(kj >= qi, has_prev), kj <= qi, lane < ATT_HD


def _head_lanes(h):
    half = h % 2
    return slice(ATT_HD * h, ATT_HD * h + ATT_HD), slice(ATT_HD * half, ATT_HD * half + ATT_HD)


def _att_scores(qm, kp2, kc2, bias_h, mask_prev, mask_cur):
    s_prev = jnp.where(mask_prev, _dot_nt(qm, kp2) + bias_h[:, :BLK], NEG)
    s_cur = jnp.where(mask_cur, _dot_nt(qm, kc2) + bias_h[:, BLK:], NEG)
    return s_prev, s_cur


def _att_fwd(q, k, v, v_col, bias, p_idx, r, name):
    s = q.shape[0]
    nblk = s // BLK
    nseq = nblk // r

    def body(q_ref, kp_ref, kc_ref, vp_ref, vc_ref, b_ref, o_ref, lse_ref):
        j = pl.program_id(0)
        mask_prev, mask_cur, lo_half = _att_masks((j % nseq) != 0)

        def head(h, q2, kp2, kc2, vp2, vc2):
            out_l, pair_l = _head_lanes(h)
            sel = lo_half if h % 2 == 0 else jnp.logical_not(lo_half)
            qm = jnp.where(sel, q2, 0.0).astype(BF16)
            s_prev, s_cur = _att_scores(qm, kp2, kc2, b_ref[0, h], mask_prev, mask_cur)
            yield
            m = jnp.maximum(jnp.max(s_prev, axis=1, keepdims=True), jnp.max(s_cur, axis=1, keepdims=True))
            p_prev = jnp.exp(s_prev - m)
            p_cur = jnp.exp(s_cur - m)
            l = jnp.sum(p_prev, axis=1, keepdims=True) + jnp.sum(p_cur, axis=1, keepdims=True)
            yield
            o2 = _dot(p_prev.astype(BF16), vp2) + _dot(p_cur.astype(BF16), vc2)
            o_ref[:, out_l] = (o2 * (1.0 / l))[:, pair_l]
            lse_ref[:, out_l] = jnp.broadcast_to(m + jnp.log(l), (BLK, ATT_HD))

        gens = []
        for pp in range(ATT_HEADS // 2):
            lanes = slice(128 * pp, 128 * pp + 128)
            slabs = (q_ref[:, lanes], kp_ref[:, lanes].astype(BF16), kc_ref[:, lanes].astype(BF16),
                     vp_ref[:, lanes].astype(BF16), vc_ref[:, lanes].astype(BF16))
            gens += [head(2 * pp, *slabs), head(2 * pp + 1, *slabs)]
        _interleave(gens)

    cur = pl.BlockSpec((BLK, 512), lambda j: (j, 0))
    prev = pl.BlockSpec((BLK, 512), lambda j: (jnp.maximum(j - 1, 0), 0))
    vcur = pl.BlockSpec((BLK, 512), lambda j: (j, v_col))
    vprev = pl.BlockSpec((BLK, 512), lambda j: (jnp.maximum(j - 1, 0), v_col))
    return pl.pallas_call(
        body, grid=(nblk,),
        in_specs=[cur, prev, cur, vprev, vcur,
                  pl.BlockSpec((1, ATT_HEADS, BLK, 2 * BLK), lambda j: (p_idx, 0, 0, 0))],
        out_specs=[cur, cur],
        out_shape=[_sds((s, 512))] * 2,
        compiler_params=_cp(("parallel",)), name=name)(q, k, k, v, v, bias)


def _post_fwd(o_dn, proj, o_pats, lse_pats, dnw_t, bd128):
    s = o_dn.shape[0]

    def body(o_ref, z_ref, gate_ref, o1_ref, o4_ref, o16_ref, s1_ref, s4_ref, s16_ref, wn_ref, bd_ref,
             mixed_ref, oatt_ref, l1_ref, l4_ref, l16_ref, scr_a, scr_b, scr_c, scr_d):
        o = o_ref[...]
        z = z_ref[...]
        rstd = lax.rsqrt(_segsum(o * o, bd_ref[...]) * (1.0 / DN_HD) + EPS)
        y_dn = o * rstd * wn_ref[...] * (z * _sigmoid(z))
        mixed_ref[:, 0:512] = y_dn.astype(BF16)
        lses = (s1_ref[...], _from_pattern(s4_ref, R4, scr_a), _from_pattern(s16_ref, R16, scr_b))
        outs = (o1_ref[...], _from_pattern(o4_ref, R4, scr_c), _from_pattern(o16_ref, R16, scr_d))
        m = jnp.maximum(jnp.maximum(lses[0], lses[1]), lses[2])
        tot = jnp.exp(lses[0] - m) + jnp.exp(lses[1] - m) + jnp.exp(lses[2] - m)
        big_l = m + jnp.log(tot)
        acc = jnp.zeros_like(m)
        for lse_p, o_p in zip(lses, outs):
            acc = acc + jnp.exp(lse_p - big_l) * o_p
        gate = gate_ref[...]
        oatt_ref[...] = acc
        l1_ref[...] = big_l
        mixed_ref[:, 512:1024] = (acc * (gate * _sigmoid(gate))).astype(BF16)
        _to_patterns(big_l, ((l4_ref, R4), (l16_ref, R16)), scr_a)

    blk = pl.BlockSpec((TM, 512), lambda i: (i, 0))
    p4, p16 = _pattern_spec(R4), _pattern_spec(R16)
    return pl.pallas_call(
        body, grid=(s // TM,),
        in_specs=[blk, pl.BlockSpec((TM, 512), lambda i: (i, COL_Z)),
                  pl.BlockSpec((TM, 512), lambda i: (i, COL_GATE)), blk, p4, p16, blk, p4, p16,
                  pl.BlockSpec((1, 512), lambda i: (0, 0)), pl.BlockSpec((512, 512), lambda i: (0, 0))],
        out_specs=[pl.BlockSpec((TM, D_MODEL), lambda i: (i, 0)), blk, blk, p4, p16],
        out_shape=[_sds((s, D_MODEL), BF16), _sds((s, 512)), _sds((s, 512)), _pattern_shape(s, R4),
                   _pattern_shape(s, R16)],
        scratch_shapes=[SLABS] * 4,
        compiler_params=_cp(("parallel",)), name="post_fwd")(o_dn, proj, proj, *o_pats, *lse_pats, dnw_t, bd128)


def _out_fwd(x, mixed, w_out_bf, tgt):
    s = x.shape[0]
    tm = 256

    def body(x_ref, m_ref, w_ref, t_ref, dy_ref, loss_ref):
        i = pl.program_id(0)

        @pl.when(i == 0)
        def _():
            loss_ref[...] = jnp.zeros_like(loss_ref)

        y = x_ref[...] + jnp.dot(m_ref[...], w_ref[...], preferred_element_type=F32)
        err = y - t_ref[...]
        dy_ref[...] = err * (1.0 / D_MODEL)
        part = 0.5 * jnp.sum(jnp.mean(err * err, axis=-1, keepdims=True), axis=0, keepdims=True)
        loss_ref[...] = loss_ref[...] + part

    blk = pl.BlockSpec((tm, D_MODEL), lambda i: (i, 0))
    return pl.pallas_call(
        body, grid=(s // tm,),
        in_specs=[blk, blk, pl.BlockSpec((D_MODEL, D_MODEL), lambda i: (0, 0)), blk],
        out_specs=[blk, pl.BlockSpec((8, 128), lambda i: (0, 0))],
        out_shape=[_sds((s, D_MODEL)), _sds((8, 128))],
        compiler_params=_cp(("arbitrary",)), name="out_fwd")(x, mixed, w_out_bf, tgt)


def _out_bwd(dy, mixed, w_out_bf):
    s = dy.shape[0]
    tm = 256

    def body(dy_ref, m_ref, w_ref, dm_ref, dw_ref):
        i = pl.program_id(0)

        @pl.when(i == 0)
        def _():
            dw_ref[...] = jnp.zeros_like(dw_ref)

        dyb = dy_ref[...].astype(BF16)
        dm_ref[...] = lax.dot_general(dyb, w_ref[...], (((1,), (1,)), ((), ())), preferred_element_type=F32)
        dw_ref[...] = dw_ref[...] + lax.dot_general(m_ref[...], dyb, (((0,), (0,)), ((), ())),
                                                    preferred_element_type=F32)

    blk = pl.BlockSpec((tm, D_MODEL), lambda i: (i, 0))
    full = pl.BlockSpec((D_MODEL, D_MODEL), lambda i: (0, 0))
    return pl.pallas_call(
        body, grid=(s // tm,), in_specs=[blk, blk, full], out_specs=[blk, full],
        out_shape=[_sds((s, D_MODEL)), _sds((D_MODEL, D_MODEL))],
        compiler_params=_cp(("arbitrary",)), name="out_bwd")(dy, mixed, w_out_bf)


def _post_bwd(dmixed, o_dn, proj, o_att, dnw_t, bd128, bd64):
    s = o_dn.shape[0]
    tm = 256

    def body(ddn_ref, datt_ref, o_ref, z_ref, gate_ref, oatt_ref, wn_ref, bd128_ref, bd64_ref,
             do_ref, dz_ref, dgate_ref, doatt_ref, do4_ref, do16_ref, delta_ref, dl4_ref, dl16_ref, dnw_ref, scr):
        i = pl.program_id(0)

        @pl.when(i == 0)
        def _():
            dnw_ref[...] = jnp.zeros_like(dnw_ref)

        bd128v = bd128_ref[...]
        o = o_ref[...]
        z = z_ref[...]
        wn = wn_ref[...]
        dy = ddn_ref[...]
        rstd = lax.rsqrt(_segsum(o * o, bd128v) * (1.0 / DN_HD) + EPS)
        nrm = o * rstd
        sz = _sigmoid(z)
        dz_ref[...] = dy * nrm * wn * _silu_grad(z, sz)
        dn = dy * z * sz
        gw = dn * wn
        do_ref[...] = rstd * (gw - nrm * (_segsum(gw * nrm, bd128v) * (1.0 / DN_HD)))
        colsum = jnp.sum(dn * nrm, axis=0, keepdims=True)
        fold = colsum[:, 0:128] + colsum[:, 128:256] + colsum[:, 256:384] + colsum[:, 384:512]
        dnw_ref[...] = dnw_ref[...] + fold
        dya = datt_ref[...]
        gate = gate_ref[...]
        oatt = oatt_ref[...]
        sg = _sigmoid(gate)
        dgate_ref[...] = dya * oatt * _silu_grad(gate, sg)
        doa = dya * gate * sg
        doatt_ref[...] = doa
        delta = _segsum(doa * oatt, bd64_ref[...])
        delta_ref[...] = delta
        _to_patterns(doa, ((do4_ref, R4), (do16_ref, R16)), scr)
        _to_patterns(delta, ((dl4_ref, R4), (dl16_ref, R16)), scr)

    blk = pl.BlockSpec((tm, 512), lambda i: (i, 0))
    cst = pl.BlockSpec((512, 512), lambda i: (0, 0))
    p4, p16 = _pattern_spec(R4), _pattern_spec(R16)
    return pl.pallas_call(
        body, grid=(s // tm,),
        in_specs=[blk, pl.BlockSpec((tm, 512), lambda i: (i, 1)), blk,
                  pl.BlockSpec((tm, 512), lambda i: (i, COL_Z)), pl.BlockSpec((tm, 512), lambda i: (i, COL_GATE)),
                  blk, pl.BlockSpec((1, 512), lambda i: (0, 0)), cst, cst],
        out_specs=[blk, blk, blk, blk, p4, p16, blk, p4, p16, pl.BlockSpec((8, 128), lambda i: (0, 0))],
        out_shape=[_sds((s, 512))] * 4 + [_pattern_shape(s, R4), _pattern_shape(s, R16), _sds((s, 512)),
                                          _pattern_shape(s, R4), _pattern_shape(s, R16), _sds((8, 128))],
        scratch_shapes=[SLABS],
        compiler_params=_cp(("arbitrary",)), name="post_bwd")(dmixed, dmixed, o_dn, proj, proj, o_att, dnw_t,
                                                               bd128, bd64)


def _att_bwd(q, k, v, v_col, do, big_l, delta, bias, p_idx, r, name):
    s = q.shape[0]
    nblk = s // BLK
    nseq = nblk // r

    def body(q_ref, kp_ref, kc_ref, vp_ref, vc_ref, do_ref, l_ref, dl_ref, b_ref,
             dq_ref, dk_ref, dv_ref, ds_ref, dkc_ref, dvc_ref):
        j = pl.program_id(0)
        mask_prev, mask_cur, lo_half = _att_masks((j % nseq) != 0)

        @pl.when(j == 0)
        def _():
            dkc_ref[...] = jnp.zeros_like(dkc_ref)
            dvc_ref[...] = jnp.zeros_like(dvc_ref)
            ds_ref[...] = jnp.zeros_like(ds_ref)

        @pl.when(j < nblk)
        def _():
            def head(h, q2, do2, kp2, kc2, vp2, vc2):
                out_l, pair_l = _head_lanes(h)
                sel = lo_half if h % 2 == 0 else jnp.logical_not(lo_half)
                qm = jnp.where(sel, q2, 0.0).astype(BF16)
                dom = jnp.where(sel, do2, 0.0).astype(BF16)
                s_prev, s_cur = _att_scores(qm, kp2, kc2, b_ref[0, h], mask_prev, mask_cur)
                dp_prev = _dot_nt(dom, vp2)
                dp_cur = _dot_nt(dom, vc2)
                yield
                lh = l_ref[:, ATT_HD * h:ATT_HD * h + 1]
                dh = dl_ref[:, ATT_HD * h:ATT_HD * h + 1]
                p_prev = jnp.exp(s_prev - lh)
                p_cur = jnp.exp(s_cur - lh)
                ds_prev = p_prev * (dp_prev - dh)
                ds_cur = p_cur * (dp_cur - dh)
                ds_ref[h, :, 0:BLK] = ds_ref[h, :, 0:BLK] + ds_prev
                ds_ref[h, :, BLK:2 * BLK] = ds_ref[h, :, BLK:2 * BLK] + ds_cur
                dsb_prev, dsb_cur = ds_prev.astype(BF16), ds_cur.astype(BF16)
                pb_prev, pb_cur = p_prev.astype(BF16), p_cur.astype(BF16)
                yield
                qb, dob = q2.astype(BF16), do2.astype(BF16)
                dq_ref[:, out_l] = (_dot(dsb_prev, kp2) + _dot(dsb_cur, kc2))[:, pair_l]
                dk_ref[:, out_l] = dkc_ref[:, out_l] + _dot_tn(dsb_prev, qb)[:, pair_l]
                dv_ref[:, out_l] = dvc_ref[:, out_l] + _dot_tn(pb_prev, dob)[:, pair_l]
                yield
                dkc_ref[:, out_l] = _dot_tn(dsb_cur, qb)[:, pair_l]
                dvc_ref[:, out_l] = _dot_tn(pb_cur, dob)[:, pair_l]

            gens = []
            for pp in range(ATT_HEADS // 2):
                lanes = slice(128 * pp, 128 * pp + 128)
                slabs = (q_ref[:, lanes], do_ref[:, lanes], kp_ref[:, lanes].astype(BF16),
                         kc_ref[:, lanes].astype(BF16), vp_ref[:, lanes].astype(BF16),
                         vc_ref[:, lanes].astype(BF16))
                gens += [head(2 * pp, *slabs), head(2 * pp + 1, *slabs)]
            _interleave(gens)

        @pl.when(j == nblk)
        def _():
            dk_ref[...] = dkc_ref[...]
            dv_ref[...] = dvc_ref[...]

    last = nblk - 1
    cur = pl.BlockSpec((BLK, 512), lambda j: (jnp.minimum(j, last), 0))
    prev = pl.BlockSpec((BLK, 512), lambda j: (jnp.clip(j - 1, 0, last), 0))
    vcur = pl.BlockSpec((BLK, 512), lambda j: (jnp.minimum(j, last), v_col))
    vprev = pl.BlockSpec((BLK, 512), lambda j: (jnp.clip(j - 1, 0, last), v_col))
    return pl.pallas_call(
        body, grid=(nblk + 1,),
        in_specs=[cur, prev, cur, vprev, vcur, cur, cur, cur,
                  pl.BlockSpec((1, ATT_HEADS, BLK, 2 * BLK), lambda j: (p_idx, 0, 0, 0))],
        out_specs=[cur, prev, prev, pl.BlockSpec((ATT_HEADS, BLK, 2 * BLK), lambda j: (0, 0, 0))],
        out_shape=[_sds((s, 512))] * 3 + [_sds((ATT_HEADS, BLK, 2 * BLK))],
        scratch_shapes=[pltpu.VMEM((BLK, 512), F32), pltpu.VMEM((BLK, 512), F32)],
        compiler_params=_cp(("arbitrary",)), name=name)(q, k, k, v, v, do, big_l, delta, bias)


def _att_pre_bwd(dq_pats, dk_pats, dv_pats, proj, qw_t, kw_t, bd64):
    s = proj.shape[0]
    tm = TM

    def body(dq1_ref, dq4_ref, dq16_ref, dk1_ref, dk4_ref, dk16_ref, dv1_ref, dv4_ref, dv16_ref,
             q_ref, k_ref, qw_ref, kw_ref, bd_ref,
             dqr_ref, dkr_ref, dvr_ref, dqw_ref, dkw_ref, scr4, scr16):
        i = pl.program_id(0)

        @pl.when(i == 0)
        def _():
            dqw_ref[...] = jnp.zeros_like(dqw_ref)
            dkw_ref[...] = jnp.zeros_like(dkw_ref)

        bd = bd_ref[...]

        def total(d1_ref, d4_ref, d16_ref):
            return d1_ref[...] + _from_pattern(d4_ref, R4, scr4) + _from_pattern(d16_ref, R16, scr16)

        def one(d_refs, x_ref, w_ref, scale, dx_ref, dw_ref):
            dy = total(*d_refs) * scale
            x = x_ref[...]
            rstd = lax.rsqrt(_segsum(x * x, bd) * (1.0 / ATT_HD) + EPS)
            nrm = x * rstd
            dw_ref[...] = dw_ref[...] + jnp.sum(dy * nrm, axis=0, keepdims=True)
            g = dy * w_ref[...]
            dx_ref[...] = rstd * (g - nrm * (_segsum(g * nrm, bd) * (1.0 / ATT_HD)))

        one((dq1_ref, dq4_ref, dq16_ref), q_ref, qw_ref, ATT_HD ** -0.5, dqr_ref, dqw_ref)
        one((dk1_ref, dk4_ref, dk16_ref), k_ref, kw_ref, 1.0, dkr_ref, dkw_ref)
        dvr_ref[...] = total(dv1_ref, dv4_ref, dv16_ref)

    blk = pl.BlockSpec((tm, 512), lambda i: (i, 0))
    pats = [blk, _pattern_spec(R4), _pattern_spec(R16)]
    row = pl.BlockSpec((1, 512), lambda i: (0, 0))
    acc = pl.BlockSpec((8, 512), lambda i: (0, 0))
    return pl.pallas_call(
        body, grid=(s // tm,),
        in_specs=pats * 3 + [pl.BlockSpec((tm, 512), lambda i: (i, COL_ATT_Q)),
                             pl.BlockSpec((tm, 512), lambda i: (i, COL_ATT_K)), row, row,
                             pl.BlockSpec((512, 512), lambda i: (0, 0))],
        out_specs=[blk, blk, blk, acc, acc],
        out_shape=[_sds((s, 512))] * 3 + [_sds((8, 512))] * 2,
        scratch_shapes=[SLABS] * 2,
        compiler_params=_cp(("arbitrary",)), name="att_pre_bwd")(*dq_pats, *dk_pats, *dv_pats, proj, proj,
                                                                  qw_t, kw_t, bd64)


def _dn_scan_bwd(do, sp, qd, kt, w, vn, attn, gl):
    s = do.shape[0]
    nc = s // CHUNK

    def body(do_ref, sp_ref, qd_ref, kt_ref, w_ref, vn_ref, attn_ref, gl_ref,
             du_ref, dqd_ref, dkt_ref, dw_ref, dattn_ref, dgl_ref, ds_ref):
        n = pl.program_id(0)

        @pl.when(n == 0)
        def _():
            ds_ref[...] = jnp.zeros_like(ds_ref)

        def chain(cc, h):
            rows = slice(CHUNK * cc, CHUNK * cc + CHUNK)
            lanes = slice(128 * h, 128 * h + 128)
            dsn = ds_ref[h]
            st = sp_ref[cc, h]
            dsb, stb = dsn.astype(BF16), st.astype(BF16)
            dob = do_ref[rows, lanes].astype(BF16)
            vnb = vn_ref[rows, lanes].astype(BF16)
            dvn = _dot_tn(attn_ref[cc, h].astype(BF16), dob) + _dot(kt_ref[rows, lanes].astype(BF16), dsb)
            du_ref[rows, lanes] = dvn
            dqd_ref[rows, lanes] = _dot_nt(dob, stb)
            dattn_ref[cc, h] = _dot_nt(dob, vnb)
            dkt_ref[rows, lanes] = _dot_nt(vnb, dsb)
            tot = jnp.sum(jnp.sum(st * dsn, axis=1, keepdims=True), axis=0, keepdims=True)
            dgl_ref[cc, h] = jnp.broadcast_to(tot, (1, 128))
            qdo = _dot_tn(qd_ref[rows, lanes].astype(BF16), dob)
            yield
            dvb = dvn.astype(BF16)
            dw_ref[rows, lanes] = -_dot_nt(dvb, stb)
            ds_ref[h] = qdo + dsn * gl_ref[cc, h] - _dot_tn(w_ref[rows, lanes].astype(BF16), dvb)

        for cc in reversed(range(CPS)):
            _interleave([chain(cc, h) for h in range(DN_HEADS)])

    nsteps = nc // CPS
    big = pl.BlockSpec((CPS * CHUNK, 512), lambda n: (nsteps - 1 - n, 0))
    sq = pl.BlockSpec((CPS, DN_HEADS, CHUNK, CHUNK), lambda n: (nsteps - 1 - n, 0, 0, 0))
    glb = pl.BlockSpec((CPS, DN_HEADS, 1, 128), lambda n: (nsteps - 1 - n, 0, 0, 0))
    return pl.pallas_call(
        body, grid=(nsteps,),
        in_specs=[big, pl.BlockSpec((CPS, DN_HEADS, DN_HD, DN_HD), lambda n: (nsteps - 1 - n, 0, 0, 0)),
                  big, big, big, big, sq, glb],
        out_specs=[big, big, big, big, sq, glb],
        out_shape=[_sds((s, 512))] * 4 + [_sds((nc, DN_HEADS, CHUNK, CHUNK)), _sds((nc, DN_HEADS, 1, 128))],
        scratch_shapes=[pltpu.VMEM((DN_HEADS, DN_HD, DN_HD), F32)],
        compiler_params=_cp(("arbitrary",)), name="dn_scan_bwd")(do, sp, qd, kt, w, vn, attn, gl)


def _dn_prep_bwd(qn, kn, v, bg, bgt, tri, t_inv, attn, u, w, du, dw, dqd, dkt, dattn, dgl):
    s = qn.shape[0]
    nc = s // CHUNK

    def body(q_ref, k_ref, v_ref, bg_ref, bgt_ref, tri_ref, t_ref, attn_ref, u_ref, w_ref,
             du_ref, dw_ref, dqd_ref, dkt_ref, dattn_ref, dgl_ref,
             dq_ref, dk_ref, dv_ref, dbg_ref):
        tri_v = tri_ref[...]
        lane = lax.broadcasted_iota(jnp.int32, (CHUNK, 128), 1)
        rowi = lax.broadcasted_iota(jnp.int32, (CHUNK, 1), 0)
        ones_b = jnp.ones((CHUNK, 128), BF16)
        parts = [[] for _ in range(CPS)]

        def chain(cc, h, bgv, gc_col, gc_row):
            rows = slice(CHUNK * cc, CHUNK * cc + CHUNK)
            lanes = slice(128 * h, 128 * h + 128)
            gcc, beta, incl, strict, decay, gl = _chunk_common(bgv, gc_col, gc_row, h)
            q = q_ref[rows, lanes]
            k = k_ref[rows, lanes]
            vv = v_ref[rows, lanes]
            ts = _split(t_ref[cc, h])
            egc = jnp.exp(gcc)
            kb = k * beta
            a_mat = jnp.where(strict, _mm_nt(kb, k) * decay, 0.0)
            dvb = _mm3_tn(ts, _split(du_ref[rows, lanes]))
            dkbg = _mm3_tn(ts, _split(dw_ref[rows, lanes]))
            yield
            d_a = jnp.where(strict, -(_mm_nt(dvb, u_ref[rows, lanes]) + _mm_nt(dkbg, w_ref[rows, lanes])), 0.0)
            d_m = d_a * decay
            dattn_m = jnp.where(incl, dattn_ref[cc, h], 0.0)
            dqk = dattn_m * decay
            e_hi, e_lo = _split(d_a * a_mat + dattn_m * attn_ref[cc, h])
            yield
            dkb = _mm(d_m, k)
            dk = _mm_tn(d_m, kb) + _mm_tn(dqk, q)
            dq = _mm(dqk, k)
            e_colsum = (_dot_tn(e_hi, ones_b) + _dot_tn(e_lo, ones_b))[:, 0:1]
            e_rowsum = jnp.sum(e_hi.astype(F32) + e_lo.astype(F32), axis=1, keepdims=True)
            yield
            dgc = e_rowsum - e_colsum
            dqd = dqd_ref[rows, lanes]
            dq = dq + dqd * egc
            dgc = dgc + jnp.sum(dqd * q, axis=1, keepdims=True) * egc
            tail = jnp.exp(gl - gcc)
            dkt = dkt_ref[rows, lanes]
            dk = dk + dkt * tail
            r = jnp.sum(dkt * k, axis=1, keepdims=True) * tail
            dgc = dgc - r
            dgl_tot = jnp.sum(r, axis=0, keepdims=True) + dgl_ref[cc, h][:, 0:1] * jnp.exp(gl)
            rk = jnp.sum(dkbg * k, axis=1, keepdims=True)
            dk = dk + dkbg * (beta * egc) + dkb * beta
            dbeta = rk * egc + jnp.sum(dkb * k, axis=1, keepdims=True) + jnp.sum(dvb * vv, axis=1, keepdims=True)
            dgc = dgc + rk * beta * egc
            dgc = dgc + jnp.where(rowi == CHUNK - 1, dgl_tot, 0.0)
            dq_ref[rows, lanes] = dq
            dk_ref[rows, lanes] = dk
            dv_ref[rows, lanes] = dvb * beta
            parts[cc].append((h, dgc, dbeta))

        gens = []
        for cc in range(CPS):
            bgv = bg_ref[CHUNK * cc:CHUNK * cc + CHUNK, :]
            gc_col, gc_row = _chunk_cumsum(bgv, bgt_ref[cc], tri_v)
            gens += [chain(cc, h, bgv, gc_col, gc_row) for h in range(DN_HEADS)]
        _interleave(gens)
        for cc in range(CPS):
            dgc_mat = jnp.zeros((CHUNK, 128), F32)
            dbeta_mat = jnp.zeros((CHUNK, 128), F32)
            for h, dgc, dbeta in parts[cc]:
                dgc_mat = dgc_mat + jnp.where(lane == DN_HEADS + h, dgc, 0.0)
                dbeta_mat = dbeta_mat + jnp.where(lane == h, dbeta, 0.0)
            dbg_ref[CHUNK * cc:CHUNK * cc + CHUNK, :] = _mmx_tn(tri_v, dgc_mat) + dbeta_mat

    big = pl.BlockSpec((CPS * CHUNK, 512), lambda n: (n, 0))
    sq = pl.BlockSpec((CPS, DN_HEADS, CHUNK, CHUNK), lambda n: (n, 0, 0, 0))
    glb = pl.BlockSpec((CPS, DN_HEADS, 1, 128), lambda n: (n, 0, 0, 0))
    small = pl.BlockSpec((CPS * CHUNK, 128), lambda n: (n, 0))
    return pl.pallas_call(
        body, grid=(nc // CPS,),
        in_specs=[big, big, big, small, pl.BlockSpec((CPS, 8, CHUNK), lambda n: (n, 0, 0)),
                  pl.BlockSpec((CHUNK, CHUNK), lambda n: (0, 0)), sq, sq, big, big,
                  big, big, big, big, sq, glb],
        out_specs=[big, big, big, small],
        out_shape=[_sds((s, 512))] * 3 + [_sds((s, 128))],
        compiler_params=_cp(("parallel",)), name="dn_prep_bwd")(qn, kn, v, bg, bgt, tri, t_inv, attn, u, w,
                                                                  du, dw, dqd, dkt, dattn, dgl)


def _dn_pre_bwd(dqn, dkn, dv, dbg, proj, conv_w8, alog_l, dtb_l):
    s = proj.shape[0]
    tr = 512
    nh = tr // 8

    def body(dq_ref, dk_ref, dv_ref, dbg_ref, u_ref, halo_ref, ba_ref, w_ref, al_ref, dt_ref,
             dy_ref, dba_ref, dsm_ref):
        i = pl.program_id(0)

        @pl.when(i == 0)
        def _():
            dsm_ref[...] = jnp.zeros_like(dsm_ref)

        keep = (i > 0).astype(F32)
        for c in range(12):
            lanes = slice(128 * c, 128 * c + 128)
            y = _conv_group(u_ref[:, lanes], halo_ref[:, lanes] * keep, w_ref, c)
            sg = _sigmoid(y)
            sv = y * sg
            if c < 8:
                rs = lax.rsqrt(jnp.sum(sv * sv, axis=1, keepdims=True) + EPS)
                n = sv * rs
                if c < 4:
                    dn = dq_ref[:, lanes] * (DN_HD ** -0.5)
                else:
                    dn = dk_ref[:, slice(128 * (c - 4), 128 * (c - 3))]
                dsv = rs * (dn - n * jnp.sum(dn * n, axis=1, keepdims=True))
            else:
                dsv = dv_ref[:, slice(128 * (c - 8), 128 * (c - 7))]
            dy_ref[:, lanes] = dsv * _silu_grad(y, sg)
        _, lane, sig_b, t, nega, g = _beta_g(ba_ref[...], al_ref[...], dt_ref[...])
        dbg = dbg_ref[...]
        da = dbg * nega * _sigmoid(t)
        is_b = lane < DN_HEADS
        is_a = jnp.logical_and(lane >= DN_HEADS, lane < 2 * DN_HEADS)
        dba_ref[...] = jnp.where(is_b, dbg * sig_b * (1.0 - sig_b), jnp.where(is_a, da, 0.0))
        d_alog = jnp.sum(jnp.where(is_a, dbg * g, 0.0), axis=0, keepdims=True)
        d_dtb = jnp.sum(jnp.where(is_a, da, 0.0), axis=0, keepdims=True)
        row = lax.broadcasted_iota(jnp.int32, (8, 128), 0)
        dsm_ref[...] = dsm_ref[...] + jnp.where(row == 0, d_alog, jnp.where(row == 1, d_dtb, 0.0))

    blk = pl.BlockSpec((tr, 512), lambda i: (i, 0))
    return pl.pallas_call(
        body, grid=(s // tr,),
        in_specs=[blk, blk, blk, pl.BlockSpec((tr, 128), lambda i: (i, 0)),
                  pl.BlockSpec((tr, 1536), lambda i: (i, 0)),
                  pl.BlockSpec((8, 1536), lambda i: (jnp.maximum(i * nh - 1, 0), 0)),
                  pl.BlockSpec((tr, 128), lambda i: (i, COL_BA_128)),
                  pl.BlockSpec((8, 1536), lambda i: (0, 0)),
                  pl.BlockSpec((1, 128), lambda i: (0, 0)), pl.BlockSpec((1, 128), lambda i: (0, 0))],
        out_specs=[pl.BlockSpec((tr, 1536), lambda i: (i, 0)), pl.BlockSpec((tr, 128), lambda i: (i, 0)),
                   pl.BlockSpec((8, 128), lambda i: (0, 0))],
        out_shape=[_sds((s, 1536)), _sds((s, 128)), _sds((8, 128))],
        compiler_params=_cp(("arbitrary",)), name="dn_pre_bwd")(dqn, dkn, dv, dbg, proj, proj, proj, conv_w8,
                                                                 alog_l, dtb_l)


def _conv_bwd(dy, proj, conv_w8):
    s = dy.shape[0]
    tr = 512
    nh = tr // 8
    nblk = s // tr

    def body(dy_ref, dyn_ref, u_ref, halo_ref, w_ref, du_ref, dw_ref):
        i = pl.program_id(0)

        @pl.when(i == 0)
        def _():
            dw_ref[...] = jnp.zeros_like(dw_ref)

        keep_prev = (i > 0).astype(F32)
        keep_next = (i < nblk - 1).astype(F32)
        row = lax.broadcasted_iota(jnp.int32, (8, 128), 0)
        for c in range(12):
            lanes = slice(128 * c, 128 * c + 128)
            dyc = dy_ref[:, lanes]
            dcat = jnp.concatenate([dyc, dyn_ref[:, lanes] * keep_next], axis=0)
            xcat = jnp.concatenate([halo_ref[:, lanes] * keep_prev, u_ref[:, lanes]], axis=0)
            du = dyc * w_ref[CONV_W - 1:CONV_W, lanes]
            dwc = jnp.where(row == CONV_W - 1, jnp.sum(dyc * u_ref[:, lanes], axis=0, keepdims=True), 0.0)
            for k in range(1, CONV_W):
                du = du + pltpu.roll(dcat, tr + 8 - k, 0)[0:tr] * w_ref[CONV_W - 1 - k:CONV_W - k, lanes]
                ush = pltpu.roll(xcat, k, 0)[8:8 + tr]
                dwc = dwc + jnp.where(row == CONV_W - 1 - k, jnp.sum(dyc * ush, axis=0, keepdims=True), 0.0)
            du_ref[:, lanes] = du
            dw_ref[:, lanes] = dw_ref[:, lanes] + dwc

    return pl.pallas_call(
        body, grid=(nblk,),
        in_specs=[pl.BlockSpec((tr, 1536), lambda i: (i, 0)),
                  pl.BlockSpec((8, 1536), lambda i: (jnp.minimum((i + 1) * nh, s // 8 - 1), 0)),
                  pl.BlockSpec((tr, 1536), lambda i: (i, 0)),
                  pl.BlockSpec((8, 1536), lambda i: (jnp.maximum(i * nh - 1, 0), 0)),
                  pl.BlockSpec((8, 1536), lambda i: (0, 0))],
        out_specs=[pl.BlockSpec((tr, 1536), lambda i: (i, 0)), pl.BlockSpec((8, 1536), lambda i: (0, 0))],
        out_shape=[_sds((s, 1536)), _sds((8, 1536))],
        compiler_params=_cp(("arbitrary",)), name="conv_bwd")(dy, dy, proj, proj, conv_w8)


def _in_bwd_dx(d_qkv_dn, dz, dq_att, dk_att, dv_att, dgate, dba, w_bf, x, dy, norm_w):
    s = x.shape[0]
    tm = 256

    def body(a_ref, b_ref, c_ref, d_ref, e_ref, f_ref, g_ref, w_ref, x_ref, dy_ref, nw_ref,
             dx_ref, dp_ref, dnw_ref):
        i = pl.program_id(0)

        @pl.when(i == 0)
        def _():
            dnw_ref[...] = jnp.zeros_like(dnw_ref)

        dp = jnp.concatenate([r[...].astype(BF16) for r in (a_ref, b_ref, c_ref, d_ref, e_ref, f_ref, g_ref)],
                             axis=1)
        dp_ref[...] = dp
        dh = _dot(dp, w_ref[...])
        xv = x_ref[...]
        rstd = lax.rsqrt(jnp.mean(xv * xv, axis=-1, keepdims=True) + EPS)
        xh = xv * rstd
        dnw_ref[...] = dnw_ref[...] + jnp.sum(dh * xh, axis=0, keepdims=True)
        g = dh * nw_ref[...]
        dx_ref[...] = rstd * (g - xh * jnp.mean(g * xh, axis=-1, keepdims=True)) + dy_ref[...]

    def blk(n):
        return pl.BlockSpec((tm, n), lambda i: (i, 0))

    return pl.pallas_call(
        body, grid=(s // tm,),
        in_specs=[blk(1536), blk(512), blk(512), blk(512), blk(512), blk(512), blk(128),
                  pl.BlockSpec((D_IN_PAD, D_MODEL), lambda i: (0, 0)), blk(D_MODEL), blk(D_MODEL),
                  pl.BlockSpec((1, D_MODEL), lambda i: (0, 0))],
        out_specs=[blk(D_MODEL), blk(D_IN_PAD), pl.BlockSpec((8, D_MODEL), lambda i: (0, 0))],
        out_shape=[_sds((s, D_MODEL)), _sds((s, D_IN_PAD), BF16), _sds((8, D_MODEL))],
        compiler_params=_cp(("arbitrary",)), name="in_bwd_dx")(d_qkv_dn, dz, dq_att, dk_att, dv_att, dgate, dba,
                                                                w_bf, x, dy, norm_w)


def _in_bwd_dw(x, norm_w, dp):
    s = x.shape[0]
    tm = 256
    tn = D_IN_PAD // 3

    def body(x_ref, nw_ref, dp_ref, dw_ref):
        i = pl.program_id(1)

        @pl.when(i == 0)
        def _():
            dw_ref[...] = jnp.zeros_like(dw_ref)

        xv = x_ref[...]
        rstd = lax.rsqrt(jnp.mean(xv * xv, axis=-1, keepdims=True) + EPS)
        h = (xv * rstd * nw_ref[...]).astype(BF16)
        dw_ref[...] = dw_ref[...] + _dot_tn(dp_ref[...], h)

    return pl.pallas_call(
        body, grid=(3, s // tm),
        in_specs=[pl.BlockSpec((tm, D_MODEL), lambda j, i: (i, 0)), pl.BlockSpec((1, D_MODEL), lambda j, i: (0, 0)),
                  pl.BlockSpec((tm, tn), lambda j, i: (i, j))],
        out_specs=pl.BlockSpec((tn, D_MODEL), lambda j, i: (j, 0)),
        out_shape=_sds((D_IN_PAD, D_MODEL)),
        compiler_params=_cp(("parallel", "arbitrary")), name="in_bwd_dw")(x, norm_w, dp)


def _flat(a):
    return a.reshape(-1, a.shape[-1])


def _as_pattern(a, r):
    return a if r == 1 else a.reshape(r, a.shape[0] // r, a.shape[1])


D_SHARD = D_IN // N_CHIPS
BA_START = 4 * D_DN
BA_PACKED = 4096
S1_HEAD = BA_START - D_SHARD
S1_BA = 2 * D_SHARD - BA_START


def _pack_rows(g):
    pad = jnp.zeros((D_IN_PAD - D_IN, g.shape[2]), g.dtype)
    s2_ba = 2 * DN_HEADS - S1_BA
    return jnp.concatenate([g[0][:D_SHARD], g[1][:S1_HEAD], g[2][s2_ba:D_SHARD], g[3][:D_SHARD],
                            g[1][S1_HEAD:D_SHARD], g[2][:s2_ba], pad], axis=0)


def _unpack_rows(p, rows):
    mid = BA_PACKED + S1_BA
    pad = jnp.zeros((rows - D_SHARD, p.shape[1]), p.dtype)
    return jnp.stack([jnp.concatenate([p[0:D_SHARD], pad], axis=0),
                      jnp.concatenate([p[D_SHARD:BA_START], p[BA_PACKED:mid], pad], axis=0),
                      jnp.concatenate([p[mid:BA_PACKED + 2 * DN_HEADS], p[BA_START:BA_START + S1_HEAD], pad], axis=0),
                      jnp.concatenate([p[BA_START + S1_HEAD:BA_PACKED], pad], axis=0)])


def _lane_row(vec, offset):
    return jnp.pad(vec.reshape(1, -1), ((0, 0), (offset, 128 - offset - vec.shape[0])))


def _local_step(x, tgt, norm_w, w_bf, conv_w, a_log, dt_bias, dn_norm_w, q_norm_w, k_norm_w, rel_bias, w_out_bf):
    s = x.shape[0]
    nc = s // CHUNK
    conv_w8 = jnp.pad(conv_w, ((0, 8 - CONV_W), (0, 0)))
    alog_l = _lane_row(a_log.reshape(-1), DN_HEADS)
    dtb_l = _lane_row(dt_bias.reshape(-1), DN_HEADS)
    dnw_t = jnp.tile(dn_norm_w.reshape(1, DN_HD), (1, DN_HEADS))
    qw_t = jnp.tile(q_norm_w.reshape(1, ATT_HD), (1, ATT_HEADS))
    kw_t = jnp.tile(k_norm_w.reshape(1, ATT_HD), (1, ATT_HEADS))
    bd128 = _block_ones(512, DN_HD)
    bd64 = _block_ones(512, ATT_HD)
    tri = _tri_incl()
    buckets = _bucket_tables()

    proj = _in_proj(x, norm_w, w_bf)
    qn, kn, v_dn, bg = _dn_pre(proj, conv_w8, alog_l, dtb_l)
    bgt = bg[:, 0:8].reshape(nc, CHUNK, 8).transpose(0, 2, 1)
    u, w, qd, kt, attn, t_inv, gl = _dn_prep(qn, kn, v_dn, bg, bgt, tri)
    o_dn, vn, sp = _dn_scan(u, w, qd, kt, attn, gl)
    q1, k1, q4, k4, v4, q16, k16, v16 = _att_pre(proj, qw_t, kw_t, bd64)
    rs = [r for _, r in PATTERNS]
    qkv = [(q1, k1, proj, COL_ATT_V), (_flat(q4), _flat(k4), _flat(v4), 0), (_flat(q16), _flat(k16), _flat(v16), 0)]
    bias = _bias_fwd(rel_bias, buckets)
    o_pats, lse_pats = [], []
    for p, r in enumerate(rs):
        o_p, lse_p = _att_fwd(*qkv[p], bias, p, r, "att_fwd_r%d" % r)
        o_pats.append(_as_pattern(o_p, r))
        lse_pats.append(_as_pattern(lse_p, r))
    mixed, o_att, l1, l4, l16 = _post_fwd(o_dn, proj, o_pats, lse_pats, dnw_t, bd128)
    dy, loss_blk = _out_fwd(x, mixed, w_out_bf, tgt)

    dmixed, d_w_out = _out_bwd(dy, mixed, w_out_bf)
    do_dn, dz, dgate, do1, do4, do16, dl1, dl4, dl16, d_dnw = _post_bwd(dmixed, o_dn, proj, o_att, dnw_t, bd128, bd64)
    side = [(do1, l1, dl1), (_flat(do4), _flat(l4), _flat(dl4)), (_flat(do16), _flat(l16), _flat(dl16))]
    dq_pats, dk_pats, dv_pats, ds_accs = [], [], [], []
    for p, r in enumerate(rs):
        dq_p, dk_p, dv_p, ds_p = _att_bwd(*qkv[p], *side[p], bias, p, r, "att_bwd_r%d" % r)
        dq_pats.append(_as_pattern(dq_p, r))
        dk_pats.append(_as_pattern(dk_p, r))
        dv_pats.append(_as_pattern(dv_p, r))
        ds_accs.append(ds_p)
    d_rel_bias = _bias_bwd(ds_accs, buckets)
    dq_att, dk_att, dv_att, d_qw, d_kw = _att_pre_bwd(dq_pats, dk_pats, dv_pats, proj, qw_t, kw_t, bd64)
    du, dqd, dkt, dw, dattn, dgl = _dn_scan_bwd(do_dn, sp, qd, kt, w, vn, attn, gl)
    dqn, dkn, dv_dn, dbg = _dn_prep_bwd(qn, kn, v_dn, bg, bgt, tri, t_inv, attn, u, w, du, dw, dqd, dkt, dattn, dgl)
    dyc, dba, dsm = _dn_pre_bwd(dqn, dkn, dv_dn, dbg, proj, conv_w8, alog_l, dtb_l)
    d_qkv_dn, d_conv8 = _conv_bwd(dyc, proj, conv_w8)
    grad_x, dp, d_nw8 = _in_bwd_dx(d_qkv_dn, dz, dq_att, dk_att, dv_att, dgate, dba, w_bf, x, dy, norm_w)
    d_w_in_t = _in_bwd_dw(x, norm_w, dp)

    grads = dict(
        norm_w=d_nw8[0:1, :],
        w_in_t=d_w_in_t,
        conv_w=d_conv8[0:CONV_W, :],
        a_log=dsm[0:1, DN_HEADS:2 * DN_HEADS],
        dt_bias=dsm[1:2, DN_HEADS:2 * DN_HEADS],
        dn_norm_w=d_dnw[0:1, :],
        q_norm_w=d_qw[0:1, :].reshape(ATT_HEADS, ATT_HD),
        k_norm_w=d_kw[0:1, :].reshape(ATT_HEADS, ATT_HD),
        rel_bias=d_rel_bias,
        w_out=d_w_out,
    )
    return loss_blk[0, 0], grad_x, grads


MESH_ID = pl.DeviceIdType.MESH
ANY = pl.BlockSpec(memory_space=pl.ANY)


def _position():
    return lax.axis_index("x"), lax.axis_index("y"), lax.axis_index("c")


def _other_chips(x, y):
    return [(1 - x, y), (x, 1 - y), (1 - x, 1 - y)]


SHARD_PAD = 1040
WIN = 528


def _row_split(n):
    return (n // 2) // 128 * 128


def _part(ref, cc):
    n = ref.shape[0]
    sp = _row_split(n)
    return ref.at[pl.ds(0, sp)] if cc == 0 else ref.at[pl.ds(sp, n - sp)]


def _gather_weights(wt_s, w_out_s, conv_s):
    def body(a_ref, b_ref, c_ref, ga_ref, gb_ref, gc_ref, send_sems, recv_sems, loc_sems):
        x, y, c = _position()
        me = 2 * x + y
        sib = (x, y, 1 - c)
        big = ((a_ref, ga_ref), (b_ref, gb_ref))
        local = [pltpu.make_async_copy(src, dst.at[me], loc_sems.at[t])
                 for t, (src, dst) in enumerate(big + ((c_ref, gc_ref),))]
        for cp in local:
            cp.start()
        others = _other_chips(x, y)

        def exchange(cc):
            sends = []
            for j, (px, py) in enumerate(others):
                for t, (src, dst) in enumerate(big):
                    k = 3 * j + t
                    sends.append(pltpu.make_async_remote_copy(
                        src_ref=_part(src, cc), dst_ref=_part(dst.at[me], cc), send_sem=send_sems.at[k],
                        recv_sem=recv_sems.at[k], device_id=(px, py, c), device_id_type=MESH_ID))
                sends.append(pltpu.make_async_remote_copy(
                    src_ref=c_ref, dst_ref=gc_ref.at[me], send_sem=send_sems.at[3 * j + 2],
                    recv_sem=recv_sems.at[3 * j + 2], device_id=(px, py, c), device_id_type=MESH_ID))
            for cp in sends:
                cp.start()
            for j, (px, py) in enumerate(others):
                src_chip = 2 * px + py
                for t, (src, dst) in enumerate(big):
                    landed = _part(dst.at[src_chip], cc)
                    pltpu.make_async_remote_copy(
                        src_ref=_part(src, cc), dst_ref=landed, send_sem=send_sems.at[3 * j + t],
                        recv_sem=recv_sems.at[3 * j + t], device_id=(px, py, c), device_id_type=MESH_ID).wait_recv()
                    k = 9 + 2 * j + t
                    fwd = pltpu.make_async_remote_copy(
                        src_ref=landed, dst_ref=landed, send_sem=send_sems.at[k], recv_sem=recv_sems.at[k],
                        device_id=sib, device_id_type=MESH_ID)
                    fwd.start()
                    sends.append(fwd)
                pltpu.make_async_remote_copy(
                    src_ref=c_ref, dst_ref=gc_ref.at[src_chip], send_sem=send_sems.at[3 * j + 2],
                    recv_sem=recv_sems.at[3 * j + 2], device_id=(px, py, c), device_id_type=MESH_ID).wait_recv()
            for j, (px, py) in enumerate(others):
                src_chip = 2 * px + py
                for t, (src, dst) in enumerate(big):
                    k = 9 + 2 * j + t
                    theirs = _part(dst.at[src_chip], 1 - cc)
                    pltpu.make_async_remote_copy(
                        src_ref=theirs, dst_ref=theirs, send_sem=send_sems.at[k], recv_sem=recv_sems.at[k],
                        device_id=sib, device_id_type=MESH_ID).wait_recv()
            for cp in sends:
                cp.wait_send()

        for cc in (0, 1):
            pl.when(c == cc)(functools.partial(exchange, cc))
        for cp in local:
            cp.wait()

    srcs = (wt_s, w_out_s, conv_s)
    n_sem = 9 + 6
    return pl.pallas_call(
        body, in_specs=[ANY] * 3, out_specs=[ANY] * 3,
        out_shape=[_sds((N_CHIPS,) + a.shape, a.dtype) for a in srcs],
        scratch_shapes=[pltpu.SemaphoreType.DMA((n_sem,)), pltpu.SemaphoreType.DMA((n_sem,)),
                        pltpu.SemaphoreType.DMA((3,))],
        name="gather_weights")(*srcs)


def _exchange_grads(gw_in4, gw_out4, small):
    n_big = 2
    n_dev = 8

    def body(a_ref, b_ref, s_ref, ra_ref, rb_ref, rs_ref, send_sems, recv_sems, loc_sem):
        x, y, c = _position()
        dev = 4 * x + 2 * y + c
        pairs = ((a_ref, ra_ref), (b_ref, rb_ref))
        local = pltpu.make_async_copy(s_ref, rs_ref.at[dev], loc_sem)
        local.start()
        sends = []
        for j, (px, py) in enumerate(_other_chips(x, y)):
            for t, (src, dst) in enumerate(pairs):
                k = n_big * j + t
                sends.append(pltpu.make_async_remote_copy(
                    src_ref=src.at[2 * px + py], dst_ref=dst.at[j], send_sem=send_sems.at[k],
                    recv_sem=recv_sems.at[k], device_id=(px, py, c), device_id_type=MESH_ID))
        flips = [(dx, dy, dc) for dx in (0, 1) for dy in (0, 1) for dc in (0, 1)][1:]
        for f, (dx, dy, dc) in enumerate(flips):
            k = 3 * n_big + f
            peer = (x ^ dx, y ^ dy, c ^ dc)
            sends.append(pltpu.make_async_remote_copy(
                src_ref=s_ref, dst_ref=rs_ref.at[dev], send_sem=send_sems.at[k], recv_sem=recv_sems.at[k],
                device_id=peer, device_id_type=MESH_ID))
        for cp in sends:
            cp.start()
        for j, (px, py) in enumerate(_other_chips(x, y)):
            for t, (src, dst) in enumerate(pairs):
                k = n_big * j + t
                pltpu.make_async_remote_copy(
                    src_ref=src.at[0], dst_ref=dst.at[j], send_sem=send_sems.at[k], recv_sem=recv_sems.at[k],
                    device_id=(px, py, c), device_id_type=MESH_ID).wait_recv()
        for f, (dx, dy, dc) in enumerate(flips):
            k = 3 * n_big + f
            peer = (x ^ dx, y ^ dy, c ^ dc)
            pltpu.make_async_remote_copy(
                src_ref=s_ref, dst_ref=rs_ref.at[4 * peer[0] + 2 * peer[1] + peer[2]], send_sem=send_sems.at[k],
                recv_sem=recv_sems.at[k], device_id=peer, device_id_type=MESH_ID).wait_recv()
        for cp in sends:
            cp.wait_send()
        local.wait()

    n_sem = 3 * n_big + n_dev - 1
    return pl.pallas_call(
        body, in_specs=[ANY] * 3, out_specs=[ANY] * 3,
        out_shape=[_sds((3,) + gw_in4.shape[1:], gw_in4.dtype), _sds((3,) + gw_out4.shape[1:], gw_out4.dtype),
                   _sds((n_dev,) + small.shape, small.dtype)],
        scratch_shapes=[pltpu.SemaphoreType.DMA((n_sem,)), pltpu.SemaphoreType.DMA((n_sem,)),
                        pltpu.SemaphoreType.DMA],
        name="exchange_grads")(gw_in4, gw_out4, small)


def _plane_sum(own, recv, name):
    rows, cols = own.shape
    tr = 128

    def body(o_ref, r_ref, out_ref):
        acc = o_ref[...]
        for j in range(3):
            acc = acc + r_ref[j].astype(F32)
        out_ref[...] = acc

    return pl.pallas_call(
        body, grid=(pl.cdiv(rows, tr),),
        in_specs=[pl.BlockSpec((tr, cols), lambda i: (i, 0)), pl.BlockSpec((3, tr, cols), lambda i: (0, i, 0))],
        out_specs=pl.BlockSpec((tr, cols), lambda i: (i, 0)), out_shape=_sds((rows, cols)),
        compiler_params=_cp(("parallel",)), name=name)(own, recv)


def _sum_pair(a, b, name):
    n, rows, cols = a.shape
    tr = 128

    def body(a_ref, b_ref, s_ref, sb_ref):
        tot = a_ref[...] + b_ref[...]
        s_ref[...] = tot
        sb_ref[...] = tot.astype(BF16)

    blk = pl.BlockSpec((1, tr, cols), lambda k, i: (k, i, 0))
    return pl.pallas_call(
        body, grid=(n, pl.cdiv(rows, tr)), in_specs=[blk, blk], out_specs=[blk, blk],
        out_shape=[_sds(a.shape), _sds(a.shape, BF16)],
        compiler_params=_cp(("parallel", "parallel")), name=name)(a, b)


def _fill_parts(f_in, f_out, rows_in, rows_out):
    def body(a_ref, b_ref, fa_ref, fb_ref, send_sems, recv_sems, loc_sems):
        x, y, c = _position()
        sib = (x, y, 1 - c)
        pairs = ((a_ref, fa_ref), (b_ref, fb_ref))

        def fill(cc):
            mine = [_part(dst, cc) for _, dst in pairs]
            srcs = [src.at[pl.ds(0, m.shape[0])] for (src, _), m in zip(pairs, mine)]
            local = [pltpu.make_async_copy(s, m, loc_sems.at[t]) for t, (s, m) in enumerate(zip(srcs, mine))]
            sends = [pltpu.make_async_remote_copy(src_ref=s, dst_ref=m, send_sem=send_sems.at[t],
                                                  recv_sem=recv_sems.at[t], device_id=sib, device_id_type=MESH_ID)
                     for t, (s, m) in enumerate(zip(srcs, mine))]
            for cp in local + sends:
                cp.start()
            for t, (_, dst) in enumerate(pairs):
                theirs = _part(dst, 1 - cc)
                pltpu.make_async_remote_copy(src_ref=theirs, dst_ref=theirs, send_sem=send_sems.at[t],
                                             recv_sem=recv_sems.at[t], device_id=sib, device_id_type=MESH_ID).wait_recv()
            for cp in sends:
                cp.wait_send()
            for cp in local:
                cp.wait()

        for cc in (0, 1):
            pl.when(c == cc)(functools.partial(fill, cc))

    return pl.pallas_call(
        body, in_specs=[ANY] * 2, out_specs=[ANY] * 2,
        out_shape=[_sds((rows_in, D_MODEL)), _sds((rows_out, D_MODEL))],
        scratch_shapes=[pltpu.SemaphoreType.DMA((2,)), pltpu.SemaphoreType.DMA((2,)), pltpu.SemaphoreType.DMA((2,))],
        name="fill_parts")(f_in, f_out)


def _swap_windows(u_in, u_out, win_in, win_out):
    def body(a_ref, b_ref, ra_ref, rb_ref, send_sems, recv_sems):
        x, y, c = _position()
        sib = (x, y, 1 - c)
        cps = []
        for t, (src, dst, win) in enumerate(((a_ref, ra_ref, win_in), (b_ref, rb_ref, win_out))):
            split = _row_split(src.shape[1])
            start = pl.multiple_of((1 - c) * split, split)
            cps.append(pltpu.make_async_remote_copy(
                src_ref=src.at[:, pl.ds(start, win), :], dst_ref=dst, send_sem=send_sems.at[t],
                recv_sem=recv_sems.at[t], device_id=sib, device_id_type=MESH_ID))
        for cp in cps:
            cp.start()
        for cp in cps:
            cp.wait_recv()
        for cp in cps:
            cp.wait_send()

    return pl.pallas_call(
        body, in_specs=[ANY] * 2, out_specs=[ANY] * 2,
        out_shape=[_sds((N_CHIPS, win_in, D_MODEL)), _sds((N_CHIPS, win_out, D_MODEL))],
        scratch_shapes=[pltpu.SemaphoreType.DMA((2,)), pltpu.SemaphoreType.DMA((2,))],
        name="swap_windows")(u_in, u_out)


SMALL_LAYOUT = (("norm_w", 1024), ("conv_w", 6144), ("a_log", 128), ("dt_bias", 128), ("dn_norm_w", 128),
                ("q_norm_w", 512), ("k_norm_w", 512), ("rel_bias", 256))
SMALL_TOTAL = sum(n for _, n in SMALL_LAYOUT)


def _small_offset(name):
    off = 0
    for n, size in SMALL_LAYOUT:
        if n == name:
            return off
        off += size
    raise KeyError(name)


def _pack_small(grads):
    parts = []
    for name, size in SMALL_LAYOUT:
        flat = grads[name].reshape(1, -1)
        parts.append(jnp.pad(flat, ((0, 0), (0, size - flat.shape[1]))))
    return jnp.concatenate(parts, axis=1)


def _sum_small(rows):
    n_dev = rows.shape[0]
    q_off = _small_offset("q_norm_w")
    k_off = _small_offset("k_norm_w")

    def body(r_ref, tot_ref, qk_ref):
        tot = r_ref[0:1, :]
        for d in range(1, n_dev):
            tot = tot + r_ref[d:d + 1, :]
        tot_ref[...] = tot
        for row, off in ((0, q_off), (1, k_off)):
            s4 = tot[:, off:off + 128] + tot[:, off + 128:off + 256] + tot[:, off + 256:off + 384] \
                + tot[:, off + 384:off + 512]
            qk_ref[row:row + 1, :] = s4 + pltpu.roll(s4, ATT_HD, 1)

    return pl.pallas_call(
        body, in_specs=[pl.BlockSpec(memory_space=pltpu.VMEM)],
        out_specs=[pl.BlockSpec(memory_space=pltpu.VMEM)] * 2,
        out_shape=[_sds((1, SMALL_TOTAL)), _sds((2, 128))],
        compiler_params=_cp(), name="sum_small")(rows)


def _adamw_math(w, g, m, v):
    m = ADAM_B1 * m + (1.0 - ADAM_B1) * g
    v = ADAM_B2 * v + (1.0 - ADAM_B2) * (g * g)
    m_hat = m / (1.0 - ADAM_B1 ** ADAM_STEP)
    v_hat = v / (1.0 - ADAM_B2 ** ADAM_STEP)
    delta = -ADAM_LR * (m_hat / (jnp.sqrt(v_hat) + ADAM_EPS) + ADAM_WD * w)
    return delta, m, v


def _adamw_big(g, w, m, v, name):
    rows, cols = w.shape
    tr = 128

    def body(g_ref, w_ref, m_ref, v_ref, go_ref, d_ref, nm_ref, nv_ref):
        g = g_ref[...]
        go_ref[...] = g
        d_ref[...], nm_ref[...], nv_ref[...] = _adamw_math(w_ref[...], g, m_ref[...], v_ref[...])

    blk = pl.BlockSpec((tr, cols), lambda i: (i, 0))
    return pl.pallas_call(
        body, grid=(pl.cdiv(rows, tr),), in_specs=[blk] * 4, out_specs=[blk] * 4,
        out_shape=[_sds((rows, cols))] * 4, compiler_params=_cp(("parallel",)), name=name)(g, w, m, v)


def _adamw_small(w, g, m, v, name):
    def body(w_ref, g_ref, m_ref, v_ref, d_ref, nm_ref, nv_ref):
        d_ref[...], nm_ref[...], nv_ref[...] = _adamw_math(w_ref[...], g_ref[...], m_ref[...], v_ref[...])

    vm = pl.BlockSpec(memory_space=pltpu.VMEM)
    return pl.pallas_call(body, in_specs=[vm] * 4, out_specs=[vm] * 3, out_shape=[_sds(w.shape)] * 3,
                          compiler_params=_cp(), name=name)(w, g, m, v)


WEIGHTS = ("norm_w", "w_in", "conv_w", "a_log", "dt_bias", "dn_norm_w", "q_norm_w", "k_norm_w", "rel_bias", "w_out")


def kernel(x, norm_w, w_in, conv_w, a_log, dt_bias, dn_norm_w, q_norm_w, k_norm_w, rel_bias, w_out, loss_target, m_norm_w, m_w_in, m_conv_w, m_a_log, m_dt_bias, m_dn_norm_w, m_q_norm_w, m_k_norm_w, m_rel_bias, m_w_out, v_norm_w, v_w_in, v_conv_w, v_a_log, v_dt_bias, v_dn_norm_w, v_q_norm_w, v_k_norm_w, v_rel_bias, v_w_out):
    xi, yi, ci = _position()
    chip = 2 * xi + yi
    w_loc = dict(norm_w=norm_w, w_in=w_in[0].T, conv_w=conv_w[0], a_log=a_log, dt_bias=dt_bias, dn_norm_w=dn_norm_w,
                 q_norm_w=q_norm_w, k_norm_w=k_norm_w, rel_bias=rel_bias, w_out=w_out[0])
    m_loc = dict(norm_w=m_norm_w, w_in=m_w_in[0].T, conv_w=m_conv_w[0], a_log=m_a_log, dt_bias=m_dt_bias,
                 dn_norm_w=m_dn_norm_w, q_norm_w=m_q_norm_w, k_norm_w=m_k_norm_w, rel_bias=m_rel_bias,
                 w_out=m_w_out[0])
    v_loc = dict(norm_w=v_norm_w, w_in=v_w_in[0].T, conv_w=v_conv_w[0], a_log=v_a_log, dt_bias=v_dt_bias,
                 dn_norm_w=v_dn_norm_w, q_norm_w=v_q_norm_w, k_norm_w=v_k_norm_w, rel_bias=v_rel_bias,
                 w_out=v_w_out[0])

    wt_pad = jnp.pad(w_loc["w_in"].astype(BF16), ((0, SHARD_PAD - D_SHARD), (0, 0)))
    g_in, g_out, g_conv = _gather_weights(wt_pad, w_loc["w_out"].astype(BF16), w_loc["conv_w"])
    wt_full = _pack_rows(g_in)
    w_out_full = g_out.reshape(D_MODEL, D_MODEL)
    conv_full = g_conv.transpose(1, 0, 2).reshape(CONV_W, 3 * D_DN)

    loss_local, grad_x, grads = _local_step(x[0], loss_target[0], norm_w, wt_full, conv_full, a_log, dt_bias,
                                            dn_norm_w, q_norm_w, k_norm_w, rel_bias, w_out_full)
    loss = lax.psum(loss_local, ("x", "y", "c"))

    u_in = _unpack_rows(grads["w_in_t"], SHARD_PAD)
    u_out = grads["w_out"].reshape(N_CHIPS, D_MODEL // N_CHIPS, D_MODEL)
    win_out = u_out.shape[1] // 2
    sib_in, sib_out = _swap_windows(u_in, u_out, WIN, win_out)
    in_mine = lax.dynamic_slice_in_dim(u_in, ci * _row_split(SHARD_PAD), WIN, axis=1)
    out_mine = lax.dynamic_slice_in_dim(u_out, ci * win_out, win_out, axis=1)
    h_in, hb_in = _sum_pair(in_mine, sib_in, "chip_sum_w_in")
    h_out, hb_out = _sum_pair(out_mine, sib_out, "chip_sum_w_out")
    small = _pack_small(grads)
    r_in, r_out, r_small = _exchange_grads(hb_in, hb_out, small)
    f_in = _plane_sum(lax.dynamic_index_in_dim(h_in, chip, 0, keepdims=False), r_in, "shard_sum_w_in")
    f_out = _plane_sum(lax.dynamic_index_in_dim(h_out, chip, 0, keepdims=False), r_out, "shard_sum_w_out")
    full_in, full_out = _fill_parts(f_in, f_out, SHARD_PAD, u_out.shape[1])
    tot_small, qk = _sum_small(r_small.reshape(8, SMALL_TOTAL))

    def small_grad(name, n):
        off = _small_offset(name)
        return tot_small[:, off:off + n]

    conv_all = small_grad("conv_w", CONV_W * 3 * D_DN).reshape(CONV_W, 3 * D_DN)
    g_small = dict(
        norm_w=small_grad("norm_w", D_MODEL),
        conv_w=lax.dynamic_slice_in_dim(conv_all, chip * (3 * D_DN // N_CHIPS), 3 * D_DN // N_CHIPS, axis=1),
        a_log=small_grad("a_log", DN_HEADS),
        dt_bias=small_grad("dt_bias", DN_HEADS),
        dn_norm_w=small_grad("dn_norm_w", DN_HD),
        q_norm_w=qk[0:1, 0:ATT_HD],
        k_norm_w=qk[1:2, 0:ATT_HD],
        rel_bias=small_grad("rel_bias", ATT_HEADS * N_BUCKETS).reshape(ATT_HEADS, N_BUCKETS),
    )

    out_g, out_d, out_m, out_v = {}, {}, {}, {}
    out_g["w_in"], out_d["w_in"], out_m["w_in"], out_v["w_in"] = _adamw_big(
        full_in, w_loc["w_in"], m_loc["w_in"], v_loc["w_in"], "adamw_w_in")
    out_g["w_out"], out_d["w_out"], out_m["w_out"], out_v["w_out"] = _adamw_big(
        full_out, w_loc["w_out"], m_loc["w_out"], v_loc["w_out"], "adamw_w_out")
    for name in g_small:
        out_g[name] = g_small[name]
        out_d[name], out_m[name], out_v[name] = _adamw_small(w_loc[name], g_small[name], m_loc[name], v_loc[name],
                                                             "adamw_" + name)
    for d in (out_g, out_d, out_m, out_v):
        d["w_in"] = d["w_in"].T
        for name in ("w_in", "conv_w", "w_out"):
            d[name] = d[name][None]
    return (loss, grad_x[None], *[out_g[n] for n in WEIGHTS], *[out_d[n] for n in WEIGHTS],
            *[out_m[n] for n in WEIGHTS], *[out_v[n] for n in WEIGHTS])
```

```python
import functools
import math

import numpy as np
import jax
import jax.numpy as jnp
from jax import lax
from jax.experimental import pallas as pl
from jax.experimental.pallas import tpu as pltpu

F32 = jnp.float32
BF16 = jnp.bfloat16
HI = lax.Precision.HIGHEST

D_MODEL = 1024
D_DN = 512
DN_HEADS = 4
DN_HD = 128
CONV_W = 4
CHUNK = 64
D_ATT = 512
ATT_HEADS = 8
ATT_HD = 64
PATTERNS = ((128, 1), (512, 4), (2048, 16))
N_BUCKETS = 32
MAX_DISTANCE = 2048
D_IN = 4 * D_DN + 2 * DN_HEADS + 4 * D_ATT
D_IN_PAD = 4224
EPS = 1e-6
BLK = 128
NEG = -1e30
N_CHIPS = 4

ADAM_LR = 0.001
ADAM_B1 = 0.9
ADAM_B2 = 0.999
ADAM_EPS = 1e-08
ADAM_WD = 0.01
ADAM_STEP = 10

VMEM_LIMIT = 56 * 1024 * 1024

COL_Z = 3
COL_ATT_Q = 4
COL_ATT_K = 5
COL_ATT_V = 6
COL_GATE = 7
COL_BA_128 = 32


def _cp(sem=None):
    if sem is None:
        return pltpu.CompilerParams(vmem_limit_bytes=VMEM_LIMIT)
    return pltpu.CompilerParams(dimension_semantics=sem, vmem_limit_bytes=VMEM_LIMIT)


def _sds(shape, dtype=F32):
    return jax.ShapeDtypeStruct(shape, dtype)


def _mm(a, b):
    return jnp.dot(a.astype(BF16), b.astype(BF16), preferred_element_type=F32)


def _mm_nt(a, b):
    return lax.dot_general(a.astype(BF16), b.astype(BF16), (((1,), (1,)), ((), ())),
                           preferred_element_type=F32)


def _mm_tn(a, b):
    return lax.dot_general(a.astype(BF16), b.astype(BF16), (((0,), (0,)), ((), ())),
                           preferred_element_type=F32)


def _mmx(a, b):
    return jnp.dot(a, b, precision=HI, preferred_element_type=F32)


def _mmx_nt(a, b):
    return lax.dot_general(a, b, (((1,), (1,)), ((), ())), precision=HI, preferred_element_type=F32)


def _mmx_tn(a, b):
    return lax.dot_general(a, b, (((0,), (0,)), ((), ())), precision=HI, preferred_element_type=F32)


def _dot(a, b):
    return jnp.dot(a, b, preferred_element_type=F32)


def _dot_nt(a, b):
    return lax.dot_general(a, b, (((1,), (1,)), ((), ())), preferred_element_type=F32)


def _dot_tn(a, b):
    return lax.dot_general(a, b, (((0,), (0,)), ((), ())), preferred_element_type=F32)


def _split(a):
    hi = a.astype(BF16)
    return hi, (a - hi.astype(F32)).astype(BF16)


def _mm3(a_s, b_s):
    return _dot(a_s[0], b_s[0]) + _dot(a_s[0], b_s[1]) + _dot(a_s[1], b_s[0])


def _mm3_tn(a_s, b_s):
    return _dot_tn(a_s[0], b_s[0]) + _dot_tn(a_s[0], b_s[1]) + _dot_tn(a_s[1], b_s[0])


def _interleave(gens):
    live = list(gens)
    while live:
        nxt = []
        for g in live:
            try:
                next(g)
                nxt.append(g)
            except StopIteration:
                pass
        live = nxt


def _segsum(x, bd):
    hi = x.astype(BF16)
    r1 = x - hi.astype(F32)
    mid = r1.astype(BF16)
    lo = (r1 - mid.astype(F32)).astype(BF16)
    return (jnp.dot(hi, bd, preferred_element_type=F32) + jnp.dot(mid, bd, preferred_element_type=F32)
            + jnp.dot(lo, bd, preferred_element_type=F32))


def _sigmoid(x):
    return 1.0 / (1.0 + jnp.exp(-x))


def _silu_grad(x, s):
    return s * (1.0 + x * (1.0 - s))


def _block_ones(n, seg):
    i = np.arange(n)
    return jnp.asarray((i[:, None] // seg == i[None, :] // seg).astype(np.float32), dtype=BF16)


def _tri_incl():
    i = np.arange(CHUNK)
    return jnp.asarray((i[:, None] >= i[None, :]).astype(np.float32))


def _t5_bucket(dist):
    max_exact = N_BUCKETS // 2
    d = np.maximum(dist, 1).astype(np.float64)
    large = max_exact + (np.log(d / max_exact) / math.log(MAX_DISTANCE / max_exact)
                         * (N_BUCKETS - max_exact)).astype(np.int32)
    large = np.minimum(large, N_BUCKETS - 1)
    return np.where(dist < max_exact, dist, large).astype(np.int32)


def _bucket_tables():
    qi = np.arange(BLK)[:, None]
    kj = np.arange(2 * BLK)[None, :]
    step = qi - kj + BLK
    return jnp.asarray(np.stack([_t5_bucket(np.clip(step, 0, None) * r) for _, r in PATTERNS]))


def _in_proj(x, norm_w, wt_bf):
    s = x.shape[0]
    tm = 256

    def body(x_ref, nw_ref, w_ref, o_ref):
        xv = x_ref[...]
        rstd = lax.rsqrt(jnp.mean(xv * xv, axis=-1, keepdims=True) + EPS)
        h = (xv * rstd * nw_ref[...]).astype(BF16)
        o_ref[...] = _dot_nt(h, w_ref[...])

    return pl.pallas_call(
        body, grid=(s // tm,),
        in_specs=[pl.BlockSpec((tm, D_MODEL), lambda i: (i, 0)),
                  pl.BlockSpec((1, D_MODEL), lambda i: (0, 0)),
                  pl.BlockSpec((D_IN_PAD, D_MODEL), lambda i: (0, 0))],
        out_specs=pl.BlockSpec((tm, D_IN_PAD), lambda i: (i, 0)),
        out_shape=_sds((s, D_IN_PAD)), compiler_params=_cp(("parallel",)), name="in_proj")(x, norm_w, wt_bf)


def _conv_group(cur, halo, w_ref, c):
    rows = cur.shape[0]
    lanes = slice(128 * c, 128 * c + 128)
    xcat = jnp.concatenate([halo, cur], axis=0)
    y = cur * w_ref[CONV_W - 1:CONV_W, lanes]
    for k in range(1, CONV_W):
        sh = pltpu.roll(xcat, k, 0)[8:8 + rows]
        y = y + sh * w_ref[CONV_W - 1 - k:CONV_W - k, lanes]
    return y


def _beta_g(ba, alog_l, dtb_l):
    lane = lax.broadcasted_iota(jnp.int32, ba.shape, 1)
    sig_b = _sigmoid(ba)
    t = ba + dtb_l
    softplus = jnp.maximum(t, 0.0) + jnp.log(1.0 + jnp.exp(-jnp.abs(t)))
    nega = -jnp.exp(alog_l)
    g = nega * softplus
    out = jnp.where(lane < DN_HEADS, sig_b, jnp.where(lane < 2 * DN_HEADS, g, 0.0))
    return out, lane, sig_b, t, nega, g


def _dn_pre(proj, conv_w8, alog_l, dtb_l):
    s = proj.shape[0]
    tr = 512
    nh = tr // 8

    def body(u_ref, halo_ref, ba_ref, w_ref, al_ref, dt_ref, q_ref, k_ref, v_ref, bg_ref):
        i = pl.program_id(0)
        keep = (i > 0).astype(F32)
        for c in range(12):
            lanes = slice(128 * c, 128 * c + 128)
            y = _conv_group(u_ref[:, lanes], halo_ref[:, lanes] * keep, w_ref, c)
            sv = y * _sigmoid(y)
            if c < 8:
                rs = lax.rsqrt(jnp.sum(sv * sv, axis=1, keepdims=True) + EPS)
                n = sv * rs
                if c < 4:
                    q_ref[:, lanes] = n * (DN_HD ** -0.5)
                else:
                    k_ref[:, slice(128 * (c - 4), 128 * (c - 3))] = n
            else:
                v_ref[:, slice(128 * (c - 8), 128 * (c - 7))] = sv
        bg_ref[...] = _beta_g(ba_ref[...], al_ref[...], dt_ref[...])[0]

    return pl.pallas_call(
        body, grid=(s // tr,),
        in_specs=[pl.BlockSpec((tr, 1536), lambda i: (i, 0)),
                  pl.BlockSpec((8, 1536), lambda i: (jnp.maximum(i * nh - 1, 0), 0)),
                  pl.BlockSpec((tr, 128), lambda i: (i, COL_BA_128)),
                  pl.BlockSpec((8, 1536), lambda i: (0, 0)),
                  pl.BlockSpec((1, 128), lambda i: (0, 0)),
                  pl.BlockSpec((1, 128), lambda i: (0, 0))],
        out_specs=[pl.BlockSpec((tr, 512), lambda i: (i, 0))] * 3 + [pl.BlockSpec((tr, 128), lambda i: (i, 0))],
        out_shape=[_sds((s, 512))] * 3 + [_sds((s, 128))],
        compiler_params=_cp(("parallel",)), name="dn_pre")(proj, proj, proj, conv_w8, alog_l, dtb_l)


CPS = 2


def _chunk_cumsum(bg, bgt, tri):
    return _mmx(tri, bg), _mmx_nt(bgt, tri)


def _chunk_common(bg, gc_col, gc_row, h):
    gcc = gc_col[:, DN_HEADS + h:DN_HEADS + h + 1]
    gcr = gc_row[DN_HEADS + h:DN_HEADS + h + 1, :]
    beta = bg[:, h:h + 1]
    ii = lax.broadcasted_iota(jnp.int32, (CHUNK, CHUNK), 0)
    jj = lax.broadcasted_iota(jnp.int32, (CHUNK, CHUNK), 1)
    incl = ii >= jj
    strict = ii > jj
    decay = jnp.exp(jnp.where(incl, gcc - gcr, NEG))
    gl = gcc[CHUNK - 1:CHUNK, :]
    return gcc, beta, incl, strict, decay, gl


def _dn_prep(qn, kn, v, bg, bgt, tri):
    s = qn.shape[0]
    nc = s // CHUNK

    def body(q_ref, k_ref, v_ref, bg_ref, bgt_ref, tri_ref,
             u_ref, w_ref, qd_ref, kt_ref, attn_ref, t_ref, gl_ref):
        tri_v = tri_ref[...]
        ii = lax.broadcasted_iota(jnp.int32, (CHUNK, CHUNK), 0)
        jj = lax.broadcasted_iota(jnp.int32, (CHUNK, CHUNK), 1)
        eye = (ii == jj).astype(F32)

        def chain(cc, h, bgv, gc_col, gc_row):
            rows = slice(CHUNK * cc, CHUNK * cc + CHUNK)
            lanes = slice(128 * h, 128 * h + 128)
            gcc, beta, incl, strict, decay, gl = _chunk_common(bgv, gc_col, gc_row, h)
            q = q_ref[rows, lanes]
            k = k_ref[rows, lanes]
            vv = v_ref[rows, lanes]
            kb = k * beta
            egc = jnp.exp(gcc)
            a_mat = jnp.where(strict, _mm_nt(kb, k) * decay, 0.0)
            attn_ref[cc, h] = jnp.where(incl, _mm_nt(q, k) * decay, 0.0)
            qd_ref[rows, lanes] = q * egc
            kt_ref[rows, lanes] = k * jnp.exp(gl - gcc)
            gl_ref[cc, h] = jnp.broadcast_to(jnp.exp(gl), (1, 128))
            yield
            p = -a_mat
            t = eye + p
            for _ in range(5):
                ps = _split(p)
                p = _mm3(ps, ps)
                yield
                t = t + _mm3(_split(t), _split(p))
                yield
            t_ref[cc, h] = t
            ts = _split(t)
            u_ref[rows, lanes] = _mm3(ts, _split(vv * beta))
            w_ref[rows, lanes] = _mm3(ts, _split(kb * egc))

        gens = []
        for cc in range(CPS):
            bgv = bg_ref[CHUNK * cc:CHUNK * cc + CHUNK, :]
            gc_col, gc_row = _chunk_cumsum(bgv, bgt_ref[cc], tri_v)
            gens += [chain(cc, h, bgv, gc_col, gc_row) for h in range(DN_HEADS)]
        _interleave(gens)

    rows_step = CPS * CHUNK
    big = pl.BlockSpec((rows_step, 512), lambda n: (n, 0))
    sq = pl.BlockSpec((CPS, DN_HEADS, CHUNK, CHUNK), lambda n: (n, 0, 0, 0))
    return pl.pallas_call(
        body, grid=(nc // CPS,),
        in_specs=[big, big, big, pl.BlockSpec((rows_step, 128), lambda n: (n, 0)),
                  pl.BlockSpec((CPS, 8, CHUNK), lambda n: (n, 0, 0)),
                  pl.BlockSpec((CHUNK, CHUNK), lambda n: (0, 0))],
        out_specs=[big, big, big, big, sq, sq, pl.BlockSpec((CPS, DN_HEADS, 1, 128), lambda n: (n, 0, 0, 0))],
        out_shape=[_sds((s, 512))] * 4 + [_sds((nc, DN_HEADS, CHUNK, CHUNK))] * 2 + [_sds((nc, DN_HEADS, 1, 128))],
        compiler_params=_cp(("parallel",)), name="dn_prep")(qn, kn, v, bg, bgt, tri)


def _dn_scan(u, w, qd, kt, attn, gl):
    s = u.shape[0]
    nc = s // CHUNK

    def body(u_ref, w_ref, qd_ref, kt_ref, attn_ref, gl_ref, o_ref, vn_ref, sp_ref, st_ref):
        n = pl.program_id(0)

        @pl.when(n == 0)
        def _():
            st_ref[...] = jnp.zeros_like(st_ref)

        def chain(cc, h):
            rows = slice(CHUNK * cc, CHUNK * cc + CHUNK)
            lanes = slice(128 * h, 128 * h + 128)
            st = st_ref[h]
            sp_ref[cc, h] = st
            stb = st.astype(BF16)
            ws = _dot(w_ref[rows, lanes].astype(BF16), stb)
            qs = _dot(qd_ref[rows, lanes].astype(BF16), stb)
            yield
            vn = u_ref[rows, lanes] - ws
            vn_ref[rows, lanes] = vn
            vnb = vn.astype(BF16)
            o_ref[rows, lanes] = qs + _dot(attn_ref[cc, h].astype(BF16), vnb)
            st_ref[h] = st * gl_ref[cc, h] + _dot_tn(kt_ref[rows, lanes].astype(BF16), vnb)

        for cc in range(CPS):
            _interleave([chain(cc, h) for h in range(DN_HEADS)])

    big = pl.BlockSpec((CPS * CHUNK, 512), lambda n: (n, 0))
    return pl.pallas_call(
        body, grid=(nc // CPS,),
        in_specs=[big, big, big, big,
                  pl.BlockSpec((CPS, DN_HEADS, CHUNK, CHUNK), lambda n: (n, 0, 0, 0)),
                  pl.BlockSpec((CPS, DN_HEADS, 1, 128), lambda n: (n, 0, 0, 0))],
        out_specs=[big, big, pl.BlockSpec((CPS, DN_HEADS, DN_HD, DN_HD), lambda n: (n, 0, 0, 0))],
        out_shape=[_sds((s, 512)), _sds((s, 512)), _sds((nc, DN_HEADS, DN_HD, DN_HD))],
        scratch_shapes=[pltpu.VMEM((DN_HEADS, DN_HD, DN_HD), F32)],
        compiler_params=_cp(("arbitrary",)), name="dn_scan")(u, w, qd, kt, attn, gl)


R4 = PATTERNS[1][1]
R16 = PATTERNS[2][1]
TM = 256


def _pattern_spec(r):
    return pl.BlockSpec((r, TM // r, 512), lambda i: (0, i, 0))


def _pattern_shape(s, r):
    return _sds((r, s // r, 512))


SLABS = pltpu.VMEM((4, TM, 128), F32)


def _to_patterns(val, dsts, scr):
    for c in range(4):
        lanes = slice(128 * c, 128 * c + 128)
        scr[c] = val[:, lanes]
        for dst_ref, r in dsts:
            for a in range(r):
                dst_ref[a, :, lanes] = scr[c, pl.ds(a, TM // r, stride=r), :]


def _from_pattern(src_ref, r, scr):
    for c in range(4):
        for a in range(r):
            scr[c, pl.ds(a, TM // r, stride=r), :] = src_ref[a, :, 128 * c:128 * c + 128]
    return jnp.concatenate([scr[c] for c in range(4)], axis=1)


def _att_pre(proj, qw_t, kw_t, bd64):
    s = proj.shape[0]

    def body(q_ref, k_ref, v_ref, qw_ref, kw_ref, bd_ref,
             q1_ref, k1_ref, q4_ref, k4_ref, v4_ref, q16_ref, k16_ref, v16_ref, scr):
        bd = bd_ref[...]
        q = q_ref[...]
        k = k_ref[...]
        qn = q * lax.rsqrt(_segsum(q * q, bd) * (1.0 / ATT_HD) + EPS) * qw_ref[...] * (ATT_HD ** -0.5)
        kn = k * lax.rsqrt(_segsum(k * k, bd) * (1.0 / ATT_HD) + EPS) * kw_ref[...]
        q1_ref[...] = qn
        k1_ref[...] = kn
        _to_patterns(qn, ((q4_ref, R4), (q16_ref, R16)), scr)
        _to_patterns(kn, ((k4_ref, R4), (k16_ref, R16)), scr)
        _to_patterns(v_ref[...], ((v4_ref, R4), (v16_ref, R16)), scr)

    row = pl.BlockSpec((1, 512), lambda i: (0, 0))
    tok = pl.BlockSpec((TM, 512), lambda i: (i, 0))
    return pl.pallas_call(
        body, grid=(s // TM,),
        in_specs=[pl.BlockSpec((TM, 512), lambda i: (i, COL_ATT_Q)),
                  pl.BlockSpec((TM, 512), lambda i: (i, COL_ATT_K)),
                  pl.BlockSpec((TM, 512), lambda i: (i, COL_ATT_V)),
                  row, row, pl.BlockSpec((512, 512), lambda i: (0, 0))],
        out_specs=[tok, tok] + [_pattern_spec(R4)] * 3 + [_pattern_spec(R16)] * 3,
        out_shape=[_sds((s, 512))] * 2 + [_pattern_shape(s, R4)] * 3 + [_pattern_shape(s, R16)] * 3,
        scratch_shapes=[SLABS],
        compiler_params=_cp(("parallel",)), name="att_pre")(proj, proj, proj, qw_t, kw_t, bd64)


def _bias_fwd(rel_bias, buckets):
    def body(rb_ref, bk_ref, o_ref):
        for p in range(len(PATTERNS)):
            bk = bk_ref[p]
            for h in range(ATT_HEADS):
                acc = jnp.zeros((BLK, 2 * BLK), F32)
                for b in range(N_BUCKETS):
                    acc = jnp.where(bk == b, rb_ref[h, b], acc)
                o_ref[p, h] = acc

    return pl.pallas_call(
        body,
        in_specs=[pl.BlockSpec(memory_space=pltpu.SMEM), pl.BlockSpec(memory_space=pltpu.VMEM)],
        out_specs=pl.BlockSpec(memory_space=pltpu.VMEM),
        out_shape=_sds((len(PATTERNS), ATT_HEADS, BLK, 2 * BLK)),
        compiler_params=_cp(), name="bias_fwd")(rel_bias, buckets)


def _bias_bwd(ds_accs, buckets):
    def body(ds0_ref, ds1_ref, ds2_ref, bk_ref, o_ref):
        for h in range(ATT_HEADS):
            for b in range(N_BUCKETS):
                tot = jnp.zeros((), F32)
                for p, ds_ref in enumerate((ds0_ref, ds1_ref, ds2_ref)):
                    tot = tot + jnp.sum(jnp.where(bk_ref[p] == b, ds_ref[h], 0.0))
                o_ref[h, b] = tot

    return pl.pallas_call(
        body,
        in_specs=[pl.BlockSpec(memory_space=pltpu.VMEM)] * 4,
        out_specs=pl.BlockSpec(memory_space=pltpu.SMEM),
        out_shape=_sds((ATT_HEADS, N_BUCKETS)),
        compiler_params=_cp(), name="bias_bwd")(*ds_accs, buckets)


def _att_masks(has_prev):
    qi = lax.broadcasted_iota(jnp.int32, (BLK, BLK), 0)
    kj = lax.broadcasted_iota(jnp.int32, (BLK, BLK), 1)
    lane = lax.broadcasted_iota(jnp.int32, (BLK, 2 * ATT_HD), 1)
    return jnp.logical_and(kj >= qi, has_prev), kj <= qi, lane < ATT_HD


def _head_lanes(h):
    half = h % 2
    return slice(ATT_HD * h, ATT_HD * h + ATT_HD), slice(ATT_HD * half, ATT_HD * half + ATT_HD)


def _att_scores(qm, kp2, kc2, bias_h, mask_prev, mask_cur):
    s_prev = jnp.where(mask_prev, _dot_nt(qm, kp2) + bias_h[:, :BLK], NEG)
    s_cur = jnp.where(mask_cur, _dot_nt(qm, kc2) + bias_h[:, BLK:], NEG)
    return s_prev, s_cur


def _att_fwd(q, k, v, v_col, bias, p_idx, r, name):
    s = q.shape[0]
    nblk = s // BLK
    nseq = nblk // r

    def body(q_ref, kp_ref, kc_ref, vp_ref, vc_ref, b_ref, o_ref, lse_ref):
        j = pl.program_id(0)
        mask_prev, mask_cur, lo_half = _att_masks((j % nseq) != 0)

        def head(h, q2, kp2, kc2, vp2, vc2):
            out_l, pair_l = _head_lanes(h)
            sel = lo_half if h % 2 == 0 else jnp.logical_not(lo_half)
            qm = jnp.where(sel, q2, 0.0).astype(BF16)
            s_prev, s_cur = _att_scores(qm, kp2, kc2, b_ref[0, h], mask_prev, mask_cur)
            yield
            m = jnp.maximum(jnp.max(s_prev, axis=1, keepdims=True), jnp.max(s_cur, axis=1, keepdims=True))
            p_prev = jnp.exp(s_prev - m)
            p_cur = jnp.exp(s_cur - m)
            l = jnp.sum(p_prev, axis=1, keepdims=True) + jnp.sum(p_cur, axis=1, keepdims=True)
            yield
            o2 = _dot(p_prev.astype(BF16), vp2) + _dot(p_cur.astype(BF16), vc2)
            o_ref[:, out_l] = (o2 * (1.0 / l))[:, pair_l]
            lse_ref[:, out_l] = jnp.broadcast_to(m + jnp.log(l), (BLK, ATT_HD))

        gens = []
        for pp in range(ATT_HEADS // 2):
            lanes = slice(128 * pp, 128 * pp + 128)
            slabs = (q_ref[:, lanes], kp_ref[:, lanes].astype(BF16), kc_ref[:, lanes].astype(BF16),
                     vp_ref[:, lanes].astype(BF16), vc_ref[:, lanes].astype(BF16))
            gens += [head(2 * pp, *slabs), head(2 * pp + 1, *slabs)]
        _interleave(gens)

    cur = pl.BlockSpec((BLK, 512), lambda j: (j, 0))
    prev = pl.BlockSpec((BLK, 512), lambda j: (jnp.maximum(j - 1, 0), 0))
    vcur = pl.BlockSpec((BLK, 512), lambda j: (j, v_col))
    vprev = pl.BlockSpec((BLK, 512), lambda j: (jnp.maximum(j - 1, 0), v_col))
    return pl.pallas_call(
        body, grid=(nblk,),
        in_specs=[cur, prev, cur, vprev, vcur,
                  pl.BlockSpec((1, ATT_HEADS, BLK, 2 * BLK), lambda j: (p_idx, 0, 0, 0))],
        out_specs=[cur, cur],
        out_shape=[_sds((s, 512))] * 2,
        compiler_params=_cp(("parallel",)), name=name)(q, k, k, v, v, bias)


def _post_fwd(o_dn, proj, o_pats, lse_pats, dnw_t, bd128):
    s = o_dn.shape[0]

    def body(o_ref, z_ref, gate_ref, o1_ref, o4_ref, o16_ref, s1_ref, s4_ref, s16_ref, wn_ref, bd_ref,
             mixed_ref, oatt_ref, l1_ref, l4_ref, l16_ref, scr_a, scr_b, scr_c, scr_d):
        o = o_ref[...]
        z = z_ref[...]
        rstd = lax.rsqrt(_segsum(o * o, bd_ref[...]) * (1.0 / DN_HD) + EPS)
        y_dn = o * rstd * wn_ref[...] * (z * _sigmoid(z))
        mixed_ref[:, 0:512] = y_dn.astype(BF16)
        lses = (s1_ref[...], _from_pattern(s4_ref, R4, scr_a), _from_pattern(s16_ref, R16, scr_b))
        outs = (o1_ref[...], _from_pattern(o4_ref, R4, scr_c), _from_pattern(o16_ref, R16, scr_d))
        m = jnp.maximum(jnp.maximum(lses[0], lses[1]), lses[2])
        tot = jnp.exp(lses[0] - m) + jnp.exp(lses[1] - m) + jnp.exp(lses[2] - m)
        big_l = m + jnp.log(tot)
        acc = jnp.zeros_like(m)
        for lse_p, o_p in zip(lses, outs):
            acc = acc + jnp.exp(lse_p - big_l) * o_p
        gate = gate_ref[...]
        oatt_ref[...] = acc
        l1_ref[...] = big_l
        mixed_ref[:, 512:1024] = (acc * (gate * _sigmoid(gate))).astype(BF16)
        _to_patterns(big_l, ((l4_ref, R4), (l16_ref, R16)), scr_a)

    blk = pl.BlockSpec((TM, 512), lambda i: (i, 0))
    p4, p16 = _pattern_spec(R4), _pattern_spec(R16)
    return pl.pallas_call(
        body, grid=(s // TM,),
        in_specs=[blk, pl.BlockSpec((TM, 512), lambda i: (i, COL_Z)),
                  pl.BlockSpec((TM, 512), lambda i: (i, COL_GATE)), blk, p4, p16, blk, p4, p16,
                  pl.BlockSpec((1, 512), lambda i: (0, 0)), pl.BlockSpec((512, 512), lambda i: (0, 0))],
        out_specs=[pl.BlockSpec((TM, D_MODEL), lambda i: (i, 0)), blk, blk, p4, p16],
        out_shape=[_sds((s, D_MODEL), BF16), _sds((s, 512)), _sds((s, 512)), _pattern_shape(s, R4),
                   _pattern_shape(s, R16)],
        scratch_shapes=[SLABS] * 4,
        compiler_params=_cp(("parallel",)), name="post_fwd")(o_dn, proj, proj, *o_pats, *lse_pats, dnw_t, bd128)


def _out_fwd(x, mixed, w_out_bf, tgt):
    s = x.shape[0]
    tm = 256

    def body(x_ref, m_ref, w_ref, t_ref, dy_ref, loss_ref):
        i = pl.program_id(0)

        @pl.when(i == 0)
        def _():
            loss_ref[...] = jnp.zeros_like(loss_ref)

        y = x_ref[...] + jnp.dot(m_ref[...], w_ref[...], preferred_element_type=F32)
        err = y - t_ref[...]
        dy_ref[...] = err * (1.0 / D_MODEL)
        part = 0.5 * jnp.sum(jnp.mean(err * err, axis=-1, keepdims=True), axis=0, keepdims=True)
        loss_ref[...] = loss_ref[...] + part

    blk = pl.BlockSpec((tm, D_MODEL), lambda i: (i, 0))
    return pl.pallas_call(
        body, grid=(s // tm,),
        in_specs=[blk, blk, pl.BlockSpec((D_MODEL, D_MODEL), lambda i: (0, 0)), blk],
        out_specs=[blk, pl.BlockSpec((8, 128), lambda i: (0, 0))],
        out_shape=[_sds((s, D_MODEL)), _sds((8, 128))],
        compiler_params=_cp(("arbitrary",)), name="out_fwd")(x, mixed, w_out_bf, tgt)


def _out_bwd(dy, mixed, w_out_bf):
    s = dy.shape[0]
    tm = 256

    def body(dy_ref, m_ref, w_ref, dm_ref, dw_ref):
        i = pl.program_id(0)

        @pl.when(i == 0)
        def _():
            dw_ref[...] = jnp.zeros_like(dw_ref)

        dyb = dy_ref[...].astype(BF16)
        dm_ref[...] = lax.dot_general(dyb, w_ref[...], (((1,), (1,)), ((), ())), preferred_element_type=F32)
        dw_ref[...] = dw_ref[...] + lax.dot_general(m_ref[...], dyb, (((0,), (0,)), ((), ())),
                                                    preferred_element_type=F32)

    blk = pl.BlockSpec((tm, D_MODEL), lambda i: (i, 0))
    full = pl.BlockSpec((D_MODEL, D_MODEL), lambda i: (0, 0))
    return pl.pallas_call(
        body, grid=(s // tm,), in_specs=[blk, blk, full], out_specs=[blk, full],
        out_shape=[_sds((s, D_MODEL)), _sds((D_MODEL, D_MODEL))],
        compiler_params=_cp(("arbitrary",)), name="out_bwd")(dy, mixed, w_out_bf)


def _post_bwd(dmixed, o_dn, proj, o_att, dnw_t, bd128, bd64):
    s = o_dn.shape[0]
    tm = 256

    def body(ddn_ref, datt_ref, o_ref, z_ref, gate_ref, oatt_ref, wn_ref, bd128_ref, bd64_ref,
             do_ref, dz_ref, dgate_ref, doatt_ref, do4_ref, do16_ref, delta_ref, dl4_ref, dl16_ref, dnw_ref, scr):
        i = pl.program_id(0)

        @pl.when(i == 0)
        def _():
            dnw_ref[...] = jnp.zeros_like(dnw_ref)

        bd128v = bd128_ref[...]
        o = o_ref[...]
        z = z_ref[...]
        wn = wn_ref[...]
        dy = ddn_ref[...]
        rstd = lax.rsqrt(_segsum(o * o, bd128v) * (1.0 / DN_HD) + EPS)
        nrm = o * rstd
        sz = _sigmoid(z)
        dz_ref[...] = dy * nrm * wn * _silu_grad(z, sz)
        dn = dy * z * sz
        gw = dn * wn
        do_ref[...] = rstd * (gw - nrm * (_segsum(gw * nrm, bd128v) * (1.0 / DN_HD)))
        colsum = jnp.sum(dn * nrm, axis=0, keepdims=True)
        fold = colsum[:, 0:128] + colsum[:, 128:256] + colsum[:, 256:384] + colsum[:, 384:512]
        dnw_ref[...] = dnw_ref[...] + fold
        dya = datt_ref[...]
        gate = gate_ref[...]
        oatt = oatt_ref[...]
        sg = _sigmoid(gate)
        dgate_ref[...] = dya * oatt * _silu_grad(gate, sg)
        doa = dya * gate * sg
        doatt_ref[...] = doa
        delta = _segsum(doa * oatt, bd64_ref[...])
        delta_ref[...] = delta
        _to_patterns(doa, ((do4_ref, R4), (do16_ref, R16)), scr)
        _to_patterns(delta, ((dl4_ref, R4), (dl16_ref, R16)), scr)

    blk = pl.BlockSpec((tm, 512), lambda i: (i, 0))
    cst = pl.BlockSpec((512, 512), lambda i: (0, 0))
    p4, p16 = _pattern_spec(R4), _pattern_spec(R16)
    return pl.pallas_call(
        body, grid=(s // tm,),
        in_specs=[blk, pl.BlockSpec((tm, 512), lambda i: (i, 1)), blk,
                  pl.BlockSpec((tm, 512), lambda i: (i, COL_Z)), pl.BlockSpec((tm, 512), lambda i: (i, COL_GATE)),
                  blk, pl.BlockSpec((1, 512), lambda i: (0, 0)), cst, cst],
        out_specs=[blk, blk, blk, blk, p4, p16, blk, p4, p16, pl.BlockSpec((8, 128), lambda i: (0, 0))],
        out_shape=[_sds((s, 512))] * 4 + [_pattern_shape(s, R4), _pattern_shape(s, R16), _sds((s, 512)),
                                          _pattern_shape(s, R4), _pattern_shape(s, R16), _sds((8, 128))],
        scratch_shapes=[SLABS],
        compiler_params=_cp(("arbitrary",)), name="post_bwd")(dmixed, dmixed, o_dn, proj, proj, o_att, dnw_t,
                                                               bd128, bd64)


def _att_bwd(q, k, v, v_col, do, big_l, delta, bias, p_idx, r, name):
    s = q.shape[0]
    nblk = s // BLK
    nseq = nblk // r

    def body(q_ref, kp_ref, kc_ref, vp_ref, vc_ref, do_ref, l_ref, dl_ref, b_ref,
             dq_ref, dk_ref, dv_ref, ds_ref, dkc_ref, dvc_ref):
        j = pl.program_id(0)
        mask_prev, mask_cur, lo_half = _att_masks((j % nseq) != 0)

        @pl.when(j == 0)
        def _():
            dkc_ref[...] = jnp.zeros_like(dkc_ref)
            dvc_ref[...] = jnp.zeros_like(dvc_ref)
            ds_ref[...] = jnp.zeros_like(ds_ref)

        @pl.when(j < nblk)
        def _():
            def head(h, q2, do2, kp2, kc2, vp2, vc2):
                out_l, pair_l = _head_lanes(h)
                sel = lo_half if h % 2 == 0 else jnp.logical_not(lo_half)
                qm = jnp.where(sel, q2, 0.0).astype(BF16)
                dom = jnp.where(sel, do2, 0.0).astype(BF16)
                s_prev, s_cur = _att_scores(qm, kp2, kc2, b_ref[0, h], mask_prev, mask_cur)
                dp_prev = _dot_nt(dom, vp2)
                dp_cur = _dot_nt(dom, vc2)
                yield
                lh = l_ref[:, ATT_HD * h:ATT_HD * h + 1]
                dh = dl_ref[:, ATT_HD * h:ATT_HD * h + 1]
                p_prev = jnp.exp(s_prev - lh)
                p_cur = jnp.exp(s_cur - lh)
                ds_prev = p_prev * (dp_prev - dh)
                ds_cur = p_cur * (dp_cur - dh)
                ds_ref[h, :, 0:BLK] = ds_ref[h, :, 0:BLK] + ds_prev
                ds_ref[h, :, BLK:2 * BLK] = ds_ref[h, :, BLK:2 * BLK] + ds_cur
                dsb_prev, dsb_cur = ds_prev.astype(BF16), ds_cur.astype(BF16)
                pb_prev, pb_cur = p_prev.astype(BF16), p_cur.astype(BF16)
                yield
                qb, dob = q2.astype(BF16), do2.astype(BF16)
                dq_ref[:, out_l] = (_dot(dsb_prev, kp2) + _dot(dsb_cur, kc2))[:, pair_l]
                dk_ref[:, out_l] = dkc_ref[:, out_l] + _dot_tn(dsb_prev, qb)[:, pair_l]
                dv_ref[:, out_l] = dvc_ref[:, out_l] + _dot_tn(pb_prev, dob)[:, pair_l]
                yield
                dkc_ref[:, out_l] = _dot_tn(dsb_cur, qb)[:, pair_l]
                dvc_ref[:, out_l] = _dot_tn(pb_cur, dob)[:, pair_l]

            gens = []
            for pp in range(ATT_HEADS // 2):
                lanes = slice(128 * pp, 128 * pp + 128)
                slabs = (q_ref[:, lanes], do_ref[:, lanes], kp_ref[:, lanes].astype(BF16),
                         kc_ref[:, lanes].astype(BF16), vp_ref[:, lanes].astype(BF16),
                         vc_ref[:, lanes].astype(BF16))
                gens += [head(2 * pp, *slabs), head(2 * pp + 1, *slabs)]
            _interleave(gens)

        @pl.when(j == nblk)
        def _():
            dk_ref[...] = dkc_ref[...]
            dv_ref[...] = dvc_ref[...]

    last = nblk - 1
    cur = pl.BlockSpec((BLK, 512), lambda j: (jnp.minimum(j, last), 0))
    prev = pl.BlockSpec((BLK, 512), lambda j: (jnp.clip(j - 1, 0, last), 0))
    vcur = pl.BlockSpec((BLK, 512), lambda j: (jnp.minimum(j, last), v_col))
    vprev = pl.BlockSpec((BLK, 512), lambda j: (jnp.clip(j - 1, 0, last), v_col))
    return pl.pallas_call(
        body, grid=(nblk + 1,),
        in_specs=[cur, prev, cur, vprev, vcur, cur, cur, cur,
                  pl.BlockSpec((1, ATT_HEADS, BLK, 2 * BLK), lambda j: (p_idx, 0, 0, 0))],
        out_specs=[cur, prev, prev, pl.BlockSpec((ATT_HEADS, BLK, 2 * BLK), lambda j: (0, 0, 0))],
        out_shape=[_sds((s, 512))] * 3 + [_sds((ATT_HEADS, BLK, 2 * BLK))],
        scratch_shapes=[pltpu.VMEM((BLK, 512), F32), pltpu.VMEM((BLK, 512), F32)],
        compiler_params=_cp(("arbitrary",)), name=name)(q, k, k, v, v, do, big_l, delta, bias)


def _att_pre_bwd(dq_pats, dk_pats, dv_pats, proj, qw_t, kw_t, bd64):
    s = proj.shape[0]
    tm = TM

    def body(dq1_ref, dq4_ref, dq16_ref, dk1_ref, dk4_ref, dk16_ref, dv1_ref, dv4_ref, dv16_ref,
             q_ref, k_ref, qw_ref, kw_ref, bd_ref,
             dqr_ref, dkr_ref, dvr_ref, dqw_ref, dkw_ref, scr4, scr16):
        i = pl.program_id(0)

        @pl.when(i == 0)
        def _():
            dqw_ref[...] = jnp.zeros_like(dqw_ref)
            dkw_ref[...] = jnp.zeros_like(dkw_ref)

        bd = bd_ref[...]

        def total(d1_ref, d4_ref, d16_ref):
            return d1_ref[...] + _from_pattern(d4_ref, R4, scr4) + _from_pattern(d16_ref, R16, scr16)

        def one(d_refs, x_ref, w_ref, scale, dx_ref, dw_ref):
            dy = total(*d_refs) * scale
            x = x_ref[...]
            rstd = lax.rsqrt(_segsum(x * x, bd) * (1.0 / ATT_HD) + EPS)
            nrm = x * rstd
            dw_ref[...] = dw_ref[...] + jnp.sum(dy * nrm, axis=0, keepdims=True)
            g = dy * w_ref[...]
            dx_ref[...] = rstd * (g - nrm * (_segsum(g * nrm, bd) * (1.0 / ATT_HD)))

        one((dq1_ref, dq4_ref, dq16_ref), q_ref, qw_ref, ATT_HD ** -0.5, dqr_ref, dqw_ref)
        one((dk1_ref, dk4_ref, dk16_ref), k_ref, kw_ref, 1.0, dkr_ref, dkw_ref)
        dvr_ref[...] = total(dv1_ref, dv4_ref, dv16_ref)

    blk = pl.BlockSpec((tm, 512), lambda i: (i, 0))
    pats = [blk, _pattern_spec(R4), _pattern_spec(R16)]
    row = pl.BlockSpec((1, 512), lambda i: (0, 0))
    acc = pl.BlockSpec((8, 512), lambda i: (0, 0))
    return pl.pallas_call(
        body, grid=(s // tm,),
        in_specs=pats * 3 + [pl.BlockSpec((tm, 512), lambda i: (i, COL_ATT_Q)),
                             pl.BlockSpec((tm, 512), lambda i: (i, COL_ATT_K)), row, row,
                             pl.BlockSpec((512, 512), lambda i: (0, 0))],
        out_specs=[blk, blk, blk, acc, acc],
        out_shape=[_sds((s, 512))] * 3 + [_sds((8, 512))] * 2,
        scratch_shapes=[SLABS] * 2,
        compiler_params=_cp(("arbitrary",)), name="att_pre_bwd")(*dq_pats, *dk_pats, *dv_pats, proj, proj,
                                                                  qw_t, kw_t, bd64)


def _dn_scan_bwd(do, sp, qd, kt, w, vn, attn, gl):
    s = do.shape[0]
    nc = s // CHUNK

    def body(do_ref, sp_ref, qd_ref, kt_ref, w_ref, vn_ref, attn_ref, gl_ref,
             du_ref, dqd_ref, dkt_ref, dw_ref, dattn_ref, dgl_ref, ds_ref):
        n = pl.program_id(0)

        @pl.when(n == 0)
        def _():
            ds_ref[...] = jnp.zeros_like(ds_ref)

        def chain(cc, h):
            rows = slice(CHUNK * cc, CHUNK * cc + CHUNK)
            lanes = slice(128 * h, 128 * h + 128)
            dsn = ds_ref[h]
            st = sp_ref[cc, h]
            dsb, stb = dsn.astype(BF16), st.astype(BF16)
            dob = do_ref[rows, lanes].astype(BF16)
            vnb = vn_ref[rows, lanes].astype(BF16)
            dvn = _dot_tn(attn_ref[cc, h].astype(BF16), dob) + _dot(kt_ref[rows, lanes].astype(BF16), dsb)
            du_ref[rows, lanes] = dvn
            dqd_ref[rows, lanes] = _dot_nt(dob, stb)
            dattn_ref[cc, h] = _dot_nt(dob, vnb)
            dkt_ref[rows, lanes] = _dot_nt(vnb, dsb)
            tot = jnp.sum(jnp.sum(st * dsn, axis=1, keepdims=True), axis=0, keepdims=True)
            dgl_ref[cc, h] = jnp.broadcast_to(tot, (1, 128))
            qdo = _dot_tn(qd_ref[rows, lanes].astype(BF16), dob)
            yield
            dvb = dvn.astype(BF16)
            dw_ref[rows, lanes] = -_dot_nt(dvb, stb)
            ds_ref[h] = qdo + dsn * gl_ref[cc, h] - _dot_tn(w_ref[rows, lanes].astype(BF16), dvb)

        for cc in reversed(range(CPS)):
            _interleave([chain(cc, h) for h in range(DN_HEADS)])

    nsteps = nc // CPS
    big = pl.BlockSpec((CPS * CHUNK, 512), lambda n: (nsteps - 1 - n, 0))
    sq = pl.BlockSpec((CPS, DN_HEADS, CHUNK, CHUNK), lambda n: (nsteps - 1 - n, 0, 0, 0))
    glb = pl.BlockSpec((CPS, DN_HEADS, 1, 128), lambda n: (nsteps - 1 - n, 0, 0, 0))
    return pl.pallas_call(
        body, grid=(nsteps,),
        in_specs=[big, pl.BlockSpec((CPS, DN_HEADS, DN_HD, DN_HD), lambda n: (nsteps - 1 - n, 0, 0, 0)),
                  big, big, big, big, sq, glb],
        out_specs=[big, big, big, big, sq, glb],
        out_shape=[_sds((s, 512))] * 4 + [_sds((nc, DN_HEADS, CHUNK, CHUNK)), _sds((nc, DN_HEADS, 1, 128))],
        scratch_shapes=[pltpu.VMEM((DN_HEADS, DN_HD, DN_HD), F32)],
        compiler_params=_cp(("arbitrary",)), name="dn_scan_bwd")(do, sp, qd, kt, w, vn, attn, gl)


def _dn_prep_bwd(qn, kn, v, bg, bgt, tri, t_inv, attn, u, w, du, dw, dqd, dkt, dattn, dgl):
    s = qn.shape[0]
    nc = s // CHUNK

    def body(q_ref, k_ref, v_ref, bg_ref, bgt_ref, tri_ref, t_ref, attn_ref, u_ref, w_ref,
             du_ref, dw_ref, dqd_ref, dkt_ref, dattn_ref, dgl_ref,
             dq_ref, dk_ref, dv_ref, dbg_ref):
        tri_v = tri_ref[...]
        lane = lax.broadcasted_iota(jnp.int32, (CHUNK, 128), 1)
        rowi = lax.broadcasted_iota(jnp.int32, (CHUNK, 1), 0)
        ones_b = jnp.ones((CHUNK, 128), BF16)
        parts = [[] for _ in range(CPS)]

        def chain(cc, h, bgv, gc_col, gc_row):
            rows = slice(CHUNK * cc, CHUNK * cc + CHUNK)
            lanes = slice(128 * h, 128 * h + 128)
            gcc, beta, incl, strict, decay, gl = _chunk_common(bgv, gc_col, gc_row, h)
            q = q_ref[rows, lanes]
            k = k_ref[rows, lanes]
            vv = v_ref[rows, lanes]
            ts = _split(t_ref[cc, h])
            egc = jnp.exp(gcc)
            kb = k * beta
            a_mat = jnp.where(strict, _mm_nt(kb, k) * decay, 0.0)
            dvb = _mm3_tn(ts, _split(du_ref[rows, lanes]))
            dkbg = _mm3_tn(ts, _split(dw_ref[rows, lanes]))
            yield
            d_a = jnp.where(strict, -(_mm_nt(dvb, u_ref[rows, lanes]) + _mm_nt(dkbg, w_ref[rows, lanes])), 0.0)
            d_m = d_a * decay
            dattn_m = jnp.where(incl, dattn_ref[cc, h], 0.0)
            dqk = dattn_m * decay
            e_hi, e_lo = _split(d_a * a_mat + dattn_m * attn_ref[cc, h])
            yield
            dkb = _mm(d_m, k)
            dk = _mm_tn(d_m, kb) + _mm_tn(dqk, q)
            dq = _mm(dqk, k)
            e_colsum = (_dot_tn(e_hi, ones_b) + _dot_tn(e_lo, ones_b))[:, 0:1]
            e_rowsum = jnp.sum(e_hi.astype(F32) + e_lo.astype(F32), axis=1, keepdims=True)
            yield
            dgc = e_rowsum - e_colsum
            dqd = dqd_ref[rows, lanes]
            dq = dq + dqd * egc
            dgc = dgc + jnp.sum(dqd * q, axis=1, keepdims=True) * egc
            tail = jnp.exp(gl - gcc)
            dkt = dkt_ref[rows, lanes]
            dk = dk + dkt * tail
            r = jnp.sum(dkt * k, axis=1, keepdims=True) * tail
            dgc = dgc - r
            dgl_tot = jnp.sum(r, axis=0, keepdims=True) + dgl_ref[cc, h][:, 0:1] * jnp.exp(gl)
            rk = jnp.sum(dkbg * k, axis=1, keepdims=True)
            dk = dk + dkbg * (beta * egc) + dkb * beta
            dbeta = rk * egc + jnp.sum(dkb * k, axis=1, keepdims=True) + jnp.sum(dvb * vv, axis=1, keepdims=True)
            dgc = dgc + rk * beta * egc
            dgc = dgc + jnp.where(rowi == CHUNK - 1, dgl_tot, 0.0)
            dq_ref[rows, lanes] = dq
            dk_ref[rows, lanes] = dk
            dv_ref[rows, lanes] = dvb * beta
            parts[cc].append((h, dgc, dbeta))

        gens = []
        for cc in range(CPS):
            bgv = bg_ref[CHUNK * cc:CHUNK * cc + CHUNK, :]
            gc_col, gc_row = _chunk_cumsum(bgv, bgt_ref[cc], tri_v)
            gens += [chain(cc, h, bgv, gc_col, gc_row) for h in range(DN_HEADS)]
        _interleave(gens)
        for cc in range(CPS):
            dgc_mat = jnp.zeros((CHUNK, 128), F32)
            dbeta_mat = jnp.zeros((CHUNK, 128), F32)
            for h, dgc, dbeta in parts[cc]:
                dgc_mat = dgc_mat + jnp.where(lane == DN_HEADS + h, dgc, 0.0)
                dbeta_mat = dbeta_mat + jnp.where(lane == h, dbeta, 0.0)
            dbg_ref[CHUNK * cc:CHUNK * cc + CHUNK, :] = _mmx_tn(tri_v, dgc_mat) + dbeta_mat

    big = pl.BlockSpec((CPS * CHUNK, 512), lambda n: (n, 0))
    sq = pl.BlockSpec((CPS, DN_HEADS, CHUNK, CHUNK), lambda n: (n, 0, 0, 0))
    glb = pl.BlockSpec((CPS, DN_HEADS, 1, 128), lambda n: (n, 0, 0, 0))
    small = pl.BlockSpec((CPS * CHUNK, 128), lambda n: (n, 0))
    return pl.pallas_call(
        body, grid=(nc // CPS,),
        in_specs=[big, big, big, small, pl.BlockSpec((CPS, 8, CHUNK), lambda n: (n, 0, 0)),
                  pl.BlockSpec((CHUNK, CHUNK), lambda n: (0, 0)), sq, sq, big, big,
                  big, big, big, big, sq, glb],
        out_specs=[big, big, big, small],
        out_shape=[_sds((s, 512))] * 3 + [_sds((s, 128))],
        compiler_params=_cp(("parallel",)), name="dn_prep_bwd")(qn, kn, v, bg, bgt, tri, t_inv, attn, u, w,
                                                                  du, dw, dqd, dkt, dattn, dgl)


def _dn_pre_bwd(dqn, dkn, dv, dbg, proj, conv_w8, alog_l, dtb_l):
    s = proj.shape[0]
    tr = 512
    nh = tr // 8

    def body(dq_ref, dk_ref, dv_ref, dbg_ref, u_ref, halo_ref, ba_ref, w_ref, al_ref, dt_ref,
             dy_ref, dba_ref, dsm_ref):
        i = pl.program_id(0)

        @pl.when(i == 0)
        def _():
            dsm_ref[...] = jnp.zeros_like(dsm_ref)

        keep = (i > 0).astype(F32)
        for c in range(12):
            lanes = slice(128 * c, 128 * c + 128)
            y = _conv_group(u_ref[:, lanes], halo_ref[:, lanes] * keep, w_ref, c)
            sg = _sigmoid(y)
            sv = y * sg
            if c < 8:
                rs = lax.rsqrt(jnp.sum(sv * sv, axis=1, keepdims=True) + EPS)
                n = sv * rs
                if c < 4:
                    dn = dq_ref[:, lanes] * (DN_HD ** -0.5)
                else:
                    dn = dk_ref[:, slice(128 * (c - 4), 128 * (c - 3))]
                dsv = rs * (dn - n * jnp.sum(dn * n, axis=1, keepdims=True))
            else:
                dsv = dv_ref[:, slice(128 * (c - 8), 128 * (c - 7))]
            dy_ref[:, lanes] = dsv * _silu_grad(y, sg)
        _, lane, sig_b, t, nega, g = _beta_g(ba_ref[...], al_ref[...], dt_ref[...])
        dbg = dbg_ref[...]
        da = dbg * nega * _sigmoid(t)
        is_b = lane < DN_HEADS
        is_a = jnp.logical_and(lane >= DN_HEADS, lane < 2 * DN_HEADS)
        dba_ref[...] = jnp.where(is_b, dbg * sig_b * (1.0 - sig_b), jnp.where(is_a, da, 0.0))
        d_alog = jnp.sum(jnp.where(is_a, dbg * g, 0.0), axis=0, keepdims=True)
        d_dtb = jnp.sum(jnp.where(is_a, da, 0.0), axis=0, keepdims=True)
        row = lax.broadcasted_iota(jnp.int32, (8, 128), 0)
        dsm_ref[...] = dsm_ref[...] + jnp.where(row == 0, d_alog, jnp.where(row == 1, d_dtb, 0.0))

    blk = pl.BlockSpec((tr, 512), lambda i: (i, 0))
    return pl.pallas_call(
        body, grid=(s // tr,),
        in_specs=[blk, blk, blk, pl.BlockSpec((tr, 128), lambda i: (i, 0)),
                  pl.BlockSpec((tr, 1536), lambda i: (i, 0)),
                  pl.BlockSpec((8, 1536), lambda i: (jnp.maximum(i * nh - 1, 0), 0)),
                  pl.BlockSpec((tr, 128), lambda i: (i, COL_BA_128)),
                  pl.BlockSpec((8, 1536), lambda i: (0, 0)),
                  pl.BlockSpec((1, 128), lambda i: (0, 0)), pl.BlockSpec((1, 128), lambda i: (0, 0))],
        out_specs=[pl.BlockSpec((tr, 1536), lambda i: (i, 0)), pl.BlockSpec((tr, 128), lambda i: (i, 0)),
                   pl.BlockSpec((8, 128), lambda i: (0, 0))],
        out_shape=[_sds((s, 1536)), _sds((s, 128)), _sds((8, 128))],
        compiler_params=_cp(("arbitrary",)), name="dn_pre_bwd")(dqn, dkn, dv, dbg, proj, proj, proj, conv_w8,
                                                                 alog_l, dtb_l)


def _conv_bwd(dy, proj, conv_w8):
    s = dy.shape[0]
    tr = 512
    nh = tr // 8
    nblk = s // tr

    def body(dy_ref, dyn_ref, u_ref, halo_ref, w_ref, du_ref, dw_ref):
        i = pl.program_id(0)

        @pl.when(i == 0)
        def _():
            dw_ref[...] = jnp.zeros_like(dw_ref)

        keep_prev = (i > 0).astype(F32)
        keep_next = (i < nblk - 1).astype(F32)
        row = lax.broadcasted_iota(jnp.int32, (8, 128), 0)
        for c in range(12):
            lanes = slice(128 * c, 128 * c + 128)
            dyc = dy_ref[:, lanes]
            dcat = jnp.concatenate([dyc, dyn_ref[:, lanes] * keep_next], axis=0)
            xcat = jnp.concatenate([halo_ref[:, lanes] * keep_prev, u_ref[:, lanes]], axis=0)
            du = dyc * w_ref[CONV_W - 1:CONV_W, lanes]
            dwc = jnp.where(row == CONV_W - 1, jnp.sum(dyc * u_ref[:, lanes], axis=0, keepdims=True), 0.0)
            for k in range(1, CONV_W):
                du = du + pltpu.roll(dcat, tr + 8 - k, 0)[0:tr] * w_ref[CONV_W - 1 - k:CONV_W - k, lanes]
                ush = pltpu.roll(xcat, k, 0)[8:8 + tr]
                dwc = dwc + jnp.where(row == CONV_W - 1 - k, jnp.sum(dyc * ush, axis=0, keepdims=True), 0.0)
            du_ref[:, lanes] = du
            dw_ref[:, lanes] = dw_ref[:, lanes] + dwc

    return pl.pallas_call(
        body, grid=(nblk,),
        in_specs=[pl.BlockSpec((tr, 1536), lambda i: (i, 0)),
                  pl.BlockSpec((8, 1536), lambda i: (jnp.minimum((i + 1) * nh, s // 8 - 1), 0)),
                  pl.BlockSpec((tr, 1536), lambda i: (i, 0)),
                  pl.BlockSpec((8, 1536), lambda i: (jnp.maximum(i * nh - 1, 0), 0)),
                  pl.BlockSpec((8, 1536), lambda i: (0, 0))],
        out_specs=[pl.BlockSpec((tr, 1536), lambda i: (i, 0)), pl.BlockSpec((8, 1536), lambda i: (0, 0))],
        out_shape=[_sds((s, 1536)), _sds((8, 1536))],
        compiler_params=_cp(("arbitrary",)), name="conv_bwd")(dy, dy, proj, proj, conv_w8)


def _in_bwd_dx(d_qkv_dn, dz, dq_att, dk_att, dv_att, dgate, dba, w_bf, x, dy, norm_w):
    s = x.shape[0]
    tm = 256

    def body(a_ref, b_ref, c_ref, d_ref, e_ref, f_ref, g_ref, w_ref, x_ref, dy_ref, nw_ref,
             dx_ref, dp_ref, dnw_ref):
        i = pl.program_id(0)

        @pl.when(i == 0)
        def _():
            dnw_ref[...] = jnp.zeros_like(dnw_ref)

        dp = jnp.concatenate([r[...].astype(BF16) for r in (a_ref, b_ref, c_ref, d_ref, e_ref, f_ref, g_ref)],
                             axis=1)
        dp_ref[...] = dp
        dh = _dot(dp, w_ref[...])
        xv = x_ref[...]
        rstd = lax.rsqrt(jnp.mean(xv * xv, axis=-1, keepdims=True) + EPS)
        xh = xv * rstd
        dnw_ref[...] = dnw_ref[...] + jnp.sum(dh * xh, axis=0, keepdims=True)
        g = dh * nw_ref[...]
        dx_ref[...] = rstd * (g - xh * jnp.mean(g * xh, axis=-1, keepdims=True)) + dy_ref[...]

    def blk(n):
        return pl.BlockSpec((tm, n), lambda i: (i, 0))

    return pl.pallas_call(
        body, grid=(s // tm,),
        in_specs=[blk(1536), blk(512), blk(512), blk(512), blk(512), blk(512), blk(128),
                  pl.BlockSpec((D_IN_PAD, D_MODEL), lambda i: (0, 0)), blk(D_MODEL), blk(D_MODEL),
                  pl.BlockSpec((1, D_MODEL), lambda i: (0, 0))],
        out_specs=[blk(D_MODEL), blk(D_IN_PAD), pl.BlockSpec((8, D_MODEL), lambda i: (0, 0))],
        out_shape=[_sds((s, D_MODEL)), _sds((s, D_IN_PAD), BF16), _sds((8, D_MODEL))],
        compiler_params=_cp(("arbitrary",)), name="in_bwd_dx")(d_qkv_dn, dz, dq_att, dk_att, dv_att, dgate, dba,
                                                                w_bf, x, dy, norm_w)


def _in_bwd_dw(x, norm_w, dp):
    s = x.shape[0]
    tm = 256
    tn = D_IN_PAD // 3

    def body(x_ref, nw_ref, dp_ref, dw_ref):
        i = pl.program_id(1)

        @pl.when(i == 0)
        def _():
            dw_ref[...] = jnp.zeros_like(dw_ref)

        xv = x_ref[...]
        rstd = lax.rsqrt(jnp.mean(xv * xv, axis=-1, keepdims=True) + EPS)
        h = (xv * rstd * nw_ref[...]).astype(BF16)
        dw_ref[...] = dw_ref[...] + _dot_tn(dp_ref[...], h)

    return pl.pallas_call(
        body, grid=(3, s // tm),
        in_specs=[pl.BlockSpec((tm, D_MODEL), lambda j, i: (i, 0)), pl.BlockSpec((1, D_MODEL), lambda j, i: (0, 0)),
                  pl.BlockSpec((tm, tn), lambda j, i: (i, j))],
        out_specs=pl.BlockSpec((tn, D_MODEL), lambda j, i: (j, 0)),
        out_shape=_sds((D_IN_PAD, D_MODEL)),
        compiler_params=_cp(("parallel", "arbitrary")), name="in_bwd_dw")(x, norm_w, dp)


def _flat(a):
    return a.reshape(-1, a.shape[-1])


def _as_pattern(a, r):
    return a if r == 1 else a.reshape(r, a.shape[0] // r, a.shape[1])


D_SHARD = D_IN // N_CHIPS
BA_START = 4 * D_DN
BA_PACKED = 4096
S1_HEAD = BA_START - D_SHARD
S1_BA = 2 * D_SHARD - BA_START


def _pack_rows(g):
    pad = jnp.zeros((D_IN_PAD - D_IN, g.shape[2]), g.dtype)
    s2_ba = 2 * DN_HEADS - S1_BA
    return jnp.concatenate([g[0][:D_SHARD], g[1][:S1_HEAD], g[2][s2_ba:D_SHARD], g[3][:D_SHARD],
                            g[1][S1_HEAD:D_SHARD], g[2][:s2_ba], pad], axis=0)


def _unpack_rows(p, rows):
    mid = BA_PACKED + S1_BA
    pad = jnp.zeros((rows - D_SHARD, p.shape[1]), p.dtype)
    return jnp.stack([jnp.concatenate([p[0:D_SHARD], pad], axis=0),
                      jnp.concatenate([p[D_SHARD:BA_START], p[BA_PACKED:mid], pad], axis=0),
                      jnp.concatenate([p[mid:BA_PACKED + 2 * DN_HEADS], p[BA_START:BA_START + S1_HEAD], pad], axis=0),
                      jnp.concatenate([p[BA_START + S1_HEAD:BA_PACKED], pad], axis=0)])


def _lane_row(vec, offset):
    return jnp.pad(vec.reshape(1, -1), ((0, 0), (offset, 128 - offset - vec.shape[0])))


def _local_step(x, tgt, norm_w, w_bf, conv_w, a_log, dt_bias, dn_norm_w, q_norm_w, k_norm_w, rel_bias, w_out_bf):
    s = x.shape[0]
    nc = s // CHUNK
    conv_w8 = jnp.pad(conv_w, ((0, 8 - CONV_W), (0, 0)))
    alog_l = _lane_row(a_log.reshape(-1), DN_HEADS)
    dtb_l = _lane_row(dt_bias.reshape(-1), DN_HEADS)
    dnw_t = jnp.tile(dn_norm_w.reshape(1, DN_HD), (1, DN_HEADS))
    qw_t = jnp.tile(q_norm_w.reshape(1, ATT_HD), (1, ATT_HEADS))
    kw_t = jnp.tile(k_norm_w.reshape(1, ATT_HD), (1, ATT_HEADS))
    bd128 = _block_ones(512, DN_HD)
    bd64 = _block_ones(512, ATT_HD)
    tri = _tri_incl()
    buckets = _bucket_tables()

    proj = _in_proj(x, norm_w, w_bf)
    qn, kn, v_dn, bg = _dn_pre(proj, conv_w8, alog_l, dtb_l)
    bgt = bg[:, 0:8].reshape(nc, CHUNK, 8).transpose(0, 2, 1)
    u, w, qd, kt, attn, t_inv, gl = _dn_prep(qn, kn, v_dn, bg, bgt, tri)
    o_dn, vn, sp = _dn_scan(u, w, qd, kt, attn, gl)
    q1, k1, q4, k4, v4, q16, k16, v16 = _att_pre(proj, qw_t, kw_t, bd64)
    rs = [r for _, r in PATTERNS]
    qkv = [(q1, k1, proj, COL_ATT_V), (_flat(q4), _flat(k4), _flat(v4), 0), (_flat(q16), _flat(k16), _flat(v16), 0)]
    bias = _bias_fwd(rel_bias, buckets)
    o_pats, lse_pats = [], []
    for p, r in enumerate(rs):
        o_p, lse_p = _att_fwd(*qkv[p], bias, p, r, "att_fwd_r%d" % r)
        o_pats.append(_as_pattern(o_p, r))
        lse_pats.append(_as_pattern(lse_p, r))
    mixed, o_att, l1, l4, l16 = _post_fwd(o_dn, proj, o_pats, lse_pats, dnw_t, bd128)
    dy, loss_blk = _out_fwd(x, mixed, w_out_bf, tgt)

    dmixed, d_w_out = _out_bwd(dy, mixed, w_out_bf)
    do_dn, dz, dgate, do1, do4, do16, dl1, dl4, dl16, d_dnw = _post_bwd(dmixed, o_dn, proj, o_att, dnw_t, bd128, bd64)
    side = [(do1, l1, dl1), (_flat(do4), _flat(l4), _flat(dl4)), (_flat(do16), _flat(l16), _flat(dl16))]
    dq_pats, dk_pats, dv_pats, ds_accs = [], [], [], []
    for p, r in enumerate(rs):
        dq_p, dk_p, dv_p, ds_p = _att_bwd(*qkv[p], *side[p], bias, p, r, "att_bwd_r%d" % r)
        dq_pats.append(_as_pattern(dq_p, r))
        dk_pats.append(_as_pattern(dk_p, r))
        dv_pats.append(_as_pattern(dv_p, r))
        ds_accs.append(ds_p)
    d_rel_bias = _bias_bwd(ds_accs, buckets)
    dq_att, dk_att, dv_att, d_qw, d_kw = _att_pre_bwd(dq_pats, dk_pats, dv_pats, proj, qw_t, kw_t, bd64)
    du, dqd, dkt, dw, dattn, dgl = _dn_scan_bwd(do_dn, sp, qd, kt, w, vn, attn, gl)
    dqn, dkn, dv_dn, dbg = _dn_prep_bwd(qn, kn, v_dn, bg, bgt, tri, t_inv, attn, u, w, du, dw, dqd, dkt, dattn, dgl)
    dyc, dba, dsm = _dn_pre_bwd(dqn, dkn, dv_dn, dbg, proj, conv_w8, alog_l, dtb_l)
    d_qkv_dn, d_conv8 = _conv_bwd(dyc, proj, conv_w8)
    grad_x, dp, d_nw8 = _in_bwd_dx(d_qkv_dn, dz, dq_att, dk_att, dv_att, dgate, dba, w_bf, x, dy, norm_w)
    d_w_in_t = _in_bwd_dw(x, norm_w, dp)

    grads = dict(
        norm_w=d_nw8[0:1, :],
        w_in_t=d_w_in_t,
        conv_w=d_conv8[0:CONV_W, :],
        a_log=dsm[0:1, DN_HEADS:2 * DN_HEADS],
        dt_bias=dsm[1:2, DN_HEADS:2 * DN_HEADS],
        dn_norm_w=d_dnw[0:1, :],
        q_norm_w=d_qw[0:1, :].reshape(ATT_HEADS, ATT_HD),
        k_norm_w=d_kw[0:1, :].reshape(ATT_HEADS, ATT_HD),
        rel_bias=d_rel_bias,
        w_out=d_w_out,
    )
    return loss_blk[0, 0], grad_x, grads


MESH_ID = pl.DeviceIdType.MESH
ANY = pl.BlockSpec(memory_space=pl.ANY)


def _position():
    return lax.axis_index("x"), lax.axis_index("y"), lax.axis_index("c")


def _other_chips(x, y):
    return [(1 - x, y), (x, 1 - y), (1 - x, 1 - y)]


SHARD_PAD = 1040
WIN = 528


def _row_split(n):
    return (n // 2) // 128 * 128


def _part(ref, cc):
    n = ref.shape[0]
    sp = _row_split(n)
    return ref.at[pl.ds(0, sp)] if cc == 0 else ref.at[pl.ds(sp, n - sp)]


def _gather_weights(wt_s, w_out_s, conv_s):
    def body(a_ref, b_ref, c_ref, ga_ref, gb_ref, gc_ref, send_sems, recv_sems, loc_sems, a_vmem, b_vmem):
        x, y, c = _position()
        me = 2 * x + y
        sib = (x, y, 1 - c)
        big = ((a_ref, ga_ref), (b_ref, gb_ref))
        stage_in = [pltpu.make_async_copy(a_ref, a_vmem, loc_sems.at[0]),
                    pltpu.make_async_copy(b_ref, b_vmem, loc_sems.at[1])]
        local = [pltpu.make_async_copy(a_vmem, ga_ref.at[me], loc_sems.at[0]),
                 pltpu.make_async_copy(b_vmem, gb_ref.at[me], loc_sems.at[1]),
                 pltpu.make_async_copy(c_ref, gc_ref.at[me], loc_sems.at[2])]
        for cp in stage_in:
            cp.start()
        local[2].start()
        for cp in stage_in:
            cp.wait()
        for cp in local[:2]:
            cp.start()
        others = _other_chips(x, y)

        def exchange(cc):
            sends = []
            for j, (px, py) in enumerate(others):
                for t, (src, dst) in enumerate(big):
                    k = 3 * j + t
                    sends.append(pltpu.make_async_remote_copy(
                        src_ref=_part(src, cc), dst_ref=_part(dst.at[me], cc), send_sem=send_sems.at[k],
                        recv_sem=recv_sems.at[k], device_id=(px, py, c), device_id_type=MESH_ID))
                sends.append(pltpu.make_async_remote_copy(
                    src_ref=c_ref, dst_ref=gc_ref.at[me], send_sem=send_sems.at[3 * j + 2],
                    recv_sem=recv_sems.at[3 * j + 2], device_id=(px, py, c), device_id_type=MESH_ID))
            for cp in sends:
                cp.start()
            for j, (px, py) in enumerate(others):
                src_chip = 2 * px + py
                for t, (src, dst) in enumerate(big):
                    landed = _part(dst.at[src_chip], cc)
                    pltpu.make_async_remote_copy(
                        src_ref=_part(src, cc), dst_ref=landed, send_sem=send_sems.at[3 * j + t],
                        recv_sem=recv_sems.at[3 * j + t], device_id=(px, py, c), device_id_type=MESH_ID).wait_recv()
                    k = 9 + 2 * j + t
                    fwd = pltpu.make_async_remote_copy(
                        src_ref=landed, dst_ref=landed, send_sem=send_sems.at[k], recv_sem=recv_sems.at[k],
                        device_id=sib, device_id_type=MESH_ID)
                    fwd.start()
                    sends.append(fwd)
                pltpu.make_async_remote_copy(
                    src_ref=c_ref, dst_ref=gc_ref.at[src_chip], send_sem=send_sems.at[3 * j + 2],
                    recv_sem=recv_sems.at[3 * j + 2], device_id=(px, py, c), device_id_type=MESH_ID).wait_recv()
            for j, (px, py) in enumerate(others):
                src_chip = 2 * px + py
                for t, (src, dst) in enumerate(big):
                    k = 9 + 2 * j + t
                    theirs = _part(dst.at[src_chip], 1 - cc)
                    pltpu.make_async_remote_copy(
                        src_ref=theirs, dst_ref=theirs, send_sem=send_sems.at[k], recv_sem=recv_sems.at[k],
                        device_id=sib, device_id_type=MESH_ID).wait_recv()
            for cp in sends:
                cp.wait_send()

        for cc in (0, 1):
            pl.when(c == cc)(functools.partial(exchange, cc))
        for cp in local:
            cp.wait()

    srcs = (wt_s, w_out_s, conv_s)
    n_sem = 9 + 6
    return pl.pallas_call(
        body, in_specs=[ANY] * 3, out_specs=[ANY] * 3,
        out_shape=[_sds((N_CHIPS,) + a.shape, a.dtype) for a in srcs],
        scratch_shapes=[pltpu.SemaphoreType.DMA((n_sem,)), pltpu.SemaphoreType.DMA((n_sem,)),
                        pltpu.SemaphoreType.DMA((3,)), pltpu.VMEM(wt_s.shape, wt_s.dtype),
                        pltpu.VMEM(w_out_s.shape, w_out_s.dtype)],
        compiler_params=_cp(), name="gather_weights")(*srcs)


def _exchange_grads(gw_in4, gw_out4, small):
    n_big = 2
    n_dev = 8

    def body(a_ref, b_ref, s_ref, ra_ref, rb_ref, rs_ref, send_sems, recv_sems, loc_sem):
        x, y, c = _position()
        dev = 4 * x + 2 * y + c
        pairs = ((a_ref, ra_ref), (b_ref, rb_ref))
        local = pltpu.make_async_copy(s_ref, rs_ref.at[dev], loc_sem)
        local.start()
        sends = []
        for j, (px, py) in enumerate(_other_chips(x, y)):
            for t, (src, dst) in enumerate(pairs):
                k = n_big * j + t
                sends.append(pltpu.make_async_remote_copy(
                    src_ref=src.at[2 * px + py], dst_ref=dst.at[j], send_sem=send_sems.at[k],
                    recv_sem=recv_sems.at[k], device_id=(px, py, c), device_id_type=MESH_ID))
        flips = [(dx, dy, dc) for dx in (0, 1) for dy in (0, 1) for dc in (0, 1)][1:]
        for f, (dx, dy, dc) in enumerate(flips):
            k = 3 * n_big + f
            peer = (x ^ dx, y ^ dy, c ^ dc)
            sends.append(pltpu.make_async_remote_copy(
                src_ref=s_ref, dst_ref=rs_ref.at[dev], send_sem=send_sems.at[k], recv_sem=recv_sems.at[k],
                device_id=peer, device_id_type=MESH_ID))
        for cp in sends:
            cp.start()
        for j, (px, py) in enumerate(_other_chips(x, y)):
            for t, (src, dst) in enumerate(pairs):
                k = n_big * j + t
                pltpu.make_async_remote_copy(
                    src_ref=src.at[0], dst_ref=dst.at[j], send_sem=send_sems.at[k], recv_sem=recv_sems.at[k],
                    device_id=(px, py, c), device_id_type=MESH_ID).wait_recv()
        for f, (dx, dy, dc) in enumerate(flips):
            k = 3 * n_big + f
            peer = (x ^ dx, y ^ dy, c ^ dc)
            pltpu.make_async_remote_copy(
                src_ref=s_ref, dst_ref=rs_ref.at[4 * peer[0] + 2 * peer[1] + peer[2]], send_sem=send_sems.at[k],
                recv_sem=recv_sems.at[k], device_id=peer, device_id_type=MESH_ID).wait_recv()
        for cp in sends:
            cp.wait_send()
        local.wait()

    n_sem = 3 * n_big + n_dev - 1
    return pl.pallas_call(
        body, in_specs=[ANY] * 3, out_specs=[ANY] * 3,
        out_shape=[_sds((3,) + gw_in4.shape[1:], gw_in4.dtype), _sds((3,) + gw_out4.shape[1:], gw_out4.dtype),
                   _sds((n_dev,) + small.shape, small.dtype)],
        scratch_shapes=[pltpu.SemaphoreType.DMA((n_sem,)), pltpu.SemaphoreType.DMA((n_sem,)),
                        pltpu.SemaphoreType.DMA],
        name="exchange_grads")(gw_in4, gw_out4, small)


def _plane_sum(own, recv, name):
    rows, cols = own.shape
    tr = 128

    def body(o_ref, r_ref, out_ref):
        acc = o_ref[...]
        for j in range(3):
            acc = acc + r_ref[j].astype(F32)
        out_ref[...] = acc

    return pl.pallas_call(
        body, grid=(pl.cdiv(rows, tr),),
        in_specs=[pl.BlockSpec((tr, cols), lambda i: (i, 0)), pl.BlockSpec((3, tr, cols), lambda i: (0, i, 0))],
        out_specs=pl.BlockSpec((tr, cols), lambda i: (i, 0)), out_shape=_sds((rows, cols)),
        compiler_params=_cp(("parallel",)), name=name)(own, recv)


def _sum_pair(a, b, name):
    n, rows, cols = a.shape
    tr = 128

    def body(a_ref, b_ref, s_ref, sb_ref):
        tot = a_ref[...] + b_ref[...]
        s_ref[...] = tot
        sb_ref[...] = tot.astype(BF16)

    blk = pl.BlockSpec((1, tr, cols), lambda k, i: (k, i, 0))
    return pl.pallas_call(
        body, grid=(n, pl.cdiv(rows, tr)), in_specs=[blk, blk], out_specs=[blk, blk],
        out_shape=[_sds(a.shape), _sds(a.shape, BF16)],
        compiler_params=_cp(("parallel", "parallel")), name=name)(a, b)


def _fill_parts(f_in, f_out, rows_in, rows_out):
    def body(a_ref, b_ref, fa_ref, fb_ref, send_sems, recv_sems, loc_sems, a_vmem, b_vmem):
        x, y, c = _position()
        sib = (x, y, 1 - c)
        pairs = ((a_ref, fa_ref), (b_ref, fb_ref))
        stage_in = [pltpu.make_async_copy(a_ref, a_vmem, loc_sems.at[0]),
                    pltpu.make_async_copy(b_ref, b_vmem, loc_sems.at[1])]
        for cp in stage_in:
            cp.start()
        for cp in stage_in:
            cp.wait()

        def fill(cc):
            mine = [_part(dst, cc) for _, dst in pairs]
            srcs = [src.at[pl.ds(0, m.shape[0])] for src, m in zip((a_vmem, b_vmem), mine)]
            local = [pltpu.make_async_copy(s, m, loc_sems.at[t]) for t, (s, m) in enumerate(zip(srcs, mine))]
            sends = [pltpu.make_async_remote_copy(src_ref=s, dst_ref=m, send_sem=send_sems.at[t],
                                                  recv_sem=recv_sems.at[t], device_id=sib, device_id_type=MESH_ID)
                     for t, (s, m) in enumerate(zip(srcs, mine))]
            for cp in local + sends:
                cp.start()
            for t, (_, dst) in enumerate(pairs):
                theirs = _part(dst, 1 - cc)
                pltpu.make_async_remote_copy(src_ref=theirs, dst_ref=theirs, send_sem=send_sems.at[t],
                                             recv_sem=recv_sems.at[t], device_id=sib, device_id_type=MESH_ID).wait_recv()
            for cp in sends:
                cp.wait_send()
            for cp in local:
                cp.wait()

        for cc in (0, 1):
            pl.when(c == cc)(functools.partial(fill, cc))

    return pl.pallas_call(
        body, in_specs=[ANY] * 2, out_specs=[ANY] * 2,
        out_shape=[_sds((rows_in, D_MODEL)), _sds((rows_out, D_MODEL))],
        scratch_shapes=[pltpu.SemaphoreType.DMA((2,)), pltpu.SemaphoreType.DMA((2,)), pltpu.SemaphoreType.DMA((2,)),
                        pltpu.VMEM(f_in.shape, F32), pltpu.VMEM(f_out.shape, F32)],
        compiler_params=_cp(), name="fill_parts")(f_in, f_out)


def _swap_windows(u_in, u_out, win_in, win_out):
    def body(a_ref, b_ref, ra_ref, rb_ref, send_sems, recv_sems):
        x, y, c = _position()
        sib = (x, y, 1 - c)
        cps = []
        for t, (src, dst, win) in enumerate(((a_ref, ra_ref, win_in), (b_ref, rb_ref, win_out))):
            split = _row_split(src.shape[1])
            start = pl.multiple_of((1 - c) * split, split)
            cps.append(pltpu.make_async_remote_copy(
                src_ref=src.at[:, pl.ds(start, win), :], dst_ref=dst, send_sem=send_sems.at[t],
                recv_sem=recv_sems.at[t], device_id=sib, device_id_type=MESH_ID))
        for cp in cps:
            cp.start()
        for cp in cps:
            cp.wait_recv()
        for cp in cps:
            cp.wait_send()

    return pl.pallas_call(
        body, in_specs=[ANY] * 2, out_specs=[ANY] * 2,
        out_shape=[_sds((N_CHIPS, win_in, D_MODEL)), _sds((N_CHIPS, win_out, D_MODEL))],
        scratch_shapes=[pltpu.SemaphoreType.DMA((2,)), pltpu.SemaphoreType.DMA((2,))],
        name="swap_windows")(u_in, u_out)


SMALL_LAYOUT = (("norm_w", 1024), ("conv_w", 6144), ("a_log", 128), ("dt_bias", 128), ("dn_norm_w", 128),
                ("q_norm_w", 512), ("k_norm_w", 512), ("rel_bias", 256))
SMALL_TOTAL = sum(n for _, n in SMALL_LAYOUT)


def _small_offset(name):
    off = 0
    for n, size in SMALL_LAYOUT:
        if n == name:
            return off
        off += size
    raise KeyError(name)


def _pack_small(grads):
    parts = []
    for name, size in SMALL_LAYOUT:
        flat = grads[name].reshape(1, -1)
        parts.append(jnp.pad(flat, ((0, 0), (0, size - flat.shape[1]))))
    return jnp.concatenate(parts, axis=1)


def _sum_small(rows):
    n_dev = rows.shape[0]
    q_off = _small_offset("q_norm_w")
    k_off = _small_offset("k_norm_w")

    def body(r_ref, tot_ref, qk_ref):
        tot = r_ref[0:1, :]
        for d in range(1, n_dev):
            tot = tot + r_ref[d:d + 1, :]
        tot_ref[...] = tot
        for row, off in ((0, q_off), (1, k_off)):
            s4 = tot[:, off:off + 128] + tot[:, off + 128:off + 256] + tot[:, off + 256:off + 384] \
                + tot[:, off + 384:off + 512]
            qk_ref[row:row + 1, :] = s4 + pltpu.roll(s4, ATT_HD, 1)

    return pl.pallas_call(
        body, in_specs=[pl.BlockSpec(memory_space=pltpu.VMEM)],
        out_specs=[pl.BlockSpec(memory_space=pltpu.VMEM)] * 2,
        out_shape=[_sds((1, SMALL_TOTAL)), _sds((2, 128))],
        compiler_params=_cp(), name="sum_small")(rows)


def _adamw_math(w, g, m, v):
    m = ADAM_B1 * m + (1.0 - ADAM_B1) * g
    v = ADAM_B2 * v + (1.0 - ADAM_B2) * (g * g)
    m_hat = m / (1.0 - ADAM_B1 ** ADAM_STEP)
    v_hat = v / (1.0 - ADAM_B2 ** ADAM_STEP)
    delta = -ADAM_LR * (m_hat / (jnp.sqrt(v_hat) + ADAM_EPS) + ADAM_WD * w)
    return delta, m, v


def _adamw_big(g, w, m, v, name):
    rows, cols = w.shape
    tr = 128

    def body(g_ref, w_ref, m_ref, v_ref, go_ref, d_ref, nm_ref, nv_ref):
        g = g_ref[...]
        go_ref[...] = g
        d_ref[...], nm_ref[...], nv_ref[...] = _adamw_math(w_ref[...], g, m_ref[...], v_ref[...])

    blk = pl.BlockSpec((tr, cols), lambda i: (i, 0))
    return pl.pallas_call(
        body, grid=(pl.cdiv(rows, tr),), in_specs=[blk] * 4, out_specs=[blk] * 4,
        out_shape=[_sds((rows, cols))] * 4, compiler_params=_cp(("parallel",)), name=name)(g, w, m, v)


def _adamw_small(w, g, m, v, name):
    def body(w_ref, g_ref, m_ref, v_ref, d_ref, nm_ref, nv_ref):
        d_ref[...], nm_ref[...], nv_ref[...] = _adamw_math(w_ref[...], g_ref[...], m_ref[...], v_ref[...])

    vm = pl.BlockSpec(memory_space=pltpu.VMEM)
    return pl.pallas_call(body, in_specs=[vm] * 4, out_specs=[vm] * 3, out_shape=[_sds(w.shape)] * 3,
                          compiler_params=_cp(), name=name)(w, g, m, v)


WEIGHTS = ("norm_w", "w_in", "conv_w", "a_log", "dt_bias", "dn_norm_w", "q_norm_w", "k_norm_w", "rel_bias", "w_out")


def kernel(x, norm_w, w_in, conv_w, a_log, dt_bias, dn_norm_w, q_norm_w, k_norm_w, rel_bias, w_out, loss_target, m_norm_w, m_w_in, m_conv_w, m_a_log, m_dt_bias, m_dn_norm_w, m_q_norm_w, m_k_norm_w, m_rel_bias, m_w_out, v_norm_w, v_w_in, v_conv_w, v_a_log, v_dt_bias, v_dn_norm_w, v_q_norm_w, v_k_norm_w, v_rel_bias, v_w_out):
    xi, yi, ci = _position()
    chip = 2 * xi + yi
    w_loc = dict(norm_w=norm_w, w_in=w_in[0].T, conv_w=conv_w[0], a_log=a_log, dt_bias=dt_bias, dn_norm_w=dn_norm_w,
                 q_norm_w=q_norm_w, k_norm_w=k_norm_w, rel_bias=rel_bias, w_out=w_out[0])
    m_loc = dict(norm_w=m_norm_w, w_in=m_w_in[0].T, conv_w=m_conv_w[0], a_log=m_a_log, dt_bias=m_dt_bias,
                 dn_norm_w=m_dn_norm_w, q_norm_w=m_q_norm_w, k_norm_w=m_k_norm_w, rel_bias=m_rel_bias,
                 w_out=m_w_out[0])
    v_loc = dict(norm_w=v_norm_w, w_in=v_w_in[0].T, conv_w=v_conv_w[0], a_log=v_a_log, dt_bias=v_dt_bias,
                 dn_norm_w=v_dn_norm_w, q_norm_w=v_q_norm_w, k_norm_w=v_k_norm_w, rel_bias=v_rel_bias,
                 w_out=v_w_out[0])

    wt_pad = jnp.pad(w_loc["w_in"].astype(BF16), ((0, SHARD_PAD - D_SHARD), (0, 0)))
    g_in, g_out, g_conv = _gather_weights(wt_pad, w_loc["w_out"].astype(BF16), w_loc["conv_w"])
    wt_full = _pack_rows(g_in)
    w_out_full = g_out.reshape(D_MODEL, D_MODEL)
    conv_full = g_conv.transpose(1, 0, 2).reshape(CONV_W, 3 * D_DN)

    loss_local, grad_x, grads = _local_step(x[0], loss_target[0], norm_w, wt_full, conv_full, a_log, dt_bias,
                                            dn_norm_w, q_norm_w, k_norm_w, rel_bias, w_out_full)
    loss = lax.psum(loss_local, ("x", "y", "c"))

    u_in = _unpack_rows(grads["w_in_t"], SHARD_PAD)
    u_out = grads["w_out"].reshape(N_CHIPS, D_MODEL // N_CHIPS, D_MODEL)
    win_out = u_out.shape[1] // 2
    sib_in, sib_out = _swap_windows(u_in, u_out, WIN, win_out)
    in_mine = lax.dynamic_slice_in_dim(u_in, ci * _row_split(SHARD_PAD), WIN, axis=1)
    out_mine = lax.dynamic_slice_in_dim(u_out, ci * win_out, win_out, axis=1)
    h_in, hb_in = _sum_pair(in_mine, sib_in, "chip_sum_w_in")
    h_out, hb_out = _sum_pair(out_mine, sib_out, "chip_sum_w_out")
    small = _pack_small(grads)
    r_in, r_out, r_small = _exchange_grads(hb_in, hb_out, small)
    f_in = _plane_sum(lax.dynamic_index_in_dim(h_in, chip, 0, keepdims=False), r_in, "shard_sum_w_in")
    f_out = _plane_sum(lax.dynamic_index_in_dim(h_out, chip, 0, keepdims=False), r_out, "shard_sum_w_out")
    full_in, full_out = _fill_parts(f_in, f_out, SHARD_PAD, u_out.shape[1])
    tot_small, qk = _sum_small(r_small.reshape(8, SMALL_TOTAL))

    def small_grad(name, n):
        off = _small_offset(name)
        return tot_small[:, off:off + n]

    conv_all = small_grad("conv_w", CONV_W * 3 * D_DN).reshape(CONV_W, 3 * D_DN)
    g_small = dict(
        norm_w=small_grad("norm_w", D_MODEL),
        conv_w=lax.dynamic_slice_in_dim(conv_all, chip * (3 * D_DN // N_CHIPS), 3 * D_DN // N_CHIPS, axis=1),
        a_log=small_grad("a_log", DN_HEADS),
        dt_bias=small_grad("dt_bias", DN_HEADS),
        dn_norm_w=small_grad("dn_norm_w", DN_HD),
        q_norm_w=qk[0:1, 0:ATT_HD],
        k_norm_w=qk[1:2, 0:ATT_HD],
        rel_bias=small_grad("rel_bias", ATT_HEADS * N_BUCKETS).reshape(ATT_HEADS, N_BUCKETS),
    )

    out_g, out_d, out_m, out_v = {}, {}, {}, {}
    out_g["w_in"], out_d["w_in"], out_m["w_in"], out_v["w_in"] = _adamw_big(
        full_in, w_loc["w_in"], m_loc["w_in"], v_loc["w_in"], "adamw_w_in")
    out_g["w_out"], out_d["w_out"], out_m["w_out"], out_v["w_out"] = _adamw_big(
        full_out, w_loc["w_out"], m_loc["w_out"], v_loc["w_out"], "adamw_w_out")
    for name in g_small:
        out_g[name] = g_small[name]
        out_d[name], out_m[name], out_v[name] = _adamw_small(w_loc[name], g_small[name], m_loc[name], v_loc[name],
                                                             "adamw_" + name)
    for d in (out_g, out_d, out_m, out_v):
        d["w_in"] = d["w_in"].T
        for name in ("w_in", "conv_w", "w_out"):
            d[name] = d[name][None]
    return (loss, grad_x[None], *[out_g[n] for n in WEIGHTS], *[out_d[n] for n in WEIGHTS],
            *[out_m[n] for n in WEIGHTS], *[out_v[n] for n in WEIGHTS])
```

```python
import functools
import math

import numpy as np
import jax
import jax.numpy as jnp
from jax import lax
from jax.experimental import pallas as pl
from jax.experimental.pallas import tpu as pltpu

F32 = jnp.float32
BF16 = jnp.bfloat16
HI = lax.Precision.HIGHEST

D_MODEL = 1024
D_DN = 512
DN_HEADS = 4
DN_HD = 128
CONV_W = 4
CHUNK = 64
D_ATT = 512
ATT_HEADS = 8
ATT_HD = 64
PATTERNS = ((128, 1), (512, 4), (2048, 16))
N_BUCKETS = 32
MAX_DISTANCE = 2048
D_IN = 4 * D_DN + 2 * DN_HEADS + 4 * D_ATT
D_IN_PAD = 4224
EPS = 1e-6
BLK = 128
NEG = -1e30
N_CHIPS = 4

ADAM_LR = 0.001
ADAM_B1 = 0.9
ADAM_B2 = 0.999
ADAM_EPS = 1e-08
ADAM_WD = 0.01
ADAM_STEP = 10

VMEM_LIMIT = 56 * 1024 * 1024

COL_Z = 3
COL_ATT_Q = 4
COL_ATT_K = 5
COL_ATT_V = 6
COL_GATE = 7
COL_BA_128 = 32


def _cp(sem=None):
    if sem is None:
        return pltpu.CompilerParams(vmem_limit_bytes=VMEM_LIMIT)
    return pltpu.CompilerParams(dimension_semantics=sem, vmem_limit_bytes=VMEM_LIMIT)


def _sds(shape, dtype=F32):
    return jax.ShapeDtypeStruct(shape, dtype)


def _mm(a, b):
    return jnp.dot(a.astype(BF16), b.astype(BF16), preferred_element_type=F32)


def _mm_nt(a, b):
    return lax.dot_general(a.astype(BF16), b.astype(BF16), (((1,), (1,)), ((), ())),
                           preferred_element_type=F32)


def _mm_tn(a, b):
    return lax.dot_general(a.astype(BF16), b.astype(BF16), (((0,), (0,)), ((), ())),
                           preferred_element_type=F32)


def _mmx(a, b):
    return jnp.dot(a, b, precision=HI, preferred_element_type=F32)


def _mmx_nt(a, b):
    return lax.dot_general(a, b, (((1,), (1,)), ((), ())), precision=HI, preferred_element_type=F32)


def _mmx_tn(a, b):
    return lax.dot_general(a, b, (((0,), (0,)), ((), ())), precision=HI, preferred_element_type=F32)


def _dot(a, b):
    return jnp.dot(a, b, preferred_element_type=F32)


def _dot_nt(a, b):
    return lax.dot_general(a, b, (((1,), (1,)), ((), ())), preferred_element_type=F32)


def _dot_tn(a, b):
    return lax.dot_general(a, b, (((0,), (0,)), ((), ())), preferred_element_type=F32)


def _split(a):
    hi = a.astype(BF16)
    return hi, (a - hi.astype(F32)).astype(BF16)


def _mm3(a_s, b_s):
    return _dot(a_s[0], b_s[0]) + _dot(a_s[0], b_s[1]) + _dot(a_s[1], b_s[0])


def _mm3_tn(a_s, b_s):
    return _dot_tn(a_s[0], b_s[0]) + _dot_tn(a_s[0], b_s[1]) + _dot_tn(a_s[1], b_s[0])


def _interleave(gens):
    live = list(gens)
    while live:
        nxt = []
        for g in live:
            try:
                next(g)
                nxt.append(g)
            except StopIteration:
                pass
        live = nxt


def _segsum(x, bd):
    hi = x.astype(BF16)
    r1 = x - hi.astype(F32)
    mid = r1.astype(BF16)
    lo = (r1 - mid.astype(F32)).astype(BF16)
    return (jnp.dot(hi, bd, preferred_element_type=F32) + jnp.dot(mid, bd, preferred_element_type=F32)
            + jnp.dot(lo, bd, preferred_element_type=F32))


def _sigmoid(x):
    return 1.0 / (1.0 + jnp.exp(-x))


def _silu_grad(x, s):
    return s * (1.0 + x * (1.0 - s))


def _block_ones(n, seg):
    i = np.arange(n)
    return jnp.asarray((i[:, None] // seg == i[None, :] // seg).astype(np.float32), dtype=BF16)


def _tri_incl():
    i = np.arange(CHUNK)
    return jnp.asarray((i[:, None] >= i[None, :]).astype(np.float32))


def _t5_bucket(dist):
    max_exact = N_BUCKETS // 2
    d = np.maximum(dist, 1).astype(np.float64)
    large = max_exact + (np.log(d / max_exact) / math.log(MAX_DISTANCE / max_exact)
                         * (N_BUCKETS - max_exact)).astype(np.int32)
    large = np.minimum(large, N_BUCKETS - 1)
    return np.where(dist < max_exact, dist, large).astype(np.int32)


def _bucket_tables():
    qi = np.arange(BLK)[:, None]
    kj = np.arange(2 * BLK)[None, :]
    step = qi - kj + BLK
    return jnp.asarray(np.stack([_t5_bucket(np.clip(step, 0, None) * r) for _, r in PATTERNS]))


def _in_proj(x, norm_w, wt_bf):
    s = x.shape[0]
    tm = 256

    def body(x_ref, nw_ref, w_ref, o_ref):
        xv = x_ref[...]
        rstd = lax.rsqrt(jnp.mean(xv * xv, axis=-1, keepdims=True) + EPS)
        h = (xv * rstd * nw_ref[...]).astype(BF16)
        o_ref[...] = _dot_nt(h, w_ref[...])

    return pl.pallas_call(
        body, grid=(s // tm,),
        in_specs=[pl.BlockSpec((tm, D_MODEL), lambda i: (i, 0)),
                  pl.BlockSpec((1, D_MODEL), lambda i: (0, 0)),
                  pl.BlockSpec((D_IN_PAD, D_MODEL), lambda i: (0, 0))],
        out_specs=pl.BlockSpec((tm, D_IN_PAD), lambda i: (i, 0)),
        out_shape=_sds((s, D_IN_PAD)), compiler_params=_cp(("parallel",)), name="in_proj")(x, norm_w, wt_bf)


def _conv_group(cur, halo, w_ref, c):
    rows = cur.shape[0]
    lanes = slice(128 * c, 128 * c + 128)
    xcat = jnp.concatenate([halo, cur], axis=0)
    y = cur * w_ref[CONV_W - 1:CONV_W, lanes]
    for k in range(1, CONV_W):
        sh = pltpu.roll(xcat, k, 0)[8:8 + rows]
        y = y + sh * w_ref[CONV_W - 1 - k:CONV_W - k, lanes]
    return y


def _beta_g(ba, alog_l, dtb_l):
    lane = lax.broadcasted_iota(jnp.int32, ba.shape, 1)
    sig_b = _sigmoid(ba)
    t = ba + dtb_l
    softplus = jnp.maximum(t, 0.0) + jnp.log(1.0 + jnp.exp(-jnp.abs(t)))
    nega = -jnp.exp(alog_l)
    g = nega * softplus
    out = jnp.where(lane < DN_HEADS, sig_b, jnp.where(lane < 2 * DN_HEADS, g, 0.0))
    return out, lane, sig_b, t, nega, g


def _dn_pre(proj, conv_w8, alog_l, dtb_l):
    s = proj.shape[0]
    tr = 512
    nh = tr // 8

    def body(u_ref, halo_ref, ba_ref, w_ref, al_ref, dt_ref, q_ref, k_ref, v_ref, bg_ref):
        i = pl.program_id(0)
        keep = (i > 0).astype(F32)
        for c in range(12):
            lanes = slice(128 * c, 128 * c + 128)
            y = _conv_group(u_ref[:, lanes], halo_ref[:, lanes] * keep, w_ref, c)
            sv = y * _sigmoid(y)
            if c < 8:
                rs = lax.rsqrt(jnp.sum(sv * sv, axis=1, keepdims=True) + EPS)
                n = sv * rs
                if c < 4:
                    q_ref[:, lanes] = n * (DN_HD ** -0.5)
                else:
                    k_ref[:, slice(128 * (c - 4), 128 * (c - 3))] = n
            else:
                v_ref[:, slice(128 * (c - 8), 128 * (c - 7))] = sv
        bg_ref[...] = _beta_g(ba_ref[...], al_ref[...], dt_ref[...])[0]

    return pl.pallas_call(
        body, grid=(s // tr,),
        in_specs=[pl.BlockSpec((tr, 1536), lambda i: (i, 0)),
                  pl.BlockSpec((8, 1536), lambda i: (jnp.maximum(i * nh - 1, 0), 0)),
                  pl.BlockSpec((tr, 128), lambda i: (i, COL_BA_128)),
                  pl.BlockSpec((8, 1536), lambda i: (0, 0)),
                  pl.BlockSpec((1, 128), lambda i: (0, 0)),
                  pl.BlockSpec((1, 128), lambda i: (0, 0))],
        out_specs=[pl.BlockSpec((tr, 512), lambda i: (i, 0))] * 3 + [pl.BlockSpec((tr, 128), lambda i: (i, 0))],
        out_shape=[_sds((s, 512))] * 3 + [_sds((s, 128))],
        compiler_params=_cp(("parallel",)), name="dn_pre")(proj, proj, proj, conv_w8, alog_l, dtb_l)


CPS = 2


def _split3(a):
    hi = a.astype(BF16)
    r1 = a - hi.astype(F32)
    mid = r1.astype(BF16)
    return hi, mid, (r1 - mid.astype(F32)).astype(BF16)


def _lane_select():
    r = np.arange(128)
    return jnp.asarray((r[None, :, None] == np.arange(8)[:, None, None]) * np.ones((1, 1, 128)), dtype=BF16)


def _lane_bcast(a3, sel):
    return _dot(a3[0], sel) + _dot(a3[1], sel) + _dot(a3[2], sel)


def _rowsum_b(z, ones_b):
    hi, lo = _split(z)
    return _dot(hi, ones_b) + _dot(lo, ones_b)


def _chunk_cumsum(bg, bgt, tri):
    return _split3(bg), _split3(_mmx(tri, bg)), _mmx_nt(bgt, tri)


def _chunk_common(bg3, gc3, gc_row, h, sel_ref):
    gcc = _lane_bcast(gc3, sel_ref[DN_HEADS + h])
    beta = _lane_bcast(bg3, sel_ref[h])
    gcr = gc_row[DN_HEADS + h:DN_HEADS + h + 1, :]
    ii = lax.broadcasted_iota(jnp.int32, (CHUNK, CHUNK), 0)
    jj = lax.broadcasted_iota(jnp.int32, (CHUNK, CHUNK), 1)
    incl = ii >= jj
    strict = ii > jj
    decay = jnp.exp(jnp.where(incl, gcc[:, 0:CHUNK] - gcr, NEG))
    gl = gcc[CHUNK - 1:CHUNK, :]
    return gcc, beta, incl, strict, decay, gl


def _dn_prep(qn, kn, v, bg, bgt, tri, sel):
    s = qn.shape[0]
    nc = s // CHUNK

    def body(q_ref, k_ref, v_ref, bg_ref, bgt_ref, tri_ref, sel_ref,
             u_ref, w_ref, qd_ref, kt_ref, attn_ref, t_ref, gl_ref):
        tri_v = tri_ref[...]
        ii = lax.broadcasted_iota(jnp.int32, (CHUNK, CHUNK), 0)
        jj = lax.broadcasted_iota(jnp.int32, (CHUNK, CHUNK), 1)
        eye = (ii == jj).astype(F32)

        def chain(cc, h, bg3, gc3, gc_row):
            rows = slice(CHUNK * cc, CHUNK * cc + CHUNK)
            lanes = slice(128 * h, 128 * h + 128)
            gcc, beta, incl, strict, decay, gl = _chunk_common(bg3, gc3, gc_row, h, sel_ref)
            yield
            q = q_ref[rows, lanes]
            k = k_ref[rows, lanes]
            vv = v_ref[rows, lanes]
            kb = k * beta
            egc = jnp.exp(gcc)
            a_mat = jnp.where(strict, _mm_nt(kb, k) * decay, 0.0)
            attn_ref[cc, h] = jnp.where(incl, _mm_nt(q, k) * decay, 0.0)
            qd_ref[rows, lanes] = q * egc
            kt_ref[rows, lanes] = k * jnp.exp(gl - gcc)
            gl_ref[cc, h] = jnp.exp(gl)
            yield
            p = -a_mat
            t = eye + p
            for _ in range(5):
                ps = _split(p)
                p = _mm3(ps, ps)
                yield
                t = t + _mm3(_split(t), _split(p))
                yield
            t_ref[cc, h] = t
            ts = _split(t)
            u_ref[rows, lanes] = _mm3(ts, _split(vv * beta))
            w_ref[rows, lanes] = _mm3(ts, _split(kb * egc))

        gens = []
        for cc in range(CPS):
            bgv = bg_ref[CHUNK * cc:CHUNK * cc + CHUNK, :]
            bg3, gc3, gc_row = _chunk_cumsum(bgv, bgt_ref[cc], tri_v)
            gens += [chain(cc, h, bg3, gc3, gc_row) for h in range(DN_HEADS)]
        _interleave(gens)

    rows_step = CPS * CHUNK
    big = pl.BlockSpec((rows_step, 512), lambda n: (n, 0))
    sq = pl.BlockSpec((CPS, DN_HEADS, CHUNK, CHUNK), lambda n: (n, 0, 0, 0))
    return pl.pallas_call(
        body, grid=(nc // CPS,),
        in_specs=[big, big, big, pl.BlockSpec((rows_step, 128), lambda n: (n, 0)),
                  pl.BlockSpec((CPS, 8, CHUNK), lambda n: (n, 0, 0)),
                  pl.BlockSpec((CHUNK, CHUNK), lambda n: (0, 0)),
                  pl.BlockSpec((8, 128, 128), lambda n: (0, 0, 0))],
        out_specs=[big, big, big, big, sq, sq, pl.BlockSpec((CPS, DN_HEADS, 1, 128), lambda n: (n, 0, 0, 0))],
        out_shape=[_sds((s, 512))] * 4 + [_sds((nc, DN_HEADS, CHUNK, CHUNK))] * 2 + [_sds((nc, DN_HEADS, 1, 128))],
        compiler_params=_cp(("parallel",)), name="dn_prep")(qn, kn, v, bg, bgt, tri, sel)


def _dn_scan(u, w, qd, kt, attn, gl):
    s = u.shape[0]
    nc = s // CHUNK

    def body(u_ref, w_ref, qd_ref, kt_ref, attn_ref, gl_ref, o_ref, vn_ref, sp_ref, st_ref):
        n = pl.program_id(0)

        @pl.when(n == 0)
        def _():
            st_ref[...] = jnp.zeros_like(st_ref)

        def chain(cc, h):
            rows = slice(CHUNK * cc, CHUNK * cc + CHUNK)
            lanes = slice(128 * h, 128 * h + 128)
            st = st_ref[h]
            sp_ref[cc, h] = st
            stb = st.astype(BF16)
            ws = _dot(w_ref[rows, lanes].astype(BF16), stb)
            qs = _dot(qd_ref[rows, lanes].astype(BF16), stb)
            yield
            vn = u_ref[rows, lanes] - ws
            vn_ref[rows, lanes] = vn
            vnb = vn.astype(BF16)
            o_ref[rows, lanes] = qs + _dot(attn_ref[cc, h].astype(BF16), vnb)
            st_ref[h] = st * gl_ref[cc, h] + _dot_tn(kt_ref[rows, lanes].astype(BF16), vnb)

        for cc in range(CPS):
            _interleave([chain(cc, h) for h in range(DN_HEADS)])

    big = pl.BlockSpec((CPS * CHUNK, 512), lambda n: (n, 0))
    return pl.pallas_call(
        body, grid=(nc // CPS,),
        in_specs=[big, big, big, big,
                  pl.BlockSpec((CPS, DN_HEADS, CHUNK, CHUNK), lambda n: (n, 0, 0, 0)),
                  pl.BlockSpec((CPS, DN_HEADS, 1, 128), lambda n: (n, 0, 0, 0))],
        out_specs=[big, big, pl.BlockSpec((CPS, DN_HEADS, DN_HD, DN_HD), lambda n: (n, 0, 0, 0))],
        out_shape=[_sds((s, 512)), _sds((s, 512)), _sds((nc, DN_HEADS, DN_HD, DN_HD))],
        scratch_shapes=[pltpu.VMEM((DN_HEADS, DN_HD, DN_HD), F32)],
        compiler_params=_cp(("arbitrary",)), name="dn_scan")(u, w, qd, kt, attn, gl)


R4 = PATTERNS[1][1]
R16 = PATTERNS[2][1]
TM = 256


def _pattern_spec(r):
    return pl.BlockSpec((r, TM // r, 512), lambda i: (0, i, 0))


def _pattern_shape(s, r):
    return _sds((r, s // r, 512))


SLABS = pltpu.VMEM((4, TM, 128), F32)


def _to_patterns(val, dsts, scr):
    for c in range(4):
        lanes = slice(128 * c, 128 * c + 128)
        scr[c] = val[:, lanes]
        for dst_ref, r in dsts:
            for a in range(r):
                dst_ref[a, :, lanes] = scr[c, pl.ds(a, TM // r, stride=r), :]


def _from_pattern(src_ref, r, scr):
    for c in range(4):
        for a in range(r):
            scr[c, pl.ds(a, TM // r, stride=r), :] = src_ref[a, :, 128 * c:128 * c + 128]
    return jnp.concatenate([scr[c] for c in range(4)], axis=1)


def _att_pre(proj, qw_t, kw_t, bd64):
    s = proj.shape[0]

    def body(q_ref, k_ref, v_ref, qw_ref, kw_ref, bd_ref,
             q1_ref, k1_ref, q4_ref, k4_ref, v4_ref, q16_ref, k16_ref, v16_ref, scr):
        bd = bd_ref[...]
        q = q_ref[...]
        k = k_ref[...]
        qn = q * lax.rsqrt(_segsum(q * q, bd) * (1.0 / ATT_HD) + EPS) * qw_ref[...] * (ATT_HD ** -0.5)
        kn = k * lax.rsqrt(_segsum(k * k, bd) * (1.0 / ATT_HD) + EPS) * kw_ref[...]
        q1_ref[...] = qn
        k1_ref[...] = kn
        _to_patterns(qn, ((q4_ref, R4), (q16_ref, R16)), scr)
        _to_patterns(kn, ((k4_ref, R4), (k16_ref, R16)), scr)
        _to_patterns(v_ref[...], ((v4_ref, R4), (v16_ref, R16)), scr)

    row = pl.BlockSpec((1, 512), lambda i: (0, 0))
    tok = pl.BlockSpec((TM, 512), lambda i: (i, 0))
    return pl.pallas_call(
        body, grid=(s // TM,),
        in_specs=[pl.BlockSpec((TM, 512), lambda i: (i, COL_ATT_Q)),
                  pl.BlockSpec((TM, 512), lambda i: (i, COL_ATT_K)),
                  pl.BlockSpec((TM, 512), lambda i: (i, COL_ATT_V)),
                  row, row, pl.BlockSpec((512, 512), lambda i: (0, 0))],
        out_specs=[tok, tok] + [_pattern_spec(R4)] * 3 + [_pattern_spec(R16)] * 3,
        out_shape=[_sds((s, 512))] * 2 + [_pattern_shape(s, R4)] * 3 + [_pattern_shape(s, R16)] * 3,
        scratch_shapes=[SLABS],
        compiler_params=_cp(("parallel",)), name="att_pre")(proj, proj, proj, qw_t, kw_t, bd64)


def _bias_fwd(rel_bias, buckets):
    def body(rb_ref, bk_ref, o_ref):
        for p in range(len(PATTERNS)):
            bk = bk_ref[p]
            for h in range(ATT_HEADS):
                acc = jnp.zeros((BLK, 2 * BLK), F32)
                for b in range(N_BUCKETS):
                    acc = jnp.where(bk == b, rb_ref[h, b], acc)
                o_ref[p, h] = acc

    return pl.pallas_call(
        body,
        in_specs=[pl.BlockSpec(memory_space=pltpu.SMEM), pl.BlockSpec(memory_space=pltpu.VMEM)],
        out_specs=pl.BlockSpec(memory_space=pltpu.VMEM),
        out_shape=_sds((len(PATTERNS), ATT_HEADS, BLK, 2 * BLK)),
        compiler_params=_cp(), name="bias_fwd")(rel_bias, buckets)


def _bias_bwd(ds_accs, buckets):
    def body(ds0_ref, ds1_ref, ds2_ref, bk_ref, o_ref):
        for h in range(ATT_HEADS):
            for b in range(N_BUCKETS):
                tot = jnp.zeros((), F32)
                for p, ds_ref in enumerate((ds0_ref, ds1_ref, ds2_ref)):
                    tot = tot + jnp.sum(jnp.where(bk_ref[p] == b, ds_ref[h], 0.0))
                o_ref[h, b] = tot

    return pl.pallas_call(
        body,
        in_specs=[pl.BlockSpec(memory_space=pltpu.VMEM)] * 4,
        out_specs=pl.BlockSpec(memory_space=pltpu.SMEM),
        out_shape=_sds((ATT_HEADS, N_BUCKETS)),
        compiler_params=_cp(), name="bias_bwd")(*ds_accs, buckets)


def _att_masks(has_prev):
    qi = lax.broadcasted_iota(jnp.int32, (BLK, BLK), 0)
    kj = lax.broadcasted_iota(jnp.int32, (BLK, BLK), 1)
    lane = lax.broadcasted_iota(jnp.int32, (BLK, 2 * ATT_HD), 1)
    return jnp.logical_and(kj >= qi, has_prev), kj <= qi, lane < ATT_HD


def _head_lanes(h):
    half = h % 2
    return slice(ATT_HD * h, ATT_HD * h + ATT_HD), slice(ATT_HD * half, ATT_HD * half + ATT_HD)


def _att_scores(qm, kp2, kc2, bias_h, mask_prev, mask_cur):
    s_prev = jnp.where(mask_prev, _dot_nt(qm, kp2) + bias_h[:, :BLK], NEG)
    s_cur = jnp.where(mask_cur, _dot_nt(qm, kc2) + bias_h[:, BLK:], NEG)
    return s_prev, s_cur


def _att_fwd(q, k, v, v_col, bias, p_idx, r, name):
    s = q.shape[0]
    nblk = s // BLK
    nseq = nblk // r

    def body(q_ref, kp_ref, kc_ref, vp_ref, vc_ref, b_ref, o_ref, lse_ref):
        j = pl.program_id(0)
        mask_prev, mask_cur, lo_half = _att_masks((j % nseq) != 0)

        def head(h, q2, kp2, kc2, vp2, vc2):
            out_l, pair_l = _head_lanes(h)
            sel = lo_half if h % 2 == 0 else jnp.logical_not(lo_half)
            qm = jnp.where(sel, q2, 0.0).astype(BF16)
            s_prev, s_cur = _att_scores(qm, kp2, kc2, b_ref[0, h], mask_prev, mask_cur)
            yield
            m = jnp.maximum(jnp.max(s_prev, axis=1, keepdims=True), jnp.max(s_cur, axis=1, keepdims=True))
            p_prev = jnp.exp(s_prev - m)
            p_cur = jnp.exp(s_cur - m)
            l = jnp.sum(p_prev, axis=1, keepdims=True) + jnp.sum(p_cur, axis=1, keepdims=True)
            yield
            o2 = _dot(p_prev.astype(BF16), vp2) + _dot(p_cur.astype(BF16), vc2)
            o_ref[:, out_l] = (o2 * (1.0 / l))[:, pair_l]
            lse_ref[:, out_l] = jnp.broadcast_to(m + jnp.log(l), (BLK, ATT_HD))

        gens = []
        for pp in range(ATT_HEADS // 2):
            lanes = slice(128 * pp, 128 * pp + 128)
            slabs = (q_ref[:, lanes], kp_ref[:, lanes].astype(BF16), kc_ref[:, lanes].astype(BF16),
                     vp_ref[:, lanes].astype(BF16), vc_ref[:, lanes].astype(BF16))
            gens += [head(2 * pp, *slabs), head(2 * pp + 1, *slabs)]
        _interleave(gens)

    cur = pl.BlockSpec((BLK, 512), lambda j: (j, 0))
    prev = pl.BlockSpec((BLK, 512), lambda j: (jnp.maximum(j - 1, 0), 0))
    vcur = pl.BlockSpec((BLK, 512), lambda j: (j, v_col))
    vprev = pl.BlockSpec((BLK, 512), lambda j: (jnp.maximum(j - 1, 0), v_col))
    return pl.pallas_call(
        body, grid=(nblk,),
        in_specs=[cur, prev, cur, vprev, vcur,
                  pl.BlockSpec((1, ATT_HEADS, BLK, 2 * BLK), lambda j: (p_idx, 0, 0, 0))],
        out_specs=[cur, cur],
        out_shape=[_sds((s, 512))] * 2,
        compiler_params=_cp(("parallel",)), name=name)(q, k, k, v, v, bias)


def _post_fwd(o_dn, proj, o_pats, lse_pats, dnw_t, bd128):
    s = o_dn.shape[0]

    def body(o_ref, z_ref, gate_ref, o1_ref, o4_ref, o16_ref, s1_ref, s4_ref, s16_ref, wn_ref, bd_ref,
             mixed_ref, oatt_ref, l1_ref, l4_ref, l16_ref, scr_a, scr_b, scr_c, scr_d):
        o = o_ref[...]
        z = z_ref[...]
        rstd = lax.rsqrt(_segsum(o * o, bd_ref[...]) * (1.0 / DN_HD) + EPS)
        y_dn = o * rstd * wn_ref[...] * (z * _sigmoid(z))
        mixed_ref[:, 0:512] = y_dn.astype(BF16)
        lses = (s1_ref[...], _from_pattern(s4_ref, R4, scr_a), _from_pattern(s16_ref, R16, scr_b))
        outs = (o1_ref[...], _from_pattern(o4_ref, R4, scr_c), _from_pattern(o16_ref, R16, scr_d))
        m = jnp.maximum(jnp.maximum(lses[0], lses[1]), lses[2])
        tot = jnp.exp(lses[0] - m) + jnp.exp(lses[1] - m) + jnp.exp(lses[2] - m)
        big_l = m + jnp.log(tot)
        acc = jnp.zeros_like(m)
        for lse_p, o_p in zip(lses, outs):
            acc = acc + jnp.exp(lse_p - big_l) * o_p
        gate = gate_ref[...]
        oatt_ref[...] = acc
        l1_ref[...] = big_l
        mixed_ref[:, 512:1024] = (acc * (gate * _sigmoid(gate))).astype(BF16)
        _to_patterns(big_l, ((l4_ref, R4), (l16_ref, R16)), scr_a)

    blk = pl.BlockSpec((TM, 512), lambda i: (i, 0))
    p4, p16 = _pattern_spec(R4), _pattern_spec(R16)
    return pl.pallas_call(
        body, grid=(s // TM,),
        in_specs=[blk, pl.BlockSpec((TM, 512), lambda i: (i, COL_Z)),
                  pl.BlockSpec((TM, 512), lambda i: (i, COL_GATE)), blk, p4, p16, blk, p4, p16,
                  pl.BlockSpec((1, 512), lambda i: (0, 0)), pl.BlockSpec((512, 512), lambda i: (0, 0))],
        out_specs=[pl.BlockSpec((TM, D_MODEL), lambda i: (i, 0)), blk, blk, p4, p16],
        out_shape=[_sds((s, D_MODEL), BF16), _sds((s, 512)), _sds((s, 512)), _pattern_shape(s, R4),
                   _pattern_shape(s, R16)],
        scratch_shapes=[SLABS] * 4,
        compiler_params=_cp(("parallel",)), name="post_fwd")(o_dn, proj, proj, *o_pats, *lse_pats, dnw_t, bd128)


def _out_fwd(x, mixed, w_out_bf, tgt):
    s = x.shape[0]
    tm = 256

    def body(x_ref, m_ref, w_ref, t_ref, dy_ref, loss_ref):
        i = pl.program_id(0)

        @pl.when(i == 0)
        def _():
            loss_ref[...] = jnp.zeros_like(loss_ref)

        y = x_ref[...] + jnp.dot(m_ref[...], w_ref[...], preferred_element_type=F32)
        err = y - t_ref[...]
        dy_ref[...] = err * (1.0 / D_MODEL)
        part = 0.5 * jnp.sum(jnp.mean(err * err, axis=-1, keepdims=True), axis=0, keepdims=True)
        loss_ref[...] = loss_ref[...] + part

    blk = pl.BlockSpec((tm, D_MODEL), lambda i: (i, 0))
    return pl.pallas_call(
        body, grid=(s // tm,),
        in_specs=[blk, blk, pl.BlockSpec((D_MODEL, D_MODEL), lambda i: (0, 0)), blk],
        out_specs=[blk, pl.BlockSpec((8, 128), lambda i: (0, 0))],
        out_shape=[_sds((s, D_MODEL)), _sds((8, 128))],
        compiler_params=_cp(("arbitrary",)), name="out_fwd")(x, mixed, w_out_bf, tgt)


def _out_bwd(dy, mixed, w_out_bf):
    s = dy.shape[0]
    tm = 256

    def body(dy_ref, m_ref, w_ref, dm_ref, dw_ref):
        i = pl.program_id(0)

        @pl.when(i == 0)
        def _():
            dw_ref[...] = jnp.zeros_like(dw_ref)

        dyb = dy_ref[...].astype(BF16)
        dm_ref[...] = lax.dot_general(dyb, w_ref[...], (((1,), (1,)), ((), ())), preferred_element_type=F32)
        dw_ref[...] = dw_ref[...] + lax.dot_general(m_ref[...], dyb, (((0,), (0,)), ((), ())),
                                                    preferred_element_type=F32)

    blk = pl.BlockSpec((tm, D_MODEL), lambda i: (i, 0))
    full = pl.BlockSpec((D_MODEL, D_MODEL), lambda i: (0, 0))
    return pl.pallas_call(
        body, grid=(s // tm,), in_specs=[blk, blk, full], out_specs=[blk, full],
        out_shape=[_sds((s, D_MODEL)), _sds((D_MODEL, D_MODEL))],
        compiler_params=_cp(("arbitrary",)), name="out_bwd")(dy, mixed, w_out_bf)


def _post_bwd(dmixed, o_dn, proj, o_att, dnw_t, bd128, bd64):
    s = o_dn.shape[0]
    tm = 256

    def body(ddn_ref, datt_ref, o_ref, z_ref, gate_ref, oatt_ref, wn_ref, bd128_ref, bd64_ref,
             do_ref, dz_ref, dgate_ref, doatt_ref, do4_ref, do16_ref, delta_ref, dl4_ref, dl16_ref, dnw_ref, scr):
        i = pl.program_id(0)

        @pl.when(i == 0)
        def _():
            dnw_ref[...] = jnp.zeros_like(dnw_ref)

        bd128v = bd128_ref[...]
        o = o_ref[...]
        z = z_ref[...]
        wn = wn_ref[...]
        dy = ddn_ref[...]
        rstd = lax.rsqrt(_segsum(o * o, bd128v) * (1.0 / DN_HD) + EPS)
        nrm = o * rstd
        sz = _sigmoid(z)
        dz_ref[...] = dy * nrm * wn * _silu_grad(z, sz)
        dn = dy * z * sz
        gw = dn * wn
        do_ref[...] = rstd * (gw - nrm * (_segsum(gw * nrm, bd128v) * (1.0 / DN_HD)))
        colsum = jnp.sum(dn * nrm, axis=0, keepdims=True)
        fold = colsum[:, 0:128] + colsum[:, 128:256] + colsum[:, 256:384] + colsum[:, 384:512]
        dnw_ref[...] = dnw_ref[...] + fold
        dya = datt_ref[...]
        gate = gate_ref[...]
        oatt = oatt_ref[...]
        sg = _sigmoid(gate)
        dgate_ref[...] = dya * oatt * _silu_grad(gate, sg)
        doa = dya * gate * sg
        doatt_ref[...] = doa
        delta = _segsum(doa * oatt, bd64_ref[...])
        delta_ref[...] = delta
        _to_patterns(doa, ((do4_ref, R4), (do16_ref, R16)), scr)
        _to_patterns(delta, ((dl4_ref, R4), (dl16_ref, R16)), scr)

    blk = pl.BlockSpec((tm, 512), lambda i: (i, 0))
    cst = pl.BlockSpec((512, 512), lambda i: (0, 0))
    p4, p16 = _pattern_spec(R4), _pattern_spec(R16)
    return pl.pallas_call(
        body, grid=(s // tm,),
        in_specs=[blk, pl.BlockSpec((tm, 512), lambda i: (i, 1)), blk,
                  pl.BlockSpec((tm, 512), lambda i: (i, COL_Z)), pl.BlockSpec((tm, 512), lambda i: (i, COL_GATE)),
                  blk, pl.BlockSpec((1, 512), lambda i: (0, 0)), cst, cst],
        out_specs=[blk, blk, blk, blk, p4, p16, blk, p4, p16, pl.BlockSpec((8, 128), lambda i: (0, 0))],
        out_shape=[_sds((s, 512))] * 4 + [_pattern_shape(s, R4), _pattern_shape(s, R16), _sds((s, 512)),
                                          _pattern_shape(s, R4), _pattern_shape(s, R16), _sds((8, 128))],
        scratch_shapes=[SLABS],
        compiler_params=_cp(("arbitrary",)), name="post_bwd")(dmixed, dmixed, o_dn, proj, proj, o_att, dnw_t,
                                                               bd128, bd64)


def _att_bwd(q, k, v, v_col, do, big_l, delta, bias, p_idx, r, name):
    s = q.shape[0]
    nblk = s // BLK
    nseq = nblk // r

    def body(q_ref, kp_ref, kc_ref, vp_ref, vc_ref, do_ref, l_ref, dl_ref, b_ref,
             dq_ref, dk_ref, dv_ref, ds_ref, dkc_ref, dvc_ref):
        j = pl.program_id(0)
        mask_prev, mask_cur, lo_half = _att_masks((j % nseq) != 0)

        @pl.when(j == 0)
        def _():
            dkc_ref[...] = jnp.zeros_like(dkc_ref)
            dvc_ref[...] = jnp.zeros_like(dvc_ref)
            ds_ref[...] = jnp.zeros_like(ds_ref)

        @pl.when(j < nblk)
        def _():
            def head(h, q2, do2, kp2, kc2, vp2, vc2):
                out_l, pair_l = _head_lanes(h)
                sel = lo_half if h % 2 == 0 else jnp.logical_not(lo_half)
                qm = jnp.where(sel, q2, 0.0).astype(BF16)
                dom = jnp.where(sel, do2, 0.0).astype(BF16)
                s_prev, s_cur = _att_scores(qm, kp2, kc2, b_ref[0, h], mask_prev, mask_cur)
                dp_prev = _dot_nt(dom, vp2)
                dp_cur = _dot_nt(dom, vc2)
                yield
                lh = l_ref[:, ATT_HD * h:ATT_HD * h + 1]
                dh = dl_ref[:, ATT_HD * h:ATT_HD * h + 1]
                p_prev = jnp.exp(s_prev - lh)
                p_cur = jnp.exp(s_cur - lh)
                ds_prev = p_prev * (dp_prev - dh)
                ds_cur = p_cur * (dp_cur - dh)
                ds_ref[h, :, 0:BLK] = ds_ref[h, :, 0:BLK] + ds_prev
                ds_ref[h, :, BLK:2 * BLK] = ds_ref[h, :, BLK:2 * BLK] + ds_cur
                dsb_prev, dsb_cur = ds_prev.astype(BF16), ds_cur.astype(BF16)
                pb_prev, pb_cur = p_prev.astype(BF16), p_cur.astype(BF16)
                yield
                qb, dob = q2.astype(BF16), do2.astype(BF16)
                dq_ref[:, out_l] = (_dot(dsb_prev, kp2) + _dot(dsb_cur, kc2))[:, pair_l]
                dk_ref[:, out_l] = dkc_ref[:, out_l] + _dot_tn(dsb_prev, qb)[:, pair_l]
                dv_ref[:, out_l] = dvc_ref[:, out_l] + _dot_tn(pb_prev, dob)[:, pair_l]
                yield
                dkc_ref[:, out_l] = _dot_tn(dsb_cur, qb)[:, pair_l]
                dvc_ref[:, out_l] = _dot_tn(pb_cur, dob)[:, pair_l]

            gens = []
            for pp in range(ATT_HEADS // 2):
                lanes = slice(128 * pp, 128 * pp + 128)
                slabs = (q_ref[:, lanes], do_ref[:, lanes], kp_ref[:, lanes].astype(BF16),
                         kc_ref[:, lanes].astype(BF16), vp_ref[:, lanes].astype(BF16),
                         vc_ref[:, lanes].astype(BF16))
                gens += [head(2 * pp, *slabs), head(2 * pp + 1, *slabs)]
            _interleave(gens)

        @pl.when(j == nblk)
        def _():
            dk_ref[...] = dkc_ref[...]
            dv_ref[...] = dvc_ref[...]

    last = nblk - 1
    cur = pl.BlockSpec((BLK, 512), lambda j: (jnp.minimum(j, last), 0))
    prev = pl.BlockSpec((BLK, 512), lambda j: (jnp.clip(j - 1, 0, last), 0))
    vcur = pl.BlockSpec((BLK, 512), lambda j: (jnp.minimum(j, last), v_col))
    vprev = pl.BlockSpec((BLK, 512), lambda j: (jnp.clip(j - 1, 0, last), v_col))
    return pl.pallas_call(
        body, grid=(nblk + 1,),
        in_specs=[cur, prev, cur, vprev, vcur, cur, cur, cur,
                  pl.BlockSpec((1, ATT_HEADS, BLK, 2 * BLK), lambda j: (p_idx, 0, 0, 0))],
        out_specs=[cur, prev, prev, pl.BlockSpec((ATT_HEADS, BLK, 2 * BLK), lambda j: (0, 0, 0))],
        out_shape=[_sds((s, 512))] * 3 + [_sds((ATT_HEADS, BLK, 2 * BLK))],
        scratch_shapes=[pltpu.VMEM((BLK, 512), F32), pltpu.VMEM((BLK, 512), F32)],
        compiler_params=_cp(("arbitrary",)), name=name)(q, k, k, v, v, do, big_l, delta, bias)


def _att_pre_bwd(dq_pats, dk_pats, dv_pats, proj, qw_t, kw_t, bd64):
    s = proj.shape[0]
    tm = TM

    def body(dq1_ref, dq4_ref, dq16_ref, dk1_ref, dk4_ref, dk16_ref, dv1_ref, dv4_ref, dv16_ref,
             q_ref, k_ref, qw_ref, kw_ref, bd_ref,
             dqr_ref, dkr_ref, dvr_ref, dqw_ref, dkw_ref, scr4, scr16):
        i = pl.program_id(0)

        @pl.when(i == 0)
        def _():
            dqw_ref[...] = jnp.zeros_like(dqw_ref)
            dkw_ref[...] = jnp.zeros_like(dkw_ref)

        bd = bd_ref[...]

        def total(d1_ref, d4_ref, d16_ref):
            return d1_ref[...] + _from_pattern(d4_ref, R4, scr4) + _from_pattern(d16_ref, R16, scr16)

        def one(d_refs, x_ref, w_ref, scale, dx_ref, dw_ref):
            dy = total(*d_refs) * scale
            x = x_ref[...]
            rstd = lax.rsqrt(_segsum(x * x, bd) * (1.0 / ATT_HD) + EPS)
            nrm = x * rstd
            dw_ref[...] = dw_ref[...] + jnp.sum(dy * nrm, axis=0, keepdims=True)
            g = dy * w_ref[...]
            dx_ref[...] = rstd * (g - nrm * (_segsum(g * nrm, bd) * (1.0 / ATT_HD)))

        one((dq1_ref, dq4_ref, dq16_ref), q_ref, qw_ref, ATT_HD ** -0.5, dqr_ref, dqw_ref)
        one((dk1_ref, dk4_ref, dk16_ref), k_ref, kw_ref, 1.0, dkr_ref, dkw_ref)
        dvr_ref[...] = total(dv1_ref, dv4_ref, dv16_ref)

    blk = pl.BlockSpec((tm, 512), lambda i: (i, 0))
    pats = [blk, _pattern_spec(R4), _pattern_spec(R16)]
    row = pl.BlockSpec((1, 512), lambda i: (0, 0))
    acc = pl.BlockSpec((8, 512), lambda i: (0, 0))
    return pl.pallas_call(
        body, grid=(s // tm,),
        in_specs=pats * 3 + [pl.BlockSpec((tm, 512), lambda i: (i, COL_ATT_Q)),
                             pl.BlockSpec((tm, 512), lambda i: (i, COL_ATT_K)), row, row,
                             pl.BlockSpec((512, 512), lambda i: (0, 0))],
        out_specs=[blk, blk, blk, acc, acc],
        out_shape=[_sds((s, 512))] * 3 + [_sds((8, 512))] * 2,
        scratch_shapes=[SLABS] * 2,
        compiler_params=_cp(("arbitrary",)), name="att_pre_bwd")(*dq_pats, *dk_pats, *dv_pats, proj, proj,
                                                                  qw_t, kw_t, bd64)


def _dn_scan_bwd(do, sp, qd, kt, w, vn, attn, gl):
    s = do.shape[0]
    nc = s // CHUNK

    def body(do_ref, sp_ref, qd_ref, kt_ref, w_ref, vn_ref, attn_ref, gl_ref,
             du_ref, dqd_ref, dkt_ref, dw_ref, dattn_ref, dgl_ref, ds_ref):
        n = pl.program_id(0)

        @pl.when(n == 0)
        def _():
            ds_ref[...] = jnp.zeros_like(ds_ref)

        def chain(cc, h):
            rows = slice(CHUNK * cc, CHUNK * cc + CHUNK)
            lanes = slice(128 * h, 128 * h + 128)
            dsn = ds_ref[h]
            st = sp_ref[cc, h]
            dsb, stb = dsn.astype(BF16), st.astype(BF16)
            dob = do_ref[rows, lanes].astype(BF16)
            vnb = vn_ref[rows, lanes].astype(BF16)
            dvn = _dot_tn(attn_ref[cc, h].astype(BF16), dob) + _dot(kt_ref[rows, lanes].astype(BF16), dsb)
            du_ref[rows, lanes] = dvn
            dqd_ref[rows, lanes] = _dot_nt(dob, stb)
            dattn_ref[cc, h] = _dot_nt(dob, vnb)
            dkt_ref[rows, lanes] = _dot_nt(vnb, dsb)
            tot = jnp.sum(jnp.sum(st * dsn, axis=1, keepdims=True), axis=0, keepdims=True)
            dgl_ref[cc, h] = jnp.broadcast_to(tot, (1, 128))
            qdo = _dot_tn(qd_ref[rows, lanes].astype(BF16), dob)
            yield
            dvb = dvn.astype(BF16)
            dw_ref[rows, lanes] = -_dot_nt(dvb, stb)
            ds_ref[h] = qdo + dsn * gl_ref[cc, h] - _dot_tn(w_ref[rows, lanes].astype(BF16), dvb)

        for cc in reversed(range(CPS)):
            _interleave([chain(cc, h) for h in range(DN_HEADS)])

    nsteps = nc // CPS
    big = pl.BlockSpec((CPS * CHUNK, 512), lambda n: (nsteps - 1 - n, 0))
    sq = pl.BlockSpec((CPS, DN_HEADS, CHUNK, CHUNK), lambda n: (nsteps - 1 - n, 0, 0, 0))
    glb = pl.BlockSpec((CPS, DN_HEADS, 1, 128), lambda n: (nsteps - 1 - n, 0, 0, 0))
    return pl.pallas_call(
        body, grid=(nsteps,),
        in_specs=[big, pl.BlockSpec((CPS, DN_HEADS, DN_HD, DN_HD), lambda n: (nsteps - 1 - n, 0, 0, 0)),
                  big, big, big, big, sq, glb],
        out_specs=[big, big, big, big, sq, glb],
        out_shape=[_sds((s, 512))] * 4 + [_sds((nc, DN_HEADS, CHUNK, CHUNK)), _sds((nc, DN_HEADS, 1, 128))],
        scratch_shapes=[pltpu.VMEM((DN_HEADS, DN_HD, DN_HD), F32)],
        compiler_params=_cp(("arbitrary",)), name="dn_scan_bwd")(do, sp, qd, kt, w, vn, attn, gl)


def _dn_prep_bwd(qn, kn, v, bg, bgt, tri, sel, t_inv, attn, u, w, du, dw, dqd, dkt, dattn, dgl):
    s = qn.shape[0]
    nc = s // CHUNK

    def body(q_ref, k_ref, v_ref, bg_ref, bgt_ref, tri_ref, sel_ref, t_ref, attn_ref, u_ref, w_ref,
             du_ref, dw_ref, dqd_ref, dkt_ref, dattn_ref, dgl_ref,
             dq_ref, dk_ref, dv_ref, dbg_ref):
        tri_v = tri_ref[...]
        lane = lax.broadcasted_iota(jnp.int32, (CHUNK, 128), 1)
        rowi = lax.broadcasted_iota(jnp.int32, (CHUNK, 128), 0)
        ones_b = jnp.ones((CHUNK, 128), BF16)
        ones_sq = jnp.ones((128, 128), BF16)
        parts = [[] for _ in range(CPS)]

        def chain(cc, h, bg3, gc3, gc_row):
            rows = slice(CHUNK * cc, CHUNK * cc + CHUNK)
            lanes = slice(128 * h, 128 * h + 128)
            gcc, beta, incl, strict, decay, gl = _chunk_common(bg3, gc3, gc_row, h, sel_ref)
            yield
            q = q_ref[rows, lanes]
            k = k_ref[rows, lanes]
            vv = v_ref[rows, lanes]
            ts = _split(t_ref[cc, h])
            egc = jnp.exp(gcc)
            kb = k * beta
            a_mat = jnp.where(strict, _mm_nt(kb, k) * decay, 0.0)
            dvb = _mm3_tn(ts, _split(du_ref[rows, lanes]))
            dkbg = _mm3_tn(ts, _split(dw_ref[rows, lanes]))
            yield
            d_a = jnp.where(strict, -(_mm_nt(dvb, u_ref[rows, lanes]) + _mm_nt(dkbg, w_ref[rows, lanes])), 0.0)
            d_m = d_a * decay
            dattn_m = jnp.where(incl, dattn_ref[cc, h], 0.0)
            dqk = dattn_m * decay
            e_hi, e_lo = _split(d_a * a_mat + dattn_m * attn_ref[cc, h])
            yield
            dkb = _mm(d_m, k)
            dk = _mm_tn(d_m, kb) + _mm_tn(dqk, q)
            dq = _mm(dqk, k)
            e_colsum = _dot_tn(e_hi, ones_b) + _dot_tn(e_lo, ones_b)
            e_rowsum = _dot(e_hi, ones_b) + _dot(e_lo, ones_b)
            dqd = dqd_ref[rows, lanes]
            dkt = dkt_ref[rows, lanes]
            s_dqd = _rowsum_b(dqd * q, ones_sq)
            s_dkt = _rowsum_b(dkt * k, ones_sq)
            rk = _rowsum_b(dkbg * k, ones_sq)
            s_dkb = _rowsum_b(dkb * k, ones_sq)
            s_dvb = _rowsum_b(dvb * vv, ones_sq)
            yield
            tail = jnp.exp(gl - gcc)
            r = s_dkt * tail
            dgl_tot = jnp.sum(r, axis=0, keepdims=True) + dgl_ref[cc, h] * jnp.exp(gl)
            dgc = e_rowsum - e_colsum + s_dqd * egc - r + rk * beta * egc
            dgc = dgc + jnp.where(rowi == CHUNK - 1, dgl_tot, 0.0)
            dq_ref[rows, lanes] = dq + dqd * egc
            dk_ref[rows, lanes] = dk + dkt * tail + dkbg * (beta * egc) + dkb * beta
            dv_ref[rows, lanes] = dvb * beta
            parts[cc].append((h, dgc, rk * egc + s_dkb + s_dvb))

        gens = []
        for cc in range(CPS):
            bgv = bg_ref[CHUNK * cc:CHUNK * cc + CHUNK, :]
            bg3, gc3, gc_row = _chunk_cumsum(bgv, bgt_ref[cc], tri_v)
            gens += [chain(cc, h, bg3, gc3, gc_row) for h in range(DN_HEADS)]
        _interleave(gens)
        for cc in range(CPS):
            dgc_mat = jnp.zeros((CHUNK, 128), F32)
            dbeta_mat = jnp.zeros((CHUNK, 128), F32)
            for h, dgc, dbeta in parts[cc]:
                dgc_mat = dgc_mat + jnp.where(lane == DN_HEADS + h, dgc, 0.0)
                dbeta_mat = dbeta_mat + jnp.where(lane == h, dbeta, 0.0)
            dbg_ref[CHUNK * cc:CHUNK * cc + CHUNK, :] = _mmx_tn(tri_v, dgc_mat) + dbeta_mat

    big = pl.BlockSpec((CPS * CHUNK, 512), lambda n: (n, 0))
    sq = pl.BlockSpec((CPS, DN_HEADS, CHUNK, CHUNK), lambda n: (n, 0, 0, 0))
    glb = pl.BlockSpec((CPS, DN_HEADS, 1, 128), lambda n: (n, 0, 0, 0))
    small = pl.BlockSpec((CPS * CHUNK, 128), lambda n: (n, 0))
    return pl.pallas_call(
        body, grid=(nc // CPS,),
        in_specs=[big, big, big, small, pl.BlockSpec((CPS, 8, CHUNK), lambda n: (n, 0, 0)),
                  pl.BlockSpec((CHUNK, CHUNK), lambda n: (0, 0)),
                  pl.BlockSpec((8, 128, 128), lambda n: (0, 0, 0)), sq, sq, big, big,
                  big, big, big, big, sq, glb],
        out_specs=[big, big, big, small],
        out_shape=[_sds((s, 512))] * 3 + [_sds((s, 128))],
        compiler_params=_cp(("parallel",)), name="dn_prep_bwd")(qn, kn, v, bg, bgt, tri, sel, t_inv, attn, u, w,
                                                                  du, dw, dqd, dkt, dattn, dgl)


def _dn_pre_bwd(dqn, dkn, dv, dbg, proj, conv_w8, alog_l, dtb_l):
    s = proj.shape[0]
    tr = 512
    nh = tr // 8

    def body(dq_ref, dk_ref, dv_ref, dbg_ref, u_ref, halo_ref, ba_ref, w_ref, al_ref, dt_ref,
             dy_ref, dba_ref, dsm_ref):
        i = pl.program_id(0)

        @pl.when(i == 0)
        def _():
            dsm_ref[...] = jnp.zeros_like(dsm_ref)

        keep = (i > 0).astype(F32)
        for c in range(12):
            lanes = slice(128 * c, 128 * c + 128)
            y = _conv_group(u_ref[:, lanes], halo_ref[:, lanes] * keep, w_ref, c)
            sg = _sigmoid(y)
            sv = y * sg
            if c < 8:
                rs = lax.rsqrt(jnp.sum(sv * sv, axis=1, keepdims=True) + EPS)
                n = sv * rs
                if c < 4:
                    dn = dq_ref[:, lanes] * (DN_HD ** -0.5)
                else:
                    dn = dk_ref[:, slice(128 * (c - 4), 128 * (c - 3))]
                dsv = rs * (dn - n * jnp.sum(dn * n, axis=1, keepdims=True))
            else:
                dsv = dv_ref[:, slice(128 * (c - 8), 128 * (c - 7))]
            dy_ref[:, lanes] = dsv * _silu_grad(y, sg)
        _, lane, sig_b, t, nega, g = _beta_g(ba_ref[...], al_ref[...], dt_ref[...])
        dbg = dbg_ref[...]
        da = dbg * nega * _sigmoid(t)
        is_b = lane < DN_HEADS
        is_a = jnp.logical_and(lane >= DN_HEADS, lane < 2 * DN_HEADS)
        dba_ref[...] = jnp.where(is_b, dbg * sig_b * (1.0 - sig_b), jnp.where(is_a, da, 0.0))
        d_alog = jnp.sum(jnp.where(is_a, dbg * g, 0.0), axis=0, keepdims=True)
        d_dtb = jnp.sum(jnp.where(is_a, da, 0.0), axis=0, keepdims=True)
        row = lax.broadcasted_iota(jnp.int32, (8, 128), 0)
        dsm_ref[...] = dsm_ref[...] + jnp.where(row == 0, d_alog, jnp.where(row == 1, d_dtb, 0.0))

    blk = pl.BlockSpec((tr, 512), lambda i: (i, 0))
    return pl.pallas_call(
        body, grid=(s // tr,),
        in_specs=[blk, blk, blk, pl.BlockSpec((tr, 128), lambda i: (i, 0)),
                  pl.BlockSpec((tr, 1536), lambda i: (i, 0)),
                  pl.BlockSpec((8, 1536), lambda i: (jnp.maximum(i * nh - 1, 0), 0)),
                  pl.BlockSpec((tr, 128), lambda i: (i, COL_BA_128)),
                  pl.BlockSpec((8, 1536), lambda i: (0, 0)),
                  pl.BlockSpec((1, 128), lambda i: (0, 0)), pl.BlockSpec((1, 128), lambda i: (0, 0))],
        out_specs=[pl.BlockSpec((tr, 1536), lambda i: (i, 0)), pl.BlockSpec((tr, 128), lambda i: (i, 0)),
                   pl.BlockSpec((8, 128), lambda i: (0, 0))],
        out_shape=[_sds((s, 1536)), _sds((s, 128)), _sds((8, 128))],
        compiler_params=_cp(("arbitrary",)), name="dn_pre_bwd")(dqn, dkn, dv, dbg, proj, proj, proj, conv_w8,
                                                                 alog_l, dtb_l)


def _conv_bwd(dy, proj, conv_w8):
    s = dy.shape[0]
    tr = 512
    nh = tr // 8
    nblk = s // tr

    def body(dy_ref, dyn_ref, u_ref, halo_ref, w_ref, du_ref, dw_ref):
        i = pl.program_id(0)

        @pl.when(i == 0)
        def _():
            dw_ref[...] = jnp.zeros_like(dw_ref)

        keep_prev = (i > 0).astype(F32)
        keep_next = (i < nblk - 1).astype(F32)
        row = lax.broadcasted_iota(jnp.int32, (8, 128), 0)
        for c in range(12):
            lanes = slice(128 * c, 128 * c + 128)
            dyc = dy_ref[:, lanes]
            dcat = jnp.concatenate([dyc, dyn_ref[:, lanes] * keep_next], axis=0)
            xcat = jnp.concatenate([halo_ref[:, lanes] * keep_prev, u_ref[:, lanes]], axis=0)
            du = dyc * w_ref[CONV_W - 1:CONV_W, lanes]
            dwc = jnp.where(row == CONV_W - 1, jnp.sum(dyc * u_ref[:, lanes], axis=0, keepdims=True), 0.0)
            for k in range(1, CONV_W):
                du = du + pltpu.roll(dcat, tr + 8 - k, 0)[0:tr] * w_ref[CONV_W - 1 - k:CONV_W - k, lanes]
                ush = pltpu.roll(xcat, k, 0)[8:8 + tr]
                dwc = dwc + jnp.where(row == CONV_W - 1 - k, jnp.sum(dyc * ush, axis=0, keepdims=True), 0.0)
            du_ref[:, lanes] = du
            dw_ref[:, lanes] = dw_ref[:, lanes] + dwc

    return pl.pallas_call(
        body, grid=(nblk,),
        in_specs=[pl.BlockSpec((tr, 1536), lambda i: (i, 0)),
                  pl.BlockSpec((8, 1536), lambda i: (jnp.minimum((i + 1) * nh, s // 8 - 1), 0)),
                  pl.BlockSpec((tr, 1536), lambda i: (i, 0)),
                  pl.BlockSpec((8, 1536), lambda i: (jnp.maximum(i * nh - 1, 0), 0)),
                  pl.BlockSpec((8, 1536), lambda i: (0, 0))],
        out_specs=[pl.BlockSpec((tr, 1536), lambda i: (i, 0)), pl.BlockSpec((8, 1536), lambda i: (0, 0))],
        out_shape=[_sds((s, 1536)), _sds((8, 1536))],
        compiler_params=_cp(("arbitrary",)), name="conv_bwd")(dy, dy, proj, proj, conv_w8)


def _in_bwd_dx(d_qkv_dn, dz, dq_att, dk_att, dv_att, dgate, dba, w_bf, x, dy, norm_w):
    s = x.shape[0]
    tm = 256

    def body(a_ref, b_ref, c_ref, d_ref, e_ref, f_ref, g_ref, w_ref, x_ref, dy_ref, nw_ref,
             dx_ref, dp_ref, dnw_ref):
        i = pl.program_id(0)

        @pl.when(i == 0)
        def _():
            dnw_ref[...] = jnp.zeros_like(dnw_ref)

        dp = jnp.concatenate([r[...].astype(BF16) for r in (a_ref, b_ref, c_ref, d_ref, e_ref, f_ref, g_ref)],
                             axis=1)
        dp_ref[...] = dp
        dh = _dot(dp, w_ref[...])
        xv = x_ref[...]
        rstd = lax.rsqrt(jnp.mean(xv * xv, axis=-1, keepdims=True) + EPS)
        xh = xv * rstd
        dnw_ref[...] = dnw_ref[...] + jnp.sum(dh * xh, axis=0, keepdims=True)
        g = dh * nw_ref[...]
        dx_ref[...] = rstd * (g - xh * jnp.mean(g * xh, axis=-1, keepdims=True)) + dy_ref[...]

    def blk(n):
        return pl.BlockSpec((tm, n), lambda i: (i, 0))

    return pl.pallas_call(
        body, grid=(s // tm,),
        in_specs=[blk(1536), blk(512), blk(512), blk(512), blk(512), blk(512), blk(128),
                  pl.BlockSpec((D_IN_PAD, D_MODEL), lambda i: (0, 0)), blk(D_MODEL), blk(D_MODEL),
                  pl.BlockSpec((1, D_MODEL), lambda i: (0, 0))],
        out_specs=[blk(D_MODEL), blk(D_IN_PAD), pl.BlockSpec((8, D_MODEL), lambda i: (0, 0))],
        out_shape=[_sds((s, D_MODEL)), _sds((s, D_IN_PAD), BF16), _sds((8, D_MODEL))],
        compiler_params=_cp(("arbitrary",)), name="in_bwd_dx")(d_qkv_dn, dz, dq_att, dk_att, dv_att, dgate, dba,
                                                                w_bf, x, dy, norm_w)


def _in_bwd_dw(x, norm_w, dp):
    s = x.shape[0]
    tm = 256
    tn = D_IN_PAD // 3

    def body(x_ref, nw_ref, dp_ref, dw_ref):
        i = pl.program_id(1)

        @pl.when(i == 0)
        def _():
            dw_ref[...] = jnp.zeros_like(dw_ref)

        xv = x_ref[...]
        rstd = lax.rsqrt(jnp.mean(xv * xv, axis=-1, keepdims=True) + EPS)
        h = (xv * rstd * nw_ref[...]).astype(BF16)
        dw_ref[...] = dw_ref[...] + _dot_tn(dp_ref[...], h)

    return pl.pallas_call(
        body, grid=(3, s // tm),
        in_specs=[pl.BlockSpec((tm, D_MODEL), lambda j, i: (i, 0)), pl.BlockSpec((1, D_MODEL), lambda j, i: (0, 0)),
                  pl.BlockSpec((tm, tn), lambda j, i: (i, j))],
        out_specs=pl.BlockSpec((tn, D_MODEL), lambda j, i: (j, 0)),
        out_shape=_sds((D_IN_PAD, D_MODEL)),
        compiler_params=_cp(("parallel", "arbitrary")), name="in_bwd_dw")(x, norm_w, dp)


def _flat(a):
    return a.reshape(-1, a.shape[-1])


def _as_pattern(a, r):
    return a if r == 1 else a.reshape(r, a.shape[0] // r, a.shape[1])


D_SHARD = D_IN // N_CHIPS
BA_START = 4 * D_DN
BA_PACKED = 4096
S1_HEAD = BA_START - D_SHARD
S1_BA = 2 * D_SHARD - BA_START


def _pack_rows(g):
    pad = jnp.zeros((D_IN_PAD - D_IN, g.shape[2]), g.dtype)
    s2_ba = 2 * DN_HEADS - S1_BA
    return jnp.concatenate([g[0][:D_SHARD], g[1][:S1_HEAD], g[2][s2_ba:D_SHARD], g[3][:D_SHARD],
                            g[1][S1_HEAD:D_SHARD], g[2][:s2_ba], pad], axis=0)


def _unpack_rows(p, rows):
    mid = BA_PACKED + S1_BA
    pad = jnp.zeros((rows - D_SHARD, p.shape[1]), p.dtype)
    return jnp.stack([jnp.concatenate([p[0:D_SHARD], pad], axis=0),
                      jnp.concatenate([p[D_SHARD:BA_START], p[BA_PACKED:mid], pad], axis=0),
                      jnp.concatenate([p[mid:BA_PACKED + 2 * DN_HEADS], p[BA_START:BA_START + S1_HEAD], pad], axis=0),
                      jnp.concatenate([p[BA_START + S1_HEAD:BA_PACKED], pad], axis=0)])


def _lane_row(vec, offset):
    return jnp.pad(vec.reshape(1, -1), ((0, 0), (offset, 128 - offset - vec.shape[0])))


def _local_step(x, tgt, norm_w, w_bf, conv_w, a_log, dt_bias, dn_norm_w, q_norm_w, k_norm_w, rel_bias, w_out_bf):
    s = x.shape[0]
    nc = s // CHUNK
    conv_w8 = jnp.pad(conv_w, ((0, 8 - CONV_W), (0, 0)))
    alog_l = _lane_row(a_log.reshape(-1), DN_HEADS)
    dtb_l = _lane_row(dt_bias.reshape(-1), DN_HEADS)
    dnw_t = jnp.tile(dn_norm_w.reshape(1, DN_HD), (1, DN_HEADS))
    qw_t = jnp.tile(q_norm_w.reshape(1, ATT_HD), (1, ATT_HEADS))
    kw_t = jnp.tile(k_norm_w.reshape(1, ATT_HD), (1, ATT_HEADS))
    bd128 = _block_ones(512, DN_HD)
    bd64 = _block_ones(512, ATT_HD)
    tri = _tri_incl()
    sel = _lane_select()
    buckets = _bucket_tables()

    proj = _in_proj(x, norm_w, w_bf)
    qn, kn, v_dn, bg = _dn_pre(proj, conv_w8, alog_l, dtb_l)
    bgt = bg[:, 0:8].reshape(nc, CHUNK, 8).transpose(0, 2, 1)
    u, w, qd, kt, attn, t_inv, gl = _dn_prep(qn, kn, v_dn, bg, bgt, tri, sel)
    o_dn, vn, sp = _dn_scan(u, w, qd, kt, attn, gl)
    q1, k1, q4, k4, v4, q16, k16, v16 = _att_pre(proj, qw_t, kw_t, bd64)
    rs = [r for _, r in PATTERNS]
    qkv = [(q1, k1, proj, COL_ATT_V), (_flat(q4), _flat(k4), _flat(v4), 0), (_flat(q16), _flat(k16), _flat(v16), 0)]
    bias = _bias_fwd(rel_bias, buckets)
    o_pats, lse_pats = [], []
    for p, r in enumerate(rs):
        o_p, lse_p = _att_fwd(*qkv[p], bias, p, r, "att_fwd_r%d" % r)
        o_pats.append(_as_pattern(o_p, r))
        lse_pats.append(_as_pattern(lse_p, r))
    mixed, o_att, l1, l4, l16 = _post_fwd(o_dn, proj, o_pats, lse_pats, dnw_t, bd128)
    dy, loss_blk = _out_fwd(x, mixed, w_out_bf, tgt)

    dmixed, d_w_out = _out_bwd(dy, mixed, w_out_bf)
    do_dn, dz, dgate, do1, do4, do16, dl1, dl4, dl16, d_dnw = _post_bwd(dmixed, o_dn, proj, o_att, dnw_t, bd128, bd64)
    side = [(do1, l1, dl1), (_flat(do4), _flat(l4), _flat(dl4)), (_flat(do16), _flat(l16), _flat(dl16))]
    dq_pats, dk_pats, dv_pats, ds_accs = [], [], [], []
    for p, r in enumerate(rs):
        dq_p, dk_p, dv_p, ds_p = _att_bwd(*qkv[p], *side[p], bias, p, r, "att_bwd_r%d" % r)
        dq_pats.append(_as_pattern(dq_p, r))
        dk_pats.append(_as_pattern(dk_p, r))
        dv_pats.append(_as_pattern(dv_p, r))
        ds_accs.append(ds_p)
    d_rel_bias = _bias_bwd(ds_accs, buckets)
    dq_att, dk_att, dv_att, d_qw, d_kw = _att_pre_bwd(dq_pats, dk_pats, dv_pats, proj, qw_t, kw_t, bd64)
    du, dqd, dkt, dw, dattn, dgl = _dn_scan_bwd(do_dn, sp, qd, kt, w, vn, attn, gl)
    dqn, dkn, dv_dn, dbg = _dn_prep_bwd(qn, kn, v_dn, bg, bgt, tri, sel, t_inv, attn, u, w, du, dw, dqd, dkt, dattn,
                                        dgl)
    dyc, dba, dsm = _dn_pre_bwd(dqn, dkn, dv_dn, dbg, proj, conv_w8, alog_l, dtb_l)
    d_qkv_dn, d_conv8 = _conv_bwd(dyc, proj, conv_w8)
    grad_x, dp, d_nw8 = _in_bwd_dx(d_qkv_dn, dz, dq_att, dk_att, dv_att, dgate, dba, w_bf, x, dy, norm_w)
    d_w_in_t = _in_bwd_dw(x, norm_w, dp)

    grads = dict(
        norm_w=d_nw8[0:1, :],
        w_in_t=d_w_in_t,
        conv_w=d_conv8[0:CONV_W, :],
        a_log=dsm[0:1, DN_HEADS:2 * DN_HEADS],
        dt_bias=dsm[1:2, DN_HEADS:2 * DN_HEADS],
        dn_norm_w=d_dnw[0:1, :],
        q_norm_w=d_qw[0:1, :].reshape(ATT_HEADS, ATT_HD),
        k_norm_w=d_kw[0:1, :].reshape(ATT_HEADS, ATT_HD),
        rel_bias=d_rel_bias,
        w_out=d_w_out,
    )
    return loss_blk[0, 0], grad_x, grads


MESH_ID = pl.DeviceIdType.MESH
ANY = pl.BlockSpec(memory_space=pl.ANY)


def _position():
    return lax.axis_index("x"), lax.axis_index("y"), lax.axis_index("c")


def _other_chips(x, y):
    return [(1 - x, y), (x, 1 - y), (1 - x, 1 - y)]


SHARD_PAD = 1040
WIN = 528


def _row_split(n):
    return (n // 2) // 128 * 128


def _part(ref, cc):
    n = ref.shape[0]
    sp = _row_split(n)
    return ref.at[pl.ds(0, sp)] if cc == 0 else ref.at[pl.ds(sp, n - sp)]


def _gather_weights(wt_s, w_out_s, conv_s):
    def body(a_ref, b_ref, c_ref, ga_ref, gb_ref, gc_ref, send_sems, recv_sems, loc_sems, a_vmem, b_vmem):
        x, y, c = _position()
        me = 2 * x + y
        sib = (x, y, 1 - c)
        big = ((a_ref, ga_ref), (b_ref, gb_ref))
        stage_in = [pltpu.make_async_copy(a_ref, a_vmem, loc_sems.at[0]),
                    pltpu.make_async_copy(b_ref, b_vmem, loc_sems.at[1])]
        local = [pltpu.make_async_copy(a_vmem, ga_ref.at[me], loc_sems.at[0]),
                 pltpu.make_async_copy(b_vmem, gb_ref.at[me], loc_sems.at[1]),
                 pltpu.make_async_copy(c_ref, gc_ref.at[me], loc_sems.at[2])]
        for cp in stage_in:
            cp.start()
        local[2].start()
        for cp in stage_in:
            cp.wait()
        for cp in local[:2]:
            cp.start()
        others = _other_chips(x, y)

        def exchange(cc):
            sends = []
            for j, (px, py) in enumerate(others):
                for t, (src, dst) in enumerate(big):
                    k = 3 * j + t
                    sends.append(pltpu.make_async_remote_copy(
                        src_ref=_part(src, cc), dst_ref=_part(dst.at[me], cc), send_sem=send_sems.at[k],
                        recv_sem=recv_sems.at[k], device_id=(px, py, c), device_id_type=MESH_ID))
                sends.append(pltpu.make_async_remote_copy(
                    src_ref=c_ref, dst_ref=gc_ref.at[me], send_sem=send_sems.at[3 * j + 2],
                    recv_sem=recv_sems.at[3 * j + 2], device_id=(px, py, c), device_id_type=MESH_ID))
            for cp in sends:
                cp.start()
            for j, (px, py) in enumerate(others):
                src_chip = 2 * px + py
                for t, (src, dst) in enumerate(big):
                    landed = _part(dst.at[src_chip], cc)
                    pltpu.make_async_remote_copy(
                        src_ref=_part(src, cc), dst_ref=landed, send_sem=send_sems.at[3 * j + t],
                        recv_sem=recv_sems.at[3 * j + t], device_id=(px, py, c), device_id_type=MESH_ID).wait_recv()
                    k = 9 + 2 * j + t
                    fwd = pltpu.make_async_remote_copy(
                        src_ref=landed, dst_ref=landed, send_sem=send_sems.at[k], recv_sem=recv_sems.at[k],
                        device_id=sib, device_id_type=MESH_ID)
                    fwd.start()
                    sends.append(fwd)
                pltpu.make_async_remote_copy(
                    src_ref=c_ref, dst_ref=gc_ref.at[src_chip], send_sem=send_sems.at[3 * j + 2],
                    recv_sem=recv_sems.at[3 * j + 2], device_id=(px, py, c), device_id_type=MESH_ID).wait_recv()
            for j, (px, py) in enumerate(others):
                src_chip = 2 * px + py
                for t, (src, dst) in enumerate(big):
                    k = 9 + 2 * j + t
                    theirs = _part(dst.at[src_chip], 1 - cc)
                    pltpu.make_async_remote_copy(
                        src_ref=theirs, dst_ref=theirs, send_sem=send_sems.at[k], recv_sem=recv_sems.at[k],
                        device_id=sib, device_id_type=MESH_ID).wait_recv()
            for cp in sends:
                cp.wait_send()

        for cc in (0, 1):
            pl.when(c == cc)(functools.partial(exchange, cc))
        for cp in local:
            cp.wait()

    srcs = (wt_s, w_out_s, conv_s)
    n_sem = 9 + 6
    return pl.pallas_call(
        body, in_specs=[ANY] * 3, out_specs=[ANY] * 3,
        out_shape=[_sds((N_CHIPS,) + a.shape, a.dtype) for a in srcs],
        scratch_shapes=[pltpu.SemaphoreType.DMA((n_sem,)), pltpu.SemaphoreType.DMA((n_sem,)),
                        pltpu.SemaphoreType.DMA((3,)), pltpu.VMEM(wt_s.shape, wt_s.dtype),
                        pltpu.VMEM(w_out_s.shape, w_out_s.dtype)],
        compiler_params=_cp(), name="gather_weights")(*srcs)


def _exchange_grads(gw_in4, gw_out4, small):
    n_big = 2
    n_dev = 8

    def body(a_ref, b_ref, s_ref, ra_ref, rb_ref, rs_ref, send_sems, recv_sems, loc_sem):
        x, y, c = _position()
        dev = 4 * x + 2 * y + c
        pairs = ((a_ref, ra_ref), (b_ref, rb_ref))
        local = pltpu.make_async_copy(s_ref, rs_ref.at[dev], loc_sem)
        local.start()
        sends = []
        for j, (px, py) in enumerate(_other_chips(x, y)):
            for t, (src, dst) in enumerate(pairs):
                k = n_big * j + t
                sends.append(pltpu.make_async_remote_copy(
                    src_ref=src.at[2 * px + py], dst_ref=dst.at[j], send_sem=send_sems.at[k],
                    recv_sem=recv_sems.at[k], device_id=(px, py, c), device_id_type=MESH_ID))
        flips = [(dx, dy, dc) for dx in (0, 1) for dy in (0, 1) for dc in (0, 1)][1:]
        for f, (dx, dy, dc) in enumerate(flips):
            k = 3 * n_big + f
            peer = (x ^ dx, y ^ dy, c ^ dc)
            sends.append(pltpu.make_async_remote_copy(
                src_ref=s_ref, dst_ref=rs_ref.at[dev], send_sem=send_sems.at[k], recv_sem=recv_sems.at[k],
                device_id=peer, device_id_type=MESH_ID))
        for cp in sends:
            cp.start()
        for j, (px, py) in enumerate(_other_chips(x, y)):
            for t, (src, dst) in enumerate(pairs):
                k = n_big * j + t
                pltpu.make_async_remote_copy(
                    src_ref=src.at[0], dst_ref=dst.at[j], send_sem=send_sems.at[k], recv_sem=recv_sems.at[k],
                    device_id=(px, py, c), device_id_type=MESH_ID).wait_recv()
        for f, (dx, dy, dc) in enumerate(flips):
            k = 3 * n_big + f
            peer = (x ^ dx, y ^ dy, c ^ dc)
            pltpu.make_async_remote_copy(
                src_ref=s_ref, dst_ref=rs_ref.at[4 * peer[0] + 2 * peer[1] + peer[2]], send_sem=send_sems.at[k],
                recv_sem=recv_sems.at[k], device_id=peer, device_id_type=MESH_ID).wait_recv()
        for cp in sends:
            cp.wait_send()
        local.wait()

    n_sem = 3 * n_big + n_dev - 1
    return pl.pallas_call(
        body, in_specs=[ANY] * 3, out_specs=[ANY] * 3,
        out_shape=[_sds((3,) + gw_in4.shape[1:], gw_in4.dtype), _sds((3,) + gw_out4.shape[1:], gw_out4.dtype),
                   _sds((n_dev,) + small.shape, small.dtype)],
        scratch_shapes=[pltpu.SemaphoreType.DMA((n_sem,)), pltpu.SemaphoreType.DMA((n_sem,)),
                        pltpu.SemaphoreType.DMA],
        name="exchange_grads")(gw_in4, gw_out4, small)


def _plane_sum(own, recv, name):
    rows, cols = own.shape
    tr = 128

    def body(o_ref, r_ref, out_ref):
        acc = o_ref[...]
        for j in range(3):
            acc = acc + r_ref[j].astype(F32)
        out_ref[...] = acc

    return pl.pallas_call(
        body, grid=(pl.cdiv(rows, tr),),
        in_specs=[pl.BlockSpec((tr, cols), lambda i: (i, 0)), pl.BlockSpec((3, tr, cols), lambda i: (0, i, 0))],
        out_specs=pl.BlockSpec((tr, cols), lambda i: (i, 0)), out_shape=_sds((rows, cols)),
        compiler_params=_cp(("parallel",)), name=name)(own, recv)


def _sum_pair(a, b, name):
    n, rows, cols = a.shape
    tr = 128

    def body(a_ref, b_ref, s_ref, sb_ref):
        tot = a_ref[...] + b_ref[...]
        s_ref[...] = tot
        sb_ref[...] = tot.astype(BF16)

    blk = pl.BlockSpec((1, tr, cols), lambda k, i: (k, i, 0))
    return pl.pallas_call(
        body, grid=(n, pl.cdiv(rows, tr)), in_specs=[blk, blk], out_specs=[blk, blk],
        out_shape=[_sds(a.shape), _sds(a.shape, BF16)],
        compiler_params=_cp(("parallel", "parallel")), name=name)(a, b)


def _fill_parts(f_in, f_out, rows_in, rows_out):
    def body(a_ref, b_ref, fa_ref, fb_ref, send_sems, recv_sems, loc_sems, a_vmem, b_vmem):
        x, y, c = _position()
        sib = (x, y, 1 - c)
        pairs = ((a_ref, fa_ref), (b_ref, fb_ref))
        stage_in = [pltpu.make_async_copy(a_ref, a_vmem, loc_sems.at[0]),
                    pltpu.make_async_copy(b_ref, b_vmem, loc_sems.at[1])]
        for cp in stage_in:
            cp.start()
        for cp in stage_in:
            cp.wait()

        def fill(cc):
            mine = [_part(dst, cc) for _, dst in pairs]
            srcs = [src.at[pl.ds(0, m.shape[0])] for src, m in zip((a_vmem, b_vmem), mine)]
            local = [pltpu.make_async_copy(s, m, loc_sems.at[t]) for t, (s, m) in enumerate(zip(srcs, mine))]
            sends = [pltpu.make_async_remote_copy(src_ref=s, dst_ref=m, send_sem=send_sems.at[t],
                                                  recv_sem=recv_sems.at[t], device_id=sib, device_id_type=MESH_ID)
                     for t, (s, m) in enumerate(zip(srcs, mine))]
            for cp in local + sends:
                cp.start()
            for t, (_, dst) in enumerate(pairs):
                theirs = _part(dst, 1 - cc)
                pltpu.make_async_remote_copy(src_ref=theirs, dst_ref=theirs, send_sem=send_sems.at[t],
                                             recv_sem=recv_sems.at[t], device_id=sib, device_id_type=MESH_ID).wait_recv()
            for cp in sends:
                cp.wait_send()
            for cp in local:
                cp.wait()

        for cc in (0, 1):
            pl.when(c == cc)(functools.partial(fill, cc))

    return pl.pallas_call(
        body, in_specs=[ANY] * 2, out_specs=[ANY] * 2,
        out_shape=[_sds((rows_in, D_MODEL)), _sds((rows_out, D_MODEL))],
        scratch_shapes=[pltpu.SemaphoreType.DMA((2,)), pltpu.SemaphoreType.DMA((2,)), pltpu.SemaphoreType.DMA((2,)),
                        pltpu.VMEM(f_in.shape, F32), pltpu.VMEM(f_out.shape, F32)],
        compiler_params=_cp(), name="fill_parts")(f_in, f_out)


def _swap_windows(u_in, u_out, win_in, win_out):
    def body(a_ref, b_ref, ra_ref, rb_ref, send_sems, recv_sems):
        x, y, c = _position()
        sib = (x, y, 1 - c)
        cps = []
        for t, (src, dst, win) in enumerate(((a_ref, ra_ref, win_in), (b_ref, rb_ref, win_out))):
            split = _row_split(src.shape[1])
            start = pl.multiple_of((1 - c) * split, split)
            cps.append(pltpu.make_async_remote_copy(
                src_ref=src.at[:, pl.ds(start, win), :], dst_ref=dst, send_sem=send_sems.at[t],
                recv_sem=recv_sems.at[t], device_id=sib, device_id_type=MESH_ID))
        for cp in cps:
            cp.start()
        for cp in cps:
            cp.wait_recv()
        for cp in cps:
            cp.wait_send()

    return pl.pallas_call(
        body, in_specs=[ANY] * 2, out_specs=[ANY] * 2,
        out_shape=[_sds((N_CHIPS, win_in, D_MODEL)), _sds((N_CHIPS, win_out, D_MODEL))],
        scratch_shapes=[pltpu.SemaphoreType.DMA((2,)), pltpu.SemaphoreType.DMA((2,))],
        name="swap_windows")(u_in, u_out)


SMALL_LAYOUT = (("norm_w", 1024), ("conv_w", 6144), ("a_log", 128), ("dt_bias", 128), ("dn_norm_w", 128),
                ("q_norm_w", 512), ("k_norm_w", 512), ("rel_bias", 256))
SMALL_TOTAL = sum(n for _, n in SMALL_LAYOUT)


def _small_offset(name):
    off = 0
    for n, size in SMALL_LAYOUT:
        if n == name:
            return off
        off += size
    raise KeyError(name)


def _pack_small(grads):
    parts = []
    for name, size in SMALL_LAYOUT:
        flat = grads[name].reshape(1, -1)
        parts.append(jnp.pad(flat, ((0, 0), (0, size - flat.shape[1]))))
    return jnp.concatenate(parts, axis=1)


def _sum_small(rows):
    n_dev = rows.shape[0]
    q_off = _small_offset("q_norm_w")
    k_off = _small_offset("k_norm_w")

    def body(r_ref, tot_ref, qk_ref):
        tot = r_ref[0:1, :]
        for d in range(1, n_dev):
            tot = tot + r_ref[d:d + 1, :]
        tot_ref[...] = tot
        for row, off in ((0, q_off), (1, k_off)):
            s4 = tot[:, off:off + 128] + tot[:, off + 128:off + 256] + tot[:, off + 256:off + 384] \
                + tot[:, off + 384:off + 512]
            qk_ref[row:row + 1, :] = s4 + pltpu.roll(s4, ATT_HD, 1)

    return pl.pallas_call(
        body, in_specs=[pl.BlockSpec(memory_space=pltpu.VMEM)],
        out_specs=[pl.BlockSpec(memory_space=pltpu.VMEM)] * 2,
        out_shape=[_sds((1, SMALL_TOTAL)), _sds((2, 128))],
        compiler_params=_cp(), name="sum_small")(rows)


def _adamw_math(w, g, m, v):
    m = ADAM_B1 * m + (1.0 - ADAM_B1) * g
    v = ADAM_B2 * v + (1.0 - ADAM_B2) * (g * g)
    m_hat = m / (1.0 - ADAM_B1 ** ADAM_STEP)
    v_hat = v / (1.0 - ADAM_B2 ** ADAM_STEP)
    delta = -ADAM_LR * (m_hat / (jnp.sqrt(v_hat) + ADAM_EPS) + ADAM_WD * w)
    return delta, m, v


def _adamw_big(g, w, m, v, name):
    rows, cols = w.shape
    tr = 128

    def body(g_ref, w_ref, m_ref, v_ref, go_ref, d_ref, nm_ref, nv_ref):
        g = g_ref[...]
        go_ref[...] = g
        d_ref[...], nm_ref[...], nv_ref[...] = _adamw_math(w_ref[...], g, m_ref[...], v_ref[...])

    blk = pl.BlockSpec((tr, cols), lambda i: (i, 0))
    return pl.pallas_call(
        body, grid=(pl.cdiv(rows, tr),), in_specs=[blk] * 4, out_specs=[blk] * 4,
        out_shape=[_sds((rows, cols))] * 4, compiler_params=_cp(("parallel",)), name=name)(g, w, m, v)


def _adamw_rows(g, w, m, v, name):
    rows, cols = w.shape
    tr = 128

    def body(g_ref, w_ref, m_ref, v_ref, go_ref, d_ref, nm_ref, nv_ref, s_g, s_d, s_m, s_v):
        g = g_ref[...]
        s_g[...] = g
        s_d[...], s_m[...], s_v[...] = _adamw_math(w_ref[...], g, m_ref[...], v_ref[...])
        for i in range(tr):
            for scr, out in ((s_g, go_ref), (s_d, d_ref), (s_m, nm_ref), (s_v, nv_ref)):
                out[i] = scr[i:i + 1, :]

    blk = pl.BlockSpec((tr, cols), lambda i: (i, 0))
    oblk = pl.BlockSpec((tr, 1, cols), lambda i: (i, 0, 0))
    return pl.pallas_call(
        body, grid=(pl.cdiv(rows, tr),), in_specs=[blk] * 4, out_specs=[oblk] * 4,
        out_shape=[_sds((rows, 1, cols))] * 4, scratch_shapes=[pltpu.VMEM((tr, cols), F32)] * 4,
        compiler_params=_cp(("parallel",)), name=name)(g, w, m, v)


def _adamw_small(w, g, m, v, name):
    def body(w_ref, g_ref, m_ref, v_ref, d_ref, nm_ref, nv_ref):
        d_ref[...], nm_ref[...], nv_ref[...] = _adamw_math(w_ref[...], g_ref[...], m_ref[...], v_ref[...])

    vm = pl.BlockSpec(memory_space=pltpu.VMEM)
    return pl.pallas_call(body, in_specs=[vm] * 4, out_specs=[vm] * 3, out_shape=[_sds(w.shape)] * 3,
                          compiler_params=_cp(), name=name)(w, g, m, v)


WEIGHTS = ("norm_w", "w_in", "conv_w", "a_log", "dt_bias", "dn_norm_w", "q_norm_w", "k_norm_w", "rel_bias", "w_out")


def kernel(x, norm_w, w_in, conv_w, a_log, dt_bias, dn_norm_w, q_norm_w, k_norm_w, rel_bias, w_out, loss_target, m_norm_w, m_w_in, m_conv_w, m_a_log, m_dt_bias, m_dn_norm_w, m_q_norm_w, m_k_norm_w, m_rel_bias, m_w_out, v_norm_w, v_w_in, v_conv_w, v_a_log, v_dt_bias, v_dn_norm_w, v_q_norm_w, v_k_norm_w, v_rel_bias, v_w_out):
    xi, yi, ci = _position()
    chip = 2 * xi + yi
    w_loc = dict(norm_w=norm_w, w_in=w_in[0].T, conv_w=conv_w[0], a_log=a_log, dt_bias=dt_bias, dn_norm_w=dn_norm_w,
                 q_norm_w=q_norm_w, k_norm_w=k_norm_w, rel_bias=rel_bias, w_out=w_out[0])
    m_loc = dict(norm_w=m_norm_w, w_in=m_w_in[0].T, conv_w=m_conv_w[0], a_log=m_a_log, dt_bias=m_dt_bias,
                 dn_norm_w=m_dn_norm_w, q_norm_w=m_q_norm_w, k_norm_w=m_k_norm_w, rel_bias=m_rel_bias,
                 w_out=m_w_out[0])
    v_loc = dict(norm_w=v_norm_w, w_in=v_w_in[0].T, conv_w=v_conv_w[0], a_log=v_a_log, dt_bias=v_dt_bias,
                 dn_norm_w=v_dn_norm_w, q_norm_w=v_q_norm_w, k_norm_w=v_k_norm_w, rel_bias=v_rel_bias,
                 w_out=v_w_out[0])

    wt_pad = jnp.pad(w_loc["w_in"].astype(BF16), ((0, SHARD_PAD - D_SHARD), (0, 0)))
    g_in, g_out, g_conv = _gather_weights(wt_pad, w_loc["w_out"].astype(BF16), w_loc["conv_w"])
    wt_full = _pack_rows(g_in)
    w_out_full = g_out.reshape(D_MODEL, D_MODEL)
    conv_full = g_conv.transpose(1, 0, 2).reshape(CONV_W, 3 * D_DN)

    loss_local, grad_x, grads = _local_step(x[0], loss_target[0], norm_w, wt_full, conv_full, a_log, dt_bias,
                                            dn_norm_w, q_norm_w, k_norm_w, rel_bias, w_out_full)
    loss = lax.psum(loss_local, ("x", "y", "c"))

    u_in = _unpack_rows(grads["w_in_t"], SHARD_PAD)
    u_out = grads["w_out"].reshape(N_CHIPS, D_MODEL // N_CHIPS, D_MODEL)
    win_out = u_out.shape[1] // 2
    sib_in, sib_out = _swap_windows(u_in, u_out, WIN, win_out)
    in_mine = lax.dynamic_slice_in_dim(u_in, ci * _row_split(SHARD_PAD), WIN, axis=1)
    out_mine = lax.dynamic_slice_in_dim(u_out, ci * win_out, win_out, axis=1)
    h_in, hb_in = _sum_pair(in_mine, sib_in, "chip_sum_w_in")
    h_out, hb_out = _sum_pair(out_mine, sib_out, "chip_sum_w_out")
    small = _pack_small(grads)
    r_in, r_out, r_small = _exchange_grads(hb_in, hb_out, small)
    f_in = _plane_sum(lax.dynamic_index_in_dim(h_in, chip, 0, keepdims=False), r_in, "shard_sum_w_in")
    f_out = _plane_sum(lax.dynamic_index_in_dim(h_out, chip, 0, keepdims=False), r_out, "shard_sum_w_out")
    full_in, full_out = _fill_parts(f_in, f_out, SHARD_PAD, u_out.shape[1])
    tot_small, qk = _sum_small(r_small.reshape(8, SMALL_TOTAL))

    def small_grad(name, n):
        off = _small_offset(name)
        return tot_small[:, off:off + n]

    conv_all = small_grad("conv_w", CONV_W * 3 * D_DN).reshape(CONV_W, 3 * D_DN)
    g_small = dict(
        norm_w=small_grad("norm_w", D_MODEL),
        conv_w=lax.dynamic_slice_in_dim(conv_all, chip * (3 * D_DN // N_CHIPS), 3 * D_DN // N_CHIPS, axis=1),
        a_log=small_grad("a_log", DN_HEADS),
        dt_bias=small_grad("dt_bias", DN_HEADS),
        dn_norm_w=small_grad("dn_norm_w", DN_HD),
        q_norm_w=qk[0:1, 0:ATT_HD],
        k_norm_w=qk[1:2, 0:ATT_HD],
        rel_bias=small_grad("rel_bias", ATT_HEADS * N_BUCKETS).reshape(ATT_HEADS, N_BUCKETS),
    )

    out_g, out_d, out_m, out_v = {}, {}, {}, {}
    out_g["w_in"], out_d["w_in"], out_m["w_in"], out_v["w_in"] = _adamw_rows(
        full_in, w_loc["w_in"], m_loc["w_in"], v_loc["w_in"], "adamw_w_in")
    out_g["w_out"], out_d["w_out"], out_m["w_out"], out_v["w_out"] = _adamw_big(
        full_out, w_loc["w_out"], m_loc["w_out"], v_loc["w_out"], "adamw_w_out")
    for name in g_small:
        out_g[name] = g_small[name]
        out_d[name], out_m[name], out_v[name] = _adamw_small(w_loc[name], g_small[name], m_loc[name], v_loc[name],
                                                             "adamw_" + name)
    for d in (out_g, out_d, out_m, out_v):
        d["w_in"] = d["w_in"].transpose(1, 2, 0)
        for name in ("conv_w", "w_out"):
            d[name] = d[name][None]
    return (loss, grad_x[None], *[out_g[n] for n in WEIGHTS], *[out_d[n] for n in WEIGHTS],
            *[out_m[n] for n in WEIGHTS], *[out_v[n] for n in WEIGHTS])
```

```python
import functools
import math

import numpy as np
import jax
import jax.numpy as jnp
from jax import lax
from jax.experimental import pallas as pl
from jax.experimental.pallas import tpu as pltpu

F32 = jnp.float32
BF16 = jnp.bfloat16
HI = lax.Precision.HIGHEST

D_MODEL = 1024
D_DN = 512
DN_HEADS = 4
DN_HD = 128
CONV_W = 4
CHUNK = 64
D_ATT = 512
ATT_HEADS = 8
ATT_HD = 64
PATTERNS = ((128, 1), (512, 4), (2048, 16))
N_BUCKETS = 32
MAX_DISTANCE = 2048
D_IN = 4 * D_DN + 2 * DN_HEADS + 4 * D_ATT
D_IN_PAD = 4224
EPS = 1e-6
BLK = 128
NEG = -1e30
N_CHIPS = 4

ADAM_LR = 0.001
ADAM_B1 = 0.9
ADAM_B2 = 0.999
ADAM_EPS = 1e-08
ADAM_WD = 0.01
ADAM_STEP = 10

VMEM_LIMIT = 56 * 1024 * 1024

COL_Z = 3
COL_ATT_Q = 4
COL_ATT_K = 5
COL_ATT_V = 6
COL_GATE = 7
COL_BA_128 = 32


def _cp(sem=None):
    if sem is None:
        return pltpu.CompilerParams(vmem_limit_bytes=VMEM_LIMIT)
    return pltpu.CompilerParams(dimension_semantics=sem, vmem_limit_bytes=VMEM_LIMIT)


def _sds(shape, dtype=F32):
    return jax.ShapeDtypeStruct(shape, dtype)


def _mm(a, b):
    return jnp.dot(a.astype(BF16), b.astype(BF16), preferred_element_type=F32)


def _mm_nt(a, b):
    return lax.dot_general(a.astype(BF16), b.astype(BF16), (((1,), (1,)), ((), ())),
                           preferred_element_type=F32)


def _mm_tn(a, b):
    return lax.dot_general(a.astype(BF16), b.astype(BF16), (((0,), (0,)), ((), ())),
                           preferred_element_type=F32)


def _mmx(a, b):
    return jnp.dot(a, b, precision=HI, preferred_element_type=F32)


def _mmx_nt(a, b):
    return lax.dot_general(a, b, (((1,), (1,)), ((), ())), precision=HI, preferred_element_type=F32)


def _mmx_tn(a, b):
    return lax.dot_general(a, b, (((0,), (0,)), ((), ())), precision=HI, preferred_element_type=F32)


def _dot(a, b):
    return jnp.dot(a, b, preferred_element_type=F32)


def _dot_nt(a, b):
    return lax.dot_general(a, b, (((1,), (1,)), ((), ())), preferred_element_type=F32)


def _dot_tn(a, b):
    return lax.dot_general(a, b, (((0,), (0,)), ((), ())), preferred_element_type=F32)


def _split(a):
    hi = a.astype(BF16)
    return hi, (a - hi.astype(F32)).astype(BF16)


def _mm3(a_s, b_s):
    return _dot(a_s[0], b_s[0]) + _dot(a_s[0], b_s[1]) + _dot(a_s[1], b_s[0])


def _mm3_tn(a_s, b_s):
    return _dot_tn(a_s[0], b_s[0]) + _dot_tn(a_s[0], b_s[1]) + _dot_tn(a_s[1], b_s[0])


def _interleave(gens):
    live = list(gens)
    while live:
        nxt = []
        for g in live:
            try:
                next(g)
                nxt.append(g)
            except StopIteration:
                pass
        live = nxt


def _segsum(x, bd):
    hi = x.astype(BF16)
    r1 = x - hi.astype(F32)
    mid = r1.astype(BF16)
    lo = (r1 - mid.astype(F32)).astype(BF16)
    return (jnp.dot(hi, bd, preferred_element_type=F32) + jnp.dot(mid, bd, preferred_element_type=F32)
            + jnp.dot(lo, bd, preferred_element_type=F32))


def _sigmoid(x):
    return 1.0 / (1.0 + jnp.exp(-x))


def _silu_grad(x, s):
    return s * (1.0 + x * (1.0 - s))


def _block_ones(n, seg):
    i = np.arange(n)
    return jnp.asarray((i[:, None] // seg == i[None, :] // seg).astype(np.float32), dtype=BF16)


def _tri_incl():
    i = np.arange(CHUNK)
    return jnp.asarray((i[:, None] >= i[None, :]).astype(np.float32))


def _t5_bucket(dist):
    max_exact = N_BUCKETS // 2
    d = np.maximum(dist, 1).astype(np.float64)
    large = max_exact + (np.log(d / max_exact) / math.log(MAX_DISTANCE / max_exact)
                         * (N_BUCKETS - max_exact)).astype(np.int32)
    large = np.minimum(large, N_BUCKETS - 1)
    return np.where(dist < max_exact, dist, large).astype(np.int32)


def _bucket_tables():
    qi = np.arange(BLK)[:, None]
    kj = np.arange(2 * BLK)[None, :]
    step = qi - kj + BLK
    return jnp.asarray(np.stack([_t5_bucket(np.clip(step, 0, None) * r) for _, r in PATTERNS]))


def _in_proj(x, norm_w, wt_bf):
    s = x.shape[0]
    tm = 512

    def body(x_ref, nw_ref, w_ref, o_ref):
        xv = x_ref[...]
        rstd = lax.rsqrt(jnp.mean(xv * xv, axis=-1, keepdims=True) + EPS)
        h = (xv * rstd * nw_ref[...]).astype(BF16)
        o_ref[...] = _dot_nt(h, w_ref[...])

    return pl.pallas_call(
        body, grid=(s // tm,),
        in_specs=[pl.BlockSpec((tm, D_MODEL), lambda i: (i, 0)),
                  pl.BlockSpec((1, D_MODEL), lambda i: (0, 0)),
                  pl.BlockSpec((D_IN_PAD, D_MODEL), lambda i: (0, 0))],
        out_specs=pl.BlockSpec((tm, D_IN_PAD), lambda i: (i, 0)),
        out_shape=_sds((s, D_IN_PAD)), compiler_params=_cp(("parallel",)), name="in_proj")(x, norm_w, wt_bf)


def _conv_group(cur, halo, w_ref, c):
    rows = cur.shape[0]
    lanes = slice(128 * c, 128 * c + 128)
    xcat = jnp.concatenate([halo, cur], axis=0)
    y = cur * w_ref[CONV_W - 1:CONV_W, lanes]
    for k in range(1, CONV_W):
        sh = pltpu.roll(xcat, k, 0)[8:8 + rows]
        y = y + sh * w_ref[CONV_W - 1 - k:CONV_W - k, lanes]
    return y


def _beta_g(ba, alog_l, dtb_l):
    lane = lax.broadcasted_iota(jnp.int32, ba.shape, 1)
    sig_b = _sigmoid(ba)
    t = ba + dtb_l
    softplus = jnp.maximum(t, 0.0) + jnp.log(1.0 + jnp.exp(-jnp.abs(t)))
    nega = -jnp.exp(alog_l)
    g = nega * softplus
    out = jnp.where(lane < DN_HEADS, sig_b, jnp.where(lane < 2 * DN_HEADS, g, 0.0))
    return out, lane, sig_b, t, nega, g


def _dn_pre(proj, conv_w8, alog_l, dtb_l):
    s = proj.shape[0]
    tr = 512
    nh = tr // 8

    def body(u_ref, halo_ref, ba_ref, w_ref, al_ref, dt_ref, q_ref, k_ref, v_ref, bg_ref):
        i = pl.program_id(0)
        keep = (i > 0).astype(F32)
        for c in range(12):
            lanes = slice(128 * c, 128 * c + 128)
            y = _conv_group(u_ref[:, lanes], halo_ref[:, lanes] * keep, w_ref, c)
            sv = y * _sigmoid(y)
            if c < 8:
                rs = lax.rsqrt(jnp.sum(sv * sv, axis=1, keepdims=True) + EPS)
                n = sv * rs
                if c < 4:
                    q_ref[:, lanes] = n * (DN_HD ** -0.5)
                else:
                    k_ref[:, slice(128 * (c - 4), 128 * (c - 3))] = n
            else:
                v_ref[:, slice(128 * (c - 8), 128 * (c - 7))] = sv
        bg_ref[...] = _beta_g(ba_ref[...], al_ref[...], dt_ref[...])[0]

    return pl.pallas_call(
        body, grid=(s // tr,),
        in_specs=[pl.BlockSpec((tr, 1536), lambda i: (i, 0)),
                  pl.BlockSpec((8, 1536), lambda i: (jnp.maximum(i * nh - 1, 0), 0)),
                  pl.BlockSpec((tr, 128), lambda i: (i, COL_BA_128)),
                  pl.BlockSpec((8, 1536), lambda i: (0, 0)),
                  pl.BlockSpec((1, 128), lambda i: (0, 0)),
                  pl.BlockSpec((1, 128), lambda i: (0, 0))],
        out_specs=[pl.BlockSpec((tr, 512), lambda i: (i, 0))] * 3 + [pl.BlockSpec((tr, 128), lambda i: (i, 0))],
        out_shape=[_sds((s, 512))] * 3 + [_sds((s, 128))],
        compiler_params=_cp(("parallel",)), name="dn_pre")(proj, proj, proj, conv_w8, alog_l, dtb_l)


CPS = 4


def _split3(a):
    hi = a.astype(BF16)
    r1 = a - hi.astype(F32)
    mid = r1.astype(BF16)
    return hi, mid, (r1 - mid.astype(F32)).astype(BF16)


def _lane_select():
    r = np.arange(128)
    return jnp.asarray((r[None, :, None] == np.arange(8)[:, None, None]) * np.ones((1, 1, 128)), dtype=BF16)


def _lane_bcast(a3, sel):
    return _dot(a3[0], sel) + _dot(a3[1], sel) + _dot(a3[2], sel)


def _rowsum_b(z, ones_b):
    hi, lo = _split(z)
    return _dot(hi, ones_b) + _dot(lo, ones_b)


def _chunk_cumsum(bg, bgt, tri):
    return _split3(bg), _split3(_mmx(tri, bg)), _mmx_nt(bgt, tri)


def _chunk_common(bg3, gc3, gc_row, h, sel_ref):
    gcc = _lane_bcast(gc3, sel_ref[DN_HEADS + h])
    beta = _lane_bcast(bg3, sel_ref[h])
    gcr = gc_row[DN_HEADS + h:DN_HEADS + h + 1, :]
    ii = lax.broadcasted_iota(jnp.int32, (CHUNK, CHUNK), 0)
    jj = lax.broadcasted_iota(jnp.int32, (CHUNK, CHUNK), 1)
    incl = ii >= jj
    strict = ii > jj
    decay = jnp.exp(jnp.where(incl, gcc[:, 0:CHUNK] - gcr, NEG))
    gl = gcc[CHUNK - 1:CHUNK, :]
    return gcc, beta, incl, strict, decay, gl


def _dn_prep(qn, kn, v, bg, bgt, tri, sel):
    s = qn.shape[0]
    nc = s // CHUNK

    def body(q_ref, k_ref, v_ref, bg_ref, bgt_ref, tri_ref, sel_ref,
             u_ref, w_ref, qd_ref, kt_ref, attn_ref, t_ref, gl_ref):
        tri_v = tri_ref[...]
        ii = lax.broadcasted_iota(jnp.int32, (CHUNK, CHUNK), 0)
        jj = lax.broadcasted_iota(jnp.int32, (CHUNK, CHUNK), 1)
        eye = (ii == jj).astype(F32)

        def chain(cc, h, bg3, gc3, gc_row):
            rows = slice(CHUNK * cc, CHUNK * cc + CHUNK)
            lanes = slice(128 * h, 128 * h + 128)
            gcc, beta, incl, strict, decay, gl = _chunk_common(bg3, gc3, gc_row, h, sel_ref)
            yield
            q = q_ref[rows, lanes]
            k = k_ref[rows, lanes]
            vv = v_ref[rows, lanes]
            kb = k * beta
            egc = jnp.exp(gcc)
            a_mat = jnp.where(strict, _mm_nt(kb, k) * decay, 0.0)
            attn_ref[cc, h] = jnp.where(incl, _mm_nt(q, k) * decay, 0.0)
            qd_ref[rows, lanes] = q * egc
            kt_ref[rows, lanes] = k * jnp.exp(gl - gcc)
            gl_ref[cc, h] = jnp.exp(gl)
            yield
            p = -a_mat
            t = eye + p
            for _ in range(5):
                ps = _split(p)
                p = _mm3(ps, ps)
                yield
                t = t + _mm3(_split(t), _split(p))
                yield
            t_ref[cc, h] = t
            ts = _split(t)
            u_ref[rows, lanes] = _mm3(ts, _split(vv * beta))
            w_ref[rows, lanes] = _mm3(ts, _split(kb * egc))

        gens = []
        for cc in range(CPS):
            bgv = bg_ref[CHUNK * cc:CHUNK * cc + CHUNK, :]
            bg3, gc3, gc_row = _chunk_cumsum(bgv, bgt_ref[cc], tri_v)
            gens += [chain(cc, h, bg3, gc3, gc_row) for h in range(DN_HEADS)]
        _interleave(gens)

    rows_step = CPS * CHUNK
    big = pl.BlockSpec((rows_step, 512), lambda n: (n, 0))
    sq = pl.BlockSpec((CPS, DN_HEADS, CHUNK, CHUNK), lambda n: (n, 0, 0, 0))
    return pl.pallas_call(
        body, grid=(nc // CPS,),
        in_specs=[big, big, big, pl.BlockSpec((rows_step, 128), lambda n: (n, 0)),
                  pl.BlockSpec((CPS, 8, CHUNK), lambda n: (n, 0, 0)),
                  pl.BlockSpec((CHUNK, CHUNK), lambda n: (0, 0)),
                  pl.BlockSpec((8, 128, 128), lambda n: (0, 0, 0))],
        out_specs=[big, big, big, big, sq, sq, pl.BlockSpec((CPS, DN_HEADS, 1, 128), lambda n: (n, 0, 0, 0))],
        out_shape=[_sds((s, 512))] * 4 + [_sds((nc, DN_HEADS, CHUNK, CHUNK))] * 2 + [_sds((nc, DN_HEADS, 1, 128))],
        compiler_params=_cp(("parallel",)), name="dn_prep")(qn, kn, v, bg, bgt, tri, sel)


def _dn_scan(u, w, qd, kt, attn, gl):
    s = u.shape[0]
    nc = s // CHUNK

    def body(u_ref, w_ref, qd_ref, kt_ref, attn_ref, gl_ref, o_ref, vn_ref, sp_ref, st_ref):
        n = pl.program_id(0)

        @pl.when(n == 0)
        def _():
            st_ref[...] = jnp.zeros_like(st_ref)

        def chain(cc, h):
            rows = slice(CHUNK * cc, CHUNK * cc + CHUNK)
            lanes = slice(128 * h, 128 * h + 128)
            st = st_ref[h]
            sp_ref[cc, h] = st
            stb = st.astype(BF16)
            ws = _dot(w_ref[rows, lanes].astype(BF16), stb)
            qs = _dot(qd_ref[rows, lanes].astype(BF16), stb)
            yield
            vn = u_ref[rows, lanes] - ws
            vn_ref[rows, lanes] = vn
            vnb = vn.astype(BF16)
            o_ref[rows, lanes] = qs + _dot(attn_ref[cc, h].astype(BF16), vnb)
            st_ref[h] = st * gl_ref[cc, h] + _dot_tn(kt_ref[rows, lanes].astype(BF16), vnb)

        for cc in range(CPS):
            _interleave([chain(cc, h) for h in range(DN_HEADS)])

    big = pl.BlockSpec((CPS * CHUNK, 512), lambda n: (n, 0))
    return pl.pallas_call(
        body, grid=(nc // CPS,),
        in_specs=[big, big, big, big,
                  pl.BlockSpec((CPS, DN_HEADS, CHUNK, CHUNK), lambda n: (n, 0, 0, 0)),
                  pl.BlockSpec((CPS, DN_HEADS, 1, 128), lambda n: (n, 0, 0, 0))],
        out_specs=[big, big, pl.BlockSpec((CPS, DN_HEADS, DN_HD, DN_HD), lambda n: (n, 0, 0, 0))],
        out_shape=[_sds((s, 512)), _sds((s, 512)), _sds((nc, DN_HEADS, DN_HD, DN_HD))],
        scratch_shapes=[pltpu.VMEM((DN_HEADS, DN_HD, DN_HD), F32)],
        compiler_params=_cp(("arbitrary",)), name="dn_scan")(u, w, qd, kt, attn, gl)


R4 = PATTERNS[1][1]
R16 = PATTERNS[2][1]
TM = 512


def _pattern_spec(r):
    return pl.BlockSpec((r, TM // r, 512), lambda i: (0, i, 0))


def _pattern_shape(s, r):
    return _sds((r, s // r, 512))


SLABS = pltpu.VMEM((4, TM, 128), F32)


def _to_patterns(val, dsts, scr):
    for c in range(4):
        lanes = slice(128 * c, 128 * c + 128)
        scr[c] = val[:, lanes]
        for dst_ref, r in dsts:
            for a in range(r):
                dst_ref[a, :, lanes] = scr[c, pl.ds(a, TM // r, stride=r), :]


def _from_pattern(src_ref, r, scr):
    for c in range(4):
        for a in range(r):
            scr[c, pl.ds(a, TM // r, stride=r), :] = src_ref[a, :, 128 * c:128 * c + 128]
    return jnp.concatenate([scr[c] for c in range(4)], axis=1)


def _att_pre(proj, qw_t, kw_t, bd64):
    s = proj.shape[0]

    def body(q_ref, k_ref, v_ref, qw_ref, kw_ref, bd_ref,
             q1_ref, k1_ref, q4_ref, k4_ref, v4_ref, q16_ref, k16_ref, v16_ref, scr):
        bd = bd_ref[...]
        q = q_ref[...]
        k = k_ref[...]
        qn = q * lax.rsqrt(_segsum(q * q, bd) * (1.0 / ATT_HD) + EPS) * qw_ref[...] * (ATT_HD ** -0.5)
        kn = k * lax.rsqrt(_segsum(k * k, bd) * (1.0 / ATT_HD) + EPS) * kw_ref[...]
        q1_ref[...] = qn
        k1_ref[...] = kn
        _to_patterns(qn, ((q4_ref, R4), (q16_ref, R16)), scr)
        _to_patterns(kn, ((k4_ref, R4), (k16_ref, R16)), scr)
        _to_patterns(v_ref[...], ((v4_ref, R4), (v16_ref, R16)), scr)

    row = pl.BlockSpec((1, 512), lambda i: (0, 0))
    tok = pl.BlockSpec((TM, 512), lambda i: (i, 0))
    return pl.pallas_call(
        body, grid=(s // TM,),
        in_specs=[pl.BlockSpec((TM, 512), lambda i: (i, COL_ATT_Q)),
                  pl.BlockSpec((TM, 512), lambda i: (i, COL_ATT_K)),
                  pl.BlockSpec((TM, 512), lambda i: (i, COL_ATT_V)),
                  row, row, pl.BlockSpec((512, 512), lambda i: (0, 0))],
        out_specs=[tok, tok] + [_pattern_spec(R4)] * 3 + [_pattern_spec(R16)] * 3,
        out_shape=[_sds((s, 512))] * 2 + [_pattern_shape(s, R4)] * 3 + [_pattern_shape(s, R16)] * 3,
        scratch_shapes=[SLABS],
        compiler_params=_cp(("parallel",)), name="att_pre")(proj, proj, proj, qw_t, kw_t, bd64)


def _bias_fwd(rel_bias, buckets):
    def body(rb_ref, bk_ref, o_ref):
        for p in range(len(PATTERNS)):
            bk = bk_ref[p]
            for h in range(ATT_HEADS):
                acc = jnp.zeros((BLK, 2 * BLK), F32)
                for b in range(N_BUCKETS):
                    acc = jnp.where(bk == b, rb_ref[h, b], acc)
                o_ref[p, h] = acc

    return pl.pallas_call(
        body,
        in_specs=[pl.BlockSpec(memory_space=pltpu.SMEM), pl.BlockSpec(memory_space=pltpu.VMEM)],
        out_specs=pl.BlockSpec(memory_space=pltpu.VMEM),
        out_shape=_sds((len(PATTERNS), ATT_HEADS, BLK, 2 * BLK)),
        compiler_params=_cp(), name="bias_fwd")(rel_bias, buckets)


def _bias_bwd(ds_accs, buckets):
    def body(ds0_ref, ds1_ref, ds2_ref, bk_ref, o_ref):
        for h in range(ATT_HEADS):
            for b in range(N_BUCKETS):
                tot = jnp.zeros((), F32)
                for p, ds_ref in enumerate((ds0_ref, ds1_ref, ds2_ref)):
                    tot = tot + jnp.sum(jnp.where(bk_ref[p] == b, ds_ref[h], 0.0))
                o_ref[h, b] = tot

    return pl.pallas_call(
        body,
        in_specs=[pl.BlockSpec(memory_space=pltpu.VMEM)] * 4,
        out_specs=pl.BlockSpec(memory_space=pltpu.SMEM),
        out_shape=_sds((ATT_HEADS, N_BUCKETS)),
        compiler_params=_cp(), name="bias_bwd")(*ds_accs, buckets)


QB = 2


def _att_masks(has_prev):
    qi = lax.broadcasted_iota(jnp.int32, (BLK, BLK), 0)
    kj = lax.broadcasted_iota(jnp.int32, (BLK, BLK), 1)
    lane = lax.broadcasted_iota(jnp.int32, (BLK, 2 * ATT_HD), 1)
    return jnp.logical_and(kj >= qi, has_prev), kj <= qi, lane < ATT_HD


def _head_lanes(h):
    half = h % 2
    return slice(ATT_HD * h, ATT_HD * h + ATT_HD), slice(ATT_HD * half, ATT_HD * half + ATT_HD)


def _att_scores(qm, kp2, kc2, bias_h, mask_prev, mask_cur):
    s_prev = jnp.where(mask_prev, _dot_nt(qm, kp2) + bias_h[:, :BLK], NEG)
    s_cur = jnp.where(mask_cur, _dot_nt(qm, kc2) + bias_h[:, BLK:], NEG)
    return s_prev, s_cur


def _att_fwd(q, k, v, v_col, bias, p_idx, r, name):
    s = q.shape[0]
    nblk = s // BLK
    nseq = nblk // r

    def body(q_ref, kp_ref, kc_ref, vp_ref, vc_ref, b_ref, o_ref, lse_ref):
        j = pl.program_id(0)

        def head(h, rows, masks, q2, kp2, kc2, vp2, vc2):
            mask_prev, mask_cur, lo_half = masks
            out_l, pair_l = _head_lanes(h)
            sel = lo_half if h % 2 == 0 else jnp.logical_not(lo_half)
            qm = jnp.where(sel, q2, 0.0).astype(BF16)
            s_prev, s_cur = _att_scores(qm, kp2, kc2, b_ref[0, h], mask_prev, mask_cur)
            yield
            m = jnp.maximum(jnp.max(s_prev, axis=1, keepdims=True), jnp.max(s_cur, axis=1, keepdims=True))
            p_prev = jnp.exp(s_prev - m)
            p_cur = jnp.exp(s_cur - m)
            l = jnp.sum(p_prev, axis=1, keepdims=True) + jnp.sum(p_cur, axis=1, keepdims=True)
            yield
            o2 = _dot(p_prev.astype(BF16), vp2) + _dot(p_cur.astype(BF16), vc2)
            o_ref[rows, out_l] = (o2 * (1.0 / l))[:, pair_l]
            lse_ref[rows, out_l] = jnp.broadcast_to(m + jnp.log(l), (BLK, ATT_HD))

        for sub in range(QB):
            rows = slice(BLK * sub, BLK * sub + BLK)
            before = slice(BLK * (sub - 1), BLK * sub)
            masks = _att_masks(((QB * j + sub) % nseq) != 0)
            gens = []
            for pp in range(ATT_HEADS // 2):
                lanes = slice(128 * pp, 128 * pp + 128)
                kp = kp_ref[:, lanes] if sub == 0 else kc_ref[before, lanes]
                vp = vp_ref[:, lanes] if sub == 0 else vc_ref[before, lanes]
                slabs = (q_ref[rows, lanes], kp.astype(BF16), kc_ref[rows, lanes].astype(BF16),
                         vp.astype(BF16), vc_ref[rows, lanes].astype(BF16))
                gens += [head(2 * pp, rows, masks, *slabs), head(2 * pp + 1, rows, masks, *slabs)]
            _interleave(gens)

    cur = pl.BlockSpec((QB * BLK, 512), lambda j: (j, 0))
    prev = pl.BlockSpec((BLK, 512), lambda j: (jnp.maximum(QB * j - 1, 0), 0))
    vcur = pl.BlockSpec((QB * BLK, 512), lambda j: (j, v_col))
    vprev = pl.BlockSpec((BLK, 512), lambda j: (jnp.maximum(QB * j - 1, 0), v_col))
    return pl.pallas_call(
        body, grid=(nblk // QB,),
        in_specs=[cur, prev, cur, vprev, vcur,
                  pl.BlockSpec((1, ATT_HEADS, BLK, 2 * BLK), lambda j: (p_idx, 0, 0, 0))],
        out_specs=[cur, cur],
        out_shape=[_sds((s, 512))] * 2,
        compiler_params=_cp(("parallel",)), name=name)(q, k, k, v, v, bias)


def _post_fwd(o_dn, proj, o_pats, lse_pats, dnw_t, bd128):
    s = o_dn.shape[0]

    def body(o_ref, z_ref, gate_ref, o1_ref, o4_ref, o16_ref, s1_ref, s4_ref, s16_ref, wn_ref, bd_ref,
             mixed_ref, oatt_ref, l1_ref, l4_ref, l16_ref, scr_a, scr_b, scr_c, scr_d):
        o = o_ref[...]
        z = z_ref[...]
        rstd = lax.rsqrt(_segsum(o * o, bd_ref[...]) * (1.0 / DN_HD) + EPS)
        y_dn = o * rstd * wn_ref[...] * (z * _sigmoid(z))
        mixed_ref[:, 0:512] = y_dn.astype(BF16)
        lses = (s1_ref[...], _from_pattern(s4_ref, R4, scr_a), _from_pattern(s16_ref, R16, scr_b))
        outs = (o1_ref[...], _from_pattern(o4_ref, R4, scr_c), _from_pattern(o16_ref, R16, scr_d))
        m = jnp.maximum(jnp.maximum(lses[0], lses[1]), lses[2])
        tot = jnp.exp(lses[0] - m) + jnp.exp(lses[1] - m) + jnp.exp(lses[2] - m)
        big_l = m + jnp.log(tot)
        acc = jnp.zeros_like(m)
        for lse_p, o_p in zip(lses, outs):
            acc = acc + jnp.exp(lse_p - big_l) * o_p
        gate = gate_ref[...]
        oatt_ref[...] = acc
        l1_ref[...] = big_l
        mixed_ref[:, 512:1024] = (acc * (gate * _sigmoid(gate))).astype(BF16)
        _to_patterns(big_l, ((l4_ref, R4), (l16_ref, R16)), scr_a)

    blk = pl.BlockSpec((TM, 512), lambda i: (i, 0))
    p4, p16 = _pattern_spec(R4), _pattern_spec(R16)
    return pl.pallas_call(
        body, grid=(s // TM,),
        in_specs=[blk, pl.BlockSpec((TM, 512), lambda i: (i, COL_Z)),
                  pl.BlockSpec((TM, 512), lambda i: (i, COL_GATE)), blk, p4, p16, blk, p4, p16,
                  pl.BlockSpec((1, 512), lambda i: (0, 0)), pl.BlockSpec((512, 512), lambda i: (0, 0))],
        out_specs=[pl.BlockSpec((TM, D_MODEL), lambda i: (i, 0)), blk, blk, p4, p16],
        out_shape=[_sds((s, D_MODEL), BF16), _sds((s, 512)), _sds((s, 512)), _pattern_shape(s, R4),
                   _pattern_shape(s, R16)],
        scratch_shapes=[SLABS] * 4,
        compiler_params=_cp(("parallel",)), name="post_fwd")(o_dn, proj, proj, *o_pats, *lse_pats, dnw_t, bd128)


def _out_fwd(x, mixed, w_out_bf, tgt):
    s = x.shape[0]
    tm = 512

    def body(x_ref, m_ref, w_ref, t_ref, dy_ref, loss_ref):
        i = pl.program_id(0)

        @pl.when(i == 0)
        def _():
            loss_ref[...] = jnp.zeros_like(loss_ref)

        y = x_ref[...] + jnp.dot(m_ref[...], w_ref[...], preferred_element_type=F32)
        err = y - t_ref[...]
        dy_ref[...] = err * (1.0 / D_MODEL)
        part = 0.5 * jnp.sum(jnp.mean(err * err, axis=-1, keepdims=True), axis=0, keepdims=True)
        loss_ref[...] = loss_ref[...] + part

    blk = pl.BlockSpec((tm, D_MODEL), lambda i: (i, 0))
    return pl.pallas_call(
        body, grid=(s // tm,),
        in_specs=[blk, blk, pl.BlockSpec((D_MODEL, D_MODEL), lambda i: (0, 0)), blk],
        out_specs=[blk, pl.BlockSpec((8, 128), lambda i: (0, 0))],
        out_shape=[_sds((s, D_MODEL)), _sds((8, 128))],
        compiler_params=_cp(("arbitrary",)), name="out_fwd")(x, mixed, w_out_bf, tgt)


def _out_bwd(dy, mixed, w_out_bf):
    s = dy.shape[0]
    tm = 512

    def body(dy_ref, m_ref, w_ref, dm_ref, dw_ref):
        i = pl.program_id(0)

        @pl.when(i == 0)
        def _():
            dw_ref[...] = jnp.zeros_like(dw_ref)

        dyb = dy_ref[...].astype(BF16)
        dm_ref[...] = lax.dot_general(dyb, w_ref[...], (((1,), (1,)), ((), ())), preferred_element_type=F32)
        dw_ref[...] = dw_ref[...] + lax.dot_general(m_ref[...], dyb, (((0,), (0,)), ((), ())),
                                                    preferred_element_type=F32)

    blk = pl.BlockSpec((tm, D_MODEL), lambda i: (i, 0))
    full = pl.BlockSpec((D_MODEL, D_MODEL), lambda i: (0, 0))
    return pl.pallas_call(
        body, grid=(s // tm,), in_specs=[blk, blk, full], out_specs=[blk, full],
        out_shape=[_sds((s, D_MODEL)), _sds((D_MODEL, D_MODEL))],
        compiler_params=_cp(("arbitrary",)), name="out_bwd")(dy, mixed, w_out_bf)


def _post_bwd(dmixed, o_dn, proj, o_att, dnw_t, bd128, bd64):
    s = o_dn.shape[0]
    tm = TM

    def body(ddn_ref, datt_ref, o_ref, z_ref, gate_ref, oatt_ref, wn_ref, bd128_ref, bd64_ref,
             do_ref, dz_ref, dgate_ref, doatt_ref, do4_ref, do16_ref, delta_ref, dl4_ref, dl16_ref, dnw_ref, scr):
        i = pl.program_id(0)

        @pl.when(i == 0)
        def _():
            dnw_ref[...] = jnp.zeros_like(dnw_ref)

        bd128v = bd128_ref[...]
        o = o_ref[...]
        z = z_ref[...]
        wn = wn_ref[...]
        dy = ddn_ref[...]
        rstd = lax.rsqrt(_segsum(o * o, bd128v) * (1.0 / DN_HD) + EPS)
        nrm = o * rstd
        sz = _sigmoid(z)
        dz_ref[...] = dy * nrm * wn * _silu_grad(z, sz)
        dn = dy * z * sz
        gw = dn * wn
        do_ref[...] = rstd * (gw - nrm * (_segsum(gw * nrm, bd128v) * (1.0 / DN_HD)))
        colsum = jnp.sum(dn * nrm, axis=0, keepdims=True)
        fold = colsum[:, 0:128] + colsum[:, 128:256] + colsum[:, 256:384] + colsum[:, 384:512]
        dnw_ref[...] = dnw_ref[...] + fold
        dya = datt_ref[...]
        gate = gate_ref[...]
        oatt = oatt_ref[...]
        sg = _sigmoid(gate)
        dgate_ref[...] = dya * oatt * _silu_grad(gate, sg)
        doa = dya * gate * sg
        doatt_ref[...] = doa
        delta = _segsum(doa * oatt, bd64_ref[...])
        delta_ref[...] = delta
        _to_patterns(doa, ((do4_ref, R4), (do16_ref, R16)), scr)
        _to_patterns(delta, ((dl4_ref, R4), (dl16_ref, R16)), scr)

    blk = pl.BlockSpec((tm, 512), lambda i: (i, 0))
    cst = pl.BlockSpec((512, 512), lambda i: (0, 0))
    p4, p16 = _pattern_spec(R4), _pattern_spec(R16)
    return pl.pallas_call(
        body, grid=(s // tm,),
        in_specs=[blk, pl.BlockSpec((tm, 512), lambda i: (i, 1)), blk,
                  pl.BlockSpec((tm, 512), lambda i: (i, COL_Z)), pl.BlockSpec((tm, 512), lambda i: (i, COL_GATE)),
                  blk, pl.BlockSpec((1, 512), lambda i: (0, 0)), cst, cst],
        out_specs=[blk, blk, blk, blk, p4, p16, blk, p4, p16, pl.BlockSpec((8, 128), lambda i: (0, 0))],
        out_shape=[_sds((s, 512))] * 4 + [_pattern_shape(s, R4), _pattern_shape(s, R16), _sds((s, 512)),
                                          _pattern_shape(s, R4), _pattern_shape(s, R16), _sds((8, 128))],
        scratch_shapes=[SLABS],
        compiler_params=_cp(("arbitrary",)), name="post_bwd")(dmixed, dmixed, o_dn, proj, proj, o_att, dnw_t,
                                                               bd128, bd64)


def _att_bwd(q, k, v, v_col, do, big_l, delta, bias, p_idx, r, name):
    s = q.shape[0]
    nblk = s // BLK
    nseq = nblk // r
    nstep = nblk // QB

    def body(q_ref, kp_ref, kc_ref, vp_ref, vc_ref, do_ref, l_ref, dl_ref, b_ref,
             dq_ref, dk_ref, dv_ref, ds_ref, dkc_ref, dvc_ref):
        j = pl.program_id(0)

        @pl.when(j == 0)
        def _():
            dkc_ref[...] = jnp.zeros_like(dkc_ref)
            dvc_ref[...] = jnp.zeros_like(dvc_ref)
            ds_ref[...] = jnp.zeros_like(ds_ref)

        @pl.when(j < nstep)
        def _():
            def head(h, sub, masks, q2, do2, kp2, kc2, vp2, vc2):
                mask_prev, mask_cur, lo_half = masks
                rows = slice(BLK * sub, BLK * sub + BLK)
                out_l, pair_l = _head_lanes(h)
                sel = lo_half if h % 2 == 0 else jnp.logical_not(lo_half)
                qm = jnp.where(sel, q2, 0.0).astype(BF16)
                dom = jnp.where(sel, do2, 0.0).astype(BF16)
                s_prev, s_cur = _att_scores(qm, kp2, kc2, b_ref[0, h], mask_prev, mask_cur)
                dp_prev = _dot_nt(dom, vp2)
                dp_cur = _dot_nt(dom, vc2)
                yield
                lh = l_ref[rows, ATT_HD * h:ATT_HD * h + 1]
                dh = dl_ref[rows, ATT_HD * h:ATT_HD * h + 1]
                p_prev = jnp.exp(s_prev - lh)
                p_cur = jnp.exp(s_cur - lh)
                ds_prev = p_prev * (dp_prev - dh)
                ds_cur = p_cur * (dp_cur - dh)
                ds_ref[h, :, 0:BLK] = ds_ref[h, :, 0:BLK] + ds_prev
                ds_ref[h, :, BLK:2 * BLK] = ds_ref[h, :, BLK:2 * BLK] + ds_cur
                dsb_prev, dsb_cur = ds_prev.astype(BF16), ds_cur.astype(BF16)
                pb_prev, pb_cur = p_prev.astype(BF16), p_cur.astype(BF16)
                yield
                qb, dob = q2.astype(BF16), do2.astype(BF16)
                dq_ref[rows, out_l] = (_dot(dsb_prev, kp2) + _dot(dsb_cur, kc2))[:, pair_l]
                dk_prev = _dot_tn(dsb_prev, qb)[:, pair_l]
                dv_prev = _dot_tn(pb_prev, dob)[:, pair_l]
                if sub == 0:
                    last = slice(BLK * (QB - 1), BLK * QB)
                    dk_ref[last, out_l] = dkc_ref[last, out_l] + dk_prev
                    dv_ref[last, out_l] = dvc_ref[last, out_l] + dv_prev
                else:
                    before = slice(BLK * (sub - 1), BLK * sub)
                    dkc_ref[before, out_l] = dkc_ref[before, out_l] + dk_prev
                    dvc_ref[before, out_l] = dvc_ref[before, out_l] + dv_prev
                yield
                dkc_ref[rows, out_l] = _dot_tn(dsb_cur, qb)[:, pair_l]
                dvc_ref[rows, out_l] = _dot_tn(pb_cur, dob)[:, pair_l]

            done = slice(0, BLK * (QB - 1))
            dk_ref[done, :] = dkc_ref[done, :]
            dv_ref[done, :] = dvc_ref[done, :]
            for sub in range(QB):
                rows = slice(BLK * sub, BLK * sub + BLK)
                before = slice(BLK * (sub - 1), BLK * sub)
                masks = _att_masks(((QB * j + sub) % nseq) != 0)
                gens = []
                for pp in range(ATT_HEADS // 2):
                    lanes = slice(128 * pp, 128 * pp + 128)
                    kp = kp_ref[:, lanes] if sub == 0 else kc_ref[before, lanes]
                    vp = vp_ref[:, lanes] if sub == 0 else vc_ref[before, lanes]
                    slabs = (q_ref[rows, lanes], do_ref[rows, lanes], kp.astype(BF16),
                             kc_ref[rows, lanes].astype(BF16), vp.astype(BF16), vc_ref[rows, lanes].astype(BF16))
                    gens += [head(2 * pp, sub, masks, *slabs), head(2 * pp + 1, sub, masks, *slabs)]
                _interleave(gens)

        @pl.when(j == nstep)
        def _():
            dk_ref[...] = dkc_ref[...]
            dv_ref[...] = dvc_ref[...]

    last_step = nstep - 1
    cur = pl.BlockSpec((QB * BLK, 512), lambda j: (jnp.minimum(j, last_step), 0))
    lag = pl.BlockSpec((QB * BLK, 512), lambda j: (jnp.clip(j - 1, 0, last_step), 0))
    prev = pl.BlockSpec((BLK, 512), lambda j: (jnp.clip(QB * j - 1, 0, nblk - 1), 0))
    vcur = pl.BlockSpec((QB * BLK, 512), lambda j: (jnp.minimum(j, last_step), v_col))
    vprev = pl.BlockSpec((BLK, 512), lambda j: (jnp.clip(QB * j - 1, 0, nblk - 1), v_col))
    return pl.pallas_call(
        body, grid=(nstep + 1,),
        in_specs=[cur, prev, cur, vprev, vcur, cur, cur, cur,
                  pl.BlockSpec((1, ATT_HEADS, BLK, 2 * BLK), lambda j: (p_idx, 0, 0, 0))],
        out_specs=[cur, lag, lag, pl.BlockSpec((ATT_HEADS, BLK, 2 * BLK), lambda j: (0, 0, 0))],
        out_shape=[_sds((s, 512))] * 3 + [_sds((ATT_HEADS, BLK, 2 * BLK))],
        scratch_shapes=[pltpu.VMEM((QB * BLK, 512), F32), pltpu.VMEM((QB * BLK, 512), F32)],
        compiler_params=_cp(("arbitrary",)), name=name)(q, k, k, v, v, do, big_l, delta, bias)


def _att_pre_bwd(dq_pats, dk_pats, dv_pats, proj, qw_t, kw_t, bd64):
    s = proj.shape[0]
    tm = TM

    def body(dq1_ref, dq4_ref, dq16_ref, dk1_ref, dk4_ref, dk16_ref, dv1_ref, dv4_ref, dv16_ref,
             q_ref, k_ref, qw_ref, kw_ref, bd_ref,
             dqr_ref, dkr_ref, dvr_ref, dqw_ref, dkw_ref, scr4, scr16):
        i = pl.program_id(0)

        @pl.when(i == 0)
        def _():
            dqw_ref[...] = jnp.zeros_like(dqw_ref)
            dkw_ref[...] = jnp.zeros_like(dkw_ref)

        bd = bd_ref[...]

        def total(d1_ref, d4_ref, d16_ref):
            return d1_ref[...] + _from_pattern(d4_ref, R4, scr4) + _from_pattern(d16_ref, R16, scr16)

        def one(d_refs, x_ref, w_ref, scale, dx_ref, dw_ref):
            dy = total(*d_refs) * scale
            x = x_ref[...]
            rstd = lax.rsqrt(_segsum(x * x, bd) * (1.0 / ATT_HD) + EPS)
            nrm = x * rstd
            dw_ref[...] = dw_ref[...] + jnp.sum(dy * nrm, axis=0, keepdims=True)
            g = dy * w_ref[...]
            dx_ref[...] = rstd * (g - nrm * (_segsum(g * nrm, bd) * (1.0 / ATT_HD)))

        one((dq1_ref, dq4_ref, dq16_ref), q_ref, qw_ref, ATT_HD ** -0.5, dqr_ref, dqw_ref)
        one((dk1_ref, dk4_ref, dk16_ref), k_ref, kw_ref, 1.0, dkr_ref, dkw_ref)
        dvr_ref[...] = total(dv1_ref, dv4_ref, dv16_ref)

    blk = pl.BlockSpec((tm, 512), lambda i: (i, 0))
    pats = [blk, _pattern_spec(R4), _pattern_spec(R16)]
    row = pl.BlockSpec((1, 512), lambda i: (0, 0))
    acc = pl.BlockSpec((8, 512), lambda i: (0, 0))
    return pl.pallas_call(
        body, grid=(s // tm,),
        in_specs=pats * 3 + [pl.BlockSpec((tm, 512), lambda i: (i, COL_ATT_Q)),
                             pl.BlockSpec((tm, 512), lambda i: (i, COL_ATT_K)), row, row,
                             pl.BlockSpec((512, 512), lambda i: (0, 0))],
        out_specs=[blk, blk, blk, acc, acc],
        out_shape=[_sds((s, 512))] * 3 + [_sds((8, 512))] * 2,
        scratch_shapes=[SLABS] * 2,
        compiler_params=_cp(("arbitrary",)), name="att_pre_bwd")(*dq_pats, *dk_pats, *dv_pats, proj, proj,
                                                                  qw_t, kw_t, bd64)


def _dn_scan_bwd(do, sp, qd, kt, w, vn, attn, gl):
    s = do.shape[0]
    nc = s // CHUNK

    def body(do_ref, sp_ref, qd_ref, kt_ref, w_ref, vn_ref, attn_ref, gl_ref,
             du_ref, dqd_ref, dkt_ref, dw_ref, dattn_ref, dgl_ref, ds_ref):
        n = pl.program_id(0)

        @pl.when(n == 0)
        def _():
            ds_ref[...] = jnp.zeros_like(ds_ref)

        def chain(cc, h):
            rows = slice(CHUNK * cc, CHUNK * cc + CHUNK)
            lanes = slice(128 * h, 128 * h + 128)
            dsn = ds_ref[h]
            st = sp_ref[cc, h]
            dsb, stb = dsn.astype(BF16), st.astype(BF16)
            dob = do_ref[rows, lanes].astype(BF16)
            vnb = vn_ref[rows, lanes].astype(BF16)
            dvn = _dot_tn(attn_ref[cc, h].astype(BF16), dob) + _dot(kt_ref[rows, lanes].astype(BF16), dsb)
            du_ref[rows, lanes] = dvn
            dqd_ref[rows, lanes] = _dot_nt(dob, stb)
            dattn_ref[cc, h] = _dot_nt(dob, vnb)
            dkt_ref[rows, lanes] = _dot_nt(vnb, dsb)
            tot = jnp.sum(jnp.sum(st * dsn, axis=1, keepdims=True), axis=0, keepdims=True)
            dgl_ref[cc, h] = jnp.broadcast_to(tot, (1, 128))
            qdo = _dot_tn(qd_ref[rows, lanes].astype(BF16), dob)
            yield
            dvb = dvn.astype(BF16)
            dw_ref[rows, lanes] = -_dot_nt(dvb, stb)
            ds_ref[h] = qdo + dsn * gl_ref[cc, h] - _dot_tn(w_ref[rows, lanes].astype(BF16), dvb)

        for cc in reversed(range(CPS)):
            _interleave([chain(cc, h) for h in range(DN_HEADS)])

    nsteps = nc // CPS
    big = pl.BlockSpec((CPS * CHUNK, 512), lambda n: (nsteps - 1 - n, 0))
    sq = pl.BlockSpec((CPS, DN_HEADS, CHUNK, CHUNK), lambda n: (nsteps - 1 - n, 0, 0, 0))
    glb = pl.BlockSpec((CPS, DN_HEADS, 1, 128), lambda n: (nsteps - 1 - n, 0, 0, 0))
    return pl.pallas_call(
        body, grid=(nsteps,),
        in_specs=[big, pl.BlockSpec((CPS, DN_HEADS, DN_HD, DN_HD), lambda n: (nsteps - 1 - n, 0, 0, 0)),
                  big, big, big, big, sq, glb],
        out_specs=[big, big, big, big, sq, glb],
        out_shape=[_sds((s, 512))] * 4 + [_sds((nc, DN_HEADS, CHUNK, CHUNK)), _sds((nc, DN_HEADS, 1, 128))],
        scratch_shapes=[pltpu.VMEM((DN_HEADS, DN_HD, DN_HD), F32)],
        compiler_params=_cp(("arbitrary",)), name="dn_scan_bwd")(do, sp, qd, kt, w, vn, attn, gl)


def _dn_prep_bwd(qn, kn, v, bg, bgt, tri, sel, t_inv, attn, u, w, du, dw, dqd, dkt, dattn, dgl):
    s = qn.shape[0]
    nc = s // CHUNK

    def body(q_ref, k_ref, v_ref, bg_ref, bgt_ref, tri_ref, sel_ref, t_ref, attn_ref, u_ref, w_ref,
             du_ref, dw_ref, dqd_ref, dkt_ref, dattn_ref, dgl_ref,
             dq_ref, dk_ref, dv_ref, dbg_ref):
        tri_v = tri_ref[...]
        lane = lax.broadcasted_iota(jnp.int32, (CHUNK, 128), 1)
        rowi = lax.broadcasted_iota(jnp.int32, (CHUNK, 128), 0)
        ones_b = jnp.ones((CHUNK, 128), BF16)
        ones_sq = jnp.ones((128, 128), BF16)
        parts = [[] for _ in range(CPS)]

        def chain(cc, h, bg3, gc3, gc_row):
            rows = slice(CHUNK * cc, CHUNK * cc + CHUNK)
            lanes = slice(128 * h, 128 * h + 128)
            gcc, beta, incl, strict, decay, gl = _chunk_common(bg3, gc3, gc_row, h, sel_ref)
            yield
            q = q_ref[rows, lanes]
            k = k_ref[rows, lanes]
            vv = v_ref[rows, lanes]
            ts = _split(t_ref[cc, h])
            egc = jnp.exp(gcc)
            kb = k * beta
            a_mat = jnp.where(strict, _mm_nt(kb, k) * decay, 0.0)
            dvb = _mm3_tn(ts, _split(du_ref[rows, lanes]))
            dkbg = _mm3_tn(ts, _split(dw_ref[rows, lanes]))
            yield
            d_a = jnp.where(strict, -(_mm_nt(dvb, u_ref[rows, lanes]) + _mm_nt(dkbg, w_ref[rows, lanes])), 0.0)
            d_m = d_a * decay
            dattn_m = jnp.where(incl, dattn_ref[cc, h], 0.0)
            dqk = dattn_m * decay
            e_hi, e_lo = _split(d_a * a_mat + dattn_m * attn_ref[cc, h])
            yield
            dkb = _mm(d_m, k)
            dk = _mm_tn(d_m, kb) + _mm_tn(dqk, q)
            dq = _mm(dqk, k)
            e_colsum = _dot_tn(e_hi, ones_b) + _dot_tn(e_lo, ones_b)
            e_rowsum = _dot(e_hi, ones_b) + _dot(e_lo, ones_b)
            dqd = dqd_ref[rows, lanes]
            dkt = dkt_ref[rows, lanes]
            s_dqd = _rowsum_b(dqd * q, ones_sq)
            s_dkt = _rowsum_b(dkt * k, ones_sq)
            rk = _rowsum_b(dkbg * k, ones_sq)
            s_dkb = _rowsum_b(dkb * k, ones_sq)
            s_dvb = _rowsum_b(dvb * vv, ones_sq)
            yield
            tail = jnp.exp(gl - gcc)
            r = s_dkt * tail
            dgl_tot = jnp.sum(r, axis=0, keepdims=True) + dgl_ref[cc, h] * jnp.exp(gl)
            dgc = e_rowsum - e_colsum + s_dqd * egc - r + rk * beta * egc
            dgc = dgc + jnp.where(rowi == CHUNK - 1, dgl_tot, 0.0)
            dq_ref[rows, lanes] = dq + dqd * egc
            dk_ref[rows, lanes] = dk + dkt * tail + dkbg * (beta * egc) + dkb * beta
            dv_ref[rows, lanes] = dvb * beta
            parts[cc].append((h, dgc, rk * egc + s_dkb + s_dvb))

        gens = []
        for cc in range(CPS):
            bgv = bg_ref[CHUNK * cc:CHUNK * cc + CHUNK, :]
            bg3, gc3, gc_row = _chunk_cumsum(bgv, bgt_ref[cc], tri_v)
            gens += [chain(cc, h, bg3, gc3, gc_row) for h in range(DN_HEADS)]
        _interleave(gens)
        for cc in range(CPS):
            dgc_mat = jnp.zeros((CHUNK, 128), F32)
            dbeta_mat = jnp.zeros((CHUNK, 128), F32)
            for h, dgc, dbeta in parts[cc]:
                dgc_mat = dgc_mat + jnp.where(lane == DN_HEADS + h, dgc, 0.0)
                dbeta_mat = dbeta_mat + jnp.where(lane == h, dbeta, 0.0)
            dbg_ref[CHUNK * cc:CHUNK * cc + CHUNK, :] = _mmx_tn(tri_v, dgc_mat) + dbeta_mat

    big = pl.BlockSpec((CPS * CHUNK, 512), lambda n: (n, 0))
    sq = pl.BlockSpec((CPS, DN_HEADS, CHUNK, CHUNK), lambda n: (n, 0, 0, 0))
    glb = pl.BlockSpec((CPS, DN_HEADS, 1, 128), lambda n: (n, 0, 0, 0))
    small = pl.BlockSpec((CPS * CHUNK, 128), lambda n: (n, 0))
    return pl.pallas_call(
        body, grid=(nc // CPS,),
        in_specs=[big, big, big, small, pl.BlockSpec((CPS, 8, CHUNK), lambda n: (n, 0, 0)),
                  pl.BlockSpec((CHUNK, CHUNK), lambda n: (0, 0)),
                  pl.BlockSpec((8, 128, 128), lambda n: (0, 0, 0)), sq, sq, big, big,
                  big, big, big, big, sq, glb],
        out_specs=[big, big, big, small],
        out_shape=[_sds((s, 512))] * 3 + [_sds((s, 128))],
        compiler_params=_cp(("parallel",)), name="dn_prep_bwd")(qn, kn, v, bg, bgt, tri, sel, t_inv, attn, u, w,
                                                                  du, dw, dqd, dkt, dattn, dgl)


def _dn_pre_bwd(dqn, dkn, dv, dbg, proj, conv_w8, alog_l, dtb_l):
    s = proj.shape[0]
    tr = 512
    nh = tr // 8

    def body(dq_ref, dk_ref, dv_ref, dbg_ref, u_ref, halo_ref, ba_ref, w_ref, al_ref, dt_ref,
             dy_ref, dba_ref, dsm_ref):
        i = pl.program_id(0)

        @pl.when(i == 0)
        def _():
            dsm_ref[...] = jnp.zeros_like(dsm_ref)

        keep = (i > 0).astype(F32)
        for c in range(12):
            lanes = slice(128 * c, 128 * c + 128)
            y = _conv_group(u_ref[:, lanes], halo_ref[:, lanes] * keep, w_ref, c)
            sg = _sigmoid(y)
            sv = y * sg
            if c < 8:
                rs = lax.rsqrt(jnp.sum(sv * sv, axis=1, keepdims=True) + EPS)
                n = sv * rs
                if c < 4:
                    dn = dq_ref[:, lanes] * (DN_HD ** -0.5)
                else:
                    dn = dk_ref[:, slice(128 * (c - 4), 128 * (c - 3))]
                dsv = rs * (dn - n * jnp.sum(dn * n, axis=1, keepdims=True))
            else:
                dsv = dv_ref[:, slice(128 * (c - 8), 128 * (c - 7))]
            dy_ref[:, lanes] = dsv * _silu_grad(y, sg)
        _, lane, sig_b, t, nega, g = _beta_g(ba_ref[...], al_ref[...], dt_ref[...])
        dbg = dbg_ref[...]
        da = dbg * nega * _sigmoid(t)
        is_b = lane < DN_HEADS
        is_a = jnp.logical_and(lane >= DN_HEADS, lane < 2 * DN_HEADS)
        dba_ref[...] = jnp.where(is_b, dbg * sig_b * (1.0 - sig_b), jnp.where(is_a, da, 0.0))
        d_alog = jnp.sum(jnp.where(is_a, dbg * g, 0.0), axis=0, keepdims=True)
        d_dtb = jnp.sum(jnp.where(is_a, da, 0.0), axis=0, keepdims=True)
        row = lax.broadcasted_iota(jnp.int32, (8, 128), 0)
        dsm_ref[...] = dsm_ref[...] + jnp.where(row == 0, d_alog, jnp.where(row == 1, d_dtb, 0.0))

    blk = pl.BlockSpec((tr, 512), lambda i: (i, 0))
    return pl.pallas_call(
        body, grid=(s // tr,),
        in_specs=[blk, blk, blk, pl.BlockSpec((tr, 128), lambda i: (i, 0)),
                  pl.BlockSpec((tr, 1536), lambda i: (i, 0)),
                  pl.BlockSpec((8, 1536), lambda i: (jnp.maximum(i * nh - 1, 0), 0)),
                  pl.BlockSpec((tr, 128), lambda i: (i, COL_BA_128)),
                  pl.BlockSpec((8, 1536), lambda i: (0, 0)),
                  pl.BlockSpec((1, 128), lambda i: (0, 0)), pl.BlockSpec((1, 128), lambda i: (0, 0))],
        out_specs=[pl.BlockSpec((tr, 1536), lambda i: (i, 0)), pl.BlockSpec((tr, 128), lambda i: (i, 0)),
                   pl.BlockSpec((8, 128), lambda i: (0, 0))],
        out_shape=[_sds((s, 1536)), _sds((s, 128)), _sds((8, 128))],
        compiler_params=_cp(("arbitrary",)), name="dn_pre_bwd")(dqn, dkn, dv, dbg, proj, proj, proj, conv_w8,
                                                                 alog_l, dtb_l)


def _conv_bwd(dy, proj, conv_w8):
    s = dy.shape[0]
    tr = 512
    nh = tr // 8
    nblk = s // tr

    def body(dy_ref, dyn_ref, u_ref, halo_ref, w_ref, du_ref, dw_ref):
        i = pl.program_id(0)

        @pl.when(i == 0)
        def _():
            dw_ref[...] = jnp.zeros_like(dw_ref)

        keep_prev = (i > 0).astype(F32)
        keep_next = (i < nblk - 1).astype(F32)
        row = lax.broadcasted_iota(jnp.int32, (8, 128), 0)
        for c in range(12):
            lanes = slice(128 * c, 128 * c + 128)
            dyc = dy_ref[:, lanes]
            dcat = jnp.concatenate([dyc, dyn_ref[:, lanes] * keep_next], axis=0)
            xcat = jnp.concatenate([halo_ref[:, lanes] * keep_prev, u_ref[:, lanes]], axis=0)
            du = dyc * w_ref[CONV_W - 1:CONV_W, lanes]
            dwc = jnp.where(row == CONV_W - 1, jnp.sum(dyc * u_ref[:, lanes], axis=0, keepdims=True), 0.0)
            for k in range(1, CONV_W):
                du = du + pltpu.roll(dcat, tr + 8 - k, 0)[0:tr] * w_ref[CONV_W - 1 - k:CONV_W - k, lanes]
                ush = pltpu.roll(xcat, k, 0)[8:8 + tr]
                dwc = dwc + jnp.where(row == CONV_W - 1 - k, jnp.sum(dyc * ush, axis=0, keepdims=True), 0.0)
            du_ref[:, lanes] = du
            dw_ref[:, lanes] = dw_ref[:, lanes] + dwc

    return pl.pallas_call(
        body, grid=(nblk,),
        in_specs=[pl.BlockSpec((tr, 1536), lambda i: (i, 0)),
                  pl.BlockSpec((8, 1536), lambda i: (jnp.minimum((i + 1) * nh, s // 8 - 1), 0)),
                  pl.BlockSpec((tr, 1536), lambda i: (i, 0)),
                  pl.BlockSpec((8, 1536), lambda i: (jnp.maximum(i * nh - 1, 0), 0)),
                  pl.BlockSpec((8, 1536), lambda i: (0, 0))],
        out_specs=[pl.BlockSpec((tr, 1536), lambda i: (i, 0)), pl.BlockSpec((8, 1536), lambda i: (0, 0))],
        out_shape=[_sds((s, 1536)), _sds((8, 1536))],
        compiler_params=_cp(("arbitrary",)), name="conv_bwd")(dy, dy, proj, proj, conv_w8)


def _in_bwd_dx(d_qkv_dn, dz, dq_att, dk_att, dv_att, dgate, dba, w_bf, x, dy, norm_w):
    s = x.shape[0]
    tm = 256

    def body(a_ref, b_ref, c_ref, d_ref, e_ref, f_ref, g_ref, w_ref, x_ref, dy_ref, nw_ref,
             dx_ref, dp_ref, dnw_ref):
        i = pl.program_id(0)

        @pl.when(i == 0)
        def _():
            dnw_ref[...] = jnp.zeros_like(dnw_ref)

        dp = jnp.concatenate([r[...].astype(BF16) for r in (a_ref, b_ref, c_ref, d_ref, e_ref, f_ref, g_ref)],
                             axis=1)
        dp_ref[...] = dp
        dh = _dot(dp, w_ref[...])
        xv = x_ref[...]
        rstd = lax.rsqrt(jnp.mean(xv * xv, axis=-1, keepdims=True) + EPS)
        xh = xv * rstd
        dnw_ref[...] = dnw_ref[...] + jnp.sum(dh * xh, axis=0, keepdims=True)
        g = dh * nw_ref[...]
        dx_ref[...] = rstd * (g - xh * jnp.mean(g * xh, axis=-1, keepdims=True)) + dy_ref[...]

    def blk(n):
        return pl.BlockSpec((tm, n), lambda i: (i, 0))

    return pl.pallas_call(
        body, grid=(s // tm,),
        in_specs=[blk(1536), blk(512), blk(512), blk(512), blk(512), blk(512), blk(128),
                  pl.BlockSpec((D_IN_PAD, D_MODEL), lambda i: (0, 0)), blk(D_MODEL), blk(D_MODEL),
                  pl.BlockSpec((1, D_MODEL), lambda i: (0, 0))],
        out_specs=[blk(D_MODEL), blk(D_IN_PAD), pl.BlockSpec((8, D_MODEL), lambda i: (0, 0))],
        out_shape=[_sds((s, D_MODEL)), _sds((s, D_IN_PAD), BF16), _sds((8, D_MODEL))],
        compiler_params=_cp(("arbitrary",)), name="in_bwd_dx")(d_qkv_dn, dz, dq_att, dk_att, dv_att, dgate, dba,
                                                                w_bf, x, dy, norm_w)


def _in_bwd_dw(x, norm_w, dp):
    s = x.shape[0]
    tm = 512
    n_col = 3
    tn = D_IN_PAD // n_col

    def body(x_ref, nw_ref, dp_ref, dw_ref):
        i = pl.program_id(1)

        @pl.when(i == 0)
        def _():
            dw_ref[...] = jnp.zeros_like(dw_ref)

        xv = x_ref[...]
        rstd = lax.rsqrt(jnp.mean(xv * xv, axis=-1, keepdims=True) + EPS)
        h = (xv * rstd * nw_ref[...]).astype(BF16)
        dw_ref[...] = dw_ref[...] + _dot_tn(dp_ref[...], h)

    return pl.pallas_call(
        body, grid=(n_col, s // tm),
        in_specs=[pl.BlockSpec((tm, D_MODEL), lambda j, i: (i, 0)), pl.BlockSpec((1, D_MODEL), lambda j, i: (0, 0)),
                  pl.BlockSpec((tm, tn), lambda j, i: (i, j))],
        out_specs=pl.BlockSpec((tn, D_MODEL), lambda j, i: (j, 0)),
        out_shape=_sds((D_IN_PAD, D_MODEL)),
        compiler_params=_cp(("parallel", "arbitrary")), name="in_bwd_dw")(x, norm_w, dp)


def _flat(a):
    return a.reshape(-1, a.shape[-1])


def _as_pattern(a, r):
    return a if r == 1 else a.reshape(r, a.shape[0] // r, a.shape[1])


D_SHARD = D_IN // N_CHIPS
BA_START = 4 * D_DN
BA_PACKED = 4096
S1_HEAD = BA_START - D_SHARD
S1_BA = 2 * D_SHARD - BA_START


def _pack_rows(g):
    pad = jnp.zeros((D_IN_PAD - D_IN, g.shape[2]), g.dtype)
    s2_ba = 2 * DN_HEADS - S1_BA
    return jnp.concatenate([g[0][:D_SHARD], g[1][:S1_HEAD], g[2][s2_ba:D_SHARD], g[3][:D_SHARD],
                            g[1][S1_HEAD:D_SHARD], g[2][:s2_ba], pad], axis=0)


def _unpack_rows(p, rows):
    mid = BA_PACKED + S1_BA
    pad = jnp.zeros((rows - D_SHARD, p.shape[1]), p.dtype)
    return jnp.stack([jnp.concatenate([p[0:D_SHARD], pad], axis=0),
                      jnp.concatenate([p[D_SHARD:BA_START], p[BA_PACKED:mid], pad], axis=0),
                      jnp.concatenate([p[mid:BA_PACKED + 2 * DN_HEADS], p[BA_START:BA_START + S1_HEAD], pad], axis=0),
                      jnp.concatenate([p[BA_START + S1_HEAD:BA_PACKED], pad], axis=0)])


def _lane_row(vec, offset):
    return jnp.pad(vec.reshape(1, -1), ((0, 0), (offset, 128 - offset - vec.shape[0])))


def _local_step(x, tgt, norm_w, w_bf, conv_w, a_log, dt_bias, dn_norm_w, q_norm_w, k_norm_w, rel_bias, w_out_bf):
    s = x.shape[0]
    nc = s // CHUNK
    conv_w8 = jnp.pad(conv_w, ((0, 8 - CONV_W), (0, 0)))
    alog_l = _lane_row(a_log.reshape(-1), DN_HEADS)
    dtb_l = _lane_row(dt_bias.reshape(-1), DN_HEADS)
    dnw_t = jnp.tile(dn_norm_w.reshape(1, DN_HD), (1, DN_HEADS))
    qw_t = jnp.tile(q_norm_w.reshape(1, ATT_HD), (1, ATT_HEADS))
    kw_t = jnp.tile(k_norm_w.reshape(1, ATT_HD), (1, ATT_HEADS))
    bd128 = _block_ones(512, DN_HD)
    bd64 = _block_ones(512, ATT_HD)
    tri = _tri_incl()
    sel = _lane_select()
    buckets = _bucket_tables()

    proj = _in_proj(x, norm_w, w_bf)
    qn, kn, v_dn, bg = _dn_pre(proj, conv_w8, alog_l, dtb_l)
    bgt = bg[:, 0:8].reshape(nc, CHUNK, 8).transpose(0, 2, 1)
    u, w, qd, kt, attn, t_inv, gl = _dn_prep(qn, kn, v_dn, bg, bgt, tri, sel)
    o_dn, vn, sp = _dn_scan(u, w, qd, kt, attn, gl)
    q1, k1, q4, k4, v4, q16, k16, v16 = _att_pre(proj, qw_t, kw_t, bd64)
    rs = [r for _, r in PATTERNS]
    qkv = [(q1, k1, proj, COL_ATT_V), (_flat(q4), _flat(k4), _flat(v4), 0), (_flat(q16), _flat(k16), _flat(v16), 0)]
    bias = _bias_fwd(rel_bias, buckets)
    o_pats, lse_pats = [], []
    for p, r in enumerate(rs):
        o_p, lse_p = _att_fwd(*qkv[p], bias, p, r, "att_fwd_r%d" % r)
        o_pats.append(_as_pattern(o_p, r))
        lse_pats.append(_as_pattern(lse_p, r))
    mixed, o_att, l1, l4, l16 = _post_fwd(o_dn, proj, o_pats, lse_pats, dnw_t, bd128)
    dy, loss_blk = _out_fwd(x, mixed, w_out_bf, tgt)

    dmixed, d_w_out = _out_bwd(dy, mixed, w_out_bf)
    do_dn, dz, dgate, do1, do4, do16, dl1, dl4, dl16, d_dnw = _post_bwd(dmixed, o_dn, proj, o_att, dnw_t, bd128, bd64)
    side = [(do1, l1, dl1), (_flat(do4), _flat(l4), _flat(dl4)), (_flat(do16), _flat(l16), _flat(dl16))]
    dq_pats, dk_pats, dv_pats, ds_accs = [], [], [], []
    for p, r in enumerate(rs):
        dq_p, dk_p, dv_p, ds_p = _att_bwd(*qkv[p], *side[p], bias, p, r, "att_bwd_r%d" % r)
        dq_pats.append(_as_pattern(dq_p, r))
        dk_pats.append(_as_pattern(dk_p, r))
        dv_pats.append(_as_pattern(dv_p, r))
        ds_accs.append(ds_p)
    d_rel_bias = _bias_bwd(ds_accs, buckets)
    dq_att, dk_att, dv_att, d_qw, d_kw = _att_pre_bwd(dq_pats, dk_pats, dv_pats, proj, qw_t, kw_t, bd64)
    du, dqd, dkt, dw, dattn, dgl = _dn_scan_bwd(do_dn, sp, qd, kt, w, vn, attn, gl)
    dqn, dkn, dv_dn, dbg = _dn_prep_bwd(qn, kn, v_dn, bg, bgt, tri, sel, t_inv, attn, u, w, du, dw, dqd, dkt, dattn,
                                        dgl)
    dyc, dba, dsm = _dn_pre_bwd(dqn, dkn, dv_dn, dbg, proj, conv_w8, alog_l, dtb_l)
    d_qkv_dn, d_conv8 = _conv_bwd(dyc, proj, conv_w8)
    grad_x, dp, d_nw8 = _in_bwd_dx(d_qkv_dn, dz, dq_att, dk_att, dv_att, dgate, dba, w_bf, x, dy, norm_w)
    d_w_in_t = _in_bwd_dw(x, norm_w, dp)

    grads = dict(
        norm_w=d_nw8[0:1, :],
        w_in_t=d_w_in_t,
        conv_w=d_conv8[0:CONV_W, :],
        a_log=dsm[0:1, DN_HEADS:2 * DN_HEADS],
        dt_bias=dsm[1:2, DN_HEADS:2 * DN_HEADS],
        dn_norm_w=d_dnw[0:1, :],
        q_norm_w=d_qw[0:1, :].reshape(ATT_HEADS, ATT_HD),
        k_norm_w=d_kw[0:1, :].reshape(ATT_HEADS, ATT_HD),
        rel_bias=d_rel_bias,
        w_out=d_w_out,
    )
    return loss_blk[0, 0], grad_x, grads


MESH_ID = pl.DeviceIdType.MESH
ANY = pl.BlockSpec(memory_space=pl.ANY)


def _position():
    return lax.axis_index("x"), lax.axis_index("y"), lax.axis_index("c")


def _other_chips(x, y):
    return [(1 - x, y), (x, 1 - y), (1 - x, 1 - y)]


SHARD_PAD = 1040
WIN = 528


def _row_split(n):
    return (n // 2) // 128 * 128


def _part(ref, cc):
    n = ref.shape[0]
    sp = _row_split(n)
    return ref.at[pl.ds(0, sp)] if cc == 0 else ref.at[pl.ds(sp, n - sp)]


def _gather_weights(wt_s, w_out_s, conv_s):
    def body(a_ref, b_ref, c_ref, ga_ref, gb_ref, gc_ref, send_sems, recv_sems, loc_sems, a_vmem, b_vmem):
        x, y, c = _position()
        me = 2 * x + y
        sib = (x, y, 1 - c)
        big = ((a_ref, ga_ref), (b_ref, gb_ref))
        stage_in = [pltpu.make_async_copy(a_ref, a_vmem, loc_sems.at[0]),
                    pltpu.make_async_copy(b_ref, b_vmem, loc_sems.at[1])]
        local = [pltpu.make_async_copy(a_vmem, ga_ref.at[me], loc_sems.at[0]),
                 pltpu.make_async_copy(b_vmem, gb_ref.at[me], loc_sems.at[1]),
                 pltpu.make_async_copy(c_ref, gc_ref.at[me], loc_sems.at[2])]
        for cp in stage_in:
            cp.start()
        local[2].start()
        for cp in stage_in:
            cp.wait()
        for cp in local[:2]:
            cp.start()
        others = _other_chips(x, y)

        def exchange(cc):
            sends = []
            for j, (px, py) in enumerate(others):
                for t, (src, dst) in enumerate(big):
                    k = 3 * j + t
                    sends.append(pltpu.make_async_remote_copy(
                        src_ref=_part(src, cc), dst_ref=_part(dst.at[me], cc), send_sem=send_sems.at[k],
                        recv_sem=recv_sems.at[k], device_id=(px, py, c), device_id_type=MESH_ID))
                sends.append(pltpu.make_async_remote_copy(
                    src_ref=c_ref, dst_ref=gc_ref.at[me], send_sem=send_sems.at[3 * j + 2],
                    recv_sem=recv_sems.at[3 * j + 2], device_id=(px, py, c), device_id_type=MESH_ID))
            for cp in sends:
                cp.start()
            for j, (px, py) in enumerate(others):
                src_chip = 2 * px + py
                for t, (src, dst) in enumerate(big):
                    landed = _part(dst.at[src_chip], cc)
                    pltpu.make_async_remote_copy(
                        src_ref=_part(src, cc), dst_ref=landed, send_sem=send_sems.at[3 * j + t],
                        recv_sem=recv_sems.at[3 * j + t], device_id=(px, py, c), device_id_type=MESH_ID).wait_recv()
                    k = 9 + 2 * j + t
                    fwd = pltpu.make_async_remote_copy(
                        src_ref=landed, dst_ref=landed, send_sem=send_sems.at[k], recv_sem=recv_sems.at[k],
                        device_id=sib, device_id_type=MESH_ID)
                    fwd.start()
                    sends.append(fwd)
                pltpu.make_async_remote_copy(
                    src_ref=c_ref, dst_ref=gc_ref.at[src_chip], send_sem=send_sems.at[3 * j + 2],
                    recv_sem=recv_sems.at[3 * j + 2], device_id=(px, py, c), device_id_type=MESH_ID).wait_recv()
            for j, (px, py) in enumerate(others):
                src_chip = 2 * px + py
                for t, (src, dst) in enumerate(big):
                    k = 9 + 2 * j + t
                    theirs = _part(dst.at[src_chip], 1 - cc)
                    pltpu.make_async_remote_copy(
                        src_ref=theirs, dst_ref=theirs, send_sem=send_sems.at[k], recv_sem=recv_sems.at[k],
                        device_id=sib, device_id_type=MESH_ID).wait_recv()
            for cp in sends:
                cp.wait_send()

        for cc in (0, 1):
            pl.when(c == cc)(functools.partial(exchange, cc))
        for cp in local:
            cp.wait()

    srcs = (wt_s, w_out_s, conv_s)
    n_sem = 9 + 6
    return pl.pallas_call(
        body, in_specs=[ANY] * 3, out_specs=[ANY] * 3,
        out_shape=[_sds((N_CHIPS,) + a.shape, a.dtype) for a in srcs],
        scratch_shapes=[pltpu.SemaphoreType.DMA((n_sem,)), pltpu.SemaphoreType.DMA((n_sem,)),
                        pltpu.SemaphoreType.DMA((3,)), pltpu.VMEM(wt_s.shape, wt_s.dtype),
                        pltpu.VMEM(w_out_s.shape, w_out_s.dtype)],
        compiler_params=_cp(), name="gather_weights")(*srcs)


def _exchange_grads(gw_in4, gw_out4, small):
    n_big = 2
    n_dev = 8

    def body(a_ref, b_ref, s_ref, ra_ref, rb_ref, rs_ref, send_sems, recv_sems, loc_sem):
        x, y, c = _position()
        dev = 4 * x + 2 * y + c
        pairs = ((a_ref, ra_ref), (b_ref, rb_ref))
        local = pltpu.make_async_copy(s_ref, rs_ref.at[dev], loc_sem)
        local.start()
        sends = []
        for j, (px, py) in enumerate(_other_chips(x, y)):
            for t, (src, dst) in enumerate(pairs):
                k = n_big * j + t
                sends.append(pltpu.make_async_remote_copy(
                    src_ref=src.at[2 * px + py], dst_ref=dst.at[j], send_sem=send_sems.at[k],
                    recv_sem=recv_sems.at[k], device_id=(px, py, c), device_id_type=MESH_ID))
        flips = [(dx, dy, dc) for dx in (0, 1) for dy in (0, 1) for dc in (0, 1)][1:]
        for f, (dx, dy, dc) in enumerate(flips):
            k = 3 * n_big + f
            peer = (x ^ dx, y ^ dy, c ^ dc)
            sends.append(pltpu.make_async_remote_copy(
                src_ref=s_ref, dst_ref=rs_ref.at[dev], send_sem=send_sems.at[k], recv_sem=recv_sems.at[k],
                device_id=peer, device_id_type=MESH_ID))
        for cp in sends:
            cp.start()
        for j, (px, py) in enumerate(_other_chips(x, y)):
            for t, (src, dst) in enumerate(pairs):
                k = n_big * j + t
                pltpu.make_async_remote_copy(
                    src_ref=src.at[0], dst_ref=dst.at[j], send_sem=send_sems.at[k], recv_sem=recv_sems.at[k],
                    device_id=(px, py, c), device_id_type=MESH_ID).wait_recv()
        for f, (dx, dy, dc) in enumerate(flips):
            k = 3 * n_big + f
            peer = (x ^ dx, y ^ dy, c ^ dc)
            pltpu.make_async_remote_copy(
                src_ref=s_ref, dst_ref=rs_ref.at[4 * peer[0] + 2 * peer[1] + peer[2]], send_sem=send_sems.at[k],
                recv_sem=recv_sems.at[k], device_id=peer, device_id_type=MESH_ID).wait_recv()
        for cp in sends:
            cp.wait_send()
        local.wait()

    n_sem = 3 * n_big + n_dev - 1
    return pl.pallas_call(
        body, in_specs=[ANY] * 3, out_specs=[ANY] * 3,
        out_shape=[_sds((3,) + gw_in4.shape[1:], gw_in4.dtype), _sds((3,) + gw_out4.shape[1:], gw_out4.dtype),
                   _sds((n_dev,) + small.shape, small.dtype)],
        scratch_shapes=[pltpu.SemaphoreType.DMA((n_sem,)), pltpu.SemaphoreType.DMA((n_sem,)),
                        pltpu.SemaphoreType.DMA],
        name="exchange_grads")(gw_in4, gw_out4, small)


def _plane_sum(own, recv, name):
    rows, cols = own.shape
    tr = 128

    def body(o_ref, r_ref, out_ref):
        acc = o_ref[...]
        for j in range(3):
            acc = acc + r_ref[j].astype(F32)
        out_ref[...] = acc

    return pl.pallas_call(
        body, grid=(pl.cdiv(rows, tr),),
        in_specs=[pl.BlockSpec((tr, cols), lambda i: (i, 0)), pl.BlockSpec((3, tr, cols), lambda i: (0, i, 0))],
        out_specs=pl.BlockSpec((tr, cols), lambda i: (i, 0)), out_shape=_sds((rows, cols)),
        compiler_params=_cp(("parallel",)), name=name)(own, recv)


def _sum_pair(a, b, name):
    n, rows, cols = a.shape
    tr = 128

    def body(a_ref, b_ref, s_ref, sb_ref):
        tot = a_ref[...] + b_ref[...]
        s_ref[...] = tot
        sb_ref[...] = tot.astype(BF16)

    blk = pl.BlockSpec((1, tr, cols), lambda k, i: (k, i, 0))
    return pl.pallas_call(
        body, grid=(n, pl.cdiv(rows, tr)), in_specs=[blk, blk], out_specs=[blk, blk],
        out_shape=[_sds(a.shape), _sds(a.shape, BF16)],
        compiler_params=_cp(("parallel", "parallel")), name=name)(a, b)


def _fill_parts(f_in, f_out, rows_in, rows_out):
    def body(a_ref, b_ref, fa_ref, fb_ref, send_sems, recv_sems, loc_sems, a_vmem, b_vmem):
        x, y, c = _position()
        sib = (x, y, 1 - c)
        pairs = ((a_ref, fa_ref), (b_ref, fb_ref))
        stage_in = [pltpu.make_async_copy(a_ref, a_vmem, loc_sems.at[0]),
                    pltpu.make_async_copy(b_ref, b_vmem, loc_sems.at[1])]
        for cp in stage_in:
            cp.start()
        for cp in stage_in:
            cp.wait()

        def fill(cc):
            mine = [_part(dst, cc) for _, dst in pairs]
            srcs = [src.at[pl.ds(0, m.shape[0])] for src, m in zip((a_vmem, b_vmem), mine)]
            local = [pltpu.make_async_copy(s, m, loc_sems.at[t]) for t, (s, m) in enumerate(zip(srcs, mine))]
            sends = [pltpu.make_async_remote_copy(src_ref=s, dst_ref=m, send_sem=send_sems.at[t],
                                                  recv_sem=recv_sems.at[t], device_id=sib, device_id_type=MESH_ID)
                     for t, (s, m) in enumerate(zip(srcs, mine))]
            for cp in local + sends:
                cp.start()
            for t, (_, dst) in enumerate(pairs):
                theirs = _part(dst, 1 - cc)
                pltpu.make_async_remote_copy(src_ref=theirs, dst_ref=theirs, send_sem=send_sems.at[t],
                                             recv_sem=recv_sems.at[t], device_id=sib, device_id_type=MESH_ID).wait_recv()
            for cp in sends:
                cp.wait_send()
            for cp in local:
                cp.wait()

        for cc in (0, 1):
            pl.when(c == cc)(functools.partial(fill, cc))

    return pl.pallas_call(
        body, in_specs=[ANY] * 2, out_specs=[ANY] * 2,
        out_shape=[_sds((rows_in, D_MODEL)), _sds((rows_out, D_MODEL))],
        scratch_shapes=[pltpu.SemaphoreType.DMA((2,)), pltpu.SemaphoreType.DMA((2,)), pltpu.SemaphoreType.DMA((2,)),
                        pltpu.VMEM(f_in.shape, F32), pltpu.VMEM(f_out.shape, F32)],
        compiler_params=_cp(), name="fill_parts")(f_in, f_out)


def _swap_windows(u_in, u_out, win_in, win_out):
    def body(a_ref, b_ref, ra_ref, rb_ref, send_sems, recv_sems):
        x, y, c = _position()
        sib = (x, y, 1 - c)
        cps = []
        for t, (src, dst, win) in enumerate(((a_ref, ra_ref, win_in), (b_ref, rb_ref, win_out))):
            split = _row_split(src.shape[1])
            start = pl.multiple_of((1 - c) * split, split)
            cps.append(pltpu.make_async_remote_copy(
                src_ref=src.at[:, pl.ds(start, win), :], dst_ref=dst, send_sem=send_sems.at[t],
                recv_sem=recv_sems.at[t], device_id=sib, device_id_type=MESH_ID))
        for cp in cps:
            cp.start()
        for cp in cps:
            cp.wait_recv()
        for cp in cps:
            cp.wait_send()

    return pl.pallas_call(
        body, in_specs=[ANY] * 2, out_specs=[ANY] * 2,
        out_shape=[_sds((N_CHIPS, win_in, D_MODEL)), _sds((N_CHIPS, win_out, D_MODEL))],
        scratch_shapes=[pltpu.SemaphoreType.DMA((2,)), pltpu.SemaphoreType.DMA((2,))],
        name="swap_windows")(u_in, u_out)


SMALL_LAYOUT = (("norm_w", 1024), ("conv_w", 6144), ("a_log", 128), ("dt_bias", 128), ("dn_norm_w", 128),
                ("q_norm_w", 512), ("k_norm_w", 512), ("rel_bias", 256))
SMALL_TOTAL = sum(n for _, n in SMALL_LAYOUT)


def _small_offset(name):
    off = 0
    for n, size in SMALL_LAYOUT:
        if n == name:
            return off
        off += size
    raise KeyError(name)


def _pack_small(grads):
    parts = []
    for name, size in SMALL_LAYOUT:
        flat = grads[name].reshape(1, -1)
        parts.append(jnp.pad(flat, ((0, 0), (0, size - flat.shape[1]))))
    return jnp.concatenate(parts, axis=1)


def _sum_small(rows):
    n_dev = rows.shape[0]
    q_off = _small_offset("q_norm_w")
    k_off = _small_offset("k_norm_w")

    def body(r_ref, tot_ref, qk_ref):
        tot = r_ref[0:1, :]
        for d in range(1, n_dev):
            tot = tot + r_ref[d:d + 1, :]
        tot_ref[...] = tot
        for row, off in ((0, q_off), (1, k_off)):
            s4 = tot[:, off:off + 128] + tot[:, off + 128:off + 256] + tot[:, off + 256:off + 384] \
                + tot[:, off + 384:off + 512]
            qk_ref[row:row + 1, :] = s4 + pltpu.roll(s4, ATT_HD, 1)

    return pl.pallas_call(
        body, in_specs=[pl.BlockSpec(memory_space=pltpu.VMEM)],
        out_specs=[pl.BlockSpec(memory_space=pltpu.VMEM)] * 2,
        out_shape=[_sds((1, SMALL_TOTAL)), _sds((2, 128))],
        compiler_params=_cp(), name="sum_small")(rows)


def _adamw_math(w, g, m, v):
    m = ADAM_B1 * m + (1.0 - ADAM_B1) * g
    v = ADAM_B2 * v + (1.0 - ADAM_B2) * (g * g)
    m_hat = m / (1.0 - ADAM_B1 ** ADAM_STEP)
    v_hat = v / (1.0 - ADAM_B2 ** ADAM_STEP)
    delta = -ADAM_LR * (m_hat / (jnp.sqrt(v_hat) + ADAM_EPS) + ADAM_WD * w)
    return delta, m, v


def _adamw_big(g, w, m, v, name):
    rows, cols = w.shape
    tr = 128

    def body(g_ref, w_ref, m_ref, v_ref, go_ref, d_ref, nm_ref, nv_ref):
        g = g_ref[...]
        go_ref[...] = g
        d_ref[...], nm_ref[...], nv_ref[...] = _adamw_math(w_ref[...], g, m_ref[...], v_ref[...])

    blk = pl.BlockSpec((tr, cols), lambda i: (i, 0))
    return pl.pallas_call(
        body, grid=(pl.cdiv(rows, tr),), in_specs=[blk] * 4, out_specs=[blk] * 4,
        out_shape=[_sds((rows, cols))] * 4, compiler_params=_cp(("parallel",)), name=name)(g, w, m, v)


def _adamw_rows(g, w, m, v, name):
    rows, cols = w.shape
    tr = 128

    def body(g_ref, w_ref, m_ref, v_ref, go_ref, d_ref, nm_ref, nv_ref, s_g, s_d, s_m, s_v):
        g = g_ref[...]
        s_g[...] = g
        s_d[...], s_m[...], s_v[...] = _adamw_math(w_ref[...], g, m_ref[...], v_ref[...])
        for i in range(tr):
            for scr, out in ((s_g, go_ref), (s_d, d_ref), (s_m, nm_ref), (s_v, nv_ref)):
                out[i] = scr[i:i + 1, :]

    blk = pl.BlockSpec((tr, cols), lambda i: (i, 0))
    oblk = pl.BlockSpec((tr, 1, cols), lambda i: (i, 0, 0))
    return pl.pallas_call(
        body, grid=(pl.cdiv(rows, tr),), in_specs=[blk] * 4, out_specs=[oblk] * 4,
        out_shape=[_sds((rows, 1, cols))] * 4, scratch_shapes=[pltpu.VMEM((tr, cols), F32)] * 4,
        compiler_params=_cp(("parallel",)), name=name)(g, w, m, v)


def _adamw_small(w, g, m, v, name):
    def body(w_ref, g_ref, m_ref, v_ref, d_ref, nm_ref, nv_ref):
        d_ref[...], nm_ref[...], nv_ref[...] = _adamw_math(w_ref[...], g_ref[...], m_ref[...], v_ref[...])

    vm = pl.BlockSpec(memory_space=pltpu.VMEM)
    return pl.pallas_call(body, in_specs=[vm] * 4, out_specs=[vm] * 3, out_shape=[_sds(w.shape)] * 3,
                          compiler_params=_cp(), name=name)(w, g, m, v)


WEIGHTS = ("norm_w", "w_in", "conv_w", "a_log", "dt_bias", "dn_norm_w", "q_norm_w", "k_norm_w", "rel_bias", "w_out")


def kernel(x, norm_w, w_in, conv_w, a_log, dt_bias, dn_norm_w, q_norm_w, k_norm_w, rel_bias, w_out, loss_target, m_norm_w, m_w_in, m_conv_w, m_a_log, m_dt_bias, m_dn_norm_w, m_q_norm_w, m_k_norm_w, m_rel_bias, m_w_out, v_norm_w, v_w_in, v_conv_w, v_a_log, v_dt_bias, v_dn_norm_w, v_q_norm_w, v_k_norm_w, v_rel_bias, v_w_out):
    xi, yi, ci = _position()
    chip = 2 * xi + yi
    w_loc = dict(norm_w=norm_w, w_in=w_in[0].T, conv_w=conv_w[0], a_log=a_log, dt_bias=dt_bias, dn_norm_w=dn_norm_w,
                 q_norm_w=q_norm_w, k_norm_w=k_norm_w, rel_bias=rel_bias, w_out=w_out[0])
    m_loc = dict(norm_w=m_norm_w, w_in=m_w_in[0].T, conv_w=m_conv_w[0], a_log=m_a_log, dt_bias=m_dt_bias,
                 dn_norm_w=m_dn_norm_w, q_norm_w=m_q_norm_w, k_norm_w=m_k_norm_w, rel_bias=m_rel_bias,
                 w_out=m_w_out[0])
    v_loc = dict(norm_w=v_norm_w, w_in=v_w_in[0].T, conv_w=v_conv_w[0], a_log=v_a_log, dt_bias=v_dt_bias,
                 dn_norm_w=v_dn_norm_w, q_norm_w=v_q_norm_w, k_norm_w=v_k_norm_w, rel_bias=v_rel_bias,
                 w_out=v_w_out[0])

    wt_pad = jnp.pad(w_loc["w_in"].astype(BF16), ((0, SHARD_PAD - D_SHARD), (0, 0)))
    g_in, g_out, g_conv = _gather_weights(wt_pad, w_loc["w_out"].astype(BF16), w_loc["conv_w"])
    wt_full = _pack_rows(g_in)
    w_out_full = g_out.reshape(D_MODEL, D_MODEL)
    conv_full = g_conv.transpose(1, 0, 2).reshape(CONV_W, 3 * D_DN)

    loss_local, grad_x, grads = _local_step(x[0], loss_target[0], norm_w, wt_full, conv_full, a_log, dt_bias,
                                            dn_norm_w, q_norm_w, k_norm_w, rel_bias, w_out_full)
    loss = lax.psum(loss_local, ("x", "y", "c"))

    u_in = _unpack_rows(grads["w_in_t"], SHARD_PAD)
    u_out = grads["w_out"].reshape(N_CHIPS, D_MODEL // N_CHIPS, D_MODEL)
    win_out = u_out.shape[1] // 2
    sib_in, sib_out = _swap_windows(u_in, u_out, WIN, win_out)
    in_mine = lax.dynamic_slice_in_dim(u_in, ci * _row_split(SHARD_PAD), WIN, axis=1)
    out_mine = lax.dynamic_slice_in_dim(u_out, ci * win_out, win_out, axis=1)
    h_in, hb_in = _sum_pair(in_mine, sib_in, "chip_sum_w_in")
    h_out, hb_out = _sum_pair(out_mine, sib_out, "chip_sum_w_out")
    small = _pack_small(grads)
    r_in, r_out, r_small = _exchange_grads(hb_in, hb_out, small)
    f_in = _plane_sum(lax.dynamic_index_in_dim(h_in, chip, 0, keepdims=False), r_in, "shard_sum_w_in")
    f_out = _plane_sum(lax.dynamic_index_in_dim(h_out, chip, 0, keepdims=False), r_out, "shard_sum_w_out")
    full_in, full_out = _fill_parts(f_in, f_out, SHARD_PAD, u_out.shape[1])
    tot_small, qk = _sum_small(r_small.reshape(8, SMALL_TOTAL))

    def small_grad(name, n):
        off = _small_offset(name)
        return tot_small[:, off:off + n]

    conv_all = small_grad("conv_w", CONV_W * 3 * D_DN).reshape(CONV_W, 3 * D_DN)
    g_small = dict(
        norm_w=small_grad("norm_w", D_MODEL),
        conv_w=lax.dynamic_slice_in_dim(conv_all, chip * (3 * D_DN // N_CHIPS), 3 * D_DN // N_CHIPS, axis=1),
        a_log=small_grad("a_log", DN_HEADS),
        dt_bias=small_grad("dt_bias", DN_HEADS),
        dn_norm_w=small_grad("dn_norm_w", DN_HD),
        q_norm_w=qk[0:1, 0:ATT_HD],
        k_norm_w=qk[1:2, 0:ATT_HD],
        rel_bias=small_grad("rel_bias", ATT_HEADS * N_BUCKETS).reshape(ATT_HEADS, N_BUCKETS),
    )

    out_g, out_d, out_m, out_v = {}, {}, {}, {}
    out_g["w_in"], out_d["w_in"], out_m["w_in"], out_v["w_in"] = _adamw_rows(
        full_in, w_loc["w_in"], m_loc["w_in"], v_loc["w_in"], "adamw_w_in")
    out_g["w_out"], out_d["w_out"], out_m["w_out"], out_v["w_out"] = _adamw_big(
        full_out, w_loc["w_out"], m_loc["w_out"], v_loc["w_out"], "adamw_w_out")
    for name in g_small:
        out_g[name] = g_small[name]
        out_d[name], out_m[name], out_v[name] = _adamw_small(w_loc[name], g_small[name], m_loc[name], v_loc[name],
                                                             "adamw_" + name)
    for d in (out_g, out_d, out_m, out_v):
        d["w_in"] = d["w_in"].transpose(1, 2, 0)
        for name in ("conv_w", "w_out"):
            d[name] = d[name][None]
    return (loss, grad_x[None], *[out_g[n] for n in WEIGHTS], *[out_d[n] for n in WEIGHTS],
            *[out_m[n] for n in WEIGHTS], *[out_v[n] for n in WEIGHTS])
```

```python
import functools
import math

import numpy as np
import jax
import jax.numpy as jnp
from jax import lax
from jax.experimental import pallas as pl
from jax.experimental.pallas import tpu as pltpu

F32 = jnp.float32
BF16 = jnp.bfloat16
HI = lax.Precision.HIGHEST

D_MODEL = 1024
D_DN = 512
DN_HEADS = 4
DN_HD = 128
CONV_W = 4
CHUNK = 64
D_ATT = 512
ATT_HEADS = 8
ATT_HD = 64
PATTERNS = ((128, 1), (512, 4), (2048, 16))
N_BUCKETS = 32
MAX_DISTANCE = 2048
D_IN = 4 * D_DN + 2 * DN_HEADS + 4 * D_ATT
D_IN_PAD = 4224
EPS = 1e-6
BLK = 128
NEG = -1e30
N_CHIPS = 4

ADAM_LR = 0.001
ADAM_B1 = 0.9
ADAM_B2 = 0.999
ADAM_EPS = 1e-08
ADAM_WD = 0.01
ADAM_STEP = 10

VMEM_LIMIT = 56 * 1024 * 1024

COL_Z = 3
COL_ATT_Q = 4
COL_ATT_K = 5
COL_ATT_V = 6
COL_GATE = 7
COL_BA_128 = 32


def _cp(sem=None):
    if sem is None:
        return pltpu.CompilerParams(vmem_limit_bytes=VMEM_LIMIT)
    return pltpu.CompilerParams(dimension_semantics=sem, vmem_limit_bytes=VMEM_LIMIT)


def _sds(shape, dtype=F32):
    return jax.ShapeDtypeStruct(shape, dtype)


def _mm(a, b):
    return jnp.dot(a.astype(BF16), b.astype(BF16), preferred_element_type=F32)


def _mm_nt(a, b):
    return lax.dot_general(a.astype(BF16), b.astype(BF16), (((1,), (1,)), ((), ())),
                           preferred_element_type=F32)


def _mm_tn(a, b):
    return lax.dot_general(a.astype(BF16), b.astype(BF16), (((0,), (0,)), ((), ())),
                           preferred_element_type=F32)


def _mmx(a, b):
    return jnp.dot(a, b, precision=HI, preferred_element_type=F32)


def _mmx_nt(a, b):
    return lax.dot_general(a, b, (((1,), (1,)), ((), ())), precision=HI, preferred_element_type=F32)


def _mmx_tn(a, b):
    return lax.dot_general(a, b, (((0,), (0,)), ((), ())), precision=HI, preferred_element_type=F32)


def _dot(a, b):
    return jnp.dot(a, b, preferred_element_type=F32)


def _dot_nt(a, b):
    return lax.dot_general(a, b, (((1,), (1,)), ((), ())), preferred_element_type=F32)


def _dot_tn(a, b):
    return lax.dot_general(a, b, (((0,), (0,)), ((), ())), preferred_element_type=F32)


def _split(a):
    hi = a.astype(BF16)
    return hi, (a - hi.astype(F32)).astype(BF16)


def _mm3(a_s, b_s):
    return _dot(a_s[0], b_s[0]) + _dot(a_s[0], b_s[1]) + _dot(a_s[1], b_s[0])


def _mm3_tn(a_s, b_s):
    return _dot_tn(a_s[0], b_s[0]) + _dot_tn(a_s[0], b_s[1]) + _dot_tn(a_s[1], b_s[0])


def _interleave(gens):
    live = list(gens)
    while live:
        nxt = []
        for g in live:
            try:
                next(g)
                nxt.append(g)
            except StopIteration:
                pass
        live = nxt


def _segsum(x, bd):
    hi = x.astype(BF16)
    r1 = x - hi.astype(F32)
    mid = r1.astype(BF16)
    lo = (r1 - mid.astype(F32)).astype(BF16)
    return (jnp.dot(hi, bd, preferred_element_type=F32) + jnp.dot(mid, bd, preferred_element_type=F32)
            + jnp.dot(lo, bd, preferred_element_type=F32))


def _sigmoid(x):
    return 1.0 / (1.0 + jnp.exp(-x))


def _silu_grad(x, s):
    return s * (1.0 + x * (1.0 - s))


def _block_ones(n, seg):
    i = np.arange(n)
    return jnp.asarray((i[:, None] // seg == i[None, :] // seg).astype(np.float32), dtype=BF16)


def _tri_incl():
    i = np.arange(CHUNK)
    return jnp.asarray((i[:, None] >= i[None, :]).astype(np.float32))


def _t5_bucket(dist):
    max_exact = N_BUCKETS // 2
    d = np.maximum(dist, 1).astype(np.float64)
    large = max_exact + (np.log(d / max_exact) / math.log(MAX_DISTANCE / max_exact)
                         * (N_BUCKETS - max_exact)).astype(np.int32)
    large = np.minimum(large, N_BUCKETS - 1)
    return np.where(dist < max_exact, dist, large).astype(np.int32)


def _bucket_tables():
    qi = np.arange(BLK)[:, None]
    kj = np.arange(2 * BLK)[None, :]
    step = qi - kj + BLK
    return jnp.asarray(np.stack([_t5_bucket(np.clip(step, 0, None) * r) for _, r in PATTERNS]))


def _in_proj(x, norm_w, wt_bf):
    s = x.shape[0]
    tm = 512

    def body(x_ref, nw_ref, w_ref, o_ref):
        xv = x_ref[...]
        rstd = lax.rsqrt(jnp.mean(xv * xv, axis=-1, keepdims=True) + EPS)
        h = (xv * rstd * nw_ref[...]).astype(BF16)
        o_ref[...] = _dot_nt(h, w_ref[...])

    return pl.pallas_call(
        body, grid=(s // tm,),
        in_specs=[pl.BlockSpec((tm, D_MODEL), lambda i: (i, 0)),
                  pl.BlockSpec((1, D_MODEL), lambda i: (0, 0)),
                  pl.BlockSpec((D_IN_PAD, D_MODEL), lambda i: (0, 0))],
        out_specs=pl.BlockSpec((tm, D_IN_PAD), lambda i: (i, 0)),
        out_shape=_sds((s, D_IN_PAD)), compiler_params=_cp(("parallel",)), name="in_proj")(x, norm_w, wt_bf)


def _conv_group(cur, halo, w_ref, c):
    rows = cur.shape[0]
    lanes = slice(128 * c, 128 * c + 128)
    xcat = jnp.concatenate([halo, cur], axis=0)
    y = cur * w_ref[CONV_W - 1:CONV_W, lanes]
    for k in range(1, CONV_W):
        sh = pltpu.roll(xcat, k, 0)[8:8 + rows]
        y = y + sh * w_ref[CONV_W - 1 - k:CONV_W - k, lanes]
    return y


def _beta_g(ba, alog_l, dtb_l):
    lane = lax.broadcasted_iota(jnp.int32, ba.shape, 1)
    sig_b = _sigmoid(ba)
    t = ba + dtb_l
    softplus = jnp.maximum(t, 0.0) + jnp.log(1.0 + jnp.exp(-jnp.abs(t)))
    nega = -jnp.exp(alog_l)
    g = nega * softplus
    out = jnp.where(lane < DN_HEADS, sig_b, jnp.where(lane < 2 * DN_HEADS, g, 0.0))
    return out, lane, sig_b, t, nega, g


def _dn_pre(proj, conv_w8, alog_l, dtb_l):
    s = proj.shape[0]
    tr = 512
    nh = tr // 8

    def body(u_ref, halo_ref, ba_ref, w_ref, al_ref, dt_ref, q_ref, k_ref, v_ref, bg_ref):
        i = pl.program_id(0)
        keep = (i > 0).astype(F32)
        for c in range(12):
            lanes = slice(128 * c, 128 * c + 128)
            y = _conv_group(u_ref[:, lanes], halo_ref[:, lanes] * keep, w_ref, c)
            sv = y * _sigmoid(y)
            if c < 8:
                rs = lax.rsqrt(jnp.sum(sv * sv, axis=1, keepdims=True) + EPS)
                n = sv * rs
                if c < 4:
                    q_ref[:, lanes] = n * (DN_HD ** -0.5)
                else:
                    k_ref[:, slice(128 * (c - 4), 128 * (c - 3))] = n
            else:
                v_ref[:, slice(128 * (c - 8), 128 * (c - 7))] = sv
        bg_ref[...] = _beta_g(ba_ref[...], al_ref[...], dt_ref[...])[0]

    return pl.pallas_call(
        body, grid=(s // tr,),
        in_specs=[pl.BlockSpec((tr, 1536), lambda i: (i, 0)),
                  pl.BlockSpec((8, 1536), lambda i: (jnp.maximum(i * nh - 1, 0), 0)),
                  pl.BlockSpec((tr, 128), lambda i: (i, COL_BA_128)),
                  pl.BlockSpec((8, 1536), lambda i: (0, 0)),
                  pl.BlockSpec((1, 128), lambda i: (0, 0)),
                  pl.BlockSpec((1, 128), lambda i: (0, 0))],
        out_specs=[pl.BlockSpec((tr, 512), lambda i: (i, 0))] * 3 + [pl.BlockSpec((tr, 128), lambda i: (i, 0))],
        out_shape=[_sds((s, 512))] * 3 + [_sds((s, 128))],
        compiler_params=_cp(("parallel",)), name="dn_pre")(proj, proj, proj, conv_w8, alog_l, dtb_l)


CPS = 4
CPS_SCAN = 8


def _split3(a):
    hi = a.astype(BF16)
    r1 = a - hi.astype(F32)
    mid = r1.astype(BF16)
    return hi, mid, (r1 - mid.astype(F32)).astype(BF16)


def _lane_select():
    r = np.arange(128)
    return jnp.asarray((r[None, :, None] == np.arange(8)[:, None, None]) * np.ones((1, 1, 128)), dtype=BF16)


def _lane_bcast(a3, sel):
    return _dot(a3[0], sel) + _dot(a3[1], sel) + _dot(a3[2], sel)


def _rowsum_b(z, ones_b):
    hi, lo = _split(z)
    return _dot(hi, ones_b) + _dot(lo, ones_b)


def _chunk_cumsum(bg, bgt, tri):
    return _split3(bg), _split3(_mmx(tri, bg)), _mmx_nt(bgt, tri)


def _chunk_common(bg3, gc3, gc_row, h, sel_ref):
    gcc = _lane_bcast(gc3, sel_ref[DN_HEADS + h])
    beta = _lane_bcast(bg3, sel_ref[h])
    gcr = gc_row[DN_HEADS + h:DN_HEADS + h + 1, :]
    ii = lax.broadcasted_iota(jnp.int32, (CHUNK, CHUNK), 0)
    jj = lax.broadcasted_iota(jnp.int32, (CHUNK, CHUNK), 1)
    incl = ii >= jj
    strict = ii > jj
    decay = jnp.exp(jnp.where(incl, gcc[:, 0:CHUNK] - gcr, NEG))
    gl = gcc[CHUNK - 1:CHUNK, :]
    return gcc, beta, incl, strict, decay, gl


def _dn_prep(qn, kn, v, bg, bgt, tri, sel):
    s = qn.shape[0]
    nc = s // CHUNK

    def body(q_ref, k_ref, v_ref, bg_ref, bgt_ref, tri_ref, sel_ref,
             u_ref, w_ref, qd_ref, kt_ref, attn_ref, t_ref, gl_ref):
        tri_v = tri_ref[...]
        ii = lax.broadcasted_iota(jnp.int32, (CHUNK, CHUNK), 0)
        jj = lax.broadcasted_iota(jnp.int32, (CHUNK, CHUNK), 1)
        eye = (ii == jj).astype(F32)

        def chain(cc, h, bg3, gc3, gc_row):
            rows = slice(CHUNK * cc, CHUNK * cc + CHUNK)
            lanes = slice(128 * h, 128 * h + 128)
            gcc, beta, incl, strict, decay, gl = _chunk_common(bg3, gc3, gc_row, h, sel_ref)
            yield
            q = q_ref[rows, lanes]
            k = k_ref[rows, lanes]
            vv = v_ref[rows, lanes]
            kb = k * beta
            egc = jnp.exp(gcc)
            a_mat = jnp.where(strict, _mm_nt(kb, k) * decay, 0.0)
            attn_ref[cc, h] = jnp.where(incl, _mm_nt(q, k) * decay, 0.0)
            qd_ref[rows, lanes] = q * egc
            kt_ref[rows, lanes] = k * jnp.exp(gl - gcc)
            gl_ref[cc, h] = jnp.exp(gl)
            yield
            p = -a_mat
            t = eye + p
            for _ in range(5):
                ps = _split(p)
                p = _mm3(ps, ps)
                yield
                t = t + _mm3(_split(t), _split(p))
                yield
            t_ref[cc, h] = t
            ts = _split(t)
            u_ref[rows, lanes] = _mm3(ts, _split(vv * beta))
            w_ref[rows, lanes] = _mm3(ts, _split(kb * egc))

        gens = []
        for cc in range(CPS):
            bgv = bg_ref[CHUNK * cc:CHUNK * cc + CHUNK, :]
            bg3, gc3, gc_row = _chunk_cumsum(bgv, bgt_ref[cc], tri_v)
            gens += [chain(cc, h, bg3, gc3, gc_row) for h in range(DN_HEADS)]
        _interleave(gens)

    rows_step = CPS * CHUNK
    big = pl.BlockSpec((rows_step, 512), lambda n: (n, 0))
    sq = pl.BlockSpec((CPS, DN_HEADS, CHUNK, CHUNK), lambda n: (n, 0, 0, 0))
    return pl.pallas_call(
        body, grid=(nc // CPS,),
        in_specs=[big, big, big, pl.BlockSpec((rows_step, 128), lambda n: (n, 0)),
                  pl.BlockSpec((CPS, 8, CHUNK), lambda n: (n, 0, 0)),
                  pl.BlockSpec((CHUNK, CHUNK), lambda n: (0, 0)),
                  pl.BlockSpec((8, 128, 128), lambda n: (0, 0, 0))],
        out_specs=[big, big, big, big, sq, sq, pl.BlockSpec((CPS, DN_HEADS, 1, 128), lambda n: (n, 0, 0, 0))],
        out_shape=[_sds((s, 512))] * 4 + [_sds((nc, DN_HEADS, CHUNK, CHUNK))] * 2 + [_sds((nc, DN_HEADS, 1, 128))],
        compiler_params=_cp(("parallel",)), name="dn_prep")(qn, kn, v, bg, bgt, tri, sel)


def _dn_scan(u, w, qd, kt, attn, gl):
    s = u.shape[0]
    nc = s // CHUNK

    def body(u_ref, w_ref, qd_ref, kt_ref, attn_ref, gl_ref, o_ref, vn_ref, sp_ref, st_ref):
        n = pl.program_id(0)

        @pl.when(n == 0)
        def _():
            st_ref[...] = jnp.zeros_like(st_ref)

        def chain(cc, h):
            rows = slice(CHUNK * cc, CHUNK * cc + CHUNK)
            lanes = slice(128 * h, 128 * h + 128)
            st = st_ref[h]
            sp_ref[cc, h] = st
            stb = st.astype(BF16)
            ws = _dot(w_ref[rows, lanes].astype(BF16), stb)
            qs = _dot(qd_ref[rows, lanes].astype(BF16), stb)
            yield
            vn = u_ref[rows, lanes] - ws
            vn_ref[rows, lanes] = vn
            vnb = vn.astype(BF16)
            o_ref[rows, lanes] = qs + _dot(attn_ref[cc, h].astype(BF16), vnb)
            st_ref[h] = st * gl_ref[cc, h] + _dot_tn(kt_ref[rows, lanes].astype(BF16), vnb)

        for cc in range(CPS_SCAN):
            _interleave([chain(cc, h) for h in range(DN_HEADS)])

    big = pl.BlockSpec((CPS_SCAN * CHUNK, 512), lambda n: (n, 0))
    return pl.pallas_call(
        body, grid=(nc // CPS_SCAN,),
        in_specs=[big, big, big, big,
                  pl.BlockSpec((CPS_SCAN, DN_HEADS, CHUNK, CHUNK), lambda n: (n, 0, 0, 0)),
                  pl.BlockSpec((CPS_SCAN, DN_HEADS, 1, 128), lambda n: (n, 0, 0, 0))],
        out_specs=[big, big, pl.BlockSpec((CPS_SCAN, DN_HEADS, DN_HD, DN_HD), lambda n: (n, 0, 0, 0))],
        out_shape=[_sds((s, 512)), _sds((s, 512)), _sds((nc, DN_HEADS, DN_HD, DN_HD))],
        scratch_shapes=[pltpu.VMEM((DN_HEADS, DN_HD, DN_HD), F32)],
        compiler_params=_cp(("arbitrary",)), name="dn_scan")(u, w, qd, kt, attn, gl)


R4 = PATTERNS[1][1]
R16 = PATTERNS[2][1]
TM = 512


def _pattern_spec(r):
    return pl.BlockSpec((r, TM // r, 512), lambda i: (0, i, 0))


def _pattern_shape(s, r, dtype=F32):
    return _sds((r, s // r, 512), dtype)


SLABS = pltpu.VMEM((4, TM, 128), F32)


def _to_patterns(val, dsts, scr):
    for c in range(4):
        lanes = slice(128 * c, 128 * c + 128)
        scr[c] = val[:, lanes]
        for dst_ref, r in dsts:
            for a in range(r):
                dst_ref[a, :, lanes] = scr[c, pl.ds(a, TM // r, stride=r), :].astype(dst_ref.dtype)


def _from_pattern(src_ref, r, scr):
    for c in range(4):
        for a in range(r):
            scr[c, pl.ds(a, TM // r, stride=r), :] = src_ref[a, :, 128 * c:128 * c + 128]
    return jnp.concatenate([scr[c] for c in range(4)], axis=1)


def _att_pre(proj, qw_t, kw_t, bd64):
    s = proj.shape[0]

    def body(q_ref, k_ref, v_ref, qw_ref, kw_ref, bd_ref,
             q1_ref, k1_ref, v1_ref, q4_ref, k4_ref, v4_ref, q16_ref, k16_ref, v16_ref, scr):
        bd = bd_ref[...]
        q = q_ref[...]
        k = k_ref[...]
        qn = q * lax.rsqrt(_segsum(q * q, bd) * (1.0 / ATT_HD) + EPS) * qw_ref[...] * (ATT_HD ** -0.5)
        kn = k * lax.rsqrt(_segsum(k * k, bd) * (1.0 / ATT_HD) + EPS) * kw_ref[...]
        q1_ref[...] = qn.astype(BF16)
        k1_ref[...] = kn.astype(BF16)
        v1_ref[...] = v_ref[...].astype(BF16)
        _to_patterns(qn, ((q4_ref, R4), (q16_ref, R16)), scr)
        _to_patterns(kn, ((k4_ref, R4), (k16_ref, R16)), scr)
        _to_patterns(v_ref[...], ((v4_ref, R4), (v16_ref, R16)), scr)

    row = pl.BlockSpec((1, 512), lambda i: (0, 0))
    tok = pl.BlockSpec((TM, 512), lambda i: (i, 0))
    return pl.pallas_call(
        body, grid=(s // TM,),
        in_specs=[pl.BlockSpec((TM, 512), lambda i: (i, COL_ATT_Q)),
                  pl.BlockSpec((TM, 512), lambda i: (i, COL_ATT_K)),
                  pl.BlockSpec((TM, 512), lambda i: (i, COL_ATT_V)),
                  row, row, pl.BlockSpec((512, 512), lambda i: (0, 0))],
        out_specs=[tok] * 3 + [_pattern_spec(R4)] * 3 + [_pattern_spec(R16)] * 3,
        out_shape=[_sds((s, 512), BF16)] * 3 + [_pattern_shape(s, R4, BF16)] * 3 + [_pattern_shape(s, R16, BF16)] * 3,
        scratch_shapes=[SLABS],
        compiler_params=_cp(("parallel",)), name="att_pre")(proj, proj, proj, qw_t, kw_t, bd64)


def _bias_fwd(rel_bias, buckets):
    def body(rb_ref, bk_ref, o_ref):
        for p in range(len(PATTERNS)):
            bk = bk_ref[p]
            for h in range(ATT_HEADS):
                acc = jnp.zeros((BLK, 2 * BLK), F32)
                for b in range(N_BUCKETS):
                    acc = jnp.where(bk == b, rb_ref[h, b], acc)
                o_ref[p, h] = acc

    return pl.pallas_call(
        body,
        in_specs=[pl.BlockSpec(memory_space=pltpu.SMEM), pl.BlockSpec(memory_space=pltpu.VMEM)],
        out_specs=pl.BlockSpec(memory_space=pltpu.VMEM),
        out_shape=_sds((len(PATTERNS), ATT_HEADS, BLK, 2 * BLK)),
        compiler_params=_cp(), name="bias_fwd")(rel_bias, buckets)


def _bias_bwd(ds_accs, buckets):
    def body(ds0_ref, ds1_ref, ds2_ref, bk_ref, o_ref):
        for h in range(ATT_HEADS):
            for b in range(N_BUCKETS):
                tot = jnp.zeros((), F32)
                for p, ds_ref in enumerate((ds0_ref, ds1_ref, ds2_ref)):
                    tot = tot + jnp.sum(jnp.where(bk_ref[p] == b, ds_ref[h], 0.0))
                o_ref[h, b] = tot

    return pl.pallas_call(
        body,
        in_specs=[pl.BlockSpec(memory_space=pltpu.VMEM)] * 4,
        out_specs=pl.BlockSpec(memory_space=pltpu.SMEM),
        out_shape=_sds((ATT_HEADS, N_BUCKETS)),
        compiler_params=_cp(), name="bias_bwd")(*ds_accs, buckets)


QB = 4


def _att_masks(has_prev):
    qi = lax.broadcasted_iota(jnp.int32, (BLK, BLK), 0)
    kj = lax.broadcasted_iota(jnp.int32, (BLK, BLK), 1)
    lane = lax.broadcasted_iota(jnp.int32, (BLK, 2 * ATT_HD), 1)
    return jnp.logical_and(kj >= qi, has_prev), kj <= qi, lane < ATT_HD


def _head_lanes(h):
    half = h % 2
    return slice(ATT_HD * h, ATT_HD * h + ATT_HD), slice(ATT_HD * half, ATT_HD * half + ATT_HD)


def _att_scores(qm, kp2, kc2, bias_h, mask_prev, mask_cur):
    s_prev = jnp.where(mask_prev, _dot_nt(qm, kp2) + bias_h[:, :BLK], NEG)
    s_cur = jnp.where(mask_cur, _dot_nt(qm, kc2) + bias_h[:, BLK:], NEG)
    return s_prev, s_cur


def _att_fwd(q, k, v, v_col, bias, p_idx, r, name):
    s = q.shape[0]
    nblk = s // BLK
    nseq = nblk // r

    def body(q_ref, kp_ref, kc_ref, vp_ref, vc_ref, b_ref, o_ref, lse_ref):
        j = pl.program_id(0)

        def head(h, rows, masks, q2, kp2, kc2, vp2, vc2):
            mask_prev, mask_cur, lo_half = masks
            out_l, pair_l = _head_lanes(h)
            sel = lo_half if h % 2 == 0 else jnp.logical_not(lo_half)
            qm = jnp.where(sel, q2, jnp.zeros_like(q2))
            s_prev, s_cur = _att_scores(qm, kp2, kc2, b_ref[0, h], mask_prev, mask_cur)
            yield
            m = jnp.maximum(jnp.max(s_prev, axis=1, keepdims=True), jnp.max(s_cur, axis=1, keepdims=True))
            p_prev = jnp.exp(s_prev - m)
            p_cur = jnp.exp(s_cur - m)
            l = jnp.sum(p_prev, axis=1, keepdims=True) + jnp.sum(p_cur, axis=1, keepdims=True)
            yield
            o2 = _dot(p_prev.astype(BF16), vp2) + _dot(p_cur.astype(BF16), vc2)
            o_ref[rows, out_l] = (o2 * (1.0 / l))[:, pair_l]
            lse_ref[rows, out_l] = jnp.broadcast_to(m + jnp.log(l), (BLK, ATT_HD))

        for sub in range(QB):
            rows = slice(BLK * sub, BLK * sub + BLK)
            before = slice(BLK * (sub - 1), BLK * sub)
            masks = _att_masks(((QB * j + sub) % nseq) != 0)
            gens = []
            for pp in range(ATT_HEADS // 2):
                lanes = slice(128 * pp, 128 * pp + 128)
                kp = kp_ref[:, lanes] if sub == 0 else kc_ref[before, lanes]
                vp = vp_ref[:, lanes] if sub == 0 else vc_ref[before, lanes]
                slabs = (q_ref[rows, lanes], kp, kc_ref[rows, lanes], vp, vc_ref[rows, lanes])
                gens += [head(2 * pp, rows, masks, *slabs), head(2 * pp + 1, rows, masks, *slabs)]
            _interleave(gens)

    cur = pl.BlockSpec((QB * BLK, 512), lambda j: (j, 0))
    prev = pl.BlockSpec((BLK, 512), lambda j: (jnp.maximum(QB * j - 1, 0), 0))
    vcur = pl.BlockSpec((QB * BLK, 512), lambda j: (j, v_col))
    vprev = pl.BlockSpec((BLK, 512), lambda j: (jnp.maximum(QB * j - 1, 0), v_col))
    return pl.pallas_call(
        body, grid=(nblk // QB,),
        in_specs=[cur, prev, cur, vprev, vcur,
                  pl.BlockSpec((1, ATT_HEADS, BLK, 2 * BLK), lambda j: (p_idx, 0, 0, 0))],
        out_specs=[cur, cur],
        out_shape=[_sds((s, 512))] * 2,
        compiler_params=_cp(("parallel",)), name=name)(q, k, k, v, v, bias)


def _post_fwd(o_dn, proj, o_pats, lse_pats, dnw_t, bd128):
    s = o_dn.shape[0]

    def body(o_ref, z_ref, gate_ref, o1_ref, o4_ref, o16_ref, s1_ref, s4_ref, s16_ref, wn_ref, bd_ref,
             mixed_ref, oatt_ref, l1_ref, l4_ref, l16_ref, scr_a, scr_b, scr_c, scr_d):
        o = o_ref[...]
        z = z_ref[...]
        rstd = lax.rsqrt(_segsum(o * o, bd_ref[...]) * (1.0 / DN_HD) + EPS)
        y_dn = o * rstd * wn_ref[...] * (z * _sigmoid(z))
        mixed_ref[:, 0:512] = y_dn.astype(BF16)
        lses = (s1_ref[...], _from_pattern(s4_ref, R4, scr_a), _from_pattern(s16_ref, R16, scr_b))
        outs = (o1_ref[...], _from_pattern(o4_ref, R4, scr_c), _from_pattern(o16_ref, R16, scr_d))
        m = jnp.maximum(jnp.maximum(lses[0], lses[1]), lses[2])
        tot = jnp.exp(lses[0] - m) + jnp.exp(lses[1] - m) + jnp.exp(lses[2] - m)
        big_l = m + jnp.log(tot)
        acc = jnp.zeros_like(m)
        for lse_p, o_p in zip(lses, outs):
            acc = acc + jnp.exp(lse_p - big_l) * o_p
        gate = gate_ref[...]
        oatt_ref[...] = acc
        l1_ref[...] = big_l
        mixed_ref[:, 512:1024] = (acc * (gate * _sigmoid(gate))).astype(BF16)
        _to_patterns(big_l, ((l4_ref, R4), (l16_ref, R16)), scr_a)

    blk = pl.BlockSpec((TM, 512), lambda i: (i, 0))
    p4, p16 = _pattern_spec(R4), _pattern_spec(R16)
    return pl.pallas_call(
        body, grid=(s // TM,),
        in_specs=[blk, pl.BlockSpec((TM, 512), lambda i: (i, COL_Z)),
                  pl.BlockSpec((TM, 512), lambda i: (i, COL_GATE)), blk, p4, p16, blk, p4, p16,
                  pl.BlockSpec((1, 512), lambda i: (0, 0)), pl.BlockSpec((512, 512), lambda i: (0, 0))],
        out_specs=[pl.BlockSpec((TM, D_MODEL), lambda i: (i, 0)), blk, blk, p4, p16],
        out_shape=[_sds((s, D_MODEL), BF16), _sds((s, 512)), _sds((s, 512)), _pattern_shape(s, R4),
                   _pattern_shape(s, R16)],
        scratch_shapes=[SLABS] * 4,
        compiler_params=_cp(("parallel",)), name="post_fwd")(o_dn, proj, proj, *o_pats, *lse_pats, dnw_t, bd128)


def _out_fwd(x, mixed, w_out_bf, tgt):
    s = x.shape[0]
    tm = 512

    def body(x_ref, m_ref, w_ref, t_ref, dy_ref, loss_ref):
        i = pl.program_id(0)

        @pl.when(i == 0)
        def _():
            loss_ref[...] = jnp.zeros_like(loss_ref)

        y = x_ref[...] + jnp.dot(m_ref[...], w_ref[...], preferred_element_type=F32)
        err = y - t_ref[...]
        dy_ref[...] = err * (1.0 / D_MODEL)
        part = 0.5 * jnp.sum(jnp.mean(err * err, axis=-1, keepdims=True), axis=0, keepdims=True)
        loss_ref[...] = loss_ref[...] + part

    blk = pl.BlockSpec((tm, D_MODEL), lambda i: (i, 0))
    return pl.pallas_call(
        body, grid=(s // tm,),
        in_specs=[blk, blk, pl.BlockSpec((D_MODEL, D_MODEL), lambda i: (0, 0)), blk],
        out_specs=[blk, pl.BlockSpec((8, 128), lambda i: (0, 0))],
        out_shape=[_sds((s, D_MODEL)), _sds((8, 128))],
        compiler_params=_cp(("arbitrary",)), name="out_fwd")(x, mixed, w_out_bf, tgt)


def _out_bwd(dy, mixed, w_out_bf):
    s = dy.shape[0]
    tm = 512

    def body(dy_ref, m_ref, w_ref, dm_ref, dw_ref):
        i = pl.program_id(0)

        @pl.when(i == 0)
        def _():
            dw_ref[...] = jnp.zeros_like(dw_ref)

        dyb = dy_ref[...].astype(BF16)
        dm_ref[...] = lax.dot_general(dyb, w_ref[...], (((1,), (1,)), ((), ())), preferred_element_type=F32)
        dw_ref[...] = dw_ref[...] + lax.dot_general(m_ref[...], dyb, (((0,), (0,)), ((), ())),
                                                    preferred_element_type=F32)

    blk = pl.BlockSpec((tm, D_MODEL), lambda i: (i, 0))
    full = pl.BlockSpec((D_MODEL, D_MODEL), lambda i: (0, 0))
    return pl.pallas_call(
        body, grid=(s // tm,), in_specs=[blk, blk, full], out_specs=[blk, full],
        out_shape=[_sds((s, D_MODEL)), _sds((D_MODEL, D_MODEL))],
        compiler_params=_cp(("arbitrary",)), name="out_bwd")(dy, mixed, w_out_bf)


def _post_bwd(dmixed, o_dn, proj, o_att, dnw_t, bd128, bd64):
    s = o_dn.shape[0]
    tm = TM

    def body(ddn_ref, datt_ref, o_ref, z_ref, gate_ref, oatt_ref, wn_ref, bd128_ref, bd64_ref,
             do_ref, dz_ref, dgate_ref, doatt_ref, do4_ref, do16_ref, delta_ref, dl4_ref, dl16_ref, dnw_ref, scr):
        i = pl.program_id(0)

        @pl.when(i == 0)
        def _():
            dnw_ref[...] = jnp.zeros_like(dnw_ref)

        bd128v = bd128_ref[...]
        o = o_ref[...]
        z = z_ref[...]
        wn = wn_ref[...]
        dy = ddn_ref[...]
        rstd = lax.rsqrt(_segsum(o * o, bd128v) * (1.0 / DN_HD) + EPS)
        nrm = o * rstd
        sz = _sigmoid(z)
        dz_ref[...] = dy * nrm * wn * _silu_grad(z, sz)
        dn = dy * z * sz
        gw = dn * wn
        do_ref[...] = rstd * (gw - nrm * (_segsum(gw * nrm, bd128v) * (1.0 / DN_HD)))
        colsum = jnp.sum(dn * nrm, axis=0, keepdims=True)
        fold = colsum[:, 0:128] + colsum[:, 128:256] + colsum[:, 256:384] + colsum[:, 384:512]
        dnw_ref[...] = dnw_ref[...] + fold
        dya = datt_ref[...]
        gate = gate_ref[...]
        oatt = oatt_ref[...]
        sg = _sigmoid(gate)
        dgate_ref[...] = dya * oatt * _silu_grad(gate, sg)
        doa = dya * gate * sg
        doatt_ref[...] = doa.astype(BF16)
        delta = _segsum(doa * oatt, bd64_ref[...])
        delta_ref[...] = delta
        _to_patterns(doa, ((do4_ref, R4), (do16_ref, R16)), scr)
        _to_patterns(delta, ((dl4_ref, R4), (dl16_ref, R16)), scr)

    blk = pl.BlockSpec((tm, 512), lambda i: (i, 0))
    cst = pl.BlockSpec((512, 512), lambda i: (0, 0))
    p4, p16 = _pattern_spec(R4), _pattern_spec(R16)
    return pl.pallas_call(
        body, grid=(s // tm,),
        in_specs=[blk, pl.BlockSpec((tm, 512), lambda i: (i, 1)), blk,
                  pl.BlockSpec((tm, 512), lambda i: (i, COL_Z)), pl.BlockSpec((tm, 512), lambda i: (i, COL_GATE)),
                  blk, pl.BlockSpec((1, 512), lambda i: (0, 0)), cst, cst],
        out_specs=[blk, blk, blk, blk, p4, p16, blk, p4, p16, pl.BlockSpec((8, 128), lambda i: (0, 0))],
        out_shape=[_sds((s, 512))] * 3 + [_sds((s, 512), BF16), _pattern_shape(s, R4, BF16),
                                          _pattern_shape(s, R16, BF16), _sds((s, 512)),
                                          _pattern_shape(s, R4), _pattern_shape(s, R16), _sds((8, 128))],
        scratch_shapes=[SLABS],
        compiler_params=_cp(("arbitrary",)), name="post_bwd")(dmixed, dmixed, o_dn, proj, proj, o_att, dnw_t,
                                                               bd128, bd64)


def _att_bwd(q, k, v, v_col, do, big_l, delta, bias, p_idx, r, name):
    s = q.shape[0]
    nblk = s // BLK
    nseq = nblk // r
    nstep = nblk // QB

    def body(q_ref, kp_ref, kc_ref, vp_ref, vc_ref, do_ref, l_ref, dl_ref, b_ref,
             dq_ref, dk_ref, dv_ref, ds_ref, dkc_ref, dvc_ref):
        j = pl.program_id(0)

        @pl.when(j == 0)
        def _():
            dkc_ref[...] = jnp.zeros_like(dkc_ref)
            dvc_ref[...] = jnp.zeros_like(dvc_ref)
            ds_ref[...] = jnp.zeros_like(ds_ref)

        @pl.when(j < nstep)
        def _():
            def head(h, sub, masks, q2, do2, kp2, kc2, vp2, vc2):
                mask_prev, mask_cur, lo_half = masks
                rows = slice(BLK * sub, BLK * sub + BLK)
                out_l, pair_l = _head_lanes(h)
                sel = lo_half if h % 2 == 0 else jnp.logical_not(lo_half)
                qm = jnp.where(sel, q2, jnp.zeros_like(q2))
                dom = jnp.where(sel, do2, jnp.zeros_like(do2))
                s_prev, s_cur = _att_scores(qm, kp2, kc2, b_ref[0, h], mask_prev, mask_cur)
                dp_prev = _dot_nt(dom, vp2)
                dp_cur = _dot_nt(dom, vc2)
                yield
                lh = l_ref[rows, ATT_HD * h:ATT_HD * h + 1]
                dh = dl_ref[rows, ATT_HD * h:ATT_HD * h + 1]
                p_prev = jnp.exp(s_prev - lh)
                p_cur = jnp.exp(s_cur - lh)
                ds_prev = p_prev * (dp_prev - dh)
                ds_cur = p_cur * (dp_cur - dh)
                ds_ref[h, :, 0:BLK] = ds_ref[h, :, 0:BLK] + ds_prev
                ds_ref[h, :, BLK:2 * BLK] = ds_ref[h, :, BLK:2 * BLK] + ds_cur
                dsb_prev, dsb_cur = ds_prev.astype(BF16), ds_cur.astype(BF16)
                pb_prev, pb_cur = p_prev.astype(BF16), p_cur.astype(BF16)
                yield
                dq_ref[rows, out_l] = (_dot(dsb_prev, kp2) + _dot(dsb_cur, kc2))[:, pair_l]
                dk_prev = _dot_tn(dsb_prev, q2)[:, pair_l]
                dv_prev = _dot_tn(pb_prev, do2)[:, pair_l]
                if sub == 0:
                    last = slice(BLK * (QB - 1), BLK * QB)
                    dk_ref[last, out_l] = dkc_ref[last, out_l] + dk_prev
                    dv_ref[last, out_l] = dvc_ref[last, out_l] + dv_prev
                else:
                    before = slice(BLK * (sub - 1), BLK * sub)
                    dkc_ref[before, out_l] = dkc_ref[before, out_l] + dk_prev
                    dvc_ref[before, out_l] = dvc_ref[before, out_l] + dv_prev
                yield
                dkc_ref[rows, out_l] = _dot_tn(dsb_cur, q2)[:, pair_l]
                dvc_ref[rows, out_l] = _dot_tn(pb_cur, do2)[:, pair_l]

            done = slice(0, BLK * (QB - 1))
            dk_ref[done, :] = dkc_ref[done, :]
            dv_ref[done, :] = dvc_ref[done, :]
            for sub in range(QB):
                rows = slice(BLK * sub, BLK * sub + BLK)
                before = slice(BLK * (sub - 1), BLK * sub)
                masks = _att_masks(((QB * j + sub) % nseq) != 0)
                gens = []
                for pp in range(ATT_HEADS // 2):
                    lanes = slice(128 * pp, 128 * pp + 128)
                    kp = kp_ref[:, lanes] if sub == 0 else kc_ref[before, lanes]
                    vp = vp_ref[:, lanes] if sub == 0 else vc_ref[before, lanes]
                    slabs = (q_ref[rows, lanes], do_ref[rows, lanes], kp, kc_ref[rows, lanes], vp,
                             vc_ref[rows, lanes])
                    gens += [head(2 * pp, sub, masks, *slabs), head(2 * pp + 1, sub, masks, *slabs)]
                _interleave(gens)

        @pl.when(j == nstep)
        def _():
            dk_ref[...] = dkc_ref[...]
            dv_ref[...] = dvc_ref[...]

    last_step = nstep - 1
    cur = pl.BlockSpec((QB * BLK, 512), lambda j: (jnp.minimum(j, last_step), 0))
    lag = pl.BlockSpec((QB * BLK, 512), lambda j: (jnp.clip(j - 1, 0, last_step), 0))
    prev = pl.BlockSpec((BLK, 512), lambda j: (jnp.clip(QB * j - 1, 0, nblk - 1), 0))
    vcur = pl.BlockSpec((QB * BLK, 512), lambda j: (jnp.minimum(j, last_step), v_col))
    vprev = pl.BlockSpec((BLK, 512), lambda j: (jnp.clip(QB * j - 1, 0, nblk - 1), v_col))
    return pl.pallas_call(
        body, grid=(nstep + 1,),
        in_specs=[cur, prev, cur, vprev, vcur, cur, cur, cur,
                  pl.BlockSpec((1, ATT_HEADS, BLK, 2 * BLK), lambda j: (p_idx, 0, 0, 0))],
        out_specs=[cur, lag, lag, pl.BlockSpec((ATT_HEADS, BLK, 2 * BLK), lambda j: (0, 0, 0))],
        out_shape=[_sds((s, 512))] * 3 + [_sds((ATT_HEADS, BLK, 2 * BLK))],
        scratch_shapes=[pltpu.VMEM((QB * BLK, 512), F32), pltpu.VMEM((QB * BLK, 512), F32)],
        compiler_params=_cp(("arbitrary",)), name=name)(q, k, k, v, v, do, big_l, delta, bias)


def _att_pre_bwd(dq_pats, dk_pats, dv_pats, proj, qw_t, kw_t, bd64):
    s = proj.shape[0]
    tm = TM

    def body(dq1_ref, dq4_ref, dq16_ref, dk1_ref, dk4_ref, dk16_ref, dv1_ref, dv4_ref, dv16_ref,
             q_ref, k_ref, qw_ref, kw_ref, bd_ref,
             dqr_ref, dkr_ref, dvr_ref, dqw_ref, dkw_ref, scr4, scr16):
        i = pl.program_id(0)

        @pl.when(i == 0)
        def _():
            dqw_ref[...] = jnp.zeros_like(dqw_ref)
            dkw_ref[...] = jnp.zeros_like(dkw_ref)

        bd = bd_ref[...]

        def total(d1_ref, d4_ref, d16_ref):
            return d1_ref[...] + _from_pattern(d4_ref, R4, scr4) + _from_pattern(d16_ref, R16, scr16)

        def one(d_refs, x_ref, w_ref, scale, dx_ref, dw_ref):
            dy = total(*d_refs) * scale
            x = x_ref[...]
            rstd = lax.rsqrt(_segsum(x * x, bd) * (1.0 / ATT_HD) + EPS)
            nrm = x * rstd
            dw_ref[...] = dw_ref[...] + jnp.sum(dy * nrm, axis=0, keepdims=True)
            g = dy * w_ref[...]
            dx_ref[...] = rstd * (g - nrm * (_segsum(g * nrm, bd) * (1.0 / ATT_HD)))

        one((dq1_ref, dq4_ref, dq16_ref), q_ref, qw_ref, ATT_HD ** -0.5, dqr_ref, dqw_ref)
        one((dk1_ref, dk4_ref, dk16_ref), k_ref, kw_ref, 1.0, dkr_ref, dkw_ref)
        dvr_ref[...] = total(dv1_ref, dv4_ref, dv16_ref)

    blk = pl.BlockSpec((tm, 512), lambda i: (i, 0))
    pats = [blk, _pattern_spec(R4), _pattern_spec(R16)]
    row = pl.BlockSpec((1, 512), lambda i: (0, 0))
    acc = pl.BlockSpec((8, 512), lambda i: (0, 0))
    return pl.pallas_call(
        body, grid=(s // tm,),
        in_specs=pats * 3 + [pl.BlockSpec((tm, 512), lambda i: (i, COL_ATT_Q)),
                             pl.BlockSpec((tm, 512), lambda i: (i, COL_ATT_K)), row, row,
                             pl.BlockSpec((512, 512), lambda i: (0, 0))],
        out_specs=[blk, blk, blk, acc, acc],
        out_shape=[_sds((s, 512))] * 3 + [_sds((8, 512))] * 2,
        scratch_shapes=[SLABS] * 2,
        compiler_params=_cp(("arbitrary",)), name="att_pre_bwd")(*dq_pats, *dk_pats, *dv_pats, proj, proj,
                                                                  qw_t, kw_t, bd64)


def _dn_scan_bwd(do, sp, qd, kt, w, vn, attn, gl):
    s = do.shape[0]
    nc = s // CHUNK

    def body(do_ref, sp_ref, qd_ref, kt_ref, w_ref, vn_ref, attn_ref, gl_ref,
             du_ref, dqd_ref, dkt_ref, dw_ref, dattn_ref, dgl_ref, ds_ref):
        n = pl.program_id(0)

        @pl.when(n == 0)
        def _():
            ds_ref[...] = jnp.zeros_like(ds_ref)

        def chain(cc, h):
            rows = slice(CHUNK * cc, CHUNK * cc + CHUNK)
            lanes = slice(128 * h, 128 * h + 128)
            dsn = ds_ref[h]
            st = sp_ref[cc, h]
            dsb, stb = dsn.astype(BF16), st.astype(BF16)
            dob = do_ref[rows, lanes].astype(BF16)
            vnb = vn_ref[rows, lanes].astype(BF16)
            dvn = _dot_tn(attn_ref[cc, h].astype(BF16), dob) + _dot(kt_ref[rows, lanes].astype(BF16), dsb)
            du_ref[rows, lanes] = dvn
            dqd_ref[rows, lanes] = _dot_nt(dob, stb)
            dattn_ref[cc, h] = _dot_nt(dob, vnb)
            dkt_ref[rows, lanes] = _dot_nt(vnb, dsb)
            tot = jnp.sum(jnp.sum(st * dsn, axis=1, keepdims=True), axis=0, keepdims=True)
            dgl_ref[cc, h] = jnp.broadcast_to(tot, (1, 128))
            qdo = _dot_tn(qd_ref[rows, lanes].astype(BF16), dob)
            yield
            dvb = dvn.astype(BF16)
            dw_ref[rows, lanes] = -_dot_nt(dvb, stb)
            ds_ref[h] = qdo + dsn * gl_ref[cc, h] - _dot_tn(w_ref[rows, lanes].astype(BF16), dvb)

        for cc in reversed(range(CPS_SCAN)):
            _interleave([chain(cc, h) for h in range(DN_HEADS)])

    nsteps = nc // CPS_SCAN
    big = pl.BlockSpec((CPS_SCAN * CHUNK, 512), lambda n: (nsteps - 1 - n, 0))
    sq = pl.BlockSpec((CPS_SCAN, DN_HEADS, CHUNK, CHUNK), lambda n: (nsteps - 1 - n, 0, 0, 0))
    glb = pl.BlockSpec((CPS_SCAN, DN_HEADS, 1, 128), lambda n: (nsteps - 1 - n, 0, 0, 0))
    return pl.pallas_call(
        body, grid=(nsteps,),
        in_specs=[big, pl.BlockSpec((CPS_SCAN, DN_HEADS, DN_HD, DN_HD), lambda n: (nsteps - 1 - n, 0, 0, 0)),
                  big, big, big, big, sq, glb],
        out_specs=[big, big, big, big, sq, glb],
        out_shape=[_sds((s, 512))] * 4 + [_sds((nc, DN_HEADS, CHUNK, CHUNK)), _sds((nc, DN_HEADS, 1, 128))],
        scratch_shapes=[pltpu.VMEM((DN_HEADS, DN_HD, DN_HD), F32)],
        compiler_params=_cp(("arbitrary",)), name="dn_scan_bwd")(do, sp, qd, kt, w, vn, attn, gl)


def _dn_prep_bwd(qn, kn, v, bg, bgt, tri, sel, t_inv, attn, u, w, du, dw, dqd, dkt, dattn, dgl):
    s = qn.shape[0]
    nc = s // CHUNK

    def body(q_ref, k_ref, v_ref, bg_ref, bgt_ref, tri_ref, sel_ref, t_ref, attn_ref, u_ref, w_ref,
             du_ref, dw_ref, dqd_ref, dkt_ref, dattn_ref, dgl_ref,
             dq_ref, dk_ref, dv_ref, dbg_ref):
        tri_v = tri_ref[...]
        lane = lax.broadcasted_iota(jnp.int32, (CHUNK, 128), 1)
        rowi = lax.broadcasted_iota(jnp.int32, (CHUNK, 128), 0)
        ones_b = jnp.ones((CHUNK, 128), BF16)
        ones_sq = jnp.ones((128, 128), BF16)
        parts = [[] for _ in range(CPS)]

        def chain(cc, h, bg3, gc3, gc_row):
            rows = slice(CHUNK * cc, CHUNK * cc + CHUNK)
            lanes = slice(128 * h, 128 * h + 128)
            gcc, beta, incl, strict, decay, gl = _chunk_common(bg3, gc3, gc_row, h, sel_ref)
            yield
            q = q_ref[rows, lanes]
            k = k_ref[rows, lanes]
            vv = v_ref[rows, lanes]
            ts = _split(t_ref[cc, h])
            egc = jnp.exp(gcc)
            kb = k * beta
            a_mat = jnp.where(strict, _mm_nt(kb, k) * decay, 0.0)
            dvb = _mm3_tn(ts, _split(du_ref[rows, lanes]))
            dkbg = _mm3_tn(ts, _split(dw_ref[rows, lanes]))
            yield
            d_a = jnp.where(strict, -(_mm_nt(dvb, u_ref[rows, lanes]) + _mm_nt(dkbg, w_ref[rows, lanes])), 0.0)
            d_m = d_a * decay
            dattn_m = jnp.where(incl, dattn_ref[cc, h], 0.0)
            dqk = dattn_m * decay
            e_hi, e_lo = _split(d_a * a_mat + dattn_m * attn_ref[cc, h])
            yield
            dkb = _mm(d_m, k)
            dk = _mm_tn(d_m, kb) + _mm_tn(dqk, q)
            dq = _mm(dqk, k)
            e_colsum = _dot_tn(e_hi, ones_b) + _dot_tn(e_lo, ones_b)
            e_rowsum = _dot(e_hi, ones_b) + _dot(e_lo, ones_b)
            dqd = dqd_ref[rows, lanes]
            dkt = dkt_ref[rows, lanes]
            s_dqd = _rowsum_b(dqd * q, ones_sq)
            s_dkt = _rowsum_b(dkt * k, ones_sq)
            rk = _rowsum_b(dkbg * k, ones_sq)
            s_dkb = _rowsum_b(dkb * k, ones_sq)
            s_dvb = _rowsum_b(dvb * vv, ones_sq)
            yield
            tail = jnp.exp(gl - gcc)
            r = s_dkt * tail
            dgl_tot = jnp.sum(r, axis=0, keepdims=True) + dgl_ref[cc, h] * jnp.exp(gl)
            dgc = e_rowsum - e_colsum + s_dqd * egc - r + rk * beta * egc
            dgc = dgc + jnp.where(rowi == CHUNK - 1, dgl_tot, 0.0)
            dq_ref[rows, lanes] = dq + dqd * egc
            dk_ref[rows, lanes] = dk + dkt * tail + dkbg * (beta * egc) + dkb * beta
            dv_ref[rows, lanes] = dvb * beta
            parts[cc].append((h, dgc, rk * egc + s_dkb + s_dvb))

        gens = []
        for cc in range(CPS):
            bgv = bg_ref[CHUNK * cc:CHUNK * cc + CHUNK, :]
            bg3, gc3, gc_row = _chunk_cumsum(bgv, bgt_ref[cc], tri_v)
            gens += [chain(cc, h, bg3, gc3, gc_row) for h in range(DN_HEADS)]
        _interleave(gens)
        for cc in range(CPS):
            dgc_mat = jnp.zeros((CHUNK, 128), F32)
            dbeta_mat = jnp.zeros((CHUNK, 128), F32)
            for h, dgc, dbeta in parts[cc]:
                dgc_mat = dgc_mat + jnp.where(lane == DN_HEADS + h, dgc, 0.0)
                dbeta_mat = dbeta_mat + jnp.where(lane == h, dbeta, 0.0)
            dbg_ref[CHUNK * cc:CHUNK * cc + CHUNK, :] = _mmx_tn(tri_v, dgc_mat) + dbeta_mat

    big = pl.BlockSpec((CPS * CHUNK, 512), lambda n: (n, 0))
    sq = pl.BlockSpec((CPS, DN_HEADS, CHUNK, CHUNK), lambda n: (n, 0, 0, 0))
    glb = pl.BlockSpec((CPS, DN_HEADS, 1, 128), lambda n: (n, 0, 0, 0))
    small = pl.BlockSpec((CPS * CHUNK, 128), lambda n: (n, 0))
    return pl.pallas_call(
        body, grid=(nc // CPS,),
        in_specs=[big, big, big, small, pl.BlockSpec((CPS, 8, CHUNK), lambda n: (n, 0, 0)),
                  pl.BlockSpec((CHUNK, CHUNK), lambda n: (0, 0)),
                  pl.BlockSpec((8, 128, 128), lambda n: (0, 0, 0)), sq, sq, big, big,
                  big, big, big, big, sq, glb],
        out_specs=[big, big, big, small],
        out_shape=[_sds((s, 512))] * 3 + [_sds((s, 128))],
        compiler_params=_cp(("parallel",)), name="dn_prep_bwd")(qn, kn, v, bg, bgt, tri, sel, t_inv, attn, u, w,
                                                                  du, dw, dqd, dkt, dattn, dgl)


def _dn_pre_bwd(dqn, dkn, dv, dbg, proj, conv_w8, alog_l, dtb_l):
    s = proj.shape[0]
    tr = 512
    nh = tr // 8

    def body(dq_ref, dk_ref, dv_ref, dbg_ref, u_ref, halo_ref, ba_ref, w_ref, al_ref, dt_ref,
             dy_ref, dba_ref, dsm_ref):
        i = pl.program_id(0)

        @pl.when(i == 0)
        def _():
            dsm_ref[...] = jnp.zeros_like(dsm_ref)

        keep = (i > 0).astype(F32)
        for c in range(12):
            lanes = slice(128 * c, 128 * c + 128)
            y = _conv_group(u_ref[:, lanes], halo_ref[:, lanes] * keep, w_ref, c)
            sg = _sigmoid(y)
            sv = y * sg
            if c < 8:
                rs = lax.rsqrt(jnp.sum(sv * sv, axis=1, keepdims=True) + EPS)
                n = sv * rs
                if c < 4:
                    dn = dq_ref[:, lanes] * (DN_HD ** -0.5)
                else:
                    dn = dk_ref[:, slice(128 * (c - 4), 128 * (c - 3))]
                dsv = rs * (dn - n * jnp.sum(dn * n, axis=1, keepdims=True))
            else:
                dsv = dv_ref[:, slice(128 * (c - 8), 128 * (c - 7))]
            dy_ref[:, lanes] = dsv * _silu_grad(y, sg)
        _, lane, sig_b, t, nega, g = _beta_g(ba_ref[...], al_ref[...], dt_ref[...])
        dbg = dbg_ref[...]
        da = dbg * nega * _sigmoid(t)
        is_b = lane < DN_HEADS
        is_a = jnp.logical_and(lane >= DN_HEADS, lane < 2 * DN_HEADS)
        dba_ref[...] = jnp.where(is_b, dbg * sig_b * (1.0 - sig_b), jnp.where(is_a, da, 0.0))
        d_alog = jnp.sum(jnp.where(is_a, dbg * g, 0.0), axis=0, keepdims=True)
        d_dtb = jnp.sum(jnp.where(is_a, da, 0.0), axis=0, keepdims=True)
        row = lax.broadcasted_iota(jnp.int32, (8, 128), 0)
        dsm_ref[...] = dsm_ref[...] + jnp.where(row == 0, d_alog, jnp.where(row == 1, d_dtb, 0.0))

    blk = pl.BlockSpec((tr, 512), lambda i: (i, 0))
    return pl.pallas_call(
        body, grid=(s // tr,),
        in_specs=[blk, blk, blk, pl.BlockSpec((tr, 128), lambda i: (i, 0)),
                  pl.BlockSpec((tr, 1536), lambda i: (i, 0)),
                  pl.BlockSpec((8, 1536), lambda i: (jnp.maximum(i * nh - 1, 0), 0)),
                  pl.BlockSpec((tr, 128), lambda i: (i, COL_BA_128)),
                  pl.BlockSpec((8, 1536), lambda i: (0, 0)),
                  pl.BlockSpec((1, 128), lambda i: (0, 0)), pl.BlockSpec((1, 128), lambda i: (0, 0))],
        out_specs=[pl.BlockSpec((tr, 1536), lambda i: (i, 0)), pl.BlockSpec((tr, 128), lambda i: (i, 0)),
                   pl.BlockSpec((8, 128), lambda i: (0, 0))],
        out_shape=[_sds((s, 1536)), _sds((s, 128)), _sds((8, 128))],
        compiler_params=_cp(("arbitrary",)), name="dn_pre_bwd")(dqn, dkn, dv, dbg, proj, proj, proj, conv_w8,
                                                                 alog_l, dtb_l)


def _conv_bwd(dy, proj, conv_w8):
    s = dy.shape[0]
    tr = 512
    nh = tr // 8
    nblk = s // tr

    def body(dy_ref, dyn_ref, u_ref, halo_ref, w_ref, du_ref, dw_ref):
        i = pl.program_id(0)

        @pl.when(i == 0)
        def _():
            dw_ref[...] = jnp.zeros_like(dw_ref)

        keep_prev = (i > 0).astype(F32)
        keep_next = (i < nblk - 1).astype(F32)
        row = lax.broadcasted_iota(jnp.int32, (8, 128), 0)
        for c in range(12):
            lanes = slice(128 * c, 128 * c + 128)
            dyc = dy_ref[:, lanes]
            dcat = jnp.concatenate([dyc, dyn_ref[:, lanes] * keep_next], axis=0)
            xcat = jnp.concatenate([halo_ref[:, lanes] * keep_prev, u_ref[:, lanes]], axis=0)
            du = dyc * w_ref[CONV_W - 1:CONV_W, lanes]
            dwc = jnp.where(row == CONV_W - 1, jnp.sum(dyc * u_ref[:, lanes], axis=0, keepdims=True), 0.0)
            for k in range(1, CONV_W):
                du = du + pltpu.roll(dcat, tr + 8 - k, 0)[0:tr] * w_ref[CONV_W - 1 - k:CONV_W - k, lanes]
                ush = pltpu.roll(xcat, k, 0)[8:8 + tr]
                dwc = dwc + jnp.where(row == CONV_W - 1 - k, jnp.sum(dyc * ush, axis=0, keepdims=True), 0.0)
            du_ref[:, lanes] = du
            dw_ref[:, lanes] = dw_ref[:, lanes] + dwc

    return pl.pallas_call(
        body, grid=(nblk,),
        in_specs=[pl.BlockSpec((tr, 1536), lambda i: (i, 0)),
                  pl.BlockSpec((8, 1536), lambda i: (jnp.minimum((i + 1) * nh, s // 8 - 1), 0)),
                  pl.BlockSpec((tr, 1536), lambda i: (i, 0)),
                  pl.BlockSpec((8, 1536), lambda i: (jnp.maximum(i * nh - 1, 0), 0)),
                  pl.BlockSpec((8, 1536), lambda i: (0, 0))],
        out_specs=[pl.BlockSpec((tr, 1536), lambda i: (i, 0)), pl.BlockSpec((8, 1536), lambda i: (0, 0))],
        out_shape=[_sds((s, 1536)), _sds((8, 1536))],
        compiler_params=_cp(("arbitrary",)), name="conv_bwd")(dy, dy, proj, proj, conv_w8)


def _in_bwd_dx(d_qkv_dn, dz, dq_att, dk_att, dv_att, dgate, dba, w_bf, x, dy, norm_w):
    s = x.shape[0]
    tm = 256

    def body(a_ref, b_ref, c_ref, d_ref, e_ref, f_ref, g_ref, w_ref, x_ref, dy_ref, nw_ref,
             dx_ref, dp_ref, dnw_ref):
        i = pl.program_id(0)

        @pl.when(i == 0)
        def _():
            dnw_ref[...] = jnp.zeros_like(dnw_ref)

        dp = jnp.concatenate([r[...].astype(BF16) for r in (a_ref, b_ref, c_ref, d_ref, e_ref, f_ref, g_ref)],
                             axis=1)
        dp_ref[...] = dp
        dh = _dot(dp, w_ref[...])
        xv = x_ref[...]
        rstd = lax.rsqrt(jnp.mean(xv * xv, axis=-1, keepdims=True) + EPS)
        xh = xv * rstd
        dnw_ref[...] = dnw_ref[...] + jnp.sum(dh * xh, axis=0, keepdims=True)
        g = dh * nw_ref[...]
        dx_ref[...] = rstd * (g - xh * jnp.mean(g * xh, axis=-1, keepdims=True)) + dy_ref[...]

    def blk(n):
        return pl.BlockSpec((tm, n), lambda i: (i, 0))

    return pl.pallas_call(
        body, grid=(s // tm,),
        in_specs=[blk(1536), blk(512), blk(512), blk(512), blk(512), blk(512), blk(128),
                  pl.BlockSpec((D_IN_PAD, D_MODEL), lambda i: (0, 0)), blk(D_MODEL), blk(D_MODEL),
                  pl.BlockSpec((1, D_MODEL), lambda i: (0, 0))],
        out_specs=[blk(D_MODEL), blk(D_IN_PAD), pl.BlockSpec((8, D_MODEL), lambda i: (0, 0))],
        out_shape=[_sds((s, D_MODEL)), _sds((s, D_IN_PAD), BF16), _sds((8, D_MODEL))],
        compiler_params=_cp(("arbitrary",)), name="in_bwd_dx")(d_qkv_dn, dz, dq_att, dk_att, dv_att, dgate, dba,
                                                                w_bf, x, dy, norm_w)


def _in_bwd_dw(x, norm_w, dp):
    s = x.shape[0]
    tm = 512
    n_col = 3
    tn = D_IN_PAD // n_col

    def body(x_ref, nw_ref, dp_ref, dw_ref):
        i = pl.program_id(1)

        @pl.when(i == 0)
        def _():
            dw_ref[...] = jnp.zeros_like(dw_ref)

        xv = x_ref[...]
        rstd = lax.rsqrt(jnp.mean(xv * xv, axis=-1, keepdims=True) + EPS)
        h = (xv * rstd * nw_ref[...]).astype(BF16)
        dw_ref[...] = dw_ref[...] + _dot_tn(dp_ref[...], h)

    return pl.pallas_call(
        body, grid=(n_col, s // tm),
        in_specs=[pl.BlockSpec((tm, D_MODEL), lambda j, i: (i, 0)), pl.BlockSpec((1, D_MODEL), lambda j, i: (0, 0)),
                  pl.BlockSpec((tm, tn), lambda j, i: (i, j))],
        out_specs=pl.BlockSpec((tn, D_MODEL), lambda j, i: (j, 0)),
        out_shape=_sds((D_IN_PAD, D_MODEL)),
        compiler_params=_cp(("parallel", "arbitrary")), name="in_bwd_dw")(x, norm_w, dp)


def _flat(a):
    return a.reshape(-1, a.shape[-1])


def _as_pattern(a, r):
    return a if r == 1 else a.reshape(r, a.shape[0] // r, a.shape[1])


D_SHARD = D_IN // N_CHIPS
BA_START = 4 * D_DN
BA_PACKED = 4096
S1_HEAD = BA_START - D_SHARD
S1_BA = 2 * D_SHARD - BA_START


def _pack_rows(g):
    pad = jnp.zeros((D_IN_PAD - D_IN, g.shape[2]), g.dtype)
    s2_ba = 2 * DN_HEADS - S1_BA
    return jnp.concatenate([g[0][:D_SHARD], g[1][:S1_HEAD], g[2][s2_ba:D_SHARD], g[3][:D_SHARD],
                            g[1][S1_HEAD:D_SHARD], g[2][:s2_ba], pad], axis=0)


def _pack_w(g):
    n = g.shape[2]

    def body(g_ref, o_ref):
        g32 = g_ref.bitcast(jnp.uint32)
        o32 = o_ref.bitcast(jnp.uint32)
        full, head, ba1 = D_SHARD // 2, S1_HEAD // 2, S1_BA // 2
        ba2 = DN_HEADS - ba1
        pieces = [(0, 0, full), (1, 0, head), (2, ba2, full), (3, 0, full), (1, head, full), (2, 0, ba2)]
        at = 0
        for chip, lo, hi in pieces:
            o32[at:at + hi - lo, :] = g32[chip, lo:hi, :]
            at += hi - lo
        o32[at:D_IN_PAD // 2, :] = jnp.zeros((D_IN_PAD // 2 - at, n), jnp.uint32)

    vm = pl.BlockSpec(memory_space=pltpu.VMEM)
    return pl.pallas_call(body, in_specs=[vm], out_specs=vm, out_shape=_sds((D_IN_PAD, n), g.dtype),
                          compiler_params=_cp(), name="pack_w")(g)


def _unpack_rows(p, rows):
    mid = BA_PACKED + S1_BA
    pad = jnp.zeros((rows - D_SHARD, p.shape[1]), p.dtype)
    return jnp.stack([jnp.concatenate([p[0:D_SHARD], pad], axis=0),
                      jnp.concatenate([p[D_SHARD:BA_START], p[BA_PACKED:mid], pad], axis=0),
                      jnp.concatenate([p[mid:BA_PACKED + 2 * DN_HEADS], p[BA_START:BA_START + S1_HEAD], pad], axis=0),
                      jnp.concatenate([p[BA_START + S1_HEAD:BA_PACKED], pad], axis=0)])


def _lane_row(vec, offset):
    return jnp.pad(vec.reshape(1, -1), ((0, 0), (offset, 128 - offset - vec.shape[0])))


def _local_step(x, tgt, norm_w, w_bf, conv_w, a_log, dt_bias, dn_norm_w, q_norm_w, k_norm_w, rel_bias, w_out_bf):
    s = x.shape[0]
    nc = s // CHUNK
    conv_w8 = jnp.pad(conv_w, ((0, 8 - CONV_W), (0, 0)))
    alog_l = _lane_row(a_log.reshape(-1), DN_HEADS)
    dtb_l = _lane_row(dt_bias.reshape(-1), DN_HEADS)
    dnw_t = jnp.tile(dn_norm_w.reshape(1, DN_HD), (1, DN_HEADS))
    qw_t = jnp.tile(q_norm_w.reshape(1, ATT_HD), (1, ATT_HEADS))
    kw_t = jnp.tile(k_norm_w.reshape(1, ATT_HD), (1, ATT_HEADS))
    bd128 = _block_ones(512, DN_HD)
    bd64 = _block_ones(512, ATT_HD)
    tri = _tri_incl()
    sel = _lane_select()
    buckets = _bucket_tables()

    proj = _in_proj(x, norm_w, w_bf)
    qn, kn, v_dn, bg = _dn_pre(proj, conv_w8, alog_l, dtb_l)
    bgt = bg[:, 0:8].reshape(nc, CHUNK, 8).transpose(0, 2, 1)
    u, w, qd, kt, attn, t_inv, gl = _dn_prep(qn, kn, v_dn, bg, bgt, tri, sel)
    o_dn, vn, sp = _dn_scan(u, w, qd, kt, attn, gl)
    q1, k1, v1, q4, k4, v4, q16, k16, v16 = _att_pre(proj, qw_t, kw_t, bd64)
    rs = [r for _, r in PATTERNS]
    qkv = [(q1, k1, v1, 0), (_flat(q4), _flat(k4), _flat(v4), 0), (_flat(q16), _flat(k16), _flat(v16), 0)]
    bias = _bias_fwd(rel_bias, buckets)
    o_pats, lse_pats = [], []
    for p, r in enumerate(rs):
        o_p, lse_p = _att_fwd(*qkv[p], bias, p, r, "att_fwd_r%d" % r)
        o_pats.append(_as_pattern(o_p, r))
        lse_pats.append(_as_pattern(lse_p, r))
    mixed, o_att, l1, l4, l16 = _post_fwd(o_dn, proj, o_pats, lse_pats, dnw_t, bd128)
    dy, loss_blk = _out_fwd(x, mixed, w_out_bf, tgt)

    dmixed, d_w_out = _out_bwd(dy, mixed, w_out_bf)
    do_dn, dz, dgate, do1, do4, do16, dl1, dl4, dl16, d_dnw = _post_bwd(dmixed, o_dn, proj, o_att, dnw_t, bd128, bd64)
    side = [(do1, l1, dl1), (_flat(do4), _flat(l4), _flat(dl4)), (_flat(do16), _flat(l16), _flat(dl16))]
    dq_pats, dk_pats, dv_pats, ds_accs = [], [], [], []
    for p, r in enumerate(rs):
        dq_p, dk_p, dv_p, ds_p = _att_bwd(*qkv[p], *side[p], bias, p, r, "att_bwd_r%d" % r)
        dq_pats.append(_as_pattern(dq_p, r))
        dk_pats.append(_as_pattern(dk_p, r))
        dv_pats.append(_as_pattern(dv_p, r))
        ds_accs.append(ds_p)
    d_rel_bias = _bias_bwd(ds_accs, buckets)
    dq_att, dk_att, dv_att, d_qw, d_kw = _att_pre_bwd(dq_pats, dk_pats, dv_pats, proj, qw_t, kw_t, bd64)
    du, dqd, dkt, dw, dattn, dgl = _dn_scan_bwd(do_dn, sp, qd, kt, w, vn, attn, gl)
    dqn, dkn, dv_dn, dbg = _dn_prep_bwd(qn, kn, v_dn, bg, bgt, tri, sel, t_inv, attn, u, w, du, dw, dqd, dkt, dattn,
                                        dgl)
    dyc, dba, dsm = _dn_pre_bwd(dqn, dkn, dv_dn, dbg, proj, conv_w8, alog_l, dtb_l)
    d_qkv_dn, d_conv8 = _conv_bwd(dyc, proj, conv_w8)
    grad_x, dp, d_nw8 = _in_bwd_dx(d_qkv_dn, dz, dq_att, dk_att, dv_att, dgate, dba, w_bf, x, dy, norm_w)
    d_w_in_t = _in_bwd_dw(x, norm_w, dp)

    grads = dict(
        norm_w=d_nw8[0:1, :],
        w_in_t=d_w_in_t,
        conv_w=d_conv8[0:CONV_W, :],
        a_log=dsm[0:1, DN_HEADS:2 * DN_HEADS],
        dt_bias=dsm[1:2, DN_HEADS:2 * DN_HEADS],
        dn_norm_w=d_dnw[0:1, :],
        q_norm_w=d_qw[0:1, :].reshape(ATT_HEADS, ATT_HD),
        k_norm_w=d_kw[0:1, :].reshape(ATT_HEADS, ATT_HD),
        rel_bias=d_rel_bias,
        w_out=d_w_out,
    )
    return loss_blk[0, 0], grad_x, grads


MESH_ID = pl.DeviceIdType.MESH
ANY = pl.BlockSpec(memory_space=pl.ANY)


def _position():
    return lax.axis_index("x"), lax.axis_index("y"), lax.axis_index("c")


def _other_chips(x, y):
    return [(1 - x, y), (x, 1 - y), (1 - x, 1 - y)]


SHARD_PAD = 1040
WIN = 528


def _row_split(n):
    return (n // 2) // 128 * 128


def _part(ref, cc):
    n = ref.shape[0]
    sp = _row_split(n)
    return ref.at[pl.ds(0, sp)] if cc == 0 else ref.at[pl.ds(sp, n - sp)]


def _gather_weights(wt_s, w_out_s, conv_s):
    def body(a_ref, b_ref, c_ref, ga_ref, gb_ref, gc_ref, send_sems, recv_sems, loc_sems, a_vmem, b_vmem):
        x, y, c = _position()
        me = 2 * x + y
        sib = (x, y, 1 - c)
        big = ((a_ref, ga_ref), (b_ref, gb_ref))
        stage_in = [pltpu.make_async_copy(a_ref, a_vmem, loc_sems.at[0]),
                    pltpu.make_async_copy(b_ref, b_vmem, loc_sems.at[1])]
        local = [pltpu.make_async_copy(a_vmem, ga_ref.at[me], loc_sems.at[0]),
                 pltpu.make_async_copy(b_vmem, gb_ref.at[me], loc_sems.at[1]),
                 pltpu.make_async_copy(c_ref, gc_ref.at[me], loc_sems.at[2])]
        for cp in stage_in:
            cp.start()
        local[2].start()
        for cp in stage_in:
            cp.wait()
        for cp in local[:2]:
            cp.start()
        others = _other_chips(x, y)

        def exchange(cc):
            sends = []
            for j, (px, py) in enumerate(others):
                for t, (src, dst) in enumerate(big):
                    k = 3 * j + t
                    sends.append(pltpu.make_async_remote_copy(
                        src_ref=_part(src, cc), dst_ref=_part(dst.at[me], cc), send_sem=send_sems.at[k],
                        recv_sem=recv_sems.at[k], device_id=(px, py, c), device_id_type=MESH_ID))
                sends.append(pltpu.make_async_remote_copy(
                    src_ref=c_ref, dst_ref=gc_ref.at[me], send_sem=send_sems.at[3 * j + 2],
                    recv_sem=recv_sems.at[3 * j + 2], device_id=(px, py, c), device_id_type=MESH_ID))
            for cp in sends:
                cp.start()
            for j, (px, py) in enumerate(others):
                src_chip = 2 * px + py
                for t, (src, dst) in enumerate(big):
                    landed = _part(dst.at[src_chip], cc)
                    pltpu.make_async_remote_copy(
                        src_ref=_part(src, cc), dst_ref=landed, send_sem=send_sems.at[3 * j + t],
                        recv_sem=recv_sems.at[3 * j + t], device_id=(px, py, c), device_id_type=MESH_ID).wait_recv()
                    k = 9 + 2 * j + t
                    fwd = pltpu.make_async_remote_copy(
                        src_ref=landed, dst_ref=landed, send_sem=send_sems.at[k], recv_sem=recv_sems.at[k],
                        device_id=sib, device_id_type=MESH_ID)
                    fwd.start()
                    sends.append(fwd)
                pltpu.make_async_remote_copy(
                    src_ref=c_ref, dst_ref=gc_ref.at[src_chip], send_sem=send_sems.at[3 * j + 2],
                    recv_sem=recv_sems.at[3 * j + 2], device_id=(px, py, c), device_id_type=MESH_ID).wait_recv()
            for j, (px, py) in enumerate(others):
                src_chip = 2 * px + py
                for t, (src, dst) in enumerate(big):
                    k = 9 + 2 * j + t
                    theirs = _part(dst.at[src_chip], 1 - cc)
                    pltpu.make_async_remote_copy(
                        src_ref=theirs, dst_ref=theirs, send_sem=send_sems.at[k], recv_sem=recv_sems.at[k],
                        device_id=sib, device_id_type=MESH_ID).wait_recv()
            for cp in sends:
                cp.wait_send()

        for cc in (0, 1):
            pl.when(c == cc)(functools.partial(exchange, cc))
        for cp in local:
            cp.wait()

    srcs = (wt_s, w_out_s, conv_s)
    n_sem = 9 + 6
    return pl.pallas_call(
        body, in_specs=[ANY] * 3, out_specs=[ANY] * 3,
        out_shape=[_sds((N_CHIPS,) + a.shape, a.dtype) for a in srcs],
        scratch_shapes=[pltpu.SemaphoreType.DMA((n_sem,)), pltpu.SemaphoreType.DMA((n_sem,)),
                        pltpu.SemaphoreType.DMA((3,)), pltpu.VMEM(wt_s.shape, wt_s.dtype),
                        pltpu.VMEM(w_out_s.shape, w_out_s.dtype)],
        compiler_params=_cp(), name="gather_weights")(*srcs)


def _exchange_grads(gw_in4, gw_out4, small):
    n_big = 2
    n_dev = 8

    def body(a_ref, b_ref, s_ref, ra_ref, rb_ref, rs_ref, send_sems, recv_sems, loc_sem):
        x, y, c = _position()
        dev = 4 * x + 2 * y + c
        pairs = ((a_ref, ra_ref), (b_ref, rb_ref))
        local = pltpu.make_async_copy(s_ref, rs_ref.at[dev], loc_sem)
        local.start()
        sends = []
        for j, (px, py) in enumerate(_other_chips(x, y)):
            for t, (src, dst) in enumerate(pairs):
                k = n_big * j + t
                sends.append(pltpu.make_async_remote_copy(
                    src_ref=src.at[2 * px + py], dst_ref=dst.at[j], send_sem=send_sems.at[k],
                    recv_sem=recv_sems.at[k], device_id=(px, py, c), device_id_type=MESH_ID))
        flips = [(dx, dy, dc) for dx in (0, 1) for dy in (0, 1) for dc in (0, 1)][1:]
        for f, (dx, dy, dc) in enumerate(flips):
            k = 3 * n_big + f
            peer = (x ^ dx, y ^ dy, c ^ dc)
            sends.append(pltpu.make_async_remote_copy(
                src_ref=s_ref, dst_ref=rs_ref.at[dev], send_sem=send_sems.at[k], recv_sem=recv_sems.at[k],
                device_id=peer, device_id_type=MESH_ID))
        for cp in sends:
            cp.start()
        for j, (px, py) in enumerate(_other_chips(x, y)):
            for t, (src, dst) in enumerate(pairs):
                k = n_big * j + t
                pltpu.make_async_remote_copy(
                    src_ref=src.at[0], dst_ref=dst.at[j], send_sem=send_sems.at[k], recv_sem=recv_sems.at[k],
                    device_id=(px, py, c), device_id_type=MESH_ID).wait_recv()
        for f, (dx, dy, dc) in enumerate(flips):
            k = 3 * n_big + f
            peer = (x ^ dx, y ^ dy, c ^ dc)
            pltpu.make_async_remote_copy(
                src_ref=s_ref, dst_ref=rs_ref.at[4 * peer[0] + 2 * peer[1] + peer[2]], send_sem=send_sems.at[k],
                recv_sem=recv_sems.at[k], device_id=peer, device_id_type=MESH_ID).wait_recv()
        for cp in sends:
            cp.wait_send()
        local.wait()

    n_sem = 3 * n_big + n_dev - 1
    return pl.pallas_call(
        body, in_specs=[ANY] * 3, out_specs=[ANY] * 3,
        out_shape=[_sds((3,) + gw_in4.shape[1:], gw_in4.dtype), _sds((3,) + gw_out4.shape[1:], gw_out4.dtype),
                   _sds((n_dev,) + small.shape, small.dtype)],
        scratch_shapes=[pltpu.SemaphoreType.DMA((n_sem,)), pltpu.SemaphoreType.DMA((n_sem,)),
                        pltpu.SemaphoreType.DMA],
        name="exchange_grads")(gw_in4, gw_out4, small)


def _plane_sum(own, recv, name):
    rows, cols = own.shape
    tr = 128

    def body(o_ref, r_ref, out_ref):
        acc = o_ref[...]
        for j in range(3):
            acc = acc + r_ref[j].astype(F32)
        out_ref[...] = acc

    return pl.pallas_call(
        body, grid=(pl.cdiv(rows, tr),),
        in_specs=[pl.BlockSpec((tr, cols), lambda i: (i, 0)), pl.BlockSpec((3, tr, cols), lambda i: (0, i, 0))],
        out_specs=pl.BlockSpec((tr, cols), lambda i: (i, 0)), out_shape=_sds((rows, cols)),
        compiler_params=_cp(("parallel",)), name=name)(own, recv)


def _sum_pair(a, b, name):
    n, rows, cols = a.shape
    tr = 128

    def body(a_ref, b_ref, s_ref, sb_ref):
        tot = a_ref[...] + b_ref[...]
        s_ref[...] = tot
        sb_ref[...] = tot.astype(BF16)

    blk = pl.BlockSpec((1, tr, cols), lambda k, i: (k, i, 0))
    return pl.pallas_call(
        body, grid=(n, pl.cdiv(rows, tr)), in_specs=[blk, blk], out_specs=[blk, blk],
        out_shape=[_sds(a.shape), _sds(a.shape, BF16)],
        compiler_params=_cp(("parallel", "parallel")), name=name)(a, b)


def _fill_parts(f_in, f_out, rows_in, rows_out):
    def body(a_ref, b_ref, fa_ref, fb_ref, send_sems, recv_sems, loc_sems, a_vmem, b_vmem):
        x, y, c = _position()
        sib = (x, y, 1 - c)
        pairs = ((a_ref, fa_ref), (b_ref, fb_ref))
        stage_in = [pltpu.make_async_copy(a_ref, a_vmem, loc_sems.at[0]),
                    pltpu.make_async_copy(b_ref, b_vmem, loc_sems.at[1])]
        for cp in stage_in:
            cp.start()
        for cp in stage_in:
            cp.wait()

        def fill(cc):
            mine = [_part(dst, cc) for _, dst in pairs]
            srcs = [src.at[pl.ds(0, m.shape[0])] for src, m in zip((a_vmem, b_vmem), mine)]
            local = [pltpu.make_async_copy(s, m, loc_sems.at[t]) for t, (s, m) in enumerate(zip(srcs, mine))]
            sends = [pltpu.make_async_remote_copy(src_ref=s, dst_ref=m, send_sem=send_sems.at[t],
                                                  recv_sem=recv_sems.at[t], device_id=sib, device_id_type=MESH_ID)
                     for t, (s, m) in enumerate(zip(srcs, mine))]
            for cp in local + sends:
                cp.start()
            for t, (_, dst) in enumerate(pairs):
                theirs = _part(dst, 1 - cc)
                pltpu.make_async_remote_copy(src_ref=theirs, dst_ref=theirs, send_sem=send_sems.at[t],
                                             recv_sem=recv_sems.at[t], device_id=sib, device_id_type=MESH_ID).wait_recv()
            for cp in sends:
                cp.wait_send()
            for cp in local:
                cp.wait()

        for cc in (0, 1):
            pl.when(c == cc)(functools.partial(fill, cc))

    return pl.pallas_call(
        body, in_specs=[ANY] * 2, out_specs=[ANY] * 2,
        out_shape=[_sds((rows_in, D_MODEL)), _sds((rows_out, D_MODEL))],
        scratch_shapes=[pltpu.SemaphoreType.DMA((2,)), pltpu.SemaphoreType.DMA((2,)), pltpu.SemaphoreType.DMA((2,)),
                        pltpu.VMEM(f_in.shape, F32), pltpu.VMEM(f_out.shape, F32)],
        compiler_params=_cp(), name="fill_parts")(f_in, f_out)


def _swap_windows(u_in, u_out, win_in, win_out):
    def body(a_ref, b_ref, ra_ref, rb_ref, send_sems, recv_sems):
        x, y, c = _position()
        sib = (x, y, 1 - c)
        cps = []
        for t, (src, dst, win) in enumerate(((a_ref, ra_ref, win_in), (b_ref, rb_ref, win_out))):
            split = _row_split(src.shape[1])
            start = pl.multiple_of((1 - c) * split, split)
            cps.append(pltpu.make_async_remote_copy(
                src_ref=src.at[:, pl.ds(start, win), :], dst_ref=dst, send_sem=send_sems.at[t],
                recv_sem=recv_sems.at[t], device_id=sib, device_id_type=MESH_ID))
        for cp in cps:
            cp.start()
        for cp in cps:
            cp.wait_recv()
        for cp in cps:
            cp.wait_send()

    return pl.pallas_call(
        body, in_specs=[ANY] * 2, out_specs=[ANY] * 2,
        out_shape=[_sds((N_CHIPS, win_in, D_MODEL)), _sds((N_CHIPS, win_out, D_MODEL))],
        scratch_shapes=[pltpu.SemaphoreType.DMA((2,)), pltpu.SemaphoreType.DMA((2,))],
        name="swap_windows")(u_in, u_out)


SMALL_LAYOUT = (("norm_w", 1024), ("conv_w", 6144), ("a_log", 128), ("dt_bias", 128), ("dn_norm_w", 128),
                ("q_norm_w", 512), ("k_norm_w", 512), ("rel_bias", 256))
SMALL_TOTAL = sum(n for _, n in SMALL_LAYOUT)


def _small_offset(name):
    off = 0
    for n, size in SMALL_LAYOUT:
        if n == name:
            return off
        off += size
    raise KeyError(name)


def _pack_small(grads):
    parts = []
    for name, size in SMALL_LAYOUT:
        flat = grads[name].reshape(1, -1)
        parts.append(jnp.pad(flat, ((0, 0), (0, size - flat.shape[1]))))
    return jnp.concatenate(parts, axis=1)


def _sum_small(rows):
    n_dev = rows.shape[0]
    q_off = _small_offset("q_norm_w")
    k_off = _small_offset("k_norm_w")

    def body(r_ref, tot_ref, qk_ref):
        tot = r_ref[0:1, :]
        for d in range(1, n_dev):
            tot = tot + r_ref[d:d + 1, :]
        tot_ref[...] = tot
        for row, off in ((0, q_off), (1, k_off)):
            s4 = tot[:, off:off + 128] + tot[:, off + 128:off + 256] + tot[:, off + 256:off + 384] \
                + tot[:, off + 384:off + 512]
            qk_ref[row:row + 1, :] = s4 + pltpu.roll(s4, ATT_HD, 1)

    return pl.pallas_call(
        body, in_specs=[pl.BlockSpec(memory_space=pltpu.VMEM)],
        out_specs=[pl.BlockSpec(memory_space=pltpu.VMEM)] * 2,
        out_shape=[_sds((1, SMALL_TOTAL)), _sds((2, 128))],
        compiler_params=_cp(), name="sum_small")(rows)


def _adamw_math(w, g, m, v):
    m = ADAM_B1 * m + (1.0 - ADAM_B1) * g
    v = ADAM_B2 * v + (1.0 - ADAM_B2) * (g * g)
    m_hat = m / (1.0 - ADAM_B1 ** ADAM_STEP)
    v_hat = v / (1.0 - ADAM_B2 ** ADAM_STEP)
    delta = -ADAM_LR * (m_hat / (jnp.sqrt(v_hat) + ADAM_EPS) + ADAM_WD * w)
    return delta, m, v


def _adamw_big(g, w, m, v, name):
    rows, cols = w.shape
    tr = 128

    def body(g_ref, w_ref, m_ref, v_ref, go_ref, d_ref, nm_ref, nv_ref):
        g = g_ref[...]
        go_ref[...] = g
        d_ref[...], nm_ref[...], nv_ref[...] = _adamw_math(w_ref[...], g, m_ref[...], v_ref[...])

    blk = pl.BlockSpec((tr, cols), lambda i: (i, 0))
    return pl.pallas_call(
        body, grid=(pl.cdiv(rows, tr),), in_specs=[blk] * 4, out_specs=[blk] * 4,
        out_shape=[_sds((rows, cols))] * 4, compiler_params=_cp(("parallel",)), name=name)(g, w, m, v)


def _adamw_rows(g, w, m, v, name):
    rows, cols = w.shape
    tr = 128

    def body(g_ref, w_ref, m_ref, v_ref, go_ref, d_ref, nm_ref, nv_ref, s_g, s_d, s_m, s_v):
        g = g_ref[...]
        s_g[...] = g
        s_d[...], s_m[...], s_v[...] = _adamw_math(w_ref[...], g, m_ref[...], v_ref[...])
        for i in range(tr):
            for scr, out in ((s_g, go_ref), (s_d, d_ref), (s_m, nm_ref), (s_v, nv_ref)):
                out[i] = scr[i:i + 1, :]

    blk = pl.BlockSpec((tr, cols), lambda i: (i, 0))
    oblk = pl.BlockSpec((tr, 1, cols), lambda i: (i, 0, 0))
    return pl.pallas_call(
        body, grid=(pl.cdiv(rows, tr),), in_specs=[blk] * 4, out_specs=[oblk] * 4,
        out_shape=[_sds((rows, 1, cols))] * 4, scratch_shapes=[pltpu.VMEM((tr, cols), F32)] * 4,
        compiler_params=_cp(("parallel",)), name=name)(g, w, m, v)


def _adamw_small(w, g, m, v, name):
    def body(w_ref, g_ref, m_ref, v_ref, d_ref, nm_ref, nv_ref):
        d_ref[...], nm_ref[...], nv_ref[...] = _adamw_math(w_ref[...], g_ref[...], m_ref[...], v_ref[...])

    vm = pl.BlockSpec(memory_space=pltpu.VMEM)
    return pl.pallas_call(body, in_specs=[vm] * 4, out_specs=[vm] * 3, out_shape=[_sds(w.shape)] * 3,
                          compiler_params=_cp(), name=name)(w, g, m, v)


WEIGHTS = ("norm_w", "w_in", "conv_w", "a_log", "dt_bias", "dn_norm_w", "q_norm_w", "k_norm_w", "rel_bias", "w_out")


def kernel(x, norm_w, w_in, conv_w, a_log, dt_bias, dn_norm_w, q_norm_w, k_norm_w, rel_bias, w_out, loss_target, m_norm_w, m_w_in, m_conv_w, m_a_log, m_dt_bias, m_dn_norm_w, m_q_norm_w, m_k_norm_w, m_rel_bias, m_w_out, v_norm_w, v_w_in, v_conv_w, v_a_log, v_dt_bias, v_dn_norm_w, v_q_norm_w, v_k_norm_w, v_rel_bias, v_w_out):
    xi, yi, ci = _position()
    chip = 2 * xi + yi
    w_loc = dict(norm_w=norm_w, w_in=w_in[0].T, conv_w=conv_w[0], a_log=a_log, dt_bias=dt_bias, dn_norm_w=dn_norm_w,
                 q_norm_w=q_norm_w, k_norm_w=k_norm_w, rel_bias=rel_bias, w_out=w_out[0])
    m_loc = dict(norm_w=m_norm_w, w_in=m_w_in[0].T, conv_w=m_conv_w[0], a_log=m_a_log, dt_bias=m_dt_bias,
                 dn_norm_w=m_dn_norm_w, q_norm_w=m_q_norm_w, k_norm_w=m_k_norm_w, rel_bias=m_rel_bias,
                 w_out=m_w_out[0])
    v_loc = dict(norm_w=v_norm_w, w_in=v_w_in[0].T, conv_w=v_conv_w[0], a_log=v_a_log, dt_bias=v_dt_bias,
                 dn_norm_w=v_dn_norm_w, q_norm_w=v_q_norm_w, k_norm_w=v_k_norm_w, rel_bias=v_rel_bias,
                 w_out=v_w_out[0])

    wt_pad = jnp.pad(w_loc["w_in"].astype(BF16), ((0, SHARD_PAD - D_SHARD), (0, 0)))
    g_in, g_out, g_conv = _gather_weights(wt_pad, w_loc["w_out"].astype(BF16), w_loc["conv_w"])
    wt_full = _pack_w(g_in)
    w_out_full = g_out.reshape(D_MODEL, D_MODEL)
    conv_full = g_conv.transpose(1, 0, 2).reshape(CONV_W, 3 * D_DN)

    loss_local, grad_x, grads = _local_step(x[0], loss_target[0], norm_w, wt_full, conv_full, a_log, dt_bias,
                                            dn_norm_w, q_norm_w, k_norm_w, rel_bias, w_out_full)
    loss = lax.psum(loss_local, ("x", "y", "c"))

    u_in = _unpack_rows(grads["w_in_t"], SHARD_PAD)
    u_out = grads["w_out"].reshape(N_CHIPS, D_MODEL // N_CHIPS, D_MODEL)
    win_out = u_out.shape[1] // 2
    sib_in, sib_out = _swap_windows(u_in, u_out, WIN, win_out)
    in_mine = lax.dynamic_slice_in_dim(u_in, ci * _row_split(SHARD_PAD), WIN, axis=1)
    out_mine = lax.dynamic_slice_in_dim(u_out, ci * win_out, win_out, axis=1)
    h_in, hb_in = _sum_pair(in_mine, sib_in, "chip_sum_w_in")
    h_out, hb_out = _sum_pair(out_mine, sib_out, "chip_sum_w_out")
    small = _pack_small(grads)
    r_in, r_out, r_small = _exchange_grads(hb_in, hb_out, small)
    f_in = _plane_sum(lax.dynamic_index_in_dim(h_in, chip, 0, keepdims=False), r_in, "shard_sum_w_in")
    f_out = _plane_sum(lax.dynamic_index_in_dim(h_out, chip, 0, keepdims=False), r_out, "shard_sum_w_out")
    full_in, full_out = _fill_parts(f_in, f_out, SHARD_PAD, u_out.shape[1])
    tot_small, qk = _sum_small(r_small.reshape(8, SMALL_TOTAL))

    def small_grad(name, n):
        off = _small_offset(name)
        return tot_small[:, off:off + n]

    conv_all = small_grad("conv_w", CONV_W * 3 * D_DN).reshape(CONV_W, 3 * D_DN)
    g_small = dict(
        norm_w=small_grad("norm_w", D_MODEL),
        conv_w=lax.dynamic_slice_in_dim(conv_all, chip * (3 * D_DN // N_CHIPS), 3 * D_DN // N_CHIPS, axis=1),
        a_log=small_grad("a_log", DN_HEADS),
        dt_bias=small_grad("dt_bias", DN_HEADS),
        dn_norm_w=small_grad("dn_norm_w", DN_HD),
        q_norm_w=qk[0:1, 0:ATT_HD],
        k_norm_w=qk[1:2, 0:ATT_HD],
        rel_bias=small_grad("rel_bias", ATT_HEADS * N_BUCKETS).reshape(ATT_HEADS, N_BUCKETS),
    )

    out_g, out_d, out_m, out_v = {}, {}, {}, {}
    out_g["w_in"], out_d["w_in"], out_m["w_in"], out_v["w_in"] = _adamw_rows(
        full_in, w_loc["w_in"], m_loc["w_in"], v_loc["w_in"], "adamw_w_in")
    out_g["w_out"], out_d["w_out"], out_m["w_out"], out_v["w_out"] = _adamw_big(
        full_out, w_loc["w_out"], m_loc["w_out"], v_loc["w_out"], "adamw_w_out")
    for name in g_small:
        out_g[name] = g_small[name]
        out_d[name], out_m[name], out_v[name] = _adamw_small(w_loc[name], g_small[name], m_loc[name], v_loc[name],
                                                             "adamw_" + name)
    for d in (out_g, out_d, out_m, out_v):
        d["w_in"] = d["w_in"].transpose(1, 2, 0)
        for name in ("conv_w", "w_out"):
            d[name] = d[name][None]
    return (loss, grad_x[None], *[out_g[n] for n in WEIGHTS], *[out_d[n] for n in WEIGHTS],
            *[out_m[n] for n in WEIGHTS], *[out_v[n] for n in WEIGHTS])
```

```python
import functools
import math

import numpy as np
import jax
import jax.numpy as jnp
from jax import lax
from jax.experimental import pallas as pl
from jax.experimental.pallas import tpu as pltpu

F32 = jnp.float32
BF16 = jnp.bfloat16
HI = lax.Precision.HIGHEST

D_MODEL = 1024
D_DN = 512
DN_HEADS = 4
DN_HD = 128
CONV_W = 4
CHUNK = 64
D_ATT = 512
ATT_HEADS = 8
ATT_HD = 64
PATTERNS = ((128, 1), (512, 4), (2048, 16))
N_BUCKETS = 32
MAX_DISTANCE = 2048
D_IN = 4 * D_DN + 2 * DN_HEADS + 4 * D_ATT
D_IN_PAD = 4224
EPS = 1e-6
BLK = 128
NEG = -1e30
N_CHIPS = 4

ADAM_LR = 0.001
ADAM_B1 = 0.9
ADAM_B2 = 0.999
ADAM_EPS = 1e-08
ADAM_WD = 0.01
ADAM_STEP = 10

VMEM_LIMIT = 56 * 1024 * 1024

COL_Z = 3
COL_ATT_Q = 4
COL_ATT_K = 5
COL_ATT_V = 6
COL_GATE = 7
COL_BA_128 = 32


def _cp(sem=None):
    if sem is None:
        return pltpu.CompilerParams(vmem_limit_bytes=VMEM_LIMIT)
    return pltpu.CompilerParams(dimension_semantics=sem, vmem_limit_bytes=VMEM_LIMIT)


def _sds(shape, dtype=F32):
    return jax.ShapeDtypeStruct(shape, dtype)


def _mm(a, b):
    return jnp.dot(a.astype(BF16), b.astype(BF16), preferred_element_type=F32)


def _mm_nt(a, b):
    return lax.dot_general(a.astype(BF16), b.astype(BF16), (((1,), (1,)), ((), ())),
                           preferred_element_type=F32)


def _mm_tn(a, b):
    return lax.dot_general(a.astype(BF16), b.astype(BF16), (((0,), (0,)), ((), ())),
                           preferred_element_type=F32)


def _mmx(a, b):
    return jnp.dot(a, b, precision=HI, preferred_element_type=F32)


def _mmx_nt(a, b):
    return lax.dot_general(a, b, (((1,), (1,)), ((), ())), precision=HI, preferred_element_type=F32)


def _mmx_tn(a, b):
    return lax.dot_general(a, b, (((0,), (0,)), ((), ())), precision=HI, preferred_element_type=F32)


def _dot(a, b):
    return jnp.dot(a, b, preferred_element_type=F32)


def _dot_nt(a, b):
    return lax.dot_general(a, b, (((1,), (1,)), ((), ())), preferred_element_type=F32)


def _dot_tn(a, b):
    return lax.dot_general(a, b, (((0,), (0,)), ((), ())), preferred_element_type=F32)


def _split(a):
    hi = a.astype(BF16)
    return hi, (a - hi.astype(F32)).astype(BF16)


def _mm3(a_s, b_s):
    return _dot(a_s[0], b_s[0]) + _dot(a_s[0], b_s[1]) + _dot(a_s[1], b_s[0])


def _mm3_tn(a_s, b_s):
    return _dot_tn(a_s[0], b_s[0]) + _dot_tn(a_s[0], b_s[1]) + _dot_tn(a_s[1], b_s[0])


def _interleave(gens):
    live = list(gens)
    while live:
        nxt = []
        for g in live:
            try:
                next(g)
                nxt.append(g)
            except StopIteration:
                pass
        live = nxt


def _segsum(x, bd):
    hi = x.astype(BF16)
    r1 = x - hi.astype(F32)
    mid = r1.astype(BF16)
    lo = (r1 - mid.astype(F32)).astype(BF16)
    return (jnp.dot(hi, bd, preferred_element_type=F32) + jnp.dot(mid, bd, preferred_element_type=F32)
            + jnp.dot(lo, bd, preferred_element_type=F32))


def _sigmoid(x):
    return 1.0 / (1.0 + jnp.exp(-x))


def _silu_grad(x, s):
    return s * (1.0 + x * (1.0 - s))


def _block_ones(n, seg):
    i = np.arange(n)
    return jnp.asarray((i[:, None] // seg == i[None, :] // seg).astype(np.float32), dtype=BF16)


def _tri_incl():
    i = np.arange(CHUNK)
    return jnp.asarray((i[:, None] >= i[None, :]).astype(np.float32))


def _t5_bucket(dist):
    max_exact = N_BUCKETS // 2
    d = np.maximum(dist, 1).astype(np.float64)
    large = max_exact + (np.log(d / max_exact) / math.log(MAX_DISTANCE / max_exact)
                         * (N_BUCKETS - max_exact)).astype(np.int32)
    large = np.minimum(large, N_BUCKETS - 1)
    return np.where(dist < max_exact, dist, large).astype(np.int32)


def _bucket_tables():
    qi = np.arange(BLK)[:, None]
    kj = np.arange(2 * BLK)[None, :]
    step = qi - kj + BLK
    return jnp.asarray(np.stack([_t5_bucket(np.clip(step, 0, None) * r) for _, r in PATTERNS]))


def _in_proj(x, norm_w, wt_bf):
    s = x.shape[0]
    tm = 512

    def body(x_ref, nw_ref, w_ref, o_ref):
        xv = x_ref[...]
        rstd = lax.rsqrt(jnp.mean(xv * xv, axis=-1, keepdims=True) + EPS)
        h = (xv * rstd * nw_ref[...]).astype(BF16)
        o_ref[...] = _dot_nt(h, w_ref[...])

    return pl.pallas_call(
        body, grid=(s // tm,),
        in_specs=[pl.BlockSpec((tm, D_MODEL), lambda i: (i, 0)),
                  pl.BlockSpec((1, D_MODEL), lambda i: (0, 0)),
                  pl.BlockSpec((D_IN_PAD, D_MODEL), lambda i: (0, 0))],
        out_specs=pl.BlockSpec((tm, D_IN_PAD), lambda i: (i, 0)),
        out_shape=_sds((s, D_IN_PAD)), compiler_params=_cp(("parallel",)), name="in_proj")(x, norm_w, wt_bf)


def _conv_group(cur, halo, w_ref, c):
    rows = cur.shape[0]
    lanes = slice(128 * c, 128 * c + 128)
    xcat = jnp.concatenate([halo, cur], axis=0)
    y = cur * w_ref[CONV_W - 1:CONV_W, lanes]
    for k in range(1, CONV_W):
        sh = pltpu.roll(xcat, k, 0)[8:8 + rows]
        y = y + sh * w_ref[CONV_W - 1 - k:CONV_W - k, lanes]
    return y


def _beta_g(ba, alog_l, dtb_l):
    lane = lax.broadcasted_iota(jnp.int32, ba.shape, 1)
    sig_b = _sigmoid(ba)
    t = ba + dtb_l
    softplus = jnp.maximum(t, 0.0) + jnp.log(1.0 + jnp.exp(-jnp.abs(t)))
    nega = -jnp.exp(alog_l)
    g = nega * softplus
    out = jnp.where(lane < DN_HEADS, sig_b, jnp.where(lane < 2 * DN_HEADS, g, 0.0))
    return out, lane, sig_b, t, nega, g


def _dn_pre(proj, conv_w8, alog_l, dtb_l):
    s = proj.shape[0]
    tr = 512
    nh = tr // 8

    def body(u_ref, halo_ref, ba_ref, w_ref, al_ref, dt_ref, q_ref, k_ref, v_ref, bg_ref):
        i = pl.program_id(0)
        keep = (i > 0).astype(F32)
        for c in range(12):
            lanes = slice(128 * c, 128 * c + 128)
            y = _conv_group(u_ref[:, lanes], halo_ref[:, lanes] * keep, w_ref, c)
            sv = y * _sigmoid(y)
            if c < 8:
                rs = lax.rsqrt(jnp.sum(sv * sv, axis=1, keepdims=True) + EPS)
                n = sv * rs
                if c < 4:
                    q_ref[:, lanes] = n * (DN_HD ** -0.5)
                else:
                    k_ref[:, slice(128 * (c - 4), 128 * (c - 3))] = n
            else:
                v_ref[:, slice(128 * (c - 8), 128 * (c - 7))] = sv
        bg_ref[...] = _beta_g(ba_ref[...], al_ref[...], dt_ref[...])[0]

    return pl.pallas_call(
        body, grid=(s // tr,),
        in_specs=[pl.BlockSpec((tr, 1536), lambda i: (i, 0)),
                  pl.BlockSpec((8, 1536), lambda i: (jnp.maximum(i * nh - 1, 0), 0)),
                  pl.BlockSpec((tr, 128), lambda i: (i, COL_BA_128)),
                  pl.BlockSpec((8, 1536), lambda i: (0, 0)),
                  pl.BlockSpec((1, 128), lambda i: (0, 0)),
                  pl.BlockSpec((1, 128), lambda i: (0, 0))],
        out_specs=[pl.BlockSpec((tr, 512), lambda i: (i, 0))] * 3 + [pl.BlockSpec((tr, 128), lambda i: (i, 0))],
        out_shape=[_sds((s, 512))] * 3 + [_sds((s, 128))],
        compiler_params=_cp(("parallel",)), name="dn_pre")(proj, proj, proj, conv_w8, alog_l, dtb_l)


CPS = 4
CPS_SCAN = 8


def _split3(a):
    hi = a.astype(BF16)
    r1 = a - hi.astype(F32)
    mid = r1.astype(BF16)
    return hi, mid, (r1 - mid.astype(F32)).astype(BF16)


def _lane_select():
    r = np.arange(128)
    return jnp.asarray((r[None, :, None] == np.arange(8)[:, None, None]) * np.ones((1, 1, 128)), dtype=BF16)


def _lane_bcast(a3, sel):
    return _dot(a3[0], sel) + _dot(a3[1], sel) + _dot(a3[2], sel)


def _rowsum_b(z, ones_b):
    hi, lo = _split(z)
    return _dot(hi, ones_b) + _dot(lo, ones_b)


def _chunk_cumsum(bg, bgt, tri):
    return _split3(bg), _split3(_mmx(tri, bg)), _mmx_nt(bgt, tri)


def _chunk_common(bg3, gc3, gc_row, h, sel_ref):
    gcc = _lane_bcast(gc3, sel_ref[DN_HEADS + h])
    beta = _lane_bcast(bg3, sel_ref[h])
    gcr = gc_row[DN_HEADS + h:DN_HEADS + h + 1, :]
    ii = lax.broadcasted_iota(jnp.int32, (CHUNK, CHUNK), 0)
    jj = lax.broadcasted_iota(jnp.int32, (CHUNK, CHUNK), 1)
    incl = ii >= jj
    strict = ii > jj
    decay = jnp.exp(jnp.where(incl, gcc[:, 0:CHUNK] - gcr, NEG))
    gl = gcc[CHUNK - 1:CHUNK, :]
    return gcc, beta, incl, strict, decay, gl


def _dn_prep(qn, kn, v, bg, bgt, tri, sel):
    s = qn.shape[0]
    nc = s // CHUNK

    def body(q_ref, k_ref, v_ref, bg_ref, bgt_ref, tri_ref, sel_ref,
             u_ref, w_ref, qd_ref, kt_ref, attn_ref, t_ref, gl_ref):
        tri_v = tri_ref[...]
        ii = lax.broadcasted_iota(jnp.int32, (CHUNK, CHUNK), 0)
        jj = lax.broadcasted_iota(jnp.int32, (CHUNK, CHUNK), 1)
        eye = (ii == jj).astype(F32)

        def chain(cc, h, bg3, gc3, gc_row):
            rows = slice(CHUNK * cc, CHUNK * cc + CHUNK)
            lanes = slice(128 * h, 128 * h + 128)
            gcc, beta, incl, strict, decay, gl = _chunk_common(bg3, gc3, gc_row, h, sel_ref)
            yield
            q = q_ref[rows, lanes]
            k = k_ref[rows, lanes]
            vv = v_ref[rows, lanes]
            kb = k * beta
            egc = jnp.exp(gcc)
            a_mat = jnp.where(strict, _mm_nt(kb, k) * decay, 0.0)
            attn_ref[cc, h] = jnp.where(incl, _mm_nt(q, k) * decay, 0.0)
            qd_ref[rows, lanes] = q * egc
            kt_ref[rows, lanes] = k * jnp.exp(gl - gcc)
            gl_ref[cc, h] = jnp.exp(gl)
            yield
            p = -a_mat
            t = eye + p
            for _ in range(5):
                ps = _split(p)
                p = _mm3(ps, ps)
                yield
                t = t + _mm3(_split(t), _split(p))
                yield
            t_ref[cc, h] = t
            ts = _split(t)
            u_ref[rows, lanes] = _mm3(ts, _split(vv * beta))
            w_ref[rows, lanes] = _mm3(ts, _split(kb * egc))

        gens = []
        for cc in range(CPS):
            bgv = bg_ref[CHUNK * cc:CHUNK * cc + CHUNK, :]
            bg3, gc3, gc_row = _chunk_cumsum(bgv, bgt_ref[cc], tri_v)
            gens += [chain(cc, h, bg3, gc3, gc_row) for h in range(DN_HEADS)]
        _interleave(gens)

    rows_step = CPS * CHUNK
    big = pl.BlockSpec((rows_step, 512), lambda n: (n, 0))
    sq = pl.BlockSpec((CPS, DN_HEADS, CHUNK, CHUNK), lambda n: (n, 0, 0, 0))
    return pl.pallas_call(
        body, grid=(nc // CPS,),
        in_specs=[big, big, big, pl.BlockSpec((rows_step, 128), lambda n: (n, 0)),
                  pl.BlockSpec((CPS, 8, CHUNK), lambda n: (n, 0, 0)),
                  pl.BlockSpec((CHUNK, CHUNK), lambda n: (0, 0)),
                  pl.BlockSpec((8, 128, 128), lambda n: (0, 0, 0))],
        out_specs=[big, big, big, big, sq, sq, pl.BlockSpec((CPS, DN_HEADS, 1, 128), lambda n: (n, 0, 0, 0))],
        out_shape=[_sds((s, 512))] * 4 + [_sds((nc, DN_HEADS, CHUNK, CHUNK))] * 2 + [_sds((nc, DN_HEADS, 1, 128))],
        compiler_params=_cp(("parallel",)), name="dn_prep")(qn, kn, v, bg, bgt, tri, sel)


def _dn_scan(u, w, qd, kt, attn, gl):
    s = u.shape[0]
    nc = s // CHUNK

    def body(u_ref, w_ref, qd_ref, kt_ref, attn_ref, gl_ref, o_ref, vn_ref, sp_ref, st_ref):
        n = pl.program_id(0)

        @pl.when(n == 0)
        def _():
            st_ref[...] = jnp.zeros_like(st_ref)

        def chain(cc, h):
            rows = slice(CHUNK * cc, CHUNK * cc + CHUNK)
            lanes = slice(128 * h, 128 * h + 128)
            st = st_ref[h]
            sp_ref[cc, h] = st
            stb = st.astype(BF16)
            ws = _dot(w_ref[rows, lanes].astype(BF16), stb)
            qs = _dot(qd_ref[rows, lanes].astype(BF16), stb)
            yield
            vn = u_ref[rows, lanes] - ws
            vn_ref[rows, lanes] = vn
            vnb = vn.astype(BF16)
            o_ref[rows, lanes] = qs + _dot(attn_ref[cc, h].astype(BF16), vnb)
            st_ref[h] = st * gl_ref[cc, h] + _dot_tn(kt_ref[rows, lanes].astype(BF16), vnb)

        for cc in range(CPS_SCAN):
            _interleave([chain(cc, h) for h in range(DN_HEADS)])

    big = pl.BlockSpec((CPS_SCAN * CHUNK, 512), lambda n: (n, 0))
    return pl.pallas_call(
        body, grid=(nc // CPS_SCAN,),
        in_specs=[big, big, big, big,
                  pl.BlockSpec((CPS_SCAN, DN_HEADS, CHUNK, CHUNK), lambda n: (n, 0, 0, 0)),
                  pl.BlockSpec((CPS_SCAN, DN_HEADS, 1, 128), lambda n: (n, 0, 0, 0))],
        out_specs=[big, big, pl.BlockSpec((CPS_SCAN, DN_HEADS, DN_HD, DN_HD), lambda n: (n, 0, 0, 0))],
        out_shape=[_sds((s, 512)), _sds((s, 512)), _sds((nc, DN_HEADS, DN_HD, DN_HD))],
        scratch_shapes=[pltpu.VMEM((DN_HEADS, DN_HD, DN_HD), F32)],
        compiler_params=_cp(("arbitrary",)), name="dn_scan")(u, w, qd, kt, attn, gl)


R4 = PATTERNS[1][1]
R16 = PATTERNS[2][1]
TM = 512


def _pattern_spec(r, width=512):
    return pl.BlockSpec((r, TM // r, width), lambda i: (0, i, 0))


def _pattern_shape(s, r, dtype=F32, width=512):
    return _sds((r, s // r, width), dtype)


SLABS = pltpu.VMEM((4, TM, 128), F32)

HEAD_SLOT = 128 // ATT_HEADS


def _head_expand():
    src = np.arange(128)[:, None]
    dst = np.arange(512)[None, :]
    return jnp.asarray((src == (dst // ATT_HD) * HEAD_SLOT).astype(np.float32), dtype=BF16)


def _head_compact():
    src = np.arange(512)[:, None]
    dst = np.arange(128)[None, :]
    return jnp.asarray((src // ATT_HD == dst // HEAD_SLOT).astype(np.float32), dtype=BF16)


def _to_patterns(val, dsts, scr):
    for c in range(val.shape[1] // 128):
        lanes = slice(128 * c, 128 * c + 128)
        scr[c] = val[:, lanes]
        for dst_ref, r in dsts:
            for a in range(r):
                dst_ref[a, :, lanes] = scr[c, pl.ds(a, TM // r, stride=r), :].astype(dst_ref.dtype)


def _from_pattern(src_ref, r, scr):
    n_slab = src_ref.shape[2] // 128
    for c in range(n_slab):
        for a in range(r):
            scr[c, pl.ds(a, TM // r, stride=r), :] = src_ref[a, :, 128 * c:128 * c + 128]
    return jnp.concatenate([scr[c] for c in range(n_slab)], axis=1) if n_slab > 1 else scr[0]


def _att_pre(proj, qw_t, kw_t, bd64):
    s = proj.shape[0]

    def body(q_ref, k_ref, v_ref, qw_ref, kw_ref, bd_ref,
             q1_ref, k1_ref, v1_ref, q4_ref, k4_ref, v4_ref, q16_ref, k16_ref, v16_ref, scr):
        bd = bd_ref[...]
        q = q_ref[...]
        k = k_ref[...]
        qn = q * lax.rsqrt(_segsum(q * q, bd) * (1.0 / ATT_HD) + EPS) * qw_ref[...] * (ATT_HD ** -0.5)
        kn = k * lax.rsqrt(_segsum(k * k, bd) * (1.0 / ATT_HD) + EPS) * kw_ref[...]
        q1_ref[...] = qn.astype(BF16)
        k1_ref[...] = kn.astype(BF16)
        v1_ref[...] = v_ref[...].astype(BF16)
        _to_patterns(qn, ((q4_ref, R4), (q16_ref, R16)), scr)
        _to_patterns(kn, ((k4_ref, R4), (k16_ref, R16)), scr)
        _to_patterns(v_ref[...], ((v4_ref, R4), (v16_ref, R16)), scr)

    row = pl.BlockSpec((1, 512), lambda i: (0, 0))
    tok = pl.BlockSpec((TM, 512), lambda i: (i, 0))
    return pl.pallas_call(
        body, grid=(s // TM,),
        in_specs=[pl.BlockSpec((TM, 512), lambda i: (i, COL_ATT_Q)),
                  pl.BlockSpec((TM, 512), lambda i: (i, COL_ATT_K)),
                  pl.BlockSpec((TM, 512), lambda i: (i, COL_ATT_V)),
                  row, row, pl.BlockSpec((512, 512), lambda i: (0, 0))],
        out_specs=[tok] * 3 + [_pattern_spec(R4)] * 3 + [_pattern_spec(R16)] * 3,
        out_shape=[_sds((s, 512), BF16)] * 3 + [_pattern_shape(s, R4, BF16)] * 3 + [_pattern_shape(s, R16, BF16)] * 3,
        scratch_shapes=[SLABS],
        compiler_params=_cp(("parallel",)), name="att_pre")(proj, proj, proj, qw_t, kw_t, bd64)


def _bias_fwd(rel_bias, buckets):
    def body(rb_ref, bk_ref, o_ref):
        for p in range(len(PATTERNS)):
            bk = bk_ref[p]
            for h in range(ATT_HEADS):
                acc = jnp.zeros((BLK, 2 * BLK), F32)
                for b in range(N_BUCKETS):
                    acc = jnp.where(bk == b, rb_ref[h, b], acc)
                o_ref[p, h] = acc

    return pl.pallas_call(
        body,
        in_specs=[pl.BlockSpec(memory_space=pltpu.SMEM), pl.BlockSpec(memory_space=pltpu.VMEM)],
        out_specs=pl.BlockSpec(memory_space=pltpu.VMEM),
        out_shape=_sds((len(PATTERNS), ATT_HEADS, BLK, 2 * BLK)),
        compiler_params=_cp(), name="bias_fwd")(rel_bias, buckets)


def _bias_bwd(ds_accs, buckets):
    def body(ds0_ref, ds1_ref, ds2_ref, bk_ref, o_ref):
        for h in range(ATT_HEADS):
            for b in range(N_BUCKETS):
                tot = jnp.zeros((), F32)
                for p, ds_ref in enumerate((ds0_ref, ds1_ref, ds2_ref)):
                    tot = tot + jnp.sum(jnp.where(bk_ref[p] == b, ds_ref[h], 0.0))
                o_ref[h, b] = tot

    return pl.pallas_call(
        body,
        in_specs=[pl.BlockSpec(memory_space=pltpu.VMEM)] * 4,
        out_specs=pl.BlockSpec(memory_space=pltpu.SMEM),
        out_shape=_sds((ATT_HEADS, N_BUCKETS)),
        compiler_params=_cp(), name="bias_bwd")(*ds_accs, buckets)


QB_FWD = 2
QB_BWD = 4


def _att_masks(has_prev):
    qi = lax.broadcasted_iota(jnp.int32, (BLK, BLK), 0)
    kj = lax.broadcasted_iota(jnp.int32, (BLK, BLK), 1)
    lane = lax.broadcasted_iota(jnp.int32, (BLK, 2 * ATT_HD), 1)
    return jnp.logical_and(kj >= qi, has_prev), kj <= qi, lane < ATT_HD


def _head_lanes(h):
    half = h % 2
    return slice(ATT_HD * h, ATT_HD * h + ATT_HD), slice(ATT_HD * half, ATT_HD * half + ATT_HD)


def _att_scores(qm, kp2, kc2, bias_h, mask_prev, mask_cur):
    s_prev = jnp.where(mask_prev, _dot_nt(qm, kp2) + bias_h[:, :BLK], NEG)
    s_cur = jnp.where(mask_cur, _dot_nt(qm, kc2) + bias_h[:, BLK:], NEG)
    return s_prev, s_cur


def _att_fwd(q, k, v, v_col, bias, p_idx, r, name):
    QB = QB_FWD
    s = q.shape[0]
    nblk = s // BLK
    nseq = nblk // r

    def body(q_ref, kp_ref, kc_ref, vp_ref, vc_ref, b_ref, o_ref, lse_ref):
        j = pl.program_id(0)

        def head(h, rows, masks, q2, kp2, kc2, vp2, vc2):
            mask_prev, mask_cur, lo_half = masks
            out_l, pair_l = _head_lanes(h)
            sel = lo_half if h % 2 == 0 else jnp.logical_not(lo_half)
            qm = jnp.where(sel, q2, jnp.zeros_like(q2))
            s_prev, s_cur = _att_scores(qm, kp2, kc2, b_ref[0, h], mask_prev, mask_cur)
            yield
            m = jnp.maximum(jnp.max(s_prev, axis=1, keepdims=True), jnp.max(s_cur, axis=1, keepdims=True))
            p_prev = jnp.exp(s_prev - m)
            p_cur = jnp.exp(s_cur - m)
            l = jnp.sum(p_prev, axis=1, keepdims=True) + jnp.sum(p_cur, axis=1, keepdims=True)
            yield
            o2 = _dot(p_prev.astype(BF16), vp2) + _dot(p_cur.astype(BF16), vc2)
            o_ref[rows, out_l] = (o2 * (1.0 / l))[:, pair_l]
            lse_ref[rows, HEAD_SLOT * h:HEAD_SLOT * h + HEAD_SLOT] = jnp.broadcast_to(m + jnp.log(l), (BLK, HEAD_SLOT))

        for sub in range(QB):
            rows = slice(BLK * sub, BLK * sub + BLK)
            before = slice(BLK * (sub - 1), BLK * sub)
            masks = _att_masks(((QB * j + sub) % nseq) != 0)
            gens = []
            for pp in range(ATT_HEADS // 2):
                lanes = slice(128 * pp, 128 * pp + 128)
                kp = kp_ref[:, lanes] if sub == 0 else kc_ref[before, lanes]
                vp = vp_ref[:, lanes] if sub == 0 else vc_ref[before, lanes]
                slabs = (q_ref[rows, lanes], kp, kc_ref[rows, lanes], vp, vc_ref[rows, lanes])
                gens += [head(2 * pp, rows, masks, *slabs), head(2 * pp + 1, rows, masks, *slabs)]
            _interleave(gens)

    cur = pl.BlockSpec((QB * BLK, 512), lambda j: (j, 0))
    prev = pl.BlockSpec((BLK, 512), lambda j: (jnp.maximum(QB * j - 1, 0), 0))
    vcur = pl.BlockSpec((QB * BLK, 512), lambda j: (j, v_col))
    vprev = pl.BlockSpec((BLK, 512), lambda j: (jnp.maximum(QB * j - 1, 0), v_col))
    return pl.pallas_call(
        body, grid=(nblk // QB,),
        in_specs=[cur, prev, cur, vprev, vcur,
                  pl.BlockSpec((1, ATT_HEADS, BLK, 2 * BLK), lambda j: (p_idx, 0, 0, 0))],
        out_specs=[cur, pl.BlockSpec((QB * BLK, 128), lambda j: (j, 0))],
        out_shape=[_sds((s, 512)), _sds((s, 128))],
        compiler_params=_cp(("parallel",)), name=name)(q, k, k, v, v, bias)


def _post_fwd(o_dn, proj, o_pats, lse_pats, dnw_t, bd128):
    s = o_dn.shape[0]

    def body(o_ref, z_ref, gate_ref, o1_ref, o4_ref, o16_ref, s1_ref, s4_ref, s16_ref, wn_ref, bd_ref, ex_ref,
             mixed_ref, oatt_ref, l1_ref, l4_ref, l16_ref, scr_a, scr_b, scr_c, scr_d):
        o = o_ref[...]
        z = z_ref[...]
        rstd = lax.rsqrt(_segsum(o * o, bd_ref[...]) * (1.0 / DN_HD) + EPS)
        y_dn = o * rstd * wn_ref[...] * (z * _sigmoid(z))
        mixed_ref[:, 0:512] = y_dn.astype(BF16)
        lses = (s1_ref[...], _from_pattern(s4_ref, R4, scr_a), _from_pattern(s16_ref, R16, scr_b))
        m = jnp.maximum(jnp.maximum(lses[0], lses[1]), lses[2])
        tot = jnp.exp(lses[0] - m) + jnp.exp(lses[1] - m) + jnp.exp(lses[2] - m)
        big_l = m + jnp.log(tot)
        l1_ref[...] = big_l
        _to_patterns(big_l, ((l4_ref, R4), (l16_ref, R16)), scr_a)
        ex = ex_ref[...]
        outs = (o1_ref[...], _from_pattern(o4_ref, R4, scr_c), _from_pattern(o16_ref, R16, scr_d))
        acc = jnp.zeros_like(o)
        for lse_p, o_p in zip(lses, outs):
            acc = acc + _lane_bcast(_split3(jnp.exp(lse_p - big_l)), ex) * o_p
        gate = gate_ref[...]
        oatt_ref[...] = acc
        mixed_ref[:, 512:1024] = (acc * (gate * _sigmoid(gate))).astype(BF16)

    blk = pl.BlockSpec((TM, 512), lambda i: (i, 0))
    cblk = pl.BlockSpec((TM, 128), lambda i: (i, 0))
    p4, p16 = _pattern_spec(R4), _pattern_spec(R16)
    c4, c16 = _pattern_spec(R4, 128), _pattern_spec(R16, 128)
    return pl.pallas_call(
        body, grid=(s // TM,),
        in_specs=[blk, pl.BlockSpec((TM, 512), lambda i: (i, COL_Z)),
                  pl.BlockSpec((TM, 512), lambda i: (i, COL_GATE)), blk, p4, p16, cblk, c4, c16,
                  pl.BlockSpec((1, 512), lambda i: (0, 0)), pl.BlockSpec((512, 512), lambda i: (0, 0)),
                  pl.BlockSpec((128, 512), lambda i: (0, 0))],
        out_specs=[pl.BlockSpec((TM, D_MODEL), lambda i: (i, 0)), blk, cblk, c4, c16],
        out_shape=[_sds((s, D_MODEL), BF16), _sds((s, 512)), _sds((s, 128)), _pattern_shape(s, R4, F32, 128),
                   _pattern_shape(s, R16, F32, 128)],
        scratch_shapes=[SLABS] * 4,
        compiler_params=_cp(("parallel",)), name="post_fwd")(o_dn, proj, proj, *o_pats, *lse_pats, dnw_t, bd128,
                                                              _head_expand())


def _out_fwd(x, mixed, w_out_bf, tgt):
    s = x.shape[0]
    tm = 512

    def body(x_ref, m_ref, w_ref, t_ref, dy_ref, loss_ref):
        i = pl.program_id(0)

        @pl.when(i == 0)
        def _():
            loss_ref[...] = jnp.zeros_like(loss_ref)

        y = x_ref[...] + jnp.dot(m_ref[...], w_ref[...], preferred_element_type=F32)
        err = y - t_ref[...]
        dy_ref[...] = err * (1.0 / D_MODEL)
        part = 0.5 * jnp.sum(jnp.mean(err * err, axis=-1, keepdims=True), axis=0, keepdims=True)
        loss_ref[...] = loss_ref[...] + part

    blk = pl.BlockSpec((tm, D_MODEL), lambda i: (i, 0))
    return pl.pallas_call(
        body, grid=(s // tm,),
        in_specs=[blk, blk, pl.BlockSpec((D_MODEL, D_MODEL), lambda i: (0, 0)), blk],
        out_specs=[blk, pl.BlockSpec((8, 128), lambda i: (0, 0))],
        out_shape=[_sds((s, D_MODEL)), _sds((8, 128))],
        compiler_params=_cp(("arbitrary",)), name="out_fwd")(x, mixed, w_out_bf, tgt)


def _out_bwd(dy, mixed, w_out_bf):
    s = dy.shape[0]
    tm = 512

    def body(dy_ref, m_ref, w_ref, dm_ref, dw_ref):
        i = pl.program_id(0)

        @pl.when(i == 0)
        def _():
            dw_ref[...] = jnp.zeros_like(dw_ref)

        dyb = dy_ref[...].astype(BF16)
        dm_ref[...] = lax.dot_general(dyb, w_ref[...], (((1,), (1,)), ((), ())), preferred_element_type=F32)
        dw_ref[...] = dw_ref[...] + lax.dot_general(m_ref[...], dyb, (((0,), (0,)), ((), ())),
                                                    preferred_element_type=F32)

    blk = pl.BlockSpec((tm, D_MODEL), lambda i: (i, 0))
    full = pl.BlockSpec((D_MODEL, D_MODEL), lambda i: (0, 0))
    return pl.pallas_call(
        body, grid=(s // tm,), in_specs=[blk, blk, full], out_specs=[blk, full],
        out_shape=[_sds((s, D_MODEL)), _sds((D_MODEL, D_MODEL))],
        compiler_params=_cp(("arbitrary",)), name="out_bwd")(dy, mixed, w_out_bf)


def _post_bwd(dmixed, o_dn, proj, o_att, dnw_t, bd128):
    s = o_dn.shape[0]
    tm = TM

    def body(ddn_ref, datt_ref, o_ref, z_ref, gate_ref, oatt_ref, wn_ref, bd128_ref, hc_ref,
             do_ref, dz_ref, dgate_ref, doatt_ref, do4_ref, do16_ref, delta_ref, dl4_ref, dl16_ref, dnw_ref, scr):
        i = pl.program_id(0)

        @pl.when(i == 0)
        def _():
            dnw_ref[...] = jnp.zeros_like(dnw_ref)

        bd128v = bd128_ref[...]
        o = o_ref[...]
        z = z_ref[...]
        wn = wn_ref[...]
        dy = ddn_ref[...]
        rstd = lax.rsqrt(_segsum(o * o, bd128v) * (1.0 / DN_HD) + EPS)
        nrm = o * rstd
        sz = _sigmoid(z)
        dz_ref[...] = dy * nrm * wn * _silu_grad(z, sz)
        dn = dy * z * sz
        gw = dn * wn
        do_ref[...] = rstd * (gw - nrm * (_segsum(gw * nrm, bd128v) * (1.0 / DN_HD)))
        colsum = jnp.sum(dn * nrm, axis=0, keepdims=True)
        fold = colsum[:, 0:128] + colsum[:, 128:256] + colsum[:, 256:384] + colsum[:, 384:512]
        dnw_ref[...] = dnw_ref[...] + fold
        dya = datt_ref[...]
        gate = gate_ref[...]
        oatt = oatt_ref[...]
        sg = _sigmoid(gate)
        dgate_ref[...] = dya * oatt * _silu_grad(gate, sg)
        doa = dya * gate * sg
        doatt_ref[...] = doa.astype(BF16)
        delta = _segsum(doa * oatt, hc_ref[...])
        delta_ref[...] = delta
        _to_patterns(doa, ((do4_ref, R4), (do16_ref, R16)), scr)
        _to_patterns(delta, ((dl4_ref, R4), (dl16_ref, R16)), scr)

    blk = pl.BlockSpec((tm, 512), lambda i: (i, 0))
    cblk = pl.BlockSpec((tm, 128), lambda i: (i, 0))
    p4, p16 = _pattern_spec(R4), _pattern_spec(R16)
    c4, c16 = _pattern_spec(R4, 128), _pattern_spec(R16, 128)
    return pl.pallas_call(
        body, grid=(s // tm,),
        in_specs=[blk, pl.BlockSpec((tm, 512), lambda i: (i, 1)), blk,
                  pl.BlockSpec((tm, 512), lambda i: (i, COL_Z)), pl.BlockSpec((tm, 512), lambda i: (i, COL_GATE)),
                  blk, pl.BlockSpec((1, 512), lambda i: (0, 0)), pl.BlockSpec((512, 512), lambda i: (0, 0)),
                  pl.BlockSpec((512, 128), lambda i: (0, 0))],
        out_specs=[blk, blk, blk, blk, p4, p16, cblk, c4, c16, pl.BlockSpec((8, 128), lambda i: (0, 0))],
        out_shape=[_sds((s, 512))] * 3 + [_sds((s, 512), BF16), _pattern_shape(s, R4, BF16),
                                          _pattern_shape(s, R16, BF16), _sds((s, 128)),
                                          _pattern_shape(s, R4, F32, 128), _pattern_shape(s, R16, F32, 128),
                                          _sds((8, 128))],
        scratch_shapes=[SLABS],
        compiler_params=_cp(("arbitrary",)), name="post_bwd")(dmixed, dmixed, o_dn, proj, proj, o_att, dnw_t,
                                                               bd128, _head_compact())


def _att_bwd(q, k, v, v_col, do, big_l, delta, bias, p_idx, r, name):
    QB = QB_BWD
    s = q.shape[0]
    nblk = s // BLK
    nseq = nblk // r
    nstep = nblk // QB

    def body(q_ref, kp_ref, kc_ref, vp_ref, vc_ref, do_ref, l_ref, dl_ref, b_ref,
             dq_ref, dk_ref, dv_ref, ds_ref, dkc_ref, dvc_ref):
        j = pl.program_id(0)

        @pl.when(j == 0)
        def _():
            dkc_ref[...] = jnp.zeros_like(dkc_ref)
            dvc_ref[...] = jnp.zeros_like(dvc_ref)
            ds_ref[...] = jnp.zeros_like(ds_ref)

        @pl.when(j < nstep)
        def _():
            def head(h, sub, masks, q2, do2, kp2, kc2, vp2, vc2):
                mask_prev, mask_cur, lo_half = masks
                rows = slice(BLK * sub, BLK * sub + BLK)
                out_l, pair_l = _head_lanes(h)
                sel = lo_half if h % 2 == 0 else jnp.logical_not(lo_half)
                qm = jnp.where(sel, q2, jnp.zeros_like(q2))
                dom = jnp.where(sel, do2, jnp.zeros_like(do2))
                s_prev, s_cur = _att_scores(qm, kp2, kc2, b_ref[0, h], mask_prev, mask_cur)
                dp_prev = _dot_nt(dom, vp2)
                dp_cur = _dot_nt(dom, vc2)
                yield
                lh = l_ref[rows, HEAD_SLOT * h:HEAD_SLOT * h + 1]
                dh = dl_ref[rows, HEAD_SLOT * h:HEAD_SLOT * h + 1]
                p_prev = jnp.exp(s_prev - lh)
                p_cur = jnp.exp(s_cur - lh)
                ds_prev = p_prev * (dp_prev - dh)
                ds_cur = p_cur * (dp_cur - dh)
                ds_ref[h, :, 0:BLK] = ds_ref[h, :, 0:BLK] + ds_prev
                ds_ref[h, :, BLK:2 * BLK] = ds_ref[h, :, BLK:2 * BLK] + ds_cur
                dsb_prev, dsb_cur = ds_prev.astype(BF16), ds_cur.astype(BF16)
                pb_prev, pb_cur = p_prev.astype(BF16), p_cur.astype(BF16)
                yield
                dq_ref[rows, out_l] = (_dot(dsb_prev, kp2) + _dot(dsb_cur, kc2))[:, pair_l]
                dk_prev = _dot_tn(dsb_prev, q2)[:, pair_l]
                dv_prev = _dot_tn(pb_prev, do2)[:, pair_l]
                if sub == 0:
                    last = slice(BLK * (QB - 1), BLK * QB)
                    dk_ref[last, out_l] = dkc_ref[last, out_l] + dk_prev
                    dv_ref[last, out_l] = dvc_ref[last, out_l] + dv_prev
                else:
                    before = slice(BLK * (sub - 1), BLK * sub)
                    dkc_ref[before, out_l] = dkc_ref[before, out_l] + dk_prev
                    dvc_ref[before, out_l] = dvc_ref[before, out_l] + dv_prev
                yield
                dkc_ref[rows, out_l] = _dot_tn(dsb_cur, q2)[:, pair_l]
                dvc_ref[rows, out_l] = _dot_tn(pb_cur, do2)[:, pair_l]

            done = slice(0, BLK * (QB - 1))
            dk_ref[done, :] = dkc_ref[done, :]
            dv_ref[done, :] = dvc_ref[done, :]
            for sub in range(QB):
                rows = slice(BLK * sub, BLK * sub + BLK)
                before = slice(BLK * (sub - 1), BLK * sub)
                masks = _att_masks(((QB * j + sub) % nseq) != 0)
                gens = []
                for pp in range(ATT_HEADS // 2):
                    lanes = slice(128 * pp, 128 * pp + 128)
                    kp = kp_ref[:, lanes] if sub == 0 else kc_ref[before, lanes]
                    vp = vp_ref[:, lanes] if sub == 0 else vc_ref[before, lanes]
                    slabs = (q_ref[rows, lanes], do_ref[rows, lanes], kp, kc_ref[rows, lanes], vp,
                             vc_ref[rows, lanes])
                    gens += [head(2 * pp, sub, masks, *slabs), head(2 * pp + 1, sub, masks, *slabs)]
                _interleave(gens)

        @pl.when(j == nstep)
        def _():
            dk_ref[...] = dkc_ref[...]
            dv_ref[...] = dvc_ref[...]

    last_step = nstep - 1
    cur = pl.BlockSpec((QB * BLK, 512), lambda j: (jnp.minimum(j, last_step), 0))
    compact = pl.BlockSpec((QB * BLK, 128), lambda j: (jnp.minimum(j, last_step), 0))
    lag = pl.BlockSpec((QB * BLK, 512), lambda j: (jnp.clip(j - 1, 0, last_step), 0))
    prev = pl.BlockSpec((BLK, 512), lambda j: (jnp.clip(QB * j - 1, 0, nblk - 1), 0))
    vcur = pl.BlockSpec((QB * BLK, 512), lambda j: (jnp.minimum(j, last_step), v_col))
    vprev = pl.BlockSpec((BLK, 512), lambda j: (jnp.clip(QB * j - 1, 0, nblk - 1), v_col))
    return pl.pallas_call(
        body, grid=(nstep + 1,),
        in_specs=[cur, prev, cur, vprev, vcur, cur, compact, compact,
                  pl.BlockSpec((1, ATT_HEADS, BLK, 2 * BLK), lambda j: (p_idx, 0, 0, 0))],
        out_specs=[cur, lag, lag, pl.BlockSpec((ATT_HEADS, BLK, 2 * BLK), lambda j: (0, 0, 0))],
        out_shape=[_sds((s, 512))] * 3 + [_sds((ATT_HEADS, BLK, 2 * BLK))],
        scratch_shapes=[pltpu.VMEM((QB * BLK, 512), F32), pltpu.VMEM((QB * BLK, 512), F32)],
        compiler_params=_cp(("arbitrary",)), name=name)(q, k, k, v, v, do, big_l, delta, bias)


def _att_pre_bwd(dq_pats, dk_pats, dv_pats, proj, qw_t, kw_t, bd64):
    s = proj.shape[0]
    tm = TM

    def body(dq1_ref, dq4_ref, dq16_ref, dk1_ref, dk4_ref, dk16_ref, dv1_ref, dv4_ref, dv16_ref,
             q_ref, k_ref, qw_ref, kw_ref, bd_ref,
             dqr_ref, dkr_ref, dvr_ref, dqw_ref, dkw_ref, scr4, scr16):
        i = pl.program_id(0)

        @pl.when(i == 0)
        def _():
            dqw_ref[...] = jnp.zeros_like(dqw_ref)
            dkw_ref[...] = jnp.zeros_like(dkw_ref)

        bd = bd_ref[...]

        def total(d1_ref, d4_ref, d16_ref):
            return d1_ref[...] + _from_pattern(d4_ref, R4, scr4) + _from_pattern(d16_ref, R16, scr16)

        def one(d_refs, x_ref, w_ref, scale, dx_ref, dw_ref):
            dy = total(*d_refs) * scale
            x = x_ref[...]
            rstd = lax.rsqrt(_segsum(x * x, bd) * (1.0 / ATT_HD) + EPS)
            nrm = x * rstd
            dw_ref[...] = dw_ref[...] + jnp.sum(dy * nrm, axis=0, keepdims=True)
            g = dy * w_ref[...]
            dx_ref[...] = rstd * (g - nrm * (_segsum(g * nrm, bd) * (1.0 / ATT_HD)))

        one((dq1_ref, dq4_ref, dq16_ref), q_ref, qw_ref, ATT_HD ** -0.5, dqr_ref, dqw_ref)
        one((dk1_ref, dk4_ref, dk16_ref), k_ref, kw_ref, 1.0, dkr_ref, dkw_ref)
        dvr_ref[...] = total(dv1_ref, dv4_ref, dv16_ref)

    blk = pl.BlockSpec((tm, 512), lambda i: (i, 0))
    pats = [blk, _pattern_spec(R4), _pattern_spec(R16)]
    row = pl.BlockSpec((1, 512), lambda i: (0, 0))
    acc = pl.BlockSpec((8, 512), lambda i: (0, 0))
    return pl.pallas_call(
        body, grid=(s // tm,),
        in_specs=pats * 3 + [pl.BlockSpec((tm, 512), lambda i: (i, COL_ATT_Q)),
                             pl.BlockSpec((tm, 512), lambda i: (i, COL_ATT_K)), row, row,
                             pl.BlockSpec((512, 512), lambda i: (0, 0))],
        out_specs=[blk, blk, blk, acc, acc],
        out_shape=[_sds((s, 512))] * 3 + [_sds((8, 512))] * 2,
        scratch_shapes=[SLABS] * 2,
        compiler_params=_cp(("arbitrary",)), name="att_pre_bwd")(*dq_pats, *dk_pats, *dv_pats, proj, proj,
                                                                  qw_t, kw_t, bd64)


def _dn_scan_bwd(do, sp, qd, kt, w, vn, attn, gl):
    s = do.shape[0]
    nc = s // CHUNK

    def body(do_ref, sp_ref, qd_ref, kt_ref, w_ref, vn_ref, attn_ref, gl_ref,
             du_ref, dqd_ref, dkt_ref, dw_ref, dattn_ref, dgl_ref, ds_ref):
        n = pl.program_id(0)

        @pl.when(n == 0)
        def _():
            ds_ref[...] = jnp.zeros_like(ds_ref)

        def chain(cc, h):
            rows = slice(CHUNK * cc, CHUNK * cc + CHUNK)
            lanes = slice(128 * h, 128 * h + 128)
            dsn = ds_ref[h]
            st = sp_ref[cc, h]
            dsb, stb = dsn.astype(BF16), st.astype(BF16)
            dob = do_ref[rows, lanes].astype(BF16)
            vnb = vn_ref[rows, lanes].astype(BF16)
            dvn = _dot_tn(attn_ref[cc, h].astype(BF16), dob) + _dot(kt_ref[rows, lanes].astype(BF16), dsb)
            du_ref[rows, lanes] = dvn
            dqd_ref[rows, lanes] = _dot_nt(dob, stb)
            dattn_ref[cc, h] = _dot_nt(dob, vnb)
            dkt_ref[rows, lanes] = _dot_nt(vnb, dsb)
            tot = jnp.sum(jnp.sum(st * dsn, axis=1, keepdims=True), axis=0, keepdims=True)
            dgl_ref[cc, h] = jnp.broadcast_to(tot, (1, 128))
            qdo = _dot_tn(qd_ref[rows, lanes].astype(BF16), dob)
            yield
            dvb = dvn.astype(BF16)
            dw_ref[rows, lanes] = -_dot_nt(dvb, stb)
            ds_ref[h] = qdo + dsn * gl_ref[cc, h] - _dot_tn(w_ref[rows, lanes].astype(BF16), dvb)

        for cc in reversed(range(CPS_SCAN)):
            _interleave([chain(cc, h) for h in range(DN_HEADS)])

    nsteps = nc // CPS_SCAN
    big = pl.BlockSpec((CPS_SCAN * CHUNK, 512), lambda n: (nsteps - 1 - n, 0))
    sq = pl.BlockSpec((CPS_SCAN, DN_HEADS, CHUNK, CHUNK), lambda n: (nsteps - 1 - n, 0, 0, 0))
    glb = pl.BlockSpec((CPS_SCAN, DN_HEADS, 1, 128), lambda n: (nsteps - 1 - n, 0, 0, 0))
    return pl.pallas_call(
        body, grid=(nsteps,),
        in_specs=[big, pl.BlockSpec((CPS_SCAN, DN_HEADS, DN_HD, DN_HD), lambda n: (nsteps - 1 - n, 0, 0, 0)),
                  big, big, big, big, sq, glb],
        out_specs=[big, big, big, big, sq, glb],
        out_shape=[_sds((s, 512))] * 4 + [_sds((nc, DN_HEADS, CHUNK, CHUNK)), _sds((nc, DN_HEADS, 1, 128))],
        scratch_shapes=[pltpu.VMEM((DN_HEADS, DN_HD, DN_HD), F32)],
        compiler_params=_cp(("arbitrary",)), name="dn_scan_bwd")(do, sp, qd, kt, w, vn, attn, gl)


def _dn_prep_bwd(qn, kn, v, bg, bgt, tri, sel, t_inv, attn, u, w, du, dw, dqd, dkt, dattn, dgl):
    s = qn.shape[0]
    nc = s // CHUNK

    def body(q_ref, k_ref, v_ref, bg_ref, bgt_ref, tri_ref, sel_ref, t_ref, attn_ref, u_ref, w_ref,
             du_ref, dw_ref, dqd_ref, dkt_ref, dattn_ref, dgl_ref,
             dq_ref, dk_ref, dv_ref, dbg_ref):
        tri_v = tri_ref[...]
        lane = lax.broadcasted_iota(jnp.int32, (CHUNK, 128), 1)
        rowi = lax.broadcasted_iota(jnp.int32, (CHUNK, 128), 0)
        ones_b = jnp.ones((CHUNK, 128), BF16)
        ones_sq = jnp.ones((128, 128), BF16)
        parts = [[] for _ in range(CPS)]

        def chain(cc, h, bg3, gc3, gc_row):
            rows = slice(CHUNK * cc, CHUNK * cc + CHUNK)
            lanes = slice(128 * h, 128 * h + 128)
            gcc, beta, incl, strict, decay, gl = _chunk_common(bg3, gc3, gc_row, h, sel_ref)
            yield
            q = q_ref[rows, lanes]
            k = k_ref[rows, lanes]
            vv = v_ref[rows, lanes]
            ts = _split(t_ref[cc, h])
            egc = jnp.exp(gcc)
            kb = k * beta
            a_mat = jnp.where(strict, _mm_nt(kb, k) * decay, 0.0)
            dvb = _mm3_tn(ts, _split(du_ref[rows, lanes]))
            dkbg = _mm3_tn(ts, _split(dw_ref[rows, lanes]))
            yield
            d_a = jnp.where(strict, -(_mm_nt(dvb, u_ref[rows, lanes]) + _mm_nt(dkbg, w_ref[rows, lanes])), 0.0)
            d_m = d_a * decay
            dattn_m = jnp.where(incl, dattn_ref[cc, h], 0.0)
            dqk = dattn_m * decay
            e_hi, e_lo = _split(d_a * a_mat + dattn_m * attn_ref[cc, h])
            yield
            dkb = _mm(d_m, k)
            dk = _mm_tn(d_m, kb) + _mm_tn(dqk, q)
            dq = _mm(dqk, k)
            e_colsum = _dot_tn(e_hi, ones_b) + _dot_tn(e_lo, ones_b)
            e_rowsum = _dot(e_hi, ones_b) + _dot(e_lo, ones_b)
            dqd = dqd_ref[rows, lanes]
            dkt = dkt_ref[rows, lanes]
            s_dqd = _rowsum_b(dqd * q, ones_sq)
            s_dkt = _rowsum_b(dkt * k, ones_sq)
            rk = _rowsum_b(dkbg * k, ones_sq)
            s_dkb = _rowsum_b(dkb * k, ones_sq)
            s_dvb = _rowsum_b(dvb * vv, ones_sq)
            yield
            tail = jnp.exp(gl - gcc)
            r = s_dkt * tail
            dgl_tot = jnp.sum(r, axis=0, keepdims=True) + dgl_ref[cc, h] * jnp.exp(gl)
            dgc = e_rowsum - e_colsum + s_dqd * egc - r + rk * beta * egc
            dgc = dgc + jnp.where(rowi == CHUNK - 1, dgl_tot, 0.0)
            dq_ref[rows, lanes] = dq + dqd * egc
            dk_ref[rows, lanes] = dk + dkt * tail + dkbg * (beta * egc) + dkb * beta
            dv_ref[rows, lanes] = dvb * beta
            parts[cc].append((h, dgc, rk * egc + s_dkb + s_dvb))

        gens = []
        for cc in range(CPS):
            bgv = bg_ref[CHUNK * cc:CHUNK * cc + CHUNK, :]
            bg3, gc3, gc_row = _chunk_cumsum(bgv, bgt_ref[cc], tri_v)
            gens += [chain(cc, h, bg3, gc3, gc_row) for h in range(DN_HEADS)]
        _interleave(gens)
        for cc in range(CPS):
            dgc_mat = jnp.zeros((CHUNK, 128), F32)
            dbeta_mat = jnp.zeros((CHUNK, 128), F32)
            for h, dgc, dbeta in parts[cc]:
                dgc_mat = dgc_mat + jnp.where(lane == DN_HEADS + h, dgc, 0.0)
                dbeta_mat = dbeta_mat + jnp.where(lane == h, dbeta, 0.0)
            dbg_ref[CHUNK * cc:CHUNK * cc + CHUNK, :] = _mmx_tn(tri_v, dgc_mat) + dbeta_mat

    big = pl.BlockSpec((CPS * CHUNK, 512), lambda n: (n, 0))
    sq = pl.BlockSpec((CPS, DN_HEADS, CHUNK, CHUNK), lambda n: (n, 0, 0, 0))
    glb = pl.BlockSpec((CPS, DN_HEADS, 1, 128), lambda n: (n, 0, 0, 0))
    small = pl.BlockSpec((CPS * CHUNK, 128), lambda n: (n, 0))
    return pl.pallas_call(
        body, grid=(nc // CPS,),
        in_specs=[big, big, big, small, pl.BlockSpec((CPS, 8, CHUNK), lambda n: (n, 0, 0)),
                  pl.BlockSpec((CHUNK, CHUNK), lambda n: (0, 0)),
                  pl.BlockSpec((8, 128, 128), lambda n: (0, 0, 0)), sq, sq, big, big,
                  big, big, big, big, sq, glb],
        out_specs=[big, big, big, small],
        out_shape=[_sds((s, 512))] * 3 + [_sds((s, 128))],
        compiler_params=_cp(("parallel",)), name="dn_prep_bwd")(qn, kn, v, bg, bgt, tri, sel, t_inv, attn, u, w,
                                                                  du, dw, dqd, dkt, dattn, dgl)


def _dn_pre_bwd(dqn, dkn, dv, dbg, proj, conv_w8, alog_l, dtb_l):
    s = proj.shape[0]
    tr = 512
    nh = tr // 8

    def body(dq_ref, dk_ref, dv_ref, dbg_ref, u_ref, halo_ref, ba_ref, w_ref, al_ref, dt_ref,
             dy_ref, dba_ref, dsm_ref):
        i = pl.program_id(0)

        @pl.when(i == 0)
        def _():
            dsm_ref[...] = jnp.zeros_like(dsm_ref)

        keep = (i > 0).astype(F32)
        for c in range(12):
            lanes = slice(128 * c, 128 * c + 128)
            y = _conv_group(u_ref[:, lanes], halo_ref[:, lanes] * keep, w_ref, c)
            sg = _sigmoid(y)
            sv = y * sg
            if c < 8:
                rs = lax.rsqrt(jnp.sum(sv * sv, axis=1, keepdims=True) + EPS)
                n = sv * rs
                if c < 4:
                    dn = dq_ref[:, lanes] * (DN_HD ** -0.5)
                else:
                    dn = dk_ref[:, slice(128 * (c - 4), 128 * (c - 3))]
                dsv = rs * (dn - n * jnp.sum(dn * n, axis=1, keepdims=True))
            else:
                dsv = dv_ref[:, slice(128 * (c - 8), 128 * (c - 7))]
            dy_ref[:, lanes] = dsv * _silu_grad(y, sg)
        _, lane, sig_b, t, nega, g = _beta_g(ba_ref[...], al_ref[...], dt_ref[...])
        dbg = dbg_ref[...]
        da = dbg * nega * _sigmoid(t)
        is_b = lane < DN_HEADS
        is_a = jnp.logical_and(lane >= DN_HEADS, lane < 2 * DN_HEADS)
        dba_ref[...] = jnp.where(is_b, dbg * sig_b * (1.0 - sig_b), jnp.where(is_a, da, 0.0))
        d_alog = jnp.sum(jnp.where(is_a, dbg * g, 0.0), axis=0, keepdims=True)
        d_dtb = jnp.sum(jnp.where(is_a, da, 0.0), axis=0, keepdims=True)
        row = lax.broadcasted_iota(jnp.int32, (8, 128), 0)
        dsm_ref[...] = dsm_ref[...] + jnp.where(row == 0, d_alog, jnp.where(row == 1, d_dtb, 0.0))

    blk = pl.BlockSpec((tr, 512), lambda i: (i, 0))
    return pl.pallas_call(
        body, grid=(s // tr,),
        in_specs=[blk, blk, blk, pl.BlockSpec((tr, 128), lambda i: (i, 0)),
                  pl.BlockSpec((tr, 1536), lambda i: (i, 0)),
                  pl.BlockSpec((8, 1536), lambda i: (jnp.maximum(i * nh - 1, 0), 0)),
                  pl.BlockSpec((tr, 128), lambda i: (i, COL_BA_128)),
                  pl.BlockSpec((8, 1536), lambda i: (0, 0)),
                  pl.BlockSpec((1, 128), lambda i: (0, 0)), pl.BlockSpec((1, 128), lambda i: (0, 0))],
        out_specs=[pl.BlockSpec((tr, 1536), lambda i: (i, 0)), pl.BlockSpec((tr, 128), lambda i: (i, 0)),
                   pl.BlockSpec((8, 128), lambda i: (0, 0))],
        out_shape=[_sds((s, 1536)), _sds((s, 128)), _sds((8, 128))],
        compiler_params=_cp(("arbitrary",)), name="dn_pre_bwd")(dqn, dkn, dv, dbg, proj, proj, proj, conv_w8,
                                                                 alog_l, dtb_l)


def _conv_bwd(dy, proj, conv_w8):
    s = dy.shape[0]
    tr = 512
    nh = tr // 8
    nblk = s // tr

    def body(dy_ref, dyn_ref, u_ref, halo_ref, w_ref, du_ref, dw_ref):
        i = pl.program_id(0)

        @pl.when(i == 0)
        def _():
            dw_ref[...] = jnp.zeros_like(dw_ref)

        keep_prev = (i > 0).astype(F32)
        keep_next = (i < nblk - 1).astype(F32)
        row = lax.broadcasted_iota(jnp.int32, (8, 128), 0)
        for c in range(12):
            lanes = slice(128 * c, 128 * c + 128)
            dyc = dy_ref[:, lanes]
            dcat = jnp.concatenate([dyc, dyn_ref[:, lanes] * keep_next], axis=0)
            xcat = jnp.concatenate([halo_ref[:, lanes] * keep_prev, u_ref[:, lanes]], axis=0)
            du = dyc * w_ref[CONV_W - 1:CONV_W, lanes]
            dwc = jnp.where(row == CONV_W - 1, jnp.sum(dyc * u_ref[:, lanes], axis=0, keepdims=True), 0.0)
            for k in range(1, CONV_W):
                du = du + pltpu.roll(dcat, tr + 8 - k, 0)[0:tr] * w_ref[CONV_W - 1 - k:CONV_W - k, lanes]
                ush = pltpu.roll(xcat, k, 0)[8:8 + tr]
                dwc = dwc + jnp.where(row == CONV_W - 1 - k, jnp.sum(dyc * ush, axis=0, keepdims=True), 0.0)
            du_ref[:, lanes] = du
            dw_ref[:, lanes] = dw_ref[:, lanes] + dwc

    return pl.pallas_call(
        body, grid=(nblk,),
        in_specs=[pl.BlockSpec((tr, 1536), lambda i: (i, 0)),
                  pl.BlockSpec((8, 1536), lambda i: (jnp.minimum((i + 1) * nh, s // 8 - 1), 0)),
                  pl.BlockSpec((tr, 1536), lambda i: (i, 0)),
                  pl.BlockSpec((8, 1536), lambda i: (jnp.maximum(i * nh - 1, 0), 0)),
                  pl.BlockSpec((8, 1536), lambda i: (0, 0))],
        out_specs=[pl.BlockSpec((tr, 1536), lambda i: (i, 0)), pl.BlockSpec((8, 1536), lambda i: (0, 0))],
        out_shape=[_sds((s, 1536)), _sds((8, 1536))],
        compiler_params=_cp(("arbitrary",)), name="conv_bwd")(dy, dy, proj, proj, conv_w8)


def _in_bwd_dx(d_qkv_dn, dz, dq_att, dk_att, dv_att, dgate, dba, w_bf, x, dy, norm_w):
    s = x.shape[0]
    tm = 256

    def body(a_ref, b_ref, c_ref, d_ref, e_ref, f_ref, g_ref, w_ref, x_ref, dy_ref, nw_ref,
             dx_ref, dp_ref, dnw_ref):
        i = pl.program_id(0)

        @pl.when(i == 0)
        def _():
            dnw_ref[...] = jnp.zeros_like(dnw_ref)

        dp = jnp.concatenate([r[...].astype(BF16) for r in (a_ref, b_ref, c_ref, d_ref, e_ref, f_ref, g_ref)],
                             axis=1)
        dp_ref[...] = dp
        dh = _dot(dp, w_ref[...])
        xv = x_ref[...]
        rstd = lax.rsqrt(jnp.mean(xv * xv, axis=-1, keepdims=True) + EPS)
        xh = xv * rstd
        dnw_ref[...] = dnw_ref[...] + jnp.sum(dh * xh, axis=0, keepdims=True)
        g = dh * nw_ref[...]
        dx_ref[...] = rstd * (g - xh * jnp.mean(g * xh, axis=-1, keepdims=True)) + dy_ref[...]

    def blk(n):
        return pl.BlockSpec((tm, n), lambda i: (i, 0))

    return pl.pallas_call(
        body, grid=(s // tm,),
        in_specs=[blk(1536), blk(512), blk(512), blk(512), blk(512), blk(512), blk(128),
                  pl.BlockSpec((D_IN_PAD, D_MODEL), lambda i: (0, 0)), blk(D_MODEL), blk(D_MODEL),
                  pl.BlockSpec((1, D_MODEL), lambda i: (0, 0))],
        out_specs=[blk(D_MODEL), blk(D_IN_PAD), pl.BlockSpec((8, D_MODEL), lambda i: (0, 0))],
        out_shape=[_sds((s, D_MODEL)), _sds((s, D_IN_PAD), BF16), _sds((8, D_MODEL))],
        compiler_params=_cp(("arbitrary",)), name="in_bwd_dx")(d_qkv_dn, dz, dq_att, dk_att, dv_att, dgate, dba,
                                                                w_bf, x, dy, norm_w)


def _in_bwd_dw(x, norm_w, dp):
    s = x.shape[0]
    tm = 512
    n_col = 3
    tn = D_IN_PAD // n_col

    def body(x_ref, nw_ref, dp_ref, dw_ref):
        i = pl.program_id(1)

        @pl.when(i == 0)
        def _():
            dw_ref[...] = jnp.zeros_like(dw_ref)

        xv = x_ref[...]
        rstd = lax.rsqrt(jnp.mean(xv * xv, axis=-1, keepdims=True) + EPS)
        h = (xv * rstd * nw_ref[...]).astype(BF16)
        dw_ref[...] = dw_ref[...] + _dot_tn(dp_ref[...], h)

    return pl.pallas_call(
        body, grid=(n_col, s // tm),
        in_specs=[pl.BlockSpec((tm, D_MODEL), lambda j, i: (i, 0)), pl.BlockSpec((1, D_MODEL), lambda j, i: (0, 0)),
                  pl.BlockSpec((tm, tn), lambda j, i: (i, j))],
        out_specs=pl.BlockSpec((tn, D_MODEL), lambda j, i: (j, 0)),
        out_shape=_sds((D_IN_PAD, D_MODEL)),
        compiler_params=_cp(("parallel", "arbitrary")), name="in_bwd_dw")(x, norm_w, dp)


def _flat(a):
    return a.reshape(-1, a.shape[-1])


def _as_pattern(a, r):
    return a if r == 1 else a.reshape(r, a.shape[0] // r, a.shape[1])


D_SHARD = D_IN // N_CHIPS
BA_START = 4 * D_DN
BA_PACKED = 4096
S1_HEAD = BA_START - D_SHARD
S1_BA = 2 * D_SHARD - BA_START


def _pack_rows(g):
    pad = jnp.zeros((D_IN_PAD - D_IN, g.shape[2]), g.dtype)
    s2_ba = 2 * DN_HEADS - S1_BA
    return jnp.concatenate([g[0][:D_SHARD], g[1][:S1_HEAD], g[2][s2_ba:D_SHARD], g[3][:D_SHARD],
                            g[1][S1_HEAD:D_SHARD], g[2][:s2_ba], pad], axis=0)


def _pack_w(g):
    n = g.shape[2]

    def body(g_ref, o_ref):
        g32 = g_ref.bitcast(jnp.uint32)
        o32 = o_ref.bitcast(jnp.uint32)
        full, head, ba1 = D_SHARD // 2, S1_HEAD // 2, S1_BA // 2
        ba2 = DN_HEADS - ba1
        pieces = [(0, 0, full), (1, 0, head), (2, ba2, full), (3, 0, full), (1, head, full), (2, 0, ba2)]
        at = 0
        for chip, lo, hi in pieces:
            o32[at:at + hi - lo, :] = g32[chip, lo:hi, :]
            at += hi - lo
        o32[at:D_IN_PAD // 2, :] = jnp.zeros((D_IN_PAD // 2 - at, n), jnp.uint32)

    vm = pl.BlockSpec(memory_space=pltpu.VMEM)
    return pl.pallas_call(body, in_specs=[vm], out_specs=vm, out_shape=_sds((D_IN_PAD, n), g.dtype),
                          compiler_params=_cp(), name="pack_w")(g)


def _unpack_rows(p, rows):
    mid = BA_PACKED + S1_BA
    pad = jnp.zeros((rows - D_SHARD, p.shape[1]), p.dtype)
    return jnp.stack([jnp.concatenate([p[0:D_SHARD], pad], axis=0),
                      jnp.concatenate([p[D_SHARD:BA_START], p[BA_PACKED:mid], pad], axis=0),
                      jnp.concatenate([p[mid:BA_PACKED + 2 * DN_HEADS], p[BA_START:BA_START + S1_HEAD], pad], axis=0),
                      jnp.concatenate([p[BA_START + S1_HEAD:BA_PACKED], pad], axis=0)])


def _lane_row(vec, offset):
    return jnp.pad(vec.reshape(1, -1), ((0, 0), (offset, 128 - offset - vec.shape[0])))


def _local_step(x, tgt, norm_w, w_bf, conv_w, a_log, dt_bias, dn_norm_w, q_norm_w, k_norm_w, rel_bias, w_out_bf):
    s = x.shape[0]
    nc = s // CHUNK
    conv_w8 = jnp.pad(conv_w, ((0, 8 - CONV_W), (0, 0)))
    alog_l = _lane_row(a_log.reshape(-1), DN_HEADS)
    dtb_l = _lane_row(dt_bias.reshape(-1), DN_HEADS)
    dnw_t = jnp.tile(dn_norm_w.reshape(1, DN_HD), (1, DN_HEADS))
    qw_t = jnp.tile(q_norm_w.reshape(1, ATT_HD), (1, ATT_HEADS))
    kw_t = jnp.tile(k_norm_w.reshape(1, ATT_HD), (1, ATT_HEADS))
    bd128 = _block_ones(512, DN_HD)
    bd64 = _block_ones(512, ATT_HD)
    tri = _tri_incl()
    sel = _lane_select()
    buckets = _bucket_tables()

    proj = _in_proj(x, norm_w, w_bf)
    qn, kn, v_dn, bg = _dn_pre(proj, conv_w8, alog_l, dtb_l)
    bgt = bg[:, 0:8].reshape(nc, CHUNK, 8).transpose(0, 2, 1)
    u, w, qd, kt, attn, t_inv, gl = _dn_prep(qn, kn, v_dn, bg, bgt, tri, sel)
    o_dn, vn, sp = _dn_scan(u, w, qd, kt, attn, gl)
    q1, k1, v1, q4, k4, v4, q16, k16, v16 = _att_pre(proj, qw_t, kw_t, bd64)
    rs = [r for _, r in PATTERNS]
    qkv = [(q1, k1, v1, 0), (_flat(q4), _flat(k4), _flat(v4), 0), (_flat(q16), _flat(k16), _flat(v16), 0)]
    bias = _bias_fwd(rel_bias, buckets)
    o_pats, lse_pats = [], []
    for p, r in enumerate(rs):
        o_p, lse_p = _att_fwd(*qkv[p], bias, p, r, "att_fwd_r%d" % r)
        o_pats.append(_as_pattern(o_p, r))
        lse_pats.append(_as_pattern(lse_p, r))
    mixed, o_att, l1, l4, l16 = _post_fwd(o_dn, proj, o_pats, lse_pats, dnw_t, bd128)
    dy, loss_blk = _out_fwd(x, mixed, w_out_bf, tgt)

    dmixed, d_w_out = _out_bwd(dy, mixed, w_out_bf)
    do_dn, dz, dgate, do1, do4, do16, dl1, dl4, dl16, d_dnw = _post_bwd(dmixed, o_dn, proj, o_att, dnw_t, bd128)
    side = [(do1, l1, dl1), (_flat(do4), _flat(l4), _flat(dl4)), (_flat(do16), _flat(l16), _flat(dl16))]
    dq_pats, dk_pats, dv_pats, ds_accs = [], [], [], []
    for p, r in enumerate(rs):
        dq_p, dk_p, dv_p, ds_p = _att_bwd(*qkv[p], *side[p], bias, p, r, "att_bwd_r%d" % r)
        dq_pats.append(_as_pattern(dq_p, r))
        dk_pats.append(_as_pattern(dk_p, r))
        dv_pats.append(_as_pattern(dv_p, r))
        ds_accs.append(ds_p)
    d_rel_bias = _bias_bwd(ds_accs, buckets)
    dq_att, dk_att, dv_att, d_qw, d_kw = _att_pre_bwd(dq_pats, dk_pats, dv_pats, proj, qw_t, kw_t, bd64)
    du, dqd, dkt, dw, dattn, dgl = _dn_scan_bwd(do_dn, sp, qd, kt, w, vn, attn, gl)
    dqn, dkn, dv_dn, dbg = _dn_prep_bwd(qn, kn, v_dn, bg, bgt, tri, sel, t_inv, attn, u, w, du, dw, dqd, dkt, dattn,
                                        dgl)
    dyc, dba, dsm = _dn_pre_bwd(dqn, dkn, dv_dn, dbg, proj, conv_w8, alog_l, dtb_l)
    d_qkv_dn, d_conv8 = _conv_bwd(dyc, proj, conv_w8)
    grad_x, dp, d_nw8 = _in_bwd_dx(d_qkv_dn, dz, dq_att, dk_att, dv_att, dgate, dba, w_bf, x, dy, norm_w)
    d_w_in_t = _in_bwd_dw(x, norm_w, dp)

    grads = dict(
        norm_w=d_nw8[0:1, :],
        w_in_t=d_w_in_t,
        conv_w=d_conv8[0:CONV_W, :],
        a_log=dsm[0:1, DN_HEADS:2 * DN_HEADS],
        dt_bias=dsm[1:2, DN_HEADS:2 * DN_HEADS],
        dn_norm_w=d_dnw[0:1, :],
        q_norm_w=d_qw[0:1, :].reshape(ATT_HEADS, ATT_HD),
        k_norm_w=d_kw[0:1, :].reshape(ATT_HEADS, ATT_HD),
        rel_bias=d_rel_bias,
        w_out=d_w_out,
    )
    return loss_blk[0, 0], grad_x, grads


MESH_ID = pl.DeviceIdType.MESH
ANY = pl.BlockSpec(memory_space=pl.ANY)


def _position():
    return lax.axis_index("x"), lax.axis_index("y"), lax.axis_index("c")


def _other_chips(x, y):
    return [(1 - x, y), (x, 1 - y), (1 - x, 1 - y)]


SHARD_PAD = 1040
WIN = 528


def _row_split(n):
    return (n // 2) // 128 * 128


def _part(ref, cc):
    n = ref.shape[0]
    sp = _row_split(n)
    return ref.at[pl.ds(0, sp)] if cc == 0 else ref.at[pl.ds(sp, n - sp)]


def _gather_weights(wt_s, w_out_s, conv_s):
    def body(a_ref, b_ref, c_ref, ga_ref, gb_ref, gc_ref, send_sems, recv_sems, loc_sems, a_vmem, b_vmem):
        x, y, c = _position()
        me = 2 * x + y
        sib = (x, y, 1 - c)
        big = ((a_ref, ga_ref), (b_ref, gb_ref))
        stage_in = [pltpu.make_async_copy(a_ref, a_vmem, loc_sems.at[0]),
                    pltpu.make_async_copy(b_ref, b_vmem, loc_sems.at[1])]
        local = [pltpu.make_async_copy(a_vmem, ga_ref.at[me], loc_sems.at[0]),
                 pltpu.make_async_copy(b_vmem, gb_ref.at[me], loc_sems.at[1]),
                 pltpu.make_async_copy(c_ref, gc_ref.at[me], loc_sems.at[2])]
        for cp in stage_in:
            cp.start()
        local[2].start()
        for cp in stage_in:
            cp.wait()
        for cp in local[:2]:
            cp.start()
        others = _other_chips(x, y)

        def exchange(cc):
            sends = []
            for j, (px, py) in enumerate(others):
                for t, (src, dst) in enumerate(big):
                    k = 3 * j + t
                    sends.append(pltpu.make_async_remote_copy(
                        src_ref=_part(src, cc), dst_ref=_part(dst.at[me], cc), send_sem=send_sems.at[k],
                        recv_sem=recv_sems.at[k], device_id=(px, py, c), device_id_type=MESH_ID))
                sends.append(pltpu.make_async_remote_copy(
                    src_ref=c_ref, dst_ref=gc_ref.at[me], send_sem=send_sems.at[3 * j + 2],
                    recv_sem=recv_sems.at[3 * j + 2], device_id=(px, py, c), device_id_type=MESH_ID))
            for cp in sends:
                cp.start()
            for j, (px, py) in enumerate(others):
                src_chip = 2 * px + py
                for t, (src, dst) in enumerate(big):
                    landed = _part(dst.at[src_chip], cc)
                    pltpu.make_async_remote_copy(
                        src_ref=_part(src, cc), dst_ref=landed, send_sem=send_sems.at[3 * j + t],
                        recv_sem=recv_sems.at[3 * j + t], device_id=(px, py, c), device_id_type=MESH_ID).wait_recv()
                    k = 9 + 2 * j + t
                    fwd = pltpu.make_async_remote_copy(
                        src_ref=landed, dst_ref=landed, send_sem=send_sems.at[k], recv_sem=recv_sems.at[k],
                        device_id=sib, device_id_type=MESH_ID)
                    fwd.start()
                    sends.append(fwd)
                pltpu.make_async_remote_copy(
                    src_ref=c_ref, dst_ref=gc_ref.at[src_chip], send_sem=send_sems.at[3 * j + 2],
                    recv_sem=recv_sems.at[3 * j + 2], device_id=(px, py, c), device_id_type=MESH_ID).wait_recv()
            for j, (px, py) in enumerate(others):
                src_chip = 2 * px + py
                for t, (src, dst) in enumerate(big):
                    k = 9 + 2 * j + t
                    theirs = _part(dst.at[src_chip], 1 - cc)
                    pltpu.make_async_remote_copy(
                        src_ref=theirs, dst_ref=theirs, send_sem=send_sems.at[k], recv_sem=recv_sems.at[k],
                        device_id=sib, device_id_type=MESH_ID).wait_recv()
            for cp in sends:
                cp.wait_send()

        for cc in (0, 1):
            pl.when(c == cc)(functools.partial(exchange, cc))
        for cp in local:
            cp.wait()

    srcs = (wt_s, w_out_s, conv_s)
    n_sem = 9 + 6
    return pl.pallas_call(
        body, in_specs=[ANY] * 3, out_specs=[ANY] * 3,
        out_shape=[_sds((N_CHIPS,) + a.shape, a.dtype) for a in srcs],
        scratch_shapes=[pltpu.SemaphoreType.DMA((n_sem,)), pltpu.SemaphoreType.DMA((n_sem,)),
                        pltpu.SemaphoreType.DMA((3,)), pltpu.VMEM(wt_s.shape, wt_s.dtype),
                        pltpu.VMEM(w_out_s.shape, w_out_s.dtype)],
        compiler_params=_cp(), name="gather_weights")(*srcs)


def _exchange_grads(gw_in4, gw_out4, small):
    n_big = 2
    n_dev = 8

    def body(a_ref, b_ref, s_ref, ra_ref, rb_ref, rs_ref, send_sems, recv_sems, loc_sem):
        x, y, c = _position()
        dev = 4 * x + 2 * y + c
        pairs = ((a_ref, ra_ref), (b_ref, rb_ref))
        local = pltpu.make_async_copy(s_ref, rs_ref.at[dev], loc_sem)
        local.start()
        sends = []
        for j, (px, py) in enumerate(_other_chips(x, y)):
            for t, (src, dst) in enumerate(pairs):
                k = n_big * j + t
                sends.append(pltpu.make_async_remote_copy(
                    src_ref=src.at[2 * px + py], dst_ref=dst.at[j], send_sem=send_sems.at[k],
                    recv_sem=recv_sems.at[k], device_id=(px, py, c), device_id_type=MESH_ID))
        flips = [(dx, dy, dc) for dx in (0, 1) for dy in (0, 1) for dc in (0, 1)][1:]
        for f, (dx, dy, dc) in enumerate(flips):
            k = 3 * n_big + f
            peer = (x ^ dx, y ^ dy, c ^ dc)
            sends.append(pltpu.make_async_remote_copy(
                src_ref=s_ref, dst_ref=rs_ref.at[dev], send_sem=send_sems.at[k], recv_sem=recv_sems.at[k],
                device_id=peer, device_id_type=MESH_ID))
        for cp in sends:
            cp.start()
        for j, (px, py) in enumerate(_other_chips(x, y)):
            for t, (src, dst) in enumerate(pairs):
                k = n_big * j + t
                pltpu.make_async_remote_copy(
                    src_ref=src.at[0], dst_ref=dst.at[j], send_sem=send_sems.at[k], recv_sem=recv_sems.at[k],
                    device_id=(px, py, c), device_id_type=MESH_ID).wait_recv()
        for f, (dx, dy, dc) in enumerate(flips):
            k = 3 * n_big + f
            peer = (x ^ dx, y ^ dy, c ^ dc)
            pltpu.make_async_remote_copy(
                src_ref=s_ref, dst_ref=rs_ref.at[4 * peer[0] + 2 * peer[1] + peer[2]], send_sem=send_sems.at[k],
                recv_sem=recv_sems.at[k], device_id=peer, device_id_type=MESH_ID).wait_recv()
        for cp in sends:
            cp.wait_send()
        local.wait()

    n_sem = 3 * n_big + n_dev - 1
    return pl.pallas_call(
        body, in_specs=[ANY] * 3, out_specs=[ANY] * 3,
        out_shape=[_sds((3,) + gw_in4.shape[1:], gw_in4.dtype), _sds((3,) + gw_out4.shape[1:], gw_out4.dtype),
                   _sds((n_dev,) + small.shape, small.dtype)],
        scratch_shapes=[pltpu.SemaphoreType.DMA((n_sem,)), pltpu.SemaphoreType.DMA((n_sem,)),
                        pltpu.SemaphoreType.DMA],
        name="exchange_grads")(gw_in4, gw_out4, small)


def _plane_sum(own, recv, name):
    rows, cols = own.shape
    tr = 128

    def body(o_ref, r_ref, out_ref):
        acc = o_ref[...]
        for j in range(3):
            acc = acc + r_ref[j].astype(F32)
        out_ref[...] = acc

    return pl.pallas_call(
        body, grid=(pl.cdiv(rows, tr),),
        in_specs=[pl.BlockSpec((tr, cols), lambda i: (i, 0)), pl.BlockSpec((3, tr, cols), lambda i: (0, i, 0))],
        out_specs=pl.BlockSpec((tr, cols), lambda i: (i, 0)), out_shape=_sds((rows, cols)),
        compiler_params=_cp(("parallel",)), name=name)(own, recv)


def _sum_pair(a, b, name):
    n, rows, cols = a.shape
    tr = 128

    def body(a_ref, b_ref, s_ref, sb_ref):
        tot = a_ref[...] + b_ref[...]
        s_ref[...] = tot
        sb_ref[...] = tot.astype(BF16)

    blk = pl.BlockSpec((1, tr, cols), lambda k, i: (k, i, 0))
    return pl.pallas_call(
        body, grid=(n, pl.cdiv(rows, tr)), in_specs=[blk, blk], out_specs=[blk, blk],
        out_shape=[_sds(a.shape), _sds(a.shape, BF16)],
        compiler_params=_cp(("parallel", "parallel")), name=name)(a, b)


def _fill_parts(f_in, f_out, rows_in, rows_out):
    def body(a_ref, b_ref, fa_ref, fb_ref, send_sems, recv_sems, loc_sems, a_vmem, b_vmem):
        x, y, c = _position()
        sib = (x, y, 1 - c)
        pairs = ((a_ref, fa_ref), (b_ref, fb_ref))
        stage_in = [pltpu.make_async_copy(a_ref, a_vmem, loc_sems.at[0]),
                    pltpu.make_async_copy(b_ref, b_vmem, loc_sems.at[1])]
        for cp in stage_in:
            cp.start()
        for cp in stage_in:
            cp.wait()

        def fill(cc):
            mine = [_part(dst, cc) for _, dst in pairs]
            srcs = [src.at[pl.ds(0, m.shape[0])] for src, m in zip((a_vmem, b_vmem), mine)]
            local = [pltpu.make_async_copy(s, m, loc_sems.at[t]) for t, (s, m) in enumerate(zip(srcs, mine))]
            sends = [pltpu.make_async_remote_copy(src_ref=s, dst_ref=m, send_sem=send_sems.at[t],
                                                  recv_sem=recv_sems.at[t], device_id=sib, device_id_type=MESH_ID)
                     for t, (s, m) in enumerate(zip(srcs, mine))]
            for cp in local + sends:
                cp.start()
            for t, (_, dst) in enumerate(pairs):
                theirs = _part(dst, 1 - cc)
                pltpu.make_async_remote_copy(src_ref=theirs, dst_ref=theirs, send_sem=send_sems.at[t],
                                             recv_sem=recv_sems.at[t], device_id=sib, device_id_type=MESH_ID).wait_recv()
            for cp in sends:
                cp.wait_send()
            for cp in local:
                cp.wait()

        for cc in (0, 1):
            pl.when(c == cc)(functools.partial(fill, cc))

    return pl.pallas_call(
        body, in_specs=[ANY] * 2, out_specs=[ANY] * 2,
        out_shape=[_sds((rows_in, D_MODEL)), _sds((rows_out, D_MODEL))],
        scratch_shapes=[pltpu.SemaphoreType.DMA((2,)), pltpu.SemaphoreType.DMA((2,)), pltpu.SemaphoreType.DMA((2,)),
                        pltpu.VMEM(f_in.shape, F32), pltpu.VMEM(f_out.shape, F32)],
        compiler_params=_cp(), name="fill_parts")(f_in, f_out)


def _swap_windows(u_in, u_out, win_in, win_out):
    def body(a_ref, b_ref, ra_ref, rb_ref, send_sems, recv_sems):
        x, y, c = _position()
        sib = (x, y, 1 - c)
        cps = []
        for t, (src, dst, win) in enumerate(((a_ref, ra_ref, win_in), (b_ref, rb_ref, win_out))):
            split = _row_split(src.shape[1])
            start = pl.multiple_of((1 - c) * split, split)
            cps.append(pltpu.make_async_remote_copy(
                src_ref=src.at[:, pl.ds(start, win), :], dst_ref=dst, send_sem=send_sems.at[t],
                recv_sem=recv_sems.at[t], device_id=sib, device_id_type=MESH_ID))
        for cp in cps:
            cp.start()
        for cp in cps:
            cp.wait_recv()
        for cp in cps:
            cp.wait_send()

    return pl.pallas_call(
        body, in_specs=[ANY] * 2, out_specs=[ANY] * 2,
        out_shape=[_sds((N_CHIPS, win_in, D_MODEL)), _sds((N_CHIPS, win_out, D_MODEL))],
        scratch_shapes=[pltpu.SemaphoreType.DMA((2,)), pltpu.SemaphoreType.DMA((2,))],
        name="swap_windows")(u_in, u_out)


SMALL_LAYOUT = (("norm_w", 1024), ("conv_w", 6144), ("a_log", 128), ("dt_bias", 128), ("dn_norm_w", 128),
                ("q_norm_w", 512), ("k_norm_w", 512), ("rel_bias", 256))
SMALL_TOTAL = sum(n for _, n in SMALL_LAYOUT)


def _small_offset(name):
    off = 0
    for n, size in SMALL_LAYOUT:
        if n == name:
            return off
        off += size
    raise KeyError(name)


def _pack_small(grads):
    parts = []
    for name, size in SMALL_LAYOUT:
        flat = grads[name].reshape(1, -1)
        parts.append(jnp.pad(flat, ((0, 0), (0, size - flat.shape[1]))))
    return jnp.concatenate(parts, axis=1)


def _sum_small(rows):
    n_dev = rows.shape[0]
    q_off = _small_offset("q_norm_w")
    k_off = _small_offset("k_norm_w")

    def body(r_ref, tot_ref, qk_ref):
        tot = r_ref[0:1, :]
        for d in range(1, n_dev):
            tot = tot + r_ref[d:d + 1, :]
        tot_ref[...] = tot
        for row, off in ((0, q_off), (1, k_off)):
            s4 = tot[:, off:off + 128] + tot[:, off + 128:off + 256] + tot[:, off + 256:off + 384] \
                + tot[:, off + 384:off + 512]
            qk_ref[row:row + 1, :] = s4 + pltpu.roll(s4, ATT_HD, 1)

    return pl.pallas_call(
        body, in_specs=[pl.BlockSpec(memory_space=pltpu.VMEM)],
        out_specs=[pl.BlockSpec(memory_space=pltpu.VMEM)] * 2,
        out_shape=[_sds((1, SMALL_TOTAL)), _sds((2, 128))],
        compiler_params=_cp(), name="sum_small")(rows)


def _adamw_math(w, g, m, v):
    m = ADAM_B1 * m + (1.0 - ADAM_B1) * g
    v = ADAM_B2 * v + (1.0 - ADAM_B2) * (g * g)
    m_hat = m / (1.0 - ADAM_B1 ** ADAM_STEP)
    v_hat = v / (1.0 - ADAM_B2 ** ADAM_STEP)
    delta = -ADAM_LR * (m_hat / (jnp.sqrt(v_hat) + ADAM_EPS) + ADAM_WD * w)
    return delta, m, v


def _adamw_big(g, w, m, v, name):
    rows, cols = w.shape
    tr = 128

    def body(g_ref, w_ref, m_ref, v_ref, go_ref, d_ref, nm_ref, nv_ref):
        g = g_ref[...]
        go_ref[...] = g
        d_ref[...], nm_ref[...], nv_ref[...] = _adamw_math(w_ref[...], g, m_ref[...], v_ref[...])

    blk = pl.BlockSpec((tr, cols), lambda i: (i, 0))
    return pl.pallas_call(
        body, grid=(pl.cdiv(rows, tr),), in_specs=[blk] * 4, out_specs=[blk] * 4,
        out_shape=[_sds((rows, cols))] * 4, compiler_params=_cp(("parallel",)), name=name)(g, w, m, v)


def _adamw_rows(g, w, m, v, name):
    rows, cols = w.shape
    tr = 128

    def body(g_ref, w_ref, m_ref, v_ref, go_ref, d_ref, nm_ref, nv_ref, s_g, s_d, s_m, s_v):
        g = g_ref[...]
        s_g[...] = g
        s_d[...], s_m[...], s_v[...] = _adamw_math(w_ref[...], g, m_ref[...], v_ref[...])
        for i in range(tr):
            for scr, out in ((s_g, go_ref), (s_d, d_ref), (s_m, nm_ref), (s_v, nv_ref)):
                out[i] = scr[i:i + 1, :]

    blk = pl.BlockSpec((tr, cols), lambda i: (i, 0))
    oblk = pl.BlockSpec((tr, 1, cols), lambda i: (i, 0, 0))
    return pl.pallas_call(
        body, grid=(pl.cdiv(rows, tr),), in_specs=[blk] * 4, out_specs=[oblk] * 4,
        out_shape=[_sds((rows, 1, cols))] * 4, scratch_shapes=[pltpu.VMEM((tr, cols), F32)] * 4,
        compiler_params=_cp(("parallel",)), name=name)(g, w, m, v)


def _adamw_small(w, g, m, v, name):
    def body(w_ref, g_ref, m_ref, v_ref, d_ref, nm_ref, nv_ref):
        d_ref[...], nm_ref[...], nv_ref[...] = _adamw_math(w_ref[...], g_ref[...], m_ref[...], v_ref[...])

    vm = pl.BlockSpec(memory_space=pltpu.VMEM)
    return pl.pallas_call(body, in_specs=[vm] * 4, out_specs=[vm] * 3, out_shape=[_sds(w.shape)] * 3,
                          compiler_params=_cp(), name=name)(w, g, m, v)


WEIGHTS = ("norm_w", "w_in", "conv_w", "a_log", "dt_bias", "dn_norm_w", "q_norm_w", "k_norm_w", "rel_bias", "w_out")


def kernel(x, norm_w, w_in, conv_w, a_log, dt_bias, dn_norm_w, q_norm_w, k_norm_w, rel_bias, w_out, loss_target, m_norm_w, m_w_in, m_conv_w, m_a_log, m_dt_bias, m_dn_norm_w, m_q_norm_w, m_k_norm_w, m_rel_bias, m_w_out, v_norm_w, v_w_in, v_conv_w, v_a_log, v_dt_bias, v_dn_norm_w, v_q_norm_w, v_k_norm_w, v_rel_bias, v_w_out):
    xi, yi, ci = _position()
    chip = 2 * xi + yi
    w_loc = dict(norm_w=norm_w, w_in=w_in[0].T, conv_w=conv_w[0], a_log=a_log, dt_bias=dt_bias, dn_norm_w=dn_norm_w,
                 q_norm_w=q_norm_w, k_norm_w=k_norm_w, rel_bias=rel_bias, w_out=w_out[0])
    m_loc = dict(norm_w=m_norm_w, w_in=m_w_in[0].T, conv_w=m_conv_w[0], a_log=m_a_log, dt_bias=m_dt_bias,
                 dn_norm_w=m_dn_norm_w, q_norm_w=m_q_norm_w, k_norm_w=m_k_norm_w, rel_bias=m_rel_bias,
                 w_out=m_w_out[0])
    v_loc = dict(norm_w=v_norm_w, w_in=v_w_in[0].T, conv_w=v_conv_w[0], a_log=v_a_log, dt_bias=v_dt_bias,
                 dn_norm_w=v_dn_norm_w, q_norm_w=v_q_norm_w, k_norm_w=v_k_norm_w, rel_bias=v_rel_bias,
                 w_out=v_w_out[0])

    wt_pad = jnp.pad(w_loc["w_in"].astype(BF16), ((0, SHARD_PAD - D_SHARD), (0, 0)))
    g_in, g_out, g_conv = _gather_weights(wt_pad, w_loc["w_out"].astype(BF16), w_loc["conv_w"])
    wt_full = _pack_w(g_in)
    w_out_full = g_out.reshape(D_MODEL, D_MODEL)
    conv_full = g_conv.transpose(1, 0, 2).reshape(CONV_W, 3 * D_DN)

    loss_local, grad_x, grads = _local_step(x[0], loss_target[0], norm_w, wt_full, conv_full, a_log, dt_bias,
                                            dn_norm_w, q_norm_w, k_norm_w, rel_bias, w_out_full)
    loss = lax.psum(loss_local, ("x", "y", "c"))

    u_in = _unpack_rows(grads["w_in_t"], SHARD_PAD)
    u_out = grads["w_out"].reshape(N_CHIPS, D_MODEL // N_CHIPS, D_MODEL)
    win_out = u_out.shape[1] // 2
    sib_in, sib_out = _swap_windows(u_in, u_out, WIN, win_out)
    in_mine = lax.dynamic_slice_in_dim(u_in, ci * _row_split(SHARD_PAD), WIN, axis=1)
    out_mine = lax.dynamic_slice_in_dim(u_out, ci * win_out, win_out, axis=1)
    h_in, hb_in = _sum_pair(in_mine, sib_in, "chip_sum_w_in")
    h_out, hb_out = _sum_pair(out_mine, sib_out, "chip_sum_w_out")
    small = _pack_small(grads)
    r_in, r_out, r_small = _exchange_grads(hb_in, hb_out, small)
    f_in = _plane_sum(lax.dynamic_index_in_dim(h_in, chip, 0, keepdims=False), r_in, "shard_sum_w_in")
    f_out = _plane_sum(lax.dynamic_index_in_dim(h_out, chip, 0, keepdims=False), r_out, "shard_sum_w_out")
    full_in, full_out = _fill_parts(f_in, f_out, SHARD_PAD, u_out.shape[1])
    tot_small, qk = _sum_small(r_small.reshape(8, SMALL_TOTAL))

    def small_grad(name, n):
        off = _small_offset(name)
        return tot_small[:, off:off + n]

    conv_all = small_grad("conv_w", CONV_W * 3 * D_DN).reshape(CONV_W, 3 * D_DN)
    g_small = dict(
        norm_w=small_grad("norm_w", D_MODEL),
        conv_w=lax.dynamic_slice_in_dim(conv_all, chip * (3 * D_DN // N_CHIPS), 3 * D_DN // N_CHIPS, axis=1),
        a_log=small_grad("a_log", DN_HEADS),
        dt_bias=small_grad("dt_bias", DN_HEADS),
        dn_norm_w=small_grad("dn_norm_w", DN_HD),
        q_norm_w=qk[0:1, 0:ATT_HD],
        k_norm_w=qk[1:2, 0:ATT_HD],
        rel_bias=small_grad("rel_bias", ATT_HEADS * N_BUCKETS).reshape(ATT_HEADS, N_BUCKETS),
    )

    out_g, out_d, out_m, out_v = {}, {}, {}, {}
    out_g["w_in"], out_d["w_in"], out_m["w_in"], out_v["w_in"] = _adamw_rows(
        full_in, w_loc["w_in"], m_loc["w_in"], v_loc["w_in"], "adamw_w_in")
    out_g["w_out"], out_d["w_out"], out_m["w_out"], out_v["w_out"] = _adamw_big(
        full_out, w_loc["w_out"], m_loc["w_out"], v_loc["w_out"], "adamw_w_out")
    for name in g_small:
        out_g[name] = g_small[name]
        out_d[name], out_m[name], out_v[name] = _adamw_small(w_loc[name], g_small[name], m_loc[name], v_loc[name],
                                                             "adamw_" + name)
    for d in (out_g, out_d, out_m, out_v):
        d["w_in"] = d["w_in"].transpose(1, 2, 0)
        for name in ("conv_w", "w_out"):
            d[name] = d[name][None]
    return (loss, grad_x[None], *[out_g[n] for n in WEIGHTS], *[out_d[n] for n in WEIGHTS],
            *[out_m[n] for n in WEIGHTS], *[out_v[n] for n in WEIGHTS])
```

```python
import functools
import math

import numpy as np
import jax
import jax.numpy as jnp
from jax import lax
from jax.experimental import pallas as pl
from jax.experimental.pallas import tpu as pltpu

F32 = jnp.float32
BF16 = jnp.bfloat16
HI = lax.Precision.HIGHEST

D_MODEL = 1024
D_DN = 512
DN_HEADS = 4
DN_HD = 128
CONV_W = 4
CHUNK = 64
D_ATT = 512
ATT_HEADS = 8
ATT_HD = 64
PATTERNS = ((128, 1), (512, 4), (2048, 16))
N_BUCKETS = 32
MAX_DISTANCE = 2048
D_IN = 4 * D_DN + 2 * DN_HEADS + 4 * D_ATT
D_IN_PAD = 4224
EPS = 1e-6
BLK = 128
NEG = -1e30
N_CHIPS = 4

ADAM_LR = 0.001
ADAM_B1 = 0.9
ADAM_B2 = 0.999
ADAM_EPS = 1e-08
ADAM_WD = 0.01
ADAM_STEP = 10

VMEM_LIMIT = 56 * 1024 * 1024

COL_Z = 3
COL_ATT_Q = 4
COL_ATT_K = 5
COL_ATT_V = 6
COL_GATE = 7
COL_BA_128 = 32


def _cp(sem=None):
    if sem is None:
        return pltpu.CompilerParams(vmem_limit_bytes=VMEM_LIMIT)
    return pltpu.CompilerParams(dimension_semantics=sem, vmem_limit_bytes=VMEM_LIMIT)


def _sds(shape, dtype=F32):
    return jax.ShapeDtypeStruct(shape, dtype)


def _mm(a, b):
    return jnp.dot(a.astype(BF16), b.astype(BF16), preferred_element_type=F32)


def _mm_nt(a, b):
    return lax.dot_general(a.astype(BF16), b.astype(BF16), (((1,), (1,)), ((), ())),
                           preferred_element_type=F32)


def _mm_tn(a, b):
    return lax.dot_general(a.astype(BF16), b.astype(BF16), (((0,), (0,)), ((), ())),
                           preferred_element_type=F32)


def _mmx(a, b):
    return jnp.dot(a, b, precision=HI, preferred_element_type=F32)


def _mmx_nt(a, b):
    return lax.dot_general(a, b, (((1,), (1,)), ((), ())), precision=HI, preferred_element_type=F32)


def _mmx_tn(a, b):
    return lax.dot_general(a, b, (((0,), (0,)), ((), ())), precision=HI, preferred_element_type=F32)


def _dot(a, b):
    return jnp.dot(a, b, preferred_element_type=F32)


def _dot_nt(a, b):
    return lax.dot_general(a, b, (((1,), (1,)), ((), ())), preferred_element_type=F32)


def _dot_tn(a, b):
    return lax.dot_general(a, b, (((0,), (0,)), ((), ())), preferred_element_type=F32)


def _split(a):
    hi = a.astype(BF16)
    return hi, (a - hi.astype(F32)).astype(BF16)


def _mm3(a_s, b_s):
    return _dot(a_s[0], b_s[0]) + _dot(a_s[0], b_s[1]) + _dot(a_s[1], b_s[0])


def _mm3_tn(a_s, b_s):
    return _dot_tn(a_s[0], b_s[0]) + _dot_tn(a_s[0], b_s[1]) + _dot_tn(a_s[1], b_s[0])


def _interleave(gens):
    live = list(gens)
    while live:
        nxt = []
        for g in live:
            try:
                next(g)
                nxt.append(g)
            except StopIteration:
                pass
        live = nxt


def _segsum(x, bd):
    hi = x.astype(BF16)
    r1 = x - hi.astype(F32)
    mid = r1.astype(BF16)
    lo = (r1 - mid.astype(F32)).astype(BF16)
    return (jnp.dot(hi, bd, preferred_element_type=F32) + jnp.dot(mid, bd, preferred_element_type=F32)
            + jnp.dot(lo, bd, preferred_element_type=F32))


def _compact_mat(seg, n=512):
    slot = 128 * seg // n
    src = np.arange(n)[:, None]
    dst = np.arange(128)[None, :]
    return jnp.asarray((src // seg == dst // slot).astype(np.float32), dtype=BF16)


def _seg_mean(x, cm, seg):
    hi, lo = _split(x)
    return (_dot(hi, cm) + _dot(lo, cm)) * (1.0 / seg)


def _seg_expand(c, cm, seg):
    hi, lo = _split(c)
    return (_dot_nt(hi, cm) + _dot_nt(lo, cm)) * (cm.shape[0] / (128.0 * seg))


def _seg_rstd(x, cm, seg):
    return _seg_expand(lax.rsqrt(_seg_mean(x * x, cm, seg) + EPS), cm, seg)


def _sigmoid(x):
    return 1.0 / (1.0 + jnp.exp(-x))


def _silu_grad(x, s):
    return s * (1.0 + x * (1.0 - s))


def _tri_incl():
    i = np.arange(CHUNK)
    return jnp.asarray((i[:, None] >= i[None, :]).astype(np.float32))


def _t5_bucket(dist):
    max_exact = N_BUCKETS // 2
    d = np.maximum(dist, 1).astype(np.float64)
    large = max_exact + (np.log(d / max_exact) / math.log(MAX_DISTANCE / max_exact)
                         * (N_BUCKETS - max_exact)).astype(np.int32)
    large = np.minimum(large, N_BUCKETS - 1)
    return np.where(dist < max_exact, dist, large).astype(np.int32)


def _bucket_tables():
    qi = np.arange(BLK)[:, None]
    kj = np.arange(2 * BLK)[None, :]
    step = qi - kj + BLK
    return jnp.asarray(np.stack([_t5_bucket(np.clip(step, 0, None) * r) for _, r in PATTERNS]))


def _in_proj(x, norm_w, wt_bf):
    s = x.shape[0]
    tm = 512

    def body(x_ref, nw_ref, w_ref, o_ref):
        xv = x_ref[...]
        rstd = lax.rsqrt(jnp.mean(xv * xv, axis=-1, keepdims=True) + EPS)
        h = (xv * rstd * nw_ref[...]).astype(BF16)
        o_ref[...] = _dot_nt(h, w_ref[...])

    return pl.pallas_call(
        body, grid=(s // tm,),
        in_specs=[pl.BlockSpec((tm, D_MODEL), lambda i: (i, 0)),
                  pl.BlockSpec((1, D_MODEL), lambda i: (0, 0)),
                  pl.BlockSpec((D_IN_PAD, D_MODEL), lambda i: (0, 0))],
        out_specs=pl.BlockSpec((tm, D_IN_PAD), lambda i: (i, 0)),
        out_shape=_sds((s, D_IN_PAD)), compiler_params=_cp(("parallel",)), name="in_proj")(x, norm_w, wt_bf)


def _conv_group(cur, halo, w_ref, c):
    rows = cur.shape[0]
    lanes = slice(128 * c, 128 * c + 128)
    xcat = jnp.concatenate([halo, cur], axis=0)
    y = cur * w_ref[CONV_W - 1:CONV_W, lanes]
    for k in range(1, CONV_W):
        sh = pltpu.roll(xcat, k, 0)[8:8 + rows]
        y = y + sh * w_ref[CONV_W - 1 - k:CONV_W - k, lanes]
    return y


def _beta_g(ba, alog_l, dtb_l):
    lane = lax.broadcasted_iota(jnp.int32, ba.shape, 1)
    sig_b = _sigmoid(ba)
    t = ba + dtb_l
    softplus = jnp.maximum(t, 0.0) + jnp.log(1.0 + jnp.exp(-jnp.abs(t)))
    nega = -jnp.exp(alog_l)
    g = nega * softplus
    out = jnp.where(lane < DN_HEADS, sig_b, jnp.where(lane < 2 * DN_HEADS, g, 0.0))
    return out, lane, sig_b, t, nega, g


def _dn_pre(proj, conv_w8, alog_l, dtb_l):
    s = proj.shape[0]
    tr = 512
    nh = tr // 8

    def body(u_ref, halo_ref, ba_ref, w_ref, al_ref, dt_ref, q_ref, k_ref, v_ref, bg_ref):
        i = pl.program_id(0)
        keep = (i > 0).astype(F32)
        for c in range(12):
            lanes = slice(128 * c, 128 * c + 128)
            y = _conv_group(u_ref[:, lanes], halo_ref[:, lanes] * keep, w_ref, c)
            sv = y * _sigmoid(y)
            if c < 8:
                rs = lax.rsqrt(jnp.sum(sv * sv, axis=1, keepdims=True) + EPS)
                n = sv * rs
                if c < 4:
                    q_ref[:, lanes] = n * (DN_HD ** -0.5)
                else:
                    k_ref[:, slice(128 * (c - 4), 128 * (c - 3))] = n
            else:
                v_ref[:, slice(128 * (c - 8), 128 * (c - 7))] = sv
        bg_ref[...] = _beta_g(ba_ref[...], al_ref[...], dt_ref[...])[0]

    return pl.pallas_call(
        body, grid=(s // tr,),
        in_specs=[pl.BlockSpec((tr, 1536), lambda i: (i, 0)),
                  pl.BlockSpec((8, 1536), lambda i: (jnp.maximum(i * nh - 1, 0), 0)),
                  pl.BlockSpec((tr, 128), lambda i: (i, COL_BA_128)),
                  pl.BlockSpec((8, 1536), lambda i: (0, 0)),
                  pl.BlockSpec((1, 128), lambda i: (0, 0)),
                  pl.BlockSpec((1, 128), lambda i: (0, 0))],
        out_specs=[pl.BlockSpec((tr, 512), lambda i: (i, 0))] * 3 + [pl.BlockSpec((tr, 128), lambda i: (i, 0))],
        out_shape=[_sds((s, 512))] * 3 + [_sds((s, 128))],
        compiler_params=_cp(("parallel",)), name="dn_pre")(proj, proj, proj, conv_w8, alog_l, dtb_l)


CPS = 4
CPS_SCAN = 8


def _split3(a):
    hi = a.astype(BF16)
    r1 = a - hi.astype(F32)
    mid = r1.astype(BF16)
    return hi, mid, (r1 - mid.astype(F32)).astype(BF16)


def _lane_select():
    r = np.arange(128)
    return jnp.asarray((r[None, :, None] == np.arange(8)[:, None, None]) * np.ones((1, 1, 128)), dtype=BF16)


def _lane_bcast(a3, sel):
    return _dot(a3[0], sel) + _dot(a3[1], sel) + _dot(a3[2], sel)


def _rowsum_b(z, ones_b):
    hi, lo = _split(z)
    return _dot(hi, ones_b) + _dot(lo, ones_b)


def _chunk_cumsum(bg, bgt, tri):
    return _split3(bg), _split3(_mmx(tri, bg)), _mmx_nt(bgt, tri)


def _chunk_common(bg3, gc3, gc_row, h, sel_ref):
    gcc = _lane_bcast(gc3, sel_ref[DN_HEADS + h])
    beta = _lane_bcast(bg3, sel_ref[h])
    gcr = gc_row[DN_HEADS + h:DN_HEADS + h + 1, :]
    ii = lax.broadcasted_iota(jnp.int32, (CHUNK, CHUNK), 0)
    jj = lax.broadcasted_iota(jnp.int32, (CHUNK, CHUNK), 1)
    incl = ii >= jj
    strict = ii > jj
    decay = jnp.exp(jnp.where(incl, gcc[:, 0:CHUNK] - gcr, NEG))
    gl = gcc[CHUNK - 1:CHUNK, :]
    return gcc, beta, incl, strict, decay, gl


def _dn_prep(qn, kn, v, bg, bgt, tri, sel):
    s = qn.shape[0]
    nc = s // CHUNK

    def body(q_ref, k_ref, v_ref, bg_ref, bgt_ref, tri_ref, sel_ref,
             u_ref, w_ref, qd_ref, kt_ref, attn_ref, t_ref, gl_ref):
        tri_v = tri_ref[...]
        ii = lax.broadcasted_iota(jnp.int32, (CHUNK, CHUNK), 0)
        jj = lax.broadcasted_iota(jnp.int32, (CHUNK, CHUNK), 1)
        eye = (ii == jj).astype(F32)

        def chain(cc, h, bg3, gc3, gc_row):
            rows = slice(CHUNK * cc, CHUNK * cc + CHUNK)
            lanes = slice(128 * h, 128 * h + 128)
            gcc, beta, incl, strict, decay, gl = _chunk_common(bg3, gc3, gc_row, h, sel_ref)
            yield
            q = q_ref[rows, lanes]
            k = k_ref[rows, lanes]
            vv = v_ref[rows, lanes]
            kb = k * beta
            egc = jnp.exp(gcc)
            a_mat = jnp.where(strict, _mm_nt(kb, k) * decay, 0.0)
            attn_ref[cc, h] = jnp.where(incl, _mm_nt(q, k) * decay, 0.0)
            qd_ref[rows, lanes] = q * egc
            kt_ref[rows, lanes] = k * jnp.exp(gl - gcc)
            gl_ref[cc, h] = jnp.exp(gl)
            yield
            p = -a_mat
            t = eye + p
            for _ in range(5):
                ps = _split(p)
                p = _mm3(ps, ps)
                yield
                t = t + _mm3(_split(t), _split(p))
                yield
            t_ref[cc, h] = t
            ts = _split(t)
            u_ref[rows, lanes] = _mm3(ts, _split(vv * beta))
            w_ref[rows, lanes] = _mm3(ts, _split(kb * egc))

        gens = []
        for cc in range(CPS):
            bgv = bg_ref[CHUNK * cc:CHUNK * cc + CHUNK, :]
            bg3, gc3, gc_row = _chunk_cumsum(bgv, bgt_ref[cc], tri_v)
            gens += [chain(cc, h, bg3, gc3, gc_row) for h in range(DN_HEADS)]
        _interleave(gens)

    rows_step = CPS * CHUNK
    big = pl.BlockSpec((rows_step, 512), lambda n: (n, 0))
    sq = pl.BlockSpec((CPS, DN_HEADS, CHUNK, CHUNK), lambda n: (n, 0, 0, 0))
    return pl.pallas_call(
        body, grid=(nc // CPS,),
        in_specs=[big, big, big, pl.BlockSpec((rows_step, 128), lambda n: (n, 0)),
                  pl.BlockSpec((CPS, 8, CHUNK), lambda n: (n, 0, 0)),
                  pl.BlockSpec((CHUNK, CHUNK), lambda n: (0, 0)),
                  pl.BlockSpec((8, 128, 128), lambda n: (0, 0, 0))],
        out_specs=[big, big, big, big, sq, sq, pl.BlockSpec((CPS, DN_HEADS, 1, 128), lambda n: (n, 0, 0, 0))],
        out_shape=[_sds((s, 512))] * 4 + [_sds((nc, DN_HEADS, CHUNK, CHUNK))] * 2 + [_sds((nc, DN_HEADS, 1, 128))],
        compiler_params=_cp(("parallel",)), name="dn_prep")(qn, kn, v, bg, bgt, tri, sel)


def _dn_scan(u, w, qd, kt, attn, gl):
    s = u.shape[0]
    nc = s // CHUNK

    def body(u_ref, w_ref, qd_ref, kt_ref, attn_ref, gl_ref, o_ref, vn_ref, sp_ref, st_ref):
        n = pl.program_id(0)

        @pl.when(n == 0)
        def _():
            st_ref[...] = jnp.zeros_like(st_ref)

        def chain(cc, h):
            rows = slice(CHUNK * cc, CHUNK * cc + CHUNK)
            lanes = slice(128 * h, 128 * h + 128)
            st = st_ref[h]
            sp_ref[cc, h] = st
            stb = st.astype(BF16)
            ws = _dot(w_ref[rows, lanes].astype(BF16), stb)
            qs = _dot(qd_ref[rows, lanes].astype(BF16), stb)
            yield
            vn = u_ref[rows, lanes] - ws
            vn_ref[rows, lanes] = vn
            vnb = vn.astype(BF16)
            o_ref[rows, lanes] = qs + _dot(attn_ref[cc, h].astype(BF16), vnb)
            st_ref[h] = st * gl_ref[cc, h] + _dot_tn(kt_ref[rows, lanes].astype(BF16), vnb)

        for cc in range(CPS_SCAN):
            _interleave([chain(cc, h) for h in range(DN_HEADS)])

    big = pl.BlockSpec((CPS_SCAN * CHUNK, 512), lambda n: (n, 0))
    return pl.pallas_call(
        body, grid=(nc // CPS_SCAN,),
        in_specs=[big, big, big, big,
                  pl.BlockSpec((CPS_SCAN, DN_HEADS, CHUNK, CHUNK), lambda n: (n, 0, 0, 0)),
                  pl.BlockSpec((CPS_SCAN, DN_HEADS, 1, 128), lambda n: (n, 0, 0, 0))],
        out_specs=[big, big, pl.BlockSpec((CPS_SCAN, DN_HEADS, DN_HD, DN_HD), lambda n: (n, 0, 0, 0))],
        out_shape=[_sds((s, 512)), _sds((s, 512)), _sds((nc, DN_HEADS, DN_HD, DN_HD))],
        scratch_shapes=[pltpu.VMEM((DN_HEADS, DN_HD, DN_HD), F32)],
        compiler_params=_cp(("arbitrary",)), name="dn_scan")(u, w, qd, kt, attn, gl)


R4 = PATTERNS[1][1]
R16 = PATTERNS[2][1]
TM = 512


def _pattern_spec(r, width=512):
    return pl.BlockSpec((r, TM // r, width), lambda i: (0, i, 0))


def _pattern_shape(s, r, dtype=F32, width=512):
    return _sds((r, s // r, width), dtype)


SLABS = pltpu.VMEM((4, TM, 128), F32)

HEAD_SLOT = 128 // ATT_HEADS


def _head_expand():
    src = np.arange(128)[:, None]
    dst = np.arange(512)[None, :]
    return jnp.asarray((src == (dst // ATT_HD) * HEAD_SLOT).astype(np.float32), dtype=BF16)


def _head_compact():
    src = np.arange(512)[:, None]
    dst = np.arange(128)[None, :]
    return jnp.asarray((src // ATT_HD == dst // HEAD_SLOT).astype(np.float32), dtype=BF16)


def _to_patterns(val, dsts, scr):
    for c in range(val.shape[1] // 128):
        lanes = slice(128 * c, 128 * c + 128)
        scr[c] = val[:, lanes]
        for dst_ref, r in dsts:
            for a in range(r):
                dst_ref[a, :, lanes] = scr[c, pl.ds(a, TM // r, stride=r), :].astype(dst_ref.dtype)


def _from_pattern(src_ref, r, scr):
    n_slab = src_ref.shape[2] // 128
    for c in range(n_slab):
        for a in range(r):
            scr[c, pl.ds(a, TM // r, stride=r), :] = src_ref[a, :, 128 * c:128 * c + 128]
    return jnp.concatenate([scr[c] for c in range(n_slab)], axis=1) if n_slab > 1 else scr[0]


def _att_pre(proj, qw_t, kw_t, bd64):
    s = proj.shape[0]

    def body(q_ref, k_ref, v_ref, qw_ref, kw_ref, bd_ref,
             q1_ref, k1_ref, v1_ref, q4_ref, k4_ref, v4_ref, q16_ref, k16_ref, v16_ref, scr):
        bd = bd_ref[...]
        q = q_ref[...]
        k = k_ref[...]
        qn = q * _seg_rstd(q, bd, ATT_HD) * qw_ref[...] * (ATT_HD ** -0.5)
        kn = k * _seg_rstd(k, bd, ATT_HD) * kw_ref[...]
        q1_ref[...] = qn.astype(BF16)
        k1_ref[...] = kn.astype(BF16)
        v1_ref[...] = v_ref[...].astype(BF16)
        _to_patterns(qn, ((q4_ref, R4), (q16_ref, R16)), scr)
        _to_patterns(kn, ((k4_ref, R4), (k16_ref, R16)), scr)
        _to_patterns(v_ref[...], ((v4_ref, R4), (v16_ref, R16)), scr)

    row = pl.BlockSpec((1, 512), lambda i: (0, 0))
    tok = pl.BlockSpec((TM, 512), lambda i: (i, 0))
    return pl.pallas_call(
        body, grid=(s // TM,),
        in_specs=[pl.BlockSpec((TM, 512), lambda i: (i, COL_ATT_Q)),
                  pl.BlockSpec((TM, 512), lambda i: (i, COL_ATT_K)),
                  pl.BlockSpec((TM, 512), lambda i: (i, COL_ATT_V)),
                  row, row, pl.BlockSpec((512, 128), lambda i: (0, 0))],
        out_specs=[tok] * 3 + [_pattern_spec(R4)] * 3 + [_pattern_spec(R16)] * 3,
        out_shape=[_sds((s, 512), BF16)] * 3 + [_pattern_shape(s, R4, BF16)] * 3 + [_pattern_shape(s, R16, BF16)] * 3,
        scratch_shapes=[SLABS],
        compiler_params=_cp(("parallel",)), name="att_pre")(proj, proj, proj, qw_t, kw_t, bd64)


def _bias_fwd(rel_bias, buckets):
    def body(rb_ref, bk_ref, o_ref):
        for p in range(len(PATTERNS)):
            bk = bk_ref[p]
            for h in range(ATT_HEADS):
                acc = jnp.zeros((BLK, 2 * BLK), F32)
                for b in range(N_BUCKETS):
                    acc = jnp.where(bk == b, rb_ref[h, b], acc)
                o_ref[p, h] = acc

    return pl.pallas_call(
        body,
        in_specs=[pl.BlockSpec(memory_space=pltpu.SMEM), pl.BlockSpec(memory_space=pltpu.VMEM)],
        out_specs=pl.BlockSpec(memory_space=pltpu.VMEM),
        out_shape=_sds((len(PATTERNS), ATT_HEADS, BLK, 2 * BLK)),
        compiler_params=_cp(), name="bias_fwd")(rel_bias, buckets)


def _bias_bwd(ds_accs, buckets):
    def body(ds0_ref, ds1_ref, ds2_ref, bk_ref, o_ref):
        for h in range(ATT_HEADS):
            for b in range(N_BUCKETS):
                tot = jnp.zeros((), F32)
                for p, ds_ref in enumerate((ds0_ref, ds1_ref, ds2_ref)):
                    tot = tot + jnp.sum(jnp.where(bk_ref[p] == b, ds_ref[h], 0.0))
                o_ref[h, b] = tot

    return pl.pallas_call(
        body,
        in_specs=[pl.BlockSpec(memory_space=pltpu.VMEM)] * 4,
        out_specs=pl.BlockSpec(memory_space=pltpu.SMEM),
        out_shape=_sds((ATT_HEADS, N_BUCKETS)),
        compiler_params=_cp(), name="bias_bwd")(*ds_accs, buckets)


QB_FWD = 2
QB_BWD = 4


def _att_masks(has_prev):
    qi = lax.broadcasted_iota(jnp.int32, (BLK, BLK), 0)
    kj = lax.broadcasted_iota(jnp.int32, (BLK, BLK), 1)
    lane = lax.broadcasted_iota(jnp.int32, (BLK, 2 * ATT_HD), 1)
    return jnp.logical_and(kj >= qi, has_prev), kj <= qi, lane < ATT_HD


def _head_lanes(h):
    half = h % 2
    return slice(ATT_HD * h, ATT_HD * h + ATT_HD), slice(ATT_HD * half, ATT_HD * half + ATT_HD)


def _att_scores(qm, kp2, kc2, bias_h, mask_prev, mask_cur):
    s_prev = jnp.where(mask_prev, _dot_nt(qm, kp2) + bias_h[:, :BLK], NEG)
    s_cur = jnp.where(mask_cur, _dot_nt(qm, kc2) + bias_h[:, BLK:], NEG)
    return s_prev, s_cur


def _att_fwd(q, k, v, v_col, bias, p_idx, r, name):
    QB = QB_FWD
    s = q.shape[0]
    nblk = s // BLK
    nseq = nblk // r

    def body(q_ref, kp_ref, kc_ref, vp_ref, vc_ref, b_ref, o_ref, lse_ref):
        j = pl.program_id(0)

        def head(h, rows, masks, q2, kp2, kc2, vp2, vc2):
            mask_prev, mask_cur, lo_half = masks
            out_l, pair_l = _head_lanes(h)
            sel = lo_half if h % 2 == 0 else jnp.logical_not(lo_half)
            qm = jnp.where(sel, q2, jnp.zeros_like(q2))
            s_prev, s_cur = _att_scores(qm, kp2, kc2, b_ref[0, h], mask_prev, mask_cur)
            yield
            m = jnp.maximum(jnp.max(s_prev, axis=1, keepdims=True), jnp.max(s_cur, axis=1, keepdims=True))
            p_prev = jnp.exp(s_prev - m)
            p_cur = jnp.exp(s_cur - m)
            l = jnp.sum(p_prev, axis=1, keepdims=True) + jnp.sum(p_cur, axis=1, keepdims=True)
            yield
            o2 = _dot(p_prev.astype(BF16), vp2) + _dot(p_cur.astype(BF16), vc2)
            o_ref[rows, out_l] = (o2 * (1.0 / l))[:, pair_l]
            lse_ref[rows, HEAD_SLOT * h:HEAD_SLOT * h + HEAD_SLOT] = jnp.broadcast_to(m + jnp.log(l), (BLK, HEAD_SLOT))

        for sub in range(QB):
            rows = slice(BLK * sub, BLK * sub + BLK)
            before = slice(BLK * (sub - 1), BLK * sub)
            masks = _att_masks(((QB * j + sub) % nseq) != 0)
            gens = []
            for pp in range(ATT_HEADS // 2):
                lanes = slice(128 * pp, 128 * pp + 128)
                kp = kp_ref[:, lanes] if sub == 0 else kc_ref[before, lanes]
                vp = vp_ref[:, lanes] if sub == 0 else vc_ref[before, lanes]
                slabs = (q_ref[rows, lanes], kp, kc_ref[rows, lanes], vp, vc_ref[rows, lanes])
                gens += [head(2 * pp, rows, masks, *slabs), head(2 * pp + 1, rows, masks, *slabs)]
            _interleave(gens)

    cur = pl.BlockSpec((QB * BLK, 512), lambda j: (j, 0))
    prev = pl.BlockSpec((BLK, 512), lambda j: (jnp.maximum(QB * j - 1, 0), 0))
    vcur = pl.BlockSpec((QB * BLK, 512), lambda j: (j, v_col))
    vprev = pl.BlockSpec((BLK, 512), lambda j: (jnp.maximum(QB * j - 1, 0), v_col))
    return pl.pallas_call(
        body, grid=(nblk // QB,),
        in_specs=[cur, prev, cur, vprev, vcur,
                  pl.BlockSpec((1, ATT_HEADS, BLK, 2 * BLK), lambda j: (p_idx, 0, 0, 0))],
        out_specs=[cur, pl.BlockSpec((QB * BLK, 128), lambda j: (j, 0))],
        out_shape=[_sds((s, 512)), _sds((s, 128))],
        compiler_params=_cp(("parallel",)), name=name)(q, k, k, v, v, bias)


def _post_fwd(o_dn, proj, o_pats, lse_pats, dnw_t, bd128):
    s = o_dn.shape[0]

    def body(o_ref, z_ref, gate_ref, o1_ref, o4_ref, o16_ref, s1_ref, s4_ref, s16_ref, wn_ref, bd_ref, ex_ref,
             mixed_ref, oatt_ref, l1_ref, l4_ref, l16_ref, scr_a, scr_b, scr_c, scr_d):
        o = o_ref[...]
        z = z_ref[...]
        rstd = _seg_rstd(o, bd_ref[...], DN_HD)
        y_dn = o * rstd * wn_ref[...] * (z * _sigmoid(z))
        mixed_ref[:, 0:512] = y_dn.astype(BF16)
        lses = (s1_ref[...], _from_pattern(s4_ref, R4, scr_a), _from_pattern(s16_ref, R16, scr_b))
        m = jnp.maximum(jnp.maximum(lses[0], lses[1]), lses[2])
        tot = jnp.exp(lses[0] - m) + jnp.exp(lses[1] - m) + jnp.exp(lses[2] - m)
        big_l = m + jnp.log(tot)
        l1_ref[...] = big_l
        _to_patterns(big_l, ((l4_ref, R4), (l16_ref, R16)), scr_a)
        ex = ex_ref[...]
        outs = (o1_ref[...], _from_pattern(o4_ref, R4, scr_c), _from_pattern(o16_ref, R16, scr_d))
        acc = jnp.zeros_like(o)
        for lse_p, o_p in zip(lses, outs):
            acc = acc + _lane_bcast(_split3(jnp.exp(lse_p - big_l)), ex) * o_p
        gate = gate_ref[...]
        oatt_ref[...] = acc
        mixed_ref[:, 512:1024] = (acc * (gate * _sigmoid(gate))).astype(BF16)

    blk = pl.BlockSpec((TM, 512), lambda i: (i, 0))
    cblk = pl.BlockSpec((TM, 128), lambda i: (i, 0))
    p4, p16 = _pattern_spec(R4), _pattern_spec(R16)
    c4, c16 = _pattern_spec(R4, 128), _pattern_spec(R16, 128)
    return pl.pallas_call(
        body, grid=(s // TM,),
        in_specs=[blk, pl.BlockSpec((TM, 512), lambda i: (i, COL_Z)),
                  pl.BlockSpec((TM, 512), lambda i: (i, COL_GATE)), blk, p4, p16, cblk, c4, c16,
                  pl.BlockSpec((1, 512), lambda i: (0, 0)), pl.BlockSpec((512, 128), lambda i: (0, 0)),
                  pl.BlockSpec((128, 512), lambda i: (0, 0))],
        out_specs=[pl.BlockSpec((TM, D_MODEL), lambda i: (i, 0)), blk, cblk, c4, c16],
        out_shape=[_sds((s, D_MODEL), BF16), _sds((s, 512)), _sds((s, 128)), _pattern_shape(s, R4, F32, 128),
                   _pattern_shape(s, R16, F32, 128)],
        scratch_shapes=[SLABS] * 4,
        compiler_params=_cp(("parallel",)), name="post_fwd")(o_dn, proj, proj, *o_pats, *lse_pats, dnw_t, bd128,
                                                              _head_expand())


def _out_fwd(x, mixed, w_out_bf, tgt):
    s = x.shape[0]
    tm = 512

    def body(x_ref, m_ref, w_ref, t_ref, dy_ref, loss_ref):
        i = pl.program_id(0)

        @pl.when(i == 0)
        def _():
            loss_ref[...] = jnp.zeros_like(loss_ref)

        y = x_ref[...] + jnp.dot(m_ref[...], w_ref[...], preferred_element_type=F32)
        err = y - t_ref[...]
        dy_ref[...] = err * (1.0 / D_MODEL)
        part = 0.5 * jnp.sum(jnp.mean(err * err, axis=-1, keepdims=True), axis=0, keepdims=True)
        loss_ref[...] = loss_ref[...] + part

    blk = pl.BlockSpec((tm, D_MODEL), lambda i: (i, 0))
    return pl.pallas_call(
        body, grid=(s // tm,),
        in_specs=[blk, blk, pl.BlockSpec((D_MODEL, D_MODEL), lambda i: (0, 0)), blk],
        out_specs=[blk, pl.BlockSpec((8, 128), lambda i: (0, 0))],
        out_shape=[_sds((s, D_MODEL)), _sds((8, 128))],
        compiler_params=_cp(("arbitrary",)), name="out_fwd")(x, mixed, w_out_bf, tgt)


def _out_bwd(dy, mixed, w_out_bf):
    s = dy.shape[0]
    tm = 512

    def body(dy_ref, m_ref, w_ref, dm_ref, dw_ref):
        i = pl.program_id(0)

        @pl.when(i == 0)
        def _():
            dw_ref[...] = jnp.zeros_like(dw_ref)

        dyb = dy_ref[...].astype(BF16)
        dm_ref[...] = lax.dot_general(dyb, w_ref[...], (((1,), (1,)), ((), ())), preferred_element_type=F32)
        dw_ref[...] = dw_ref[...] + lax.dot_general(m_ref[...], dyb, (((0,), (0,)), ((), ())),
                                                    preferred_element_type=F32)

    blk = pl.BlockSpec((tm, D_MODEL), lambda i: (i, 0))
    full = pl.BlockSpec((D_MODEL, D_MODEL), lambda i: (0, 0))
    return pl.pallas_call(
        body, grid=(s // tm,), in_specs=[blk, blk, full], out_specs=[blk, full],
        out_shape=[_sds((s, D_MODEL)), _sds((D_MODEL, D_MODEL))],
        compiler_params=_cp(("arbitrary",)), name="out_bwd")(dy, mixed, w_out_bf)


def _post_bwd(dmixed, o_dn, proj, o_att, dnw_t, bd128):
    s = o_dn.shape[0]
    tm = TM

    def body(ddn_ref, datt_ref, o_ref, z_ref, gate_ref, oatt_ref, wn_ref, bd128_ref, hc_ref,
             do_ref, dz_ref, dgate_ref, doatt_ref, do4_ref, do16_ref, delta_ref, dl4_ref, dl16_ref, dnw_ref, scr):
        i = pl.program_id(0)

        @pl.when(i == 0)
        def _():
            dnw_ref[...] = jnp.zeros_like(dnw_ref)

        bd128v = bd128_ref[...]
        o = o_ref[...]
        z = z_ref[...]
        wn = wn_ref[...]
        dy = ddn_ref[...]
        rstd = _seg_rstd(o, bd128v, DN_HD)
        nrm = o * rstd
        sz = _sigmoid(z)
        dz_ref[...] = (dy * nrm * wn * _silu_grad(z, sz)).astype(BF16)
        dn = dy * z * sz
        gw = dn * wn
        do_ref[...] = rstd * (gw - nrm * _seg_expand(_seg_mean(gw * nrm, bd128v, DN_HD), bd128v, DN_HD))
        colsum = jnp.sum(dn * nrm, axis=0, keepdims=True)
        fold = colsum[:, 0:128] + colsum[:, 128:256] + colsum[:, 256:384] + colsum[:, 384:512]
        dnw_ref[...] = dnw_ref[...] + fold
        dya = datt_ref[...]
        gate = gate_ref[...]
        oatt = oatt_ref[...]
        sg = _sigmoid(gate)
        dgate_ref[...] = (dya * oatt * _silu_grad(gate, sg)).astype(BF16)
        doa = dya * gate * sg
        doatt_ref[...] = doa.astype(BF16)
        delta = _segsum(doa * oatt, hc_ref[...])
        delta_ref[...] = delta
        _to_patterns(doa, ((do4_ref, R4), (do16_ref, R16)), scr)
        _to_patterns(delta, ((dl4_ref, R4), (dl16_ref, R16)), scr)

    blk = pl.BlockSpec((tm, 512), lambda i: (i, 0))
    cblk = pl.BlockSpec((tm, 128), lambda i: (i, 0))
    p4, p16 = _pattern_spec(R4), _pattern_spec(R16)
    c4, c16 = _pattern_spec(R4, 128), _pattern_spec(R16, 128)
    return pl.pallas_call(
        body, grid=(s // tm,),
        in_specs=[blk, pl.BlockSpec((tm, 512), lambda i: (i, 1)), blk,
                  pl.BlockSpec((tm, 512), lambda i: (i, COL_Z)), pl.BlockSpec((tm, 512), lambda i: (i, COL_GATE)),
                  blk, pl.BlockSpec((1, 512), lambda i: (0, 0)), pl.BlockSpec((512, 128), lambda i: (0, 0)),
                  pl.BlockSpec((512, 128), lambda i: (0, 0))],
        out_specs=[blk, blk, blk, blk, p4, p16, cblk, c4, c16, pl.BlockSpec((8, 128), lambda i: (0, 0))],
        out_shape=[_sds((s, 512))] + [_sds((s, 512), BF16)] * 3 + [_pattern_shape(s, R4, BF16),
                                          _pattern_shape(s, R16, BF16), _sds((s, 128)),
                                          _pattern_shape(s, R4, F32, 128), _pattern_shape(s, R16, F32, 128),
                                          _sds((8, 128))],
        scratch_shapes=[SLABS],
        compiler_params=_cp(("arbitrary",)), name="post_bwd")(dmixed, dmixed, o_dn, proj, proj, o_att, dnw_t,
                                                               bd128, _head_compact())


def _att_bwd(q, k, v, v_col, do, big_l, delta, bias, p_idx, r, name):
    QB = QB_BWD
    s = q.shape[0]
    nblk = s // BLK
    nseq = nblk // r
    nstep = nblk // QB

    def body(q_ref, kp_ref, kc_ref, vp_ref, vc_ref, do_ref, l_ref, dl_ref, b_ref,
             dq_ref, dk_ref, dv_ref, ds_ref, dkc_ref, dvc_ref):
        j = pl.program_id(0)

        @pl.when(j == 0)
        def _():
            dkc_ref[...] = jnp.zeros_like(dkc_ref)
            dvc_ref[...] = jnp.zeros_like(dvc_ref)
            ds_ref[...] = jnp.zeros_like(ds_ref)

        @pl.when(j < nstep)
        def _():
            def head(h, sub, masks, q2, do2, kp2, kc2, vp2, vc2):
                mask_prev, mask_cur, lo_half = masks
                rows = slice(BLK * sub, BLK * sub + BLK)
                out_l, pair_l = _head_lanes(h)
                sel = lo_half if h % 2 == 0 else jnp.logical_not(lo_half)
                qm = jnp.where(sel, q2, jnp.zeros_like(q2))
                dom = jnp.where(sel, do2, jnp.zeros_like(do2))
                s_prev, s_cur = _att_scores(qm, kp2, kc2, b_ref[0, h], mask_prev, mask_cur)
                dp_prev = _dot_nt(dom, vp2)
                dp_cur = _dot_nt(dom, vc2)
                yield
                lh = l_ref[rows, HEAD_SLOT * h:HEAD_SLOT * h + 1]
                dh = dl_ref[rows, HEAD_SLOT * h:HEAD_SLOT * h + 1]
                p_prev = jnp.exp(s_prev - lh)
                p_cur = jnp.exp(s_cur - lh)
                ds_prev = p_prev * (dp_prev - dh)
                ds_cur = p_cur * (dp_cur - dh)
                ds_ref[h, :, 0:BLK] = ds_ref[h, :, 0:BLK] + ds_prev
                ds_ref[h, :, BLK:2 * BLK] = ds_ref[h, :, BLK:2 * BLK] + ds_cur
                dsb_prev, dsb_cur = ds_prev.astype(BF16), ds_cur.astype(BF16)
                pb_prev, pb_cur = p_prev.astype(BF16), p_cur.astype(BF16)
                yield
                dq_ref[rows, out_l] = (_dot(dsb_prev, kp2) + _dot(dsb_cur, kc2))[:, pair_l]
                dk_prev = _dot_tn(dsb_prev, q2)[:, pair_l]
                dv_prev = _dot_tn(pb_prev, do2)[:, pair_l]
                if sub == 0:
                    last = slice(BLK * (QB - 1), BLK * QB)
                    dk_ref[last, out_l] = dkc_ref[last, out_l] + dk_prev
                    dv_ref[last, out_l] = dvc_ref[last, out_l] + dv_prev
                else:
                    before = slice(BLK * (sub - 1), BLK * sub)
                    dkc_ref[before, out_l] = dkc_ref[before, out_l] + dk_prev
                    dvc_ref[before, out_l] = dvc_ref[before, out_l] + dv_prev
                yield
                dkc_ref[rows, out_l] = _dot_tn(dsb_cur, q2)[:, pair_l]
                dvc_ref[rows, out_l] = _dot_tn(pb_cur, do2)[:, pair_l]

            done = slice(0, BLK * (QB - 1))
            dk_ref[done, :] = dkc_ref[done, :]
            dv_ref[done, :] = dvc_ref[done, :]
            for sub in range(QB):
                rows = slice(BLK * sub, BLK * sub + BLK)
                before = slice(BLK * (sub - 1), BLK * sub)
                masks = _att_masks(((QB * j + sub) % nseq) != 0)
                gens = []
                for pp in range(ATT_HEADS // 2):
                    lanes = slice(128 * pp, 128 * pp + 128)
                    kp = kp_ref[:, lanes] if sub == 0 else kc_ref[before, lanes]
                    vp = vp_ref[:, lanes] if sub == 0 else vc_ref[before, lanes]
                    slabs = (q_ref[rows, lanes], do_ref[rows, lanes], kp, kc_ref[rows, lanes], vp,
                             vc_ref[rows, lanes])
                    gens += [head(2 * pp, sub, masks, *slabs), head(2 * pp + 1, sub, masks, *slabs)]
                _interleave(gens)

        @pl.when(j == nstep)
        def _():
            dk_ref[...] = dkc_ref[...]
            dv_ref[...] = dvc_ref[...]

    last_step = nstep - 1
    cur = pl.BlockSpec((QB * BLK, 512), lambda j: (jnp.minimum(j, last_step), 0))
    compact = pl.BlockSpec((QB * BLK, 128), lambda j: (jnp.minimum(j, last_step), 0))
    lag = pl.BlockSpec((QB * BLK, 512), lambda j: (jnp.clip(j - 1, 0, last_step), 0))
    prev = pl.BlockSpec((BLK, 512), lambda j: (jnp.clip(QB * j - 1, 0, nblk - 1), 0))
    vcur = pl.BlockSpec((QB * BLK, 512), lambda j: (jnp.minimum(j, last_step), v_col))
    vprev = pl.BlockSpec((BLK, 512), lambda j: (jnp.clip(QB * j - 1, 0, nblk - 1), v_col))
    return pl.pallas_call(
        body, grid=(nstep + 1,),
        in_specs=[cur, prev, cur, vprev, vcur, cur, compact, compact,
                  pl.BlockSpec((1, ATT_HEADS, BLK, 2 * BLK), lambda j: (p_idx, 0, 0, 0))],
        out_specs=[cur, lag, lag, pl.BlockSpec((ATT_HEADS, BLK, 2 * BLK), lambda j: (0, 0, 0))],
        out_shape=[_sds((s, 512))] * 3 + [_sds((ATT_HEADS, BLK, 2 * BLK))],
        scratch_shapes=[pltpu.VMEM((QB * BLK, 512), F32), pltpu.VMEM((QB * BLK, 512), F32)],
        compiler_params=_cp(("arbitrary",)), name=name)(q, k, k, v, v, do, big_l, delta, bias)


def _att_pre_bwd(dq_pats, dk_pats, dv_pats, proj, qw_t, kw_t, bd64):
    s = proj.shape[0]
    tm = TM

    def body(dq1_ref, dq4_ref, dq16_ref, dk1_ref, dk4_ref, dk16_ref, dv1_ref, dv4_ref, dv16_ref,
             q_ref, k_ref, qw_ref, kw_ref, bd_ref,
             dqr_ref, dkr_ref, dvr_ref, dqw_ref, dkw_ref, scr4, scr16):
        i = pl.program_id(0)

        @pl.when(i == 0)
        def _():
            dqw_ref[...] = jnp.zeros_like(dqw_ref)
            dkw_ref[...] = jnp.zeros_like(dkw_ref)

        bd = bd_ref[...]

        def total(d1_ref, d4_ref, d16_ref):
            return d1_ref[...] + _from_pattern(d4_ref, R4, scr4) + _from_pattern(d16_ref, R16, scr16)

        def one(d_refs, x_ref, w_ref, scale, dx_ref, dw_ref):
            dy = total(*d_refs) * scale
            x = x_ref[...]
            rstd = _seg_rstd(x, bd, ATT_HD)
            nrm = x * rstd
            dw_ref[...] = dw_ref[...] + jnp.sum(dy * nrm, axis=0, keepdims=True)
            g = dy * w_ref[...]
            dx_ref[...] = (rstd * (g - nrm * _seg_expand(_seg_mean(g * nrm, bd, ATT_HD), bd, ATT_HD))).astype(BF16)

        one((dq1_ref, dq4_ref, dq16_ref), q_ref, qw_ref, ATT_HD ** -0.5, dqr_ref, dqw_ref)
        one((dk1_ref, dk4_ref, dk16_ref), k_ref, kw_ref, 1.0, dkr_ref, dkw_ref)
        dvr_ref[...] = total(dv1_ref, dv4_ref, dv16_ref).astype(BF16)

    blk = pl.BlockSpec((tm, 512), lambda i: (i, 0))
    pats = [blk, _pattern_spec(R4), _pattern_spec(R16)]
    row = pl.BlockSpec((1, 512), lambda i: (0, 0))
    acc = pl.BlockSpec((8, 512), lambda i: (0, 0))
    return pl.pallas_call(
        body, grid=(s // tm,),
        in_specs=pats * 3 + [pl.BlockSpec((tm, 512), lambda i: (i, COL_ATT_Q)),
                             pl.BlockSpec((tm, 512), lambda i: (i, COL_ATT_K)), row, row,
                             pl.BlockSpec((512, 128), lambda i: (0, 0))],
        out_specs=[blk, blk, blk, acc, acc],
        out_shape=[_sds((s, 512), BF16)] * 3 + [_sds((8, 512))] * 2,
        scratch_shapes=[SLABS] * 2,
        compiler_params=_cp(("arbitrary",)), name="att_pre_bwd")(*dq_pats, *dk_pats, *dv_pats, proj, proj,
                                                                  qw_t, kw_t, bd64)


def _dn_scan_bwd(do, sp, qd, kt, w, vn, attn, gl):
    s = do.shape[0]
    nc = s // CHUNK

    def body(do_ref, sp_ref, qd_ref, kt_ref, w_ref, vn_ref, attn_ref, gl_ref,
             du_ref, dqd_ref, dkt_ref, dw_ref, dattn_ref, dgl_ref, ds_ref):
        n = pl.program_id(0)

        @pl.when(n == 0)
        def _():
            ds_ref[...] = jnp.zeros_like(ds_ref)

        def chain(cc, h):
            rows = slice(CHUNK * cc, CHUNK * cc + CHUNK)
            lanes = slice(128 * h, 128 * h + 128)
            dsn = ds_ref[h]
            st = sp_ref[cc, h]
            dsb, stb = dsn.astype(BF16), st.astype(BF16)
            dob = do_ref[rows, lanes].astype(BF16)
            vnb = vn_ref[rows, lanes].astype(BF16)
            dvn = _dot_tn(attn_ref[cc, h].astype(BF16), dob) + _dot(kt_ref[rows, lanes].astype(BF16), dsb)
            du_ref[rows, lanes] = dvn
            dqd_ref[rows, lanes] = _dot_nt(dob, stb)
            dattn_ref[cc, h] = _dot_nt(dob, vnb)
            dkt_ref[rows, lanes] = _dot_nt(vnb, dsb)
            tot = jnp.sum(jnp.sum(st * dsn, axis=1, keepdims=True), axis=0, keepdims=True)
            dgl_ref[cc, h] = jnp.broadcast_to(tot, (1, 128))
            qdo = _dot_tn(qd_ref[rows, lanes].astype(BF16), dob)
            yield
            dvb = dvn.astype(BF16)
            dw_ref[rows, lanes] = -_dot_nt(dvb, stb)
            ds_ref[h] = qdo + dsn * gl_ref[cc, h] - _dot_tn(w_ref[rows, lanes].astype(BF16), dvb)

        for cc in reversed(range(CPS_SCAN)):
            _interleave([chain(cc, h) for h in range(DN_HEADS)])

    nsteps = nc // CPS_SCAN
    big = pl.BlockSpec((CPS_SCAN * CHUNK, 512), lambda n: (nsteps - 1 - n, 0))
    sq = pl.BlockSpec((CPS_SCAN, DN_HEADS, CHUNK, CHUNK), lambda n: (nsteps - 1 - n, 0, 0, 0))
    glb = pl.BlockSpec((CPS_SCAN, DN_HEADS, 1, 128), lambda n: (nsteps - 1 - n, 0, 0, 0))
    return pl.pallas_call(
        body, grid=(nsteps,),
        in_specs=[big, pl.BlockSpec((CPS_SCAN, DN_HEADS, DN_HD, DN_HD), lambda n: (nsteps - 1 - n, 0, 0, 0)),
                  big, big, big, big, sq, glb],
        out_specs=[big, big, big, big, sq, glb],
        out_shape=[_sds((s, 512))] * 4 + [_sds((nc, DN_HEADS, CHUNK, CHUNK)), _sds((nc, DN_HEADS, 1, 128))],
        scratch_shapes=[pltpu.VMEM((DN_HEADS, DN_HD, DN_HD), F32)],
        compiler_params=_cp(("arbitrary",)), name="dn_scan_bwd")(do, sp, qd, kt, w, vn, attn, gl)


def _dn_prep_bwd(qn, kn, v, bg, bgt, tri, sel, t_inv, attn, u, w, du, dw, dqd, dkt, dattn, dgl):
    s = qn.shape[0]
    nc = s // CHUNK

    def body(q_ref, k_ref, v_ref, bg_ref, bgt_ref, tri_ref, sel_ref, t_ref, attn_ref, u_ref, w_ref,
             du_ref, dw_ref, dqd_ref, dkt_ref, dattn_ref, dgl_ref,
             dq_ref, dk_ref, dv_ref, dbg_ref):
        tri_v = tri_ref[...]
        lane = lax.broadcasted_iota(jnp.int32, (CHUNK, 128), 1)
        rowi = lax.broadcasted_iota(jnp.int32, (CHUNK, 128), 0)
        ones_b = jnp.ones((CHUNK, 128), BF16)
        ones_sq = jnp.ones((128, 128), BF16)
        parts = [[] for _ in range(CPS)]

        def chain(cc, h, bg3, gc3, gc_row):
            rows = slice(CHUNK * cc, CHUNK * cc + CHUNK)
            lanes = slice(128 * h, 128 * h + 128)
            gcc, beta, incl, strict, decay, gl = _chunk_common(bg3, gc3, gc_row, h, sel_ref)
            yield
            q = q_ref[rows, lanes]
            k = k_ref[rows, lanes]
            vv = v_ref[rows, lanes]
            ts = _split(t_ref[cc, h])
            egc = jnp.exp(gcc)
            kb = k * beta
            a_mat = jnp.where(strict, _mm_nt(kb, k) * decay, 0.0)
            dvb = _mm3_tn(ts, _split(du_ref[rows, lanes]))
            dkbg = _mm3_tn(ts, _split(dw_ref[rows, lanes]))
            yield
            d_a = jnp.where(strict, -(_mm_nt(dvb, u_ref[rows, lanes]) + _mm_nt(dkbg, w_ref[rows, lanes])), 0.0)
            d_m = d_a * decay
            dattn_m = jnp.where(incl, dattn_ref[cc, h], 0.0)
            dqk = dattn_m * decay
            e_hi, e_lo = _split(d_a * a_mat + dattn_m * attn_ref[cc, h])
            yield
            dkb = _mm(d_m, k)
            dk = _mm_tn(d_m, kb) + _mm_tn(dqk, q)
            dq = _mm(dqk, k)
            e_colsum = _dot_tn(e_hi, ones_b) + _dot_tn(e_lo, ones_b)
            e_rowsum = _dot(e_hi, ones_b) + _dot(e_lo, ones_b)
            dqd = dqd_ref[rows, lanes]
            dkt = dkt_ref[rows, lanes]
            s_dqd = _rowsum_b(dqd * q, ones_sq)
            s_dkt = _rowsum_b(dkt * k, ones_sq)
            rk = _rowsum_b(dkbg * k, ones_sq)
            s_dkb = _rowsum_b(dkb * k, ones_sq)
            s_dvb = _rowsum_b(dvb * vv, ones_sq)
            yield
            tail = jnp.exp(gl - gcc)
            r = s_dkt * tail
            dgl_tot = jnp.sum(r, axis=0, keepdims=True) + dgl_ref[cc, h] * jnp.exp(gl)
            dgc = e_rowsum - e_colsum + s_dqd * egc - r + rk * beta * egc
            dgc = dgc + jnp.where(rowi == CHUNK - 1, dgl_tot, 0.0)
            dq_ref[rows, lanes] = dq + dqd * egc
            dk_ref[rows, lanes] = dk + dkt * tail + dkbg * (beta * egc) + dkb * beta
            dv_ref[rows, lanes] = dvb * beta
            parts[cc].append((h, dgc, rk * egc + s_dkb + s_dvb))

        gens = []
        for cc in range(CPS):
            bgv = bg_ref[CHUNK * cc:CHUNK * cc + CHUNK, :]
            bg3, gc3, gc_row = _chunk_cumsum(bgv, bgt_ref[cc], tri_v)
            gens += [chain(cc, h, bg3, gc3, gc_row) for h in range(DN_HEADS)]
        _interleave(gens)
        for cc in range(CPS):
            dgc_mat = jnp.zeros((CHUNK, 128), F32)
            dbeta_mat = jnp.zeros((CHUNK, 128), F32)
            for h, dgc, dbeta in parts[cc]:
                dgc_mat = dgc_mat + jnp.where(lane == DN_HEADS + h, dgc, 0.0)
                dbeta_mat = dbeta_mat + jnp.where(lane == h, dbeta, 0.0)
            dbg_ref[CHUNK * cc:CHUNK * cc + CHUNK, :] = _mmx_tn(tri_v, dgc_mat) + dbeta_mat

    big = pl.BlockSpec((CPS * CHUNK, 512), lambda n: (n, 0))
    sq = pl.BlockSpec((CPS, DN_HEADS, CHUNK, CHUNK), lambda n: (n, 0, 0, 0))
    glb = pl.BlockSpec((CPS, DN_HEADS, 1, 128), lambda n: (n, 0, 0, 0))
    small = pl.BlockSpec((CPS * CHUNK, 128), lambda n: (n, 0))
    return pl.pallas_call(
        body, grid=(nc // CPS,),
        in_specs=[big, big, big, small, pl.BlockSpec((CPS, 8, CHUNK), lambda n: (n, 0, 0)),
                  pl.BlockSpec((CHUNK, CHUNK), lambda n: (0, 0)),
                  pl.BlockSpec((8, 128, 128), lambda n: (0, 0, 0)), sq, sq, big, big,
                  big, big, big, big, sq, glb],
        out_specs=[big, big, big, small],
        out_shape=[_sds((s, 512))] * 3 + [_sds((s, 128))],
        compiler_params=_cp(("parallel",)), name="dn_prep_bwd")(qn, kn, v, bg, bgt, tri, sel, t_inv, attn, u, w,
                                                                  du, dw, dqd, dkt, dattn, dgl)


def _dn_pre_bwd(dqn, dkn, dv, dbg, proj, conv_w8, alog_l, dtb_l):
    s = proj.shape[0]
    tr = 512
    nh = tr // 8

    def body(dq_ref, dk_ref, dv_ref, dbg_ref, u_ref, halo_ref, ba_ref, w_ref, al_ref, dt_ref,
             dy_ref, dba_ref, dsm_ref):
        i = pl.program_id(0)

        @pl.when(i == 0)
        def _():
            dsm_ref[...] = jnp.zeros_like(dsm_ref)

        keep = (i > 0).astype(F32)
        for c in range(12):
            lanes = slice(128 * c, 128 * c + 128)
            y = _conv_group(u_ref[:, lanes], halo_ref[:, lanes] * keep, w_ref, c)
            sg = _sigmoid(y)
            sv = y * sg
            if c < 8:
                rs = lax.rsqrt(jnp.sum(sv * sv, axis=1, keepdims=True) + EPS)
                n = sv * rs
                if c < 4:
                    dn = dq_ref[:, lanes] * (DN_HD ** -0.5)
                else:
                    dn = dk_ref[:, slice(128 * (c - 4), 128 * (c - 3))]
                dsv = rs * (dn - n * jnp.sum(dn * n, axis=1, keepdims=True))
            else:
                dsv = dv_ref[:, slice(128 * (c - 8), 128 * (c - 7))]
            dy_ref[:, lanes] = dsv * _silu_grad(y, sg)
        _, lane, sig_b, t, nega, g = _beta_g(ba_ref[...], al_ref[...], dt_ref[...])
        dbg = dbg_ref[...]
        da = dbg * nega * _sigmoid(t)
        is_b = lane < DN_HEADS
        is_a = jnp.logical_and(lane >= DN_HEADS, lane < 2 * DN_HEADS)
        dba_ref[...] = jnp.where(is_b, dbg * sig_b * (1.0 - sig_b), jnp.where(is_a, da, 0.0)).astype(BF16)
        d_alog = jnp.sum(jnp.where(is_a, dbg * g, 0.0), axis=0, keepdims=True)
        d_dtb = jnp.sum(jnp.where(is_a, da, 0.0), axis=0, keepdims=True)
        row = lax.broadcasted_iota(jnp.int32, (8, 128), 0)
        dsm_ref[...] = dsm_ref[...] + jnp.where(row == 0, d_alog, jnp.where(row == 1, d_dtb, 0.0))

    blk = pl.BlockSpec((tr, 512), lambda i: (i, 0))
    return pl.pallas_call(
        body, grid=(s // tr,),
        in_specs=[blk, blk, blk, pl.BlockSpec((tr, 128), lambda i: (i, 0)),
                  pl.BlockSpec((tr, 1536), lambda i: (i, 0)),
                  pl.BlockSpec((8, 1536), lambda i: (jnp.maximum(i * nh - 1, 0), 0)),
                  pl.BlockSpec((tr, 128), lambda i: (i, COL_BA_128)),
                  pl.BlockSpec((8, 1536), lambda i: (0, 0)),
                  pl.BlockSpec((1, 128), lambda i: (0, 0)), pl.BlockSpec((1, 128), lambda i: (0, 0))],
        out_specs=[pl.BlockSpec((tr, 1536), lambda i: (i, 0)), pl.BlockSpec((tr, 128), lambda i: (i, 0)),
                   pl.BlockSpec((8, 128), lambda i: (0, 0))],
        out_shape=[_sds((s, 1536)), _sds((s, 128), BF16), _sds((8, 128))],
        compiler_params=_cp(("arbitrary",)), name="dn_pre_bwd")(dqn, dkn, dv, dbg, proj, proj, proj, conv_w8,
                                                                 alog_l, dtb_l)


def _conv_bwd(dy, proj, conv_w8):
    s = dy.shape[0]
    tr = 512
    nh = tr // 8
    nblk = s // tr

    def body(dy_ref, dyn_ref, u_ref, halo_ref, w_ref, du_ref, dw_ref):
        i = pl.program_id(0)

        @pl.when(i == 0)
        def _():
            dw_ref[...] = jnp.zeros_like(dw_ref)

        keep_prev = (i > 0).astype(F32)
        keep_next = (i < nblk - 1).astype(F32)
        row = lax.broadcasted_iota(jnp.int32, (8, 128), 0)
        for c in range(12):
            lanes = slice(128 * c, 128 * c + 128)
            dyc = dy_ref[:, lanes]
            dcat = jnp.concatenate([dyc, dyn_ref[:, lanes] * keep_next], axis=0)
            xcat = jnp.concatenate([halo_ref[:, lanes] * keep_prev, u_ref[:, lanes]], axis=0)
            du = dyc * w_ref[CONV_W - 1:CONV_W, lanes]
            dwc = jnp.where(row == CONV_W - 1, jnp.sum(dyc * u_ref[:, lanes], axis=0, keepdims=True), 0.0)
            for k in range(1, CONV_W):
                du = du + pltpu.roll(dcat, tr + 8 - k, 0)[0:tr] * w_ref[CONV_W - 1 - k:CONV_W - k, lanes]
                ush = pltpu.roll(xcat, k, 0)[8:8 + tr]
                dwc = dwc + jnp.where(row == CONV_W - 1 - k, jnp.sum(dyc * ush, axis=0, keepdims=True), 0.0)
            du_ref[:, lanes] = du.astype(BF16)
            dw_ref[:, lanes] = dw_ref[:, lanes] + dwc

    return pl.pallas_call(
        body, grid=(nblk,),
        in_specs=[pl.BlockSpec((tr, 1536), lambda i: (i, 0)),
                  pl.BlockSpec((8, 1536), lambda i: (jnp.minimum((i + 1) * nh, s // 8 - 1), 0)),
                  pl.BlockSpec((tr, 1536), lambda i: (i, 0)),
                  pl.BlockSpec((8, 1536), lambda i: (jnp.maximum(i * nh - 1, 0), 0)),
                  pl.BlockSpec((8, 1536), lambda i: (0, 0))],
        out_specs=[pl.BlockSpec((tr, 1536), lambda i: (i, 0)), pl.BlockSpec((8, 1536), lambda i: (0, 0))],
        out_shape=[_sds((s, 1536), BF16), _sds((8, 1536))],
        compiler_params=_cp(("arbitrary",)), name="conv_bwd")(dy, dy, proj, proj, conv_w8)


def _in_bwd_dx(d_qkv_dn, dz, dq_att, dk_att, dv_att, dgate, dba, w_bf, x, dy, norm_w):
    s = x.shape[0]
    tm = 512

    def body(a_ref, b_ref, c_ref, d_ref, e_ref, f_ref, g_ref, w_ref, x_ref, dy_ref, nw_ref,
             dx_ref, dp_ref, dnw_ref):
        i = pl.program_id(0)

        @pl.when(i == 0)
        def _():
            dnw_ref[...] = jnp.zeros_like(dnw_ref)

        dp = jnp.concatenate([r[...].astype(BF16) for r in (a_ref, b_ref, c_ref, d_ref, e_ref, f_ref, g_ref)],
                             axis=1)
        dp_ref[...] = dp
        dh = _dot(dp, w_ref[...])
        xv = x_ref[...]
        rstd = lax.rsqrt(jnp.mean(xv * xv, axis=-1, keepdims=True) + EPS)
        xh = xv * rstd
        dnw_ref[...] = dnw_ref[...] + jnp.sum(dh * xh, axis=0, keepdims=True)
        g = dh * nw_ref[...]
        dx_ref[...] = rstd * (g - xh * jnp.mean(g * xh, axis=-1, keepdims=True)) + dy_ref[...]

    def blk(n):
        return pl.BlockSpec((tm, n), lambda i: (i, 0))

    return pl.pallas_call(
        body, grid=(s // tm,),
        in_specs=[blk(1536), blk(512), blk(512), blk(512), blk(512), blk(512), blk(128),
                  pl.BlockSpec((D_IN_PAD, D_MODEL), lambda i: (0, 0)), blk(D_MODEL), blk(D_MODEL),
                  pl.BlockSpec((1, D_MODEL), lambda i: (0, 0))],
        out_specs=[blk(D_MODEL), blk(D_IN_PAD), pl.BlockSpec((8, D_MODEL), lambda i: (0, 0))],
        out_shape=[_sds((s, D_MODEL)), _sds((s, D_IN_PAD), BF16), _sds((8, D_MODEL))],
        compiler_params=_cp(("arbitrary",)), name="in_bwd_dx")(d_qkv_dn, dz, dq_att, dk_att, dv_att, dgate, dba,
                                                                w_bf, x, dy, norm_w)


def _in_bwd_dw(x, norm_w, dp):
    s = x.shape[0]
    tm = 512
    n_col = 3
    tn = D_IN_PAD // n_col

    def body(x_ref, nw_ref, dp_ref, dw_ref):
        i = pl.program_id(1)

        @pl.when(i == 0)
        def _():
            dw_ref[...] = jnp.zeros_like(dw_ref)

        xv = x_ref[...]
        rstd = lax.rsqrt(jnp.mean(xv * xv, axis=-1, keepdims=True) + EPS)
        h = (xv * rstd * nw_ref[...]).astype(BF16)
        dw_ref[...] = dw_ref[...] + _dot_tn(dp_ref[...], h)

    return pl.pallas_call(
        body, grid=(n_col, s // tm),
        in_specs=[pl.BlockSpec((tm, D_MODEL), lambda j, i: (i, 0)), pl.BlockSpec((1, D_MODEL), lambda j, i: (0, 0)),
                  pl.BlockSpec((tm, tn), lambda j, i: (i, j))],
        out_specs=pl.BlockSpec((tn, D_MODEL), lambda j, i: (j, 0)),
        out_shape=_sds((D_IN_PAD, D_MODEL)),
        compiler_params=_cp(("parallel", "arbitrary")), name="in_bwd_dw")(x, norm_w, dp)


def _flat(a):
    return a.reshape(-1, a.shape[-1])


def _as_pattern(a, r):
    return a if r == 1 else a.reshape(r, a.shape[0] // r, a.shape[1])


D_SHARD = D_IN // N_CHIPS
BA_START = 4 * D_DN
BA_PACKED = 4096
S1_HEAD = BA_START - D_SHARD
S1_BA = 2 * D_SHARD - BA_START


def _pack_rows(g):
    pad = jnp.zeros((D_IN_PAD - D_IN, g.shape[2]), g.dtype)
    s2_ba = 2 * DN_HEADS - S1_BA
    return jnp.concatenate([g[0][:D_SHARD], g[1][:S1_HEAD], g[2][s2_ba:D_SHARD], g[3][:D_SHARD],
                            g[1][S1_HEAD:D_SHARD], g[2][:s2_ba], pad], axis=0)


def _pack_w(g):
    n = g.shape[2]

    def body(g_ref, o_ref):
        g32 = g_ref.bitcast(jnp.uint32)
        o32 = o_ref.bitcast(jnp.uint32)
        full, head, ba1 = D_SHARD // 2, S1_HEAD // 2, S1_BA // 2
        ba2 = DN_HEADS - ba1
        pieces = [(0, 0, full), (1, 0, head), (2, ba2, full), (3, 0, full), (1, head, full), (2, 0, ba2)]
        at = 0
        for chip, lo, hi in pieces:
            o32[at:at + hi - lo, :] = g32[chip, lo:hi, :]
            at += hi - lo
        o32[at:D_IN_PAD // 2, :] = jnp.zeros((D_IN_PAD // 2 - at, n), jnp.uint32)

    vm = pl.BlockSpec(memory_space=pltpu.VMEM)
    return pl.pallas_call(body, in_specs=[vm], out_specs=vm, out_shape=_sds((D_IN_PAD, n), g.dtype),
                          compiler_params=_cp(), name="pack_w")(g)


def _unpack_w(p, rows):
    n = p.shape[1]
    tn = 256
    mid = BA_PACKED + S1_BA
    pieces = [(0, 0, (0, D_SHARD)), (1, 0, (D_SHARD, BA_START)), (1, S1_HEAD, (BA_PACKED, mid)),
              (2, 0, (mid, BA_PACKED + 2 * DN_HEADS)), (2, 2 * DN_HEADS - S1_BA, (BA_START, BA_START + S1_HEAD)),
              (3, 0, (BA_START + S1_HEAD, BA_PACKED))]

    def body(p_ref, o_ref):
        for chip, at, (lo, hi) in pieces:
            o_ref[chip, at:at + hi - lo, :] = p_ref[lo:hi, :]
        for chip in range(N_CHIPS):
            o_ref[chip, D_SHARD:rows, :] = jnp.zeros((rows - D_SHARD, tn), p.dtype)

    return pl.pallas_call(
        body, grid=(n // tn,),
        in_specs=[pl.BlockSpec((D_IN_PAD, tn), lambda j: (0, j))],
        out_specs=pl.BlockSpec((N_CHIPS, rows, tn), lambda j: (0, 0, j)),
        out_shape=_sds((N_CHIPS, rows, n), p.dtype),
        compiler_params=_cp(("parallel",)), name="unpack_w")(p)


def _unpack_rows(p, rows):
    mid = BA_PACKED + S1_BA
    pad = jnp.zeros((rows - D_SHARD, p.shape[1]), p.dtype)
    return jnp.stack([jnp.concatenate([p[0:D_SHARD], pad], axis=0),
                      jnp.concatenate([p[D_SHARD:BA_START], p[BA_PACKED:mid], pad], axis=0),
                      jnp.concatenate([p[mid:BA_PACKED + 2 * DN_HEADS], p[BA_START:BA_START + S1_HEAD], pad], axis=0),
                      jnp.concatenate([p[BA_START + S1_HEAD:BA_PACKED], pad], axis=0)])


def _lane_row(vec, offset):
    return jnp.pad(vec.reshape(1, -1), ((0, 0), (offset, 128 - offset - vec.shape[0])))


def _local_step(x, tgt, norm_w, w_bf, conv_w, a_log, dt_bias, dn_norm_w, q_norm_w, k_norm_w, rel_bias, w_out_bf):
    s = x.shape[0]
    nc = s // CHUNK
    conv_w8 = jnp.pad(conv_w, ((0, 8 - CONV_W), (0, 0)))
    alog_l = _lane_row(a_log.reshape(-1), DN_HEADS)
    dtb_l = _lane_row(dt_bias.reshape(-1), DN_HEADS)
    dnw_t = jnp.tile(dn_norm_w.reshape(1, DN_HD), (1, DN_HEADS))
    qw_t = jnp.tile(q_norm_w.reshape(1, ATT_HD), (1, ATT_HEADS))
    kw_t = jnp.tile(k_norm_w.reshape(1, ATT_HD), (1, ATT_HEADS))
    bd128 = _compact_mat(DN_HD)
    bd64 = _compact_mat(ATT_HD)
    tri = _tri_incl()
    sel = _lane_select()
    buckets = _bucket_tables()

    proj = _in_proj(x, norm_w, w_bf)
    qn, kn, v_dn, bg = _dn_pre(proj, conv_w8, alog_l, dtb_l)
    bgt = bg[:, 0:8].reshape(nc, CHUNK, 8).transpose(0, 2, 1)
    u, w, qd, kt, attn, t_inv, gl = _dn_prep(qn, kn, v_dn, bg, bgt, tri, sel)
    o_dn, vn, sp = _dn_scan(u, w, qd, kt, attn, gl)
    q1, k1, v1, q4, k4, v4, q16, k16, v16 = _att_pre(proj, qw_t, kw_t, bd64)
    rs = [r for _, r in PATTERNS]
    qkv = [(q1, k1, v1, 0), (_flat(q4), _flat(k4), _flat(v4), 0), (_flat(q16), _flat(k16), _flat(v16), 0)]
    bias = _bias_fwd(rel_bias, buckets)
    o_pats, lse_pats = [], []
    for p, r in enumerate(rs):
        o_p, lse_p = _att_fwd(*qkv[p], bias, p, r, "att_fwd_r%d" % r)
        o_pats.append(_as_pattern(o_p, r))
        lse_pats.append(_as_pattern(lse_p, r))
    mixed, o_att, l1, l4, l16 = _post_fwd(o_dn, proj, o_pats, lse_pats, dnw_t, bd128)
    dy, loss_blk = _out_fwd(x, mixed, w_out_bf, tgt)

    dmixed, d_w_out = _out_bwd(dy, mixed, w_out_bf)
    do_dn, dz, dgate, do1, do4, do16, dl1, dl4, dl16, d_dnw = _post_bwd(dmixed, o_dn, proj, o_att, dnw_t, bd128)
    side = [(do1, l1, dl1), (_flat(do4), _flat(l4), _flat(dl4)), (_flat(do16), _flat(l16), _flat(dl16))]
    dq_pats, dk_pats, dv_pats, ds_accs = [], [], [], []
    for p, r in enumerate(rs):
        dq_p, dk_p, dv_p, ds_p = _att_bwd(*qkv[p], *side[p], bias, p, r, "att_bwd_r%d" % r)
        dq_pats.append(_as_pattern(dq_p, r))
        dk_pats.append(_as_pattern(dk_p, r))
        dv_pats.append(_as_pattern(dv_p, r))
        ds_accs.append(ds_p)
    d_rel_bias = _bias_bwd(ds_accs, buckets)
    dq_att, dk_att, dv_att, d_qw, d_kw = _att_pre_bwd(dq_pats, dk_pats, dv_pats, proj, qw_t, kw_t, bd64)
    du, dqd, dkt, dw, dattn, dgl = _dn_scan_bwd(do_dn, sp, qd, kt, w, vn, attn, gl)
    dqn, dkn, dv_dn, dbg = _dn_prep_bwd(qn, kn, v_dn, bg, bgt, tri, sel, t_inv, attn, u, w, du, dw, dqd, dkt, dattn,
                                        dgl)
    dyc, dba, dsm = _dn_pre_bwd(dqn, dkn, dv_dn, dbg, proj, conv_w8, alog_l, dtb_l)
    d_qkv_dn, d_conv8 = _conv_bwd(dyc, proj, conv_w8)
    grad_x, dp, d_nw8 = _in_bwd_dx(d_qkv_dn, dz, dq_att, dk_att, dv_att, dgate, dba, w_bf, x, dy, norm_w)
    d_w_in_t = _in_bwd_dw(x, norm_w, dp)

    grads = dict(
        norm_w=d_nw8[0:1, :],
        w_in_t=d_w_in_t,
        conv_w=d_conv8[0:CONV_W, :],
        a_log=dsm[0:1, DN_HEADS:2 * DN_HEADS],
        dt_bias=dsm[1:2, DN_HEADS:2 * DN_HEADS],
        dn_norm_w=d_dnw[0:1, :],
        q_norm_w=d_qw[0:1, :].reshape(ATT_HEADS, ATT_HD),
        k_norm_w=d_kw[0:1, :].reshape(ATT_HEADS, ATT_HD),
        rel_bias=d_rel_bias,
        w_out=d_w_out,
    )
    return loss_blk[0, 0], grad_x, grads


MESH_ID = pl.DeviceIdType.MESH
ANY = pl.BlockSpec(memory_space=pl.ANY)


def _position():
    return lax.axis_index("x"), lax.axis_index("y"), lax.axis_index("c")


def _other_chips(x, y):
    return [(1 - x, y), (x, 1 - y), (1 - x, 1 - y)]


SHARD_PAD = 1040
WIN = 528


def _row_split(n):
    return (n // 2) // 128 * 128


def _part(ref, cc):
    n = ref.shape[0]
    sp = _row_split(n)
    return ref.at[pl.ds(0, sp)] if cc == 0 else ref.at[pl.ds(sp, n - sp)]


def _gather_weights(wt_s, w_out_s, conv_s):
    def body(a_ref, b_ref, c_ref, ga_ref, gb_ref, gc_ref, send_sems, recv_sems, loc_sems, a_vmem, b_vmem):
        x, y, c = _position()
        me = 2 * x + y
        sib = (x, y, 1 - c)
        big = ((a_ref, ga_ref), (b_ref, gb_ref))
        stage_in = [pltpu.make_async_copy(a_ref, a_vmem, loc_sems.at[0]),
                    pltpu.make_async_copy(b_ref, b_vmem, loc_sems.at[1])]
        local = [pltpu.make_async_copy(a_vmem, ga_ref.at[me], loc_sems.at[0]),
                 pltpu.make_async_copy(b_vmem, gb_ref.at[me], loc_sems.at[1]),
                 pltpu.make_async_copy(c_ref, gc_ref.at[me], loc_sems.at[2])]
        for cp in stage_in:
            cp.start()
        local[2].start()
        for cp in stage_in:
            cp.wait()
        for cp in local[:2]:
            cp.start()
        others = _other_chips(x, y)

        def exchange(cc):
            sends = []
            for j, (px, py) in enumerate(others):
                for t, (src, dst) in enumerate(big):
                    k = 3 * j + t
                    sends.append(pltpu.make_async_remote_copy(
                        src_ref=_part(src, cc), dst_ref=_part(dst.at[me], cc), send_sem=send_sems.at[k],
                        recv_sem=recv_sems.at[k], device_id=(px, py, c), device_id_type=MESH_ID))
                sends.append(pltpu.make_async_remote_copy(
                    src_ref=c_ref, dst_ref=gc_ref.at[me], send_sem=send_sems.at[3 * j + 2],
                    recv_sem=recv_sems.at[3 * j + 2], device_id=(px, py, c), device_id_type=MESH_ID))
            for cp in sends:
                cp.start()
            for j, (px, py) in enumerate(others):
                src_chip = 2 * px + py
                for t, (src, dst) in enumerate(big):
                    landed = _part(dst.at[src_chip], cc)
                    pltpu.make_async_remote_copy(
                        src_ref=_part(src, cc), dst_ref=landed, send_sem=send_sems.at[3 * j + t],
                        recv_sem=recv_sems.at[3 * j + t], device_id=(px, py, c), device_id_type=MESH_ID).wait_recv()
                    k = 9 + 2 * j + t
                    fwd = pltpu.make_async_remote_copy(
                        src_ref=landed, dst_ref=landed, send_sem=send_sems.at[k], recv_sem=recv_sems.at[k],
                        device_id=sib, device_id_type=MESH_ID)
                    fwd.start()
                    sends.append(fwd)
                pltpu.make_async_remote_copy(
                    src_ref=c_ref, dst_ref=gc_ref.at[src_chip], send_sem=send_sems.at[3 * j + 2],
                    recv_sem=recv_sems.at[3 * j + 2], device_id=(px, py, c), device_id_type=MESH_ID).wait_recv()
            for j, (px, py) in enumerate(others):
                src_chip = 2 * px + py
                for t, (src, dst) in enumerate(big):
                    k = 9 + 2 * j + t
                    theirs = _part(dst.at[src_chip], 1 - cc)
                    pltpu.make_async_remote_copy(
                        src_ref=theirs, dst_ref=theirs, send_sem=send_sems.at[k], recv_sem=recv_sems.at[k],
                        device_id=sib, device_id_type=MESH_ID).wait_recv()
            for cp in sends:
                cp.wait_send()

        for cc in (0, 1):
            pl.when(c == cc)(functools.partial(exchange, cc))
        for cp in local:
            cp.wait()

    srcs = (wt_s, w_out_s, conv_s)
    n_sem = 9 + 6
    return pl.pallas_call(
        body, in_specs=[ANY] * 3, out_specs=[ANY] * 3,
        out_shape=[_sds((N_CHIPS,) + a.shape, a.dtype) for a in srcs],
        scratch_shapes=[pltpu.SemaphoreType.DMA((n_sem,)), pltpu.SemaphoreType.DMA((n_sem,)),
                        pltpu.SemaphoreType.DMA((3,)), pltpu.VMEM(wt_s.shape, wt_s.dtype),
                        pltpu.VMEM(w_out_s.shape, w_out_s.dtype)],
        compiler_params=_cp(), name="gather_weights")(*srcs)


def _exchange_grads(gw_in4, gw_out4, small):
    n_big = 2
    n_dev = 8

    def body(a_ref, b_ref, s_ref, ra_ref, rb_ref, rs_ref, send_sems, recv_sems, loc_sem):
        x, y, c = _position()
        dev = 4 * x + 2 * y + c
        pairs = ((a_ref, ra_ref), (b_ref, rb_ref))
        local = pltpu.make_async_copy(s_ref, rs_ref.at[dev], loc_sem)
        local.start()
        sends = []
        for j, (px, py) in enumerate(_other_chips(x, y)):
            for t, (src, dst) in enumerate(pairs):
                k = n_big * j + t
                sends.append(pltpu.make_async_remote_copy(
                    src_ref=src.at[2 * px + py], dst_ref=dst.at[j], send_sem=send_sems.at[k],
                    recv_sem=recv_sems.at[k], device_id=(px, py, c), device_id_type=MESH_ID))
        flips = [(dx, dy, dc) for dx in (0, 1) for dy in (0, 1) for dc in (0, 1)][1:]
        for f, (dx, dy, dc) in enumerate(flips):
            k = 3 * n_big + f
            peer = (x ^ dx, y ^ dy, c ^ dc)
            sends.append(pltpu.make_async_remote_copy(
                src_ref=s_ref, dst_ref=rs_ref.at[dev], send_sem=send_sems.at[k], recv_sem=recv_sems.at[k],
                device_id=peer, device_id_type=MESH_ID))
        for cp in sends:
            cp.start()
        for j, (px, py) in enumerate(_other_chips(x, y)):
            for t, (src, dst) in enumerate(pairs):
                k = n_big * j + t
                pltpu.make_async_remote_copy(
                    src_ref=src.at[0], dst_ref=dst.at[j], send_sem=send_sems.at[k], recv_sem=recv_sems.at[k],
                    device_id=(px, py, c), device_id_type=MESH_ID).wait_recv()
        for f, (dx, dy, dc) in enumerate(flips):
            k = 3 * n_big + f
            peer = (x ^ dx, y ^ dy, c ^ dc)
            pltpu.make_async_remote_copy(
                src_ref=s_ref, dst_ref=rs_ref.at[4 * peer[0] + 2 * peer[1] + peer[2]], send_sem=send_sems.at[k],
                recv_sem=recv_sems.at[k], device_id=peer, device_id_type=MESH_ID).wait_recv()
        for cp in sends:
            cp.wait_send()
        local.wait()

    n_sem = 3 * n_big + n_dev - 1
    return pl.pallas_call(
        body, in_specs=[ANY] * 3, out_specs=[ANY] * 3,
        out_shape=[_sds((3,) + gw_in4.shape[1:], gw_in4.dtype), _sds((3,) + gw_out4.shape[1:], gw_out4.dtype),
                   _sds((n_dev,) + small.shape, small.dtype)],
        scratch_shapes=[pltpu.SemaphoreType.DMA((n_sem,)), pltpu.SemaphoreType.DMA((n_sem,)),
                        pltpu.SemaphoreType.DMA],
        name="exchange_grads")(gw_in4, gw_out4, small)


def _plane_sum(own, recv, name):
    rows, cols = own.shape
    tr = 128

    def body(o_ref, r_ref, out_ref):
        acc = o_ref[...]
        for j in range(3):
            acc = acc + r_ref[j].astype(F32)
        out_ref[...] = acc

    return pl.pallas_call(
        body, grid=(pl.cdiv(rows, tr),),
        in_specs=[pl.BlockSpec((tr, cols), lambda i: (i, 0)), pl.BlockSpec((3, tr, cols), lambda i: (0, i, 0))],
        out_specs=pl.BlockSpec((tr, cols), lambda i: (i, 0)), out_shape=_sds((rows, cols)),
        compiler_params=_cp(("parallel",)), name=name)(own, recv)


def _sum_pair(a, b, name):
    n, rows, cols = a.shape
    tr = 128

    def body(a_ref, b_ref, s_ref, sb_ref):
        tot = a_ref[...] + b_ref[...]
        s_ref[...] = tot
        sb_ref[...] = tot.astype(BF16)

    blk = pl.BlockSpec((1, tr, cols), lambda k, i: (k, i, 0))
    return pl.pallas_call(
        body, grid=(n, pl.cdiv(rows, tr)), in_specs=[blk, blk], out_specs=[blk, blk],
        out_shape=[_sds(a.shape), _sds(a.shape, BF16)],
        compiler_params=_cp(("parallel", "parallel")), name=name)(a, b)


def _fill_parts(f_in, f_out, rows_in, rows_out):
    def body(a_ref, b_ref, fa_ref, fb_ref, send_sems, recv_sems, loc_sems, a_vmem, b_vmem):
        x, y, c = _position()
        sib = (x, y, 1 - c)
        pairs = ((a_ref, fa_ref), (b_ref, fb_ref))
        stage_in = [pltpu.make_async_copy(a_ref, a_vmem, loc_sems.at[0]),
                    pltpu.make_async_copy(b_ref, b_vmem, loc_sems.at[1])]
        for cp in stage_in:
            cp.start()
        for cp in stage_in:
            cp.wait()

        def fill(cc):
            mine = [_part(dst, cc) for _, dst in pairs]
            srcs = [src.at[pl.ds(0, m.shape[0])] for src, m in zip((a_vmem, b_vmem), mine)]
            local = [pltpu.make_async_copy(s, m, loc_sems.at[t]) for t, (s, m) in enumerate(zip(srcs, mine))]
            sends = [pltpu.make_async_remote_copy(src_ref=s, dst_ref=m, send_sem=send_sems.at[t],
                                                  recv_sem=recv_sems.at[t], device_id=sib, device_id_type=MESH_ID)
                     for t, (s, m) in enumerate(zip(srcs, mine))]
            for cp in local + sends:
                cp.start()
            for t, (_, dst) in enumerate(pairs):
                theirs = _part(dst, 1 - cc)
                pltpu.make_async_remote_copy(src_ref=theirs, dst_ref=theirs, send_sem=send_sems.at[t],
                                             recv_sem=recv_sems.at[t], device_id=sib, device_id_type=MESH_ID).wait_recv()
            for cp in sends:
                cp.wait_send()
            for cp in local:
                cp.wait()

        for cc in (0, 1):
            pl.when(c == cc)(functools.partial(fill, cc))

    return pl.pallas_call(
        body, in_specs=[ANY] * 2, out_specs=[ANY] * 2,
        out_shape=[_sds((rows_in, D_MODEL)), _sds((rows_out, D_MODEL))],
        scratch_shapes=[pltpu.SemaphoreType.DMA((2,)), pltpu.SemaphoreType.DMA((2,)), pltpu.SemaphoreType.DMA((2,)),
                        pltpu.VMEM(f_in.shape, F32), pltpu.VMEM(f_out.shape, F32)],
        compiler_params=_cp(), name="fill_parts")(f_in, f_out)


def _swap_windows(u_in, u_out, win_in, win_out):
    def body(a_ref, b_ref, ra_ref, rb_ref, send_sems, recv_sems):
        x, y, c = _position()
        sib = (x, y, 1 - c)
        cps = []
        for t, (src, dst, win) in enumerate(((a_ref, ra_ref, win_in), (b_ref, rb_ref, win_out))):
            split = _row_split(src.shape[1])
            start = pl.multiple_of((1 - c) * split, split)
            cps.append(pltpu.make_async_remote_copy(
                src_ref=src.at[:, pl.ds(start, win), :], dst_ref=dst, send_sem=send_sems.at[t],
                recv_sem=recv_sems.at[t], device_id=sib, device_id_type=MESH_ID))
        for cp in cps:
            cp.start()
        for cp in cps:
            cp.wait_recv()
        for cp in cps:
            cp.wait_send()

    return pl.pallas_call(
        body, in_specs=[ANY] * 2, out_specs=[ANY] * 2,
        out_shape=[_sds((N_CHIPS, win_in, D_MODEL)), _sds((N_CHIPS, win_out, D_MODEL))],
        scratch_shapes=[pltpu.SemaphoreType.DMA((2,)), pltpu.SemaphoreType.DMA((2,))],
        name="swap_windows")(u_in, u_out)


SMALL_LAYOUT = (("norm_w", 1024), ("conv_w", 6144), ("a_log", 128), ("dt_bias", 128), ("dn_norm_w", 128),
                ("q_norm_w", 512), ("k_norm_w", 512), ("rel_bias", 256))
SMALL_TOTAL = sum(n for _, n in SMALL_LAYOUT)


def _small_offset(name):
    off = 0
    for n, size in SMALL_LAYOUT:
        if n == name:
            return off
        off += size
    raise KeyError(name)


def _pack_small(grads):
    parts = []
    for name, size in SMALL_LAYOUT:
        flat = grads[name].reshape(1, -1)
        parts.append(jnp.pad(flat, ((0, 0), (0, size - flat.shape[1]))))
    return jnp.concatenate(parts, axis=1)


def _sum_small(rows):
    n_dev = rows.shape[0]
    q_off = _small_offset("q_norm_w")
    k_off = _small_offset("k_norm_w")

    def body(r_ref, tot_ref, qk_ref):
        tot = r_ref[0:1, :]
        for d in range(1, n_dev):
            tot = tot + r_ref[d:d + 1, :]
        tot_ref[...] = tot
        for row, off in ((0, q_off), (1, k_off)):
            s4 = tot[:, off:off + 128] + tot[:, off + 128:off + 256] + tot[:, off + 256:off + 384] \
                + tot[:, off + 384:off + 512]
            qk_ref[row:row + 1, :] = s4 + pltpu.roll(s4, ATT_HD, 1)

    return pl.pallas_call(
        body, in_specs=[pl.BlockSpec(memory_space=pltpu.VMEM)],
        out_specs=[pl.BlockSpec(memory_space=pltpu.VMEM)] * 2,
        out_shape=[_sds((1, SMALL_TOTAL)), _sds((2, 128))],
        compiler_params=_cp(), name="sum_small")(rows)


def _adamw_math(w, g, m, v):
    m = ADAM_B1 * m + (1.0 - ADAM_B1) * g
    v = ADAM_B2 * v + (1.0 - ADAM_B2) * (g * g)
    m_hat = m / (1.0 - ADAM_B1 ** ADAM_STEP)
    v_hat = v / (1.0 - ADAM_B2 ** ADAM_STEP)
    delta = -ADAM_LR * (m_hat / (jnp.sqrt(v_hat) + ADAM_EPS) + ADAM_WD * w)
    return delta, m, v


def _adamw_big(g, w, m, v, name):
    rows, cols = w.shape
    tr = 128

    def body(g_ref, w_ref, m_ref, v_ref, go_ref, d_ref, nm_ref, nv_ref):
        g = g_ref[...]
        go_ref[...] = g
        d_ref[...], nm_ref[...], nv_ref[...] = _adamw_math(w_ref[...], g, m_ref[...], v_ref[...])

    blk = pl.BlockSpec((tr, cols), lambda i: (i, 0))
    return pl.pallas_call(
        body, grid=(pl.cdiv(rows, tr),), in_specs=[blk] * 4, out_specs=[blk] * 4,
        out_shape=[_sds((rows, cols))] * 4, compiler_params=_cp(("parallel",)), name=name)(g, w, m, v)


def _adamw_rows(g, w, m, v, name):
    rows, cols = w.shape
    tr = 128

    def body(g_ref, w_ref, m_ref, v_ref, go_ref, d_ref, nm_ref, nv_ref, s_g, s_d, s_m, s_v):
        g = g_ref[...]
        s_g[...] = g
        s_d[...], s_m[...], s_v[...] = _adamw_math(w_ref[...], g, m_ref[...], v_ref[...])
        for i in range(tr):
            for scr, out in ((s_g, go_ref), (s_d, d_ref), (s_m, nm_ref), (s_v, nv_ref)):
                out[i] = scr[i:i + 1, :]

    blk = pl.BlockSpec((tr, cols), lambda i: (i, 0))
    oblk = pl.BlockSpec((tr, 1, cols), lambda i: (i, 0, 0))
    return pl.pallas_call(
        body, grid=(pl.cdiv(rows, tr),), in_specs=[blk] * 4, out_specs=[oblk] * 4,
        out_shape=[_sds((rows, 1, cols))] * 4, scratch_shapes=[pltpu.VMEM((tr, cols), F32)] * 4,
        compiler_params=_cp(("parallel",)), name=name)(g, w, m, v)


def _adamw_small(w, g, m, v, name):
    def body(w_ref, g_ref, m_ref, v_ref, d_ref, nm_ref, nv_ref):
        d_ref[...], nm_ref[...], nv_ref[...] = _adamw_math(w_ref[...], g_ref[...], m_ref[...], v_ref[...])

    vm = pl.BlockSpec(memory_space=pltpu.VMEM)
    return pl.pallas_call(body, in_specs=[vm] * 4, out_specs=[vm] * 3, out_shape=[_sds(w.shape)] * 3,
                          compiler_params=_cp(), name=name)(w, g, m, v)


WEIGHTS = ("norm_w", "w_in", "conv_w", "a_log", "dt_bias", "dn_norm_w", "q_norm_w", "k_norm_w", "rel_bias", "w_out")


def kernel(x, norm_w, w_in, conv_w, a_log, dt_bias, dn_norm_w, q_norm_w, k_norm_w, rel_bias, w_out, loss_target, m_norm_w, m_w_in, m_conv_w, m_a_log, m_dt_bias, m_dn_norm_w, m_q_norm_w, m_k_norm_w, m_rel_bias, m_w_out, v_norm_w, v_w_in, v_conv_w, v_a_log, v_dt_bias, v_dn_norm_w, v_q_norm_w, v_k_norm_w, v_rel_bias, v_w_out):
    xi, yi, ci = _position()
    chip = 2 * xi + yi
    w_loc = dict(norm_w=norm_w, w_in=w_in[0].T, conv_w=conv_w[0], a_log=a_log, dt_bias=dt_bias, dn_norm_w=dn_norm_w,
                 q_norm_w=q_norm_w, k_norm_w=k_norm_w, rel_bias=rel_bias, w_out=w_out[0])
    m_loc = dict(norm_w=m_norm_w, w_in=m_w_in[0].T, conv_w=m_conv_w[0], a_log=m_a_log, dt_bias=m_dt_bias,
                 dn_norm_w=m_dn_norm_w, q_norm_w=m_q_norm_w, k_norm_w=m_k_norm_w, rel_bias=m_rel_bias,
                 w_out=m_w_out[0])
    v_loc = dict(norm_w=v_norm_w, w_in=v_w_in[0].T, conv_w=v_conv_w[0], a_log=v_a_log, dt_bias=v_dt_bias,
                 dn_norm_w=v_dn_norm_w, q_norm_w=v_q_norm_w, k_norm_w=v_k_norm_w, rel_bias=v_rel_bias,
                 w_out=v_w_out[0])

    wt_pad = jnp.pad(w_loc["w_in"].astype(BF16), ((0, SHARD_PAD - D_SHARD), (0, 0)))
    g_in, g_out, g_conv = _gather_weights(wt_pad, w_loc["w_out"].astype(BF16), w_loc["conv_w"])
    wt_full = _pack_w(g_in)
    w_out_full = g_out.reshape(D_MODEL, D_MODEL)
    conv_full = g_conv.transpose(1, 0, 2).reshape(CONV_W, 3 * D_DN)

    loss_local, grad_x, grads = _local_step(x[0], loss_target[0], norm_w, wt_full, conv_full, a_log, dt_bias,
                                            dn_norm_w, q_norm_w, k_norm_w, rel_bias, w_out_full)
    loss = lax.psum(loss_local, ("x", "y", "c"))

    u_in = _unpack_w(grads["w_in_t"], SHARD_PAD)
    u_out = grads["w_out"].reshape(N_CHIPS, D_MODEL // N_CHIPS, D_MODEL)
    win_out = u_out.shape[1] // 2
    sib_in, sib_out = _swap_windows(u_in, u_out, WIN, win_out)
    in_mine = lax.dynamic_slice_in_dim(u_in, ci * _row_split(SHARD_PAD), WIN, axis=1)
    out_mine = lax.dynamic_slice_in_dim(u_out, ci * win_out, win_out, axis=1)
    h_in, hb_in = _sum_pair(in_mine, sib_in, "chip_sum_w_in")
    h_out, hb_out = _sum_pair(out_mine, sib_out, "chip_sum_w_out")
    small = _pack_small(grads)
    r_in, r_out, r_small = _exchange_grads(hb_in, hb_out, small)
    f_in = _plane_sum(lax.dynamic_index_in_dim(h_in, chip, 0, keepdims=False), r_in, "shard_sum_w_in")
    f_out = _plane_sum(lax.dynamic_index_in_dim(h_out, chip, 0, keepdims=False), r_out, "shard_sum_w_out")
    full_in, full_out = _fill_parts(f_in, f_out, SHARD_PAD, u_out.shape[1])
    tot_small, qk = _sum_small(r_small.reshape(8, SMALL_TOTAL))

    def small_grad(name, n):
        off = _small_offset(name)
        return tot_small[:, off:off + n]

    conv_all = small_grad("conv_w", CONV_W * 3 * D_DN).reshape(CONV_W, 3 * D_DN)
    g_small = dict(
        norm_w=small_grad("norm_w", D_MODEL),
        conv_w=lax.dynamic_slice_in_dim(conv_all, chip * (3 * D_DN // N_CHIPS), 3 * D_DN // N_CHIPS, axis=1),
        a_log=small_grad("a_log", DN_HEADS),
        dt_bias=small_grad("dt_bias", DN_HEADS),
        dn_norm_w=small_grad("dn_norm_w", DN_HD),
        q_norm_w=qk[0:1, 0:ATT_HD],
        k_norm_w=qk[1:2, 0:ATT_HD],
        rel_bias=small_grad("rel_bias", ATT_HEADS * N_BUCKETS).reshape(ATT_HEADS, N_BUCKETS),
    )

    out_g, out_d, out_m, out_v = {}, {}, {}, {}
    out_g["w_in"], out_d["w_in"], out_m["w_in"], out_v["w_in"] = _adamw_rows(
        full_in, w_loc["w_in"], m_loc["w_in"], v_loc["w_in"], "adamw_w_in")
    out_g["w_out"], out_d["w_out"], out_m["w_out"], out_v["w_out"] = _adamw_big(
        full_out, w_loc["w_out"], m_loc["w_out"], v_loc["w_out"], "adamw_w_out")
    for name in g_small:
        out_g[name] = g_small[name]
        out_d[name], out_m[name], out_v[name] = _adamw_small(w_loc[name], g_small[name], m_loc[name], v_loc[name],
                                                             "adamw_" + name)
    for d in (out_g, out_d, out_m, out_v):
        d["w_in"] = d["w_in"].transpose(1, 2, 0)
        for name in ("conv_w", "w_out"):
            d[name] = d[name][None]
    return (loss, grad_x[None], *[out_g[n] for n in WEIGHTS], *[out_d[n] for n in WEIGHTS],
            *[out_m[n] for n in WEIGHTS], *[out_v[n] for n in WEIGHTS])
```

```python
import functools
import math

import numpy as np
import jax
import jax.numpy as jnp
from jax import lax
from jax.experimental import pallas as pl
from jax.experimental.pallas import tpu as pltpu

F32 = jnp.float32
BF16 = jnp.bfloat16
HI = lax.Precision.HIGHEST

D_MODEL = 1024
D_DN = 512
DN_HEADS = 4
DN_HD = 128
CONV_W = 4
CHUNK = 64
D_ATT = 512
ATT_HEADS = 8
ATT_HD = 64
PATTERNS = ((128, 1), (512, 4), (2048, 16))
N_BUCKETS = 32
MAX_DISTANCE = 2048
D_IN = 4 * D_DN + 2 * DN_HEADS + 4 * D_ATT
D_IN_PAD = 4224
EPS = 1e-6
BLK = 128
NEG = -1e30
N_CHIPS = 4

ADAM_LR = 0.001
ADAM_B1 = 0.9
ADAM_B2 = 0.999
ADAM_EPS = 1e-08
ADAM_WD = 0.01
ADAM_STEP = 10

VMEM_LIMIT = 56 * 1024 * 1024

COL_Z = 3
COL_ATT_Q = 4
COL_ATT_K = 5
COL_ATT_V = 6
COL_GATE = 7
COL_BA_128 = 32


def _cp(sem=None):
    if sem is None:
        return pltpu.CompilerParams(vmem_limit_bytes=VMEM_LIMIT)
    return pltpu.CompilerParams(dimension_semantics=sem, vmem_limit_bytes=VMEM_LIMIT)


def _sds(shape, dtype=F32):
    return jax.ShapeDtypeStruct(shape, dtype)


def _mm(a, b):
    return jnp.dot(a.astype(BF16), b.astype(BF16), preferred_element_type=F32)


def _mm_nt(a, b):
    return lax.dot_general(a.astype(BF16), b.astype(BF16), (((1,), (1,)), ((), ())),
                           preferred_element_type=F32)


def _mm_tn(a, b):
    return lax.dot_general(a.astype(BF16), b.astype(BF16), (((0,), (0,)), ((), ())),
                           preferred_element_type=F32)


def _mmx(a, b):
    return jnp.dot(a, b, precision=HI, preferred_element_type=F32)


def _mmx_nt(a, b):
    return lax.dot_general(a, b, (((1,), (1,)), ((), ())), precision=HI, preferred_element_type=F32)


def _mmx_tn(a, b):
    return lax.dot_general(a, b, (((0,), (0,)), ((), ())), precision=HI, preferred_element_type=F32)


def _dot(a, b):
    return jnp.dot(a, b, preferred_element_type=F32)


def _dot_nt(a, b):
    return lax.dot_general(a, b, (((1,), (1,)), ((), ())), preferred_element_type=F32)


def _dot_tn(a, b):
    return lax.dot_general(a, b, (((0,), (0,)), ((), ())), preferred_element_type=F32)


def _split(a):
    hi = a.astype(BF16)
    return hi, (a - hi.astype(F32)).astype(BF16)


def _mm3(a_s, b_s):
    return _dot(a_s[0], b_s[0]) + _dot(a_s[0], b_s[1]) + _dot(a_s[1], b_s[0])


def _mm3_tn(a_s, b_s):
    return _dot_tn(a_s[0], b_s[0]) + _dot_tn(a_s[0], b_s[1]) + _dot_tn(a_s[1], b_s[0])


def _interleave(gens):
    live = list(gens)
    while live:
        nxt = []
        for g in live:
            try:
                next(g)
                nxt.append(g)
            except StopIteration:
                pass
        live = nxt


def _segsum(x, bd):
    hi = x.astype(BF16)
    r1 = x - hi.astype(F32)
    mid = r1.astype(BF16)
    lo = (r1 - mid.astype(F32)).astype(BF16)
    return (jnp.dot(hi, bd, preferred_element_type=F32) + jnp.dot(mid, bd, preferred_element_type=F32)
            + jnp.dot(lo, bd, preferred_element_type=F32))


def _compact_mat(seg, n=512):
    slot = 128 * seg // n
    src = np.arange(n)[:, None]
    dst = np.arange(128)[None, :]
    return jnp.asarray((src // seg == dst // slot).astype(np.float32), dtype=BF16)


def _seg_mean(x, cm, seg):
    hi, lo = _split(x)
    return (_dot(hi, cm) + _dot(lo, cm)) * (1.0 / seg)


def _seg_expand(c, cm, seg):
    hi, lo = _split(c)
    return (_dot_nt(hi, cm) + _dot_nt(lo, cm)) * (cm.shape[0] / (128.0 * seg))


def _seg_rstd(x, cm, seg):
    return _seg_expand(lax.rsqrt(_seg_mean(x * x, cm, seg) + EPS), cm, seg)


def _sigmoid(x):
    return 1.0 / (1.0 + jnp.exp(-x))


def _silu_grad(x, s):
    return s * (1.0 + x * (1.0 - s))


def _tri_incl():
    i = np.arange(CHUNK)
    return jnp.asarray((i[:, None] >= i[None, :]).astype(np.float32))


def _t5_bucket(dist):
    max_exact = N_BUCKETS // 2
    d = np.maximum(dist, 1).astype(np.float64)
    large = max_exact + (np.log(d / max_exact) / math.log(MAX_DISTANCE / max_exact)
                         * (N_BUCKETS - max_exact)).astype(np.int32)
    large = np.minimum(large, N_BUCKETS - 1)
    return np.where(dist < max_exact, dist, large).astype(np.int32)


def _bucket_tables():
    qi = np.arange(BLK)[:, None]
    kj = np.arange(2 * BLK)[None, :]
    step = qi - kj + BLK
    return jnp.asarray(np.stack([_t5_bucket(np.clip(step, 0, None) * r) for _, r in PATTERNS]))


def _in_proj(x, norm_w, wt_bf):
    s = x.shape[0]
    tm = 512

    def body(x_ref, nw_ref, w_ref, o_ref):
        xv = x_ref[...]
        rstd = lax.rsqrt(jnp.mean(xv * xv, axis=-1, keepdims=True) + EPS)
        h = (xv * rstd * nw_ref[...]).astype(BF16)
        o_ref[...] = _dot_nt(h, w_ref[...])

    return pl.pallas_call(
        body, grid=(s // tm,),
        in_specs=[pl.BlockSpec((tm, D_MODEL), lambda i: (i, 0)),
                  pl.BlockSpec((1, D_MODEL), lambda i: (0, 0)),
                  pl.BlockSpec((D_IN_PAD, D_MODEL), lambda i: (0, 0))],
        out_specs=pl.BlockSpec((tm, D_IN_PAD), lambda i: (i, 0)),
        out_shape=_sds((s, D_IN_PAD)), compiler_params=_cp(("parallel",)), name="in_proj")(x, norm_w, wt_bf)


def _conv_group(cur, halo, w_ref, c):
    rows = cur.shape[0]
    lanes = slice(128 * c, 128 * c + 128)
    xcat = jnp.concatenate([halo, cur], axis=0)
    y = cur * w_ref[CONV_W - 1:CONV_W, lanes]
    for k in range(1, CONV_W):
        sh = pltpu.roll(xcat, k, 0)[8:8 + rows]
        y = y + sh * w_ref[CONV_W - 1 - k:CONV_W - k, lanes]
    return y


def _beta_g(ba, alog_l, dtb_l):
    lane = lax.broadcasted_iota(jnp.int32, ba.shape, 1)
    sig_b = _sigmoid(ba)
    t = ba + dtb_l
    softplus = jnp.maximum(t, 0.0) + jnp.log(1.0 + jnp.exp(-jnp.abs(t)))
    nega = -jnp.exp(alog_l)
    g = nega * softplus
    out = jnp.where(lane < DN_HEADS, sig_b, jnp.where(lane < 2 * DN_HEADS, g, 0.0))
    return out, lane, sig_b, t, nega, g


def _dn_pre(proj, conv_w8, alog_l, dtb_l):
    s = proj.shape[0]
    tr = 512
    nh = tr // 8

    def body(u_ref, halo_ref, ba_ref, w_ref, al_ref, dt_ref, q_ref, k_ref, v_ref, bg_ref):
        i = pl.program_id(0)
        keep = (i > 0).astype(F32)
        for c in range(12):
            lanes = slice(128 * c, 128 * c + 128)
            y = _conv_group(u_ref[:, lanes], halo_ref[:, lanes] * keep, w_ref, c)
            sv = y * _sigmoid(y)
            if c < 8:
                rs = lax.rsqrt(jnp.sum(sv * sv, axis=1, keepdims=True) + EPS)
                n = sv * rs
                if c < 4:
                    q_ref[:, lanes] = n * (DN_HD ** -0.5)
                else:
                    k_ref[:, slice(128 * (c - 4), 128 * (c - 3))] = n
            else:
                v_ref[:, slice(128 * (c - 8), 128 * (c - 7))] = sv
        bg_ref[...] = _beta_g(ba_ref[...], al_ref[...], dt_ref[...])[0]

    return pl.pallas_call(
        body, grid=(s // tr,),
        in_specs=[pl.BlockSpec((tr, 1536), lambda i: (i, 0)),
                  pl.BlockSpec((8, 1536), lambda i: (jnp.maximum(i * nh - 1, 0), 0)),
                  pl.BlockSpec((tr, 128), lambda i: (i, COL_BA_128)),
                  pl.BlockSpec((8, 1536), lambda i: (0, 0)),
                  pl.BlockSpec((1, 128), lambda i: (0, 0)),
                  pl.BlockSpec((1, 128), lambda i: (0, 0))],
        out_specs=[pl.BlockSpec((tr, 512), lambda i: (i, 0))] * 3 + [pl.BlockSpec((tr, 128), lambda i: (i, 0))],
        out_shape=[_sds((s, 512))] * 3 + [_sds((s, 128))],
        compiler_params=_cp(("parallel",)), name="dn_pre")(proj, proj, proj, conv_w8, alog_l, dtb_l)


CPS = 4
CPS_SCAN = 8


def _split3(a):
    hi = a.astype(BF16)
    r1 = a - hi.astype(F32)
    mid = r1.astype(BF16)
    return hi, mid, (r1 - mid.astype(F32)).astype(BF16)


def _lane_select():
    r = np.arange(128)
    return jnp.asarray((r[None, :, None] == np.arange(8)[:, None, None]) * np.ones((1, 1, 128)), dtype=BF16)


def _lane_bcast(a3, sel):
    return _dot(a3[0], sel) + _dot(a3[1], sel) + _dot(a3[2], sel)


def _rowsum_b(z, ones_b):
    hi, lo = _split(z)
    return _dot(hi, ones_b) + _dot(lo, ones_b)


def _chunk_cumsum(bg, bgt, tri):
    return _split3(bg), _split3(_mmx(tri, bg)), _mmx_nt(bgt, tri)


def _chunk_common(bg3, gc3, gc_row, h, sel_ref):
    gcc = _lane_bcast(gc3, sel_ref[DN_HEADS + h])
    beta = _lane_bcast(bg3, sel_ref[h])
    gcr = gc_row[DN_HEADS + h:DN_HEADS + h + 1, :]
    ii = lax.broadcasted_iota(jnp.int32, (CHUNK, CHUNK), 0)
    jj = lax.broadcasted_iota(jnp.int32, (CHUNK, CHUNK), 1)
    incl = ii >= jj
    strict = ii > jj
    decay = jnp.exp(jnp.where(incl, gcc[:, 0:CHUNK] - gcr, NEG))
    gl = gcc[CHUNK - 1:CHUNK, :]
    return gcc, beta, incl, strict, decay, gl


def _dn_prep(qn, kn, v, bg, bgt, tri, sel):
    s = qn.shape[0]
    nc = s // CHUNK

    def body(q_ref, k_ref, v_ref, bg_ref, bgt_ref, tri_ref, sel_ref,
             u_ref, w_ref, qd_ref, kt_ref, attn_ref, t_ref, gl_ref):
        tri_v = tri_ref[...]
        ii = lax.broadcasted_iota(jnp.int32, (CHUNK, CHUNK), 0)
        jj = lax.broadcasted_iota(jnp.int32, (CHUNK, CHUNK), 1)
        eye = (ii == jj).astype(F32)

        def chain(cc, h, bg3, gc3, gc_row):
            rows = slice(CHUNK * cc, CHUNK * cc + CHUNK)
            lanes = slice(128 * h, 128 * h + 128)
            gcc, beta, incl, strict, decay, gl = _chunk_common(bg3, gc3, gc_row, h, sel_ref)
            yield
            q = q_ref[rows, lanes]
            k = k_ref[rows, lanes]
            vv = v_ref[rows, lanes]
            kb = k * beta
            egc = jnp.exp(gcc)
            a_mat = jnp.where(strict, _mm_nt(kb, k) * decay, 0.0)
            attn_ref[cc, h] = jnp.where(incl, _mm_nt(q, k) * decay, 0.0)
            qd_ref[rows, lanes] = q * egc
            kt_ref[rows, lanes] = k * jnp.exp(gl - gcc)
            gl_ref[cc, h] = jnp.exp(gl)
            yield
            p = -a_mat
            t = eye + p
            for _ in range(5):
                ps = _split(p)
                p = _mm3(ps, ps)
                yield
                t = t + _mm3(_split(t), _split(p))
                yield
            t_ref[cc, h] = t
            ts = _split(t)
            u_ref[rows, lanes] = _mm3(ts, _split(vv * beta))
            w_ref[rows, lanes] = _mm3(ts, _split(kb * egc))

        gens = []
        for cc in range(CPS):
            bgv = bg_ref[CHUNK * cc:CHUNK * cc + CHUNK, :]
            bg3, gc3, gc_row = _chunk_cumsum(bgv, bgt_ref[cc], tri_v)
            gens += [chain(cc, h, bg3, gc3, gc_row) for h in range(DN_HEADS)]
        _interleave(gens)

    rows_step = CPS * CHUNK
    big = pl.BlockSpec((rows_step, 512), lambda n: (n, 0))
    sq = pl.BlockSpec((CPS, DN_HEADS, CHUNK, CHUNK), lambda n: (n, 0, 0, 0))
    return pl.pallas_call(
        body, grid=(nc // CPS,),
        in_specs=[big, big, big, pl.BlockSpec((rows_step, 128), lambda n: (n, 0)),
                  pl.BlockSpec((CPS, 8, CHUNK), lambda n: (n, 0, 0)),
                  pl.BlockSpec((CHUNK, CHUNK), lambda n: (0, 0)),
                  pl.BlockSpec((8, 128, 128), lambda n: (0, 0, 0))],
        out_specs=[big, big, big, big, sq, sq, pl.BlockSpec((CPS, DN_HEADS, 1, 128), lambda n: (n, 0, 0, 0))],
        out_shape=[_sds((s, 512))] * 4 + [_sds((nc, DN_HEADS, CHUNK, CHUNK))] * 2 + [_sds((nc, DN_HEADS, 1, 128))],
        compiler_params=_cp(("parallel",)), name="dn_prep")(qn, kn, v, bg, bgt, tri, sel)


def _dn_scan(u, w, qd, kt, attn, gl):
    s = u.shape[0]
    nc = s // CHUNK

    def body(u_ref, w_ref, qd_ref, kt_ref, attn_ref, gl_ref, o_ref, vn_ref, sp_ref, st_ref):
        n = pl.program_id(0)

        @pl.when(n == 0)
        def _():
            st_ref[...] = jnp.zeros_like(st_ref)

        def chain(cc, h):
            rows = slice(CHUNK * cc, CHUNK * cc + CHUNK)
            lanes = slice(128 * h, 128 * h + 128)
            st = st_ref[h]
            sp_ref[cc, h] = st
            stb = st.astype(BF16)
            ws = _dot(w_ref[rows, lanes].astype(BF16), stb)
            qs = _dot(qd_ref[rows, lanes].astype(BF16), stb)
            yield
            vn = u_ref[rows, lanes] - ws
            vn_ref[rows, lanes] = vn
            vnb = vn.astype(BF16)
            o_ref[rows, lanes] = qs + _dot(attn_ref[cc, h].astype(BF16), vnb)
            st_ref[h] = st * gl_ref[cc, h] + _dot_tn(kt_ref[rows, lanes].astype(BF16), vnb)

        for cc in range(CPS_SCAN):
            _interleave([chain(cc, h) for h in range(DN_HEADS)])

    big = pl.BlockSpec((CPS_SCAN * CHUNK, 512), lambda n: (n, 0))
    return pl.pallas_call(
        body, grid=(nc // CPS_SCAN,),
        in_specs=[big, big, big, big,
                  pl.BlockSpec((CPS_SCAN, DN_HEADS, CHUNK, CHUNK), lambda n: (n, 0, 0, 0)),
                  pl.BlockSpec((CPS_SCAN, DN_HEADS, 1, 128), lambda n: (n, 0, 0, 0))],
        out_specs=[big, big, pl.BlockSpec((CPS_SCAN, DN_HEADS, DN_HD, DN_HD), lambda n: (n, 0, 0, 0))],
        out_shape=[_sds((s, 512)), _sds((s, 512)), _sds((nc, DN_HEADS, DN_HD, DN_HD))],
        scratch_shapes=[pltpu.VMEM((DN_HEADS, DN_HD, DN_HD), F32)],
        compiler_params=_cp(("arbitrary",)), name="dn_scan")(u, w, qd, kt, attn, gl)


R4 = PATTERNS[1][1]
R16 = PATTERNS[2][1]
TM = 512


def _pattern_spec(r, width=512):
    return pl.BlockSpec((r, TM // r, width), lambda i: (0, i, 0))


def _pattern_shape(s, r, dtype=F32, width=512):
    return _sds((r, s // r, width), dtype)


SLABS = pltpu.VMEM((4, TM, 128), F32)

HEAD_SLOT = 128 // ATT_HEADS


def _head_expand():
    src = np.arange(128)[:, None]
    dst = np.arange(512)[None, :]
    return jnp.asarray((src == (dst // ATT_HD) * HEAD_SLOT).astype(np.float32), dtype=BF16)


def _head_compact():
    src = np.arange(512)[:, None]
    dst = np.arange(128)[None, :]
    return jnp.asarray((src // ATT_HD == dst // HEAD_SLOT).astype(np.float32), dtype=BF16)


def _to_patterns(val, dsts, scr):
    for c in range(val.shape[1] // 128):
        lanes = slice(128 * c, 128 * c + 128)
        scr[c] = val[:, lanes]
        for dst_ref, r in dsts:
            for a in range(r):
                dst_ref[a, :, lanes] = scr[c, pl.ds(a, TM // r, stride=r), :].astype(dst_ref.dtype)


def _from_pattern(src_ref, r, scr):
    n_slab = src_ref.shape[2] // 128
    for c in range(n_slab):
        for a in range(r):
            scr[c, pl.ds(a, TM // r, stride=r), :] = src_ref[a, :, 128 * c:128 * c + 128]
    return jnp.concatenate([scr[c] for c in range(n_slab)], axis=1) if n_slab > 1 else scr[0]


def _att_pre(proj, qw_t, kw_t, bd64):
    s = proj.shape[0]

    def body(q_ref, k_ref, v_ref, qw_ref, kw_ref, bd_ref,
             q1_ref, k1_ref, v1_ref, q4_ref, k4_ref, v4_ref, q16_ref, k16_ref, v16_ref, scr):
        bd = bd_ref[...]
        q = q_ref[...]
        k = k_ref[...]
        qn = q * _seg_rstd(q, bd, ATT_HD) * qw_ref[...] * (ATT_HD ** -0.5)
        kn = k * _seg_rstd(k, bd, ATT_HD) * kw_ref[...]
        q1_ref[...] = qn.astype(BF16)
        k1_ref[...] = kn.astype(BF16)
        v1_ref[...] = v_ref[...].astype(BF16)
        _to_patterns(qn, ((q4_ref, R4), (q16_ref, R16)), scr)
        _to_patterns(kn, ((k4_ref, R4), (k16_ref, R16)), scr)
        _to_patterns(v_ref[...], ((v4_ref, R4), (v16_ref, R16)), scr)

    row = pl.BlockSpec((1, 512), lambda i: (0, 0))
    tok = pl.BlockSpec((TM, 512), lambda i: (i, 0))
    return pl.pallas_call(
        body, grid=(s // TM,),
        in_specs=[pl.BlockSpec((TM, 512), lambda i: (i, COL_ATT_Q)),
                  pl.BlockSpec((TM, 512), lambda i: (i, COL_ATT_K)),
                  pl.BlockSpec((TM, 512), lambda i: (i, COL_ATT_V)),
                  row, row, pl.BlockSpec((512, 128), lambda i: (0, 0))],
        out_specs=[tok] * 3 + [_pattern_spec(R4)] * 3 + [_pattern_spec(R16)] * 3,
        out_shape=[_sds((s, 512), BF16)] * 3 + [_pattern_shape(s, R4, BF16)] * 3 + [_pattern_shape(s, R16, BF16)] * 3,
        scratch_shapes=[SLABS],
        compiler_params=_cp(("parallel",)), name="att_pre")(proj, proj, proj, qw_t, kw_t, bd64)


def _bias_fwd(rel_bias, buckets):
    def body(rb_ref, bk_ref, o_ref):
        for p in range(len(PATTERNS)):
            bk = bk_ref[p]
            for h in range(ATT_HEADS):
                acc = jnp.zeros((BLK, 2 * BLK), F32)
                for b in range(N_BUCKETS):
                    acc = jnp.where(bk == b, rb_ref[h, b], acc)
                o_ref[p, h] = acc

    return pl.pallas_call(
        body,
        in_specs=[pl.BlockSpec(memory_space=pltpu.SMEM), pl.BlockSpec(memory_space=pltpu.VMEM)],
        out_specs=pl.BlockSpec(memory_space=pltpu.VMEM),
        out_shape=_sds((len(PATTERNS), ATT_HEADS, BLK, 2 * BLK)),
        compiler_params=_cp(), name="bias_fwd")(rel_bias, buckets)


def _bias_bwd(ds_accs, buckets):
    def body(ds0_ref, ds1_ref, ds2_ref, bk_ref, o_ref):
        for h in range(ATT_HEADS):
            for b in range(N_BUCKETS):
                tot = jnp.zeros((), F32)
                for p, ds_ref in enumerate((ds0_ref, ds1_ref, ds2_ref)):
                    tot = tot + jnp.sum(jnp.where(bk_ref[p] == b, ds_ref[h], 0.0))
                o_ref[h, b] = tot

    return pl.pallas_call(
        body,
        in_specs=[pl.BlockSpec(memory_space=pltpu.VMEM)] * 4,
        out_specs=pl.BlockSpec(memory_space=pltpu.SMEM),
        out_shape=_sds((ATT_HEADS, N_BUCKETS)),
        compiler_params=_cp(), name="bias_bwd")(*ds_accs, buckets)


QB_FWD = 2
QB_BWD = 4


def _att_masks(has_prev):
    qi = lax.broadcasted_iota(jnp.int32, (BLK, BLK), 0)
    kj = lax.broadcasted_iota(jnp.int32, (BLK, BLK), 1)
    lane = lax.broadcasted_iota(jnp.int32, (BLK, 2 * ATT_HD), 1)
    return jnp.logical_and(kj >= qi, has_prev), kj <= qi, lane < ATT_HD


def _head_lanes(h):
    half = h % 2
    return slice(ATT_HD * h, ATT_HD * h + ATT_HD), slice(ATT_HD * half, ATT_HD * half + ATT_HD)


def _att_scores(qm, kp2, kc2, bias_h, mask_prev, mask_cur):
    s_prev = jnp.where(mask_prev, _dot_nt(qm, kp2) + bias_h[:, :BLK], NEG)
    s_cur = jnp.where(mask_cur, _dot_nt(qm, kc2) + bias_h[:, BLK:], NEG)
    return s_prev, s_cur


def _att_fwd(q, k, v, v_col, bias, p_idx, r, name):
    QB = QB_FWD
    s = q.shape[0]
    nblk = s // BLK
    nseq = nblk // r

    def body(q_ref, kp_ref, kc_ref, vp_ref, vc_ref, b_ref, o_ref, lse_ref):
        j = pl.program_id(0)

        def head(h, rows, masks, q2, kp2, kc2, vp2, vc2):
            mask_prev, mask_cur, lo_half = masks
            out_l, pair_l = _head_lanes(h)
            sel = lo_half if h % 2 == 0 else jnp.logical_not(lo_half)
            qm = jnp.where(sel, q2, jnp.zeros_like(q2))
            s_prev, s_cur = _att_scores(qm, kp2, kc2, b_ref[0, h], mask_prev, mask_cur)
            yield
            m = jnp.maximum(jnp.max(s_prev, axis=1, keepdims=True), jnp.max(s_cur, axis=1, keepdims=True))
            p_prev = jnp.exp(s_prev - m)
            p_cur = jnp.exp(s_cur - m)
            l = jnp.sum(p_prev, axis=1, keepdims=True) + jnp.sum(p_cur, axis=1, keepdims=True)
            yield
            o2 = _dot(p_prev.astype(BF16), vp2) + _dot(p_cur.astype(BF16), vc2)
            o_ref[rows, out_l] = (o2 * (1.0 / l))[:, pair_l]
            lse_ref[rows, HEAD_SLOT * h:HEAD_SLOT * h + HEAD_SLOT] = jnp.broadcast_to(m + jnp.log(l), (BLK, HEAD_SLOT))

        for sub in range(QB):
            rows = slice(BLK * sub, BLK * sub + BLK)
            before = slice(BLK * (sub - 1), BLK * sub)
            masks = _att_masks(((QB * j + sub) % nseq) != 0)
            gens = []
            for pp in range(ATT_HEADS // 2):
                lanes = slice(128 * pp, 128 * pp + 128)
                kp = kp_ref[:, lanes] if sub == 0 else kc_ref[before, lanes]
                vp = vp_ref[:, lanes] if sub == 0 else vc_ref[before, lanes]
                slabs = (q_ref[rows, lanes], kp, kc_ref[rows, lanes], vp, vc_ref[rows, lanes])
                gens += [head(2 * pp, rows, masks, *slabs), head(2 * pp + 1, rows, masks, *slabs)]
            _interleave(gens)

    cur = pl.BlockSpec((QB * BLK, 512), lambda j: (j, 0))
    prev = pl.BlockSpec((BLK, 512), lambda j: (jnp.maximum(QB * j - 1, 0), 0))
    vcur = pl.BlockSpec((QB * BLK, 512), lambda j: (j, v_col))
    vprev = pl.BlockSpec((BLK, 512), lambda j: (jnp.maximum(QB * j - 1, 0), v_col))
    return pl.pallas_call(
        body, grid=(nblk // QB,),
        in_specs=[cur, prev, cur, vprev, vcur,
                  pl.BlockSpec((1, ATT_HEADS, BLK, 2 * BLK), lambda j: (p_idx, 0, 0, 0))],
        out_specs=[cur, pl.BlockSpec((QB * BLK, 128), lambda j: (j, 0))],
        out_shape=[_sds((s, 512)), _sds((s, 128))],
        compiler_params=_cp(("parallel",)), name=name)(q, k, k, v, v, bias)


def _post_fwd(o_dn, proj, o_pats, lse_pats, dnw_t, bd128):
    s = o_dn.shape[0]

    def body(o_ref, z_ref, gate_ref, o1_ref, o4_ref, o16_ref, s1_ref, s4_ref, s16_ref, wn_ref, bd_ref, ex_ref,
             mixed_ref, oatt_ref, l1_ref, l4_ref, l16_ref, scr_a, scr_b, scr_c, scr_d):
        o = o_ref[...]
        z = z_ref[...]
        rstd = _seg_rstd(o, bd_ref[...], DN_HD)
        y_dn = o * rstd * wn_ref[...] * (z * _sigmoid(z))
        mixed_ref[:, 0:512] = y_dn.astype(BF16)
        lses = (s1_ref[...], _from_pattern(s4_ref, R4, scr_a), _from_pattern(s16_ref, R16, scr_b))
        m = jnp.maximum(jnp.maximum(lses[0], lses[1]), lses[2])
        tot = jnp.exp(lses[0] - m) + jnp.exp(lses[1] - m) + jnp.exp(lses[2] - m)
        big_l = m + jnp.log(tot)
        l1_ref[...] = big_l
        _to_patterns(big_l, ((l4_ref, R4), (l16_ref, R16)), scr_a)
        ex = ex_ref[...]
        outs = (o1_ref[...], _from_pattern(o4_ref, R4, scr_c), _from_pattern(o16_ref, R16, scr_d))
        acc = jnp.zeros_like(o)
        for lse_p, o_p in zip(lses, outs):
            acc = acc + _lane_bcast(_split3(jnp.exp(lse_p - big_l)), ex) * o_p
        gate = gate_ref[...]
        oatt_ref[...] = acc
        mixed_ref[:, 512:1024] = (acc * (gate * _sigmoid(gate))).astype(BF16)

    blk = pl.BlockSpec((TM, 512), lambda i: (i, 0))
    cblk = pl.BlockSpec((TM, 128), lambda i: (i, 0))
    p4, p16 = _pattern_spec(R4), _pattern_spec(R16)
    c4, c16 = _pattern_spec(R4, 128), _pattern_spec(R16, 128)
    return pl.pallas_call(
        body, grid=(s // TM,),
        in_specs=[blk, pl.BlockSpec((TM, 512), lambda i: (i, COL_Z)),
                  pl.BlockSpec((TM, 512), lambda i: (i, COL_GATE)), blk, p4, p16, cblk, c4, c16,
                  pl.BlockSpec((1, 512), lambda i: (0, 0)), pl.BlockSpec((512, 128), lambda i: (0, 0)),
                  pl.BlockSpec((128, 512), lambda i: (0, 0))],
        out_specs=[pl.BlockSpec((TM, D_MODEL), lambda i: (i, 0)), blk, cblk, c4, c16],
        out_shape=[_sds((s, D_MODEL), BF16), _sds((s, 512)), _sds((s, 128)), _pattern_shape(s, R4, F32, 128),
                   _pattern_shape(s, R16, F32, 128)],
        scratch_shapes=[SLABS] * 4,
        compiler_params=_cp(("parallel",)), name="post_fwd")(o_dn, proj, proj, *o_pats, *lse_pats, dnw_t, bd128,
                                                              _head_expand())


def _out_fwd(x, mixed, w_out_bf, tgt):
    s = x.shape[0]
    tm = 512

    def body(x_ref, m_ref, w_ref, t_ref, dy_ref, loss_ref):
        i = pl.program_id(0)

        @pl.when(i == 0)
        def _():
            loss_ref[...] = jnp.zeros_like(loss_ref)

        y = x_ref[...] + jnp.dot(m_ref[...], w_ref[...], preferred_element_type=F32)
        err = y - t_ref[...]
        dy_ref[...] = err * (1.0 / D_MODEL)
        part = 0.5 * jnp.sum(jnp.mean(err * err, axis=-1, keepdims=True), axis=0, keepdims=True)
        loss_ref[...] = loss_ref[...] + part

    blk = pl.BlockSpec((tm, D_MODEL), lambda i: (i, 0))
    return pl.pallas_call(
        body, grid=(s // tm,),
        in_specs=[blk, blk, pl.BlockSpec((D_MODEL, D_MODEL), lambda i: (0, 0)), blk],
        out_specs=[blk, pl.BlockSpec((8, 128), lambda i: (0, 0))],
        out_shape=[_sds((s, D_MODEL)), _sds((8, 128))],
        compiler_params=_cp(("arbitrary",)), name="out_fwd")(x, mixed, w_out_bf, tgt)


def _out_bwd(dy, mixed, w_out_bf):
    s = dy.shape[0]
    tm = 512

    def body(dy_ref, m_ref, w_ref, dm_ref, dw_ref):
        i = pl.program_id(0)

        @pl.when(i == 0)
        def _():
            dw_ref[...] = jnp.zeros_like(dw_ref)

        dyb = dy_ref[...].astype(BF16)
        dm_ref[...] = lax.dot_general(dyb, w_ref[...], (((1,), (1,)), ((), ())), preferred_element_type=F32)
        dw_ref[...] = dw_ref[...] + lax.dot_general(m_ref[...], dyb, (((0,), (0,)), ((), ())),
                                                    preferred_element_type=F32)

    blk = pl.BlockSpec((tm, D_MODEL), lambda i: (i, 0))
    full = pl.BlockSpec((D_MODEL, D_MODEL), lambda i: (0, 0))
    return pl.pallas_call(
        body, grid=(s // tm,), in_specs=[blk, blk, full], out_specs=[blk, full],
        out_shape=[_sds((s, D_MODEL)), _sds((D_MODEL, D_MODEL))],
        compiler_params=_cp(("arbitrary",)), name="out_bwd")(dy, mixed, w_out_bf)


def _post_bwd(dmixed, o_dn, proj, o_att, dnw_t, bd128):
    s = o_dn.shape[0]
    tm = TM

    def body(ddn_ref, datt_ref, o_ref, z_ref, gate_ref, oatt_ref, wn_ref, bd128_ref, hc_ref,
             do_ref, dz_ref, dgate_ref, doatt_ref, do4_ref, do16_ref, delta_ref, dl4_ref, dl16_ref, dnw_ref, scr):
        i = pl.program_id(0)

        @pl.when(i == 0)
        def _():
            dnw_ref[...] = jnp.zeros_like(dnw_ref)

        bd128v = bd128_ref[...]
        o = o_ref[...]
        z = z_ref[...]
        wn = wn_ref[...]
        dy = ddn_ref[...]
        rstd = _seg_rstd(o, bd128v, DN_HD)
        nrm = o * rstd
        sz = _sigmoid(z)
        dz_ref[...] = (dy * nrm * wn * _silu_grad(z, sz)).astype(BF16)
        dn = dy * z * sz
        gw = dn * wn
        do_ref[...] = rstd * (gw - nrm * _seg_expand(_seg_mean(gw * nrm, bd128v, DN_HD), bd128v, DN_HD))
        colsum = jnp.sum(dn * nrm, axis=0, keepdims=True)
        fold = colsum[:, 0:128] + colsum[:, 128:256] + colsum[:, 256:384] + colsum[:, 384:512]
        dnw_ref[...] = dnw_ref[...] + fold
        dya = datt_ref[...]
        gate = gate_ref[...]
        oatt = oatt_ref[...]
        sg = _sigmoid(gate)
        dgate_ref[...] = (dya * oatt * _silu_grad(gate, sg)).astype(BF16)
        doa = dya * gate * sg
        doatt_ref[...] = doa.astype(BF16)
        delta = _segsum(doa * oatt, hc_ref[...])
        delta_ref[...] = delta
        _to_patterns(doa, ((do4_ref, R4), (do16_ref, R16)), scr)
        _to_patterns(delta, ((dl4_ref, R4), (dl16_ref, R16)), scr)

    blk = pl.BlockSpec((tm, 512), lambda i: (i, 0))
    cblk = pl.BlockSpec((tm, 128), lambda i: (i, 0))
    p4, p16 = _pattern_spec(R4), _pattern_spec(R16)
    c4, c16 = _pattern_spec(R4, 128), _pattern_spec(R16, 128)
    return pl.pallas_call(
        body, grid=(s // tm,),
        in_specs=[blk, pl.BlockSpec((tm, 512), lambda i: (i, 1)), blk,
                  pl.BlockSpec((tm, 512), lambda i: (i, COL_Z)), pl.BlockSpec((tm, 512), lambda i: (i, COL_GATE)),
                  blk, pl.BlockSpec((1, 512), lambda i: (0, 0)), pl.BlockSpec((512, 128), lambda i: (0, 0)),
                  pl.BlockSpec((512, 128), lambda i: (0, 0))],
        out_specs=[blk, blk, blk, blk, p4, p16, cblk, c4, c16, pl.BlockSpec((8, 128), lambda i: (0, 0))],
        out_shape=[_sds((s, 512))] + [_sds((s, 512), BF16)] * 3 + [_pattern_shape(s, R4, BF16),
                                          _pattern_shape(s, R16, BF16), _sds((s, 128)),
                                          _pattern_shape(s, R4, F32, 128), _pattern_shape(s, R16, F32, 128),
                                          _sds((8, 128))],
        scratch_shapes=[SLABS],
        compiler_params=_cp(("arbitrary",)), name="post_bwd")(dmixed, dmixed, o_dn, proj, proj, o_att, dnw_t,
                                                               bd128, _head_compact())


def _att_bwd(q, k, v, v_col, do, big_l, delta, bias, p_idx, r, name):
    QB = QB_BWD
    s = q.shape[0]
    nblk = s // BLK
    nseq = nblk // r
    nstep = nblk // QB

    def body(q_ref, kp_ref, kc_ref, vp_ref, vc_ref, do_ref, l_ref, dl_ref, b_ref,
             dq_ref, dk_ref, dv_ref, ds_ref, dkc_ref, dvc_ref):
        j = pl.program_id(0)

        @pl.when(j == 0)
        def _():
            dkc_ref[...] = jnp.zeros_like(dkc_ref)
            dvc_ref[...] = jnp.zeros_like(dvc_ref)
            ds_ref[...] = jnp.zeros_like(ds_ref)

        @pl.when(j < nstep)
        def _():
            def head(h, sub, masks, q2, do2, kp2, kc2, vp2, vc2):
                mask_prev, mask_cur, lo_half = masks
                rows = slice(BLK * sub, BLK * sub + BLK)
                out_l, pair_l = _head_lanes(h)
                sel = lo_half if h % 2 == 0 else jnp.logical_not(lo_half)
                qm = jnp.where(sel, q2, jnp.zeros_like(q2))
                dom = jnp.where(sel, do2, jnp.zeros_like(do2))
                s_prev, s_cur = _att_scores(qm, kp2, kc2, b_ref[0, h], mask_prev, mask_cur)
                dp_prev = _dot_nt(dom, vp2)
                dp_cur = _dot_nt(dom, vc2)
                yield
                lh = l_ref[rows, HEAD_SLOT * h:HEAD_SLOT * h + 1]
                dh = dl_ref[rows, HEAD_SLOT * h:HEAD_SLOT * h + 1]
                p_prev = jnp.exp(s_prev - lh)
                p_cur = jnp.exp(s_cur - lh)
                ds_prev = p_prev * (dp_prev - dh)
                ds_cur = p_cur * (dp_cur - dh)
                ds_ref[h, :, 0:BLK] = ds_ref[h, :, 0:BLK] + ds_prev
                ds_ref[h, :, BLK:2 * BLK] = ds_ref[h, :, BLK:2 * BLK] + ds_cur
                dsb_prev, dsb_cur = ds_prev.astype(BF16), ds_cur.astype(BF16)
                pb_prev, pb_cur = p_prev.astype(BF16), p_cur.astype(BF16)
                yield
                dq_ref[rows, out_l] = (_dot(dsb_prev, kp2) + _dot(dsb_cur, kc2))[:, pair_l]
                dk_prev = _dot_tn(dsb_prev, q2)[:, pair_l]
                dv_prev = _dot_tn(pb_prev, do2)[:, pair_l]
                if sub == 0:
                    last = slice(BLK * (QB - 1), BLK * QB)
                    dk_ref[last, out_l] = dkc_ref[last, out_l] + dk_prev
                    dv_ref[last, out_l] = dvc_ref[last, out_l] + dv_prev
                else:
                    before = slice(BLK * (sub - 1), BLK * sub)
                    dkc_ref[before, out_l] = dkc_ref[before, out_l] + dk_prev
                    dvc_ref[before, out_l] = dvc_ref[before, out_l] + dv_prev
                yield
                dkc_ref[rows, out_l] = _dot_tn(dsb_cur, q2)[:, pair_l]
                dvc_ref[rows, out_l] = _dot_tn(pb_cur, do2)[:, pair_l]

            done = slice(0, BLK * (QB - 1))
            dk_ref[done, :] = dkc_ref[done, :]
            dv_ref[done, :] = dvc_ref[done, :]
            for sub in range(QB):
                rows = slice(BLK * sub, BLK * sub + BLK)
                before = slice(BLK * (sub - 1), BLK * sub)
                masks = _att_masks(((QB * j + sub) % nseq) != 0)
                gens = []
                for pp in range(ATT_HEADS // 2):
                    lanes = slice(128 * pp, 128 * pp + 128)
                    kp = kp_ref[:, lanes] if sub == 0 else kc_ref[before, lanes]
                    vp = vp_ref[:, lanes] if sub == 0 else vc_ref[before, lanes]
                    slabs = (q_ref[rows, lanes], do_ref[rows, lanes], kp, kc_ref[rows, lanes], vp,
                             vc_ref[rows, lanes])
                    gens += [head(2 * pp, sub, masks, *slabs), head(2 * pp + 1, sub, masks, *slabs)]
                _interleave(gens)

        @pl.when(j == nstep)
        def _():
            dk_ref[...] = dkc_ref[...]
            dv_ref[...] = dvc_ref[...]

    last_step = nstep - 1
    cur = pl.BlockSpec((QB * BLK, 512), lambda j: (jnp.minimum(j, last_step), 0))
    compact = pl.BlockSpec((QB * BLK, 128), lambda j: (jnp.minimum(j, last_step), 0))
    lag = pl.BlockSpec((QB * BLK, 512), lambda j: (jnp.clip(j - 1, 0, last_step), 0))
    prev = pl.BlockSpec((BLK, 512), lambda j: (jnp.clip(QB * j - 1, 0, nblk - 1), 0))
    vcur = pl.BlockSpec((QB * BLK, 512), lambda j: (jnp.minimum(j, last_step), v_col))
    vprev = pl.BlockSpec((BLK, 512), lambda j: (jnp.clip(QB * j - 1, 0, nblk - 1), v_col))
    return pl.pallas_call(
        body, grid=(nstep + 1,),
        in_specs=[cur, prev, cur, vprev, vcur, cur, compact, compact,
                  pl.BlockSpec((1, ATT_HEADS, BLK, 2 * BLK), lambda j: (p_idx, 0, 0, 0))],
        out_specs=[cur, lag, lag, pl.BlockSpec((ATT_HEADS, BLK, 2 * BLK), lambda j: (0, 0, 0))],
        out_shape=[_sds((s, 512))] * 3 + [_sds((ATT_HEADS, BLK, 2 * BLK))],
        scratch_shapes=[pltpu.VMEM((QB * BLK, 512), F32), pltpu.VMEM((QB * BLK, 512), F32)],
        compiler_params=_cp(("arbitrary",)), name=name)(q, k, k, v, v, do, big_l, delta, bias)


def _att_pre_bwd(dq_pats, dk_pats, dv_pats, proj, qw_t, kw_t, bd64):
    s = proj.shape[0]
    tm = TM

    def body(dq1_ref, dq4_ref, dq16_ref, dk1_ref, dk4_ref, dk16_ref, dv1_ref, dv4_ref, dv16_ref,
             q_ref, k_ref, qw_ref, kw_ref, bd_ref,
             dqr_ref, dkr_ref, dvr_ref, dqw_ref, dkw_ref, scr4, scr16):
        i = pl.program_id(0)

        @pl.when(i == 0)
        def _():
            dqw_ref[...] = jnp.zeros_like(dqw_ref)
            dkw_ref[...] = jnp.zeros_like(dkw_ref)

        bd = bd_ref[...]

        def total(d1_ref, d4_ref, d16_ref):
            return d1_ref[...] + _from_pattern(d4_ref, R4, scr4) + _from_pattern(d16_ref, R16, scr16)

        def one(d_refs, x_ref, w_ref, scale, dx_ref, dw_ref):
            dy = total(*d_refs) * scale
            x = x_ref[...]
            rstd = _seg_rstd(x, bd, ATT_HD)
            nrm = x * rstd
            dw_ref[...] = dw_ref[...] + jnp.sum(dy * nrm, axis=0, keepdims=True)
            g = dy * w_ref[...]
            dx_ref[...] = (rstd * (g - nrm * _seg_expand(_seg_mean(g * nrm, bd, ATT_HD), bd, ATT_HD))).astype(BF16)

        one((dq1_ref, dq4_ref, dq16_ref), q_ref, qw_ref, ATT_HD ** -0.5, dqr_ref, dqw_ref)
        one((dk1_ref, dk4_ref, dk16_ref), k_ref, kw_ref, 1.0, dkr_ref, dkw_ref)
        dvr_ref[...] = total(dv1_ref, dv4_ref, dv16_ref).astype(BF16)

    blk = pl.BlockSpec((tm, 512), lambda i: (i, 0))
    pats = [blk, _pattern_spec(R4), _pattern_spec(R16)]
    row = pl.BlockSpec((1, 512), lambda i: (0, 0))
    acc = pl.BlockSpec((8, 512), lambda i: (0, 0))
    return pl.pallas_call(
        body, grid=(s // tm,),
        in_specs=pats * 3 + [pl.BlockSpec((tm, 512), lambda i: (i, COL_ATT_Q)),
                             pl.BlockSpec((tm, 512), lambda i: (i, COL_ATT_K)), row, row,
                             pl.BlockSpec((512, 128), lambda i: (0, 0))],
        out_specs=[blk, blk, blk, acc, acc],
        out_shape=[_sds((s, 512), BF16)] * 3 + [_sds((8, 512))] * 2,
        scratch_shapes=[SLABS] * 2,
        compiler_params=_cp(("arbitrary",)), name="att_pre_bwd")(*dq_pats, *dk_pats, *dv_pats, proj, proj,
                                                                  qw_t, kw_t, bd64)


def _dn_scan_bwd(do, sp, qd, kt, w, vn, attn, gl):
    s = do.shape[0]
    nc = s // CHUNK

    def body(do_ref, sp_ref, qd_ref, kt_ref, w_ref, vn_ref, attn_ref, gl_ref,
             du_ref, dqd_ref, dkt_ref, dw_ref, dattn_ref, dgl_ref, ds_ref):
        n = pl.program_id(0)

        @pl.when(n == 0)
        def _():
            ds_ref[...] = jnp.zeros_like(ds_ref)

        def chain(cc, h):
            rows = slice(CHUNK * cc, CHUNK * cc + CHUNK)
            lanes = slice(128 * h, 128 * h + 128)
            dsn = ds_ref[h]
            st = sp_ref[cc, h]
            dsb, stb = dsn.astype(BF16), st.astype(BF16)
            dob = do_ref[rows, lanes].astype(BF16)
            vnb = vn_ref[rows, lanes].astype(BF16)
            dvn = _dot_tn(attn_ref[cc, h].astype(BF16), dob) + _dot(kt_ref[rows, lanes].astype(BF16), dsb)
            du_ref[rows, lanes] = dvn
            dqd_ref[rows, lanes] = _dot_nt(dob, stb)
            dattn_ref[cc, h] = _dot_nt(dob, vnb)
            dkt_ref[rows, lanes] = _dot_nt(vnb, dsb)
            tot = jnp.sum(jnp.sum(st * dsn, axis=1, keepdims=True), axis=0, keepdims=True)
            dgl_ref[cc, h] = jnp.broadcast_to(tot, (1, 128))
            qdo = _dot_tn(qd_ref[rows, lanes].astype(BF16), dob)
            yield
            dvb = dvn.astype(BF16)
            dw_ref[rows, lanes] = -_dot_nt(dvb, stb)
            ds_ref[h] = qdo + dsn * gl_ref[cc, h] - _dot_tn(w_ref[rows, lanes].astype(BF16), dvb)

        for cc in reversed(range(CPS_SCAN)):
            _interleave([chain(cc, h) for h in range(DN_HEADS)])

    nsteps = nc // CPS_SCAN
    big = pl.BlockSpec((CPS_SCAN * CHUNK, 512), lambda n: (nsteps - 1 - n, 0))
    sq = pl.BlockSpec((CPS_SCAN, DN_HEADS, CHUNK, CHUNK), lambda n: (nsteps - 1 - n, 0, 0, 0))
    glb = pl.BlockSpec((CPS_SCAN, DN_HEADS, 1, 128), lambda n: (nsteps - 1 - n, 0, 0, 0))
    return pl.pallas_call(
        body, grid=(nsteps,),
        in_specs=[big, pl.BlockSpec((CPS_SCAN, DN_HEADS, DN_HD, DN_HD), lambda n: (nsteps - 1 - n, 0, 0, 0)),
                  big, big, big, big, sq, glb],
        out_specs=[big, big, big, big, sq, glb],
        out_shape=[_sds((s, 512))] * 4 + [_sds((nc, DN_HEADS, CHUNK, CHUNK)), _sds((nc, DN_HEADS, 1, 128))],
        scratch_shapes=[pltpu.VMEM((DN_HEADS, DN_HD, DN_HD), F32)],
        compiler_params=_cp(("arbitrary",)), name="dn_scan_bwd")(do, sp, qd, kt, w, vn, attn, gl)


def _dn_prep_bwd(qn, kn, v, bg, bgt, tri, sel, t_inv, attn, u, w, du, dw, dqd, dkt, dattn, dgl):
    s = qn.shape[0]
    nc = s // CHUNK

    def body(q_ref, k_ref, v_ref, bg_ref, bgt_ref, tri_ref, sel_ref, t_ref, attn_ref, u_ref, w_ref,
             du_ref, dw_ref, dqd_ref, dkt_ref, dattn_ref, dgl_ref,
             dq_ref, dk_ref, dv_ref, dbg_ref):
        tri_v = tri_ref[...]
        lane = lax.broadcasted_iota(jnp.int32, (CHUNK, 128), 1)
        rowi = lax.broadcasted_iota(jnp.int32, (CHUNK, 128), 0)
        ones_b = jnp.ones((CHUNK, 128), BF16)
        ones_sq = jnp.ones((128, 128), BF16)
        parts = [[] for _ in range(CPS)]

        def chain(cc, h, bg3, gc3, gc_row):
            rows = slice(CHUNK * cc, CHUNK * cc + CHUNK)
            lanes = slice(128 * h, 128 * h + 128)
            gcc, beta, incl, strict, decay, gl = _chunk_common(bg3, gc3, gc_row, h, sel_ref)
            yield
            q = q_ref[rows, lanes]
            k = k_ref[rows, lanes]
            vv = v_ref[rows, lanes]
            ts = _split(t_ref[cc, h])
            egc = jnp.exp(gcc)
            kb = k * beta
            a_mat = jnp.where(strict, _mm_nt(kb, k) * decay, 0.0)
            dvb = _mm3_tn(ts, _split(du_ref[rows, lanes]))
            dkbg = _mm3_tn(ts, _split(dw_ref[rows, lanes]))
            yield
            d_a = jnp.where(strict, -(_mm_nt(dvb, u_ref[rows, lanes]) + _mm_nt(dkbg, w_ref[rows, lanes])), 0.0)
            d_m = d_a * decay
            dattn_m = jnp.where(incl, dattn_ref[cc, h], 0.0)
            dqk = dattn_m * decay
            e_hi, e_lo = _split(d_a * a_mat + dattn_m * attn_ref[cc, h])
            yield
            dkb = _mm(d_m, k)
            dk = _mm_tn(d_m, kb) + _mm_tn(dqk, q)
            dq = _mm(dqk, k)
            e_colsum = _dot_tn(e_hi, ones_b) + _dot_tn(e_lo, ones_b)
            e_rowsum = _dot(e_hi, ones_b) + _dot(e_lo, ones_b)
            dqd = dqd_ref[rows, lanes]
            dkt = dkt_ref[rows, lanes]
            s_dqd = _rowsum_b(dqd * q, ones_sq)
            s_dkt = _rowsum_b(dkt * k, ones_sq)
            rk = _rowsum_b(dkbg * k, ones_sq)
            s_dkb = _rowsum_b(dkb * k, ones_sq)
            s_dvb = _rowsum_b(dvb * vv, ones_sq)
            yield
            tail = jnp.exp(gl - gcc)
            r = s_dkt * tail
            dgl_tot = jnp.sum(r, axis=0, keepdims=True) + dgl_ref[cc, h] * jnp.exp(gl)
            dgc = e_rowsum - e_colsum + s_dqd * egc - r + rk * beta * egc
            dgc = dgc + jnp.where(rowi == CHUNK - 1, dgl_tot, 0.0)
            dq_ref[rows, lanes] = dq + dqd * egc
            dk_ref[rows, lanes] = dk + dkt * tail + dkbg * (beta * egc) + dkb * beta
            dv_ref[rows, lanes] = dvb * beta
            parts[cc].append((h, dgc, rk * egc + s_dkb + s_dvb))

        gens = []
        for cc in range(CPS):
            bgv = bg_ref[CHUNK * cc:CHUNK * cc + CHUNK, :]
            bg3, gc3, gc_row = _chunk_cumsum(bgv, bgt_ref[cc], tri_v)
            gens += [chain(cc, h, bg3, gc3, gc_row) for h in range(DN_HEADS)]
        _interleave(gens)
        for cc in range(CPS):
            dgc_mat = jnp.zeros((CHUNK, 128), F32)
            dbeta_mat = jnp.zeros((CHUNK, 128), F32)
            for h, dgc, dbeta in parts[cc]:
                dgc_mat = dgc_mat + jnp.where(lane == DN_HEADS + h, dgc, 0.0)
                dbeta_mat = dbeta_mat + jnp.where(lane == h, dbeta, 0.0)
            dbg_ref[CHUNK * cc:CHUNK * cc + CHUNK, :] = _mmx_tn(tri_v, dgc_mat) + dbeta_mat

    big = pl.BlockSpec((CPS * CHUNK, 512), lambda n: (n, 0))
    sq = pl.BlockSpec((CPS, DN_HEADS, CHUNK, CHUNK), lambda n: (n, 0, 0, 0))
    glb = pl.BlockSpec((CPS, DN_HEADS, 1, 128), lambda n: (n, 0, 0, 0))
    small = pl.BlockSpec((CPS * CHUNK, 128), lambda n: (n, 0))
    return pl.pallas_call(
        body, grid=(nc // CPS,),
        in_specs=[big, big, big, small, pl.BlockSpec((CPS, 8, CHUNK), lambda n: (n, 0, 0)),
                  pl.BlockSpec((CHUNK, CHUNK), lambda n: (0, 0)),
                  pl.BlockSpec((8, 128, 128), lambda n: (0, 0, 0)), sq, sq, big, big,
                  big, big, big, big, sq, glb],
        out_specs=[big, big, big, small],
        out_shape=[_sds((s, 512))] * 3 + [_sds((s, 128))],
        compiler_params=_cp(("parallel",)), name="dn_prep_bwd")(qn, kn, v, bg, bgt, tri, sel, t_inv, attn, u, w,
                                                                  du, dw, dqd, dkt, dattn, dgl)


def _dn_pre_bwd(dqn, dkn, dv, dbg, proj, conv_w8, alog_l, dtb_l):
    s = proj.shape[0]
    tr = 512
    nh = tr // 8

    def body(dq_ref, dk_ref, dv_ref, dbg_ref, u_ref, halo_ref, ba_ref, w_ref, al_ref, dt_ref,
             dy_ref, dba_ref, dsm_ref):
        i = pl.program_id(0)

        @pl.when(i == 0)
        def _():
            dsm_ref[...] = jnp.zeros_like(dsm_ref)

        keep = (i > 0).astype(F32)
        for c in range(12):
            lanes = slice(128 * c, 128 * c + 128)
            y = _conv_group(u_ref[:, lanes], halo_ref[:, lanes] * keep, w_ref, c)
            sg = _sigmoid(y)
            sv = y * sg
            if c < 8:
                rs = lax.rsqrt(jnp.sum(sv * sv, axis=1, keepdims=True) + EPS)
                n = sv * rs
                if c < 4:
                    dn = dq_ref[:, lanes] * (DN_HD ** -0.5)
                else:
                    dn = dk_ref[:, slice(128 * (c - 4), 128 * (c - 3))]
                dsv = rs * (dn - n * jnp.sum(dn * n, axis=1, keepdims=True))
            else:
                dsv = dv_ref[:, slice(128 * (c - 8), 128 * (c - 7))]
            dy_ref[:, lanes] = dsv * _silu_grad(y, sg)
        _, lane, sig_b, t, nega, g = _beta_g(ba_ref[...], al_ref[...], dt_ref[...])
        dbg = dbg_ref[...]
        da = dbg * nega * _sigmoid(t)
        is_b = lane < DN_HEADS
        is_a = jnp.logical_and(lane >= DN_HEADS, lane < 2 * DN_HEADS)
        dba_ref[...] = jnp.where(is_b, dbg * sig_b * (1.0 - sig_b), jnp.where(is_a, da, 0.0)).astype(BF16)
        d_alog = jnp.sum(jnp.where(is_a, dbg * g, 0.0), axis=0, keepdims=True)
        d_dtb = jnp.sum(jnp.where(is_a, da, 0.0), axis=0, keepdims=True)
        row = lax.broadcasted_iota(jnp.int32, (8, 128), 0)
        dsm_ref[...] = dsm_ref[...] + jnp.where(row == 0, d_alog, jnp.where(row == 1, d_dtb, 0.0))

    blk = pl.BlockSpec((tr, 512), lambda i: (i, 0))
    return pl.pallas_call(
        body, grid=(s // tr,),
        in_specs=[blk, blk, blk, pl.BlockSpec((tr, 128), lambda i: (i, 0)),
                  pl.BlockSpec((tr, 1536), lambda i: (i, 0)),
                  pl.BlockSpec((8, 1536), lambda i: (jnp.maximum(i * nh - 1, 0), 0)),
                  pl.BlockSpec((tr, 128), lambda i: (i, COL_BA_128)),
                  pl.BlockSpec((8, 1536), lambda i: (0, 0)),
                  pl.BlockSpec((1, 128), lambda i: (0, 0)), pl.BlockSpec((1, 128), lambda i: (0, 0))],
        out_specs=[pl.BlockSpec((tr, 1536), lambda i: (i, 0)), pl.BlockSpec((tr, 128), lambda i: (i, 0)),
                   pl.BlockSpec((8, 128), lambda i: (0, 0))],
        out_shape=[_sds((s, 1536)), _sds((s, 128), BF16), _sds((8, 128))],
        compiler_params=_cp(("arbitrary",)), name="dn_pre_bwd")(dqn, dkn, dv, dbg, proj, proj, proj, conv_w8,
                                                                 alog_l, dtb_l)


def _conv_bwd(dy, proj, conv_w8):
    s = dy.shape[0]
    tr = 512
    nh = tr // 8
    nblk = s // tr

    def body(dy_ref, dyn_ref, u_ref, halo_ref, w_ref, du_ref, dw_ref):
        i = pl.program_id(0)

        @pl.when(i == 0)
        def _():
            dw_ref[...] = jnp.zeros_like(dw_ref)

        keep_prev = (i > 0).astype(F32)
        keep_next = (i < nblk - 1).astype(F32)
        row = lax.broadcasted_iota(jnp.int32, (8, 128), 0)
        for c in range(12):
            lanes = slice(128 * c, 128 * c + 128)
            dyc = dy_ref[:, lanes]
            dcat = jnp.concatenate([dyc, dyn_ref[:, lanes] * keep_next], axis=0)
            xcat = jnp.concatenate([halo_ref[:, lanes] * keep_prev, u_ref[:, lanes]], axis=0)
            du = dyc * w_ref[CONV_W - 1:CONV_W, lanes]
            dwc = jnp.where(row == CONV_W - 1, jnp.sum(dyc * u_ref[:, lanes], axis=0, keepdims=True), 0.0)
            for k in range(1, CONV_W):
                du = du + pltpu.roll(dcat, tr + 8 - k, 0)[0:tr] * w_ref[CONV_W - 1 - k:CONV_W - k, lanes]
                ush = pltpu.roll(xcat, k, 0)[8:8 + tr]
                dwc = dwc + jnp.where(row == CONV_W - 1 - k, jnp.sum(dyc * ush, axis=0, keepdims=True), 0.0)
            du_ref[:, lanes] = du.astype(BF16)
            dw_ref[:, lanes] = dw_ref[:, lanes] + dwc

    return pl.pallas_call(
        body, grid=(nblk,),
        in_specs=[pl.BlockSpec((tr, 1536), lambda i: (i, 0)),
                  pl.BlockSpec((8, 1536), lambda i: (jnp.minimum((i + 1) * nh, s // 8 - 1), 0)),
                  pl.BlockSpec((tr, 1536), lambda i: (i, 0)),
                  pl.BlockSpec((8, 1536), lambda i: (jnp.maximum(i * nh - 1, 0), 0)),
                  pl.BlockSpec((8, 1536), lambda i: (0, 0))],
        out_specs=[pl.BlockSpec((tr, 1536), lambda i: (i, 0)), pl.BlockSpec((8, 1536), lambda i: (0, 0))],
        out_shape=[_sds((s, 1536), BF16), _sds((8, 1536))],
        compiler_params=_cp(("arbitrary",)), name="conv_bwd")(dy, dy, proj, proj, conv_w8)


PIECE_WIDTHS = (1536, 512, 512, 512, 512, 512, 128)


def _in_bwd_dx(pieces, w_bf, x, dy, norm_w, chip_sums=None):
    s = x.shape[0]
    tm = 512
    n_piece = len(PIECE_WIDTHS)
    n_step = s // tm
    n_sent = 0 if chip_sums is None else 3 * len(chip_sums)

    def body(*refs):
        piece_refs = refs[:n_piece]
        w_ref, x_ref, dy_ref, nw_ref = refs[n_piece:n_piece + 4]
        rest = refs[n_piece + 4:]
        i = pl.program_id(0)

        def copies():
            hb_refs = rest[:n_sent // 3]
            recv_refs = rest[n_sent // 3 + 2:2 * (n_sent // 3) + 2]
            send_sems, recv_sems = rest[-2:]
            xi, yi, ci = _position()
            out = []
            for j, (px, py) in enumerate(_other_chips(xi, yi)):
                for t, (src, dst) in enumerate(zip(hb_refs, recv_refs)):
                    k = len(hb_refs) * j + t
                    out.append(pltpu.make_async_remote_copy(
                        src_ref=src.at[2 * px + py], dst_ref=dst.at[j], send_sem=send_sems.at[k],
                        recv_sem=recv_sems.at[k], device_id=(px, py, ci), device_id_type=MESH_ID))
            return out

        dx_ref, dnw_ref = rest[n_sent // 3:n_sent // 3 + 2]

        @pl.when(i == 0)
        def _():
            dnw_ref[...] = jnp.zeros_like(dnw_ref)
            for cp in (copies() if n_sent else []):
                cp.start()

        dp = jnp.concatenate([r[...] for r in piece_refs], axis=1)
        dh = _dot(dp, w_ref[...])
        xv = x_ref[...]
        rstd = lax.rsqrt(jnp.mean(xv * xv, axis=-1, keepdims=True) + EPS)
        xh = xv * rstd
        dnw_ref[...] = dnw_ref[...] + jnp.sum(dh * xh, axis=0, keepdims=True)
        g = dh * nw_ref[...]
        dx_ref[...] = rstd * (g - xh * jnp.mean(g * xh, axis=-1, keepdims=True)) + dy_ref[...]

        if n_sent:
            @pl.when(i == n_step - 1)
            def _():
                cps = copies()
                for cp in cps:
                    cp.wait_recv()
                for cp in cps:
                    cp.wait_send()

    def blk(n):
        return pl.BlockSpec((tm, n), lambda i: (i, 0))

    in_specs = [blk(n) for n in PIECE_WIDTHS] + [pl.BlockSpec((D_IN_PAD, D_MODEL), lambda i: (0, 0)), blk(D_MODEL),
                                                blk(D_MODEL), pl.BlockSpec((1, D_MODEL), lambda i: (0, 0))]
    out_specs = [blk(D_MODEL), pl.BlockSpec((8, D_MODEL), lambda i: (0, 0))]
    out_shape = [_sds((s, D_MODEL)), _sds((8, D_MODEL))]
    scratch = []
    extra = ()
    if n_sent:
        extra = tuple(chip_sums)
        in_specs += [ANY] * len(extra)
        out_specs += [ANY] * len(extra)
        out_shape += [_sds((3,) + a.shape[1:], a.dtype) for a in extra]
        scratch = [pltpu.SemaphoreType.DMA((n_sent,)), pltpu.SemaphoreType.DMA((n_sent,))]
    return pl.pallas_call(
        body, grid=(n_step,), in_specs=in_specs, out_specs=out_specs, out_shape=out_shape, scratch_shapes=scratch,
        compiler_params=_cp(("arbitrary",)), name="in_bwd_dx")(*pieces, w_bf, x, dy, norm_w, *extra)


def _in_bwd_dw(pieces, x, norm_w):
    s = x.shape[0]
    tm = 512
    n_piece = len(PIECE_WIDTHS)

    def body(*refs):
        piece_refs = refs[:n_piece]
        x_ref, nw_ref, dw_ref = refs[n_piece:]
        i = pl.program_id(0)

        @pl.when(i == 0)
        def _():
            dw_ref[...] = jnp.zeros_like(dw_ref)

        xv = x_ref[...]
        rstd = lax.rsqrt(jnp.mean(xv * xv, axis=-1, keepdims=True) + EPS)
        h = (xv * rstd * nw_ref[...]).astype(BF16)
        at = 0
        for r, width in zip(piece_refs, PIECE_WIDTHS):
            dw_ref[at:at + width, :] = dw_ref[at:at + width, :] + _dot_tn(r[...], h)
            at += width

    return pl.pallas_call(
        body, grid=(s // tm,),
        in_specs=[pl.BlockSpec((tm, n), lambda i: (i, 0)) for n in PIECE_WIDTHS]
        + [pl.BlockSpec((tm, D_MODEL), lambda i: (i, 0)), pl.BlockSpec((1, D_MODEL), lambda i: (0, 0))],
        out_specs=pl.BlockSpec((D_IN_PAD, D_MODEL), lambda i: (0, 0)),
        out_shape=_sds((D_IN_PAD, D_MODEL)),
        compiler_params=_cp(("arbitrary",)), name="in_bwd_dw")(*pieces, x, norm_w)


def _flat(a):
    return a.reshape(-1, a.shape[-1])


def _as_pattern(a, r):
    return a if r == 1 else a.reshape(r, a.shape[0] // r, a.shape[1])


D_SHARD = D_IN // N_CHIPS
BA_START = 4 * D_DN
BA_PACKED = 4096
S1_HEAD = BA_START - D_SHARD
S1_BA = 2 * D_SHARD - BA_START


def _pack_rows(g):
    pad = jnp.zeros((D_IN_PAD - D_IN, g.shape[2]), g.dtype)
    s2_ba = 2 * DN_HEADS - S1_BA
    return jnp.concatenate([g[0][:D_SHARD], g[1][:S1_HEAD], g[2][s2_ba:D_SHARD], g[3][:D_SHARD],
                            g[1][S1_HEAD:D_SHARD], g[2][:s2_ba], pad], axis=0)


def _pack_w(g):
    n = g.shape[2]

    def body(g_ref, o_ref):
        g32 = g_ref.bitcast(jnp.uint32)
        o32 = o_ref.bitcast(jnp.uint32)
        full, head, ba1 = D_SHARD // 2, S1_HEAD // 2, S1_BA // 2
        ba2 = DN_HEADS - ba1
        pieces = [(0, 0, full), (1, 0, head), (2, ba2, full), (3, 0, full), (1, head, full), (2, 0, ba2)]
        at = 0
        for chip, lo, hi in pieces:
            o32[at:at + hi - lo, :] = g32[chip, lo:hi, :]
            at += hi - lo
        o32[at:D_IN_PAD // 2, :] = jnp.zeros((D_IN_PAD // 2 - at, n), jnp.uint32)

    vm = pl.BlockSpec(memory_space=pltpu.VMEM)
    return pl.pallas_call(body, in_specs=[vm], out_specs=vm, out_shape=_sds((D_IN_PAD, n), g.dtype),
                          compiler_params=_cp(), name="pack_w")(g)


def _unpack_w(p, rows):
    n = p.shape[1]
    tn = 256
    mid = BA_PACKED + S1_BA
    pieces = [(0, 0, (0, D_SHARD)), (1, 0, (D_SHARD, BA_START)), (1, S1_HEAD, (BA_PACKED, mid)),
              (2, 0, (mid, BA_PACKED + 2 * DN_HEADS)), (2, 2 * DN_HEADS - S1_BA, (BA_START, BA_START + S1_HEAD)),
              (3, 0, (BA_START + S1_HEAD, BA_PACKED))]

    def body(p_ref, o_ref):
        for chip, at, (lo, hi) in pieces:
            o_ref[chip, at:at + hi - lo, :] = p_ref[lo:hi, :]
        for chip in range(N_CHIPS):
            o_ref[chip, D_SHARD:rows, :] = jnp.zeros((rows - D_SHARD, tn), p.dtype)

    return pl.pallas_call(
        body, grid=(n // tn,),
        in_specs=[pl.BlockSpec((D_IN_PAD, tn), lambda j: (0, j))],
        out_specs=pl.BlockSpec((N_CHIPS, rows, tn), lambda j: (0, 0, j)),
        out_shape=_sds((N_CHIPS, rows, n), p.dtype),
        compiler_params=_cp(("parallel",)), name="unpack_w")(p)


def _unpack_rows(p, rows):
    mid = BA_PACKED + S1_BA
    pad = jnp.zeros((rows - D_SHARD, p.shape[1]), p.dtype)
    return jnp.stack([jnp.concatenate([p[0:D_SHARD], pad], axis=0),
                      jnp.concatenate([p[D_SHARD:BA_START], p[BA_PACKED:mid], pad], axis=0),
                      jnp.concatenate([p[mid:BA_PACKED + 2 * DN_HEADS], p[BA_START:BA_START + S1_HEAD], pad], axis=0),
                      jnp.concatenate([p[BA_START + S1_HEAD:BA_PACKED], pad], axis=0)])


def _lane_row(vec, offset):
    return jnp.pad(vec.reshape(1, -1), ((0, 0), (offset, 128 - offset - vec.shape[0])))


def _local_step(x, tgt, norm_w, w_bf, conv_w, a_log, dt_bias, dn_norm_w, q_norm_w, k_norm_w, rel_bias, w_out_bf):
    s = x.shape[0]
    nc = s // CHUNK
    conv_w8 = jnp.pad(conv_w, ((0, 8 - CONV_W), (0, 0)))
    alog_l = _lane_row(a_log.reshape(-1), DN_HEADS)
    dtb_l = _lane_row(dt_bias.reshape(-1), DN_HEADS)
    dnw_t = jnp.tile(dn_norm_w.reshape(1, DN_HD), (1, DN_HEADS))
    qw_t = jnp.tile(q_norm_w.reshape(1, ATT_HD), (1, ATT_HEADS))
    kw_t = jnp.tile(k_norm_w.reshape(1, ATT_HD), (1, ATT_HEADS))
    bd128 = _compact_mat(DN_HD)
    bd64 = _compact_mat(ATT_HD)
    tri = _tri_incl()
    sel = _lane_select()
    buckets = _bucket_tables()

    proj = _in_proj(x, norm_w, w_bf)
    qn, kn, v_dn, bg = _dn_pre(proj, conv_w8, alog_l, dtb_l)
    bgt = bg[:, 0:8].reshape(nc, CHUNK, 8).transpose(0, 2, 1)
    u, w, qd, kt, attn, t_inv, gl = _dn_prep(qn, kn, v_dn, bg, bgt, tri, sel)
    o_dn, vn, sp = _dn_scan(u, w, qd, kt, attn, gl)
    q1, k1, v1, q4, k4, v4, q16, k16, v16 = _att_pre(proj, qw_t, kw_t, bd64)
    rs = [r for _, r in PATTERNS]
    qkv = [(q1, k1, v1, 0), (_flat(q4), _flat(k4), _flat(v4), 0), (_flat(q16), _flat(k16), _flat(v16), 0)]
    bias = _bias_fwd(rel_bias, buckets)
    o_pats, lse_pats = [], []
    for p, r in enumerate(rs):
        o_p, lse_p = _att_fwd(*qkv[p], bias, p, r, "att_fwd_r%d" % r)
        o_pats.append(_as_pattern(o_p, r))
        lse_pats.append(_as_pattern(lse_p, r))
    mixed, o_att, l1, l4, l16 = _post_fwd(o_dn, proj, o_pats, lse_pats, dnw_t, bd128)
    dy, loss_blk = _out_fwd(x, mixed, w_out_bf, tgt)

    dmixed, d_w_out = _out_bwd(dy, mixed, w_out_bf)
    do_dn, dz, dgate, do1, do4, do16, dl1, dl4, dl16, d_dnw = _post_bwd(dmixed, o_dn, proj, o_att, dnw_t, bd128)
    side = [(do1, l1, dl1), (_flat(do4), _flat(l4), _flat(dl4)), (_flat(do16), _flat(l16), _flat(dl16))]
    dq_pats, dk_pats, dv_pats, ds_accs = [], [], [], []
    for p, r in enumerate(rs):
        dq_p, dk_p, dv_p, ds_p = _att_bwd(*qkv[p], *side[p], bias, p, r, "att_bwd_r%d" % r)
        dq_pats.append(_as_pattern(dq_p, r))
        dk_pats.append(_as_pattern(dk_p, r))
        dv_pats.append(_as_pattern(dv_p, r))
        ds_accs.append(ds_p)
    d_rel_bias = _bias_bwd(ds_accs, buckets)
    dq_att, dk_att, dv_att, d_qw, d_kw = _att_pre_bwd(dq_pats, dk_pats, dv_pats, proj, qw_t, kw_t, bd64)
    du, dqd, dkt, dw, dattn, dgl = _dn_scan_bwd(do_dn, sp, qd, kt, w, vn, attn, gl)
    dqn, dkn, dv_dn, dbg = _dn_prep_bwd(qn, kn, v_dn, bg, bgt, tri, sel, t_inv, attn, u, w, du, dw, dqd, dkt, dattn,
                                        dgl)
    dyc, dba, dsm = _dn_pre_bwd(dqn, dkn, dv_dn, dbg, proj, conv_w8, alog_l, dtb_l)
    d_qkv_dn, d_conv8 = _conv_bwd(dyc, proj, conv_w8)
    pieces = (d_qkv_dn, dz, dq_att, dk_att, dv_att, dgate, dba)
    d_w_in_t = _in_bwd_dw(pieces, x, norm_w)
    last = functools.partial(_in_bwd_dx, pieces, w_bf, x, dy, norm_w)

    grads = dict(
        w_in_t=d_w_in_t,
        conv_w=d_conv8[0:CONV_W, :],
        a_log=dsm[0:1, DN_HEADS:2 * DN_HEADS],
        dt_bias=dsm[1:2, DN_HEADS:2 * DN_HEADS],
        dn_norm_w=d_dnw[0:1, :],
        q_norm_w=d_qw[0:1, :].reshape(ATT_HEADS, ATT_HD),
        k_norm_w=d_kw[0:1, :].reshape(ATT_HEADS, ATT_HD),
        rel_bias=d_rel_bias,
        w_out=d_w_out,
    )
    return loss_blk[0, 0], last, grads


MESH_ID = pl.DeviceIdType.MESH
ANY = pl.BlockSpec(memory_space=pl.ANY)


def _position():
    return lax.axis_index("x"), lax.axis_index("y"), lax.axis_index("c")


def _other_chips(x, y):
    return [(1 - x, y), (x, 1 - y), (1 - x, 1 - y)]


SHARD_PAD = 1040
WIN = 528


def _row_split(n):
    return (n // 2) // 128 * 128


def _part(ref, cc):
    n = ref.shape[0]
    sp = _row_split(n)
    return ref.at[pl.ds(0, sp)] if cc == 0 else ref.at[pl.ds(sp, n - sp)]


def _gather_weights(wt_s, w_out_s, conv_s):
    def body(a_ref, b_ref, c_ref, ga_ref, gb_ref, gc_ref, send_sems, recv_sems, loc_sems, a_vmem, b_vmem):
        x, y, c = _position()
        me = 2 * x + y
        sib = (x, y, 1 - c)
        big = ((a_ref, ga_ref), (b_ref, gb_ref))
        stage_in = [pltpu.make_async_copy(a_ref, a_vmem, loc_sems.at[0]),
                    pltpu.make_async_copy(b_ref, b_vmem, loc_sems.at[1])]
        local = [pltpu.make_async_copy(a_vmem, ga_ref.at[me], loc_sems.at[0]),
                 pltpu.make_async_copy(b_vmem, gb_ref.at[me], loc_sems.at[1]),
                 pltpu.make_async_copy(c_ref, gc_ref.at[me], loc_sems.at[2])]
        for cp in stage_in:
            cp.start()
        local[2].start()
        for cp in stage_in:
            cp.wait()
        for cp in local[:2]:
            cp.start()
        others = _other_chips(x, y)

        def exchange(cc):
            sends = []
            for j, (px, py) in enumerate(others):
                for t, (src, dst) in enumerate(big):
                    k = 3 * j + t
                    sends.append(pltpu.make_async_remote_copy(
                        src_ref=_part(src, cc), dst_ref=_part(dst.at[me], cc), send_sem=send_sems.at[k],
                        recv_sem=recv_sems.at[k], device_id=(px, py, c), device_id_type=MESH_ID))
                sends.append(pltpu.make_async_remote_copy(
                    src_ref=c_ref, dst_ref=gc_ref.at[me], send_sem=send_sems.at[3 * j + 2],
                    recv_sem=recv_sems.at[3 * j + 2], device_id=(px, py, c), device_id_type=MESH_ID))
            for cp in sends:
                cp.start()
            for j, (px, py) in enumerate(others):
                src_chip = 2 * px + py
                for t, (src, dst) in enumerate(big):
                    landed = _part(dst.at[src_chip], cc)
                    pltpu.make_async_remote_copy(
                        src_ref=_part(src, cc), dst_ref=landed, send_sem=send_sems.at[3 * j + t],
                        recv_sem=recv_sems.at[3 * j + t], device_id=(px, py, c), device_id_type=MESH_ID).wait_recv()
                    k = 9 + 2 * j + t
                    fwd = pltpu.make_async_remote_copy(
                        src_ref=landed, dst_ref=landed, send_sem=send_sems.at[k], recv_sem=recv_sems.at[k],
                        device_id=sib, device_id_type=MESH_ID)
                    fwd.start()
                    sends.append(fwd)
                pltpu.make_async_remote_copy(
                    src_ref=c_ref, dst_ref=gc_ref.at[src_chip], send_sem=send_sems.at[3 * j + 2],
                    recv_sem=recv_sems.at[3 * j + 2], device_id=(px, py, c), device_id_type=MESH_ID).wait_recv()
            for j, (px, py) in enumerate(others):
                src_chip = 2 * px + py
                for t, (src, dst) in enumerate(big):
                    k = 9 + 2 * j + t
                    theirs = _part(dst.at[src_chip], 1 - cc)
                    pltpu.make_async_remote_copy(
                        src_ref=theirs, dst_ref=theirs, send_sem=send_sems.at[k], recv_sem=recv_sems.at[k],
                        device_id=sib, device_id_type=MESH_ID).wait_recv()
            for cp in sends:
                cp.wait_send()

        for cc in (0, 1):
            pl.when(c == cc)(functools.partial(exchange, cc))
        for cp in local:
            cp.wait()

    srcs = (wt_s, w_out_s, conv_s)
    n_sem = 9 + 6
    return pl.pallas_call(
        body, in_specs=[ANY] * 3, out_specs=[ANY] * 3,
        out_shape=[_sds((N_CHIPS,) + a.shape, a.dtype) for a in srcs],
        scratch_shapes=[pltpu.SemaphoreType.DMA((n_sem,)), pltpu.SemaphoreType.DMA((n_sem,)),
                        pltpu.SemaphoreType.DMA((3,)), pltpu.VMEM(wt_s.shape, wt_s.dtype),
                        pltpu.VMEM(w_out_s.shape, w_out_s.dtype)],
        compiler_params=_cp(), name="gather_weights")(*srcs)


N_DEV = 8
PEER_FLIPS = [(dx, dy, dc) for dx in (0, 1) for dy in (0, 1) for dc in (0, 1)][1:]


def _small_copies(s_ref, rs_ref, send_sems, recv_sems):
    x, y, c = _position()
    dev = 4 * x + 2 * y + c
    sends, recvs = [], []
    for f, (dx, dy, dc) in enumerate(PEER_FLIPS):
        peer = (x ^ dx, y ^ dy, c ^ dc)
        sends.append(pltpu.make_async_remote_copy(
            src_ref=s_ref, dst_ref=rs_ref.at[dev], send_sem=send_sems.at[f], recv_sem=recv_sems.at[f],
            device_id=peer, device_id_type=MESH_ID))
        recvs.append(pltpu.make_async_remote_copy(
            src_ref=s_ref, dst_ref=rs_ref.at[4 * peer[0] + 2 * peer[1] + peer[2]], send_sem=send_sems.at[f],
            recv_sem=recv_sems.at[f], device_id=peer, device_id_type=MESH_ID))
    return sends, recvs


def _plane_sum(own, recv, name):
    rows, cols = own.shape
    tr = 128

    def body(o_ref, r_ref, out_ref):
        acc = o_ref[...]
        for j in range(3):
            acc = acc + r_ref[j].astype(F32)
        out_ref[...] = acc

    return pl.pallas_call(
        body, grid=(pl.cdiv(rows, tr),),
        in_specs=[pl.BlockSpec((tr, cols), lambda i: (i, 0)), pl.BlockSpec((3, tr, cols), lambda i: (0, i, 0))],
        out_specs=pl.BlockSpec((tr, cols), lambda i: (i, 0)), out_shape=_sds((rows, cols)),
        compiler_params=_cp(("parallel",)), name=name)(own, recv)


def _sum_pair(a, b, name):
    n, rows, cols = a.shape
    tr = 128

    def body(a_ref, b_ref, s_ref, sb_ref):
        tot = a_ref[...] + b_ref[...]
        s_ref[...] = tot
        sb_ref[...] = tot.astype(BF16)

    blk = pl.BlockSpec((1, tr, cols), lambda k, i: (k, i, 0))
    return pl.pallas_call(
        body, grid=(n, pl.cdiv(rows, tr)), in_specs=[blk, blk], out_specs=[blk, blk],
        out_shape=[_sds(a.shape), _sds(a.shape, BF16)],
        compiler_params=_cp(("parallel", "parallel")), name=name)(a, b)


def _fill_parts(f_in, f_out, small, rows_in, rows_out):
    def body(a_ref, b_ref, s_ref, fa_ref, fb_ref, rs_ref, send_sems, recv_sems, loc_sems, small_send, small_recv,
             a_vmem, b_vmem):
        x, y, c = _position()
        sib = (x, y, 1 - c)
        pairs = ((a_ref, fa_ref), (b_ref, fb_ref))
        own_small = pltpu.make_async_copy(s_ref, rs_ref.at[4 * x + 2 * y + c], loc_sems.at[2])
        own_small.start()
        small_sends, small_recvs = _small_copies(s_ref, rs_ref, small_send, small_recv)
        for cp in small_sends:
            cp.start()
        stage_in = [pltpu.make_async_copy(a_ref, a_vmem, loc_sems.at[0]),
                    pltpu.make_async_copy(b_ref, b_vmem, loc_sems.at[1])]
        for cp in stage_in:
            cp.start()
        for cp in stage_in:
            cp.wait()

        def fill(cc):
            mine = [_part(dst, cc) for _, dst in pairs]
            srcs = [src.at[pl.ds(0, m.shape[0])] for src, m in zip((a_vmem, b_vmem), mine)]
            local = [pltpu.make_async_copy(s, m, loc_sems.at[t]) for t, (s, m) in enumerate(zip(srcs, mine))]
            sends = [pltpu.make_async_remote_copy(src_ref=s, dst_ref=m, send_sem=send_sems.at[t],
                                                  recv_sem=recv_sems.at[t], device_id=sib, device_id_type=MESH_ID)
                     for t, (s, m) in enumerate(zip(srcs, mine))]
            for cp in local + sends:
                cp.start()
            for t, (_, dst) in enumerate(pairs):
                theirs = _part(dst, 1 - cc)
                pltpu.make_async_remote_copy(src_ref=theirs, dst_ref=theirs, send_sem=send_sems.at[t],
                                             recv_sem=recv_sems.at[t], device_id=sib, device_id_type=MESH_ID).wait_recv()
            for cp in sends:
                cp.wait_send()
            for cp in local:
                cp.wait()

        for cc in (0, 1):
            pl.when(c == cc)(functools.partial(fill, cc))
        for cp in small_recvs:
            cp.wait_recv()
        for cp in small_sends:
            cp.wait_send()
        own_small.wait()

    n_peer = len(PEER_FLIPS)
    return pl.pallas_call(
        body, in_specs=[ANY] * 3, out_specs=[ANY] * 3,
        out_shape=[_sds((rows_in, D_MODEL)), _sds((rows_out, D_MODEL)), _sds((N_DEV,) + small.shape, small.dtype)],
        scratch_shapes=[pltpu.SemaphoreType.DMA((2,)), pltpu.SemaphoreType.DMA((2,)), pltpu.SemaphoreType.DMA((3,)),
                        pltpu.SemaphoreType.DMA((n_peer,)), pltpu.SemaphoreType.DMA((n_peer,)),
                        pltpu.VMEM(f_in.shape, F32), pltpu.VMEM(f_out.shape, F32)],
        compiler_params=_cp(), name="fill_parts")(f_in, f_out, small)


def _swap_windows(u_in, u_out, win_in, win_out):
    def body(a_ref, b_ref, ra_ref, rb_ref, send_sems, recv_sems):
        x, y, c = _position()
        sib = (x, y, 1 - c)
        cps = []
        for t, (src, dst, win) in enumerate(((a_ref, ra_ref, win_in), (b_ref, rb_ref, win_out))):
            split = _row_split(src.shape[1])
            start = pl.multiple_of((1 - c) * split, split)
            cps.append(pltpu.make_async_remote_copy(
                src_ref=src.at[:, pl.ds(start, win), :], dst_ref=dst, send_sem=send_sems.at[t],
                recv_sem=recv_sems.at[t], device_id=sib, device_id_type=MESH_ID))
        for cp in cps:
            cp.start()
        for cp in cps:
            cp.wait_recv()
        for cp in cps:
            cp.wait_send()

    return pl.pallas_call(
        body, in_specs=[ANY] * 2, out_specs=[ANY] * 2,
        out_shape=[_sds((N_CHIPS, win_in, D_MODEL)), _sds((N_CHIPS, win_out, D_MODEL))],
        scratch_shapes=[pltpu.SemaphoreType.DMA((2,)), pltpu.SemaphoreType.DMA((2,))],
        name="swap_windows")(u_in, u_out)


SMALL_LAYOUT = (("norm_w", 1024), ("conv_w", 6144), ("a_log", 128), ("dt_bias", 128), ("dn_norm_w", 128),
                ("q_norm_w", 512), ("k_norm_w", 512), ("rel_bias", 256))
SMALL_TOTAL = sum(n for _, n in SMALL_LAYOUT)


def _small_offset(name):
    off = 0
    for n, size in SMALL_LAYOUT:
        if n == name:
            return off
        off += size
    raise KeyError(name)


def _pack_small(grads):
    parts = []
    for name, size in SMALL_LAYOUT:
        flat = grads[name].reshape(1, -1)
        parts.append(jnp.pad(flat, ((0, 0), (0, size - flat.shape[1]))))
    return jnp.concatenate(parts, axis=1)


def _sum_small(rows):
    n_dev = rows.shape[0]
    q_off = _small_offset("q_norm_w")
    k_off = _small_offset("k_norm_w")

    def body(r_ref, tot_ref, qk_ref):
        tot = r_ref[0:1, :]
        for d in range(1, n_dev):
            tot = tot + r_ref[d:d + 1, :]
        tot_ref[...] = tot
        for row, off in ((0, q_off), (1, k_off)):
            s4 = tot[:, off:off + 128] + tot[:, off + 128:off + 256] + tot[:, off + 256:off + 384] \
                + tot[:, off + 384:off + 512]
            qk_ref[row:row + 1, :] = s4 + pltpu.roll(s4, ATT_HD, 1)

    return pl.pallas_call(
        body, in_specs=[pl.BlockSpec(memory_space=pltpu.VMEM)],
        out_specs=[pl.BlockSpec(memory_space=pltpu.VMEM)] * 2,
        out_shape=[_sds((1, SMALL_TOTAL)), _sds((2, 128))],
        compiler_params=_cp(), name="sum_small")(rows)


def _adamw_math(w, g, m, v):
    m = ADAM_B1 * m + (1.0 - ADAM_B1) * g
    v = ADAM_B2 * v + (1.0 - ADAM_B2) * (g * g)
    m_hat = m / (1.0 - ADAM_B1 ** ADAM_STEP)
    v_hat = v / (1.0 - ADAM_B2 ** ADAM_STEP)
    delta = -ADAM_LR * (m_hat / (jnp.sqrt(v_hat) + ADAM_EPS) + ADAM_WD * w)
    return delta, m, v


def _adamw_big(g, w, m, v, name):
    rows, cols = w.shape
    tr = 128

    def body(g_ref, w_ref, m_ref, v_ref, go_ref, d_ref, nm_ref, nv_ref):
        g = g_ref[...]
        go_ref[...] = g
        d_ref[...], nm_ref[...], nv_ref[...] = _adamw_math(w_ref[...], g, m_ref[...], v_ref[...])

    blk = pl.BlockSpec((tr, cols), lambda i: (i, 0))
    return pl.pallas_call(
        body, grid=(pl.cdiv(rows, tr),), in_specs=[blk] * 4, out_specs=[blk] * 4,
        out_shape=[_sds((rows, cols))] * 4, compiler_params=_cp(("parallel",)), name=name)(g, w, m, v)


def _adamw_rows(g, w, m, v, name):
    rows, cols = w.shape
    tr = 128

    def body(g_ref, w_ref, m_ref, v_ref, go_ref, d_ref, nm_ref, nv_ref, s_g, s_d, s_m, s_v):
        g = g_ref[...]
        s_g[...] = g
        s_d[...], s_m[...], s_v[...] = _adamw_math(w_ref[...], g, m_ref[...], v_ref[...])
        for i in range(tr):
            for scr, out in ((s_g, go_ref), (s_d, d_ref), (s_m, nm_ref), (s_v, nv_ref)):
                out[i] = scr[i:i + 1, :]

    blk = pl.BlockSpec((tr, cols), lambda i: (i, 0))
    oblk = pl.BlockSpec((tr, 1, cols), lambda i: (i, 0, 0))
    return pl.pallas_call(
        body, grid=(pl.cdiv(rows, tr),), in_specs=[blk] * 4, out_specs=[oblk] * 4,
        out_shape=[_sds((rows, 1, cols))] * 4, scratch_shapes=[pltpu.VMEM((tr, cols), F32)] * 4,
        compiler_params=_cp(("parallel",)), name=name)(g, w, m, v)


def _adamw_small(w, g, m, v, name):
    def body(w_ref, g_ref, m_ref, v_ref, d_ref, nm_ref, nv_ref):
        d_ref[...], nm_ref[...], nv_ref[...] = _adamw_math(w_ref[...], g_ref[...], m_ref[...], v_ref[...])

    vm = pl.BlockSpec(memory_space=pltpu.VMEM)
    return pl.pallas_call(body, in_specs=[vm] * 4, out_specs=[vm] * 3, out_shape=[_sds(w.shape)] * 3,
                          compiler_params=_cp(), name=name)(w, g, m, v)


WEIGHTS = ("norm_w", "w_in", "conv_w", "a_log", "dt_bias", "dn_norm_w", "q_norm_w", "k_norm_w", "rel_bias", "w_out")


def kernel(x, norm_w, w_in, conv_w, a_log, dt_bias, dn_norm_w, q_norm_w, k_norm_w, rel_bias, w_out, loss_target, m_norm_w, m_w_in, m_conv_w, m_a_log, m_dt_bias, m_dn_norm_w, m_q_norm_w, m_k_norm_w, m_rel_bias, m_w_out, v_norm_w, v_w_in, v_conv_w, v_a_log, v_dt_bias, v_dn_norm_w, v_q_norm_w, v_k_norm_w, v_rel_bias, v_w_out):
    xi, yi, ci = _position()
    chip = 2 * xi + yi
    w_loc = dict(norm_w=norm_w, w_in=w_in[0].T, conv_w=conv_w[0], a_log=a_log, dt_bias=dt_bias, dn_norm_w=dn_norm_w,
                 q_norm_w=q_norm_w, k_norm_w=k_norm_w, rel_bias=rel_bias, w_out=w_out[0])
    m_loc = dict(norm_w=m_norm_w, w_in=m_w_in[0].T, conv_w=m_conv_w[0], a_log=m_a_log, dt_bias=m_dt_bias,
                 dn_norm_w=m_dn_norm_w, q_norm_w=m_q_norm_w, k_norm_w=m_k_norm_w, rel_bias=m_rel_bias,
                 w_out=m_w_out[0])
    v_loc = dict(norm_w=v_norm_w, w_in=v_w_in[0].T, conv_w=v_conv_w[0], a_log=v_a_log, dt_bias=v_dt_bias,
                 dn_norm_w=v_dn_norm_w, q_norm_w=v_q_norm_w, k_norm_w=v_k_norm_w, rel_bias=v_rel_bias,
                 w_out=v_w_out[0])

    wt_pad = jnp.pad(w_loc["w_in"].astype(BF16), ((0, SHARD_PAD - D_SHARD), (0, 0)))
    g_in, g_out, g_conv = _gather_weights(wt_pad, w_loc["w_out"].astype(BF16), w_loc["conv_w"])
    wt_full = _pack_w(g_in)
    w_out_full = g_out.reshape(D_MODEL, D_MODEL)
    conv_full = g_conv.transpose(1, 0, 2).reshape(CONV_W, 3 * D_DN)

    loss_local, last_kernel, grads = _local_step(x[0], loss_target[0], norm_w, wt_full, conv_full, a_log, dt_bias,
                                                 dn_norm_w, q_norm_w, k_norm_w, rel_bias, w_out_full)
    loss = lax.psum(loss_local, ("x", "y", "c"))

    u_in = _unpack_w(grads["w_in_t"], SHARD_PAD)
    u_out = grads["w_out"].reshape(N_CHIPS, D_MODEL // N_CHIPS, D_MODEL)
    win_out = u_out.shape[1] // 2
    sib_in, sib_out = _swap_windows(u_in, u_out, WIN, win_out)
    in_mine = lax.dynamic_slice_in_dim(u_in, ci * _row_split(SHARD_PAD), WIN, axis=1)
    out_mine = lax.dynamic_slice_in_dim(u_out, ci * win_out, win_out, axis=1)
    h_in, hb_in = _sum_pair(in_mine, sib_in, "chip_sum_w_in")
    h_out, hb_out = _sum_pair(out_mine, sib_out, "chip_sum_w_out")
    grad_x, d_nw8, r_in, r_out = last_kernel(chip_sums=(hb_in, hb_out))
    grads["norm_w"] = d_nw8[0:1, :]
    small = _pack_small(grads)
    f_in = _plane_sum(lax.dynamic_index_in_dim(h_in, chip, 0, keepdims=False), r_in, "shard_sum_w_in")
    f_out = _plane_sum(lax.dynamic_index_in_dim(h_out, chip, 0, keepdims=False), r_out, "shard_sum_w_out")
    full_in, full_out, r_small = _fill_parts(f_in, f_out, small, SHARD_PAD, u_out.shape[1])
    tot_small, qk = _sum_small(r_small.reshape(8, SMALL_TOTAL))

    def small_grad(name, n):
        off = _small_offset(name)
        return tot_small[:, off:off + n]

    conv_all = small_grad("conv_w", CONV_W * 3 * D_DN).reshape(CONV_W, 3 * D_DN)
    g_small = dict(
        norm_w=small_grad("norm_w", D_MODEL),
        conv_w=lax.dynamic_slice_in_dim(conv_all, chip * (3 * D_DN // N_CHIPS), 3 * D_DN // N_CHIPS, axis=1),
        a_log=small_grad("a_log", DN_HEADS),
        dt_bias=small_grad("dt_bias", DN_HEADS),
        dn_norm_w=small_grad("dn_norm_w", DN_HD),
        q_norm_w=qk[0:1, 0:ATT_HD],
        k_norm_w=qk[1:2, 0:ATT_HD],
        rel_bias=small_grad("rel_bias", ATT_HEADS * N_BUCKETS).reshape(ATT_HEADS, N_BUCKETS),
    )

    out_g, out_d, out_m, out_v = {}, {}, {}, {}
    out_g["w_in"], out_d["w_in"], out_m["w_in"], out_v["w_in"] = _adamw_rows(
        full_in, w_loc["w_in"], m_loc["w_in"], v_loc["w_in"], "adamw_w_in")
    out_g["w_out"], out_d["w_out"], out_m["w_out"], out_v["w_out"] = _adamw_big(
        full_out, w_loc["w_out"], m_loc["w_out"], v_loc["w_out"], "adamw_w_out")
    for name in g_small:
        out_g[name] = g_small[name]
        out_d[name], out_m[name], out_v[name] = _adamw_small(w_loc[name], g_small[name], m_loc[name], v_loc[name],
                                                             "adamw_" + name)
    for d in (out_g, out_d, out_m, out_v):
        d["w_in"] = d["w_in"].transpose(1, 2, 0)
        for name in ("conv_w", "w_out"):
            d[name] = d[name][None]
    return (loss, grad_x[None], *[out_g[n] for n in WEIGHTS], *[out_d[n] for n in WEIGHTS],
            *[out_m[n] for n in WEIGHTS], *[out_v[n] for n in WEIGHTS])
```

```python
import functools
import math

import numpy as np
import jax
import jax.numpy as jnp
from jax import lax
from jax.experimental import pallas as pl
from jax.experimental.pallas import tpu as pltpu

F32 = jnp.float32
BF16 = jnp.bfloat16
HI = lax.Precision.HIGHEST

D_MODEL = 1024
D_DN = 512
DN_HEADS = 4
DN_HD = 128
CONV_W = 4
CHUNK = 64
D_ATT = 512
ATT_HEADS = 8
ATT_HD = 64
PATTERNS = ((128, 1), (512, 4), (2048, 16))
N_BUCKETS = 32
MAX_DISTANCE = 2048
D_IN = 4 * D_DN + 2 * DN_HEADS + 4 * D_ATT
D_IN_PAD = 4224
EPS = 1e-6
BLK = 128
NEG = -1e30
N_CHIPS = 4

ADAM_LR = 0.001
ADAM_B1 = 0.9
ADAM_B2 = 0.999
ADAM_EPS = 1e-08
ADAM_WD = 0.01
ADAM_STEP = 10

VMEM_LIMIT = 56 * 1024 * 1024

COL_Z = 3
COL_ATT_Q = 4
COL_ATT_K = 5
COL_ATT_V = 6
COL_GATE = 7
COL_BA_128 = 32


def _cp(sem=None):
    if sem is None:
        return pltpu.CompilerParams(vmem_limit_bytes=VMEM_LIMIT)
    return pltpu.CompilerParams(dimension_semantics=sem, vmem_limit_bytes=VMEM_LIMIT)


def _sds(shape, dtype=F32):
    return jax.ShapeDtypeStruct(shape, dtype)


def _mm(a, b):
    return jnp.dot(a.astype(BF16), b.astype(BF16), preferred_element_type=F32)


def _mm_nt(a, b):
    return lax.dot_general(a.astype(BF16), b.astype(BF16), (((1,), (1,)), ((), ())),
                           preferred_element_type=F32)


def _mm_tn(a, b):
    return lax.dot_general(a.astype(BF16), b.astype(BF16), (((0,), (0,)), ((), ())),
                           preferred_element_type=F32)


def _mmx(a, b):
    return jnp.dot(a, b, precision=HI, preferred_element_type=F32)


def _mmx_nt(a, b):
    return lax.dot_general(a, b, (((1,), (1,)), ((), ())), precision=HI, preferred_element_type=F32)


def _mmx_tn(a, b):
    return lax.dot_general(a, b, (((0,), (0,)), ((), ())), precision=HI, preferred_element_type=F32)


def _dot(a, b):
    return jnp.dot(a, b, preferred_element_type=F32)


def _dot_nt(a, b):
    return lax.dot_general(a, b, (((1,), (1,)), ((), ())), preferred_element_type=F32)


def _dot_tn(a, b):
    return lax.dot_general(a, b, (((0,), (0,)), ((), ())), preferred_element_type=F32)


def _split(a):
    hi = a.astype(BF16)
    return hi, (a - hi.astype(F32)).astype(BF16)


def _mm3(a_s, b_s):
    return _dot(a_s[0], b_s[0]) + _dot(a_s[0], b_s[1]) + _dot(a_s[1], b_s[0])


def _mm3_tn(a_s, b_s):
    return _dot_tn(a_s[0], b_s[0]) + _dot_tn(a_s[0], b_s[1]) + _dot_tn(a_s[1], b_s[0])


def _interleave(gens):
    live = list(gens)
    while live:
        nxt = []
        for g in live:
            try:
                next(g)
                nxt.append(g)
            except StopIteration:
                pass
        live = nxt


def _segsum(x, bd):
    hi = x.astype(BF16)
    r1 = x - hi.astype(F32)
    mid = r1.astype(BF16)
    lo = (r1 - mid.astype(F32)).astype(BF16)
    return (jnp.dot(hi, bd, preferred_element_type=F32) + jnp.dot(mid, bd, preferred_element_type=F32)
            + jnp.dot(lo, bd, preferred_element_type=F32))


def _compact_mat(seg, n=512):
    slot = 128 * seg // n
    src = np.arange(n)[:, None]
    dst = np.arange(128)[None, :]
    return jnp.asarray((src // seg == dst // slot).astype(np.float32), dtype=BF16)


def _seg_mean(x, cm, seg):
    hi, lo = _split(x)
    return (_dot(hi, cm) + _dot(lo, cm)) * (1.0 / seg)


def _seg_expand(c, cm, seg):
    hi, lo = _split(c)
    return (_dot_nt(hi, cm) + _dot_nt(lo, cm)) * (cm.shape[0] / (128.0 * seg))


def _seg_rstd(x, cm, seg):
    return _seg_expand(lax.rsqrt(_seg_mean(x * x, cm, seg) + EPS), cm, seg)


def _sigmoid(x):
    return 1.0 / (1.0 + jnp.exp(-x))


def _silu_grad(x, s):
    return s * (1.0 + x * (1.0 - s))


def _tri_incl():
    i = np.arange(CHUNK)
    return jnp.asarray((i[:, None] >= i[None, :]).astype(np.float32))


def _t5_bucket(dist):
    max_exact = N_BUCKETS // 2
    d = np.maximum(dist, 1).astype(np.float64)
    large = max_exact + (np.log(d / max_exact) / math.log(MAX_DISTANCE / max_exact)
                         * (N_BUCKETS - max_exact)).astype(np.int32)
    large = np.minimum(large, N_BUCKETS - 1)
    return np.where(dist < max_exact, dist, large).astype(np.int32)


def _bucket_tables():
    qi = np.arange(BLK)[:, None]
    kj = np.arange(2 * BLK)[None, :]
    step = qi - kj + BLK
    return jnp.asarray(np.stack([_t5_bucket(np.clip(step, 0, None) * r) for _, r in PATTERNS]))


def _in_proj(x, norm_w, wt_bf):
    s = x.shape[0]
    tm = 512

    def body(x_ref, nw_ref, w_ref, o_ref):
        xv = x_ref[...]
        rstd = lax.rsqrt(jnp.mean(xv * xv, axis=-1, keepdims=True) + EPS)
        h = (xv * rstd * nw_ref[...]).astype(BF16)
        o_ref[...] = _dot_nt(h, w_ref[...])

    return pl.pallas_call(
        body, grid=(s // tm,),
        in_specs=[pl.BlockSpec((tm, D_MODEL), lambda i: (i, 0)),
                  pl.BlockSpec((1, D_MODEL), lambda i: (0, 0)),
                  pl.BlockSpec((D_IN_PAD, D_MODEL), lambda i: (0, 0))],
        out_specs=pl.BlockSpec((tm, D_IN_PAD), lambda i: (i, 0)),
        out_shape=_sds((s, D_IN_PAD)), compiler_params=_cp(("parallel",)), name="in_proj")(x, norm_w, wt_bf)


def _conv_group(cur, halo, w_ref, c):
    rows = cur.shape[0]
    lanes = slice(128 * c, 128 * c + 128)
    xcat = jnp.concatenate([halo, cur], axis=0)
    y = cur * w_ref[CONV_W - 1:CONV_W, lanes]
    for k in range(1, CONV_W):
        sh = pltpu.roll(xcat, k, 0)[8:8 + rows]
        y = y + sh * w_ref[CONV_W - 1 - k:CONV_W - k, lanes]
    return y


def _beta_g(ba, alog_l, dtb_l):
    lane = lax.broadcasted_iota(jnp.int32, ba.shape, 1)
    sig_b = _sigmoid(ba)
    t = ba + dtb_l
    softplus = jnp.maximum(t, 0.0) + jnp.log(1.0 + jnp.exp(-jnp.abs(t)))
    nega = -jnp.exp(alog_l)
    g = nega * softplus
    out = jnp.where(lane < DN_HEADS, sig_b, jnp.where(lane < 2 * DN_HEADS, g, 0.0))
    return out, lane, sig_b, t, nega, g


def _dn_pre(proj, conv_w8, alog_l, dtb_l):
    s = proj.shape[0]
    tr = 512
    nh = tr // 8

    def body(u_ref, halo_ref, ba_ref, w_ref, al_ref, dt_ref, q_ref, k_ref, v_ref, bg_ref):
        i = pl.program_id(0)
        keep = (i > 0).astype(F32)
        for c in range(12):
            lanes = slice(128 * c, 128 * c + 128)
            y = _conv_group(u_ref[:, lanes], halo_ref[:, lanes] * keep, w_ref, c)
            sv = y * _sigmoid(y)
            if c < 8:
                rs = lax.rsqrt(jnp.sum(sv * sv, axis=1, keepdims=True) + EPS)
                n = sv * rs
                if c < 4:
                    q_ref[:, lanes] = n * (DN_HD ** -0.5)
                else:
                    k_ref[:, slice(128 * (c - 4), 128 * (c - 3))] = n
            else:
                v_ref[:, slice(128 * (c - 8), 128 * (c - 7))] = sv
        bg_ref[...] = _beta_g(ba_ref[...], al_ref[...], dt_ref[...])[0]

    return pl.pallas_call(
        body, grid=(s // tr,),
        in_specs=[pl.BlockSpec((tr, 1536), lambda i: (i, 0)),
                  pl.BlockSpec((8, 1536), lambda i: (jnp.maximum(i * nh - 1, 0), 0)),
                  pl.BlockSpec((tr, 128), lambda i: (i, COL_BA_128)),
                  pl.BlockSpec((8, 1536), lambda i: (0, 0)),
                  pl.BlockSpec((1, 128), lambda i: (0, 0)),
                  pl.BlockSpec((1, 128), lambda i: (0, 0))],
        out_specs=[pl.BlockSpec((tr, 512), lambda i: (i, 0))] * 3 + [pl.BlockSpec((tr, 128), lambda i: (i, 0))],
        out_shape=[_sds((s, 512))] * 3 + [_sds((s, 128))],
        compiler_params=_cp(("parallel",)), name="dn_pre")(proj, proj, proj, conv_w8, alog_l, dtb_l)


CPS = 4
CPS_SCAN = 8


def _split3(a):
    hi = a.astype(BF16)
    r1 = a - hi.astype(F32)
    mid = r1.astype(BF16)
    return hi, mid, (r1 - mid.astype(F32)).astype(BF16)


def _lane_select():
    r = np.arange(128)
    return jnp.asarray((r[None, :, None] == np.arange(8)[:, None, None]) * np.ones((1, 1, 128)), dtype=BF16)


def _lane_bcast(a3, sel):
    return _dot(a3[0], sel) + _dot(a3[1], sel) + _dot(a3[2], sel)


def _rowsum_b(z, ones_b):
    hi, lo = _split(z)
    return _dot(hi, ones_b) + _dot(lo, ones_b)


def _chunk_cumsum(bg, bgt, tri):
    return _split3(bg), _split3(_mmx(tri, bg)), _mmx_nt(bgt, tri)


def _chunk_common(bg3, gc3, gc_row, h, sel_ref):
    gcc = _lane_bcast(gc3, sel_ref[DN_HEADS + h])
    beta = _lane_bcast(bg3, sel_ref[h])
    gcr = gc_row[DN_HEADS + h:DN_HEADS + h + 1, :]
    ii = lax.broadcasted_iota(jnp.int32, (CHUNK, CHUNK), 0)
    jj = lax.broadcasted_iota(jnp.int32, (CHUNK, CHUNK), 1)
    incl = ii >= jj
    strict = ii > jj
    decay = jnp.exp(jnp.where(incl, gcc[:, 0:CHUNK] - gcr, NEG))
    gl = gcc[CHUNK - 1:CHUNK, :]
    return gcc, beta, incl, strict, decay, gl


def _dn_prep(qn, kn, v, bg, bgt, tri, sel):
    s = qn.shape[0]
    nc = s // CHUNK

    def body(q_ref, k_ref, v_ref, bg_ref, bgt_ref, tri_ref, sel_ref,
             u_ref, w_ref, qd_ref, kt_ref, attn_ref, t_ref, gl_ref):
        tri_v = tri_ref[...]
        ii = lax.broadcasted_iota(jnp.int32, (CHUNK, CHUNK), 0)
        jj = lax.broadcasted_iota(jnp.int32, (CHUNK, CHUNK), 1)
        eye = (ii == jj).astype(F32)

        def chain(cc, h, bg3, gc3, gc_row):
            rows = slice(CHUNK * cc, CHUNK * cc + CHUNK)
            lanes = slice(128 * h, 128 * h + 128)
            gcc, beta, incl, strict, decay, gl = _chunk_common(bg3, gc3, gc_row, h, sel_ref)
            yield
            q = q_ref[rows, lanes]
            k = k_ref[rows, lanes]
            vv = v_ref[rows, lanes]
            kb = k * beta
            egc = jnp.exp(gcc)
            a_mat = jnp.where(strict, _mm_nt(kb, k) * decay, 0.0)
            attn_ref[cc, h] = jnp.where(incl, _mm_nt(q, k) * decay, 0.0)
            qd_ref[rows, lanes] = q * egc
            kt_ref[rows, lanes] = k * jnp.exp(gl - gcc)
            gl_ref[cc, h] = jnp.exp(gl)
            yield
            p = -a_mat
            t = eye + p
            for _ in range(5):
                ps = _split(p)
                p = _mm3(ps, ps)
                yield
                t = t + _mm3(_split(t), _split(p))
                yield
            t_ref[cc, h] = t
            ts = _split(t)
            u_ref[rows, lanes] = _mm3(ts, _split(vv * beta))
            w_ref[rows, lanes] = _mm3(ts, _split(kb * egc))

        gens = []
        for cc in range(CPS):
            bgv = bg_ref[CHUNK * cc:CHUNK * cc + CHUNK, :]
            bg3, gc3, gc_row = _chunk_cumsum(bgv, bgt_ref[cc], tri_v)
            gens += [chain(cc, h, bg3, gc3, gc_row) for h in range(DN_HEADS)]
        _interleave(gens)

    rows_step = CPS * CHUNK
    big = pl.BlockSpec((rows_step, 512), lambda n: (n, 0))
    sq = pl.BlockSpec((CPS, DN_HEADS, CHUNK, CHUNK), lambda n: (n, 0, 0, 0))
    return pl.pallas_call(
        body, grid=(nc // CPS,),
        in_specs=[big, big, big, pl.BlockSpec((rows_step, 128), lambda n: (n, 0)),
                  pl.BlockSpec((CPS, 8, CHUNK), lambda n: (n, 0, 0)),
                  pl.BlockSpec((CHUNK, CHUNK), lambda n: (0, 0)),
                  pl.BlockSpec((8, 128, 128), lambda n: (0, 0, 0))],
        out_specs=[big, big, big, big, sq, sq, pl.BlockSpec((CPS, DN_HEADS, 1, 128), lambda n: (n, 0, 0, 0))],
        out_shape=[_sds((s, 512))] * 4 + [_sds((nc, DN_HEADS, CHUNK, CHUNK))] * 2 + [_sds((nc, DN_HEADS, 1, 128))],
        compiler_params=_cp(("parallel",)), name="dn_prep")(qn, kn, v, bg, bgt, tri, sel)


def _dn_scan(u, w, qd, kt, attn, gl):
    s = u.shape[0]
    nc = s // CHUNK

    def body(u_ref, w_ref, qd_ref, kt_ref, attn_ref, gl_ref, o_ref, vn_ref, sp_ref, st_ref):
        n = pl.program_id(0)

        @pl.when(n == 0)
        def _():
            st_ref[...] = jnp.zeros_like(st_ref)

        def chain(cc, h):
            rows = slice(CHUNK * cc, CHUNK * cc + CHUNK)
            lanes = slice(128 * h, 128 * h + 128)
            st = st_ref[h]
            sp_ref[cc, h] = st
            stb = st.astype(BF16)
            ws = _dot(w_ref[rows, lanes].astype(BF16), stb)
            qs = _dot(qd_ref[rows, lanes].astype(BF16), stb)
            yield
            vn = u_ref[rows, lanes] - ws
            vn_ref[rows, lanes] = vn
            vnb = vn.astype(BF16)
            o_ref[rows, lanes] = qs + _dot(attn_ref[cc, h].astype(BF16), vnb)
            st_ref[h] = st * gl_ref[cc, h] + _dot_tn(kt_ref[rows, lanes].astype(BF16), vnb)

        for cc in range(CPS_SCAN):
            _interleave([chain(cc, h) for h in range(DN_HEADS)])

    big = pl.BlockSpec((CPS_SCAN * CHUNK, 512), lambda n: (n, 0))
    return pl.pallas_call(
        body, grid=(nc // CPS_SCAN,),
        in_specs=[big, big, big, big,
                  pl.BlockSpec((CPS_SCAN, DN_HEADS, CHUNK, CHUNK), lambda n: (n, 0, 0, 0)),
                  pl.BlockSpec((CPS_SCAN, DN_HEADS, 1, 128), lambda n: (n, 0, 0, 0))],
        out_specs=[big, big, pl.BlockSpec((CPS_SCAN, DN_HEADS, DN_HD, DN_HD), lambda n: (n, 0, 0, 0))],
        out_shape=[_sds((s, 512)), _sds((s, 512)), _sds((nc, DN_HEADS, DN_HD, DN_HD))],
        scratch_shapes=[pltpu.VMEM((DN_HEADS, DN_HD, DN_HD), F32)],
        compiler_params=_cp(("arbitrary",)), name="dn_scan")(u, w, qd, kt, attn, gl)


R4 = PATTERNS[1][1]
R16 = PATTERNS[2][1]
TM = 512


def _pattern_spec(r, width=512):
    return pl.BlockSpec((r, TM // r, width), lambda i: (0, i, 0))


def _pattern_shape(s, r, dtype=F32, width=512):
    return _sds((r, s // r, width), dtype)


SLABS = pltpu.VMEM((4, TM, 128), F32)

HEAD_SLOT = 128 // ATT_HEADS


def _head_expand():
    src = np.arange(128)[:, None]
    dst = np.arange(512)[None, :]
    return jnp.asarray((src == (dst // ATT_HD) * HEAD_SLOT).astype(np.float32), dtype=BF16)


def _head_compact():
    src = np.arange(512)[:, None]
    dst = np.arange(128)[None, :]
    return jnp.asarray((src // ATT_HD == dst // HEAD_SLOT).astype(np.float32), dtype=BF16)


def _to_patterns(val, dsts, scr):
    for c in range(val.shape[1] // 128):
        lanes = slice(128 * c, 128 * c + 128)
        scr[c] = val[:, lanes]
        for dst_ref, r in dsts:
            for a in range(r):
                dst_ref[a, :, lanes] = scr[c, pl.ds(a, TM // r, stride=r), :].astype(dst_ref.dtype)


def _from_pattern(src_ref, r, scr):
    n_slab = src_ref.shape[2] // 128
    for c in range(n_slab):
        for a in range(r):
            scr[c, pl.ds(a, TM // r, stride=r), :] = src_ref[a, :, 128 * c:128 * c + 128]
    return jnp.concatenate([scr[c] for c in range(n_slab)], axis=1) if n_slab > 1 else scr[0]


def _att_pre(proj, qw_t, kw_t, bd64):
    s = proj.shape[0]

    def body(q_ref, k_ref, v_ref, qw_ref, kw_ref, bd_ref,
             q1_ref, k1_ref, v1_ref, q4_ref, k4_ref, v4_ref, q16_ref, k16_ref, v16_ref, scr):
        bd = bd_ref[...]
        q = q_ref[...]
        k = k_ref[...]
        qn = q * _seg_rstd(q, bd, ATT_HD) * qw_ref[...] * (ATT_HD ** -0.5)
        kn = k * _seg_rstd(k, bd, ATT_HD) * kw_ref[...]
        q1_ref[...] = qn.astype(BF16)
        k1_ref[...] = kn.astype(BF16)
        v1_ref[...] = v_ref[...].astype(BF16)
        _to_patterns(qn, ((q4_ref, R4), (q16_ref, R16)), scr)
        _to_patterns(kn, ((k4_ref, R4), (k16_ref, R16)), scr)
        _to_patterns(v_ref[...], ((v4_ref, R4), (v16_ref, R16)), scr)

    row = pl.BlockSpec((1, 512), lambda i: (0, 0))
    tok = pl.BlockSpec((TM, 512), lambda i: (i, 0))
    return pl.pallas_call(
        body, grid=(s // TM,),
        in_specs=[pl.BlockSpec((TM, 512), lambda i: (i, COL_ATT_Q)),
                  pl.BlockSpec((TM, 512), lambda i: (i, COL_ATT_K)),
                  pl.BlockSpec((TM, 512), lambda i: (i, COL_ATT_V)),
                  row, row, pl.BlockSpec((512, 128), lambda i: (0, 0))],
        out_specs=[tok] * 3 + [_pattern_spec(R4)] * 3 + [_pattern_spec(R16)] * 3,
        out_shape=[_sds((s, 512), BF16)] * 3 + [_pattern_shape(s, R4, BF16)] * 3 + [_pattern_shape(s, R16, BF16)] * 3,
        scratch_shapes=[SLABS],
        compiler_params=_cp(("parallel",)), name="att_pre")(proj, proj, proj, qw_t, kw_t, bd64)


def _bias_fwd(rel_bias, buckets):
    def body(rb_ref, bk_ref, o_ref):
        for p in range(len(PATTERNS)):
            bk = bk_ref[p]
            for h in range(ATT_HEADS):
                acc = jnp.zeros((BLK, 2 * BLK), F32)
                for b in range(N_BUCKETS):
                    acc = jnp.where(bk == b, rb_ref[h, b], acc)
                o_ref[p, h] = acc

    return pl.pallas_call(
        body,
        in_specs=[pl.BlockSpec(memory_space=pltpu.SMEM), pl.BlockSpec(memory_space=pltpu.VMEM)],
        out_specs=pl.BlockSpec(memory_space=pltpu.VMEM),
        out_shape=_sds((len(PATTERNS), ATT_HEADS, BLK, 2 * BLK)),
        compiler_params=_cp(), name="bias_fwd")(rel_bias, buckets)


def _bias_bwd(ds_accs, buckets):
    def body(ds0_ref, ds1_ref, ds2_ref, bk_ref, o_ref):
        for h in range(ATT_HEADS):
            for b in range(N_BUCKETS):
                tot = jnp.zeros((), F32)
                for p, ds_ref in enumerate((ds0_ref, ds1_ref, ds2_ref)):
                    tot = tot + jnp.sum(jnp.where(bk_ref[p] == b, ds_ref[h], 0.0))
                o_ref[h, b] = tot

    return pl.pallas_call(
        body,
        in_specs=[pl.BlockSpec(memory_space=pltpu.VMEM)] * 4,
        out_specs=pl.BlockSpec(memory_space=pltpu.SMEM),
        out_shape=_sds((ATT_HEADS, N_BUCKETS)),
        compiler_params=_cp(), name="bias_bwd")(*ds_accs, buckets)


QB_FWD = 2
QB_BWD = 4


def _att_masks(has_prev):
    qi = lax.broadcasted_iota(jnp.int32, (BLK, BLK), 0)
    kj = lax.broadcasted_iota(jnp.int32, (BLK, BLK), 1)
    lane = lax.broadcasted_iota(jnp.int32, (BLK, 2 * ATT_HD), 1)
    return jnp.logical_and(kj >= qi, has_prev), kj <= qi, lane < ATT_HD


def _head_lanes(h):
    half = h % 2
    return slice(ATT_HD * h, ATT_HD * h + ATT_HD), slice(ATT_HD * half, ATT_HD * half + ATT_HD)


def _att_scores(qm, kp2, kc2, bias_h, mask_prev, mask_cur):
    s_prev = jnp.where(mask_prev, _dot_nt(qm, kp2) + bias_h[:, :BLK], NEG)
    s_cur = jnp.where(mask_cur, _dot_nt(qm, kc2) + bias_h[:, BLK:], NEG)
    return s_prev, s_cur


def _att_fwd(q, k, v, v_col, bias, p_idx, r, name):
    QB = QB_FWD
    s = q.shape[0]
    nblk = s // BLK
    nseq = nblk // r

    def body(q_ref, kp_ref, kc_ref, vp_ref, vc_ref, b_ref, o_ref, lse_ref):
        j = pl.program_id(0)

        def head(h, rows, masks, q2, kp2, kc2, vp2, vc2):
            mask_prev, mask_cur, lo_half = masks
            out_l, pair_l = _head_lanes(h)
            sel = lo_half if h % 2 == 0 else jnp.logical_not(lo_half)
            qm = jnp.where(sel, q2, jnp.zeros_like(q2))
            s_prev, s_cur = _att_scores(qm, kp2, kc2, b_ref[0, h], mask_prev, mask_cur)
            yield
            m = jnp.maximum(jnp.max(s_prev, axis=1, keepdims=True), jnp.max(s_cur, axis=1, keepdims=True))
            p_prev = jnp.exp(s_prev - m)
            p_cur = jnp.exp(s_cur - m)
            l = jnp.sum(p_prev, axis=1, keepdims=True) + jnp.sum(p_cur, axis=1, keepdims=True)
            yield
            o2 = _dot(p_prev.astype(BF16), vp2) + _dot(p_cur.astype(BF16), vc2)
            o_ref[rows, out_l] = (o2 * (1.0 / l))[:, pair_l]
            lse_ref[rows, HEAD_SLOT * h:HEAD_SLOT * h + HEAD_SLOT] = jnp.broadcast_to(m + jnp.log(l), (BLK, HEAD_SLOT))

        for sub in range(QB):
            rows = slice(BLK * sub, BLK * sub + BLK)
            before = slice(BLK * (sub - 1), BLK * sub)
            masks = _att_masks(((QB * j + sub) % nseq) != 0)
            gens = []
            for pp in range(ATT_HEADS // 2):
                lanes = slice(128 * pp, 128 * pp + 128)
                kp = kp_ref[:, lanes] if sub == 0 else kc_ref[before, lanes]
                vp = vp_ref[:, lanes] if sub == 0 else vc_ref[before, lanes]
                slabs = (q_ref[rows, lanes], kp, kc_ref[rows, lanes], vp, vc_ref[rows, lanes])
                gens += [head(2 * pp, rows, masks, *slabs), head(2 * pp + 1, rows, masks, *slabs)]
            _interleave(gens)

    cur = pl.BlockSpec((QB * BLK, 512), lambda j: (j, 0))
    prev = pl.BlockSpec((BLK, 512), lambda j: (jnp.maximum(QB * j - 1, 0), 0))
    vcur = pl.BlockSpec((QB * BLK, 512), lambda j: (j, v_col))
    vprev = pl.BlockSpec((BLK, 512), lambda j: (jnp.maximum(QB * j - 1, 0), v_col))
    return pl.pallas_call(
        body, grid=(nblk // QB,),
        in_specs=[cur, prev, cur, vprev, vcur,
                  pl.BlockSpec((1, ATT_HEADS, BLK, 2 * BLK), lambda j: (p_idx, 0, 0, 0))],
        out_specs=[cur, pl.BlockSpec((QB * BLK, 128), lambda j: (j, 0))],
        out_shape=[_sds((s, 512)), _sds((s, 128))],
        compiler_params=_cp(("parallel",)), name=name)(q, k, k, v, v, bias)


def _post_fwd(o_dn, proj, o_pats, lse_pats, dnw_t, bd128):
    s = o_dn.shape[0]

    def body(o_ref, z_ref, gate_ref, o1_ref, o4_ref, o16_ref, s1_ref, s4_ref, s16_ref, wn_ref, bd_ref, ex_ref,
             mixed_ref, oatt_ref, l1_ref, l4_ref, l16_ref, scr_a, scr_b, scr_c, scr_d):
        o = o_ref[...]
        z = z_ref[...]
        rstd = _seg_rstd(o, bd_ref[...], DN_HD)
        y_dn = o * rstd * wn_ref[...] * (z * _sigmoid(z))
        mixed_ref[:, 0:512] = y_dn.astype(BF16)
        lses = (s1_ref[...], _from_pattern(s4_ref, R4, scr_a), _from_pattern(s16_ref, R16, scr_b))
        m = jnp.maximum(jnp.maximum(lses[0], lses[1]), lses[2])
        tot = jnp.exp(lses[0] - m) + jnp.exp(lses[1] - m) + jnp.exp(lses[2] - m)
        big_l = m + jnp.log(tot)
        l1_ref[...] = big_l
        _to_patterns(big_l, ((l4_ref, R4), (l16_ref, R16)), scr_a)
        ex = ex_ref[...]
        outs = (o1_ref[...], _from_pattern(o4_ref, R4, scr_c), _from_pattern(o16_ref, R16, scr_d))
        acc = jnp.zeros_like(o)
        for lse_p, o_p in zip(lses, outs):
            acc = acc + _lane_bcast(_split3(jnp.exp(lse_p - big_l)), ex) * o_p
        gate = gate_ref[...]
        oatt_ref[...] = acc
        mixed_ref[:, 512:1024] = (acc * (gate * _sigmoid(gate))).astype(BF16)

    blk = pl.BlockSpec((TM, 512), lambda i: (i, 0))
    cblk = pl.BlockSpec((TM, 128), lambda i: (i, 0))
    p4, p16 = _pattern_spec(R4), _pattern_spec(R16)
    c4, c16 = _pattern_spec(R4, 128), _pattern_spec(R16, 128)
    return pl.pallas_call(
        body, grid=(s // TM,),
        in_specs=[blk, pl.BlockSpec((TM, 512), lambda i: (i, COL_Z)),
                  pl.BlockSpec((TM, 512), lambda i: (i, COL_GATE)), blk, p4, p16, cblk, c4, c16,
                  pl.BlockSpec((1, 512), lambda i: (0, 0)), pl.BlockSpec((512, 128), lambda i: (0, 0)),
                  pl.BlockSpec((128, 512), lambda i: (0, 0))],
        out_specs=[pl.BlockSpec((TM, D_MODEL), lambda i: (i, 0)), blk, cblk, c4, c16],
        out_shape=[_sds((s, D_MODEL), BF16), _sds((s, 512)), _sds((s, 128)), _pattern_shape(s, R4, F32, 128),
                   _pattern_shape(s, R16, F32, 128)],
        scratch_shapes=[SLABS] * 4,
        compiler_params=_cp(("parallel",)), name="post_fwd")(o_dn, proj, proj, *o_pats, *lse_pats, dnw_t, bd128,
                                                              _head_expand())


def _out_fwd(x, mixed, w_out_bf, tgt):
    s = x.shape[0]
    tm = 512

    def body(x_ref, m_ref, w_ref, t_ref, dy_ref, loss_ref):
        i = pl.program_id(0)

        @pl.when(i == 0)
        def _():
            loss_ref[...] = jnp.zeros_like(loss_ref)

        y = x_ref[...] + jnp.dot(m_ref[...], w_ref[...], preferred_element_type=F32)
        err = y - t_ref[...]
        dy_ref[...] = err * (1.0 / D_MODEL)
        part = 0.5 * jnp.sum(jnp.mean(err * err, axis=-1, keepdims=True), axis=0, keepdims=True)
        loss_ref[...] = loss_ref[...] + part

    blk = pl.BlockSpec((tm, D_MODEL), lambda i: (i, 0))
    return pl.pallas_call(
        body, grid=(s // tm,),
        in_specs=[blk, blk, pl.BlockSpec((D_MODEL, D_MODEL), lambda i: (0, 0)), blk],
        out_specs=[blk, pl.BlockSpec((8, 128), lambda i: (0, 0))],
        out_shape=[_sds((s, D_MODEL)), _sds((8, 128))],
        compiler_params=_cp(("arbitrary",)), name="out_fwd")(x, mixed, w_out_bf, tgt)


def _out_bwd(dy, mixed, w_out_bf):
    s = dy.shape[0]
    tm = 512

    def body(dy_ref, m_ref, w_ref, dm_ref, dw_ref):
        i = pl.program_id(0)

        @pl.when(i == 0)
        def _():
            dw_ref[...] = jnp.zeros_like(dw_ref)

        dyb = dy_ref[...].astype(BF16)
        dm_ref[...] = lax.dot_general(dyb, w_ref[...], (((1,), (1,)), ((), ())), preferred_element_type=F32)
        dw_ref[...] = dw_ref[...] + lax.dot_general(m_ref[...], dyb, (((0,), (0,)), ((), ())),
                                                    preferred_element_type=F32)

    blk = pl.BlockSpec((tm, D_MODEL), lambda i: (i, 0))
    full = pl.BlockSpec((D_MODEL, D_MODEL), lambda i: (0, 0))
    return pl.pallas_call(
        body, grid=(s // tm,), in_specs=[blk, blk, full], out_specs=[blk, full],
        out_shape=[_sds((s, D_MODEL)), _sds((D_MODEL, D_MODEL))],
        compiler_params=_cp(("arbitrary",)), name="out_bwd")(dy, mixed, w_out_bf)


def _post_bwd(dmixed, o_dn, proj, o_att, dnw_t, bd128):
    s = o_dn.shape[0]
    tm = TM

    def body(ddn_ref, datt_ref, o_ref, z_ref, gate_ref, oatt_ref, wn_ref, bd128_ref, hc_ref,
             do_ref, dz_ref, dgate_ref, doatt_ref, do4_ref, do16_ref, delta_ref, dl4_ref, dl16_ref, dnw_ref, scr):
        i = pl.program_id(0)

        @pl.when(i == 0)
        def _():
            dnw_ref[...] = jnp.zeros_like(dnw_ref)

        bd128v = bd128_ref[...]
        o = o_ref[...]
        z = z_ref[...]
        wn = wn_ref[...]
        dy = ddn_ref[...]
        rstd = _seg_rstd(o, bd128v, DN_HD)
        nrm = o * rstd
        sz = _sigmoid(z)
        dz_ref[...] = (dy * nrm * wn * _silu_grad(z, sz)).astype(BF16)
        dn = dy * z * sz
        gw = dn * wn
        do_ref[...] = rstd * (gw - nrm * _seg_expand(_seg_mean(gw * nrm, bd128v, DN_HD), bd128v, DN_HD))
        colsum = jnp.sum(dn * nrm, axis=0, keepdims=True)
        fold = colsum[:, 0:128] + colsum[:, 128:256] + colsum[:, 256:384] + colsum[:, 384:512]
        dnw_ref[...] = dnw_ref[...] + fold
        dya = datt_ref[...]
        gate = gate_ref[...]
        oatt = oatt_ref[...]
        sg = _sigmoid(gate)
        dgate_ref[...] = (dya * oatt * _silu_grad(gate, sg)).astype(BF16)
        doa = dya * gate * sg
        doatt_ref[...] = doa.astype(BF16)
        delta = _segsum(doa * oatt, hc_ref[...])
        delta_ref[...] = delta
        _to_patterns(doa, ((do4_ref, R4), (do16_ref, R16)), scr)
        _to_patterns(delta, ((dl4_ref, R4), (dl16_ref, R16)), scr)

    blk = pl.BlockSpec((tm, 512), lambda i: (i, 0))
    cblk = pl.BlockSpec((tm, 128), lambda i: (i, 0))
    p4, p16 = _pattern_spec(R4), _pattern_spec(R16)
    c4, c16 = _pattern_spec(R4, 128), _pattern_spec(R16, 128)
    return pl.pallas_call(
        body, grid=(s // tm,),
        in_specs=[blk, pl.BlockSpec((tm, 512), lambda i: (i, 1)), blk,
                  pl.BlockSpec((tm, 512), lambda i: (i, COL_Z)), pl.BlockSpec((tm, 512), lambda i: (i, COL_GATE)),
                  blk, pl.BlockSpec((1, 512), lambda i: (0, 0)), pl.BlockSpec((512, 128), lambda i: (0, 0)),
                  pl.BlockSpec((512, 128), lambda i: (0, 0))],
        out_specs=[blk, blk, blk, blk, p4, p16, cblk, c4, c16, pl.BlockSpec((8, 128), lambda i: (0, 0))],
        out_shape=[_sds((s, 512))] + [_sds((s, 512), BF16)] * 3 + [_pattern_shape(s, R4, BF16),
                                          _pattern_shape(s, R16, BF16), _sds((s, 128)),
                                          _pattern_shape(s, R4, F32, 128), _pattern_shape(s, R16, F32, 128),
                                          _sds((8, 128))],
        scratch_shapes=[SLABS],
        compiler_params=_cp(("arbitrary",)), name="post_bwd")(dmixed, dmixed, o_dn, proj, proj, o_att, dnw_t,
                                                               bd128, _head_compact())


def _att_bwd(q, k, v, v_col, do, big_l, delta, bias, p_idx, r, name):
    QB = QB_BWD
    s = q.shape[0]
    nblk = s // BLK
    nseq = nblk // r
    nstep = nblk // QB

    def body(q_ref, kp_ref, kc_ref, vp_ref, vc_ref, do_ref, l_ref, dl_ref, b_ref,
             dq_ref, dk_ref, dv_ref, ds_ref, dkc_ref, dvc_ref):
        j = pl.program_id(0)

        @pl.when(j == 0)
        def _():
            dkc_ref[...] = jnp.zeros_like(dkc_ref)
            dvc_ref[...] = jnp.zeros_like(dvc_ref)
            ds_ref[...] = jnp.zeros_like(ds_ref)

        @pl.when(j < nstep)
        def _():
            def head(h, sub, masks, q2, do2, kp2, kc2, vp2, vc2):
                mask_prev, mask_cur, lo_half = masks
                rows = slice(BLK * sub, BLK * sub + BLK)
                out_l, pair_l = _head_lanes(h)
                sel = lo_half if h % 2 == 0 else jnp.logical_not(lo_half)
                qm = jnp.where(sel, q2, jnp.zeros_like(q2))
                dom = jnp.where(sel, do2, jnp.zeros_like(do2))
                s_prev, s_cur = _att_scores(qm, kp2, kc2, b_ref[0, h], mask_prev, mask_cur)
                dp_prev = _dot_nt(dom, vp2)
                dp_cur = _dot_nt(dom, vc2)
                yield
                lh = l_ref[rows, HEAD_SLOT * h:HEAD_SLOT * h + 1]
                dh = dl_ref[rows, HEAD_SLOT * h:HEAD_SLOT * h + 1]
                p_prev = jnp.exp(s_prev - lh)
                p_cur = jnp.exp(s_cur - lh)
                ds_prev = p_prev * (dp_prev - dh)
                ds_cur = p_cur * (dp_cur - dh)
                ds_ref[h, :, 0:BLK] = ds_ref[h, :, 0:BLK] + ds_prev
                ds_ref[h, :, BLK:2 * BLK] = ds_ref[h, :, BLK:2 * BLK] + ds_cur
                dsb_prev, dsb_cur = ds_prev.astype(BF16), ds_cur.astype(BF16)
                pb_prev, pb_cur = p_prev.astype(BF16), p_cur.astype(BF16)
                yield
                dq_ref[rows, out_l] = (_dot(dsb_prev, kp2) + _dot(dsb_cur, kc2))[:, pair_l]
                dk_prev = _dot_tn(dsb_prev, q2)[:, pair_l]
                dv_prev = _dot_tn(pb_prev, do2)[:, pair_l]
                if sub == 0:
                    last = slice(BLK * (QB - 1), BLK * QB)
                    dk_ref[last, out_l] = dkc_ref[last, out_l] + dk_prev
                    dv_ref[last, out_l] = dvc_ref[last, out_l] + dv_prev
                else:
                    before = slice(BLK * (sub - 1), BLK * sub)
                    dkc_ref[before, out_l] = dkc_ref[before, out_l] + dk_prev
                    dvc_ref[before, out_l] = dvc_ref[before, out_l] + dv_prev
                yield
                dkc_ref[rows, out_l] = _dot_tn(dsb_cur, q2)[:, pair_l]
                dvc_ref[rows, out_l] = _dot_tn(pb_cur, do2)[:, pair_l]

            done = slice(0, BLK * (QB - 1))
            dk_ref[done, :] = dkc_ref[done, :]
            dv_ref[done, :] = dvc_ref[done, :]
            for sub in range(QB):
                rows = slice(BLK * sub, BLK * sub + BLK)
                before = slice(BLK * (sub - 1), BLK * sub)
                masks = _att_masks(((QB * j + sub) % nseq) != 0)
                gens = []
                for pp in range(ATT_HEADS // 2):
                    lanes = slice(128 * pp, 128 * pp + 128)
                    kp = kp_ref[:, lanes] if sub == 0 else kc_ref[before, lanes]
                    vp = vp_ref[:, lanes] if sub == 0 else vc_ref[before, lanes]
                    slabs = (q_ref[rows, lanes], do_ref[rows, lanes], kp, kc_ref[rows, lanes], vp,
                             vc_ref[rows, lanes])
                    gens += [head(2 * pp, sub, masks, *slabs), head(2 * pp + 1, sub, masks, *slabs)]
                _interleave(gens)

        @pl.when(j == nstep)
        def _():
            dk_ref[...] = dkc_ref[...]
            dv_ref[...] = dvc_ref[...]

    last_step = nstep - 1
    cur = pl.BlockSpec((QB * BLK, 512), lambda j: (jnp.minimum(j, last_step), 0))
    compact = pl.BlockSpec((QB * BLK, 128), lambda j: (jnp.minimum(j, last_step), 0))
    lag = pl.BlockSpec((QB * BLK, 512), lambda j: (jnp.clip(j - 1, 0, last_step), 0))
    prev = pl.BlockSpec((BLK, 512), lambda j: (jnp.clip(QB * j - 1, 0, nblk - 1), 0))
    vcur = pl.BlockSpec((QB * BLK, 512), lambda j: (jnp.minimum(j, last_step), v_col))
    vprev = pl.BlockSpec((BLK, 512), lambda j: (jnp.clip(QB * j - 1, 0, nblk - 1), v_col))
    return pl.pallas_call(
        body, grid=(nstep + 1,),
        in_specs=[cur, prev, cur, vprev, vcur, cur, compact, compact,
                  pl.BlockSpec((1, ATT_HEADS, BLK, 2 * BLK), lambda j: (p_idx, 0, 0, 0))],
        out_specs=[cur, lag, lag, pl.BlockSpec((ATT_HEADS, BLK, 2 * BLK), lambda j: (0, 0, 0))],
        out_shape=[_sds((s, 512))] * 3 + [_sds((ATT_HEADS, BLK, 2 * BLK))],
        scratch_shapes=[pltpu.VMEM((QB * BLK, 512), F32), pltpu.VMEM((QB * BLK, 512), F32)],
        compiler_params=_cp(("arbitrary",)), name=name)(q, k, k, v, v, do, big_l, delta, bias)


def _att_pre_bwd(dq_pats, dk_pats, dv_pats, proj, qw_t, kw_t, bd64):
    s = proj.shape[0]
    tm = TM

    def body(dq1_ref, dq4_ref, dq16_ref, dk1_ref, dk4_ref, dk16_ref, dv1_ref, dv4_ref, dv16_ref,
             q_ref, k_ref, qw_ref, kw_ref, bd_ref,
             dqr_ref, dkr_ref, dvr_ref, dqw_ref, dkw_ref, scr4, scr16):
        i = pl.program_id(0)

        @pl.when(i == 0)
        def _():
            dqw_ref[...] = jnp.zeros_like(dqw_ref)
            dkw_ref[...] = jnp.zeros_like(dkw_ref)

        bd = bd_ref[...]

        def total(d1_ref, d4_ref, d16_ref):
            return d1_ref[...] + _from_pattern(d4_ref, R4, scr4) + _from_pattern(d16_ref, R16, scr16)

        def one(d_refs, x_ref, w_ref, scale, dx_ref, dw_ref):
            dy = total(*d_refs) * scale
            x = x_ref[...]
            rstd = _seg_rstd(x, bd, ATT_HD)
            nrm = x * rstd
            dw_ref[...] = dw_ref[...] + jnp.sum(dy * nrm, axis=0, keepdims=True)
            g = dy * w_ref[...]
            dx_ref[...] = (rstd * (g - nrm * _seg_expand(_seg_mean(g * nrm, bd, ATT_HD), bd, ATT_HD))).astype(BF16)

        one((dq1_ref, dq4_ref, dq16_ref), q_ref, qw_ref, ATT_HD ** -0.5, dqr_ref, dqw_ref)
        one((dk1_ref, dk4_ref, dk16_ref), k_ref, kw_ref, 1.0, dkr_ref, dkw_ref)
        dvr_ref[...] = total(dv1_ref, dv4_ref, dv16_ref).astype(BF16)

    blk = pl.BlockSpec((tm, 512), lambda i: (i, 0))
    pats = [blk, _pattern_spec(R4), _pattern_spec(R16)]
    row = pl.BlockSpec((1, 512), lambda i: (0, 0))
    acc = pl.BlockSpec((8, 512), lambda i: (0, 0))
    return pl.pallas_call(
        body, grid=(s // tm,),
        in_specs=pats * 3 + [pl.BlockSpec((tm, 512), lambda i: (i, COL_ATT_Q)),
                             pl.BlockSpec((tm, 512), lambda i: (i, COL_ATT_K)), row, row,
                             pl.BlockSpec((512, 128), lambda i: (0, 0))],
        out_specs=[blk, blk, blk, acc, acc],
        out_shape=[_sds((s, 512), BF16)] * 3 + [_sds((8, 512))] * 2,
        scratch_shapes=[SLABS] * 2,
        compiler_params=_cp(("arbitrary",)), name="att_pre_bwd")(*dq_pats, *dk_pats, *dv_pats, proj, proj,
                                                                  qw_t, kw_t, bd64)


def _dn_scan_bwd(do, sp, qd, kt, w, vn, attn, gl):
    s = do.shape[0]
    nc = s // CHUNK

    def body(do_ref, sp_ref, qd_ref, kt_ref, w_ref, vn_ref, attn_ref, gl_ref,
             du_ref, dqd_ref, dkt_ref, dw_ref, dattn_ref, dgl_ref, ds_ref):
        n = pl.program_id(0)

        @pl.when(n == 0)
        def _():
            ds_ref[...] = jnp.zeros_like(ds_ref)

        def chain(cc, h):
            rows = slice(CHUNK * cc, CHUNK * cc + CHUNK)
            lanes = slice(128 * h, 128 * h + 128)
            dsn = ds_ref[h]
            st = sp_ref[cc, h]
            dsb, stb = dsn.astype(BF16), st.astype(BF16)
            dob = do_ref[rows, lanes].astype(BF16)
            vnb = vn_ref[rows, lanes].astype(BF16)
            dvn = _dot_tn(attn_ref[cc, h].astype(BF16), dob) + _dot(kt_ref[rows, lanes].astype(BF16), dsb)
            du_ref[rows, lanes] = dvn
            dqd_ref[rows, lanes] = _dot_nt(dob, stb)
            dattn_ref[cc, h] = _dot_nt(dob, vnb)
            dkt_ref[rows, lanes] = _dot_nt(vnb, dsb)
            tot = jnp.sum(jnp.sum(st * dsn, axis=1, keepdims=True), axis=0, keepdims=True)
            dgl_ref[cc, h] = jnp.broadcast_to(tot, (1, 128))
            qdo = _dot_tn(qd_ref[rows, lanes].astype(BF16), dob)
            yield
            dvb = dvn.astype(BF16)
            dw_ref[rows, lanes] = -_dot_nt(dvb, stb)
            ds_ref[h] = qdo + dsn * gl_ref[cc, h] - _dot_tn(w_ref[rows, lanes].astype(BF16), dvb)

        for cc in reversed(range(CPS_SCAN)):
            _interleave([chain(cc, h) for h in range(DN_HEADS)])

    nsteps = nc // CPS_SCAN
    big = pl.BlockSpec((CPS_SCAN * CHUNK, 512), lambda n: (nsteps - 1 - n, 0))
    sq = pl.BlockSpec((CPS_SCAN, DN_HEADS, CHUNK, CHUNK), lambda n: (nsteps - 1 - n, 0, 0, 0))
    glb = pl.BlockSpec((CPS_SCAN, DN_HEADS, 1, 128), lambda n: (nsteps - 1 - n, 0, 0, 0))
    return pl.pallas_call(
        body, grid=(nsteps,),
        in_specs=[big, pl.BlockSpec((CPS_SCAN, DN_HEADS, DN_HD, DN_HD), lambda n: (nsteps - 1 - n, 0, 0, 0)),
                  big, big, big, big, sq, glb],
        out_specs=[big, big, big, big, sq, glb],
        out_shape=[_sds((s, 512))] * 4 + [_sds((nc, DN_HEADS, CHUNK, CHUNK)), _sds((nc, DN_HEADS, 1, 128))],
        scratch_shapes=[pltpu.VMEM((DN_HEADS, DN_HD, DN_HD), F32)],
        compiler_params=_cp(("arbitrary",)), name="dn_scan_bwd")(do, sp, qd, kt, w, vn, attn, gl)


def _dn_prep_bwd(qn, kn, v, bg, bgt, tri, sel, t_inv, attn, u, w, du, dw, dqd, dkt, dattn, dgl):
    s = qn.shape[0]
    nc = s // CHUNK

    def body(q_ref, k_ref, v_ref, bg_ref, bgt_ref, tri_ref, sel_ref, t_ref, attn_ref, u_ref, w_ref,
             du_ref, dw_ref, dqd_ref, dkt_ref, dattn_ref, dgl_ref,
             dq_ref, dk_ref, dv_ref, dbg_ref):
        tri_v = tri_ref[...]
        lane = lax.broadcasted_iota(jnp.int32, (CHUNK, 128), 1)
        rowi = lax.broadcasted_iota(jnp.int32, (CHUNK, 128), 0)
        ones_b = jnp.ones((CHUNK, 128), BF16)
        ones_sq = jnp.ones((128, 128), BF16)
        parts = [[] for _ in range(CPS)]

        def chain(cc, h, bg3, gc3, gc_row):
            rows = slice(CHUNK * cc, CHUNK * cc + CHUNK)
            lanes = slice(128 * h, 128 * h + 128)
            gcc, beta, incl, strict, decay, gl = _chunk_common(bg3, gc3, gc_row, h, sel_ref)
            yield
            q = q_ref[rows, lanes]
            k = k_ref[rows, lanes]
            vv = v_ref[rows, lanes]
            ts = _split(t_ref[cc, h])
            egc = jnp.exp(gcc)
            kb = k * beta
            a_mat = jnp.where(strict, _mm_nt(kb, k) * decay, 0.0)
            dvb = _mm3_tn(ts, _split(du_ref[rows, lanes]))
            dkbg = _mm3_tn(ts, _split(dw_ref[rows, lanes]))
            yield
            d_a = jnp.where(strict, -(_mm_nt(dvb, u_ref[rows, lanes]) + _mm_nt(dkbg, w_ref[rows, lanes])), 0.0)
            d_m = d_a * decay
            dattn_m = jnp.where(incl, dattn_ref[cc, h], 0.0)
            dqk = dattn_m * decay
            e_hi, e_lo = _split(d_a * a_mat + dattn_m * attn_ref[cc, h])
            yield
            dkb = _mm(d_m, k)
            dk = _mm_tn(d_m, kb) + _mm_tn(dqk, q)
            dq = _mm(dqk, k)
            e_colsum = _dot_tn(e_hi, ones_b) + _dot_tn(e_lo, ones_b)
            e_rowsum = _dot(e_hi, ones_b) + _dot(e_lo, ones_b)
            dqd = dqd_ref[rows, lanes]
            dkt = dkt_ref[rows, lanes]
            s_dqd = _rowsum_b(dqd * q, ones_sq)
            s_dkt = _rowsum_b(dkt * k, ones_sq)
            rk = _rowsum_b(dkbg * k, ones_sq)
            s_dkb = _rowsum_b(dkb * k, ones_sq)
            s_dvb = _rowsum_b(dvb * vv, ones_sq)
            yield
            tail = jnp.exp(gl - gcc)
            r = s_dkt * tail
            dgl_tot = jnp.sum(r, axis=0, keepdims=True) + dgl_ref[cc, h] * jnp.exp(gl)
            dgc = e_rowsum - e_colsum + s_dqd * egc - r + rk * beta * egc
            dgc = dgc + jnp.where(rowi == CHUNK - 1, dgl_tot, 0.0)
            dq_ref[rows, lanes] = dq + dqd * egc
            dk_ref[rows, lanes] = dk + dkt * tail + dkbg * (beta * egc) + dkb * beta
            dv_ref[rows, lanes] = dvb * beta
            parts[cc].append((h, dgc, rk * egc + s_dkb + s_dvb))

        gens = []
        for cc in range(CPS):
            bgv = bg_ref[CHUNK * cc:CHUNK * cc + CHUNK, :]
            bg3, gc3, gc_row = _chunk_cumsum(bgv, bgt_ref[cc], tri_v)
            gens += [chain(cc, h, bg3, gc3, gc_row) for h in range(DN_HEADS)]
        _interleave(gens)
        for cc in range(CPS):
            dgc_mat = jnp.zeros((CHUNK, 128), F32)
            dbeta_mat = jnp.zeros((CHUNK, 128), F32)
            for h, dgc, dbeta in parts[cc]:
                dgc_mat = dgc_mat + jnp.where(lane == DN_HEADS + h, dgc, 0.0)
                dbeta_mat = dbeta_mat + jnp.where(lane == h, dbeta, 0.0)
            dbg_ref[CHUNK * cc:CHUNK * cc + CHUNK, :] = _mmx_tn(tri_v, dgc_mat) + dbeta_mat

    big = pl.BlockSpec((CPS * CHUNK, 512), lambda n: (n, 0))
    sq = pl.BlockSpec((CPS, DN_HEADS, CHUNK, CHUNK), lambda n: (n, 0, 0, 0))
    glb = pl.BlockSpec((CPS, DN_HEADS, 1, 128), lambda n: (n, 0, 0, 0))
    small = pl.BlockSpec((CPS * CHUNK, 128), lambda n: (n, 0))
    return pl.pallas_call(
        body, grid=(nc // CPS,),
        in_specs=[big, big, big, small, pl.BlockSpec((CPS, 8, CHUNK), lambda n: (n, 0, 0)),
                  pl.BlockSpec((CHUNK, CHUNK), lambda n: (0, 0)),
                  pl.BlockSpec((8, 128, 128), lambda n: (0, 0, 0)), sq, sq, big, big,
                  big, big, big, big, sq, glb],
        out_specs=[big, big, big, small],
        out_shape=[_sds((s, 512))] * 3 + [_sds((s, 128))],
        compiler_params=_cp(("parallel",)), name="dn_prep_bwd")(qn, kn, v, bg, bgt, tri, sel, t_inv, attn, u, w,
                                                                  du, dw, dqd, dkt, dattn, dgl)


def _dn_pre_bwd(dqn, dkn, dv, dbg, proj, conv_w8, alog_l, dtb_l):
    s = proj.shape[0]
    tr = 512
    nh = tr // 8

    def body(dq_ref, dk_ref, dv_ref, dbg_ref, u_ref, halo_ref, ba_ref, w_ref, al_ref, dt_ref,
             dy_ref, dba_ref, dsm_ref):
        i = pl.program_id(0)

        @pl.when(i == 0)
        def _():
            dsm_ref[...] = jnp.zeros_like(dsm_ref)

        keep = (i > 0).astype(F32)
        for c in range(12):
            lanes = slice(128 * c, 128 * c + 128)
            y = _conv_group(u_ref[:, lanes], halo_ref[:, lanes] * keep, w_ref, c)
            sg = _sigmoid(y)
            sv = y * sg
            if c < 8:
                rs = lax.rsqrt(jnp.sum(sv * sv, axis=1, keepdims=True) + EPS)
                n = sv * rs
                if c < 4:
                    dn = dq_ref[:, lanes] * (DN_HD ** -0.5)
                else:
                    dn = dk_ref[:, slice(128 * (c - 4), 128 * (c - 3))]
                dsv = rs * (dn - n * jnp.sum(dn * n, axis=1, keepdims=True))
            else:
                dsv = dv_ref[:, slice(128 * (c - 8), 128 * (c - 7))]
            dy_ref[:, lanes] = dsv * _silu_grad(y, sg)
        _, lane, sig_b, t, nega, g = _beta_g(ba_ref[...], al_ref[...], dt_ref[...])
        dbg = dbg_ref[...]
        da = dbg * nega * _sigmoid(t)
        is_b = lane < DN_HEADS
        is_a = jnp.logical_and(lane >= DN_HEADS, lane < 2 * DN_HEADS)
        dba_ref[...] = jnp.where(is_b, dbg * sig_b * (1.0 - sig_b), jnp.where(is_a, da, 0.0)).astype(BF16)
        d_alog = jnp.sum(jnp.where(is_a, dbg * g, 0.0), axis=0, keepdims=True)
        d_dtb = jnp.sum(jnp.where(is_a, da, 0.0), axis=0, keepdims=True)
        row = lax.broadcasted_iota(jnp.int32, (8, 128), 0)
        dsm_ref[...] = dsm_ref[...] + jnp.where(row == 0, d_alog, jnp.where(row == 1, d_dtb, 0.0))

    blk = pl.BlockSpec((tr, 512), lambda i: (i, 0))
    return pl.pallas_call(
        body, grid=(s // tr,),
        in_specs=[blk, blk, blk, pl.BlockSpec((tr, 128), lambda i: (i, 0)),
                  pl.BlockSpec((tr, 1536), lambda i: (i, 0)),
                  pl.BlockSpec((8, 1536), lambda i: (jnp.maximum(i * nh - 1, 0), 0)),
                  pl.BlockSpec((tr, 128), lambda i: (i, COL_BA_128)),
                  pl.BlockSpec((8, 1536), lambda i: (0, 0)),
                  pl.BlockSpec((1, 128), lambda i: (0, 0)), pl.BlockSpec((1, 128), lambda i: (0, 0))],
        out_specs=[pl.BlockSpec((tr, 1536), lambda i: (i, 0)), pl.BlockSpec((tr, 128), lambda i: (i, 0)),
                   pl.BlockSpec((8, 128), lambda i: (0, 0))],
        out_shape=[_sds((s, 1536)), _sds((s, 128), BF16), _sds((8, 128))],
        compiler_params=_cp(("arbitrary",)), name="dn_pre_bwd")(dqn, dkn, dv, dbg, proj, proj, proj, conv_w8,
                                                                 alog_l, dtb_l)


def _conv_bwd(dy, proj, conv_w8):
    s = dy.shape[0]
    tr = 512
    nh = tr // 8
    nblk = s // tr

    def body(dy_ref, dyn_ref, u_ref, halo_ref, w_ref, du_ref, dw_ref):
        i = pl.program_id(0)

        @pl.when(i == 0)
        def _():
            dw_ref[...] = jnp.zeros_like(dw_ref)

        keep_prev = (i > 0).astype(F32)
        keep_next = (i < nblk - 1).astype(F32)
        row = lax.broadcasted_iota(jnp.int32, (8, 128), 0)
        for c in range(12):
            lanes = slice(128 * c, 128 * c + 128)
            dyc = dy_ref[:, lanes]
            dcat = jnp.concatenate([dyc, dyn_ref[:, lanes] * keep_next], axis=0)
            xcat = jnp.concatenate([halo_ref[:, lanes] * keep_prev, u_ref[:, lanes]], axis=0)
            du = dyc * w_ref[CONV_W - 1:CONV_W, lanes]
            dwc = jnp.where(row == CONV_W - 1, jnp.sum(dyc * u_ref[:, lanes], axis=0, keepdims=True), 0.0)
            for k in range(1, CONV_W):
                du = du + pltpu.roll(dcat, tr + 8 - k, 0)[0:tr] * w_ref[CONV_W - 1 - k:CONV_W - k, lanes]
                ush = pltpu.roll(xcat, k, 0)[8:8 + tr]
                dwc = dwc + jnp.where(row == CONV_W - 1 - k, jnp.sum(dyc * ush, axis=0, keepdims=True), 0.0)
            du_ref[:, lanes] = du.astype(BF16)
            dw_ref[:, lanes] = dw_ref[:, lanes] + dwc

    return pl.pallas_call(
        body, grid=(nblk,),
        in_specs=[pl.BlockSpec((tr, 1536), lambda i: (i, 0)),
                  pl.BlockSpec((8, 1536), lambda i: (jnp.minimum((i + 1) * nh, s // 8 - 1), 0)),
                  pl.BlockSpec((tr, 1536), lambda i: (i, 0)),
                  pl.BlockSpec((8, 1536), lambda i: (jnp.maximum(i * nh - 1, 0), 0)),
                  pl.BlockSpec((8, 1536), lambda i: (0, 0))],
        out_specs=[pl.BlockSpec((tr, 1536), lambda i: (i, 0)), pl.BlockSpec((8, 1536), lambda i: (0, 0))],
        out_shape=[_sds((s, 1536), BF16), _sds((8, 1536))],
        compiler_params=_cp(("arbitrary",)), name="conv_bwd")(dy, dy, proj, proj, conv_w8)


PIECE_WIDTHS = (1536, 512, 512, 512, 512, 512, 128)


def _in_bwd_dx(pieces, w_bf, x, dy, norm_w, chip_sums=None):
    s = x.shape[0]
    tm = 512
    n_piece = len(PIECE_WIDTHS)
    n_step = s // tm
    n_sent = 0 if chip_sums is None else 3 * len(chip_sums)

    def body(*refs):
        piece_refs = refs[:n_piece]
        w_ref, x_ref, dy_ref, nw_ref = refs[n_piece:n_piece + 4]
        rest = refs[n_piece + 4:]
        i = pl.program_id(0)

        def copies():
            hb_refs = rest[:n_sent // 3]
            recv_refs = rest[n_sent // 3 + 2:2 * (n_sent // 3) + 2]
            send_sems, recv_sems = rest[-2:]
            xi, yi, ci = _position()
            out = []
            for j, (px, py) in enumerate(_other_chips(xi, yi)):
                for t, (src, dst) in enumerate(zip(hb_refs, recv_refs)):
                    k = len(hb_refs) * j + t
                    out.append(pltpu.make_async_remote_copy(
                        src_ref=src.at[2 * px + py], dst_ref=dst.at[j], send_sem=send_sems.at[k],
                        recv_sem=recv_sems.at[k], device_id=(px, py, ci), device_id_type=MESH_ID))
            return out

        dx_ref, dnw_ref = rest[n_sent // 3:n_sent // 3 + 2]

        @pl.when(i == 0)
        def _():
            dnw_ref[...] = jnp.zeros_like(dnw_ref)
            for cp in (copies() if n_sent else []):
                cp.start()

        dp = jnp.concatenate([r[...] for r in piece_refs], axis=1)
        dh = _dot(dp, w_ref[...])
        xv = x_ref[...]
        rstd = lax.rsqrt(jnp.mean(xv * xv, axis=-1, keepdims=True) + EPS)
        xh = xv * rstd
        dnw_ref[...] = dnw_ref[...] + jnp.sum(dh * xh, axis=0, keepdims=True)
        g = dh * nw_ref[...]
        dx_ref[...] = rstd * (g - xh * jnp.mean(g * xh, axis=-1, keepdims=True)) + dy_ref[...]

        if n_sent:
            @pl.when(i == n_step - 1)
            def _():
                cps = copies()
                for cp in cps:
                    cp.wait_recv()
                for cp in cps:
                    cp.wait_send()

    def blk(n):
        return pl.BlockSpec((tm, n), lambda i: (i, 0))

    in_specs = [blk(n) for n in PIECE_WIDTHS] + [pl.BlockSpec((D_IN_PAD, D_MODEL), lambda i: (0, 0)), blk(D_MODEL),
                                                blk(D_MODEL), pl.BlockSpec((1, D_MODEL), lambda i: (0, 0))]
    out_specs = [blk(D_MODEL), pl.BlockSpec((8, D_MODEL), lambda i: (0, 0))]
    out_shape = [_sds((s, D_MODEL)), _sds((8, D_MODEL))]
    scratch = []
    extra = ()
    if n_sent:
        extra = tuple(chip_sums)
        in_specs += [ANY] * len(extra)
        out_specs += [ANY] * len(extra)
        out_shape += [_sds((3,) + a.shape[1:], a.dtype) for a in extra]
        scratch = [pltpu.SemaphoreType.DMA((n_sent,)), pltpu.SemaphoreType.DMA((n_sent,))]
    return pl.pallas_call(
        body, grid=(n_step,), in_specs=in_specs, out_specs=out_specs, out_shape=out_shape, scratch_shapes=scratch,
        compiler_params=_cp(("arbitrary",)), name="in_bwd_dx")(*pieces, w_bf, x, dy, norm_w, *extra)


def _in_bwd_dw(pieces, x, norm_w):
    s = x.shape[0]
    tm = 512
    n_piece = len(PIECE_WIDTHS)

    def body(*refs):
        piece_refs = refs[:n_piece]
        x_ref, nw_ref, dw_ref = refs[n_piece:]
        i = pl.program_id(0)

        @pl.when(i == 0)
        def _():
            dw_ref[...] = jnp.zeros_like(dw_ref)

        xv = x_ref[...]
        rstd = lax.rsqrt(jnp.mean(xv * xv, axis=-1, keepdims=True) + EPS)
        h = (xv * rstd * nw_ref[...]).astype(BF16)
        at = 0
        for r, width in zip(piece_refs, PIECE_WIDTHS):
            dw_ref[at:at + width, :] = dw_ref[at:at + width, :] + _dot_tn(r[...], h)
            at += width

    return pl.pallas_call(
        body, grid=(s // tm,),
        in_specs=[pl.BlockSpec((tm, n), lambda i: (i, 0)) for n in PIECE_WIDTHS]
        + [pl.BlockSpec((tm, D_MODEL), lambda i: (i, 0)), pl.BlockSpec((1, D_MODEL), lambda i: (0, 0))],
        out_specs=pl.BlockSpec((D_IN_PAD, D_MODEL), lambda i: (0, 0)),
        out_shape=_sds((D_IN_PAD, D_MODEL)),
        compiler_params=_cp(("arbitrary",)), name="in_bwd_dw")(*pieces, x, norm_w)


def _flat(a):
    return a.reshape(-1, a.shape[-1])


def _as_pattern(a, r):
    return a if r == 1 else a.reshape(r, a.shape[0] // r, a.shape[1])


D_SHARD = D_IN // N_CHIPS
BA_START = 4 * D_DN
BA_PACKED = 4096
S1_HEAD = BA_START - D_SHARD
S1_BA = 2 * D_SHARD - BA_START


def _pack_rows(g):
    pad = jnp.zeros((D_IN_PAD - D_IN, g.shape[2]), g.dtype)
    s2_ba = 2 * DN_HEADS - S1_BA
    return jnp.concatenate([g[0][:D_SHARD], g[1][:S1_HEAD], g[2][s2_ba:D_SHARD], g[3][:D_SHARD],
                            g[1][S1_HEAD:D_SHARD], g[2][:s2_ba], pad], axis=0)


def _pack_w(g):
    n = g.shape[2]

    def body(g_ref, o_ref):
        g32 = g_ref.bitcast(jnp.uint32)
        o32 = o_ref.bitcast(jnp.uint32)
        full, head, ba1 = D_SHARD // 2, S1_HEAD // 2, S1_BA // 2
        ba2 = DN_HEADS - ba1
        pieces = [(0, 0, full), (1, 0, head), (2, ba2, full), (3, 0, full), (1, head, full), (2, 0, ba2)]
        at = 0
        for chip, lo, hi in pieces:
            o32[at:at + hi - lo, :] = g32[chip, lo:hi, :]
            at += hi - lo
        o32[at:D_IN_PAD // 2, :] = jnp.zeros((D_IN_PAD // 2 - at, n), jnp.uint32)

    vm = pl.BlockSpec(memory_space=pltpu.VMEM)
    return pl.pallas_call(body, in_specs=[vm], out_specs=vm, out_shape=_sds((D_IN_PAD, n), g.dtype),
                          compiler_params=_cp(), name="pack_w")(g)


def _unpack_w(p, rows):
    n = p.shape[1]
    tn = 256
    mid = BA_PACKED + S1_BA
    pieces = [(0, 0, (0, D_SHARD)), (1, 0, (D_SHARD, BA_START)), (1, S1_HEAD, (BA_PACKED, mid)),
              (2, 0, (mid, BA_PACKED + 2 * DN_HEADS)), (2, 2 * DN_HEADS - S1_BA, (BA_START, BA_START + S1_HEAD)),
              (3, 0, (BA_START + S1_HEAD, BA_PACKED))]

    def body(p_ref, o_ref):
        for chip, at, (lo, hi) in pieces:
            o_ref[chip, at:at + hi - lo, :] = p_ref[lo:hi, :]
        for chip in range(N_CHIPS):
            o_ref[chip, D_SHARD:rows, :] = jnp.zeros((rows - D_SHARD, tn), p.dtype)

    return pl.pallas_call(
        body, grid=(n // tn,),
        in_specs=[pl.BlockSpec((D_IN_PAD, tn), lambda j: (0, j))],
        out_specs=pl.BlockSpec((N_CHIPS, rows, tn), lambda j: (0, 0, j)),
        out_shape=_sds((N_CHIPS, rows, n), p.dtype),
        compiler_params=_cp(("parallel",)), name="unpack_w")(p)


def _unpack_rows(p, rows):
    mid = BA_PACKED + S1_BA
    pad = jnp.zeros((rows - D_SHARD, p.shape[1]), p.dtype)
    return jnp.stack([jnp.concatenate([p[0:D_SHARD], pad], axis=0),
                      jnp.concatenate([p[D_SHARD:BA_START], p[BA_PACKED:mid], pad], axis=0),
                      jnp.concatenate([p[mid:BA_PACKED + 2 * DN_HEADS], p[BA_START:BA_START + S1_HEAD], pad], axis=0),
                      jnp.concatenate([p[BA_START + S1_HEAD:BA_PACKED], pad], axis=0)])


def _lane_row(vec, offset):
    return jnp.pad(vec.reshape(1, -1), ((0, 0), (offset, 128 - offset - vec.shape[0])))


def _local_step(x, tgt, norm_w, w_bf, conv_w, a_log, dt_bias, dn_norm_w, q_norm_w, k_norm_w, rel_bias, w_out_bf):
    s = x.shape[0]
    nc = s // CHUNK
    conv_w8 = jnp.pad(conv_w, ((0, 8 - CONV_W), (0, 0)))
    alog_l = _lane_row(a_log.reshape(-1), DN_HEADS)
    dtb_l = _lane_row(dt_bias.reshape(-1), DN_HEADS)
    dnw_t = jnp.tile(dn_norm_w.reshape(1, DN_HD), (1, DN_HEADS))
    qw_t = jnp.tile(q_norm_w.reshape(1, ATT_HD), (1, ATT_HEADS))
    kw_t = jnp.tile(k_norm_w.reshape(1, ATT_HD), (1, ATT_HEADS))
    bd128 = _compact_mat(DN_HD)
    bd64 = _compact_mat(ATT_HD)
    tri = _tri_incl()
    sel = _lane_select()
    buckets = _bucket_tables()

    proj = _in_proj(x, norm_w, w_bf)
    qn, kn, v_dn, bg = _dn_pre(proj, conv_w8, alog_l, dtb_l)
    bgt = bg[:, 0:8].reshape(nc, CHUNK, 8).transpose(0, 2, 1)
    u, w, qd, kt, attn, t_inv, gl = _dn_prep(qn, kn, v_dn, bg, bgt, tri, sel)
    o_dn, vn, sp = _dn_scan(u, w, qd, kt, attn, gl)
    q1, k1, v1, q4, k4, v4, q16, k16, v16 = _att_pre(proj, qw_t, kw_t, bd64)
    rs = [r for _, r in PATTERNS]
    qkv = [(q1, k1, v1, 0), (_flat(q4), _flat(k4), _flat(v4), 0), (_flat(q16), _flat(k16), _flat(v16), 0)]
    bias = _bias_fwd(rel_bias, buckets)
    o_pats, lse_pats = [], []
    for p, r in enumerate(rs):
        o_p, lse_p = _att_fwd(*qkv[p], bias, p, r, "att_fwd_r%d" % r)
        o_pats.append(_as_pattern(o_p, r))
        lse_pats.append(_as_pattern(lse_p, r))
    mixed, o_att, l1, l4, l16 = _post_fwd(o_dn, proj, o_pats, lse_pats, dnw_t, bd128)
    dy, loss_blk = _out_fwd(x, mixed, w_out_bf, tgt)

    dmixed, d_w_out = _out_bwd(dy, mixed, w_out_bf)
    do_dn, dz, dgate, do1, do4, do16, dl1, dl4, dl16, d_dnw = _post_bwd(dmixed, o_dn, proj, o_att, dnw_t, bd128)
    side = [(do1, l1, dl1), (_flat(do4), _flat(l4), _flat(dl4)), (_flat(do16), _flat(l16), _flat(dl16))]
    dq_pats, dk_pats, dv_pats, ds_accs = [], [], [], []
    for p, r in enumerate(rs):
        dq_p, dk_p, dv_p, ds_p = _att_bwd(*qkv[p], *side[p], bias, p, r, "att_bwd_r%d" % r)
        dq_pats.append(_as_pattern(dq_p, r))
        dk_pats.append(_as_pattern(dk_p, r))
        dv_pats.append(_as_pattern(dv_p, r))
        ds_accs.append(ds_p)
    d_rel_bias = _bias_bwd(ds_accs, buckets)
    dq_att, dk_att, dv_att, d_qw, d_kw = _att_pre_bwd(dq_pats, dk_pats, dv_pats, proj, qw_t, kw_t, bd64)
    du, dqd, dkt, dw, dattn, dgl = _dn_scan_bwd(do_dn, sp, qd, kt, w, vn, attn, gl)
    dqn, dkn, dv_dn, dbg = _dn_prep_bwd(qn, kn, v_dn, bg, bgt, tri, sel, t_inv, attn, u, w, du, dw, dqd, dkt, dattn,
                                        dgl)
    dyc, dba, dsm = _dn_pre_bwd(dqn, dkn, dv_dn, dbg, proj, conv_w8, alog_l, dtb_l)
    d_qkv_dn, d_conv8 = _conv_bwd(dyc, proj, conv_w8)
    pieces = (d_qkv_dn, dz, dq_att, dk_att, dv_att, dgate, dba)
    d_w_in_t = _in_bwd_dw(pieces, x, norm_w)
    last = functools.partial(_in_bwd_dx, pieces, w_bf, x, dy, norm_w)

    grads = dict(
        w_in_t=d_w_in_t,
        conv_w=d_conv8[0:CONV_W, :],
        a_log=dsm[0:1, DN_HEADS:2 * DN_HEADS],
        dt_bias=dsm[1:2, DN_HEADS:2 * DN_HEADS],
        dn_norm_w=d_dnw[0:1, :],
        q_norm_w=d_qw[0:1, :].reshape(ATT_HEADS, ATT_HD),
        k_norm_w=d_kw[0:1, :].reshape(ATT_HEADS, ATT_HD),
        rel_bias=d_rel_bias,
        w_out=d_w_out,
    )
    return loss_blk[0, 0], last, grads


MESH_ID = pl.DeviceIdType.MESH
ANY = pl.BlockSpec(memory_space=pl.ANY)


def _position():
    return lax.axis_index("x"), lax.axis_index("y"), lax.axis_index("c")


def _other_chips(x, y):
    return [(1 - x, y), (x, 1 - y), (1 - x, 1 - y)]


SHARD_PAD = 1040
WIN = 528


def _row_split(n):
    return (n // 2) // 128 * 128


def _part(ref, cc):
    n = ref.shape[0]
    sp = _row_split(n)
    return ref.at[pl.ds(0, sp)] if cc == 0 else ref.at[pl.ds(sp, n - sp)]


def _gather_weights(wt_s, w_out_s, conv_s):
    def body(a_ref, b_ref, c_ref, ga_ref, gb_ref, gc_ref, send_sems, recv_sems, loc_sems, a_vmem, b_vmem):
        x, y, c = _position()
        me = 2 * x + y
        sib = (x, y, 1 - c)
        big = ((a_ref, ga_ref), (b_ref, gb_ref))
        stage_in = [pltpu.make_async_copy(a_ref, a_vmem, loc_sems.at[0]),
                    pltpu.make_async_copy(b_ref, b_vmem, loc_sems.at[1])]
        local = [pltpu.make_async_copy(a_vmem, ga_ref.at[me], loc_sems.at[0]),
                 pltpu.make_async_copy(b_vmem, gb_ref.at[me], loc_sems.at[1]),
                 pltpu.make_async_copy(c_ref, gc_ref.at[me], loc_sems.at[2])]
        for cp in stage_in:
            cp.start()
        local[2].start()
        for cp in stage_in:
            cp.wait()
        for cp in local[:2]:
            cp.start()
        others = _other_chips(x, y)

        def exchange(cc):
            sends = []
            for j, (px, py) in enumerate(others):
                for t, (src, dst) in enumerate(big):
                    k = 3 * j + t
                    sends.append(pltpu.make_async_remote_copy(
                        src_ref=_part(src, cc), dst_ref=_part(dst.at[me], cc), send_sem=send_sems.at[k],
                        recv_sem=recv_sems.at[k], device_id=(px, py, c), device_id_type=MESH_ID))
                sends.append(pltpu.make_async_remote_copy(
                    src_ref=c_ref, dst_ref=gc_ref.at[me], send_sem=send_sems.at[3 * j + 2],
                    recv_sem=recv_sems.at[3 * j + 2], device_id=(px, py, c), device_id_type=MESH_ID))
            for cp in sends:
                cp.start()
            for j, (px, py) in enumerate(others):
                src_chip = 2 * px + py
                for t, (src, dst) in enumerate(big):
                    landed = _part(dst.at[src_chip], cc)
                    pltpu.make_async_remote_copy(
                        src_ref=_part(src, cc), dst_ref=landed, send_sem=send_sems.at[3 * j + t],
                        recv_sem=recv_sems.at[3 * j + t], device_id=(px, py, c), device_id_type=MESH_ID).wait_recv()
                    k = 9 + 2 * j + t
                    fwd = pltpu.make_async_remote_copy(
                        src_ref=landed, dst_ref=landed, send_sem=send_sems.at[k], recv_sem=recv_sems.at[k],
                        device_id=sib, device_id_type=MESH_ID)
                    fwd.start()
                    sends.append(fwd)
                pltpu.make_async_remote_copy(
                    src_ref=c_ref, dst_ref=gc_ref.at[src_chip], send_sem=send_sems.at[3 * j + 2],
                    recv_sem=recv_sems.at[3 * j + 2], device_id=(px, py, c), device_id_type=MESH_ID).wait_recv()
            for j, (px, py) in enumerate(others):
                src_chip = 2 * px + py
                for t, (src, dst) in enumerate(big):
                    k = 9 + 2 * j + t
                    theirs = _part(dst.at[src_chip], 1 - cc)
                    pltpu.make_async_remote_copy(
                        src_ref=theirs, dst_ref=theirs, send_sem=send_sems.at[k], recv_sem=recv_sems.at[k],
                        device_id=sib, device_id_type=MESH_ID).wait_recv()
            for cp in sends:
                cp.wait_send()

        for cc in (0, 1):
            pl.when(c == cc)(functools.partial(exchange, cc))
        for cp in local:
            cp.wait()

    srcs = (wt_s, w_out_s, conv_s)
    n_sem = 9 + 6
    return pl.pallas_call(
        body, in_specs=[ANY] * 3, out_specs=[ANY] * 3,
        out_shape=[_sds((N_CHIPS,) + a.shape, a.dtype) for a in srcs],
        scratch_shapes=[pltpu.SemaphoreType.DMA((n_sem,)), pltpu.SemaphoreType.DMA((n_sem,)),
                        pltpu.SemaphoreType.DMA((3,)), pltpu.VMEM(wt_s.shape, wt_s.dtype),
                        pltpu.VMEM(w_out_s.shape, w_out_s.dtype)],
        compiler_params=_cp(), name="gather_weights")(*srcs)


N_DEV = 8
PEER_FLIPS = [(dx, dy, dc) for dx in (0, 1) for dy in (0, 1) for dc in (0, 1)][1:]


def _small_copies(s_ref, rs_ref, send_sems, recv_sems):
    x, y, c = _position()
    dev = 4 * x + 2 * y + c
    sends, recvs = [], []
    for f, (dx, dy, dc) in enumerate(PEER_FLIPS):
        peer = (x ^ dx, y ^ dy, c ^ dc)
        sends.append(pltpu.make_async_remote_copy(
            src_ref=s_ref, dst_ref=rs_ref.at[dev], send_sem=send_sems.at[f], recv_sem=recv_sems.at[f],
            device_id=peer, device_id_type=MESH_ID))
        recvs.append(pltpu.make_async_remote_copy(
            src_ref=s_ref, dst_ref=rs_ref.at[4 * peer[0] + 2 * peer[1] + peer[2]], send_sem=send_sems.at[f],
            recv_sem=recv_sems.at[f], device_id=peer, device_id_type=MESH_ID))
    return sends, recvs


def _plane_sum(own, recv, name):
    rows, cols = own.shape
    tr = 128

    def body(o_ref, r_ref, out_ref):
        acc = o_ref[...]
        for j in range(3):
            acc = acc + r_ref[j].astype(F32)
        out_ref[...] = acc

    return pl.pallas_call(
        body, grid=(pl.cdiv(rows, tr),),
        in_specs=[pl.BlockSpec((tr, cols), lambda i: (i, 0)), pl.BlockSpec((3, tr, cols), lambda i: (0, i, 0))],
        out_specs=pl.BlockSpec((tr, cols), lambda i: (i, 0)), out_shape=_sds((rows, cols)),
        compiler_params=_cp(("parallel",)), name=name)(own, recv)


def _fill_parts(f_in, f_out, small, rows_in, rows_out):
    def body(a_ref, b_ref, s_ref, fa_ref, fb_ref, rs_ref, send_sems, recv_sems, loc_sems, small_send, small_recv,
             a_vmem, b_vmem):
        x, y, c = _position()
        sib = (x, y, 1 - c)
        pairs = ((a_ref, fa_ref), (b_ref, fb_ref))
        own_small = pltpu.make_async_copy(s_ref, rs_ref.at[4 * x + 2 * y + c], loc_sems.at[2])
        own_small.start()
        small_sends, small_recvs = _small_copies(s_ref, rs_ref, small_send, small_recv)
        for cp in small_sends:
            cp.start()
        stage_in = [pltpu.make_async_copy(a_ref, a_vmem, loc_sems.at[0]),
                    pltpu.make_async_copy(b_ref, b_vmem, loc_sems.at[1])]
        for cp in stage_in:
            cp.start()
        for cp in stage_in:
            cp.wait()

        def fill(cc):
            mine = [_part(dst, cc) for _, dst in pairs]
            srcs = [src.at[pl.ds(0, m.shape[0])] for src, m in zip((a_vmem, b_vmem), mine)]
            local = [pltpu.make_async_copy(s, m, loc_sems.at[t]) for t, (s, m) in enumerate(zip(srcs, mine))]
            sends = [pltpu.make_async_remote_copy(src_ref=s, dst_ref=m, send_sem=send_sems.at[t],
                                                  recv_sem=recv_sems.at[t], device_id=sib, device_id_type=MESH_ID)
                     for t, (s, m) in enumerate(zip(srcs, mine))]
            for cp in local + sends:
                cp.start()
            for t, (_, dst) in enumerate(pairs):
                theirs = _part(dst, 1 - cc)
                pltpu.make_async_remote_copy(src_ref=theirs, dst_ref=theirs, send_sem=send_sems.at[t],
                                             recv_sem=recv_sems.at[t], device_id=sib, device_id_type=MESH_ID).wait_recv()
            for cp in sends:
                cp.wait_send()
            for cp in local:
                cp.wait()

        for cc in (0, 1):
            pl.when(c == cc)(functools.partial(fill, cc))
        for cp in small_recvs:
            cp.wait_recv()
        for cp in small_sends:
            cp.wait_send()
        own_small.wait()

    n_peer = len(PEER_FLIPS)
    return pl.pallas_call(
        body, in_specs=[ANY] * 3, out_specs=[ANY] * 3,
        out_shape=[_sds((rows_in, D_MODEL)), _sds((rows_out, D_MODEL)), _sds((N_DEV,) + small.shape, small.dtype)],
        scratch_shapes=[pltpu.SemaphoreType.DMA((2,)), pltpu.SemaphoreType.DMA((2,)), pltpu.SemaphoreType.DMA((3,)),
                        pltpu.SemaphoreType.DMA((n_peer,)), pltpu.SemaphoreType.DMA((n_peer,)),
                        pltpu.VMEM(f_in.shape, F32), pltpu.VMEM(f_out.shape, F32)],
        compiler_params=_cp(), name="fill_parts")(f_in, f_out, small)


def _chip_sums(u_in, u_out, win_in, win_out):
    wins = (win_in, win_out)

    def body(a_ref, b_ref, ha_ref, hba_ref, hb_ref, hbb_ref, send_sems, recv_sems, loc_sems, out_sems,
             mine_a, theirs_a, sum_a, sumb_a, mine_b, theirs_b, sum_b, sumb_b):
        x, y, c = _position()
        sib = (x, y, 1 - c)
        groups = ((a_ref, mine_a, theirs_a, sum_a, sumb_a, ha_ref, hba_ref, win_in),
                  (b_ref, mine_b, theirs_b, sum_b, sumb_b, hb_ref, hbb_ref, win_out))
        loads, sends = [], []
        for t, (src, mine, theirs, _, _, _, _, win) in enumerate(groups):
            split = _row_split(src.shape[1])
            for k in range(N_CHIPS):
                n = N_CHIPS * t + k
                loads.append(pltpu.make_async_copy(
                    src.at[k, pl.ds(pl.multiple_of(c * split, split), win), :], mine.at[k], loc_sems.at[n]))
                sends.append(pltpu.make_async_remote_copy(
                    src_ref=src.at[k, pl.ds(pl.multiple_of((1 - c) * split, split), win), :], dst_ref=theirs.at[k],
                    send_sem=send_sems.at[n], recv_sem=recv_sems.at[n], device_id=sib, device_id_type=MESH_ID))
        for cp in sends + loads:
            cp.start()
        stores = []
        for t, (_, mine, theirs, tot, totb, h_out, hb_out, _) in enumerate(groups):
            for k in range(N_CHIPS):
                n = N_CHIPS * t + k
                loads[n].wait()
                sends[n].wait_recv()
                val = mine[k] + theirs[k]
                tot[k] = val
                totb[k] = val.astype(BF16)
                stores += [pltpu.make_async_copy(tot.at[k], h_out.at[k], out_sems.at[2 * n]),
                           pltpu.make_async_copy(totb.at[k], hb_out.at[k], out_sems.at[2 * n + 1])]
                stores[-2].start()
                stores[-1].start()
        for cp in sends:
            cp.wait_send()
        for cp in stores:
            cp.wait()

    shapes = [(N_CHIPS, w, D_MODEL) for w in wins]
    n_cp = 2 * N_CHIPS
    vmem = []
    for shp in shapes:
        vmem += [pltpu.VMEM(shp, F32), pltpu.VMEM(shp, F32), pltpu.VMEM(shp, F32), pltpu.VMEM(shp, BF16)]
    return pl.pallas_call(
        body, in_specs=[ANY] * 2, out_specs=[ANY] * 4,
        out_shape=[_sds(shapes[0]), _sds(shapes[0], BF16), _sds(shapes[1]), _sds(shapes[1], BF16)],
        scratch_shapes=[pltpu.SemaphoreType.DMA((n_cp,)), pltpu.SemaphoreType.DMA((n_cp,)),
                        pltpu.SemaphoreType.DMA((n_cp,)), pltpu.SemaphoreType.DMA((2 * n_cp,))] + vmem,
        compiler_params=_cp(), name="chip_sums")(u_in, u_out)


SMALL_LAYOUT = (("norm_w", 1024), ("conv_w", 6144), ("a_log", 128), ("dt_bias", 128), ("dn_norm_w", 128),
                ("q_norm_w", 512), ("k_norm_w", 512), ("rel_bias", 256), ("loss", 128))
SMALL_TOTAL = sum(n for _, n in SMALL_LAYOUT)


def _small_offset(name):
    off = 0
    for n, size in SMALL_LAYOUT:
        if n == name:
            return off
        off += size
    raise KeyError(name)


def _pack_small(grads):
    parts = []
    for name, size in SMALL_LAYOUT:
        flat = grads[name].reshape(1, -1)
        parts.append(jnp.pad(flat, ((0, 0), (0, size - flat.shape[1]))))
    return jnp.concatenate(parts, axis=1)


def _sum_small(rows):
    n_dev = rows.shape[0]
    q_off = _small_offset("q_norm_w")
    k_off = _small_offset("k_norm_w")

    def body(r_ref, tot_ref, qk_ref):
        tot = r_ref[0:1, :]
        for d in range(1, n_dev):
            tot = tot + r_ref[d:d + 1, :]
        tot_ref[...] = tot
        for row, off in ((0, q_off), (1, k_off)):
            s4 = tot[:, off:off + 128] + tot[:, off + 128:off + 256] + tot[:, off + 256:off + 384] \
                + tot[:, off + 384:off + 512]
            qk_ref[row:row + 1, :] = s4 + pltpu.roll(s4, ATT_HD, 1)

    return pl.pallas_call(
        body, in_specs=[pl.BlockSpec(memory_space=pltpu.VMEM)],
        out_specs=[pl.BlockSpec(memory_space=pltpu.VMEM)] * 2,
        out_shape=[_sds((1, SMALL_TOTAL)), _sds((2, 128))],
        compiler_params=_cp(), name="sum_small")(rows)


def _adamw_math(w, g, m, v):
    m = ADAM_B1 * m + (1.0 - ADAM_B1) * g
    v = ADAM_B2 * v + (1.0 - ADAM_B2) * (g * g)
    m_hat = m / (1.0 - ADAM_B1 ** ADAM_STEP)
    v_hat = v / (1.0 - ADAM_B2 ** ADAM_STEP)
    delta = -ADAM_LR * (m_hat / (jnp.sqrt(v_hat) + ADAM_EPS) + ADAM_WD * w)
    return delta, m, v


def _adamw_big(g, w, m, v, name):
    rows, cols = w.shape
    tr = 128

    def body(g_ref, w_ref, m_ref, v_ref, go_ref, d_ref, nm_ref, nv_ref):
        g = g_ref[...]
        go_ref[...] = g
        d_ref[...], nm_ref[...], nv_ref[...] = _adamw_math(w_ref[...], g, m_ref[...], v_ref[...])

    blk = pl.BlockSpec((tr, cols), lambda i: (i, 0))
    return pl.pallas_call(
        body, grid=(pl.cdiv(rows, tr),), in_specs=[blk] * 4, out_specs=[blk] * 4,
        out_shape=[_sds((rows, cols))] * 4, compiler_params=_cp(("parallel",)), name=name)(g, w, m, v)


def _adamw_rows(g, w, m, v, name):
    rows, cols = w.shape
    tr = 128

    def body(g_ref, w_ref, m_ref, v_ref, go_ref, d_ref, nm_ref, nv_ref, s_g, s_d, s_m, s_v):
        g = g_ref[...]
        s_g[...] = g
        s_d[...], s_m[...], s_v[...] = _adamw_math(w_ref[...], g, m_ref[...], v_ref[...])
        for i in range(tr):
            for scr, out in ((s_g, go_ref), (s_d, d_ref), (s_m, nm_ref), (s_v, nv_ref)):
                out[i] = scr[i:i + 1, :]

    blk = pl.BlockSpec((tr, cols), lambda i: (i, 0))
    oblk = pl.BlockSpec((tr, 1, cols), lambda i: (i, 0, 0))
    return pl.pallas_call(
        body, grid=(pl.cdiv(rows, tr),), in_specs=[blk] * 4, out_specs=[oblk] * 4,
        out_shape=[_sds((rows, 1, cols))] * 4, scratch_shapes=[pltpu.VMEM((tr, cols), F32)] * 4,
        compiler_params=_cp(("parallel",)), name=name)(g, w, m, v)


def _adamw_small(w, g, m, v, name):
    def body(w_ref, g_ref, m_ref, v_ref, d_ref, nm_ref, nv_ref):
        d_ref[...], nm_ref[...], nv_ref[...] = _adamw_math(w_ref[...], g_ref[...], m_ref[...], v_ref[...])

    vm = pl.BlockSpec(memory_space=pltpu.VMEM)
    return pl.pallas_call(body, in_specs=[vm] * 4, out_specs=[vm] * 3, out_shape=[_sds(w.shape)] * 3,
                          compiler_params=_cp(), name=name)(w, g, m, v)


WEIGHTS = ("norm_w", "w_in", "conv_w", "a_log", "dt_bias", "dn_norm_w", "q_norm_w", "k_norm_w", "rel_bias", "w_out")


def kernel(x, norm_w, w_in, conv_w, a_log, dt_bias, dn_norm_w, q_norm_w, k_norm_w, rel_bias, w_out, loss_target, m_norm_w, m_w_in, m_conv_w, m_a_log, m_dt_bias, m_dn_norm_w, m_q_norm_w, m_k_norm_w, m_rel_bias, m_w_out, v_norm_w, v_w_in, v_conv_w, v_a_log, v_dt_bias, v_dn_norm_w, v_q_norm_w, v_k_norm_w, v_rel_bias, v_w_out):
    xi, yi, ci = _position()
    chip = 2 * xi + yi
    w_loc = dict(norm_w=norm_w, w_in=w_in[0].T, conv_w=conv_w[0], a_log=a_log, dt_bias=dt_bias, dn_norm_w=dn_norm_w,
                 q_norm_w=q_norm_w, k_norm_w=k_norm_w, rel_bias=rel_bias, w_out=w_out[0])
    m_loc = dict(norm_w=m_norm_w, w_in=m_w_in[0].T, conv_w=m_conv_w[0], a_log=m_a_log, dt_bias=m_dt_bias,
                 dn_norm_w=m_dn_norm_w, q_norm_w=m_q_norm_w, k_norm_w=m_k_norm_w, rel_bias=m_rel_bias,
                 w_out=m_w_out[0])
    v_loc = dict(norm_w=v_norm_w, w_in=v_w_in[0].T, conv_w=v_conv_w[0], a_log=v_a_log, dt_bias=v_dt_bias,
                 dn_norm_w=v_dn_norm_w, q_norm_w=v_q_norm_w, k_norm_w=v_k_norm_w, rel_bias=v_rel_bias,
                 w_out=v_w_out[0])

    wt_pad = jnp.pad(w_loc["w_in"].astype(BF16), ((0, SHARD_PAD - D_SHARD), (0, 0)))
    g_in, g_out, g_conv = _gather_weights(wt_pad, w_loc["w_out"].astype(BF16), w_loc["conv_w"])
    wt_full = _pack_w(g_in)
    w_out_full = g_out.reshape(D_MODEL, D_MODEL)
    conv_full = g_conv.transpose(1, 0, 2).reshape(CONV_W, 3 * D_DN)

    loss_local, last_kernel, grads = _local_step(x[0], loss_target[0], norm_w, wt_full, conv_full, a_log, dt_bias,
                                                 dn_norm_w, q_norm_w, k_norm_w, rel_bias, w_out_full)
    grads["loss"] = loss_local

    u_in = _unpack_w(grads["w_in_t"], SHARD_PAD)
    u_out = grads["w_out"].reshape(N_CHIPS, D_MODEL // N_CHIPS, D_MODEL)
    win_out = u_out.shape[1] // 2
    h_in, hb_in, h_out, hb_out = _chip_sums(u_in, u_out, WIN, win_out)
    grad_x, d_nw8, r_in, r_out = last_kernel(chip_sums=(hb_in, hb_out))
    grads["norm_w"] = d_nw8[0:1, :]
    small = _pack_small(grads)
    f_in = _plane_sum(lax.dynamic_index_in_dim(h_in, chip, 0, keepdims=False), r_in, "shard_sum_w_in")
    f_out = _plane_sum(lax.dynamic_index_in_dim(h_out, chip, 0, keepdims=False), r_out, "shard_sum_w_out")
    full_in, full_out, r_small = _fill_parts(f_in, f_out, small, SHARD_PAD, u_out.shape[1])
    tot_small, qk = _sum_small(r_small.reshape(8, SMALL_TOTAL))

    def small_grad(name, n):
        off = _small_offset(name)
        return tot_small[:, off:off + n]

    loss = small_grad("loss", 1).reshape(())
    conv_all = small_grad("conv_w", CONV_W * 3 * D_DN).reshape(CONV_W, 3 * D_DN)
    g_small = dict(
        norm_w=small_grad("norm_w", D_MODEL),
        conv_w=lax.dynamic_slice_in_dim(conv_all, chip * (3 * D_DN // N_CHIPS), 3 * D_DN // N_CHIPS, axis=1),
        a_log=small_grad("a_log", DN_HEADS),
        dt_bias=small_grad("dt_bias", DN_HEADS),
        dn_norm_w=small_grad("dn_norm_w", DN_HD),
        q_norm_w=qk[0:1, 0:ATT_HD],
        k_norm_w=qk[1:2, 0:ATT_HD],
        rel_bias=small_grad("rel_bias", ATT_HEADS * N_BUCKETS).reshape(ATT_HEADS, N_BUCKETS),
    )

    out_g, out_d, out_m, out_v = {}, {}, {}, {}
    out_g["w_in"], out_d["w_in"], out_m["w_in"], out_v["w_in"] = _adamw_rows(
        full_in, w_loc["w_in"], m_loc["w_in"], v_loc["w_in"], "adamw_w_in")
    out_g["w_out"], out_d["w_out"], out_m["w_out"], out_v["w_out"] = _adamw_big(
        full_out, w_loc["w_out"], m_loc["w_out"], v_loc["w_out"], "adamw_w_out")
    for name in g_small:
        out_g[name] = g_small[name]
        out_d[name], out_m[name], out_v[name] = _adamw_small(w_loc[name], g_small[name], m_loc[name], v_loc[name],
                                                             "adamw_" + name)
    for d in (out_g, out_d, out_m, out_v):
        d["w_in"] = d["w_in"].transpose(1, 2, 0)
        for name in ("conv_w", "w_out"):
            d[name] = d[name][None]
    return (loss, grad_x[None], *[out_g[n] for n in WEIGHTS], *[out_d[n] for n in WEIGHTS],
            *[out_m[n] for n in WEIGHTS], *[out_v[n] for n in WEIGHTS])
```

```python
import functools
import math

import numpy as np
import jax
import jax.numpy as jnp
from jax import lax
from jax.experimental import pallas as pl
from jax.experimental.pallas import tpu as pltpu

F32 = jnp.float32
BF16 = jnp.bfloat16
HI = lax.Precision.HIGHEST

D_MODEL = 1024
D_DN = 512
DN_HEADS = 4
DN_HD = 128
CONV_W = 4
CHUNK = 64
D_ATT = 512
ATT_HEADS = 8
ATT_HD = 64
PATTERNS = ((128, 1), (512, 4), (2048, 16))
N_BUCKETS = 32
MAX_DISTANCE = 2048
D_IN = 4 * D_DN + 2 * DN_HEADS + 4 * D_ATT
D_IN_PAD = 4224
EPS = 1e-6
BLK = 128
NEG = -1e30
N_CHIPS = 4

ADAM_LR = 0.001
ADAM_B1 = 0.9
ADAM_B2 = 0.999
ADAM_EPS = 1e-08
ADAM_WD = 0.01
ADAM_STEP = 10

VMEM_LIMIT = 56 * 1024 * 1024

COL_Z = 3
COL_ATT_Q = 4
COL_ATT_K = 5
COL_ATT_V = 6
COL_GATE = 7
COL_BA_128 = 32


def _cp(sem=None):
    if sem is None:
        return pltpu.CompilerParams(vmem_limit_bytes=VMEM_LIMIT)
    return pltpu.CompilerParams(dimension_semantics=sem, vmem_limit_bytes=VMEM_LIMIT)


def _sds(shape, dtype=F32):
    return jax.ShapeDtypeStruct(shape, dtype)


def _mm(a, b):
    return jnp.dot(a.astype(BF16), b.astype(BF16), preferred_element_type=F32)


def _mm_nt(a, b):
    return lax.dot_general(a.astype(BF16), b.astype(BF16), (((1,), (1,)), ((), ())),
                           preferred_element_type=F32)


def _mm_tn(a, b):
    return lax.dot_general(a.astype(BF16), b.astype(BF16), (((0,), (0,)), ((), ())),
                           preferred_element_type=F32)


def _mmx(a, b):
    return jnp.dot(a, b, precision=HI, preferred_element_type=F32)


def _mmx_nt(a, b):
    return lax.dot_general(a, b, (((1,), (1,)), ((), ())), precision=HI, preferred_element_type=F32)


def _mmx_tn(a, b):
    return lax.dot_general(a, b, (((0,), (0,)), ((), ())), precision=HI, preferred_element_type=F32)


def _dot(a, b):
    return jnp.dot(a, b, preferred_element_type=F32)


def _dot_nt(a, b):
    return lax.dot_general(a, b, (((1,), (1,)), ((), ())), preferred_element_type=F32)


def _dot_tn(a, b):
    return lax.dot_general(a, b, (((0,), (0,)), ((), ())), preferred_element_type=F32)


def _split(a):
    hi = a.astype(BF16)
    return hi, (a - hi.astype(F32)).astype(BF16)


def _mm3(a_s, b_s):
    return _dot(a_s[0], b_s[0]) + _dot(a_s[0], b_s[1]) + _dot(a_s[1], b_s[0])


def _mm3_tn(a_s, b_s):
    return _dot_tn(a_s[0], b_s[0]) + _dot_tn(a_s[0], b_s[1]) + _dot_tn(a_s[1], b_s[0])


def _interleave(gens):
    live = list(gens)
    while live:
        nxt = []
        for g in live:
            try:
                next(g)
                nxt.append(g)
            except StopIteration:
                pass
        live = nxt


def _segsum(x, bd):
    hi = x.astype(BF16)
    r1 = x - hi.astype(F32)
    mid = r1.astype(BF16)
    lo = (r1 - mid.astype(F32)).astype(BF16)
    return (jnp.dot(hi, bd, preferred_element_type=F32) + jnp.dot(mid, bd, preferred_element_type=F32)
            + jnp.dot(lo, bd, preferred_element_type=F32))


def _compact_mat(seg, n=512):
    slot = 128 * seg // n
    src = np.arange(n)[:, None]
    dst = np.arange(128)[None, :]
    return jnp.asarray((src // seg == dst // slot).astype(np.float32), dtype=BF16)


def _seg_mean(x, cm, seg):
    hi, lo = _split(x)
    return (_dot(hi, cm) + _dot(lo, cm)) * (1.0 / seg)


def _seg_expand(c, cm, seg):
    hi, lo = _split(c)
    return (_dot_nt(hi, cm) + _dot_nt(lo, cm)) * (cm.shape[0] / (128.0 * seg))


def _seg_rstd(x, cm, seg):
    return _seg_expand(lax.rsqrt(_seg_mean(x * x, cm, seg) + EPS), cm, seg)


def _sigmoid(x):
    return 1.0 / (1.0 + jnp.exp(-x))


def _silu_grad(x, s):
    return s * (1.0 + x * (1.0 - s))


def _tri_incl():
    i = np.arange(CHUNK)
    return jnp.asarray((i[:, None] >= i[None, :]).astype(np.float32))


def _t5_bucket(dist):
    max_exact = N_BUCKETS // 2
    d = np.maximum(dist, 1).astype(np.float64)
    large = max_exact + (np.log(d / max_exact) / math.log(MAX_DISTANCE / max_exact)
                         * (N_BUCKETS - max_exact)).astype(np.int32)
    large = np.minimum(large, N_BUCKETS - 1)
    return np.where(dist < max_exact, dist, large).astype(np.int32)


def _bucket_tables():
    qi = np.arange(BLK)[:, None]
    kj = np.arange(2 * BLK)[None, :]
    step = qi - kj + BLK
    return jnp.asarray(np.stack([_t5_bucket(np.clip(step, 0, None) * r) for _, r in PATTERNS]))


def _in_proj(x, norm_w, wt_bf):
    s = x.shape[0]
    tm = 512

    def body(x_ref, nw_ref, w_ref, o_ref):
        xv = x_ref[...]
        rstd = lax.rsqrt(jnp.mean(xv * xv, axis=-1, keepdims=True) + EPS)
        h = (xv * rstd * nw_ref[...]).astype(BF16)
        o_ref[...] = _dot_nt(h, w_ref[...])

    return pl.pallas_call(
        body, grid=(s // tm,),
        in_specs=[pl.BlockSpec((tm, D_MODEL), lambda i: (i, 0)),
                  pl.BlockSpec((1, D_MODEL), lambda i: (0, 0)),
                  pl.BlockSpec((D_IN_PAD, D_MODEL), lambda i: (0, 0))],
        out_specs=pl.BlockSpec((tm, D_IN_PAD), lambda i: (i, 0)),
        out_shape=_sds((s, D_IN_PAD)), compiler_params=_cp(("parallel",)), name="in_proj")(x, norm_w, wt_bf)


def _conv_group(cur, halo, w_ref, c):
    rows = cur.shape[0]
    lanes = slice(128 * c, 128 * c + 128)
    xcat = jnp.concatenate([halo, cur], axis=0)
    y = cur * w_ref[CONV_W - 1:CONV_W, lanes]
    for k in range(1, CONV_W):
        sh = pltpu.roll(xcat, k, 0)[8:8 + rows]
        y = y + sh * w_ref[CONV_W - 1 - k:CONV_W - k, lanes]
    return y


def _beta_g(ba, alog_l, dtb_l):
    lane = lax.broadcasted_iota(jnp.int32, ba.shape, 1)
    sig_b = _sigmoid(ba)
    t = ba + dtb_l
    softplus = jnp.maximum(t, 0.0) + jnp.log(1.0 + jnp.exp(-jnp.abs(t)))
    nega = -jnp.exp(alog_l)
    g = nega * softplus
    out = jnp.where(lane < DN_HEADS, sig_b, jnp.where(lane < 2 * DN_HEADS, g, 0.0))
    return out, lane, sig_b, t, nega, g


def _dn_pre(proj, conv_w8, alog_l, dtb_l):
    s = proj.shape[0]
    tr = 512
    nh = tr // 8

    def body(u_ref, halo_ref, ba_ref, w_ref, al_ref, dt_ref, q_ref, k_ref, v_ref, bg_ref):
        i = pl.program_id(0)
        keep = (i > 0).astype(F32)
        for c in range(12):
            lanes = slice(128 * c, 128 * c + 128)
            y = _conv_group(u_ref[:, lanes], halo_ref[:, lanes] * keep, w_ref, c)
            sv = y * _sigmoid(y)
            if c < 8:
                rs = lax.rsqrt(jnp.sum(sv * sv, axis=1, keepdims=True) + EPS)
                n = sv * rs
                if c < 4:
                    q_ref[:, lanes] = n * (DN_HD ** -0.5)
                else:
                    k_ref[:, slice(128 * (c - 4), 128 * (c - 3))] = n
            else:
                v_ref[:, slice(128 * (c - 8), 128 * (c - 7))] = sv
        bg_ref[...] = _beta_g(ba_ref[...], al_ref[...], dt_ref[...])[0]

    return pl.pallas_call(
        body, grid=(s // tr,),
        in_specs=[pl.BlockSpec((tr, 1536), lambda i: (i, 0)),
                  pl.BlockSpec((8, 1536), lambda i: (jnp.maximum(i * nh - 1, 0), 0)),
                  pl.BlockSpec((tr, 128), lambda i: (i, COL_BA_128)),
                  pl.BlockSpec((8, 1536), lambda i: (0, 0)),
                  pl.BlockSpec((1, 128), lambda i: (0, 0)),
                  pl.BlockSpec((1, 128), lambda i: (0, 0))],
        out_specs=[pl.BlockSpec((tr, 512), lambda i: (i, 0))] * 3 + [pl.BlockSpec((tr, 128), lambda i: (i, 0))],
        out_shape=[_sds((s, 512))] * 3 + [_sds((s, 128))],
        compiler_params=_cp(("parallel",)), name="dn_pre")(proj, proj, proj, conv_w8, alog_l, dtb_l)


CPS = 4
CPS_SCAN = 8


def _split3(a):
    hi = a.astype(BF16)
    r1 = a - hi.astype(F32)
    mid = r1.astype(BF16)
    return hi, mid, (r1 - mid.astype(F32)).astype(BF16)


def _lane_select():
    r = np.arange(128)
    return jnp.asarray((r[None, :, None] == np.arange(8)[:, None, None]) * np.ones((1, 1, 128)), dtype=BF16)


def _lane_bcast(a3, sel):
    return _dot(a3[0], sel) + _dot(a3[1], sel) + _dot(a3[2], sel)


def _rowsum_b(z, ones_b):
    hi, lo = _split(z)
    return _dot(hi, ones_b) + _dot(lo, ones_b)


def _chunk_cumsum(bg, bgt, tri):
    return _split3(bg), _split3(_mmx(tri, bg)), _mmx_nt(bgt, tri)


def _chunk_common(bg3, gc3, gc_row, h, sel_ref):
    gcc = _lane_bcast(gc3, sel_ref[DN_HEADS + h])
    beta = _lane_bcast(bg3, sel_ref[h])
    gcr = gc_row[DN_HEADS + h:DN_HEADS + h + 1, :]
    ii = lax.broadcasted_iota(jnp.int32, (CHUNK, CHUNK), 0)
    jj = lax.broadcasted_iota(jnp.int32, (CHUNK, CHUNK), 1)
    incl = ii >= jj
    strict = ii > jj
    decay = jnp.exp(jnp.where(incl, gcc[:, 0:CHUNK] - gcr, NEG))
    gl = gcc[CHUNK - 1:CHUNK, :]
    return gcc, beta, incl, strict, decay, gl


def _dn_prep(qn, kn, v, bg, bgt, tri, sel):
    s = qn.shape[0]
    nc = s // CHUNK

    def body(q_ref, k_ref, v_ref, bg_ref, bgt_ref, tri_ref, sel_ref,
             u_ref, w_ref, qd_ref, kt_ref, attn_ref, t_ref, gl_ref):
        tri_v = tri_ref[...]
        ii = lax.broadcasted_iota(jnp.int32, (CHUNK, CHUNK), 0)
        jj = lax.broadcasted_iota(jnp.int32, (CHUNK, CHUNK), 1)
        eye = (ii == jj).astype(F32)

        def chain(cc, h, bg3, gc3, gc_row):
            rows = slice(CHUNK * cc, CHUNK * cc + CHUNK)
            lanes = slice(128 * h, 128 * h + 128)
            gcc, beta, incl, strict, decay, gl = _chunk_common(bg3, gc3, gc_row, h, sel_ref)
            yield
            q = q_ref[rows, lanes]
            k = k_ref[rows, lanes]
            vv = v_ref[rows, lanes]
            kb = k * beta
            egc = jnp.exp(gcc)
            a_mat = jnp.where(strict, _mm_nt(kb, k) * decay, 0.0)
            attn_ref[cc, h] = jnp.where(incl, _mm_nt(q, k) * decay, 0.0)
            qd_ref[rows, lanes] = q * egc
            kt_ref[rows, lanes] = k * jnp.exp(gl - gcc)
            gl_ref[cc, h] = jnp.exp(gl)
            yield
            p = -a_mat
            t = eye + p
            for _ in range(5):
                ps = _split(p)
                p = _mm3(ps, ps)
                yield
                t = t + _mm3(_split(t), _split(p))
                yield
            t_ref[cc, h] = t
            ts = _split(t)
            u_ref[rows, lanes] = _mm3(ts, _split(vv * beta))
            w_ref[rows, lanes] = _mm3(ts, _split(kb * egc))

        gens = []
        for cc in range(CPS):
            bgv = bg_ref[CHUNK * cc:CHUNK * cc + CHUNK, :]
            bg3, gc3, gc_row = _chunk_cumsum(bgv, bgt_ref[cc], tri_v)
            gens += [chain(cc, h, bg3, gc3, gc_row) for h in range(DN_HEADS)]
        _interleave(gens)

    rows_step = CPS * CHUNK
    big = pl.BlockSpec((rows_step, 512), lambda n: (n, 0))
    sq = pl.BlockSpec((CPS, DN_HEADS, CHUNK, CHUNK), lambda n: (n, 0, 0, 0))
    return pl.pallas_call(
        body, grid=(nc // CPS,),
        in_specs=[big, big, big, pl.BlockSpec((rows_step, 128), lambda n: (n, 0)),
                  pl.BlockSpec((CPS, 8, CHUNK), lambda n: (n, 0, 0)),
                  pl.BlockSpec((CHUNK, CHUNK), lambda n: (0, 0)),
                  pl.BlockSpec((8, 128, 128), lambda n: (0, 0, 0))],
        out_specs=[big, big, big, big, sq, sq, pl.BlockSpec((CPS, DN_HEADS, 1, 128), lambda n: (n, 0, 0, 0))],
        out_shape=[_sds((s, 512))] * 4 + [_sds((nc, DN_HEADS, CHUNK, CHUNK))] * 2 + [_sds((nc, DN_HEADS, 1, 128))],
        compiler_params=_cp(("parallel",)), name="dn_prep")(qn, kn, v, bg, bgt, tri, sel)


def _dn_scan(u, w, qd, kt, attn, gl):
    s = u.shape[0]
    nc = s // CHUNK

    def body(u_ref, w_ref, qd_ref, kt_ref, attn_ref, gl_ref, o_ref, vn_ref, sp_ref, st_ref):
        n = pl.program_id(0)

        @pl.when(n == 0)
        def _():
            st_ref[...] = jnp.zeros_like(st_ref)

        def chain(cc, h):
            rows = slice(CHUNK * cc, CHUNK * cc + CHUNK)
            lanes = slice(128 * h, 128 * h + 128)
            st = st_ref[h]
            sp_ref[cc, h] = st
            stb = st.astype(BF16)
            ws = _dot(w_ref[rows, lanes].astype(BF16), stb)
            qs = _dot(qd_ref[rows, lanes].astype(BF16), stb)
            yield
            vn = u_ref[rows, lanes] - ws
            vn_ref[rows, lanes] = vn
            vnb = vn.astype(BF16)
            o_ref[rows, lanes] = qs + _dot(attn_ref[cc, h].astype(BF16), vnb)
            st_ref[h] = st * gl_ref[cc, h] + _dot_tn(kt_ref[rows, lanes].astype(BF16), vnb)

        for cc in range(CPS_SCAN):
            _interleave([chain(cc, h) for h in range(DN_HEADS)])

    big = pl.BlockSpec((CPS_SCAN * CHUNK, 512), lambda n: (n, 0))
    return pl.pallas_call(
        body, grid=(nc // CPS_SCAN,),
        in_specs=[big, big, big, big,
                  pl.BlockSpec((CPS_SCAN, DN_HEADS, CHUNK, CHUNK), lambda n: (n, 0, 0, 0)),
                  pl.BlockSpec((CPS_SCAN, DN_HEADS, 1, 128), lambda n: (n, 0, 0, 0))],
        out_specs=[big, big, pl.BlockSpec((CPS_SCAN, DN_HEADS, DN_HD, DN_HD), lambda n: (n, 0, 0, 0))],
        out_shape=[_sds((s, 512)), _sds((s, 512)), _sds((nc, DN_HEADS, DN_HD, DN_HD))],
        scratch_shapes=[pltpu.VMEM((DN_HEADS, DN_HD, DN_HD), F32)],
        compiler_params=_cp(("arbitrary",)), name="dn_scan")(u, w, qd, kt, attn, gl)


R4 = PATTERNS[1][1]
R16 = PATTERNS[2][1]
TM = 512


def _pattern_spec(r, width=512):
    return pl.BlockSpec((r, TM // r, width), lambda i: (0, i, 0))


def _pattern_shape(s, r, dtype=F32, width=512):
    return _sds((r, s // r, width), dtype)


SLABS = pltpu.VMEM((4, TM, 128), F32)

HEAD_SLOT = 128 // ATT_HEADS


def _head_expand():
    src = np.arange(128)[:, None]
    dst = np.arange(512)[None, :]
    return jnp.asarray((src == (dst // ATT_HD) * HEAD_SLOT).astype(np.float32), dtype=BF16)


def _head_compact():
    src = np.arange(512)[:, None]
    dst = np.arange(128)[None, :]
    return jnp.asarray((src // ATT_HD == dst // HEAD_SLOT).astype(np.float32), dtype=BF16)


def _to_patterns(val, dsts, scr):
    for c in range(val.shape[1] // 128):
        lanes = slice(128 * c, 128 * c + 128)
        scr[c] = val[:, lanes]
        for dst_ref, r in dsts:
            for a in range(r):
                dst_ref[a, :, lanes] = scr[c, pl.ds(a, TM // r, stride=r), :].astype(dst_ref.dtype)


def _from_pattern(src_ref, r, scr):
    n_slab = src_ref.shape[2] // 128
    for c in range(n_slab):
        for a in range(r):
            scr[c, pl.ds(a, TM // r, stride=r), :] = src_ref[a, :, 128 * c:128 * c + 128]
    return jnp.concatenate([scr[c] for c in range(n_slab)], axis=1) if n_slab > 1 else scr[0]


def _att_pre(proj, qw_t, kw_t, bd64):
    s = proj.shape[0]

    def body(q_ref, k_ref, v_ref, qw_ref, kw_ref, bd_ref,
             q1_ref, k1_ref, v1_ref, q4_ref, k4_ref, v4_ref, q16_ref, k16_ref, v16_ref, scr):
        bd = bd_ref[...]
        q = q_ref[...]
        k = k_ref[...]
        qn = q * _seg_rstd(q, bd, ATT_HD) * qw_ref[...] * (ATT_HD ** -0.5)
        kn = k * _seg_rstd(k, bd, ATT_HD) * kw_ref[...]
        q1_ref[...] = qn.astype(BF16)
        k1_ref[...] = kn.astype(BF16)
        v1_ref[...] = v_ref[...].astype(BF16)
        _to_patterns(qn, ((q4_ref, R4), (q16_ref, R16)), scr)
        _to_patterns(kn, ((k4_ref, R4), (k16_ref, R16)), scr)
        _to_patterns(v_ref[...], ((v4_ref, R4), (v16_ref, R16)), scr)

    row = pl.BlockSpec((1, 512), lambda i: (0, 0))
    tok = pl.BlockSpec((TM, 512), lambda i: (i, 0))
    return pl.pallas_call(
        body, grid=(s // TM,),
        in_specs=[pl.BlockSpec((TM, 512), lambda i: (i, COL_ATT_Q)),
                  pl.BlockSpec((TM, 512), lambda i: (i, COL_ATT_K)),
                  pl.BlockSpec((TM, 512), lambda i: (i, COL_ATT_V)),
                  row, row, pl.BlockSpec((512, 128), lambda i: (0, 0))],
        out_specs=[tok] * 3 + [_pattern_spec(R4)] * 3 + [_pattern_spec(R16)] * 3,
        out_shape=[_sds((s, 512), BF16)] * 3 + [_pattern_shape(s, R4, BF16)] * 3 + [_pattern_shape(s, R16, BF16)] * 3,
        scratch_shapes=[SLABS],
        compiler_params=_cp(("parallel",)), name="att_pre")(proj, proj, proj, qw_t, kw_t, bd64)


def _bias_fwd(rel_bias, buckets):
    def body(rb_ref, bk_ref, o_ref):
        for p in range(len(PATTERNS)):
            bk = bk_ref[p]
            for h in range(ATT_HEADS):
                acc = jnp.zeros((BLK, 2 * BLK), F32)
                for b in range(N_BUCKETS):
                    acc = jnp.where(bk == b, rb_ref[h, b], acc)
                o_ref[p, h] = acc

    return pl.pallas_call(
        body,
        in_specs=[pl.BlockSpec(memory_space=pltpu.SMEM), pl.BlockSpec(memory_space=pltpu.VMEM)],
        out_specs=pl.BlockSpec(memory_space=pltpu.VMEM),
        out_shape=_sds((len(PATTERNS), ATT_HEADS, BLK, 2 * BLK)),
        compiler_params=_cp(), name="bias_fwd")(rel_bias, buckets)


def _bias_bwd(ds_accs, buckets):
    def body(ds0_ref, ds1_ref, ds2_ref, bk_ref, o_ref):
        for h in range(ATT_HEADS):
            for b in range(N_BUCKETS):
                tot = jnp.zeros((), F32)
                for p, ds_ref in enumerate((ds0_ref, ds1_ref, ds2_ref)):
                    tot = tot + jnp.sum(jnp.where(bk_ref[p] == b, ds_ref[h], 0.0))
                o_ref[h, b] = tot

    return pl.pallas_call(
        body,
        in_specs=[pl.BlockSpec(memory_space=pltpu.VMEM)] * 4,
        out_specs=pl.BlockSpec(memory_space=pltpu.SMEM),
        out_shape=_sds((ATT_HEADS, N_BUCKETS)),
        compiler_params=_cp(), name="bias_bwd")(*ds_accs, buckets)


QB_FWD = 2
QB_BWD = 4


def _att_masks(has_prev):
    qi = lax.broadcasted_iota(jnp.int32, (BLK, BLK), 0)
    kj = lax.broadcasted_iota(jnp.int32, (BLK, BLK), 1)
    lane = lax.broadcasted_iota(jnp.int32, (BLK, 2 * ATT_HD), 1)
    return jnp.logical_and(kj >= qi, has_prev), kj <= qi, lane < ATT_HD


def _head_lanes(h):
    half = h % 2
    return slice(ATT_HD * h, ATT_HD * h + ATT_HD), slice(ATT_HD * half, ATT_HD * half + ATT_HD)


def _att_scores(qm, kp2, kc2, bias_h, mask_prev, mask_cur):
    s_prev = jnp.where(mask_prev, _dot_nt(qm, kp2) + bias_h[:, :BLK], NEG)
    s_cur = jnp.where(mask_cur, _dot_nt(qm, kc2) + bias_h[:, BLK:], NEG)
    return s_prev, s_cur


def _att_fwd(q, k, v, v_col, bias, p_idx, r, name):
    QB = QB_FWD
    s = q.shape[0]
    nblk = s // BLK
    nseq = nblk // r

    def body(q_ref, kp_ref, kc_ref, vp_ref, vc_ref, b_ref, o_ref, lse_ref):
        j = pl.program_id(0)

        def head(h, rows, masks, q2, kp2, kc2, vp2, vc2):
            mask_prev, mask_cur, lo_half = masks
            out_l, pair_l = _head_lanes(h)
            sel = lo_half if h % 2 == 0 else jnp.logical_not(lo_half)
            qm = jnp.where(sel, q2, jnp.zeros_like(q2))
            s_prev, s_cur = _att_scores(qm, kp2, kc2, b_ref[0, h], mask_prev, mask_cur)
            yield
            m = jnp.maximum(jnp.max(s_prev, axis=1, keepdims=True), jnp.max(s_cur, axis=1, keepdims=True))
            p_prev = jnp.exp(s_prev - m)
            p_cur = jnp.exp(s_cur - m)
            l = jnp.sum(p_prev, axis=1, keepdims=True) + jnp.sum(p_cur, axis=1, keepdims=True)
            yield
            o2 = _dot(p_prev.astype(BF16), vp2) + _dot(p_cur.astype(BF16), vc2)
            o_ref[rows, out_l] = (o2 * (1.0 / l))[:, pair_l]
            lse_ref[rows, HEAD_SLOT * h:HEAD_SLOT * h + HEAD_SLOT] = jnp.broadcast_to(m + jnp.log(l), (BLK, HEAD_SLOT))

        for sub in range(QB):
            rows = slice(BLK * sub, BLK * sub + BLK)
            before = slice(BLK * (sub - 1), BLK * sub)
            masks = _att_masks(((QB * j + sub) % nseq) != 0)
            gens = []
            for pp in range(ATT_HEADS // 2):
                lanes = slice(128 * pp, 128 * pp + 128)
                kp = kp_ref[:, lanes] if sub == 0 else kc_ref[before, lanes]
                vp = vp_ref[:, lanes] if sub == 0 else vc_ref[before, lanes]
                slabs = (q_ref[rows, lanes], kp, kc_ref[rows, lanes], vp, vc_ref[rows, lanes])
                gens += [head(2 * pp, rows, masks, *slabs), head(2 * pp + 1, rows, masks, *slabs)]
            _interleave(gens)

    cur = pl.BlockSpec((QB * BLK, 512), lambda j: (j, 0))
    prev = pl.BlockSpec((BLK, 512), lambda j: (jnp.maximum(QB * j - 1, 0), 0))
    vcur = pl.BlockSpec((QB * BLK, 512), lambda j: (j, v_col))
    vprev = pl.BlockSpec((BLK, 512), lambda j: (jnp.maximum(QB * j - 1, 0), v_col))
    return pl.pallas_call(
        body, grid=(nblk // QB,),
        in_specs=[cur, prev, cur, vprev, vcur,
                  pl.BlockSpec((1, ATT_HEADS, BLK, 2 * BLK), lambda j: (p_idx, 0, 0, 0))],
        out_specs=[cur, pl.BlockSpec((QB * BLK, 128), lambda j: (j, 0))],
        out_shape=[_sds((s, 512)), _sds((s, 128))],
        compiler_params=_cp(("parallel",)), name=name)(q, k, k, v, v, bias)


def _post_fwd(o_dn, proj, o_pats, lse_pats, dnw_t, bd128):
    s = o_dn.shape[0]

    def body(o_ref, z_ref, gate_ref, o1_ref, o4_ref, o16_ref, s1_ref, s4_ref, s16_ref, wn_ref, bd_ref, ex_ref,
             mixed_ref, oatt_ref, l1_ref, l4_ref, l16_ref, scr_a, scr_b, scr_c, scr_d):
        o = o_ref[...]
        z = z_ref[...]
        rstd = _seg_rstd(o, bd_ref[...], DN_HD)
        y_dn = o * rstd * wn_ref[...] * (z * _sigmoid(z))
        mixed_ref[:, 0:512] = y_dn.astype(BF16)
        lses = (s1_ref[...], _from_pattern(s4_ref, R4, scr_a), _from_pattern(s16_ref, R16, scr_b))
        m = jnp.maximum(jnp.maximum(lses[0], lses[1]), lses[2])
        tot = jnp.exp(lses[0] - m) + jnp.exp(lses[1] - m) + jnp.exp(lses[2] - m)
        big_l = m + jnp.log(tot)
        l1_ref[...] = big_l
        _to_patterns(big_l, ((l4_ref, R4), (l16_ref, R16)), scr_a)
        ex = ex_ref[...]
        outs = (o1_ref[...], _from_pattern(o4_ref, R4, scr_c), _from_pattern(o16_ref, R16, scr_d))
        acc = jnp.zeros_like(o)
        for lse_p, o_p in zip(lses, outs):
            acc = acc + _lane_bcast(_split3(jnp.exp(lse_p - big_l)), ex) * o_p
        gate = gate_ref[...]
        oatt_ref[...] = acc
        mixed_ref[:, 512:1024] = (acc * (gate * _sigmoid(gate))).astype(BF16)

    blk = pl.BlockSpec((TM, 512), lambda i: (i, 0))
    cblk = pl.BlockSpec((TM, 128), lambda i: (i, 0))
    p4, p16 = _pattern_spec(R4), _pattern_spec(R16)
    c4, c16 = _pattern_spec(R4, 128), _pattern_spec(R16, 128)
    return pl.pallas_call(
        body, grid=(s // TM,),
        in_specs=[blk, pl.BlockSpec((TM, 512), lambda i: (i, COL_Z)),
                  pl.BlockSpec((TM, 512), lambda i: (i, COL_GATE)), blk, p4, p16, cblk, c4, c16,
                  pl.BlockSpec((1, 512), lambda i: (0, 0)), pl.BlockSpec((512, 128), lambda i: (0, 0)),
                  pl.BlockSpec((128, 512), lambda i: (0, 0))],
        out_specs=[pl.BlockSpec((TM, D_MODEL), lambda i: (i, 0)), blk, cblk, c4, c16],
        out_shape=[_sds((s, D_MODEL), BF16), _sds((s, 512)), _sds((s, 128)), _pattern_shape(s, R4, F32, 128),
                   _pattern_shape(s, R16, F32, 128)],
        scratch_shapes=[SLABS] * 4,
        compiler_params=_cp(("parallel",)), name="post_fwd")(o_dn, proj, proj, *o_pats, *lse_pats, dnw_t, bd128,
                                                              _head_expand())


def _out_fwd(x, mixed, w_out_bf, tgt):
    s = x.shape[0]
    tm = 512

    def body(x_ref, m_ref, w_ref, t_ref, dy_ref, loss_ref):
        i = pl.program_id(0)

        @pl.when(i == 0)
        def _():
            loss_ref[...] = jnp.zeros_like(loss_ref)

        y = x_ref[...] + jnp.dot(m_ref[...], w_ref[...], preferred_element_type=F32)
        err = y - t_ref[...]
        dy_ref[...] = err * (1.0 / D_MODEL)
        part = 0.5 * jnp.sum(jnp.mean(err * err, axis=-1, keepdims=True), axis=0, keepdims=True)
        loss_ref[...] = loss_ref[...] + part

    blk = pl.BlockSpec((tm, D_MODEL), lambda i: (i, 0))
    return pl.pallas_call(
        body, grid=(s // tm,),
        in_specs=[blk, blk, pl.BlockSpec((D_MODEL, D_MODEL), lambda i: (0, 0)), blk],
        out_specs=[blk, pl.BlockSpec((8, 128), lambda i: (0, 0))],
        out_shape=[_sds((s, D_MODEL)), _sds((8, 128))],
        compiler_params=_cp(("arbitrary",)), name="out_fwd")(x, mixed, w_out_bf, tgt)


def _out_bwd(dy, mixed, w_out_bf):
    s = dy.shape[0]
    tm = 512

    def body(dy_ref, m_ref, w_ref, dm_ref, dw_ref):
        i = pl.program_id(0)

        @pl.when(i == 0)
        def _():
            dw_ref[...] = jnp.zeros_like(dw_ref)

        dyb = dy_ref[...].astype(BF16)
        dm_ref[...] = lax.dot_general(dyb, w_ref[...], (((1,), (1,)), ((), ())), preferred_element_type=F32)
        dw_ref[...] = dw_ref[...] + lax.dot_general(m_ref[...], dyb, (((0,), (0,)), ((), ())),
                                                    preferred_element_type=F32)

    blk = pl.BlockSpec((tm, D_MODEL), lambda i: (i, 0))
    full = pl.BlockSpec((D_MODEL, D_MODEL), lambda i: (0, 0))
    return pl.pallas_call(
        body, grid=(s // tm,), in_specs=[blk, blk, full], out_specs=[blk, full],
        out_shape=[_sds((s, D_MODEL)), _sds((D_MODEL, D_MODEL))],
        compiler_params=_cp(("arbitrary",)), name="out_bwd")(dy, mixed, w_out_bf)


def _post_bwd(dmixed, o_dn, proj, o_att, dnw_t, bd128):
    s = o_dn.shape[0]
    tm = TM

    def body(ddn_ref, datt_ref, o_ref, z_ref, gate_ref, oatt_ref, wn_ref, bd128_ref, hc_ref,
             do_ref, dz_ref, dgate_ref, doatt_ref, do4_ref, do16_ref, delta_ref, dl4_ref, dl16_ref, dnw_ref, scr):
        i = pl.program_id(0)

        @pl.when(i == 0)
        def _():
            dnw_ref[...] = jnp.zeros_like(dnw_ref)

        bd128v = bd128_ref[...]
        o = o_ref[...]
        z = z_ref[...]
        wn = wn_ref[...]
        dy = ddn_ref[...]
        rstd = _seg_rstd(o, bd128v, DN_HD)
        nrm = o * rstd
        sz = _sigmoid(z)
        dz_ref[...] = (dy * nrm * wn * _silu_grad(z, sz)).astype(BF16)
        dn = dy * z * sz
        gw = dn * wn
        do_ref[...] = rstd * (gw - nrm * _seg_expand(_seg_mean(gw * nrm, bd128v, DN_HD), bd128v, DN_HD))
        colsum = jnp.sum(dn * nrm, axis=0, keepdims=True)
        fold = colsum[:, 0:128] + colsum[:, 128:256] + colsum[:, 256:384] + colsum[:, 384:512]
        dnw_ref[...] = dnw_ref[...] + fold
        dya = datt_ref[...]
        gate = gate_ref[...]
        oatt = oatt_ref[...]
        sg = _sigmoid(gate)
        dgate_ref[...] = (dya * oatt * _silu_grad(gate, sg)).astype(BF16)
        doa = dya * gate * sg
        doatt_ref[...] = doa.astype(BF16)
        delta = _segsum(doa * oatt, hc_ref[...])
        delta_ref[...] = delta
        _to_patterns(doa, ((do4_ref, R4), (do16_ref, R16)), scr)
        _to_patterns(delta, ((dl4_ref, R4), (dl16_ref, R16)), scr)

    blk = pl.BlockSpec((tm, 512), lambda i: (i, 0))
    cblk = pl.BlockSpec((tm, 128), lambda i: (i, 0))
    p4, p16 = _pattern_spec(R4), _pattern_spec(R16)
    c4, c16 = _pattern_spec(R4, 128), _pattern_spec(R16, 128)
    return pl.pallas_call(
        body, grid=(s // tm,),
        in_specs=[blk, pl.BlockSpec((tm, 512), lambda i: (i, 1)), blk,
                  pl.BlockSpec((tm, 512), lambda i: (i, COL_Z)), pl.BlockSpec((tm, 512), lambda i: (i, COL_GATE)),
                  blk, pl.BlockSpec((1, 512), lambda i: (0, 0)), pl.BlockSpec((512, 128), lambda i: (0, 0)),
                  pl.BlockSpec((512, 128), lambda i: (0, 0))],
        out_specs=[blk, blk, blk, blk, p4, p16, cblk, c4, c16, pl.BlockSpec((8, 128), lambda i: (0, 0))],
        out_shape=[_sds((s, 512))] + [_sds((s, 512), BF16)] * 3 + [_pattern_shape(s, R4, BF16),
                                          _pattern_shape(s, R16, BF16), _sds((s, 128)),
                                          _pattern_shape(s, R4, F32, 128), _pattern_shape(s, R16, F32, 128),
                                          _sds((8, 128))],
        scratch_shapes=[SLABS],
        compiler_params=_cp(("arbitrary",)), name="post_bwd")(dmixed, dmixed, o_dn, proj, proj, o_att, dnw_t,
                                                               bd128, _head_compact())


def _att_bwd(q, k, v, v_col, do, big_l, delta, bias, p_idx, r, name):
    QB = QB_BWD
    s = q.shape[0]
    nblk = s // BLK
    nseq = nblk // r
    nstep = nblk // QB

    def body(q_ref, kp_ref, kc_ref, vp_ref, vc_ref, do_ref, l_ref, dl_ref, b_ref,
             dq_ref, dk_ref, dv_ref, ds_ref, dkc_ref, dvc_ref):
        j = pl.program_id(0)

        @pl.when(j == 0)
        def _():
            dkc_ref[...] = jnp.zeros_like(dkc_ref)
            dvc_ref[...] = jnp.zeros_like(dvc_ref)
            ds_ref[...] = jnp.zeros_like(ds_ref)

        @pl.when(j < nstep)
        def _():
            def head(h, sub, masks, q2, do2, kp2, kc2, vp2, vc2):
                mask_prev, mask_cur, lo_half = masks
                rows = slice(BLK * sub, BLK * sub + BLK)
                out_l, pair_l = _head_lanes(h)
                sel = lo_half if h % 2 == 0 else jnp.logical_not(lo_half)
                qm = jnp.where(sel, q2, jnp.zeros_like(q2))
                dom = jnp.where(sel, do2, jnp.zeros_like(do2))
                s_prev, s_cur = _att_scores(qm, kp2, kc2, b_ref[0, h], mask_prev, mask_cur)
                dp_prev = _dot_nt(dom, vp2)
                dp_cur = _dot_nt(dom, vc2)
                yield
                lh = l_ref[rows, HEAD_SLOT * h:HEAD_SLOT * h + 1]
                dh = dl_ref[rows, HEAD_SLOT * h:HEAD_SLOT * h + 1]
                p_prev = jnp.exp(s_prev - lh)
                p_cur = jnp.exp(s_cur - lh)
                ds_prev = p_prev * (dp_prev - dh)
                ds_cur = p_cur * (dp_cur - dh)
                ds_ref[h, :, 0:BLK] = ds_ref[h, :, 0:BLK] + ds_prev
                ds_ref[h, :, BLK:2 * BLK] = ds_ref[h, :, BLK:2 * BLK] + ds_cur
                dsb_prev, dsb_cur = ds_prev.astype(BF16), ds_cur.astype(BF16)
                pb_prev, pb_cur = p_prev.astype(BF16), p_cur.astype(BF16)
                yield
                dq_ref[rows, out_l] = (_dot(dsb_prev, kp2) + _dot(dsb_cur, kc2))[:, pair_l]
                dk_prev = _dot_tn(dsb_prev, q2)[:, pair_l]
                dv_prev = _dot_tn(pb_prev, do2)[:, pair_l]
                if sub == 0:
                    last = slice(BLK * (QB - 1), BLK * QB)
                    dk_ref[last, out_l] = dkc_ref[last, out_l] + dk_prev
                    dv_ref[last, out_l] = dvc_ref[last, out_l] + dv_prev
                else:
                    before = slice(BLK * (sub - 1), BLK * sub)
                    dkc_ref[before, out_l] = dkc_ref[before, out_l] + dk_prev
                    dvc_ref[before, out_l] = dvc_ref[before, out_l] + dv_prev
                yield
                dkc_ref[rows, out_l] = _dot_tn(dsb_cur, q2)[:, pair_l]
                dvc_ref[rows, out_l] = _dot_tn(pb_cur, do2)[:, pair_l]

            done = slice(0, BLK * (QB - 1))
            dk_ref[done, :] = dkc_ref[done, :]
            dv_ref[done, :] = dvc_ref[done, :]
            for sub in range(QB):
                rows = slice(BLK * sub, BLK * sub + BLK)
                before = slice(BLK * (sub - 1), BLK * sub)
                masks = _att_masks(((QB * j + sub) % nseq) != 0)
                gens = []
                for pp in range(ATT_HEADS // 2):
                    lanes = slice(128 * pp, 128 * pp + 128)
                    kp = kp_ref[:, lanes] if sub == 0 else kc_ref[before, lanes]
                    vp = vp_ref[:, lanes] if sub == 0 else vc_ref[before, lanes]
                    slabs = (q_ref[rows, lanes], do_ref[rows, lanes], kp, kc_ref[rows, lanes], vp,
                             vc_ref[rows, lanes])
                    gens += [head(2 * pp, sub, masks, *slabs), head(2 * pp + 1, sub, masks, *slabs)]
                _interleave(gens)

        @pl.when(j == nstep)
        def _():
            dk_ref[...] = dkc_ref[...]
            dv_ref[...] = dvc_ref[...]

    last_step = nstep - 1
    cur = pl.BlockSpec((QB * BLK, 512), lambda j: (jnp.minimum(j, last_step), 0))
    compact = pl.BlockSpec((QB * BLK, 128), lambda j: (jnp.minimum(j, last_step), 0))
    lag = pl.BlockSpec((QB * BLK, 512), lambda j: (jnp.clip(j - 1, 0, last_step), 0))
    prev = pl.BlockSpec((BLK, 512), lambda j: (jnp.clip(QB * j - 1, 0, nblk - 1), 0))
    vcur = pl.BlockSpec((QB * BLK, 512), lambda j: (jnp.minimum(j, last_step), v_col))
    vprev = pl.BlockSpec((BLK, 512), lambda j: (jnp.clip(QB * j - 1, 0, nblk - 1), v_col))
    return pl.pallas_call(
        body, grid=(nstep + 1,),
        in_specs=[cur, prev, cur, vprev, vcur, cur, compact, compact,
                  pl.BlockSpec((1, ATT_HEADS, BLK, 2 * BLK), lambda j: (p_idx, 0, 0, 0))],
        out_specs=[cur, lag, lag, pl.BlockSpec((ATT_HEADS, BLK, 2 * BLK), lambda j: (0, 0, 0))],
        out_shape=[_sds((s, 512))] * 3 + [_sds((ATT_HEADS, BLK, 2 * BLK))],
        scratch_shapes=[pltpu.VMEM((QB * BLK, 512), F32), pltpu.VMEM((QB * BLK, 512), F32)],
        compiler_params=_cp(("arbitrary",)), name=name)(q, k, k, v, v, do, big_l, delta, bias)


def _att_pre_bwd(dq_pats, dk_pats, dv_pats, proj, qw_t, kw_t, bd64):
    s = proj.shape[0]
    tm = TM

    def body(dq1_ref, dq4_ref, dq16_ref, dk1_ref, dk4_ref, dk16_ref, dv1_ref, dv4_ref, dv16_ref,
             q_ref, k_ref, qw_ref, kw_ref, bd_ref,
             dqr_ref, dkr_ref, dvr_ref, dqw_ref, dkw_ref, scr4, scr16):
        i = pl.program_id(0)

        @pl.when(i == 0)
        def _():
            dqw_ref[...] = jnp.zeros_like(dqw_ref)
            dkw_ref[...] = jnp.zeros_like(dkw_ref)

        bd = bd_ref[...]

        def total(d1_ref, d4_ref, d16_ref):
            return d1_ref[...] + _from_pattern(d4_ref, R4, scr4) + _from_pattern(d16_ref, R16, scr16)

        def one(d_refs, x_ref, w_ref, scale, dx_ref, dw_ref):
            dy = total(*d_refs) * scale
            x = x_ref[...]
            rstd = _seg_rstd(x, bd, ATT_HD)
            nrm = x * rstd
            dw_ref[...] = dw_ref[...] + jnp.sum(dy * nrm, axis=0, keepdims=True)
            g = dy * w_ref[...]
            dx_ref[...] = (rstd * (g - nrm * _seg_expand(_seg_mean(g * nrm, bd, ATT_HD), bd, ATT_HD))).astype(BF16)

        one((dq1_ref, dq4_ref, dq16_ref), q_ref, qw_ref, ATT_HD ** -0.5, dqr_ref, dqw_ref)
        one((dk1_ref, dk4_ref, dk16_ref), k_ref, kw_ref, 1.0, dkr_ref, dkw_ref)
        dvr_ref[...] = total(dv1_ref, dv4_ref, dv16_ref).astype(BF16)

    blk = pl.BlockSpec((tm, 512), lambda i: (i, 0))
    pats = [blk, _pattern_spec(R4), _pattern_spec(R16)]
    row = pl.BlockSpec((1, 512), lambda i: (0, 0))
    acc = pl.BlockSpec((8, 512), lambda i: (0, 0))
    return pl.pallas_call(
        body, grid=(s // tm,),
        in_specs=pats * 3 + [pl.BlockSpec((tm, 512), lambda i: (i, COL_ATT_Q)),
                             pl.BlockSpec((tm, 512), lambda i: (i, COL_ATT_K)), row, row,
                             pl.BlockSpec((512, 128), lambda i: (0, 0))],
        out_specs=[blk, blk, blk, acc, acc],
        out_shape=[_sds((s, 512), BF16)] * 3 + [_sds((8, 512))] * 2,
        scratch_shapes=[SLABS] * 2,
        compiler_params=_cp(("arbitrary",)), name="att_pre_bwd")(*dq_pats, *dk_pats, *dv_pats, proj, proj,
                                                                  qw_t, kw_t, bd64)


def _dn_scan_bwd(do, sp, qd, kt, w, vn, attn, gl):
    s = do.shape[0]
    nc = s // CHUNK

    def body(do_ref, sp_ref, qd_ref, kt_ref, w_ref, vn_ref, attn_ref, gl_ref,
             du_ref, dqd_ref, dkt_ref, dw_ref, dattn_ref, dgl_ref, ds_ref):
        n = pl.program_id(0)

        @pl.when(n == 0)
        def _():
            ds_ref[...] = jnp.zeros_like(ds_ref)

        def chain(cc, h):
            rows = slice(CHUNK * cc, CHUNK * cc + CHUNK)
            lanes = slice(128 * h, 128 * h + 128)
            dsn = ds_ref[h]
            st = sp_ref[cc, h]
            dsb, stb = dsn.astype(BF16), st.astype(BF16)
            dob = do_ref[rows, lanes].astype(BF16)
            vnb = vn_ref[rows, lanes].astype(BF16)
            dvn = _dot_tn(attn_ref[cc, h].astype(BF16), dob) + _dot(kt_ref[rows, lanes].astype(BF16), dsb)
            du_ref[rows, lanes] = dvn
            dqd_ref[rows, lanes] = _dot_nt(dob, stb)
            dattn_ref[cc, h] = _dot_nt(dob, vnb)
            dkt_ref[rows, lanes] = _dot_nt(vnb, dsb)
            tot = jnp.sum(jnp.sum(st * dsn, axis=1, keepdims=True), axis=0, keepdims=True)
            dgl_ref[cc, h] = jnp.broadcast_to(tot, (1, 128))
            qdo = _dot_tn(qd_ref[rows, lanes].astype(BF16), dob)
            yield
            dvb = dvn.astype(BF16)
            dw_ref[rows, lanes] = -_dot_nt(dvb, stb)
            ds_ref[h] = qdo + dsn * gl_ref[cc, h] - _dot_tn(w_ref[rows, lanes].astype(BF16), dvb)

        for cc in reversed(range(CPS_SCAN)):
            _interleave([chain(cc, h) for h in range(DN_HEADS)])

    nsteps = nc // CPS_SCAN
    big = pl.BlockSpec((CPS_SCAN * CHUNK, 512), lambda n: (nsteps - 1 - n, 0))
    sq = pl.BlockSpec((CPS_SCAN, DN_HEADS, CHUNK, CHUNK), lambda n: (nsteps - 1 - n, 0, 0, 0))
    glb = pl.BlockSpec((CPS_SCAN, DN_HEADS, 1, 128), lambda n: (nsteps - 1 - n, 0, 0, 0))
    return pl.pallas_call(
        body, grid=(nsteps,),
        in_specs=[big, pl.BlockSpec((CPS_SCAN, DN_HEADS, DN_HD, DN_HD), lambda n: (nsteps - 1 - n, 0, 0, 0)),
                  big, big, big, big, sq, glb],
        out_specs=[big, big, big, big, sq, glb],
        out_shape=[_sds((s, 512))] * 4 + [_sds((nc, DN_HEADS, CHUNK, CHUNK)), _sds((nc, DN_HEADS, 1, 128))],
        scratch_shapes=[pltpu.VMEM((DN_HEADS, DN_HD, DN_HD), F32)],
        compiler_params=_cp(("arbitrary",)), name="dn_scan_bwd")(do, sp, qd, kt, w, vn, attn, gl)


def _dn_prep_bwd(qn, kn, v, bg, bgt, tri, sel, t_inv, attn, u, w, du, dw, dqd, dkt, dattn, dgl):
    s = qn.shape[0]
    nc = s // CHUNK

    def body(q_ref, k_ref, v_ref, bg_ref, bgt_ref, tri_ref, sel_ref, t_ref, attn_ref, u_ref, w_ref,
             du_ref, dw_ref, dqd_ref, dkt_ref, dattn_ref, dgl_ref,
             dq_ref, dk_ref, dv_ref, dbg_ref):
        tri_v = tri_ref[...]
        lane = lax.broadcasted_iota(jnp.int32, (CHUNK, 128), 1)
        rowi = lax.broadcasted_iota(jnp.int32, (CHUNK, 128), 0)
        ones_b = jnp.ones((CHUNK, 128), BF16)
        ones_sq = jnp.ones((128, 128), BF16)
        parts = [[] for _ in range(CPS)]

        def chain(cc, h, bg3, gc3, gc_row):
            rows = slice(CHUNK * cc, CHUNK * cc + CHUNK)
            lanes = slice(128 * h, 128 * h + 128)
            gcc, beta, incl, strict, decay, gl = _chunk_common(bg3, gc3, gc_row, h, sel_ref)
            yield
            q = q_ref[rows, lanes]
            k = k_ref[rows, lanes]
            vv = v_ref[rows, lanes]
            ts = _split(t_ref[cc, h])
            egc = jnp.exp(gcc)
            kb = k * beta
            a_mat = jnp.where(strict, _mm_nt(kb, k) * decay, 0.0)
            dvb = _mm3_tn(ts, _split(du_ref[rows, lanes]))
            dkbg = _mm3_tn(ts, _split(dw_ref[rows, lanes]))
            yield
            d_a = jnp.where(strict, -(_mm_nt(dvb, u_ref[rows, lanes]) + _mm_nt(dkbg, w_ref[rows, lanes])), 0.0)
            d_m = d_a * decay
            dattn_m = jnp.where(incl, dattn_ref[cc, h], 0.0)
            dqk = dattn_m * decay
            e_hi, e_lo = _split(d_a * a_mat + dattn_m * attn_ref[cc, h])
            yield
            dkb = _mm(d_m, k)
            dk = _mm_tn(d_m, kb) + _mm_tn(dqk, q)
            dq = _mm(dqk, k)
            e_colsum = _dot_tn(e_hi, ones_b) + _dot_tn(e_lo, ones_b)
            e_rowsum = _dot(e_hi, ones_b) + _dot(e_lo, ones_b)
            dqd = dqd_ref[rows, lanes]
            dkt = dkt_ref[rows, lanes]
            s_dqd = _rowsum_b(dqd * q, ones_sq)
            s_dkt = _rowsum_b(dkt * k, ones_sq)
            rk = _rowsum_b(dkbg * k, ones_sq)
            s_dkb = _rowsum_b(dkb * k, ones_sq)
            s_dvb = _rowsum_b(dvb * vv, ones_sq)
            yield
            tail = jnp.exp(gl - gcc)
            r = s_dkt * tail
            dgl_tot = jnp.sum(r, axis=0, keepdims=True) + dgl_ref[cc, h] * jnp.exp(gl)
            dgc = e_rowsum - e_colsum + s_dqd * egc - r + rk * beta * egc
            dgc = dgc + jnp.where(rowi == CHUNK - 1, dgl_tot, 0.0)
            dq_ref[rows, lanes] = dq + dqd * egc
            dk_ref[rows, lanes] = dk + dkt * tail + dkbg * (beta * egc) + dkb * beta
            dv_ref[rows, lanes] = dvb * beta
            parts[cc].append((h, dgc, rk * egc + s_dkb + s_dvb))

        gens = []
        for cc in range(CPS):
            bgv = bg_ref[CHUNK * cc:CHUNK * cc + CHUNK, :]
            bg3, gc3, gc_row = _chunk_cumsum(bgv, bgt_ref[cc], tri_v)
            gens += [chain(cc, h, bg3, gc3, gc_row) for h in range(DN_HEADS)]
        _interleave(gens)
        for cc in range(CPS):
            dgc_mat = jnp.zeros((CHUNK, 128), F32)
            dbeta_mat = jnp.zeros((CHUNK, 128), F32)
            for h, dgc, dbeta in parts[cc]:
                dgc_mat = dgc_mat + jnp.where(lane == DN_HEADS + h, dgc, 0.0)
                dbeta_mat = dbeta_mat + jnp.where(lane == h, dbeta, 0.0)
            dbg_ref[CHUNK * cc:CHUNK * cc + CHUNK, :] = _mmx_tn(tri_v, dgc_mat) + dbeta_mat

    big = pl.BlockSpec((CPS * CHUNK, 512), lambda n: (n, 0))
    sq = pl.BlockSpec((CPS, DN_HEADS, CHUNK, CHUNK), lambda n: (n, 0, 0, 0))
    glb = pl.BlockSpec((CPS, DN_HEADS, 1, 128), lambda n: (n, 0, 0, 0))
    small = pl.BlockSpec((CPS * CHUNK, 128), lambda n: (n, 0))
    return pl.pallas_call(
        body, grid=(nc // CPS,),
        in_specs=[big, big, big, small, pl.BlockSpec((CPS, 8, CHUNK), lambda n: (n, 0, 0)),
                  pl.BlockSpec((CHUNK, CHUNK), lambda n: (0, 0)),
                  pl.BlockSpec((8, 128, 128), lambda n: (0, 0, 0)), sq, sq, big, big,
                  big, big, big, big, sq, glb],
        out_specs=[big, big, big, small],
        out_shape=[_sds((s, 512))] * 3 + [_sds((s, 128))],
        compiler_params=_cp(("parallel",)), name="dn_prep_bwd")(qn, kn, v, bg, bgt, tri, sel, t_inv, attn, u, w,
                                                                  du, dw, dqd, dkt, dattn, dgl)


def _dn_pre_bwd(dqn, dkn, dv, dbg, proj, conv_w8, alog_l, dtb_l):
    s = proj.shape[0]
    tr = 512
    nh = tr // 8

    def body(dq_ref, dk_ref, dv_ref, dbg_ref, u_ref, halo_ref, ba_ref, w_ref, al_ref, dt_ref,
             dy_ref, dba_ref, dsm_ref):
        i = pl.program_id(0)

        @pl.when(i == 0)
        def _():
            dsm_ref[...] = jnp.zeros_like(dsm_ref)

        keep = (i > 0).astype(F32)
        for c in range(12):
            lanes = slice(128 * c, 128 * c + 128)
            y = _conv_group(u_ref[:, lanes], halo_ref[:, lanes] * keep, w_ref, c)
            sg = _sigmoid(y)
            sv = y * sg
            if c < 8:
                rs = lax.rsqrt(jnp.sum(sv * sv, axis=1, keepdims=True) + EPS)
                n = sv * rs
                if c < 4:
                    dn = dq_ref[:, lanes] * (DN_HD ** -0.5)
                else:
                    dn = dk_ref[:, slice(128 * (c - 4), 128 * (c - 3))]
                dsv = rs * (dn - n * jnp.sum(dn * n, axis=1, keepdims=True))
            else:
                dsv = dv_ref[:, slice(128 * (c - 8), 128 * (c - 7))]
            dy_ref[:, lanes] = dsv * _silu_grad(y, sg)
        _, lane, sig_b, t, nega, g = _beta_g(ba_ref[...], al_ref[...], dt_ref[...])
        dbg = dbg_ref[...]
        da = dbg * nega * _sigmoid(t)
        is_b = lane < DN_HEADS
        is_a = jnp.logical_and(lane >= DN_HEADS, lane < 2 * DN_HEADS)
        dba_ref[...] = jnp.where(is_b, dbg * sig_b * (1.0 - sig_b), jnp.where(is_a, da, 0.0)).astype(BF16)
        d_alog = jnp.sum(jnp.where(is_a, dbg * g, 0.0), axis=0, keepdims=True)
        d_dtb = jnp.sum(jnp.where(is_a, da, 0.0), axis=0, keepdims=True)
        row = lax.broadcasted_iota(jnp.int32, (8, 128), 0)
        dsm_ref[...] = dsm_ref[...] + jnp.where(row == 0, d_alog, jnp.where(row == 1, d_dtb, 0.0))

    blk = pl.BlockSpec((tr, 512), lambda i: (i, 0))
    return pl.pallas_call(
        body, grid=(s // tr,),
        in_specs=[blk, blk, blk, pl.BlockSpec((tr, 128), lambda i: (i, 0)),
                  pl.BlockSpec((tr, 1536), lambda i: (i, 0)),
                  pl.BlockSpec((8, 1536), lambda i: (jnp.maximum(i * nh - 1, 0), 0)),
                  pl.BlockSpec((tr, 128), lambda i: (i, COL_BA_128)),
                  pl.BlockSpec((8, 1536), lambda i: (0, 0)),
                  pl.BlockSpec((1, 128), lambda i: (0, 0)), pl.BlockSpec((1, 128), lambda i: (0, 0))],
        out_specs=[pl.BlockSpec((tr, 1536), lambda i: (i, 0)), pl.BlockSpec((tr, 128), lambda i: (i, 0)),
                   pl.BlockSpec((8, 128), lambda i: (0, 0))],
        out_shape=[_sds((s, 1536)), _sds((s, 128), BF16), _sds((8, 128))],
        compiler_params=_cp(("arbitrary",)), name="dn_pre_bwd")(dqn, dkn, dv, dbg, proj, proj, proj, conv_w8,
                                                                 alog_l, dtb_l)


def _conv_bwd(dy, proj, conv_w8):
    s = dy.shape[0]
    tr = 512
    nh = tr // 8
    nblk = s // tr

    def body(dy_ref, dyn_ref, u_ref, halo_ref, w_ref, du_ref, dw_ref):
        i = pl.program_id(0)

        @pl.when(i == 0)
        def _():
            dw_ref[...] = jnp.zeros_like(dw_ref)

        keep_prev = (i > 0).astype(F32)
        keep_next = (i < nblk - 1).astype(F32)
        row = lax.broadcasted_iota(jnp.int32, (8, 128), 0)
        for c in range(12):
            lanes = slice(128 * c, 128 * c + 128)
            dyc = dy_ref[:, lanes]
            dcat = jnp.concatenate([dyc, dyn_ref[:, lanes] * keep_next], axis=0)
            xcat = jnp.concatenate([halo_ref[:, lanes] * keep_prev, u_ref[:, lanes]], axis=0)
            du = dyc * w_ref[CONV_W - 1:CONV_W, lanes]
            dwc = jnp.where(row == CONV_W - 1, jnp.sum(dyc * u_ref[:, lanes], axis=0, keepdims=True), 0.0)
            for k in range(1, CONV_W):
                du = du + pltpu.roll(dcat, tr + 8 - k, 0)[0:tr] * w_ref[CONV_W - 1 - k:CONV_W - k, lanes]
                ush = pltpu.roll(xcat, k, 0)[8:8 + tr]
                dwc = dwc + jnp.where(row == CONV_W - 1 - k, jnp.sum(dyc * ush, axis=0, keepdims=True), 0.0)
            du_ref[:, lanes] = du.astype(BF16)
            dw_ref[:, lanes] = dw_ref[:, lanes] + dwc

    return pl.pallas_call(
        body, grid=(nblk,),
        in_specs=[pl.BlockSpec((tr, 1536), lambda i: (i, 0)),
                  pl.BlockSpec((8, 1536), lambda i: (jnp.minimum((i + 1) * nh, s // 8 - 1), 0)),
                  pl.BlockSpec((tr, 1536), lambda i: (i, 0)),
                  pl.BlockSpec((8, 1536), lambda i: (jnp.maximum(i * nh - 1, 0), 0)),
                  pl.BlockSpec((8, 1536), lambda i: (0, 0))],
        out_specs=[pl.BlockSpec((tr, 1536), lambda i: (i, 0)), pl.BlockSpec((8, 1536), lambda i: (0, 0))],
        out_shape=[_sds((s, 1536), BF16), _sds((8, 1536))],
        compiler_params=_cp(("arbitrary",)), name="conv_bwd")(dy, dy, proj, proj, conv_w8)


PIECE_WIDTHS = (1536, 512, 512, 512, 512, 512, 128)


def _in_bwd_dx(pieces, w_bf, x, dy, norm_w, chip_sums=None):
    s = x.shape[0]
    tm = 512
    n_piece = len(PIECE_WIDTHS)
    n_step = s // tm
    n_sent = 0 if chip_sums is None else 3 * len(chip_sums)

    def body(*refs):
        piece_refs = refs[:n_piece]
        w_ref, x_ref, dy_ref, nw_ref = refs[n_piece:n_piece + 4]
        rest = refs[n_piece + 4:]
        i = pl.program_id(0)

        def copies():
            hb_refs = rest[:n_sent // 3]
            recv_refs = rest[n_sent // 3 + 2:2 * (n_sent // 3) + 2]
            send_sems, recv_sems = rest[-2:]
            xi, yi, ci = _position()
            out = []
            for j, (px, py) in enumerate(_other_chips(xi, yi)):
                for t, (src, dst) in enumerate(zip(hb_refs, recv_refs)):
                    k = len(hb_refs) * j + t
                    out.append(pltpu.make_async_remote_copy(
                        src_ref=src.at[2 * px + py], dst_ref=dst.at[j], send_sem=send_sems.at[k],
                        recv_sem=recv_sems.at[k], device_id=(px, py, ci), device_id_type=MESH_ID))
            return out

        dx_ref, dnw_ref = rest[n_sent // 3:n_sent // 3 + 2]

        @pl.when(i == 0)
        def _():
            dnw_ref[...] = jnp.zeros_like(dnw_ref)
            for cp in (copies() if n_sent else []):
                cp.start()

        dp = jnp.concatenate([r[...] for r in piece_refs], axis=1)
        dh = _dot(dp, w_ref[...])
        xv = x_ref[...]
        rstd = lax.rsqrt(jnp.mean(xv * xv, axis=-1, keepdims=True) + EPS)
        xh = xv * rstd
        dnw_ref[...] = dnw_ref[...] + jnp.sum(dh * xh, axis=0, keepdims=True)
        g = dh * nw_ref[...]
        dx_ref[...] = rstd * (g - xh * jnp.mean(g * xh, axis=-1, keepdims=True)) + dy_ref[...]

        if n_sent:
            @pl.when(i == n_step - 1)
            def _():
                cps = copies()
                for cp in cps:
                    cp.wait_recv()
                for cp in cps:
                    cp.wait_send()

    def blk(n):
        return pl.BlockSpec((tm, n), lambda i: (i, 0))

    in_specs = [blk(n) for n in PIECE_WIDTHS] + [pl.BlockSpec((D_IN_PAD, D_MODEL), lambda i: (0, 0)), blk(D_MODEL),
                                                blk(D_MODEL), pl.BlockSpec((1, D_MODEL), lambda i: (0, 0))]
    out_specs = [blk(D_MODEL), pl.BlockSpec((8, D_MODEL), lambda i: (0, 0))]
    out_shape = [_sds((s, D_MODEL)), _sds((8, D_MODEL))]
    scratch = []
    extra = ()
    if n_sent:
        extra = tuple(chip_sums)
        in_specs += [ANY] * len(extra)
        out_specs += [ANY] * len(extra)
        out_shape += [_sds((3,) + a.shape[1:], a.dtype) for a in extra]
        scratch = [pltpu.SemaphoreType.DMA((n_sent,)), pltpu.SemaphoreType.DMA((n_sent,))]
    return pl.pallas_call(
        body, grid=(n_step,), in_specs=in_specs, out_specs=out_specs, out_shape=out_shape, scratch_shapes=scratch,
        compiler_params=_cp(("arbitrary",)), name="in_bwd_dx")(*pieces, w_bf, x, dy, norm_w, *extra)


def _in_bwd_dw(pieces, x, norm_w):
    s = x.shape[0]
    tm = 512
    n_piece = len(PIECE_WIDTHS)

    def body(*refs):
        piece_refs = refs[:n_piece]
        x_ref, nw_ref, dw_ref = refs[n_piece:]
        i = pl.program_id(0)

        @pl.when(i == 0)
        def _():
            dw_ref[...] = jnp.zeros_like(dw_ref)

        xv = x_ref[...]
        rstd = lax.rsqrt(jnp.mean(xv * xv, axis=-1, keepdims=True) + EPS)
        h = (xv * rstd * nw_ref[...]).astype(BF16)
        at = 0
        for r, width in zip(piece_refs, PIECE_WIDTHS):
            dw_ref[at:at + width, :] = dw_ref[at:at + width, :] + _dot_tn(r[...], h)
            at += width

    return pl.pallas_call(
        body, grid=(s // tm,),
        in_specs=[pl.BlockSpec((tm, n), lambda i: (i, 0)) for n in PIECE_WIDTHS]
        + [pl.BlockSpec((tm, D_MODEL), lambda i: (i, 0)), pl.BlockSpec((1, D_MODEL), lambda i: (0, 0))],
        out_specs=pl.BlockSpec((D_IN_PAD, D_MODEL), lambda i: (0, 0)),
        out_shape=_sds((D_IN_PAD, D_MODEL)),
        compiler_params=_cp(("arbitrary",)), name="in_bwd_dw")(*pieces, x, norm_w)


def _flat(a):
    return a.reshape(-1, a.shape[-1])


def _as_pattern(a, r):
    return a if r == 1 else a.reshape(r, a.shape[0] // r, a.shape[1])


D_SHARD = D_IN // N_CHIPS
BA_START = 4 * D_DN
BA_PACKED = 4096
S1_HEAD = BA_START - D_SHARD
S1_BA = 2 * D_SHARD - BA_START


def _pack_rows(g):
    pad = jnp.zeros((D_IN_PAD - D_IN, g.shape[2]), g.dtype)
    s2_ba = 2 * DN_HEADS - S1_BA
    return jnp.concatenate([g[0][:D_SHARD], g[1][:S1_HEAD], g[2][s2_ba:D_SHARD], g[3][:D_SHARD],
                            g[1][S1_HEAD:D_SHARD], g[2][:s2_ba], pad], axis=0)


def _pack_w(g):
    n = g.shape[2]

    def body(g_ref, o_ref):
        g32 = g_ref.bitcast(jnp.uint32)
        o32 = o_ref.bitcast(jnp.uint32)
        full, head, ba1 = D_SHARD // 2, S1_HEAD // 2, S1_BA // 2
        ba2 = DN_HEADS - ba1
        pieces = [(0, 0, full), (1, 0, head), (2, ba2, full), (3, 0, full), (1, head, full), (2, 0, ba2)]
        at = 0
        for chip, lo, hi in pieces:
            o32[at:at + hi - lo, :] = g32[chip, lo:hi, :]
            at += hi - lo
        o32[at:D_IN_PAD // 2, :] = jnp.zeros((D_IN_PAD // 2 - at, n), jnp.uint32)

    vm = pl.BlockSpec(memory_space=pltpu.VMEM)
    return pl.pallas_call(body, in_specs=[vm], out_specs=vm, out_shape=_sds((D_IN_PAD, n), g.dtype),
                          compiler_params=_cp(), name="pack_w")(g)


def _unpack_w(p, rows):
    n = p.shape[1]
    tn = 256
    mid = BA_PACKED + S1_BA
    pieces = [(0, 0, (0, D_SHARD)), (1, 0, (D_SHARD, BA_START)), (1, S1_HEAD, (BA_PACKED, mid)),
              (2, 0, (mid, BA_PACKED + 2 * DN_HEADS)), (2, 2 * DN_HEADS - S1_BA, (BA_START, BA_START + S1_HEAD)),
              (3, 0, (BA_START + S1_HEAD, BA_PACKED))]

    def body(p_ref, o_ref):
        for chip, at, (lo, hi) in pieces:
            o_ref[chip, at:at + hi - lo, :] = p_ref[lo:hi, :]
        for chip in range(N_CHIPS):
            o_ref[chip, D_SHARD:rows, :] = jnp.zeros((rows - D_SHARD, tn), p.dtype)

    return pl.pallas_call(
        body, grid=(n // tn,),
        in_specs=[pl.BlockSpec((D_IN_PAD, tn), lambda j: (0, j))],
        out_specs=pl.BlockSpec((N_CHIPS, rows, tn), lambda j: (0, 0, j)),
        out_shape=_sds((N_CHIPS, rows, n), p.dtype),
        compiler_params=_cp(("parallel",)), name="unpack_w")(p)


def _unpack_rows(p, rows):
    mid = BA_PACKED + S1_BA
    pad = jnp.zeros((rows - D_SHARD, p.shape[1]), p.dtype)
    return jnp.stack([jnp.concatenate([p[0:D_SHARD], pad], axis=0),
                      jnp.concatenate([p[D_SHARD:BA_START], p[BA_PACKED:mid], pad], axis=0),
                      jnp.concatenate([p[mid:BA_PACKED + 2 * DN_HEADS], p[BA_START:BA_START + S1_HEAD], pad], axis=0),
                      jnp.concatenate([p[BA_START + S1_HEAD:BA_PACKED], pad], axis=0)])


def _lane_row(vec, offset):
    return jnp.pad(vec.reshape(1, -1), ((0, 0), (offset, 128 - offset - vec.shape[0])))


def _local_step(x, tgt, norm_w, w_bf, conv_w, a_log, dt_bias, dn_norm_w, q_norm_w, k_norm_w, rel_bias, w_out_bf):
    s = x.shape[0]
    nc = s // CHUNK
    conv_w8 = jnp.pad(conv_w, ((0, 8 - CONV_W), (0, 0)))
    alog_l = _lane_row(a_log.reshape(-1), DN_HEADS)
    dtb_l = _lane_row(dt_bias.reshape(-1), DN_HEADS)
    dnw_t = jnp.tile(dn_norm_w.reshape(1, DN_HD), (1, DN_HEADS))
    qw_t = jnp.tile(q_norm_w.reshape(1, ATT_HD), (1, ATT_HEADS))
    kw_t = jnp.tile(k_norm_w.reshape(1, ATT_HD), (1, ATT_HEADS))
    bd128 = _compact_mat(DN_HD)
    bd64 = _compact_mat(ATT_HD)
    tri = _tri_incl()
    sel = _lane_select()
    buckets = _bucket_tables()

    proj = _in_proj(x, norm_w, w_bf)
    qn, kn, v_dn, bg = _dn_pre(proj, conv_w8, alog_l, dtb_l)
    bgt = bg[:, 0:8].reshape(nc, CHUNK, 8).transpose(0, 2, 1)
    u, w, qd, kt, attn, t_inv, gl = _dn_prep(qn, kn, v_dn, bg, bgt, tri, sel)
    o_dn, vn, sp = _dn_scan(u, w, qd, kt, attn, gl)
    q1, k1, v1, q4, k4, v4, q16, k16, v16 = _att_pre(proj, qw_t, kw_t, bd64)
    rs = [r for _, r in PATTERNS]
    qkv = [(q1, k1, v1, 0), (_flat(q4), _flat(k4), _flat(v4), 0), (_flat(q16), _flat(k16), _flat(v16), 0)]
    bias = _bias_fwd(rel_bias, buckets)
    o_pats, lse_pats = [], []
    for p, r in enumerate(rs):
        o_p, lse_p = _att_fwd(*qkv[p], bias, p, r, "att_fwd_r%d" % r)
        o_pats.append(_as_pattern(o_p, r))
        lse_pats.append(_as_pattern(lse_p, r))
    mixed, o_att, l1, l4, l16 = _post_fwd(o_dn, proj, o_pats, lse_pats, dnw_t, bd128)
    dy, loss_blk = _out_fwd(x, mixed, w_out_bf, tgt)

    dmixed, d_w_out = _out_bwd(dy, mixed, w_out_bf)
    do_dn, dz, dgate, do1, do4, do16, dl1, dl4, dl16, d_dnw = _post_bwd(dmixed, o_dn, proj, o_att, dnw_t, bd128)
    side = [(do1, l1, dl1), (_flat(do4), _flat(l4), _flat(dl4)), (_flat(do16), _flat(l16), _flat(dl16))]
    dq_pats, dk_pats, dv_pats, ds_accs = [], [], [], []
    for p, r in enumerate(rs):
        dq_p, dk_p, dv_p, ds_p = _att_bwd(*qkv[p], *side[p], bias, p, r, "att_bwd_r%d" % r)
        dq_pats.append(_as_pattern(dq_p, r))
        dk_pats.append(_as_pattern(dk_p, r))
        dv_pats.append(_as_pattern(dv_p, r))
        ds_accs.append(ds_p)
    d_rel_bias = _bias_bwd(ds_accs, buckets)
    dq_att, dk_att, dv_att, d_qw, d_kw = _att_pre_bwd(dq_pats, dk_pats, dv_pats, proj, qw_t, kw_t, bd64)
    du, dqd, dkt, dw, dattn, dgl = _dn_scan_bwd(do_dn, sp, qd, kt, w, vn, attn, gl)
    dqn, dkn, dv_dn, dbg = _dn_prep_bwd(qn, kn, v_dn, bg, bgt, tri, sel, t_inv, attn, u, w, du, dw, dqd, dkt, dattn,
                                        dgl)
    dyc, dba, dsm = _dn_pre_bwd(dqn, dkn, dv_dn, dbg, proj, conv_w8, alog_l, dtb_l)
    d_qkv_dn, d_conv8 = _conv_bwd(dyc, proj, conv_w8)
    pieces = (d_qkv_dn, dz, dq_att, dk_att, dv_att, dgate, dba)
    d_w_in_t = _in_bwd_dw(pieces, x, norm_w)
    last = functools.partial(_in_bwd_dx, pieces, w_bf, x, dy, norm_w)

    grads = dict(
        w_in_t=d_w_in_t,
        conv_w=d_conv8[0:CONV_W, :],
        a_log=dsm[0:1, DN_HEADS:2 * DN_HEADS],
        dt_bias=dsm[1:2, DN_HEADS:2 * DN_HEADS],
        dn_norm_w=d_dnw[0:1, :],
        q_norm_w=d_qw[0:1, :].reshape(ATT_HEADS, ATT_HD),
        k_norm_w=d_kw[0:1, :].reshape(ATT_HEADS, ATT_HD),
        rel_bias=d_rel_bias,
        w_out=d_w_out,
    )
    return loss_blk[0, 0], last, grads


MESH_ID = pl.DeviceIdType.MESH
ANY = pl.BlockSpec(memory_space=pl.ANY)


def _position():
    return lax.axis_index("x"), lax.axis_index("y"), lax.axis_index("c")


def _other_chips(x, y):
    return [(1 - x, y), (x, 1 - y), (1 - x, 1 - y)]


SHARD_PAD = 1040
WIN = 528


def _row_split(n):
    return (n // 2) // 128 * 128


def _part(ref, cc):
    n = ref.shape[0]
    sp = _row_split(n)
    return ref.at[pl.ds(0, sp)] if cc == 0 else ref.at[pl.ds(sp, n - sp)]


def _gather_weights(wt_s, w_out_s, conv_s):
    def body(a_ref, b_ref, c_ref, ga_ref, gb_ref, gc_ref, send_sems, recv_sems, loc_sems, a_vmem, b_vmem):
        x, y, c = _position()
        me = 2 * x + y
        sib = (x, y, 1 - c)
        big = ((a_ref, ga_ref), (b_ref, gb_ref))
        stage_in = [pltpu.make_async_copy(a_ref, a_vmem, loc_sems.at[0]),
                    pltpu.make_async_copy(b_ref, b_vmem, loc_sems.at[1])]
        local = [pltpu.make_async_copy(a_vmem, ga_ref.at[me], loc_sems.at[0]),
                 pltpu.make_async_copy(b_vmem, gb_ref.at[me], loc_sems.at[1]),
                 pltpu.make_async_copy(c_ref, gc_ref.at[me], loc_sems.at[2])]
        for cp in stage_in:
            cp.start()
        local[2].start()
        for cp in stage_in:
            cp.wait()
        for cp in local[:2]:
            cp.start()
        others = _other_chips(x, y)

        def exchange(cc):
            sends = []
            for j, (px, py) in enumerate(others):
                for t, (src, dst) in enumerate(big):
                    k = 3 * j + t
                    sends.append(pltpu.make_async_remote_copy(
                        src_ref=_part(src, cc), dst_ref=_part(dst.at[me], cc), send_sem=send_sems.at[k],
                        recv_sem=recv_sems.at[k], device_id=(px, py, c), device_id_type=MESH_ID))
                sends.append(pltpu.make_async_remote_copy(
                    src_ref=c_ref, dst_ref=gc_ref.at[me], send_sem=send_sems.at[3 * j + 2],
                    recv_sem=recv_sems.at[3 * j + 2], device_id=(px, py, c), device_id_type=MESH_ID))
            for cp in sends:
                cp.start()
            for j, (px, py) in enumerate(others):
                src_chip = 2 * px + py
                for t, (src, dst) in enumerate(big):
                    landed = _part(dst.at[src_chip], cc)
                    pltpu.make_async_remote_copy(
                        src_ref=_part(src, cc), dst_ref=landed, send_sem=send_sems.at[3 * j + t],
                        recv_sem=recv_sems.at[3 * j + t], device_id=(px, py, c), device_id_type=MESH_ID).wait_recv()
                    k = 9 + 2 * j + t
                    fwd = pltpu.make_async_remote_copy(
                        src_ref=landed, dst_ref=landed, send_sem=send_sems.at[k], recv_sem=recv_sems.at[k],
                        device_id=sib, device_id_type=MESH_ID)
                    fwd.start()
                    sends.append(fwd)
                pltpu.make_async_remote_copy(
                    src_ref=c_ref, dst_ref=gc_ref.at[src_chip], send_sem=send_sems.at[3 * j + 2],
                    recv_sem=recv_sems.at[3 * j + 2], device_id=(px, py, c), device_id_type=MESH_ID).wait_recv()
            for j, (px, py) in enumerate(others):
                src_chip = 2 * px + py
                for t, (src, dst) in enumerate(big):
                    k = 9 + 2 * j + t
                    theirs = _part(dst.at[src_chip], 1 - cc)
                    pltpu.make_async_remote_copy(
                        src_ref=theirs, dst_ref=theirs, send_sem=send_sems.at[k], recv_sem=recv_sems.at[k],
                        device_id=sib, device_id_type=MESH_ID).wait_recv()
            for cp in sends:
                cp.wait_send()

        for cc in (0, 1):
            pl.when(c == cc)(functools.partial(exchange, cc))
        for cp in local:
            cp.wait()

    srcs = (wt_s, w_out_s, conv_s)
    n_sem = 9 + 6
    return pl.pallas_call(
        body, in_specs=[ANY] * 3, out_specs=[ANY] * 3,
        out_shape=[_sds((N_CHIPS,) + a.shape, a.dtype) for a in srcs],
        scratch_shapes=[pltpu.SemaphoreType.DMA((n_sem,)), pltpu.SemaphoreType.DMA((n_sem,)),
                        pltpu.SemaphoreType.DMA((3,)), pltpu.VMEM(wt_s.shape, wt_s.dtype),
                        pltpu.VMEM(w_out_s.shape, w_out_s.dtype)],
        compiler_params=_cp(), name="gather_weights")(*srcs)


N_DEV = 8
PEER_FLIPS = [(dx, dy, dc) for dx in (0, 1) for dy in (0, 1) for dc in (0, 1)][1:]


def _small_copies(s_ref, rs_ref, send_sems, recv_sems):
    x, y, c = _position()
    dev = 4 * x + 2 * y + c
    sends, recvs = [], []
    for f, (dx, dy, dc) in enumerate(PEER_FLIPS):
        peer = (x ^ dx, y ^ dy, c ^ dc)
        sends.append(pltpu.make_async_remote_copy(
            src_ref=s_ref, dst_ref=rs_ref.at[dev], send_sem=send_sems.at[f], recv_sem=recv_sems.at[f],
            device_id=peer, device_id_type=MESH_ID))
        recvs.append(pltpu.make_async_remote_copy(
            src_ref=s_ref, dst_ref=rs_ref.at[4 * peer[0] + 2 * peer[1] + peer[2]], send_sem=send_sems.at[f],
            recv_sem=recv_sems.at[f], device_id=peer, device_id_type=MESH_ID))
    return sends, recvs


def _fill_parts(h_in, r_in, h_out, r_out, small, rows_in, rows_out):
    def body(ha_ref, ra_ref, hb_ref, rb_ref, s_ref, fa_ref, fb_ref, rs_ref, send_sems, recv_sems, loc_sems,
             small_send, small_recv, a_vmem, b_vmem, ra_vmem, rb_vmem):
        x, y, c = _position()
        sib = (x, y, 1 - c)
        chip = 2 * x + y
        pairs = ((a_vmem, fa_ref), (b_vmem, fb_ref))
        own_small = pltpu.make_async_copy(s_ref, rs_ref.at[4 * x + 2 * y + c], loc_sems.at[2])
        own_small.start()
        small_sends, small_recvs = _small_copies(s_ref, rs_ref, small_send, small_recv)
        for cp in small_sends:
            cp.start()
        stage_in = [pltpu.make_async_copy(ha_ref.at[chip], a_vmem, loc_sems.at[0]),
                    pltpu.make_async_copy(hb_ref.at[chip], b_vmem, loc_sems.at[1]),
                    pltpu.make_async_copy(ra_ref, ra_vmem, loc_sems.at[3]),
                    pltpu.make_async_copy(rb_ref, rb_vmem, loc_sems.at[4])]
        for cp in stage_in:
            cp.start()
        for cp in stage_in:
            cp.wait()
        for tot, recv in ((a_vmem, ra_vmem), (b_vmem, rb_vmem)):
            acc = tot[...]
            for j in range(3):
                acc = acc + recv[j].astype(F32)
            tot[...] = acc

        def fill(cc):
            mine = [_part(dst, cc) for _, dst in pairs]
            srcs = [src.at[pl.ds(0, m.shape[0])] for src, m in zip((a_vmem, b_vmem), mine)]
            local = [pltpu.make_async_copy(s, m, loc_sems.at[t]) for t, (s, m) in enumerate(zip(srcs, mine))]
            sends = [pltpu.make_async_remote_copy(src_ref=s, dst_ref=m, send_sem=send_sems.at[t],
                                                  recv_sem=recv_sems.at[t], device_id=sib, device_id_type=MESH_ID)
                     for t, (s, m) in enumerate(zip(srcs, mine))]
            for cp in local + sends:
                cp.start()
            for t, (_, dst) in enumerate(pairs):
                theirs = _part(dst, 1 - cc)
                pltpu.make_async_remote_copy(src_ref=theirs, dst_ref=theirs, send_sem=send_sems.at[t],
                                             recv_sem=recv_sems.at[t], device_id=sib, device_id_type=MESH_ID).wait_recv()
            for cp in sends:
                cp.wait_send()
            for cp in local:
                cp.wait()

        for cc in (0, 1):
            pl.when(c == cc)(functools.partial(fill, cc))
        for cp in small_recvs:
            cp.wait_recv()
        for cp in small_sends:
            cp.wait_send()
        own_small.wait()

    n_peer = len(PEER_FLIPS)
    return pl.pallas_call(
        body, in_specs=[ANY] * 5, out_specs=[ANY] * 3,
        out_shape=[_sds((rows_in, D_MODEL)), _sds((rows_out, D_MODEL)), _sds((N_DEV,) + small.shape, small.dtype)],
        scratch_shapes=[pltpu.SemaphoreType.DMA((2,)), pltpu.SemaphoreType.DMA((2,)), pltpu.SemaphoreType.DMA((5,)),
                        pltpu.SemaphoreType.DMA((n_peer,)), pltpu.SemaphoreType.DMA((n_peer,)),
                        pltpu.VMEM(h_in.shape[1:], F32), pltpu.VMEM(h_out.shape[1:], F32),
                        pltpu.VMEM(r_in.shape, r_in.dtype), pltpu.VMEM(r_out.shape, r_out.dtype)],
        compiler_params=_cp(), name="fill_parts")(h_in, r_in, h_out, r_out, small)


def _chip_sums(u_in, u_out, win_in, win_out):
    wins = (win_in, win_out)

    def body(a_ref, b_ref, ha_ref, hba_ref, hb_ref, hbb_ref, send_sems, recv_sems, loc_sems, out_sems,
             mine_a, theirs_a, sum_a, sumb_a, mine_b, theirs_b, sum_b, sumb_b):
        x, y, c = _position()
        sib = (x, y, 1 - c)
        groups = ((a_ref, mine_a, theirs_a, sum_a, sumb_a, ha_ref, hba_ref, win_in),
                  (b_ref, mine_b, theirs_b, sum_b, sumb_b, hb_ref, hbb_ref, win_out))
        loads, sends = [], []
        for t, (src, mine, theirs, _, _, _, _, win) in enumerate(groups):
            split = _row_split(src.shape[1])
            for k in range(N_CHIPS):
                n = N_CHIPS * t + k
                loads.append(pltpu.make_async_copy(
                    src.at[k, pl.ds(pl.multiple_of(c * split, split), win), :], mine.at[k], loc_sems.at[n]))
                sends.append(pltpu.make_async_remote_copy(
                    src_ref=src.at[k, pl.ds(pl.multiple_of((1 - c) * split, split), win), :], dst_ref=theirs.at[k],
                    send_sem=send_sems.at[n], recv_sem=recv_sems.at[n], device_id=sib, device_id_type=MESH_ID))
        for cp in sends + loads:
            cp.start()
        stores = []
        for t, (_, mine, theirs, tot, totb, h_out, hb_out, _) in enumerate(groups):
            for k in range(N_CHIPS):
                n = N_CHIPS * t + k
                loads[n].wait()
                sends[n].wait_recv()
                val = mine[k] + theirs[k]
                tot[k] = val
                totb[k] = val.astype(BF16)
                stores += [pltpu.make_async_copy(tot.at[k], h_out.at[k], out_sems.at[2 * n]),
                           pltpu.make_async_copy(totb.at[k], hb_out.at[k], out_sems.at[2 * n + 1])]
                stores[-2].start()
                stores[-1].start()
        for cp in sends:
            cp.wait_send()
        for cp in stores:
            cp.wait()

    shapes = [(N_CHIPS, w, D_MODEL) for w in wins]
    n_cp = 2 * N_CHIPS
    vmem = []
    for shp in shapes:
        vmem += [pltpu.VMEM(shp, F32), pltpu.VMEM(shp, F32), pltpu.VMEM(shp, F32), pltpu.VMEM(shp, BF16)]
    return pl.pallas_call(
        body, in_specs=[ANY] * 2, out_specs=[ANY] * 4,
        out_shape=[_sds(shapes[0]), _sds(shapes[0], BF16), _sds(shapes[1]), _sds(shapes[1], BF16)],
        scratch_shapes=[pltpu.SemaphoreType.DMA((n_cp,)), pltpu.SemaphoreType.DMA((n_cp,)),
                        pltpu.SemaphoreType.DMA((n_cp,)), pltpu.SemaphoreType.DMA((2 * n_cp,))] + vmem,
        compiler_params=_cp(), name="chip_sums")(u_in, u_out)


SMALL_LAYOUT = (("norm_w", 1024), ("conv_w", 6144), ("a_log", 128), ("dt_bias", 128), ("dn_norm_w", 128),
                ("q_norm_w", 512), ("k_norm_w", 512), ("rel_bias", 256), ("loss", 128))
SMALL_TOTAL = sum(n for _, n in SMALL_LAYOUT)


def _small_offset(name):
    off = 0
    for n, size in SMALL_LAYOUT:
        if n == name:
            return off
        off += size
    raise KeyError(name)


def _pack_small(grads):
    parts = []
    for name, size in SMALL_LAYOUT:
        flat = grads[name].reshape(1, -1)
        parts.append(jnp.pad(flat, ((0, 0), (0, size - flat.shape[1]))))
    return jnp.concatenate(parts, axis=1)


def _sum_small(rows):
    n_dev = rows.shape[0]
    q_off = _small_offset("q_norm_w")
    k_off = _small_offset("k_norm_w")

    def body(r_ref, tot_ref, qk_ref):
        tot = r_ref[0:1, :]
        for d in range(1, n_dev):
            tot = tot + r_ref[d:d + 1, :]
        tot_ref[...] = tot
        for row, off in ((0, q_off), (1, k_off)):
            s4 = tot[:, off:off + 128] + tot[:, off + 128:off + 256] + tot[:, off + 256:off + 384] \
                + tot[:, off + 384:off + 512]
            qk_ref[row:row + 1, :] = s4 + pltpu.roll(s4, ATT_HD, 1)

    return pl.pallas_call(
        body, in_specs=[pl.BlockSpec(memory_space=pltpu.VMEM)],
        out_specs=[pl.BlockSpec(memory_space=pltpu.VMEM)] * 2,
        out_shape=[_sds((1, SMALL_TOTAL)), _sds((2, 128))],
        compiler_params=_cp(), name="sum_small")(rows)


def _adamw_math(w, g, m, v):
    m = ADAM_B1 * m + (1.0 - ADAM_B1) * g
    v = ADAM_B2 * v + (1.0 - ADAM_B2) * (g * g)
    m_hat = m / (1.0 - ADAM_B1 ** ADAM_STEP)
    v_hat = v / (1.0 - ADAM_B2 ** ADAM_STEP)
    delta = -ADAM_LR * (m_hat / (jnp.sqrt(v_hat) + ADAM_EPS) + ADAM_WD * w)
    return delta, m, v


def _adamw_big(g, w, m, v, name):
    rows, cols = w.shape
    tr = 128

    def body(g_ref, w_ref, m_ref, v_ref, go_ref, d_ref, nm_ref, nv_ref):
        g = g_ref[...]
        go_ref[...] = g
        d_ref[...], nm_ref[...], nv_ref[...] = _adamw_math(w_ref[...], g, m_ref[...], v_ref[...])

    blk = pl.BlockSpec((tr, cols), lambda i: (i, 0))
    return pl.pallas_call(
        body, grid=(pl.cdiv(rows, tr),), in_specs=[blk] * 4, out_specs=[blk] * 4,
        out_shape=[_sds((rows, cols))] * 4, compiler_params=_cp(("parallel",)), name=name)(g, w, m, v)


def _adamw_rows(g, w, m, v, name):
    rows, cols = w.shape
    tr = 128

    def body(g_ref, w_ref, m_ref, v_ref, go_ref, d_ref, nm_ref, nv_ref, s_g, s_d, s_m, s_v):
        g = g_ref[...]
        s_g[...] = g
        s_d[...], s_m[...], s_v[...] = _adamw_math(w_ref[...], g, m_ref[...], v_ref[...])
        for i in range(tr):
            for scr, out in ((s_g, go_ref), (s_d, d_ref), (s_m, nm_ref), (s_v, nv_ref)):
                out[i] = scr[i:i + 1, :]

    blk = pl.BlockSpec((tr, cols), lambda i: (i, 0))
    oblk = pl.BlockSpec((tr, 1, cols), lambda i: (i, 0, 0))
    return pl.pallas_call(
        body, grid=(pl.cdiv(rows, tr),), in_specs=[blk] * 4, out_specs=[oblk] * 4,
        out_shape=[_sds((rows, 1, cols))] * 4, scratch_shapes=[pltpu.VMEM((tr, cols), F32)] * 4,
        compiler_params=_cp(("parallel",)), name=name)(g, w, m, v)


def _adamw_small(w, g, m, v, name):
    def body(w_ref, g_ref, m_ref, v_ref, d_ref, nm_ref, nv_ref):
        d_ref[...], nm_ref[...], nv_ref[...] = _adamw_math(w_ref[...], g_ref[...], m_ref[...], v_ref[...])

    vm = pl.BlockSpec(memory_space=pltpu.VMEM)
    return pl.pallas_call(body, in_specs=[vm] * 4, out_specs=[vm] * 3, out_shape=[_sds(w.shape)] * 3,
                          compiler_params=_cp(), name=name)(w, g, m, v)


WEIGHTS = ("norm_w", "w_in", "conv_w", "a_log", "dt_bias", "dn_norm_w", "q_norm_w", "k_norm_w", "rel_bias", "w_out")


def kernel(x, norm_w, w_in, conv_w, a_log, dt_bias, dn_norm_w, q_norm_w, k_norm_w, rel_bias, w_out, loss_target, m_norm_w, m_w_in, m_conv_w, m_a_log, m_dt_bias, m_dn_norm_w, m_q_norm_w, m_k_norm_w, m_rel_bias, m_w_out, v_norm_w, v_w_in, v_conv_w, v_a_log, v_dt_bias, v_dn_norm_w, v_q_norm_w, v_k_norm_w, v_rel_bias, v_w_out):
    xi, yi, _ = _position()
    chip = 2 * xi + yi
    w_loc = dict(norm_w=norm_w, w_in=w_in[0].T, conv_w=conv_w[0], a_log=a_log, dt_bias=dt_bias, dn_norm_w=dn_norm_w,
                 q_norm_w=q_norm_w, k_norm_w=k_norm_w, rel_bias=rel_bias, w_out=w_out[0])
    m_loc = dict(norm_w=m_norm_w, w_in=m_w_in[0].T, conv_w=m_conv_w[0], a_log=m_a_log, dt_bias=m_dt_bias,
                 dn_norm_w=m_dn_norm_w, q_norm_w=m_q_norm_w, k_norm_w=m_k_norm_w, rel_bias=m_rel_bias,
                 w_out=m_w_out[0])
    v_loc = dict(norm_w=v_norm_w, w_in=v_w_in[0].T, conv_w=v_conv_w[0], a_log=v_a_log, dt_bias=v_dt_bias,
                 dn_norm_w=v_dn_norm_w, q_norm_w=v_q_norm_w, k_norm_w=v_k_norm_w, rel_bias=v_rel_bias,
                 w_out=v_w_out[0])

    wt_pad = jnp.pad(w_loc["w_in"].astype(BF16), ((0, SHARD_PAD - D_SHARD), (0, 0)))
    g_in, g_out, g_conv = _gather_weights(wt_pad, w_loc["w_out"].astype(BF16), w_loc["conv_w"])
    wt_full = _pack_w(g_in)
    w_out_full = g_out.reshape(D_MODEL, D_MODEL)
    conv_full = g_conv.transpose(1, 0, 2).reshape(CONV_W, 3 * D_DN)

    loss_local, last_kernel, grads = _local_step(x[0], loss_target[0], norm_w, wt_full, conv_full, a_log, dt_bias,
                                                 dn_norm_w, q_norm_w, k_norm_w, rel_bias, w_out_full)
    grads["loss"] = loss_local

    u_in = _unpack_w(grads["w_in_t"], SHARD_PAD)
    u_out = grads["w_out"].reshape(N_CHIPS, D_MODEL // N_CHIPS, D_MODEL)
    win_out = u_out.shape[1] // 2
    h_in, hb_in, h_out, hb_out = _chip_sums(u_in, u_out, WIN, win_out)
    grad_x, d_nw8, r_in, r_out = last_kernel(chip_sums=(hb_in, hb_out))
    grads["norm_w"] = d_nw8[0:1, :]
    small = _pack_small(grads)
    full_in, full_out, r_small = _fill_parts(h_in, r_in, h_out, r_out, small, SHARD_PAD, u_out.shape[1])
    tot_small, qk = _sum_small(r_small.reshape(8, SMALL_TOTAL))

    def small_grad(name, n):
        off = _small_offset(name)
        return tot_small[:, off:off + n]

    loss = small_grad("loss", 1).reshape(())
    conv_all = small_grad("conv_w", CONV_W * 3 * D_DN).reshape(CONV_W, 3 * D_DN)
    g_small = dict(
        norm_w=small_grad("norm_w", D_MODEL),
        conv_w=lax.dynamic_slice_in_dim(conv_all, chip * (3 * D_DN // N_CHIPS), 3 * D_DN // N_CHIPS, axis=1),
        a_log=small_grad("a_log", DN_HEADS),
        dt_bias=small_grad("dt_bias", DN_HEADS),
        dn_norm_w=small_grad("dn_norm_w", DN_HD),
        q_norm_w=qk[0:1, 0:ATT_HD],
        k_norm_w=qk[1:2, 0:ATT_HD],
        rel_bias=small_grad("rel_bias", ATT_HEADS * N_BUCKETS).reshape(ATT_HEADS, N_BUCKETS),
    )

    out_g, out_d, out_m, out_v = {}, {}, {}, {}
    out_g["w_in"], out_d["w_in"], out_m["w_in"], out_v["w_in"] = _adamw_rows(
        full_in, w_loc["w_in"], m_loc["w_in"], v_loc["w_in"], "adamw_w_in")
    out_g["w_out"], out_d["w_out"], out_m["w_out"], out_v["w_out"] = _adamw_big(
        full_out, w_loc["w_out"], m_loc["w_out"], v_loc["w_out"], "adamw_w_out")
    for name in g_small:
        out_g[name] = g_small[name]
        out_d[name], out_m[name], out_v[name] = _adamw_small(w_loc[name], g_small[name], m_loc[name], v_loc[name],
                                                             "adamw_" + name)
    for d in (out_g, out_d, out_m, out_v):
        d["w_in"] = d["w_in"].transpose(1, 2, 0)
        for name in ("conv_w", "w_out"):
            d[name] = d[name][None]
    return (loss, grad_x[None], *[out_g[n] for n in WEIGHTS], *[out_d[n] for n in WEIGHTS],
            *[out_m[n] for n in WEIGHTS], *[out_v[n] for n in WEIGHTS])
```

```python
import functools
import math

import numpy as np
import jax
import jax.numpy as jnp
from jax import lax
from jax.experimental import pallas as pl
from jax.experimental.pallas import tpu as pltpu

F32 = jnp.float32
BF16 = jnp.bfloat16
HI = lax.Precision.HIGHEST

D_MODEL = 1024
D_DN = 512
DN_HEADS = 4
DN_HD = 128
CONV_W = 4
CHUNK = 64
D_ATT = 512
ATT_HEADS = 8
ATT_HD = 64
PATTERNS = ((128, 1), (512, 4), (2048, 16))
N_BUCKETS = 32
MAX_DISTANCE = 2048
D_IN = 4 * D_DN + 2 * DN_HEADS + 4 * D_ATT
D_IN_PAD = 4224
EPS = 1e-6
BLK = 128
NEG = -1e30
N_CHIPS = 4

ADAM_LR = 0.001
ADAM_B1 = 0.9
ADAM_B2 = 0.999
ADAM_EPS = 1e-08
ADAM_WD = 0.01
ADAM_STEP = 10

VMEM_LIMIT = 56 * 1024 * 1024

COL_Z = 3
COL_ATT_Q = 4
COL_ATT_K = 5
COL_ATT_V = 6
COL_GATE = 7
COL_BA_128 = 32


def _cp(sem=None):
    if sem is None:
        return pltpu.CompilerParams(vmem_limit_bytes=VMEM_LIMIT)
    return pltpu.CompilerParams(dimension_semantics=sem, vmem_limit_bytes=VMEM_LIMIT)


def _sds(shape, dtype=F32):
    return jax.ShapeDtypeStruct(shape, dtype)


def _mm(a, b):
    return jnp.dot(a.astype(BF16), b.astype(BF16), preferred_element_type=F32)


def _mm_nt(a, b):
    return lax.dot_general(a.astype(BF16), b.astype(BF16), (((1,), (1,)), ((), ())),
                           preferred_element_type=F32)


def _mm_tn(a, b):
    return lax.dot_general(a.astype(BF16), b.astype(BF16), (((0,), (0,)), ((), ())),
                           preferred_element_type=F32)


def _mmx(a, b):
    return jnp.dot(a, b, precision=HI, preferred_element_type=F32)


def _mmx_nt(a, b):
    return lax.dot_general(a, b, (((1,), (1,)), ((), ())), precision=HI, preferred_element_type=F32)


def _mmx_tn(a, b):
    return lax.dot_general(a, b, (((0,), (0,)), ((), ())), precision=HI, preferred_element_type=F32)


def _dot(a, b):
    return jnp.dot(a, b, preferred_element_type=F32)


def _dot_nt(a, b):
    return lax.dot_general(a, b, (((1,), (1,)), ((), ())), preferred_element_type=F32)


def _dot_tn(a, b):
    return lax.dot_general(a, b, (((0,), (0,)), ((), ())), preferred_element_type=F32)


def _split(a):
    hi = a.astype(BF16)
    return hi, (a - hi.astype(F32)).astype(BF16)


def _mm3(a_s, b_s):
    return _dot(a_s[0], b_s[0]) + _dot(a_s[0], b_s[1]) + _dot(a_s[1], b_s[0])


def _mm3_tn(a_s, b_s):
    return _dot_tn(a_s[0], b_s[0]) + _dot_tn(a_s[0], b_s[1]) + _dot_tn(a_s[1], b_s[0])


def _interleave(gens):
    live = list(gens)
    while live:
        nxt = []
        for g in live:
            try:
                next(g)
                nxt.append(g)
            except StopIteration:
                pass
        live = nxt


def _segsum(x, bd):
    hi = x.astype(BF16)
    r1 = x - hi.astype(F32)
    mid = r1.astype(BF16)
    lo = (r1 - mid.astype(F32)).astype(BF16)
    return (jnp.dot(hi, bd, preferred_element_type=F32) + jnp.dot(mid, bd, preferred_element_type=F32)
            + jnp.dot(lo, bd, preferred_element_type=F32))


def _compact_mat(seg, n=512):
    slot = 128 * seg // n
    src = np.arange(n)[:, None]
    dst = np.arange(128)[None, :]
    return jnp.asarray((src // seg == dst // slot).astype(np.float32), dtype=BF16)


def _seg_mean(x, cm, seg):
    hi, lo = _split(x)
    return (_dot(hi, cm) + _dot(lo, cm)) * (1.0 / seg)


def _seg_expand(c, cm, seg):
    hi, lo = _split(c)
    return (_dot_nt(hi, cm) + _dot_nt(lo, cm)) * (cm.shape[0] / (128.0 * seg))


def _seg_rstd(x, cm, seg):
    return _seg_expand(lax.rsqrt(_seg_mean(x * x, cm, seg) + EPS), cm, seg)


def _sigmoid(x):
    return 1.0 / (1.0 + jnp.exp(-x))


def _silu_grad(x, s):
    return s * (1.0 + x * (1.0 - s))


def _tri_incl():
    i = np.arange(CHUNK)
    return jnp.asarray((i[:, None] >= i[None, :]).astype(np.float32))


def _t5_bucket(dist):
    max_exact = N_BUCKETS // 2
    d = np.maximum(dist, 1).astype(np.float64)
    large = max_exact + (np.log(d / max_exact) / math.log(MAX_DISTANCE / max_exact)
                         * (N_BUCKETS - max_exact)).astype(np.int32)
    large = np.minimum(large, N_BUCKETS - 1)
    return np.where(dist < max_exact, dist, large).astype(np.int32)


def _bucket_tables():
    qi = np.arange(BLK)[:, None]
    kj = np.arange(2 * BLK)[None, :]
    step = qi - kj + BLK
    return jnp.asarray(np.stack([_t5_bucket(np.clip(step, 0, None) * r) for _, r in PATTERNS]))


def _in_proj(x, norm_w, wt_bf):
    s = x.shape[0]
    tm = 512

    def body(x_ref, nw_ref, w_ref, o_ref):
        xv = x_ref[...]
        rstd = lax.rsqrt(jnp.mean(xv * xv, axis=-1, keepdims=True) + EPS)
        h = (xv * rstd * nw_ref[...]).astype(BF16)
        o_ref[...] = _dot_nt(h, w_ref[...])

    return pl.pallas_call(
        body, grid=(s // tm,),
        in_specs=[pl.BlockSpec((tm, D_MODEL), lambda i: (i, 0)),
                  pl.BlockSpec((1, D_MODEL), lambda i: (0, 0)),
                  pl.BlockSpec((D_IN_PAD, D_MODEL), lambda i: (0, 0))],
        out_specs=pl.BlockSpec((tm, D_IN_PAD), lambda i: (i, 0)),
        out_shape=_sds((s, D_IN_PAD)), compiler_params=_cp(("parallel",)), name="in_proj")(x, norm_w, wt_bf)


def _conv_group(cur, halo, w_ref, c):
    rows = cur.shape[0]
    lanes = slice(128 * c, 128 * c + 128)
    xcat = jnp.concatenate([halo, cur], axis=0)
    y = cur * w_ref[CONV_W - 1:CONV_W, lanes]
    for k in range(1, CONV_W):
        sh = pltpu.roll(xcat, k, 0)[8:8 + rows]
        y = y + sh * w_ref[CONV_W - 1 - k:CONV_W - k, lanes]
    return y


def _beta_g(ba, alog_l, dtb_l):
    lane = lax.broadcasted_iota(jnp.int32, ba.shape, 1)
    sig_b = _sigmoid(ba)
    t = ba + dtb_l
    softplus = jnp.maximum(t, 0.0) + jnp.log(1.0 + jnp.exp(-jnp.abs(t)))
    nega = -jnp.exp(alog_l)
    g = nega * softplus
    out = jnp.where(lane < DN_HEADS, sig_b, jnp.where(lane < 2 * DN_HEADS, g, 0.0))
    return out, lane, sig_b, t, nega, g


def _dn_pre(proj, conv_w8, alog_l, dtb_l):
    s = proj.shape[0]
    tr = 512
    nh = tr // 8

    def body(u_ref, halo_ref, ba_ref, w_ref, al_ref, dt_ref, q_ref, k_ref, v_ref, bg_ref):
        i = pl.program_id(0)
        keep = (i > 0).astype(F32)
        for c in range(12):
            lanes = slice(128 * c, 128 * c + 128)
            y = _conv_group(u_ref[:, lanes], halo_ref[:, lanes] * keep, w_ref, c)
            sv = y * _sigmoid(y)
            if c < 8:
                rs = lax.rsqrt(jnp.sum(sv * sv, axis=1, keepdims=True) + EPS)
                n = sv * rs
                if c < 4:
                    q_ref[:, lanes] = n * (DN_HD ** -0.5)
                else:
                    k_ref[:, slice(128 * (c - 4), 128 * (c - 3))] = n
            else:
                v_ref[:, slice(128 * (c - 8), 128 * (c - 7))] = sv
        bg_ref[...] = _beta_g(ba_ref[...], al_ref[...], dt_ref[...])[0]

    return pl.pallas_call(
        body, grid=(s // tr,),
        in_specs=[pl.BlockSpec((tr, 1536), lambda i: (i, 0)),
                  pl.BlockSpec((8, 1536), lambda i: (jnp.maximum(i * nh - 1, 0), 0)),
                  pl.BlockSpec((tr, 128), lambda i: (i, COL_BA_128)),
                  pl.BlockSpec((8, 1536), lambda i: (0, 0)),
                  pl.BlockSpec((1, 128), lambda i: (0, 0)),
                  pl.BlockSpec((1, 128), lambda i: (0, 0))],
        out_specs=[pl.BlockSpec((tr, 512), lambda i: (i, 0))] * 3 + [pl.BlockSpec((tr, 128), lambda i: (i, 0))],
        out_shape=[_sds((s, 512))] * 3 + [_sds((s, 128))],
        compiler_params=_cp(("parallel",)), name="dn_pre")(proj, proj, proj, conv_w8, alog_l, dtb_l)


CPS = 4
CPS_SCAN = 8


def _split3(a):
    hi = a.astype(BF16)
    r1 = a - hi.astype(F32)
    mid = r1.astype(BF16)
    return hi, mid, (r1 - mid.astype(F32)).astype(BF16)


def _lane_select():
    r = np.arange(128)
    return jnp.asarray((r[None, :, None] == np.arange(8)[:, None, None]) * np.ones((1, 1, 128)), dtype=BF16)


def _lane_bcast(a3, sel):
    return _dot(a3[0], sel) + _dot(a3[1], sel) + _dot(a3[2], sel)


def _rowsum_b(z, ones_b):
    hi, lo = _split(z)
    return _dot(hi, ones_b) + _dot(lo, ones_b)


def _chunk_cumsum(bg, bgt, tri):
    return _split3(bg), _split3(_mmx(tri, bg)), _mmx_nt(bgt, tri)


def _chunk_common(bg3, gc3, gc_row, h, sel_ref):
    gcc = _lane_bcast(gc3, sel_ref[DN_HEADS + h])
    beta = _lane_bcast(bg3, sel_ref[h])
    gcr = gc_row[DN_HEADS + h:DN_HEADS + h + 1, :]
    ii = lax.broadcasted_iota(jnp.int32, (CHUNK, CHUNK), 0)
    jj = lax.broadcasted_iota(jnp.int32, (CHUNK, CHUNK), 1)
    incl = ii >= jj
    strict = ii > jj
    decay = jnp.exp(jnp.where(incl, gcc[:, 0:CHUNK] - gcr, NEG))
    gl = gcc[CHUNK - 1:CHUNK, :]
    return gcc, beta, incl, strict, decay, gl


def _dn_prep(qn, kn, v, bg, bgt, tri, sel):
    s = qn.shape[0]
    nc = s // CHUNK

    def body(q_ref, k_ref, v_ref, bg_ref, bgt_ref, tri_ref, sel_ref,
             u_ref, w_ref, qd_ref, kt_ref, attn_ref, t_ref, gl_ref):
        tri_v = tri_ref[...]
        ii = lax.broadcasted_iota(jnp.int32, (CHUNK, CHUNK), 0)
        jj = lax.broadcasted_iota(jnp.int32, (CHUNK, CHUNK), 1)
        eye = (ii == jj).astype(F32)

        def chain(cc, h, bg3, gc3, gc_row):
            rows = slice(CHUNK * cc, CHUNK * cc + CHUNK)
            lanes = slice(128 * h, 128 * h + 128)
            gcc, beta, incl, strict, decay, gl = _chunk_common(bg3, gc3, gc_row, h, sel_ref)
            yield
            q = q_ref[rows, lanes]
            k = k_ref[rows, lanes]
            vv = v_ref[rows, lanes]
            kb = k * beta
            egc = jnp.exp(gcc)
            a_mat = jnp.where(strict, _mm_nt(kb, k) * decay, 0.0)
            attn_ref[cc, h] = jnp.where(incl, _mm_nt(q, k) * decay, 0.0)
            qd_ref[rows, lanes] = (q * egc).astype(BF16)
            kt_ref[rows, lanes] = (k * jnp.exp(gl - gcc)).astype(BF16)
            gl_ref[cc, h] = jnp.exp(gl)
            yield
            p = -a_mat
            t = eye + p
            for _ in range(5):
                ps = _split(p)
                p = _mm3(ps, ps)
                yield
                t = t + _mm3(_split(t), _split(p))
                yield
            t_ref[cc, h] = t
            ts = _split(t)
            u_ref[rows, lanes] = _mm3(ts, _split(vv * beta))
            w_ref[rows, lanes] = _mm3(ts, _split(kb * egc)).astype(BF16)

        gens = []
        for cc in range(CPS):
            bgv = bg_ref[CHUNK * cc:CHUNK * cc + CHUNK, :]
            bg3, gc3, gc_row = _chunk_cumsum(bgv, bgt_ref[cc], tri_v)
            gens += [chain(cc, h, bg3, gc3, gc_row) for h in range(DN_HEADS)]
        _interleave(gens)

    rows_step = CPS * CHUNK
    big = pl.BlockSpec((rows_step, 512), lambda n: (n, 0))
    sq = pl.BlockSpec((CPS, DN_HEADS, CHUNK, CHUNK), lambda n: (n, 0, 0, 0))
    return pl.pallas_call(
        body, grid=(nc // CPS,),
        in_specs=[big, big, big, pl.BlockSpec((rows_step, 128), lambda n: (n, 0)),
                  pl.BlockSpec((CPS, 8, CHUNK), lambda n: (n, 0, 0)),
                  pl.BlockSpec((CHUNK, CHUNK), lambda n: (0, 0)),
                  pl.BlockSpec((8, 128, 128), lambda n: (0, 0, 0))],
        out_specs=[big, big, big, big, sq, sq, pl.BlockSpec((CPS, DN_HEADS, 1, 128), lambda n: (n, 0, 0, 0))],
        out_shape=[_sds((s, 512))] + [_sds((s, 512), BF16)] * 3 + [_sds((nc, DN_HEADS, CHUNK, CHUNK))] * 2
        + [_sds((nc, DN_HEADS, 1, 128))],
        compiler_params=_cp(("parallel",)), name="dn_prep")(qn, kn, v, bg, bgt, tri, sel)


def _dn_scan(u, w, qd, kt, attn, gl):
    s = u.shape[0]
    nc = s // CHUNK

    def body(u_ref, w_ref, qd_ref, kt_ref, attn_ref, gl_ref, o_ref, vn_ref, sp_ref, st_ref):
        n = pl.program_id(0)

        @pl.when(n == 0)
        def _():
            st_ref[...] = jnp.zeros_like(st_ref)

        def chain(cc, h):
            rows = slice(CHUNK * cc, CHUNK * cc + CHUNK)
            lanes = slice(128 * h, 128 * h + 128)
            st = st_ref[h]
            sp_ref[cc, h] = st
            stb = st.astype(BF16)
            ws = _dot(w_ref[rows, lanes].astype(BF16), stb)
            qs = _dot(qd_ref[rows, lanes].astype(BF16), stb)
            yield
            vn = u_ref[rows, lanes] - ws
            vnb = vn.astype(BF16)
            vn_ref[rows, lanes] = vnb
            o_ref[rows, lanes] = qs + _dot(attn_ref[cc, h].astype(BF16), vnb)
            st_ref[h] = st * gl_ref[cc, h] + _dot_tn(kt_ref[rows, lanes].astype(BF16), vnb)

        for cc in range(CPS_SCAN):
            _interleave([chain(cc, h) for h in range(DN_HEADS)])

    big = pl.BlockSpec((CPS_SCAN * CHUNK, 512), lambda n: (n, 0))
    return pl.pallas_call(
        body, grid=(nc // CPS_SCAN,),
        in_specs=[big, big, big, big,
                  pl.BlockSpec((CPS_SCAN, DN_HEADS, CHUNK, CHUNK), lambda n: (n, 0, 0, 0)),
                  pl.BlockSpec((CPS_SCAN, DN_HEADS, 1, 128), lambda n: (n, 0, 0, 0))],
        out_specs=[big, big, pl.BlockSpec((CPS_SCAN, DN_HEADS, DN_HD, DN_HD), lambda n: (n, 0, 0, 0))],
        out_shape=[_sds((s, 512)), _sds((s, 512), BF16), _sds((nc, DN_HEADS, DN_HD, DN_HD))],
        scratch_shapes=[pltpu.VMEM((DN_HEADS, DN_HD, DN_HD), F32)],
        compiler_params=_cp(("arbitrary",)), name="dn_scan")(u, w, qd, kt, attn, gl)


R4 = PATTERNS[1][1]
R16 = PATTERNS[2][1]
TM = 512


def _pattern_spec(r, width=512):
    return pl.BlockSpec((r, TM // r, width), lambda i: (0, i, 0))


def _pattern_shape(s, r, dtype=F32, width=512):
    return _sds((r, s // r, width), dtype)


SLABS = pltpu.VMEM((4, TM, 128), F32)

HEAD_SLOT = 128 // ATT_HEADS


def _head_expand():
    src = np.arange(128)[:, None]
    dst = np.arange(512)[None, :]
    return jnp.asarray((src == (dst // ATT_HD) * HEAD_SLOT).astype(np.float32), dtype=BF16)


def _head_compact():
    src = np.arange(512)[:, None]
    dst = np.arange(128)[None, :]
    return jnp.asarray((src // ATT_HD == dst // HEAD_SLOT).astype(np.float32), dtype=BF16)


def _to_patterns(val, dsts, scr):
    for c in range(val.shape[1] // 128):
        lanes = slice(128 * c, 128 * c + 128)
        scr[c] = val[:, lanes]
        for dst_ref, r in dsts:
            for a in range(r):
                dst_ref[a, :, lanes] = scr[c, pl.ds(a, TM // r, stride=r), :].astype(dst_ref.dtype)


def _from_pattern(src_ref, r, scr):
    n_slab = src_ref.shape[2] // 128
    for c in range(n_slab):
        for a in range(r):
            scr[c, pl.ds(a, TM // r, stride=r), :] = src_ref[a, :, 128 * c:128 * c + 128].astype(F32)
    return jnp.concatenate([scr[c] for c in range(n_slab)], axis=1) if n_slab > 1 else scr[0]


def _att_pre(proj, qw_t, kw_t, bd64):
    s = proj.shape[0]

    def body(q_ref, k_ref, v_ref, qw_ref, kw_ref, bd_ref,
             q1_ref, k1_ref, v1_ref, q4_ref, k4_ref, v4_ref, q16_ref, k16_ref, v16_ref, scr):
        bd = bd_ref[...]
        q = q_ref[...]
        k = k_ref[...]
        qn = q * _seg_rstd(q, bd, ATT_HD) * qw_ref[...] * (ATT_HD ** -0.5)
        kn = k * _seg_rstd(k, bd, ATT_HD) * kw_ref[...]
        q1_ref[...] = qn.astype(BF16)
        k1_ref[...] = kn.astype(BF16)
        v1_ref[...] = v_ref[...].astype(BF16)
        _to_patterns(qn, ((q4_ref, R4), (q16_ref, R16)), scr)
        _to_patterns(kn, ((k4_ref, R4), (k16_ref, R16)), scr)
        _to_patterns(v_ref[...], ((v4_ref, R4), (v16_ref, R16)), scr)

    row = pl.BlockSpec((1, 512), lambda i: (0, 0))
    tok = pl.BlockSpec((TM, 512), lambda i: (i, 0))
    return pl.pallas_call(
        body, grid=(s // TM,),
        in_specs=[pl.BlockSpec((TM, 512), lambda i: (i, COL_ATT_Q)),
                  pl.BlockSpec((TM, 512), lambda i: (i, COL_ATT_K)),
                  pl.BlockSpec((TM, 512), lambda i: (i, COL_ATT_V)),
                  row, row, pl.BlockSpec((512, 128), lambda i: (0, 0))],
        out_specs=[tok] * 3 + [_pattern_spec(R4)] * 3 + [_pattern_spec(R16)] * 3,
        out_shape=[_sds((s, 512), BF16)] * 3 + [_pattern_shape(s, R4, BF16)] * 3 + [_pattern_shape(s, R16, BF16)] * 3,
        scratch_shapes=[SLABS],
        compiler_params=_cp(("parallel",)), name="att_pre")(proj, proj, proj, qw_t, kw_t, bd64)


def _bias_fwd(rel_bias, buckets):
    def body(rb_ref, bk_ref, o_ref):
        for p in range(len(PATTERNS)):
            bk = bk_ref[p]
            for h in range(ATT_HEADS):
                acc = jnp.zeros((BLK, 2 * BLK), F32)
                for b in range(N_BUCKETS):
                    acc = jnp.where(bk == b, rb_ref[h, b], acc)
                o_ref[p, h] = acc

    return pl.pallas_call(
        body,
        in_specs=[pl.BlockSpec(memory_space=pltpu.SMEM), pl.BlockSpec(memory_space=pltpu.VMEM)],
        out_specs=pl.BlockSpec(memory_space=pltpu.VMEM),
        out_shape=_sds((len(PATTERNS), ATT_HEADS, BLK, 2 * BLK)),
        compiler_params=_cp(), name="bias_fwd")(rel_bias, buckets)


def _bias_bwd(ds_accs, buckets):
    def body(ds0_ref, ds1_ref, ds2_ref, bk_ref, o_ref):
        for h in range(ATT_HEADS):
            for b in range(N_BUCKETS):
                tot = jnp.zeros((), F32)
                for p, ds_ref in enumerate((ds0_ref, ds1_ref, ds2_ref)):
                    tot = tot + jnp.sum(jnp.where(bk_ref[p] == b, ds_ref[h], 0.0))
                o_ref[h, b] = tot

    return pl.pallas_call(
        body,
        in_specs=[pl.BlockSpec(memory_space=pltpu.VMEM)] * 4,
        out_specs=pl.BlockSpec(memory_space=pltpu.SMEM),
        out_shape=_sds((ATT_HEADS, N_BUCKETS)),
        compiler_params=_cp(), name="bias_bwd")(*ds_accs, buckets)


QB_FWD = 2
QB_BWD = 4


def _att_masks(has_prev):
    qi = lax.broadcasted_iota(jnp.int32, (BLK, BLK), 0)
    kj = lax.broadcasted_iota(jnp.int32, (BLK, BLK), 1)
    lane = lax.broadcasted_iota(jnp.int32, (BLK, 2 * ATT_HD), 1)
    return jnp.logical_and(kj >= qi, has_prev), kj <= qi, lane < ATT_HD


def _head_lanes(h):
    half = h % 2
    return slice(ATT_HD * h, ATT_HD * h + ATT_HD), slice(ATT_HD * half, ATT_HD * half + ATT_HD)


def _att_scores(qm, kp2, kc2, bias_h, mask_prev, mask_cur):
    s_prev = jnp.where(mask_prev, _dot_nt(qm, kp2) + bias_h[:, :BLK], NEG)
    s_cur = jnp.where(mask_cur, _dot_nt(qm, kc2) + bias_h[:, BLK:], NEG)
    return s_prev, s_cur


def _att_fwd(q, k, v, v_col, bias, p_idx, r, name):
    QB = QB_FWD
    s = q.shape[0]
    nblk = s // BLK
    nseq = nblk // r

    def body(q_ref, kp_ref, kc_ref, vp_ref, vc_ref, b_ref, o_ref, lse_ref):
        j = pl.program_id(0)

        def head(h, rows, masks, q2, kp2, kc2, vp2, vc2):
            mask_prev, mask_cur, lo_half = masks
            out_l, pair_l = _head_lanes(h)
            sel = lo_half if h % 2 == 0 else jnp.logical_not(lo_half)
            qm = jnp.where(sel, q2, jnp.zeros_like(q2))
            s_prev, s_cur = _att_scores(qm, kp2, kc2, b_ref[0, h], mask_prev, mask_cur)
            yield
            m = jnp.maximum(jnp.max(s_prev, axis=1, keepdims=True), jnp.max(s_cur, axis=1, keepdims=True))
            p_prev = jnp.exp(s_prev - m)
            p_cur = jnp.exp(s_cur - m)
            l = jnp.sum(p_prev, axis=1, keepdims=True) + jnp.sum(p_cur, axis=1, keepdims=True)
            yield
            o2 = _dot(p_prev.astype(BF16), vp2) + _dot(p_cur.astype(BF16), vc2)
            o_ref[rows, out_l] = (o2 * (1.0 / l))[:, pair_l].astype(BF16)
            lse_ref[rows, HEAD_SLOT * h:HEAD_SLOT * h + HEAD_SLOT] = jnp.broadcast_to(m + jnp.log(l), (BLK, HEAD_SLOT))

        for sub in range(QB):
            rows = slice(BLK * sub, BLK * sub + BLK)
            before = slice(BLK * (sub - 1), BLK * sub)
            masks = _att_masks(((QB * j + sub) % nseq) != 0)
            gens = []
            for pp in range(ATT_HEADS // 2):
                lanes = slice(128 * pp, 128 * pp + 128)
                kp = kp_ref[:, lanes] if sub == 0 else kc_ref[before, lanes]
                vp = vp_ref[:, lanes] if sub == 0 else vc_ref[before, lanes]
                slabs = (q_ref[rows, lanes], kp, kc_ref[rows, lanes], vp, vc_ref[rows, lanes])
                gens += [head(2 * pp, rows, masks, *slabs), head(2 * pp + 1, rows, masks, *slabs)]
            _interleave(gens)

    cur = pl.BlockSpec((QB * BLK, 512), lambda j: (j, 0))
    prev = pl.BlockSpec((BLK, 512), lambda j: (jnp.maximum(QB * j - 1, 0), 0))
    vcur = pl.BlockSpec((QB * BLK, 512), lambda j: (j, v_col))
    vprev = pl.BlockSpec((BLK, 512), lambda j: (jnp.maximum(QB * j - 1, 0), v_col))
    return pl.pallas_call(
        body, grid=(nblk // QB,),
        in_specs=[cur, prev, cur, vprev, vcur,
                  pl.BlockSpec((1, ATT_HEADS, BLK, 2 * BLK), lambda j: (p_idx, 0, 0, 0))],
        out_specs=[cur, pl.BlockSpec((QB * BLK, 128), lambda j: (j, 0))],
        out_shape=[_sds((s, 512), BF16), _sds((s, 128))],
        compiler_params=_cp(("parallel",)), name=name)(q, k, k, v, v, bias)


def _post_fwd(o_dn, proj, o_pats, lse_pats, dnw_t, bd128):
    s = o_dn.shape[0]

    def body(o_ref, z_ref, gate_ref, o1_ref, o4_ref, o16_ref, s1_ref, s4_ref, s16_ref, wn_ref, bd_ref, ex_ref,
             mixed_ref, oatt_ref, l1_ref, l4_ref, l16_ref, scr_a, scr_b, scr_c, scr_d):
        o = o_ref[...]
        z = z_ref[...]
        rstd = _seg_rstd(o, bd_ref[...], DN_HD)
        y_dn = o * rstd * wn_ref[...] * (z * _sigmoid(z))
        mixed_ref[:, 0:512] = y_dn.astype(BF16)
        lses = (s1_ref[...], _from_pattern(s4_ref, R4, scr_a), _from_pattern(s16_ref, R16, scr_b))
        m = jnp.maximum(jnp.maximum(lses[0], lses[1]), lses[2])
        tot = jnp.exp(lses[0] - m) + jnp.exp(lses[1] - m) + jnp.exp(lses[2] - m)
        big_l = m + jnp.log(tot)
        l1_ref[...] = big_l
        _to_patterns(big_l, ((l4_ref, R4), (l16_ref, R16)), scr_a)
        ex = ex_ref[...]
        outs = (o1_ref[...], _from_pattern(o4_ref, R4, scr_c), _from_pattern(o16_ref, R16, scr_d))
        acc = jnp.zeros_like(o)
        for lse_p, o_p in zip(lses, outs):
            acc = acc + _lane_bcast(_split3(jnp.exp(lse_p - big_l)), ex) * o_p
        gate = gate_ref[...]
        oatt_ref[...] = acc
        mixed_ref[:, 512:1024] = (acc * (gate * _sigmoid(gate))).astype(BF16)

    blk = pl.BlockSpec((TM, 512), lambda i: (i, 0))
    cblk = pl.BlockSpec((TM, 128), lambda i: (i, 0))
    p4, p16 = _pattern_spec(R4), _pattern_spec(R16)
    c4, c16 = _pattern_spec(R4, 128), _pattern_spec(R16, 128)
    return pl.pallas_call(
        body, grid=(s // TM,),
        in_specs=[blk, pl.BlockSpec((TM, 512), lambda i: (i, COL_Z)),
                  pl.BlockSpec((TM, 512), lambda i: (i, COL_GATE)), blk, p4, p16, cblk, c4, c16,
                  pl.BlockSpec((1, 512), lambda i: (0, 0)), pl.BlockSpec((512, 128), lambda i: (0, 0)),
                  pl.BlockSpec((128, 512), lambda i: (0, 0))],
        out_specs=[pl.BlockSpec((TM, D_MODEL), lambda i: (i, 0)), blk, cblk, c4, c16],
        out_shape=[_sds((s, D_MODEL), BF16), _sds((s, 512)), _sds((s, 128)), _pattern_shape(s, R4, F32, 128),
                   _pattern_shape(s, R16, F32, 128)],
        scratch_shapes=[SLABS] * 4,
        compiler_params=_cp(("parallel",)), name="post_fwd")(o_dn, proj, proj, *o_pats, *lse_pats, dnw_t, bd128,
                                                              _head_expand())


def _out_fwd(x, mixed, w_out_bf, tgt):
    s = x.shape[0]
    tm = 512

    def body(x_ref, m_ref, w_ref, t_ref, dy_ref, loss_ref):
        i = pl.program_id(0)

        @pl.when(i == 0)
        def _():
            loss_ref[...] = jnp.zeros_like(loss_ref)

        y = x_ref[...] + jnp.dot(m_ref[...], w_ref[...], preferred_element_type=F32)
        err = y - t_ref[...]
        dy_ref[...] = err * (1.0 / D_MODEL)
        part = 0.5 * jnp.sum(jnp.mean(err * err, axis=-1, keepdims=True), axis=0, keepdims=True)
        loss_ref[...] = loss_ref[...] + part

    blk = pl.BlockSpec((tm, D_MODEL), lambda i: (i, 0))
    return pl.pallas_call(
        body, grid=(s // tm,),
        in_specs=[blk, blk, pl.BlockSpec((D_MODEL, D_MODEL), lambda i: (0, 0)), blk],
        out_specs=[blk, pl.BlockSpec((8, 128), lambda i: (0, 0))],
        out_shape=[_sds((s, D_MODEL)), _sds((8, 128))],
        compiler_params=_cp(("arbitrary",)), name="out_fwd")(x, mixed, w_out_bf, tgt)


def _out_bwd(dy, mixed, w_out_bf):
    s = dy.shape[0]
    tm = 512

    def body(dy_ref, m_ref, w_ref, dm_ref, dw_ref):
        i = pl.program_id(0)

        @pl.when(i == 0)
        def _():
            dw_ref[...] = jnp.zeros_like(dw_ref)

        dyb = dy_ref[...].astype(BF16)
        dm_ref[...] = lax.dot_general(dyb, w_ref[...], (((1,), (1,)), ((), ())), preferred_element_type=F32)
        dw_ref[...] = dw_ref[...] + lax.dot_general(m_ref[...], dyb, (((0,), (0,)), ((), ())),
                                                    preferred_element_type=F32)

    blk = pl.BlockSpec((tm, D_MODEL), lambda i: (i, 0))
    full = pl.BlockSpec((D_MODEL, D_MODEL), lambda i: (0, 0))
    return pl.pallas_call(
        body, grid=(s // tm,), in_specs=[blk, blk, full], out_specs=[blk, full],
        out_shape=[_sds((s, D_MODEL)), _sds((D_MODEL, D_MODEL))],
        compiler_params=_cp(("arbitrary",)), name="out_bwd")(dy, mixed, w_out_bf)


def _post_bwd(dmixed, o_dn, proj, o_att, dnw_t, bd128):
    s = o_dn.shape[0]
    tm = TM

    def body(ddn_ref, datt_ref, o_ref, z_ref, gate_ref, oatt_ref, wn_ref, bd128_ref, hc_ref,
             do_ref, dz_ref, dgate_ref, doatt_ref, do4_ref, do16_ref, delta_ref, dl4_ref, dl16_ref, dnw_ref, scr):
        i = pl.program_id(0)

        @pl.when(i == 0)
        def _():
            dnw_ref[...] = jnp.zeros_like(dnw_ref)

        bd128v = bd128_ref[...]
        o = o_ref[...]
        z = z_ref[...]
        wn = wn_ref[...]
        dy = ddn_ref[...]
        rstd = _seg_rstd(o, bd128v, DN_HD)
        nrm = o * rstd
        sz = _sigmoid(z)
        dz_ref[...] = (dy * nrm * wn * _silu_grad(z, sz)).astype(BF16)
        dn = dy * z * sz
        gw = dn * wn
        do_ref[...] = rstd * (gw - nrm * _seg_expand(_seg_mean(gw * nrm, bd128v, DN_HD), bd128v, DN_HD))
        colsum = jnp.sum(dn * nrm, axis=0, keepdims=True)
        fold = colsum[:, 0:128] + colsum[:, 128:256] + colsum[:, 256:384] + colsum[:, 384:512]
        dnw_ref[...] = dnw_ref[...] + fold
        dya = datt_ref[...]
        gate = gate_ref[...]
        oatt = oatt_ref[...]
        sg = _sigmoid(gate)
        dgate_ref[...] = (dya * oatt * _silu_grad(gate, sg)).astype(BF16)
        doa = dya * gate * sg
        doatt_ref[...] = doa.astype(BF16)
        delta = _segsum(doa * oatt, hc_ref[...])
        delta_ref[...] = delta
        _to_patterns(doa, ((do4_ref, R4), (do16_ref, R16)), scr)
        _to_patterns(delta, ((dl4_ref, R4), (dl16_ref, R16)), scr)

    blk = pl.BlockSpec((tm, 512), lambda i: (i, 0))
    cblk = pl.BlockSpec((tm, 128), lambda i: (i, 0))
    p4, p16 = _pattern_spec(R4), _pattern_spec(R16)
    c4, c16 = _pattern_spec(R4, 128), _pattern_spec(R16, 128)
    return pl.pallas_call(
        body, grid=(s // tm,),
        in_specs=[blk, pl.BlockSpec((tm, 512), lambda i: (i, 1)), blk,
                  pl.BlockSpec((tm, 512), lambda i: (i, COL_Z)), pl.BlockSpec((tm, 512), lambda i: (i, COL_GATE)),
                  blk, pl.BlockSpec((1, 512), lambda i: (0, 0)), pl.BlockSpec((512, 128), lambda i: (0, 0)),
                  pl.BlockSpec((512, 128), lambda i: (0, 0))],
        out_specs=[blk, blk, blk, blk, p4, p16, cblk, c4, c16, pl.BlockSpec((8, 128), lambda i: (0, 0))],
        out_shape=[_sds((s, 512))] + [_sds((s, 512), BF16)] * 3 + [_pattern_shape(s, R4, BF16),
                                          _pattern_shape(s, R16, BF16), _sds((s, 128)),
                                          _pattern_shape(s, R4, F32, 128), _pattern_shape(s, R16, F32, 128),
                                          _sds((8, 128))],
        scratch_shapes=[SLABS],
        compiler_params=_cp(("arbitrary",)), name="post_bwd")(dmixed, dmixed, o_dn, proj, proj, o_att, dnw_t,
                                                               bd128, _head_compact())


def _att_bwd(q, k, v, v_col, do, big_l, delta, bias, p_idx, r, name):
    QB = QB_BWD
    s = q.shape[0]
    nblk = s // BLK
    nseq = nblk // r
    nstep = nblk // QB

    def body(q_ref, kp_ref, kc_ref, vp_ref, vc_ref, do_ref, l_ref, dl_ref, b_ref,
             dq_ref, dk_ref, dv_ref, ds_ref, dkc_ref, dvc_ref):
        j = pl.program_id(0)

        @pl.when(j == 0)
        def _():
            dkc_ref[...] = jnp.zeros_like(dkc_ref)
            dvc_ref[...] = jnp.zeros_like(dvc_ref)
            ds_ref[...] = jnp.zeros_like(ds_ref)

        @pl.when(j < nstep)
        def _():
            def head(h, sub, masks, q2, do2, kp2, kc2, vp2, vc2):
                mask_prev, mask_cur, lo_half = masks
                rows = slice(BLK * sub, BLK * sub + BLK)
                out_l, pair_l = _head_lanes(h)
                sel = lo_half if h % 2 == 0 else jnp.logical_not(lo_half)
                qm = jnp.where(sel, q2, jnp.zeros_like(q2))
                dom = jnp.where(sel, do2, jnp.zeros_like(do2))
                s_prev, s_cur = _att_scores(qm, kp2, kc2, b_ref[0, h], mask_prev, mask_cur)
                dp_prev = _dot_nt(dom, vp2)
                dp_cur = _dot_nt(dom, vc2)
                yield
                lh = l_ref[rows, HEAD_SLOT * h:HEAD_SLOT * h + 1]
                dh = dl_ref[rows, HEAD_SLOT * h:HEAD_SLOT * h + 1]
                p_prev = jnp.exp(s_prev - lh)
                p_cur = jnp.exp(s_cur - lh)
                ds_prev = p_prev * (dp_prev - dh)
                ds_cur = p_cur * (dp_cur - dh)
                ds_ref[h, :, 0:BLK] = ds_ref[h, :, 0:BLK] + ds_prev
                ds_ref[h, :, BLK:2 * BLK] = ds_ref[h, :, BLK:2 * BLK] + ds_cur
                dsb_prev, dsb_cur = ds_prev.astype(BF16), ds_cur.astype(BF16)
                pb_prev, pb_cur = p_prev.astype(BF16), p_cur.astype(BF16)
                yield
                dq_ref[rows, out_l] = (_dot(dsb_prev, kp2) + _dot(dsb_cur, kc2))[:, pair_l].astype(BF16)
                dk_prev = _dot_tn(dsb_prev, q2)[:, pair_l]
                dv_prev = _dot_tn(pb_prev, do2)[:, pair_l]
                if sub == 0:
                    last = slice(BLK * (QB - 1), BLK * QB)
                    dk_ref[last, out_l] = (dkc_ref[last, out_l] + dk_prev).astype(BF16)
                    dv_ref[last, out_l] = (dvc_ref[last, out_l] + dv_prev).astype(BF16)
                else:
                    before = slice(BLK * (sub - 1), BLK * sub)
                    dkc_ref[before, out_l] = dkc_ref[before, out_l] + dk_prev
                    dvc_ref[before, out_l] = dvc_ref[before, out_l] + dv_prev
                yield
                dkc_ref[rows, out_l] = _dot_tn(dsb_cur, q2)[:, pair_l]
                dvc_ref[rows, out_l] = _dot_tn(pb_cur, do2)[:, pair_l]

            done = slice(0, BLK * (QB - 1))
            dk_ref[done, :] = dkc_ref[done, :].astype(BF16)
            dv_ref[done, :] = dvc_ref[done, :].astype(BF16)
            for sub in range(QB):
                rows = slice(BLK * sub, BLK * sub + BLK)
                before = slice(BLK * (sub - 1), BLK * sub)
                masks = _att_masks(((QB * j + sub) % nseq) != 0)
                gens = []
                for pp in range(ATT_HEADS // 2):
                    lanes = slice(128 * pp, 128 * pp + 128)
                    kp = kp_ref[:, lanes] if sub == 0 else kc_ref[before, lanes]
                    vp = vp_ref[:, lanes] if sub == 0 else vc_ref[before, lanes]
                    slabs = (q_ref[rows, lanes], do_ref[rows, lanes], kp, kc_ref[rows, lanes], vp,
                             vc_ref[rows, lanes])
                    gens += [head(2 * pp, sub, masks, *slabs), head(2 * pp + 1, sub, masks, *slabs)]
                _interleave(gens)

        @pl.when(j == nstep)
        def _():
            dk_ref[...] = dkc_ref[...].astype(BF16)
            dv_ref[...] = dvc_ref[...].astype(BF16)

    last_step = nstep - 1
    cur = pl.BlockSpec((QB * BLK, 512), lambda j: (jnp.minimum(j, last_step), 0))
    compact = pl.BlockSpec((QB * BLK, 128), lambda j: (jnp.minimum(j, last_step), 0))
    lag = pl.BlockSpec((QB * BLK, 512), lambda j: (jnp.clip(j - 1, 0, last_step), 0))
    prev = pl.BlockSpec((BLK, 512), lambda j: (jnp.clip(QB * j - 1, 0, nblk - 1), 0))
    vcur = pl.BlockSpec((QB * BLK, 512), lambda j: (jnp.minimum(j, last_step), v_col))
    vprev = pl.BlockSpec((BLK, 512), lambda j: (jnp.clip(QB * j - 1, 0, nblk - 1), v_col))
    return pl.pallas_call(
        body, grid=(nstep + 1,),
        in_specs=[cur, prev, cur, vprev, vcur, cur, compact, compact,
                  pl.BlockSpec((1, ATT_HEADS, BLK, 2 * BLK), lambda j: (p_idx, 0, 0, 0))],
        out_specs=[cur, lag, lag, pl.BlockSpec((ATT_HEADS, BLK, 2 * BLK), lambda j: (0, 0, 0))],
        out_shape=[_sds((s, 512), BF16)] * 3 + [_sds((ATT_HEADS, BLK, 2 * BLK))],
        scratch_shapes=[pltpu.VMEM((QB * BLK, 512), F32), pltpu.VMEM((QB * BLK, 512), F32)],
        compiler_params=_cp(("arbitrary",)), name=name)(q, k, k, v, v, do, big_l, delta, bias)


def _att_pre_bwd(dq_pats, dk_pats, dv_pats, proj, qw_t, kw_t, bd64):
    s = proj.shape[0]
    tm = TM

    def body(dq1_ref, dq4_ref, dq16_ref, dk1_ref, dk4_ref, dk16_ref, dv1_ref, dv4_ref, dv16_ref,
             q_ref, k_ref, qw_ref, kw_ref, bd_ref,
             dqr_ref, dkr_ref, dvr_ref, dqw_ref, dkw_ref, scr4, scr16):
        i = pl.program_id(0)

        @pl.when(i == 0)
        def _():
            dqw_ref[...] = jnp.zeros_like(dqw_ref)
            dkw_ref[...] = jnp.zeros_like(dkw_ref)

        bd = bd_ref[...]

        def total(d1_ref, d4_ref, d16_ref):
            return d1_ref[...] + _from_pattern(d4_ref, R4, scr4) + _from_pattern(d16_ref, R16, scr16)

        def one(d_refs, x_ref, w_ref, scale, dx_ref, dw_ref):
            dy = total(*d_refs) * scale
            x = x_ref[...]
            rstd = _seg_rstd(x, bd, ATT_HD)
            nrm = x * rstd
            dw_ref[...] = dw_ref[...] + jnp.sum(dy * nrm, axis=0, keepdims=True)
            g = dy * w_ref[...]
            dx_ref[...] = (rstd * (g - nrm * _seg_expand(_seg_mean(g * nrm, bd, ATT_HD), bd, ATT_HD))).astype(BF16)

        one((dq1_ref, dq4_ref, dq16_ref), q_ref, qw_ref, ATT_HD ** -0.5, dqr_ref, dqw_ref)
        one((dk1_ref, dk4_ref, dk16_ref), k_ref, kw_ref, 1.0, dkr_ref, dkw_ref)
        dvr_ref[...] = total(dv1_ref, dv4_ref, dv16_ref).astype(BF16)

    blk = pl.BlockSpec((tm, 512), lambda i: (i, 0))
    pats = [blk, _pattern_spec(R4), _pattern_spec(R16)]
    row = pl.BlockSpec((1, 512), lambda i: (0, 0))
    acc = pl.BlockSpec((8, 512), lambda i: (0, 0))
    return pl.pallas_call(
        body, grid=(s // tm,),
        in_specs=pats * 3 + [pl.BlockSpec((tm, 512), lambda i: (i, COL_ATT_Q)),
                             pl.BlockSpec((tm, 512), lambda i: (i, COL_ATT_K)), row, row,
                             pl.BlockSpec((512, 128), lambda i: (0, 0))],
        out_specs=[blk, blk, blk, acc, acc],
        out_shape=[_sds((s, 512), BF16)] * 3 + [_sds((8, 512))] * 2,
        scratch_shapes=[SLABS] * 2,
        compiler_params=_cp(("arbitrary",)), name="att_pre_bwd")(*dq_pats, *dk_pats, *dv_pats, proj, proj,
                                                                  qw_t, kw_t, bd64)


def _dn_scan_bwd(do, sp, qd, kt, w, vn, attn, gl):
    s = do.shape[0]
    nc = s // CHUNK

    def body(do_ref, sp_ref, qd_ref, kt_ref, w_ref, vn_ref, attn_ref, gl_ref,
             du_ref, dqd_ref, dkt_ref, dw_ref, dattn_ref, dgl_ref, ds_ref):
        n = pl.program_id(0)

        @pl.when(n == 0)
        def _():
            ds_ref[...] = jnp.zeros_like(ds_ref)

        def chain(cc, h):
            rows = slice(CHUNK * cc, CHUNK * cc + CHUNK)
            lanes = slice(128 * h, 128 * h + 128)
            dsn = ds_ref[h]
            st = sp_ref[cc, h]
            dsb, stb = dsn.astype(BF16), st.astype(BF16)
            dob = do_ref[rows, lanes].astype(BF16)
            vnb = vn_ref[rows, lanes].astype(BF16)
            dvn = _dot_tn(attn_ref[cc, h].astype(BF16), dob) + _dot(kt_ref[rows, lanes].astype(BF16), dsb)
            du_ref[rows, lanes] = dvn
            dqd_ref[rows, lanes] = _dot_nt(dob, stb)
            dattn_ref[cc, h] = _dot_nt(dob, vnb)
            dkt_ref[rows, lanes] = _dot_nt(vnb, dsb)
            tot = jnp.sum(jnp.sum(st * dsn, axis=1, keepdims=True), axis=0, keepdims=True)
            dgl_ref[cc, h] = jnp.broadcast_to(tot, (1, 128))
            qdo = _dot_tn(qd_ref[rows, lanes].astype(BF16), dob)
            yield
            dvb = dvn.astype(BF16)
            dw_ref[rows, lanes] = -_dot_nt(dvb, stb)
            ds_ref[h] = qdo + dsn * gl_ref[cc, h] - _dot_tn(w_ref[rows, lanes].astype(BF16), dvb)

        for cc in reversed(range(CPS_SCAN)):
            _interleave([chain(cc, h) for h in range(DN_HEADS)])

    nsteps = nc // CPS_SCAN
    big = pl.BlockSpec((CPS_SCAN * CHUNK, 512), lambda n: (nsteps - 1 - n, 0))
    sq = pl.BlockSpec((CPS_SCAN, DN_HEADS, CHUNK, CHUNK), lambda n: (nsteps - 1 - n, 0, 0, 0))
    glb = pl.BlockSpec((CPS_SCAN, DN_HEADS, 1, 128), lambda n: (nsteps - 1 - n, 0, 0, 0))
    return pl.pallas_call(
        body, grid=(nsteps,),
        in_specs=[big, pl.BlockSpec((CPS_SCAN, DN_HEADS, DN_HD, DN_HD), lambda n: (nsteps - 1 - n, 0, 0, 0)),
                  big, big, big, big, sq, glb],
        out_specs=[big, big, big, big, sq, glb],
        out_shape=[_sds((s, 512))] * 4 + [_sds((nc, DN_HEADS, CHUNK, CHUNK)), _sds((nc, DN_HEADS, 1, 128))],
        scratch_shapes=[pltpu.VMEM((DN_HEADS, DN_HD, DN_HD), F32)],
        compiler_params=_cp(("arbitrary",)), name="dn_scan_bwd")(do, sp, qd, kt, w, vn, attn, gl)


def _dn_prep_bwd(qn, kn, v, bg, bgt, tri, sel, t_inv, attn, u, w, du, dw, dqd, dkt, dattn, dgl):
    s = qn.shape[0]
    nc = s // CHUNK

    def body(q_ref, k_ref, v_ref, bg_ref, bgt_ref, tri_ref, sel_ref, t_ref, attn_ref, u_ref, w_ref,
             du_ref, dw_ref, dqd_ref, dkt_ref, dattn_ref, dgl_ref,
             dq_ref, dk_ref, dv_ref, dbg_ref):
        tri_v = tri_ref[...]
        lane = lax.broadcasted_iota(jnp.int32, (CHUNK, 128), 1)
        rowi = lax.broadcasted_iota(jnp.int32, (CHUNK, 128), 0)
        ones_b = jnp.ones((CHUNK, 128), BF16)
        ones_sq = jnp.ones((128, 128), BF16)
        parts = [[] for _ in range(CPS)]

        def chain(cc, h, bg3, gc3, gc_row):
            rows = slice(CHUNK * cc, CHUNK * cc + CHUNK)
            lanes = slice(128 * h, 128 * h + 128)
            gcc, beta, incl, strict, decay, gl = _chunk_common(bg3, gc3, gc_row, h, sel_ref)
            yield
            q = q_ref[rows, lanes]
            k = k_ref[rows, lanes]
            vv = v_ref[rows, lanes]
            ts = _split(t_ref[cc, h])
            egc = jnp.exp(gcc)
            kb = k * beta
            a_mat = jnp.where(strict, _mm_nt(kb, k) * decay, 0.0)
            dvb = _mm3_tn(ts, _split(du_ref[rows, lanes]))
            dkbg = _mm3_tn(ts, _split(dw_ref[rows, lanes]))
            yield
            d_a = jnp.where(strict, -(_mm_nt(dvb, u_ref[rows, lanes]) + _mm_nt(dkbg, w_ref[rows, lanes])), 0.0)
            d_m = d_a * decay
            dattn_m = jnp.where(incl, dattn_ref[cc, h], 0.0)
            dqk = dattn_m * decay
            e_hi, e_lo = _split(d_a * a_mat + dattn_m * attn_ref[cc, h])
            yield
            dkb = _mm(d_m, k)
            dk = _mm_tn(d_m, kb) + _mm_tn(dqk, q)
            dq = _mm(dqk, k)
            e_colsum = _dot_tn(e_hi, ones_b) + _dot_tn(e_lo, ones_b)
            e_rowsum = _dot(e_hi, ones_b) + _dot(e_lo, ones_b)
            dqd = dqd_ref[rows, lanes]
            dkt = dkt_ref[rows, lanes]
            s_dqd = _rowsum_b(dqd * q, ones_sq)
            s_dkt = _rowsum_b(dkt * k, ones_sq)
            rk = _rowsum_b(dkbg * k, ones_sq)
            s_dkb = _rowsum_b(dkb * k, ones_sq)
            s_dvb = _rowsum_b(dvb * vv, ones_sq)
            yield
            tail = jnp.exp(gl - gcc)
            r = s_dkt * tail
            dgl_tot = jnp.sum(r, axis=0, keepdims=True) + dgl_ref[cc, h] * jnp.exp(gl)
            dgc = e_rowsum - e_colsum + s_dqd * egc - r + rk * beta * egc
            dgc = dgc + jnp.where(rowi == CHUNK - 1, dgl_tot, 0.0)
            dq_ref[rows, lanes] = dq + dqd * egc
            dk_ref[rows, lanes] = dk + dkt * tail + dkbg * (beta * egc) + dkb * beta
            dv_ref[rows, lanes] = dvb * beta
            parts[cc].append((h, dgc, rk * egc + s_dkb + s_dvb))

        gens = []
        for cc in range(CPS):
            bgv = bg_ref[CHUNK * cc:CHUNK * cc + CHUNK, :]
            bg3, gc3, gc_row = _chunk_cumsum(bgv, bgt_ref[cc], tri_v)
            gens += [chain(cc, h, bg3, gc3, gc_row) for h in range(DN_HEADS)]
        _interleave(gens)
        for cc in range(CPS):
            dgc_mat = jnp.zeros((CHUNK, 128), F32)
            dbeta_mat = jnp.zeros((CHUNK, 128), F32)
            for h, dgc, dbeta in parts[cc]:
                dgc_mat = dgc_mat + jnp.where(lane == DN_HEADS + h, dgc, 0.0)
                dbeta_mat = dbeta_mat + jnp.where(lane == h, dbeta, 0.0)
            dbg_ref[CHUNK * cc:CHUNK * cc + CHUNK, :] = _mmx_tn(tri_v, dgc_mat) + dbeta_mat

    big = pl.BlockSpec((CPS * CHUNK, 512), lambda n: (n, 0))
    sq = pl.BlockSpec((CPS, DN_HEADS, CHUNK, CHUNK), lambda n: (n, 0, 0, 0))
    glb = pl.BlockSpec((CPS, DN_HEADS, 1, 128), lambda n: (n, 0, 0, 0))
    small = pl.BlockSpec((CPS * CHUNK, 128), lambda n: (n, 0))
    return pl.pallas_call(
        body, grid=(nc // CPS,),
        in_specs=[big, big, big, small, pl.BlockSpec((CPS, 8, CHUNK), lambda n: (n, 0, 0)),
                  pl.BlockSpec((CHUNK, CHUNK), lambda n: (0, 0)),
                  pl.BlockSpec((8, 128, 128), lambda n: (0, 0, 0)), sq, sq, big, big,
                  big, big, big, big, sq, glb],
        out_specs=[big, big, big, small],
        out_shape=[_sds((s, 512))] * 3 + [_sds((s, 128))],
        compiler_params=_cp(("parallel",)), name="dn_prep_bwd")(qn, kn, v, bg, bgt, tri, sel, t_inv, attn, u, w,
                                                                  du, dw, dqd, dkt, dattn, dgl)


def _dn_pre_bwd(dqn, dkn, dv, dbg, proj, conv_w8, alog_l, dtb_l):
    s = proj.shape[0]
    tr = 512
    nh = tr // 8

    def body(dq_ref, dk_ref, dv_ref, dbg_ref, u_ref, halo_ref, ba_ref, w_ref, al_ref, dt_ref,
             dy_ref, dba_ref, dsm_ref):
        i = pl.program_id(0)

        @pl.when(i == 0)
        def _():
            dsm_ref[...] = jnp.zeros_like(dsm_ref)

        keep = (i > 0).astype(F32)
        for c in range(12):
            lanes = slice(128 * c, 128 * c + 128)
            y = _conv_group(u_ref[:, lanes], halo_ref[:, lanes] * keep, w_ref, c)
            sg = _sigmoid(y)
            sv = y * sg
            if c < 8:
                rs = lax.rsqrt(jnp.sum(sv * sv, axis=1, keepdims=True) + EPS)
                n = sv * rs
                if c < 4:
                    dn = dq_ref[:, lanes] * (DN_HD ** -0.5)
                else:
                    dn = dk_ref[:, slice(128 * (c - 4), 128 * (c - 3))]
                dsv = rs * (dn - n * jnp.sum(dn * n, axis=1, keepdims=True))
            else:
                dsv = dv_ref[:, slice(128 * (c - 8), 128 * (c - 7))]
            dy_ref[:, lanes] = dsv * _silu_grad(y, sg)
        _, lane, sig_b, t, nega, g = _beta_g(ba_ref[...], al_ref[...], dt_ref[...])
        dbg = dbg_ref[...]
        da = dbg * nega * _sigmoid(t)
        is_b = lane < DN_HEADS
        is_a = jnp.logical_and(lane >= DN_HEADS, lane < 2 * DN_HEADS)
        dba_ref[...] = jnp.where(is_b, dbg * sig_b * (1.0 - sig_b), jnp.where(is_a, da, 0.0)).astype(BF16)
        d_alog = jnp.sum(jnp.where(is_a, dbg * g, 0.0), axis=0, keepdims=True)
        d_dtb = jnp.sum(jnp.where(is_a, da, 0.0), axis=0, keepdims=True)
        row = lax.broadcasted_iota(jnp.int32, (8, 128), 0)
        dsm_ref[...] = dsm_ref[...] + jnp.where(row == 0, d_alog, jnp.where(row == 1, d_dtb, 0.0))

    blk = pl.BlockSpec((tr, 512), lambda i: (i, 0))
    return pl.pallas_call(
        body, grid=(s // tr,),
        in_specs=[blk, blk, blk, pl.BlockSpec((tr, 128), lambda i: (i, 0)),
                  pl.BlockSpec((tr, 1536), lambda i: (i, 0)),
                  pl.BlockSpec((8, 1536), lambda i: (jnp.maximum(i * nh - 1, 0), 0)),
                  pl.BlockSpec((tr, 128), lambda i: (i, COL_BA_128)),
                  pl.BlockSpec((8, 1536), lambda i: (0, 0)),
                  pl.BlockSpec((1, 128), lambda i: (0, 0)), pl.BlockSpec((1, 128), lambda i: (0, 0))],
        out_specs=[pl.BlockSpec((tr, 1536), lambda i: (i, 0)), pl.BlockSpec((tr, 128), lambda i: (i, 0)),
                   pl.BlockSpec((8, 128), lambda i: (0, 0))],
        out_shape=[_sds((s, 1536)), _sds((s, 128), BF16), _sds((8, 128))],
        compiler_params=_cp(("arbitrary",)), name="dn_pre_bwd")(dqn, dkn, dv, dbg, proj, proj, proj, conv_w8,
                                                                 alog_l, dtb_l)


def _conv_bwd(dy, proj, conv_w8):
    s = dy.shape[0]
    tr = 512
    nh = tr // 8
    nblk = s // tr

    def body(dy_ref, dyn_ref, u_ref, halo_ref, w_ref, du_ref, dw_ref):
        i = pl.program_id(0)

        @pl.when(i == 0)
        def _():
            dw_ref[...] = jnp.zeros_like(dw_ref)

        keep_prev = (i > 0).astype(F32)
        keep_next = (i < nblk - 1).astype(F32)
        row = lax.broadcasted_iota(jnp.int32, (8, 128), 0)
        for c in range(12):
            lanes = slice(128 * c, 128 * c + 128)
            dyc = dy_ref[:, lanes]
            dcat = jnp.concatenate([dyc, dyn_ref[:, lanes] * keep_next], axis=0)
            xcat = jnp.concatenate([halo_ref[:, lanes] * keep_prev, u_ref[:, lanes]], axis=0)
            du = dyc * w_ref[CONV_W - 1:CONV_W, lanes]
            dwc = jnp.where(row == CONV_W - 1, jnp.sum(dyc * u_ref[:, lanes], axis=0, keepdims=True), 0.0)
            for k in range(1, CONV_W):
                du = du + pltpu.roll(dcat, tr + 8 - k, 0)[0:tr] * w_ref[CONV_W - 1 - k:CONV_W - k, lanes]
                ush = pltpu.roll(xcat, k, 0)[8:8 + tr]
                dwc = dwc + jnp.where(row == CONV_W - 1 - k, jnp.sum(dyc * ush, axis=0, keepdims=True), 0.0)
            du_ref[:, lanes] = du.astype(BF16)
            dw_ref[:, lanes] = dw_ref[:, lanes] + dwc

    return pl.pallas_call(
        body, grid=(nblk,),
        in_specs=[pl.BlockSpec((tr, 1536), lambda i: (i, 0)),
                  pl.BlockSpec((8, 1536), lambda i: (jnp.minimum((i + 1) * nh, s // 8 - 1), 0)),
                  pl.BlockSpec((tr, 1536), lambda i: (i, 0)),
                  pl.BlockSpec((8, 1536), lambda i: (jnp.maximum(i * nh - 1, 0), 0)),
                  pl.BlockSpec((8, 1536), lambda i: (0, 0))],
        out_specs=[pl.BlockSpec((tr, 1536), lambda i: (i, 0)), pl.BlockSpec((8, 1536), lambda i: (0, 0))],
        out_shape=[_sds((s, 1536), BF16), _sds((8, 1536))],
        compiler_params=_cp(("arbitrary",)), name="conv_bwd")(dy, dy, proj, proj, conv_w8)


PIECE_WIDTHS = (1536, 512, 512, 512, 512, 512, 128)


def _in_bwd_dx(pieces, w_bf, x, dy, norm_w, chip_sums=None):
    s = x.shape[0]
    tm = 512
    n_piece = len(PIECE_WIDTHS)
    n_step = s // tm
    n_sent = 0 if chip_sums is None else 3 * len(chip_sums)

    def body(*refs):
        piece_refs = refs[:n_piece]
        w_ref, x_ref, dy_ref, nw_ref = refs[n_piece:n_piece + 4]
        rest = refs[n_piece + 4:]
        i = pl.program_id(0)

        def copies():
            hb_refs = rest[:n_sent // 3]
            recv_refs = rest[n_sent // 3 + 2:2 * (n_sent // 3) + 2]
            send_sems, recv_sems = rest[-2:]
            xi, yi, ci = _position()
            out = []
            for j, (px, py) in enumerate(_other_chips(xi, yi)):
                for t, (src, dst) in enumerate(zip(hb_refs, recv_refs)):
                    k = len(hb_refs) * j + t
                    out.append(pltpu.make_async_remote_copy(
                        src_ref=src.at[2 * px + py], dst_ref=dst.at[j], send_sem=send_sems.at[k],
                        recv_sem=recv_sems.at[k], device_id=(px, py, ci), device_id_type=MESH_ID))
            return out

        dx_ref, dnw_ref = rest[n_sent // 3:n_sent // 3 + 2]

        @pl.when(i == 0)
        def _():
            dnw_ref[...] = jnp.zeros_like(dnw_ref)
            for cp in (copies() if n_sent else []):
                cp.start()

        dp = jnp.concatenate([r[...] for r in piece_refs], axis=1)
        dh = _dot(dp, w_ref[...])
        xv = x_ref[...]
        rstd = lax.rsqrt(jnp.mean(xv * xv, axis=-1, keepdims=True) + EPS)
        xh = xv * rstd
        dnw_ref[...] = dnw_ref[...] + jnp.sum(dh * xh, axis=0, keepdims=True)
        g = dh * nw_ref[...]
        dx_ref[...] = rstd * (g - xh * jnp.mean(g * xh, axis=-1, keepdims=True)) + dy_ref[...]

        if n_sent:
            @pl.when(i == n_step - 1)
            def _():
                cps = copies()
                for cp in cps:
                    cp.wait_recv()
                for cp in cps:
                    cp.wait_send()

    def blk(n):
        return pl.BlockSpec((tm, n), lambda i: (i, 0))

    in_specs = [blk(n) for n in PIECE_WIDTHS] + [pl.BlockSpec((D_IN_PAD, D_MODEL), lambda i: (0, 0)), blk(D_MODEL),
                                                blk(D_MODEL), pl.BlockSpec((1, D_MODEL), lambda i: (0, 0))]
    out_specs = [blk(D_MODEL), pl.BlockSpec((8, D_MODEL), lambda i: (0, 0))]
    out_shape = [_sds((s, D_MODEL)), _sds((8, D_MODEL))]
    scratch = []
    extra = ()
    if n_sent:
        extra = tuple(chip_sums)
        in_specs += [ANY] * len(extra)
        out_specs += [ANY] * len(extra)
        out_shape += [_sds((3,) + a.shape[1:], a.dtype) for a in extra]
        scratch = [pltpu.SemaphoreType.DMA((n_sent,)), pltpu.SemaphoreType.DMA((n_sent,))]
    return pl.pallas_call(
        body, grid=(n_step,), in_specs=in_specs, out_specs=out_specs, out_shape=out_shape, scratch_shapes=scratch,
        compiler_params=_cp(("arbitrary",)), name="in_bwd_dx")(*pieces, w_bf, x, dy, norm_w, *extra)


def _in_bwd_dw(pieces, x, norm_w):
    s = x.shape[0]
    tm = 512
    n_piece = len(PIECE_WIDTHS)

    def body(*refs):
        piece_refs = refs[:n_piece]
        x_ref, nw_ref, dw_ref = refs[n_piece:]
        i = pl.program_id(0)

        @pl.when(i == 0)
        def _():
            dw_ref[...] = jnp.zeros_like(dw_ref)

        xv = x_ref[...]
        rstd = lax.rsqrt(jnp.mean(xv * xv, axis=-1, keepdims=True) + EPS)
        h = (xv * rstd * nw_ref[...]).astype(BF16)
        at = 0
        for r, width in zip(piece_refs, PIECE_WIDTHS):
            dw_ref[at:at + width, :] = dw_ref[at:at + width, :] + _dot_tn(r[...], h)
            at += width

    return pl.pallas_call(
        body, grid=(s // tm,),
        in_specs=[pl.BlockSpec((tm, n), lambda i: (i, 0)) for n in PIECE_WIDTHS]
        + [pl.BlockSpec((tm, D_MODEL), lambda i: (i, 0)), pl.BlockSpec((1, D_MODEL), lambda i: (0, 0))],
        out_specs=pl.BlockSpec((D_IN_PAD, D_MODEL), lambda i: (0, 0)),
        out_shape=_sds((D_IN_PAD, D_MODEL)),
        compiler_params=_cp(("arbitrary",)), name="in_bwd_dw")(*pieces, x, norm_w)


def _flat(a):
    return a.reshape(-1, a.shape[-1])


def _as_pattern(a, r):
    return a if r == 1 else a.reshape(r, a.shape[0] // r, a.shape[1])


D_SHARD = D_IN // N_CHIPS
BA_START = 4 * D_DN
BA_PACKED = 4096
S1_HEAD = BA_START - D_SHARD
S1_BA = 2 * D_SHARD - BA_START


def _pack_rows(g):
    pad = jnp.zeros((D_IN_PAD - D_IN, g.shape[2]), g.dtype)
    s2_ba = 2 * DN_HEADS - S1_BA
    return jnp.concatenate([g[0][:D_SHARD], g[1][:S1_HEAD], g[2][s2_ba:D_SHARD], g[3][:D_SHARD],
                            g[1][S1_HEAD:D_SHARD], g[2][:s2_ba], pad], axis=0)


def _pack_w(g):
    n = g.shape[2]

    def body(g_ref, o_ref):
        g32 = g_ref.bitcast(jnp.uint32)
        o32 = o_ref.bitcast(jnp.uint32)
        full, head, ba1 = D_SHARD // 2, S1_HEAD // 2, S1_BA // 2
        ba2 = DN_HEADS - ba1
        pieces = [(0, 0, full), (1, 0, head), (2, ba2, full), (3, 0, full), (1, head, full), (2, 0, ba2)]
        at = 0
        for chip, lo, hi in pieces:
            o32[at:at + hi - lo, :] = g32[chip, lo:hi, :]
            at += hi - lo
        o32[at:D_IN_PAD // 2, :] = jnp.zeros((D_IN_PAD // 2 - at, n), jnp.uint32)

    vm = pl.BlockSpec(memory_space=pltpu.VMEM)
    return pl.pallas_call(body, in_specs=[vm], out_specs=vm, out_shape=_sds((D_IN_PAD, n), g.dtype),
                          compiler_params=_cp(), name="pack_w")(g)


def _unpack_w(p, rows):
    n = p.shape[1]
    tn = 256
    mid = BA_PACKED + S1_BA
    pieces = [(0, 0, (0, D_SHARD)), (1, 0, (D_SHARD, BA_START)), (1, S1_HEAD, (BA_PACKED, mid)),
              (2, 0, (mid, BA_PACKED + 2 * DN_HEADS)), (2, 2 * DN_HEADS - S1_BA, (BA_START, BA_START + S1_HEAD)),
              (3, 0, (BA_START + S1_HEAD, BA_PACKED))]

    def body(p_ref, o_ref):
        for chip, at, (lo, hi) in pieces:
            o_ref[chip, at:at + hi - lo, :] = p_ref[lo:hi, :]
        for chip in range(N_CHIPS):
            o_ref[chip, D_SHARD:rows, :] = jnp.zeros((rows - D_SHARD, tn), p.dtype)

    return pl.pallas_call(
        body, grid=(n // tn,),
        in_specs=[pl.BlockSpec((D_IN_PAD, tn), lambda j: (0, j))],
        out_specs=pl.BlockSpec((N_CHIPS, rows, tn), lambda j: (0, 0, j)),
        out_shape=_sds((N_CHIPS, rows, n), p.dtype),
        compiler_params=_cp(("parallel",)), name="unpack_w")(p)


def _unpack_rows(p, rows):
    mid = BA_PACKED + S1_BA
    pad = jnp.zeros((rows - D_SHARD, p.shape[1]), p.dtype)
    return jnp.stack([jnp.concatenate([p[0:D_SHARD], pad], axis=0),
                      jnp.concatenate([p[D_SHARD:BA_START], p[BA_PACKED:mid], pad], axis=0),
                      jnp.concatenate([p[mid:BA_PACKED + 2 * DN_HEADS], p[BA_START:BA_START + S1_HEAD], pad], axis=0),
                      jnp.concatenate([p[BA_START + S1_HEAD:BA_PACKED], pad], axis=0)])


def _lane_row(vec, offset):
    return jnp.pad(vec.reshape(1, -1), ((0, 0), (offset, 128 - offset - vec.shape[0])))


def _local_step(x, tgt, norm_w, w_bf, conv_w, a_log, dt_bias, dn_norm_w, q_norm_w, k_norm_w, rel_bias, w_out_bf):
    s = x.shape[0]
    nc = s // CHUNK
    conv_w8 = jnp.pad(conv_w, ((0, 8 - CONV_W), (0, 0)))
    alog_l = _lane_row(a_log.reshape(-1), DN_HEADS)
    dtb_l = _lane_row(dt_bias.reshape(-1), DN_HEADS)
    dnw_t = jnp.tile(dn_norm_w.reshape(1, DN_HD), (1, DN_HEADS))
    qw_t = jnp.tile(q_norm_w.reshape(1, ATT_HD), (1, ATT_HEADS))
    kw_t = jnp.tile(k_norm_w.reshape(1, ATT_HD), (1, ATT_HEADS))
    bd128 = _compact_mat(DN_HD)
    bd64 = _compact_mat(ATT_HD)
    tri = _tri_incl()
    sel = _lane_select()
    buckets = _bucket_tables()

    proj = _in_proj(x, norm_w, w_bf)
    qn, kn, v_dn, bg = _dn_pre(proj, conv_w8, alog_l, dtb_l)
    bgt = bg[:, 0:8].reshape(nc, CHUNK, 8).transpose(0, 2, 1)
    u, w, qd, kt, attn, t_inv, gl = _dn_prep(qn, kn, v_dn, bg, bgt, tri, sel)
    o_dn, vn, sp = _dn_scan(u, w, qd, kt, attn, gl)
    q1, k1, v1, q4, k4, v4, q16, k16, v16 = _att_pre(proj, qw_t, kw_t, bd64)
    rs = [r for _, r in PATTERNS]
    qkv = [(q1, k1, v1, 0), (_flat(q4), _flat(k4), _flat(v4), 0), (_flat(q16), _flat(k16), _flat(v16), 0)]
    bias = _bias_fwd(rel_bias, buckets)
    o_pats, lse_pats = [], []
    for p, r in enumerate(rs):
        o_p, lse_p = _att_fwd(*qkv[p], bias, p, r, "att_fwd_r%d" % r)
        o_pats.append(_as_pattern(o_p, r))
        lse_pats.append(_as_pattern(lse_p, r))
    mixed, o_att, l1, l4, l16 = _post_fwd(o_dn, proj, o_pats, lse_pats, dnw_t, bd128)
    dy, loss_blk = _out_fwd(x, mixed, w_out_bf, tgt)

    dmixed, d_w_out = _out_bwd(dy, mixed, w_out_bf)
    do_dn, dz, dgate, do1, do4, do16, dl1, dl4, dl16, d_dnw = _post_bwd(dmixed, o_dn, proj, o_att, dnw_t, bd128)
    side = [(do1, l1, dl1), (_flat(do4), _flat(l4), _flat(dl4)), (_flat(do16), _flat(l16), _flat(dl16))]
    dq_pats, dk_pats, dv_pats, ds_accs = [], [], [], []
    for p, r in enumerate(rs):
        dq_p, dk_p, dv_p, ds_p = _att_bwd(*qkv[p], *side[p], bias, p, r, "att_bwd_r%d" % r)
        dq_pats.append(_as_pattern(dq_p, r))
        dk_pats.append(_as_pattern(dk_p, r))
        dv_pats.append(_as_pattern(dv_p, r))
        ds_accs.append(ds_p)
    d_rel_bias = _bias_bwd(ds_accs, buckets)
    dq_att, dk_att, dv_att, d_qw, d_kw = _att_pre_bwd(dq_pats, dk_pats, dv_pats, proj, qw_t, kw_t, bd64)
    du, dqd, dkt, dw, dattn, dgl = _dn_scan_bwd(do_dn, sp, qd, kt, w, vn, attn, gl)
    dqn, dkn, dv_dn, dbg = _dn_prep_bwd(qn, kn, v_dn, bg, bgt, tri, sel, t_inv, attn, u, w, du, dw, dqd, dkt, dattn,
                                        dgl)
    dyc, dba, dsm = _dn_pre_bwd(dqn, dkn, dv_dn, dbg, proj, conv_w8, alog_l, dtb_l)
    d_qkv_dn, d_conv8 = _conv_bwd(dyc, proj, conv_w8)
    pieces = (d_qkv_dn, dz, dq_att, dk_att, dv_att, dgate, dba)
    d_w_in_t = _in_bwd_dw(pieces, x, norm_w)
    last = functools.partial(_in_bwd_dx, pieces, w_bf, x, dy, norm_w)

    grads = dict(
        w_in_t=d_w_in_t,
        conv_w=d_conv8[0:CONV_W, :],
        a_log=dsm[0:1, DN_HEADS:2 * DN_HEADS],
        dt_bias=dsm[1:2, DN_HEADS:2 * DN_HEADS],
        dn_norm_w=d_dnw[0:1, :],
        q_norm_w=d_qw[0:1, :].reshape(ATT_HEADS, ATT_HD),
        k_norm_w=d_kw[0:1, :].reshape(ATT_HEADS, ATT_HD),
        rel_bias=d_rel_bias,
        w_out=d_w_out,
    )
    return loss_blk[0, 0], last, grads


MESH_ID = pl.DeviceIdType.MESH
ANY = pl.BlockSpec(memory_space=pl.ANY)


def _position():
    return lax.axis_index("x"), lax.axis_index("y"), lax.axis_index("c")


def _other_chips(x, y):
    return [(1 - x, y), (x, 1 - y), (1 - x, 1 - y)]


SHARD_PAD = 1040
WIN = 528


def _row_split(n):
    return (n // 2) // 128 * 128


def _part(ref, cc):
    n = ref.shape[0]
    sp = _row_split(n)
    return ref.at[pl.ds(0, sp)] if cc == 0 else ref.at[pl.ds(sp, n - sp)]


def _gather_weights(wt_s, w_out_s, conv_s):
    def body(a_ref, b_ref, c_ref, ga_ref, gb_ref, gc_ref, send_sems, recv_sems, loc_sems, a_vmem, b_vmem):
        x, y, c = _position()
        me = 2 * x + y
        sib = (x, y, 1 - c)
        big = ((a_ref, ga_ref), (b_ref, gb_ref))
        stage_in = [pltpu.make_async_copy(a_ref, a_vmem, loc_sems.at[0]),
                    pltpu.make_async_copy(b_ref, b_vmem, loc_sems.at[1])]
        local = [pltpu.make_async_copy(a_vmem, ga_ref.at[me], loc_sems.at[0]),
                 pltpu.make_async_copy(b_vmem, gb_ref.at[me], loc_sems.at[1]),
                 pltpu.make_async_copy(c_ref, gc_ref.at[me], loc_sems.at[2])]
        for cp in stage_in:
            cp.start()
        local[2].start()
        for cp in stage_in:
            cp.wait()
        for cp in local[:2]:
            cp.start()
        others = _other_chips(x, y)

        def exchange(cc):
            sends = []
            for j, (px, py) in enumerate(others):
                for t, (src, dst) in enumerate(big):
                    k = 3 * j + t
                    sends.append(pltpu.make_async_remote_copy(
                        src_ref=_part(src, cc), dst_ref=_part(dst.at[me], cc), send_sem=send_sems.at[k],
                        recv_sem=recv_sems.at[k], device_id=(px, py, c), device_id_type=MESH_ID))
                sends.append(pltpu.make_async_remote_copy(
                    src_ref=c_ref, dst_ref=gc_ref.at[me], send_sem=send_sems.at[3 * j + 2],
                    recv_sem=recv_sems.at[3 * j + 2], device_id=(px, py, c), device_id_type=MESH_ID))
            for cp in sends:
                cp.start()
            for j, (px, py) in enumerate(others):
                src_chip = 2 * px + py
                for t, (src, dst) in enumerate(big):
                    landed = _part(dst.at[src_chip], cc)
                    pltpu.make_async_remote_copy(
                        src_ref=_part(src, cc), dst_ref=landed, send_sem=send_sems.at[3 * j + t],
                        recv_sem=recv_sems.at[3 * j + t], device_id=(px, py, c), device_id_type=MESH_ID).wait_recv()
                    k = 9 + 2 * j + t
                    fwd = pltpu.make_async_remote_copy(
                        src_ref=landed, dst_ref=landed, send_sem=send_sems.at[k], recv_sem=recv_sems.at[k],
                        device_id=sib, device_id_type=MESH_ID)
                    fwd.start()
                    sends.append(fwd)
                pltpu.make_async_remote_copy(
                    src_ref=c_ref, dst_ref=gc_ref.at[src_chip], send_sem=send_sems.at[3 * j + 2],
                    recv_sem=recv_sems.at[3 * j + 2], device_id=(px, py, c), device_id_type=MESH_ID).wait_recv()
            for j, (px, py) in enumerate(others):
                src_chip = 2 * px + py
                for t, (src, dst) in enumerate(big):
                    k = 9 + 2 * j + t
                    theirs = _part(dst.at[src_chip], 1 - cc)
                    pltpu.make_async_remote_copy(
                        src_ref=theirs, dst_ref=theirs, send_sem=send_sems.at[k], recv_sem=recv_sems.at[k],
                        device_id=sib, device_id_type=MESH_ID).wait_recv()
            for cp in sends:
                cp.wait_send()

        for cc in (0, 1):
            pl.when(c == cc)(functools.partial(exchange, cc))
        for cp in local:
            cp.wait()

    srcs = (wt_s, w_out_s, conv_s)
    n_sem = 9 + 6
    return pl.pallas_call(
        body, in_specs=[ANY] * 3, out_specs=[ANY] * 3,
        out_shape=[_sds((N_CHIPS,) + a.shape, a.dtype) for a in srcs],
        scratch_shapes=[pltpu.SemaphoreType.DMA((n_sem,)), pltpu.SemaphoreType.DMA((n_sem,)),
                        pltpu.SemaphoreType.DMA((3,)), pltpu.VMEM(wt_s.shape, wt_s.dtype),
                        pltpu.VMEM(w_out_s.shape, w_out_s.dtype)],
        compiler_params=_cp(), name="gather_weights")(*srcs)


N_DEV = 8
PEER_FLIPS = [(dx, dy, dc) for dx in (0, 1) for dy in (0, 1) for dc in (0, 1)][1:]


def _small_copies(s_ref, rs_ref, send_sems, recv_sems):
    x, y, c = _position()
    dev = 4 * x + 2 * y + c
    sends, recvs = [], []
    for f, (dx, dy, dc) in enumerate(PEER_FLIPS):
        peer = (x ^ dx, y ^ dy, c ^ dc)
        sends.append(pltpu.make_async_remote_copy(
            src_ref=s_ref, dst_ref=rs_ref.at[dev], send_sem=send_sems.at[f], recv_sem=recv_sems.at[f],
            device_id=peer, device_id_type=MESH_ID))
        recvs.append(pltpu.make_async_remote_copy(
            src_ref=s_ref, dst_ref=rs_ref.at[4 * peer[0] + 2 * peer[1] + peer[2]], send_sem=send_sems.at[f],
            recv_sem=recv_sems.at[f], device_id=peer, device_id_type=MESH_ID))
    return sends, recvs


def _fill_parts(h_in, r_in, h_out, r_out, small, rows_in, rows_out):
    def body(ha_ref, ra_ref, hb_ref, rb_ref, s_ref, fa_ref, fb_ref, rs_ref, send_sems, recv_sems, loc_sems,
             small_send, small_recv, a_vmem, b_vmem, ra_vmem, rb_vmem):
        x, y, c = _position()
        sib = (x, y, 1 - c)
        chip = 2 * x + y
        pairs = ((a_vmem, fa_ref), (b_vmem, fb_ref))
        own_small = pltpu.make_async_copy(s_ref, rs_ref.at[4 * x + 2 * y + c], loc_sems.at[2])
        own_small.start()
        small_sends, small_recvs = _small_copies(s_ref, rs_ref, small_send, small_recv)
        for cp in small_sends:
            cp.start()
        stage_in = [pltpu.make_async_copy(ha_ref.at[chip], a_vmem, loc_sems.at[0]),
                    pltpu.make_async_copy(hb_ref.at[chip], b_vmem, loc_sems.at[1]),
                    pltpu.make_async_copy(ra_ref, ra_vmem, loc_sems.at[3]),
                    pltpu.make_async_copy(rb_ref, rb_vmem, loc_sems.at[4])]
        for cp in stage_in:
            cp.start()
        for cp in stage_in:
            cp.wait()
        for tot, recv in ((a_vmem, ra_vmem), (b_vmem, rb_vmem)):
            acc = tot[...]
            for j in range(3):
                acc = acc + recv[j].astype(F32)
            tot[...] = acc

        def fill(cc):
            mine = [_part(dst, cc) for _, dst in pairs]
            srcs = [src.at[pl.ds(0, m.shape[0])] for src, m in zip((a_vmem, b_vmem), mine)]
            local = [pltpu.make_async_copy(s, m, loc_sems.at[t]) for t, (s, m) in enumerate(zip(srcs, mine))]
            sends = [pltpu.make_async_remote_copy(src_ref=s, dst_ref=m, send_sem=send_sems.at[t],
                                                  recv_sem=recv_sems.at[t], device_id=sib, device_id_type=MESH_ID)
                     for t, (s, m) in enumerate(zip(srcs, mine))]
            for cp in local + sends:
                cp.start()
            for t, (_, dst) in enumerate(pairs):
                theirs = _part(dst, 1 - cc)
                pltpu.make_async_remote_copy(src_ref=theirs, dst_ref=theirs, send_sem=send_sems.at[t],
                                             recv_sem=recv_sems.at[t], device_id=sib, device_id_type=MESH_ID).wait_recv()
            for cp in sends:
                cp.wait_send()
            for cp in local:
                cp.wait()

        for cc in (0, 1):
            pl.when(c == cc)(functools.partial(fill, cc))
        for cp in small_recvs:
            cp.wait_recv()
        for cp in small_sends:
            cp.wait_send()
        own_small.wait()

    n_peer = len(PEER_FLIPS)
    return pl.pallas_call(
        body, in_specs=[ANY] * 5, out_specs=[ANY] * 3,
        out_shape=[_sds((rows_in, D_MODEL)), _sds((rows_out, D_MODEL)), _sds((N_DEV,) + small.shape, small.dtype)],
        scratch_shapes=[pltpu.SemaphoreType.DMA((2,)), pltpu.SemaphoreType.DMA((2,)), pltpu.SemaphoreType.DMA((5,)),
                        pltpu.SemaphoreType.DMA((n_peer,)), pltpu.SemaphoreType.DMA((n_peer,)),
                        pltpu.VMEM(h_in.shape[1:], F32), pltpu.VMEM(h_out.shape[1:], F32),
                        pltpu.VMEM(r_in.shape, r_in.dtype), pltpu.VMEM(r_out.shape, r_out.dtype)],
        compiler_params=_cp(), name="fill_parts")(h_in, r_in, h_out, r_out, small)


def _chip_sums(u_in, u_out, win_in, win_out):
    wins = (win_in, win_out)

    def body(a_ref, b_ref, ha_ref, hba_ref, hb_ref, hbb_ref, send_sems, recv_sems, loc_sems, out_sems,
             mine_a, theirs_a, sum_a, sumb_a, mine_b, theirs_b, sum_b, sumb_b):
        x, y, c = _position()
        sib = (x, y, 1 - c)
        groups = ((a_ref, mine_a, theirs_a, sum_a, sumb_a, ha_ref, hba_ref, win_in),
                  (b_ref, mine_b, theirs_b, sum_b, sumb_b, hb_ref, hbb_ref, win_out))
        loads, sends = [], []
        for t, (src, mine, theirs, _, _, _, _, win) in enumerate(groups):
            split = _row_split(src.shape[1])
            for k in range(N_CHIPS):
                n = N_CHIPS * t + k
                loads.append(pltpu.make_async_copy(
                    src.at[k, pl.ds(pl.multiple_of(c * split, split), win), :], mine.at[k], loc_sems.at[n]))
                sends.append(pltpu.make_async_remote_copy(
                    src_ref=src.at[k, pl.ds(pl.multiple_of((1 - c) * split, split), win), :], dst_ref=theirs.at[k],
                    send_sem=send_sems.at[n], recv_sem=recv_sems.at[n], device_id=sib, device_id_type=MESH_ID))
        for cp in sends + loads:
            cp.start()
        stores = []
        for t, (_, mine, theirs, tot, totb, h_out, hb_out, _) in enumerate(groups):
            for k in range(N_CHIPS):
                n = N_CHIPS * t + k
                loads[n].wait()
                sends[n].wait_recv()
                val = mine[k] + theirs[k]
                tot[k] = val
                totb[k] = val.astype(BF16)
                stores += [pltpu.make_async_copy(tot.at[k], h_out.at[k], out_sems.at[2 * n]),
                           pltpu.make_async_copy(totb.at[k], hb_out.at[k], out_sems.at[2 * n + 1])]
                stores[-2].start()
                stores[-1].start()
        for cp in sends:
            cp.wait_send()
        for cp in stores:
            cp.wait()

    shapes = [(N_CHIPS, w, D_MODEL) for w in wins]
    n_cp = 2 * N_CHIPS
    vmem = []
    for shp in shapes:
        vmem += [pltpu.VMEM(shp, F32), pltpu.VMEM(shp, F32), pltpu.VMEM(shp, F32), pltpu.VMEM(shp, BF16)]
    return pl.pallas_call(
        body, in_specs=[ANY] * 2, out_specs=[ANY] * 4,
        out_shape=[_sds(shapes[0]), _sds(shapes[0], BF16), _sds(shapes[1]), _sds(shapes[1], BF16)],
        scratch_shapes=[pltpu.SemaphoreType.DMA((n_cp,)), pltpu.SemaphoreType.DMA((n_cp,)),
                        pltpu.SemaphoreType.DMA((n_cp,)), pltpu.SemaphoreType.DMA((2 * n_cp,))] + vmem,
        compiler_params=_cp(), name="chip_sums")(u_in, u_out)


SMALL_LAYOUT = (("norm_w", 1024), ("conv_w", 6144), ("a_log", 128), ("dt_bias", 128), ("dn_norm_w", 128),
                ("q_norm_w", 512), ("k_norm_w", 512), ("rel_bias", 256), ("loss", 128))
SMALL_TOTAL = sum(n for _, n in SMALL_LAYOUT)


def _small_offset(name):
    off = 0
    for n, size in SMALL_LAYOUT:
        if n == name:
            return off
        off += size
    raise KeyError(name)


def _pack_small(grads):
    parts = []
    for name, size in SMALL_LAYOUT:
        flat = grads[name].reshape(1, -1)
        parts.append(jnp.pad(flat, ((0, 0), (0, size - flat.shape[1]))))
    return jnp.concatenate(parts, axis=1)


def _sum_small(rows):
    n_dev = rows.shape[0]
    q_off = _small_offset("q_norm_w")
    k_off = _small_offset("k_norm_w")

    def body(r_ref, tot_ref, qk_ref):
        tot = r_ref[0:1, :]
        for d in range(1, n_dev):
            tot = tot + r_ref[d:d + 1, :]
        tot_ref[...] = tot
        for row, off in ((0, q_off), (1, k_off)):
            s4 = tot[:, off:off + 128] + tot[:, off + 128:off + 256] + tot[:, off + 256:off + 384] \
                + tot[:, off + 384:off + 512]
            qk_ref[row:row + 1, :] = s4 + pltpu.roll(s4, ATT_HD, 1)

    return pl.pallas_call(
        body, in_specs=[pl.BlockSpec(memory_space=pltpu.VMEM)],
        out_specs=[pl.BlockSpec(memory_space=pltpu.VMEM)] * 2,
        out_shape=[_sds((1, SMALL_TOTAL)), _sds((2, 128))],
        compiler_params=_cp(), name="sum_small")(rows)


def _adamw_math(w, g, m, v):
    m = ADAM_B1 * m + (1.0 - ADAM_B1) * g
    v = ADAM_B2 * v + (1.0 - ADAM_B2) * (g * g)
    m_hat = m / (1.0 - ADAM_B1 ** ADAM_STEP)
    v_hat = v / (1.0 - ADAM_B2 ** ADAM_STEP)
    delta = -ADAM_LR * (m_hat / (jnp.sqrt(v_hat) + ADAM_EPS) + ADAM_WD * w)
    return delta, m, v


def _adamw_big(g, w, m, v, name):
    rows, cols = w.shape
    tr = 128

    def body(g_ref, w_ref, m_ref, v_ref, go_ref, d_ref, nm_ref, nv_ref):
        g = g_ref[...]
        go_ref[...] = g
        d_ref[...], nm_ref[...], nv_ref[...] = _adamw_math(w_ref[...], g, m_ref[...], v_ref[...])

    blk = pl.BlockSpec((tr, cols), lambda i: (i, 0))
    return pl.pallas_call(
        body, grid=(pl.cdiv(rows, tr),), in_specs=[blk] * 4, out_specs=[blk] * 4,
        out_shape=[_sds((rows, cols))] * 4, compiler_params=_cp(("parallel",)), name=name)(g, w, m, v)


def _adamw_rows(g, w, m, v, name):
    rows, cols = w.shape
    tr = 128

    def body(g_ref, w_ref, m_ref, v_ref, go_ref, d_ref, nm_ref, nv_ref, s_g, s_d, s_m, s_v):
        g = g_ref[...]
        s_g[...] = g
        s_d[...], s_m[...], s_v[...] = _adamw_math(w_ref[...], g, m_ref[...], v_ref[...])
        for i in range(tr):
            for scr, out in ((s_g, go_ref), (s_d, d_ref), (s_m, nm_ref), (s_v, nv_ref)):
                out[i] = scr[i:i + 1, :]

    blk = pl.BlockSpec((tr, cols), lambda i: (i, 0))
    oblk = pl.BlockSpec((tr, 1, cols), lambda i: (i, 0, 0))
    return pl.pallas_call(
        body, grid=(pl.cdiv(rows, tr),), in_specs=[blk] * 4, out_specs=[oblk] * 4,
        out_shape=[_sds((rows, 1, cols))] * 4, scratch_shapes=[pltpu.VMEM((tr, cols), F32)] * 4,
        compiler_params=_cp(("parallel",)), name=name)(g, w, m, v)


def _adamw_small(w, g, m, v, name):
    def body(w_ref, g_ref, m_ref, v_ref, d_ref, nm_ref, nv_ref):
        d_ref[...], nm_ref[...], nv_ref[...] = _adamw_math(w_ref[...], g_ref[...], m_ref[...], v_ref[...])

    vm = pl.BlockSpec(memory_space=pltpu.VMEM)
    return pl.pallas_call(body, in_specs=[vm] * 4, out_specs=[vm] * 3, out_shape=[_sds(w.shape)] * 3,
                          compiler_params=_cp(), name=name)(w, g, m, v)


WEIGHTS = ("norm_w", "w_in", "conv_w", "a_log", "dt_bias", "dn_norm_w", "q_norm_w", "k_norm_w", "rel_bias", "w_out")


def kernel(x, norm_w, w_in, conv_w, a_log, dt_bias, dn_norm_w, q_norm_w, k_norm_w, rel_bias, w_out, loss_target, m_norm_w, m_w_in, m_conv_w, m_a_log, m_dt_bias, m_dn_norm_w, m_q_norm_w, m_k_norm_w, m_rel_bias, m_w_out, v_norm_w, v_w_in, v_conv_w, v_a_log, v_dt_bias, v_dn_norm_w, v_q_norm_w, v_k_norm_w, v_rel_bias, v_w_out):
    xi, yi, _ = _position()
    chip = 2 * xi + yi
    w_loc = dict(norm_w=norm_w, w_in=w_in[0].T, conv_w=conv_w[0], a_log=a_log, dt_bias=dt_bias, dn_norm_w=dn_norm_w,
                 q_norm_w=q_norm_w, k_norm_w=k_norm_w, rel_bias=rel_bias, w_out=w_out[0])
    m_loc = dict(norm_w=m_norm_w, w_in=m_w_in[0].T, conv_w=m_conv_w[0], a_log=m_a_log, dt_bias=m_dt_bias,
                 dn_norm_w=m_dn_norm_w, q_norm_w=m_q_norm_w, k_norm_w=m_k_norm_w, rel_bias=m_rel_bias,
                 w_out=m_w_out[0])
    v_loc = dict(norm_w=v_norm_w, w_in=v_w_in[0].T, conv_w=v_conv_w[0], a_log=v_a_log, dt_bias=v_dt_bias,
                 dn_norm_w=v_dn_norm_w, q_norm_w=v_q_norm_w, k_norm_w=v_k_norm_w, rel_bias=v_rel_bias,
                 w_out=v_w_out[0])

    wt_pad = jnp.pad(w_loc["w_in"].astype(BF16), ((0, SHARD_PAD - D_SHARD), (0, 0)))
    g_in, g_out, g_conv = _gather_weights(wt_pad, w_loc["w_out"].astype(BF16), w_loc["conv_w"])
    wt_full = _pack_w(g_in)
    w_out_full = g_out.reshape(D_MODEL, D_MODEL)
    conv_full = g_conv.transpose(1, 0, 2).reshape(CONV_W, 3 * D_DN)

    loss_local, last_kernel, grads = _local_step(x[0], loss_target[0], norm_w, wt_full, conv_full, a_log, dt_bias,
                                                 dn_norm_w, q_norm_w, k_norm_w, rel_bias, w_out_full)
    grads["loss"] = loss_local

    u_in = _unpack_w(grads["w_in_t"], SHARD_PAD)
    u_out = grads["w_out"].reshape(N_CHIPS, D_MODEL // N_CHIPS, D_MODEL)
    win_out = u_out.shape[1] // 2
    h_in, hb_in, h_out, hb_out = _chip_sums(u_in, u_out, WIN, win_out)
    grad_x, d_nw8, r_in, r_out = last_kernel(chip_sums=(hb_in, hb_out))
    grads["norm_w"] = d_nw8[0:1, :]
    small = _pack_small(grads)
    full_in, full_out, r_small = _fill_parts(h_in, r_in, h_out, r_out, small, SHARD_PAD, u_out.shape[1])
    tot_small, qk = _sum_small(r_small.reshape(8, SMALL_TOTAL))

    def small_grad(name, n):
        off = _small_offset(name)
        return tot_small[:, off:off + n]

    loss = small_grad("loss", 1).reshape(())
    conv_all = small_grad("conv_w", CONV_W * 3 * D_DN).reshape(CONV_W, 3 * D_DN)
    g_small = dict(
        norm_w=small_grad("norm_w", D_MODEL),
        conv_w=lax.dynamic_slice_in_dim(conv_all, chip * (3 * D_DN // N_CHIPS), 3 * D_DN // N_CHIPS, axis=1),
        a_log=small_grad("a_log", DN_HEADS),
        dt_bias=small_grad("dt_bias", DN_HEADS),
        dn_norm_w=small_grad("dn_norm_w", DN_HD),
        q_norm_w=qk[0:1, 0:ATT_HD],
        k_norm_w=qk[1:2, 0:ATT_HD],
        rel_bias=small_grad("rel_bias", ATT_HEADS * N_BUCKETS).reshape(ATT_HEADS, N_BUCKETS),
    )

    out_g, out_d, out_m, out_v = {}, {}, {}, {}
    out_g["w_in"], out_d["w_in"], out_m["w_in"], out_v["w_in"] = _adamw_rows(
        full_in, w_loc["w_in"], m_loc["w_in"], v_loc["w_in"], "adamw_w_in")
    out_g["w_out"], out_d["w_out"], out_m["w_out"], out_v["w_out"] = _adamw_big(
        full_out, w_loc["w_out"], m_loc["w_out"], v_loc["w_out"], "adamw_w_out")
    for name in g_small:
        out_g[name] = g_small[name]
        out_d[name], out_m[name], out_v[name] = _adamw_small(w_loc[name], g_small[name], m_loc[name], v_loc[name],
                                                             "adamw_" + name)
    for d in (out_g, out_d, out_m, out_v):
        d["w_in"] = d["w_in"].transpose(1, 2, 0)
        for name in ("conv_w", "w_out"):
            d[name] = d[name][None]
    return (loss, grad_x[None], *[out_g[n] for n in WEIGHTS], *[out_d[n] for n in WEIGHTS],
            *[out_m[n] for n in WEIGHTS], *[out_v[n] for n in WEIGHTS])
```

```python
import functools
import math

import numpy as np
import jax
import jax.numpy as jnp
from jax import lax
from jax.experimental import pallas as pl
from jax.experimental.pallas import tpu as pltpu

F32 = jnp.float32
BF16 = jnp.bfloat16
HI = lax.Precision.HIGHEST

D_MODEL = 1024
D_DN = 512
DN_HEADS = 4
DN_HD = 128
CONV_W = 4
CHUNK = 64
D_ATT = 512
ATT_HEADS = 8
ATT_HD = 64
PATTERNS = ((128, 1), (512, 4), (2048, 16))
N_BUCKETS = 32
MAX_DISTANCE = 2048
D_IN = 4 * D_DN + 2 * DN_HEADS + 4 * D_ATT
D_IN_PAD = 4224
EPS = 1e-6
BLK = 128
NEG = -1e30
N_CHIPS = 4

ADAM_LR = 0.001
ADAM_B1 = 0.9
ADAM_B2 = 0.999
ADAM_EPS = 1e-08
ADAM_WD = 0.01
ADAM_STEP = 10

VMEM_LIMIT = 56 * 1024 * 1024

COL_Z = 3
COL_ATT_Q = 4
COL_ATT_K = 5
COL_ATT_V = 6
COL_GATE = 7
COL_BA_128 = 32


def _cp(sem=None):
    if sem is None:
        return pltpu.CompilerParams(vmem_limit_bytes=VMEM_LIMIT)
    return pltpu.CompilerParams(dimension_semantics=sem, vmem_limit_bytes=VMEM_LIMIT)


def _sds(shape, dtype=F32):
    return jax.ShapeDtypeStruct(shape, dtype)


def _mm(a, b):
    return jnp.dot(a.astype(BF16), b.astype(BF16), preferred_element_type=F32)


def _mm_nt(a, b):
    return lax.dot_general(a.astype(BF16), b.astype(BF16), (((1,), (1,)), ((), ())),
                           preferred_element_type=F32)


def _mm_tn(a, b):
    return lax.dot_general(a.astype(BF16), b.astype(BF16), (((0,), (0,)), ((), ())),
                           preferred_element_type=F32)


def _mmx(a, b):
    return jnp.dot(a, b, precision=HI, preferred_element_type=F32)


def _mmx_nt(a, b):
    return lax.dot_general(a, b, (((1,), (1,)), ((), ())), precision=HI, preferred_element_type=F32)


def _mmx_tn(a, b):
    return lax.dot_general(a, b, (((0,), (0,)), ((), ())), precision=HI, preferred_element_type=F32)


def _dot(a, b):
    return jnp.dot(a, b, preferred_element_type=F32)


def _dot_nt(a, b):
    return lax.dot_general(a, b, (((1,), (1,)), ((), ())), preferred_element_type=F32)


def _dot_tn(a, b):
    return lax.dot_general(a, b, (((0,), (0,)), ((), ())), preferred_element_type=F32)


def _split(a):
    hi = a.astype(BF16)
    return hi, (a - hi.astype(F32)).astype(BF16)


def _mm3(a_s, b_s):
    return _dot(a_s[0], b_s[0]) + _dot(a_s[0], b_s[1]) + _dot(a_s[1], b_s[0])


def _mm3_tn(a_s, b_s):
    return _dot_tn(a_s[0], b_s[0]) + _dot_tn(a_s[0], b_s[1]) + _dot_tn(a_s[1], b_s[0])


def _interleave(gens):
    live = list(gens)
    while live:
        nxt = []
        for g in live:
            try:
                next(g)
                nxt.append(g)
            except StopIteration:
                pass
        live = nxt


def _segsum(x, bd):
    hi = x.astype(BF16)
    r1 = x - hi.astype(F32)
    mid = r1.astype(BF16)
    lo = (r1 - mid.astype(F32)).astype(BF16)
    return (jnp.dot(hi, bd, preferred_element_type=F32) + jnp.dot(mid, bd, preferred_element_type=F32)
            + jnp.dot(lo, bd, preferred_element_type=F32))


def _compact_mat(seg, n=512):
    slot = 128 * seg // n
    src = np.arange(n)[:, None]
    dst = np.arange(128)[None, :]
    return jnp.asarray((src // seg == dst // slot).astype(np.float32), dtype=BF16)


def _seg_mean(x, cm, seg):
    hi, lo = _split(x)
    return (_dot(hi, cm) + _dot(lo, cm)) * (1.0 / seg)


def _seg_expand(c, cm, seg):
    hi, lo = _split(c)
    return (_dot_nt(hi, cm) + _dot_nt(lo, cm)) * (cm.shape[0] / (128.0 * seg))


def _seg_rstd(x, cm, seg):
    return _seg_expand(lax.rsqrt(_seg_mean(x * x, cm, seg) + EPS), cm, seg)


def _sigmoid(x):
    return 1.0 / (1.0 + jnp.exp(-x))


def _silu_grad(x, s):
    return s * (1.0 + x * (1.0 - s))


def _tri_incl():
    i = np.arange(CHUNK)
    return jnp.asarray((i[:, None] >= i[None, :]).astype(np.float32))


def _t5_bucket(dist):
    max_exact = N_BUCKETS // 2
    d = np.maximum(dist, 1).astype(np.float64)
    large = max_exact + (np.log(d / max_exact) / math.log(MAX_DISTANCE / max_exact)
                         * (N_BUCKETS - max_exact)).astype(np.int32)
    large = np.minimum(large, N_BUCKETS - 1)
    return np.where(dist < max_exact, dist, large).astype(np.int32)


def _bucket_tables():
    qi = np.arange(BLK)[:, None]
    kj = np.arange(2 * BLK)[None, :]
    step = qi - kj + BLK
    return jnp.asarray(np.stack([_t5_bucket(np.clip(step, 0, None) * r) for _, r in PATTERNS]))


def _in_proj(x, norm_w, wt_bf):
    s = x.shape[0]
    tm = 512

    def body(x_ref, nw_ref, w_ref, o_ref):
        xv = x_ref[...]
        rstd = lax.rsqrt(jnp.mean(xv * xv, axis=-1, keepdims=True) + EPS)
        h = (xv * rstd * nw_ref[...]).astype(BF16)
        o_ref[...] = _dot_nt(h, w_ref[...])

    return pl.pallas_call(
        body, grid=(s // tm,),
        in_specs=[pl.BlockSpec((tm, D_MODEL), lambda i: (i, 0)),
                  pl.BlockSpec((1, D_MODEL), lambda i: (0, 0)),
                  pl.BlockSpec((D_IN_PAD, D_MODEL), lambda i: (0, 0))],
        out_specs=pl.BlockSpec((tm, D_IN_PAD), lambda i: (i, 0)),
        out_shape=_sds((s, D_IN_PAD)), compiler_params=_cp(("parallel",)), name="in_proj")(x, norm_w, wt_bf)


def _conv_group(cur, halo, w_ref, c):
    rows = cur.shape[0]
    lanes = slice(128 * c, 128 * c + 128)
    xcat = jnp.concatenate([halo, cur], axis=0)
    y = cur * w_ref[CONV_W - 1:CONV_W, lanes]
    for k in range(1, CONV_W):
        sh = pltpu.roll(xcat, k, 0)[8:8 + rows]
        y = y + sh * w_ref[CONV_W - 1 - k:CONV_W - k, lanes]
    return y


def _beta_g(ba, alog_l, dtb_l):
    lane = lax.broadcasted_iota(jnp.int32, ba.shape, 1)
    sig_b = _sigmoid(ba)
    t = ba + dtb_l
    softplus = jnp.maximum(t, 0.0) + jnp.log(1.0 + jnp.exp(-jnp.abs(t)))
    nega = -jnp.exp(alog_l)
    g = nega * softplus
    out = jnp.where(lane < DN_HEADS, sig_b, jnp.where(lane < 2 * DN_HEADS, g, 0.0))
    return out, lane, sig_b, t, nega, g


def _dn_pre(proj, conv_w8, alog_l, dtb_l):
    s = proj.shape[0]
    tr = 512
    nh = tr // 8

    def body(u_ref, halo_ref, ba_ref, w_ref, al_ref, dt_ref, q_ref, k_ref, v_ref, bg_ref):
        i = pl.program_id(0)
        keep = (i > 0).astype(F32)
        for c in range(12):
            lanes = slice(128 * c, 128 * c + 128)
            y = _conv_group(u_ref[:, lanes], halo_ref[:, lanes] * keep, w_ref, c)
            sv = y * _sigmoid(y)
            if c < 8:
                rs = lax.rsqrt(jnp.sum(sv * sv, axis=1, keepdims=True) + EPS)
                n = sv * rs
                if c < 4:
                    q_ref[:, lanes] = n * (DN_HD ** -0.5)
                else:
                    k_ref[:, slice(128 * (c - 4), 128 * (c - 3))] = n
            else:
                v_ref[:, slice(128 * (c - 8), 128 * (c - 7))] = sv
        bg_ref[...] = _beta_g(ba_ref[...], al_ref[...], dt_ref[...])[0]

    return pl.pallas_call(
        body, grid=(s // tr,),
        in_specs=[pl.BlockSpec((tr, 1536), lambda i: (i, 0)),
                  pl.BlockSpec((8, 1536), lambda i: (jnp.maximum(i * nh - 1, 0), 0)),
                  pl.BlockSpec((tr, 128), lambda i: (i, COL_BA_128)),
                  pl.BlockSpec((8, 1536), lambda i: (0, 0)),
                  pl.BlockSpec((1, 128), lambda i: (0, 0)),
                  pl.BlockSpec((1, 128), lambda i: (0, 0))],
        out_specs=[pl.BlockSpec((tr, 512), lambda i: (i, 0))] * 3 + [pl.BlockSpec((tr, 128), lambda i: (i, 0))],
        out_shape=[_sds((s, 512))] * 3 + [_sds((s, 128))],
        compiler_params=_cp(("parallel",)), name="dn_pre")(proj, proj, proj, conv_w8, alog_l, dtb_l)


CPS = 4
CPS_SCAN = 8
SPLIT_ITERS = 2


def _split3(a):
    hi = a.astype(BF16)
    r1 = a - hi.astype(F32)
    mid = r1.astype(BF16)
    return hi, mid, (r1 - mid.astype(F32)).astype(BF16)


def _lane_select():
    r = np.arange(128)
    return jnp.asarray((r[None, :, None] == np.arange(8)[:, None, None]) * np.ones((1, 1, 128)), dtype=BF16)


def _lane_bcast(a3, sel):
    return _dot(a3[0], sel) + _dot(a3[1], sel) + _dot(a3[2], sel)


def _rowsum_b(z, ones_b):
    hi, lo = _split(z)
    return _dot(hi, ones_b) + _dot(lo, ones_b)


def _chunk_cumsum(bg, bgt, tri):
    return _split3(bg), _split3(_mmx(tri, bg)), _mmx_nt(bgt, tri)


def _chunk_common(bg3, gc3, gc_row, h, sel_ref):
    gcc = _lane_bcast(gc3, sel_ref[DN_HEADS + h])
    beta = _dot(bg3[0], sel_ref[h]) + _dot(bg3[1], sel_ref[h])
    gcr = gc_row[DN_HEADS + h:DN_HEADS + h + 1, :]
    ii = lax.broadcasted_iota(jnp.int32, (CHUNK, CHUNK), 0)
    jj = lax.broadcasted_iota(jnp.int32, (CHUNK, CHUNK), 1)
    incl = ii >= jj
    strict = ii > jj
    decay = jnp.exp(jnp.where(incl, gcc[:, 0:CHUNK] - gcr, NEG))
    gl = gcc[CHUNK - 1:CHUNK, :]
    return gcc, beta, incl, strict, decay, gl


def _dn_prep(qn, kn, v, bg, bgt, tri, sel):
    s = qn.shape[0]
    nc = s // CHUNK

    def body(q_ref, k_ref, v_ref, bg_ref, bgt_ref, tri_ref, sel_ref,
             u_ref, w_ref, qd_ref, kt_ref, attn_ref, t_ref, gl_ref):
        tri_v = tri_ref[...]
        ii = lax.broadcasted_iota(jnp.int32, (CHUNK, CHUNK), 0)
        jj = lax.broadcasted_iota(jnp.int32, (CHUNK, CHUNK), 1)
        eye = (ii == jj).astype(F32)

        def chain(cc, h, bg3, gc3, gc_row):
            rows = slice(CHUNK * cc, CHUNK * cc + CHUNK)
            lanes = slice(128 * h, 128 * h + 128)
            gcc, beta, incl, strict, decay, gl = _chunk_common(bg3, gc3, gc_row, h, sel_ref)
            yield
            q = q_ref[rows, lanes]
            k = k_ref[rows, lanes]
            vv = v_ref[rows, lanes]
            kb = k * beta
            egc = jnp.exp(gcc)
            a_mat = jnp.where(strict, _mm_nt(kb, k) * decay, 0.0)
            attn_ref[cc, h] = jnp.where(incl, _mm_nt(q, k) * decay, 0.0)
            qd_ref[rows, lanes] = (q * egc).astype(BF16)
            kt_ref[rows, lanes] = (k * jnp.exp(gl - gcc)).astype(BF16)
            gl_ref[cc, h] = jnp.exp(gl)
            yield
            p = -a_mat
            t = eye + p
            for it in range(5):
                if it < SPLIT_ITERS:
                    ps = _split(p)
                    p = _mm3(ps, ps)
                    yield
                    t = t + _mm3(_split(t), _split(p))
                else:
                    p = _mm(p, p)
                    yield
                    t = t + _mm(t, p)
                yield
            t_ref[cc, h] = t
            ts = _split(t)
            u_ref[rows, lanes] = _mm3(ts, _split(vv * beta))
            w_ref[rows, lanes] = _mm3(ts, _split(kb * egc)).astype(BF16)

        gens = []
        for cc in range(CPS):
            bgv = bg_ref[CHUNK * cc:CHUNK * cc + CHUNK, :]
            bg3, gc3, gc_row = _chunk_cumsum(bgv, bgt_ref[cc], tri_v)
            gens += [chain(cc, h, bg3, gc3, gc_row) for h in range(DN_HEADS)]
        _interleave(gens)

    rows_step = CPS * CHUNK
    big = pl.BlockSpec((rows_step, 512), lambda n: (n, 0))
    sq = pl.BlockSpec((CPS, DN_HEADS, CHUNK, CHUNK), lambda n: (n, 0, 0, 0))
    return pl.pallas_call(
        body, grid=(nc // CPS,),
        in_specs=[big, big, big, pl.BlockSpec((rows_step, 128), lambda n: (n, 0)),
                  pl.BlockSpec((CPS, 8, CHUNK), lambda n: (n, 0, 0)),
                  pl.BlockSpec((CHUNK, CHUNK), lambda n: (0, 0)),
                  pl.BlockSpec((8, 128, 128), lambda n: (0, 0, 0))],
        out_specs=[big, big, big, big, sq, sq, pl.BlockSpec((CPS, DN_HEADS, 1, 128), lambda n: (n, 0, 0, 0))],
        out_shape=[_sds((s, 512))] + [_sds((s, 512), BF16)] * 3 + [_sds((nc, DN_HEADS, CHUNK, CHUNK))] * 2
        + [_sds((nc, DN_HEADS, 1, 128))],
        compiler_params=_cp(("parallel",)), name="dn_prep")(qn, kn, v, bg, bgt, tri, sel)


def _dn_scan(u, w, qd, kt, attn, gl):
    s = u.shape[0]
    nc = s // CHUNK

    def body(u_ref, w_ref, qd_ref, kt_ref, attn_ref, gl_ref, o_ref, vn_ref, sp_ref, st_ref):
        n = pl.program_id(0)

        @pl.when(n == 0)
        def _():
            st_ref[...] = jnp.zeros_like(st_ref)

        def chain(cc, h):
            rows = slice(CHUNK * cc, CHUNK * cc + CHUNK)
            lanes = slice(128 * h, 128 * h + 128)
            st = st_ref[h]
            sp_ref[cc, h] = st
            stb = st.astype(BF16)
            ws = _dot(w_ref[rows, lanes].astype(BF16), stb)
            qs = _dot(qd_ref[rows, lanes].astype(BF16), stb)
            yield
            vn = u_ref[rows, lanes] - ws
            vnb = vn.astype(BF16)
            vn_ref[rows, lanes] = vnb
            o_ref[rows, lanes] = qs + _dot(attn_ref[cc, h].astype(BF16), vnb)
            st_ref[h] = st * gl_ref[cc, h] + _dot_tn(kt_ref[rows, lanes].astype(BF16), vnb)

        for cc in range(CPS_SCAN):
            _interleave([chain(cc, h) for h in range(DN_HEADS)])

    big = pl.BlockSpec((CPS_SCAN * CHUNK, 512), lambda n: (n, 0))
    return pl.pallas_call(
        body, grid=(nc // CPS_SCAN,),
        in_specs=[big, big, big, big,
                  pl.BlockSpec((CPS_SCAN, DN_HEADS, CHUNK, CHUNK), lambda n: (n, 0, 0, 0)),
                  pl.BlockSpec((CPS_SCAN, DN_HEADS, 1, 128), lambda n: (n, 0, 0, 0))],
        out_specs=[big, big, pl.BlockSpec((CPS_SCAN, DN_HEADS, DN_HD, DN_HD), lambda n: (n, 0, 0, 0))],
        out_shape=[_sds((s, 512)), _sds((s, 512), BF16), _sds((nc, DN_HEADS, DN_HD, DN_HD))],
        scratch_shapes=[pltpu.VMEM((DN_HEADS, DN_HD, DN_HD), F32)],
        compiler_params=_cp(("arbitrary",)), name="dn_scan")(u, w, qd, kt, attn, gl)


R4 = PATTERNS[1][1]
R16 = PATTERNS[2][1]
TM = 512


def _pattern_spec(r, width=512):
    return pl.BlockSpec((r, TM // r, width), lambda i: (0, i, 0))


def _pattern_shape(s, r, dtype=F32, width=512):
    return _sds((r, s // r, width), dtype)


SLABS = pltpu.VMEM((4, TM, 128), F32)

HEAD_SLOT = 128 // ATT_HEADS


def _head_expand():
    src = np.arange(128)[:, None]
    dst = np.arange(512)[None, :]
    return jnp.asarray((src == (dst // ATT_HD) * HEAD_SLOT).astype(np.float32), dtype=BF16)


def _head_compact():
    src = np.arange(512)[:, None]
    dst = np.arange(128)[None, :]
    return jnp.asarray((src // ATT_HD == dst // HEAD_SLOT).astype(np.float32), dtype=BF16)


def _to_patterns(val, dsts, scr):
    for c in range(val.shape[1] // 128):
        lanes = slice(128 * c, 128 * c + 128)
        scr[c] = val[:, lanes]
        for dst_ref, r in dsts:
            for a in range(r):
                dst_ref[a, :, lanes] = scr[c, pl.ds(a, TM // r, stride=r), :].astype(dst_ref.dtype)


def _from_pattern(src_ref, r, scr):
    n_slab = src_ref.shape[2] // 128
    for c in range(n_slab):
        for a in range(r):
            scr[c, pl.ds(a, TM // r, stride=r), :] = src_ref[a, :, 128 * c:128 * c + 128].astype(F32)
    return jnp.concatenate([scr[c] for c in range(n_slab)], axis=1) if n_slab > 1 else scr[0]


def _att_pre(proj, qw_t, kw_t, bd64):
    s = proj.shape[0]

    def body(q_ref, k_ref, v_ref, qw_ref, kw_ref, bd_ref,
             q1_ref, k1_ref, v1_ref, q4_ref, k4_ref, v4_ref, q16_ref, k16_ref, v16_ref, scr):
        bd = bd_ref[...]
        q = q_ref[...]
        k = k_ref[...]
        qn = q * _seg_rstd(q, bd, ATT_HD) * qw_ref[...] * (ATT_HD ** -0.5)
        kn = k * _seg_rstd(k, bd, ATT_HD) * kw_ref[...]
        q1_ref[...] = qn.astype(BF16)
        k1_ref[...] = kn.astype(BF16)
        v1_ref[...] = v_ref[...].astype(BF16)
        _to_patterns(qn, ((q4_ref, R4), (q16_ref, R16)), scr)
        _to_patterns(kn, ((k4_ref, R4), (k16_ref, R16)), scr)
        _to_patterns(v_ref[...], ((v4_ref, R4), (v16_ref, R16)), scr)

    row = pl.BlockSpec((1, 512), lambda i: (0, 0))
    tok = pl.BlockSpec((TM, 512), lambda i: (i, 0))
    return pl.pallas_call(
        body, grid=(s // TM,),
        in_specs=[pl.BlockSpec((TM, 512), lambda i: (i, COL_ATT_Q)),
                  pl.BlockSpec((TM, 512), lambda i: (i, COL_ATT_K)),
                  pl.BlockSpec((TM, 512), lambda i: (i, COL_ATT_V)),
                  row, row, pl.BlockSpec((512, 128), lambda i: (0, 0))],
        out_specs=[tok] * 3 + [_pattern_spec(R4)] * 3 + [_pattern_spec(R16)] * 3,
        out_shape=[_sds((s, 512), BF16)] * 3 + [_pattern_shape(s, R4, BF16)] * 3 + [_pattern_shape(s, R16, BF16)] * 3,
        scratch_shapes=[SLABS],
        compiler_params=_cp(("parallel",)), name="att_pre")(proj, proj, proj, qw_t, kw_t, bd64)


def _bias_fwd(rel_bias, buckets):
    def body(rb_ref, bk_ref, o_ref):
        for p in range(len(PATTERNS)):
            bk = bk_ref[p]
            for h in range(ATT_HEADS):
                acc = jnp.zeros((BLK, 2 * BLK), F32)
                for b in range(N_BUCKETS):
                    acc = jnp.where(bk == b, rb_ref[h, b], acc)
                o_ref[p, h] = acc

    return pl.pallas_call(
        body,
        in_specs=[pl.BlockSpec(memory_space=pltpu.SMEM), pl.BlockSpec(memory_space=pltpu.VMEM)],
        out_specs=pl.BlockSpec(memory_space=pltpu.VMEM),
        out_shape=_sds((len(PATTERNS), ATT_HEADS, BLK, 2 * BLK)),
        compiler_params=_cp(), name="bias_fwd")(rel_bias, buckets)


def _bias_bwd(ds_accs, buckets):
    def body(ds0_ref, ds1_ref, ds2_ref, bk_ref, o_ref):
        for h in range(ATT_HEADS):
            for b in range(N_BUCKETS):
                tot = jnp.zeros((), F32)
                for p, ds_ref in enumerate((ds0_ref, ds1_ref, ds2_ref)):
                    tot = tot + jnp.sum(jnp.where(bk_ref[p] == b, ds_ref[h], 0.0))
                o_ref[h, b] = tot

    return pl.pallas_call(
        body,
        in_specs=[pl.BlockSpec(memory_space=pltpu.VMEM)] * 4,
        out_specs=pl.BlockSpec(memory_space=pltpu.SMEM),
        out_shape=_sds((ATT_HEADS, N_BUCKETS)),
        compiler_params=_cp(), name="bias_bwd")(*ds_accs, buckets)


QB_FWD = 2
QB_BWD = 4


def _att_masks(has_prev):
    qi = lax.broadcasted_iota(jnp.int32, (BLK, BLK), 0)
    kj = lax.broadcasted_iota(jnp.int32, (BLK, BLK), 1)
    lane = lax.broadcasted_iota(jnp.int32, (BLK, 2 * ATT_HD), 1)
    return jnp.logical_and(kj >= qi, has_prev), kj <= qi, lane < ATT_HD


def _head_lanes(h):
    half = h % 2
    return slice(ATT_HD * h, ATT_HD * h + ATT_HD), slice(ATT_HD * half, ATT_HD * half + ATT_HD)


def _att_scores(qm, kp2, kc2, bias_h, mask_prev, mask_cur):
    s_prev = jnp.where(mask_prev, _dot_nt(qm, kp2) + bias_h[:, :BLK], NEG)
    s_cur = jnp.where(mask_cur, _dot_nt(qm, kc2) + bias_h[:, BLK:], NEG)
    return s_prev, s_cur


def _att_fwd(q, k, v, v_col, bias, p_idx, r, name):
    QB = QB_FWD
    s = q.shape[0]
    nblk = s // BLK
    nseq = nblk // r

    def body(q_ref, kp_ref, kc_ref, vp_ref, vc_ref, b_ref, o_ref, lse_ref):
        j = pl.program_id(0)

        def head(h, rows, masks, q2, kp2, kc2, vp2, vc2):
            mask_prev, mask_cur, lo_half = masks
            out_l, pair_l = _head_lanes(h)
            sel = lo_half if h % 2 == 0 else jnp.logical_not(lo_half)
            qm = jnp.where(sel, q2, jnp.zeros_like(q2))
            s_prev, s_cur = _att_scores(qm, kp2, kc2, b_ref[0, h], mask_prev, mask_cur)
            yield
            m = jnp.maximum(jnp.max(s_prev, axis=1, keepdims=True), jnp.max(s_cur, axis=1, keepdims=True))
            p_prev = jnp.exp(s_prev - m)
            p_cur = jnp.exp(s_cur - m)
            l = jnp.sum(p_prev, axis=1, keepdims=True) + jnp.sum(p_cur, axis=1, keepdims=True)
            yield
            o2 = _dot(p_prev.astype(BF16), vp2) + _dot(p_cur.astype(BF16), vc2)
            o_ref[rows, out_l] = (o2 * (1.0 / l))[:, pair_l].astype(BF16)
            lse_ref[rows, HEAD_SLOT * h:HEAD_SLOT * h + HEAD_SLOT] = jnp.broadcast_to(m + jnp.log(l), (BLK, HEAD_SLOT))

        for sub in range(QB):
            rows = slice(BLK * sub, BLK * sub + BLK)
            before = slice(BLK * (sub - 1), BLK * sub)
            masks = _att_masks(((QB * j + sub) % nseq) != 0)
            gens = []
            for pp in range(ATT_HEADS // 2):
                lanes = slice(128 * pp, 128 * pp + 128)
                kp = kp_ref[:, lanes] if sub == 0 else kc_ref[before, lanes]
                vp = vp_ref[:, lanes] if sub == 0 else vc_ref[before, lanes]
                slabs = (q_ref[rows, lanes], kp, kc_ref[rows, lanes], vp, vc_ref[rows, lanes])
                gens += [head(2 * pp, rows, masks, *slabs), head(2 * pp + 1, rows, masks, *slabs)]
            _interleave(gens)

    cur = pl.BlockSpec((QB * BLK, 512), lambda j: (j, 0))
    prev = pl.BlockSpec((BLK, 512), lambda j: (jnp.maximum(QB * j - 1, 0), 0))
    vcur = pl.BlockSpec((QB * BLK, 512), lambda j: (j, v_col))
    vprev = pl.BlockSpec((BLK, 512), lambda j: (jnp.maximum(QB * j - 1, 0), v_col))
    return pl.pallas_call(
        body, grid=(nblk // QB,),
        in_specs=[cur, prev, cur, vprev, vcur,
                  pl.BlockSpec((1, ATT_HEADS, BLK, 2 * BLK), lambda j: (p_idx, 0, 0, 0))],
        out_specs=[cur, pl.BlockSpec((QB * BLK, 128), lambda j: (j, 0))],
        out_shape=[_sds((s, 512), BF16), _sds((s, 128))],
        compiler_params=_cp(("parallel",)), name=name)(q, k, k, v, v, bias)


def _post_fwd(o_dn, proj, o_pats, lse_pats, dnw_t, bd128):
    s = o_dn.shape[0]

    def body(o_ref, z_ref, gate_ref, o1_ref, o4_ref, o16_ref, s1_ref, s4_ref, s16_ref, wn_ref, bd_ref, ex_ref,
             mixed_ref, oatt_ref, l1_ref, l4_ref, l16_ref, scr_a, scr_b, scr_c, scr_d):
        o = o_ref[...]
        z = z_ref[...]
        rstd = _seg_rstd(o, bd_ref[...], DN_HD)
        y_dn = o * rstd * wn_ref[...] * (z * _sigmoid(z))
        mixed_ref[:, 0:512] = y_dn.astype(BF16)
        lses = (s1_ref[...], _from_pattern(s4_ref, R4, scr_a), _from_pattern(s16_ref, R16, scr_b))
        m = jnp.maximum(jnp.maximum(lses[0], lses[1]), lses[2])
        tot = jnp.exp(lses[0] - m) + jnp.exp(lses[1] - m) + jnp.exp(lses[2] - m)
        big_l = m + jnp.log(tot)
        l1_ref[...] = big_l
        _to_patterns(big_l, ((l4_ref, R4), (l16_ref, R16)), scr_a)
        ex = ex_ref[...]
        outs = (o1_ref[...], _from_pattern(o4_ref, R4, scr_c), _from_pattern(o16_ref, R16, scr_d))
        acc = jnp.zeros_like(o)
        for lse_p, o_p in zip(lses, outs):
            acc = acc + _lane_bcast(_split3(jnp.exp(lse_p - big_l)), ex) * o_p
        gate = gate_ref[...]
        oatt_ref[...] = acc
        mixed_ref[:, 512:1024] = (acc * (gate * _sigmoid(gate))).astype(BF16)

    blk = pl.BlockSpec((TM, 512), lambda i: (i, 0))
    cblk = pl.BlockSpec((TM, 128), lambda i: (i, 0))
    p4, p16 = _pattern_spec(R4), _pattern_spec(R16)
    c4, c16 = _pattern_spec(R4, 128), _pattern_spec(R16, 128)
    return pl.pallas_call(
        body, grid=(s // TM,),
        in_specs=[blk, pl.BlockSpec((TM, 512), lambda i: (i, COL_Z)),
                  pl.BlockSpec((TM, 512), lambda i: (i, COL_GATE)), blk, p4, p16, cblk, c4, c16,
                  pl.BlockSpec((1, 512), lambda i: (0, 0)), pl.BlockSpec((512, 128), lambda i: (0, 0)),
                  pl.BlockSpec((128, 512), lambda i: (0, 0))],
        out_specs=[pl.BlockSpec((TM, D_MODEL), lambda i: (i, 0)), blk, cblk, c4, c16],
        out_shape=[_sds((s, D_MODEL), BF16), _sds((s, 512)), _sds((s, 128)), _pattern_shape(s, R4, F32, 128),
                   _pattern_shape(s, R16, F32, 128)],
        scratch_shapes=[SLABS] * 4,
        compiler_params=_cp(("parallel",)), name="post_fwd")(o_dn, proj, proj, *o_pats, *lse_pats, dnw_t, bd128,
                                                              _head_expand())


def _out_fwd(x, mixed, w_out_bf, tgt):
    s = x.shape[0]
    tm = 512

    def body(x_ref, m_ref, w_ref, t_ref, dy_ref, loss_ref):
        i = pl.program_id(0)

        @pl.when(i == 0)
        def _():
            loss_ref[...] = jnp.zeros_like(loss_ref)

        y = x_ref[...] + jnp.dot(m_ref[...], w_ref[...], preferred_element_type=F32)
        err = y - t_ref[...]
        dy_ref[...] = err * (1.0 / D_MODEL)
        part = 0.5 * jnp.sum(jnp.mean(err * err, axis=-1, keepdims=True), axis=0, keepdims=True)
        loss_ref[...] = loss_ref[...] + part

    blk = pl.BlockSpec((tm, D_MODEL), lambda i: (i, 0))
    return pl.pallas_call(
        body, grid=(s // tm,),
        in_specs=[blk, blk, pl.BlockSpec((D_MODEL, D_MODEL), lambda i: (0, 0)), blk],
        out_specs=[blk, pl.BlockSpec((8, 128), lambda i: (0, 0))],
        out_shape=[_sds((s, D_MODEL)), _sds((8, 128))],
        compiler_params=_cp(("arbitrary",)), name="out_fwd")(x, mixed, w_out_bf, tgt)


def _out_bwd(dy, mixed, w_out_bf):
    s = dy.shape[0]
    tm = 512

    def body(dy_ref, m_ref, w_ref, dm_ref, dw_ref):
        i = pl.program_id(0)

        @pl.when(i == 0)
        def _():
            dw_ref[...] = jnp.zeros_like(dw_ref)

        dyb = dy_ref[...].astype(BF16)
        dm_ref[...] = lax.dot_general(dyb, w_ref[...], (((1,), (1,)), ((), ())), preferred_element_type=F32)
        dw_ref[...] = dw_ref[...] + lax.dot_general(m_ref[...], dyb, (((0,), (0,)), ((), ())),
                                                    preferred_element_type=F32)

    blk = pl.BlockSpec((tm, D_MODEL), lambda i: (i, 0))
    full = pl.BlockSpec((D_MODEL, D_MODEL), lambda i: (0, 0))
    return pl.pallas_call(
        body, grid=(s // tm,), in_specs=[blk, blk, full], out_specs=[blk, full],
        out_shape=[_sds((s, D_MODEL)), _sds((D_MODEL, D_MODEL))],
        compiler_params=_cp(("arbitrary",)), name="out_bwd")(dy, mixed, w_out_bf)


def _post_bwd(dmixed, o_dn, proj, o_att, dnw_t, bd128):
    s = o_dn.shape[0]
    tm = TM

    def body(ddn_ref, datt_ref, o_ref, z_ref, gate_ref, oatt_ref, wn_ref, bd128_ref, hc_ref,
             do_ref, dz_ref, dgate_ref, doatt_ref, do4_ref, do16_ref, delta_ref, dl4_ref, dl16_ref, dnw_ref, scr):
        i = pl.program_id(0)

        @pl.when(i == 0)
        def _():
            dnw_ref[...] = jnp.zeros_like(dnw_ref)

        bd128v = bd128_ref[...]
        o = o_ref[...]
        z = z_ref[...]
        wn = wn_ref[...]
        dy = ddn_ref[...]
        rstd = _seg_rstd(o, bd128v, DN_HD)
        nrm = o * rstd
        sz = _sigmoid(z)
        dz_ref[...] = (dy * nrm * wn * _silu_grad(z, sz)).astype(BF16)
        dn = dy * z * sz
        gw = dn * wn
        do_ref[...] = rstd * (gw - nrm * _seg_expand(_seg_mean(gw * nrm, bd128v, DN_HD), bd128v, DN_HD))
        colsum = jnp.sum(dn * nrm, axis=0, keepdims=True)
        fold = colsum[:, 0:128] + colsum[:, 128:256] + colsum[:, 256:384] + colsum[:, 384:512]
        dnw_ref[...] = dnw_ref[...] + fold
        dya = datt_ref[...]
        gate = gate_ref[...]
        oatt = oatt_ref[...]
        sg = _sigmoid(gate)
        dgate_ref[...] = (dya * oatt * _silu_grad(gate, sg)).astype(BF16)
        doa = dya * gate * sg
        doatt_ref[...] = doa.astype(BF16)
        delta = _segsum(doa * oatt, hc_ref[...])
        delta_ref[...] = delta
        _to_patterns(doa, ((do4_ref, R4), (do16_ref, R16)), scr)
        _to_patterns(delta, ((dl4_ref, R4), (dl16_ref, R16)), scr)

    blk = pl.BlockSpec((tm, 512), lambda i: (i, 0))
    cblk = pl.BlockSpec((tm, 128), lambda i: (i, 0))
    p4, p16 = _pattern_spec(R4), _pattern_spec(R16)
    c4, c16 = _pattern_spec(R4, 128), _pattern_spec(R16, 128)
    return pl.pallas_call(
        body, grid=(s // tm,),
        in_specs=[blk, pl.BlockSpec((tm, 512), lambda i: (i, 1)), blk,
                  pl.BlockSpec((tm, 512), lambda i: (i, COL_Z)), pl.BlockSpec((tm, 512), lambda i: (i, COL_GATE)),
                  blk, pl.BlockSpec((1, 512), lambda i: (0, 0)), pl.BlockSpec((512, 128), lambda i: (0, 0)),
                  pl.BlockSpec((512, 128), lambda i: (0, 0))],
        out_specs=[blk, blk, blk, blk, p4, p16, cblk, c4, c16, pl.BlockSpec((8, 128), lambda i: (0, 0))],
        out_shape=[_sds((s, 512))] + [_sds((s, 512), BF16)] * 3 + [_pattern_shape(s, R4, BF16),
                                          _pattern_shape(s, R16, BF16), _sds((s, 128)),
                                          _pattern_shape(s, R4, F32, 128), _pattern_shape(s, R16, F32, 128),
                                          _sds((8, 128))],
        scratch_shapes=[SLABS],
        compiler_params=_cp(("arbitrary",)), name="post_bwd")(dmixed, dmixed, o_dn, proj, proj, o_att, dnw_t,
                                                               bd128, _head_compact())


def _att_bwd(q, k, v, v_col, do, big_l, delta, bias, p_idx, r, name):
    QB = QB_BWD
    s = q.shape[0]
    nblk = s // BLK
    nseq = nblk // r
    nstep = nblk // QB

    def body(q_ref, kp_ref, kc_ref, vp_ref, vc_ref, do_ref, l_ref, dl_ref, b_ref,
             dq_ref, dk_ref, dv_ref, ds_ref, dkc_ref, dvc_ref):
        j = pl.program_id(0)

        @pl.when(j == 0)
        def _():
            dkc_ref[...] = jnp.zeros_like(dkc_ref)
            dvc_ref[...] = jnp.zeros_like(dvc_ref)
            ds_ref[...] = jnp.zeros_like(ds_ref)

        @pl.when(j < nstep)
        def _():
            def head(h, sub, masks, q2, do2, kp2, kc2, vp2, vc2):
                mask_prev, mask_cur, lo_half = masks
                rows = slice(BLK * sub, BLK * sub + BLK)
                out_l, pair_l = _head_lanes(h)
                sel = lo_half if h % 2 == 0 else jnp.logical_not(lo_half)
                qm = jnp.where(sel, q2, jnp.zeros_like(q2))
                dom = jnp.where(sel, do2, jnp.zeros_like(do2))
                s_prev, s_cur = _att_scores(qm, kp2, kc2, b_ref[0, h], mask_prev, mask_cur)
                dp_prev = _dot_nt(dom, vp2)
                dp_cur = _dot_nt(dom, vc2)
                yield
                lh = l_ref[rows, HEAD_SLOT * h:HEAD_SLOT * h + 1]
                dh = dl_ref[rows, HEAD_SLOT * h:HEAD_SLOT * h + 1]
                p_prev = jnp.exp(s_prev - lh)
                p_cur = jnp.exp(s_cur - lh)
                ds_prev = p_prev * (dp_prev - dh)
                ds_cur = p_cur * (dp_cur - dh)
                ds_ref[h, :, 0:BLK] = ds_ref[h, :, 0:BLK] + ds_prev
                ds_ref[h, :, BLK:2 * BLK] = ds_ref[h, :, BLK:2 * BLK] + ds_cur
                dsb_prev, dsb_cur = ds_prev.astype(BF16), ds_cur.astype(BF16)
                pb_prev, pb_cur = p_prev.astype(BF16), p_cur.astype(BF16)
                yield
                dq_ref[rows, out_l] = (_dot(dsb_prev, kp2) + _dot(dsb_cur, kc2))[:, pair_l].astype(BF16)
                dk_prev = _dot_tn(dsb_prev, q2)[:, pair_l]
                dv_prev = _dot_tn(pb_prev, do2)[:, pair_l]
                if sub == 0:
                    last = slice(BLK * (QB - 1), BLK * QB)
                    dk_ref[last, out_l] = (dkc_ref[last, out_l] + dk_prev).astype(BF16)
                    dv_ref[last, out_l] = (dvc_ref[last, out_l] + dv_prev).astype(BF16)
                else:
                    before = slice(BLK * (sub - 1), BLK * sub)
                    dkc_ref[before, out_l] = dkc_ref[before, out_l] + dk_prev
                    dvc_ref[before, out_l] = dvc_ref[before, out_l] + dv_prev
                yield
                dkc_ref[rows, out_l] = _dot_tn(dsb_cur, q2)[:, pair_l]
                dvc_ref[rows, out_l] = _dot_tn(pb_cur, do2)[:, pair_l]

            done = slice(0, BLK * (QB - 1))
            dk_ref[done, :] = dkc_ref[done, :].astype(BF16)
            dv_ref[done, :] = dvc_ref[done, :].astype(BF16)
            for sub in range(QB):
                rows = slice(BLK * sub, BLK * sub + BLK)
                before = slice(BLK * (sub - 1), BLK * sub)
                masks = _att_masks(((QB * j + sub) % nseq) != 0)
                gens = []
                for pp in range(ATT_HEADS // 2):
                    lanes = slice(128 * pp, 128 * pp + 128)
                    kp = kp_ref[:, lanes] if sub == 0 else kc_ref[before, lanes]
                    vp = vp_ref[:, lanes] if sub == 0 else vc_ref[before, lanes]
                    slabs = (q_ref[rows, lanes], do_ref[rows, lanes], kp, kc_ref[rows, lanes], vp,
                             vc_ref[rows, lanes])
                    gens += [head(2 * pp, sub, masks, *slabs), head(2 * pp + 1, sub, masks, *slabs)]
                _interleave(gens)

        @pl.when(j == nstep)
        def _():
            dk_ref[...] = dkc_ref[...].astype(BF16)
            dv_ref[...] = dvc_ref[...].astype(BF16)

    last_step = nstep - 1
    cur = pl.BlockSpec((QB * BLK, 512), lambda j: (jnp.minimum(j, last_step), 0))
    compact = pl.BlockSpec((QB * BLK, 128), lambda j: (jnp.minimum(j, last_step), 0))
    lag = pl.BlockSpec((QB * BLK, 512), lambda j: (jnp.clip(j - 1, 0, last_step), 0))
    prev = pl.BlockSpec((BLK, 512), lambda j: (jnp.clip(QB * j - 1, 0, nblk - 1), 0))
    vcur = pl.BlockSpec((QB * BLK, 512), lambda j: (jnp.minimum(j, last_step), v_col))
    vprev = pl.BlockSpec((BLK, 512), lambda j: (jnp.clip(QB * j - 1, 0, nblk - 1), v_col))
    return pl.pallas_call(
        body, grid=(nstep + 1,),
        in_specs=[cur, prev, cur, vprev, vcur, cur, compact, compact,
                  pl.BlockSpec((1, ATT_HEADS, BLK, 2 * BLK), lambda j: (p_idx, 0, 0, 0))],
        out_specs=[cur, lag, lag, pl.BlockSpec((ATT_HEADS, BLK, 2 * BLK), lambda j: (0, 0, 0))],
        out_shape=[_sds((s, 512), BF16)] * 3 + [_sds((ATT_HEADS, BLK, 2 * BLK))],
        scratch_shapes=[pltpu.VMEM((QB * BLK, 512), F32), pltpu.VMEM((QB * BLK, 512), F32)],
        compiler_params=_cp(("arbitrary",)), name=name)(q, k, k, v, v, do, big_l, delta, bias)


def _att_pre_bwd(dq_pats, dk_pats, dv_pats, proj, qw_t, kw_t, bd64):
    s = proj.shape[0]
    tm = TM

    def body(dq1_ref, dq4_ref, dq16_ref, dk1_ref, dk4_ref, dk16_ref, dv1_ref, dv4_ref, dv16_ref,
             q_ref, k_ref, qw_ref, kw_ref, bd_ref,
             dqr_ref, dkr_ref, dvr_ref, dqw_ref, dkw_ref, scr4, scr16):
        i = pl.program_id(0)

        @pl.when(i == 0)
        def _():
            dqw_ref[...] = jnp.zeros_like(dqw_ref)
            dkw_ref[...] = jnp.zeros_like(dkw_ref)

        bd = bd_ref[...]

        def total(d1_ref, d4_ref, d16_ref):
            return d1_ref[...] + _from_pattern(d4_ref, R4, scr4) + _from_pattern(d16_ref, R16, scr16)

        def one(d_refs, x_ref, w_ref, scale, dx_ref, dw_ref):
            dy = total(*d_refs) * scale
            x = x_ref[...]
            rstd = _seg_rstd(x, bd, ATT_HD)
            nrm = x * rstd
            dw_ref[...] = dw_ref[...] + jnp.sum(dy * nrm, axis=0, keepdims=True)
            g = dy * w_ref[...]
            dx_ref[...] = (rstd * (g - nrm * _seg_expand(_seg_mean(g * nrm, bd, ATT_HD), bd, ATT_HD))).astype(BF16)

        one((dq1_ref, dq4_ref, dq16_ref), q_ref, qw_ref, ATT_HD ** -0.5, dqr_ref, dqw_ref)
        one((dk1_ref, dk4_ref, dk16_ref), k_ref, kw_ref, 1.0, dkr_ref, dkw_ref)
        dvr_ref[...] = total(dv1_ref, dv4_ref, dv16_ref).astype(BF16)

    blk = pl.BlockSpec((tm, 512), lambda i: (i, 0))
    pats = [blk, _pattern_spec(R4), _pattern_spec(R16)]
    row = pl.BlockSpec((1, 512), lambda i: (0, 0))
    acc = pl.BlockSpec((8, 512), lambda i: (0, 0))
    return pl.pallas_call(
        body, grid=(s // tm,),
        in_specs=pats * 3 + [pl.BlockSpec((tm, 512), lambda i: (i, COL_ATT_Q)),
                             pl.BlockSpec((tm, 512), lambda i: (i, COL_ATT_K)), row, row,
                             pl.BlockSpec((512, 128), lambda i: (0, 0))],
        out_specs=[blk, blk, blk, acc, acc],
        out_shape=[_sds((s, 512), BF16)] * 3 + [_sds((8, 512))] * 2,
        scratch_shapes=[SLABS] * 2,
        compiler_params=_cp(("arbitrary",)), name="att_pre_bwd")(*dq_pats, *dk_pats, *dv_pats, proj, proj,
                                                                  qw_t, kw_t, bd64)


def _dn_scan_bwd(do, sp, qd, kt, w, vn, attn, gl):
    s = do.shape[0]
    nc = s // CHUNK

    def body(do_ref, sp_ref, qd_ref, kt_ref, w_ref, vn_ref, attn_ref, gl_ref,
             du_ref, dqd_ref, dkt_ref, dw_ref, dattn_ref, dgl_ref, ds_ref):
        n = pl.program_id(0)

        @pl.when(n == 0)
        def _():
            ds_ref[...] = jnp.zeros_like(ds_ref)

        def chain(cc, h):
            rows = slice(CHUNK * cc, CHUNK * cc + CHUNK)
            lanes = slice(128 * h, 128 * h + 128)
            dsn = ds_ref[h]
            st = sp_ref[cc, h]
            dsb, stb = dsn.astype(BF16), st.astype(BF16)
            dob = do_ref[rows, lanes].astype(BF16)
            vnb = vn_ref[rows, lanes].astype(BF16)
            dvn = _dot_tn(attn_ref[cc, h].astype(BF16), dob) + _dot(kt_ref[rows, lanes].astype(BF16), dsb)
            du_ref[rows, lanes] = dvn
            dqd_ref[rows, lanes] = _dot_nt(dob, stb)
            dattn_ref[cc, h] = _dot_nt(dob, vnb)
            dkt_ref[rows, lanes] = _dot_nt(vnb, dsb)
            tot = jnp.sum(jnp.sum(st * dsn, axis=1, keepdims=True), axis=0, keepdims=True)
            dgl_ref[cc, h] = jnp.broadcast_to(tot, (1, 128))
            qdo = _dot_tn(qd_ref[rows, lanes].astype(BF16), dob)
            yield
            dvb = dvn.astype(BF16)
            dw_ref[rows, lanes] = -_dot_nt(dvb, stb)
            ds_ref[h] = qdo + dsn * gl_ref[cc, h] - _dot_tn(w_ref[rows, lanes].astype(BF16), dvb)

        for cc in reversed(range(CPS_SCAN)):
            _interleave([chain(cc, h) for h in range(DN_HEADS)])

    nsteps = nc // CPS_SCAN
    big = pl.BlockSpec((CPS_SCAN * CHUNK, 512), lambda n: (nsteps - 1 - n, 0))
    sq = pl.BlockSpec((CPS_SCAN, DN_HEADS, CHUNK, CHUNK), lambda n: (nsteps - 1 - n, 0, 0, 0))
    glb = pl.BlockSpec((CPS_SCAN, DN_HEADS, 1, 128), lambda n: (nsteps - 1 - n, 0, 0, 0))
    return pl.pallas_call(
        body, grid=(nsteps,),
        in_specs=[big, pl.BlockSpec((CPS_SCAN, DN_HEADS, DN_HD, DN_HD), lambda n: (nsteps - 1 - n, 0, 0, 0)),
                  big, big, big, big, sq, glb],
        out_specs=[big, big, big, big, sq, glb],
        out_shape=[_sds((s, 512))] * 4 + [_sds((nc, DN_HEADS, CHUNK, CHUNK)), _sds((nc, DN_HEADS, 1, 128))],
        scratch_shapes=[pltpu.VMEM((DN_HEADS, DN_HD, DN_HD), F32)],
        compiler_params=_cp(("arbitrary",)), name="dn_scan_bwd")(do, sp, qd, kt, w, vn, attn, gl)


def _dn_prep_bwd(qn, kn, v, bg, bgt, tri, sel, t_inv, attn, u, w, du, dw, dqd, dkt, dattn, dgl):
    s = qn.shape[0]
    nc = s // CHUNK

    def body(q_ref, k_ref, v_ref, bg_ref, bgt_ref, tri_ref, sel_ref, t_ref, attn_ref, u_ref, w_ref,
             du_ref, dw_ref, dqd_ref, dkt_ref, dattn_ref, dgl_ref,
             dq_ref, dk_ref, dv_ref, dbg_ref):
        tri_v = tri_ref[...]
        lane = lax.broadcasted_iota(jnp.int32, (CHUNK, 128), 1)
        rowi = lax.broadcasted_iota(jnp.int32, (CHUNK, 128), 0)
        ones_b = jnp.ones((CHUNK, 128), BF16)
        ones_sq = jnp.ones((128, 128), BF16)
        parts = [[] for _ in range(CPS)]

        def chain(cc, h, bg3, gc3, gc_row):
            rows = slice(CHUNK * cc, CHUNK * cc + CHUNK)
            lanes = slice(128 * h, 128 * h + 128)
            gcc, beta, incl, strict, decay, gl = _chunk_common(bg3, gc3, gc_row, h, sel_ref)
            yield
            q = q_ref[rows, lanes]
            k = k_ref[rows, lanes]
            vv = v_ref[rows, lanes]
            ts = _split(t_ref[cc, h])
            egc = jnp.exp(gcc)
            kb = k * beta
            a_mat = jnp.where(strict, _mm_nt(kb, k) * decay, 0.0)
            dvb = _mm3_tn(ts, _split(du_ref[rows, lanes]))
            dkbg = _mm3_tn(ts, _split(dw_ref[rows, lanes]))
            yield
            d_a = jnp.where(strict, -(_mm_nt(dvb, u_ref[rows, lanes]) + _mm_nt(dkbg, w_ref[rows, lanes])), 0.0)
            d_m = d_a * decay
            dattn_m = jnp.where(incl, dattn_ref[cc, h], 0.0)
            dqk = dattn_m * decay
            e_hi, e_lo = _split(d_a * a_mat + dattn_m * attn_ref[cc, h])
            yield
            dkb = _mm(d_m, k)
            dk = _mm_tn(d_m, kb) + _mm_tn(dqk, q)
            dq = _mm(dqk, k)
            e_colsum = _dot_tn(e_hi, ones_b) + _dot_tn(e_lo, ones_b)
            e_rowsum = _dot(e_hi, ones_b) + _dot(e_lo, ones_b)
            dqd = dqd_ref[rows, lanes]
            dkt = dkt_ref[rows, lanes]
            sums = _rowsum_b(jnp.concatenate([dqd * q, dkt * k, dkbg * k, dkb * k, dvb * vv], axis=0), ones_sq)
            s_dqd, s_dkt, rk, s_dkb, s_dvb = (sums[CHUNK * n:CHUNK * n + CHUNK] for n in range(5))
            yield
            tail = jnp.exp(gl - gcc)
            r = s_dkt * tail
            dgl_tot = jnp.sum(r, axis=0, keepdims=True) + dgl_ref[cc, h] * jnp.exp(gl)
            dgc = e_rowsum - e_colsum + s_dqd * egc - r + rk * beta * egc
            dgc = dgc + jnp.where(rowi == CHUNK - 1, dgl_tot, 0.0)
            dq_ref[rows, lanes] = dq + dqd * egc
            dk_ref[rows, lanes] = dk + dkt * tail + dkbg * (beta * egc) + dkb * beta
            dv_ref[rows, lanes] = dvb * beta
            parts[cc].append((h, dgc, rk * egc + s_dkb + s_dvb))

        gens = []
        for cc in range(CPS):
            bgv = bg_ref[CHUNK * cc:CHUNK * cc + CHUNK, :]
            bg3, gc3, gc_row = _chunk_cumsum(bgv, bgt_ref[cc], tri_v)
            gens += [chain(cc, h, bg3, gc3, gc_row) for h in range(DN_HEADS)]
        _interleave(gens)
        for cc in range(CPS):
            dgc_mat = jnp.zeros((CHUNK, 128), F32)
            dbeta_mat = jnp.zeros((CHUNK, 128), F32)
            for h, dgc, dbeta in parts[cc]:
                dgc_mat = dgc_mat + jnp.where(lane == DN_HEADS + h, dgc, 0.0)
                dbeta_mat = dbeta_mat + jnp.where(lane == h, dbeta, 0.0)
            dbg_ref[CHUNK * cc:CHUNK * cc + CHUNK, :] = _mmx_tn(tri_v, dgc_mat) + dbeta_mat

    big = pl.BlockSpec((CPS * CHUNK, 512), lambda n: (n, 0))
    sq = pl.BlockSpec((CPS, DN_HEADS, CHUNK, CHUNK), lambda n: (n, 0, 0, 0))
    glb = pl.BlockSpec((CPS, DN_HEADS, 1, 128), lambda n: (n, 0, 0, 0))
    small = pl.BlockSpec((CPS * CHUNK, 128), lambda n: (n, 0))
    return pl.pallas_call(
        body, grid=(nc // CPS,),
        in_specs=[big, big, big, small, pl.BlockSpec((CPS, 8, CHUNK), lambda n: (n, 0, 0)),
                  pl.BlockSpec((CHUNK, CHUNK), lambda n: (0, 0)),
                  pl.BlockSpec((8, 128, 128), lambda n: (0, 0, 0)), sq, sq, big, big,
                  big, big, big, big, sq, glb],
        out_specs=[big, big, big, small],
        out_shape=[_sds((s, 512))] * 3 + [_sds((s, 128))],
        compiler_params=_cp(("parallel",)), name="dn_prep_bwd")(qn, kn, v, bg, bgt, tri, sel, t_inv, attn, u, w,
                                                                  du, dw, dqd, dkt, dattn, dgl)


def _dn_pre_bwd(dqn, dkn, dv, dbg, proj, conv_w8, alog_l, dtb_l):
    s = proj.shape[0]
    tr = 512
    nh = tr // 8

    def body(dq_ref, dk_ref, dv_ref, dbg_ref, u_ref, halo_ref, ba_ref, w_ref, al_ref, dt_ref,
             dy_ref, dba_ref, dsm_ref):
        i = pl.program_id(0)

        @pl.when(i == 0)
        def _():
            dsm_ref[...] = jnp.zeros_like(dsm_ref)

        keep = (i > 0).astype(F32)
        for c in range(12):
            lanes = slice(128 * c, 128 * c + 128)
            y = _conv_group(u_ref[:, lanes], halo_ref[:, lanes] * keep, w_ref, c)
            sg = _sigmoid(y)
            sv = y * sg
            if c < 8:
                rs = lax.rsqrt(jnp.sum(sv * sv, axis=1, keepdims=True) + EPS)
                n = sv * rs
                if c < 4:
                    dn = dq_ref[:, lanes] * (DN_HD ** -0.5)
                else:
                    dn = dk_ref[:, slice(128 * (c - 4), 128 * (c - 3))]
                dsv = rs * (dn - n * jnp.sum(dn * n, axis=1, keepdims=True))
            else:
                dsv = dv_ref[:, slice(128 * (c - 8), 128 * (c - 7))]
            dy_ref[:, lanes] = dsv * _silu_grad(y, sg)
        _, lane, sig_b, t, nega, g = _beta_g(ba_ref[...], al_ref[...], dt_ref[...])
        dbg = dbg_ref[...]
        da = dbg * nega * _sigmoid(t)
        is_b = lane < DN_HEADS
        is_a = jnp.logical_and(lane >= DN_HEADS, lane < 2 * DN_HEADS)
        dba_ref[...] = jnp.where(is_b, dbg * sig_b * (1.0 - sig_b), jnp.where(is_a, da, 0.0)).astype(BF16)
        d_alog = jnp.sum(jnp.where(is_a, dbg * g, 0.0), axis=0, keepdims=True)
        d_dtb = jnp.sum(jnp.where(is_a, da, 0.0), axis=0, keepdims=True)
        row = lax.broadcasted_iota(jnp.int32, (8, 128), 0)
        dsm_ref[...] = dsm_ref[...] + jnp.where(row == 0, d_alog, jnp.where(row == 1, d_dtb, 0.0))

    blk = pl.BlockSpec((tr, 512), lambda i: (i, 0))
    return pl.pallas_call(
        body, grid=(s // tr,),
        in_specs=[blk, blk, blk, pl.BlockSpec((tr, 128), lambda i: (i, 0)),
                  pl.BlockSpec((tr, 1536), lambda i: (i, 0)),
                  pl.BlockSpec((8, 1536), lambda i: (jnp.maximum(i * nh - 1, 0), 0)),
                  pl.BlockSpec((tr, 128), lambda i: (i, COL_BA_128)),
                  pl.BlockSpec((8, 1536), lambda i: (0, 0)),
                  pl.BlockSpec((1, 128), lambda i: (0, 0)), pl.BlockSpec((1, 128), lambda i: (0, 0))],
        out_specs=[pl.BlockSpec((tr, 1536), lambda i: (i, 0)), pl.BlockSpec((tr, 128), lambda i: (i, 0)),
                   pl.BlockSpec((8, 128), lambda i: (0, 0))],
        out_shape=[_sds((s, 1536)), _sds((s, 128), BF16), _sds((8, 128))],
        compiler_params=_cp(("arbitrary",)), name="dn_pre_bwd")(dqn, dkn, dv, dbg, proj, proj, proj, conv_w8,
                                                                 alog_l, dtb_l)


def _conv_bwd(dy, proj, conv_w8):
    s = dy.shape[0]
    tr = 512
    nh = tr // 8
    nblk = s // tr

    def body(dy_ref, dyn_ref, u_ref, halo_ref, w_ref, du_ref, dw_ref):
        i = pl.program_id(0)

        @pl.when(i == 0)
        def _():
            dw_ref[...] = jnp.zeros_like(dw_ref)

        keep_prev = (i > 0).astype(F32)
        keep_next = (i < nblk - 1).astype(F32)
        row = lax.broadcasted_iota(jnp.int32, (8, 128), 0)
        for c in range(12):
            lanes = slice(128 * c, 128 * c + 128)
            dyc = dy_ref[:, lanes]
            dcat = jnp.concatenate([dyc, dyn_ref[:, lanes] * keep_next], axis=0)
            xcat = jnp.concatenate([halo_ref[:, lanes] * keep_prev, u_ref[:, lanes]], axis=0)
            du = dyc * w_ref[CONV_W - 1:CONV_W, lanes]
            dwc = jnp.where(row == CONV_W - 1, jnp.sum(dyc * u_ref[:, lanes], axis=0, keepdims=True), 0.0)
            for k in range(1, CONV_W):
                du = du + pltpu.roll(dcat, tr + 8 - k, 0)[0:tr] * w_ref[CONV_W - 1 - k:CONV_W - k, lanes]
                ush = pltpu.roll(xcat, k, 0)[8:8 + tr]
                dwc = dwc + jnp.where(row == CONV_W - 1 - k, jnp.sum(dyc * ush, axis=0, keepdims=True), 0.0)
            du_ref[:, lanes] = du.astype(BF16)
            dw_ref[:, lanes] = dw_ref[:, lanes] + dwc

    return pl.pallas_call(
        body, grid=(nblk,),
        in_specs=[pl.BlockSpec((tr, 1536), lambda i: (i, 0)),
                  pl.BlockSpec((8, 1536), lambda i: (jnp.minimum((i + 1) * nh, s // 8 - 1), 0)),
                  pl.BlockSpec((tr, 1536), lambda i: (i, 0)),
                  pl.BlockSpec((8, 1536), lambda i: (jnp.maximum(i * nh - 1, 0), 0)),
                  pl.BlockSpec((8, 1536), lambda i: (0, 0))],
        out_specs=[pl.BlockSpec((tr, 1536), lambda i: (i, 0)), pl.BlockSpec((8, 1536), lambda i: (0, 0))],
        out_shape=[_sds((s, 1536), BF16), _sds((8, 1536))],
        compiler_params=_cp(("arbitrary",)), name="conv_bwd")(dy, dy, proj, proj, conv_w8)


PIECE_WIDTHS = (1536, 512, 512, 512, 512, 512, 128)


def _in_bwd_dx(pieces, w_bf, x, dy, norm_w, chip_sums=None):
    s = x.shape[0]
    tm = 512
    n_piece = len(PIECE_WIDTHS)
    n_step = s // tm
    n_sent = 0 if chip_sums is None else 3 * len(chip_sums)

    def body(*refs):
        piece_refs = refs[:n_piece]
        w_ref, x_ref, dy_ref, nw_ref = refs[n_piece:n_piece + 4]
        rest = refs[n_piece + 4:]
        i = pl.program_id(0)

        def copies():
            hb_refs = rest[:n_sent // 3]
            recv_refs = rest[n_sent // 3 + 2:2 * (n_sent // 3) + 2]
            send_sems, recv_sems = rest[-2:]
            xi, yi, ci = _position()
            out = []
            for j, (px, py) in enumerate(_other_chips(xi, yi)):
                for t, (src, dst) in enumerate(zip(hb_refs, recv_refs)):
                    k = len(hb_refs) * j + t
                    out.append(pltpu.make_async_remote_copy(
                        src_ref=src.at[2 * px + py], dst_ref=dst.at[j], send_sem=send_sems.at[k],
                        recv_sem=recv_sems.at[k], device_id=(px, py, ci), device_id_type=MESH_ID))
            return out

        dx_ref, dnw_ref = rest[n_sent // 3:n_sent // 3 + 2]

        @pl.when(i == 0)
        def _():
            dnw_ref[...] = jnp.zeros_like(dnw_ref)
            for cp in (copies() if n_sent else []):
                cp.start()

        dp = jnp.concatenate([r[...] for r in piece_refs], axis=1)
        dh = _dot(dp, w_ref[...])
        xv = x_ref[...]
        rstd = lax.rsqrt(jnp.mean(xv * xv, axis=-1, keepdims=True) + EPS)
        xh = xv * rstd
        dnw_ref[...] = dnw_ref[...] + jnp.sum(dh * xh, axis=0, keepdims=True)
        g = dh * nw_ref[...]
        dx_ref[...] = rstd * (g - xh * jnp.mean(g * xh, axis=-1, keepdims=True)) + dy_ref[...]

        if n_sent:
            @pl.when(i == n_step - 1)
            def _():
                cps = copies()
                for cp in cps:
                    cp.wait_recv()
                for cp in cps:
                    cp.wait_send()

    def blk(n):
        return pl.BlockSpec((tm, n), lambda i: (i, 0))

    in_specs = [blk(n) for n in PIECE_WIDTHS] + [pl.BlockSpec((D_IN_PAD, D_MODEL), lambda i: (0, 0)), blk(D_MODEL),
                                                blk(D_MODEL), pl.BlockSpec((1, D_MODEL), lambda i: (0, 0))]
    out_specs = [blk(D_MODEL), pl.BlockSpec((8, D_MODEL), lambda i: (0, 0))]
    out_shape = [_sds((s, D_MODEL)), _sds((8, D_MODEL))]
    scratch = []
    extra = ()
    if n_sent:
        extra = tuple(chip_sums)
        in_specs += [ANY] * len(extra)
        out_specs += [ANY] * len(extra)
        out_shape += [_sds((3,) + a.shape[1:], a.dtype) for a in extra]
        scratch = [pltpu.SemaphoreType.DMA((n_sent,)), pltpu.SemaphoreType.DMA((n_sent,))]
    return pl.pallas_call(
        body, grid=(n_step,), in_specs=in_specs, out_specs=out_specs, out_shape=out_shape, scratch_shapes=scratch,
        compiler_params=_cp(("arbitrary",)), name="in_bwd_dx")(*pieces, w_bf, x, dy, norm_w, *extra)


def _in_bwd_dw(pieces, x, norm_w):
    s = x.shape[0]
    tm = 512
    n_piece = len(PIECE_WIDTHS)

    def body(*refs):
        piece_refs = refs[:n_piece]
        x_ref, nw_ref, dw_ref = refs[n_piece:]
        i = pl.program_id(0)

        @pl.when(i == 0)
        def _():
            dw_ref[...] = jnp.zeros_like(dw_ref)

        xv = x_ref[...]
        rstd = lax.rsqrt(jnp.mean(xv * xv, axis=-1, keepdims=True) + EPS)
        h = (xv * rstd * nw_ref[...]).astype(BF16)
        at = 0
        for r, width in zip(piece_refs, PIECE_WIDTHS):
            dw_ref[at:at + width, :] = dw_ref[at:at + width, :] + _dot_tn(r[...], h)
            at += width

    return pl.pallas_call(
        body, grid=(s // tm,),
        in_specs=[pl.BlockSpec((tm, n), lambda i: (i, 0)) for n in PIECE_WIDTHS]
        + [pl.BlockSpec((tm, D_MODEL), lambda i: (i, 0)), pl.BlockSpec((1, D_MODEL), lambda i: (0, 0))],
        out_specs=pl.BlockSpec((D_IN_PAD, D_MODEL), lambda i: (0, 0)),
        out_shape=_sds((D_IN_PAD, D_MODEL)),
        compiler_params=_cp(("arbitrary",)), name="in_bwd_dw")(*pieces, x, norm_w)


def _flat(a):
    return a.reshape(-1, a.shape[-1])


def _as_pattern(a, r):
    return a if r == 1 else a.reshape(r, a.shape[0] // r, a.shape[1])


D_SHARD = D_IN // N_CHIPS
BA_START = 4 * D_DN
BA_PACKED = 4096
S1_HEAD = BA_START - D_SHARD
S1_BA = 2 * D_SHARD - BA_START


def _pack_rows(g):
    pad = jnp.zeros((D_IN_PAD - D_IN, g.shape[2]), g.dtype)
    s2_ba = 2 * DN_HEADS - S1_BA
    return jnp.concatenate([g[0][:D_SHARD], g[1][:S1_HEAD], g[2][s2_ba:D_SHARD], g[3][:D_SHARD],
                            g[1][S1_HEAD:D_SHARD], g[2][:s2_ba], pad], axis=0)


def _pack_w(g):
    n = g.shape[2]

    def body(g_ref, o_ref):
        g32 = g_ref.bitcast(jnp.uint32)
        o32 = o_ref.bitcast(jnp.uint32)
        full, head, ba1 = D_SHARD // 2, S1_HEAD // 2, S1_BA // 2
        ba2 = DN_HEADS - ba1
        pieces = [(0, 0, full), (1, 0, head), (2, ba2, full), (3, 0, full), (1, head, full), (2, 0, ba2)]
        at = 0
        for chip, lo, hi in pieces:
            o32[at:at + hi - lo, :] = g32[chip, lo:hi, :]
            at += hi - lo
        o32[at:D_IN_PAD // 2, :] = jnp.zeros((D_IN_PAD // 2 - at, n), jnp.uint32)

    vm = pl.BlockSpec(memory_space=pltpu.VMEM)
    return pl.pallas_call(body, in_specs=[vm], out_specs=vm, out_shape=_sds((D_IN_PAD, n), g.dtype),
                          compiler_params=_cp(), name="pack_w")(g)


def _unpack_w(p, rows):
    n = p.shape[1]
    tn = 256
    mid = BA_PACKED + S1_BA
    pieces = [(0, 0, (0, D_SHARD)), (1, 0, (D_SHARD, BA_START)), (1, S1_HEAD, (BA_PACKED, mid)),
              (2, 0, (mid, BA_PACKED + 2 * DN_HEADS)), (2, 2 * DN_HEADS - S1_BA, (BA_START, BA_START + S1_HEAD)),
              (3, 0, (BA_START + S1_HEAD, BA_PACKED))]

    def body(p_ref, o_ref):
        for chip, at, (lo, hi) in pieces:
            o_ref[chip, at:at + hi - lo, :] = p_ref[lo:hi, :]
        for chip in range(N_CHIPS):
            o_ref[chip, D_SHARD:rows, :] = jnp.zeros((rows - D_SHARD, tn), p.dtype)

    return pl.pallas_call(
        body, grid=(n // tn,),
        in_specs=[pl.BlockSpec((D_IN_PAD, tn), lambda j: (0, j))],
        out_specs=pl.BlockSpec((N_CHIPS, rows, tn), lambda j: (0, 0, j)),
        out_shape=_sds((N_CHIPS, rows, n), p.dtype),
        compiler_params=_cp(("parallel",)), name="unpack_w")(p)


def _unpack_rows(p, rows):
    mid = BA_PACKED + S1_BA
    pad = jnp.zeros((rows - D_SHARD, p.shape[1]), p.dtype)
    return jnp.stack([jnp.concatenate([p[0:D_SHARD], pad], axis=0),
                      jnp.concatenate([p[D_SHARD:BA_START], p[BA_PACKED:mid], pad], axis=0),
                      jnp.concatenate([p[mid:BA_PACKED + 2 * DN_HEADS], p[BA_START:BA_START + S1_HEAD], pad], axis=0),
                      jnp.concatenate([p[BA_START + S1_HEAD:BA_PACKED], pad], axis=0)])


def _lane_row(vec, offset):
    return jnp.pad(vec.reshape(1, -1), ((0, 0), (offset, 128 - offset - vec.shape[0])))


def _local_step(x, tgt, norm_w, w_bf, conv_w, a_log, dt_bias, dn_norm_w, q_norm_w, k_norm_w, rel_bias, w_out_bf):
    s = x.shape[0]
    nc = s // CHUNK
    conv_w8 = jnp.pad(conv_w, ((0, 8 - CONV_W), (0, 0)))
    alog_l = _lane_row(a_log.reshape(-1), DN_HEADS)
    dtb_l = _lane_row(dt_bias.reshape(-1), DN_HEADS)
    dnw_t = jnp.tile(dn_norm_w.reshape(1, DN_HD), (1, DN_HEADS))
    qw_t = jnp.tile(q_norm_w.reshape(1, ATT_HD), (1, ATT_HEADS))
    kw_t = jnp.tile(k_norm_w.reshape(1, ATT_HD), (1, ATT_HEADS))
    bd128 = _compact_mat(DN_HD)
    bd64 = _compact_mat(ATT_HD)
    tri = _tri_incl()
    sel = _lane_select()
    buckets = _bucket_tables()

    proj = _in_proj(x, norm_w, w_bf)
    qn, kn, v_dn, bg = _dn_pre(proj, conv_w8, alog_l, dtb_l)
    bgt = bg[:, 0:8].reshape(nc, CHUNK, 8).transpose(0, 2, 1)
    u, w, qd, kt, attn, t_inv, gl = _dn_prep(qn, kn, v_dn, bg, bgt, tri, sel)
    o_dn, vn, sp = _dn_scan(u, w, qd, kt, attn, gl)
    q1, k1, v1, q4, k4, v4, q16, k16, v16 = _att_pre(proj, qw_t, kw_t, bd64)
    rs = [r for _, r in PATTERNS]
    qkv = [(q1, k1, v1, 0), (_flat(q4), _flat(k4), _flat(v4), 0), (_flat(q16), _flat(k16), _flat(v16), 0)]
    bias = _bias_fwd(rel_bias, buckets)
    o_pats, lse_pats = [], []
    for p, r in enumerate(rs):
        o_p, lse_p = _att_fwd(*qkv[p], bias, p, r, "att_fwd_r%d" % r)
        o_pats.append(_as_pattern(o_p, r))
        lse_pats.append(_as_pattern(lse_p, r))
    mixed, o_att, l1, l4, l16 = _post_fwd(o_dn, proj, o_pats, lse_pats, dnw_t, bd128)
    dy, loss_blk = _out_fwd(x, mixed, w_out_bf, tgt)

    dmixed, d_w_out = _out_bwd(dy, mixed, w_out_bf)
    do_dn, dz, dgate, do1, do4, do16, dl1, dl4, dl16, d_dnw = _post_bwd(dmixed, o_dn, proj, o_att, dnw_t, bd128)
    side = [(do1, l1, dl1), (_flat(do4), _flat(l4), _flat(dl4)), (_flat(do16), _flat(l16), _flat(dl16))]
    dq_pats, dk_pats, dv_pats, ds_accs = [], [], [], []
    for p, r in enumerate(rs):
        dq_p, dk_p, dv_p, ds_p = _att_bwd(*qkv[p], *side[p], bias, p, r, "att_bwd_r%d" % r)
        dq_pats.append(_as_pattern(dq_p, r))
        dk_pats.append(_as_pattern(dk_p, r))
        dv_pats.append(_as_pattern(dv_p, r))
        ds_accs.append(ds_p)
    d_rel_bias = _bias_bwd(ds_accs, buckets)
    dq_att, dk_att, dv_att, d_qw, d_kw = _att_pre_bwd(dq_pats, dk_pats, dv_pats, proj, qw_t, kw_t, bd64)
    du, dqd, dkt, dw, dattn, dgl = _dn_scan_bwd(do_dn, sp, qd, kt, w, vn, attn, gl)
    dqn, dkn, dv_dn, dbg = _dn_prep_bwd(qn, kn, v_dn, bg, bgt, tri, sel, t_inv, attn, u, w, du, dw, dqd, dkt, dattn,
                                        dgl)
    dyc, dba, dsm = _dn_pre_bwd(dqn, dkn, dv_dn, dbg, proj, conv_w8, alog_l, dtb_l)
    d_qkv_dn, d_conv8 = _conv_bwd(dyc, proj, conv_w8)
    pieces = (d_qkv_dn, dz, dq_att, dk_att, dv_att, dgate, dba)
    d_w_in_t = _in_bwd_dw(pieces, x, norm_w)
    last = functools.partial(_in_bwd_dx, pieces, w_bf, x, dy, norm_w)

    grads = dict(
        w_in_t=d_w_in_t,
        conv_w=d_conv8[0:CONV_W, :],
        a_log=dsm[0:1, DN_HEADS:2 * DN_HEADS],
        dt_bias=dsm[1:2, DN_HEADS:2 * DN_HEADS],
        dn_norm_w=d_dnw[0:1, :],
        q_norm_w=d_qw[0:1, :].reshape(ATT_HEADS, ATT_HD),
        k_norm_w=d_kw[0:1, :].reshape(ATT_HEADS, ATT_HD),
        rel_bias=d_rel_bias,
        w_out=d_w_out,
    )
    return loss_blk[0, 0], last, grads


MESH_ID = pl.DeviceIdType.MESH
ANY = pl.BlockSpec(memory_space=pl.ANY)


def _position():
    return lax.axis_index("x"), lax.axis_index("y"), lax.axis_index("c")


def _other_chips(x, y):
    return [(1 - x, y), (x, 1 - y), (1 - x, 1 - y)]


SHARD_PAD = 1040
WIN = 528


def _row_split(n):
    return (n // 2) // 128 * 128


def _part(ref, cc):
    n = ref.shape[0]
    sp = _row_split(n)
    return ref.at[pl.ds(0, sp)] if cc == 0 else ref.at[pl.ds(sp, n - sp)]


def _gather_weights(wt_s, w_out_s, conv_s):
    def body(a_ref, b_ref, c_ref, ga_ref, gb_ref, gc_ref, send_sems, recv_sems, loc_sems, a_vmem, b_vmem):
        x, y, c = _position()
        me = 2 * x + y
        sib = (x, y, 1 - c)
        big = ((a_ref, ga_ref), (b_ref, gb_ref))
        stage_in = [pltpu.make_async_copy(a_ref, a_vmem, loc_sems.at[0]),
                    pltpu.make_async_copy(b_ref, b_vmem, loc_sems.at[1])]
        local = [pltpu.make_async_copy(a_vmem, ga_ref.at[me], loc_sems.at[0]),
                 pltpu.make_async_copy(b_vmem, gb_ref.at[me], loc_sems.at[1]),
                 pltpu.make_async_copy(c_ref, gc_ref.at[me], loc_sems.at[2])]
        for cp in stage_in:
            cp.start()
        local[2].start()
        for cp in stage_in:
            cp.wait()
        for cp in local[:2]:
            cp.start()
        others = _other_chips(x, y)

        def exchange(cc):
            sends = []
            for j, (px, py) in enumerate(others):
                for t, (src, dst) in enumerate(big):
                    k = 3 * j + t
                    sends.append(pltpu.make_async_remote_copy(
                        src_ref=_part(src, cc), dst_ref=_part(dst.at[me], cc), send_sem=send_sems.at[k],
                        recv_sem=recv_sems.at[k], device_id=(px, py, c), device_id_type=MESH_ID))
                sends.append(pltpu.make_async_remote_copy(
                    src_ref=c_ref, dst_ref=gc_ref.at[me], send_sem=send_sems.at[3 * j + 2],
                    recv_sem=recv_sems.at[3 * j + 2], device_id=(px, py, c), device_id_type=MESH_ID))
            for cp in sends:
                cp.start()
            for j, (px, py) in enumerate(others):
                src_chip = 2 * px + py
                for t, (src, dst) in enumerate(big):
                    landed = _part(dst.at[src_chip], cc)
                    pltpu.make_async_remote_copy(
                        src_ref=_part(src, cc), dst_ref=landed, send_sem=send_sems.at[3 * j + t],
                        recv_sem=recv_sems.at[3 * j + t], device_id=(px, py, c), device_id_type=MESH_ID).wait_recv()
                    k = 9 + 2 * j + t
                    fwd = pltpu.make_async_remote_copy(
                        src_ref=landed, dst_ref=landed, send_sem=send_sems.at[k], recv_sem=recv_sems.at[k],
                        device_id=sib, device_id_type=MESH_ID)
                    fwd.start()
                    sends.append(fwd)
                pltpu.make_async_remote_copy(
                    src_ref=c_ref, dst_ref=gc_ref.at[src_chip], send_sem=send_sems.at[3 * j + 2],
                    recv_sem=recv_sems.at[3 * j + 2], device_id=(px, py, c), device_id_type=MESH_ID).wait_recv()
            for j, (px, py) in enumerate(others):
                src_chip = 2 * px + py
                for t, (src, dst) in enumerate(big):
                    k = 9 + 2 * j + t
                    theirs = _part(dst.at[src_chip], 1 - cc)
                    pltpu.make_async_remote_copy(
                        src_ref=theirs, dst_ref=theirs, send_sem=send_sems.at[k], recv_sem=recv_sems.at[k],
                        device_id=sib, device_id_type=MESH_ID).wait_recv()
            for cp in sends:
                cp.wait_send()

        for cc in (0, 1):
            pl.when(c == cc)(functools.partial(exchange, cc))
        for cp in local:
            cp.wait()

    srcs = (wt_s, w_out_s, conv_s)
    n_sem = 9 + 6
    return pl.pallas_call(
        body, in_specs=[ANY] * 3, out_specs=[ANY] * 3,
        out_shape=[_sds((N_CHIPS,) + a.shape, a.dtype) for a in srcs],
        scratch_shapes=[pltpu.SemaphoreType.DMA((n_sem,)), pltpu.SemaphoreType.DMA((n_sem,)),
                        pltpu.SemaphoreType.DMA((3,)), pltpu.VMEM(wt_s.shape, wt_s.dtype),
                        pltpu.VMEM(w_out_s.shape, w_out_s.dtype)],
        compiler_params=_cp(), name="gather_weights")(*srcs)


N_DEV = 8
PEER_FLIPS = [(dx, dy, dc) for dx in (0, 1) for dy in (0, 1) for dc in (0, 1)][1:]


def _small_copies(s_ref, rs_ref, send_sems, recv_sems):
    x, y, c = _position()
    dev = 4 * x + 2 * y + c
    sends, recvs = [], []
    for f, (dx, dy, dc) in enumerate(PEER_FLIPS):
        peer = (x ^ dx, y ^ dy, c ^ dc)
        sends.append(pltpu.make_async_remote_copy(
            src_ref=s_ref, dst_ref=rs_ref.at[dev], send_sem=send_sems.at[f], recv_sem=recv_sems.at[f],
            device_id=peer, device_id_type=MESH_ID))
        recvs.append(pltpu.make_async_remote_copy(
            src_ref=s_ref, dst_ref=rs_ref.at[4 * peer[0] + 2 * peer[1] + peer[2]], send_sem=send_sems.at[f],
            recv_sem=recv_sems.at[f], device_id=peer, device_id_type=MESH_ID))
    return sends, recvs


def _fill_parts(h_in, r_in, h_out, r_out, small, rows_in, rows_out):
    def body(ha_ref, ra_ref, hb_ref, rb_ref, s_ref, fa_ref, fb_ref, rs_ref, send_sems, recv_sems, loc_sems,
             small_send, small_recv, a_vmem, b_vmem, ra_vmem, rb_vmem):
        x, y, c = _position()
        sib = (x, y, 1 - c)
        chip = 2 * x + y
        pairs = ((a_vmem, fa_ref), (b_vmem, fb_ref))
        own_small = pltpu.make_async_copy(s_ref, rs_ref.at[4 * x + 2 * y + c], loc_sems.at[2])
        own_small.start()
        small_sends, small_recvs = _small_copies(s_ref, rs_ref, small_send, small_recv)
        for cp in small_sends:
            cp.start()
        stage_in = [pltpu.make_async_copy(ha_ref.at[chip], a_vmem, loc_sems.at[0]),
                    pltpu.make_async_copy(hb_ref.at[chip], b_vmem, loc_sems.at[1]),
                    pltpu.make_async_copy(ra_ref, ra_vmem, loc_sems.at[3]),
                    pltpu.make_async_copy(rb_ref, rb_vmem, loc_sems.at[4])]
        for cp in stage_in:
            cp.start()
        for cp in stage_in:
            cp.wait()
        for tot, recv in ((a_vmem, ra_vmem), (b_vmem, rb_vmem)):
            acc = tot[...]
            for j in range(3):
                acc = acc + recv[j].astype(F32)
            tot[...] = acc

        def fill(cc):
            mine = [_part(dst, cc) for _, dst in pairs]
            srcs = [src.at[pl.ds(0, m.shape[0])] for src, m in zip((a_vmem, b_vmem), mine)]
            local = [pltpu.make_async_copy(s, m, loc_sems.at[t]) for t, (s, m) in enumerate(zip(srcs, mine))]
            sends = [pltpu.make_async_remote_copy(src_ref=s, dst_ref=m, send_sem=send_sems.at[t],
                                                  recv_sem=recv_sems.at[t], device_id=sib, device_id_type=MESH_ID)
                     for t, (s, m) in enumerate(zip(srcs, mine))]
            for cp in local + sends:
                cp.start()
            for t, (_, dst) in enumerate(pairs):
                theirs = _part(dst, 1 - cc)
                pltpu.make_async_remote_copy(src_ref=theirs, dst_ref=theirs, send_sem=send_sems.at[t],
                                             recv_sem=recv_sems.at[t], device_id=sib, device_id_type=MESH_ID).wait_recv()
            for cp in sends:
                cp.wait_send()
            for cp in local:
                cp.wait()

        for cc in (0, 1):
            pl.when(c == cc)(functools.partial(fill, cc))
        for cp in small_recvs:
            cp.wait_recv()
        for cp in small_sends:
            cp.wait_send()
        own_small.wait()

    n_peer = len(PEER_FLIPS)
    return pl.pallas_call(
        body, in_specs=[ANY] * 5, out_specs=[ANY] * 3,
        out_shape=[_sds((rows_in, D_MODEL)), _sds((rows_out, D_MODEL)), _sds((N_DEV,) + small.shape, small.dtype)],
        scratch_shapes=[pltpu.SemaphoreType.DMA((2,)), pltpu.SemaphoreType.DMA((2,)), pltpu.SemaphoreType.DMA((5,)),
                        pltpu.SemaphoreType.DMA((n_peer,)), pltpu.SemaphoreType.DMA((n_peer,)),
                        pltpu.VMEM(h_in.shape[1:], F32), pltpu.VMEM(h_out.shape[1:], F32),
                        pltpu.VMEM(r_in.shape, r_in.dtype), pltpu.VMEM(r_out.shape, r_out.dtype)],
        compiler_params=_cp(), name="fill_parts")(h_in, r_in, h_out, r_out, small)


def _chip_sums(u_in, u_out, win_in, win_out):
    wins = (win_in, win_out)

    def body(a_ref, b_ref, ha_ref, hba_ref, hb_ref, hbb_ref, send_sems, recv_sems, loc_sems, out_sems,
             mine_a, theirs_a, sum_a, sumb_a, mine_b, theirs_b, sum_b, sumb_b):
        x, y, c = _position()
        sib = (x, y, 1 - c)
        groups = ((a_ref, mine_a, theirs_a, sum_a, sumb_a, ha_ref, hba_ref, win_in),
                  (b_ref, mine_b, theirs_b, sum_b, sumb_b, hb_ref, hbb_ref, win_out))
        loads, sends = [], []
        for t, (src, mine, theirs, _, _, _, _, win) in enumerate(groups):
            split = _row_split(src.shape[1])
            for k in range(N_CHIPS):
                n = N_CHIPS * t + k
                loads.append(pltpu.make_async_copy(
                    src.at[k, pl.ds(pl.multiple_of(c * split, split), win), :], mine.at[k], loc_sems.at[n]))
                sends.append(pltpu.make_async_remote_copy(
                    src_ref=src.at[k, pl.ds(pl.multiple_of((1 - c) * split, split), win), :], dst_ref=theirs.at[k],
                    send_sem=send_sems.at[n], recv_sem=recv_sems.at[n], device_id=sib, device_id_type=MESH_ID))
        for cp in sends + loads:
            cp.start()
        stores = []
        for t, (_, mine, theirs, tot, totb, h_out, hb_out, _) in enumerate(groups):
            for k in range(N_CHIPS):
                n = N_CHIPS * t + k
                loads[n].wait()
                sends[n].wait_recv()
                val = mine[k] + theirs[k]
                tot[k] = val
                totb[k] = val.astype(BF16)
                stores += [pltpu.make_async_copy(tot.at[k], h_out.at[k], out_sems.at[2 * n]),
                           pltpu.make_async_copy(totb.at[k], hb_out.at[k], out_sems.at[2 * n + 1])]
                stores[-2].start()
                stores[-1].start()
        for cp in sends:
            cp.wait_send()
        for cp in stores:
            cp.wait()

    shapes = [(N_CHIPS, w, D_MODEL) for w in wins]
    n_cp = 2 * N_CHIPS
    vmem = []
    for shp in shapes:
        vmem += [pltpu.VMEM(shp, F32), pltpu.VMEM(shp, F32), pltpu.VMEM(shp, F32), pltpu.VMEM(shp, BF16)]
    return pl.pallas_call(
        body, in_specs=[ANY] * 2, out_specs=[ANY] * 4,
        out_shape=[_sds(shapes[0]), _sds(shapes[0], BF16), _sds(shapes[1]), _sds(shapes[1], BF16)],
        scratch_shapes=[pltpu.SemaphoreType.DMA((n_cp,)), pltpu.SemaphoreType.DMA((n_cp,)),
                        pltpu.SemaphoreType.DMA((n_cp,)), pltpu.SemaphoreType.DMA((2 * n_cp,))] + vmem,
        compiler_params=_cp(), name="chip_sums")(u_in, u_out)


SMALL_LAYOUT = (("norm_w", 1024), ("conv_w", 6144), ("a_log", 128), ("dt_bias", 128), ("dn_norm_w", 128),
                ("q_norm_w", 512), ("k_norm_w", 512), ("rel_bias", 256), ("loss", 128))
SMALL_TOTAL = sum(n for _, n in SMALL_LAYOUT)


def _small_offset(name):
    off = 0
    for n, size in SMALL_LAYOUT:
        if n == name:
            return off
        off += size
    raise KeyError(name)


def _pack_small(grads):
    parts = []
    for name, size in SMALL_LAYOUT:
        flat = grads[name].reshape(1, -1)
        parts.append(jnp.pad(flat, ((0, 0), (0, size - flat.shape[1]))))
    return jnp.concatenate(parts, axis=1)


def _sum_small(rows):
    n_dev = rows.shape[0]
    q_off = _small_offset("q_norm_w")
    k_off = _small_offset("k_norm_w")

    def body(r_ref, tot_ref, qk_ref):
        tot = r_ref[0:1, :]
        for d in range(1, n_dev):
            tot = tot + r_ref[d:d + 1, :]
        tot_ref[...] = tot
        for row, off in ((0, q_off), (1, k_off)):
            s4 = tot[:, off:off + 128] + tot[:, off + 128:off + 256] + tot[:, off + 256:off + 384] \
                + tot[:, off + 384:off + 512]
            qk_ref[row:row + 1, :] = s4 + pltpu.roll(s4, ATT_HD, 1)

    return pl.pallas_call(
        body, in_specs=[pl.BlockSpec(memory_space=pltpu.VMEM)],
        out_specs=[pl.BlockSpec(memory_space=pltpu.VMEM)] * 2,
        out_shape=[_sds((1, SMALL_TOTAL)), _sds((2, 128))],
        compiler_params=_cp(), name="sum_small")(rows)


def _adamw_math(w, g, m, v):
    m = ADAM_B1 * m + (1.0 - ADAM_B1) * g
    v = ADAM_B2 * v + (1.0 - ADAM_B2) * (g * g)
    m_hat = m / (1.0 - ADAM_B1 ** ADAM_STEP)
    v_hat = v / (1.0 - ADAM_B2 ** ADAM_STEP)
    delta = -ADAM_LR * (m_hat / (jnp.sqrt(v_hat) + ADAM_EPS) + ADAM_WD * w)
    return delta, m, v


def _adamw_big(g, w, m, v, name):
    rows, cols = w.shape
    tr = 128

    def body(g_ref, w_ref, m_ref, v_ref, go_ref, d_ref, nm_ref, nv_ref):
        g = g_ref[...]
        go_ref[...] = g
        d_ref[...], nm_ref[...], nv_ref[...] = _adamw_math(w_ref[...], g, m_ref[...], v_ref[...])

    blk = pl.BlockSpec((tr, cols), lambda i: (i, 0))
    return pl.pallas_call(
        body, grid=(pl.cdiv(rows, tr),), in_specs=[blk] * 4, out_specs=[blk] * 4,
        out_shape=[_sds((rows, cols))] * 4, compiler_params=_cp(("parallel",)), name=name)(g, w, m, v)


def _adamw_rows(g, w, m, v, name):
    rows, cols = w.shape
    tr = 128

    def body(g_ref, w_ref, m_ref, v_ref, go_ref, d_ref, nm_ref, nv_ref, s_g, s_d, s_m, s_v):
        g = g_ref[...]
        s_g[...] = g
        s_d[...], s_m[...], s_v[...] = _adamw_math(w_ref[...], g, m_ref[...], v_ref[...])
        for i in range(tr):
            for scr, out in ((s_g, go_ref), (s_d, d_ref), (s_m, nm_ref), (s_v, nv_ref)):
                out[i] = scr[i:i + 1, :]

    blk = pl.BlockSpec((tr, cols), lambda i: (i, 0))
    oblk = pl.BlockSpec((tr, 1, cols), lambda i: (i, 0, 0))
    return pl.pallas_call(
        body, grid=(pl.cdiv(rows, tr),), in_specs=[blk] * 4, out_specs=[oblk] * 4,
        out_shape=[_sds((rows, 1, cols))] * 4, scratch_shapes=[pltpu.VMEM((tr, cols), F32)] * 4,
        compiler_params=_cp(("parallel",)), name=name)(g, w, m, v)


def _adamw_small(w, g, m, v, name):
    def body(w_ref, g_ref, m_ref, v_ref, d_ref, nm_ref, nv_ref):
        d_ref[...], nm_ref[...], nv_ref[...] = _adamw_math(w_ref[...], g_ref[...], m_ref[...], v_ref[...])

    vm = pl.BlockSpec(memory_space=pltpu.VMEM)
    return pl.pallas_call(body, in_specs=[vm] * 4, out_specs=[vm] * 3, out_shape=[_sds(w.shape)] * 3,
                          compiler_params=_cp(), name=name)(w, g, m, v)


WEIGHTS = ("norm_w", "w_in", "conv_w", "a_log", "dt_bias", "dn_norm_w", "q_norm_w", "k_norm_w", "rel_bias", "w_out")


def kernel(x, norm_w, w_in, conv_w, a_log, dt_bias, dn_norm_w, q_norm_w, k_norm_w, rel_bias, w_out, loss_target, m_norm_w, m_w_in, m_conv_w, m_a_log, m_dt_bias, m_dn_norm_w, m_q_norm_w, m_k_norm_w, m_rel_bias, m_w_out, v_norm_w, v_w_in, v_conv_w, v_a_log, v_dt_bias, v_dn_norm_w, v_q_norm_w, v_k_norm_w, v_rel_bias, v_w_out):
    xi, yi, _ = _position()
    chip = 2 * xi + yi
    w_loc = dict(norm_w=norm_w, w_in=w_in[0].T, conv_w=conv_w[0], a_log=a_log, dt_bias=dt_bias, dn_norm_w=dn_norm_w,
                 q_norm_w=q_norm_w, k_norm_w=k_norm_w, rel_bias=rel_bias, w_out=w_out[0])
    m_loc = dict(norm_w=m_norm_w, w_in=m_w_in[0].T, conv_w=m_conv_w[0], a_log=m_a_log, dt_bias=m_dt_bias,
                 dn_norm_w=m_dn_norm_w, q_norm_w=m_q_norm_w, k_norm_w=m_k_norm_w, rel_bias=m_rel_bias,
                 w_out=m_w_out[0])
    v_loc = dict(norm_w=v_norm_w, w_in=v_w_in[0].T, conv_w=v_conv_w[0], a_log=v_a_log, dt_bias=v_dt_bias,
                 dn_norm_w=v_dn_norm_w, q_norm_w=v_q_norm_w, k_norm_w=v_k_norm_w, rel_bias=v_rel_bias,
                 w_out=v_w_out[0])

    wt_pad = jnp.pad(w_loc["w_in"].astype(BF16), ((0, SHARD_PAD - D_SHARD), (0, 0)))
    g_in, g_out, g_conv = _gather_weights(wt_pad, w_loc["w_out"].astype(BF16), w_loc["conv_w"])
    wt_full = _pack_w(g_in)
    w_out_full = g_out.reshape(D_MODEL, D_MODEL)
    conv_full = g_conv.transpose(1, 0, 2).reshape(CONV_W, 3 * D_DN)

    loss_local, last_kernel, grads = _local_step(x[0], loss_target[0], norm_w, wt_full, conv_full, a_log, dt_bias,
                                                 dn_norm_w, q_norm_w, k_norm_w, rel_bias, w_out_full)
    grads["loss"] = loss_local

    u_in = _unpack_w(grads["w_in_t"], SHARD_PAD)
    u_out = grads["w_out"].reshape(N_CHIPS, D_MODEL // N_CHIPS, D_MODEL)
    win_out = u_out.shape[1] // 2
    h_in, hb_in, h_out, hb_out = _chip_sums(u_in, u_out, WIN, win_out)
    grad_x, d_nw8, r_in, r_out = last_kernel(chip_sums=(hb_in, hb_out))
    grads["norm_w"] = d_nw8[0:1, :]
    small = _pack_small(grads)
    full_in, full_out, r_small = _fill_parts(h_in, r_in, h_out, r_out, small, SHARD_PAD, u_out.shape[1])
    tot_small, qk = _sum_small(r_small.reshape(8, SMALL_TOTAL))

    def small_grad(name, n):
        off = _small_offset(name)
        return tot_small[:, off:off + n]

    loss = small_grad("loss", 1).reshape(())
    conv_all = small_grad("conv_w", CONV_W * 3 * D_DN).reshape(CONV_W, 3 * D_DN)
    g_small = dict(
        norm_w=small_grad("norm_w", D_MODEL),
        conv_w=lax.dynamic_slice_in_dim(conv_all, chip * (3 * D_DN // N_CHIPS), 3 * D_DN // N_CHIPS, axis=1),
        a_log=small_grad("a_log", DN_HEADS),
        dt_bias=small_grad("dt_bias", DN_HEADS),
        dn_norm_w=small_grad("dn_norm_w", DN_HD),
        q_norm_w=qk[0:1, 0:ATT_HD],
        k_norm_w=qk[1:2, 0:ATT_HD],
        rel_bias=small_grad("rel_bias", ATT_HEADS * N_BUCKETS).reshape(ATT_HEADS, N_BUCKETS),
    )

    out_g, out_d, out_m, out_v = {}, {}, {}, {}
    out_g["w_in"], out_d["w_in"], out_m["w_in"], out_v["w_in"] = _adamw_rows(
        full_in, w_loc["w_in"], m_loc["w_in"], v_loc["w_in"], "adamw_w_in")
    out_g["w_out"], out_d["w_out"], out_m["w_out"], out_v["w_out"] = _adamw_big(
        full_out, w_loc["w_out"], m_loc["w_out"], v_loc["w_out"], "adamw_w_out")
    for name in g_small:
        out_g[name] = g_small[name]
        out_d[name], out_m[name], out_v[name] = _adamw_small(w_loc[name], g_small[name], m_loc[name], v_loc[name],
                                                             "adamw_" + name)
    for d in (out_g, out_d, out_m, out_v):
        d["w_in"] = d["w_in"].transpose(1, 2, 0)
        for name in ("conv_w", "w_out"):
            d[name] = d[name][None]
    return (loss, grad_x[None], *[out_g[n] for n in WEIGHTS], *[out_d[n] for n in WEIGHTS],
            *[out_m[n] for n in WEIGHTS], *[out_v[n] for n in WEIGHTS])
```

```python
import functools
import math

import numpy as np
import jax
import jax.numpy as jnp
from jax import lax
from jax.experimental import pallas as pl
from jax.experimental.pallas import tpu as pltpu

F32 = jnp.float32
BF16 = jnp.bfloat16
HI = lax.Precision.HIGHEST

D_MODEL = 1024
D_DN = 512
DN_HEADS = 4
DN_HD = 128
CONV_W = 4
CHUNK = 64
D_ATT = 512
ATT_HEADS = 8
ATT_HD = 64
PATTERNS = ((128, 1), (512, 4), (2048, 16))
N_BUCKETS = 32
MAX_DISTANCE = 2048
D_IN = 4 * D_DN + 2 * DN_HEADS + 4 * D_ATT
D_IN_PAD = 4224
EPS = 1e-6
BLK = 128
NEG = -1e30
N_CHIPS = 4

ADAM_LR = 0.001
ADAM_B1 = 0.9
ADAM_B2 = 0.999
ADAM_EPS = 1e-08
ADAM_WD = 0.01
ADAM_STEP = 10

VMEM_LIMIT = 56 * 1024 * 1024

COL_Z = 3
COL_ATT_Q = 4
COL_ATT_K = 5
COL_ATT_V = 6
COL_GATE = 7
COL_BA_128 = 32


def _cp(sem=None):
    if sem is None:
        return pltpu.CompilerParams(vmem_limit_bytes=VMEM_LIMIT)
    return pltpu.CompilerParams(dimension_semantics=sem, vmem_limit_bytes=VMEM_LIMIT)


def _sds(shape, dtype=F32):
    return jax.ShapeDtypeStruct(shape, dtype)


def _mm(a, b):
    return jnp.dot(a.astype(BF16), b.astype(BF16), preferred_element_type=F32)


def _mm_nt(a, b):
    return lax.dot_general(a.astype(BF16), b.astype(BF16), (((1,), (1,)), ((), ())),
                           preferred_element_type=F32)


def _mm_tn(a, b):
    return lax.dot_general(a.astype(BF16), b.astype(BF16), (((0,), (0,)), ((), ())),
                           preferred_element_type=F32)


def _mmx(a, b):
    return jnp.dot(a, b, precision=HI, preferred_element_type=F32)


def _mmx_nt(a, b):
    return lax.dot_general(a, b, (((1,), (1,)), ((), ())), precision=HI, preferred_element_type=F32)


def _mmx_tn(a, b):
    return lax.dot_general(a, b, (((0,), (0,)), ((), ())), precision=HI, preferred_element_type=F32)


def _dot(a, b):
    return jnp.dot(a, b, preferred_element_type=F32)


def _dot_nt(a, b):
    return lax.dot_general(a, b, (((1,), (1,)), ((), ())), preferred_element_type=F32)


def _dot_tn(a, b):
    return lax.dot_general(a, b, (((0,), (0,)), ((), ())), preferred_element_type=F32)


def _split(a):
    hi = a.astype(BF16)
    return hi, (a - hi.astype(F32)).astype(BF16)


def _mm3(a_s, b_s):
    return _dot(a_s[0], b_s[0]) + _dot(a_s[0], b_s[1]) + _dot(a_s[1], b_s[0])


def _mm3_tn(a_s, b_s):
    return _dot_tn(a_s[0], b_s[0]) + _dot_tn(a_s[0], b_s[1]) + _dot_tn(a_s[1], b_s[0])


def _interleave(gens):
    live = list(gens)
    while live:
        nxt = []
        for g in live:
            try:
                next(g)
                nxt.append(g)
            except StopIteration:
                pass
        live = nxt


def _segsum(x, bd):
    hi = x.astype(BF16)
    r1 = x - hi.astype(F32)
    mid = r1.astype(BF16)
    lo = (r1 - mid.astype(F32)).astype(BF16)
    return (jnp.dot(hi, bd, preferred_element_type=F32) + jnp.dot(mid, bd, preferred_element_type=F32)
            + jnp.dot(lo, bd, preferred_element_type=F32))


def _compact_mat(seg, n=512):
    slot = 128 * seg // n
    src = np.arange(n)[:, None]
    dst = np.arange(128)[None, :]
    return jnp.asarray((src // seg == dst // slot).astype(np.float32), dtype=BF16)


def _seg_mean(x, cm, seg):
    hi, lo = _split(x)
    return (_dot(hi, cm) + _dot(lo, cm)) * (1.0 / seg)


def _seg_expand(c, cm, seg):
    hi, lo = _split(c)
    return (_dot_nt(hi, cm) + _dot_nt(lo, cm)) * (cm.shape[0] / (128.0 * seg))


def _seg_rstd(x, cm, seg):
    return _seg_expand(lax.rsqrt(_seg_mean(x * x, cm, seg) + EPS), cm, seg)


def _sigmoid(x):
    return 1.0 / (1.0 + jnp.exp(-x))


def _silu_grad(x, s):
    return s * (1.0 + x * (1.0 - s))


def _tri_incl():
    i = np.arange(CHUNK)
    return jnp.asarray((i[:, None] >= i[None, :]).astype(np.float32))


def _t5_bucket(dist):
    max_exact = N_BUCKETS // 2
    d = np.maximum(dist, 1).astype(np.float64)
    large = max_exact + (np.log(d / max_exact) / math.log(MAX_DISTANCE / max_exact)
                         * (N_BUCKETS - max_exact)).astype(np.int32)
    large = np.minimum(large, N_BUCKETS - 1)
    return np.where(dist < max_exact, dist, large).astype(np.int32)


def _bucket_tables():
    qi = np.arange(BLK)[:, None]
    kj = np.arange(2 * BLK)[None, :]
    step = qi - kj + BLK
    return jnp.asarray(np.stack([_t5_bucket(np.clip(step, 0, None) * r) for _, r in PATTERNS]))


def _in_proj(x, norm_w, wt_bf):
    s = x.shape[0]
    tm = 512

    def body(x_ref, nw_ref, w_ref, o_ref):
        xv = x_ref[...]
        rstd = lax.rsqrt(jnp.mean(xv * xv, axis=-1, keepdims=True) + EPS)
        h = (xv * rstd * nw_ref[...]).astype(BF16)
        o_ref[...] = _dot_nt(h, w_ref[...])

    return pl.pallas_call(
        body, grid=(s // tm,),
        in_specs=[pl.BlockSpec((tm, D_MODEL), lambda i: (i, 0)),
                  pl.BlockSpec((1, D_MODEL), lambda i: (0, 0)),
                  pl.BlockSpec((D_IN_PAD, D_MODEL), lambda i: (0, 0))],
        out_specs=pl.BlockSpec((tm, D_IN_PAD), lambda i: (i, 0)),
        out_shape=_sds((s, D_IN_PAD)), compiler_params=_cp(("parallel",)), name="in_proj")(x, norm_w, wt_bf)


def _conv_group(cur, halo, w_ref, c):
    rows = cur.shape[0]
    lanes = slice(128 * c, 128 * c + 128)
    xcat = jnp.concatenate([halo, cur], axis=0)
    y = cur * w_ref[CONV_W - 1:CONV_W, lanes]
    for k in range(1, CONV_W):
        sh = pltpu.roll(xcat, k, 0)[8:8 + rows]
        y = y + sh * w_ref[CONV_W - 1 - k:CONV_W - k, lanes]
    return y


def _beta_g(ba, alog_l, dtb_l):
    lane = lax.broadcasted_iota(jnp.int32, ba.shape, 1)
    sig_b = _sigmoid(ba)
    t = ba + dtb_l
    softplus = jnp.maximum(t, 0.0) + jnp.log(1.0 + jnp.exp(-jnp.abs(t)))
    nega = -jnp.exp(alog_l)
    g = nega * softplus
    out = jnp.where(lane < DN_HEADS, sig_b, jnp.where(lane < 2 * DN_HEADS, g, 0.0))
    return out, lane, sig_b, t, nega, g


def _dn_pre(proj, conv_w8, alog_l, dtb_l):
    s = proj.shape[0]
    tr = 512
    nh = tr // 8

    def body(u_ref, halo_ref, ba_ref, w_ref, al_ref, dt_ref, q_ref, k_ref, v_ref, bg_ref):
        i = pl.program_id(0)
        keep = (i > 0).astype(F32)
        for c in range(12):
            lanes = slice(128 * c, 128 * c + 128)
            y = _conv_group(u_ref[:, lanes], halo_ref[:, lanes] * keep, w_ref, c)
            sv = y * _sigmoid(y)
            if c < 8:
                rs = lax.rsqrt(jnp.sum(sv * sv, axis=1, keepdims=True) + EPS)
                n = sv * rs
                if c < 4:
                    q_ref[:, lanes] = n * (DN_HD ** -0.5)
                else:
                    k_ref[:, slice(128 * (c - 4), 128 * (c - 3))] = n
            else:
                v_ref[:, slice(128 * (c - 8), 128 * (c - 7))] = sv
        bg_ref[...] = _beta_g(ba_ref[...], al_ref[...], dt_ref[...])[0]

    return pl.pallas_call(
        body, grid=(s // tr,),
        in_specs=[pl.BlockSpec((tr, 1536), lambda i: (i, 0)),
                  pl.BlockSpec((8, 1536), lambda i: (jnp.maximum(i * nh - 1, 0), 0)),
                  pl.BlockSpec((tr, 128), lambda i: (i, COL_BA_128)),
                  pl.BlockSpec((8, 1536), lambda i: (0, 0)),
                  pl.BlockSpec((1, 128), lambda i: (0, 0)),
                  pl.BlockSpec((1, 128), lambda i: (0, 0))],
        out_specs=[pl.BlockSpec((tr, 512), lambda i: (i, 0))] * 3 + [pl.BlockSpec((tr, 128), lambda i: (i, 0))],
        out_shape=[_sds((s, 512))] * 3 + [_sds((s, 128))],
        compiler_params=_cp(("parallel",)), name="dn_pre")(proj, proj, proj, conv_w8, alog_l, dtb_l)


CPS = 4
CPS_SCAN = 8
SPLIT_ITERS = 2


def _split3(a):
    hi = a.astype(BF16)
    r1 = a - hi.astype(F32)
    mid = r1.astype(BF16)
    return hi, mid, (r1 - mid.astype(F32)).astype(BF16)


def _lane_select():
    r = np.arange(128)
    return jnp.asarray((r[None, :, None] == np.arange(8)[:, None, None]) * np.ones((1, 1, 128)), dtype=BF16)


def _lane_bcast(a3, sel):
    return _dot(a3[0], sel) + _dot(a3[1], sel) + _dot(a3[2], sel)


def _rowsum_b(z, ones_b):
    hi, lo = _split(z)
    return _dot(hi, ones_b) + _dot(lo, ones_b)


def _chunk_cumsum(bg, bgt, tri):
    return _split3(bg), _split3(_mmx(tri, bg)), _mmx_nt(bgt, tri)


def _chunk_common(bg3, gc3, gc_row, h, sel_ref):
    gcc = _lane_bcast(gc3, sel_ref[DN_HEADS + h])
    beta = _dot(bg3[0], sel_ref[h]) + _dot(bg3[1], sel_ref[h])
    gcr = gc_row[DN_HEADS + h:DN_HEADS + h + 1, :]
    ii = lax.broadcasted_iota(jnp.int32, (CHUNK, CHUNK), 0)
    jj = lax.broadcasted_iota(jnp.int32, (CHUNK, CHUNK), 1)
    incl = ii >= jj
    strict = ii > jj
    decay = jnp.exp(jnp.where(incl, gcc[:, 0:CHUNK] - gcr, NEG))
    gl = gcc[CHUNK - 1:CHUNK, :]
    return gcc, beta, incl, strict, decay, gl


def _dn_prep(qn, kn, v, bg, bgt, tri, sel):
    s = qn.shape[0]
    nc = s // CHUNK

    def body(q_ref, k_ref, v_ref, bg_ref, bgt_ref, tri_ref, sel_ref,
             u_ref, w_ref, qd_ref, kt_ref, attn_ref, t_ref, gl_ref):
        tri_v = tri_ref[...]
        ii = lax.broadcasted_iota(jnp.int32, (CHUNK, CHUNK), 0)
        jj = lax.broadcasted_iota(jnp.int32, (CHUNK, CHUNK), 1)
        eye = (ii == jj).astype(F32)

        def chain(cc, h, bg3, gc3, gc_row):
            rows = slice(CHUNK * cc, CHUNK * cc + CHUNK)
            lanes = slice(128 * h, 128 * h + 128)
            gcc, beta, incl, strict, decay, gl = _chunk_common(bg3, gc3, gc_row, h, sel_ref)
            yield
            q = q_ref[rows, lanes]
            k = k_ref[rows, lanes]
            vv = v_ref[rows, lanes]
            kb = k * beta
            egc = jnp.exp(gcc)
            a_mat = jnp.where(strict, _mm_nt(kb, k) * decay, 0.0)
            attn_ref[cc, h] = jnp.where(incl, _mm_nt(q, k) * decay, 0.0)
            qd_ref[rows, lanes] = (q * egc).astype(BF16)
            kt_ref[rows, lanes] = (k * jnp.exp(gl - gcc)).astype(BF16)
            gl_ref[cc, h] = jnp.exp(gl)
            yield
            p = -a_mat
            t = eye + p
            for it in range(5):
                if it < SPLIT_ITERS:
                    ps = _split(p)
                    p = _mm3(ps, ps)
                    yield
                    t = t + _mm3(_split(t), _split(p))
                else:
                    p = _mm(p, p)
                    yield
                    t = t + _mm(t, p)
                yield
            t_ref[cc, h] = t
            ts = _split(t)
            u_ref[rows, lanes] = _mm3(ts, _split(vv * beta))
            w_ref[rows, lanes] = _mm3(ts, _split(kb * egc)).astype(BF16)

        gens = []
        for cc in range(CPS):
            bgv = bg_ref[CHUNK * cc:CHUNK * cc + CHUNK, :]
            bg3, gc3, gc_row = _chunk_cumsum(bgv, bgt_ref[cc], tri_v)
            gens += [chain(cc, h, bg3, gc3, gc_row) for h in range(DN_HEADS)]
        _interleave(gens)

    rows_step = CPS * CHUNK
    big = pl.BlockSpec((rows_step, 512), lambda n: (n, 0))
    sq = pl.BlockSpec((CPS, DN_HEADS, CHUNK, CHUNK), lambda n: (n, 0, 0, 0))
    return pl.pallas_call(
        body, grid=(nc // CPS,),
        in_specs=[big, big, big, pl.BlockSpec((rows_step, 128), lambda n: (n, 0)),
                  pl.BlockSpec((CPS, 8, CHUNK), lambda n: (n, 0, 0)),
                  pl.BlockSpec((CHUNK, CHUNK), lambda n: (0, 0)),
                  pl.BlockSpec((8, 128, 128), lambda n: (0, 0, 0))],
        out_specs=[big, big, big, big, sq, sq, pl.BlockSpec((CPS, DN_HEADS, 1, 128), lambda n: (n, 0, 0, 0))],
        out_shape=[_sds((s, 512))] + [_sds((s, 512), BF16)] * 3 + [_sds((nc, DN_HEADS, CHUNK, CHUNK))] * 2
        + [_sds((nc, DN_HEADS, 1, 128))],
        compiler_params=_cp(("parallel",)), name="dn_prep")(qn, kn, v, bg, bgt, tri, sel)


def _dn_scan(u, w, qd, kt, attn, gl):
    s = u.shape[0]
    nc = s // CHUNK

    def body(u_ref, w_ref, qd_ref, kt_ref, attn_ref, gl_ref, o_ref, vn_ref, sp_ref, st_ref):
        n = pl.program_id(0)

        @pl.when(n == 0)
        def _():
            st_ref[...] = jnp.zeros_like(st_ref)

        def chain(cc, h):
            rows = slice(CHUNK * cc, CHUNK * cc + CHUNK)
            lanes = slice(128 * h, 128 * h + 128)
            st = st_ref[h]
            sp_ref[cc, h] = st
            stb = st.astype(BF16)
            ws = _dot(w_ref[rows, lanes].astype(BF16), stb)
            qs = _dot(qd_ref[rows, lanes].astype(BF16), stb)
            yield
            vn = u_ref[rows, lanes] - ws
            vnb = vn.astype(BF16)
            vn_ref[rows, lanes] = vnb
            o_ref[rows, lanes] = qs + _dot(attn_ref[cc, h].astype(BF16), vnb)
            st_ref[h] = st * gl_ref[cc, h] + _dot_tn(kt_ref[rows, lanes].astype(BF16), vnb)

        for cc in range(CPS_SCAN):
            _interleave([chain(cc, h) for h in range(DN_HEADS)])

    big = pl.BlockSpec((CPS_SCAN * CHUNK, 512), lambda n: (n, 0))
    return pl.pallas_call(
        body, grid=(nc // CPS_SCAN,),
        in_specs=[big, big, big, big,
                  pl.BlockSpec((CPS_SCAN, DN_HEADS, CHUNK, CHUNK), lambda n: (n, 0, 0, 0)),
                  pl.BlockSpec((CPS_SCAN, DN_HEADS, 1, 128), lambda n: (n, 0, 0, 0))],
        out_specs=[big, big, pl.BlockSpec((CPS_SCAN, DN_HEADS, DN_HD, DN_HD), lambda n: (n, 0, 0, 0))],
        out_shape=[_sds((s, 512)), _sds((s, 512), BF16), _sds((nc, DN_HEADS, DN_HD, DN_HD))],
        scratch_shapes=[pltpu.VMEM((DN_HEADS, DN_HD, DN_HD), F32)],
        compiler_params=_cp(("arbitrary",)), name="dn_scan")(u, w, qd, kt, attn, gl)


R4 = PATTERNS[1][1]
R16 = PATTERNS[2][1]
TM = 512


def _pattern_spec(r, width=512):
    return pl.BlockSpec((r, TM // r, width), lambda i: (0, i, 0))


def _pattern_shape(s, r, dtype=F32, width=512):
    return _sds((r, s // r, width), dtype)


SLABS = pltpu.VMEM((4, TM, 128), F32)

HEAD_SLOT = 128 // ATT_HEADS


def _head_expand():
    src = np.arange(128)[:, None]
    dst = np.arange(512)[None, :]
    return jnp.asarray((src == (dst // ATT_HD) * HEAD_SLOT).astype(np.float32), dtype=BF16)


def _head_compact():
    src = np.arange(512)[:, None]
    dst = np.arange(128)[None, :]
    return jnp.asarray((src // ATT_HD == dst // HEAD_SLOT).astype(np.float32), dtype=BF16)


def _to_patterns(val, dsts, scr):
    for c in range(val.shape[1] // 128):
        lanes = slice(128 * c, 128 * c + 128)
        scr[c] = val[:, lanes]
        for dst_ref, r in dsts:
            for a in range(r):
                dst_ref[a, :, lanes] = scr[c, pl.ds(a, TM // r, stride=r), :].astype(dst_ref.dtype)


def _from_pattern(src_ref, r, scr):
    n_slab = src_ref.shape[2] // 128
    for c in range(n_slab):
        for a in range(r):
            scr[c, pl.ds(a, TM // r, stride=r), :] = src_ref[a, :, 128 * c:128 * c + 128].astype(F32)
    return jnp.concatenate([scr[c] for c in range(n_slab)], axis=1) if n_slab > 1 else scr[0]


def _att_pre(proj, qw_t, kw_t, bd64):
    s = proj.shape[0]

    def body(q_ref, k_ref, v_ref, qw_ref, kw_ref, bd_ref,
             q1_ref, k1_ref, v1_ref, q4_ref, k4_ref, v4_ref, q16_ref, k16_ref, v16_ref, scr):
        bd = bd_ref[...]
        q = q_ref[...]
        k = k_ref[...]
        qn = q * _seg_rstd(q, bd, ATT_HD) * qw_ref[...] * (ATT_HD ** -0.5)
        kn = k * _seg_rstd(k, bd, ATT_HD) * kw_ref[...]
        q1_ref[...] = qn.astype(BF16)
        k1_ref[...] = kn.astype(BF16)
        v1_ref[...] = v_ref[...].astype(BF16)
        _to_patterns(qn, ((q4_ref, R4), (q16_ref, R16)), scr)
        _to_patterns(kn, ((k4_ref, R4), (k16_ref, R16)), scr)
        _to_patterns(v_ref[...], ((v4_ref, R4), (v16_ref, R16)), scr)

    row = pl.BlockSpec((1, 512), lambda i: (0, 0))
    tok = pl.BlockSpec((TM, 512), lambda i: (i, 0))
    return pl.pallas_call(
        body, grid=(s // TM,),
        in_specs=[pl.BlockSpec((TM, 512), lambda i: (i, COL_ATT_Q)),
                  pl.BlockSpec((TM, 512), lambda i: (i, COL_ATT_K)),
                  pl.BlockSpec((TM, 512), lambda i: (i, COL_ATT_V)),
                  row, row, pl.BlockSpec((512, 128), lambda i: (0, 0))],
        out_specs=[tok] * 3 + [_pattern_spec(R4)] * 3 + [_pattern_spec(R16)] * 3,
        out_shape=[_sds((s, 512), BF16)] * 3 + [_pattern_shape(s, R4, BF16)] * 3 + [_pattern_shape(s, R16, BF16)] * 3,
        scratch_shapes=[SLABS],
        compiler_params=_cp(("parallel",)), name="att_pre")(proj, proj, proj, qw_t, kw_t, bd64)


def _bias_table(rb_ref, bk_ref, o_ref):
    for p in range(len(PATTERNS)):
        bk = bk_ref[p]
        for h in range(ATT_HEADS):
            acc = jnp.zeros((BLK, 2 * BLK), F32)
            for b in range(N_BUCKETS):
                acc = jnp.where(bk == b, rb_ref[h, b], acc)
            o_ref[p, h] = acc


def _bias_grad(ds_refs, bk_ref, o_ref):
    for h in range(ATT_HEADS):
        for b in range(N_BUCKETS):
            tot = jnp.zeros((), F32)
            for p, ds_ref in enumerate(ds_refs):
                tot = tot + jnp.sum(jnp.where(bk_ref[p] == b, ds_ref[h], 0.0))
            o_ref[h, b] = tot


QB_FWD = 2
QB_BWD = 4


def _att_masks(has_prev):
    qi = lax.broadcasted_iota(jnp.int32, (BLK, BLK), 0)
    kj = lax.broadcasted_iota(jnp.int32, (BLK, BLK), 1)
    lane = lax.broadcasted_iota(jnp.int32, (BLK, 2 * ATT_HD), 1)
    return jnp.logical_and(kj >= qi, has_prev), kj <= qi, lane < ATT_HD


def _head_lanes(h):
    half = h % 2
    return slice(ATT_HD * h, ATT_HD * h + ATT_HD), slice(ATT_HD * half, ATT_HD * half + ATT_HD)


def _att_scores(qm, kp2, kc2, bias_h, mask_prev, mask_cur):
    s_prev = jnp.where(mask_prev, _dot_nt(qm, kp2) + bias_h[:, :BLK], NEG)
    s_cur = jnp.where(mask_cur, _dot_nt(qm, kc2) + bias_h[:, BLK:], NEG)
    return s_prev, s_cur


def _att_fwd(q, k, v, v_col, bias, p_idx, r, name):
    QB = QB_FWD
    s = q.shape[0]
    nblk = s // BLK
    nseq = nblk // r

    def body(q_ref, kp_ref, kc_ref, vp_ref, vc_ref, b_ref, o_ref, lse_ref):
        j = pl.program_id(0)

        def head(h, rows, masks, q2, kp2, kc2, vp2, vc2):
            mask_prev, mask_cur, lo_half = masks
            out_l, pair_l = _head_lanes(h)
            sel = lo_half if h % 2 == 0 else jnp.logical_not(lo_half)
            qm = jnp.where(sel, q2, jnp.zeros_like(q2))
            s_prev, s_cur = _att_scores(qm, kp2, kc2, b_ref[0, h], mask_prev, mask_cur)
            yield
            m = jnp.maximum(jnp.max(s_prev, axis=1, keepdims=True), jnp.max(s_cur, axis=1, keepdims=True))
            p_prev = jnp.exp(s_prev - m)
            p_cur = jnp.exp(s_cur - m)
            l = jnp.sum(p_prev, axis=1, keepdims=True) + jnp.sum(p_cur, axis=1, keepdims=True)
            yield
            o2 = _dot(p_prev.astype(BF16), vp2) + _dot(p_cur.astype(BF16), vc2)
            o_ref[rows, out_l] = (o2 * (1.0 / l))[:, pair_l].astype(BF16)
            lse_ref[rows, HEAD_SLOT * h:HEAD_SLOT * h + HEAD_SLOT] = jnp.broadcast_to(m + jnp.log(l), (BLK, HEAD_SLOT))

        for sub in range(QB):
            rows = slice(BLK * sub, BLK * sub + BLK)
            before = slice(BLK * (sub - 1), BLK * sub)
            masks = _att_masks(((QB * j + sub) % nseq) != 0)
            gens = []
            for pp in range(ATT_HEADS // 2):
                lanes = slice(128 * pp, 128 * pp + 128)
                kp = kp_ref[:, lanes] if sub == 0 else kc_ref[before, lanes]
                vp = vp_ref[:, lanes] if sub == 0 else vc_ref[before, lanes]
                slabs = (q_ref[rows, lanes], kp, kc_ref[rows, lanes], vp, vc_ref[rows, lanes])
                gens += [head(2 * pp, rows, masks, *slabs), head(2 * pp + 1, rows, masks, *slabs)]
            _interleave(gens)

    cur = pl.BlockSpec((QB * BLK, 512), lambda j: (j, 0))
    prev = pl.BlockSpec((BLK, 512), lambda j: (jnp.maximum(QB * j - 1, 0), 0))
    vcur = pl.BlockSpec((QB * BLK, 512), lambda j: (j, v_col))
    vprev = pl.BlockSpec((BLK, 512), lambda j: (jnp.maximum(QB * j - 1, 0), v_col))
    return pl.pallas_call(
        body, grid=(nblk // QB,),
        in_specs=[cur, prev, cur, vprev, vcur,
                  pl.BlockSpec((1, ATT_HEADS, BLK, 2 * BLK), lambda j: (p_idx, 0, 0, 0))],
        out_specs=[cur, pl.BlockSpec((QB * BLK, 128), lambda j: (j, 0))],
        out_shape=[_sds((s, 512), BF16), _sds((s, 128))],
        compiler_params=_cp(("parallel",)), name=name)(q, k, k, v, v, bias)


def _post_fwd(o_dn, proj, o_pats, lse_pats, dnw_t, bd128):
    s = o_dn.shape[0]

    def body(o_ref, z_ref, gate_ref, o1_ref, o4_ref, o16_ref, s1_ref, s4_ref, s16_ref, wn_ref, bd_ref, ex_ref,
             mixed_ref, oatt_ref, l1_ref, l4_ref, l16_ref, scr_a, scr_b, scr_c, scr_d):
        o = o_ref[...]
        z = z_ref[...]
        rstd = _seg_rstd(o, bd_ref[...], DN_HD)
        y_dn = o * rstd * wn_ref[...] * (z * _sigmoid(z))
        mixed_ref[:, 0:512] = y_dn.astype(BF16)
        lses = (s1_ref[...], _from_pattern(s4_ref, R4, scr_a), _from_pattern(s16_ref, R16, scr_b))
        m = jnp.maximum(jnp.maximum(lses[0], lses[1]), lses[2])
        tot = jnp.exp(lses[0] - m) + jnp.exp(lses[1] - m) + jnp.exp(lses[2] - m)
        big_l = m + jnp.log(tot)
        l1_ref[...] = big_l
        _to_patterns(big_l, ((l4_ref, R4), (l16_ref, R16)), scr_a)
        ex = ex_ref[...]
        outs = (o1_ref[...], _from_pattern(o4_ref, R4, scr_c), _from_pattern(o16_ref, R16, scr_d))
        acc = jnp.zeros_like(o)
        for lse_p, o_p in zip(lses, outs):
            acc = acc + _lane_bcast(_split3(jnp.exp(lse_p - big_l)), ex) * o_p
        gate = gate_ref[...]
        oatt_ref[...] = acc
        mixed_ref[:, 512:1024] = (acc * (gate * _sigmoid(gate))).astype(BF16)

    blk = pl.BlockSpec((TM, 512), lambda i: (i, 0))
    cblk = pl.BlockSpec((TM, 128), lambda i: (i, 0))
    p4, p16 = _pattern_spec(R4), _pattern_spec(R16)
    c4, c16 = _pattern_spec(R4, 128), _pattern_spec(R16, 128)
    return pl.pallas_call(
        body, grid=(s // TM,),
        in_specs=[blk, pl.BlockSpec((TM, 512), lambda i: (i, COL_Z)),
                  pl.BlockSpec((TM, 512), lambda i: (i, COL_GATE)), blk, p4, p16, cblk, c4, c16,
                  pl.BlockSpec((1, 512), lambda i: (0, 0)), pl.BlockSpec((512, 128), lambda i: (0, 0)),
                  pl.BlockSpec((128, 512), lambda i: (0, 0))],
        out_specs=[pl.BlockSpec((TM, D_MODEL), lambda i: (i, 0)), blk, cblk, c4, c16],
        out_shape=[_sds((s, D_MODEL), BF16), _sds((s, 512)), _sds((s, 128)), _pattern_shape(s, R4, F32, 128),
                   _pattern_shape(s, R16, F32, 128)],
        scratch_shapes=[SLABS] * 4,
        compiler_params=_cp(("parallel",)), name="post_fwd")(o_dn, proj, proj, *o_pats, *lse_pats, dnw_t, bd128,
                                                              _head_expand())


def _out_fwd(x, mixed, w_out_bf, tgt):
    s = x.shape[0]
    tm = 512

    def body(x_ref, m_ref, w_ref, t_ref, dy_ref, loss_ref):
        i = pl.program_id(0)

        @pl.when(i == 0)
        def _():
            loss_ref[...] = jnp.zeros_like(loss_ref)

        y = x_ref[...] + jnp.dot(m_ref[...], w_ref[...], preferred_element_type=F32)
        err = y - t_ref[...]
        dy_ref[...] = err * (1.0 / D_MODEL)
        part = 0.5 * jnp.sum(jnp.mean(err * err, axis=-1, keepdims=True), axis=0, keepdims=True)
        loss_ref[...] = loss_ref[...] + part

    blk = pl.BlockSpec((tm, D_MODEL), lambda i: (i, 0))
    return pl.pallas_call(
        body, grid=(s // tm,),
        in_specs=[blk, blk, pl.BlockSpec((D_MODEL, D_MODEL), lambda i: (0, 0)), blk],
        out_specs=[blk, pl.BlockSpec((8, 128), lambda i: (0, 0))],
        out_shape=[_sds((s, D_MODEL)), _sds((8, 128))],
        compiler_params=_cp(("arbitrary",)), name="out_fwd")(x, mixed, w_out_bf, tgt)


def _out_bwd(dy, mixed, w_out_bf):
    s = dy.shape[0]
    tm = 512

    def body(dy_ref, m_ref, w_ref, dm_ref, dw_ref):
        i = pl.program_id(0)

        @pl.when(i == 0)
        def _():
            dw_ref[...] = jnp.zeros_like(dw_ref)

        dyb = dy_ref[...].astype(BF16)
        dm_ref[...] = lax.dot_general(dyb, w_ref[...], (((1,), (1,)), ((), ())), preferred_element_type=F32)
        dw_ref[...] = dw_ref[...] + lax.dot_general(m_ref[...], dyb, (((0,), (0,)), ((), ())),
                                                    preferred_element_type=F32)

    blk = pl.BlockSpec((tm, D_MODEL), lambda i: (i, 0))
    full = pl.BlockSpec((D_MODEL, D_MODEL), lambda i: (0, 0))
    return pl.pallas_call(
        body, grid=(s // tm,), in_specs=[blk, blk, full], out_specs=[blk, full],
        out_shape=[_sds((s, D_MODEL)), _sds((D_MODEL, D_MODEL))],
        compiler_params=_cp(("arbitrary",)), name="out_bwd")(dy, mixed, w_out_bf)


def _post_bwd(dmixed, o_dn, proj, o_att, dnw_t, bd128):
    s = o_dn.shape[0]
    tm = TM

    def body(ddn_ref, datt_ref, o_ref, z_ref, gate_ref, oatt_ref, wn_ref, bd128_ref, hc_ref,
             do_ref, dz_ref, dgate_ref, doatt_ref, do4_ref, do16_ref, delta_ref, dl4_ref, dl16_ref, dnw_ref, scr):
        i = pl.program_id(0)

        @pl.when(i == 0)
        def _():
            dnw_ref[...] = jnp.zeros_like(dnw_ref)

        bd128v = bd128_ref[...]
        o = o_ref[...]
        z = z_ref[...]
        wn = wn_ref[...]
        dy = ddn_ref[...]
        rstd = _seg_rstd(o, bd128v, DN_HD)
        nrm = o * rstd
        sz = _sigmoid(z)
        dz_ref[...] = (dy * nrm * wn * _silu_grad(z, sz)).astype(BF16)
        dn = dy * z * sz
        gw = dn * wn
        do_ref[...] = rstd * (gw - nrm * _seg_expand(_seg_mean(gw * nrm, bd128v, DN_HD), bd128v, DN_HD))
        colsum = jnp.sum(dn * nrm, axis=0, keepdims=True)
        fold = colsum[:, 0:128] + colsum[:, 128:256] + colsum[:, 256:384] + colsum[:, 384:512]
        dnw_ref[...] = dnw_ref[...] + fold
        dya = datt_ref[...]
        gate = gate_ref[...]
        oatt = oatt_ref[...]
        sg = _sigmoid(gate)
        dgate_ref[...] = (dya * oatt * _silu_grad(gate, sg)).astype(BF16)
        doa = dya * gate * sg
        doatt_ref[...] = doa.astype(BF16)
        delta = _segsum(doa * oatt, hc_ref[...])
        delta_ref[...] = delta
        _to_patterns(doa, ((do4_ref, R4), (do16_ref, R16)), scr)
        _to_patterns(delta, ((dl4_ref, R4), (dl16_ref, R16)), scr)

    blk = pl.BlockSpec((tm, 512), lambda i: (i, 0))
    cblk = pl.BlockSpec((tm, 128), lambda i: (i, 0))
    p4, p16 = _pattern_spec(R4), _pattern_spec(R16)
    c4, c16 = _pattern_spec(R4, 128), _pattern_spec(R16, 128)
    return pl.pallas_call(
        body, grid=(s // tm,),
        in_specs=[blk, pl.BlockSpec((tm, 512), lambda i: (i, 1)), blk,
                  pl.BlockSpec((tm, 512), lambda i: (i, COL_Z)), pl.BlockSpec((tm, 512), lambda i: (i, COL_GATE)),
                  blk, pl.BlockSpec((1, 512), lambda i: (0, 0)), pl.BlockSpec((512, 128), lambda i: (0, 0)),
                  pl.BlockSpec((512, 128), lambda i: (0, 0))],
        out_specs=[blk, blk, blk, blk, p4, p16, cblk, c4, c16, pl.BlockSpec((8, 128), lambda i: (0, 0))],
        out_shape=[_sds((s, 512))] + [_sds((s, 512), BF16)] * 3 + [_pattern_shape(s, R4, BF16),
                                          _pattern_shape(s, R16, BF16), _sds((s, 128)),
                                          _pattern_shape(s, R4, F32, 128), _pattern_shape(s, R16, F32, 128),
                                          _sds((8, 128))],
        scratch_shapes=[SLABS],
        compiler_params=_cp(("arbitrary",)), name="post_bwd")(dmixed, dmixed, o_dn, proj, proj, o_att, dnw_t,
                                                               bd128, _head_compact())


def _att_bwd(q, k, v, v_col, do, big_l, delta, bias, p_idx, r, name):
    QB = QB_BWD
    s = q.shape[0]
    nblk = s // BLK
    nseq = nblk // r
    nstep = nblk // QB

    def body(q_ref, kp_ref, kc_ref, vp_ref, vc_ref, do_ref, l_ref, dl_ref, b_ref,
             dq_ref, dk_ref, dv_ref, ds_ref, dkc_ref, dvc_ref):
        j = pl.program_id(0)

        @pl.when(j == 0)
        def _():
            dkc_ref[...] = jnp.zeros_like(dkc_ref)
            dvc_ref[...] = jnp.zeros_like(dvc_ref)
            ds_ref[...] = jnp.zeros_like(ds_ref)

        @pl.when(j < nstep)
        def _():
            def head(h, sub, masks, q2, do2, kp2, kc2, vp2, vc2):
                mask_prev, mask_cur, lo_half = masks
                rows = slice(BLK * sub, BLK * sub + BLK)
                out_l, pair_l = _head_lanes(h)
                sel = lo_half if h % 2 == 0 else jnp.logical_not(lo_half)
                qm = jnp.where(sel, q2, jnp.zeros_like(q2))
                dom = jnp.where(sel, do2, jnp.zeros_like(do2))
                s_prev, s_cur = _att_scores(qm, kp2, kc2, b_ref[0, h], mask_prev, mask_cur)
                dp_prev = _dot_nt(dom, vp2)
                dp_cur = _dot_nt(dom, vc2)
                yield
                lh = l_ref[rows, HEAD_SLOT * h:HEAD_SLOT * h + 1]
                dh = dl_ref[rows, HEAD_SLOT * h:HEAD_SLOT * h + 1]
                p_prev = jnp.exp(s_prev - lh)
                p_cur = jnp.exp(s_cur - lh)
                ds_prev = p_prev * (dp_prev - dh)
                ds_cur = p_cur * (dp_cur - dh)
                ds_ref[h, :, 0:BLK] = ds_ref[h, :, 0:BLK] + ds_prev
                ds_ref[h, :, BLK:2 * BLK] = ds_ref[h, :, BLK:2 * BLK] + ds_cur
                dsb_prev, dsb_cur = ds_prev.astype(BF16), ds_cur.astype(BF16)
                pb_prev, pb_cur = p_prev.astype(BF16), p_cur.astype(BF16)
                yield
                dq_ref[rows, out_l] = (_dot(dsb_prev, kp2) + _dot(dsb_cur, kc2))[:, pair_l].astype(BF16)
                dk_prev = _dot_tn(dsb_prev, q2)[:, pair_l]
                dv_prev = _dot_tn(pb_prev, do2)[:, pair_l]
                if sub == 0:
                    last = slice(BLK * (QB - 1), BLK * QB)
                    dk_ref[last, out_l] = (dkc_ref[last, out_l] + dk_prev).astype(BF16)
                    dv_ref[last, out_l] = (dvc_ref[last, out_l] + dv_prev).astype(BF16)
                else:
                    before = slice(BLK * (sub - 1), BLK * sub)
                    dkc_ref[before, out_l] = dkc_ref[before, out_l] + dk_prev
                    dvc_ref[before, out_l] = dvc_ref[before, out_l] + dv_prev
                yield
                dkc_ref[rows, out_l] = _dot_tn(dsb_cur, q2)[:, pair_l]
                dvc_ref[rows, out_l] = _dot_tn(pb_cur, do2)[:, pair_l]

            done = slice(0, BLK * (QB - 1))
            dk_ref[done, :] = dkc_ref[done, :].astype(BF16)
            dv_ref[done, :] = dvc_ref[done, :].astype(BF16)
            for sub in range(QB):
                rows = slice(BLK * sub, BLK * sub + BLK)
                before = slice(BLK * (sub - 1), BLK * sub)
                masks = _att_masks(((QB * j + sub) % nseq) != 0)
                gens = []
                for pp in range(ATT_HEADS // 2):
                    lanes = slice(128 * pp, 128 * pp + 128)
                    kp = kp_ref[:, lanes] if sub == 0 else kc_ref[before, lanes]
                    vp = vp_ref[:, lanes] if sub == 0 else vc_ref[before, lanes]
                    slabs = (q_ref[rows, lanes], do_ref[rows, lanes], kp, kc_ref[rows, lanes], vp,
                             vc_ref[rows, lanes])
                    gens += [head(2 * pp, sub, masks, *slabs), head(2 * pp + 1, sub, masks, *slabs)]
                _interleave(gens)

        @pl.when(j == nstep)
        def _():
            dk_ref[...] = dkc_ref[...].astype(BF16)
            dv_ref[...] = dvc_ref[...].astype(BF16)

    last_step = nstep - 1
    cur = pl.BlockSpec((QB * BLK, 512), lambda j: (jnp.minimum(j, last_step), 0))
    compact = pl.BlockSpec((QB * BLK, 128), lambda j: (jnp.minimum(j, last_step), 0))
    lag = pl.BlockSpec((QB * BLK, 512), lambda j: (jnp.clip(j - 1, 0, last_step), 0))
    prev = pl.BlockSpec((BLK, 512), lambda j: (jnp.clip(QB * j - 1, 0, nblk - 1), 0))
    vcur = pl.BlockSpec((QB * BLK, 512), lambda j: (jnp.minimum(j, last_step), v_col))
    vprev = pl.BlockSpec((BLK, 512), lambda j: (jnp.clip(QB * j - 1, 0, nblk - 1), v_col))
    return pl.pallas_call(
        body, grid=(nstep + 1,),
        in_specs=[cur, prev, cur, vprev, vcur, cur, compact, compact,
                  pl.BlockSpec((1, ATT_HEADS, BLK, 2 * BLK), lambda j: (p_idx, 0, 0, 0))],
        out_specs=[cur, lag, lag, pl.BlockSpec((ATT_HEADS, BLK, 2 * BLK), lambda j: (0, 0, 0))],
        out_shape=[_sds((s, 512), BF16)] * 3 + [_sds((ATT_HEADS, BLK, 2 * BLK))],
        scratch_shapes=[pltpu.VMEM((QB * BLK, 512), F32), pltpu.VMEM((QB * BLK, 512), F32)],
        compiler_params=_cp(("arbitrary",)), name=name)(q, k, k, v, v, do, big_l, delta, bias)


def _att_pre_bwd(dq_pats, dk_pats, dv_pats, proj, qw_t, kw_t, bd64):
    s = proj.shape[0]
    tm = TM

    def body(dq1_ref, dq4_ref, dq16_ref, dk1_ref, dk4_ref, dk16_ref, dv1_ref, dv4_ref, dv16_ref,
             q_ref, k_ref, qw_ref, kw_ref, bd_ref,
             dqr_ref, dkr_ref, dvr_ref, dqw_ref, dkw_ref, scr4, scr16):
        i = pl.program_id(0)

        @pl.when(i == 0)
        def _():
            dqw_ref[...] = jnp.zeros_like(dqw_ref)
            dkw_ref[...] = jnp.zeros_like(dkw_ref)

        bd = bd_ref[...]

        def total(d1_ref, d4_ref, d16_ref):
            return d1_ref[...] + _from_pattern(d4_ref, R4, scr4) + _from_pattern(d16_ref, R16, scr16)

        def one(d_refs, x_ref, w_ref, scale, dx_ref, dw_ref):
            dy = total(*d_refs) * scale
            x = x_ref[...]
            rstd = _seg_rstd(x, bd, ATT_HD)
            nrm = x * rstd
            dw_ref[...] = dw_ref[...] + jnp.sum(dy * nrm, axis=0, keepdims=True)
            g = dy * w_ref[...]
            dx_ref[...] = (rstd * (g - nrm * _seg_expand(_seg_mean(g * nrm, bd, ATT_HD), bd, ATT_HD))).astype(BF16)

        one((dq1_ref, dq4_ref, dq16_ref), q_ref, qw_ref, ATT_HD ** -0.5, dqr_ref, dqw_ref)
        one((dk1_ref, dk4_ref, dk16_ref), k_ref, kw_ref, 1.0, dkr_ref, dkw_ref)
        dvr_ref[...] = total(dv1_ref, dv4_ref, dv16_ref).astype(BF16)

    blk = pl.BlockSpec((tm, 512), lambda i: (i, 0))
    pats = [blk, _pattern_spec(R4), _pattern_spec(R16)]
    row = pl.BlockSpec((1, 512), lambda i: (0, 0))
    acc = pl.BlockSpec((8, 512), lambda i: (0, 0))
    return pl.pallas_call(
        body, grid=(s // tm,),
        in_specs=pats * 3 + [pl.BlockSpec((tm, 512), lambda i: (i, COL_ATT_Q)),
                             pl.BlockSpec((tm, 512), lambda i: (i, COL_ATT_K)), row, row,
                             pl.BlockSpec((512, 128), lambda i: (0, 0))],
        out_specs=[blk, blk, blk, acc, acc],
        out_shape=[_sds((s, 512), BF16)] * 3 + [_sds((8, 512))] * 2,
        scratch_shapes=[SLABS] * 2,
        compiler_params=_cp(("arbitrary",)), name="att_pre_bwd")(*dq_pats, *dk_pats, *dv_pats, proj, proj,
                                                                  qw_t, kw_t, bd64)


def _dn_scan_bwd(do, sp, qd, kt, w, vn, attn, gl):
    s = do.shape[0]
    nc = s // CHUNK

    def body(do_ref, sp_ref, qd_ref, kt_ref, w_ref, vn_ref, attn_ref, gl_ref,
             du_ref, dqd_ref, dkt_ref, dw_ref, dattn_ref, dgl_ref, ds_ref):
        n = pl.program_id(0)

        @pl.when(n == 0)
        def _():
            ds_ref[...] = jnp.zeros_like(ds_ref)

        def chain(cc, h):
            rows = slice(CHUNK * cc, CHUNK * cc + CHUNK)
            lanes = slice(128 * h, 128 * h + 128)
            dsn = ds_ref[h]
            st = sp_ref[cc, h]
            dsb, stb = dsn.astype(BF16), st.astype(BF16)
            dob = do_ref[rows, lanes].astype(BF16)
            vnb = vn_ref[rows, lanes].astype(BF16)
            dvn = _dot_tn(attn_ref[cc, h].astype(BF16), dob) + _dot(kt_ref[rows, lanes].astype(BF16), dsb)
            du_ref[rows, lanes] = dvn
            dqd_ref[rows, lanes] = _dot_nt(dob, stb)
            dattn_ref[cc, h] = _dot_nt(dob, vnb)
            dkt_ref[rows, lanes] = _dot_nt(vnb, dsb)
            tot = jnp.sum(jnp.sum(st * dsn, axis=1, keepdims=True), axis=0, keepdims=True)
            dgl_ref[cc, h] = jnp.broadcast_to(tot, (1, 128))
            qdo = _dot_tn(qd_ref[rows, lanes].astype(BF16), dob)
            yield
            dvb = dvn.astype(BF16)
            dw_ref[rows, lanes] = -_dot_nt(dvb, stb)
            ds_ref[h] = qdo + dsn * gl_ref[cc, h] - _dot_tn(w_ref[rows, lanes].astype(BF16), dvb)

        for cc in reversed(range(CPS_SCAN)):
            _interleave([chain(cc, h) for h in range(DN_HEADS)])

    nsteps = nc // CPS_SCAN
    big = pl.BlockSpec((CPS_SCAN * CHUNK, 512), lambda n: (nsteps - 1 - n, 0))
    sq = pl.BlockSpec((CPS_SCAN, DN_HEADS, CHUNK, CHUNK), lambda n: (nsteps - 1 - n, 0, 0, 0))
    glb = pl.BlockSpec((CPS_SCAN, DN_HEADS, 1, 128), lambda n: (nsteps - 1 - n, 0, 0, 0))
    return pl.pallas_call(
        body, grid=(nsteps,),
        in_specs=[big, pl.BlockSpec((CPS_SCAN, DN_HEADS, DN_HD, DN_HD), lambda n: (nsteps - 1 - n, 0, 0, 0)),
                  big, big, big, big, sq, glb],
        out_specs=[big, big, big, big, sq, glb],
        out_shape=[_sds((s, 512))] * 4 + [_sds((nc, DN_HEADS, CHUNK, CHUNK)), _sds((nc, DN_HEADS, 1, 128))],
        scratch_shapes=[pltpu.VMEM((DN_HEADS, DN_HD, DN_HD), F32)],
        compiler_params=_cp(("arbitrary",)), name="dn_scan_bwd")(do, sp, qd, kt, w, vn, attn, gl)


def _dn_prep_bwd(qn, kn, v, bg, bgt, tri, sel, t_inv, attn, u, w, du, dw, dqd, dkt, dattn, dgl):
    s = qn.shape[0]
    nc = s // CHUNK

    def body(q_ref, k_ref, v_ref, bg_ref, bgt_ref, tri_ref, sel_ref, t_ref, attn_ref, u_ref, w_ref,
             du_ref, dw_ref, dqd_ref, dkt_ref, dattn_ref, dgl_ref,
             dq_ref, dk_ref, dv_ref, dbg_ref):
        tri_v = tri_ref[...]
        lane = lax.broadcasted_iota(jnp.int32, (CHUNK, 128), 1)
        rowi = lax.broadcasted_iota(jnp.int32, (CHUNK, 128), 0)
        ones_b = jnp.ones((CHUNK, 128), BF16)
        ones_sq = jnp.ones((128, 128), BF16)
        parts = [[] for _ in range(CPS)]

        def chain(cc, h, bg3, gc3, gc_row):
            rows = slice(CHUNK * cc, CHUNK * cc + CHUNK)
            lanes = slice(128 * h, 128 * h + 128)
            gcc, beta, incl, strict, decay, gl = _chunk_common(bg3, gc3, gc_row, h, sel_ref)
            yield
            q = q_ref[rows, lanes]
            k = k_ref[rows, lanes]
            vv = v_ref[rows, lanes]
            ts = _split(t_ref[cc, h])
            egc = jnp.exp(gcc)
            kb = k * beta
            a_mat = jnp.where(strict, _mm_nt(kb, k) * decay, 0.0)
            dvb = _mm3_tn(ts, _split(du_ref[rows, lanes]))
            dkbg = _mm3_tn(ts, _split(dw_ref[rows, lanes]))
            yield
            d_a = jnp.where(strict, -(_mm_nt(dvb, u_ref[rows, lanes]) + _mm_nt(dkbg, w_ref[rows, lanes])), 0.0)
            d_m = d_a * decay
            dattn_m = jnp.where(incl, dattn_ref[cc, h], 0.0)
            dqk = dattn_m * decay
            e_hi, e_lo = _split(d_a * a_mat + dattn_m * attn_ref[cc, h])
            yield
            dkb = _mm(d_m, k)
            dk = _mm_tn(d_m, kb) + _mm_tn(dqk, q)
            dq = _mm(dqk, k)
            e_colsum = _dot_tn(e_hi, ones_b) + _dot_tn(e_lo, ones_b)
            e_rowsum = _dot(e_hi, ones_b) + _dot(e_lo, ones_b)
            dqd = dqd_ref[rows, lanes]
            dkt = dkt_ref[rows, lanes]
            sums = _rowsum_b(jnp.concatenate([dqd * q, dkt * k, dkbg * k, dkb * k, dvb * vv], axis=0), ones_sq)
            s_dqd, s_dkt, rk, s_dkb, s_dvb = (sums[CHUNK * n:CHUNK * n + CHUNK] for n in range(5))
            yield
            tail = jnp.exp(gl - gcc)
            r = s_dkt * tail
            dgl_tot = jnp.sum(r, axis=0, keepdims=True) + dgl_ref[cc, h] * jnp.exp(gl)
            dgc = e_rowsum - e_colsum + s_dqd * egc - r + rk * beta * egc
            dgc = dgc + jnp.where(rowi == CHUNK - 1, dgl_tot, 0.0)
            dq_ref[rows, lanes] = dq + dqd * egc
            dk_ref[rows, lanes] = dk + dkt * tail + dkbg * (beta * egc) + dkb * beta
            dv_ref[rows, lanes] = dvb * beta
            parts[cc].append((h, dgc, rk * egc + s_dkb + s_dvb))

        gens = []
        for cc in range(CPS):
            bgv = bg_ref[CHUNK * cc:CHUNK * cc + CHUNK, :]
            bg3, gc3, gc_row = _chunk_cumsum(bgv, bgt_ref[cc], tri_v)
            gens += [chain(cc, h, bg3, gc3, gc_row) for h in range(DN_HEADS)]
        _interleave(gens)
        for cc in range(CPS):
            dgc_mat = jnp.zeros((CHUNK, 128), F32)
            dbeta_mat = jnp.zeros((CHUNK, 128), F32)
            for h, dgc, dbeta in parts[cc]:
                dgc_mat = dgc_mat + jnp.where(lane == DN_HEADS + h, dgc, 0.0)
                dbeta_mat = dbeta_mat + jnp.where(lane == h, dbeta, 0.0)
            dbg_ref[CHUNK * cc:CHUNK * cc + CHUNK, :] = _mmx_tn(tri_v, dgc_mat) + dbeta_mat

    big = pl.BlockSpec((CPS * CHUNK, 512), lambda n: (n, 0))
    sq = pl.BlockSpec((CPS, DN_HEADS, CHUNK, CHUNK), lambda n: (n, 0, 0, 0))
    glb = pl.BlockSpec((CPS, DN_HEADS, 1, 128), lambda n: (n, 0, 0, 0))
    small = pl.BlockSpec((CPS * CHUNK, 128), lambda n: (n, 0))
    return pl.pallas_call(
        body, grid=(nc // CPS,),
        in_specs=[big, big, big, small, pl.BlockSpec((CPS, 8, CHUNK), lambda n: (n, 0, 0)),
                  pl.BlockSpec((CHUNK, CHUNK), lambda n: (0, 0)),
                  pl.BlockSpec((8, 128, 128), lambda n: (0, 0, 0)), sq, sq, big, big,
                  big, big, big, big, sq, glb],
        out_specs=[big, big, big, small],
        out_shape=[_sds((s, 512))] * 3 + [_sds((s, 128))],
        compiler_params=_cp(("parallel",)), name="dn_prep_bwd")(qn, kn, v, bg, bgt, tri, sel, t_inv, attn, u, w,
                                                                  du, dw, dqd, dkt, dattn, dgl)


def _dn_pre_bwd(dqn, dkn, dv, dbg, proj, conv_w8, alog_l, dtb_l):
    s = proj.shape[0]
    tr = 512
    nh = tr // 8

    def body(dq_ref, dk_ref, dv_ref, dbg_ref, u_ref, halo_ref, ba_ref, w_ref, al_ref, dt_ref,
             dy_ref, dba_ref, dsm_ref):
        i = pl.program_id(0)

        @pl.when(i == 0)
        def _():
            dsm_ref[...] = jnp.zeros_like(dsm_ref)

        keep = (i > 0).astype(F32)
        for c in range(12):
            lanes = slice(128 * c, 128 * c + 128)
            y = _conv_group(u_ref[:, lanes], halo_ref[:, lanes] * keep, w_ref, c)
            sg = _sigmoid(y)
            sv = y * sg
            if c < 8:
                rs = lax.rsqrt(jnp.sum(sv * sv, axis=1, keepdims=True) + EPS)
                n = sv * rs
                if c < 4:
                    dn = dq_ref[:, lanes] * (DN_HD ** -0.5)
                else:
                    dn = dk_ref[:, slice(128 * (c - 4), 128 * (c - 3))]
                dsv = rs * (dn - n * jnp.sum(dn * n, axis=1, keepdims=True))
            else:
                dsv = dv_ref[:, slice(128 * (c - 8), 128 * (c - 7))]
            dy_ref[:, lanes] = dsv * _silu_grad(y, sg)
        _, lane, sig_b, t, nega, g = _beta_g(ba_ref[...], al_ref[...], dt_ref[...])
        dbg = dbg_ref[...]
        da = dbg * nega * _sigmoid(t)
        is_b = lane < DN_HEADS
        is_a = jnp.logical_and(lane >= DN_HEADS, lane < 2 * DN_HEADS)
        dba_ref[...] = jnp.where(is_b, dbg * sig_b * (1.0 - sig_b), jnp.where(is_a, da, 0.0)).astype(BF16)
        d_alog = jnp.sum(jnp.where(is_a, dbg * g, 0.0), axis=0, keepdims=True)
        d_dtb = jnp.sum(jnp.where(is_a, da, 0.0), axis=0, keepdims=True)
        row = lax.broadcasted_iota(jnp.int32, (8, 128), 0)
        dsm_ref[...] = dsm_ref[...] + jnp.where(row == 0, d_alog, jnp.where(row == 1, d_dtb, 0.0))

    blk = pl.BlockSpec((tr, 512), lambda i: (i, 0))
    return pl.pallas_call(
        body, grid=(s // tr,),
        in_specs=[blk, blk, blk, pl.BlockSpec((tr, 128), lambda i: (i, 0)),
                  pl.BlockSpec((tr, 1536), lambda i: (i, 0)),
                  pl.BlockSpec((8, 1536), lambda i: (jnp.maximum(i * nh - 1, 0), 0)),
                  pl.BlockSpec((tr, 128), lambda i: (i, COL_BA_128)),
                  pl.BlockSpec((8, 1536), lambda i: (0, 0)),
                  pl.BlockSpec((1, 128), lambda i: (0, 0)), pl.BlockSpec((1, 128), lambda i: (0, 0))],
        out_specs=[pl.BlockSpec((tr, 1536), lambda i: (i, 0)), pl.BlockSpec((tr, 128), lambda i: (i, 0)),
                   pl.BlockSpec((8, 128), lambda i: (0, 0))],
        out_shape=[_sds((s, 1536)), _sds((s, 128), BF16), _sds((8, 128))],
        compiler_params=_cp(("arbitrary",)), name="dn_pre_bwd")(dqn, dkn, dv, dbg, proj, proj, proj, conv_w8,
                                                                 alog_l, dtb_l)


def _conv_bwd(dy, proj, conv_w8):
    s = dy.shape[0]
    tr = 512
    nh = tr // 8
    nblk = s // tr

    def body(dy_ref, dyn_ref, u_ref, halo_ref, w_ref, du_ref, dw_ref):
        i = pl.program_id(0)

        @pl.when(i == 0)
        def _():
            dw_ref[...] = jnp.zeros_like(dw_ref)

        keep_prev = (i > 0).astype(F32)
        keep_next = (i < nblk - 1).astype(F32)
        row = lax.broadcasted_iota(jnp.int32, (8, 128), 0)
        for c in range(12):
            lanes = slice(128 * c, 128 * c + 128)
            dyc = dy_ref[:, lanes]
            dcat = jnp.concatenate([dyc, dyn_ref[:, lanes] * keep_next], axis=0)
            xcat = jnp.concatenate([halo_ref[:, lanes] * keep_prev, u_ref[:, lanes]], axis=0)
            du = dyc * w_ref[CONV_W - 1:CONV_W, lanes]
            dwc = jnp.where(row == CONV_W - 1, jnp.sum(dyc * u_ref[:, lanes], axis=0, keepdims=True), 0.0)
            for k in range(1, CONV_W):
                du = du + pltpu.roll(dcat, tr + 8 - k, 0)[0:tr] * w_ref[CONV_W - 1 - k:CONV_W - k, lanes]
                ush = pltpu.roll(xcat, k, 0)[8:8 + tr]
                dwc = dwc + jnp.where(row == CONV_W - 1 - k, jnp.sum(dyc * ush, axis=0, keepdims=True), 0.0)
            du_ref[:, lanes] = du.astype(BF16)
            dw_ref[:, lanes] = dw_ref[:, lanes] + dwc

    return pl.pallas_call(
        body, grid=(nblk,),
        in_specs=[pl.BlockSpec((tr, 1536), lambda i: (i, 0)),
                  pl.BlockSpec((8, 1536), lambda i: (jnp.minimum((i + 1) * nh, s // 8 - 1), 0)),
                  pl.BlockSpec((tr, 1536), lambda i: (i, 0)),
                  pl.BlockSpec((8, 1536), lambda i: (jnp.maximum(i * nh - 1, 0), 0)),
                  pl.BlockSpec((8, 1536), lambda i: (0, 0))],
        out_specs=[pl.BlockSpec((tr, 1536), lambda i: (i, 0)), pl.BlockSpec((8, 1536), lambda i: (0, 0))],
        out_shape=[_sds((s, 1536), BF16), _sds((8, 1536))],
        compiler_params=_cp(("arbitrary",)), name="conv_bwd")(dy, dy, proj, proj, conv_w8)


PIECE_WIDTHS = (1536, 512, 512, 512, 512, 512, 128)


def _in_bwd_dx(pieces, w_bf, x, dy, norm_w, ds_accs, buckets, chip_sums=None):
    s = x.shape[0]
    tm = 512
    n_piece = len(PIECE_WIDTHS)
    n_step = s // tm
    n_arr = 0 if chip_sums is None else len(chip_sums)
    n_sent = 3 * n_arr

    def body(*refs):
        refs = list(refs)

        def take(n):
            return [refs.pop(0) for _ in range(n)]

        piece_refs = take(n_piece)
        w_ref, x_ref, dy_ref, nw_ref = take(4)
        ds_refs = take(len(ds_accs))
        bk_ref, = take(1)
        hb_refs = take(n_arr)
        dx_ref, dnw_ref, drb_ref = take(3)
        recv_refs = take(n_arr)
        i = pl.program_id(0)

        def copies():
            send_sems, recv_sems = refs
            xi, yi, ci = _position()
            out = []
            for j, (px, py) in enumerate(_other_chips(xi, yi)):
                for t, (src, dst) in enumerate(zip(hb_refs, recv_refs)):
                    k = n_arr * j + t
                    out.append(pltpu.make_async_remote_copy(
                        src_ref=src.at[2 * px + py], dst_ref=dst.at[j], send_sem=send_sems.at[k],
                        recv_sem=recv_sems.at[k], device_id=(px, py, ci), device_id_type=MESH_ID))
            return out

        @pl.when(i == 0)
        def _():
            dnw_ref[...] = jnp.zeros_like(dnw_ref)
            for cp in (copies() if n_sent else []):
                cp.start()

        @pl.when(i == 1)
        def _():
            _bias_grad(ds_refs, bk_ref, drb_ref)

        dp = jnp.concatenate([r[...] for r in piece_refs], axis=1)
        dh = _dot(dp, w_ref[...])
        xv = x_ref[...]
        rstd = lax.rsqrt(jnp.mean(xv * xv, axis=-1, keepdims=True) + EPS)
        xh = xv * rstd
        dnw_ref[...] = dnw_ref[...] + jnp.sum(dh * xh, axis=0, keepdims=True)
        g = dh * nw_ref[...]
        dx_ref[...] = rstd * (g - xh * jnp.mean(g * xh, axis=-1, keepdims=True)) + dy_ref[...]

        if n_sent:
            @pl.when(i == n_step - 1)
            def _():
                cps = copies()
                for cp in cps:
                    cp.wait_recv()
                for cp in cps:
                    cp.wait_send()

    def blk(n):
        return pl.BlockSpec((tm, n), lambda i: (i, 0))

    vm = pl.BlockSpec(memory_space=pltpu.VMEM)
    in_specs = [blk(n) for n in PIECE_WIDTHS] + [pl.BlockSpec((D_IN_PAD, D_MODEL), lambda i: (0, 0)), blk(D_MODEL),
                                                blk(D_MODEL), pl.BlockSpec((1, D_MODEL), lambda i: (0, 0))]
    in_specs += [vm] * (len(ds_accs) + 1)
    out_specs = [blk(D_MODEL), pl.BlockSpec((8, D_MODEL), lambda i: (0, 0)), pl.BlockSpec(memory_space=pltpu.SMEM)]
    out_shape = [_sds((s, D_MODEL)), _sds((8, D_MODEL)), _sds((ATT_HEADS, N_BUCKETS))]
    scratch = []
    extra = ()
    if n_sent:
        extra = tuple(chip_sums)
        in_specs += [ANY] * n_arr
        out_specs += [ANY] * n_arr
        out_shape += [_sds((3,) + a.shape[1:], a.dtype) for a in extra]
        scratch = [pltpu.SemaphoreType.DMA((n_sent,)), pltpu.SemaphoreType.DMA((n_sent,))]
    return pl.pallas_call(
        body, grid=(n_step,), in_specs=in_specs, out_specs=out_specs, out_shape=out_shape, scratch_shapes=scratch,
        compiler_params=_cp(("arbitrary",)), name="in_bwd_dx")(*pieces, w_bf, x, dy, norm_w, *ds_accs, buckets, *extra)


def _in_bwd_dw(pieces, x, norm_w):
    s = x.shape[0]
    tm = 512
    n_piece = len(PIECE_WIDTHS)

    def body(*refs):
        piece_refs = refs[:n_piece]
        x_ref, nw_ref, dw_ref = refs[n_piece:]
        i = pl.program_id(0)

        @pl.when(i == 0)
        def _():
            dw_ref[...] = jnp.zeros_like(dw_ref)

        xv = x_ref[...]
        rstd = lax.rsqrt(jnp.mean(xv * xv, axis=-1, keepdims=True) + EPS)
        h = (xv * rstd * nw_ref[...]).astype(BF16)
        at = 0
        for r, width in zip(piece_refs, PIECE_WIDTHS):
            dw_ref[at:at + width, :] = dw_ref[at:at + width, :] + _dot_tn(r[...], h)
            at += width

    return pl.pallas_call(
        body, grid=(s // tm,),
        in_specs=[pl.BlockSpec((tm, n), lambda i: (i, 0)) for n in PIECE_WIDTHS]
        + [pl.BlockSpec((tm, D_MODEL), lambda i: (i, 0)), pl.BlockSpec((1, D_MODEL), lambda i: (0, 0))],
        out_specs=pl.BlockSpec((D_IN_PAD, D_MODEL), lambda i: (0, 0)),
        out_shape=_sds((D_IN_PAD, D_MODEL)),
        compiler_params=_cp(("arbitrary",)), name="in_bwd_dw")(*pieces, x, norm_w)


def _flat(a):
    return a.reshape(-1, a.shape[-1])


def _as_pattern(a, r):
    return a if r == 1 else a.reshape(r, a.shape[0] // r, a.shape[1])


D_SHARD = D_IN // N_CHIPS
BA_START = 4 * D_DN
BA_PACKED = 4096
S1_HEAD = BA_START - D_SHARD
S1_BA = 2 * D_SHARD - BA_START


def _pack_rows(g):
    pad = jnp.zeros((D_IN_PAD - D_IN, g.shape[2]), g.dtype)
    s2_ba = 2 * DN_HEADS - S1_BA
    return jnp.concatenate([g[0][:D_SHARD], g[1][:S1_HEAD], g[2][s2_ba:D_SHARD], g[3][:D_SHARD],
                            g[1][S1_HEAD:D_SHARD], g[2][:s2_ba], pad], axis=0)


def _pack_w(g):
    n = g.shape[2]

    def body(g_ref, o_ref):
        g32 = g_ref.bitcast(jnp.uint32)
        o32 = o_ref.bitcast(jnp.uint32)
        full, head, ba1 = D_SHARD // 2, S1_HEAD // 2, S1_BA // 2
        ba2 = DN_HEADS - ba1
        pieces = [(0, 0, full), (1, 0, head), (2, ba2, full), (3, 0, full), (1, head, full), (2, 0, ba2)]
        at = 0
        for chip, lo, hi in pieces:
            o32[at:at + hi - lo, :] = g32[chip, lo:hi, :]
            at += hi - lo
        o32[at:D_IN_PAD // 2, :] = jnp.zeros((D_IN_PAD // 2 - at, n), jnp.uint32)

    vm = pl.BlockSpec(memory_space=pltpu.VMEM)
    return pl.pallas_call(body, in_specs=[vm], out_specs=vm, out_shape=_sds((D_IN_PAD, n), g.dtype),
                          compiler_params=_cp(), name="pack_w")(g)


def _unpack_w(p, rows):
    n = p.shape[1]
    tn = 256
    mid = BA_PACKED + S1_BA
    pieces = [(0, 0, (0, D_SHARD)), (1, 0, (D_SHARD, BA_START)), (1, S1_HEAD, (BA_PACKED, mid)),
              (2, 0, (mid, BA_PACKED + 2 * DN_HEADS)), (2, 2 * DN_HEADS - S1_BA, (BA_START, BA_START + S1_HEAD)),
              (3, 0, (BA_START + S1_HEAD, BA_PACKED))]

    def body(p_ref, o_ref):
        for chip, at, (lo, hi) in pieces:
            o_ref[chip, at:at + hi - lo, :] = p_ref[lo:hi, :]
        for chip in range(N_CHIPS):
            o_ref[chip, D_SHARD:rows, :] = jnp.zeros((rows - D_SHARD, tn), p.dtype)

    return pl.pallas_call(
        body, grid=(n // tn,),
        in_specs=[pl.BlockSpec((D_IN_PAD, tn), lambda j: (0, j))],
        out_specs=pl.BlockSpec((N_CHIPS, rows, tn), lambda j: (0, 0, j)),
        out_shape=_sds((N_CHIPS, rows, n), p.dtype),
        compiler_params=_cp(("parallel",)), name="unpack_w")(p)


def _unpack_rows(p, rows):
    mid = BA_PACKED + S1_BA
    pad = jnp.zeros((rows - D_SHARD, p.shape[1]), p.dtype)
    return jnp.stack([jnp.concatenate([p[0:D_SHARD], pad], axis=0),
                      jnp.concatenate([p[D_SHARD:BA_START], p[BA_PACKED:mid], pad], axis=0),
                      jnp.concatenate([p[mid:BA_PACKED + 2 * DN_HEADS], p[BA_START:BA_START + S1_HEAD], pad], axis=0),
                      jnp.concatenate([p[BA_START + S1_HEAD:BA_PACKED], pad], axis=0)])


def _lane_row(vec, offset):
    return jnp.pad(vec.reshape(1, -1), ((0, 0), (offset, 128 - offset - vec.shape[0])))


def _local_step(x, tgt, norm_w, w_bf, conv_w, a_log, dt_bias, dn_norm_w, q_norm_w, k_norm_w, bias, w_out_bf):
    s = x.shape[0]
    nc = s // CHUNK
    conv_w8 = jnp.pad(conv_w, ((0, 8 - CONV_W), (0, 0)))
    alog_l = _lane_row(a_log.reshape(-1), DN_HEADS)
    dtb_l = _lane_row(dt_bias.reshape(-1), DN_HEADS)
    dnw_t = jnp.tile(dn_norm_w.reshape(1, DN_HD), (1, DN_HEADS))
    qw_t = jnp.tile(q_norm_w.reshape(1, ATT_HD), (1, ATT_HEADS))
    kw_t = jnp.tile(k_norm_w.reshape(1, ATT_HD), (1, ATT_HEADS))
    bd128 = _compact_mat(DN_HD)
    bd64 = _compact_mat(ATT_HD)
    tri = _tri_incl()
    sel = _lane_select()
    buckets = _bucket_tables()

    proj = _in_proj(x, norm_w, w_bf)
    qn, kn, v_dn, bg = _dn_pre(proj, conv_w8, alog_l, dtb_l)
    bgt = bg[:, 0:8].reshape(nc, CHUNK, 8).transpose(0, 2, 1)
    u, w, qd, kt, attn, t_inv, gl = _dn_prep(qn, kn, v_dn, bg, bgt, tri, sel)
    o_dn, vn, sp = _dn_scan(u, w, qd, kt, attn, gl)
    q1, k1, v1, q4, k4, v4, q16, k16, v16 = _att_pre(proj, qw_t, kw_t, bd64)
    rs = [r for _, r in PATTERNS]
    qkv = [(q1, k1, v1, 0), (_flat(q4), _flat(k4), _flat(v4), 0), (_flat(q16), _flat(k16), _flat(v16), 0)]
    o_pats, lse_pats = [], []
    for p, r in enumerate(rs):
        o_p, lse_p = _att_fwd(*qkv[p], bias, p, r, "att_fwd_r%d" % r)
        o_pats.append(_as_pattern(o_p, r))
        lse_pats.append(_as_pattern(lse_p, r))
    mixed, o_att, l1, l4, l16 = _post_fwd(o_dn, proj, o_pats, lse_pats, dnw_t, bd128)
    dy, loss_blk = _out_fwd(x, mixed, w_out_bf, tgt)

    dmixed, d_w_out = _out_bwd(dy, mixed, w_out_bf)
    do_dn, dz, dgate, do1, do4, do16, dl1, dl4, dl16, d_dnw = _post_bwd(dmixed, o_dn, proj, o_att, dnw_t, bd128)
    side = [(do1, l1, dl1), (_flat(do4), _flat(l4), _flat(dl4)), (_flat(do16), _flat(l16), _flat(dl16))]
    dq_pats, dk_pats, dv_pats, ds_accs = [], [], [], []
    for p, r in enumerate(rs):
        dq_p, dk_p, dv_p, ds_p = _att_bwd(*qkv[p], *side[p], bias, p, r, "att_bwd_r%d" % r)
        dq_pats.append(_as_pattern(dq_p, r))
        dk_pats.append(_as_pattern(dk_p, r))
        dv_pats.append(_as_pattern(dv_p, r))
        ds_accs.append(ds_p)
    dq_att, dk_att, dv_att, d_qw, d_kw = _att_pre_bwd(dq_pats, dk_pats, dv_pats, proj, qw_t, kw_t, bd64)
    du, dqd, dkt, dw, dattn, dgl = _dn_scan_bwd(do_dn, sp, qd, kt, w, vn, attn, gl)
    dqn, dkn, dv_dn, dbg = _dn_prep_bwd(qn, kn, v_dn, bg, bgt, tri, sel, t_inv, attn, u, w, du, dw, dqd, dkt, dattn,
                                        dgl)
    dyc, dba, dsm = _dn_pre_bwd(dqn, dkn, dv_dn, dbg, proj, conv_w8, alog_l, dtb_l)
    d_qkv_dn, d_conv8 = _conv_bwd(dyc, proj, conv_w8)
    pieces = (d_qkv_dn, dz, dq_att, dk_att, dv_att, dgate, dba)
    d_w_in_t = _in_bwd_dw(pieces, x, norm_w)
    last = functools.partial(_in_bwd_dx, pieces, w_bf, x, dy, norm_w, ds_accs, buckets)

    grads = dict(
        w_in_t=d_w_in_t,
        conv_w=d_conv8[0:CONV_W, :],
        a_log=dsm[0:1, DN_HEADS:2 * DN_HEADS],
        dt_bias=dsm[1:2, DN_HEADS:2 * DN_HEADS],
        dn_norm_w=d_dnw[0:1, :],
        q_norm_w=d_qw[0:1, :].reshape(ATT_HEADS, ATT_HD),
        k_norm_w=d_kw[0:1, :].reshape(ATT_HEADS, ATT_HD),
        w_out=d_w_out,
    )
    return loss_blk[0, 0], last, grads


MESH_ID = pl.DeviceIdType.MESH
ANY = pl.BlockSpec(memory_space=pl.ANY)


def _position():
    return lax.axis_index("x"), lax.axis_index("y"), lax.axis_index("c")


def _other_chips(x, y):
    return [(1 - x, y), (x, 1 - y), (1 - x, 1 - y)]


SHARD_PAD = 1040
WIN = 528


def _row_split(n):
    return (n // 2) // 128 * 128


def _part(ref, cc):
    n = ref.shape[0]
    sp = _row_split(n)
    return ref.at[pl.ds(0, sp)] if cc == 0 else ref.at[pl.ds(sp, n - sp)]


def _gather_weights(wt_s, w_out_s, conv_s, rel_bias, buckets):
    def body(a_ref, b_ref, c_ref, rb_ref, bk_ref, ga_ref, gb_ref, gc_ref, bias_ref, send_sems, recv_sems, loc_sems,
             a_vmem, b_vmem):
        x, y, c = _position()
        me = 2 * x + y
        sib = (x, y, 1 - c)
        big = ((a_ref, ga_ref), (b_ref, gb_ref))
        stage_in = [pltpu.make_async_copy(a_ref, a_vmem, loc_sems.at[0]),
                    pltpu.make_async_copy(b_ref, b_vmem, loc_sems.at[1])]
        local = [pltpu.make_async_copy(a_vmem, ga_ref.at[me], loc_sems.at[0]),
                 pltpu.make_async_copy(b_vmem, gb_ref.at[me], loc_sems.at[1]),
                 pltpu.make_async_copy(c_ref, gc_ref.at[me], loc_sems.at[2])]
        for cp in stage_in:
            cp.start()
        local[2].start()
        for cp in stage_in:
            cp.wait()
        for cp in local[:2]:
            cp.start()
        others = _other_chips(x, y)

        def exchange(cc):
            sends = []
            for j, (px, py) in enumerate(others):
                for t, (src, dst) in enumerate(big):
                    k = 3 * j + t
                    sends.append(pltpu.make_async_remote_copy(
                        src_ref=_part(src, cc), dst_ref=_part(dst.at[me], cc), send_sem=send_sems.at[k],
                        recv_sem=recv_sems.at[k], device_id=(px, py, c), device_id_type=MESH_ID))
                sends.append(pltpu.make_async_remote_copy(
                    src_ref=c_ref, dst_ref=gc_ref.at[me], send_sem=send_sems.at[3 * j + 2],
                    recv_sem=recv_sems.at[3 * j + 2], device_id=(px, py, c), device_id_type=MESH_ID))
            for cp in sends:
                cp.start()
            _bias_table(rb_ref, bk_ref, bias_ref)
            for j, (px, py) in enumerate(others):
                src_chip = 2 * px + py
                for t, (src, dst) in enumerate(big):
                    landed = _part(dst.at[src_chip], cc)
                    pltpu.make_async_remote_copy(
                        src_ref=_part(src, cc), dst_ref=landed, send_sem=send_sems.at[3 * j + t],
                        recv_sem=recv_sems.at[3 * j + t], device_id=(px, py, c), device_id_type=MESH_ID).wait_recv()
                    k = 9 + 2 * j + t
                    fwd = pltpu.make_async_remote_copy(
                        src_ref=landed, dst_ref=landed, send_sem=send_sems.at[k], recv_sem=recv_sems.at[k],
                        device_id=sib, device_id_type=MESH_ID)
                    fwd.start()
                    sends.append(fwd)
                pltpu.make_async_remote_copy(
                    src_ref=c_ref, dst_ref=gc_ref.at[src_chip], send_sem=send_sems.at[3 * j + 2],
                    recv_sem=recv_sems.at[3 * j + 2], device_id=(px, py, c), device_id_type=MESH_ID).wait_recv()
            for j, (px, py) in enumerate(others):
                src_chip = 2 * px + py
                for t, (src, dst) in enumerate(big):
                    k = 9 + 2 * j + t
                    theirs = _part(dst.at[src_chip], 1 - cc)
                    pltpu.make_async_remote_copy(
                        src_ref=theirs, dst_ref=theirs, send_sem=send_sems.at[k], recv_sem=recv_sems.at[k],
                        device_id=sib, device_id_type=MESH_ID).wait_recv()
            for cp in sends:
                cp.wait_send()

        for cc in (0, 1):
            pl.when(c == cc)(functools.partial(exchange, cc))
        for cp in local:
            cp.wait()

    srcs = (wt_s, w_out_s, conv_s)
    n_sem = 9 + 6
    return pl.pallas_call(
        body, in_specs=[ANY] * 3 + [pl.BlockSpec(memory_space=pltpu.SMEM), pl.BlockSpec(memory_space=pltpu.VMEM)],
        out_specs=[ANY] * 3 + [pl.BlockSpec(memory_space=pltpu.VMEM)],
        out_shape=[_sds((N_CHIPS,) + a.shape, a.dtype) for a in srcs]
        + [_sds((len(PATTERNS), ATT_HEADS, BLK, 2 * BLK))],
        scratch_shapes=[pltpu.SemaphoreType.DMA((n_sem,)), pltpu.SemaphoreType.DMA((n_sem,)),
                        pltpu.SemaphoreType.DMA((3,)), pltpu.VMEM(wt_s.shape, wt_s.dtype),
                        pltpu.VMEM(w_out_s.shape, w_out_s.dtype)],
        compiler_params=_cp(), name="gather_weights")(*srcs, rel_bias, buckets)


N_DEV = 8
PEER_FLIPS = [(dx, dy, dc) for dx in (0, 1) for dy in (0, 1) for dc in (0, 1)][1:]


def _small_copies(s_ref, rs_ref, send_sems, recv_sems):
    x, y, c = _position()
    dev = 4 * x + 2 * y + c
    sends, recvs = [], []
    for f, (dx, dy, dc) in enumerate(PEER_FLIPS):
        peer = (x ^ dx, y ^ dy, c ^ dc)
        sends.append(pltpu.make_async_remote_copy(
            src_ref=s_ref, dst_ref=rs_ref.at[dev], send_sem=send_sems.at[f], recv_sem=recv_sems.at[f],
            device_id=peer, device_id_type=MESH_ID))
        recvs.append(pltpu.make_async_remote_copy(
            src_ref=s_ref, dst_ref=rs_ref.at[4 * peer[0] + 2 * peer[1] + peer[2]], send_sem=send_sems.at[f],
            recv_sem=recv_sems.at[f], device_id=peer, device_id_type=MESH_ID))
    return sends, recvs


def _fill_parts(h_in, r_in, h_out, r_out, small, rows_in, rows_out):
    def body(ha_ref, ra_ref, hb_ref, rb_ref, s_ref, fa_ref, fb_ref, rs_ref, send_sems, recv_sems, loc_sems,
             small_send, small_recv, a_vmem, b_vmem, ra_vmem, rb_vmem):
        x, y, c = _position()
        sib = (x, y, 1 - c)
        chip = 2 * x + y
        pairs = ((a_vmem, fa_ref), (b_vmem, fb_ref))
        own_small = pltpu.make_async_copy(s_ref, rs_ref.at[4 * x + 2 * y + c], loc_sems.at[2])
        own_small.start()
        small_sends, small_recvs = _small_copies(s_ref, rs_ref, small_send, small_recv)
        for cp in small_sends:
            cp.start()
        stage_in = [pltpu.make_async_copy(ha_ref.at[chip], a_vmem, loc_sems.at[0]),
                    pltpu.make_async_copy(hb_ref.at[chip], b_vmem, loc_sems.at[1]),
                    pltpu.make_async_copy(ra_ref, ra_vmem, loc_sems.at[3]),
                    pltpu.make_async_copy(rb_ref, rb_vmem, loc_sems.at[4])]
        for cp in stage_in:
            cp.start()
        for cp in stage_in:
            cp.wait()
        for tot, recv in ((a_vmem, ra_vmem), (b_vmem, rb_vmem)):
            acc = tot[...]
            for j in range(3):
                acc = acc + recv[j].astype(F32)
            tot[...] = acc

        def fill(cc):
            mine = [_part(dst, cc) for _, dst in pairs]
            srcs = [src.at[pl.ds(0, m.shape[0])] for src, m in zip((a_vmem, b_vmem), mine)]
            local = [pltpu.make_async_copy(s, m, loc_sems.at[t]) for t, (s, m) in enumerate(zip(srcs, mine))]
            sends = [pltpu.make_async_remote_copy(src_ref=s, dst_ref=m, send_sem=send_sems.at[t],
                                                  recv_sem=recv_sems.at[t], device_id=sib, device_id_type=MESH_ID)
                     for t, (s, m) in enumerate(zip(srcs, mine))]
            for cp in local + sends:
                cp.start()
            for t, (_, dst) in enumerate(pairs):
                theirs = _part(dst, 1 - cc)
                pltpu.make_async_remote_copy(src_ref=theirs, dst_ref=theirs, send_sem=send_sems.at[t],
                                             recv_sem=recv_sems.at[t], device_id=sib, device_id_type=MESH_ID).wait_recv()
            for cp in sends:
                cp.wait_send()
            for cp in local:
                cp.wait()

        for cc in (0, 1):
            pl.when(c == cc)(functools.partial(fill, cc))
        for cp in small_recvs:
            cp.wait_recv()
        for cp in small_sends:
            cp.wait_send()
        own_small.wait()

    n_peer = len(PEER_FLIPS)
    return pl.pallas_call(
        body, in_specs=[ANY] * 5, out_specs=[ANY] * 3,
        out_shape=[_sds((rows_in, D_MODEL)), _sds((rows_out, D_MODEL)), _sds((N_DEV,) + small.shape, small.dtype)],
        scratch_shapes=[pltpu.SemaphoreType.DMA((2,)), pltpu.SemaphoreType.DMA((2,)), pltpu.SemaphoreType.DMA((5,)),
                        pltpu.SemaphoreType.DMA((n_peer,)), pltpu.SemaphoreType.DMA((n_peer,)),
                        pltpu.VMEM(h_in.shape[1:], F32), pltpu.VMEM(h_out.shape[1:], F32),
                        pltpu.VMEM(r_in.shape, r_in.dtype), pltpu.VMEM(r_out.shape, r_out.dtype)],
        compiler_params=_cp(), name="fill_parts")(h_in, r_in, h_out, r_out, small)


def _chip_sums(u_in, u_out, win_in, win_out):
    wins = (win_in, win_out)

    def body(a_ref, b_ref, ha_ref, hba_ref, hb_ref, hbb_ref, send_sems, recv_sems, loc_sems, out_sems,
             mine_a, theirs_a, sum_a, sumb_a, mine_b, theirs_b, sum_b, sumb_b):
        x, y, c = _position()
        sib = (x, y, 1 - c)
        groups = ((a_ref, mine_a, theirs_a, sum_a, sumb_a, ha_ref, hba_ref, win_in),
                  (b_ref, mine_b, theirs_b, sum_b, sumb_b, hb_ref, hbb_ref, win_out))
        loads, sends = [], []
        for t, (src, mine, theirs, _, _, _, _, win) in enumerate(groups):
            split = _row_split(src.shape[1])
            for k in range(N_CHIPS):
                n = N_CHIPS * t + k
                loads.append(pltpu.make_async_copy(
                    src.at[k, pl.ds(pl.multiple_of(c * split, split), win), :], mine.at[k], loc_sems.at[n]))
                sends.append(pltpu.make_async_remote_copy(
                    src_ref=src.at[k, pl.ds(pl.multiple_of((1 - c) * split, split), win), :], dst_ref=theirs.at[k],
                    send_sem=send_sems.at[n], recv_sem=recv_sems.at[n], device_id=sib, device_id_type=MESH_ID))
        for cp in sends + loads:
            cp.start()
        stores = []
        for t, (_, mine, theirs, tot, totb, h_out, hb_out, _) in enumerate(groups):
            for k in range(N_CHIPS):
                n = N_CHIPS * t + k
                loads[n].wait()
                sends[n].wait_recv()
                val = mine[k] + theirs[k]
                tot[k] = val
                totb[k] = val.astype(BF16)
                stores += [pltpu.make_async_copy(tot.at[k], h_out.at[k], out_sems.at[2 * n]),
                           pltpu.make_async_copy(totb.at[k], hb_out.at[k], out_sems.at[2 * n + 1])]
                stores[-2].start()
                stores[-1].start()
        for cp in sends:
            cp.wait_send()
        for cp in stores:
            cp.wait()

    shapes = [(N_CHIPS, w, D_MODEL) for w in wins]
    n_cp = 2 * N_CHIPS
    vmem = []
    for shp in shapes:
        vmem += [pltpu.VMEM(shp, F32), pltpu.VMEM(shp, F32), pltpu.VMEM(shp, F32), pltpu.VMEM(shp, BF16)]
    return pl.pallas_call(
        body, in_specs=[ANY] * 2, out_specs=[ANY] * 4,
        out_shape=[_sds(shapes[0]), _sds(shapes[0], BF16), _sds(shapes[1]), _sds(shapes[1], BF16)],
        scratch_shapes=[pltpu.SemaphoreType.DMA((n_cp,)), pltpu.SemaphoreType.DMA((n_cp,)),
                        pltpu.SemaphoreType.DMA((n_cp,)), pltpu.SemaphoreType.DMA((2 * n_cp,))] + vmem,
        compiler_params=_cp(), name="chip_sums")(u_in, u_out)


SMALL_LAYOUT = (("norm_w", 1024), ("conv_w", 6144), ("a_log", 128), ("dt_bias", 128), ("dn_norm_w", 128),
                ("q_norm_w", 512), ("k_norm_w", 512), ("rel_bias", 256), ("loss", 128))
SMALL_TOTAL = sum(n for _, n in SMALL_LAYOUT)


def _small_offset(name):
    off = 0
    for n, size in SMALL_LAYOUT:
        if n == name:
            return off
        off += size
    raise KeyError(name)


def _pack_small(grads):
    parts = []
    for name, size in SMALL_LAYOUT:
        flat = grads[name].reshape(1, -1)
        parts.append(jnp.pad(flat, ((0, 0), (0, size - flat.shape[1]))))
    return jnp.concatenate(parts, axis=1)


def _sum_small(rows):
    n_dev = rows.shape[0]
    q_off = _small_offset("q_norm_w")
    k_off = _small_offset("k_norm_w")

    def body(r_ref, tot_ref, qk_ref):
        tot = r_ref[0:1, :]
        for d in range(1, n_dev):
            tot = tot + r_ref[d:d + 1, :]
        tot_ref[...] = tot
        for row, off in ((0, q_off), (1, k_off)):
            s4 = tot[:, off:off + 128] + tot[:, off + 128:off + 256] + tot[:, off + 256:off + 384] \
                + tot[:, off + 384:off + 512]
            qk_ref[row:row + 1, :] = s4 + pltpu.roll(s4, ATT_HD, 1)

    return pl.pallas_call(
        body, in_specs=[pl.BlockSpec(memory_space=pltpu.VMEM)],
        out_specs=[pl.BlockSpec(memory_space=pltpu.VMEM)] * 2,
        out_shape=[_sds((1, SMALL_TOTAL)), _sds((2, 128))],
        compiler_params=_cp(), name="sum_small")(rows)


def _adamw_math(w, g, m, v):
    m = ADAM_B1 * m + (1.0 - ADAM_B1) * g
    v = ADAM_B2 * v + (1.0 - ADAM_B2) * (g * g)
    m_hat = m / (1.0 - ADAM_B1 ** ADAM_STEP)
    v_hat = v / (1.0 - ADAM_B2 ** ADAM_STEP)
    delta = -ADAM_LR * (m_hat / (jnp.sqrt(v_hat) + ADAM_EPS) + ADAM_WD * w)
    return delta, m, v


def _adamw_big(g, w, m, v, name):
    rows, cols = w.shape
    tr = 128

    def body(g_ref, w_ref, m_ref, v_ref, go_ref, d_ref, nm_ref, nv_ref):
        g = g_ref[...]
        go_ref[...] = g
        d_ref[...], nm_ref[...], nv_ref[...] = _adamw_math(w_ref[...], g, m_ref[...], v_ref[...])

    blk = pl.BlockSpec((tr, cols), lambda i: (i, 0))
    return pl.pallas_call(
        body, grid=(pl.cdiv(rows, tr),), in_specs=[blk] * 4, out_specs=[blk] * 4,
        out_shape=[_sds((rows, cols))] * 4, compiler_params=_cp(("parallel",)), name=name)(g, w, m, v)


def _adamw_rows(g, w, m, v, name):
    rows, cols = w.shape
    tr = 128

    def body(g_ref, w_ref, m_ref, v_ref, go_ref, d_ref, nm_ref, nv_ref, s_g, s_d, s_m, s_v):
        g = g_ref[...]
        s_g[...] = g
        s_d[...], s_m[...], s_v[...] = _adamw_math(w_ref[...], g, m_ref[...], v_ref[...])
        for i in range(tr):
            for scr, out in ((s_g, go_ref), (s_d, d_ref), (s_m, nm_ref), (s_v, nv_ref)):
                out[i] = scr[i:i + 1, :]

    blk = pl.BlockSpec((tr, cols), lambda i: (i, 0))
    oblk = pl.BlockSpec((tr, 1, cols), lambda i: (i, 0, 0))
    return pl.pallas_call(
        body, grid=(pl.cdiv(rows, tr),), in_specs=[blk] * 4, out_specs=[oblk] * 4,
        out_shape=[_sds((rows, 1, cols))] * 4, scratch_shapes=[pltpu.VMEM((tr, cols), F32)] * 4,
        compiler_params=_cp(("parallel",)), name=name)(g, w, m, v)


def _adamw_small(w, g, m, v, name):
    def body(w_ref, g_ref, m_ref, v_ref, d_ref, nm_ref, nv_ref):
        d_ref[...], nm_ref[...], nv_ref[...] = _adamw_math(w_ref[...], g_ref[...], m_ref[...], v_ref[...])

    vm = pl.BlockSpec(memory_space=pltpu.VMEM)
    return pl.pallas_call(body, in_specs=[vm] * 4, out_specs=[vm] * 3, out_shape=[_sds(w.shape)] * 3,
                          compiler_params=_cp(), name=name)(w, g, m, v)


WEIGHTS = ("norm_w", "w_in", "conv_w", "a_log", "dt_bias", "dn_norm_w", "q_norm_w", "k_norm_w", "rel_bias", "w_out")


def kernel(x, norm_w, w_in, conv_w, a_log, dt_bias, dn_norm_w, q_norm_w, k_norm_w, rel_bias, w_out, loss_target, m_norm_w, m_w_in, m_conv_w, m_a_log, m_dt_bias, m_dn_norm_w, m_q_norm_w, m_k_norm_w, m_rel_bias, m_w_out, v_norm_w, v_w_in, v_conv_w, v_a_log, v_dt_bias, v_dn_norm_w, v_q_norm_w, v_k_norm_w, v_rel_bias, v_w_out):
    xi, yi, _ = _position()
    chip = 2 * xi + yi
    w_loc = dict(norm_w=norm_w, w_in=w_in[0].T, conv_w=conv_w[0], a_log=a_log, dt_bias=dt_bias, dn_norm_w=dn_norm_w,
                 q_norm_w=q_norm_w, k_norm_w=k_norm_w, rel_bias=rel_bias, w_out=w_out[0])
    m_loc = dict(norm_w=m_norm_w, w_in=m_w_in[0].T, conv_w=m_conv_w[0], a_log=m_a_log, dt_bias=m_dt_bias,
                 dn_norm_w=m_dn_norm_w, q_norm_w=m_q_norm_w, k_norm_w=m_k_norm_w, rel_bias=m_rel_bias,
                 w_out=m_w_out[0])
    v_loc = dict(norm_w=v_norm_w, w_in=v_w_in[0].T, conv_w=v_conv_w[0], a_log=v_a_log, dt_bias=v_dt_bias,
                 dn_norm_w=v_dn_norm_w, q_norm_w=v_q_norm_w, k_norm_w=v_k_norm_w, rel_bias=v_rel_bias,
                 w_out=v_w_out[0])

    wt_pad = jnp.pad(w_loc["w_in"].astype(BF16), ((0, SHARD_PAD - D_SHARD), (0, 0)))
    g_in, g_out, g_conv, bias = _gather_weights(wt_pad, w_loc["w_out"].astype(BF16), w_loc["conv_w"], rel_bias,
                                                _bucket_tables())
    wt_full = _pack_w(g_in)
    w_out_full = g_out.reshape(D_MODEL, D_MODEL)
    conv_full = g_conv.transpose(1, 0, 2).reshape(CONV_W, 3 * D_DN)

    loss_local, last_kernel, grads = _local_step(x[0], loss_target[0], norm_w, wt_full, conv_full, a_log, dt_bias,
                                                 dn_norm_w, q_norm_w, k_norm_w, bias, w_out_full)
    grads["loss"] = loss_local

    u_in = _unpack_w(grads["w_in_t"], SHARD_PAD)
    u_out = grads["w_out"].reshape(N_CHIPS, D_MODEL // N_CHIPS, D_MODEL)
    win_out = u_out.shape[1] // 2
    h_in, hb_in, h_out, hb_out = _chip_sums(u_in, u_out, WIN, win_out)
    grad_x, d_nw8, grads["rel_bias"], r_in, r_out = last_kernel(chip_sums=(hb_in, hb_out))
    grads["norm_w"] = d_nw8[0:1, :]
    small = _pack_small(grads)
    full_in, full_out, r_small = _fill_parts(h_in, r_in, h_out, r_out, small, SHARD_PAD, u_out.shape[1])
    tot_small, qk = _sum_small(r_small.reshape(8, SMALL_TOTAL))

    def small_grad(name, n):
        off = _small_offset(name)
        return tot_small[:, off:off + n]

    loss = small_grad("loss", 1).reshape(())
    conv_all = small_grad("conv_w", CONV_W * 3 * D_DN).reshape(CONV_W, 3 * D_DN)
    g_small = dict(
        norm_w=small_grad("norm_w", D_MODEL),
        conv_w=lax.dynamic_slice_in_dim(conv_all, chip * (3 * D_DN // N_CHIPS), 3 * D_DN // N_CHIPS, axis=1),
        a_log=small_grad("a_log", DN_HEADS),
        dt_bias=small_grad("dt_bias", DN_HEADS),
        dn_norm_w=small_grad("dn_norm_w", DN_HD),
        q_norm_w=qk[0:1, 0:ATT_HD],
        k_norm_w=qk[1:2, 0:ATT_HD],
        rel_bias=small_grad("rel_bias", ATT_HEADS * N_BUCKETS).reshape(ATT_HEADS, N_BUCKETS),
    )

    out_g, out_d, out_m, out_v = {}, {}, {}, {}
    out_g["w_in"], out_d["w_in"], out_m["w_in"], out_v["w_in"] = _adamw_rows(
        full_in, w_loc["w_in"], m_loc["w_in"], v_loc["w_in"], "adamw_w_in")
    out_g["w_out"], out_d["w_out"], out_m["w_out"], out_v["w_out"] = _adamw_big(
        full_out, w_loc["w_out"], m_loc["w_out"], v_loc["w_out"], "adamw_w_out")
    for name in g_small:
        out_g[name] = g_small[name]
        out_d[name], out_m[name], out_v[name] = _adamw_small(w_loc[name], g_small[name], m_loc[name], v_loc[name],
                                                             "adamw_" + name)
    for d in (out_g, out_d, out_m, out_v):
        d["w_in"] = d["w_in"].transpose(1, 2, 0)
        for name in ("conv_w", "w_out"):
            d[name] = d[name][None]
    return (loss, grad_x[None], *[out_g[n] for n in WEIGHTS], *[out_d[n] for n in WEIGHTS],
            *[out_m[n] for n in WEIGHTS], *[out_v[n] for n in WEIGHTS])
```

```python
import functools
import math

import numpy as np
import jax
import jax.numpy as jnp
from jax import lax
from jax.experimental import pallas as pl
from jax.experimental.pallas import tpu as pltpu

F32 = jnp.float32
BF16 = jnp.bfloat16
HI = lax.Precision.HIGHEST

D_MODEL = 1024
D_DN = 512
DN_HEADS = 4
DN_HD = 128
CONV_W = 4
CHUNK = 64
D_ATT = 512
ATT_HEADS = 8
ATT_HD = 64
PATTERNS = ((128, 1), (512, 4), (2048, 16))
N_BUCKETS = 32
MAX_DISTANCE = 2048
D_IN = 4 * D_DN + 2 * DN_HEADS + 4 * D_ATT
D_IN_PAD = 4224
EPS = 1e-6
BLK = 128
NEG = -1e30
N_CHIPS = 4

ADAM_LR = 0.001
ADAM_B1 = 0.9
ADAM_B2 = 0.999
ADAM_EPS = 1e-08
ADAM_WD = 0.01
ADAM_STEP = 10

VMEM_LIMIT = 56 * 1024 * 1024

COL_Z = 3
COL_ATT_Q = 4
COL_ATT_K = 5
COL_ATT_V = 6
COL_GATE = 7
COL_BA_128 = 32


def _cp(sem=None):
    if sem is None:
        return pltpu.CompilerParams(vmem_limit_bytes=VMEM_LIMIT)
    return pltpu.CompilerParams(dimension_semantics=sem, vmem_limit_bytes=VMEM_LIMIT)


def _sds(shape, dtype=F32):
    return jax.ShapeDtypeStruct(shape, dtype)


def _mm(a, b):
    return jnp.dot(a.astype(BF16), b.astype(BF16), preferred_element_type=F32)


def _mm_nt(a, b):
    return lax.dot_general(a.astype(BF16), b.astype(BF16), (((1,), (1,)), ((), ())),
                           preferred_element_type=F32)


def _mm_tn(a, b):
    return lax.dot_general(a.astype(BF16), b.astype(BF16), (((0,), (0,)), ((), ())),
                           preferred_element_type=F32)


def _mmx(a, b):
    return jnp.dot(a, b, precision=HI, preferred_element_type=F32)


def _mmx_nt(a, b):
    return lax.dot_general(a, b, (((1,), (1,)), ((), ())), precision=HI, preferred_element_type=F32)


def _mmx_tn(a, b):
    return lax.dot_general(a, b, (((0,), (0,)), ((), ())), precision=HI, preferred_element_type=F32)


def _dot(a, b):
    return jnp.dot(a, b, preferred_element_type=F32)


def _dot_nt(a, b):
    return lax.dot_general(a, b, (((1,), (1,)), ((), ())), preferred_element_type=F32)


def _dot_tn(a, b):
    return lax.dot_general(a, b, (((0,), (0,)), ((), ())), preferred_element_type=F32)


def _split(a):
    hi = a.astype(BF16)
    return hi, (a - hi.astype(F32)).astype(BF16)


def _mm3(a_s, b_s):
    return _dot(a_s[0], b_s[0]) + _dot(a_s[0], b_s[1]) + _dot(a_s[1], b_s[0])


def _mm3_tn(a_s, b_s):
    return _dot_tn(a_s[0], b_s[0]) + _dot_tn(a_s[0], b_s[1]) + _dot_tn(a_s[1], b_s[0])


def _interleave(gens):
    live = list(gens)
    while live:
        nxt = []
        for g in live:
            try:
                next(g)
                nxt.append(g)
            except StopIteration:
                pass
        live = nxt


def _segsum(x, bd):
    hi = x.astype(BF16)
    r1 = x - hi.astype(F32)
    mid = r1.astype(BF16)
    lo = (r1 - mid.astype(F32)).astype(BF16)
    return (jnp.dot(hi, bd, preferred_element_type=F32) + jnp.dot(mid, bd, preferred_element_type=F32)
            + jnp.dot(lo, bd, preferred_element_type=F32))


def _compact_mat(seg, n=512):
    slot = 128 * seg // n
    src = np.arange(n)[:, None]
    dst = np.arange(128)[None, :]
    return jnp.asarray((src // seg == dst // slot).astype(np.float32), dtype=BF16)


def _seg_mean(x, cm, seg):
    hi, lo = _split(x)
    return (_dot(hi, cm) + _dot(lo, cm)) * (1.0 / seg)


def _seg_expand(c, cm, seg):
    hi, lo = _split(c)
    return (_dot_nt(hi, cm) + _dot_nt(lo, cm)) * (cm.shape[0] / (128.0 * seg))


def _seg_rstd(x, cm, seg):
    return _seg_expand(lax.rsqrt(_seg_mean(x * x, cm, seg) + EPS), cm, seg)


def _sigmoid(x):
    return 1.0 / (1.0 + jnp.exp(-x))


def _silu_grad(x, s):
    return s * (1.0 + x * (1.0 - s))


def _tri_incl():
    i = np.arange(CHUNK)
    return jnp.asarray((i[:, None] >= i[None, :]).astype(np.float32))


def _t5_bucket(dist):
    max_exact = N_BUCKETS // 2
    d = np.maximum(dist, 1).astype(np.float64)
    large = max_exact + (np.log(d / max_exact) / math.log(MAX_DISTANCE / max_exact)
                         * (N_BUCKETS - max_exact)).astype(np.int32)
    large = np.minimum(large, N_BUCKETS - 1)
    return np.where(dist < max_exact, dist, large).astype(np.int32)


def _bucket_tables():
    qi = np.arange(BLK)[:, None]
    kj = np.arange(2 * BLK)[None, :]
    step = qi - kj + BLK
    return jnp.asarray(np.stack([_t5_bucket(np.clip(step, 0, None) * r) for _, r in PATTERNS]))


def _in_proj(x, norm_w, wt_bf):
    s = x.shape[0]
    tm = 512

    def body(x_ref, nw_ref, w_ref, o_ref):
        xv = x_ref[...]
        rstd = lax.rsqrt(jnp.mean(xv * xv, axis=-1, keepdims=True) + EPS)
        h = (xv * rstd * nw_ref[...]).astype(BF16)
        o_ref[...] = _dot_nt(h, w_ref[...])

    return pl.pallas_call(
        body, grid=(s // tm,),
        in_specs=[pl.BlockSpec((tm, D_MODEL), lambda i: (i, 0)),
                  pl.BlockSpec((1, D_MODEL), lambda i: (0, 0)),
                  pl.BlockSpec((D_IN_PAD, D_MODEL), lambda i: (0, 0))],
        out_specs=pl.BlockSpec((tm, D_IN_PAD), lambda i: (i, 0)),
        out_shape=_sds((s, D_IN_PAD)), compiler_params=_cp(("parallel",)), name="in_proj")(x, norm_w, wt_bf)


def _conv_group(cur, halo, w_ref, c):
    rows = cur.shape[0]
    lanes = slice(128 * c, 128 * c + 128)
    xcat = jnp.concatenate([halo, cur], axis=0)
    y = cur * w_ref[CONV_W - 1:CONV_W, lanes]
    for k in range(1, CONV_W):
        sh = pltpu.roll(xcat, k, 0)[8:8 + rows]
        y = y + sh * w_ref[CONV_W - 1 - k:CONV_W - k, lanes]
    return y


def _beta_g(ba, alog_l, dtb_l):
    lane = lax.broadcasted_iota(jnp.int32, ba.shape, 1)
    sig_b = _sigmoid(ba)
    t = ba + dtb_l
    softplus = jnp.maximum(t, 0.0) + jnp.log(1.0 + jnp.exp(-jnp.abs(t)))
    nega = -jnp.exp(alog_l)
    g = nega * softplus
    out = jnp.where(lane < DN_HEADS, sig_b, jnp.where(lane < 2 * DN_HEADS, g, 0.0))
    return out, lane, sig_b, t, nega, g


def _dn_pre(proj, conv_w8, alog_l, dtb_l):
    s = proj.shape[0]
    tr = 512
    nh = tr // 8

    def body(u_ref, halo_ref, ba_ref, w_ref, al_ref, dt_ref, q_ref, k_ref, v_ref, bg_ref):
        i = pl.program_id(0)
        keep = (i > 0).astype(F32)
        for c in range(12):
            lanes = slice(128 * c, 128 * c + 128)
            y = _conv_group(u_ref[:, lanes], halo_ref[:, lanes] * keep, w_ref, c)
            sv = y * _sigmoid(y)
            if c < 8:
                rs = lax.rsqrt(jnp.sum(sv * sv, axis=1, keepdims=True) + EPS)
                n = sv * rs
                if c < 4:
                    q_ref[:, lanes] = n * (DN_HD ** -0.5)
                else:
                    k_ref[:, slice(128 * (c - 4), 128 * (c - 3))] = n
            else:
                v_ref[:, slice(128 * (c - 8), 128 * (c - 7))] = sv
        bg_ref[...] = _beta_g(ba_ref[...], al_ref[...], dt_ref[...])[0]

    return pl.pallas_call(
        body, grid=(s // tr,),
        in_specs=[pl.BlockSpec((tr, 1536), lambda i: (i, 0)),
                  pl.BlockSpec((8, 1536), lambda i: (jnp.maximum(i * nh - 1, 0), 0)),
                  pl.BlockSpec((tr, 128), lambda i: (i, COL_BA_128)),
                  pl.BlockSpec((8, 1536), lambda i: (0, 0)),
                  pl.BlockSpec((1, 128), lambda i: (0, 0)),
                  pl.BlockSpec((1, 128), lambda i: (0, 0))],
        out_specs=[pl.BlockSpec((tr, 512), lambda i: (i, 0))] * 3 + [pl.BlockSpec((tr, 128), lambda i: (i, 0))],
        out_shape=[_sds((s, 512))] * 3 + [_sds((s, 128))],
        compiler_params=_cp(("parallel",)), name="dn_pre")(proj, proj, proj, conv_w8, alog_l, dtb_l)


CPS = 8
CPS_SCAN = 8
SPLIT_ITERS = 2


def _split3(a):
    hi = a.astype(BF16)
    r1 = a - hi.astype(F32)
    mid = r1.astype(BF16)
    return hi, mid, (r1 - mid.astype(F32)).astype(BF16)


def _lane_select():
    r = np.arange(128)
    return jnp.asarray((r[None, :, None] == np.arange(8)[:, None, None]) * np.ones((1, 1, 128)), dtype=BF16)


def _lane_bcast(a3, sel):
    return _dot(a3[0], sel) + _dot(a3[1], sel) + _dot(a3[2], sel)


def _rowsum_b(z, ones_b):
    hi, lo = _split(z)
    return _dot(hi, ones_b) + _dot(lo, ones_b)


def _chunk_cumsum(bg, bgt, tri):
    return _split3(bg), _split3(_mmx(tri, bg)), _mmx_nt(bgt, tri)


def _chunk_common(bg3, gc3, gc_row, h, sel_ref):
    gcc = _lane_bcast(gc3, sel_ref[DN_HEADS + h])
    beta = _dot(bg3[0], sel_ref[h]) + _dot(bg3[1], sel_ref[h])
    gcr = gc_row[DN_HEADS + h:DN_HEADS + h + 1, :]
    ii = lax.broadcasted_iota(jnp.int32, (CHUNK, CHUNK), 0)
    jj = lax.broadcasted_iota(jnp.int32, (CHUNK, CHUNK), 1)
    incl = ii >= jj
    strict = ii > jj
    decay = jnp.exp(jnp.where(incl, gcc[:, 0:CHUNK] - gcr, NEG))
    gl = gcc[CHUNK - 1:CHUNK, :]
    return gcc, beta, incl, strict, decay, gl


def _dn_prep(qn, kn, v, bg, bgt, tri, sel):
    s = qn.shape[0]
    nc = s // CHUNK

    def body(q_ref, k_ref, v_ref, bg_ref, bgt_ref, tri_ref, sel_ref,
             u_ref, w_ref, qd_ref, kt_ref, attn_ref, t_ref, gl_ref):
        tri_v = tri_ref[...]
        ii = lax.broadcasted_iota(jnp.int32, (CHUNK, CHUNK), 0)
        jj = lax.broadcasted_iota(jnp.int32, (CHUNK, CHUNK), 1)
        eye = (ii == jj).astype(F32)

        def chain(cc, h, bg3, gc3, gc_row):
            rows = slice(CHUNK * cc, CHUNK * cc + CHUNK)
            lanes = slice(128 * h, 128 * h + 128)
            gcc, beta, incl, strict, decay, gl = _chunk_common(bg3, gc3, gc_row, h, sel_ref)
            yield
            q = q_ref[rows, lanes]
            k = k_ref[rows, lanes]
            vv = v_ref[rows, lanes]
            kb = k * beta
            egc = jnp.exp(gcc)
            a_mat = jnp.where(strict, _mm_nt(kb, k) * decay, 0.0)
            attn_ref[cc, h] = jnp.where(incl, _mm_nt(q, k) * decay, 0.0)
            qd_ref[rows, lanes] = (q * egc).astype(BF16)
            kt_ref[rows, lanes] = (k * jnp.exp(gl - gcc)).astype(BF16)
            gl_ref[cc, h] = jnp.exp(gl)
            yield
            p = -a_mat
            t = eye + p
            for it in range(5):
                if it < SPLIT_ITERS:
                    ps = _split(p)
                    p = _mm3(ps, ps)
                    yield
                    t = t + _mm3(_split(t), _split(p))
                else:
                    p = _mm(p, p)
                    yield
                    t = t + _mm(t, p)
                yield
            t_ref[cc, h] = t
            ts = _split(t)
            u_ref[rows, lanes] = _mm3(ts, _split(vv * beta))
            w_ref[rows, lanes] = _mm3(ts, _split(kb * egc)).astype(BF16)

        gens = []
        for cc in range(CPS):
            bgv = bg_ref[CHUNK * cc:CHUNK * cc + CHUNK, :]
            bg3, gc3, gc_row = _chunk_cumsum(bgv, bgt_ref[cc], tri_v)
            gens += [chain(cc, h, bg3, gc3, gc_row) for h in range(DN_HEADS)]
        _interleave(gens)

    rows_step = CPS * CHUNK
    big = pl.BlockSpec((rows_step, 512), lambda n: (n, 0))
    sq = pl.BlockSpec((CPS, DN_HEADS, CHUNK, CHUNK), lambda n: (n, 0, 0, 0))
    return pl.pallas_call(
        body, grid=(nc // CPS,),
        in_specs=[big, big, big, pl.BlockSpec((rows_step, 128), lambda n: (n, 0)),
                  pl.BlockSpec((CPS, 8, CHUNK), lambda n: (n, 0, 0)),
                  pl.BlockSpec((CHUNK, CHUNK), lambda n: (0, 0)),
                  pl.BlockSpec((8, 128, 128), lambda n: (0, 0, 0))],
        out_specs=[big, big, big, big, sq, sq, pl.BlockSpec((CPS, DN_HEADS, 1, 128), lambda n: (n, 0, 0, 0))],
        out_shape=[_sds((s, 512))] + [_sds((s, 512), BF16)] * 3 + [_sds((nc, DN_HEADS, CHUNK, CHUNK))] * 2
        + [_sds((nc, DN_HEADS, 1, 128))],
        compiler_params=_cp(("parallel",)), name="dn_prep")(qn, kn, v, bg, bgt, tri, sel)


def _dn_scan(u, w, qd, kt, attn, gl):
    s = u.shape[0]
    nc = s // CHUNK

    def body(u_ref, w_ref, qd_ref, kt_ref, attn_ref, gl_ref, o_ref, vn_ref, sp_ref, st_ref):
        n = pl.program_id(0)

        @pl.when(n == 0)
        def _():
            st_ref[...] = jnp.zeros_like(st_ref)

        def chain(cc, h):
            rows = slice(CHUNK * cc, CHUNK * cc + CHUNK)
            lanes = slice(128 * h, 128 * h + 128)
            st = st_ref[h]
            sp_ref[cc, h] = st
            stb = st.astype(BF16)
            ws = _dot(w_ref[rows, lanes].astype(BF16), stb)
            qs = _dot(qd_ref[rows, lanes].astype(BF16), stb)
            yield
            vn = u_ref[rows, lanes] - ws
            vnb = vn.astype(BF16)
            vn_ref[rows, lanes] = vnb
            o_ref[rows, lanes] = qs + _dot(attn_ref[cc, h].astype(BF16), vnb)
            st_ref[h] = st * gl_ref[cc, h] + _dot_tn(kt_ref[rows, lanes].astype(BF16), vnb)

        for cc in range(CPS_SCAN):
            _interleave([chain(cc, h) for h in range(DN_HEADS)])

    big = pl.BlockSpec((CPS_SCAN * CHUNK, 512), lambda n: (n, 0))
    return pl.pallas_call(
        body, grid=(nc // CPS_SCAN,),
        in_specs=[big, big, big, big,
                  pl.BlockSpec((CPS_SCAN, DN_HEADS, CHUNK, CHUNK), lambda n: (n, 0, 0, 0)),
                  pl.BlockSpec((CPS_SCAN, DN_HEADS, 1, 128), lambda n: (n, 0, 0, 0))],
        out_specs=[big, big, pl.BlockSpec((CPS_SCAN, DN_HEADS, DN_HD, DN_HD), lambda n: (n, 0, 0, 0))],
        out_shape=[_sds((s, 512)), _sds((s, 512), BF16), _sds((nc, DN_HEADS, DN_HD, DN_HD))],
        scratch_shapes=[pltpu.VMEM((DN_HEADS, DN_HD, DN_HD), F32)],
        compiler_params=_cp(("arbitrary",)), name="dn_scan")(u, w, qd, kt, attn, gl)


R4 = PATTERNS[1][1]
R16 = PATTERNS[2][1]
TM = 1024


def _pattern_spec(r, width=512):
    return pl.BlockSpec((r, TM // r, width), lambda i: (0, i, 0))


def _pattern_shape(s, r, dtype=F32, width=512):
    return _sds((r, s // r, width), dtype)


SLABS = pltpu.VMEM((4, TM, 128), F32)

HEAD_SLOT = 128 // ATT_HEADS


def _head_expand():
    src = np.arange(128)[:, None]
    dst = np.arange(512)[None, :]
    return jnp.asarray((src == (dst // ATT_HD) * HEAD_SLOT).astype(np.float32), dtype=BF16)


def _head_compact():
    src = np.arange(512)[:, None]
    dst = np.arange(128)[None, :]
    return jnp.asarray((src // ATT_HD == dst // HEAD_SLOT).astype(np.float32), dtype=BF16)


def _to_patterns(val, dsts, scr):
    for c in range(val.shape[1] // 128):
        lanes = slice(128 * c, 128 * c + 128)
        scr[c] = val[:, lanes]
        for dst_ref, r in dsts:
            for a in range(r):
                dst_ref[a, :, lanes] = scr[c, pl.ds(a, TM // r, stride=r), :].astype(dst_ref.dtype)


def _from_pattern(src_ref, r, scr):
    n_slab = src_ref.shape[2] // 128
    for c in range(n_slab):
        for a in range(r):
            scr[c, pl.ds(a, TM // r, stride=r), :] = src_ref[a, :, 128 * c:128 * c + 128].astype(F32)
    return jnp.concatenate([scr[c] for c in range(n_slab)], axis=1) if n_slab > 1 else scr[0]


def _att_pre(proj, qw_t, kw_t, bd64):
    s = proj.shape[0]

    def body(q_ref, k_ref, v_ref, qw_ref, kw_ref, bd_ref,
             q1_ref, k1_ref, v1_ref, q4_ref, k4_ref, v4_ref, q16_ref, k16_ref, v16_ref, scr):
        bd = bd_ref[...]
        q = q_ref[...]
        k = k_ref[...]
        qn = q * _seg_rstd(q, bd, ATT_HD) * qw_ref[...] * (ATT_HD ** -0.5)
        kn = k * _seg_rstd(k, bd, ATT_HD) * kw_ref[...]
        q1_ref[...] = qn.astype(BF16)
        k1_ref[...] = kn.astype(BF16)
        v1_ref[...] = v_ref[...].astype(BF16)
        _to_patterns(qn, ((q4_ref, R4), (q16_ref, R16)), scr)
        _to_patterns(kn, ((k4_ref, R4), (k16_ref, R16)), scr)
        _to_patterns(v_ref[...], ((v4_ref, R4), (v16_ref, R16)), scr)

    row = pl.BlockSpec((1, 512), lambda i: (0, 0))
    tok = pl.BlockSpec((TM, 512), lambda i: (i, 0))
    return pl.pallas_call(
        body, grid=(s // TM,),
        in_specs=[pl.BlockSpec((TM, 512), lambda i: (i, COL_ATT_Q)),
                  pl.BlockSpec((TM, 512), lambda i: (i, COL_ATT_K)),
                  pl.BlockSpec((TM, 512), lambda i: (i, COL_ATT_V)),
                  row, row, pl.BlockSpec((512, 128), lambda i: (0, 0))],
        out_specs=[tok] * 3 + [_pattern_spec(R4)] * 3 + [_pattern_spec(R16)] * 3,
        out_shape=[_sds((s, 512), BF16)] * 3 + [_pattern_shape(s, R4, BF16)] * 3 + [_pattern_shape(s, R16, BF16)] * 3,
        scratch_shapes=[SLABS],
        compiler_params=_cp(("parallel",)), name="att_pre")(proj, proj, proj, qw_t, kw_t, bd64)


def _bias_table(rb_ref, bk_ref, o_ref):
    for p in range(len(PATTERNS)):
        bk = bk_ref[p]
        for h in range(ATT_HEADS):
            acc = jnp.zeros((BLK, 2 * BLK), F32)
            for b in range(N_BUCKETS):
                acc = jnp.where(bk == b, rb_ref[h, b], acc)
            o_ref[p, h] = acc


def _bias_grad(ds_refs, bk_ref, o_ref):
    for h in range(ATT_HEADS):
        for b in range(N_BUCKETS):
            tot = jnp.zeros((), F32)
            for p, ds_ref in enumerate(ds_refs):
                tot = tot + jnp.sum(jnp.where(bk_ref[p] == b, ds_ref[h], 0.0))
            o_ref[h, b] = tot


QB_FWD = 2
QB_BWD = 4


def _att_masks(has_prev):
    qi = lax.broadcasted_iota(jnp.int32, (BLK, BLK), 0)
    kj = lax.broadcasted_iota(jnp.int32, (BLK, BLK), 1)
    lane = lax.broadcasted_iota(jnp.int32, (BLK, 2 * ATT_HD), 1)
    return jnp.logical_and(kj >= qi, has_prev), kj <= qi, lane < ATT_HD


def _head_lanes(h):
    half = h % 2
    return slice(ATT_HD * h, ATT_HD * h + ATT_HD), slice(ATT_HD * half, ATT_HD * half + ATT_HD)


def _att_scores(qm, kp2, kc2, bias_h, mask_prev, mask_cur):
    s_prev = jnp.where(mask_prev, _dot_nt(qm, kp2) + bias_h[:, :BLK], NEG)
    s_cur = jnp.where(mask_cur, _dot_nt(qm, kc2) + bias_h[:, BLK:], NEG)
    return s_prev, s_cur


def _att_fwd(q, k, v, v_col, bias, p_idx, r, name):
    QB = QB_FWD
    s = q.shape[0]
    nblk = s // BLK
    nseq = nblk // r

    def body(q_ref, kp_ref, kc_ref, vp_ref, vc_ref, b_ref, o_ref, lse_ref):
        j = pl.program_id(0)

        def head(h, rows, masks, q2, kp2, kc2, vp2, vc2):
            mask_prev, mask_cur, lo_half = masks
            out_l, pair_l = _head_lanes(h)
            sel = lo_half if h % 2 == 0 else jnp.logical_not(lo_half)
            qm = jnp.where(sel, q2, jnp.zeros_like(q2))
            s_prev, s_cur = _att_scores(qm, kp2, kc2, b_ref[0, h], mask_prev, mask_cur)
            yield
            m = jnp.maximum(jnp.max(s_prev, axis=1, keepdims=True), jnp.max(s_cur, axis=1, keepdims=True))
            p_prev = jnp.exp(s_prev - m)
            p_cur = jnp.exp(s_cur - m)
            l = jnp.sum(p_prev, axis=1, keepdims=True) + jnp.sum(p_cur, axis=1, keepdims=True)
            yield
            o2 = _dot(p_prev.astype(BF16), vp2) + _dot(p_cur.astype(BF16), vc2)
            o_ref[rows, out_l] = (o2 * (1.0 / l))[:, pair_l].astype(BF16)
            lse_ref[rows, HEAD_SLOT * h:HEAD_SLOT * h + HEAD_SLOT] = jnp.broadcast_to(m + jnp.log(l), (BLK, HEAD_SLOT))

        for sub in range(QB):
            rows = slice(BLK * sub, BLK * sub + BLK)
            before = slice(BLK * (sub - 1), BLK * sub)
            masks = _att_masks(((QB * j + sub) % nseq) != 0)
            gens = []
            for pp in range(ATT_HEADS // 2):
                lanes = slice(128 * pp, 128 * pp + 128)
                kp = kp_ref[:, lanes] if sub == 0 else kc_ref[before, lanes]
                vp = vp_ref[:, lanes] if sub == 0 else vc_ref[before, lanes]
                slabs = (q_ref[rows, lanes], kp, kc_ref[rows, lanes], vp, vc_ref[rows, lanes])
                gens += [head(2 * pp, rows, masks, *slabs), head(2 * pp + 1, rows, masks, *slabs)]
            _interleave(gens)

    cur = pl.BlockSpec((QB * BLK, 512), lambda j: (j, 0))
    prev = pl.BlockSpec((BLK, 512), lambda j: (jnp.maximum(QB * j - 1, 0), 0))
    vcur = pl.BlockSpec((QB * BLK, 512), lambda j: (j, v_col))
    vprev = pl.BlockSpec((BLK, 512), lambda j: (jnp.maximum(QB * j - 1, 0), v_col))
    return pl.pallas_call(
        body, grid=(nblk // QB,),
        in_specs=[cur, prev, cur, vprev, vcur,
                  pl.BlockSpec((1, ATT_HEADS, BLK, 2 * BLK), lambda j: (p_idx, 0, 0, 0))],
        out_specs=[cur, pl.BlockSpec((QB * BLK, 128), lambda j: (j, 0))],
        out_shape=[_sds((s, 512), BF16), _sds((s, 128))],
        compiler_params=_cp(("parallel",)), name=name)(q, k, k, v, v, bias)


def _post_fwd(o_dn, proj, o_pats, lse_pats, dnw_t, bd128):
    s = o_dn.shape[0]

    def body(o_ref, z_ref, gate_ref, o1_ref, o4_ref, o16_ref, s1_ref, s4_ref, s16_ref, wn_ref, bd_ref, ex_ref,
             mixed_ref, oatt_ref, l1_ref, l4_ref, l16_ref, scr_a, scr_b, scr_c, scr_d):
        o = o_ref[...]
        z = z_ref[...]
        rstd = _seg_rstd(o, bd_ref[...], DN_HD)
        y_dn = o * rstd * wn_ref[...] * (z * _sigmoid(z))
        mixed_ref[:, 0:512] = y_dn.astype(BF16)
        lses = (s1_ref[...], _from_pattern(s4_ref, R4, scr_a), _from_pattern(s16_ref, R16, scr_b))
        m = jnp.maximum(jnp.maximum(lses[0], lses[1]), lses[2])
        tot = jnp.exp(lses[0] - m) + jnp.exp(lses[1] - m) + jnp.exp(lses[2] - m)
        big_l = m + jnp.log(tot)
        l1_ref[...] = big_l
        _to_patterns(big_l, ((l4_ref, R4), (l16_ref, R16)), scr_a)
        ex = ex_ref[...]
        outs = (o1_ref[...], _from_pattern(o4_ref, R4, scr_c), _from_pattern(o16_ref, R16, scr_d))
        acc = jnp.zeros_like(o)
        for lse_p, o_p in zip(lses, outs):
            acc = acc + _lane_bcast(_split3(jnp.exp(lse_p - big_l)), ex) * o_p
        gate = gate_ref[...]
        oatt_ref[...] = acc
        mixed_ref[:, 512:1024] = (acc * (gate * _sigmoid(gate))).astype(BF16)

    blk = pl.BlockSpec((TM, 512), lambda i: (i, 0))
    cblk = pl.BlockSpec((TM, 128), lambda i: (i, 0))
    p4, p16 = _pattern_spec(R4), _pattern_spec(R16)
    c4, c16 = _pattern_spec(R4, 128), _pattern_spec(R16, 128)
    return pl.pallas_call(
        body, grid=(s // TM,),
        in_specs=[blk, pl.BlockSpec((TM, 512), lambda i: (i, COL_Z)),
                  pl.BlockSpec((TM, 512), lambda i: (i, COL_GATE)), blk, p4, p16, cblk, c4, c16,
                  pl.BlockSpec((1, 512), lambda i: (0, 0)), pl.BlockSpec((512, 128), lambda i: (0, 0)),
                  pl.BlockSpec((128, 512), lambda i: (0, 0))],
        out_specs=[pl.BlockSpec((TM, D_MODEL), lambda i: (i, 0)), blk, cblk, c4, c16],
        out_shape=[_sds((s, D_MODEL), BF16), _sds((s, 512)), _sds((s, 128)), _pattern_shape(s, R4, F32, 128),
                   _pattern_shape(s, R16, F32, 128)],
        scratch_shapes=[SLABS] * 4,
        compiler_params=_cp(("parallel",)), name="post_fwd")(o_dn, proj, proj, *o_pats, *lse_pats, dnw_t, bd128,
                                                              _head_expand())


def _out_fwd(x, mixed, w_out_bf, tgt):
    s = x.shape[0]
    tm = 512

    def body(x_ref, m_ref, w_ref, t_ref, dy_ref, loss_ref):
        i = pl.program_id(0)

        @pl.when(i == 0)
        def _():
            loss_ref[...] = jnp.zeros_like(loss_ref)

        y = x_ref[...] + jnp.dot(m_ref[...], w_ref[...], preferred_element_type=F32)
        err = y - t_ref[...]
        dy_ref[...] = err * (1.0 / D_MODEL)
        part = 0.5 * jnp.sum(jnp.mean(err * err, axis=-1, keepdims=True), axis=0, keepdims=True)
        loss_ref[...] = loss_ref[...] + part

    blk = pl.BlockSpec((tm, D_MODEL), lambda i: (i, 0))
    return pl.pallas_call(
        body, grid=(s // tm,),
        in_specs=[blk, blk, pl.BlockSpec((D_MODEL, D_MODEL), lambda i: (0, 0)), blk],
        out_specs=[blk, pl.BlockSpec((8, 128), lambda i: (0, 0))],
        out_shape=[_sds((s, D_MODEL)), _sds((8, 128))],
        compiler_params=_cp(("arbitrary",)), name="out_fwd")(x, mixed, w_out_bf, tgt)


def _out_bwd(dy, mixed, w_out_bf):
    s = dy.shape[0]
    tm = 512

    def body(dy_ref, m_ref, w_ref, dm_ref, dw_ref):
        i = pl.program_id(0)

        @pl.when(i == 0)
        def _():
            dw_ref[...] = jnp.zeros_like(dw_ref)

        dyb = dy_ref[...].astype(BF16)
        dm_ref[...] = lax.dot_general(dyb, w_ref[...], (((1,), (1,)), ((), ())), preferred_element_type=F32)
        dw_ref[...] = dw_ref[...] + lax.dot_general(m_ref[...], dyb, (((0,), (0,)), ((), ())),
                                                    preferred_element_type=F32)

    blk = pl.BlockSpec((tm, D_MODEL), lambda i: (i, 0))
    full = pl.BlockSpec((D_MODEL, D_MODEL), lambda i: (0, 0))
    return pl.pallas_call(
        body, grid=(s // tm,), in_specs=[blk, blk, full], out_specs=[blk, full],
        out_shape=[_sds((s, D_MODEL)), _sds((D_MODEL, D_MODEL))],
        compiler_params=_cp(("arbitrary",)), name="out_bwd")(dy, mixed, w_out_bf)


def _post_bwd(dmixed, o_dn, proj, o_att, dnw_t, bd128):
    s = o_dn.shape[0]
    tm = TM

    def body(ddn_ref, datt_ref, o_ref, z_ref, gate_ref, oatt_ref, wn_ref, bd128_ref, hc_ref,
             do_ref, dz_ref, dgate_ref, doatt_ref, do4_ref, do16_ref, delta_ref, dl4_ref, dl16_ref, dnw_ref, scr):
        i = pl.program_id(0)

        @pl.when(i == 0)
        def _():
            dnw_ref[...] = jnp.zeros_like(dnw_ref)

        bd128v = bd128_ref[...]
        o = o_ref[...]
        z = z_ref[...]
        wn = wn_ref[...]
        dy = ddn_ref[...]
        rstd = _seg_rstd(o, bd128v, DN_HD)
        nrm = o * rstd
        sz = _sigmoid(z)
        dz_ref[...] = (dy * nrm * wn * _silu_grad(z, sz)).astype(BF16)
        dn = dy * z * sz
        gw = dn * wn
        do_ref[...] = rstd * (gw - nrm * _seg_expand(_seg_mean(gw * nrm, bd128v, DN_HD), bd128v, DN_HD))
        colsum = jnp.sum(dn * nrm, axis=0, keepdims=True)
        fold = colsum[:, 0:128] + colsum[:, 128:256] + colsum[:, 256:384] + colsum[:, 384:512]
        dnw_ref[...] = dnw_ref[...] + fold
        dya = datt_ref[...]
        gate = gate_ref[...]
        oatt = oatt_ref[...]
        sg = _sigmoid(gate)
        dgate_ref[...] = (dya * oatt * _silu_grad(gate, sg)).astype(BF16)
        doa = dya * gate * sg
        doatt_ref[...] = doa.astype(BF16)
        delta = _segsum(doa * oatt, hc_ref[...])
        delta_ref[...] = delta
        _to_patterns(doa, ((do4_ref, R4), (do16_ref, R16)), scr)
        _to_patterns(delta, ((dl4_ref, R4), (dl16_ref, R16)), scr)

    blk = pl.BlockSpec((tm, 512), lambda i: (i, 0))
    cblk = pl.BlockSpec((tm, 128), lambda i: (i, 0))
    p4, p16 = _pattern_spec(R4), _pattern_spec(R16)
    c4, c16 = _pattern_spec(R4, 128), _pattern_spec(R16, 128)
    return pl.pallas_call(
        body, grid=(s // tm,),
        in_specs=[blk, pl.BlockSpec((tm, 512), lambda i: (i, 1)), blk,
                  pl.BlockSpec((tm, 512), lambda i: (i, COL_Z)), pl.BlockSpec((tm, 512), lambda i: (i, COL_GATE)),
                  blk, pl.BlockSpec((1, 512), lambda i: (0, 0)), pl.BlockSpec((512, 128), lambda i: (0, 0)),
                  pl.BlockSpec((512, 128), lambda i: (0, 0))],
        out_specs=[blk, blk, blk, blk, p4, p16, cblk, c4, c16, pl.BlockSpec((8, 128), lambda i: (0, 0))],
        out_shape=[_sds((s, 512))] + [_sds((s, 512), BF16)] * 3 + [_pattern_shape(s, R4, BF16),
                                          _pattern_shape(s, R16, BF16), _sds((s, 128)),
                                          _pattern_shape(s, R4, F32, 128), _pattern_shape(s, R16, F32, 128),
                                          _sds((8, 128))],
        scratch_shapes=[SLABS],
        compiler_params=_cp(("arbitrary",)), name="post_bwd")(dmixed, dmixed, o_dn, proj, proj, o_att, dnw_t,
                                                               bd128, _head_compact())


def _att_bwd(q, k, v, v_col, do, big_l, delta, bias, p_idx, r, name):
    QB = QB_BWD
    s = q.shape[0]
    nblk = s // BLK
    nseq = nblk // r
    nstep = nblk // QB

    def body(q_ref, kp_ref, kc_ref, vp_ref, vc_ref, do_ref, l_ref, dl_ref, b_ref,
             dq_ref, dk_ref, dv_ref, ds_ref, dkc_ref, dvc_ref):
        j = pl.program_id(0)

        @pl.when(j == 0)
        def _():
            dkc_ref[...] = jnp.zeros_like(dkc_ref)
            dvc_ref[...] = jnp.zeros_like(dvc_ref)
            ds_ref[...] = jnp.zeros_like(ds_ref)

        @pl.when(j < nstep)
        def _():
            def head(h, sub, masks, q2, do2, kp2, kc2, vp2, vc2):
                mask_prev, mask_cur, lo_half = masks
                rows = slice(BLK * sub, BLK * sub + BLK)
                out_l, pair_l = _head_lanes(h)
                sel = lo_half if h % 2 == 0 else jnp.logical_not(lo_half)
                qm = jnp.where(sel, q2, jnp.zeros_like(q2))
                dom = jnp.where(sel, do2, jnp.zeros_like(do2))
                s_prev, s_cur = _att_scores(qm, kp2, kc2, b_ref[0, h], mask_prev, mask_cur)
                dp_prev = _dot_nt(dom, vp2)
                dp_cur = _dot_nt(dom, vc2)
                yield
                lh = l_ref[rows, HEAD_SLOT * h:HEAD_SLOT * h + 1]
                dh = dl_ref[rows, HEAD_SLOT * h:HEAD_SLOT * h + 1]
                p_prev = jnp.exp(s_prev - lh)
                p_cur = jnp.exp(s_cur - lh)
                ds_prev = p_prev * (dp_prev - dh)
                ds_cur = p_cur * (dp_cur - dh)
                ds_ref[h, :, 0:BLK] = ds_ref[h, :, 0:BLK] + ds_prev
                ds_ref[h, :, BLK:2 * BLK] = ds_ref[h, :, BLK:2 * BLK] + ds_cur
                dsb_prev, dsb_cur = ds_prev.astype(BF16), ds_cur.astype(BF16)
                pb_prev, pb_cur = p_prev.astype(BF16), p_cur.astype(BF16)
                yield
                dq_ref[rows, out_l] = (_dot(dsb_prev, kp2) + _dot(dsb_cur, kc2))[:, pair_l].astype(BF16)
                dk_prev = _dot_tn(dsb_prev, q2)[:, pair_l]
                dv_prev = _dot_tn(pb_prev, do2)[:, pair_l]
                if sub == 0:
                    last = slice(BLK * (QB - 1), BLK * QB)
                    dk_ref[last, out_l] = (dkc_ref[last, out_l] + dk_prev).astype(BF16)
                    dv_ref[last, out_l] = (dvc_ref[last, out_l] + dv_prev).astype(BF16)
                else:
                    before = slice(BLK * (sub - 1), BLK * sub)
                    dkc_ref[before, out_l] = dkc_ref[before, out_l] + dk_prev
                    dvc_ref[before, out_l] = dvc_ref[before, out_l] + dv_prev
                yield
                dkc_ref[rows, out_l] = _dot_tn(dsb_cur, q2)[:, pair_l]
                dvc_ref[rows, out_l] = _dot_tn(pb_cur, do2)[:, pair_l]

            done = slice(0, BLK * (QB - 1))
            dk_ref[done, :] = dkc_ref[done, :].astype(BF16)
            dv_ref[done, :] = dvc_ref[done, :].astype(BF16)
            for sub in range(QB):
                rows = slice(BLK * sub, BLK * sub + BLK)
                before = slice(BLK * (sub - 1), BLK * sub)
                masks = _att_masks(((QB * j + sub) % nseq) != 0)
                gens = []
                for pp in range(ATT_HEADS // 2):
                    lanes = slice(128 * pp, 128 * pp + 128)
                    kp = kp_ref[:, lanes] if sub == 0 else kc_ref[before, lanes]
                    vp = vp_ref[:, lanes] if sub == 0 else vc_ref[before, lanes]
                    slabs = (q_ref[rows, lanes], do_ref[rows, lanes], kp, kc_ref[rows, lanes], vp,
                             vc_ref[rows, lanes])
                    gens += [head(2 * pp, sub, masks, *slabs), head(2 * pp + 1, sub, masks, *slabs)]
                _interleave(gens)

        @pl.when(j == nstep)
        def _():
            dk_ref[...] = dkc_ref[...].astype(BF16)
            dv_ref[...] = dvc_ref[...].astype(BF16)

    last_step = nstep - 1
    cur = pl.BlockSpec((QB * BLK, 512), lambda j: (jnp.minimum(j, last_step), 0))
    compact = pl.BlockSpec((QB * BLK, 128), lambda j: (jnp.minimum(j, last_step), 0))
    lag = pl.BlockSpec((QB * BLK, 512), lambda j: (jnp.clip(j - 1, 0, last_step), 0))
    prev = pl.BlockSpec((BLK, 512), lambda j: (jnp.clip(QB * j - 1, 0, nblk - 1), 0))
    vcur = pl.BlockSpec((QB * BLK, 512), lambda j: (jnp.minimum(j, last_step), v_col))
    vprev = pl.BlockSpec((BLK, 512), lambda j: (jnp.clip(QB * j - 1, 0, nblk - 1), v_col))
    return pl.pallas_call(
        body, grid=(nstep + 1,),
        in_specs=[cur, prev, cur, vprev, vcur, cur, compact, compact,
                  pl.BlockSpec((1, ATT_HEADS, BLK, 2 * BLK), lambda j: (p_idx, 0, 0, 0))],
        out_specs=[cur, lag, lag, pl.BlockSpec((ATT_HEADS, BLK, 2 * BLK), lambda j: (0, 0, 0))],
        out_shape=[_sds((s, 512), BF16)] * 3 + [_sds((ATT_HEADS, BLK, 2 * BLK))],
        scratch_shapes=[pltpu.VMEM((QB * BLK, 512), F32), pltpu.VMEM((QB * BLK, 512), F32)],
        compiler_params=_cp(("arbitrary",)), name=name)(q, k, k, v, v, do, big_l, delta, bias)


def _att_pre_bwd(dq_pats, dk_pats, dv_pats, proj, qw_t, kw_t, bd64):
    s = proj.shape[0]
    tm = TM

    def body(dq1_ref, dq4_ref, dq16_ref, dk1_ref, dk4_ref, dk16_ref, dv1_ref, dv4_ref, dv16_ref,
             q_ref, k_ref, qw_ref, kw_ref, bd_ref,
             dqr_ref, dkr_ref, dvr_ref, dqw_ref, dkw_ref, scr4, scr16):
        i = pl.program_id(0)

        @pl.when(i == 0)
        def _():
            dqw_ref[...] = jnp.zeros_like(dqw_ref)
            dkw_ref[...] = jnp.zeros_like(dkw_ref)

        bd = bd_ref[...]

        def total(d1_ref, d4_ref, d16_ref):
            return d1_ref[...] + _from_pattern(d4_ref, R4, scr4) + _from_pattern(d16_ref, R16, scr16)

        def one(d_refs, x_ref, w_ref, scale, dx_ref, dw_ref):
            dy = total(*d_refs) * scale
            x = x_ref[...]
            rstd = _seg_rstd(x, bd, ATT_HD)
            nrm = x * rstd
            dw_ref[...] = dw_ref[...] + jnp.sum(dy * nrm, axis=0, keepdims=True)
            g = dy * w_ref[...]
            dx_ref[...] = (rstd * (g - nrm * _seg_expand(_seg_mean(g * nrm, bd, ATT_HD), bd, ATT_HD))).astype(BF16)

        one((dq1_ref, dq4_ref, dq16_ref), q_ref, qw_ref, ATT_HD ** -0.5, dqr_ref, dqw_ref)
        one((dk1_ref, dk4_ref, dk16_ref), k_ref, kw_ref, 1.0, dkr_ref, dkw_ref)
        dvr_ref[...] = total(dv1_ref, dv4_ref, dv16_ref).astype(BF16)

    blk = pl.BlockSpec((tm, 512), lambda i: (i, 0))
    pats = [blk, _pattern_spec(R4), _pattern_spec(R16)]
    row = pl.BlockSpec((1, 512), lambda i: (0, 0))
    acc = pl.BlockSpec((8, 512), lambda i: (0, 0))
    return pl.pallas_call(
        body, grid=(s // tm,),
        in_specs=pats * 3 + [pl.BlockSpec((tm, 512), lambda i: (i, COL_ATT_Q)),
                             pl.BlockSpec((tm, 512), lambda i: (i, COL_ATT_K)), row, row,
                             pl.BlockSpec((512, 128), lambda i: (0, 0))],
        out_specs=[blk, blk, blk, acc, acc],
        out_shape=[_sds((s, 512), BF16)] * 3 + [_sds((8, 512))] * 2,
        scratch_shapes=[SLABS] * 2,
        compiler_params=_cp(("arbitrary",)), name="att_pre_bwd")(*dq_pats, *dk_pats, *dv_pats, proj, proj,
                                                                  qw_t, kw_t, bd64)


def _dn_scan_bwd(do, sp, qd, kt, w, vn, attn, gl):
    s = do.shape[0]
    nc = s // CHUNK

    def body(do_ref, sp_ref, qd_ref, kt_ref, w_ref, vn_ref, attn_ref, gl_ref,
             du_ref, dqd_ref, dkt_ref, dw_ref, dattn_ref, dgl_ref, ds_ref):
        n = pl.program_id(0)

        @pl.when(n == 0)
        def _():
            ds_ref[...] = jnp.zeros_like(ds_ref)

        def chain(cc, h):
            rows = slice(CHUNK * cc, CHUNK * cc + CHUNK)
            lanes = slice(128 * h, 128 * h + 128)
            dsn = ds_ref[h]
            st = sp_ref[cc, h]
            dsb, stb = dsn.astype(BF16), st.astype(BF16)
            dob = do_ref[rows, lanes].astype(BF16)
            vnb = vn_ref[rows, lanes].astype(BF16)
            dvn = _dot_tn(attn_ref[cc, h].astype(BF16), dob) + _dot(kt_ref[rows, lanes].astype(BF16), dsb)
            du_ref[rows, lanes] = dvn
            dqd_ref[rows, lanes] = _dot_nt(dob, stb)
            dattn_ref[cc, h] = _dot_nt(dob, vnb)
            dkt_ref[rows, lanes] = _dot_nt(vnb, dsb)
            tot = jnp.sum(jnp.sum(st * dsn, axis=1, keepdims=True), axis=0, keepdims=True)
            dgl_ref[cc, h] = jnp.broadcast_to(tot, (1, 128))
            qdo = _dot_tn(qd_ref[rows, lanes].astype(BF16), dob)
            yield
            dvb = dvn.astype(BF16)
            dw_ref[rows, lanes] = -_dot_nt(dvb, stb)
            ds_ref[h] = qdo + dsn * gl_ref[cc, h] - _dot_tn(w_ref[rows, lanes].astype(BF16), dvb)

        for cc in reversed(range(CPS_SCAN)):
            _interleave([chain(cc, h) for h in range(DN_HEADS)])

    nsteps = nc // CPS_SCAN
    big = pl.BlockSpec((CPS_SCAN * CHUNK, 512), lambda n: (nsteps - 1 - n, 0))
    sq = pl.BlockSpec((CPS_SCAN, DN_HEADS, CHUNK, CHUNK), lambda n: (nsteps - 1 - n, 0, 0, 0))
    glb = pl.BlockSpec((CPS_SCAN, DN_HEADS, 1, 128), lambda n: (nsteps - 1 - n, 0, 0, 0))
    return pl.pallas_call(
        body, grid=(nsteps,),
        in_specs=[big, pl.BlockSpec((CPS_SCAN, DN_HEADS, DN_HD, DN_HD), lambda n: (nsteps - 1 - n, 0, 0, 0)),
                  big, big, big, big, sq, glb],
        out_specs=[big, big, big, big, sq, glb],
        out_shape=[_sds((s, 512))] * 4 + [_sds((nc, DN_HEADS, CHUNK, CHUNK)), _sds((nc, DN_HEADS, 1, 128))],
        scratch_shapes=[pltpu.VMEM((DN_HEADS, DN_HD, DN_HD), F32)],
        compiler_params=_cp(("arbitrary",)), name="dn_scan_bwd")(do, sp, qd, kt, w, vn, attn, gl)


def _dn_prep_bwd(qn, kn, v, bg, bgt, tri, sel, t_inv, attn, u, w, du, dw, dqd, dkt, dattn, dgl):
    s = qn.shape[0]
    nc = s // CHUNK

    def body(q_ref, k_ref, v_ref, bg_ref, bgt_ref, tri_ref, sel_ref, t_ref, attn_ref, u_ref, w_ref,
             du_ref, dw_ref, dqd_ref, dkt_ref, dattn_ref, dgl_ref,
             dq_ref, dk_ref, dv_ref, dbg_ref):
        tri_v = tri_ref[...]
        lane = lax.broadcasted_iota(jnp.int32, (CHUNK, 128), 1)
        rowi = lax.broadcasted_iota(jnp.int32, (CHUNK, 128), 0)
        ones_b = jnp.ones((CHUNK, 128), BF16)
        ones_sq = jnp.ones((128, 128), BF16)
        parts = [[] for _ in range(CPS)]

        def chain(cc, h, bg3, gc3, gc_row):
            rows = slice(CHUNK * cc, CHUNK * cc + CHUNK)
            lanes = slice(128 * h, 128 * h + 128)
            gcc, beta, incl, strict, decay, gl = _chunk_common(bg3, gc3, gc_row, h, sel_ref)
            yield
            q = q_ref[rows, lanes]
            k = k_ref[rows, lanes]
            vv = v_ref[rows, lanes]
            ts = _split(t_ref[cc, h])
            egc = jnp.exp(gcc)
            kb = k * beta
            a_mat = jnp.where(strict, _mm_nt(kb, k) * decay, 0.0)
            dvb = _mm3_tn(ts, _split(du_ref[rows, lanes]))
            dkbg = _mm3_tn(ts, _split(dw_ref[rows, lanes]))
            yield
            d_a = jnp.where(strict, -(_mm_nt(dvb, u_ref[rows, lanes]) + _mm_nt(dkbg, w_ref[rows, lanes])), 0.0)
            d_m = d_a * decay
            dattn_m = jnp.where(incl, dattn_ref[cc, h], 0.0)
            dqk = dattn_m * decay
            e_hi, e_lo = _split(d_a * a_mat + dattn_m * attn_ref[cc, h])
            yield
            dkb = _mm(d_m, k)
            dk = _mm_tn(d_m, kb) + _mm_tn(dqk, q)
            dq = _mm(dqk, k)
            e_colsum = _dot_tn(e_hi, ones_b) + _dot_tn(e_lo, ones_b)
            e_rowsum = _dot(e_hi, ones_b) + _dot(e_lo, ones_b)
            dqd = dqd_ref[rows, lanes]
            dkt = dkt_ref[rows, lanes]
            sums = _rowsum_b(jnp.concatenate([dqd * q, dkt * k, dkbg * k, dkb * k, dvb * vv], axis=0), ones_sq)
            s_dqd, s_dkt, rk, s_dkb, s_dvb = (sums[CHUNK * n:CHUNK * n + CHUNK] for n in range(5))
            yield
            tail = jnp.exp(gl - gcc)
            r = s_dkt * tail
            dgl_tot = jnp.sum(r, axis=0, keepdims=True) + dgl_ref[cc, h] * jnp.exp(gl)
            dgc = e_rowsum - e_colsum + s_dqd * egc - r + rk * beta * egc
            dgc = dgc + jnp.where(rowi == CHUNK - 1, dgl_tot, 0.0)
            dq_ref[rows, lanes] = dq + dqd * egc
            dk_ref[rows, lanes] = dk + dkt * tail + dkbg * (beta * egc) + dkb * beta
            dv_ref[rows, lanes] = dvb * beta
            parts[cc].append((h, dgc, rk * egc + s_dkb + s_dvb))

        gens = []
        for cc in range(CPS):
            bgv = bg_ref[CHUNK * cc:CHUNK * cc + CHUNK, :]
            bg3, gc3, gc_row = _chunk_cumsum(bgv, bgt_ref[cc], tri_v)
            gens += [chain(cc, h, bg3, gc3, gc_row) for h in range(DN_HEADS)]
        _interleave(gens)
        for cc in range(CPS):
            dgc_mat = jnp.zeros((CHUNK, 128), F32)
            dbeta_mat = jnp.zeros((CHUNK, 128), F32)
            for h, dgc, dbeta in parts[cc]:
                dgc_mat = dgc_mat + jnp.where(lane == DN_HEADS + h, dgc, 0.0)
                dbeta_mat = dbeta_mat + jnp.where(lane == h, dbeta, 0.0)
            dbg_ref[CHUNK * cc:CHUNK * cc + CHUNK, :] = _mmx_tn(tri_v, dgc_mat) + dbeta_mat

    big = pl.BlockSpec((CPS * CHUNK, 512), lambda n: (n, 0))
    sq = pl.BlockSpec((CPS, DN_HEADS, CHUNK, CHUNK), lambda n: (n, 0, 0, 0))
    glb = pl.BlockSpec((CPS, DN_HEADS, 1, 128), lambda n: (n, 0, 0, 0))
    small = pl.BlockSpec((CPS * CHUNK, 128), lambda n: (n, 0))
    return pl.pallas_call(
        body, grid=(nc // CPS,),
        in_specs=[big, big, big, small, pl.BlockSpec((CPS, 8, CHUNK), lambda n: (n, 0, 0)),
                  pl.BlockSpec((CHUNK, CHUNK), lambda n: (0, 0)),
                  pl.BlockSpec((8, 128, 128), lambda n: (0, 0, 0)), sq, sq, big, big,
                  big, big, big, big, sq, glb],
        out_specs=[big, big, big, small],
        out_shape=[_sds((s, 512))] * 3 + [_sds((s, 128))],
        compiler_params=_cp(("parallel",)), name="dn_prep_bwd")(qn, kn, v, bg, bgt, tri, sel, t_inv, attn, u, w,
                                                                  du, dw, dqd, dkt, dattn, dgl)


def _dn_pre_bwd(dqn, dkn, dv, dbg, proj, conv_w8, alog_l, dtb_l):
    s = proj.shape[0]
    tr = 512
    nh = tr // 8

    def body(dq_ref, dk_ref, dv_ref, dbg_ref, u_ref, halo_ref, ba_ref, w_ref, al_ref, dt_ref,
             dy_ref, dba_ref, dsm_ref):
        i = pl.program_id(0)

        @pl.when(i == 0)
        def _():
            dsm_ref[...] = jnp.zeros_like(dsm_ref)

        keep = (i > 0).astype(F32)
        for c in range(12):
            lanes = slice(128 * c, 128 * c + 128)
            y = _conv_group(u_ref[:, lanes], halo_ref[:, lanes] * keep, w_ref, c)
            sg = _sigmoid(y)
            sv = y * sg
            if c < 8:
                rs = lax.rsqrt(jnp.sum(sv * sv, axis=1, keepdims=True) + EPS)
                n = sv * rs
                if c < 4:
                    dn = dq_ref[:, lanes] * (DN_HD ** -0.5)
                else:
                    dn = dk_ref[:, slice(128 * (c - 4), 128 * (c - 3))]
                dsv = rs * (dn - n * jnp.sum(dn * n, axis=1, keepdims=True))
            else:
                dsv = dv_ref[:, slice(128 * (c - 8), 128 * (c - 7))]
            dy_ref[:, lanes] = dsv * _silu_grad(y, sg)
        _, lane, sig_b, t, nega, g = _beta_g(ba_ref[...], al_ref[...], dt_ref[...])
        dbg = dbg_ref[...]
        da = dbg * nega * _sigmoid(t)
        is_b = lane < DN_HEADS
        is_a = jnp.logical_and(lane >= DN_HEADS, lane < 2 * DN_HEADS)
        dba_ref[...] = jnp.where(is_b, dbg * sig_b * (1.0 - sig_b), jnp.where(is_a, da, 0.0)).astype(BF16)
        d_alog = jnp.sum(jnp.where(is_a, dbg * g, 0.0), axis=0, keepdims=True)
        d_dtb = jnp.sum(jnp.where(is_a, da, 0.0), axis=0, keepdims=True)
        row = lax.broadcasted_iota(jnp.int32, (8, 128), 0)
        dsm_ref[...] = dsm_ref[...] + jnp.where(row == 0, d_alog, jnp.where(row == 1, d_dtb, 0.0))

    blk = pl.BlockSpec((tr, 512), lambda i: (i, 0))
    return pl.pallas_call(
        body, grid=(s // tr,),
        in_specs=[blk, blk, blk, pl.BlockSpec((tr, 128), lambda i: (i, 0)),
                  pl.BlockSpec((tr, 1536), lambda i: (i, 0)),
                  pl.BlockSpec((8, 1536), lambda i: (jnp.maximum(i * nh - 1, 0), 0)),
                  pl.BlockSpec((tr, 128), lambda i: (i, COL_BA_128)),
                  pl.BlockSpec((8, 1536), lambda i: (0, 0)),
                  pl.BlockSpec((1, 128), lambda i: (0, 0)), pl.BlockSpec((1, 128), lambda i: (0, 0))],
        out_specs=[pl.BlockSpec((tr, 1536), lambda i: (i, 0)), pl.BlockSpec((tr, 128), lambda i: (i, 0)),
                   pl.BlockSpec((8, 128), lambda i: (0, 0))],
        out_shape=[_sds((s, 1536)), _sds((s, 128), BF16), _sds((8, 128))],
        compiler_params=_cp(("arbitrary",)), name="dn_pre_bwd")(dqn, dkn, dv, dbg, proj, proj, proj, conv_w8,
                                                                 alog_l, dtb_l)


def _conv_bwd(dy, proj, conv_w8):
    s = dy.shape[0]
    tr = 512
    nh = tr // 8
    nblk = s // tr

    def body(dy_ref, dyn_ref, u_ref, halo_ref, w_ref, du_ref, dw_ref):
        i = pl.program_id(0)

        @pl.when(i == 0)
        def _():
            dw_ref[...] = jnp.zeros_like(dw_ref)

        keep_prev = (i > 0).astype(F32)
        keep_next = (i < nblk - 1).astype(F32)
        row = lax.broadcasted_iota(jnp.int32, (8, 128), 0)
        for c in range(12):
            lanes = slice(128 * c, 128 * c + 128)
            dyc = dy_ref[:, lanes]
            dcat = jnp.concatenate([dyc, dyn_ref[:, lanes] * keep_next], axis=0)
            xcat = jnp.concatenate([halo_ref[:, lanes] * keep_prev, u_ref[:, lanes]], axis=0)
            du = dyc * w_ref[CONV_W - 1:CONV_W, lanes]
            dwc = jnp.where(row == CONV_W - 1, jnp.sum(dyc * u_ref[:, lanes], axis=0, keepdims=True), 0.0)
            for k in range(1, CONV_W):
                du = du + pltpu.roll(dcat, tr + 8 - k, 0)[0:tr] * w_ref[CONV_W - 1 - k:CONV_W - k, lanes]
                ush = pltpu.roll(xcat, k, 0)[8:8 + tr]
                dwc = dwc + jnp.where(row == CONV_W - 1 - k, jnp.sum(dyc * ush, axis=0, keepdims=True), 0.0)
            du_ref[:, lanes] = du.astype(BF16)
            dw_ref[:, lanes] = dw_ref[:, lanes] + dwc

    return pl.pallas_call(
        body, grid=(nblk,),
        in_specs=[pl.BlockSpec((tr, 1536), lambda i: (i, 0)),
                  pl.BlockSpec((8, 1536), lambda i: (jnp.minimum((i + 1) * nh, s // 8 - 1), 0)),
                  pl.BlockSpec((tr, 1536), lambda i: (i, 0)),
                  pl.BlockSpec((8, 1536), lambda i: (jnp.maximum(i * nh - 1, 0), 0)),
                  pl.BlockSpec((8, 1536), lambda i: (0, 0))],
        out_specs=[pl.BlockSpec((tr, 1536), lambda i: (i, 0)), pl.BlockSpec((8, 1536), lambda i: (0, 0))],
        out_shape=[_sds((s, 1536), BF16), _sds((8, 1536))],
        compiler_params=_cp(("arbitrary",)), name="conv_bwd")(dy, dy, proj, proj, conv_w8)


PIECE_WIDTHS = (1536, 512, 512, 512, 512, 512, 128)


def _in_bwd_dx(pieces, w_bf, x, dy, norm_w, ds_accs, buckets, chip_sums=None):
    s = x.shape[0]
    tm = 512
    n_piece = len(PIECE_WIDTHS)
    n_step = s // tm
    n_arr = 0 if chip_sums is None else len(chip_sums)
    n_sent = 3 * n_arr

    def body(*refs):
        refs = list(refs)

        def take(n):
            return [refs.pop(0) for _ in range(n)]

        piece_refs = take(n_piece)
        w_ref, x_ref, dy_ref, nw_ref = take(4)
        ds_refs = take(len(ds_accs))
        bk_ref, = take(1)
        hb_refs = take(n_arr)
        dx_ref, dnw_ref, drb_ref = take(3)
        recv_refs = take(n_arr)
        i = pl.program_id(0)

        def copies():
            send_sems, recv_sems = refs
            xi, yi, ci = _position()
            out = []
            for j, (px, py) in enumerate(_other_chips(xi, yi)):
                for t, (src, dst) in enumerate(zip(hb_refs, recv_refs)):
                    k = n_arr * j + t
                    out.append(pltpu.make_async_remote_copy(
                        src_ref=src.at[2 * px + py], dst_ref=dst.at[j], send_sem=send_sems.at[k],
                        recv_sem=recv_sems.at[k], device_id=(px, py, ci), device_id_type=MESH_ID))
            return out

        @pl.when(i == 0)
        def _():
            dnw_ref[...] = jnp.zeros_like(dnw_ref)
            for cp in (copies() if n_sent else []):
                cp.start()

        @pl.when(i == 1)
        def _():
            _bias_grad(ds_refs, bk_ref, drb_ref)

        dp = jnp.concatenate([r[...] for r in piece_refs], axis=1)
        dh = _dot(dp, w_ref[...])
        xv = x_ref[...]
        rstd = lax.rsqrt(jnp.mean(xv * xv, axis=-1, keepdims=True) + EPS)
        xh = xv * rstd
        dnw_ref[...] = dnw_ref[...] + jnp.sum(dh * xh, axis=0, keepdims=True)
        g = dh * nw_ref[...]
        dx_ref[...] = rstd * (g - xh * jnp.mean(g * xh, axis=-1, keepdims=True)) + dy_ref[...]

        if n_sent:
            @pl.when(i == n_step - 1)
            def _():
                cps = copies()
                for cp in cps:
                    cp.wait_recv()
                for cp in cps:
                    cp.wait_send()

    def blk(n):
        return pl.BlockSpec((tm, n), lambda i: (i, 0))

    vm = pl.BlockSpec(memory_space=pltpu.VMEM)
    in_specs = [blk(n) for n in PIECE_WIDTHS] + [pl.BlockSpec((D_IN_PAD, D_MODEL), lambda i: (0, 0)), blk(D_MODEL),
                                                blk(D_MODEL), pl.BlockSpec((1, D_MODEL), lambda i: (0, 0))]
    in_specs += [vm] * (len(ds_accs) + 1)
    out_specs = [blk(D_MODEL), pl.BlockSpec((8, D_MODEL), lambda i: (0, 0)), pl.BlockSpec(memory_space=pltpu.SMEM)]
    out_shape = [_sds((s, D_MODEL)), _sds((8, D_MODEL)), _sds((ATT_HEADS, N_BUCKETS))]
    scratch = []
    extra = ()
    if n_sent:
        extra = tuple(chip_sums)
        in_specs += [ANY] * n_arr
        out_specs += [ANY] * n_arr
        out_shape += [_sds((3,) + a.shape[1:], a.dtype) for a in extra]
        scratch = [pltpu.SemaphoreType.DMA((n_sent,)), pltpu.SemaphoreType.DMA((n_sent,))]
    return pl.pallas_call(
        body, grid=(n_step,), in_specs=in_specs, out_specs=out_specs, out_shape=out_shape, scratch_shapes=scratch,
        compiler_params=_cp(("arbitrary",)), name="in_bwd_dx")(*pieces, w_bf, x, dy, norm_w, *ds_accs, buckets, *extra)


def _in_bwd_dw(pieces, x, norm_w):
    s = x.shape[0]
    tm = 512
    n_piece = len(PIECE_WIDTHS)

    def body(*refs):
        piece_refs = refs[:n_piece]
        x_ref, nw_ref, dw_ref = refs[n_piece:]
        i = pl.program_id(0)

        @pl.when(i == 0)
        def _():
            dw_ref[...] = jnp.zeros_like(dw_ref)

        xv = x_ref[...]
        rstd = lax.rsqrt(jnp.mean(xv * xv, axis=-1, keepdims=True) + EPS)
        h = (xv * rstd * nw_ref[...]).astype(BF16)
        at = 0
        for r, width in zip(piece_refs, PIECE_WIDTHS):
            dw_ref[at:at + width, :] = dw_ref[at:at + width, :] + _dot_tn(r[...], h)
            at += width

    return pl.pallas_call(
        body, grid=(s // tm,),
        in_specs=[pl.BlockSpec((tm, n), lambda i: (i, 0)) for n in PIECE_WIDTHS]
        + [pl.BlockSpec((tm, D_MODEL), lambda i: (i, 0)), pl.BlockSpec((1, D_MODEL), lambda i: (0, 0))],
        out_specs=pl.BlockSpec((D_IN_PAD, D_MODEL), lambda i: (0, 0)),
        out_shape=_sds((D_IN_PAD, D_MODEL)),
        compiler_params=_cp(("arbitrary",)), name="in_bwd_dw")(*pieces, x, norm_w)


def _flat(a):
    return a.reshape(-1, a.shape[-1])


def _as_pattern(a, r):
    return a if r == 1 else a.reshape(r, a.shape[0] // r, a.shape[1])


D_SHARD = D_IN // N_CHIPS
BA_START = 4 * D_DN
BA_PACKED = 4096
S1_HEAD = BA_START - D_SHARD
S1_BA = 2 * D_SHARD - BA_START


def _pack_rows(g):
    pad = jnp.zeros((D_IN_PAD - D_IN, g.shape[2]), g.dtype)
    s2_ba = 2 * DN_HEADS - S1_BA
    return jnp.concatenate([g[0][:D_SHARD], g[1][:S1_HEAD], g[2][s2_ba:D_SHARD], g[3][:D_SHARD],
                            g[1][S1_HEAD:D_SHARD], g[2][:s2_ba], pad], axis=0)


def _pack_w(g):
    n = g.shape[2]

    def body(g_ref, o_ref):
        g32 = g_ref.bitcast(jnp.uint32)
        o32 = o_ref.bitcast(jnp.uint32)
        full, head, ba1 = D_SHARD // 2, S1_HEAD // 2, S1_BA // 2
        ba2 = DN_HEADS - ba1
        pieces = [(0, 0, full), (1, 0, head), (2, ba2, full), (3, 0, full), (1, head, full), (2, 0, ba2)]
        at = 0
        for chip, lo, hi in pieces:
            o32[at:at + hi - lo, :] = g32[chip, lo:hi, :]
            at += hi - lo
        o32[at:D_IN_PAD // 2, :] = jnp.zeros((D_IN_PAD // 2 - at, n), jnp.uint32)

    vm = pl.BlockSpec(memory_space=pltpu.VMEM)
    return pl.pallas_call(body, in_specs=[vm], out_specs=vm, out_shape=_sds((D_IN_PAD, n), g.dtype),
                          compiler_params=_cp(), name="pack_w")(g)


def _unpack_w(p, rows):
    n = p.shape[1]
    tn = 256
    mid = BA_PACKED + S1_BA
    pieces = [(0, 0, (0, D_SHARD)), (1, 0, (D_SHARD, BA_START)), (1, S1_HEAD, (BA_PACKED, mid)),
              (2, 0, (mid, BA_PACKED + 2 * DN_HEADS)), (2, 2 * DN_HEADS - S1_BA, (BA_START, BA_START + S1_HEAD)),
              (3, 0, (BA_START + S1_HEAD, BA_PACKED))]

    def body(p_ref, o_ref):
        for chip, at, (lo, hi) in pieces:
            o_ref[chip, at:at + hi - lo, :] = p_ref[lo:hi, :]
        for chip in range(N_CHIPS):
            o_ref[chip, D_SHARD:rows, :] = jnp.zeros((rows - D_SHARD, tn), p.dtype)

    return pl.pallas_call(
        body, grid=(n // tn,),
        in_specs=[pl.BlockSpec((D_IN_PAD, tn), lambda j: (0, j))],
        out_specs=pl.BlockSpec((N_CHIPS, rows, tn), lambda j: (0, 0, j)),
        out_shape=_sds((N_CHIPS, rows, n), p.dtype),
        compiler_params=_cp(("parallel",)), name="unpack_w")(p)


def _unpack_rows(p, rows):
    mid = BA_PACKED + S1_BA
    pad = jnp.zeros((rows - D_SHARD, p.shape[1]), p.dtype)
    return jnp.stack([jnp.concatenate([p[0:D_SHARD], pad], axis=0),
                      jnp.concatenate([p[D_SHARD:BA_START], p[BA_PACKED:mid], pad], axis=0),
                      jnp.concatenate([p[mid:BA_PACKED + 2 * DN_HEADS], p[BA_START:BA_START + S1_HEAD], pad], axis=0),
                      jnp.concatenate([p[BA_START + S1_HEAD:BA_PACKED], pad], axis=0)])


def _lane_row(vec, offset):
    return jnp.pad(vec.reshape(1, -1), ((0, 0), (offset, 128 - offset - vec.shape[0])))


def _local_step(x, tgt, norm_w, w_bf, conv_w, a_log, dt_bias, dn_norm_w, q_norm_w, k_norm_w, bias, w_out_bf):
    s = x.shape[0]
    nc = s // CHUNK
    conv_w8 = jnp.pad(conv_w, ((0, 8 - CONV_W), (0, 0)))
    alog_l = _lane_row(a_log.reshape(-1), DN_HEADS)
    dtb_l = _lane_row(dt_bias.reshape(-1), DN_HEADS)
    dnw_t = jnp.tile(dn_norm_w.reshape(1, DN_HD), (1, DN_HEADS))
    qw_t = jnp.tile(q_norm_w.reshape(1, ATT_HD), (1, ATT_HEADS))
    kw_t = jnp.tile(k_norm_w.reshape(1, ATT_HD), (1, ATT_HEADS))
    bd128 = _compact_mat(DN_HD)
    bd64 = _compact_mat(ATT_HD)
    tri = _tri_incl()
    sel = _lane_select()
    buckets = _bucket_tables()

    proj = _in_proj(x, norm_w, w_bf)
    qn, kn, v_dn, bg = _dn_pre(proj, conv_w8, alog_l, dtb_l)
    bgt = bg[:, 0:8].reshape(nc, CHUNK, 8).transpose(0, 2, 1)
    u, w, qd, kt, attn, t_inv, gl = _dn_prep(qn, kn, v_dn, bg, bgt, tri, sel)
    o_dn, vn, sp = _dn_scan(u, w, qd, kt, attn, gl)
    q1, k1, v1, q4, k4, v4, q16, k16, v16 = _att_pre(proj, qw_t, kw_t, bd64)
    rs = [r for _, r in PATTERNS]
    qkv = [(q1, k1, v1, 0), (_flat(q4), _flat(k4), _flat(v4), 0), (_flat(q16), _flat(k16), _flat(v16), 0)]
    o_pats, lse_pats = [], []
    for p, r in enumerate(rs):
        o_p, lse_p = _att_fwd(*qkv[p], bias, p, r, "att_fwd_r%d" % r)
        o_pats.append(_as_pattern(o_p, r))
        lse_pats.append(_as_pattern(lse_p, r))
    mixed, o_att, l1, l4, l16 = _post_fwd(o_dn, proj, o_pats, lse_pats, dnw_t, bd128)
    dy, loss_blk = _out_fwd(x, mixed, w_out_bf, tgt)

    dmixed, d_w_out = _out_bwd(dy, mixed, w_out_bf)
    do_dn, dz, dgate, do1, do4, do16, dl1, dl4, dl16, d_dnw = _post_bwd(dmixed, o_dn, proj, o_att, dnw_t, bd128)
    side = [(do1, l1, dl1), (_flat(do4), _flat(l4), _flat(dl4)), (_flat(do16), _flat(l16), _flat(dl16))]
    dq_pats, dk_pats, dv_pats, ds_accs = [], [], [], []
    for p, r in enumerate(rs):
        dq_p, dk_p, dv_p, ds_p = _att_bwd(*qkv[p], *side[p], bias, p, r, "att_bwd_r%d" % r)
        dq_pats.append(_as_pattern(dq_p, r))
        dk_pats.append(_as_pattern(dk_p, r))
        dv_pats.append(_as_pattern(dv_p, r))
        ds_accs.append(ds_p)
    dq_att, dk_att, dv_att, d_qw, d_kw = _att_pre_bwd(dq_pats, dk_pats, dv_pats, proj, qw_t, kw_t, bd64)
    du, dqd, dkt, dw, dattn, dgl = _dn_scan_bwd(do_dn, sp, qd, kt, w, vn, attn, gl)
    dqn, dkn, dv_dn, dbg = _dn_prep_bwd(qn, kn, v_dn, bg, bgt, tri, sel, t_inv, attn, u, w, du, dw, dqd, dkt, dattn,
                                        dgl)
    dyc, dba, dsm = _dn_pre_bwd(dqn, dkn, dv_dn, dbg, proj, conv_w8, alog_l, dtb_l)
    d_qkv_dn, d_conv8 = _conv_bwd(dyc, proj, conv_w8)
    pieces = (d_qkv_dn, dz, dq_att, dk_att, dv_att, dgate, dba)
    d_w_in_t = _in_bwd_dw(pieces, x, norm_w)
    last = functools.partial(_in_bwd_dx, pieces, w_bf, x, dy, norm_w, ds_accs, buckets)

    grads = dict(
        w_in_t=d_w_in_t,
        conv_w=d_conv8[0:CONV_W, :],
        a_log=dsm[0:1, DN_HEADS:2 * DN_HEADS],
        dt_bias=dsm[1:2, DN_HEADS:2 * DN_HEADS],
        dn_norm_w=d_dnw[0:1, :],
        q_norm_w=d_qw[0:1, :].reshape(ATT_HEADS, ATT_HD),
        k_norm_w=d_kw[0:1, :].reshape(ATT_HEADS, ATT_HD),
        w_out=d_w_out,
    )
    return loss_blk[0, 0], last, grads


MESH_ID = pl.DeviceIdType.MESH
ANY = pl.BlockSpec(memory_space=pl.ANY)


def _position():
    return lax.axis_index("x"), lax.axis_index("y"), lax.axis_index("c")


def _other_chips(x, y):
    return [(1 - x, y), (x, 1 - y), (1 - x, 1 - y)]


SHARD_PAD = 1040
WIN = 528


def _row_split(n):
    return (n // 2) // 128 * 128


def _part(ref, cc):
    n = ref.shape[0]
    sp = _row_split(n)
    return ref.at[pl.ds(0, sp)] if cc == 0 else ref.at[pl.ds(sp, n - sp)]


def _gather_weights(wt_s, w_out_s, conv_s, rel_bias, buckets):
    def body(a_ref, b_ref, c_ref, rb_ref, bk_ref, ga_ref, gb_ref, gc_ref, bias_ref, send_sems, recv_sems, loc_sems,
             a_vmem, b_vmem):
        x, y, c = _position()
        me = 2 * x + y
        sib = (x, y, 1 - c)
        big = ((a_ref, ga_ref), (b_ref, gb_ref))
        stage_in = [pltpu.make_async_copy(a_ref, a_vmem, loc_sems.at[0]),
                    pltpu.make_async_copy(b_ref, b_vmem, loc_sems.at[1])]
        local = [pltpu.make_async_copy(a_vmem, ga_ref.at[me], loc_sems.at[0]),
                 pltpu.make_async_copy(b_vmem, gb_ref.at[me], loc_sems.at[1]),
                 pltpu.make_async_copy(c_ref, gc_ref.at[me], loc_sems.at[2])]
        for cp in stage_in:
            cp.start()
        local[2].start()
        for cp in stage_in:
            cp.wait()
        for cp in local[:2]:
            cp.start()
        others = _other_chips(x, y)

        def exchange(cc):
            sends = []
            for j, (px, py) in enumerate(others):
                for t, (src, dst) in enumerate(big):
                    k = 3 * j + t
                    sends.append(pltpu.make_async_remote_copy(
                        src_ref=_part(src, cc), dst_ref=_part(dst.at[me], cc), send_sem=send_sems.at[k],
                        recv_sem=recv_sems.at[k], device_id=(px, py, c), device_id_type=MESH_ID))
                sends.append(pltpu.make_async_remote_copy(
                    src_ref=c_ref, dst_ref=gc_ref.at[me], send_sem=send_sems.at[3 * j + 2],
                    recv_sem=recv_sems.at[3 * j + 2], device_id=(px, py, c), device_id_type=MESH_ID))
            for cp in sends:
                cp.start()
            _bias_table(rb_ref, bk_ref, bias_ref)
            for j, (px, py) in enumerate(others):
                src_chip = 2 * px + py
                for t, (src, dst) in enumerate(big):
                    landed = _part(dst.at[src_chip], cc)
                    pltpu.make_async_remote_copy(
                        src_ref=_part(src, cc), dst_ref=landed, send_sem=send_sems.at[3 * j + t],
                        recv_sem=recv_sems.at[3 * j + t], device_id=(px, py, c), device_id_type=MESH_ID).wait_recv()
                    k = 9 + 2 * j + t
                    fwd = pltpu.make_async_remote_copy(
                        src_ref=landed, dst_ref=landed, send_sem=send_sems.at[k], recv_sem=recv_sems.at[k],
                        device_id=sib, device_id_type=MESH_ID)
                    fwd.start()
                    sends.append(fwd)
                pltpu.make_async_remote_copy(
                    src_ref=c_ref, dst_ref=gc_ref.at[src_chip], send_sem=send_sems.at[3 * j + 2],
                    recv_sem=recv_sems.at[3 * j + 2], device_id=(px, py, c), device_id_type=MESH_ID).wait_recv()
            for j, (px, py) in enumerate(others):
                src_chip = 2 * px + py
                for t, (src, dst) in enumerate(big):
                    k = 9 + 2 * j + t
                    theirs = _part(dst.at[src_chip], 1 - cc)
                    pltpu.make_async_remote_copy(
                        src_ref=theirs, dst_ref=theirs, send_sem=send_sems.at[k], recv_sem=recv_sems.at[k],
                        device_id=sib, device_id_type=MESH_ID).wait_recv()
            for cp in sends:
                cp.wait_send()

        for cc in (0, 1):
            pl.when(c == cc)(functools.partial(exchange, cc))
        for cp in local:
            cp.wait()

    srcs = (wt_s, w_out_s, conv_s)
    n_sem = 9 + 6
    return pl.pallas_call(
        body, in_specs=[ANY] * 3 + [pl.BlockSpec(memory_space=pltpu.SMEM), pl.BlockSpec(memory_space=pltpu.VMEM)],
        out_specs=[ANY] * 3 + [pl.BlockSpec(memory_space=pltpu.VMEM)],
        out_shape=[_sds((N_CHIPS,) + a.shape, a.dtype) for a in srcs]
        + [_sds((len(PATTERNS), ATT_HEADS, BLK, 2 * BLK))],
        scratch_shapes=[pltpu.SemaphoreType.DMA((n_sem,)), pltpu.SemaphoreType.DMA((n_sem,)),
                        pltpu.SemaphoreType.DMA((3,)), pltpu.VMEM(wt_s.shape, wt_s.dtype),
                        pltpu.VMEM(w_out_s.shape, w_out_s.dtype)],
        compiler_params=_cp(), name="gather_weights")(*srcs, rel_bias, buckets)


N_DEV = 8
PEER_FLIPS = [(dx, dy, dc) for dx in (0, 1) for dy in (0, 1) for dc in (0, 1)][1:]


def _small_copies(s_ref, rs_ref, send_sems, recv_sems):
    x, y, c = _position()
    dev = 4 * x + 2 * y + c
    sends, recvs = [], []
    for f, (dx, dy, dc) in enumerate(PEER_FLIPS):
        peer = (x ^ dx, y ^ dy, c ^ dc)
        sends.append(pltpu.make_async_remote_copy(
            src_ref=s_ref, dst_ref=rs_ref.at[dev], send_sem=send_sems.at[f], recv_sem=recv_sems.at[f],
            device_id=peer, device_id_type=MESH_ID))
        recvs.append(pltpu.make_async_remote_copy(
            src_ref=s_ref, dst_ref=rs_ref.at[4 * peer[0] + 2 * peer[1] + peer[2]], send_sem=send_sems.at[f],
            recv_sem=recv_sems.at[f], device_id=peer, device_id_type=MESH_ID))
    return sends, recvs


def _fill_parts(h_in, r_in, h_out, r_out, small, rows_in, rows_out):
    def body(ha_ref, ra_ref, hb_ref, rb_ref, s_ref, fa_ref, fb_ref, rs_ref, send_sems, recv_sems, loc_sems,
             small_send, small_recv, a_vmem, b_vmem, ra_vmem, rb_vmem):
        x, y, c = _position()
        sib = (x, y, 1 - c)
        chip = 2 * x + y
        pairs = ((a_vmem, fa_ref), (b_vmem, fb_ref))
        own_small = pltpu.make_async_copy(s_ref, rs_ref.at[4 * x + 2 * y + c], loc_sems.at[2])
        own_small.start()
        small_sends, small_recvs = _small_copies(s_ref, rs_ref, small_send, small_recv)
        for cp in small_sends:
            cp.start()
        stage_in = [pltpu.make_async_copy(ha_ref.at[chip], a_vmem, loc_sems.at[0]),
                    pltpu.make_async_copy(hb_ref.at[chip], b_vmem, loc_sems.at[1]),
                    pltpu.make_async_copy(ra_ref, ra_vmem, loc_sems.at[3]),
                    pltpu.make_async_copy(rb_ref, rb_vmem, loc_sems.at[4])]
        for cp in stage_in:
            cp.start()
        for cp in stage_in:
            cp.wait()
        for tot, recv in ((a_vmem, ra_vmem), (b_vmem, rb_vmem)):
            acc = tot[...]
            for j in range(3):
                acc = acc + recv[j].astype(F32)
            tot[...] = acc

        def fill(cc):
            mine = [_part(dst, cc) for _, dst in pairs]
            srcs = [src.at[pl.ds(0, m.shape[0])] for src, m in zip((a_vmem, b_vmem), mine)]
            local = [pltpu.make_async_copy(s, m, loc_sems.at[t]) for t, (s, m) in enumerate(zip(srcs, mine))]
            sends = [pltpu.make_async_remote_copy(src_ref=s, dst_ref=m, send_sem=send_sems.at[t],
                                                  recv_sem=recv_sems.at[t], device_id=sib, device_id_type=MESH_ID)
                     for t, (s, m) in enumerate(zip(srcs, mine))]
            for cp in local + sends:
                cp.start()
            for t, (_, dst) in enumerate(pairs):
                theirs = _part(dst, 1 - cc)
                pltpu.make_async_remote_copy(src_ref=theirs, dst_ref=theirs, send_sem=send_sems.at[t],
                                             recv_sem=recv_sems.at[t], device_id=sib, device_id_type=MESH_ID).wait_recv()
            for cp in sends:
                cp.wait_send()
            for cp in local:
                cp.wait()

        for cc in (0, 1):
            pl.when(c == cc)(functools.partial(fill, cc))
        for cp in small_recvs:
            cp.wait_recv()
        for cp in small_sends:
            cp.wait_send()
        own_small.wait()

    n_peer = len(PEER_FLIPS)
    return pl.pallas_call(
        body, in_specs=[ANY] * 5, out_specs=[ANY] * 3,
        out_shape=[_sds((rows_in, D_MODEL)), _sds((rows_out, D_MODEL)), _sds((N_DEV,) + small.shape, small.dtype)],
        scratch_shapes=[pltpu.SemaphoreType.DMA((2,)), pltpu.SemaphoreType.DMA((2,)), pltpu.SemaphoreType.DMA((5,)),
                        pltpu.SemaphoreType.DMA((n_peer,)), pltpu.SemaphoreType.DMA((n_peer,)),
                        pltpu.VMEM(h_in.shape[1:], F32), pltpu.VMEM(h_out.shape[1:], F32),
                        pltpu.VMEM(r_in.shape, r_in.dtype), pltpu.VMEM(r_out.shape, r_out.dtype)],
        compiler_params=_cp(), name="fill_parts")(h_in, r_in, h_out, r_out, small)


def _chip_sums(u_in, u_out, win_in, win_out):
    wins = (win_in, win_out)

    def body(a_ref, b_ref, ha_ref, hba_ref, hb_ref, hbb_ref, send_sems, recv_sems, loc_sems, out_sems,
             mine_a, theirs_a, sum_a, sumb_a, mine_b, theirs_b, sum_b, sumb_b):
        x, y, c = _position()
        sib = (x, y, 1 - c)
        groups = ((a_ref, mine_a, theirs_a, sum_a, sumb_a, ha_ref, hba_ref, win_in),
                  (b_ref, mine_b, theirs_b, sum_b, sumb_b, hb_ref, hbb_ref, win_out))
        loads, sends = [], []
        for t, (src, mine, theirs, _, _, _, _, win) in enumerate(groups):
            split = _row_split(src.shape[1])
            for k in range(N_CHIPS):
                n = N_CHIPS * t + k
                loads.append(pltpu.make_async_copy(
                    src.at[k, pl.ds(pl.multiple_of(c * split, split), win), :], mine.at[k], loc_sems.at[n]))
                sends.append(pltpu.make_async_remote_copy(
                    src_ref=src.at[k, pl.ds(pl.multiple_of((1 - c) * split, split), win), :], dst_ref=theirs.at[k],
                    send_sem=send_sems.at[n], recv_sem=recv_sems.at[n], device_id=sib, device_id_type=MESH_ID))
        for cp in sends + loads:
            cp.start()
        stores = []
        for t, (_, mine, theirs, tot, totb, h_out, hb_out, _) in enumerate(groups):
            for k in range(N_CHIPS):
                n = N_CHIPS * t + k
                loads[n].wait()
                sends[n].wait_recv()
                val = mine[k] + theirs[k]
                tot[k] = val
                totb[k] = val.astype(BF16)
                stores += [pltpu.make_async_copy(tot.at[k], h_out.at[k], out_sems.at[2 * n]),
                           pltpu.make_async_copy(totb.at[k], hb_out.at[k], out_sems.at[2 * n + 1])]
                stores[-2].start()
                stores[-1].start()
        for cp in sends:
            cp.wait_send()
        for cp in stores:
            cp.wait()

    shapes = [(N_CHIPS, w, D_MODEL) for w in wins]
    n_cp = 2 * N_CHIPS
    vmem = []
    for shp in shapes:
        vmem += [pltpu.VMEM(shp, F32), pltpu.VMEM(shp, F32), pltpu.VMEM(shp, F32), pltpu.VMEM(shp, BF16)]
    return pl.pallas_call(
        body, in_specs=[ANY] * 2, out_specs=[ANY] * 4,
        out_shape=[_sds(shapes[0]), _sds(shapes[0], BF16), _sds(shapes[1]), _sds(shapes[1], BF16)],
        scratch_shapes=[pltpu.SemaphoreType.DMA((n_cp,)), pltpu.SemaphoreType.DMA((n_cp,)),
                        pltpu.SemaphoreType.DMA((n_cp,)), pltpu.SemaphoreType.DMA((2 * n_cp,))] + vmem,
        compiler_params=_cp(), name="chip_sums")(u_in, u_out)


SMALL_LAYOUT = (("norm_w", 1024), ("conv_w", 6144), ("a_log", 128), ("dt_bias", 128), ("dn_norm_w", 128),
                ("q_norm_w", 512), ("k_norm_w", 512), ("rel_bias", 256), ("loss", 128))
SMALL_TOTAL = sum(n for _, n in SMALL_LAYOUT)


def _small_offset(name):
    off = 0
    for n, size in SMALL_LAYOUT:
        if n == name:
            return off
        off += size
    raise KeyError(name)


def _pack_small(grads):
    parts = []
    for name, size in SMALL_LAYOUT:
        flat = grads[name].reshape(1, -1)
        parts.append(jnp.pad(flat, ((0, 0), (0, size - flat.shape[1]))))
    return jnp.concatenate(parts, axis=1)


def _sum_small(rows):
    n_dev = rows.shape[0]
    q_off = _small_offset("q_norm_w")
    k_off = _small_offset("k_norm_w")

    def body(r_ref, tot_ref, qk_ref):
        tot = r_ref[0:1, :]
        for d in range(1, n_dev):
            tot = tot + r_ref[d:d + 1, :]
        tot_ref[...] = tot
        for row, off in ((0, q_off), (1, k_off)):
            s4 = tot[:, off:off + 128] + tot[:, off + 128:off + 256] + tot[:, off + 256:off + 384] \
                + tot[:, off + 384:off + 512]
            qk_ref[row:row + 1, :] = s4 + pltpu.roll(s4, ATT_HD, 1)

    return pl.pallas_call(
        body, in_specs=[pl.BlockSpec(memory_space=pltpu.VMEM)],
        out_specs=[pl.BlockSpec(memory_space=pltpu.VMEM)] * 2,
        out_shape=[_sds((1, SMALL_TOTAL)), _sds((2, 128))],
        compiler_params=_cp(), name="sum_small")(rows)


def _adamw_math(w, g, m, v):
    m = ADAM_B1 * m + (1.0 - ADAM_B1) * g
    v = ADAM_B2 * v + (1.0 - ADAM_B2) * (g * g)
    m_hat = m / (1.0 - ADAM_B1 ** ADAM_STEP)
    v_hat = v / (1.0 - ADAM_B2 ** ADAM_STEP)
    delta = -ADAM_LR * (m_hat / (jnp.sqrt(v_hat) + ADAM_EPS) + ADAM_WD * w)
    return delta, m, v


def _adamw_big(g, w, m, v, name):
    rows, cols = w.shape
    tr = 128

    def body(g_ref, w_ref, m_ref, v_ref, go_ref, d_ref, nm_ref, nv_ref):
        g = g_ref[...]
        go_ref[...] = g
        d_ref[...], nm_ref[...], nv_ref[...] = _adamw_math(w_ref[...], g, m_ref[...], v_ref[...])

    blk = pl.BlockSpec((tr, cols), lambda i: (i, 0))
    return pl.pallas_call(
        body, grid=(pl.cdiv(rows, tr),), in_specs=[blk] * 4, out_specs=[blk] * 4,
        out_shape=[_sds((rows, cols))] * 4, compiler_params=_cp(("parallel",)), name=name)(g, w, m, v)


def _adamw_rows(g, w, m, v, name):
    rows, cols = w.shape
    tr = 128

    def body(g_ref, w_ref, m_ref, v_ref, go_ref, d_ref, nm_ref, nv_ref, s_g, s_d, s_m, s_v):
        g = g_ref[...]
        s_g[...] = g
        s_d[...], s_m[...], s_v[...] = _adamw_math(w_ref[...], g, m_ref[...], v_ref[...])
        for i in range(tr):
            for scr, out in ((s_g, go_ref), (s_d, d_ref), (s_m, nm_ref), (s_v, nv_ref)):
                out[i] = scr[i:i + 1, :]

    blk = pl.BlockSpec((tr, cols), lambda i: (i, 0))
    oblk = pl.BlockSpec((tr, 1, cols), lambda i: (i, 0, 0))
    return pl.pallas_call(
        body, grid=(pl.cdiv(rows, tr),), in_specs=[blk] * 4, out_specs=[oblk] * 4,
        out_shape=[_sds((rows, 1, cols))] * 4, scratch_shapes=[pltpu.VMEM((tr, cols), F32)] * 4,
        compiler_params=_cp(("parallel",)), name=name)(g, w, m, v)


def _adamw_small(w, g, m, v, name):
    def body(w_ref, g_ref, m_ref, v_ref, d_ref, nm_ref, nv_ref):
        d_ref[...], nm_ref[...], nv_ref[...] = _adamw_math(w_ref[...], g_ref[...], m_ref[...], v_ref[...])

    vm = pl.BlockSpec(memory_space=pltpu.VMEM)
    return pl.pallas_call(body, in_specs=[vm] * 4, out_specs=[vm] * 3, out_shape=[_sds(w.shape)] * 3,
                          compiler_params=_cp(), name=name)(w, g, m, v)


WEIGHTS = ("norm_w", "w_in", "conv_w", "a_log", "dt_bias", "dn_norm_w", "q_norm_w", "k_norm_w", "rel_bias", "w_out")


def kernel(x, norm_w, w_in, conv_w, a_log, dt_bias, dn_norm_w, q_norm_w, k_norm_w, rel_bias, w_out, loss_target, m_norm_w, m_w_in, m_conv_w, m_a_log, m_dt_bias, m_dn_norm_w, m_q_norm_w, m_k_norm_w, m_rel_bias, m_w_out, v_norm_w, v_w_in, v_conv_w, v_a_log, v_dt_bias, v_dn_norm_w, v_q_norm_w, v_k_norm_w, v_rel_bias, v_w_out):
    xi, yi, _ = _position()
    chip = 2 * xi + yi
    w_loc = dict(norm_w=norm_w, w_in=w_in[0].T, conv_w=conv_w[0], a_log=a_log, dt_bias=dt_bias, dn_norm_w=dn_norm_w,
                 q_norm_w=q_norm_w, k_norm_w=k_norm_w, rel_bias=rel_bias, w_out=w_out[0])
    m_loc = dict(norm_w=m_norm_w, w_in=m_w_in[0].T, conv_w=m_conv_w[0], a_log=m_a_log, dt_bias=m_dt_bias,
                 dn_norm_w=m_dn_norm_w, q_norm_w=m_q_norm_w, k_norm_w=m_k_norm_w, rel_bias=m_rel_bias,
                 w_out=m_w_out[0])
    v_loc = dict(norm_w=v_norm_w, w_in=v_w_in[0].T, conv_w=v_conv_w[0], a_log=v_a_log, dt_bias=v_dt_bias,
                 dn_norm_w=v_dn_norm_w, q_norm_w=v_q_norm_w, k_norm_w=v_k_norm_w, rel_bias=v_rel_bias,
                 w_out=v_w_out[0])

    wt_pad = jnp.pad(w_loc["w_in"].astype(BF16), ((0, SHARD_PAD - D_SHARD), (0, 0)))
    g_in, g_out, g_conv, bias = _gather_weights(wt_pad, w_loc["w_out"].astype(BF16), w_loc["conv_w"], rel_bias,
                                                _bucket_tables())
    wt_full = _pack_w(g_in)
    w_out_full = g_out.reshape(D_MODEL, D_MODEL)
    conv_full = g_conv.transpose(1, 0, 2).reshape(CONV_W, 3 * D_DN)

    loss_local, last_kernel, grads = _local_step(x[0], loss_target[0], norm_w, wt_full, conv_full, a_log, dt_bias,
                                                 dn_norm_w, q_norm_w, k_norm_w, bias, w_out_full)
    grads["loss"] = loss_local

    u_in = _unpack_w(grads["w_in_t"], SHARD_PAD)
    u_out = grads["w_out"].reshape(N_CHIPS, D_MODEL // N_CHIPS, D_MODEL)
    win_out = u_out.shape[1] // 2
    h_in, hb_in, h_out, hb_out = _chip_sums(u_in, u_out, WIN, win_out)
    grad_x, d_nw8, grads["rel_bias"], r_in, r_out = last_kernel(chip_sums=(hb_in, hb_out))
    grads["norm_w"] = d_nw8[0:1, :]
    small = _pack_small(grads)
    full_in, full_out, r_small = _fill_parts(h_in, r_in, h_out, r_out, small, SHARD_PAD, u_out.shape[1])
    tot_small, qk = _sum_small(r_small.reshape(8, SMALL_TOTAL))

    def small_grad(name, n):
        off = _small_offset(name)
        return tot_small[:, off:off + n]

    loss = small_grad("loss", 1).reshape(())
    conv_all = small_grad("conv_w", CONV_W * 3 * D_DN).reshape(CONV_W, 3 * D_DN)
    g_small = dict(
        norm_w=small_grad("norm_w", D_MODEL),
        conv_w=lax.dynamic_slice_in_dim(conv_all, chip * (3 * D_DN // N_CHIPS), 3 * D_DN // N_CHIPS, axis=1),
        a_log=small_grad("a_log", DN_HEADS),
        dt_bias=small_grad("dt_bias", DN_HEADS),
        dn_norm_w=small_grad("dn_norm_w", DN_HD),
        q_norm_w=qk[0:1, 0:ATT_HD],
        k_norm_w=qk[1:2, 0:ATT_HD],
        rel_bias=small_grad("rel_bias", ATT_HEADS * N_BUCKETS).reshape(ATT_HEADS, N_BUCKETS),
    )

    out_g, out_d, out_m, out_v = {}, {}, {}, {}
    out_g["w_in"], out_d["w_in"], out_m["w_in"], out_v["w_in"] = _adamw_rows(
        full_in, w_loc["w_in"], m_loc["w_in"], v_loc["w_in"], "adamw_w_in")
    out_g["w_out"], out_d["w_out"], out_m["w_out"], out_v["w_out"] = _adamw_big(
        full_out, w_loc["w_out"], m_loc["w_out"], v_loc["w_out"], "adamw_w_out")
    for name in g_small:
        out_g[name] = g_small[name]
        out_d[name], out_m[name], out_v[name] = _adamw_small(w_loc[name], g_small[name], m_loc[name], v_loc[name],
                                                             "adamw_" + name)
    for d in (out_g, out_d, out_m, out_v):
        d["w_in"] = d["w_in"].transpose(1, 2, 0)
        for name in ("conv_w", "w_out"):
            d[name] = d[name][None]
    return (loss, grad_x[None], *[out_g[n] for n in WEIGHTS], *[out_d[n] for n in WEIGHTS],
            *[out_m[n] for n in WEIGHTS], *[out_v[n] for n in WEIGHTS])
```

```python
import functools
import math

import numpy as np
import jax
import jax.numpy as jnp
from jax import lax
from jax.experimental import pallas as pl
from jax.experimental.pallas import tpu as pltpu

F32 = jnp.float32
BF16 = jnp.bfloat16
HI = lax.Precision.HIGHEST

D_MODEL = 1024
D_DN = 512
DN_HEADS = 4
DN_HD = 128
CONV_W = 4
CHUNK = 64
D_ATT = 512
ATT_HEADS = 8
ATT_HD = 64
PATTERNS = ((128, 1), (512, 4), (2048, 16))
N_BUCKETS = 32
MAX_DISTANCE = 2048
D_IN = 4 * D_DN + 2 * DN_HEADS + 4 * D_ATT
D_IN_PAD = 4224
EPS = 1e-6
BLK = 128
NEG = -1e30
N_CHIPS = 4

ADAM_LR = 0.001
ADAM_B1 = 0.9
ADAM_B2 = 0.999
ADAM_EPS = 1e-08
ADAM_WD = 0.01
ADAM_STEP = 10

VMEM_LIMIT = 56 * 1024 * 1024

COL_Z = 3
COL_ATT_Q = 4
COL_ATT_K = 5
COL_ATT_V = 6
COL_GATE = 7
COL_BA_128 = 32


def _cp(sem=None):
    if sem is None:
        return pltpu.CompilerParams(vmem_limit_bytes=VMEM_LIMIT)
    return pltpu.CompilerParams(dimension_semantics=sem, vmem_limit_bytes=VMEM_LIMIT)


def _sds(shape, dtype=F32):
    return jax.ShapeDtypeStruct(shape, dtype)


def _mm(a, b):
    return jnp.dot(a.astype(BF16), b.astype(BF16), preferred_element_type=F32)


def _mm_nt(a, b):
    return lax.dot_general(a.astype(BF16), b.astype(BF16), (((1,), (1,)), ((), ())),
                           preferred_element_type=F32)


def _mm_tn(a, b):
    return lax.dot_general(a.astype(BF16), b.astype(BF16), (((0,), (0,)), ((), ())),
                           preferred_element_type=F32)


def _mmx(a, b):
    return jnp.dot(a, b, precision=HI, preferred_element_type=F32)


def _mmx_nt(a, b):
    return lax.dot_general(a, b, (((1,), (1,)), ((), ())), precision=HI, preferred_element_type=F32)


def _mmx_tn(a, b):
    return lax.dot_general(a, b, (((0,), (0,)), ((), ())), precision=HI, preferred_element_type=F32)


def _dot(a, b):
    return jnp.dot(a, b, preferred_element_type=F32)


def _dot_nt(a, b):
    return lax.dot_general(a, b, (((1,), (1,)), ((), ())), preferred_element_type=F32)


def _dot_tn(a, b):
    return lax.dot_general(a, b, (((0,), (0,)), ((), ())), preferred_element_type=F32)


def _split(a):
    hi = a.astype(BF16)
    return hi, (a - hi.astype(F32)).astype(BF16)


def _mm3(a_s, b_s):
    return _dot(a_s[0], b_s[0]) + _dot(a_s[0], b_s[1]) + _dot(a_s[1], b_s[0])


def _mm3_tn(a_s, b_s):
    return _dot_tn(a_s[0], b_s[0]) + _dot_tn(a_s[0], b_s[1]) + _dot_tn(a_s[1], b_s[0])


def _interleave(gens):
    live = list(gens)
    while live:
        nxt = []
        for g in live:
            try:
                next(g)
                nxt.append(g)
            except StopIteration:
                pass
        live = nxt


def _segsum(x, bd):
    hi = x.astype(BF16)
    r1 = x - hi.astype(F32)
    mid = r1.astype(BF16)
    lo = (r1 - mid.astype(F32)).astype(BF16)
    return (jnp.dot(hi, bd, preferred_element_type=F32) + jnp.dot(mid, bd, preferred_element_type=F32)
            + jnp.dot(lo, bd, preferred_element_type=F32))


def _compact_mat(seg, n=512):
    slot = 128 * seg // n
    src = np.arange(n)[:, None]
    dst = np.arange(128)[None, :]
    return jnp.asarray((src // seg == dst // slot).astype(np.float32), dtype=BF16)


def _seg_mean(x, cm, seg):
    hi, lo = _split(x)
    return (_dot(hi, cm) + _dot(lo, cm)) * (1.0 / seg)


def _seg_expand(c, cm, seg):
    hi, lo = _split(c)
    return (_dot_nt(hi, cm) + _dot_nt(lo, cm)) * (cm.shape[0] / (128.0 * seg))


def _seg_rstd(x, cm, seg):
    return _seg_expand(lax.rsqrt(_seg_mean(x * x, cm, seg) + EPS), cm, seg)


def _sigmoid(x):
    return 1.0 / (1.0 + jnp.exp(-x))


def _silu_grad(x, s):
    return s * (1.0 + x * (1.0 - s))


def _tri_incl():
    i = np.arange(CHUNK)
    return jnp.asarray((i[:, None] >= i[None, :]).astype(np.float32))


def _t5_bucket(dist):
    max_exact = N_BUCKETS // 2
    d = np.maximum(dist, 1).astype(np.float64)
    large = max_exact + (np.log(d / max_exact) / math.log(MAX_DISTANCE / max_exact)
                         * (N_BUCKETS - max_exact)).astype(np.int32)
    large = np.minimum(large, N_BUCKETS - 1)
    return np.where(dist < max_exact, dist, large).astype(np.int32)


def _bucket_tables():
    qi = np.arange(BLK)[:, None]
    kj = np.arange(2 * BLK)[None, :]
    step = qi - kj + BLK
    return jnp.asarray(np.stack([_t5_bucket(np.clip(step, 0, None) * r) for _, r in PATTERNS]))


def _in_proj(x, norm_w, wt_bf):
    s = x.shape[0]
    tm = 512

    def body(x_ref, nw_ref, w_ref, o_ref):
        xv = x_ref[...]
        rstd = lax.rsqrt(jnp.mean(xv * xv, axis=-1, keepdims=True) + EPS)
        h = (xv * rstd * nw_ref[...]).astype(BF16)
        o_ref[...] = _dot_nt(h, w_ref[...])

    return pl.pallas_call(
        body, grid=(s // tm,),
        in_specs=[pl.BlockSpec((tm, D_MODEL), lambda i: (i, 0)),
                  pl.BlockSpec((1, D_MODEL), lambda i: (0, 0)),
                  pl.BlockSpec((D_IN_PAD, D_MODEL), lambda i: (0, 0))],
        out_specs=pl.BlockSpec((tm, D_IN_PAD), lambda i: (i, 0)),
        out_shape=_sds((s, D_IN_PAD)), compiler_params=_cp(("parallel",)), name="in_proj")(x, norm_w, wt_bf)


def _conv_group(cur, halo, w_ref, c):
    rows = cur.shape[0]
    lanes = slice(128 * c, 128 * c + 128)
    xcat = jnp.concatenate([halo, cur], axis=0)
    y = cur * w_ref[CONV_W - 1:CONV_W, lanes]
    for k in range(1, CONV_W):
        sh = pltpu.roll(xcat, k, 0)[8:8 + rows]
        y = y + sh * w_ref[CONV_W - 1 - k:CONV_W - k, lanes]
    return y


def _beta_g(ba, alog_l, dtb_l):
    lane = lax.broadcasted_iota(jnp.int32, ba.shape, 1)
    sig_b = _sigmoid(ba)
    t = ba + dtb_l
    softplus = jnp.maximum(t, 0.0) + jnp.log(1.0 + jnp.exp(-jnp.abs(t)))
    nega = -jnp.exp(alog_l)
    g = nega * softplus
    out = jnp.where(lane < DN_HEADS, sig_b, jnp.where(lane < 2 * DN_HEADS, g, 0.0))
    return out, lane, sig_b, t, nega, g


def _dn_pre(proj, conv_w8, alog_l, dtb_l):
    s = proj.shape[0]
    tr = 512
    nh = tr // 8

    def body(u_ref, halo_ref, ba_ref, w_ref, al_ref, dt_ref, q_ref, k_ref, v_ref, bg_ref):
        i = pl.program_id(0)
        keep = (i > 0).astype(F32)
        for c in range(12):
            lanes = slice(128 * c, 128 * c + 128)
            y = _conv_group(u_ref[:, lanes], halo_ref[:, lanes] * keep, w_ref, c)
            sv = y * _sigmoid(y)
            if c < 8:
                rs = lax.rsqrt(jnp.sum(sv * sv, axis=1, keepdims=True) + EPS)
                n = sv * rs
                if c < 4:
                    q_ref[:, lanes] = n * (DN_HD ** -0.5)
                else:
                    k_ref[:, slice(128 * (c - 4), 128 * (c - 3))] = n
            else:
                v_ref[:, slice(128 * (c - 8), 128 * (c - 7))] = sv
        bg_ref[...] = _beta_g(ba_ref[...], al_ref[...], dt_ref[...])[0]

    return pl.pallas_call(
        body, grid=(s // tr,),
        in_specs=[pl.BlockSpec((tr, 1536), lambda i: (i, 0)),
                  pl.BlockSpec((8, 1536), lambda i: (jnp.maximum(i * nh - 1, 0), 0)),
                  pl.BlockSpec((tr, 128), lambda i: (i, COL_BA_128)),
                  pl.BlockSpec((8, 1536), lambda i: (0, 0)),
                  pl.BlockSpec((1, 128), lambda i: (0, 0)),
                  pl.BlockSpec((1, 128), lambda i: (0, 0))],
        out_specs=[pl.BlockSpec((tr, 512), lambda i: (i, 0))] * 3 + [pl.BlockSpec((tr, 128), lambda i: (i, 0))],
        out_shape=[_sds((s, 512))] * 3 + [_sds((s, 128))],
        compiler_params=_cp(("parallel",)), name="dn_pre")(proj, proj, proj, conv_w8, alog_l, dtb_l)


CPS = 8
CPS_SCAN = 8
SPLIT_ITERS = 2


def _split3(a):
    hi = a.astype(BF16)
    r1 = a - hi.astype(F32)
    mid = r1.astype(BF16)
    return hi, mid, (r1 - mid.astype(F32)).astype(BF16)


def _lane_select():
    r = np.arange(128)
    return jnp.asarray((r[None, :, None] == np.arange(8)[:, None, None]) * np.ones((1, 1, 128)), dtype=BF16)


def _lane_bcast(a3, sel):
    return _dot(a3[0], sel) + _dot(a3[1], sel) + _dot(a3[2], sel)


def _rowsum_b(z, ones_b):
    hi, lo = _split(z)
    return _dot(hi, ones_b) + _dot(lo, ones_b)


def _chunk_cumsum(bg, bgt, tri):
    return _split3(bg), _split3(_mmx(tri, bg)), _mmx_nt(bgt, tri)


def _chunk_common(bg3, gc3, gc_row, h, sel_ref):
    gcc = _lane_bcast(gc3, sel_ref[DN_HEADS + h])
    beta = _dot(bg3[0], sel_ref[h]) + _dot(bg3[1], sel_ref[h])
    gcr = gc_row[DN_HEADS + h:DN_HEADS + h + 1, :]
    ii = lax.broadcasted_iota(jnp.int32, (CHUNK, CHUNK), 0)
    jj = lax.broadcasted_iota(jnp.int32, (CHUNK, CHUNK), 1)
    incl = ii >= jj
    strict = ii > jj
    decay = jnp.exp(jnp.where(incl, gcc[:, 0:CHUNK] - gcr, NEG))
    gl = gcc[CHUNK - 1:CHUNK, :]
    return gcc, beta, incl, strict, decay, gl


def _dn_prep(qn, kn, v, bg, bgt, tri, sel):
    s = qn.shape[0]
    nc = s // CHUNK

    def body(q_ref, k_ref, v_ref, bg_ref, bgt_ref, tri_ref, sel_ref,
             u_ref, w_ref, qd_ref, kt_ref, attn_ref, t_ref, gl_ref):
        tri_v = tri_ref[...]
        ii = lax.broadcasted_iota(jnp.int32, (CHUNK, CHUNK), 0)
        jj = lax.broadcasted_iota(jnp.int32, (CHUNK, CHUNK), 1)
        eye = (ii == jj).astype(F32)

        def chain(cc, h, bg3, gc3, gc_row):
            rows = slice(CHUNK * cc, CHUNK * cc + CHUNK)
            lanes = slice(128 * h, 128 * h + 128)
            gcc, beta, incl, strict, decay, gl = _chunk_common(bg3, gc3, gc_row, h, sel_ref)
            yield
            q = q_ref[rows, lanes]
            k = k_ref[rows, lanes]
            vv = v_ref[rows, lanes]
            kb = k * beta
            egc = jnp.exp(gcc)
            a_mat = jnp.where(strict, _mm_nt(kb, k) * decay, 0.0)
            attn_ref[cc, h] = jnp.where(incl, _mm_nt(q, k) * decay, 0.0)
            qd_ref[rows, lanes] = (q * egc).astype(BF16)
            kt_ref[rows, lanes] = (k * jnp.exp(gl - gcc)).astype(BF16)
            gl_ref[cc, h] = jnp.exp(gl)
            yield
            p = -a_mat
            t = eye + p
            for it in range(5):
                if it < SPLIT_ITERS:
                    ps = _split(p)
                    p = _mm3(ps, ps)
                    yield
                    t = t + _mm3(_split(t), _split(p))
                else:
                    p = _mm(p, p)
                    yield
                    t = t + _mm(t, p)
                yield
            t_ref[cc, h] = t
            ts = _split(t)
            u_ref[rows, lanes] = _mm3(ts, _split(vv * beta))
            w_ref[rows, lanes] = _mm3(ts, _split(kb * egc)).astype(BF16)

        gens = []
        for cc in range(CPS):
            bgv = bg_ref[CHUNK * cc:CHUNK * cc + CHUNK, :]
            bg3, gc3, gc_row = _chunk_cumsum(bgv, bgt_ref[cc], tri_v)
            gens += [chain(cc, h, bg3, gc3, gc_row) for h in range(DN_HEADS)]
        _interleave(gens)

    rows_step = CPS * CHUNK
    big = pl.BlockSpec((rows_step, 512), lambda n: (n, 0))
    sq = pl.BlockSpec((CPS, DN_HEADS, CHUNK, CHUNK), lambda n: (n, 0, 0, 0))
    return pl.pallas_call(
        body, grid=(nc // CPS,),
        in_specs=[big, big, big, pl.BlockSpec((rows_step, 128), lambda n: (n, 0)),
                  pl.BlockSpec((CPS, 8, CHUNK), lambda n: (n, 0, 0)),
                  pl.BlockSpec((CHUNK, CHUNK), lambda n: (0, 0)),
                  pl.BlockSpec((8, 128, 128), lambda n: (0, 0, 0))],
        out_specs=[big, big, big, big, sq, sq, pl.BlockSpec((CPS, DN_HEADS, 1, 128), lambda n: (n, 0, 0, 0))],
        out_shape=[_sds((s, 512))] + [_sds((s, 512), BF16)] * 3 + [_sds((nc, DN_HEADS, CHUNK, CHUNK))] * 2
        + [_sds((nc, DN_HEADS, 1, 128))],
        compiler_params=_cp(("parallel",)), name="dn_prep")(qn, kn, v, bg, bgt, tri, sel)


def _dn_scan(u, w, qd, kt, attn, gl):
    s = u.shape[0]
    nc = s // CHUNK

    def body(u_ref, w_ref, qd_ref, kt_ref, attn_ref, gl_ref, o_ref, vn_ref, sp_ref, st_ref):
        n = pl.program_id(0)

        @pl.when(n == 0)
        def _():
            st_ref[...] = jnp.zeros_like(st_ref)

        def chain(cc, h):
            rows = slice(CHUNK * cc, CHUNK * cc + CHUNK)
            lanes = slice(128 * h, 128 * h + 128)
            st = st_ref[h]
            sp_ref[cc, h] = st
            stb = st.astype(BF16)
            ws = _dot(w_ref[rows, lanes].astype(BF16), stb)
            qs = _dot(qd_ref[rows, lanes].astype(BF16), stb)
            yield
            vn = u_ref[rows, lanes] - ws
            vnb = vn.astype(BF16)
            vn_ref[rows, lanes] = vnb
            o_ref[rows, lanes] = qs + _dot(attn_ref[cc, h].astype(BF16), vnb)
            st_ref[h] = st * gl_ref[cc, h] + _dot_tn(kt_ref[rows, lanes].astype(BF16), vnb)

        for cc in range(CPS_SCAN):
            _interleave([chain(cc, h) for h in range(DN_HEADS)])

    big = pl.BlockSpec((CPS_SCAN * CHUNK, 512), lambda n: (n, 0))
    return pl.pallas_call(
        body, grid=(nc // CPS_SCAN,),
        in_specs=[big, big, big, big,
                  pl.BlockSpec((CPS_SCAN, DN_HEADS, CHUNK, CHUNK), lambda n: (n, 0, 0, 0)),
                  pl.BlockSpec((CPS_SCAN, DN_HEADS, 1, 128), lambda n: (n, 0, 0, 0))],
        out_specs=[big, big, pl.BlockSpec((CPS_SCAN, DN_HEADS, DN_HD, DN_HD), lambda n: (n, 0, 0, 0))],
        out_shape=[_sds((s, 512)), _sds((s, 512), BF16), _sds((nc, DN_HEADS, DN_HD, DN_HD))],
        scratch_shapes=[pltpu.VMEM((DN_HEADS, DN_HD, DN_HD), F32)],
        compiler_params=_cp(("arbitrary",)), name="dn_scan")(u, w, qd, kt, attn, gl)


R4 = PATTERNS[1][1]
R16 = PATTERNS[2][1]
TM = 512


def _pattern_spec(r, width=512):
    return pl.BlockSpec((r, TM // r, width), lambda i: (0, i, 0))


def _pattern_shape(s, r, dtype=F32, width=512):
    return _sds((r, s // r, width), dtype)


SLABS = pltpu.VMEM((4, TM, 128), F32)

HEAD_SLOT = 128 // ATT_HEADS


def _head_expand():
    src = np.arange(128)[:, None]
    dst = np.arange(512)[None, :]
    return jnp.asarray((src == (dst // ATT_HD) * HEAD_SLOT).astype(np.float32), dtype=BF16)


def _head_compact():
    src = np.arange(512)[:, None]
    dst = np.arange(128)[None, :]
    return jnp.asarray((src // ATT_HD == dst // HEAD_SLOT).astype(np.float32), dtype=BF16)


def _to_patterns(val, dsts, scr):
    for c in range(val.shape[1] // 128):
        lanes = slice(128 * c, 128 * c + 128)
        scr[c] = val[:, lanes]
        for dst_ref, r in dsts:
            for a in range(r):
                dst_ref[a, :, lanes] = scr[c, pl.ds(a, TM // r, stride=r), :].astype(dst_ref.dtype)


def _from_pattern(src_ref, r, scr):
    n_slab = src_ref.shape[2] // 128
    for c in range(n_slab):
        for a in range(r):
            scr[c, pl.ds(a, TM // r, stride=r), :] = src_ref[a, :, 128 * c:128 * c + 128].astype(F32)
    return jnp.concatenate([scr[c] for c in range(n_slab)], axis=1) if n_slab > 1 else scr[0]


def _att_pre(proj, qw_t, kw_t, bd64):
    s = proj.shape[0]

    def body(q_ref, k_ref, v_ref, qw_ref, kw_ref, bd_ref,
             q1_ref, k1_ref, v1_ref, q4_ref, k4_ref, v4_ref, q16_ref, k16_ref, v16_ref, scr):
        bd = bd_ref[...]
        q = q_ref[...]
        k = k_ref[...]
        qn = q * _seg_rstd(q, bd, ATT_HD) * qw_ref[...] * (ATT_HD ** -0.5)
        kn = k * _seg_rstd(k, bd, ATT_HD) * kw_ref[...]
        q1_ref[...] = qn.astype(BF16)
        k1_ref[...] = kn.astype(BF16)
        v1_ref[...] = v_ref[...].astype(BF16)
        _to_patterns(qn, ((q4_ref, R4), (q16_ref, R16)), scr)
        _to_patterns(kn, ((k4_ref, R4), (k16_ref, R16)), scr)
        _to_patterns(v_ref[...], ((v4_ref, R4), (v16_ref, R16)), scr)

    row = pl.BlockSpec((1, 512), lambda i: (0, 0))
    tok = pl.BlockSpec((TM, 512), lambda i: (i, 0))
    return pl.pallas_call(
        body, grid=(s // TM,),
        in_specs=[pl.BlockSpec((TM, 512), lambda i: (i, COL_ATT_Q)),
                  pl.BlockSpec((TM, 512), lambda i: (i, COL_ATT_K)),
                  pl.BlockSpec((TM, 512), lambda i: (i, COL_ATT_V)),
                  row, row, pl.BlockSpec((512, 128), lambda i: (0, 0))],
        out_specs=[tok] * 3 + [_pattern_spec(R4)] * 3 + [_pattern_spec(R16)] * 3,
        out_shape=[_sds((s, 512), BF16)] * 3 + [_pattern_shape(s, R4, BF16)] * 3 + [_pattern_shape(s, R16, BF16)] * 3,
        scratch_shapes=[SLABS],
        compiler_params=_cp(("parallel",)), name="att_pre")(proj, proj, proj, qw_t, kw_t, bd64)


def _bias_table(rb_ref, bk_ref, o_ref):
    for p in range(len(PATTERNS)):
        bk = bk_ref[p]
        for h in range(ATT_HEADS):
            acc = jnp.zeros((BLK, 2 * BLK), F32)
            for b in range(N_BUCKETS):
                acc = jnp.where(bk == b, rb_ref[h, b], acc)
            o_ref[p, h] = acc


def _bias_grad(ds_refs, bk_ref, o_ref):
    for h in range(ATT_HEADS):
        for b in range(N_BUCKETS):
            tot = jnp.zeros((), F32)
            for p, ds_ref in enumerate(ds_refs):
                tot = tot + jnp.sum(jnp.where(bk_ref[p] == b, ds_ref[h], 0.0))
            o_ref[h, b] = tot


QB_FWD = 2
QB_BWD = 4


def _att_masks(has_prev):
    qi = lax.broadcasted_iota(jnp.int32, (BLK, BLK), 0)
    kj = lax.broadcasted_iota(jnp.int32, (BLK, BLK), 1)
    lane = lax.broadcasted_iota(jnp.int32, (BLK, 2 * ATT_HD), 1)
    return jnp.logical_and(kj >= qi, has_prev), kj <= qi, lane < ATT_HD


def _head_lanes(h):
    half = h % 2
    return slice(ATT_HD * h, ATT_HD * h + ATT_HD), slice(ATT_HD * half, ATT_HD * half + ATT_HD)


def _att_scores(qm, kp2, kc2, bias_h, mask_prev, mask_cur):
    s_prev = jnp.where(mask_prev, _dot_nt(qm, kp2) + bias_h[:, :BLK], NEG)
    s_cur = jnp.where(mask_cur, _dot_nt(qm, kc2) + bias_h[:, BLK:], NEG)
    return s_prev, s_cur


def _att_fwd(q, k, v, v_col, bias, p_idx, r, name):
    QB = QB_FWD
    s = q.shape[0]
    nblk = s // BLK
    nseq = nblk // r

    def body(q_ref, kp_ref, kc_ref, vp_ref, vc_ref, b_ref, o_ref, lse_ref):
        j = pl.program_id(0)

        def head(h, rows, masks, q2, kp2, kc2, vp2, vc2):
            mask_prev, mask_cur, lo_half = masks
            out_l, pair_l = _head_lanes(h)
            sel = lo_half if h % 2 == 0 else jnp.logical_not(lo_half)
            qm = jnp.where(sel, q2, jnp.zeros_like(q2))
            s_prev, s_cur = _att_scores(qm, kp2, kc2, b_ref[0, h], mask_prev, mask_cur)
            yield
            m = jnp.maximum(jnp.max(s_prev, axis=1, keepdims=True), jnp.max(s_cur, axis=1, keepdims=True))
            p_prev = jnp.exp(s_prev - m)
            p_cur = jnp.exp(s_cur - m)
            l = jnp.sum(p_prev, axis=1, keepdims=True) + jnp.sum(p_cur, axis=1, keepdims=True)
            yield
            o2 = _dot(p_prev.astype(BF16), vp2) + _dot(p_cur.astype(BF16), vc2)
            o_ref[rows, out_l] = (o2 * (1.0 / l))[:, pair_l].astype(BF16)
            lse_ref[rows, HEAD_SLOT * h:HEAD_SLOT * h + HEAD_SLOT] = jnp.broadcast_to(m + jnp.log(l), (BLK, HEAD_SLOT))

        for sub in range(QB):
            rows = slice(BLK * sub, BLK * sub + BLK)
            before = slice(BLK * (sub - 1), BLK * sub)
            masks = _att_masks(((QB * j + sub) % nseq) != 0)
            gens = []
            for pp in range(ATT_HEADS // 2):
                lanes = slice(128 * pp, 128 * pp + 128)
                kp = kp_ref[:, lanes] if sub == 0 else kc_ref[before, lanes]
                vp = vp_ref[:, lanes] if sub == 0 else vc_ref[before, lanes]
                slabs = (q_ref[rows, lanes], kp, kc_ref[rows, lanes], vp, vc_ref[rows, lanes])
                gens += [head(2 * pp, rows, masks, *slabs), head(2 * pp + 1, rows, masks, *slabs)]
            _interleave(gens)

    cur = pl.BlockSpec((QB * BLK, 512), lambda j: (j, 0))
    prev = pl.BlockSpec((BLK, 512), lambda j: (jnp.maximum(QB * j - 1, 0), 0))
    vcur = pl.BlockSpec((QB * BLK, 512), lambda j: (j, v_col))
    vprev = pl.BlockSpec((BLK, 512), lambda j: (jnp.maximum(QB * j - 1, 0), v_col))
    return pl.pallas_call(
        body, grid=(nblk // QB,),
        in_specs=[cur, prev, cur, vprev, vcur,
                  pl.BlockSpec((1, ATT_HEADS, BLK, 2 * BLK), lambda j: (p_idx, 0, 0, 0))],
        out_specs=[cur, pl.BlockSpec((QB * BLK, 128), lambda j: (j, 0))],
        out_shape=[_sds((s, 512), BF16), _sds((s, 128))],
        compiler_params=_cp(("parallel",)), name=name)(q, k, k, v, v, bias)


def _post_fwd(o_dn, proj, o_pats, lse_pats, dnw_t, bd128):
    s = o_dn.shape[0]

    def body(o_ref, z_ref, gate_ref, o1_ref, o4_ref, o16_ref, s1_ref, s4_ref, s16_ref, wn_ref, bd_ref, ex_ref,
             mixed_ref, oatt_ref, l1_ref, l4_ref, l16_ref, scr_a, scr_b, scr_c, scr_d):
        o = o_ref[...]
        z = z_ref[...]
        rstd = _seg_rstd(o, bd_ref[...], DN_HD)
        y_dn = o * rstd * wn_ref[...] * (z * _sigmoid(z))
        mixed_ref[:, 0:512] = y_dn.astype(BF16)
        lses = (s1_ref[...], _from_pattern(s4_ref, R4, scr_a), _from_pattern(s16_ref, R16, scr_b))
        m = jnp.maximum(jnp.maximum(lses[0], lses[1]), lses[2])
        tot = jnp.exp(lses[0] - m) + jnp.exp(lses[1] - m) + jnp.exp(lses[2] - m)
        big_l = m + jnp.log(tot)
        l1_ref[...] = big_l
        _to_patterns(big_l, ((l4_ref, R4), (l16_ref, R16)), scr_a)
        ex = ex_ref[...]
        outs = (o1_ref[...], _from_pattern(o4_ref, R4, scr_c), _from_pattern(o16_ref, R16, scr_d))
        acc = jnp.zeros_like(o)
        for lse_p, o_p in zip(lses, outs):
            acc = acc + _lane_bcast(_split3(jnp.exp(lse_p - big_l)), ex) * o_p
        gate = gate_ref[...]
        oatt_ref[...] = acc
        mixed_ref[:, 512:1024] = (acc * (gate * _sigmoid(gate))).astype(BF16)

    blk = pl.BlockSpec((TM, 512), lambda i: (i, 0))
    cblk = pl.BlockSpec((TM, 128), lambda i: (i, 0))
    p4, p16 = _pattern_spec(R4), _pattern_spec(R16)
    c4, c16 = _pattern_spec(R4, 128), _pattern_spec(R16, 128)
    return pl.pallas_call(
        body, grid=(s // TM,),
        in_specs=[blk, pl.BlockSpec((TM, 512), lambda i: (i, COL_Z)),
                  pl.BlockSpec((TM, 512), lambda i: (i, COL_GATE)), blk, p4, p16, cblk, c4, c16,
                  pl.BlockSpec((1, 512), lambda i: (0, 0)), pl.BlockSpec((512, 128), lambda i: (0, 0)),
                  pl.BlockSpec((128, 512), lambda i: (0, 0))],
        out_specs=[pl.BlockSpec((TM, D_MODEL), lambda i: (i, 0)), blk, cblk, c4, c16],
        out_shape=[_sds((s, D_MODEL), BF16), _sds((s, 512)), _sds((s, 128)), _pattern_shape(s, R4, F32, 128),
                   _pattern_shape(s, R16, F32, 128)],
        scratch_shapes=[SLABS] * 4,
        compiler_params=_cp(("parallel",)), name="post_fwd")(o_dn, proj, proj, *o_pats, *lse_pats, dnw_t, bd128,
                                                              _head_expand())


def _out_fwd(x, mixed, w_out_bf, tgt):
    s = x.shape[0]
    tm = 512

    def body(x_ref, m_ref, w_ref, t_ref, dy_ref, loss_ref):
        i = pl.program_id(0)

        @pl.when(i == 0)
        def _():
            loss_ref[...] = jnp.zeros_like(loss_ref)

        y = x_ref[...] + jnp.dot(m_ref[...], w_ref[...], preferred_element_type=F32)
        err = y - t_ref[...]
        dy_ref[...] = err * (1.0 / D_MODEL)
        part = 0.5 * jnp.sum(jnp.mean(err * err, axis=-1, keepdims=True), axis=0, keepdims=True)
        loss_ref[...] = loss_ref[...] + part

    blk = pl.BlockSpec((tm, D_MODEL), lambda i: (i, 0))
    return pl.pallas_call(
        body, grid=(s // tm,),
        in_specs=[blk, blk, pl.BlockSpec((D_MODEL, D_MODEL), lambda i: (0, 0)), blk],
        out_specs=[blk, pl.BlockSpec((8, 128), lambda i: (0, 0))],
        out_shape=[_sds((s, D_MODEL)), _sds((8, 128))],
        compiler_params=_cp(("arbitrary",)), name="out_fwd")(x, mixed, w_out_bf, tgt)


def _out_bwd(dy, mixed, w_out_bf):
    s = dy.shape[0]
    tm = 512

    def body(dy_ref, m_ref, w_ref, dm_ref, dw_ref):
        i = pl.program_id(0)

        @pl.when(i == 0)
        def _():
            dw_ref[...] = jnp.zeros_like(dw_ref)

        dyb = dy_ref[...].astype(BF16)
        dm_ref[...] = lax.dot_general(dyb, w_ref[...], (((1,), (1,)), ((), ())), preferred_element_type=F32)
        dw_ref[...] = dw_ref[...] + lax.dot_general(m_ref[...], dyb, (((0,), (0,)), ((), ())),
                                                    preferred_element_type=F32)

    blk = pl.BlockSpec((tm, D_MODEL), lambda i: (i, 0))
    full = pl.BlockSpec((D_MODEL, D_MODEL), lambda i: (0, 0))
    return pl.pallas_call(
        body, grid=(s // tm,), in_specs=[blk, blk, full], out_specs=[blk, full],
        out_shape=[_sds((s, D_MODEL)), _sds((D_MODEL, D_MODEL))],
        compiler_params=_cp(("arbitrary",)), name="out_bwd")(dy, mixed, w_out_bf)


def _post_bwd(dmixed, o_dn, proj, o_att, dnw_t, bd128):
    s = o_dn.shape[0]
    tm = TM

    def body(ddn_ref, datt_ref, o_ref, z_ref, gate_ref, oatt_ref, wn_ref, bd128_ref, hc_ref,
             do_ref, dz_ref, dgate_ref, doatt_ref, do4_ref, do16_ref, delta_ref, dl4_ref, dl16_ref, dnw_ref, scr):
        i = pl.program_id(0)

        @pl.when(i == 0)
        def _():
            dnw_ref[...] = jnp.zeros_like(dnw_ref)

        bd128v = bd128_ref[...]
        o = o_ref[...]
        z = z_ref[...]
        wn = wn_ref[...]
        dy = ddn_ref[...]
        rstd = _seg_rstd(o, bd128v, DN_HD)
        nrm = o * rstd
        sz = _sigmoid(z)
        dz_ref[...] = (dy * nrm * wn * _silu_grad(z, sz)).astype(BF16)
        dn = dy * z * sz
        gw = dn * wn
        do_ref[...] = rstd * (gw - nrm * _seg_expand(_seg_mean(gw * nrm, bd128v, DN_HD), bd128v, DN_HD))
        colsum = jnp.sum(dn * nrm, axis=0, keepdims=True)
        fold = colsum[:, 0:128] + colsum[:, 128:256] + colsum[:, 256:384] + colsum[:, 384:512]
        dnw_ref[...] = dnw_ref[...] + fold
        dya = datt_ref[...]
        gate = gate_ref[...]
        oatt = oatt_ref[...]
        sg = _sigmoid(gate)
        dgate_ref[...] = (dya * oatt * _silu_grad(gate, sg)).astype(BF16)
        doa = dya * gate * sg
        doatt_ref[...] = doa.astype(BF16)
        delta = _segsum(doa * oatt, hc_ref[...])
        delta_ref[...] = delta
        _to_patterns(doa, ((do4_ref, R4), (do16_ref, R16)), scr)
        _to_patterns(delta, ((dl4_ref, R4), (dl16_ref, R16)), scr)

    blk = pl.BlockSpec((tm, 512), lambda i: (i, 0))
    cblk = pl.BlockSpec((tm, 128), lambda i: (i, 0))
    p4, p16 = _pattern_spec(R4), _pattern_spec(R16)
    c4, c16 = _pattern_spec(R4, 128), _pattern_spec(R16, 128)
    return pl.pallas_call(
        body, grid=(s // tm,),
        in_specs=[blk, pl.BlockSpec((tm, 512), lambda i: (i, 1)), blk,
                  pl.BlockSpec((tm, 512), lambda i: (i, COL_Z)), pl.BlockSpec((tm, 512), lambda i: (i, COL_GATE)),
                  blk, pl.BlockSpec((1, 512), lambda i: (0, 0)), pl.BlockSpec((512, 128), lambda i: (0, 0)),
                  pl.BlockSpec((512, 128), lambda i: (0, 0))],
        out_specs=[blk, blk, blk, blk, p4, p16, cblk, c4, c16, pl.BlockSpec((8, 128), lambda i: (0, 0))],
        out_shape=[_sds((s, 512))] + [_sds((s, 512), BF16)] * 3 + [_pattern_shape(s, R4, BF16),
                                          _pattern_shape(s, R16, BF16), _sds((s, 128)),
                                          _pattern_shape(s, R4, F32, 128), _pattern_shape(s, R16, F32, 128),
                                          _sds((8, 128))],
        scratch_shapes=[SLABS],
        compiler_params=_cp(("arbitrary",)), name="post_bwd")(dmixed, dmixed, o_dn, proj, proj, o_att, dnw_t,
                                                               bd128, _head_compact())


def _att_bwd(q, k, v, v_col, do, big_l, delta, bias, p_idx, r, name):
    QB = QB_BWD
    s = q.shape[0]
    nblk = s // BLK
    nseq = nblk // r
    nstep = nblk // QB

    def body(q_ref, kp_ref, kc_ref, vp_ref, vc_ref, do_ref, l_ref, dl_ref, b_ref,
             dq_ref, dk_ref, dv_ref, ds_ref, dkc_ref, dvc_ref):
        j = pl.program_id(0)

        @pl.when(j == 0)
        def _():
            dkc_ref[...] = jnp.zeros_like(dkc_ref)
            dvc_ref[...] = jnp.zeros_like(dvc_ref)
            ds_ref[...] = jnp.zeros_like(ds_ref)

        @pl.when(j < nstep)
        def _():
            def head(h, sub, masks, q2, do2, kp2, kc2, vp2, vc2):
                mask_prev, mask_cur, lo_half = masks
                rows = slice(BLK * sub, BLK * sub + BLK)
                out_l, pair_l = _head_lanes(h)
                sel = lo_half if h % 2 == 0 else jnp.logical_not(lo_half)
                qm = jnp.where(sel, q2, jnp.zeros_like(q2))
                dom = jnp.where(sel, do2, jnp.zeros_like(do2))
                s_prev, s_cur = _att_scores(qm, kp2, kc2, b_ref[0, h], mask_prev, mask_cur)
                dp_prev = _dot_nt(dom, vp2)
                dp_cur = _dot_nt(dom, vc2)
                yield
                lh = l_ref[rows, HEAD_SLOT * h:HEAD_SLOT * h + 1]
                dh = dl_ref[rows, HEAD_SLOT * h:HEAD_SLOT * h + 1]
                p_prev = jnp.exp(s_prev - lh)
                p_cur = jnp.exp(s_cur - lh)
                ds_prev = p_prev * (dp_prev - dh)
                ds_cur = p_cur * (dp_cur - dh)
                ds_ref[h, :, 0:BLK] = ds_ref[h, :, 0:BLK] + ds_prev
                ds_ref[h, :, BLK:2 * BLK] = ds_ref[h, :, BLK:2 * BLK] + ds_cur
                dsb_prev, dsb_cur = ds_prev.astype(BF16), ds_cur.astype(BF16)
                pb_prev, pb_cur = p_prev.astype(BF16), p_cur.astype(BF16)
                yield
                dq_ref[rows, out_l] = (_dot(dsb_prev, kp2) + _dot(dsb_cur, kc2))[:, pair_l].astype(BF16)
                dk_prev = _dot_tn(dsb_prev, q2)[:, pair_l]
                dv_prev = _dot_tn(pb_prev, do2)[:, pair_l]
                if sub == 0:
                    last = slice(BLK * (QB - 1), BLK * QB)
                    dk_ref[last, out_l] = (dkc_ref[last, out_l] + dk_prev).astype(BF16)
                    dv_ref[last, out_l] = (dvc_ref[last, out_l] + dv_prev).astype(BF16)
                else:
                    before = slice(BLK * (sub - 1), BLK * sub)
                    dkc_ref[before, out_l] = dkc_ref[before, out_l] + dk_prev
                    dvc_ref[before, out_l] = dvc_ref[before, out_l] + dv_prev
                yield
                dkc_ref[rows, out_l] = _dot_tn(dsb_cur, q2)[:, pair_l]
                dvc_ref[rows, out_l] = _dot_tn(pb_cur, do2)[:, pair_l]

            done = slice(0, BLK * (QB - 1))
            dk_ref[done, :] = dkc_ref[done, :].astype(BF16)
            dv_ref[done, :] = dvc_ref[done, :].astype(BF16)
            for sub in range(QB):
                rows = slice(BLK * sub, BLK * sub + BLK)
                before = slice(BLK * (sub - 1), BLK * sub)
                masks = _att_masks(((QB * j + sub) % nseq) != 0)
                gens = []
                for pp in range(ATT_HEADS // 2):
                    lanes = slice(128 * pp, 128 * pp + 128)
                    kp = kp_ref[:, lanes] if sub == 0 else kc_ref[before, lanes]
                    vp = vp_ref[:, lanes] if sub == 0 else vc_ref[before, lanes]
                    slabs = (q_ref[rows, lanes], do_ref[rows, lanes], kp, kc_ref[rows, lanes], vp,
                             vc_ref[rows, lanes])
                    gens += [head(2 * pp, sub, masks, *slabs), head(2 * pp + 1, sub, masks, *slabs)]
                _interleave(gens)

        @pl.when(j == nstep)
        def _():
            dk_ref[...] = dkc_ref[...].astype(BF16)
            dv_ref[...] = dvc_ref[...].astype(BF16)

    last_step = nstep - 1
    cur = pl.BlockSpec((QB * BLK, 512), lambda j: (jnp.minimum(j, last_step), 0))
    compact = pl.BlockSpec((QB * BLK, 128), lambda j: (jnp.minimum(j, last_step), 0))
    lag = pl.BlockSpec((QB * BLK, 512), lambda j: (jnp.clip(j - 1, 0, last_step), 0))
    prev = pl.BlockSpec((BLK, 512), lambda j: (jnp.clip(QB * j - 1, 0, nblk - 1), 0))
    vcur = pl.BlockSpec((QB * BLK, 512), lambda j: (jnp.minimum(j, last_step), v_col))
    vprev = pl.BlockSpec((BLK, 512), lambda j: (jnp.clip(QB * j - 1, 0, nblk - 1), v_col))
    return pl.pallas_call(
        body, grid=(nstep + 1,),
        in_specs=[cur, prev, cur, vprev, vcur, cur, compact, compact,
                  pl.BlockSpec((1, ATT_HEADS, BLK, 2 * BLK), lambda j: (p_idx, 0, 0, 0))],
        out_specs=[cur, lag, lag, pl.BlockSpec((ATT_HEADS, BLK, 2 * BLK), lambda j: (0, 0, 0))],
        out_shape=[_sds((s, 512), BF16)] * 3 + [_sds((ATT_HEADS, BLK, 2 * BLK))],
        scratch_shapes=[pltpu.VMEM((QB * BLK, 512), F32), pltpu.VMEM((QB * BLK, 512), F32)],
        compiler_params=_cp(("arbitrary",)), name=name)(q, k, k, v, v, do, big_l, delta, bias)


def _att_pre_bwd(dq_pats, dk_pats, dv_pats, proj, qw_t, kw_t, bd64):
    s = proj.shape[0]
    tm = TM

    def body(dq1_ref, dq4_ref, dq16_ref, dk1_ref, dk4_ref, dk16_ref, dv1_ref, dv4_ref, dv16_ref,
             q_ref, k_ref, qw_ref, kw_ref, bd_ref,
             dqr_ref, dkr_ref, dvr_ref, dqw_ref, dkw_ref, scr4, scr16):
        i = pl.program_id(0)

        @pl.when(i == 0)
        def _():
            dqw_ref[...] = jnp.zeros_like(dqw_ref)
            dkw_ref[...] = jnp.zeros_like(dkw_ref)

        bd = bd_ref[...]

        def total(d1_ref, d4_ref, d16_ref):
            return d1_ref[...] + _from_pattern(d4_ref, R4, scr4) + _from_pattern(d16_ref, R16, scr16)

        def one(d_refs, x_ref, w_ref, scale, dx_ref, dw_ref):
            dy = total(*d_refs) * scale
            x = x_ref[...]
            rstd = _seg_rstd(x, bd, ATT_HD)
            nrm = x * rstd
            dw_ref[...] = dw_ref[...] + jnp.sum(dy * nrm, axis=0, keepdims=True)
            g = dy * w_ref[...]
            dx_ref[...] = (rstd * (g - nrm * _seg_expand(_seg_mean(g * nrm, bd, ATT_HD), bd, ATT_HD))).astype(BF16)

        one((dq1_ref, dq4_ref, dq16_ref), q_ref, qw_ref, ATT_HD ** -0.5, dqr_ref, dqw_ref)
        one((dk1_ref, dk4_ref, dk16_ref), k_ref, kw_ref, 1.0, dkr_ref, dkw_ref)
        dvr_ref[...] = total(dv1_ref, dv4_ref, dv16_ref).astype(BF16)

    blk = pl.BlockSpec((tm, 512), lambda i: (i, 0))
    pats = [blk, _pattern_spec(R4), _pattern_spec(R16)]
    row = pl.BlockSpec((1, 512), lambda i: (0, 0))
    acc = pl.BlockSpec((8, 512), lambda i: (0, 0))
    return pl.pallas_call(
        body, grid=(s // tm,),
        in_specs=pats * 3 + [pl.BlockSpec((tm, 512), lambda i: (i, COL_ATT_Q)),
                             pl.BlockSpec((tm, 512), lambda i: (i, COL_ATT_K)), row, row,
                             pl.BlockSpec((512, 128), lambda i: (0, 0))],
        out_specs=[blk, blk, blk, acc, acc],
        out_shape=[_sds((s, 512), BF16)] * 3 + [_sds((8, 512))] * 2,
        scratch_shapes=[SLABS] * 2,
        compiler_params=_cp(("arbitrary",)), name="att_pre_bwd")(*dq_pats, *dk_pats, *dv_pats, proj, proj,
                                                                  qw_t, kw_t, bd64)


def _dn_scan_bwd(do, sp, qd, kt, w, vn, attn, gl):
    s = do.shape[0]
    nc = s // CHUNK

    def body(do_ref, sp_ref, qd_ref, kt_ref, w_ref, vn_ref, attn_ref, gl_ref,
             du_ref, dqd_ref, dkt_ref, dw_ref, dattn_ref, dgl_ref, ds_ref):
        n = pl.program_id(0)

        @pl.when(n == 0)
        def _():
            ds_ref[...] = jnp.zeros_like(ds_ref)

        def chain(cc, h):
            rows = slice(CHUNK * cc, CHUNK * cc + CHUNK)
            lanes = slice(128 * h, 128 * h + 128)
            dsn = ds_ref[h]
            st = sp_ref[cc, h]
            dsb, stb = dsn.astype(BF16), st.astype(BF16)
            dob = do_ref[rows, lanes].astype(BF16)
            vnb = vn_ref[rows, lanes].astype(BF16)
            dvn = _dot_tn(attn_ref[cc, h].astype(BF16), dob) + _dot(kt_ref[rows, lanes].astype(BF16), dsb)
            du_ref[rows, lanes] = dvn
            dqd_ref[rows, lanes] = _dot_nt(dob, stb)
            dattn_ref[cc, h] = _dot_nt(dob, vnb)
            dkt_ref[rows, lanes] = _dot_nt(vnb, dsb)
            tot = jnp.sum(jnp.sum(st * dsn, axis=1, keepdims=True), axis=0, keepdims=True)
            dgl_ref[cc, h] = jnp.broadcast_to(tot, (1, 128))
            qdo = _dot_tn(qd_ref[rows, lanes].astype(BF16), dob)
            yield
            dvb = dvn.astype(BF16)
            dw_ref[rows, lanes] = -_dot_nt(dvb, stb)
            ds_ref[h] = qdo + dsn * gl_ref[cc, h] - _dot_tn(w_ref[rows, lanes].astype(BF16), dvb)

        for cc in reversed(range(CPS_SCAN)):
            _interleave([chain(cc, h) for h in range(DN_HEADS)])

    nsteps = nc // CPS_SCAN
    big = pl.BlockSpec((CPS_SCAN * CHUNK, 512), lambda n: (nsteps - 1 - n, 0))
    sq = pl.BlockSpec((CPS_SCAN, DN_HEADS, CHUNK, CHUNK), lambda n: (nsteps - 1 - n, 0, 0, 0))
    glb = pl.BlockSpec((CPS_SCAN, DN_HEADS, 1, 128), lambda n: (nsteps - 1 - n, 0, 0, 0))
    return pl.pallas_call(
        body, grid=(nsteps,),
        in_specs=[big, pl.BlockSpec((CPS_SCAN, DN_HEADS, DN_HD, DN_HD), lambda n: (nsteps - 1 - n, 0, 0, 0)),
                  big, big, big, big, sq, glb],
        out_specs=[big, big, big, big, sq, glb],
        out_shape=[_sds((s, 512))] * 4 + [_sds((nc, DN_HEADS, CHUNK, CHUNK)), _sds((nc, DN_HEADS, 1, 128))],
        scratch_shapes=[pltpu.VMEM((DN_HEADS, DN_HD, DN_HD), F32)],
        compiler_params=_cp(("arbitrary",)), name="dn_scan_bwd")(do, sp, qd, kt, w, vn, attn, gl)


def _dn_prep_bwd(qn, kn, v, bg, bgt, tri, sel, t_inv, attn, u, w, du, dw, dqd, dkt, dattn, dgl):
    s = qn.shape[0]
    nc = s // CHUNK

    def body(q_ref, k_ref, v_ref, bg_ref, bgt_ref, tri_ref, sel_ref, t_ref, attn_ref, u_ref, w_ref,
             du_ref, dw_ref, dqd_ref, dkt_ref, dattn_ref, dgl_ref,
             dq_ref, dk_ref, dv_ref, dbg_ref):
        tri_v = tri_ref[...]
        lane = lax.broadcasted_iota(jnp.int32, (CHUNK, 128), 1)
        rowi = lax.broadcasted_iota(jnp.int32, (CHUNK, 128), 0)
        ones_b = jnp.ones((CHUNK, 128), BF16)
        ones_sq = jnp.ones((128, 128), BF16)
        parts = [[] for _ in range(CPS)]

        def chain(cc, h, bg3, gc3, gc_row):
            rows = slice(CHUNK * cc, CHUNK * cc + CHUNK)
            lanes = slice(128 * h, 128 * h + 128)
            gcc, beta, incl, strict, decay, gl = _chunk_common(bg3, gc3, gc_row, h, sel_ref)
            yield
            q = q_ref[rows, lanes]
            k = k_ref[rows, lanes]
            vv = v_ref[rows, lanes]
            ts = _split(t_ref[cc, h])
            egc = jnp.exp(gcc)
            kb = k * beta
            a_mat = jnp.where(strict, _mm_nt(kb, k) * decay, 0.0)
            dvb = _mm3_tn(ts, _split(du_ref[rows, lanes]))
            dkbg = _mm3_tn(ts, _split(dw_ref[rows, lanes]))
            yield
            d_a = jnp.where(strict, -(_mm_nt(dvb, u_ref[rows, lanes]) + _mm_nt(dkbg, w_ref[rows, lanes])), 0.0)
            d_m = d_a * decay
            dattn_m = jnp.where(incl, dattn_ref[cc, h], 0.0)
            dqk = dattn_m * decay
            e_hi, e_lo = _split(d_a * a_mat + dattn_m * attn_ref[cc, h])
            yield
            dkb = _mm(d_m, k)
            dk = _mm_tn(d_m, kb) + _mm_tn(dqk, q)
            dq = _mm(dqk, k)
            e_colsum = _dot_tn(e_hi, ones_b) + _dot_tn(e_lo, ones_b)
            e_rowsum = _dot(e_hi, ones_b) + _dot(e_lo, ones_b)
            dqd = dqd_ref[rows, lanes]
            dkt = dkt_ref[rows, lanes]
            sums = _rowsum_b(jnp.concatenate([dqd * q, dkt * k, dkbg * k, dkb * k, dvb * vv], axis=0), ones_sq)
            s_dqd, s_dkt, rk, s_dkb, s_dvb = (sums[CHUNK * n:CHUNK * n + CHUNK] for n in range(5))
            yield
            tail = jnp.exp(gl - gcc)
            r = s_dkt * tail
            dgl_tot = jnp.sum(r, axis=0, keepdims=True) + dgl_ref[cc, h] * jnp.exp(gl)
            dgc = e_rowsum - e_colsum + s_dqd * egc - r + rk * beta * egc
            dgc = dgc + jnp.where(rowi == CHUNK - 1, dgl_tot, 0.0)
            dq_ref[rows, lanes] = dq + dqd * egc
            dk_ref[rows, lanes] = dk + dkt * tail + dkbg * (beta * egc) + dkb * beta
            dv_ref[rows, lanes] = dvb * beta
            parts[cc].append((h, dgc, rk * egc + s_dkb + s_dvb))

        gens = []
        for cc in range(CPS):
            bgv = bg_ref[CHUNK * cc:CHUNK * cc + CHUNK, :]
            bg3, gc3, gc_row = _chunk_cumsum(bgv, bgt_ref[cc], tri_v)
            gens += [chain(cc, h, bg3, gc3, gc_row) for h in range(DN_HEADS)]
        _interleave(gens)
        for cc in range(CPS):
            dgc_mat = jnp.zeros((CHUNK, 128), F32)
            dbeta_mat = jnp.zeros((CHUNK, 128), F32)
            for h, dgc, dbeta in parts[cc]:
                dgc_mat = dgc_mat + jnp.where(lane == DN_HEADS + h, dgc, 0.0)
                dbeta_mat = dbeta_mat + jnp.where(lane == h, dbeta, 0.0)
            dbg_ref[CHUNK * cc:CHUNK * cc + CHUNK, :] = _mmx_tn(tri_v, dgc_mat) + dbeta_mat

    big = pl.BlockSpec((CPS * CHUNK, 512), lambda n: (n, 0))
    sq = pl.BlockSpec((CPS, DN_HEADS, CHUNK, CHUNK), lambda n: (n, 0, 0, 0))
    glb = pl.BlockSpec((CPS, DN_HEADS, 1, 128), lambda n: (n, 0, 0, 0))
    small = pl.BlockSpec((CPS * CHUNK, 128), lambda n: (n, 0))
    return pl.pallas_call(
        body, grid=(nc // CPS,),
        in_specs=[big, big, big, small, pl.BlockSpec((CPS, 8, CHUNK), lambda n: (n, 0, 0)),
                  pl.BlockSpec((CHUNK, CHUNK), lambda n: (0, 0)),
                  pl.BlockSpec((8, 128, 128), lambda n: (0, 0, 0)), sq, sq, big, big,
                  big, big, big, big, sq, glb],
        out_specs=[big, big, big, small],
        out_shape=[_sds((s, 512))] * 3 + [_sds((s, 128))],
        compiler_params=_cp(("parallel",)), name="dn_prep_bwd")(qn, kn, v, bg, bgt, tri, sel, t_inv, attn, u, w,
                                                                  du, dw, dqd, dkt, dattn, dgl)


def _dn_pre_bwd(dqn, dkn, dv, dbg, proj, conv_w8, alog_l, dtb_l):
    s = proj.shape[0]
    tr = 512
    nh = tr // 8

    def body(dq_ref, dk_ref, dv_ref, dbg_ref, u_ref, halo_ref, ba_ref, w_ref, al_ref, dt_ref,
             dy_ref, dba_ref, dsm_ref):
        i = pl.program_id(0)

        @pl.when(i == 0)
        def _():
            dsm_ref[...] = jnp.zeros_like(dsm_ref)

        keep = (i > 0).astype(F32)
        for c in range(12):
            lanes = slice(128 * c, 128 * c + 128)
            y = _conv_group(u_ref[:, lanes], halo_ref[:, lanes] * keep, w_ref, c)
            sg = _sigmoid(y)
            sv = y * sg
            if c < 8:
                rs = lax.rsqrt(jnp.sum(sv * sv, axis=1, keepdims=True) + EPS)
                n = sv * rs
                if c < 4:
                    dn = dq_ref[:, lanes] * (DN_HD ** -0.5)
                else:
                    dn = dk_ref[:, slice(128 * (c - 4), 128 * (c - 3))]
                dsv = rs * (dn - n * jnp.sum(dn * n, axis=1, keepdims=True))
            else:
                dsv = dv_ref[:, slice(128 * (c - 8), 128 * (c - 7))]
            dy_ref[:, lanes] = dsv * _silu_grad(y, sg)
        _, lane, sig_b, t, nega, g = _beta_g(ba_ref[...], al_ref[...], dt_ref[...])
        dbg = dbg_ref[...]
        da = dbg * nega * _sigmoid(t)
        is_b = lane < DN_HEADS
        is_a = jnp.logical_and(lane >= DN_HEADS, lane < 2 * DN_HEADS)
        dba_ref[...] = jnp.where(is_b, dbg * sig_b * (1.0 - sig_b), jnp.where(is_a, da, 0.0)).astype(BF16)
        d_alog = jnp.sum(jnp.where(is_a, dbg * g, 0.0), axis=0, keepdims=True)
        d_dtb = jnp.sum(jnp.where(is_a, da, 0.0), axis=0, keepdims=True)
        row = lax.broadcasted_iota(jnp.int32, (8, 128), 0)
        dsm_ref[...] = dsm_ref[...] + jnp.where(row == 0, d_alog, jnp.where(row == 1, d_dtb, 0.0))

    blk = pl.BlockSpec((tr, 512), lambda i: (i, 0))
    return pl.pallas_call(
        body, grid=(s // tr,),
        in_specs=[blk, blk, blk, pl.BlockSpec((tr, 128), lambda i: (i, 0)),
                  pl.BlockSpec((tr, 1536), lambda i: (i, 0)),
                  pl.BlockSpec((8, 1536), lambda i: (jnp.maximum(i * nh - 1, 0), 0)),
                  pl.BlockSpec((tr, 128), lambda i: (i, COL_BA_128)),
                  pl.BlockSpec((8, 1536), lambda i: (0, 0)),
                  pl.BlockSpec((1, 128), lambda i: (0, 0)), pl.BlockSpec((1, 128), lambda i: (0, 0))],
        out_specs=[pl.BlockSpec((tr, 1536), lambda i: (i, 0)), pl.BlockSpec((tr, 128), lambda i: (i, 0)),
                   pl.BlockSpec((8, 128), lambda i: (0, 0))],
        out_shape=[_sds((s, 1536)), _sds((s, 128), BF16), _sds((8, 128))],
        compiler_params=_cp(("arbitrary",)), name="dn_pre_bwd")(dqn, dkn, dv, dbg, proj, proj, proj, conv_w8,
                                                                 alog_l, dtb_l)


def _conv_bwd(dy, proj, conv_w8):
    s = dy.shape[0]
    tr = 512
    nh = tr // 8
    nblk = s // tr

    def body(dy_ref, dyn_ref, u_ref, halo_ref, w_ref, du_ref, dw_ref):
        i = pl.program_id(0)

        @pl.when(i == 0)
        def _():
            dw_ref[...] = jnp.zeros_like(dw_ref)

        keep_prev = (i > 0).astype(F32)
        keep_next = (i < nblk - 1).astype(F32)
        row = lax.broadcasted_iota(jnp.int32, (8, 128), 0)
        for c in range(12):
            lanes = slice(128 * c, 128 * c + 128)
            dyc = dy_ref[:, lanes]
            dcat = jnp.concatenate([dyc, dyn_ref[:, lanes] * keep_next], axis=0)
            xcat = jnp.concatenate([halo_ref[:, lanes] * keep_prev, u_ref[:, lanes]], axis=0)
            du = dyc * w_ref[CONV_W - 1:CONV_W, lanes]
            dwc = jnp.where(row == CONV_W - 1, jnp.sum(dyc * u_ref[:, lanes], axis=0, keepdims=True), 0.0)
            for k in range(1, CONV_W):
                du = du + pltpu.roll(dcat, tr + 8 - k, 0)[0:tr] * w_ref[CONV_W - 1 - k:CONV_W - k, lanes]
                ush = pltpu.roll(xcat, k, 0)[8:8 + tr]
                dwc = dwc + jnp.where(row == CONV_W - 1 - k, jnp.sum(dyc * ush, axis=0, keepdims=True), 0.0)
            du_ref[:, lanes] = du.astype(BF16)
            dw_ref[:, lanes] = dw_ref[:, lanes] + dwc

    return pl.pallas_call(
        body, grid=(nblk,),
        in_specs=[pl.BlockSpec((tr, 1536), lambda i: (i, 0)),
                  pl.BlockSpec((8, 1536), lambda i: (jnp.minimum((i + 1) * nh, s // 8 - 1), 0)),
                  pl.BlockSpec((tr, 1536), lambda i: (i, 0)),
                  pl.BlockSpec((8, 1536), lambda i: (jnp.maximum(i * nh - 1, 0), 0)),
                  pl.BlockSpec((8, 1536), lambda i: (0, 0))],
        out_specs=[pl.BlockSpec((tr, 1536), lambda i: (i, 0)), pl.BlockSpec((8, 1536), lambda i: (0, 0))],
        out_shape=[_sds((s, 1536), BF16), _sds((8, 1536))],
        compiler_params=_cp(("arbitrary",)), name="conv_bwd")(dy, dy, proj, proj, conv_w8)


PIECE_WIDTHS = (1536, 512, 512, 512, 512, 512, 128)


def _in_bwd_dx(pieces, w_bf, x, dy, norm_w, ds_accs, buckets, chip_sums=None):
    s = x.shape[0]
    tm = 512
    n_piece = len(PIECE_WIDTHS)
    n_step = s // tm
    n_arr = 0 if chip_sums is None else len(chip_sums)
    n_sent = 3 * n_arr

    def body(*refs):
        refs = list(refs)

        def take(n):
            return [refs.pop(0) for _ in range(n)]

        piece_refs = take(n_piece)
        w_ref, x_ref, dy_ref, nw_ref = take(4)
        ds_refs = take(len(ds_accs))
        bk_ref, = take(1)
        hb_refs = take(n_arr)
        dx_ref, dnw_ref, drb_ref = take(3)
        recv_refs = take(n_arr)
        i = pl.program_id(0)

        def copies():
            send_sems, recv_sems = refs
            xi, yi, ci = _position()
            out = []
            for j, (px, py) in enumerate(_other_chips(xi, yi)):
                for t, (src, dst) in enumerate(zip(hb_refs, recv_refs)):
                    k = n_arr * j + t
                    out.append(pltpu.make_async_remote_copy(
                        src_ref=src.at[2 * px + py], dst_ref=dst.at[j], send_sem=send_sems.at[k],
                        recv_sem=recv_sems.at[k], device_id=(px, py, ci), device_id_type=MESH_ID))
            return out

        @pl.when(i == 0)
        def _():
            dnw_ref[...] = jnp.zeros_like(dnw_ref)
            for cp in (copies() if n_sent else []):
                cp.start()

        @pl.when(i == 1)
        def _():
            _bias_grad(ds_refs, bk_ref, drb_ref)

        dp = jnp.concatenate([r[...] for r in piece_refs], axis=1)
        dh = _dot(dp, w_ref[...])
        xv = x_ref[...]
        rstd = lax.rsqrt(jnp.mean(xv * xv, axis=-1, keepdims=True) + EPS)
        xh = xv * rstd
        dnw_ref[...] = dnw_ref[...] + jnp.sum(dh * xh, axis=0, keepdims=True)
        g = dh * nw_ref[...]
        dx_ref[...] = rstd * (g - xh * jnp.mean(g * xh, axis=-1, keepdims=True)) + dy_ref[...]

        if n_sent:
            @pl.when(i == n_step - 1)
            def _():
                cps = copies()
                for cp in cps:
                    cp.wait_recv()
                for cp in cps:
                    cp.wait_send()

    def blk(n):
        return pl.BlockSpec((tm, n), lambda i: (i, 0))

    vm = pl.BlockSpec(memory_space=pltpu.VMEM)
    in_specs = [blk(n) for n in PIECE_WIDTHS] + [pl.BlockSpec((D_IN_PAD, D_MODEL), lambda i: (0, 0)), blk(D_MODEL),
                                                blk(D_MODEL), pl.BlockSpec((1, D_MODEL), lambda i: (0, 0))]
    in_specs += [vm] * (len(ds_accs) + 1)
    out_specs = [blk(D_MODEL), pl.BlockSpec((8, D_MODEL), lambda i: (0, 0)), pl.BlockSpec(memory_space=pltpu.SMEM)]
    out_shape = [_sds((s, D_MODEL)), _sds((8, D_MODEL)), _sds((ATT_HEADS, N_BUCKETS))]
    scratch = []
    extra = ()
    if n_sent:
        extra = tuple(chip_sums)
        in_specs += [ANY] * n_arr
        out_specs += [ANY] * n_arr
        out_shape += [_sds((3,) + a.shape[1:], a.dtype) for a in extra]
        scratch = [pltpu.SemaphoreType.DMA((n_sent,)), pltpu.SemaphoreType.DMA((n_sent,))]
    return pl.pallas_call(
        body, grid=(n_step,), in_specs=in_specs, out_specs=out_specs, out_shape=out_shape, scratch_shapes=scratch,
        compiler_params=_cp(("arbitrary",)), name="in_bwd_dx")(*pieces, w_bf, x, dy, norm_w, *ds_accs, buckets, *extra)


def _in_bwd_dw(pieces, x, norm_w):
    s = x.shape[0]
    tm = 512
    n_piece = len(PIECE_WIDTHS)

    def body(*refs):
        piece_refs = refs[:n_piece]
        x_ref, nw_ref, dw_ref = refs[n_piece:]
        i = pl.program_id(0)

        @pl.when(i == 0)
        def _():
            dw_ref[...] = jnp.zeros_like(dw_ref)

        xv = x_ref[...]
        rstd = lax.rsqrt(jnp.mean(xv * xv, axis=-1, keepdims=True) + EPS)
        h = (xv * rstd * nw_ref[...]).astype(BF16)
        at = 0
        for r, width in zip(piece_refs, PIECE_WIDTHS):
            dw_ref[at:at + width, :] = dw_ref[at:at + width, :] + _dot_tn(r[...], h)
            at += width

    return pl.pallas_call(
        body, grid=(s // tm,),
        in_specs=[pl.BlockSpec((tm, n), lambda i: (i, 0)) for n in PIECE_WIDTHS]
        + [pl.BlockSpec((tm, D_MODEL), lambda i: (i, 0)), pl.BlockSpec((1, D_MODEL), lambda i: (0, 0))],
        out_specs=pl.BlockSpec((D_IN_PAD, D_MODEL), lambda i: (0, 0)),
        out_shape=_sds((D_IN_PAD, D_MODEL)),
        compiler_params=_cp(("arbitrary",)), name="in_bwd_dw")(*pieces, x, norm_w)


def _flat(a):
    return a.reshape(-1, a.shape[-1])


def _as_pattern(a, r):
    return a if r == 1 else a.reshape(r, a.shape[0] // r, a.shape[1])


D_SHARD = D_IN // N_CHIPS
BA_START = 4 * D_DN
BA_PACKED = 4096
S1_HEAD = BA_START - D_SHARD
S1_BA = 2 * D_SHARD - BA_START


def _pack_rows(g):
    pad = jnp.zeros((D_IN_PAD - D_IN, g.shape[2]), g.dtype)
    s2_ba = 2 * DN_HEADS - S1_BA
    return jnp.concatenate([g[0][:D_SHARD], g[1][:S1_HEAD], g[2][s2_ba:D_SHARD], g[3][:D_SHARD],
                            g[1][S1_HEAD:D_SHARD], g[2][:s2_ba], pad], axis=0)


def _pack_w(g):
    n = g.shape[2]

    def body(g_ref, o_ref):
        g32 = g_ref.bitcast(jnp.uint32)
        o32 = o_ref.bitcast(jnp.uint32)
        full, head, ba1 = D_SHARD // 2, S1_HEAD // 2, S1_BA // 2
        ba2 = DN_HEADS - ba1
        pieces = [(0, 0, full), (1, 0, head), (2, ba2, full), (3, 0, full), (1, head, full), (2, 0, ba2)]
        at = 0
        for chip, lo, hi in pieces:
            o32[at:at + hi - lo, :] = g32[chip, lo:hi, :]
            at += hi - lo
        o32[at:D_IN_PAD // 2, :] = jnp.zeros((D_IN_PAD // 2 - at, n), jnp.uint32)

    vm = pl.BlockSpec(memory_space=pltpu.VMEM)
    return pl.pallas_call(body, in_specs=[vm], out_specs=vm, out_shape=_sds((D_IN_PAD, n), g.dtype),
                          compiler_params=_cp(), name="pack_w")(g)


def _unpack_w(p, rows):
    n = p.shape[1]
    tn = 256
    mid = BA_PACKED + S1_BA
    pieces = [(0, 0, (0, D_SHARD)), (1, 0, (D_SHARD, BA_START)), (1, S1_HEAD, (BA_PACKED, mid)),
              (2, 0, (mid, BA_PACKED + 2 * DN_HEADS)), (2, 2 * DN_HEADS - S1_BA, (BA_START, BA_START + S1_HEAD)),
              (3, 0, (BA_START + S1_HEAD, BA_PACKED))]

    def body(p_ref, o_ref):
        for chip, at, (lo, hi) in pieces:
            o_ref[chip, at:at + hi - lo, :] = p_ref[lo:hi, :]
        for chip in range(N_CHIPS):
            o_ref[chip, D_SHARD:rows, :] = jnp.zeros((rows - D_SHARD, tn), p.dtype)

    return pl.pallas_call(
        body, grid=(n // tn,),
        in_specs=[pl.BlockSpec((D_IN_PAD, tn), lambda j: (0, j))],
        out_specs=pl.BlockSpec((N_CHIPS, rows, tn), lambda j: (0, 0, j)),
        out_shape=_sds((N_CHIPS, rows, n), p.dtype),
        compiler_params=_cp(("parallel",)), name="unpack_w")(p)


def _unpack_rows(p, rows):
    mid = BA_PACKED + S1_BA
    pad = jnp.zeros((rows - D_SHARD, p.shape[1]), p.dtype)
    return jnp.stack([jnp.concatenate([p[0:D_SHARD], pad], axis=0),
                      jnp.concatenate([p[D_SHARD:BA_START], p[BA_PACKED:mid], pad], axis=0),
                      jnp.concatenate([p[mid:BA_PACKED + 2 * DN_HEADS], p[BA_START:BA_START + S1_HEAD], pad], axis=0),
                      jnp.concatenate([p[BA_START + S1_HEAD:BA_PACKED], pad], axis=0)])


def _lane_row(vec, offset):
    return jnp.pad(vec.reshape(1, -1), ((0, 0), (offset, 128 - offset - vec.shape[0])))


def _local_step(x, tgt, norm_w, w_bf, conv_w, a_log, dt_bias, dn_norm_w, q_norm_w, k_norm_w, bias, w_out_bf):
    s = x.shape[0]
    nc = s // CHUNK
    conv_w8 = jnp.pad(conv_w, ((0, 8 - CONV_W), (0, 0)))
    alog_l = _lane_row(a_log.reshape(-1), DN_HEADS)
    dtb_l = _lane_row(dt_bias.reshape(-1), DN_HEADS)
    dnw_t = jnp.tile(dn_norm_w.reshape(1, DN_HD), (1, DN_HEADS))
    qw_t = jnp.tile(q_norm_w.reshape(1, ATT_HD), (1, ATT_HEADS))
    kw_t = jnp.tile(k_norm_w.reshape(1, ATT_HD), (1, ATT_HEADS))
    bd128 = _compact_mat(DN_HD)
    bd64 = _compact_mat(ATT_HD)
    tri = _tri_incl()
    sel = _lane_select()
    buckets = _bucket_tables()

    proj = _in_proj(x, norm_w, w_bf)
    qn, kn, v_dn, bg = _dn_pre(proj, conv_w8, alog_l, dtb_l)
    bgt = bg[:, 0:8].reshape(nc, CHUNK, 8).transpose(0, 2, 1)
    u, w, qd, kt, attn, t_inv, gl = _dn_prep(qn, kn, v_dn, bg, bgt, tri, sel)
    o_dn, vn, sp = _dn_scan(u, w, qd, kt, attn, gl)
    q1, k1, v1, q4, k4, v4, q16, k16, v16 = _att_pre(proj, qw_t, kw_t, bd64)
    rs = [r for _, r in PATTERNS]
    qkv = [(q1, k1, v1, 0), (_flat(q4), _flat(k4), _flat(v4), 0), (_flat(q16), _flat(k16), _flat(v16), 0)]
    o_pats, lse_pats = [], []
    for p, r in enumerate(rs):
        o_p, lse_p = _att_fwd(*qkv[p], bias, p, r, "att_fwd_r%d" % r)
        o_pats.append(_as_pattern(o_p, r))
        lse_pats.append(_as_pattern(lse_p, r))
    mixed, o_att, l1, l4, l16 = _post_fwd(o_dn, proj, o_pats, lse_pats, dnw_t, bd128)
    dy, loss_blk = _out_fwd(x, mixed, w_out_bf, tgt)

    dmixed, d_w_out = _out_bwd(dy, mixed, w_out_bf)
    do_dn, dz, dgate, do1, do4, do16, dl1, dl4, dl16, d_dnw = _post_bwd(dmixed, o_dn, proj, o_att, dnw_t, bd128)
    side = [(do1, l1, dl1), (_flat(do4), _flat(l4), _flat(dl4)), (_flat(do16), _flat(l16), _flat(dl16))]
    dq_pats, dk_pats, dv_pats, ds_accs = [], [], [], []
    for p, r in enumerate(rs):
        dq_p, dk_p, dv_p, ds_p = _att_bwd(*qkv[p], *side[p], bias, p, r, "att_bwd_r%d" % r)
        dq_pats.append(_as_pattern(dq_p, r))
        dk_pats.append(_as_pattern(dk_p, r))
        dv_pats.append(_as_pattern(dv_p, r))
        ds_accs.append(ds_p)
    dq_att, dk_att, dv_att, d_qw, d_kw = _att_pre_bwd(dq_pats, dk_pats, dv_pats, proj, qw_t, kw_t, bd64)
    du, dqd, dkt, dw, dattn, dgl = _dn_scan_bwd(do_dn, sp, qd, kt, w, vn, attn, gl)
    dqn, dkn, dv_dn, dbg = _dn_prep_bwd(qn, kn, v_dn, bg, bgt, tri, sel, t_inv, attn, u, w, du, dw, dqd, dkt, dattn,
                                        dgl)
    dyc, dba, dsm = _dn_pre_bwd(dqn, dkn, dv_dn, dbg, proj, conv_w8, alog_l, dtb_l)
    d_qkv_dn, d_conv8 = _conv_bwd(dyc, proj, conv_w8)
    pieces = (d_qkv_dn, dz, dq_att, dk_att, dv_att, dgate, dba)
    d_w_in_t = _in_bwd_dw(pieces, x, norm_w)
    last = functools.partial(_in_bwd_dx, pieces, w_bf, x, dy, norm_w, ds_accs, buckets)

    grads = dict(
        w_in_t=d_w_in_t,
        conv_w=d_conv8[0:CONV_W, :],
        a_log=dsm[0:1, DN_HEADS:2 * DN_HEADS],
        dt_bias=dsm[1:2, DN_HEADS:2 * DN_HEADS],
        dn_norm_w=d_dnw[0:1, :],
        q_norm_w=d_qw[0:1, :].reshape(ATT_HEADS, ATT_HD),
        k_norm_w=d_kw[0:1, :].reshape(ATT_HEADS, ATT_HD),
        w_out=d_w_out,
    )
    return loss_blk[0, 0], last, grads


MESH_ID = pl.DeviceIdType.MESH
ANY = pl.BlockSpec(memory_space=pl.ANY)


def _position():
    return lax.axis_index("x"), lax.axis_index("y"), lax.axis_index("c")


def _other_chips(x, y):
    return [(1 - x, y), (x, 1 - y), (1 - x, 1 - y)]


SHARD_PAD = 1040
WIN = 528


def _row_split(n):
    return (n // 2) // 128 * 128


def _part(ref, cc):
    n = ref.shape[0]
    sp = _row_split(n)
    return ref.at[pl.ds(0, sp)] if cc == 0 else ref.at[pl.ds(sp, n - sp)]


def _halves(ref):
    n = ref.shape[0]
    sp = (n // 2) // 16 * 16
    return ref.at[pl.ds(0, sp)], ref.at[pl.ds(sp, n - sp)]


def _gather_weights(wt_s, w_out_s, conv_s, rel_bias, buckets):
    def body(a_ref, b_ref, c_ref, rb_ref, bk_ref, ga_ref, gb_ref, gc_ref, bias_ref, send_sems, recv_sems, loc_sems,
             a_vmem, b_vmem):
        x, y, c = _position()
        me = 2 * x + y
        sib = (x, y, 1 - c)
        big = ((a_ref, ga_ref), (b_ref, gb_ref))
        stage_in = [pltpu.make_async_copy(a_ref, a_vmem, loc_sems.at[0]),
                    pltpu.make_async_copy(b_ref, b_vmem, loc_sems.at[1])]
        local = [pltpu.make_async_copy(a_vmem, ga_ref.at[me], loc_sems.at[0]),
                 pltpu.make_async_copy(b_vmem, gb_ref.at[me], loc_sems.at[1]),
                 pltpu.make_async_copy(c_ref, gc_ref.at[me], loc_sems.at[2])]
        for cp in stage_in:
            cp.start()
        local[2].start()
        for cp in stage_in:
            cp.wait()
        for cp in local[:2]:
            cp.start()
        others = _other_chips(x, y)

        def exchange(cc):
            nbr = others[:2]
            dg = others[2]
            pending = []

            def copy(k, src, dst, to):
                cp = pltpu.make_async_remote_copy(src_ref=src, dst_ref=dst, send_sem=send_sems.at[k],
                                                  recv_sem=recv_sems.at[k], device_id=to, device_id_type=MESH_ID)
                pending.append(cp)
                cp.start()

            def landed(k, dst):
                pltpu.make_async_remote_copy(src_ref=dst, dst_ref=dst, send_sem=send_sems.at[k],
                                             recv_sem=recv_sems.at[k], device_id=sib, device_id_type=MESH_ID).wait_recv()

            def slot(dst, chip, c_part):
                return _part(dst.at[2 * chip[0] + chip[1]], c_part)

            for i, n in enumerate(nbr):
                for t, (src, dst) in enumerate(big):
                    copy(2 * i + t, _part(src, cc), _part(dst.at[me], cc), (*n, c))
            for j, chip in enumerate(others):
                copy(4 + j, c_ref, gc_ref.at[me], (*chip, c))
            _bias_table(rb_ref, bk_ref, bias_ref)
            for i, n in enumerate(nbr):
                for t, (_, dst) in enumerate(big):
                    got = slot(dst, n, cc)
                    landed(2 * i + t, got)
                    half = _halves(got)[1 - i]
                    copy(7 + 2 * (1 - i) + t, half, half, (*nbr[1 - i], c))
                    copy(11 + 2 * i + t, got, got, sib)
            for i, n in enumerate(nbr):
                for t, (_, dst) in enumerate(big):
                    half = _halves(slot(dst, dg, cc))[i]
                    landed(7 + 2 * i + t, half)
                    copy(15 + 2 * i + t, half, half, sib)
            for j, chip in enumerate(others):
                landed(4 + j, gc_ref.at[2 * chip[0] + chip[1]])
            for i, n in enumerate(nbr):
                for t, (_, dst) in enumerate(big):
                    landed(11 + 2 * i + t, slot(dst, n, 1 - cc))
                    landed(15 + 2 * i + t, _halves(slot(dst, dg, 1 - cc))[i])
            for cp in pending:
                cp.wait_send()

        for cc in (0, 1):
            pl.when(c == cc)(functools.partial(exchange, cc))
        for cp in local:
            cp.wait()

    srcs = (wt_s, w_out_s, conv_s)
    n_sem = 19
    return pl.pallas_call(
        body, in_specs=[ANY] * 3 + [pl.BlockSpec(memory_space=pltpu.SMEM), pl.BlockSpec(memory_space=pltpu.VMEM)],
        out_specs=[ANY] * 3 + [pl.BlockSpec(memory_space=pltpu.VMEM)],
        out_shape=[_sds((N_CHIPS,) + a.shape, a.dtype) for a in srcs]
        + [_sds((len(PATTERNS), ATT_HEADS, BLK, 2 * BLK))],
        scratch_shapes=[pltpu.SemaphoreType.DMA((n_sem,)), pltpu.SemaphoreType.DMA((n_sem,)),
                        pltpu.SemaphoreType.DMA((3,)), pltpu.VMEM(wt_s.shape, wt_s.dtype),
                        pltpu.VMEM(w_out_s.shape, w_out_s.dtype)],
        compiler_params=_cp(), name="gather_weights")(*srcs, rel_bias, buckets)


N_DEV = 8
PEER_FLIPS = [(dx, dy, dc) for dx in (0, 1) for dy in (0, 1) for dc in (0, 1)][1:]


def _small_copies(s_ref, rs_ref, send_sems, recv_sems):
    x, y, c = _position()
    dev = 4 * x + 2 * y + c
    sends, recvs = [], []
    for f, (dx, dy, dc) in enumerate(PEER_FLIPS):
        peer = (x ^ dx, y ^ dy, c ^ dc)
        sends.append(pltpu.make_async_remote_copy(
            src_ref=s_ref, dst_ref=rs_ref.at[dev], send_sem=send_sems.at[f], recv_sem=recv_sems.at[f],
            device_id=peer, device_id_type=MESH_ID))
        recvs.append(pltpu.make_async_remote_copy(
            src_ref=s_ref, dst_ref=rs_ref.at[4 * peer[0] + 2 * peer[1] + peer[2]], send_sem=send_sems.at[f],
            recv_sem=recv_sems.at[f], device_id=peer, device_id_type=MESH_ID))
    return sends, recvs


def _fill_parts(h_in, r_in, h_out, r_out, small, rows_in, rows_out):
    def body(ha_ref, ra_ref, hb_ref, rb_ref, s_ref, fa_ref, fb_ref, rs_ref, send_sems, recv_sems, loc_sems,
             small_send, small_recv, a_vmem, b_vmem, ra_vmem, rb_vmem):
        x, y, c = _position()
        sib = (x, y, 1 - c)
        chip = 2 * x + y
        pairs = ((a_vmem, fa_ref), (b_vmem, fb_ref))
        own_small = pltpu.make_async_copy(s_ref, rs_ref.at[4 * x + 2 * y + c], loc_sems.at[2])
        own_small.start()
        small_sends, small_recvs = _small_copies(s_ref, rs_ref, small_send, small_recv)
        for cp in small_sends:
            cp.start()
        stage_in = [pltpu.make_async_copy(ha_ref.at[chip], a_vmem, loc_sems.at[0]),
                    pltpu.make_async_copy(hb_ref.at[chip], b_vmem, loc_sems.at[1]),
                    pltpu.make_async_copy(ra_ref, ra_vmem, loc_sems.at[3]),
                    pltpu.make_async_copy(rb_ref, rb_vmem, loc_sems.at[4])]
        for cp in stage_in:
            cp.start()
        for cp in stage_in:
            cp.wait()
        for tot, recv in ((a_vmem, ra_vmem), (b_vmem, rb_vmem)):
            acc = tot[...]
            for j in range(3):
                acc = acc + recv[j].astype(F32)
            tot[...] = acc

        def fill(cc):
            mine = [_part(dst, cc) for _, dst in pairs]
            srcs = [src.at[pl.ds(0, m.shape[0])] for src, m in zip((a_vmem, b_vmem), mine)]
            local = [pltpu.make_async_copy(s, m, loc_sems.at[t]) for t, (s, m) in enumerate(zip(srcs, mine))]
            sends = [pltpu.make_async_remote_copy(src_ref=s, dst_ref=m, send_sem=send_sems.at[t],
                                                  recv_sem=recv_sems.at[t], device_id=sib, device_id_type=MESH_ID)
                     for t, (s, m) in enumerate(zip(srcs, mine))]
            for cp in local + sends:
                cp.start()
            for t, (_, dst) in enumerate(pairs):
                theirs = _part(dst, 1 - cc)
                pltpu.make_async_remote_copy(src_ref=theirs, dst_ref=theirs, send_sem=send_sems.at[t],
                                             recv_sem=recv_sems.at[t], device_id=sib, device_id_type=MESH_ID).wait_recv()
            for cp in sends:
                cp.wait_send()
            for cp in local:
                cp.wait()

        for cc in (0, 1):
            pl.when(c == cc)(functools.partial(fill, cc))
        for cp in small_recvs:
            cp.wait_recv()
        for cp in small_sends:
            cp.wait_send()
        own_small.wait()

    n_peer = len(PEER_FLIPS)
    return pl.pallas_call(
        body, in_specs=[ANY] * 5, out_specs=[ANY] * 3,
        out_shape=[_sds((rows_in, D_MODEL)), _sds((rows_out, D_MODEL)), _sds((N_DEV,) + small.shape, small.dtype)],
        scratch_shapes=[pltpu.SemaphoreType.DMA((2,)), pltpu.SemaphoreType.DMA((2,)), pltpu.SemaphoreType.DMA((5,)),
                        pltpu.SemaphoreType.DMA((n_peer,)), pltpu.SemaphoreType.DMA((n_peer,)),
                        pltpu.VMEM(h_in.shape[1:], F32), pltpu.VMEM(h_out.shape[1:], F32),
                        pltpu.VMEM(r_in.shape, r_in.dtype), pltpu.VMEM(r_out.shape, r_out.dtype)],
        compiler_params=_cp(), name="fill_parts")(h_in, r_in, h_out, r_out, small)


def _chip_sums(u_in, u_out, win_in, win_out):
    wins = (win_in, win_out)

    def body(a_ref, b_ref, ha_ref, hba_ref, hb_ref, hbb_ref, send_sems, recv_sems, loc_sems, out_sems,
             mine_a, theirs_a, sum_a, sumb_a, mine_b, theirs_b, sum_b, sumb_b):
        x, y, c = _position()
        sib = (x, y, 1 - c)
        groups = ((a_ref, mine_a, theirs_a, sum_a, sumb_a, ha_ref, hba_ref, win_in),
                  (b_ref, mine_b, theirs_b, sum_b, sumb_b, hb_ref, hbb_ref, win_out))
        loads, sends = [], []
        for t, (src, mine, theirs, _, _, _, _, win) in enumerate(groups):
            split = _row_split(src.shape[1])
            for k in range(N_CHIPS):
                n = N_CHIPS * t + k
                loads.append(pltpu.make_async_copy(
                    src.at[k, pl.ds(pl.multiple_of(c * split, split), win), :], mine.at[k], loc_sems.at[n]))
                sends.append(pltpu.make_async_remote_copy(
                    src_ref=src.at[k, pl.ds(pl.multiple_of((1 - c) * split, split), win), :], dst_ref=theirs.at[k],
                    send_sem=send_sems.at[n], recv_sem=recv_sems.at[n], device_id=sib, device_id_type=MESH_ID))
        for cp in sends + loads:
            cp.start()
        stores = []
        for t, (_, mine, theirs, tot, totb, h_out, hb_out, _) in enumerate(groups):
            for k in range(N_CHIPS):
                n = N_CHIPS * t + k
                loads[n].wait()
                sends[n].wait_recv()
                val = mine[k] + theirs[k]
                tot[k] = val
                totb[k] = val.astype(BF16)
                stores += [pltpu.make_async_copy(tot.at[k], h_out.at[k], out_sems.at[2 * n]),
                           pltpu.make_async_copy(totb.at[k], hb_out.at[k], out_sems.at[2 * n + 1])]
                stores[-2].start()
                stores[-1].start()
        for cp in sends:
            cp.wait_send()
        for cp in stores:
            cp.wait()

    shapes = [(N_CHIPS, w, D_MODEL) for w in wins]
    n_cp = 2 * N_CHIPS
    vmem = []
    for shp in shapes:
        vmem += [pltpu.VMEM(shp, F32), pltpu.VMEM(shp, F32), pltpu.VMEM(shp, F32), pltpu.VMEM(shp, BF16)]
    return pl.pallas_call(
        body, in_specs=[ANY] * 2, out_specs=[ANY] * 4,
        out_shape=[_sds(shapes[0]), _sds(shapes[0], BF16), _sds(shapes[1]), _sds(shapes[1], BF16)],
        scratch_shapes=[pltpu.SemaphoreType.DMA((n_cp,)), pltpu.SemaphoreType.DMA((n_cp,)),
                        pltpu.SemaphoreType.DMA((n_cp,)), pltpu.SemaphoreType.DMA((2 * n_cp,))] + vmem,
        compiler_params=_cp(), name="chip_sums")(u_in, u_out)


SMALL_LAYOUT = (("norm_w", 1024), ("conv_w", 6144), ("a_log", 128), ("dt_bias", 128), ("dn_norm_w", 128),
                ("q_norm_w", 512), ("k_norm_w", 512), ("rel_bias", 256), ("loss", 128))
SMALL_TOTAL = sum(n for _, n in SMALL_LAYOUT)


def _small_offset(name):
    off = 0
    for n, size in SMALL_LAYOUT:
        if n == name:
            return off
        off += size
    raise KeyError(name)


def _pack_small(grads):
    parts = []
    for name, size in SMALL_LAYOUT:
        flat = grads[name].reshape(1, -1)
        parts.append(jnp.pad(flat, ((0, 0), (0, size - flat.shape[1]))))
    return jnp.concatenate(parts, axis=1)


def _sum_small(rows):
    n_dev = rows.shape[0]
    q_off = _small_offset("q_norm_w")
    k_off = _small_offset("k_norm_w")

    def body(r_ref, tot_ref, qk_ref):
        tot = r_ref[0:1, :]
        for d in range(1, n_dev):
            tot = tot + r_ref[d:d + 1, :]
        tot_ref[...] = tot
        for row, off in ((0, q_off), (1, k_off)):
            s4 = tot[:, off:off + 128] + tot[:, off + 128:off + 256] + tot[:, off + 256:off + 384] \
                + tot[:, off + 384:off + 512]
            qk_ref[row:row + 1, :] = s4 + pltpu.roll(s4, ATT_HD, 1)

    return pl.pallas_call(
        body, in_specs=[pl.BlockSpec(memory_space=pltpu.VMEM)],
        out_specs=[pl.BlockSpec(memory_space=pltpu.VMEM)] * 2,
        out_shape=[_sds((1, SMALL_TOTAL)), _sds((2, 128))],
        compiler_params=_cp(), name="sum_small")(rows)


def _adamw_math(w, g, m, v):
    m = ADAM_B1 * m + (1.0 - ADAM_B1) * g
    v = ADAM_B2 * v + (1.0 - ADAM_B2) * (g * g)
    m_hat = m / (1.0 - ADAM_B1 ** ADAM_STEP)
    v_hat = v / (1.0 - ADAM_B2 ** ADAM_STEP)
    delta = -ADAM_LR * (m_hat / (jnp.sqrt(v_hat) + ADAM_EPS) + ADAM_WD * w)
    return delta, m, v


def _adamw_big(g, w, m, v, name):
    rows, cols = w.shape
    tr = 128

    def body(g_ref, w_ref, m_ref, v_ref, go_ref, d_ref, nm_ref, nv_ref):
        g = g_ref[...]
        go_ref[...] = g
        d_ref[...], nm_ref[...], nv_ref[...] = _adamw_math(w_ref[...], g, m_ref[...], v_ref[...])

    blk = pl.BlockSpec((tr, cols), lambda i: (i, 0))
    return pl.pallas_call(
        body, grid=(pl.cdiv(rows, tr),), in_specs=[blk] * 4, out_specs=[blk] * 4,
        out_shape=[_sds((rows, cols))] * 4, compiler_params=_cp(("parallel",)), name=name)(g, w, m, v)


def _adamw_rows(g, w, m, v, name):
    rows, cols = w.shape
    tr = 128

    def body(g_ref, w_ref, m_ref, v_ref, go_ref, d_ref, nm_ref, nv_ref, s_g, s_d, s_m, s_v):
        g = g_ref[...]
        s_g[...] = g
        s_d[...], s_m[...], s_v[...] = _adamw_math(w_ref[...], g, m_ref[...], v_ref[...])
        for i in range(tr):
            for scr, out in ((s_g, go_ref), (s_d, d_ref), (s_m, nm_ref), (s_v, nv_ref)):
                out[i] = scr[i:i + 1, :]

    blk = pl.BlockSpec((tr, cols), lambda i: (i, 0))
    oblk = pl.BlockSpec((tr, 1, cols), lambda i: (i, 0, 0))
    return pl.pallas_call(
        body, grid=(pl.cdiv(rows, tr),), in_specs=[blk] * 4, out_specs=[oblk] * 4,
        out_shape=[_sds((rows, 1, cols))] * 4, scratch_shapes=[pltpu.VMEM((tr, cols), F32)] * 4,
        compiler_params=_cp(("parallel",)), name=name)(g, w, m, v)


def _adamw_small(w, g, m, v, name):
    def body(w_ref, g_ref, m_ref, v_ref, d_ref, nm_ref, nv_ref):
        d_ref[...], nm_ref[...], nv_ref[...] = _adamw_math(w_ref[...], g_ref[...], m_ref[...], v_ref[...])

    vm = pl.BlockSpec(memory_space=pltpu.VMEM)
    return pl.pallas_call(body, in_specs=[vm] * 4, out_specs=[vm] * 3, out_shape=[_sds(w.shape)] * 3,
                          compiler_params=_cp(), name=name)(w, g, m, v)


WEIGHTS = ("norm_w", "w_in", "conv_w", "a_log", "dt_bias", "dn_norm_w", "q_norm_w", "k_norm_w", "rel_bias", "w_out")


def kernel(x, norm_w, w_in, conv_w, a_log, dt_bias, dn_norm_w, q_norm_w, k_norm_w, rel_bias, w_out, loss_target, m_norm_w, m_w_in, m_conv_w, m_a_log, m_dt_bias, m_dn_norm_w, m_q_norm_w, m_k_norm_w, m_rel_bias, m_w_out, v_norm_w, v_w_in, v_conv_w, v_a_log, v_dt_bias, v_dn_norm_w, v_q_norm_w, v_k_norm_w, v_rel_bias, v_w_out):
    xi, yi, _ = _position()
    chip = 2 * xi + yi
    w_loc = dict(norm_w=norm_w, w_in=w_in[0].T, conv_w=conv_w[0], a_log=a_log, dt_bias=dt_bias, dn_norm_w=dn_norm_w,
                 q_norm_w=q_norm_w, k_norm_w=k_norm_w, rel_bias=rel_bias, w_out=w_out[0])
    m_loc = dict(norm_w=m_norm_w, w_in=m_w_in[0].T, conv_w=m_conv_w[0], a_log=m_a_log, dt_bias=m_dt_bias,
                 dn_norm_w=m_dn_norm_w, q_norm_w=m_q_norm_w, k_norm_w=m_k_norm_w, rel_bias=m_rel_bias,
                 w_out=m_w_out[0])
    v_loc = dict(norm_w=v_norm_w, w_in=v_w_in[0].T, conv_w=v_conv_w[0], a_log=v_a_log, dt_bias=v_dt_bias,
                 dn_norm_w=v_dn_norm_w, q_norm_w=v_q_norm_w, k_norm_w=v_k_norm_w, rel_bias=v_rel_bias,
                 w_out=v_w_out[0])

    wt_pad = jnp.pad(w_loc["w_in"].astype(BF16), ((0, SHARD_PAD - D_SHARD), (0, 0)))
    g_in, g_out, g_conv, bias = _gather_weights(wt_pad, w_loc["w_out"].astype(BF16), w_loc["conv_w"], rel_bias,
                                                _bucket_tables())
    wt_full = _pack_w(g_in)
    w_out_full = g_out.reshape(D_MODEL, D_MODEL)
    conv_full = g_conv.transpose(1, 0, 2).reshape(CONV_W, 3 * D_DN)

    loss_local, last_kernel, grads = _local_step(x[0], loss_target[0], norm_w, wt_full, conv_full, a_log, dt_bias,
                                                 dn_norm_w, q_norm_w, k_norm_w, bias, w_out_full)
    grads["loss"] = loss_local

    u_in = _unpack_w(grads["w_in_t"], SHARD_PAD)
    u_out = grads["w_out"].reshape(N_CHIPS, D_MODEL // N_CHIPS, D_MODEL)
    win_out = u_out.shape[1] // 2
    h_in, hb_in, h_out, hb_out = _chip_sums(u_in, u_out, WIN, win_out)
    grad_x, d_nw8, grads["rel_bias"], r_in, r_out = last_kernel(chip_sums=(hb_in, hb_out))
    grads["norm_w"] = d_nw8[0:1, :]
    small = _pack_small(grads)
    full_in, full_out, r_small = _fill_parts(h_in, r_in, h_out, r_out, small, SHARD_PAD, u_out.shape[1])
    tot_small, qk = _sum_small(r_small.reshape(8, SMALL_TOTAL))

    def small_grad(name, n):
        off = _small_offset(name)
        return tot_small[:, off:off + n]

    loss = small_grad("loss", 1).reshape(())
    conv_all = small_grad("conv_w", CONV_W * 3 * D_DN).reshape(CONV_W, 3 * D_DN)
    g_small = dict(
        norm_w=small_grad("norm_w", D_MODEL),
        conv_w=lax.dynamic_slice_in_dim(conv_all, chip * (3 * D_DN // N_CHIPS), 3 * D_DN // N_CHIPS, axis=1),
        a_log=small_grad("a_log", DN_HEADS),
        dt_bias=small_grad("dt_bias", DN_HEADS),
        dn_norm_w=small_grad("dn_norm_w", DN_HD),
        q_norm_w=qk[0:1, 0:ATT_HD],
        k_norm_w=qk[1:2, 0:ATT_HD],
        rel_bias=small_grad("rel_bias", ATT_HEADS * N_BUCKETS).reshape(ATT_HEADS, N_BUCKETS),
    )

    out_g, out_d, out_m, out_v = {}, {}, {}, {}
    out_g["w_in"], out_d["w_in"], out_m["w_in"], out_v["w_in"] = _adamw_rows(
        full_in, w_loc["w_in"], m_loc["w_in"], v_loc["w_in"], "adamw_w_in")
    out_g["w_out"], out_d["w_out"], out_m["w_out"], out_v["w_out"] = _adamw_big(
        full_out, w_loc["w_out"], m_loc["w_out"], v_loc["w_out"], "adamw_w_out")
    for name in g_small:
        out_g[name] = g_small[name]
        out_d[name], out_m[name], out_v[name] = _adamw_small(w_loc[name], g_small[name], m_loc[name], v_loc[name],
                                                             "adamw_" + name)
    for d in (out_g, out_d, out_m, out_v):
        d["w_in"] = d["w_in"].transpose(1, 2, 0)
        for name in ("conv_w", "w_out"):
            d[name] = d[name][None]
    return (loss, grad_x[None], *[out_g[n] for n in WEIGHTS], *[out_d[n] for n in WEIGHTS],
            *[out_m[n] for n in WEIGHTS], *[out_v[n] for n in WEIGHTS])
```

```python
import functools
import math

import numpy as np
import jax
import jax.numpy as jnp
from jax import lax
from jax.experimental import pallas as pl
from jax.experimental.pallas import tpu as pltpu

F32 = jnp.float32
BF16 = jnp.bfloat16
HI = lax.Precision.HIGHEST

D_MODEL = 1024
D_DN = 512
DN_HEADS = 4
DN_HD = 128
CONV_W = 4
CHUNK = 64
D_ATT = 512
ATT_HEADS = 8
ATT_HD = 64
PATTERNS = ((128, 1), (512, 4), (2048, 16))
N_BUCKETS = 32
MAX_DISTANCE = 2048
D_IN = 4 * D_DN + 2 * DN_HEADS + 4 * D_ATT
D_IN_PAD = 4224
EPS = 1e-6
BLK = 128
NEG = -1e30
N_CHIPS = 4

ADAM_LR = 0.001
ADAM_B1 = 0.9
ADAM_B2 = 0.999
ADAM_EPS = 1e-08
ADAM_WD = 0.01
ADAM_STEP = 10

VMEM_LIMIT = 56 * 1024 * 1024

COL_Z = 3
COL_ATT_Q = 4
COL_ATT_K = 5
COL_ATT_V = 6
COL_GATE = 7
COL_BA_128 = 32


def _cp(sem=None):
    if sem is None:
        return pltpu.CompilerParams(vmem_limit_bytes=VMEM_LIMIT)
    return pltpu.CompilerParams(dimension_semantics=sem, vmem_limit_bytes=VMEM_LIMIT)


def _sds(shape, dtype=F32):
    return jax.ShapeDtypeStruct(shape, dtype)


def _mm(a, b):
    return jnp.dot(a.astype(BF16), b.astype(BF16), preferred_element_type=F32)


def _mm_nt(a, b):
    return lax.dot_general(a.astype(BF16), b.astype(BF16), (((1,), (1,)), ((), ())),
                           preferred_element_type=F32)


def _mm_tn(a, b):
    return lax.dot_general(a.astype(BF16), b.astype(BF16), (((0,), (0,)), ((), ())),
                           preferred_element_type=F32)


def _mmx(a, b):
    return jnp.dot(a, b, precision=HI, preferred_element_type=F32)


def _mmx_nt(a, b):
    return lax.dot_general(a, b, (((1,), (1,)), ((), ())), precision=HI, preferred_element_type=F32)


def _mmx_tn(a, b):
    return lax.dot_general(a, b, (((0,), (0,)), ((), ())), precision=HI, preferred_element_type=F32)


def _dot(a, b):
    return jnp.dot(a, b, preferred_element_type=F32)


def _dot_nt(a, b):
    return lax.dot_general(a, b, (((1,), (1,)), ((), ())), preferred_element_type=F32)


def _dot_tn(a, b):
    return lax.dot_general(a, b, (((0,), (0,)), ((), ())), preferred_element_type=F32)


def _split(a):
    hi = a.astype(BF16)
    return hi, (a - hi.astype(F32)).astype(BF16)


def _mm3(a_s, b_s):
    return _dot(a_s[0], b_s[0]) + _dot(a_s[0], b_s[1]) + _dot(a_s[1], b_s[0])


def _mm3_tn(a_s, b_s):
    return _dot_tn(a_s[0], b_s[0]) + _dot_tn(a_s[0], b_s[1]) + _dot_tn(a_s[1], b_s[0])


def _interleave(gens):
    live = list(gens)
    while live:
        nxt = []
        for g in live:
            try:
                next(g)
                nxt.append(g)
            except StopIteration:
                pass
        live = nxt


def _segsum(x, bd):
    hi = x.astype(BF16)
    r1 = x - hi.astype(F32)
    mid = r1.astype(BF16)
    lo = (r1 - mid.astype(F32)).astype(BF16)
    return (jnp.dot(hi, bd, preferred_element_type=F32) + jnp.dot(mid, bd, preferred_element_type=F32)
            + jnp.dot(lo, bd, preferred_element_type=F32))


def _compact_mat(seg, n=512):
    slot = 128 * seg // n
    src = np.arange(n)[:, None]
    dst = np.arange(128)[None, :]
    return jnp.asarray((src // seg == dst // slot).astype(np.float32), dtype=BF16)


def _seg_mean(x, cm, seg):
    hi, lo = _split(x)
    return (_dot(hi, cm) + _dot(lo, cm)) * (1.0 / seg)


def _seg_expand(c, cm, seg):
    hi, lo = _split(c)
    return (_dot_nt(hi, cm) + _dot_nt(lo, cm)) * (cm.shape[0] / (128.0 * seg))


def _seg_rstd(x, cm, seg):
    return _seg_expand(lax.rsqrt(_seg_mean(x * x, cm, seg) + EPS), cm, seg)


def _sigmoid(x):
    return 1.0 / (1.0 + jnp.exp(-x))


def _silu_grad(x, s):
    return s * (1.0 + x * (1.0 - s))


def _tri_incl():
    i = np.arange(CHUNK)
    return jnp.asarray((i[:, None] >= i[None, :]).astype(np.float32))


def _t5_bucket(dist):
    max_exact = N_BUCKETS // 2
    d = np.maximum(dist, 1).astype(np.float64)
    large = max_exact + (np.log(d / max_exact) / math.log(MAX_DISTANCE / max_exact)
                         * (N_BUCKETS - max_exact)).astype(np.int32)
    large = np.minimum(large, N_BUCKETS - 1)
    return np.where(dist < max_exact, dist, large).astype(np.int32)


def _bucket_tables():
    qi = np.arange(BLK)[:, None]
    kj = np.arange(2 * BLK)[None, :]
    step = qi - kj + BLK
    return jnp.asarray(np.stack([_t5_bucket(np.clip(step, 0, None) * r) for _, r in PATTERNS]))


def _in_proj(x, norm_w, wt_bf):
    s = x.shape[0]
    tm = 512

    def body(x_ref, nw_ref, w_ref, o_ref):
        xv = x_ref[...]
        rstd = lax.rsqrt(jnp.mean(xv * xv, axis=-1, keepdims=True) + EPS)
        h = (xv * rstd * nw_ref[...]).astype(BF16)
        o_ref[...] = _dot_nt(h, w_ref[...])

    return pl.pallas_call(
        body, grid=(s // tm,),
        in_specs=[pl.BlockSpec((tm, D_MODEL), lambda i: (i, 0)),
                  pl.BlockSpec((1, D_MODEL), lambda i: (0, 0)),
                  pl.BlockSpec((D_IN_PAD, D_MODEL), lambda i: (0, 0))],
        out_specs=pl.BlockSpec((tm, D_IN_PAD), lambda i: (i, 0)),
        out_shape=_sds((s, D_IN_PAD)), compiler_params=_cp(("parallel",)), name="in_proj")(x, norm_w, wt_bf)


def _conv_group(cur, halo, w_ref, c):
    rows = cur.shape[0]
    lanes = slice(128 * c, 128 * c + 128)
    xcat = jnp.concatenate([halo, cur], axis=0)
    y = cur * w_ref[CONV_W - 1:CONV_W, lanes]
    for k in range(1, CONV_W):
        sh = pltpu.roll(xcat, k, 0)[8:8 + rows]
        y = y + sh * w_ref[CONV_W - 1 - k:CONV_W - k, lanes]
    return y


def _beta_g(ba, alog_l, dtb_l):
    lane = lax.broadcasted_iota(jnp.int32, ba.shape, 1)
    sig_b = _sigmoid(ba)
    t = ba + dtb_l
    softplus = jnp.maximum(t, 0.0) + jnp.log(1.0 + jnp.exp(-jnp.abs(t)))
    nega = -jnp.exp(alog_l)
    g = nega * softplus
    out = jnp.where(lane < DN_HEADS, sig_b, jnp.where(lane < 2 * DN_HEADS, g, 0.0))
    return out, lane, sig_b, t, nega, g


def _dn_pre(proj, conv_w8, alog_l, dtb_l):
    s = proj.shape[0]
    tr = 512
    nh = tr // 8

    def body(u_ref, halo_ref, ba_ref, w_ref, al_ref, dt_ref, q_ref, k_ref, v_ref, bg_ref):
        i = pl.program_id(0)
        keep = (i > 0).astype(F32)
        for c in range(12):
            lanes = slice(128 * c, 128 * c + 128)
            y = _conv_group(u_ref[:, lanes], halo_ref[:, lanes] * keep, w_ref, c)
            sv = y * _sigmoid(y)
            if c < 8:
                rs = lax.rsqrt(jnp.sum(sv * sv, axis=1, keepdims=True) + EPS)
                n = sv * rs
                if c < 4:
                    q_ref[:, lanes] = n * (DN_HD ** -0.5)
                else:
                    k_ref[:, slice(128 * (c - 4), 128 * (c - 3))] = n
            else:
                v_ref[:, slice(128 * (c - 8), 128 * (c - 7))] = sv
        bg_ref[...] = _beta_g(ba_ref[...], al_ref[...], dt_ref[...])[0]

    return pl.pallas_call(
        body, grid=(s // tr,),
        in_specs=[pl.BlockSpec((tr, 1536), lambda i: (i, 0)),
                  pl.BlockSpec((8, 1536), lambda i: (jnp.maximum(i * nh - 1, 0), 0)),
                  pl.BlockSpec((tr, 128), lambda i: (i, COL_BA_128)),
                  pl.BlockSpec((8, 1536), lambda i: (0, 0)),
                  pl.BlockSpec((1, 128), lambda i: (0, 0)),
                  pl.BlockSpec((1, 128), lambda i: (0, 0))],
        out_specs=[pl.BlockSpec((tr, 512), lambda i: (i, 0))] * 3 + [pl.BlockSpec((tr, 128), lambda i: (i, 0))],
        out_shape=[_sds((s, 512))] * 3 + [_sds((s, 128))],
        compiler_params=_cp(("parallel",)), name="dn_pre")(proj, proj, proj, conv_w8, alog_l, dtb_l)


CPS = 8
CPS_SCAN = 8
SPLIT_ITERS = 2


def _split3(a):
    hi = a.astype(BF16)
    r1 = a - hi.astype(F32)
    mid = r1.astype(BF16)
    return hi, mid, (r1 - mid.astype(F32)).astype(BF16)


def _lane_select():
    r = np.arange(128)
    return jnp.asarray((r[None, :, None] == np.arange(8)[:, None, None]) * np.ones((1, 1, 128)), dtype=BF16)


def _lane_bcast(a3, sel):
    return _dot(a3[0], sel) + _dot(a3[1], sel) + _dot(a3[2], sel)


def _rowsum_b(z, ones_b):
    hi, lo = _split(z)
    return _dot(hi, ones_b) + _dot(lo, ones_b)


def _chunk_cumsum(bg, bgt, tri):
    return _split3(bg), _split3(_mmx(tri, bg)), _mmx_nt(bgt, tri)


def _chunk_common(bg3, gc3, gc_row, h, sel_ref):
    gcc = _lane_bcast(gc3, sel_ref[DN_HEADS + h])
    beta = _dot(bg3[0], sel_ref[h]) + _dot(bg3[1], sel_ref[h])
    gcr = gc_row[DN_HEADS + h:DN_HEADS + h + 1, :]
    ii = lax.broadcasted_iota(jnp.int32, (CHUNK, CHUNK), 0)
    jj = lax.broadcasted_iota(jnp.int32, (CHUNK, CHUNK), 1)
    incl = ii >= jj
    strict = ii > jj
    decay = jnp.exp(jnp.where(incl, gcc[:, 0:CHUNK] - gcr, NEG))
    gl = gcc[CHUNK - 1:CHUNK, :]
    return gcc, beta, incl, strict, decay, gl


def _dn_prep(qn, kn, v, bg, bgt, tri, sel):
    s = qn.shape[0]
    nc = s // CHUNK

    def body(q_ref, k_ref, v_ref, bg_ref, bgt_ref, tri_ref, sel_ref,
             u_ref, w_ref, qd_ref, kt_ref, attn_ref, t_ref, gl_ref):
        tri_v = tri_ref[...]
        ii = lax.broadcasted_iota(jnp.int32, (CHUNK, CHUNK), 0)
        jj = lax.broadcasted_iota(jnp.int32, (CHUNK, CHUNK), 1)
        eye = (ii == jj).astype(F32)

        def chain(cc, h, bg3, gc3, gc_row):
            rows = slice(CHUNK * cc, CHUNK * cc + CHUNK)
            lanes = slice(128 * h, 128 * h + 128)
            gcc, beta, incl, strict, decay, gl = _chunk_common(bg3, gc3, gc_row, h, sel_ref)
            yield
            q = q_ref[rows, lanes]
            k = k_ref[rows, lanes]
            vv = v_ref[rows, lanes]
            kb = k * beta
            egc = jnp.exp(gcc)
            a_mat = jnp.where(strict, _mm_nt(kb, k) * decay, 0.0)
            attn_ref[cc, h] = jnp.where(incl, _mm_nt(q, k) * decay, 0.0)
            qd_ref[rows, lanes] = (q * egc).astype(BF16)
            kt_ref[rows, lanes] = (k * jnp.exp(gl - gcc)).astype(BF16)
            gl_ref[cc, h] = jnp.exp(gl)
            yield
            p = -a_mat
            t = eye + p
            for it in range(5):
                if it < SPLIT_ITERS:
                    ps = _split(p)
                    p = _mm3(ps, ps)
                    yield
                    t = t + _mm3(_split(t), _split(p))
                else:
                    p = _mm(p, p)
                    yield
                    t = t + _mm(t, p)
                yield
            t_ref[cc, h] = t
            ts = _split(t)
            u_ref[rows, lanes] = _mm3(ts, _split(vv * beta))
            w_ref[rows, lanes] = _mm3(ts, _split(kb * egc)).astype(BF16)

        gens = []
        for cc in range(CPS):
            bgv = bg_ref[CHUNK * cc:CHUNK * cc + CHUNK, :]
            bg3, gc3, gc_row = _chunk_cumsum(bgv, bgt_ref[cc], tri_v)
            gens += [chain(cc, h, bg3, gc3, gc_row) for h in range(DN_HEADS)]
        _interleave(gens)

    rows_step = CPS * CHUNK
    big = pl.BlockSpec((rows_step, 512), lambda n: (n, 0))
    sq = pl.BlockSpec((CPS, DN_HEADS, CHUNK, CHUNK), lambda n: (n, 0, 0, 0))
    return pl.pallas_call(
        body, grid=(nc // CPS,),
        in_specs=[big, big, big, pl.BlockSpec((rows_step, 128), lambda n: (n, 0)),
                  pl.BlockSpec((CPS, 8, CHUNK), lambda n: (n, 0, 0)),
                  pl.BlockSpec((CHUNK, CHUNK), lambda n: (0, 0)),
                  pl.BlockSpec((8, 128, 128), lambda n: (0, 0, 0))],
        out_specs=[big, big, big, big, sq, sq, pl.BlockSpec((CPS, DN_HEADS, 1, 128), lambda n: (n, 0, 0, 0))],
        out_shape=[_sds((s, 512))] + [_sds((s, 512), BF16)] * 3 + [_sds((nc, DN_HEADS, CHUNK, CHUNK))] * 2
        + [_sds((nc, DN_HEADS, 1, 128))],
        compiler_params=_cp(("parallel",)), name="dn_prep")(qn, kn, v, bg, bgt, tri, sel)


def _dn_scan(u, w, qd, kt, attn, gl):
    s = u.shape[0]
    nc = s // CHUNK

    def body(u_ref, w_ref, qd_ref, kt_ref, attn_ref, gl_ref, o_ref, vn_ref, sp_ref, st_ref):
        n = pl.program_id(0)

        @pl.when(n == 0)
        def _():
            st_ref[...] = jnp.zeros_like(st_ref)

        def chain(cc, h):
            rows = slice(CHUNK * cc, CHUNK * cc + CHUNK)
            lanes = slice(128 * h, 128 * h + 128)
            st = st_ref[h]
            sp_ref[cc, h] = st
            stb = st.astype(BF16)
            ws = _dot(w_ref[rows, lanes].astype(BF16), stb)
            qs = _dot(qd_ref[rows, lanes].astype(BF16), stb)
            yield
            vn = u_ref[rows, lanes] - ws
            vnb = vn.astype(BF16)
            vn_ref[rows, lanes] = vnb
            o_ref[rows, lanes] = qs + _dot(attn_ref[cc, h].astype(BF16), vnb)
            st_ref[h] = st * gl_ref[cc, h] + _dot_tn(kt_ref[rows, lanes].astype(BF16), vnb)

        for cc in range(CPS_SCAN):
            _interleave([chain(cc, h) for h in range(DN_HEADS)])

    big = pl.BlockSpec((CPS_SCAN * CHUNK, 512), lambda n: (n, 0))
    return pl.pallas_call(
        body, grid=(nc // CPS_SCAN,),
        in_specs=[big, big, big, big,
                  pl.BlockSpec((CPS_SCAN, DN_HEADS, CHUNK, CHUNK), lambda n: (n, 0, 0, 0)),
                  pl.BlockSpec((CPS_SCAN, DN_HEADS, 1, 128), lambda n: (n, 0, 0, 0))],
        out_specs=[big, big, pl.BlockSpec((CPS_SCAN, DN_HEADS, DN_HD, DN_HD), lambda n: (n, 0, 0, 0))],
        out_shape=[_sds((s, 512)), _sds((s, 512), BF16), _sds((nc, DN_HEADS, DN_HD, DN_HD))],
        scratch_shapes=[pltpu.VMEM((DN_HEADS, DN_HD, DN_HD), F32)],
        compiler_params=_cp(("arbitrary",)), name="dn_scan")(u, w, qd, kt, attn, gl)


R4 = PATTERNS[1][1]
R16 = PATTERNS[2][1]
TM = 512


def _pattern_spec(r, width=512):
    return pl.BlockSpec((r, TM // r, width), lambda i: (0, i, 0))


def _pattern_shape(s, r, dtype=F32, width=512):
    return _sds((r, s // r, width), dtype)


SLABS = pltpu.VMEM((4, TM, 128), F32)

HEAD_SLOT = 128 // ATT_HEADS


def _head_expand():
    src = np.arange(128)[:, None]
    dst = np.arange(512)[None, :]
    return jnp.asarray((src == (dst // ATT_HD) * HEAD_SLOT).astype(np.float32), dtype=BF16)


def _head_compact():
    src = np.arange(512)[:, None]
    dst = np.arange(128)[None, :]
    return jnp.asarray((src // ATT_HD == dst // HEAD_SLOT).astype(np.float32), dtype=BF16)


def _to_patterns(val, dsts, scr):
    for c in range(val.shape[1] // 128):
        lanes = slice(128 * c, 128 * c + 128)
        scr[c] = val[:, lanes]
        for dst_ref, r in dsts:
            for a in range(r):
                dst_ref[a, :, lanes] = scr[c, pl.ds(a, TM // r, stride=r), :].astype(dst_ref.dtype)


def _from_pattern(src_ref, r, scr):
    n_slab = src_ref.shape[2] // 128
    for c in range(n_slab):
        for a in range(r):
            scr[c, pl.ds(a, TM // r, stride=r), :] = src_ref[a, :, 128 * c:128 * c + 128].astype(F32)
    return jnp.concatenate([scr[c] for c in range(n_slab)], axis=1) if n_slab > 1 else scr[0]


def _att_pre(proj, qw_t, kw_t, bd64):
    s = proj.shape[0]

    def body(q_ref, k_ref, v_ref, qw_ref, kw_ref, bd_ref,
             q1_ref, k1_ref, v1_ref, q4_ref, k4_ref, v4_ref, q16_ref, k16_ref, v16_ref, scr):
        bd = bd_ref[...]
        q = q_ref[...]
        k = k_ref[...]
        qn = q * _seg_rstd(q, bd, ATT_HD) * qw_ref[...] * (ATT_HD ** -0.5)
        kn = k * _seg_rstd(k, bd, ATT_HD) * kw_ref[...]
        q1_ref[...] = qn.astype(BF16)
        k1_ref[...] = kn.astype(BF16)
        v1_ref[...] = v_ref[...].astype(BF16)
        _to_patterns(qn, ((q4_ref, R4), (q16_ref, R16)), scr)
        _to_patterns(kn, ((k4_ref, R4), (k16_ref, R16)), scr)
        _to_patterns(v_ref[...], ((v4_ref, R4), (v16_ref, R16)), scr)

    row = pl.BlockSpec((1, 512), lambda i: (0, 0))
    tok = pl.BlockSpec((TM, 512), lambda i: (i, 0))
    return pl.pallas_call(
        body, grid=(s // TM,),
        in_specs=[pl.BlockSpec((TM, 512), lambda i: (i, COL_ATT_Q)),
                  pl.BlockSpec((TM, 512), lambda i: (i, COL_ATT_K)),
                  pl.BlockSpec((TM, 512), lambda i: (i, COL_ATT_V)),
                  row, row, pl.BlockSpec((512, 128), lambda i: (0, 0))],
        out_specs=[tok] * 3 + [_pattern_spec(R4)] * 3 + [_pattern_spec(R16)] * 3,
        out_shape=[_sds((s, 512), BF16)] * 3 + [_pattern_shape(s, R4, BF16)] * 3 + [_pattern_shape(s, R16, BF16)] * 3,
        scratch_shapes=[SLABS],
        compiler_params=_cp(("parallel",)), name="att_pre")(proj, proj, proj, qw_t, kw_t, bd64)


def _bias_table(rb_ref, bk_ref, o_ref):
    for p in range(len(PATTERNS)):
        bk = bk_ref[p]
        for h in range(ATT_HEADS):
            acc = jnp.zeros((BLK, 2 * BLK), F32)
            for b in range(N_BUCKETS):
                acc = jnp.where(bk == b, rb_ref[h, b], acc)
            o_ref[p, h] = acc


def _bias_grad(ds_refs, bk_ref, o_ref):
    for h in range(ATT_HEADS):
        for b in range(N_BUCKETS):
            tot = jnp.zeros((), F32)
            for p, ds_ref in enumerate(ds_refs):
                tot = tot + jnp.sum(jnp.where(bk_ref[p] == b, ds_ref[h], 0.0))
            o_ref[h, b] = tot


QB_FWD = 2
QB_BWD = 8


def _att_masks(has_prev):
    qi = lax.broadcasted_iota(jnp.int32, (BLK, BLK), 0)
    kj = lax.broadcasted_iota(jnp.int32, (BLK, BLK), 1)
    lane = lax.broadcasted_iota(jnp.int32, (BLK, 2 * ATT_HD), 1)
    return jnp.logical_and(kj >= qi, has_prev), kj <= qi, lane < ATT_HD


def _head_lanes(h):
    half = h % 2
    return slice(ATT_HD * h, ATT_HD * h + ATT_HD), slice(ATT_HD * half, ATT_HD * half + ATT_HD)


def _att_scores(qm, kp2, kc2, bias_h, mask_prev, mask_cur):
    s_prev = jnp.where(mask_prev, _dot_nt(qm, kp2) + bias_h[:, :BLK], NEG)
    s_cur = jnp.where(mask_cur, _dot_nt(qm, kc2) + bias_h[:, BLK:], NEG)
    return s_prev, s_cur


def _att_fwd(q, k, v, bias, p_idx, r, name):
    QB = QB_FWD
    s = q.shape[0]
    nblk = s // BLK
    nseq = nblk // r

    def body(q_ref, kp_ref, kc_ref, vp_ref, vc_ref, b_ref, o_ref, lse_ref):
        j = pl.program_id(0)

        def head(h, rows, masks, q2, kp2, kc2, vp2, vc2):
            mask_prev, mask_cur, lo_half = masks
            out_l, pair_l = _head_lanes(h)
            sel = lo_half if h % 2 == 0 else jnp.logical_not(lo_half)
            qm = jnp.where(sel, q2, jnp.zeros_like(q2))
            s_prev, s_cur = _att_scores(qm, kp2, kc2, b_ref[0, h], mask_prev, mask_cur)
            yield
            m = jnp.maximum(jnp.max(s_prev, axis=1, keepdims=True), jnp.max(s_cur, axis=1, keepdims=True))
            p_prev = jnp.exp(s_prev - m)
            p_cur = jnp.exp(s_cur - m)
            l = jnp.sum(p_prev, axis=1, keepdims=True) + jnp.sum(p_cur, axis=1, keepdims=True)
            yield
            o2 = _dot(p_prev.astype(BF16), vp2) + _dot(p_cur.astype(BF16), vc2)
            o_ref[rows, out_l] = (o2 * (1.0 / l))[:, pair_l].astype(BF16)
            lse_ref[rows, HEAD_SLOT * h:HEAD_SLOT * h + HEAD_SLOT] = jnp.broadcast_to(m + jnp.log(l), (BLK, HEAD_SLOT))

        for sub in range(QB):
            rows = slice(BLK * sub, BLK * sub + BLK)
            before = slice(BLK * (sub - 1), BLK * sub)
            masks = _att_masks(((QB * j + sub) % nseq) != 0)
            gens = []
            for pp in range(ATT_HEADS // 2):
                lanes = slice(128 * pp, 128 * pp + 128)
                kp = kp_ref[:, lanes] if sub == 0 else kc_ref[before, lanes]
                vp = vp_ref[:, lanes] if sub == 0 else vc_ref[before, lanes]
                slabs = (q_ref[rows, lanes], kp, kc_ref[rows, lanes], vp, vc_ref[rows, lanes])
                gens += [head(2 * pp, rows, masks, *slabs), head(2 * pp + 1, rows, masks, *slabs)]
            _interleave(gens)

    cur = pl.BlockSpec((QB * BLK, 512), lambda j: (j, 0))
    prev = pl.BlockSpec((BLK, 512), lambda j: (jnp.maximum(QB * j - 1, 0), 0))
    return pl.pallas_call(
        body, grid=(nblk // QB,),
        in_specs=[cur, prev, cur, prev, cur,
                  pl.BlockSpec((1, ATT_HEADS, BLK, 2 * BLK), lambda j: (p_idx, 0, 0, 0))],
        out_specs=[cur, pl.BlockSpec((QB * BLK, 128), lambda j: (j, 0))],
        out_shape=[_sds((s, 512), BF16), _sds((s, 128))],
        compiler_params=_cp(("parallel",)), name=name)(q, k, k, v, v, bias)


def _post_fwd(o_dn, proj, o_pats, lse_pats, dnw_t, bd128):
    s = o_dn.shape[0]

    def body(o_ref, z_ref, gate_ref, o1_ref, o4_ref, o16_ref, s1_ref, s4_ref, s16_ref, wn_ref, bd_ref, ex_ref,
             mixed_ref, oatt_ref, l1_ref, l4_ref, l16_ref, scr_a, scr_b, scr_c, scr_d):
        o = o_ref[...]
        z = z_ref[...]
        rstd = _seg_rstd(o, bd_ref[...], DN_HD)
        y_dn = o * rstd * wn_ref[...] * (z * _sigmoid(z))
        mixed_ref[:, 0:512] = y_dn.astype(BF16)
        lses = (s1_ref[...], _from_pattern(s4_ref, R4, scr_a), _from_pattern(s16_ref, R16, scr_b))
        m = jnp.maximum(jnp.maximum(lses[0], lses[1]), lses[2])
        tot = jnp.exp(lses[0] - m) + jnp.exp(lses[1] - m) + jnp.exp(lses[2] - m)
        big_l = m + jnp.log(tot)
        l1_ref[...] = big_l
        _to_patterns(big_l, ((l4_ref, R4), (l16_ref, R16)), scr_a)
        ex = ex_ref[...]
        outs = (o1_ref[...], _from_pattern(o4_ref, R4, scr_c), _from_pattern(o16_ref, R16, scr_d))
        acc = jnp.zeros_like(o)
        for lse_p, o_p in zip(lses, outs):
            acc = acc + _lane_bcast(_split3(jnp.exp(lse_p - big_l)), ex) * o_p
        gate = gate_ref[...]
        oatt_ref[...] = acc
        mixed_ref[:, 512:1024] = (acc * (gate * _sigmoid(gate))).astype(BF16)

    blk = pl.BlockSpec((TM, 512), lambda i: (i, 0))
    cblk = pl.BlockSpec((TM, 128), lambda i: (i, 0))
    p4, p16 = _pattern_spec(R4), _pattern_spec(R16)
    c4, c16 = _pattern_spec(R4, 128), _pattern_spec(R16, 128)
    return pl.pallas_call(
        body, grid=(s // TM,),
        in_specs=[blk, pl.BlockSpec((TM, 512), lambda i: (i, COL_Z)),
                  pl.BlockSpec((TM, 512), lambda i: (i, COL_GATE)), blk, p4, p16, cblk, c4, c16,
                  pl.BlockSpec((1, 512), lambda i: (0, 0)), pl.BlockSpec((512, 128), lambda i: (0, 0)),
                  pl.BlockSpec((128, 512), lambda i: (0, 0))],
        out_specs=[pl.BlockSpec((TM, D_MODEL), lambda i: (i, 0)), blk, cblk, c4, c16],
        out_shape=[_sds((s, D_MODEL), BF16), _sds((s, 512)), _sds((s, 128)), _pattern_shape(s, R4, F32, 128),
                   _pattern_shape(s, R16, F32, 128)],
        scratch_shapes=[SLABS] * 4,
        compiler_params=_cp(("parallel",)), name="post_fwd")(o_dn, proj, proj, *o_pats, *lse_pats, dnw_t, bd128,
                                                              _head_expand())


def _out_fwd(x, mixed, w_out_bf, tgt):
    s = x.shape[0]
    tm = 512

    def body(x_ref, m_ref, w_ref, t_ref, dy_ref, loss_ref):
        i = pl.program_id(0)

        @pl.when(i == 0)
        def _():
            loss_ref[...] = jnp.zeros_like(loss_ref)

        y = x_ref[...] + jnp.dot(m_ref[...], w_ref[...], preferred_element_type=F32)
        err = y - t_ref[...]
        dy_ref[...] = err * (1.0 / D_MODEL)
        part = 0.5 * jnp.sum(jnp.mean(err * err, axis=-1, keepdims=True), axis=0, keepdims=True)
        loss_ref[...] = loss_ref[...] + part

    blk = pl.BlockSpec((tm, D_MODEL), lambda i: (i, 0))
    return pl.pallas_call(
        body, grid=(s // tm,),
        in_specs=[blk, blk, pl.BlockSpec((D_MODEL, D_MODEL), lambda i: (0, 0)), blk],
        out_specs=[blk, pl.BlockSpec((8, 128), lambda i: (0, 0))],
        out_shape=[_sds((s, D_MODEL)), _sds((8, 128))],
        compiler_params=_cp(("arbitrary",)), name="out_fwd")(x, mixed, w_out_bf, tgt)


def _out_bwd(dy, mixed, w_out_bf):
    s = dy.shape[0]
    tm = 512

    def body(dy_ref, m_ref, w_ref, dm_ref, dw_ref):
        i = pl.program_id(0)

        @pl.when(i == 0)
        def _():
            dw_ref[...] = jnp.zeros_like(dw_ref)

        dyb = dy_ref[...].astype(BF16)
        dm_ref[...] = lax.dot_general(dyb, w_ref[...], (((1,), (1,)), ((), ())), preferred_element_type=F32)
        dw_ref[...] = dw_ref[...] + lax.dot_general(m_ref[...], dyb, (((0,), (0,)), ((), ())),
                                                    preferred_element_type=F32)

    blk = pl.BlockSpec((tm, D_MODEL), lambda i: (i, 0))
    full = pl.BlockSpec((D_MODEL, D_MODEL), lambda i: (0, 0))
    return pl.pallas_call(
        body, grid=(s // tm,), in_specs=[blk, blk, full], out_specs=[blk, full],
        out_shape=[_sds((s, D_MODEL)), _sds((D_MODEL, D_MODEL))],
        compiler_params=_cp(("arbitrary",)), name="out_bwd")(dy, mixed, w_out_bf)


def _post_bwd(dmixed, o_dn, proj, o_att, dnw_t, bd128):
    s = o_dn.shape[0]
    tm = TM

    def body(ddn_ref, datt_ref, o_ref, z_ref, gate_ref, oatt_ref, wn_ref, bd128_ref, hc_ref,
             do_ref, dz_ref, dgate_ref, doatt_ref, do4_ref, do16_ref, delta_ref, dl4_ref, dl16_ref, dnw_ref, scr):
        i = pl.program_id(0)

        @pl.when(i == 0)
        def _():
            dnw_ref[...] = jnp.zeros_like(dnw_ref)

        bd128v = bd128_ref[...]
        o = o_ref[...]
        z = z_ref[...]
        wn = wn_ref[...]
        dy = ddn_ref[...]
        rstd = _seg_rstd(o, bd128v, DN_HD)
        nrm = o * rstd
        sz = _sigmoid(z)
        dz_ref[...] = (dy * nrm * wn * _silu_grad(z, sz)).astype(BF16)
        dn = dy * z * sz
        gw = dn * wn
        do_ref[...] = rstd * (gw - nrm * _seg_expand(_seg_mean(gw * nrm, bd128v, DN_HD), bd128v, DN_HD))
        colsum = jnp.sum(dn * nrm, axis=0, keepdims=True)
        fold = colsum[:, 0:128] + colsum[:, 128:256] + colsum[:, 256:384] + colsum[:, 384:512]
        dnw_ref[...] = dnw_ref[...] + fold
        dya = datt_ref[...]
        gate = gate_ref[...]
        oatt = oatt_ref[...]
        sg = _sigmoid(gate)
        dgate_ref[...] = (dya * oatt * _silu_grad(gate, sg)).astype(BF16)
        doa = dya * gate * sg
        doatt_ref[...] = doa.astype(BF16)
        delta = _segsum(doa * oatt, hc_ref[...])
        delta_ref[...] = delta
        _to_patterns(doa, ((do4_ref, R4), (do16_ref, R16)), scr)
        _to_patterns(delta, ((dl4_ref, R4), (dl16_ref, R16)), scr)

    blk = pl.BlockSpec((tm, 512), lambda i: (i, 0))
    cblk = pl.BlockSpec((tm, 128), lambda i: (i, 0))
    p4, p16 = _pattern_spec(R4), _pattern_spec(R16)
    c4, c16 = _pattern_spec(R4, 128), _pattern_spec(R16, 128)
    return pl.pallas_call(
        body, grid=(s // tm,),
        in_specs=[blk, pl.BlockSpec((tm, 512), lambda i: (i, 1)), blk,
                  pl.BlockSpec((tm, 512), lambda i: (i, COL_Z)), pl.BlockSpec((tm, 512), lambda i: (i, COL_GATE)),
                  blk, pl.BlockSpec((1, 512), lambda i: (0, 0)), pl.BlockSpec((512, 128), lambda i: (0, 0)),
                  pl.BlockSpec((512, 128), lambda i: (0, 0))],
        out_specs=[blk, blk, blk, blk, p4, p16, cblk, c4, c16, pl.BlockSpec((8, 128), lambda i: (0, 0))],
        out_shape=[_sds((s, 512))] + [_sds((s, 512), BF16)] * 3 + [_pattern_shape(s, R4, BF16),
                                          _pattern_shape(s, R16, BF16), _sds((s, 128)),
                                          _pattern_shape(s, R4, F32, 128), _pattern_shape(s, R16, F32, 128),
                                          _sds((8, 128))],
        scratch_shapes=[SLABS],
        compiler_params=_cp(("arbitrary",)), name="post_bwd")(dmixed, dmixed, o_dn, proj, proj, o_att, dnw_t,
                                                               bd128, _head_compact())


def _att_bwd(q, k, v, do, big_l, delta, bias, p_idx, r, name):
    QB = QB_BWD
    s = q.shape[0]
    nblk = s // BLK
    nseq = nblk // r
    nstep = nblk // QB

    def body(q_ref, kp_ref, kc_ref, vp_ref, vc_ref, do_ref, l_ref, dl_ref, b_ref,
             dq_ref, dk_ref, dv_ref, ds_ref, dkc_ref, dvc_ref):
        j = pl.program_id(0)

        @pl.when(j == 0)
        def _():
            dkc_ref[...] = jnp.zeros_like(dkc_ref)
            dvc_ref[...] = jnp.zeros_like(dvc_ref)
            ds_ref[...] = jnp.zeros_like(ds_ref)

        @pl.when(j < nstep)
        def _():
            def head(h, sub, masks, q2, do2, kp2, kc2, vp2, vc2):
                mask_prev, mask_cur, lo_half = masks
                rows = slice(BLK * sub, BLK * sub + BLK)
                out_l, pair_l = _head_lanes(h)
                sel = lo_half if h % 2 == 0 else jnp.logical_not(lo_half)
                qm = jnp.where(sel, q2, jnp.zeros_like(q2))
                dom = jnp.where(sel, do2, jnp.zeros_like(do2))
                s_prev, s_cur = _att_scores(qm, kp2, kc2, b_ref[0, h], mask_prev, mask_cur)
                dp_prev = _dot_nt(dom, vp2)
                dp_cur = _dot_nt(dom, vc2)
                yield
                lh = l_ref[rows, HEAD_SLOT * h:HEAD_SLOT * h + 1]
                dh = dl_ref[rows, HEAD_SLOT * h:HEAD_SLOT * h + 1]
                p_prev = jnp.exp(s_prev - lh)
                p_cur = jnp.exp(s_cur - lh)
                ds_prev = p_prev * (dp_prev - dh)
                ds_cur = p_cur * (dp_cur - dh)
                ds_ref[h, :, 0:BLK] = ds_ref[h, :, 0:BLK] + ds_prev
                ds_ref[h, :, BLK:2 * BLK] = ds_ref[h, :, BLK:2 * BLK] + ds_cur
                dsb_prev, dsb_cur = ds_prev.astype(BF16), ds_cur.astype(BF16)
                pb_prev, pb_cur = p_prev.astype(BF16), p_cur.astype(BF16)
                yield
                dq_ref[rows, out_l] = (_dot(dsb_prev, kp2) + _dot(dsb_cur, kc2))[:, pair_l].astype(BF16)
                dk_prev = _dot_tn(dsb_prev, q2)[:, pair_l]
                dv_prev = _dot_tn(pb_prev, do2)[:, pair_l]
                if sub == 0:
                    last = slice(BLK * (QB - 1), BLK * QB)
                    dk_ref[last, out_l] = (dkc_ref[last, out_l] + dk_prev).astype(BF16)
                    dv_ref[last, out_l] = (dvc_ref[last, out_l] + dv_prev).astype(BF16)
                else:
                    before = slice(BLK * (sub - 1), BLK * sub)
                    dkc_ref[before, out_l] = dkc_ref[before, out_l] + dk_prev
                    dvc_ref[before, out_l] = dvc_ref[before, out_l] + dv_prev
                yield
                dkc_ref[rows, out_l] = _dot_tn(dsb_cur, q2)[:, pair_l]
                dvc_ref[rows, out_l] = _dot_tn(pb_cur, do2)[:, pair_l]

            done = slice(0, BLK * (QB - 1))
            dk_ref[done, :] = dkc_ref[done, :].astype(BF16)
            dv_ref[done, :] = dvc_ref[done, :].astype(BF16)
            for sub in range(QB):
                rows = slice(BLK * sub, BLK * sub + BLK)
                before = slice(BLK * (sub - 1), BLK * sub)
                masks = _att_masks(((QB * j + sub) % nseq) != 0)
                gens = []
                for pp in range(ATT_HEADS // 2):
                    lanes = slice(128 * pp, 128 * pp + 128)
                    kp = kp_ref[:, lanes] if sub == 0 else kc_ref[before, lanes]
                    vp = vp_ref[:, lanes] if sub == 0 else vc_ref[before, lanes]
                    slabs = (q_ref[rows, lanes], do_ref[rows, lanes], kp, kc_ref[rows, lanes], vp,
                             vc_ref[rows, lanes])
                    gens += [head(2 * pp, sub, masks, *slabs), head(2 * pp + 1, sub, masks, *slabs)]
                _interleave(gens)

        @pl.when(j == nstep)
        def _():
            dk_ref[...] = dkc_ref[...].astype(BF16)
            dv_ref[...] = dvc_ref[...].astype(BF16)

    last_step = nstep - 1
    cur = pl.BlockSpec((QB * BLK, 512), lambda j: (jnp.minimum(j, last_step), 0))
    compact = pl.BlockSpec((QB * BLK, 128), lambda j: (jnp.minimum(j, last_step), 0))
    lag = pl.BlockSpec((QB * BLK, 512), lambda j: (jnp.clip(j - 1, 0, last_step), 0))
    prev = pl.BlockSpec((BLK, 512), lambda j: (jnp.clip(QB * j - 1, 0, nblk - 1), 0))
    return pl.pallas_call(
        body, grid=(nstep + 1,),
        in_specs=[cur, prev, cur, prev, cur, cur, compact, compact,
                  pl.BlockSpec((1, ATT_HEADS, BLK, 2 * BLK), lambda j: (p_idx, 0, 0, 0))],
        out_specs=[cur, lag, lag, pl.BlockSpec((ATT_HEADS, BLK, 2 * BLK), lambda j: (0, 0, 0))],
        out_shape=[_sds((s, 512), BF16)] * 3 + [_sds((ATT_HEADS, BLK, 2 * BLK))],
        scratch_shapes=[pltpu.VMEM((QB * BLK, 512), F32), pltpu.VMEM((QB * BLK, 512), F32)],
        compiler_params=_cp(("arbitrary",)), name=name)(q, k, k, v, v, do, big_l, delta, bias)


def _att_pre_bwd(dq_pats, dk_pats, dv_pats, proj, qw_t, kw_t, bd64):
    s = proj.shape[0]
    tm = TM

    def body(dq1_ref, dq4_ref, dq16_ref, dk1_ref, dk4_ref, dk16_ref, dv1_ref, dv4_ref, dv16_ref,
             q_ref, k_ref, qw_ref, kw_ref, bd_ref,
             dqr_ref, dkr_ref, dvr_ref, dqw_ref, dkw_ref, scr4, scr16):
        i = pl.program_id(0)

        @pl.when(i == 0)
        def _():
            dqw_ref[...] = jnp.zeros_like(dqw_ref)
            dkw_ref[...] = jnp.zeros_like(dkw_ref)

        bd = bd_ref[...]

        def total(d1_ref, d4_ref, d16_ref):
            return d1_ref[...] + _from_pattern(d4_ref, R4, scr4) + _from_pattern(d16_ref, R16, scr16)

        def one(d_refs, x_ref, w_ref, scale, dx_ref, dw_ref):
            dy = total(*d_refs) * scale
            x = x_ref[...]
            rstd = _seg_rstd(x, bd, ATT_HD)
            nrm = x * rstd
            dw_ref[...] = dw_ref[...] + jnp.sum(dy * nrm, axis=0, keepdims=True)
            g = dy * w_ref[...]
            dx_ref[...] = (rstd * (g - nrm * _seg_expand(_seg_mean(g * nrm, bd, ATT_HD), bd, ATT_HD))).astype(BF16)

        one((dq1_ref, dq4_ref, dq16_ref), q_ref, qw_ref, ATT_HD ** -0.5, dqr_ref, dqw_ref)
        one((dk1_ref, dk4_ref, dk16_ref), k_ref, kw_ref, 1.0, dkr_ref, dkw_ref)
        dvr_ref[...] = total(dv1_ref, dv4_ref, dv16_ref).astype(BF16)

    blk = pl.BlockSpec((tm, 512), lambda i: (i, 0))
    pats = [blk, _pattern_spec(R4), _pattern_spec(R16)]
    row = pl.BlockSpec((1, 512), lambda i: (0, 0))
    acc = pl.BlockSpec((8, 512), lambda i: (0, 0))
    return pl.pallas_call(
        body, grid=(s // tm,),
        in_specs=pats * 3 + [pl.BlockSpec((tm, 512), lambda i: (i, COL_ATT_Q)),
                             pl.BlockSpec((tm, 512), lambda i: (i, COL_ATT_K)), row, row,
                             pl.BlockSpec((512, 128), lambda i: (0, 0))],
        out_specs=[blk, blk, blk, acc, acc],
        out_shape=[_sds((s, 512), BF16)] * 3 + [_sds((8, 512))] * 2,
        scratch_shapes=[SLABS] * 2,
        compiler_params=_cp(("arbitrary",)), name="att_pre_bwd")(*dq_pats, *dk_pats, *dv_pats, proj, proj,
                                                                  qw_t, kw_t, bd64)


def _dn_scan_bwd(do, sp, qd, kt, w, vn, attn, gl):
    s = do.shape[0]
    nc = s // CHUNK

    def body(do_ref, sp_ref, qd_ref, kt_ref, w_ref, vn_ref, attn_ref, gl_ref,
             du_ref, dqd_ref, dkt_ref, dw_ref, dattn_ref, dgl_ref, ds_ref):
        n = pl.program_id(0)

        @pl.when(n == 0)
        def _():
            ds_ref[...] = jnp.zeros_like(ds_ref)

        def chain(cc, h):
            rows = slice(CHUNK * cc, CHUNK * cc + CHUNK)
            lanes = slice(128 * h, 128 * h + 128)
            dsn = ds_ref[h]
            st = sp_ref[cc, h]
            dsb, stb = dsn.astype(BF16), st.astype(BF16)
            dob = do_ref[rows, lanes].astype(BF16)
            vnb = vn_ref[rows, lanes].astype(BF16)
            dvn = _dot_tn(attn_ref[cc, h].astype(BF16), dob) + _dot(kt_ref[rows, lanes].astype(BF16), dsb)
            du_ref[rows, lanes] = dvn
            dqd_ref[rows, lanes] = _dot_nt(dob, stb)
            dattn_ref[cc, h] = _dot_nt(dob, vnb)
            dkt_ref[rows, lanes] = _dot_nt(vnb, dsb)
            tot = jnp.sum(jnp.sum(st * dsn, axis=1, keepdims=True), axis=0, keepdims=True)
            dgl_ref[cc, h] = jnp.broadcast_to(tot, (1, 128))
            qdo = _dot_tn(qd_ref[rows, lanes].astype(BF16), dob)
            yield
            dvb = dvn.astype(BF16)
            dw_ref[rows, lanes] = -_dot_nt(dvb, stb)
            ds_ref[h] = qdo + dsn * gl_ref[cc, h] - _dot_tn(w_ref[rows, lanes].astype(BF16), dvb)

        for cc in reversed(range(CPS_SCAN)):
            _interleave([chain(cc, h) for h in range(DN_HEADS)])

    nsteps = nc // CPS_SCAN
    big = pl.BlockSpec((CPS_SCAN * CHUNK, 512), lambda n: (nsteps - 1 - n, 0))
    sq = pl.BlockSpec((CPS_SCAN, DN_HEADS, CHUNK, CHUNK), lambda n: (nsteps - 1 - n, 0, 0, 0))
    glb = pl.BlockSpec((CPS_SCAN, DN_HEADS, 1, 128), lambda n: (nsteps - 1 - n, 0, 0, 0))
    return pl.pallas_call(
        body, grid=(nsteps,),
        in_specs=[big, pl.BlockSpec((CPS_SCAN, DN_HEADS, DN_HD, DN_HD), lambda n: (nsteps - 1 - n, 0, 0, 0)),
                  big, big, big, big, sq, glb],
        out_specs=[big, big, big, big, sq, glb],
        out_shape=[_sds((s, 512))] * 4 + [_sds((nc, DN_HEADS, CHUNK, CHUNK)), _sds((nc, DN_HEADS, 1, 128))],
        scratch_shapes=[pltpu.VMEM((DN_HEADS, DN_HD, DN_HD), F32)],
        compiler_params=_cp(("arbitrary",)), name="dn_scan_bwd")(do, sp, qd, kt, w, vn, attn, gl)


def _dn_prep_bwd(qn, kn, v, bg, bgt, tri, sel, t_inv, attn, u, w, du, dw, dqd, dkt, dattn, dgl):
    s = qn.shape[0]
    nc = s // CHUNK

    def body(q_ref, k_ref, v_ref, bg_ref, bgt_ref, tri_ref, sel_ref, t_ref, attn_ref, u_ref, w_ref,
             du_ref, dw_ref, dqd_ref, dkt_ref, dattn_ref, dgl_ref,
             dq_ref, dk_ref, dv_ref, dbg_ref):
        tri_v = tri_ref[...]
        lane = lax.broadcasted_iota(jnp.int32, (CHUNK, 128), 1)
        rowi = lax.broadcasted_iota(jnp.int32, (CHUNK, 128), 0)
        ones_b = jnp.ones((CHUNK, 128), BF16)
        ones_sq = jnp.ones((128, 128), BF16)
        parts = [[] for _ in range(CPS)]

        def chain(cc, h, bg3, gc3, gc_row):
            rows = slice(CHUNK * cc, CHUNK * cc + CHUNK)
            lanes = slice(128 * h, 128 * h + 128)
            gcc, beta, incl, strict, decay, gl = _chunk_common(bg3, gc3, gc_row, h, sel_ref)
            yield
            q = q_ref[rows, lanes]
            k = k_ref[rows, lanes]
            vv = v_ref[rows, lanes]
            ts = _split(t_ref[cc, h])
            egc = jnp.exp(gcc)
            kb = k * beta
            a_mat = jnp.where(strict, _mm_nt(kb, k) * decay, 0.0)
            dvb = _mm3_tn(ts, _split(du_ref[rows, lanes]))
            dkbg = _mm3_tn(ts, _split(dw_ref[rows, lanes]))
            yield
            d_a = jnp.where(strict, -(_mm_nt(dvb, u_ref[rows, lanes]) + _mm_nt(dkbg, w_ref[rows, lanes])), 0.0)
            d_m = d_a * decay
            dattn_m = jnp.where(incl, dattn_ref[cc, h], 0.0)
            dqk = dattn_m * decay
            e_hi, e_lo = _split(d_a * a_mat + dattn_m * attn_ref[cc, h])
            yield
            dkb = _mm(d_m, k)
            dk = _mm_tn(d_m, kb) + _mm_tn(dqk, q)
            dq = _mm(dqk, k)
            e_colsum = _dot_tn(e_hi, ones_b) + _dot_tn(e_lo, ones_b)
            e_rowsum = _dot(e_hi, ones_b) + _dot(e_lo, ones_b)
            dqd = dqd_ref[rows, lanes]
            dkt = dkt_ref[rows, lanes]
            sums = _rowsum_b(jnp.concatenate([dqd * q, dkt * k, dkbg * k, dkb * k, dvb * vv], axis=0), ones_sq)
            s_dqd, s_dkt, rk, s_dkb, s_dvb = (sums[CHUNK * n:CHUNK * n + CHUNK] for n in range(5))
            yield
            tail = jnp.exp(gl - gcc)
            r = s_dkt * tail
            dgl_tot = jnp.sum(r, axis=0, keepdims=True) + dgl_ref[cc, h] * jnp.exp(gl)
            dgc = e_rowsum - e_colsum + s_dqd * egc - r + rk * beta * egc
            dgc = dgc + jnp.where(rowi == CHUNK - 1, dgl_tot, 0.0)
            dq_ref[rows, lanes] = dq + dqd * egc
            dk_ref[rows, lanes] = dk + dkt * tail + dkbg * (beta * egc) + dkb * beta
            dv_ref[rows, lanes] = dvb * beta
            parts[cc].append((h, dgc, rk * egc + s_dkb + s_dvb))

        gens = []
        for cc in range(CPS):
            bgv = bg_ref[CHUNK * cc:CHUNK * cc + CHUNK, :]
            bg3, gc3, gc_row = _chunk_cumsum(bgv, bgt_ref[cc], tri_v)
            gens += [chain(cc, h, bg3, gc3, gc_row) for h in range(DN_HEADS)]
        _interleave(gens)
        for cc in range(CPS):
            dgc_mat = jnp.zeros((CHUNK, 128), F32)
            dbeta_mat = jnp.zeros((CHUNK, 128), F32)
            for h, dgc, dbeta in parts[cc]:
                dgc_mat = dgc_mat + jnp.where(lane == DN_HEADS + h, dgc, 0.0)
                dbeta_mat = dbeta_mat + jnp.where(lane == h, dbeta, 0.0)
            dbg_ref[CHUNK * cc:CHUNK * cc + CHUNK, :] = _mmx_tn(tri_v, dgc_mat) + dbeta_mat

    big = pl.BlockSpec((CPS * CHUNK, 512), lambda n: (n, 0))
    sq = pl.BlockSpec((CPS, DN_HEADS, CHUNK, CHUNK), lambda n: (n, 0, 0, 0))
    glb = pl.BlockSpec((CPS, DN_HEADS, 1, 128), lambda n: (n, 0, 0, 0))
    small = pl.BlockSpec((CPS * CHUNK, 128), lambda n: (n, 0))
    return pl.pallas_call(
        body, grid=(nc // CPS,),
        in_specs=[big, big, big, small, pl.BlockSpec((CPS, 8, CHUNK), lambda n: (n, 0, 0)),
                  pl.BlockSpec((CHUNK, CHUNK), lambda n: (0, 0)),
                  pl.BlockSpec((8, 128, 128), lambda n: (0, 0, 0)), sq, sq, big, big,
                  big, big, big, big, sq, glb],
        out_specs=[big, big, big, small],
        out_shape=[_sds((s, 512))] * 3 + [_sds((s, 128))],
        compiler_params=_cp(("parallel",)), name="dn_prep_bwd")(qn, kn, v, bg, bgt, tri, sel, t_inv, attn, u, w,
                                                                  du, dw, dqd, dkt, dattn, dgl)


def _dn_pre_bwd(dqn, dkn, dv, dbg, proj, conv_w8, alog_l, dtb_l):
    s = proj.shape[0]
    tr = 512
    nh = tr // 8

    def body(dq_ref, dk_ref, dv_ref, dbg_ref, u_ref, halo_ref, ba_ref, w_ref, al_ref, dt_ref,
             dy_ref, dba_ref, dsm_ref):
        i = pl.program_id(0)

        @pl.when(i == 0)
        def _():
            dsm_ref[...] = jnp.zeros_like(dsm_ref)

        keep = (i > 0).astype(F32)
        for c in range(12):
            lanes = slice(128 * c, 128 * c + 128)
            y = _conv_group(u_ref[:, lanes], halo_ref[:, lanes] * keep, w_ref, c)
            sg = _sigmoid(y)
            sv = y * sg
            if c < 8:
                rs = lax.rsqrt(jnp.sum(sv * sv, axis=1, keepdims=True) + EPS)
                n = sv * rs
                if c < 4:
                    dn = dq_ref[:, lanes] * (DN_HD ** -0.5)
                else:
                    dn = dk_ref[:, slice(128 * (c - 4), 128 * (c - 3))]
                dsv = rs * (dn - n * jnp.sum(dn * n, axis=1, keepdims=True))
            else:
                dsv = dv_ref[:, slice(128 * (c - 8), 128 * (c - 7))]
            dy_ref[:, lanes] = dsv * _silu_grad(y, sg)
        _, lane, sig_b, t, nega, g = _beta_g(ba_ref[...], al_ref[...], dt_ref[...])
        dbg = dbg_ref[...]
        da = dbg * nega * _sigmoid(t)
        is_b = lane < DN_HEADS
        is_a = jnp.logical_and(lane >= DN_HEADS, lane < 2 * DN_HEADS)
        dba_ref[...] = jnp.where(is_b, dbg * sig_b * (1.0 - sig_b), jnp.where(is_a, da, 0.0)).astype(BF16)
        d_alog = jnp.sum(jnp.where(is_a, dbg * g, 0.0), axis=0, keepdims=True)
        d_dtb = jnp.sum(jnp.where(is_a, da, 0.0), axis=0, keepdims=True)
        row = lax.broadcasted_iota(jnp.int32, (8, 128), 0)
        dsm_ref[...] = dsm_ref[...] + jnp.where(row == 0, d_alog, jnp.where(row == 1, d_dtb, 0.0))

    blk = pl.BlockSpec((tr, 512), lambda i: (i, 0))
    return pl.pallas_call(
        body, grid=(s // tr,),
        in_specs=[blk, blk, blk, pl.BlockSpec((tr, 128), lambda i: (i, 0)),
                  pl.BlockSpec((tr, 1536), lambda i: (i, 0)),
                  pl.BlockSpec((8, 1536), lambda i: (jnp.maximum(i * nh - 1, 0), 0)),
                  pl.BlockSpec((tr, 128), lambda i: (i, COL_BA_128)),
                  pl.BlockSpec((8, 1536), lambda i: (0, 0)),
                  pl.BlockSpec((1, 128), lambda i: (0, 0)), pl.BlockSpec((1, 128), lambda i: (0, 0))],
        out_specs=[pl.BlockSpec((tr, 1536), lambda i: (i, 0)), pl.BlockSpec((tr, 128), lambda i: (i, 0)),
                   pl.BlockSpec((8, 128), lambda i: (0, 0))],
        out_shape=[_sds((s, 1536)), _sds((s, 128), BF16), _sds((8, 128))],
        compiler_params=_cp(("arbitrary",)), name="dn_pre_bwd")(dqn, dkn, dv, dbg, proj, proj, proj, conv_w8,
                                                                 alog_l, dtb_l)


def _conv_bwd(dy, proj, conv_w8):
    s = dy.shape[0]
    tr = 512
    nh = tr // 8
    nblk = s // tr

    def body(dy_ref, dyn_ref, u_ref, halo_ref, w_ref, du_ref, dw_ref):
        i = pl.program_id(0)

        @pl.when(i == 0)
        def _():
            dw_ref[...] = jnp.zeros_like(dw_ref)

        keep_prev = (i > 0).astype(F32)
        keep_next = (i < nblk - 1).astype(F32)
        row = lax.broadcasted_iota(jnp.int32, (8, 128), 0)
        for c in range(12):
            lanes = slice(128 * c, 128 * c + 128)
            dyc = dy_ref[:, lanes]
            dcat = jnp.concatenate([dyc, dyn_ref[:, lanes] * keep_next], axis=0)
            xcat = jnp.concatenate([halo_ref[:, lanes] * keep_prev, u_ref[:, lanes]], axis=0)
            du = dyc * w_ref[CONV_W - 1:CONV_W, lanes]
            dwc = jnp.where(row == CONV_W - 1, jnp.sum(dyc * u_ref[:, lanes], axis=0, keepdims=True), 0.0)
            for k in range(1, CONV_W):
                du = du + pltpu.roll(dcat, tr + 8 - k, 0)[0:tr] * w_ref[CONV_W - 1 - k:CONV_W - k, lanes]
                ush = pltpu.roll(xcat, k, 0)[8:8 + tr]
                dwc = dwc + jnp.where(row == CONV_W - 1 - k, jnp.sum(dyc * ush, axis=0, keepdims=True), 0.0)
            du_ref[:, lanes] = du.astype(BF16)
            dw_ref[:, lanes] = dw_ref[:, lanes] + dwc

    return pl.pallas_call(
        body, grid=(nblk,),
        in_specs=[pl.BlockSpec((tr, 1536), lambda i: (i, 0)),
                  pl.BlockSpec((8, 1536), lambda i: (jnp.minimum((i + 1) * nh, s // 8 - 1), 0)),
                  pl.BlockSpec((tr, 1536), lambda i: (i, 0)),
                  pl.BlockSpec((8, 1536), lambda i: (jnp.maximum(i * nh - 1, 0), 0)),
                  pl.BlockSpec((8, 1536), lambda i: (0, 0))],
        out_specs=[pl.BlockSpec((tr, 1536), lambda i: (i, 0)), pl.BlockSpec((8, 1536), lambda i: (0, 0))],
        out_shape=[_sds((s, 1536), BF16), _sds((8, 1536))],
        compiler_params=_cp(("arbitrary",)), name="conv_bwd")(dy, dy, proj, proj, conv_w8)


PIECE_WIDTHS = (1536, 512, 512, 512, 512, 512, 128)


def _in_bwd_dx(pieces, w_bf, x, dy, norm_w, ds_accs, buckets, chip_sums=None):
    s = x.shape[0]
    tm = 512
    n_piece = len(PIECE_WIDTHS)
    n_step = s // tm
    n_arr = 0 if chip_sums is None else len(chip_sums)
    n_sent = 3 * n_arr

    def body(*refs):
        refs = list(refs)

        def take(n):
            return [refs.pop(0) for _ in range(n)]

        piece_refs = take(n_piece)
        w_ref, x_ref, dy_ref, nw_ref = take(4)
        ds_refs = take(len(ds_accs))
        bk_ref, = take(1)
        hb_refs = take(n_arr)
        dx_ref, dnw_ref, drb_ref = take(3)
        recv_refs = take(n_arr)
        i = pl.program_id(0)

        def copies():
            send_sems, recv_sems = refs
            xi, yi, ci = _position()
            out = []
            for j, (px, py) in enumerate(_other_chips(xi, yi)):
                for t, (src, dst) in enumerate(zip(hb_refs, recv_refs)):
                    k = n_arr * j + t
                    out.append(pltpu.make_async_remote_copy(
                        src_ref=src.at[2 * px + py], dst_ref=dst.at[j], send_sem=send_sems.at[k],
                        recv_sem=recv_sems.at[k], device_id=(px, py, ci), device_id_type=MESH_ID))
            return out

        @pl.when(i == 0)
        def _():
            dnw_ref[...] = jnp.zeros_like(dnw_ref)
            for cp in (copies() if n_sent else []):
                cp.start()

        @pl.when(i == 1)
        def _():
            _bias_grad(ds_refs, bk_ref, drb_ref)

        dp = jnp.concatenate([r[...] for r in piece_refs], axis=1)
        dh = _dot(dp, w_ref[...])
        xv = x_ref[...]
        rstd = lax.rsqrt(jnp.mean(xv * xv, axis=-1, keepdims=True) + EPS)
        xh = xv * rstd
        dnw_ref[...] = dnw_ref[...] + jnp.sum(dh * xh, axis=0, keepdims=True)
        g = dh * nw_ref[...]
        dx_ref[...] = rstd * (g - xh * jnp.mean(g * xh, axis=-1, keepdims=True)) + dy_ref[...]

        if n_sent:
            @pl.when(i == n_step - 1)
            def _():
                cps = copies()
                for cp in cps:
                    cp.wait_recv()
                for cp in cps:
                    cp.wait_send()

    def blk(n):
        return pl.BlockSpec((tm, n), lambda i: (i, 0))

    vm = pl.BlockSpec(memory_space=pltpu.VMEM)
    in_specs = [blk(n) for n in PIECE_WIDTHS] + [pl.BlockSpec((D_IN_PAD, D_MODEL), lambda i: (0, 0)), blk(D_MODEL),
                                                blk(D_MODEL), pl.BlockSpec((1, D_MODEL), lambda i: (0, 0))]
    in_specs += [vm] * (len(ds_accs) + 1)
    out_specs = [blk(D_MODEL), pl.BlockSpec((8, D_MODEL), lambda i: (0, 0)), pl.BlockSpec(memory_space=pltpu.SMEM)]
    out_shape = [_sds((s, D_MODEL)), _sds((8, D_MODEL)), _sds((ATT_HEADS, N_BUCKETS))]
    scratch = []
    extra = ()
    if n_sent:
        extra = tuple(chip_sums)
        in_specs += [ANY] * n_arr
        out_specs += [ANY] * n_arr
        out_shape += [_sds((3,) + a.shape[1:], a.dtype) for a in extra]
        scratch = [pltpu.SemaphoreType.DMA((n_sent,)), pltpu.SemaphoreType.DMA((n_sent,))]
    return pl.pallas_call(
        body, grid=(n_step,), in_specs=in_specs, out_specs=out_specs, out_shape=out_shape, scratch_shapes=scratch,
        compiler_params=_cp(("arbitrary",)), name="in_bwd_dx")(*pieces, w_bf, x, dy, norm_w, *ds_accs, buckets, *extra)


def _in_bwd_dw(pieces, x, norm_w):
    s = x.shape[0]
    tm = 512
    n_piece = len(PIECE_WIDTHS)

    def body(*refs):
        piece_refs = refs[:n_piece]
        x_ref, nw_ref, dw_ref = refs[n_piece:]
        i = pl.program_id(0)

        @pl.when(i == 0)
        def _():
            dw_ref[...] = jnp.zeros_like(dw_ref)

        xv = x_ref[...]
        rstd = lax.rsqrt(jnp.mean(xv * xv, axis=-1, keepdims=True) + EPS)
        h = (xv * rstd * nw_ref[...]).astype(BF16)
        at = 0
        for r, width in zip(piece_refs, PIECE_WIDTHS):
            dw_ref[at:at + width, :] = dw_ref[at:at + width, :] + _dot_tn(r[...], h)
            at += width

    return pl.pallas_call(
        body, grid=(s // tm,),
        in_specs=[pl.BlockSpec((tm, n), lambda i: (i, 0)) for n in PIECE_WIDTHS]
        + [pl.BlockSpec((tm, D_MODEL), lambda i: (i, 0)), pl.BlockSpec((1, D_MODEL), lambda i: (0, 0))],
        out_specs=pl.BlockSpec((D_IN_PAD, D_MODEL), lambda i: (0, 0)),
        out_shape=_sds((D_IN_PAD, D_MODEL)),
        compiler_params=_cp(("arbitrary",)), name="in_bwd_dw")(*pieces, x, norm_w)


def _flat(a):
    return a.reshape(-1, a.shape[-1])


def _as_pattern(a, r):
    return a if r == 1 else a.reshape(r, a.shape[0] // r, a.shape[1])


D_SHARD = D_IN // N_CHIPS
BA_START = 4 * D_DN
BA_PACKED = 4096
S1_HEAD = BA_START - D_SHARD
S1_BA = 2 * D_SHARD - BA_START


def _pack_w(g):
    n = g.shape[2]

    def body(g_ref, o_ref):
        g32 = g_ref.bitcast(jnp.uint32)
        o32 = o_ref.bitcast(jnp.uint32)
        full, head, ba1 = D_SHARD // 2, S1_HEAD // 2, S1_BA // 2
        ba2 = DN_HEADS - ba1
        pieces = [(0, 0, full), (1, 0, head), (2, ba2, full), (3, 0, full), (1, head, full), (2, 0, ba2)]
        at = 0
        for chip, lo, hi in pieces:
            o32[at:at + hi - lo, :] = g32[chip, lo:hi, :]
            at += hi - lo
        o32[at:D_IN_PAD // 2, :] = jnp.zeros((D_IN_PAD // 2 - at, n), jnp.uint32)

    vm = pl.BlockSpec(memory_space=pltpu.VMEM)
    return pl.pallas_call(body, in_specs=[vm], out_specs=vm, out_shape=_sds((D_IN_PAD, n), g.dtype),
                          compiler_params=_cp(), name="pack_w")(g)


def _unpack_w(p, rows):
    n = p.shape[1]
    tn = 256
    mid = BA_PACKED + S1_BA
    pieces = [(0, 0, (0, D_SHARD)), (1, 0, (D_SHARD, BA_START)), (1, S1_HEAD, (BA_PACKED, mid)),
              (2, 0, (mid, BA_PACKED + 2 * DN_HEADS)), (2, 2 * DN_HEADS - S1_BA, (BA_START, BA_START + S1_HEAD)),
              (3, 0, (BA_START + S1_HEAD, BA_PACKED))]

    def body(p_ref, o_ref):
        for chip, at, (lo, hi) in pieces:
            o_ref[chip, at:at + hi - lo, :] = p_ref[lo:hi, :]
        for chip in range(N_CHIPS):
            o_ref[chip, D_SHARD:rows, :] = jnp.zeros((rows - D_SHARD, tn), p.dtype)

    return pl.pallas_call(
        body, grid=(n // tn,),
        in_specs=[pl.BlockSpec((D_IN_PAD, tn), lambda j: (0, j))],
        out_specs=pl.BlockSpec((N_CHIPS, rows, tn), lambda j: (0, 0, j)),
        out_shape=_sds((N_CHIPS, rows, n), p.dtype),
        compiler_params=_cp(("parallel",)), name="unpack_w")(p)


def _lane_row(vec, offset):
    return jnp.pad(vec.reshape(1, -1), ((0, 0), (offset, 128 - offset - vec.shape[0])))


def _local_step(x, tgt, norm_w, w_bf, conv_w, a_log, dt_bias, dn_norm_w, q_norm_w, k_norm_w, bias, w_out_bf):
    s = x.shape[0]
    nc = s // CHUNK
    conv_w8 = jnp.pad(conv_w, ((0, 8 - CONV_W), (0, 0)))
    alog_l = _lane_row(a_log.reshape(-1), DN_HEADS)
    dtb_l = _lane_row(dt_bias.reshape(-1), DN_HEADS)
    dnw_t = jnp.tile(dn_norm_w.reshape(1, DN_HD), (1, DN_HEADS))
    qw_t = jnp.tile(q_norm_w.reshape(1, ATT_HD), (1, ATT_HEADS))
    kw_t = jnp.tile(k_norm_w.reshape(1, ATT_HD), (1, ATT_HEADS))
    bd128 = _compact_mat(DN_HD)
    bd64 = _compact_mat(ATT_HD)
    tri = _tri_incl()
    sel = _lane_select()
    buckets = _bucket_tables()

    proj = _in_proj(x, norm_w, w_bf)
    qn, kn, v_dn, bg = _dn_pre(proj, conv_w8, alog_l, dtb_l)
    bgt = bg[:, 0:8].reshape(nc, CHUNK, 8).transpose(0, 2, 1)
    u, w, qd, kt, attn, t_inv, gl = _dn_prep(qn, kn, v_dn, bg, bgt, tri, sel)
    o_dn, vn, sp = _dn_scan(u, w, qd, kt, attn, gl)
    q1, k1, v1, q4, k4, v4, q16, k16, v16 = _att_pre(proj, qw_t, kw_t, bd64)
    rs = [r for _, r in PATTERNS]
    qkv = [(q1, k1, v1), (_flat(q4), _flat(k4), _flat(v4)), (_flat(q16), _flat(k16), _flat(v16))]
    o_pats, lse_pats = [], []
    for p, r in enumerate(rs):
        o_p, lse_p = _att_fwd(*qkv[p], bias, p, r, "att_fwd_r%d" % r)
        o_pats.append(_as_pattern(o_p, r))
        lse_pats.append(_as_pattern(lse_p, r))
    mixed, o_att, l1, l4, l16 = _post_fwd(o_dn, proj, o_pats, lse_pats, dnw_t, bd128)
    dy, loss_blk = _out_fwd(x, mixed, w_out_bf, tgt)

    dmixed, d_w_out = _out_bwd(dy, mixed, w_out_bf)
    do_dn, dz, dgate, do1, do4, do16, dl1, dl4, dl16, d_dnw = _post_bwd(dmixed, o_dn, proj, o_att, dnw_t, bd128)
    side = [(do1, l1, dl1), (_flat(do4), _flat(l4), _flat(dl4)), (_flat(do16), _flat(l16), _flat(dl16))]
    dq_pats, dk_pats, dv_pats, ds_accs = [], [], [], []
    for p, r in enumerate(rs):
        dq_p, dk_p, dv_p, ds_p = _att_bwd(*qkv[p], *side[p], bias, p, r, "att_bwd_r%d" % r)
        dq_pats.append(_as_pattern(dq_p, r))
        dk_pats.append(_as_pattern(dk_p, r))
        dv_pats.append(_as_pattern(dv_p, r))
        ds_accs.append(ds_p)
    dq_att, dk_att, dv_att, d_qw, d_kw = _att_pre_bwd(dq_pats, dk_pats, dv_pats, proj, qw_t, kw_t, bd64)
    du, dqd, dkt, dw, dattn, dgl = _dn_scan_bwd(do_dn, sp, qd, kt, w, vn, attn, gl)
    dqn, dkn, dv_dn, dbg = _dn_prep_bwd(qn, kn, v_dn, bg, bgt, tri, sel, t_inv, attn, u, w, du, dw, dqd, dkt, dattn,
                                        dgl)
    dyc, dba, dsm = _dn_pre_bwd(dqn, dkn, dv_dn, dbg, proj, conv_w8, alog_l, dtb_l)
    d_qkv_dn, d_conv8 = _conv_bwd(dyc, proj, conv_w8)
    pieces = (d_qkv_dn, dz, dq_att, dk_att, dv_att, dgate, dba)
    d_w_in_t = _in_bwd_dw(pieces, x, norm_w)
    last = functools.partial(_in_bwd_dx, pieces, w_bf, x, dy, norm_w, ds_accs, buckets)

    grads = dict(
        w_in_t=d_w_in_t,
        conv_w=d_conv8[0:CONV_W, :],
        a_log=dsm[0:1, DN_HEADS:2 * DN_HEADS],
        dt_bias=dsm[1:2, DN_HEADS:2 * DN_HEADS],
        dn_norm_w=d_dnw[0:1, :],
        q_norm_w=d_qw[0:1, :].reshape(ATT_HEADS, ATT_HD),
        k_norm_w=d_kw[0:1, :].reshape(ATT_HEADS, ATT_HD),
        w_out=d_w_out,
    )
    return loss_blk[0, 0], last, grads


MESH_ID = pl.DeviceIdType.MESH
ANY = pl.BlockSpec(memory_space=pl.ANY)


def _position():
    return lax.axis_index("x"), lax.axis_index("y"), lax.axis_index("c")


def _other_chips(x, y):
    return [(1 - x, y), (x, 1 - y), (1 - x, 1 - y)]


SHARD_PAD = 1040
WIN = 528


def _row_split(n):
    return (n // 2) // 128 * 128


def _part(ref, cc):
    n = ref.shape[0]
    sp = _row_split(n)
    return ref.at[pl.ds(0, sp)] if cc == 0 else ref.at[pl.ds(sp, n - sp)]


def _halves(ref):
    n = ref.shape[0]
    sp = (n // 2) // 16 * 16
    return ref.at[pl.ds(0, sp)], ref.at[pl.ds(sp, n - sp)]


def _gather_weights(wt_s, w_out_s, conv_s, rel_bias, buckets):
    def body(a_ref, b_ref, c_ref, rb_ref, bk_ref, ga_ref, gb_ref, gc_ref, bias_ref, send_sems, recv_sems, loc_sems,
             a_vmem, b_vmem):
        x, y, c = _position()
        me = 2 * x + y
        sib = (x, y, 1 - c)
        big = ((a_ref, ga_ref), (b_ref, gb_ref))
        stage_in = [pltpu.make_async_copy(a_ref, a_vmem, loc_sems.at[0]),
                    pltpu.make_async_copy(b_ref, b_vmem, loc_sems.at[1])]
        local = [pltpu.make_async_copy(a_vmem, ga_ref.at[me], loc_sems.at[0]),
                 pltpu.make_async_copy(b_vmem, gb_ref.at[me], loc_sems.at[1]),
                 pltpu.make_async_copy(c_ref, gc_ref.at[me], loc_sems.at[2])]
        for cp in stage_in:
            cp.start()
        local[2].start()
        for cp in stage_in:
            cp.wait()
        for cp in local[:2]:
            cp.start()
        others = _other_chips(x, y)

        def exchange(cc):
            nbr = others[:2]
            dg = others[2]
            pending = []

            def copy(k, src, dst, to):
                cp = pltpu.make_async_remote_copy(src_ref=src, dst_ref=dst, send_sem=send_sems.at[k],
                                                  recv_sem=recv_sems.at[k], device_id=to, device_id_type=MESH_ID)
                pending.append(cp)
                cp.start()

            def landed(k, dst):
                pltpu.make_async_remote_copy(src_ref=dst, dst_ref=dst, send_sem=send_sems.at[k],
                                             recv_sem=recv_sems.at[k], device_id=sib, device_id_type=MESH_ID).wait_recv()

            def slot(dst, chip, c_part):
                return _part(dst.at[2 * chip[0] + chip[1]], c_part)

            for i, n in enumerate(nbr):
                for t, (src, dst) in enumerate(big):
                    copy(2 * i + t, _part(src, cc), _part(dst.at[me], cc), (*n, c))
            for j, chip in enumerate(others):
                copy(4 + j, c_ref, gc_ref.at[me], (*chip, c))
            _bias_table(rb_ref, bk_ref, bias_ref)
            for i, n in enumerate(nbr):
                for t, (_, dst) in enumerate(big):
                    got = slot(dst, n, cc)
                    landed(2 * i + t, got)
                    half = _halves(got)[1 - i]
                    copy(7 + 2 * (1 - i) + t, half, half, (*nbr[1 - i], c))
                    copy(11 + 2 * i + t, got, got, sib)
            for i, n in enumerate(nbr):
                for t, (_, dst) in enumerate(big):
                    half = _halves(slot(dst, dg, cc))[i]
                    landed(7 + 2 * i + t, half)
                    copy(15 + 2 * i + t, half, half, sib)
            for j, chip in enumerate(others):
                landed(4 + j, gc_ref.at[2 * chip[0] + chip[1]])
            for i, n in enumerate(nbr):
                for t, (_, dst) in enumerate(big):
                    landed(11 + 2 * i + t, slot(dst, n, 1 - cc))
                    landed(15 + 2 * i + t, _halves(slot(dst, dg, 1 - cc))[i])
            for cp in pending:
                cp.wait_send()

        for cc in (0, 1):
            pl.when(c == cc)(functools.partial(exchange, cc))
        for cp in local:
            cp.wait()

    srcs = (wt_s, w_out_s, conv_s)
    n_sem = 19
    return pl.pallas_call(
        body, in_specs=[ANY] * 3 + [pl.BlockSpec(memory_space=pltpu.SMEM), pl.BlockSpec(memory_space=pltpu.VMEM)],
        out_specs=[ANY] * 3 + [pl.BlockSpec(memory_space=pltpu.VMEM)],
        out_shape=[_sds((N_CHIPS,) + a.shape, a.dtype) for a in srcs]
        + [_sds((len(PATTERNS), ATT_HEADS, BLK, 2 * BLK))],
        scratch_shapes=[pltpu.SemaphoreType.DMA((n_sem,)), pltpu.SemaphoreType.DMA((n_sem,)),
                        pltpu.SemaphoreType.DMA((3,)), pltpu.VMEM(wt_s.shape, wt_s.dtype),
                        pltpu.VMEM(w_out_s.shape, w_out_s.dtype)],
        compiler_params=_cp(), name="gather_weights")(*srcs, rel_bias, buckets)


N_DEV = 8
PEER_FLIPS = [(dx, dy, dc) for dx in (0, 1) for dy in (0, 1) for dc in (0, 1)][1:]


def _small_copies(s_ref, rs_ref, send_sems, recv_sems):
    x, y, c = _position()
    dev = 4 * x + 2 * y + c
    sends, recvs = [], []
    for f, (dx, dy, dc) in enumerate(PEER_FLIPS):
        peer = (x ^ dx, y ^ dy, c ^ dc)
        sends.append(pltpu.make_async_remote_copy(
            src_ref=s_ref, dst_ref=rs_ref.at[dev], send_sem=send_sems.at[f], recv_sem=recv_sems.at[f],
            device_id=peer, device_id_type=MESH_ID))
        recvs.append(pltpu.make_async_remote_copy(
            src_ref=s_ref, dst_ref=rs_ref.at[4 * peer[0] + 2 * peer[1] + peer[2]], send_sem=send_sems.at[f],
            recv_sem=recv_sems.at[f], device_id=peer, device_id_type=MESH_ID))
    return sends, recvs


def _fill_parts(h_in, r_in, h_out, r_out, small, rows_in, rows_out):
    def body(ha_ref, ra_ref, hb_ref, rb_ref, s_ref, fa_ref, fb_ref, rs_ref, send_sems, recv_sems, loc_sems,
             small_send, small_recv, a_vmem, b_vmem, ra_vmem, rb_vmem):
        x, y, c = _position()
        sib = (x, y, 1 - c)
        chip = 2 * x + y
        pairs = ((a_vmem, fa_ref), (b_vmem, fb_ref))
        own_small = pltpu.make_async_copy(s_ref, rs_ref.at[4 * x + 2 * y + c], loc_sems.at[2])
        own_small.start()
        small_sends, small_recvs = _small_copies(s_ref, rs_ref, small_send, small_recv)
        for cp in small_sends:
            cp.start()
        stage_in = [pltpu.make_async_copy(ha_ref.at[chip], a_vmem, loc_sems.at[0]),
                    pltpu.make_async_copy(hb_ref.at[chip], b_vmem, loc_sems.at[1]),
                    pltpu.make_async_copy(ra_ref, ra_vmem, loc_sems.at[3]),
                    pltpu.make_async_copy(rb_ref, rb_vmem, loc_sems.at[4])]
        for cp in stage_in:
            cp.start()
        for cp in stage_in:
            cp.wait()
        for tot, recv in ((a_vmem, ra_vmem), (b_vmem, rb_vmem)):
            acc = tot[...]
            for j in range(3):
                acc = acc + recv[j].astype(F32)
            tot[...] = acc

        def fill(cc):
            mine = [_part(dst, cc) for _, dst in pairs]
            srcs = [src.at[pl.ds(0, m.shape[0])] for src, m in zip((a_vmem, b_vmem), mine)]
            local = [pltpu.make_async_copy(s, m, loc_sems.at[t]) for t, (s, m) in enumerate(zip(srcs, mine))]
            sends = [pltpu.make_async_remote_copy(src_ref=s, dst_ref=m, send_sem=send_sems.at[t],
                                                  recv_sem=recv_sems.at[t], device_id=sib, device_id_type=MESH_ID)
                     for t, (s, m) in enumerate(zip(srcs, mine))]
            for cp in local + sends:
                cp.start()
            for t, (_, dst) in enumerate(pairs):
                theirs = _part(dst, 1 - cc)
                pltpu.make_async_remote_copy(src_ref=theirs, dst_ref=theirs, send_sem=send_sems.at[t],
                                             recv_sem=recv_sems.at[t], device_id=sib, device_id_type=MESH_ID).wait_recv()
            for cp in sends:
                cp.wait_send()
            for cp in local:
                cp.wait()

        for cc in (0, 1):
            pl.when(c == cc)(functools.partial(fill, cc))
        for cp in small_recvs:
            cp.wait_recv()
        for cp in small_sends:
            cp.wait_send()
        own_small.wait()

    n_peer = len(PEER_FLIPS)
    return pl.pallas_call(
        body, in_specs=[ANY] * 5, out_specs=[ANY] * 3,
        out_shape=[_sds((rows_in, D_MODEL)), _sds((rows_out, D_MODEL)), _sds((N_DEV,) + small.shape, small.dtype)],
        scratch_shapes=[pltpu.SemaphoreType.DMA((2,)), pltpu.SemaphoreType.DMA((2,)), pltpu.SemaphoreType.DMA((5,)),
                        pltpu.SemaphoreType.DMA((n_peer,)), pltpu.SemaphoreType.DMA((n_peer,)),
                        pltpu.VMEM(h_in.shape[1:], F32), pltpu.VMEM(h_out.shape[1:], F32),
                        pltpu.VMEM(r_in.shape, r_in.dtype), pltpu.VMEM(r_out.shape, r_out.dtype)],
        compiler_params=_cp(), name="fill_parts")(h_in, r_in, h_out, r_out, small)


def _chip_sums(u_in, u_out, win_in, win_out):
    wins = (win_in, win_out)

    def body(a_ref, b_ref, ha_ref, hba_ref, hb_ref, hbb_ref, send_sems, recv_sems, loc_sems, out_sems,
             mine_a, theirs_a, sum_a, sumb_a, mine_b, theirs_b, sum_b, sumb_b):
        x, y, c = _position()
        sib = (x, y, 1 - c)
        groups = ((a_ref, mine_a, theirs_a, sum_a, sumb_a, ha_ref, hba_ref, win_in),
                  (b_ref, mine_b, theirs_b, sum_b, sumb_b, hb_ref, hbb_ref, win_out))
        loads, sends = [], []
        for t, (src, mine, theirs, _, _, _, _, win) in enumerate(groups):
            split = _row_split(src.shape[1])
            for k in range(N_CHIPS):
                n = N_CHIPS * t + k
                loads.append(pltpu.make_async_copy(
                    src.at[k, pl.ds(pl.multiple_of(c * split, split), win), :], mine.at[k], loc_sems.at[n]))
                sends.append(pltpu.make_async_remote_copy(
                    src_ref=src.at[k, pl.ds(pl.multiple_of((1 - c) * split, split), win), :], dst_ref=theirs.at[k],
                    send_sem=send_sems.at[n], recv_sem=recv_sems.at[n], device_id=sib, device_id_type=MESH_ID))
        for cp in sends + loads:
            cp.start()
        stores = []
        for t, (_, mine, theirs, tot, totb, h_out, hb_out, _) in enumerate(groups):
            for k in range(N_CHIPS):
                n = N_CHIPS * t + k
                loads[n].wait()
                sends[n].wait_recv()
                val = mine[k] + theirs[k]
                tot[k] = val
                totb[k] = val.astype(BF16)
                stores += [pltpu.make_async_copy(tot.at[k], h_out.at[k], out_sems.at[2 * n]),
                           pltpu.make_async_copy(totb.at[k], hb_out.at[k], out_sems.at[2 * n + 1])]
                stores[-2].start()
                stores[-1].start()
        for cp in sends:
            cp.wait_send()
        for cp in stores:
            cp.wait()

    shapes = [(N_CHIPS, w, D_MODEL) for w in wins]
    n_cp = 2 * N_CHIPS
    vmem = []
    for shp in shapes:
        vmem += [pltpu.VMEM(shp, F32), pltpu.VMEM(shp, F32), pltpu.VMEM(shp, F32), pltpu.VMEM(shp, BF16)]
    return pl.pallas_call(
        body, in_specs=[ANY] * 2, out_specs=[ANY] * 4,
        out_shape=[_sds(shapes[0]), _sds(shapes[0], BF16), _sds(shapes[1]), _sds(shapes[1], BF16)],
        scratch_shapes=[pltpu.SemaphoreType.DMA((n_cp,)), pltpu.SemaphoreType.DMA((n_cp,)),
                        pltpu.SemaphoreType.DMA((n_cp,)), pltpu.SemaphoreType.DMA((2 * n_cp,))] + vmem,
        compiler_params=_cp(), name="chip_sums")(u_in, u_out)


SMALL_LAYOUT = (("norm_w", 1024), ("conv_w", 6144), ("a_log", 128), ("dt_bias", 128), ("dn_norm_w", 128),
                ("q_norm_w", 512), ("k_norm_w", 512), ("rel_bias", 256), ("loss", 128))
SMALL_TOTAL = sum(n for _, n in SMALL_LAYOUT)


def _small_offset(name):
    off = 0
    for n, size in SMALL_LAYOUT:
        if n == name:
            return off
        off += size
    raise KeyError(name)


def _pack_small(grads):
    parts = []
    for name, size in SMALL_LAYOUT:
        flat = grads[name].reshape(1, -1)
        parts.append(jnp.pad(flat, ((0, 0), (0, size - flat.shape[1]))))
    return jnp.concatenate(parts, axis=1)


def _sum_small(rows):
    n_dev = rows.shape[0]
    q_off = _small_offset("q_norm_w")
    k_off = _small_offset("k_norm_w")

    def body(r_ref, tot_ref, qk_ref):
        tot = r_ref[0:1, :]
        for d in range(1, n_dev):
            tot = tot + r_ref[d:d + 1, :]
        tot_ref[...] = tot
        for row, off in ((0, q_off), (1, k_off)):
            s4 = tot[:, off:off + 128] + tot[:, off + 128:off + 256] + tot[:, off + 256:off + 384] \
                + tot[:, off + 384:off + 512]
            qk_ref[row:row + 1, :] = s4 + pltpu.roll(s4, ATT_HD, 1)

    return pl.pallas_call(
        body, in_specs=[pl.BlockSpec(memory_space=pltpu.VMEM)],
        out_specs=[pl.BlockSpec(memory_space=pltpu.VMEM)] * 2,
        out_shape=[_sds((1, SMALL_TOTAL)), _sds((2, 128))],
        compiler_params=_cp(), name="sum_small")(rows)


def _adamw_math(w, g, m, v):
    m = ADAM_B1 * m + (1.0 - ADAM_B1) * g
    v = ADAM_B2 * v + (1.0 - ADAM_B2) * (g * g)
    m_hat = m / (1.0 - ADAM_B1 ** ADAM_STEP)
    v_hat = v / (1.0 - ADAM_B2 ** ADAM_STEP)
    delta = -ADAM_LR * (m_hat / (jnp.sqrt(v_hat) + ADAM_EPS) + ADAM_WD * w)
    return delta, m, v


def _adamw_big(g, w, m, v, name):
    rows, cols = w.shape
    tr = 128

    def body(g_ref, w_ref, m_ref, v_ref, go_ref, d_ref, nm_ref, nv_ref):
        g = g_ref[...]
        go_ref[...] = g
        d_ref[...], nm_ref[...], nv_ref[...] = _adamw_math(w_ref[...], g, m_ref[...], v_ref[...])

    blk = pl.BlockSpec((tr, cols), lambda i: (i, 0))
    return pl.pallas_call(
        body, grid=(pl.cdiv(rows, tr),), in_specs=[blk] * 4, out_specs=[blk] * 4,
        out_shape=[_sds((rows, cols))] * 4, compiler_params=_cp(("parallel",)), name=name)(g, w, m, v)


def _adamw_rows(g, w, m, v, name):
    rows, cols = w.shape
    tr = 128

    def body(g_ref, w_ref, m_ref, v_ref, go_ref, d_ref, nm_ref, nv_ref, s_g, s_d, s_m, s_v):
        g = g_ref[...]
        s_g[...] = g
        s_d[...], s_m[...], s_v[...] = _adamw_math(w_ref[...], g, m_ref[...], v_ref[...])
        for i in range(tr):
            for scr, out in ((s_g, go_ref), (s_d, d_ref), (s_m, nm_ref), (s_v, nv_ref)):
                out[i] = scr[i:i + 1, :]

    blk = pl.BlockSpec((tr, cols), lambda i: (i, 0))
    oblk = pl.BlockSpec((tr, 1, cols), lambda i: (i, 0, 0))
    return pl.pallas_call(
        body, grid=(pl.cdiv(rows, tr),), in_specs=[blk] * 4, out_specs=[oblk] * 4,
        out_shape=[_sds((rows, 1, cols))] * 4, scratch_shapes=[pltpu.VMEM((tr, cols), F32)] * 4,
        compiler_params=_cp(("parallel",)), name=name)(g, w, m, v)


def _adamw_small(w, g, m, v, name):
    def body(w_ref, g_ref, m_ref, v_ref, d_ref, nm_ref, nv_ref):
        d_ref[...], nm_ref[...], nv_ref[...] = _adamw_math(w_ref[...], g_ref[...], m_ref[...], v_ref[...])

    vm = pl.BlockSpec(memory_space=pltpu.VMEM)
    return pl.pallas_call(body, in_specs=[vm] * 4, out_specs=[vm] * 3, out_shape=[_sds(w.shape)] * 3,
                          compiler_params=_cp(), name=name)(w, g, m, v)


WEIGHTS = ("norm_w", "w_in", "conv_w", "a_log", "dt_bias", "dn_norm_w", "q_norm_w", "k_norm_w", "rel_bias", "w_out")


def kernel(x, norm_w, w_in, conv_w, a_log, dt_bias, dn_norm_w, q_norm_w, k_norm_w, rel_bias, w_out, loss_target, m_norm_w, m_w_in, m_conv_w, m_a_log, m_dt_bias, m_dn_norm_w, m_q_norm_w, m_k_norm_w, m_rel_bias, m_w_out, v_norm_w, v_w_in, v_conv_w, v_a_log, v_dt_bias, v_dn_norm_w, v_q_norm_w, v_k_norm_w, v_rel_bias, v_w_out):
    xi, yi, _ = _position()
    chip = 2 * xi + yi
    w_loc = dict(norm_w=norm_w, w_in=w_in[0].T, conv_w=conv_w[0], a_log=a_log, dt_bias=dt_bias, dn_norm_w=dn_norm_w,
                 q_norm_w=q_norm_w, k_norm_w=k_norm_w, rel_bias=rel_bias, w_out=w_out[0])
    m_loc = dict(norm_w=m_norm_w, w_in=m_w_in[0].T, conv_w=m_conv_w[0], a_log=m_a_log, dt_bias=m_dt_bias,
                 dn_norm_w=m_dn_norm_w, q_norm_w=m_q_norm_w, k_norm_w=m_k_norm_w, rel_bias=m_rel_bias,
                 w_out=m_w_out[0])
    v_loc = dict(norm_w=v_norm_w, w_in=v_w_in[0].T, conv_w=v_conv_w[0], a_log=v_a_log, dt_bias=v_dt_bias,
                 dn_norm_w=v_dn_norm_w, q_norm_w=v_q_norm_w, k_norm_w=v_k_norm_w, rel_bias=v_rel_bias,
                 w_out=v_w_out[0])

    wt_pad = jnp.pad(w_loc["w_in"].astype(BF16), ((0, SHARD_PAD - D_SHARD), (0, 0)))
    g_in, g_out, g_conv, bias = _gather_weights(wt_pad, w_loc["w_out"].astype(BF16), w_loc["conv_w"], rel_bias,
                                                _bucket_tables())
    wt_full = _pack_w(g_in)
    w_out_full = g_out.reshape(D_MODEL, D_MODEL)
    conv_full = g_conv.transpose(1, 0, 2).reshape(CONV_W, 3 * D_DN)

    loss_local, last_kernel, grads = _local_step(x[0], loss_target[0], norm_w, wt_full, conv_full, a_log, dt_bias,
                                                 dn_norm_w, q_norm_w, k_norm_w, bias, w_out_full)
    grads["loss"] = loss_local

    u_in = _unpack_w(grads["w_in_t"], SHARD_PAD)
    u_out = grads["w_out"].reshape(N_CHIPS, D_MODEL // N_CHIPS, D_MODEL)
    win_out = u_out.shape[1] // 2
    h_in, hb_in, h_out, hb_out = _chip_sums(u_in, u_out, WIN, win_out)
    grad_x, d_nw8, grads["rel_bias"], r_in, r_out = last_kernel(chip_sums=(hb_in, hb_out))
    grads["norm_w"] = d_nw8[0:1, :]
    small = _pack_small(grads)
    full_in, full_out, r_small = _fill_parts(h_in, r_in, h_out, r_out, small, SHARD_PAD, u_out.shape[1])
    tot_small, qk = _sum_small(r_small.reshape(8, SMALL_TOTAL))

    def small_grad(name, n):
        off = _small_offset(name)
        return tot_small[:, off:off + n]

    loss = small_grad("loss", 1).reshape(())
    conv_all = small_grad("conv_w", CONV_W * 3 * D_DN).reshape(CONV_W, 3 * D_DN)
    g_small = dict(
        norm_w=small_grad("norm_w", D_MODEL),
        conv_w=lax.dynamic_slice_in_dim(conv_all, chip * (3 * D_DN // N_CHIPS), 3 * D_DN // N_CHIPS, axis=1),
        a_log=small_grad("a_log", DN_HEADS),
        dt_bias=small_grad("dt_bias", DN_HEADS),
        dn_norm_w=small_grad("dn_norm_w", DN_HD),
        q_norm_w=qk[0:1, 0:ATT_HD],
        k_norm_w=qk[1:2, 0:ATT_HD],
        rel_bias=small_grad("rel_bias", ATT_HEADS * N_BUCKETS).reshape(ATT_HEADS, N_BUCKETS),
    )

    out_g, out_d, out_m, out_v = {}, {}, {}, {}
    out_g["w_in"], out_d["w_in"], out_m["w_in"], out_v["w_in"] = _adamw_rows(
        full_in, w_loc["w_in"], m_loc["w_in"], v_loc["w_in"], "adamw_w_in")
    out_g["w_out"], out_d["w_out"], out_m["w_out"], out_v["w_out"] = _adamw_big(
        full_out, w_loc["w_out"], m_loc["w_out"], v_loc["w_out"], "adamw_w_out")
    for name in g_small:
        out_g[name] = g_small[name]
        out_d[name], out_m[name], out_v[name] = _adamw_small(w_loc[name], g_small[name], m_loc[name], v_loc[name],
                                                             "adamw_" + name)
    for d in (out_g, out_d, out_m, out_v):
        d["w_in"] = d["w_in"].transpose(1, 2, 0)
        for name in ("conv_w", "w_out"):
            d[name] = d[name][None]
    return (loss, grad_x[None], *[out_g[n] for n in WEIGHTS], *[out_d[n] for n in WEIGHTS],
            *[out_m[n] for n in WEIGHTS], *[out_v[n] for n in WEIGHTS])
```

```python
import functools
import math

import numpy as np
import jax
import jax.numpy as jnp
from jax import lax
from jax.experimental import pallas as pl
from jax.experimental.pallas import tpu as pltpu

F32 = jnp.float32
BF16 = jnp.bfloat16
HI = lax.Precision.HIGHEST

D_MODEL = 1024
D_DN = 512
DN_HEADS = 4
DN_HD = 128
CONV_W = 4
CHUNK = 64
D_ATT = 512
ATT_HEADS = 8
ATT_HD = 64
PATTERNS = ((128, 1), (512, 4), (2048, 16))
N_BUCKETS = 32
MAX_DISTANCE = 2048
D_IN = 4 * D_DN + 2 * DN_HEADS + 4 * D_ATT
D_IN_PAD = 4224
EPS = 1e-6
BLK = 128
NEG = -1e30
N_CHIPS = 4

ADAM_LR = 0.001
ADAM_B1 = 0.9
ADAM_B2 = 0.999
ADAM_EPS = 1e-08
ADAM_WD = 0.01
ADAM_STEP = 10

VMEM_LIMIT = 56 * 1024 * 1024

COL_Z = 3
COL_ATT_Q = 4
COL_ATT_K = 5
COL_ATT_V = 6
COL_GATE = 7
COL_BA_128 = 32


def _cp(sem=None):
    if sem is None:
        return pltpu.CompilerParams(vmem_limit_bytes=VMEM_LIMIT)
    return pltpu.CompilerParams(dimension_semantics=sem, vmem_limit_bytes=VMEM_LIMIT)


def _sds(shape, dtype=F32):
    return jax.ShapeDtypeStruct(shape, dtype)


def _mm(a, b):
    return jnp.dot(a.astype(BF16), b.astype(BF16), preferred_element_type=F32)


def _mm_nt(a, b):
    return lax.dot_general(a.astype(BF16), b.astype(BF16), (((1,), (1,)), ((), ())),
                           preferred_element_type=F32)


def _mm_tn(a, b):
    return lax.dot_general(a.astype(BF16), b.astype(BF16), (((0,), (0,)), ((), ())),
                           preferred_element_type=F32)


def _mmx(a, b):
    return jnp.dot(a, b, precision=HI, preferred_element_type=F32)


def _mmx_nt(a, b):
    return lax.dot_general(a, b, (((1,), (1,)), ((), ())), precision=HI, preferred_element_type=F32)


def _mmx_tn(a, b):
    return lax.dot_general(a, b, (((0,), (0,)), ((), ())), precision=HI, preferred_element_type=F32)


def _dot(a, b):
    return jnp.dot(a, b, preferred_element_type=F32)


def _dot_nt(a, b):
    return lax.dot_general(a, b, (((1,), (1,)), ((), ())), preferred_element_type=F32)


def _dot_tn(a, b):
    return lax.dot_general(a, b, (((0,), (0,)), ((), ())), preferred_element_type=F32)


def _split(a):
    hi = a.astype(BF16)
    return hi, (a - hi.astype(F32)).astype(BF16)


def _mm3(a_s, b_s):
    return _dot(a_s[0], b_s[0]) + _dot(a_s[0], b_s[1]) + _dot(a_s[1], b_s[0])


def _mm3_tn(a_s, b_s):
    return _dot_tn(a_s[0], b_s[0]) + _dot_tn(a_s[0], b_s[1]) + _dot_tn(a_s[1], b_s[0])


def _interleave(gens):
    live = list(gens)
    while live:
        nxt = []
        for g in live:
            try:
                next(g)
                nxt.append(g)
            except StopIteration:
                pass
        live = nxt


def _segsum(x, bd):
    hi = x.astype(BF16)
    r1 = x - hi.astype(F32)
    mid = r1.astype(BF16)
    lo = (r1 - mid.astype(F32)).astype(BF16)
    return (jnp.dot(hi, bd, preferred_element_type=F32) + jnp.dot(mid, bd, preferred_element_type=F32)
            + jnp.dot(lo, bd, preferred_element_type=F32))


def _compact_mat(seg, n=512):
    slot = 128 * seg // n
    src = np.arange(n)[:, None]
    dst = np.arange(128)[None, :]
    return jnp.asarray((src // seg == dst // slot).astype(np.float32), dtype=BF16)


def _seg_mean(x, cm, seg):
    hi, lo = _split(x)
    return (_dot(hi, cm) + _dot(lo, cm)) * (1.0 / seg)


def _seg_expand(c, cm, seg):
    hi, lo = _split(c)
    return (_dot_nt(hi, cm) + _dot_nt(lo, cm)) * (cm.shape[0] / (128.0 * seg))


def _seg_rstd(x, cm, seg):
    return _seg_expand(lax.rsqrt(_seg_mean(x * x, cm, seg) + EPS), cm, seg)


def _sigmoid(x):
    return 1.0 / (1.0 + jnp.exp(-x))


def _silu_grad(x, s):
    return s * (1.0 + x * (1.0 - s))


def _tri_incl():
    i = np.arange(CHUNK)
    return jnp.asarray((i[:, None] >= i[None, :]).astype(np.float32))


def _t5_bucket(dist):
    max_exact = N_BUCKETS // 2
    d = np.maximum(dist, 1).astype(np.float64)
    large = max_exact + (np.log(d / max_exact) / math.log(MAX_DISTANCE / max_exact)
                         * (N_BUCKETS - max_exact)).astype(np.int32)
    large = np.minimum(large, N_BUCKETS - 1)
    return np.where(dist < max_exact, dist, large).astype(np.int32)


def _bucket_tables():
    qi = np.arange(BLK)[:, None]
    kj = np.arange(2 * BLK)[None, :]
    step = qi - kj + BLK
    return jnp.asarray(np.stack([_t5_bucket(np.clip(step, 0, None) * r) for _, r in PATTERNS]))


def _in_proj(x, norm_w, wt_bf):
    s = x.shape[0]
    tm = 512

    def body(x_ref, nw_ref, w_ref, o_ref):
        xv = x_ref[...]
        rstd = lax.rsqrt(jnp.mean(xv * xv, axis=-1, keepdims=True) + EPS)
        h = (xv * rstd * nw_ref[...]).astype(BF16)
        o_ref[...] = _dot_nt(h, w_ref[...])

    return pl.pallas_call(
        body, grid=(s // tm,),
        in_specs=[pl.BlockSpec((tm, D_MODEL), lambda i: (i, 0)),
                  pl.BlockSpec((1, D_MODEL), lambda i: (0, 0)),
                  pl.BlockSpec((D_IN_PAD, D_MODEL), lambda i: (0, 0))],
        out_specs=pl.BlockSpec((tm, D_IN_PAD), lambda i: (i, 0)),
        out_shape=_sds((s, D_IN_PAD)), compiler_params=_cp(("parallel",)), name="in_proj")(x, norm_w, wt_bf)


def _conv_group(cur, halo, w_ref, c):
    rows = cur.shape[0]
    lanes = slice(128 * c, 128 * c + 128)
    xcat = jnp.concatenate([halo, cur], axis=0)
    y = cur * w_ref[CONV_W - 1:CONV_W, lanes]
    for k in range(1, CONV_W):
        sh = pltpu.roll(xcat, k, 0)[8:8 + rows]
        y = y + sh * w_ref[CONV_W - 1 - k:CONV_W - k, lanes]
    return y


def _beta_g(ba, alog_l, dtb_l):
    lane = lax.broadcasted_iota(jnp.int32, ba.shape, 1)
    sig_b = _sigmoid(ba)
    t = ba + dtb_l
    softplus = jnp.maximum(t, 0.0) + jnp.log(1.0 + jnp.exp(-jnp.abs(t)))
    nega = -jnp.exp(alog_l)
    g = nega * softplus
    out = jnp.where(lane < DN_HEADS, sig_b, jnp.where(lane < 2 * DN_HEADS, g, 0.0))
    return out, lane, sig_b, t, nega, g


def _dn_pre(proj, conv_w8, alog_l, dtb_l):
    s = proj.shape[0]
    tr = 512
    nh = tr // 8

    def body(u_ref, halo_ref, ba_ref, w_ref, al_ref, dt_ref, q_ref, k_ref, v_ref, bg_ref):
        i = pl.program_id(0)
        keep = (i > 0).astype(F32)
        for c in range(12):
            lanes = slice(128 * c, 128 * c + 128)
            y = _conv_group(u_ref[:, lanes], halo_ref[:, lanes] * keep, w_ref, c)
            sv = y * _sigmoid(y)
            if c < 8:
                rs = lax.rsqrt(jnp.sum(sv * sv, axis=1, keepdims=True) + EPS)
                n = sv * rs
                if c < 4:
                    q_ref[:, lanes] = n * (DN_HD ** -0.5)
                else:
                    k_ref[:, slice(128 * (c - 4), 128 * (c - 3))] = n
            else:
                v_ref[:, slice(128 * (c - 8), 128 * (c - 7))] = sv
        bg_ref[...] = _beta_g(ba_ref[...], al_ref[...], dt_ref[...])[0]

    return pl.pallas_call(
        body, grid=(s // tr,),
        in_specs=[pl.BlockSpec((tr, 1536), lambda i: (i, 0)),
                  pl.BlockSpec((8, 1536), lambda i: (jnp.maximum(i * nh - 1, 0), 0)),
                  pl.BlockSpec((tr, 128), lambda i: (i, COL_BA_128)),
                  pl.BlockSpec((8, 1536), lambda i: (0, 0)),
                  pl.BlockSpec((1, 128), lambda i: (0, 0)),
                  pl.BlockSpec((1, 128), lambda i: (0, 0))],
        out_specs=[pl.BlockSpec((tr, 512), lambda i: (i, 0))] * 3 + [pl.BlockSpec((tr, 128), lambda i: (i, 0))],
        out_shape=[_sds((s, 512))] * 3 + [_sds((s, 128))],
        compiler_params=_cp(("parallel",)), name="dn_pre")(proj, proj, proj, conv_w8, alog_l, dtb_l)


CPS = 8
CPS_SCAN = 8
SPLIT_ITERS = 2


def _split3(a):
    hi = a.astype(BF16)
    r1 = a - hi.astype(F32)
    mid = r1.astype(BF16)
    return hi, mid, (r1 - mid.astype(F32)).astype(BF16)


def _lane_select():
    r = np.arange(128)
    return jnp.asarray((r[None, :, None] == np.arange(8)[:, None, None]) * np.ones((1, 1, 128)), dtype=BF16)


def _lane_bcast(a3, sel):
    return _dot(a3[0], sel) + _dot(a3[1], sel) + _dot(a3[2], sel)


def _rowsum_b(z, ones_b):
    hi, lo = _split(z)
    return _dot(hi, ones_b) + _dot(lo, ones_b)


def _chunk_cumsum(bg, bgt, tri):
    return _split3(bg), _split3(_mmx(tri, bg)), _mmx_nt(bgt, tri)


def _chunk_common(bg3, gc3, gc_row, h, sel_ref):
    gcc = _lane_bcast(gc3, sel_ref[DN_HEADS + h])
    beta = _dot(bg3[0], sel_ref[h]) + _dot(bg3[1], sel_ref[h])
    gcr = gc_row[DN_HEADS + h:DN_HEADS + h + 1, :]
    ii = lax.broadcasted_iota(jnp.int32, (CHUNK, CHUNK), 0)
    jj = lax.broadcasted_iota(jnp.int32, (CHUNK, CHUNK), 1)
    incl = ii >= jj
    strict = ii > jj
    decay = jnp.exp(jnp.where(incl, gcc[:, 0:CHUNK] - gcr, NEG))
    gl = gcc[CHUNK - 1:CHUNK, :]
    return gcc, beta, incl, strict, decay, gl


def _dn_prep(qn, kn, v, bg, bgt, tri, sel):
    s = qn.shape[0]
    nc = s // CHUNK

    def body(q_ref, k_ref, v_ref, bg_ref, bgt_ref, tri_ref, sel_ref,
             u_ref, w_ref, qd_ref, kt_ref, attn_ref, t_ref, gl_ref):
        tri_v = tri_ref[...]
        ii = lax.broadcasted_iota(jnp.int32, (CHUNK, CHUNK), 0)
        jj = lax.broadcasted_iota(jnp.int32, (CHUNK, CHUNK), 1)
        eye = (ii == jj).astype(F32)

        def chain(cc, h, bg3, gc3, gc_row):
            rows = slice(CHUNK * cc, CHUNK * cc + CHUNK)
            lanes = slice(128 * h, 128 * h + 128)
            gcc, beta, incl, strict, decay, gl = _chunk_common(bg3, gc3, gc_row, h, sel_ref)
            yield
            q = q_ref[rows, lanes]
            k = k_ref[rows, lanes]
            vv = v_ref[rows, lanes]
            kb = k * beta
            egc = jnp.exp(gcc)
            a_mat = jnp.where(strict, _mm_nt(kb, k) * decay, 0.0)
            attn_ref[cc, h] = jnp.where(incl, _mm_nt(q, k) * decay, 0.0)
            qd_ref[rows, lanes] = (q * egc).astype(BF16)
            kt_ref[rows, lanes] = (k * jnp.exp(gl - gcc)).astype(BF16)
            gl_ref[cc, h] = jnp.exp(gl)
            yield
            p = -a_mat
            t = eye + p
            for it in range(5):
                if it < SPLIT_ITERS:
                    ps = _split(p)
                    p = _mm3(ps, ps)
                    yield
                    t = t + _mm3(_split(t), _split(p))
                else:
                    p = _mm(p, p)
                    yield
                    t = t + _mm(t, p)
                yield
            t_ref[cc, h] = t
            ts = _split(t)
            u_ref[rows, lanes] = _mm3(ts, _split(vv * beta))
            w_ref[rows, lanes] = _mm3(ts, _split(kb * egc)).astype(BF16)

        gens = []
        for cc in range(CPS):
            bgv = bg_ref[CHUNK * cc:CHUNK * cc + CHUNK, :]
            bg3, gc3, gc_row = _chunk_cumsum(bgv, bgt_ref[cc], tri_v)
            gens += [chain(cc, h, bg3, gc3, gc_row) for h in range(DN_HEADS)]
        _interleave(gens)

    rows_step = CPS * CHUNK
    big = pl.BlockSpec((rows_step, 512), lambda n: (n, 0))
    sq = pl.BlockSpec((CPS, DN_HEADS, CHUNK, CHUNK), lambda n: (n, 0, 0, 0))
    return pl.pallas_call(
        body, grid=(nc // CPS,),
        in_specs=[big, big, big, pl.BlockSpec((rows_step, 128), lambda n: (n, 0)),
                  pl.BlockSpec((CPS, 8, CHUNK), lambda n: (n, 0, 0)),
                  pl.BlockSpec((CHUNK, CHUNK), lambda n: (0, 0)),
                  pl.BlockSpec((8, 128, 128), lambda n: (0, 0, 0))],
        out_specs=[big, big, big, big, sq, sq, pl.BlockSpec((CPS, DN_HEADS, 1, 128), lambda n: (n, 0, 0, 0))],
        out_shape=[_sds((s, 512))] + [_sds((s, 512), BF16)] * 3 + [_sds((nc, DN_HEADS, CHUNK, CHUNK))] * 2
        + [_sds((nc, DN_HEADS, 1, 128))],
        compiler_params=_cp(("parallel",)), name="dn_prep")(qn, kn, v, bg, bgt, tri, sel)


def _dn_scan(u, w, qd, kt, attn, gl):
    s = u.shape[0]
    nc = s // CHUNK

    def body(u_ref, w_ref, qd_ref, kt_ref, attn_ref, gl_ref, o_ref, vn_ref, sp_ref, st_ref):
        n = pl.program_id(0)

        @pl.when(n == 0)
        def _():
            st_ref[...] = jnp.zeros_like(st_ref)

        def chain(cc, h):
            rows = slice(CHUNK * cc, CHUNK * cc + CHUNK)
            lanes = slice(128 * h, 128 * h + 128)
            st = st_ref[h]
            sp_ref[cc, h] = st
            stb = st.astype(BF16)
            ws = _dot(w_ref[rows, lanes].astype(BF16), stb)
            qs = _dot(qd_ref[rows, lanes].astype(BF16), stb)
            yield
            vn = u_ref[rows, lanes] - ws
            vnb = vn.astype(BF16)
            vn_ref[rows, lanes] = vnb
            o_ref[rows, lanes] = qs + _dot(attn_ref[cc, h].astype(BF16), vnb)
            st_ref[h] = st * gl_ref[cc, h] + _dot_tn(kt_ref[rows, lanes].astype(BF16), vnb)

        for cc in range(CPS_SCAN):
            _interleave([chain(cc, h) for h in range(DN_HEADS)])

    big = pl.BlockSpec((CPS_SCAN * CHUNK, 512), lambda n: (n, 0))
    return pl.pallas_call(
        body, grid=(nc // CPS_SCAN,),
        in_specs=[big, big, big, big,
                  pl.BlockSpec((CPS_SCAN, DN_HEADS, CHUNK, CHUNK), lambda n: (n, 0, 0, 0)),
                  pl.BlockSpec((CPS_SCAN, DN_HEADS, 1, 128), lambda n: (n, 0, 0, 0))],
        out_specs=[big, big, pl.BlockSpec((CPS_SCAN, DN_HEADS, DN_HD, DN_HD), lambda n: (n, 0, 0, 0))],
        out_shape=[_sds((s, 512)), _sds((s, 512), BF16), _sds((nc, DN_HEADS, DN_HD, DN_HD))],
        scratch_shapes=[pltpu.VMEM((DN_HEADS, DN_HD, DN_HD), F32)],
        compiler_params=_cp(("arbitrary",)), name="dn_scan")(u, w, qd, kt, attn, gl)


R4 = PATTERNS[1][1]
R16 = PATTERNS[2][1]
TM = 512


def _pattern_spec(r, width=512):
    return pl.BlockSpec((r, TM // r, width), lambda i: (0, i, 0))


def _pattern_shape(s, r, dtype=F32, width=512):
    return _sds((r, s // r, width), dtype)


SLABS = pltpu.VMEM((4, TM, 128), F32)

HEAD_SLOT = 128 // ATT_HEADS


def _head_expand():
    src = np.arange(128)[:, None]
    dst = np.arange(512)[None, :]
    return jnp.asarray((src == (dst // ATT_HD) * HEAD_SLOT).astype(np.float32), dtype=BF16)


def _head_compact():
    src = np.arange(512)[:, None]
    dst = np.arange(128)[None, :]
    return jnp.asarray((src // ATT_HD == dst // HEAD_SLOT).astype(np.float32), dtype=BF16)


def _to_patterns(val, dsts, scr):
    for c in range(val.shape[1] // 128):
        lanes = slice(128 * c, 128 * c + 128)
        scr[c] = val[:, lanes]
        for dst_ref, r in dsts:
            for a in range(r):
                dst_ref[a, :, lanes] = scr[c, pl.ds(a, TM // r, stride=r), :].astype(dst_ref.dtype)


def _from_pattern(src_ref, r, scr):
    n_slab = src_ref.shape[2] // 128
    for c in range(n_slab):
        for a in range(r):
            scr[c, pl.ds(a, TM // r, stride=r), :] = src_ref[a, :, 128 * c:128 * c + 128].astype(F32)
    return jnp.concatenate([scr[c] for c in range(n_slab)], axis=1) if n_slab > 1 else scr[0]


def _att_pre(proj, qw_t, kw_t, bd64):
    s = proj.shape[0]

    def body(q_ref, k_ref, v_ref, qw_ref, kw_ref, bd_ref,
             q1_ref, k1_ref, v1_ref, q4_ref, k4_ref, v4_ref, q16_ref, k16_ref, v16_ref, scr):
        bd = bd_ref[...]
        q = q_ref[...]
        k = k_ref[...]
        qn = q * _seg_rstd(q, bd, ATT_HD) * qw_ref[...] * (ATT_HD ** -0.5)
        kn = k * _seg_rstd(k, bd, ATT_HD) * kw_ref[...]
        q1_ref[...] = qn.astype(BF16)
        k1_ref[...] = kn.astype(BF16)
        v1_ref[...] = v_ref[...].astype(BF16)
        _to_patterns(qn, ((q4_ref, R4), (q16_ref, R16)), scr)
        _to_patterns(kn, ((k4_ref, R4), (k16_ref, R16)), scr)
        _to_patterns(v_ref[...], ((v4_ref, R4), (v16_ref, R16)), scr)

    row = pl.BlockSpec((1, 512), lambda i: (0, 0))
    tok = pl.BlockSpec((TM, 512), lambda i: (i, 0))
    return pl.pallas_call(
        body, grid=(s // TM,),
        in_specs=[pl.BlockSpec((TM, 512), lambda i: (i, COL_ATT_Q)),
                  pl.BlockSpec((TM, 512), lambda i: (i, COL_ATT_K)),
                  pl.BlockSpec((TM, 512), lambda i: (i, COL_ATT_V)),
                  row, row, pl.BlockSpec((512, 128), lambda i: (0, 0))],
        out_specs=[tok] * 3 + [_pattern_spec(R4)] * 3 + [_pattern_spec(R16)] * 3,
        out_shape=[_sds((s, 512), BF16)] * 3 + [_pattern_shape(s, R4, BF16)] * 3 + [_pattern_shape(s, R16, BF16)] * 3,
        scratch_shapes=[SLABS],
        compiler_params=_cp(("parallel",)), name="att_pre")(proj, proj, proj, qw_t, kw_t, bd64)


def _bias_table(rb_ref, bk_ref, o_ref):
    for p in range(len(PATTERNS)):
        bk = bk_ref[p]
        for h in range(ATT_HEADS):
            acc = jnp.zeros((BLK, 2 * BLK), F32)
            for b in range(N_BUCKETS):
                acc = jnp.where(bk == b, rb_ref[h, b], acc)
            o_ref[p, h] = acc


def _bias_grad(ds_refs, bk_ref, o_ref):
    for h in range(ATT_HEADS):
        for b in range(N_BUCKETS):
            tot = jnp.zeros((), F32)
            for p, ds_ref in enumerate(ds_refs):
                tot = tot + jnp.sum(jnp.where(bk_ref[p] == b, ds_ref[h], 0.0))
            o_ref[h, b] = tot


QB_FWD = 2
QB_BWD = 4


def _att_masks(has_prev):
    qi = lax.broadcasted_iota(jnp.int32, (BLK, BLK), 0)
    kj = lax.broadcasted_iota(jnp.int32, (BLK, BLK), 1)
    lane = lax.broadcasted_iota(jnp.int32, (BLK, 2 * ATT_HD), 1)
    return jnp.logical_and(kj >= qi, has_prev), kj <= qi, lane < ATT_HD


def _head_lanes(h):
    half = h % 2
    return slice(ATT_HD * h, ATT_HD * h + ATT_HD), slice(ATT_HD * half, ATT_HD * half + ATT_HD)


def _att_scores(qm, kp2, kc2, bias_h, mask_prev, mask_cur):
    s_prev = jnp.where(mask_prev, _dot_nt(qm, kp2) + bias_h[:, :BLK], NEG)
    s_cur = jnp.where(mask_cur, _dot_nt(qm, kc2) + bias_h[:, BLK:], NEG)
    return s_prev, s_cur


def _att_fwd(q, k, v, bias, p_idx, r, name):
    QB = QB_FWD
    s = q.shape[0]
    nblk = s // BLK
    nseq = nblk // r

    def body(q_ref, kp_ref, kc_ref, vp_ref, vc_ref, b_ref, o_ref, lse_ref):
        j = pl.program_id(0)

        def head(h, rows, masks, q2, kp2, kc2, vp2, vc2):
            mask_prev, mask_cur, lo_half = masks
            out_l, pair_l = _head_lanes(h)
            sel = lo_half if h % 2 == 0 else jnp.logical_not(lo_half)
            qm = jnp.where(sel, q2, jnp.zeros_like(q2))
            s_prev, s_cur = _att_scores(qm, kp2, kc2, b_ref[0, h], mask_prev, mask_cur)
            yield
            m = jnp.maximum(jnp.max(s_prev, axis=1, keepdims=True), jnp.max(s_cur, axis=1, keepdims=True))
            p_prev = jnp.exp(s_prev - m)
            p_cur = jnp.exp(s_cur - m)
            l = jnp.sum(p_prev, axis=1, keepdims=True) + jnp.sum(p_cur, axis=1, keepdims=True)
            yield
            o2 = _dot(p_prev.astype(BF16), vp2) + _dot(p_cur.astype(BF16), vc2)
            o_ref[rows, out_l] = (o2 * (1.0 / l))[:, pair_l].astype(BF16)
            lse_ref[rows, HEAD_SLOT * h:HEAD_SLOT * h + HEAD_SLOT] = jnp.broadcast_to(m + jnp.log(l), (BLK, HEAD_SLOT))

        for sub in range(QB):
            rows = slice(BLK * sub, BLK * sub + BLK)
            before = slice(BLK * (sub - 1), BLK * sub)
            masks = _att_masks(((QB * j + sub) % nseq) != 0)
            gens = []
            for pp in range(ATT_HEADS // 2):
                lanes = slice(128 * pp, 128 * pp + 128)
                kp = kp_ref[:, lanes] if sub == 0 else kc_ref[before, lanes]
                vp = vp_ref[:, lanes] if sub == 0 else vc_ref[before, lanes]
                slabs = (q_ref[rows, lanes], kp, kc_ref[rows, lanes], vp, vc_ref[rows, lanes])
                gens += [head(2 * pp, rows, masks, *slabs), head(2 * pp + 1, rows, masks, *slabs)]
            _interleave(gens)

    cur = pl.BlockSpec((QB * BLK, 512), lambda j: (j, 0))
    prev = pl.BlockSpec((BLK, 512), lambda j: (jnp.maximum(QB * j - 1, 0), 0))
    return pl.pallas_call(
        body, grid=(nblk // QB,),
        in_specs=[cur, prev, cur, prev, cur,
                  pl.BlockSpec((1, ATT_HEADS, BLK, 2 * BLK), lambda j: (p_idx, 0, 0, 0))],
        out_specs=[cur, pl.BlockSpec((QB * BLK, 128), lambda j: (j, 0))],
        out_shape=[_sds((s, 512), BF16), _sds((s, 128))],
        compiler_params=_cp(("parallel",)), name=name)(q, k, k, v, v, bias)


def _post_fwd(o_dn, proj, o_pats, lse_pats, dnw_t, bd128):
    s = o_dn.shape[0]

    def body(o_ref, z_ref, gate_ref, o1_ref, o4_ref, o16_ref, s1_ref, s4_ref, s16_ref, wn_ref, bd_ref, ex_ref,
             mixed_ref, oatt_ref, l1_ref, l4_ref, l16_ref, scr_a, scr_b, scr_c, scr_d):
        o = o_ref[...]
        z = z_ref[...]
        rstd = _seg_rstd(o, bd_ref[...], DN_HD)
        y_dn = o * rstd * wn_ref[...] * (z * _sigmoid(z))
        mixed_ref[:, 0:512] = y_dn.astype(BF16)
        lses = (s1_ref[...], _from_pattern(s4_ref, R4, scr_a), _from_pattern(s16_ref, R16, scr_b))
        m = jnp.maximum(jnp.maximum(lses[0], lses[1]), lses[2])
        tot = jnp.exp(lses[0] - m) + jnp.exp(lses[1] - m) + jnp.exp(lses[2] - m)
        big_l = m + jnp.log(tot)
        l1_ref[...] = big_l
        _to_patterns(big_l, ((l4_ref, R4), (l16_ref, R16)), scr_a)
        ex = ex_ref[...]
        outs = (o1_ref[...], _from_pattern(o4_ref, R4, scr_c), _from_pattern(o16_ref, R16, scr_d))
        acc = jnp.zeros_like(o)
        for lse_p, o_p in zip(lses, outs):
            acc = acc + _lane_bcast(_split3(jnp.exp(lse_p - big_l)), ex) * o_p
        gate = gate_ref[...]
        oatt_ref[...] = acc
        mixed_ref[:, 512:1024] = (acc * (gate * _sigmoid(gate))).astype(BF16)

    blk = pl.BlockSpec((TM, 512), lambda i: (i, 0))
    cblk = pl.BlockSpec((TM, 128), lambda i: (i, 0))
    p4, p16 = _pattern_spec(R4), _pattern_spec(R16)
    c4, c16 = _pattern_spec(R4, 128), _pattern_spec(R16, 128)
    return pl.pallas_call(
        body, grid=(s // TM,),
        in_specs=[blk, pl.BlockSpec((TM, 512), lambda i: (i, COL_Z)),
                  pl.BlockSpec((TM, 512), lambda i: (i, COL_GATE)), blk, p4, p16, cblk, c4, c16,
                  pl.BlockSpec((1, 512), lambda i: (0, 0)), pl.BlockSpec((512, 128), lambda i: (0, 0)),
                  pl.BlockSpec((128, 512), lambda i: (0, 0))],
        out_specs=[pl.BlockSpec((TM, D_MODEL), lambda i: (i, 0)), blk, cblk, c4, c16],
        out_shape=[_sds((s, D_MODEL), BF16), _sds((s, 512)), _sds((s, 128)), _pattern_shape(s, R4, F32, 128),
                   _pattern_shape(s, R16, F32, 128)],
        scratch_shapes=[SLABS] * 4,
        compiler_params=_cp(("parallel",)), name="post_fwd")(o_dn, proj, proj, *o_pats, *lse_pats, dnw_t, bd128,
                                                              _head_expand())


def _out_proj(x, mixed, w_out_bf, tgt):
    s = x.shape[0]
    tm = 512

    def body(x_ref, m_ref, w_ref, t_ref, dy_ref, loss_ref, dm_ref, dw_ref):
        i = pl.program_id(0)

        @pl.when(i == 0)
        def _():
            loss_ref[...] = jnp.zeros_like(loss_ref)
            dw_ref[...] = jnp.zeros_like(dw_ref)

        y = x_ref[...] + _dot(m_ref[...], w_ref[...])
        err = y - t_ref[...]
        dy = err * (1.0 / D_MODEL)
        dy_ref[...] = dy
        part = 0.5 * jnp.sum(jnp.mean(err * err, axis=-1, keepdims=True), axis=0, keepdims=True)
        loss_ref[...] = loss_ref[...] + part
        dyb = dy.astype(BF16)
        dm_ref[...] = _dot_nt(dyb, w_ref[...])
        dw_ref[...] = dw_ref[...] + _dot_tn(m_ref[...], dyb)

    blk = pl.BlockSpec((tm, D_MODEL), lambda i: (i, 0))
    full = pl.BlockSpec((D_MODEL, D_MODEL), lambda i: (0, 0))
    return pl.pallas_call(
        body, grid=(s // tm,),
        in_specs=[blk, blk, full, blk],
        out_specs=[blk, pl.BlockSpec((8, 128), lambda i: (0, 0)), blk, full],
        out_shape=[_sds((s, D_MODEL)), _sds((8, 128)), _sds((s, D_MODEL)), _sds((D_MODEL, D_MODEL))],
        compiler_params=_cp(("arbitrary",)), name="out_proj")(x, mixed, w_out_bf, tgt)


def _post_bwd(dmixed, o_dn, proj, o_att, dnw_t, bd128):
    s = o_dn.shape[0]
    tm = TM

    def body(ddn_ref, datt_ref, o_ref, z_ref, gate_ref, oatt_ref, wn_ref, bd128_ref, hc_ref,
             do_ref, dz_ref, dgate_ref, doatt_ref, do4_ref, do16_ref, delta_ref, dl4_ref, dl16_ref, dnw_ref, scr):
        i = pl.program_id(0)

        @pl.when(i == 0)
        def _():
            dnw_ref[...] = jnp.zeros_like(dnw_ref)

        bd128v = bd128_ref[...]
        o = o_ref[...]
        z = z_ref[...]
        wn = wn_ref[...]
        dy = ddn_ref[...]
        rstd = _seg_rstd(o, bd128v, DN_HD)
        nrm = o * rstd
        sz = _sigmoid(z)
        dz_ref[...] = (dy * nrm * wn * _silu_grad(z, sz)).astype(BF16)
        dn = dy * z * sz
        gw = dn * wn
        do_ref[...] = rstd * (gw - nrm * _seg_expand(_seg_mean(gw * nrm, bd128v, DN_HD), bd128v, DN_HD))
        colsum = jnp.sum(dn * nrm, axis=0, keepdims=True)
        fold = colsum[:, 0:128] + colsum[:, 128:256] + colsum[:, 256:384] + colsum[:, 384:512]
        dnw_ref[...] = dnw_ref[...] + fold
        dya = datt_ref[...]
        gate = gate_ref[...]
        oatt = oatt_ref[...]
        sg = _sigmoid(gate)
        dgate_ref[...] = (dya * oatt * _silu_grad(gate, sg)).astype(BF16)
        doa = dya * gate * sg
        doatt_ref[...] = doa.astype(BF16)
        delta = _segsum(doa * oatt, hc_ref[...])
        delta_ref[...] = delta
        _to_patterns(doa, ((do4_ref, R4), (do16_ref, R16)), scr)
        _to_patterns(delta, ((dl4_ref, R4), (dl16_ref, R16)), scr)

    blk = pl.BlockSpec((tm, 512), lambda i: (i, 0))
    cblk = pl.BlockSpec((tm, 128), lambda i: (i, 0))
    p4, p16 = _pattern_spec(R4), _pattern_spec(R16)
    c4, c16 = _pattern_spec(R4, 128), _pattern_spec(R16, 128)
    return pl.pallas_call(
        body, grid=(s // tm,),
        in_specs=[blk, pl.BlockSpec((tm, 512), lambda i: (i, 1)), blk,
                  pl.BlockSpec((tm, 512), lambda i: (i, COL_Z)), pl.BlockSpec((tm, 512), lambda i: (i, COL_GATE)),
                  blk, pl.BlockSpec((1, 512), lambda i: (0, 0)), pl.BlockSpec((512, 128), lambda i: (0, 0)),
                  pl.BlockSpec((512, 128), lambda i: (0, 0))],
        out_specs=[blk, blk, blk, blk, p4, p16, cblk, c4, c16, pl.BlockSpec((8, 128), lambda i: (0, 0))],
        out_shape=[_sds((s, 512))] + [_sds((s, 512), BF16)] * 3 + [_pattern_shape(s, R4, BF16),
                                          _pattern_shape(s, R16, BF16), _sds((s, 128)),
                                          _pattern_shape(s, R4, F32, 128), _pattern_shape(s, R16, F32, 128),
                                          _sds((8, 128))],
        scratch_shapes=[SLABS],
        compiler_params=_cp(("arbitrary",)), name="post_bwd")(dmixed, dmixed, o_dn, proj, proj, o_att, dnw_t,
                                                               bd128, _head_compact())


def _att_bwd(q, k, v, do, big_l, delta, bias, p_idx, r, name):
    QB = QB_BWD
    s = q.shape[0]
    nblk = s // BLK
    nseq = nblk // r
    nstep = nblk // QB

    def body(q_ref, kp_ref, kc_ref, vp_ref, vc_ref, do_ref, l_ref, dl_ref, b_ref,
             dq_ref, dk_ref, dv_ref, ds_ref, dkc_ref, dvc_ref):
        j = pl.program_id(0)

        @pl.when(j == 0)
        def _():
            dkc_ref[...] = jnp.zeros_like(dkc_ref)
            dvc_ref[...] = jnp.zeros_like(dvc_ref)
            ds_ref[...] = jnp.zeros_like(ds_ref)

        @pl.when(j < nstep)
        def _():
            def head(h, sub, masks, q2, do2, kp2, kc2, vp2, vc2):
                mask_prev, mask_cur, lo_half = masks
                rows = slice(BLK * sub, BLK * sub + BLK)
                out_l, pair_l = _head_lanes(h)
                sel = lo_half if h % 2 == 0 else jnp.logical_not(lo_half)
                qm = jnp.where(sel, q2, jnp.zeros_like(q2))
                dom = jnp.where(sel, do2, jnp.zeros_like(do2))
                s_prev, s_cur = _att_scores(qm, kp2, kc2, b_ref[0, h], mask_prev, mask_cur)
                dp_prev = _dot_nt(dom, vp2)
                dp_cur = _dot_nt(dom, vc2)
                yield
                lh = l_ref[rows, HEAD_SLOT * h:HEAD_SLOT * h + 1]
                dh = dl_ref[rows, HEAD_SLOT * h:HEAD_SLOT * h + 1]
                p_prev = jnp.exp(s_prev - lh)
                p_cur = jnp.exp(s_cur - lh)
                ds_prev = p_prev * (dp_prev - dh)
                ds_cur = p_cur * (dp_cur - dh)
                ds_ref[h, :, 0:BLK] = ds_ref[h, :, 0:BLK] + ds_prev
                ds_ref[h, :, BLK:2 * BLK] = ds_ref[h, :, BLK:2 * BLK] + ds_cur
                dsb_prev, dsb_cur = ds_prev.astype(BF16), ds_cur.astype(BF16)
                pb_prev, pb_cur = p_prev.astype(BF16), p_cur.astype(BF16)
                yield
                dq_ref[rows, out_l] = (_dot(dsb_prev, kp2) + _dot(dsb_cur, kc2))[:, pair_l].astype(BF16)
                dk_prev = _dot_tn(dsb_prev, q2)[:, pair_l]
                dv_prev = _dot_tn(pb_prev, do2)[:, pair_l]
                if sub == 0:
                    last = slice(BLK * (QB - 1), BLK * QB)
                    dk_ref[last, out_l] = (dkc_ref[last, out_l] + dk_prev).astype(BF16)
                    dv_ref[last, out_l] = (dvc_ref[last, out_l] + dv_prev).astype(BF16)
                else:
                    before = slice(BLK * (sub - 1), BLK * sub)
                    dkc_ref[before, out_l] = dkc_ref[before, out_l] + dk_prev
                    dvc_ref[before, out_l] = dvc_ref[before, out_l] + dv_prev
                yield
                dkc_ref[rows, out_l] = _dot_tn(dsb_cur, q2)[:, pair_l]
                dvc_ref[rows, out_l] = _dot_tn(pb_cur, do2)[:, pair_l]

            done = slice(0, BLK * (QB - 1))
            dk_ref[done, :] = dkc_ref[done, :].astype(BF16)
            dv_ref[done, :] = dvc_ref[done, :].astype(BF16)
            for sub in range(QB):
                rows = slice(BLK * sub, BLK * sub + BLK)
                before = slice(BLK * (sub - 1), BLK * sub)
                masks = _att_masks(((QB * j + sub) % nseq) != 0)
                gens = []
                for pp in range(ATT_HEADS // 2):
                    lanes = slice(128 * pp, 128 * pp + 128)
                    kp = kp_ref[:, lanes] if sub == 0 else kc_ref[before, lanes]
                    vp = vp_ref[:, lanes] if sub == 0 else vc_ref[before, lanes]
                    slabs = (q_ref[rows, lanes], do_ref[rows, lanes], kp, kc_ref[rows, lanes], vp,
                             vc_ref[rows, lanes])
                    gens += [head(2 * pp, sub, masks, *slabs), head(2 * pp + 1, sub, masks, *slabs)]
                _interleave(gens)

        @pl.when(j == nstep)
        def _():
            dk_ref[...] = dkc_ref[...].astype(BF16)
            dv_ref[...] = dvc_ref[...].astype(BF16)

    last_step = nstep - 1
    cur = pl.BlockSpec((QB * BLK, 512), lambda j: (jnp.minimum(j, last_step), 0))
    compact = pl.BlockSpec((QB * BLK, 128), lambda j: (jnp.minimum(j, last_step), 0))
    lag = pl.BlockSpec((QB * BLK, 512), lambda j: (jnp.clip(j - 1, 0, last_step), 0))
    prev = pl.BlockSpec((BLK, 512), lambda j: (jnp.clip(QB * j - 1, 0, nblk - 1), 0))
    return pl.pallas_call(
        body, grid=(nstep + 1,),
        in_specs=[cur, prev, cur, prev, cur, cur, compact, compact,
                  pl.BlockSpec((1, ATT_HEADS, BLK, 2 * BLK), lambda j: (p_idx, 0, 0, 0))],
        out_specs=[cur, lag, lag, pl.BlockSpec((ATT_HEADS, BLK, 2 * BLK), lambda j: (0, 0, 0))],
        out_shape=[_sds((s, 512), BF16)] * 3 + [_sds((ATT_HEADS, BLK, 2 * BLK))],
        scratch_shapes=[pltpu.VMEM((QB * BLK, 512), F32), pltpu.VMEM((QB * BLK, 512), F32)],
        compiler_params=_cp(("arbitrary",)), name=name)(q, k, k, v, v, do, big_l, delta, bias)


def _att_pre_bwd(dq_pats, dk_pats, dv_pats, proj, qw_t, kw_t, bd64):
    s = proj.shape[0]
    tm = TM

    def body(dq1_ref, dq4_ref, dq16_ref, dk1_ref, dk4_ref, dk16_ref, dv1_ref, dv4_ref, dv16_ref,
             q_ref, k_ref, qw_ref, kw_ref, bd_ref,
             dqr_ref, dkr_ref, dvr_ref, dqw_ref, dkw_ref, scr4, scr16):
        i = pl.program_id(0)

        @pl.when(i == 0)
        def _():
            dqw_ref[...] = jnp.zeros_like(dqw_ref)
            dkw_ref[...] = jnp.zeros_like(dkw_ref)

        bd = bd_ref[...]

        def total(d1_ref, d4_ref, d16_ref):
            return d1_ref[...] + _from_pattern(d4_ref, R4, scr4) + _from_pattern(d16_ref, R16, scr16)

        def one(d_refs, x_ref, w_ref, scale, dx_ref, dw_ref):
            dy = total(*d_refs) * scale
            x = x_ref[...]
            rstd = _seg_rstd(x, bd, ATT_HD)
            nrm = x * rstd
            dw_ref[...] = dw_ref[...] + jnp.sum(dy * nrm, axis=0, keepdims=True)
            g = dy * w_ref[...]
            dx_ref[...] = (rstd * (g - nrm * _seg_expand(_seg_mean(g * nrm, bd, ATT_HD), bd, ATT_HD))).astype(BF16)

        one((dq1_ref, dq4_ref, dq16_ref), q_ref, qw_ref, ATT_HD ** -0.5, dqr_ref, dqw_ref)
        one((dk1_ref, dk4_ref, dk16_ref), k_ref, kw_ref, 1.0, dkr_ref, dkw_ref)
        dvr_ref[...] = total(dv1_ref, dv4_ref, dv16_ref).astype(BF16)

    blk = pl.BlockSpec((tm, 512), lambda i: (i, 0))
    pats = [blk, _pattern_spec(R4), _pattern_spec(R16)]
    row = pl.BlockSpec((1, 512), lambda i: (0, 0))
    acc = pl.BlockSpec((8, 512), lambda i: (0, 0))
    return pl.pallas_call(
        body, grid=(s // tm,),
        in_specs=pats * 3 + [pl.BlockSpec((tm, 512), lambda i: (i, COL_ATT_Q)),
                             pl.BlockSpec((tm, 512), lambda i: (i, COL_ATT_K)), row, row,
                             pl.BlockSpec((512, 128), lambda i: (0, 0))],
        out_specs=[blk, blk, blk, acc, acc],
        out_shape=[_sds((s, 512), BF16)] * 3 + [_sds((8, 512))] * 2,
        scratch_shapes=[SLABS] * 2,
        compiler_params=_cp(("arbitrary",)), name="att_pre_bwd")(*dq_pats, *dk_pats, *dv_pats, proj, proj,
                                                                  qw_t, kw_t, bd64)


def _dn_scan_bwd(do, sp, qd, kt, w, vn, attn, gl):
    s = do.shape[0]
    nc = s // CHUNK

    def body(do_ref, sp_ref, qd_ref, kt_ref, w_ref, vn_ref, attn_ref, gl_ref,
             du_ref, dqd_ref, dkt_ref, dw_ref, dattn_ref, dgl_ref, ds_ref):
        n = pl.program_id(0)

        @pl.when(n == 0)
        def _():
            ds_ref[...] = jnp.zeros_like(ds_ref)

        def chain(cc, h):
            rows = slice(CHUNK * cc, CHUNK * cc + CHUNK)
            lanes = slice(128 * h, 128 * h + 128)
            dsn = ds_ref[h]
            st = sp_ref[cc, h]
            dsb, stb = dsn.astype(BF16), st.astype(BF16)
            dob = do_ref[rows, lanes].astype(BF16)
            vnb = vn_ref[rows, lanes].astype(BF16)
            dvn = _dot_tn(attn_ref[cc, h].astype(BF16), dob) + _dot(kt_ref[rows, lanes].astype(BF16), dsb)
            du_ref[rows, lanes] = dvn
            dqd_ref[rows, lanes] = _dot_nt(dob, stb)
            dattn_ref[cc, h] = _dot_nt(dob, vnb)
            dkt_ref[rows, lanes] = _dot_nt(vnb, dsb)
            tot = jnp.sum(jnp.sum(st * dsn, axis=1, keepdims=True), axis=0, keepdims=True)
            dgl_ref[cc, h] = jnp.broadcast_to(tot, (1, 128))
            qdo = _dot_tn(qd_ref[rows, lanes].astype(BF16), dob)
            yield
            dvb = dvn.astype(BF16)
            dw_ref[rows, lanes] = -_dot_nt(dvb, stb)
            ds_ref[h] = qdo + dsn * gl_ref[cc, h] - _dot_tn(w_ref[rows, lanes].astype(BF16), dvb)

        for cc in reversed(range(CPS_SCAN)):
            _interleave([chain(cc, h) for h in range(DN_HEADS)])

    nsteps = nc // CPS_SCAN
    big = pl.BlockSpec((CPS_SCAN * CHUNK, 512), lambda n: (nsteps - 1 - n, 0))
    sq = pl.BlockSpec((CPS_SCAN, DN_HEADS, CHUNK, CHUNK), lambda n: (nsteps - 1 - n, 0, 0, 0))
    glb = pl.BlockSpec((CPS_SCAN, DN_HEADS, 1, 128), lambda n: (nsteps - 1 - n, 0, 0, 0))
    return pl.pallas_call(
        body, grid=(nsteps,),
        in_specs=[big, pl.BlockSpec((CPS_SCAN, DN_HEADS, DN_HD, DN_HD), lambda n: (nsteps - 1 - n, 0, 0, 0)),
                  big, big, big, big, sq, glb],
        out_specs=[big, big, big, big, sq, glb],
        out_shape=[_sds((s, 512))] * 4 + [_sds((nc, DN_HEADS, CHUNK, CHUNK)), _sds((nc, DN_HEADS, 1, 128))],
        scratch_shapes=[pltpu.VMEM((DN_HEADS, DN_HD, DN_HD), F32)],
        compiler_params=_cp(("arbitrary",)), name="dn_scan_bwd")(do, sp, qd, kt, w, vn, attn, gl)


def _dn_prep_bwd(qn, kn, v, bg, bgt, tri, sel, t_inv, attn, u, w, du, dw, dqd, dkt, dattn, dgl):
    s = qn.shape[0]
    nc = s // CHUNK

    def body(q_ref, k_ref, v_ref, bg_ref, bgt_ref, tri_ref, sel_ref, t_ref, attn_ref, u_ref, w_ref,
             du_ref, dw_ref, dqd_ref, dkt_ref, dattn_ref, dgl_ref,
             dq_ref, dk_ref, dv_ref, dbg_ref):
        tri_v = tri_ref[...]
        lane = lax.broadcasted_iota(jnp.int32, (CHUNK, 128), 1)
        rowi = lax.broadcasted_iota(jnp.int32, (CHUNK, 128), 0)
        ones_b = jnp.ones((CHUNK, 128), BF16)
        ones_sq = jnp.ones((128, 128), BF16)
        parts = [[] for _ in range(CPS)]

        def chain(cc, h, bg3, gc3, gc_row):
            rows = slice(CHUNK * cc, CHUNK * cc + CHUNK)
            lanes = slice(128 * h, 128 * h + 128)
            gcc, beta, incl, strict, decay, gl = _chunk_common(bg3, gc3, gc_row, h, sel_ref)
            yield
            q = q_ref[rows, lanes]
            k = k_ref[rows, lanes]
            vv = v_ref[rows, lanes]
            ts = _split(t_ref[cc, h])
            egc = jnp.exp(gcc)
            kb = k * beta
            a_mat = jnp.where(strict, _mm_nt(kb, k) * decay, 0.0)
            dvb = _mm3_tn(ts, _split(du_ref[rows, lanes]))
            dkbg = _mm3_tn(ts, _split(dw_ref[rows, lanes]))
            yield
            d_a = jnp.where(strict, -(_mm_nt(dvb, u_ref[rows, lanes]) + _mm_nt(dkbg, w_ref[rows, lanes])), 0.0)
            d_m = d_a * decay
            dattn_m = jnp.where(incl, dattn_ref[cc, h], 0.0)
            dqk = dattn_m * decay
            e_hi, e_lo = _split(d_a * a_mat + dattn_m * attn_ref[cc, h])
            yield
            dkb = _mm(d_m, k)
            dk = _mm_tn(d_m, kb) + _mm_tn(dqk, q)
            dq = _mm(dqk, k)
            e_colsum = _dot_tn(e_hi, ones_b) + _dot_tn(e_lo, ones_b)
            e_rowsum = _dot(e_hi, ones_b) + _dot(e_lo, ones_b)
            dqd = dqd_ref[rows, lanes]
            dkt = dkt_ref[rows, lanes]
            sums = _rowsum_b(jnp.concatenate([dqd * q, dkt * k, dkbg * k, dkb * k, dvb * vv], axis=0), ones_sq)
            s_dqd, s_dkt, rk, s_dkb, s_dvb = (sums[CHUNK * n:CHUNK * n + CHUNK] for n in range(5))
            yield
            tail = jnp.exp(gl - gcc)
            r = s_dkt * tail
            dgl_tot = jnp.sum(r, axis=0, keepdims=True) + dgl_ref[cc, h] * jnp.exp(gl)
            dgc = e_rowsum - e_colsum + s_dqd * egc - r + rk * beta * egc
            dgc = dgc + jnp.where(rowi == CHUNK - 1, dgl_tot, 0.0)
            dq_ref[rows, lanes] = dq + dqd * egc
            dk_ref[rows, lanes] = dk + dkt * tail + dkbg * (beta * egc) + dkb * beta
            dv_ref[rows, lanes] = dvb * beta
            parts[cc].append((h, dgc, rk * egc + s_dkb + s_dvb))

        gens = []
        for cc in range(CPS):
            bgv = bg_ref[CHUNK * cc:CHUNK * cc + CHUNK, :]
            bg3, gc3, gc_row = _chunk_cumsum(bgv, bgt_ref[cc], tri_v)
            gens += [chain(cc, h, bg3, gc3, gc_row) for h in range(DN_HEADS)]
        _interleave(gens)
        for cc in range(CPS):
            dgc_mat = jnp.zeros((CHUNK, 128), F32)
            dbeta_mat = jnp.zeros((CHUNK, 128), F32)
            for h, dgc, dbeta in parts[cc]:
                dgc_mat = dgc_mat + jnp.where(lane == DN_HEADS + h, dgc, 0.0)
                dbeta_mat = dbeta_mat + jnp.where(lane == h, dbeta, 0.0)
            dbg_ref[CHUNK * cc:CHUNK * cc + CHUNK, :] = _mmx_tn(tri_v, dgc_mat) + dbeta_mat

    big = pl.BlockSpec((CPS * CHUNK, 512), lambda n: (n, 0))
    sq = pl.BlockSpec((CPS, DN_HEADS, CHUNK, CHUNK), lambda n: (n, 0, 0, 0))
    glb = pl.BlockSpec((CPS, DN_HEADS, 1, 128), lambda n: (n, 0, 0, 0))
    small = pl.BlockSpec((CPS * CHUNK, 128), lambda n: (n, 0))
    return pl.pallas_call(
        body, grid=(nc // CPS,),
        in_specs=[big, big, big, small, pl.BlockSpec((CPS, 8, CHUNK), lambda n: (n, 0, 0)),
                  pl.BlockSpec((CHUNK, CHUNK), lambda n: (0, 0)),
                  pl.BlockSpec((8, 128, 128), lambda n: (0, 0, 0)), sq, sq, big, big,
                  big, big, big, big, sq, glb],
        out_specs=[big, big, big, small],
        out_shape=[_sds((s, 512))] * 3 + [_sds((s, 128))],
        compiler_params=_cp(("parallel",)), name="dn_prep_bwd")(qn, kn, v, bg, bgt, tri, sel, t_inv, attn, u, w,
                                                                  du, dw, dqd, dkt, dattn, dgl)


def _dn_pre_bwd(dqn, dkn, dv, dbg, proj, conv_w8, alog_l, dtb_l):
    s = proj.shape[0]
    tr = 512
    nh = tr // 8

    def body(dq_ref, dk_ref, dv_ref, dbg_ref, u_ref, halo_ref, ba_ref, w_ref, al_ref, dt_ref,
             dy_ref, dba_ref, dsm_ref):
        i = pl.program_id(0)

        @pl.when(i == 0)
        def _():
            dsm_ref[...] = jnp.zeros_like(dsm_ref)

        keep = (i > 0).astype(F32)
        for c in range(12):
            lanes = slice(128 * c, 128 * c + 128)
            y = _conv_group(u_ref[:, lanes], halo_ref[:, lanes] * keep, w_ref, c)
            sg = _sigmoid(y)
            sv = y * sg
            if c < 8:
                rs = lax.rsqrt(jnp.sum(sv * sv, axis=1, keepdims=True) + EPS)
                n = sv * rs
                if c < 4:
                    dn = dq_ref[:, lanes] * (DN_HD ** -0.5)
                else:
                    dn = dk_ref[:, slice(128 * (c - 4), 128 * (c - 3))]
                dsv = rs * (dn - n * jnp.sum(dn * n, axis=1, keepdims=True))
            else:
                dsv = dv_ref[:, slice(128 * (c - 8), 128 * (c - 7))]
            dy_ref[:, lanes] = dsv * _silu_grad(y, sg)
        _, lane, sig_b, t, nega, g = _beta_g(ba_ref[...], al_ref[...], dt_ref[...])
        dbg = dbg_ref[...]
        da = dbg * nega * _sigmoid(t)
        is_b = lane < DN_HEADS
        is_a = jnp.logical_and(lane >= DN_HEADS, lane < 2 * DN_HEADS)
        dba_ref[...] = jnp.where(is_b, dbg * sig_b * (1.0 - sig_b), jnp.where(is_a, da, 0.0)).astype(BF16)
        d_alog = jnp.sum(jnp.where(is_a, dbg * g, 0.0), axis=0, keepdims=True)
        d_dtb = jnp.sum(jnp.where(is_a, da, 0.0), axis=0, keepdims=True)
        row = lax.broadcasted_iota(jnp.int32, (8, 128), 0)
        dsm_ref[...] = dsm_ref[...] + jnp.where(row == 0, d_alog, jnp.where(row == 1, d_dtb, 0.0))

    blk = pl.BlockSpec((tr, 512), lambda i: (i, 0))
    return pl.pallas_call(
        body, grid=(s // tr,),
        in_specs=[blk, blk, blk, pl.BlockSpec((tr, 128), lambda i: (i, 0)),
                  pl.BlockSpec((tr, 1536), lambda i: (i, 0)),
                  pl.BlockSpec((8, 1536), lambda i: (jnp.maximum(i * nh - 1, 0), 0)),
                  pl.BlockSpec((tr, 128), lambda i: (i, COL_BA_128)),
                  pl.BlockSpec((8, 1536), lambda i: (0, 0)),
                  pl.BlockSpec((1, 128), lambda i: (0, 0)), pl.BlockSpec((1, 128), lambda i: (0, 0))],
        out_specs=[pl.BlockSpec((tr, 1536), lambda i: (i, 0)), pl.BlockSpec((tr, 128), lambda i: (i, 0)),
                   pl.BlockSpec((8, 128), lambda i: (0, 0))],
        out_shape=[_sds((s, 1536)), _sds((s, 128), BF16), _sds((8, 128))],
        compiler_params=_cp(("arbitrary",)), name="dn_pre_bwd")(dqn, dkn, dv, dbg, proj, proj, proj, conv_w8,
                                                                 alog_l, dtb_l)


def _conv_bwd(dy, proj, conv_w8):
    s = dy.shape[0]
    tr = 512
    nh = tr // 8
    nblk = s // tr

    def body(dy_ref, dyn_ref, u_ref, halo_ref, w_ref, du_ref, dw_ref):
        i = pl.program_id(0)

        @pl.when(i == 0)
        def _():
            dw_ref[...] = jnp.zeros_like(dw_ref)

        keep_prev = (i > 0).astype(F32)
        keep_next = (i < nblk - 1).astype(F32)
        row = lax.broadcasted_iota(jnp.int32, (8, 128), 0)
        for c in range(12):
            lanes = slice(128 * c, 128 * c + 128)
            dyc = dy_ref[:, lanes]
            dcat = jnp.concatenate([dyc, dyn_ref[:, lanes] * keep_next], axis=0)
            xcat = jnp.concatenate([halo_ref[:, lanes] * keep_prev, u_ref[:, lanes]], axis=0)
            du = dyc * w_ref[CONV_W - 1:CONV_W, lanes]
            dwc = jnp.where(row == CONV_W - 1, jnp.sum(dyc * u_ref[:, lanes], axis=0, keepdims=True), 0.0)
            for k in range(1, CONV_W):
                du = du + pltpu.roll(dcat, tr + 8 - k, 0)[0:tr] * w_ref[CONV_W - 1 - k:CONV_W - k, lanes]
                ush = pltpu.roll(xcat, k, 0)[8:8 + tr]
                dwc = dwc + jnp.where(row == CONV_W - 1 - k, jnp.sum(dyc * ush, axis=0, keepdims=True), 0.0)
            du_ref[:, lanes] = du.astype(BF16)
            dw_ref[:, lanes] = dw_ref[:, lanes] + dwc

    return pl.pallas_call(
        body, grid=(nblk,),
        in_specs=[pl.BlockSpec((tr, 1536), lambda i: (i, 0)),
                  pl.BlockSpec((8, 1536), lambda i: (jnp.minimum((i + 1) * nh, s // 8 - 1), 0)),
                  pl.BlockSpec((tr, 1536), lambda i: (i, 0)),
                  pl.BlockSpec((8, 1536), lambda i: (jnp.maximum(i * nh - 1, 0), 0)),
                  pl.BlockSpec((8, 1536), lambda i: (0, 0))],
        out_specs=[pl.BlockSpec((tr, 1536), lambda i: (i, 0)), pl.BlockSpec((8, 1536), lambda i: (0, 0))],
        out_shape=[_sds((s, 1536), BF16), _sds((8, 1536))],
        compiler_params=_cp(("arbitrary",)), name="conv_bwd")(dy, dy, proj, proj, conv_w8)


PIECE_WIDTHS = (1536, 512, 512, 512, 512, 512, 128)


def _in_bwd_dx(pieces, w_bf, x, dy, norm_w, ds_accs, buckets, chip_sums=None):
    s = x.shape[0]
    tm = 512
    n_piece = len(PIECE_WIDTHS)
    n_step = s // tm
    n_arr = 0 if chip_sums is None else len(chip_sums)
    n_sent = 3 * n_arr

    def body(*refs):
        refs = list(refs)

        def take(n):
            return [refs.pop(0) for _ in range(n)]

        piece_refs = take(n_piece)
        w_ref, x_ref, dy_ref, nw_ref = take(4)
        ds_refs = take(len(ds_accs))
        bk_ref, = take(1)
        hb_refs = take(n_arr)
        dx_ref, dnw_ref, drb_ref = take(3)
        recv_refs = take(n_arr)
        i = pl.program_id(0)

        def copies():
            send_sems, recv_sems = refs
            xi, yi, ci = _position()
            out = []
            for j, (px, py) in enumerate(_other_chips(xi, yi)):
                for t, (src, dst) in enumerate(zip(hb_refs, recv_refs)):
                    k = n_arr * j + t
                    out.append(pltpu.make_async_remote_copy(
                        src_ref=src.at[2 * px + py], dst_ref=dst.at[j], send_sem=send_sems.at[k],
                        recv_sem=recv_sems.at[k], device_id=(px, py, ci), device_id_type=MESH_ID))
            return out

        @pl.when(i == 0)
        def _():
            dnw_ref[...] = jnp.zeros_like(dnw_ref)
            for cp in (copies() if n_sent else []):
                cp.start()

        @pl.when(i == 1)
        def _():
            _bias_grad(ds_refs, bk_ref, drb_ref)

        dp = jnp.concatenate([r[...] for r in piece_refs], axis=1)
        dh = _dot(dp, w_ref[...])
        xv = x_ref[...]
        rstd = lax.rsqrt(jnp.mean(xv * xv, axis=-1, keepdims=True) + EPS)
        xh = xv * rstd
        dnw_ref[...] = dnw_ref[...] + jnp.sum(dh * xh, axis=0, keepdims=True)
        g = dh * nw_ref[...]
        dx_ref[...] = rstd * (g - xh * jnp.mean(g * xh, axis=-1, keepdims=True)) + dy_ref[...]

        if n_sent:
            @pl.when(i == n_step - 1)
            def _():
                cps = copies()
                for cp in cps:
                    cp.wait_recv()
                for cp in cps:
                    cp.wait_send()

    def blk(n):
        return pl.BlockSpec((tm, n), lambda i: (i, 0))

    vm = pl.BlockSpec(memory_space=pltpu.VMEM)
    in_specs = [blk(n) for n in PIECE_WIDTHS] + [pl.BlockSpec((D_IN_PAD, D_MODEL), lambda i: (0, 0)), blk(D_MODEL),
                                                blk(D_MODEL), pl.BlockSpec((1, D_MODEL), lambda i: (0, 0))]
    in_specs += [vm] * (len(ds_accs) + 1)
    out_specs = [blk(D_MODEL), pl.BlockSpec((8, D_MODEL), lambda i: (0, 0)), pl.BlockSpec(memory_space=pltpu.SMEM)]
    out_shape = [_sds((s, D_MODEL)), _sds((8, D_MODEL)), _sds((ATT_HEADS, N_BUCKETS))]
    scratch = []
    extra = ()
    if n_sent:
        extra = tuple(chip_sums)
        in_specs += [ANY] * n_arr
        out_specs += [ANY] * n_arr
        out_shape += [_sds((3,) + a.shape[1:], a.dtype) for a in extra]
        scratch = [pltpu.SemaphoreType.DMA((n_sent,)), pltpu.SemaphoreType.DMA((n_sent,))]
    return pl.pallas_call(
        body, grid=(n_step,), in_specs=in_specs, out_specs=out_specs, out_shape=out_shape, scratch_shapes=scratch,
        compiler_params=_cp(("arbitrary",)), name="in_bwd_dx")(*pieces, w_bf, x, dy, norm_w, *ds_accs, buckets, *extra)


def _in_bwd_dw(pieces, x, norm_w):
    s = x.shape[0]
    tm = 512
    n_piece = len(PIECE_WIDTHS)

    def body(*refs):
        piece_refs = refs[:n_piece]
        x_ref, nw_ref, dw_ref = refs[n_piece:]
        i = pl.program_id(0)

        @pl.when(i == 0)
        def _():
            dw_ref[...] = jnp.zeros_like(dw_ref)

        xv = x_ref[...]
        rstd = lax.rsqrt(jnp.mean(xv * xv, axis=-1, keepdims=True) + EPS)
        h = (xv * rstd * nw_ref[...]).astype(BF16)
        at = 0
        for r, width in zip(piece_refs, PIECE_WIDTHS):
            dw_ref[at:at + width, :] = dw_ref[at:at + width, :] + _dot_tn(r[...], h)
            at += width

    return pl.pallas_call(
        body, grid=(s // tm,),
        in_specs=[pl.BlockSpec((tm, n), lambda i: (i, 0)) for n in PIECE_WIDTHS]
        + [pl.BlockSpec((tm, D_MODEL), lambda i: (i, 0)), pl.BlockSpec((1, D_MODEL), lambda i: (0, 0))],
        out_specs=pl.BlockSpec((D_IN_PAD, D_MODEL), lambda i: (0, 0)),
        out_shape=_sds((D_IN_PAD, D_MODEL)),
        compiler_params=_cp(("arbitrary",)), name="in_bwd_dw")(*pieces, x, norm_w)


def _flat(a):
    return a.reshape(-1, a.shape[-1])


def _as_pattern(a, r):
    return a if r == 1 else a.reshape(r, a.shape[0] // r, a.shape[1])


D_SHARD = D_IN // N_CHIPS
BA_START = 4 * D_DN
BA_PACKED = 4096
S1_HEAD = BA_START - D_SHARD
S1_BA = 2 * D_SHARD - BA_START


def _pack_w(g):
    n = g.shape[2]

    def body(g_ref, o_ref):
        g32 = g_ref.bitcast(jnp.uint32)
        o32 = o_ref.bitcast(jnp.uint32)
        full, head, ba1 = D_SHARD // 2, S1_HEAD // 2, S1_BA // 2
        ba2 = DN_HEADS - ba1
        pieces = [(0, 0, full), (1, 0, head), (2, ba2, full), (3, 0, full), (1, head, full), (2, 0, ba2)]
        at = 0
        for chip, lo, hi in pieces:
            o32[at:at + hi - lo, :] = g32[chip, lo:hi, :]
            at += hi - lo
        o32[at:D_IN_PAD // 2, :] = jnp.zeros((D_IN_PAD // 2 - at, n), jnp.uint32)

    vm = pl.BlockSpec(memory_space=pltpu.VMEM)
    return pl.pallas_call(body, in_specs=[vm], out_specs=vm, out_shape=_sds((D_IN_PAD, n), g.dtype),
                          compiler_params=_cp(), name="pack_w")(g)


def _unpack_w(p, rows):
    n = p.shape[1]
    tn = 256
    mid = BA_PACKED + S1_BA
    pieces = [(0, 0, (0, D_SHARD)), (1, 0, (D_SHARD, BA_START)), (1, S1_HEAD, (BA_PACKED, mid)),
              (2, 0, (mid, BA_PACKED + 2 * DN_HEADS)), (2, 2 * DN_HEADS - S1_BA, (BA_START, BA_START + S1_HEAD)),
              (3, 0, (BA_START + S1_HEAD, BA_PACKED))]

    def body(p_ref, o_ref):
        for chip, at, (lo, hi) in pieces:
            o_ref[chip, at:at + hi - lo, :] = p_ref[lo:hi, :]
        for chip in range(N_CHIPS):
            o_ref[chip, D_SHARD:rows, :] = jnp.zeros((rows - D_SHARD, tn), p.dtype)

    return pl.pallas_call(
        body, grid=(n // tn,),
        in_specs=[pl.BlockSpec((D_IN_PAD, tn), lambda j: (0, j))],
        out_specs=pl.BlockSpec((N_CHIPS, rows, tn), lambda j: (0, 0, j)),
        out_shape=_sds((N_CHIPS, rows, n), p.dtype),
        compiler_params=_cp(("parallel",)), name="unpack_w")(p)


def _lane_row(vec, offset):
    return jnp.pad(vec.reshape(1, -1), ((0, 0), (offset, 128 - offset - vec.shape[0])))


def _local_step(x, tgt, norm_w, w_bf, conv_w, a_log, dt_bias, dn_norm_w, q_norm_w, k_norm_w, bias, w_out_bf):
    s = x.shape[0]
    nc = s // CHUNK
    conv_w8 = jnp.pad(conv_w, ((0, 8 - CONV_W), (0, 0)))
    alog_l = _lane_row(a_log.reshape(-1), DN_HEADS)
    dtb_l = _lane_row(dt_bias.reshape(-1), DN_HEADS)
    dnw_t = jnp.tile(dn_norm_w.reshape(1, DN_HD), (1, DN_HEADS))
    qw_t = jnp.tile(q_norm_w.reshape(1, ATT_HD), (1, ATT_HEADS))
    kw_t = jnp.tile(k_norm_w.reshape(1, ATT_HD), (1, ATT_HEADS))
    bd128 = _compact_mat(DN_HD)
    bd64 = _compact_mat(ATT_HD)
    tri = _tri_incl()
    sel = _lane_select()
    buckets = _bucket_tables()

    proj = _in_proj(x, norm_w, w_bf)
    qn, kn, v_dn, bg = _dn_pre(proj, conv_w8, alog_l, dtb_l)
    bgt = bg[:, 0:8].reshape(nc, CHUNK, 8).transpose(0, 2, 1)
    u, w, qd, kt, attn, t_inv, gl = _dn_prep(qn, kn, v_dn, bg, bgt, tri, sel)
    o_dn, vn, sp = _dn_scan(u, w, qd, kt, attn, gl)
    q1, k1, v1, q4, k4, v4, q16, k16, v16 = _att_pre(proj, qw_t, kw_t, bd64)
    rs = [r for _, r in PATTERNS]
    qkv = [(q1, k1, v1), (_flat(q4), _flat(k4), _flat(v4)), (_flat(q16), _flat(k16), _flat(v16))]
    o_pats, lse_pats = [], []
    for p, r in enumerate(rs):
        o_p, lse_p = _att_fwd(*qkv[p], bias, p, r, "att_fwd_r%d" % r)
        o_pats.append(_as_pattern(o_p, r))
        lse_pats.append(_as_pattern(lse_p, r))
    mixed, o_att, l1, l4, l16 = _post_fwd(o_dn, proj, o_pats, lse_pats, dnw_t, bd128)
    dy, loss_blk, dmixed, d_w_out = _out_proj(x, mixed, w_out_bf, tgt)

    do_dn, dz, dgate, do1, do4, do16, dl1, dl4, dl16, d_dnw = _post_bwd(dmixed, o_dn, proj, o_att, dnw_t, bd128)
    side = [(do1, l1, dl1), (_flat(do4), _flat(l4), _flat(dl4)), (_flat(do16), _flat(l16), _flat(dl16))]
    dq_pats, dk_pats, dv_pats, ds_accs = [], [], [], []
    for p, r in enumerate(rs):
        dq_p, dk_p, dv_p, ds_p = _att_bwd(*qkv[p], *side[p], bias, p, r, "att_bwd_r%d" % r)
        dq_pats.append(_as_pattern(dq_p, r))
        dk_pats.append(_as_pattern(dk_p, r))
        dv_pats.append(_as_pattern(dv_p, r))
        ds_accs.append(ds_p)
    dq_att, dk_att, dv_att, d_qw, d_kw = _att_pre_bwd(dq_pats, dk_pats, dv_pats, proj, qw_t, kw_t, bd64)
    du, dqd, dkt, dw, dattn, dgl = _dn_scan_bwd(do_dn, sp, qd, kt, w, vn, attn, gl)
    dqn, dkn, dv_dn, dbg = _dn_prep_bwd(qn, kn, v_dn, bg, bgt, tri, sel, t_inv, attn, u, w, du, dw, dqd, dkt, dattn,
                                        dgl)
    dyc, dba, dsm = _dn_pre_bwd(dqn, dkn, dv_dn, dbg, proj, conv_w8, alog_l, dtb_l)
    d_qkv_dn, d_conv8 = _conv_bwd(dyc, proj, conv_w8)
    pieces = (d_qkv_dn, dz, dq_att, dk_att, dv_att, dgate, dba)
    d_w_in_t = _in_bwd_dw(pieces, x, norm_w)
    last = functools.partial(_in_bwd_dx, pieces, w_bf, x, dy, norm_w, ds_accs, buckets)

    grads = dict(
        w_in_t=d_w_in_t,
        conv_w=d_conv8[0:CONV_W, :],
        a_log=dsm[0:1, DN_HEADS:2 * DN_HEADS],
        dt_bias=dsm[1:2, DN_HEADS:2 * DN_HEADS],
        dn_norm_w=d_dnw[0:1, :],
        q_norm_w=d_qw[0:1, :].reshape(ATT_HEADS, ATT_HD),
        k_norm_w=d_kw[0:1, :].reshape(ATT_HEADS, ATT_HD),
        w_out=d_w_out,
    )
    return loss_blk[0, 0], last, grads


MESH_ID = pl.DeviceIdType.MESH
ANY = pl.BlockSpec(memory_space=pl.ANY)


def _position():
    return lax.axis_index("x"), lax.axis_index("y"), lax.axis_index("c")


def _other_chips(x, y):
    return [(1 - x, y), (x, 1 - y), (1 - x, 1 - y)]


SHARD_PAD = 1040
WIN = 528


def _row_split(n):
    return (n // 2) // 128 * 128


def _part(ref, cc):
    n = ref.shape[0]
    sp = _row_split(n)
    return ref.at[pl.ds(0, sp)] if cc == 0 else ref.at[pl.ds(sp, n - sp)]


def _halves(ref):
    n = ref.shape[0]
    sp = (n // 2) // 16 * 16
    return ref.at[pl.ds(0, sp)], ref.at[pl.ds(sp, n - sp)]


def _gather_weights(wt_s, w_out_s, conv_s, rel_bias, buckets):
    def body(a_ref, b_ref, c_ref, rb_ref, bk_ref, ga_ref, gb_ref, gc_ref, bias_ref, send_sems, recv_sems, loc_sems,
             a_vmem, b_vmem):
        x, y, c = _position()
        me = 2 * x + y
        sib = (x, y, 1 - c)
        big = ((a_ref, ga_ref), (b_ref, gb_ref))
        stage_in = [pltpu.make_async_copy(a_ref, a_vmem, loc_sems.at[0]),
                    pltpu.make_async_copy(b_ref, b_vmem, loc_sems.at[1])]
        local = [pltpu.make_async_copy(a_vmem, ga_ref.at[me], loc_sems.at[0]),
                 pltpu.make_async_copy(b_vmem, gb_ref.at[me], loc_sems.at[1]),
                 pltpu.make_async_copy(c_ref, gc_ref.at[me], loc_sems.at[2])]
        for cp in stage_in:
            cp.start()
        local[2].start()
        for cp in stage_in:
            cp.wait()
        for cp in local[:2]:
            cp.start()
        others = _other_chips(x, y)

        def exchange(cc):
            nbr = others[:2]
            dg = others[2]
            pending = []

            def copy(k, src, dst, to):
                cp = pltpu.make_async_remote_copy(src_ref=src, dst_ref=dst, send_sem=send_sems.at[k],
                                                  recv_sem=recv_sems.at[k], device_id=to, device_id_type=MESH_ID)
                pending.append(cp)
                cp.start()

            def landed(k, dst):
                pltpu.make_async_remote_copy(src_ref=dst, dst_ref=dst, send_sem=send_sems.at[k],
                                             recv_sem=recv_sems.at[k], device_id=sib, device_id_type=MESH_ID).wait_recv()

            def slot(dst, chip, c_part):
                return _part(dst.at[2 * chip[0] + chip[1]], c_part)

            for i, n in enumerate(nbr):
                for t, (src, dst) in enumerate(big):
                    copy(2 * i + t, _part(src, cc), _part(dst.at[me], cc), (*n, c))
            for j, chip in enumerate(others):
                copy(4 + j, c_ref, gc_ref.at[me], (*chip, c))
            _bias_table(rb_ref, bk_ref, bias_ref)
            for i, n in enumerate(nbr):
                for t, (_, dst) in enumerate(big):
                    got = slot(dst, n, cc)
                    landed(2 * i + t, got)
                    half = _halves(got)[1 - i]
                    copy(7 + 2 * (1 - i) + t, half, half, (*nbr[1 - i], c))
                    copy(11 + 2 * i + t, got, got, sib)
            for i, n in enumerate(nbr):
                for t, (_, dst) in enumerate(big):
                    half = _halves(slot(dst, dg, cc))[i]
                    landed(7 + 2 * i + t, half)
                    copy(15 + 2 * i + t, half, half, sib)
            for j, chip in enumerate(others):
                landed(4 + j, gc_ref.at[2 * chip[0] + chip[1]])
            for i, n in enumerate(nbr):
                for t, (_, dst) in enumerate(big):
                    landed(11 + 2 * i + t, slot(dst, n, 1 - cc))
                    landed(15 + 2 * i + t, _halves(slot(dst, dg, 1 - cc))[i])
            for cp in pending:
                cp.wait_send()

        for cc in (0, 1):
            pl.when(c == cc)(functools.partial(exchange, cc))
        for cp in local:
            cp.wait()

    srcs = (wt_s, w_out_s, conv_s)
    n_sem = 19
    return pl.pallas_call(
        body, in_specs=[ANY] * 3 + [pl.BlockSpec(memory_space=pltpu.SMEM), pl.BlockSpec(memory_space=pltpu.VMEM)],
        out_specs=[ANY] * 3 + [pl.BlockSpec(memory_space=pltpu.VMEM)],
        out_shape=[_sds((N_CHIPS,) + a.shape, a.dtype) for a in srcs]
        + [_sds((len(PATTERNS), ATT_HEADS, BLK, 2 * BLK))],
        scratch_shapes=[pltpu.SemaphoreType.DMA((n_sem,)), pltpu.SemaphoreType.DMA((n_sem,)),
                        pltpu.SemaphoreType.DMA((3,)), pltpu.VMEM(wt_s.shape, wt_s.dtype),
                        pltpu.VMEM(w_out_s.shape, w_out_s.dtype)],
        compiler_params=_cp(), name="gather_weights")(*srcs, rel_bias, buckets)


N_DEV = 8
PEER_FLIPS = [(dx, dy, dc) for dx in (0, 1) for dy in (0, 1) for dc in (0, 1)][1:]


def _small_copies(s_ref, rs_ref, send_sems, recv_sems):
    x, y, c = _position()
    dev = 4 * x + 2 * y + c
    sends, recvs = [], []
    for f, (dx, dy, dc) in enumerate(PEER_FLIPS):
        peer = (x ^ dx, y ^ dy, c ^ dc)
        sends.append(pltpu.make_async_remote_copy(
            src_ref=s_ref, dst_ref=rs_ref.at[dev], send_sem=send_sems.at[f], recv_sem=recv_sems.at[f],
            device_id=peer, device_id_type=MESH_ID))
        recvs.append(pltpu.make_async_remote_copy(
            src_ref=s_ref, dst_ref=rs_ref.at[4 * peer[0] + 2 * peer[1] + peer[2]], send_sem=send_sems.at[f],
            recv_sem=recv_sems.at[f], device_id=peer, device_id_type=MESH_ID))
    return sends, recvs


def _fill_parts(h_in, r_in, h_out, r_out, small, rows_in, rows_out):
    def body(ha_ref, ra_ref, hb_ref, rb_ref, s_ref, fa_ref, fb_ref, rs_ref, send_sems, recv_sems, loc_sems,
             small_send, small_recv, a_vmem, b_vmem, ra_vmem, rb_vmem):
        x, y, c = _position()
        sib = (x, y, 1 - c)
        chip = 2 * x + y
        pairs = ((a_vmem, fa_ref), (b_vmem, fb_ref))
        own_small = pltpu.make_async_copy(s_ref, rs_ref.at[4 * x + 2 * y + c], loc_sems.at[2])
        own_small.start()
        small_sends, small_recvs = _small_copies(s_ref, rs_ref, small_send, small_recv)
        for cp in small_sends:
            cp.start()
        stage_in = [pltpu.make_async_copy(ha_ref.at[chip], a_vmem, loc_sems.at[0]),
                    pltpu.make_async_copy(hb_ref.at[chip], b_vmem, loc_sems.at[1]),
                    pltpu.make_async_copy(ra_ref, ra_vmem, loc_sems.at[3]),
                    pltpu.make_async_copy(rb_ref, rb_vmem, loc_sems.at[4])]
        for cp in stage_in:
            cp.start()
        for cp in stage_in:
            cp.wait()
        for tot, recv in ((a_vmem, ra_vmem), (b_vmem, rb_vmem)):
            acc = tot[...]
            for j in range(3):
                acc = acc + recv[j].astype(F32)
            tot[...] = acc

        def fill(cc):
            mine = [_part(dst, cc) for _, dst in pairs]
            srcs = [src.at[pl.ds(0, m.shape[0])] for src, m in zip((a_vmem, b_vmem), mine)]
            local = [pltpu.make_async_copy(s, m, loc_sems.at[t]) for t, (s, m) in enumerate(zip(srcs, mine))]
            sends = [pltpu.make_async_remote_copy(src_ref=s, dst_ref=m, send_sem=send_sems.at[t],
                                                  recv_sem=recv_sems.at[t], device_id=sib, device_id_type=MESH_ID)
                     for t, (s, m) in enumerate(zip(srcs, mine))]
            for cp in local + sends:
                cp.start()
            for t, (_, dst) in enumerate(pairs):
                theirs = _part(dst, 1 - cc)
                pltpu.make_async_remote_copy(src_ref=theirs, dst_ref=theirs, send_sem=send_sems.at[t],
                                             recv_sem=recv_sems.at[t], device_id=sib, device_id_type=MESH_ID).wait_recv()
            for cp in sends:
                cp.wait_send()
            for cp in local:
                cp.wait()

        for cc in (0, 1):
            pl.when(c == cc)(functools.partial(fill, cc))
        for cp in small_recvs:
            cp.wait_recv()
        for cp in small_sends:
            cp.wait_send()
        own_small.wait()

    n_peer = len(PEER_FLIPS)
    return pl.pallas_call(
        body, in_specs=[ANY] * 5, out_specs=[ANY] * 3,
        out_shape=[_sds((rows_in, D_MODEL)), _sds((rows_out, D_MODEL)), _sds((N_DEV,) + small.shape, small.dtype)],
        scratch_shapes=[pltpu.SemaphoreType.DMA((2,)), pltpu.SemaphoreType.DMA((2,)), pltpu.SemaphoreType.DMA((5,)),
                        pltpu.SemaphoreType.DMA((n_peer,)), pltpu.SemaphoreType.DMA((n_peer,)),
                        pltpu.VMEM(h_in.shape[1:], F32), pltpu.VMEM(h_out.shape[1:], F32),
                        pltpu.VMEM(r_in.shape, r_in.dtype), pltpu.VMEM(r_out.shape, r_out.dtype)],
        compiler_params=_cp(), name="fill_parts")(h_in, r_in, h_out, r_out, small)


def _chip_sums(u_in, u_out, win_in, win_out):
    wins = (win_in, win_out)

    def body(a_ref, b_ref, ha_ref, hba_ref, hb_ref, hbb_ref, send_sems, recv_sems, loc_sems, out_sems,
             mine_a, theirs_a, sum_a, sumb_a, mine_b, theirs_b, sum_b, sumb_b):
        x, y, c = _position()
        sib = (x, y, 1 - c)
        groups = ((a_ref, mine_a, theirs_a, sum_a, sumb_a, ha_ref, hba_ref, win_in),
                  (b_ref, mine_b, theirs_b, sum_b, sumb_b, hb_ref, hbb_ref, win_out))
        loads, sends = [], []
        for t, (src, mine, theirs, _, _, _, _, win) in enumerate(groups):
            split = _row_split(src.shape[1])
            for k in range(N_CHIPS):
                n = N_CHIPS * t + k
                loads.append(pltpu.make_async_copy(
                    src.at[k, pl.ds(pl.multiple_of(c * split, split), win), :], mine.at[k], loc_sems.at[n]))
                sends.append(pltpu.make_async_remote_copy(
                    src_ref=src.at[k, pl.ds(pl.multiple_of((1 - c) * split, split), win), :], dst_ref=theirs.at[k],
                    send_sem=send_sems.at[n], recv_sem=recv_sems.at[n], device_id=sib, device_id_type=MESH_ID))
        for cp in sends + loads:
            cp.start()
        stores = []
        for t, (_, mine, theirs, tot, totb, h_out, hb_out, _) in enumerate(groups):
            for k in range(N_CHIPS):
                n = N_CHIPS * t + k
                loads[n].wait()
                sends[n].wait_recv()
                val = mine[k] + theirs[k]
                tot[k] = val
                totb[k] = val.astype(BF16)
                stores += [pltpu.make_async_copy(tot.at[k], h_out.at[k], out_sems.at[2 * n]),
                           pltpu.make_async_copy(totb.at[k], hb_out.at[k], out_sems.at[2 * n + 1])]
                stores[-2].start()
                stores[-1].start()
        for cp in sends:
            cp.wait_send()
        for cp in stores:
            cp.wait()

    shapes = [(N_CHIPS, w, D_MODEL) for w in wins]
    n_cp = 2 * N_CHIPS
    vmem = []
    for shp in shapes:
        vmem += [pltpu.VMEM(shp, F32), pltpu.VMEM(shp, F32), pltpu.VMEM(shp, F32), pltpu.VMEM(shp, BF16)]
    return pl.pallas_call(
        body, in_specs=[ANY] * 2, out_specs=[ANY] * 4,
        out_shape=[_sds(shapes[0]), _sds(shapes[0], BF16), _sds(shapes[1]), _sds(shapes[1], BF16)],
        scratch_shapes=[pltpu.SemaphoreType.DMA((n_cp,)), pltpu.SemaphoreType.DMA((n_cp,)),
                        pltpu.SemaphoreType.DMA((n_cp,)), pltpu.SemaphoreType.DMA((2 * n_cp,))] + vmem,
        compiler_params=_cp(), name="chip_sums")(u_in, u_out)


SMALL_LAYOUT = (("norm_w", 1024), ("conv_w", 6144), ("a_log", 128), ("dt_bias", 128), ("dn_norm_w", 128),
                ("q_norm_w", 512), ("k_norm_w", 512), ("rel_bias", 256), ("loss", 128))
SMALL_TOTAL = sum(n for _, n in SMALL_LAYOUT)


def _small_offset(name):
    off = 0
    for n, size in SMALL_LAYOUT:
        if n == name:
            return off
        off += size
    raise KeyError(name)


def _pack_small(grads):
    parts = []
    for name, size in SMALL_LAYOUT:
        flat = grads[name].reshape(1, -1)
        parts.append(jnp.pad(flat, ((0, 0), (0, size - flat.shape[1]))))
    return jnp.concatenate(parts, axis=1)


def _sum_small(rows):
    n_dev = rows.shape[0]
    q_off = _small_offset("q_norm_w")
    k_off = _small_offset("k_norm_w")

    def body(r_ref, tot_ref, qk_ref):
        tot = r_ref[0:1, :]
        for d in range(1, n_dev):
            tot = tot + r_ref[d:d + 1, :]
        tot_ref[...] = tot
        for row, off in ((0, q_off), (1, k_off)):
            s4 = tot[:, off:off + 128] + tot[:, off + 128:off + 256] + tot[:, off + 256:off + 384] \
                + tot[:, off + 384:off + 512]
            qk_ref[row:row + 1, :] = s4 + pltpu.roll(s4, ATT_HD, 1)

    return pl.pallas_call(
        body, in_specs=[pl.BlockSpec(memory_space=pltpu.VMEM)],
        out_specs=[pl.BlockSpec(memory_space=pltpu.VMEM)] * 2,
        out_shape=[_sds((1, SMALL_TOTAL)), _sds((2, 128))],
        compiler_params=_cp(), name="sum_small")(rows)


def _adamw_math(w, g, m, v):
    m = ADAM_B1 * m + (1.0 - ADAM_B1) * g
    v = ADAM_B2 * v + (1.0 - ADAM_B2) * (g * g)
    m_hat = m / (1.0 - ADAM_B1 ** ADAM_STEP)
    v_hat = v / (1.0 - ADAM_B2 ** ADAM_STEP)
    delta = -ADAM_LR * (m_hat / (jnp.sqrt(v_hat) + ADAM_EPS) + ADAM_WD * w)
    return delta, m, v


def _adamw_big(g, w, m, v, name):
    rows, cols = w.shape
    tr = 128

    def body(g_ref, w_ref, m_ref, v_ref, go_ref, d_ref, nm_ref, nv_ref):
        g = g_ref[...]
        go_ref[...] = g
        d_ref[...], nm_ref[...], nv_ref[...] = _adamw_math(w_ref[...], g, m_ref[...], v_ref[...])

    blk = pl.BlockSpec((tr, cols), lambda i: (i, 0))
    return pl.pallas_call(
        body, grid=(pl.cdiv(rows, tr),), in_specs=[blk] * 4, out_specs=[blk] * 4,
        out_shape=[_sds((rows, cols))] * 4, compiler_params=_cp(("parallel",)), name=name)(g, w, m, v)


def _adamw_rows(g, w, m, v, name):
    rows, cols = w.shape
    tr = 128

    def body(g_ref, w_ref, m_ref, v_ref, go_ref, d_ref, nm_ref, nv_ref, s_g, s_d, s_m, s_v):
        g = g_ref[...]
        s_g[...] = g
        s_d[...], s_m[...], s_v[...] = _adamw_math(w_ref[...], g, m_ref[...], v_ref[...])
        for i in range(tr):
            for scr, out in ((s_g, go_ref), (s_d, d_ref), (s_m, nm_ref), (s_v, nv_ref)):
                out[i] = scr[i:i + 1, :]

    blk = pl.BlockSpec((tr, cols), lambda i: (i, 0))
    oblk = pl.BlockSpec((tr, 1, cols), lambda i: (i, 0, 0))
    return pl.pallas_call(
        body, grid=(pl.cdiv(rows, tr),), in_specs=[blk] * 4, out_specs=[oblk] * 4,
        out_shape=[_sds((rows, 1, cols))] * 4, scratch_shapes=[pltpu.VMEM((tr, cols), F32)] * 4,
        compiler_params=_cp(("parallel",)), name=name)(g, w, m, v)


def _adamw_small(w, g, m, v, name):
    def body(w_ref, g_ref, m_ref, v_ref, d_ref, nm_ref, nv_ref):
        d_ref[...], nm_ref[...], nv_ref[...] = _adamw_math(w_ref[...], g_ref[...], m_ref[...], v_ref[...])

    vm = pl.BlockSpec(memory_space=pltpu.VMEM)
    return pl.pallas_call(body, in_specs=[vm] * 4, out_specs=[vm] * 3, out_shape=[_sds(w.shape)] * 3,
                          compiler_params=_cp(), name=name)(w, g, m, v)


WEIGHTS = ("norm_w", "w_in", "conv_w", "a_log", "dt_bias", "dn_norm_w", "q_norm_w", "k_norm_w", "rel_bias", "w_out")


def kernel(x, norm_w, w_in, conv_w, a_log, dt_bias, dn_norm_w, q_norm_w, k_norm_w, rel_bias, w_out, loss_target, m_norm_w, m_w_in, m_conv_w, m_a_log, m_dt_bias, m_dn_norm_w, m_q_norm_w, m_k_norm_w, m_rel_bias, m_w_out, v_norm_w, v_w_in, v_conv_w, v_a_log, v_dt_bias, v_dn_norm_w, v_q_norm_w, v_k_norm_w, v_rel_bias, v_w_out):
    xi, yi, _ = _position()
    chip = 2 * xi + yi
    w_loc = dict(norm_w=norm_w, w_in=w_in[0].T, conv_w=conv_w[0], a_log=a_log, dt_bias=dt_bias, dn_norm_w=dn_norm_w,
                 q_norm_w=q_norm_w, k_norm_w=k_norm_w, rel_bias=rel_bias, w_out=w_out[0])
    m_loc = dict(norm_w=m_norm_w, w_in=m_w_in[0].T, conv_w=m_conv_w[0], a_log=m_a_log, dt_bias=m_dt_bias,
                 dn_norm_w=m_dn_norm_w, q_norm_w=m_q_norm_w, k_norm_w=m_k_norm_w, rel_bias=m_rel_bias,
                 w_out=m_w_out[0])
    v_loc = dict(norm_w=v_norm_w, w_in=v_w_in[0].T, conv_w=v_conv_w[0], a_log=v_a_log, dt_bias=v_dt_bias,
                 dn_norm_w=v_dn_norm_w, q_norm_w=v_q_norm_w, k_norm_w=v_k_norm_w, rel_bias=v_rel_bias,
                 w_out=v_w_out[0])

    wt_pad = jnp.pad(w_loc["w_in"].astype(BF16), ((0, SHARD_PAD - D_SHARD), (0, 0)))
    g_in, g_out, g_conv, bias = _gather_weights(wt_pad, w_loc["w_out"].astype(BF16), w_loc["conv_w"], rel_bias,
                                                _bucket_tables())
    wt_full = _pack_w(g_in)
    w_out_full = g_out.reshape(D_MODEL, D_MODEL)
    conv_full = g_conv.transpose(1, 0, 2).reshape(CONV_W, 3 * D_DN)

    loss_local, last_kernel, grads = _local_step(x[0], loss_target[0], norm_w, wt_full, conv_full, a_log, dt_bias,
                                                 dn_norm_w, q_norm_w, k_norm_w, bias, w_out_full)
    grads["loss"] = loss_local

    u_in = _unpack_w(grads["w_in_t"], SHARD_PAD)
    u_out = grads["w_out"].reshape(N_CHIPS, D_MODEL // N_CHIPS, D_MODEL)
    win_out = u_out.shape[1] // 2
    h_in, hb_in, h_out, hb_out = _chip_sums(u_in, u_out, WIN, win_out)
    grad_x, d_nw8, grads["rel_bias"], r_in, r_out = last_kernel(chip_sums=(hb_in, hb_out))
    grads["norm_w"] = d_nw8[0:1, :]
    small = _pack_small(grads)
    full_in, full_out, r_small = _fill_parts(h_in, r_in, h_out, r_out, small, SHARD_PAD, u_out.shape[1])
    tot_small, qk = _sum_small(r_small.reshape(8, SMALL_TOTAL))

    def small_grad(name, n):
        off = _small_offset(name)
        return tot_small[:, off:off + n]

    loss = small_grad("loss", 1).reshape(())
    conv_all = small_grad("conv_w", CONV_W * 3 * D_DN).reshape(CONV_W, 3 * D_DN)
    g_small = dict(
        norm_w=small_grad("norm_w", D_MODEL),
        conv_w=lax.dynamic_slice_in_dim(conv_all, chip * (3 * D_DN // N_CHIPS), 3 * D_DN // N_CHIPS, axis=1),
        a_log=small_grad("a_log", DN_HEADS),
        dt_bias=small_grad("dt_bias", DN_HEADS),
        dn_norm_w=small_grad("dn_norm_w", DN_HD),
        q_norm_w=qk[0:1, 0:ATT_HD],
        k_norm_w=qk[1:2, 0:ATT_HD],
        rel_bias=small_grad("rel_bias", ATT_HEADS * N_BUCKETS).reshape(ATT_HEADS, N_BUCKETS),
    )

    out_g, out_d, out_m, out_v = {}, {}, {}, {}
    out_g["w_in"], out_d["w_in"], out_m["w_in"], out_v["w_in"] = _adamw_rows(
        full_in, w_loc["w_in"], m_loc["w_in"], v_loc["w_in"], "adamw_w_in")
    out_g["w_out"], out_d["w_out"], out_m["w_out"], out_v["w_out"] = _adamw_big(
        full_out, w_loc["w_out"], m_loc["w_out"], v_loc["w_out"], "adamw_w_out")
    for name in g_small:
        out_g[name] = g_small[name]
        out_d[name], out_m[name], out_v[name] = _adamw_small(w_loc[name], g_small[name], m_loc[name], v_loc[name],
                                                             "adamw_" + name)
    for d in (out_g, out_d, out_m, out_v):
        d["w_in"] = d["w_in"].transpose(1, 2, 0)
        for name in ("conv_w", "w_out"):
            d[name] = d[name][None]
    return (loss, grad_x[None], *[out_g[n] for n in WEIGHTS], *[out_d[n] for n in WEIGHTS],
            *[out_m[n] for n in WEIGHTS], *[out_v[n] for n in WEIGHTS])
```

```python
import functools
import math

import numpy as np
import jax
import jax.numpy as jnp
from jax import lax
from jax.experimental import pallas as pl
from jax.experimental.pallas import tpu as pltpu

F32 = jnp.float32
BF16 = jnp.bfloat16
HI = lax.Precision.HIGHEST

D_MODEL = 1024
D_DN = 512
DN_HEADS = 4
DN_HD = 128
CONV_W = 4
CHUNK = 64
D_ATT = 512
ATT_HEADS = 8
ATT_HD = 64
PATTERNS = ((128, 1), (512, 4), (2048, 16))
N_BUCKETS = 32
MAX_DISTANCE = 2048
D_IN = 4 * D_DN + 2 * DN_HEADS + 4 * D_ATT
D_IN_PAD = 4224
EPS = 1e-6
BLK = 128
NEG = -1e30
N_CHIPS = 4

ADAM_LR = 0.001
ADAM_B1 = 0.9
ADAM_B2 = 0.999
ADAM_EPS = 1e-08
ADAM_WD = 0.01
ADAM_STEP = 10

VMEM_LIMIT = 56 * 1024 * 1024

COL_Z = 3
COL_ATT_Q = 4
COL_ATT_K = 5
COL_ATT_V = 6
COL_GATE = 7
COL_BA_128 = 32


def _cp(sem=None):
    if sem is None:
        return pltpu.CompilerParams(vmem_limit_bytes=VMEM_LIMIT)
    return pltpu.CompilerParams(dimension_semantics=sem, vmem_limit_bytes=VMEM_LIMIT)


def _sds(shape, dtype=F32):
    return jax.ShapeDtypeStruct(shape, dtype)


def _mm(a, b):
    return jnp.dot(a.astype(BF16), b.astype(BF16), preferred_element_type=F32)


def _mm_nt(a, b):
    return lax.dot_general(a.astype(BF16), b.astype(BF16), (((1,), (1,)), ((), ())),
                           preferred_element_type=F32)


def _mm_tn(a, b):
    return lax.dot_general(a.astype(BF16), b.astype(BF16), (((0,), (0,)), ((), ())),
                           preferred_element_type=F32)


def _mmx(a, b):
    return jnp.dot(a, b, precision=HI, preferred_element_type=F32)


def _mmx_nt(a, b):
    return lax.dot_general(a, b, (((1,), (1,)), ((), ())), precision=HI, preferred_element_type=F32)


def _mmx_tn(a, b):
    return lax.dot_general(a, b, (((0,), (0,)), ((), ())), precision=HI, preferred_element_type=F32)


def _dot(a, b):
    return jnp.dot(a, b, preferred_element_type=F32)


def _dot_nt(a, b):
    return lax.dot_general(a, b, (((1,), (1,)), ((), ())), preferred_element_type=F32)


def _dot_tn(a, b):
    return lax.dot_general(a, b, (((0,), (0,)), ((), ())), preferred_element_type=F32)


def _split(a):
    hi = a.astype(BF16)
    return hi, (a - hi.astype(F32)).astype(BF16)


def _mm3(a_s, b_s):
    return _dot(a_s[0], b_s[0]) + _dot(a_s[0], b_s[1]) + _dot(a_s[1], b_s[0])


def _mm3_tn(a_s, b_s):
    return _dot_tn(a_s[0], b_s[0]) + _dot_tn(a_s[0], b_s[1]) + _dot_tn(a_s[1], b_s[0])


def _interleave(gens):
    live = list(gens)
    while live:
        nxt = []
        for g in live:
            try:
                next(g)
                nxt.append(g)
            except StopIteration:
                pass
        live = nxt


def _segsum(x, bd):
    hi = x.astype(BF16)
    r1 = x - hi.astype(F32)
    mid = r1.astype(BF16)
    lo = (r1 - mid.astype(F32)).astype(BF16)
    return (jnp.dot(hi, bd, preferred_element_type=F32) + jnp.dot(mid, bd, preferred_element_type=F32)
            + jnp.dot(lo, bd, preferred_element_type=F32))


def _compact_mat(seg, n=512):
    slot = 128 * seg // n
    src = np.arange(n)[:, None]
    dst = np.arange(128)[None, :]
    return jnp.asarray((src // seg == dst // slot).astype(np.float32), dtype=BF16)


def _seg_mean(x, cm, seg):
    hi, lo = _split(x)
    return (_dot(hi, cm) + _dot(lo, cm)) * (1.0 / seg)


def _seg_expand(c, cm, seg):
    hi, lo = _split(c)
    return (_dot_nt(hi, cm) + _dot_nt(lo, cm)) * (cm.shape[0] / (128.0 * seg))


def _seg_rstd(x, cm, seg):
    return _seg_expand(lax.rsqrt(_seg_mean(x * x, cm, seg) + EPS), cm, seg)


def _sigmoid(x):
    return 1.0 / (1.0 + jnp.exp(-x))


def _silu_grad(x, s):
    return s * (1.0 + x * (1.0 - s))


def _tri_incl():
    i = np.arange(CHUNK)
    return jnp.asarray((i[:, None] >= i[None, :]).astype(np.float32))


def _t5_bucket(dist):
    max_exact = N_BUCKETS // 2
    d = np.maximum(dist, 1).astype(np.float64)
    large = max_exact + (np.log(d / max_exact) / math.log(MAX_DISTANCE / max_exact)
                         * (N_BUCKETS - max_exact)).astype(np.int32)
    large = np.minimum(large, N_BUCKETS - 1)
    return np.where(dist < max_exact, dist, large).astype(np.int32)


def _bucket_tables():
    qi = np.arange(BLK)[:, None]
    kj = np.arange(2 * BLK)[None, :]
    step = qi - kj + BLK
    return jnp.asarray(np.stack([_t5_bucket(np.clip(step, 0, None) * r) for _, r in PATTERNS]))


def _in_proj(x, norm_w, wt_bf):
    s = x.shape[0]
    tm = 512

    def body(x_ref, nw_ref, w_ref, o_ref):
        xv = x_ref[...]
        rstd = lax.rsqrt(jnp.mean(xv * xv, axis=-1, keepdims=True) + EPS)
        h = (xv * rstd * nw_ref[...]).astype(BF16)
        o_ref[...] = _dot_nt(h, w_ref[...])

    return pl.pallas_call(
        body, grid=(s // tm,),
        in_specs=[pl.BlockSpec((tm, D_MODEL), lambda i: (i, 0)),
                  pl.BlockSpec((1, D_MODEL), lambda i: (0, 0)),
                  pl.BlockSpec((D_IN_PAD, D_MODEL), lambda i: (0, 0))],
        out_specs=pl.BlockSpec((tm, D_IN_PAD), lambda i: (i, 0)),
        out_shape=_sds((s, D_IN_PAD)), compiler_params=_cp(("parallel",)), name="in_proj")(x, norm_w, wt_bf)


def _conv_group(cur, halo, w_ref, c):
    rows = cur.shape[0]
    lanes = slice(128 * c, 128 * c + 128)
    xcat = jnp.concatenate([halo, cur], axis=0)
    y = cur * w_ref[CONV_W - 1:CONV_W, lanes]
    for k in range(1, CONV_W):
        sh = pltpu.roll(xcat, k, 0)[8:8 + rows]
        y = y + sh * w_ref[CONV_W - 1 - k:CONV_W - k, lanes]
    return y


def _beta_g(ba, alog_l, dtb_l):
    lane = lax.broadcasted_iota(jnp.int32, ba.shape, 1)
    sig_b = _sigmoid(ba)
    t = ba + dtb_l
    softplus = jnp.maximum(t, 0.0) + jnp.log(1.0 + jnp.exp(-jnp.abs(t)))
    nega = -jnp.exp(alog_l)
    g = nega * softplus
    out = jnp.where(lane < DN_HEADS, sig_b, jnp.where(lane < 2 * DN_HEADS, g, 0.0))
    return out, lane, sig_b, t, nega, g


def _dn_pre(proj, conv_w8, alog_l, dtb_l):
    s = proj.shape[0]
    tr = 512
    nh = tr // 8

    def body(u_ref, halo_ref, ba_ref, w_ref, al_ref, dt_ref, q_ref, k_ref, v_ref, bg_ref):
        i = pl.program_id(0)
        keep = (i > 0).astype(F32)
        for c in range(12):
            lanes = slice(128 * c, 128 * c + 128)
            y = _conv_group(u_ref[:, lanes], halo_ref[:, lanes] * keep, w_ref, c)
            sv = y * _sigmoid(y)
            if c < 8:
                rs = lax.rsqrt(jnp.sum(sv * sv, axis=1, keepdims=True) + EPS)
                n = sv * rs
                if c < 4:
                    q_ref[:, lanes] = n * (DN_HD ** -0.5)
                else:
                    k_ref[:, slice(128 * (c - 4), 128 * (c - 3))] = n
            else:
                v_ref[:, slice(128 * (c - 8), 128 * (c - 7))] = sv
        bg_ref[...] = _beta_g(ba_ref[...], al_ref[...], dt_ref[...])[0]

    return pl.pallas_call(
        body, grid=(s // tr,),
        in_specs=[pl.BlockSpec((tr, 1536), lambda i: (i, 0)),
                  pl.BlockSpec((8, 1536), lambda i: (jnp.maximum(i * nh - 1, 0), 0)),
                  pl.BlockSpec((tr, 128), lambda i: (i, COL_BA_128)),
                  pl.BlockSpec((8, 1536), lambda i: (0, 0)),
                  pl.BlockSpec((1, 128), lambda i: (0, 0)),
                  pl.BlockSpec((1, 128), lambda i: (0, 0))],
        out_specs=[pl.BlockSpec((tr, 512), lambda i: (i, 0))] * 3 + [pl.BlockSpec((tr, 128), lambda i: (i, 0))],
        out_shape=[_sds((s, 512))] * 3 + [_sds((s, 128))],
        compiler_params=_cp(("parallel",)), name="dn_pre")(proj, proj, proj, conv_w8, alog_l, dtb_l)


CPS = 8
CPS_SCAN = 8
SPLIT_ITERS = 2


def _split3(a):
    hi = a.astype(BF16)
    r1 = a - hi.astype(F32)
    mid = r1.astype(BF16)
    return hi, mid, (r1 - mid.astype(F32)).astype(BF16)


def _lane_select():
    r = np.arange(128)
    return jnp.asarray((r[None, :, None] == np.arange(8)[:, None, None]) * np.ones((1, 1, 128)), dtype=BF16)


def _lane_bcast(a3, sel):
    return _dot(a3[0], sel) + _dot(a3[1], sel) + _dot(a3[2], sel)


def _rowsum_b(z, ones_b):
    hi, lo = _split(z)
    return _dot(hi, ones_b) + _dot(lo, ones_b)


def _chunk_cumsum(bg, bgt, tri):
    return _split3(bg), _split3(_mmx(tri, bg)), _mmx_nt(bgt, tri)


def _chunk_common(bg3, gc3, gc_row, h, sel_ref):
    gcc = _lane_bcast(gc3, sel_ref[DN_HEADS + h])
    beta = _dot(bg3[0], sel_ref[h]) + _dot(bg3[1], sel_ref[h])
    gcr = gc_row[DN_HEADS + h:DN_HEADS + h + 1, :]
    ii = lax.broadcasted_iota(jnp.int32, (CHUNK, CHUNK), 0)
    jj = lax.broadcasted_iota(jnp.int32, (CHUNK, CHUNK), 1)
    incl = ii >= jj
    strict = ii > jj
    decay = jnp.exp(jnp.where(incl, gcc[:, 0:CHUNK] - gcr, NEG))
    gl = gcc[CHUNK - 1:CHUNK, :]
    return gcc, beta, incl, strict, decay, gl


def _dn_prep(qn, kn, v, bg, bgt, tri, sel):
    s = qn.shape[0]
    nc = s // CHUNK

    def body(q_ref, k_ref, v_ref, bg_ref, bgt_ref, tri_ref, sel_ref,
             u_ref, w_ref, qd_ref, kt_ref, attn_ref, t_ref, gl_ref):
        tri_v = tri_ref[...]
        ii = lax.broadcasted_iota(jnp.int32, (CHUNK, CHUNK), 0)
        jj = lax.broadcasted_iota(jnp.int32, (CHUNK, CHUNK), 1)
        eye = (ii == jj).astype(F32)

        def chain(cc, h, bg3, gc3, gc_row):
            rows = slice(CHUNK * cc, CHUNK * cc + CHUNK)
            lanes = slice(128 * h, 128 * h + 128)
            gcc, beta, incl, strict, decay, gl = _chunk_common(bg3, gc3, gc_row, h, sel_ref)
            yield
            q = q_ref[rows, lanes]
            k = k_ref[rows, lanes]
            vv = v_ref[rows, lanes]
            kb = k * beta
            egc = jnp.exp(gcc)
            a_mat = jnp.where(strict, _mm_nt(kb, k) * decay, 0.0)
            attn_ref[cc, h] = jnp.where(incl, _mm_nt(q, k) * decay, 0.0)
            qd_ref[rows, lanes] = (q * egc).astype(BF16)
            kt_ref[rows, lanes] = (k * jnp.exp(gl - gcc)).astype(BF16)
            gl_ref[cc, h] = jnp.exp(gl)
            yield
            p = -a_mat
            t = eye + p
            for it in range(5):
                if it < SPLIT_ITERS:
                    ps = _split(p)
                    p = _mm3(ps, ps)
                    yield
                    t = t + _mm3(_split(t), _split(p))
                else:
                    p = _mm(p, p)
                    yield
                    t = t + _mm(t, p)
                yield
            t_ref[cc, h] = t
            ts = _split(t)
            u_ref[rows, lanes] = _mm3(ts, _split(vv * beta))
            w_ref[rows, lanes] = _mm3(ts, _split(kb * egc)).astype(BF16)

        gens = []
        for cc in range(CPS):
            bgv = bg_ref[CHUNK * cc:CHUNK * cc + CHUNK, :]
            bg3, gc3, gc_row = _chunk_cumsum(bgv, bgt_ref[cc], tri_v)
            gens += [chain(cc, h, bg3, gc3, gc_row) for h in range(DN_HEADS)]
        _interleave(gens)

    rows_step = CPS * CHUNK
    big = pl.BlockSpec((rows_step, 512), lambda n: (n, 0))
    sq = pl.BlockSpec((CPS, DN_HEADS, CHUNK, CHUNK), lambda n: (n, 0, 0, 0))
    return pl.pallas_call(
        body, grid=(nc // CPS,),
        in_specs=[big, big, big, pl.BlockSpec((rows_step, 128), lambda n: (n, 0)),
                  pl.BlockSpec((CPS, 8, CHUNK), lambda n: (n, 0, 0)),
                  pl.BlockSpec((CHUNK, CHUNK), lambda n: (0, 0)),
                  pl.BlockSpec((8, 128, 128), lambda n: (0, 0, 0))],
        out_specs=[big, big, big, big, sq, sq, pl.BlockSpec((CPS, DN_HEADS, 1, 128), lambda n: (n, 0, 0, 0))],
        out_shape=[_sds((s, 512))] + [_sds((s, 512), BF16)] * 3 + [_sds((nc, DN_HEADS, CHUNK, CHUNK))] * 2
        + [_sds((nc, DN_HEADS, 1, 128))],
        compiler_params=_cp(("parallel",)), name="dn_prep")(qn, kn, v, bg, bgt, tri, sel)


def _dn_scan(u, w, qd, kt, attn, gl):
    s = u.shape[0]
    nc = s // CHUNK

    def body(u_ref, w_ref, qd_ref, kt_ref, attn_ref, gl_ref, o_ref, vn_ref, sp_ref, st_ref):
        n = pl.program_id(0)

        @pl.when(n == 0)
        def _():
            st_ref[...] = jnp.zeros_like(st_ref)

        def chain(cc, h):
            rows = slice(CHUNK * cc, CHUNK * cc + CHUNK)
            lanes = slice(128 * h, 128 * h + 128)
            st = st_ref[h]
            sp_ref[cc, h] = st
            stb = st.astype(BF16)
            ws = _dot(w_ref[rows, lanes].astype(BF16), stb)
            qs = _dot(qd_ref[rows, lanes].astype(BF16), stb)
            yield
            vn = u_ref[rows, lanes] - ws
            vnb = vn.astype(BF16)
            vn_ref[rows, lanes] = vnb
            o_ref[rows, lanes] = qs + _dot(attn_ref[cc, h].astype(BF16), vnb)
            st_ref[h] = st * gl_ref[cc, h] + _dot_tn(kt_ref[rows, lanes].astype(BF16), vnb)

        for cc in range(CPS_SCAN):
            _interleave([chain(cc, h) for h in range(DN_HEADS)])

    big = pl.BlockSpec((CPS_SCAN * CHUNK, 512), lambda n: (n, 0))
    return pl.pallas_call(
        body, grid=(nc // CPS_SCAN,),
        in_specs=[big, big, big, big,
                  pl.BlockSpec((CPS_SCAN, DN_HEADS, CHUNK, CHUNK), lambda n: (n, 0, 0, 0)),
                  pl.BlockSpec((CPS_SCAN, DN_HEADS, 1, 128), lambda n: (n, 0, 0, 0))],
        out_specs=[big, big, pl.BlockSpec((CPS_SCAN, DN_HEADS, DN_HD, DN_HD), lambda n: (n, 0, 0, 0))],
        out_shape=[_sds((s, 512)), _sds((s, 512), BF16), _sds((nc, DN_HEADS, DN_HD, DN_HD))],
        scratch_shapes=[pltpu.VMEM((DN_HEADS, DN_HD, DN_HD), F32)],
        compiler_params=_cp(("arbitrary",)), name="dn_scan")(u, w, qd, kt, attn, gl)


R4 = PATTERNS[1][1]
R16 = PATTERNS[2][1]
TM = 512


def _pattern_spec(r, width=512):
    return pl.BlockSpec((r, TM // r, width), lambda i: (0, i, 0))


def _pattern_shape(s, r, dtype=F32, width=512):
    return _sds((r, s // r, width), dtype)


SLABS = pltpu.VMEM((4, TM, 128), F32)

HEAD_SLOT = 128 // ATT_HEADS


def _head_expand():
    src = np.arange(128)[:, None]
    dst = np.arange(512)[None, :]
    return jnp.asarray((src == (dst // ATT_HD) * HEAD_SLOT).astype(np.float32), dtype=BF16)


def _head_compact():
    src = np.arange(512)[:, None]
    dst = np.arange(128)[None, :]
    return jnp.asarray((src // ATT_HD == dst // HEAD_SLOT).astype(np.float32), dtype=BF16)


def _to_patterns(val, dsts, scr):
    for c in range(val.shape[1] // 128):
        lanes = slice(128 * c, 128 * c + 128)
        scr[c] = val[:, lanes]
        for dst_ref, r in dsts:
            for a in range(r):
                dst_ref[a, :, lanes] = scr[c, pl.ds(a, TM // r, stride=r), :].astype(dst_ref.dtype)


def _from_pattern(src_ref, r, scr):
    n_slab = src_ref.shape[2] // 128
    for c in range(n_slab):
        for a in range(r):
            scr[c, pl.ds(a, TM // r, stride=r), :] = src_ref[a, :, 128 * c:128 * c + 128].astype(F32)
    return jnp.concatenate([scr[c] for c in range(n_slab)], axis=1) if n_slab > 1 else scr[0]


def _att_pre(proj, qw_t, kw_t, bd64):
    s = proj.shape[0]

    def body(q_ref, k_ref, v_ref, qw_ref, kw_ref, bd_ref,
             q1_ref, k1_ref, v1_ref, q4_ref, k4_ref, v4_ref, q16_ref, k16_ref, v16_ref, scr):
        bd = bd_ref[...]
        q = q_ref[...]
        k = k_ref[...]
        qn = q * _seg_rstd(q, bd, ATT_HD) * qw_ref[...] * (ATT_HD ** -0.5)
        kn = k * _seg_rstd(k, bd, ATT_HD) * kw_ref[...]
        q1_ref[...] = qn.astype(BF16)
        k1_ref[...] = kn.astype(BF16)
        v1_ref[...] = v_ref[...].astype(BF16)
        _to_patterns(qn, ((q4_ref, R4), (q16_ref, R16)), scr)
        _to_patterns(kn, ((k4_ref, R4), (k16_ref, R16)), scr)
        _to_patterns(v_ref[...], ((v4_ref, R4), (v16_ref, R16)), scr)

    row = pl.BlockSpec((1, 512), lambda i: (0, 0))
    tok = pl.BlockSpec((TM, 512), lambda i: (i, 0))
    return pl.pallas_call(
        body, grid=(s // TM,),
        in_specs=[pl.BlockSpec((TM, 512), lambda i: (i, COL_ATT_Q)),
                  pl.BlockSpec((TM, 512), lambda i: (i, COL_ATT_K)),
                  pl.BlockSpec((TM, 512), lambda i: (i, COL_ATT_V)),
                  row, row, pl.BlockSpec((512, 128), lambda i: (0, 0))],
        out_specs=[tok] * 3 + [_pattern_spec(R4)] * 3 + [_pattern_spec(R16)] * 3,
        out_shape=[_sds((s, 512), BF16)] * 3 + [_pattern_shape(s, R4, BF16)] * 3 + [_pattern_shape(s, R16, BF16)] * 3,
        scratch_shapes=[SLABS],
        compiler_params=_cp(("parallel",)), name="att_pre")(proj, proj, proj, qw_t, kw_t, bd64)


def _bias_table(rb_ref, bk_ref, o_ref):
    for p in range(len(PATTERNS)):
        bk = bk_ref[p]
        for h in range(ATT_HEADS):
            acc = jnp.zeros((BLK, 2 * BLK), F32)
            for b in range(N_BUCKETS):
                acc = jnp.where(bk == b, rb_ref[h, b], acc)
            o_ref[p, h] = acc


def _bias_grad(ds_refs, bk_ref, o_ref):
    for h in range(ATT_HEADS):
        for b in range(N_BUCKETS):
            tot = jnp.zeros((), F32)
            for p, ds_ref in enumerate(ds_refs):
                tot = tot + jnp.sum(jnp.where(bk_ref[p] == b, ds_ref[h], 0.0))
            o_ref[h, b] = tot


QB_FWD = 2
QB_BWD = 4


def _att_masks(has_prev):
    qi = lax.broadcasted_iota(jnp.int32, (BLK, BLK), 0)
    kj = lax.broadcasted_iota(jnp.int32, (BLK, BLK), 1)
    lane = lax.broadcasted_iota(jnp.int32, (BLK, 2 * ATT_HD), 1)
    return jnp.logical_and(kj >= qi, has_prev), kj <= qi, lane < ATT_HD


def _head_lanes(h):
    half = h % 2
    return slice(ATT_HD * h, ATT_HD * h + ATT_HD), slice(ATT_HD * half, ATT_HD * half + ATT_HD)


def _att_scores(qm, kp2, kc2, bias_h, mask_prev, mask_cur):
    s_prev = jnp.where(mask_prev, _dot_nt(qm, kp2) + bias_h[:, :BLK], NEG)
    s_cur = jnp.where(mask_cur, _dot_nt(qm, kc2) + bias_h[:, BLK:], NEG)
    return s_prev, s_cur


def _att_fwd(q, k, v, bias, p_idx, r, name):
    QB = QB_FWD
    s = q.shape[0]
    nblk = s // BLK
    nseq = nblk // r

    def body(q_ref, kp_ref, kc_ref, vp_ref, vc_ref, b_ref, o_ref, lse_ref):
        j = pl.program_id(0)

        def head(h, rows, masks, q2, kp2, kc2, vp2, vc2):
            mask_prev, mask_cur, lo_half = masks
            out_l, pair_l = _head_lanes(h)
            sel = lo_half if h % 2 == 0 else jnp.logical_not(lo_half)
            qm = jnp.where(sel, q2, jnp.zeros_like(q2))
            s_prev, s_cur = _att_scores(qm, kp2, kc2, b_ref[0, h], mask_prev, mask_cur)
            yield
            m = jnp.maximum(jnp.max(s_prev, axis=1, keepdims=True), jnp.max(s_cur, axis=1, keepdims=True))
            p_prev = jnp.exp(s_prev - m)
            p_cur = jnp.exp(s_cur - m)
            l = jnp.sum(p_prev, axis=1, keepdims=True) + jnp.sum(p_cur, axis=1, keepdims=True)
            yield
            o2 = _dot(p_prev.astype(BF16), vp2) + _dot(p_cur.astype(BF16), vc2)
            o_ref[rows, out_l] = (o2 * (1.0 / l))[:, pair_l].astype(BF16)
            lse_ref[rows, HEAD_SLOT * h:HEAD_SLOT * h + HEAD_SLOT] = jnp.broadcast_to(m + jnp.log(l), (BLK, HEAD_SLOT))

        for sub in range(QB):
            rows = slice(BLK * sub, BLK * sub + BLK)
            before = slice(BLK * (sub - 1), BLK * sub)
            masks = _att_masks(((QB * j + sub) % nseq) != 0)
            gens = []
            for pp in range(ATT_HEADS // 2):
                lanes = slice(128 * pp, 128 * pp + 128)
                kp = kp_ref[:, lanes] if sub == 0 else kc_ref[before, lanes]
                vp = vp_ref[:, lanes] if sub == 0 else vc_ref[before, lanes]
                slabs = (q_ref[rows, lanes], kp, kc_ref[rows, lanes], vp, vc_ref[rows, lanes])
                gens += [head(2 * pp, rows, masks, *slabs), head(2 * pp + 1, rows, masks, *slabs)]
            _interleave(gens)

    cur = pl.BlockSpec((QB * BLK, 512), lambda j: (j, 0))
    prev = pl.BlockSpec((BLK, 512), lambda j: (jnp.maximum(QB * j - 1, 0), 0))
    return pl.pallas_call(
        body, grid=(nblk // QB,),
        in_specs=[cur, prev, cur, prev, cur,
                  pl.BlockSpec((1, ATT_HEADS, BLK, 2 * BLK), lambda j: (p_idx, 0, 0, 0))],
        out_specs=[cur, pl.BlockSpec((QB * BLK, 128), lambda j: (j, 0))],
        out_shape=[_sds((s, 512), BF16), _sds((s, 128))],
        compiler_params=_cp(("parallel",)), name=name)(q, k, k, v, v, bias)


def _post_fwd(o_dn, proj, o_pats, lse_pats, dnw_t, bd128):
    s = o_dn.shape[0]

    def body(o_ref, z_ref, gate_ref, o1_ref, o4_ref, o16_ref, s1_ref, s4_ref, s16_ref, wn_ref, bd_ref, ex_ref,
             mixed_ref, oatt_ref, l1_ref, l4_ref, l16_ref, scr_a, scr_b, scr_c, scr_d):
        o = o_ref[...]
        z = z_ref[...]
        rstd = _seg_rstd(o, bd_ref[...], DN_HD)
        y_dn = o * rstd * wn_ref[...] * (z * _sigmoid(z))
        mixed_ref[:, 0:512] = y_dn.astype(BF16)
        lses = (s1_ref[...], _from_pattern(s4_ref, R4, scr_a), _from_pattern(s16_ref, R16, scr_b))
        m = jnp.maximum(jnp.maximum(lses[0], lses[1]), lses[2])
        tot = jnp.exp(lses[0] - m) + jnp.exp(lses[1] - m) + jnp.exp(lses[2] - m)
        big_l = m + jnp.log(tot)
        l1_ref[...] = big_l
        _to_patterns(big_l, ((l4_ref, R4), (l16_ref, R16)), scr_a)
        ex = ex_ref[...]
        outs = (o1_ref[...], _from_pattern(o4_ref, R4, scr_c), _from_pattern(o16_ref, R16, scr_d))
        acc = jnp.zeros_like(o)
        for lse_p, o_p in zip(lses, outs):
            acc = acc + _lane_bcast(_split3(jnp.exp(lse_p - big_l)), ex) * o_p
        gate = gate_ref[...]
        oatt_ref[...] = acc
        mixed_ref[:, 512:1024] = (acc * (gate * _sigmoid(gate))).astype(BF16)

    blk = pl.BlockSpec((TM, 512), lambda i: (i, 0))
    cblk = pl.BlockSpec((TM, 128), lambda i: (i, 0))
    p4, p16 = _pattern_spec(R4), _pattern_spec(R16)
    c4, c16 = _pattern_spec(R4, 128), _pattern_spec(R16, 128)
    return pl.pallas_call(
        body, grid=(s // TM,),
        in_specs=[blk, pl.BlockSpec((TM, 512), lambda i: (i, COL_Z)),
                  pl.BlockSpec((TM, 512), lambda i: (i, COL_GATE)), blk, p4, p16, cblk, c4, c16,
                  pl.BlockSpec((1, 512), lambda i: (0, 0)), pl.BlockSpec((512, 128), lambda i: (0, 0)),
                  pl.BlockSpec((128, 512), lambda i: (0, 0))],
        out_specs=[pl.BlockSpec((TM, D_MODEL), lambda i: (i, 0)), blk, cblk, c4, c16],
        out_shape=[_sds((s, D_MODEL), BF16), _sds((s, 512)), _sds((s, 128)), _pattern_shape(s, R4, F32, 128),
                   _pattern_shape(s, R16, F32, 128)],
        scratch_shapes=[SLABS] * 4,
        compiler_params=_cp(("parallel",)), name="post_fwd")(o_dn, proj, proj, *o_pats, *lse_pats, dnw_t, bd128,
                                                              _head_expand())


def _out_post(x, mixed, w_out_bf, tgt, o_dn, proj, o_att, dnw_t, bd128):
    s = o_dn.shape[0]
    tm = TM

    def body(x_ref, m_ref, w_ref, t_ref, o_ref, z_ref, gate_ref, oatt_ref, wn_ref, bd128_ref, hc_ref,
             dy_ref, loss_ref, dw_ref, do_ref, dz_ref, dgate_ref, doatt_ref, do4_ref, do16_ref, delta_ref, dl4_ref,
             dl16_ref, dnw_ref, scr):
        i = pl.program_id(0)

        @pl.when(i == 0)
        def _():
            loss_ref[...] = jnp.zeros_like(loss_ref)
            dw_ref[...] = jnp.zeros_like(dw_ref)
            dnw_ref[...] = jnp.zeros_like(dnw_ref)

        err = x_ref[...] + _dot(m_ref[...], w_ref[...]) - t_ref[...]
        d_y = err * (1.0 / D_MODEL)
        dy_ref[...] = d_y
        part = 0.5 * jnp.sum(jnp.mean(err * err, axis=-1, keepdims=True), axis=0, keepdims=True)
        loss_ref[...] = loss_ref[...] + part
        dyb = d_y.astype(BF16)
        dw_ref[...] = dw_ref[...] + _dot_tn(m_ref[...], dyb)
        dm = _dot_nt(dyb, w_ref[...])

        bd128v = bd128_ref[...]
        o = o_ref[...]
        z = z_ref[...]
        wn = wn_ref[...]
        dy = dm[:, 0:512]
        rstd = _seg_rstd(o, bd128v, DN_HD)
        nrm = o * rstd
        sz = _sigmoid(z)
        dz_ref[...] = (dy * nrm * wn * _silu_grad(z, sz)).astype(BF16)
        dn = dy * z * sz
        gw = dn * wn
        do_ref[...] = rstd * (gw - nrm * _seg_expand(_seg_mean(gw * nrm, bd128v, DN_HD), bd128v, DN_HD))
        colsum = jnp.sum(dn * nrm, axis=0, keepdims=True)
        fold = colsum[:, 0:128] + colsum[:, 128:256] + colsum[:, 256:384] + colsum[:, 384:512]
        dnw_ref[...] = dnw_ref[...] + fold
        dya = dm[:, 512:1024]
        gate = gate_ref[...]
        oatt = oatt_ref[...]
        sg = _sigmoid(gate)
        dgate_ref[...] = (dya * oatt * _silu_grad(gate, sg)).astype(BF16)
        doa = dya * gate * sg
        doatt_ref[...] = doa.astype(BF16)
        delta = _segsum(doa * oatt, hc_ref[...])
        delta_ref[...] = delta
        _to_patterns(doa, ((do4_ref, R4), (do16_ref, R16)), scr)
        _to_patterns(delta, ((dl4_ref, R4), (dl16_ref, R16)), scr)

    wide = pl.BlockSpec((tm, D_MODEL), lambda i: (i, 0))
    full = pl.BlockSpec((D_MODEL, D_MODEL), lambda i: (0, 0))
    blk = pl.BlockSpec((tm, 512), lambda i: (i, 0))
    cblk = pl.BlockSpec((tm, 128), lambda i: (i, 0))
    p4, p16 = _pattern_spec(R4), _pattern_spec(R16)
    c4, c16 = _pattern_spec(R4, 128), _pattern_spec(R16, 128)
    return pl.pallas_call(
        body, grid=(s // tm,),
        in_specs=[wide, wide, full, wide, blk,
                  pl.BlockSpec((tm, 512), lambda i: (i, COL_Z)), pl.BlockSpec((tm, 512), lambda i: (i, COL_GATE)),
                  blk, pl.BlockSpec((1, 512), lambda i: (0, 0)), pl.BlockSpec((512, 128), lambda i: (0, 0)),
                  pl.BlockSpec((512, 128), lambda i: (0, 0))],
        out_specs=[wide, pl.BlockSpec((8, 128), lambda i: (0, 0)), full,
                   blk, blk, blk, blk, p4, p16, cblk, c4, c16, pl.BlockSpec((8, 128), lambda i: (0, 0))],
        out_shape=[_sds((s, D_MODEL)), _sds((8, 128)), _sds((D_MODEL, D_MODEL)), _sds((s, 512))]
        + [_sds((s, 512), BF16)] * 3 + [_pattern_shape(s, R4, BF16), _pattern_shape(s, R16, BF16), _sds((s, 128)),
                                        _pattern_shape(s, R4, F32, 128), _pattern_shape(s, R16, F32, 128),
                                        _sds((8, 128))],
        scratch_shapes=[SLABS],
        compiler_params=_cp(("arbitrary",)), name="out_post")(x, mixed, w_out_bf, tgt, o_dn, proj, proj, o_att, dnw_t,
                                                               bd128, _head_compact())


def _att_bwd(q, k, v, do, big_l, delta, bias, p_idx, r, name):
    QB = QB_BWD
    s = q.shape[0]
    nblk = s // BLK
    nseq = nblk // r
    nstep = nblk // QB

    def body(q_ref, kp_ref, kc_ref, vp_ref, vc_ref, do_ref, l_ref, dl_ref, b_ref,
             dq_ref, dk_ref, dv_ref, ds_ref, dkc_ref, dvc_ref):
        j = pl.program_id(0)

        @pl.when(j == 0)
        def _():
            dkc_ref[...] = jnp.zeros_like(dkc_ref)
            dvc_ref[...] = jnp.zeros_like(dvc_ref)
            ds_ref[...] = jnp.zeros_like(ds_ref)

        @pl.when(j < nstep)
        def _():
            def head(h, sub, masks, q2, do2, kp2, kc2, vp2, vc2):
                mask_prev, mask_cur, lo_half = masks
                rows = slice(BLK * sub, BLK * sub + BLK)
                out_l, pair_l = _head_lanes(h)
                sel = lo_half if h % 2 == 0 else jnp.logical_not(lo_half)
                qm = jnp.where(sel, q2, jnp.zeros_like(q2))
                dom = jnp.where(sel, do2, jnp.zeros_like(do2))
                s_prev, s_cur = _att_scores(qm, kp2, kc2, b_ref[0, h], mask_prev, mask_cur)
                dp_prev = _dot_nt(dom, vp2)
                dp_cur = _dot_nt(dom, vc2)
                yield
                lh = l_ref[rows, HEAD_SLOT * h:HEAD_SLOT * h + 1]
                dh = dl_ref[rows, HEAD_SLOT * h:HEAD_SLOT * h + 1]
                p_prev = jnp.exp(s_prev - lh)
                p_cur = jnp.exp(s_cur - lh)
                ds_prev = p_prev * (dp_prev - dh)
                ds_cur = p_cur * (dp_cur - dh)
                ds_ref[h, :, 0:BLK] = ds_ref[h, :, 0:BLK] + ds_prev
                ds_ref[h, :, BLK:2 * BLK] = ds_ref[h, :, BLK:2 * BLK] + ds_cur
                dsb_prev, dsb_cur = ds_prev.astype(BF16), ds_cur.astype(BF16)
                pb_prev, pb_cur = p_prev.astype(BF16), p_cur.astype(BF16)
                yield
                dq_ref[rows, out_l] = (_dot(dsb_prev, kp2) + _dot(dsb_cur, kc2))[:, pair_l].astype(BF16)
                dk_prev = _dot_tn(dsb_prev, q2)[:, pair_l]
                dv_prev = _dot_tn(pb_prev, do2)[:, pair_l]
                if sub == 0:
                    last = slice(BLK * (QB - 1), BLK * QB)
                    dk_ref[last, out_l] = (dkc_ref[last, out_l] + dk_prev).astype(BF16)
                    dv_ref[last, out_l] = (dvc_ref[last, out_l] + dv_prev).astype(BF16)
                else:
                    before = slice(BLK * (sub - 1), BLK * sub)
                    dkc_ref[before, out_l] = dkc_ref[before, out_l] + dk_prev
                    dvc_ref[before, out_l] = dvc_ref[before, out_l] + dv_prev
                yield
                dkc_ref[rows, out_l] = _dot_tn(dsb_cur, q2)[:, pair_l]
                dvc_ref[rows, out_l] = _dot_tn(pb_cur, do2)[:, pair_l]

            done = slice(0, BLK * (QB - 1))
            dk_ref[done, :] = dkc_ref[done, :].astype(BF16)
            dv_ref[done, :] = dvc_ref[done, :].astype(BF16)
            for sub in range(QB):
                rows = slice(BLK * sub, BLK * sub + BLK)
                before = slice(BLK * (sub - 1), BLK * sub)
                masks = _att_masks(((QB * j + sub) % nseq) != 0)
                gens = []
                for pp in range(ATT_HEADS // 2):
                    lanes = slice(128 * pp, 128 * pp + 128)
                    kp = kp_ref[:, lanes] if sub == 0 else kc_ref[before, lanes]
                    vp = vp_ref[:, lanes] if sub == 0 else vc_ref[before, lanes]
                    slabs = (q_ref[rows, lanes], do_ref[rows, lanes], kp, kc_ref[rows, lanes], vp,
                             vc_ref[rows, lanes])
                    gens += [head(2 * pp, sub, masks, *slabs), head(2 * pp + 1, sub, masks, *slabs)]
                _interleave(gens)

        @pl.when(j == nstep)
        def _():
            dk_ref[...] = dkc_ref[...].astype(BF16)
            dv_ref[...] = dvc_ref[...].astype(BF16)

    last_step = nstep - 1
    cur = pl.BlockSpec((QB * BLK, 512), lambda j: (jnp.minimum(j, last_step), 0))
    compact = pl.BlockSpec((QB * BLK, 128), lambda j: (jnp.minimum(j, last_step), 0))
    lag = pl.BlockSpec((QB * BLK, 512), lambda j: (jnp.clip(j - 1, 0, last_step), 0))
    prev = pl.BlockSpec((BLK, 512), lambda j: (jnp.clip(QB * j - 1, 0, nblk - 1), 0))
    return pl.pallas_call(
        body, grid=(nstep + 1,),
        in_specs=[cur, prev, cur, prev, cur, cur, compact, compact,
                  pl.BlockSpec((1, ATT_HEADS, BLK, 2 * BLK), lambda j: (p_idx, 0, 0, 0))],
        out_specs=[cur, lag, lag, pl.BlockSpec((ATT_HEADS, BLK, 2 * BLK), lambda j: (0, 0, 0))],
        out_shape=[_sds((s, 512), BF16)] * 3 + [_sds((ATT_HEADS, BLK, 2 * BLK))],
        scratch_shapes=[pltpu.VMEM((QB * BLK, 512), F32), pltpu.VMEM((QB * BLK, 512), F32)],
        compiler_params=_cp(("arbitrary",)), name=name)(q, k, k, v, v, do, big_l, delta, bias)


def _att_pre_bwd(dq_pats, dk_pats, dv_pats, proj, qw_t, kw_t, bd64):
    s = proj.shape[0]
    tm = TM

    def body(dq1_ref, dq4_ref, dq16_ref, dk1_ref, dk4_ref, dk16_ref, dv1_ref, dv4_ref, dv16_ref,
             q_ref, k_ref, qw_ref, kw_ref, bd_ref,
             dqr_ref, dkr_ref, dvr_ref, dqw_ref, dkw_ref, scr4, scr16):
        i = pl.program_id(0)

        @pl.when(i == 0)
        def _():
            dqw_ref[...] = jnp.zeros_like(dqw_ref)
            dkw_ref[...] = jnp.zeros_like(dkw_ref)

        bd = bd_ref[...]

        def total(d1_ref, d4_ref, d16_ref):
            return d1_ref[...] + _from_pattern(d4_ref, R4, scr4) + _from_pattern(d16_ref, R16, scr16)

        def one(d_refs, x_ref, w_ref, scale, dx_ref, dw_ref):
            dy = total(*d_refs) * scale
            x = x_ref[...]
            rstd = _seg_rstd(x, bd, ATT_HD)
            nrm = x * rstd
            dw_ref[...] = dw_ref[...] + jnp.sum(dy * nrm, axis=0, keepdims=True)
            g = dy * w_ref[...]
            dx_ref[...] = (rstd * (g - nrm * _seg_expand(_seg_mean(g * nrm, bd, ATT_HD), bd, ATT_HD))).astype(BF16)

        one((dq1_ref, dq4_ref, dq16_ref), q_ref, qw_ref, ATT_HD ** -0.5, dqr_ref, dqw_ref)
        one((dk1_ref, dk4_ref, dk16_ref), k_ref, kw_ref, 1.0, dkr_ref, dkw_ref)
        dvr_ref[...] = total(dv1_ref, dv4_ref, dv16_ref).astype(BF16)

    blk = pl.BlockSpec((tm, 512), lambda i: (i, 0))
    pats = [blk, _pattern_spec(R4), _pattern_spec(R16)]
    row = pl.BlockSpec((1, 512), lambda i: (0, 0))
    acc = pl.BlockSpec((8, 512), lambda i: (0, 0))
    return pl.pallas_call(
        body, grid=(s // tm,),
        in_specs=pats * 3 + [pl.BlockSpec((tm, 512), lambda i: (i, COL_ATT_Q)),
                             pl.BlockSpec((tm, 512), lambda i: (i, COL_ATT_K)), row, row,
                             pl.BlockSpec((512, 128), lambda i: (0, 0))],
        out_specs=[blk, blk, blk, acc, acc],
        out_shape=[_sds((s, 512), BF16)] * 3 + [_sds((8, 512))] * 2,
        scratch_shapes=[SLABS] * 2,
        compiler_params=_cp(("arbitrary",)), name="att_pre_bwd")(*dq_pats, *dk_pats, *dv_pats, proj, proj,
                                                                  qw_t, kw_t, bd64)


def _dn_scan_bwd(do, sp, qd, kt, w, vn, attn, gl):
    s = do.shape[0]
    nc = s // CHUNK

    def body(do_ref, sp_ref, qd_ref, kt_ref, w_ref, vn_ref, attn_ref, gl_ref,
             du_ref, dqd_ref, dkt_ref, dw_ref, dattn_ref, dgl_ref, ds_ref):
        n = pl.program_id(0)

        @pl.when(n == 0)
        def _():
            ds_ref[...] = jnp.zeros_like(ds_ref)

        def chain(cc, h):
            rows = slice(CHUNK * cc, CHUNK * cc + CHUNK)
            lanes = slice(128 * h, 128 * h + 128)
            dsn = ds_ref[h]
            st = sp_ref[cc, h]
            dsb, stb = dsn.astype(BF16), st.astype(BF16)
            dob = do_ref[rows, lanes].astype(BF16)
            vnb = vn_ref[rows, lanes].astype(BF16)
            dvn = _dot_tn(attn_ref[cc, h].astype(BF16), dob) + _dot(kt_ref[rows, lanes].astype(BF16), dsb)
            du_ref[rows, lanes] = dvn
            dqd_ref[rows, lanes] = _dot_nt(dob, stb)
            dattn_ref[cc, h] = _dot_nt(dob, vnb)
            dkt_ref[rows, lanes] = _dot_nt(vnb, dsb)
            tot = jnp.sum(jnp.sum(st * dsn, axis=1, keepdims=True), axis=0, keepdims=True)
            dgl_ref[cc, h] = jnp.broadcast_to(tot, (1, 128))
            qdo = _dot_tn(qd_ref[rows, lanes].astype(BF16), dob)
            yield
            dvb = dvn.astype(BF16)
            dw_ref[rows, lanes] = -_dot_nt(dvb, stb)
            ds_ref[h] = qdo + dsn * gl_ref[cc, h] - _dot_tn(w_ref[rows, lanes].astype(BF16), dvb)

        for cc in reversed(range(CPS_SCAN)):
            _interleave([chain(cc, h) for h in range(DN_HEADS)])

    nsteps = nc // CPS_SCAN
    big = pl.BlockSpec((CPS_SCAN * CHUNK, 512), lambda n: (nsteps - 1 - n, 0))
    sq = pl.BlockSpec((CPS_SCAN, DN_HEADS, CHUNK, CHUNK), lambda n: (nsteps - 1 - n, 0, 0, 0))
    glb = pl.BlockSpec((CPS_SCAN, DN_HEADS, 1, 128), lambda n: (nsteps - 1 - n, 0, 0, 0))
    return pl.pallas_call(
        body, grid=(nsteps,),
        in_specs=[big, pl.BlockSpec((CPS_SCAN, DN_HEADS, DN_HD, DN_HD), lambda n: (nsteps - 1 - n, 0, 0, 0)),
                  big, big, big, big, sq, glb],
        out_specs=[big, big, big, big, sq, glb],
        out_shape=[_sds((s, 512))] * 4 + [_sds((nc, DN_HEADS, CHUNK, CHUNK)), _sds((nc, DN_HEADS, 1, 128))],
        scratch_shapes=[pltpu.VMEM((DN_HEADS, DN_HD, DN_HD), F32)],
        compiler_params=_cp(("arbitrary",)), name="dn_scan_bwd")(do, sp, qd, kt, w, vn, attn, gl)


def _dn_prep_bwd(qn, kn, v, bg, bgt, tri, sel, t_inv, attn, u, w, du, dw, dqd, dkt, dattn, dgl):
    s = qn.shape[0]
    nc = s // CHUNK

    def body(q_ref, k_ref, v_ref, bg_ref, bgt_ref, tri_ref, sel_ref, t_ref, attn_ref, u_ref, w_ref,
             du_ref, dw_ref, dqd_ref, dkt_ref, dattn_ref, dgl_ref,
             dq_ref, dk_ref, dv_ref, dbg_ref):
        tri_v = tri_ref[...]
        lane = lax.broadcasted_iota(jnp.int32, (CHUNK, 128), 1)
        rowi = lax.broadcasted_iota(jnp.int32, (CHUNK, 128), 0)
        ones_b = jnp.ones((CHUNK, 128), BF16)
        ones_sq = jnp.ones((128, 128), BF16)
        parts = [[] for _ in range(CPS)]

        def chain(cc, h, bg3, gc3, gc_row):
            rows = slice(CHUNK * cc, CHUNK * cc + CHUNK)
            lanes = slice(128 * h, 128 * h + 128)
            gcc, beta, incl, strict, decay, gl = _chunk_common(bg3, gc3, gc_row, h, sel_ref)
            yield
            q = q_ref[rows, lanes]
            k = k_ref[rows, lanes]
            vv = v_ref[rows, lanes]
            ts = _split(t_ref[cc, h])
            egc = jnp.exp(gcc)
            kb = k * beta
            a_mat = jnp.where(strict, _mm_nt(kb, k) * decay, 0.0)
            dvb = _mm3_tn(ts, _split(du_ref[rows, lanes]))
            dkbg = _mm3_tn(ts, _split(dw_ref[rows, lanes]))
            yield
            d_a = jnp.where(strict, -(_mm_nt(dvb, u_ref[rows, lanes]) + _mm_nt(dkbg, w_ref[rows, lanes])), 0.0)
            d_m = d_a * decay
            dattn_m = jnp.where(incl, dattn_ref[cc, h], 0.0)
            dqk = dattn_m * decay
            e_hi, e_lo = _split(d_a * a_mat + dattn_m * attn_ref[cc, h])
            yield
            dkb = _mm(d_m, k)
            dk = _mm_tn(d_m, kb) + _mm_tn(dqk, q)
            dq = _mm(dqk, k)
            e_colsum = _dot_tn(e_hi, ones_b) + _dot_tn(e_lo, ones_b)
            e_rowsum = _dot(e_hi, ones_b) + _dot(e_lo, ones_b)
            dqd = dqd_ref[rows, lanes]
            dkt = dkt_ref[rows, lanes]
            sums = _rowsum_b(jnp.concatenate([dqd * q, dkt * k, dkbg * k, dkb * k, dvb * vv], axis=0), ones_sq)
            s_dqd, s_dkt, rk, s_dkb, s_dvb = (sums[CHUNK * n:CHUNK * n + CHUNK] for n in range(5))
            yield
            tail = jnp.exp(gl - gcc)
            r = s_dkt * tail
            dgl_tot = jnp.sum(r, axis=0, keepdims=True) + dgl_ref[cc, h] * jnp.exp(gl)
            dgc = e_rowsum - e_colsum + s_dqd * egc - r + rk * beta * egc
            dgc = dgc + jnp.where(rowi == CHUNK - 1, dgl_tot, 0.0)
            dq_ref[rows, lanes] = dq + dqd * egc
            dk_ref[rows, lanes] = dk + dkt * tail + dkbg * (beta * egc) + dkb * beta
            dv_ref[rows, lanes] = dvb * beta
            parts[cc].append((h, dgc, rk * egc + s_dkb + s_dvb))

        gens = []
        for cc in range(CPS):
            bgv = bg_ref[CHUNK * cc:CHUNK * cc + CHUNK, :]
            bg3, gc3, gc_row = _chunk_cumsum(bgv, bgt_ref[cc], tri_v)
            gens += [chain(cc, h, bg3, gc3, gc_row) for h in range(DN_HEADS)]
        _interleave(gens)
        for cc in range(CPS):
            dgc_mat = jnp.zeros((CHUNK, 128), F32)
            dbeta_mat = jnp.zeros((CHUNK, 128), F32)
            for h, dgc, dbeta in parts[cc]:
                dgc_mat = dgc_mat + jnp.where(lane == DN_HEADS + h, dgc, 0.0)
                dbeta_mat = dbeta_mat + jnp.where(lane == h, dbeta, 0.0)
            dbg_ref[CHUNK * cc:CHUNK * cc + CHUNK, :] = _mmx_tn(tri_v, dgc_mat) + dbeta_mat

    big = pl.BlockSpec((CPS * CHUNK, 512), lambda n: (n, 0))
    sq = pl.BlockSpec((CPS, DN_HEADS, CHUNK, CHUNK), lambda n: (n, 0, 0, 0))
    glb = pl.BlockSpec((CPS, DN_HEADS, 1, 128), lambda n: (n, 0, 0, 0))
    small = pl.BlockSpec((CPS * CHUNK, 128), lambda n: (n, 0))
    return pl.pallas_call(
        body, grid=(nc // CPS,),
        in_specs=[big, big, big, small, pl.BlockSpec((CPS, 8, CHUNK), lambda n: (n, 0, 0)),
                  pl.BlockSpec((CHUNK, CHUNK), lambda n: (0, 0)),
                  pl.BlockSpec((8, 128, 128), lambda n: (0, 0, 0)), sq, sq, big, big,
                  big, big, big, big, sq, glb],
        out_specs=[big, big, big, small],
        out_shape=[_sds((s, 512))] * 3 + [_sds((s, 128))],
        compiler_params=_cp(("parallel",)), name="dn_prep_bwd")(qn, kn, v, bg, bgt, tri, sel, t_inv, attn, u, w,
                                                                  du, dw, dqd, dkt, dattn, dgl)


def _dn_pre_bwd(dqn, dkn, dv, dbg, proj, conv_w8, alog_l, dtb_l):
    s = proj.shape[0]
    tr = 512
    nh = tr // 8

    def body(dq_ref, dk_ref, dv_ref, dbg_ref, u_ref, halo_ref, ba_ref, w_ref, al_ref, dt_ref,
             dy_ref, dba_ref, dsm_ref):
        i = pl.program_id(0)

        @pl.when(i == 0)
        def _():
            dsm_ref[...] = jnp.zeros_like(dsm_ref)

        keep = (i > 0).astype(F32)
        for c in range(12):
            lanes = slice(128 * c, 128 * c + 128)
            y = _conv_group(u_ref[:, lanes], halo_ref[:, lanes] * keep, w_ref, c)
            sg = _sigmoid(y)
            sv = y * sg
            if c < 8:
                rs = lax.rsqrt(jnp.sum(sv * sv, axis=1, keepdims=True) + EPS)
                n = sv * rs
                if c < 4:
                    dn = dq_ref[:, lanes] * (DN_HD ** -0.5)
                else:
                    dn = dk_ref[:, slice(128 * (c - 4), 128 * (c - 3))]
                dsv = rs * (dn - n * jnp.sum(dn * n, axis=1, keepdims=True))
            else:
                dsv = dv_ref[:, slice(128 * (c - 8), 128 * (c - 7))]
            dy_ref[:, lanes] = dsv * _silu_grad(y, sg)
        _, lane, sig_b, t, nega, g = _beta_g(ba_ref[...], al_ref[...], dt_ref[...])
        dbg = dbg_ref[...]
        da = dbg * nega * _sigmoid(t)
        is_b = lane < DN_HEADS
        is_a = jnp.logical_and(lane >= DN_HEADS, lane < 2 * DN_HEADS)
        dba_ref[...] = jnp.where(is_b, dbg * sig_b * (1.0 - sig_b), jnp.where(is_a, da, 0.0)).astype(BF16)
        d_alog = jnp.sum(jnp.where(is_a, dbg * g, 0.0), axis=0, keepdims=True)
        d_dtb = jnp.sum(jnp.where(is_a, da, 0.0), axis=0, keepdims=True)
        row = lax.broadcasted_iota(jnp.int32, (8, 128), 0)
        dsm_ref[...] = dsm_ref[...] + jnp.where(row == 0, d_alog, jnp.where(row == 1, d_dtb, 0.0))

    blk = pl.BlockSpec((tr, 512), lambda i: (i, 0))
    return pl.pallas_call(
        body, grid=(s // tr,),
        in_specs=[blk, blk, blk, pl.BlockSpec((tr, 128), lambda i: (i, 0)),
                  pl.BlockSpec((tr, 1536), lambda i: (i, 0)),
                  pl.BlockSpec((8, 1536), lambda i: (jnp.maximum(i * nh - 1, 0), 0)),
                  pl.BlockSpec((tr, 128), lambda i: (i, COL_BA_128)),
                  pl.BlockSpec((8, 1536), lambda i: (0, 0)),
                  pl.BlockSpec((1, 128), lambda i: (0, 0)), pl.BlockSpec((1, 128), lambda i: (0, 0))],
        out_specs=[pl.BlockSpec((tr, 1536), lambda i: (i, 0)), pl.BlockSpec((tr, 128), lambda i: (i, 0)),
                   pl.BlockSpec((8, 128), lambda i: (0, 0))],
        out_shape=[_sds((s, 1536)), _sds((s, 128), BF16), _sds((8, 128))],
        compiler_params=_cp(("arbitrary",)), name="dn_pre_bwd")(dqn, dkn, dv, dbg, proj, proj, proj, conv_w8,
                                                                 alog_l, dtb_l)


def _conv_bwd(dy, proj, conv_w8):
    s = dy.shape[0]
    tr = 512
    nh = tr // 8
    nblk = s // tr

    def body(dy_ref, dyn_ref, u_ref, halo_ref, w_ref, du_ref, dw_ref):
        i = pl.program_id(0)

        @pl.when(i == 0)
        def _():
            dw_ref[...] = jnp.zeros_like(dw_ref)

        keep_prev = (i > 0).astype(F32)
        keep_next = (i < nblk - 1).astype(F32)
        row = lax.broadcasted_iota(jnp.int32, (8, 128), 0)
        for c in range(12):
            lanes = slice(128 * c, 128 * c + 128)
            dyc = dy_ref[:, lanes]
            dcat = jnp.concatenate([dyc, dyn_ref[:, lanes] * keep_next], axis=0)
            xcat = jnp.concatenate([halo_ref[:, lanes] * keep_prev, u_ref[:, lanes]], axis=0)
            du = dyc * w_ref[CONV_W - 1:CONV_W, lanes]
            dwc = jnp.where(row == CONV_W - 1, jnp.sum(dyc * u_ref[:, lanes], axis=0, keepdims=True), 0.0)
            for k in range(1, CONV_W):
                du = du + pltpu.roll(dcat, tr + 8 - k, 0)[0:tr] * w_ref[CONV_W - 1 - k:CONV_W - k, lanes]
                ush = pltpu.roll(xcat, k, 0)[8:8 + tr]
                dwc = dwc + jnp.where(row == CONV_W - 1 - k, jnp.sum(dyc * ush, axis=0, keepdims=True), 0.0)
            du_ref[:, lanes] = du.astype(BF16)
            dw_ref[:, lanes] = dw_ref[:, lanes] + dwc

    return pl.pallas_call(
        body, grid=(nblk,),
        in_specs=[pl.BlockSpec((tr, 1536), lambda i: (i, 0)),
                  pl.BlockSpec((8, 1536), lambda i: (jnp.minimum((i + 1) * nh, s // 8 - 1), 0)),
                  pl.BlockSpec((tr, 1536), lambda i: (i, 0)),
                  pl.BlockSpec((8, 1536), lambda i: (jnp.maximum(i * nh - 1, 0), 0)),
                  pl.BlockSpec((8, 1536), lambda i: (0, 0))],
        out_specs=[pl.BlockSpec((tr, 1536), lambda i: (i, 0)), pl.BlockSpec((8, 1536), lambda i: (0, 0))],
        out_shape=[_sds((s, 1536), BF16), _sds((8, 1536))],
        compiler_params=_cp(("arbitrary",)), name="conv_bwd")(dy, dy, proj, proj, conv_w8)


PIECE_WIDTHS = (1536, 512, 512, 512, 512, 512, 128)


def _in_bwd_dx(pieces, w_bf, x, dy, norm_w, ds_accs, buckets, chip_sums=None):
    s = x.shape[0]
    tm = 512
    n_piece = len(PIECE_WIDTHS)
    n_step = s // tm
    n_arr = 0 if chip_sums is None else len(chip_sums)
    n_sent = 3 * n_arr

    def body(*refs):
        refs = list(refs)

        def take(n):
            return [refs.pop(0) for _ in range(n)]

        piece_refs = take(n_piece)
        w_ref, x_ref, dy_ref, nw_ref = take(4)
        ds_refs = take(len(ds_accs))
        bk_ref, = take(1)
        hb_refs = take(n_arr)
        dx_ref, dnw_ref, drb_ref = take(3)
        recv_refs = take(n_arr)
        i = pl.program_id(0)

        def copies():
            send_sems, recv_sems = refs
            xi, yi, ci = _position()
            out = []
            for j, (px, py) in enumerate(_other_chips(xi, yi)):
                for t, (src, dst) in enumerate(zip(hb_refs, recv_refs)):
                    k = n_arr * j + t
                    out.append(pltpu.make_async_remote_copy(
                        src_ref=src.at[2 * px + py], dst_ref=dst.at[j], send_sem=send_sems.at[k],
                        recv_sem=recv_sems.at[k], device_id=(px, py, ci), device_id_type=MESH_ID))
            return out

        @pl.when(i == 0)
        def _():
            dnw_ref[...] = jnp.zeros_like(dnw_ref)
            for cp in (copies() if n_sent else []):
                cp.start()

        @pl.when(i == 1)
        def _():
            _bias_grad(ds_refs, bk_ref, drb_ref)

        dp = jnp.concatenate([r[...] for r in piece_refs], axis=1)
        dh = _dot(dp, w_ref[...])
        xv = x_ref[...]
        rstd = lax.rsqrt(jnp.mean(xv * xv, axis=-1, keepdims=True) + EPS)
        xh = xv * rstd
        dnw_ref[...] = dnw_ref[...] + jnp.sum(dh * xh, axis=0, keepdims=True)
        g = dh * nw_ref[...]
        dx_ref[...] = rstd * (g - xh * jnp.mean(g * xh, axis=-1, keepdims=True)) + dy_ref[...]

        if n_sent:
            @pl.when(i == n_step - 1)
            def _():
                cps = copies()
                for cp in cps:
                    cp.wait_recv()
                for cp in cps:
                    cp.wait_send()

    def blk(n):
        return pl.BlockSpec((tm, n), lambda i: (i, 0))

    vm = pl.BlockSpec(memory_space=pltpu.VMEM)
    in_specs = [blk(n) for n in PIECE_WIDTHS] + [pl.BlockSpec((D_IN_PAD, D_MODEL), lambda i: (0, 0)), blk(D_MODEL),
                                                blk(D_MODEL), pl.BlockSpec((1, D_MODEL), lambda i: (0, 0))]
    in_specs += [vm] * (len(ds_accs) + 1)
    out_specs = [blk(D_MODEL), pl.BlockSpec((8, D_MODEL), lambda i: (0, 0)), pl.BlockSpec(memory_space=pltpu.SMEM)]
    out_shape = [_sds((s, D_MODEL)), _sds((8, D_MODEL)), _sds((ATT_HEADS, N_BUCKETS))]
    scratch = []
    extra = ()
    if n_sent:
        extra = tuple(chip_sums)
        in_specs += [ANY] * n_arr
        out_specs += [ANY] * n_arr
        out_shape += [_sds((3,) + a.shape[1:], a.dtype) for a in extra]
        scratch = [pltpu.SemaphoreType.DMA((n_sent,)), pltpu.SemaphoreType.DMA((n_sent,))]
    return pl.pallas_call(
        body, grid=(n_step,), in_specs=in_specs, out_specs=out_specs, out_shape=out_shape, scratch_shapes=scratch,
        compiler_params=_cp(("arbitrary",)), name="in_bwd_dx")(*pieces, w_bf, x, dy, norm_w, *ds_accs, buckets, *extra)


def _in_bwd_dw(pieces, x, norm_w):
    s = x.shape[0]
    tm = 512
    n_piece = len(PIECE_WIDTHS)

    def body(*refs):
        piece_refs = refs[:n_piece]
        x_ref, nw_ref, dw_ref = refs[n_piece:]
        i = pl.program_id(0)

        @pl.when(i == 0)
        def _():
            dw_ref[...] = jnp.zeros_like(dw_ref)

        xv = x_ref[...]
        rstd = lax.rsqrt(jnp.mean(xv * xv, axis=-1, keepdims=True) + EPS)
        h = (xv * rstd * nw_ref[...]).astype(BF16)
        at = 0
        for r, width in zip(piece_refs, PIECE_WIDTHS):
            dw_ref[at:at + width, :] = dw_ref[at:at + width, :] + _dot_tn(r[...], h)
            at += width

    return pl.pallas_call(
        body, grid=(s // tm,),
        in_specs=[pl.BlockSpec((tm, n), lambda i: (i, 0)) for n in PIECE_WIDTHS]
        + [pl.BlockSpec((tm, D_MODEL), lambda i: (i, 0)), pl.BlockSpec((1, D_MODEL), lambda i: (0, 0))],
        out_specs=pl.BlockSpec((D_IN_PAD, D_MODEL), lambda i: (0, 0)),
        out_shape=_sds((D_IN_PAD, D_MODEL)),
        compiler_params=_cp(("arbitrary",)), name="in_bwd_dw")(*pieces, x, norm_w)


def _flat(a):
    return a.reshape(-1, a.shape[-1])


def _as_pattern(a, r):
    return a if r == 1 else a.reshape(r, a.shape[0] // r, a.shape[1])


D_SHARD = D_IN // N_CHIPS
BA_START = 4 * D_DN
BA_PACKED = 4096
S1_HEAD = BA_START - D_SHARD
S1_BA = 2 * D_SHARD - BA_START


def _pack_w(g):
    n = g.shape[2]

    def body(g_ref, o_ref):
        g32 = g_ref.bitcast(jnp.uint32)
        o32 = o_ref.bitcast(jnp.uint32)
        full, head, ba1 = D_SHARD // 2, S1_HEAD // 2, S1_BA // 2
        ba2 = DN_HEADS - ba1
        pieces = [(0, 0, full), (1, 0, head), (2, ba2, full), (3, 0, full), (1, head, full), (2, 0, ba2)]
        at = 0
        for chip, lo, hi in pieces:
            o32[at:at + hi - lo, :] = g32[chip, lo:hi, :]
            at += hi - lo
        o32[at:D_IN_PAD // 2, :] = jnp.zeros((D_IN_PAD // 2 - at, n), jnp.uint32)

    vm = pl.BlockSpec(memory_space=pltpu.VMEM)
    return pl.pallas_call(body, in_specs=[vm], out_specs=vm, out_shape=_sds((D_IN_PAD, n), g.dtype),
                          compiler_params=_cp(), name="pack_w")(g)


def _unpack_w(p, rows):
    n = p.shape[1]
    tn = 256
    mid = BA_PACKED + S1_BA
    pieces = [(0, 0, (0, D_SHARD)), (1, 0, (D_SHARD, BA_START)), (1, S1_HEAD, (BA_PACKED, mid)),
              (2, 0, (mid, BA_PACKED + 2 * DN_HEADS)), (2, 2 * DN_HEADS - S1_BA, (BA_START, BA_START + S1_HEAD)),
              (3, 0, (BA_START + S1_HEAD, BA_PACKED))]

    def body(p_ref, o_ref):
        for chip, at, (lo, hi) in pieces:
            o_ref[chip, at:at + hi - lo, :] = p_ref[lo:hi, :]
        for chip in range(N_CHIPS):
            o_ref[chip, D_SHARD:rows, :] = jnp.zeros((rows - D_SHARD, tn), p.dtype)

    return pl.pallas_call(
        body, grid=(n // tn,),
        in_specs=[pl.BlockSpec((D_IN_PAD, tn), lambda j: (0, j))],
        out_specs=pl.BlockSpec((N_CHIPS, rows, tn), lambda j: (0, 0, j)),
        out_shape=_sds((N_CHIPS, rows, n), p.dtype),
        compiler_params=_cp(("parallel",)), name="unpack_w")(p)


def _lane_row(vec, offset):
    return jnp.pad(vec.reshape(1, -1), ((0, 0), (offset, 128 - offset - vec.shape[0])))


def _local_step(x, tgt, norm_w, w_bf, conv_w, a_log, dt_bias, dn_norm_w, q_norm_w, k_norm_w, bias, w_out_bf):
    s = x.shape[0]
    nc = s // CHUNK
    conv_w8 = jnp.pad(conv_w, ((0, 8 - CONV_W), (0, 0)))
    alog_l = _lane_row(a_log.reshape(-1), DN_HEADS)
    dtb_l = _lane_row(dt_bias.reshape(-1), DN_HEADS)
    dnw_t = jnp.tile(dn_norm_w.reshape(1, DN_HD), (1, DN_HEADS))
    qw_t = jnp.tile(q_norm_w.reshape(1, ATT_HD), (1, ATT_HEADS))
    kw_t = jnp.tile(k_norm_w.reshape(1, ATT_HD), (1, ATT_HEADS))
    bd128 = _compact_mat(DN_HD)
    bd64 = _compact_mat(ATT_HD)
    tri = _tri_incl()
    sel = _lane_select()
    buckets = _bucket_tables()

    proj = _in_proj(x, norm_w, w_bf)
    qn, kn, v_dn, bg = _dn_pre(proj, conv_w8, alog_l, dtb_l)
    bgt = bg[:, 0:8].reshape(nc, CHUNK, 8).transpose(0, 2, 1)
    u, w, qd, kt, attn, t_inv, gl = _dn_prep(qn, kn, v_dn, bg, bgt, tri, sel)
    o_dn, vn, sp = _dn_scan(u, w, qd, kt, attn, gl)
    q1, k1, v1, q4, k4, v4, q16, k16, v16 = _att_pre(proj, qw_t, kw_t, bd64)
    rs = [r for _, r in PATTERNS]
    qkv = [(q1, k1, v1), (_flat(q4), _flat(k4), _flat(v4)), (_flat(q16), _flat(k16), _flat(v16))]
    o_pats, lse_pats = [], []
    for p, r in enumerate(rs):
        o_p, lse_p = _att_fwd(*qkv[p], bias, p, r, "att_fwd_r%d" % r)
        o_pats.append(_as_pattern(o_p, r))
        lse_pats.append(_as_pattern(lse_p, r))
    mixed, o_att, l1, l4, l16 = _post_fwd(o_dn, proj, o_pats, lse_pats, dnw_t, bd128)

    (dy, loss_blk, d_w_out, do_dn, dz, dgate, do1, do4, do16, dl1, dl4, dl16,
     d_dnw) = _out_post(x, mixed, w_out_bf, tgt, o_dn, proj, o_att, dnw_t, bd128)
    side = [(do1, l1, dl1), (_flat(do4), _flat(l4), _flat(dl4)), (_flat(do16), _flat(l16), _flat(dl16))]
    dq_pats, dk_pats, dv_pats, ds_accs = [], [], [], []
    for p, r in enumerate(rs):
        dq_p, dk_p, dv_p, ds_p = _att_bwd(*qkv[p], *side[p], bias, p, r, "att_bwd_r%d" % r)
        dq_pats.append(_as_pattern(dq_p, r))
        dk_pats.append(_as_pattern(dk_p, r))
        dv_pats.append(_as_pattern(dv_p, r))
        ds_accs.append(ds_p)
    dq_att, dk_att, dv_att, d_qw, d_kw = _att_pre_bwd(dq_pats, dk_pats, dv_pats, proj, qw_t, kw_t, bd64)
    du, dqd, dkt, dw, dattn, dgl = _dn_scan_bwd(do_dn, sp, qd, kt, w, vn, attn, gl)
    dqn, dkn, dv_dn, dbg = _dn_prep_bwd(qn, kn, v_dn, bg, bgt, tri, sel, t_inv, attn, u, w, du, dw, dqd, dkt, dattn,
                                        dgl)
    dyc, dba, dsm = _dn_pre_bwd(dqn, dkn, dv_dn, dbg, proj, conv_w8, alog_l, dtb_l)
    d_qkv_dn, d_conv8 = _conv_bwd(dyc, proj, conv_w8)
    pieces = (d_qkv_dn, dz, dq_att, dk_att, dv_att, dgate, dba)
    d_w_in_t = _in_bwd_dw(pieces, x, norm_w)
    last = functools.partial(_in_bwd_dx, pieces, w_bf, x, dy, norm_w, ds_accs, buckets)

    grads = dict(
        w_in_t=d_w_in_t,
        conv_w=d_conv8[0:CONV_W, :],
        a_log=dsm[0:1, DN_HEADS:2 * DN_HEADS],
        dt_bias=dsm[1:2, DN_HEADS:2 * DN_HEADS],
        dn_norm_w=d_dnw[0:1, :],
        q_norm_w=d_qw[0:1, :].reshape(ATT_HEADS, ATT_HD),
        k_norm_w=d_kw[0:1, :].reshape(ATT_HEADS, ATT_HD),
        w_out=d_w_out,
    )
    return loss_blk[0, 0], last, grads


MESH_ID = pl.DeviceIdType.MESH
ANY = pl.BlockSpec(memory_space=pl.ANY)


def _position():
    return lax.axis_index("x"), lax.axis_index("y"), lax.axis_index("c")


def _other_chips(x, y):
    return [(1 - x, y), (x, 1 - y), (1 - x, 1 - y)]


SHARD_PAD = 1040
WIN = 528


def _row_split(n):
    return (n // 2) // 128 * 128


def _part(ref, cc):
    n = ref.shape[0]
    sp = _row_split(n)
    return ref.at[pl.ds(0, sp)] if cc == 0 else ref.at[pl.ds(sp, n - sp)]


def _halves(ref):
    n = ref.shape[0]
    sp = (n // 2) // 16 * 16
    return ref.at[pl.ds(0, sp)], ref.at[pl.ds(sp, n - sp)]


def _gather_weights(wt_s, w_out_s, conv_s, rel_bias, buckets):
    def body(a_ref, b_ref, c_ref, rb_ref, bk_ref, ga_ref, gb_ref, gc_ref, bias_ref, send_sems, recv_sems, loc_sems,
             a_vmem, b_vmem):
        x, y, c = _position()
        me = 2 * x + y
        sib = (x, y, 1 - c)
        big = ((a_ref, ga_ref), (b_ref, gb_ref))
        stage_in = [pltpu.make_async_copy(a_ref, a_vmem, loc_sems.at[0]),
                    pltpu.make_async_copy(b_ref, b_vmem, loc_sems.at[1])]
        local = [pltpu.make_async_copy(a_vmem, ga_ref.at[me], loc_sems.at[0]),
                 pltpu.make_async_copy(b_vmem, gb_ref.at[me], loc_sems.at[1]),
                 pltpu.make_async_copy(c_ref, gc_ref.at[me], loc_sems.at[2])]
        for cp in stage_in:
            cp.start()
        local[2].start()
        for cp in stage_in:
            cp.wait()
        for cp in local[:2]:
            cp.start()
        others = _other_chips(x, y)

        def exchange(cc):
            nbr = others[:2]
            dg = others[2]
            pending = []

            def copy(k, src, dst, to):
                cp = pltpu.make_async_remote_copy(src_ref=src, dst_ref=dst, send_sem=send_sems.at[k],
                                                  recv_sem=recv_sems.at[k], device_id=to, device_id_type=MESH_ID)
                pending.append(cp)
                cp.start()

            def landed(k, dst):
                pltpu.make_async_remote_copy(src_ref=dst, dst_ref=dst, send_sem=send_sems.at[k],
                                             recv_sem=recv_sems.at[k], device_id=sib, device_id_type=MESH_ID).wait_recv()

            def slot(dst, chip, c_part):
                return _part(dst.at[2 * chip[0] + chip[1]], c_part)

            for i, n in enumerate(nbr):
                for t, (src, dst) in enumerate(big):
                    copy(2 * i + t, _part(src, cc), _part(dst.at[me], cc), (*n, c))
            for j, chip in enumerate(others):
                copy(4 + j, c_ref, gc_ref.at[me], (*chip, c))
            _bias_table(rb_ref, bk_ref, bias_ref)
            for i, n in enumerate(nbr):
                for t, (_, dst) in enumerate(big):
                    got = slot(dst, n, cc)
                    landed(2 * i + t, got)
                    half = _halves(got)[1 - i]
                    copy(7 + 2 * (1 - i) + t, half, half, (*nbr[1 - i], c))
                    copy(11 + 2 * i + t, got, got, sib)
            for i, n in enumerate(nbr):
                for t, (_, dst) in enumerate(big):
                    half = _halves(slot(dst, dg, cc))[i]
                    landed(7 + 2 * i + t, half)
                    copy(15 + 2 * i + t, half, half, sib)
            for j, chip in enumerate(others):
                landed(4 + j, gc_ref.at[2 * chip[0] + chip[1]])
            for i, n in enumerate(nbr):
                for t, (_, dst) in enumerate(big):
                    landed(11 + 2 * i + t, slot(dst, n, 1 - cc))
                    landed(15 + 2 * i + t, _halves(slot(dst, dg, 1 - cc))[i])
            for cp in pending:
                cp.wait_send()

        for cc in (0, 1):
            pl.when(c == cc)(functools.partial(exchange, cc))
        for cp in local:
            cp.wait()

    srcs = (wt_s, w_out_s, conv_s)
    n_sem = 19
    return pl.pallas_call(
        body, in_specs=[ANY] * 3 + [pl.BlockSpec(memory_space=pltpu.SMEM), pl.BlockSpec(memory_space=pltpu.VMEM)],
        out_specs=[ANY] * 3 + [pl.BlockSpec(memory_space=pltpu.VMEM)],
        out_shape=[_sds((N_CHIPS,) + a.shape, a.dtype) for a in srcs]
        + [_sds((len(PATTERNS), ATT_HEADS, BLK, 2 * BLK))],
        scratch_shapes=[pltpu.SemaphoreType.DMA((n_sem,)), pltpu.SemaphoreType.DMA((n_sem,)),
                        pltpu.SemaphoreType.DMA((3,)), pltpu.VMEM(wt_s.shape, wt_s.dtype),
                        pltpu.VMEM(w_out_s.shape, w_out_s.dtype)],
        compiler_params=_cp(), name="gather_weights")(*srcs, rel_bias, buckets)


N_DEV = 8
PEER_FLIPS = [(dx, dy, dc) for dx in (0, 1) for dy in (0, 1) for dc in (0, 1)][1:]


def _small_copies(s_ref, rs_ref, send_sems, recv_sems):
    x, y, c = _position()
    dev = 4 * x + 2 * y + c
    sends, recvs = [], []
    for f, (dx, dy, dc) in enumerate(PEER_FLIPS):
        peer = (x ^ dx, y ^ dy, c ^ dc)
        sends.append(pltpu.make_async_remote_copy(
            src_ref=s_ref, dst_ref=rs_ref.at[dev], send_sem=send_sems.at[f], recv_sem=recv_sems.at[f],
            device_id=peer, device_id_type=MESH_ID))
        recvs.append(pltpu.make_async_remote_copy(
            src_ref=s_ref, dst_ref=rs_ref.at[4 * peer[0] + 2 * peer[1] + peer[2]], send_sem=send_sems.at[f],
            recv_sem=recv_sems.at[f], device_id=peer, device_id_type=MESH_ID))
    return sends, recvs


def _fill_parts(h_in, r_in, h_out, r_out, small, rows_in, rows_out):
    def body(ha_ref, ra_ref, hb_ref, rb_ref, s_ref, fa_ref, fb_ref, rs_ref, send_sems, recv_sems, loc_sems,
             small_send, small_recv, a_vmem, b_vmem, ra_vmem, rb_vmem):
        x, y, c = _position()
        sib = (x, y, 1 - c)
        chip = 2 * x + y
        pairs = ((a_vmem, fa_ref), (b_vmem, fb_ref))
        own_small = pltpu.make_async_copy(s_ref, rs_ref.at[4 * x + 2 * y + c], loc_sems.at[2])
        own_small.start()
        small_sends, small_recvs = _small_copies(s_ref, rs_ref, small_send, small_recv)
        for cp in small_sends:
            cp.start()
        stage_in = [pltpu.make_async_copy(ha_ref.at[chip], a_vmem, loc_sems.at[0]),
                    pltpu.make_async_copy(hb_ref.at[chip], b_vmem, loc_sems.at[1]),
                    pltpu.make_async_copy(ra_ref, ra_vmem, loc_sems.at[3]),
                    pltpu.make_async_copy(rb_ref, rb_vmem, loc_sems.at[4])]
        for cp in stage_in:
            cp.start()
        for cp in stage_in:
            cp.wait()
        for tot, recv in ((a_vmem, ra_vmem), (b_vmem, rb_vmem)):
            acc = tot[...]
            for j in range(3):
                acc = acc + recv[j].astype(F32)
            tot[...] = acc

        def fill(cc):
            mine = [_part(dst, cc) for _, dst in pairs]
            srcs = [src.at[pl.ds(0, m.shape[0])] for src, m in zip((a_vmem, b_vmem), mine)]
            local = [pltpu.make_async_copy(s, m, loc_sems.at[t]) for t, (s, m) in enumerate(zip(srcs, mine))]
            sends = [pltpu.make_async_remote_copy(src_ref=s, dst_ref=m, send_sem=send_sems.at[t],
                                                  recv_sem=recv_sems.at[t], device_id=sib, device_id_type=MESH_ID)
                     for t, (s, m) in enumerate(zip(srcs, mine))]
            for cp in local + sends:
                cp.start()
            for t, (_, dst) in enumerate(pairs):
                theirs = _part(dst, 1 - cc)
                pltpu.make_async_remote_copy(src_ref=theirs, dst_ref=theirs, send_sem=send_sems.at[t],
                                             recv_sem=recv_sems.at[t], device_id=sib, device_id_type=MESH_ID).wait_recv()
            for cp in sends:
                cp.wait_send()
            for cp in local:
                cp.wait()

        for cc in (0, 1):
            pl.when(c == cc)(functools.partial(fill, cc))
        for cp in small_recvs:
            cp.wait_recv()
        for cp in small_sends:
            cp.wait_send()
        own_small.wait()

    n_peer = len(PEER_FLIPS)
    return pl.pallas_call(
        body, in_specs=[ANY] * 5, out_specs=[ANY] * 3,
        out_shape=[_sds((rows_in, D_MODEL)), _sds((rows_out, D_MODEL)), _sds((N_DEV,) + small.shape, small.dtype)],
        scratch_shapes=[pltpu.SemaphoreType.DMA((2,)), pltpu.SemaphoreType.DMA((2,)), pltpu.SemaphoreType.DMA((5,)),
                        pltpu.SemaphoreType.DMA((n_peer,)), pltpu.SemaphoreType.DMA((n_peer,)),
                        pltpu.VMEM(h_in.shape[1:], F32), pltpu.VMEM(h_out.shape[1:], F32),
                        pltpu.VMEM(r_in.shape, r_in.dtype), pltpu.VMEM(r_out.shape, r_out.dtype)],
        compiler_params=_cp(), name="fill_parts")(h_in, r_in, h_out, r_out, small)


def _chip_sums(u_in, u_out, win_in, win_out):
    wins = (win_in, win_out)

    def body(a_ref, b_ref, ha_ref, hba_ref, hb_ref, hbb_ref, send_sems, recv_sems, loc_sems, out_sems,
             mine_a, theirs_a, sum_a, sumb_a, mine_b, theirs_b, sum_b, sumb_b):
        x, y, c = _position()
        sib = (x, y, 1 - c)
        groups = ((a_ref, mine_a, theirs_a, sum_a, sumb_a, ha_ref, hba_ref, win_in),
                  (b_ref, mine_b, theirs_b, sum_b, sumb_b, hb_ref, hbb_ref, win_out))
        loads, sends = [], []
        for t, (src, mine, theirs, _, _, _, _, win) in enumerate(groups):
            split = _row_split(src.shape[1])
            for k in range(N_CHIPS):
                n = N_CHIPS * t + k
                loads.append(pltpu.make_async_copy(
                    src.at[k, pl.ds(pl.multiple_of(c * split, split), win), :], mine.at[k], loc_sems.at[n]))
                sends.append(pltpu.make_async_remote_copy(
                    src_ref=src.at[k, pl.ds(pl.multiple_of((1 - c) * split, split), win), :], dst_ref=theirs.at[k],
                    send_sem=send_sems.at[n], recv_sem=recv_sems.at[n], device_id=sib, device_id_type=MESH_ID))
        for cp in sends + loads:
            cp.start()
        stores = []
        for t, (_, mine, theirs, tot, totb, h_out, hb_out, _) in enumerate(groups):
            for k in range(N_CHIPS):
                n = N_CHIPS * t + k
                loads[n].wait()
                sends[n].wait_recv()
                val = mine[k] + theirs[k]
                tot[k] = val
                totb[k] = val.astype(BF16)
                stores += [pltpu.make_async_copy(tot.at[k], h_out.at[k], out_sems.at[2 * n]),
                           pltpu.make_async_copy(totb.at[k], hb_out.at[k], out_sems.at[2 * n + 1])]
                stores[-2].start()
                stores[-1].start()
        for cp in sends:
            cp.wait_send()
        for cp in stores:
            cp.wait()

    shapes = [(N_CHIPS, w, D_MODEL) for w in wins]
    n_cp = 2 * N_CHIPS
    vmem = []
    for shp in shapes:
        vmem += [pltpu.VMEM(shp, F32), pltpu.VMEM(shp, F32), pltpu.VMEM(shp, F32), pltpu.VMEM(shp, BF16)]
    return pl.pallas_call(
        body, in_specs=[ANY] * 2, out_specs=[ANY] * 4,
        out_shape=[_sds(shapes[0]), _sds(shapes[0], BF16), _sds(shapes[1]), _sds(shapes[1], BF16)],
        scratch_shapes=[pltpu.SemaphoreType.DMA((n_cp,)), pltpu.SemaphoreType.DMA((n_cp,)),
                        pltpu.SemaphoreType.DMA((n_cp,)), pltpu.SemaphoreType.DMA((2 * n_cp,))] + vmem,
        compiler_params=_cp(), name="chip_sums")(u_in, u_out)


SMALL_LAYOUT = (("norm_w", 1024), ("conv_w", 6144), ("a_log", 128), ("dt_bias", 128), ("dn_norm_w", 128),
                ("q_norm_w", 512), ("k_norm_w", 512), ("rel_bias", 256), ("loss", 128))
SMALL_TOTAL = sum(n for _, n in SMALL_LAYOUT)


def _small_offset(name):
    off = 0
    for n, size in SMALL_LAYOUT:
        if n == name:
            return off
        off += size
    raise KeyError(name)


def _pack_small(grads):
    parts = []
    for name, size in SMALL_LAYOUT:
        flat = grads[name].reshape(1, -1)
        parts.append(jnp.pad(flat, ((0, 0), (0, size - flat.shape[1]))))
    return jnp.concatenate(parts, axis=1)


def _sum_small(rows):
    n_dev = rows.shape[0]
    q_off = _small_offset("q_norm_w")
    k_off = _small_offset("k_norm_w")

    def body(r_ref, tot_ref, qk_ref):
        tot = r_ref[0:1, :]
        for d in range(1, n_dev):
            tot = tot + r_ref[d:d + 1, :]
        tot_ref[...] = tot
        for row, off in ((0, q_off), (1, k_off)):
            s4 = tot[:, off:off + 128] + tot[:, off + 128:off + 256] + tot[:, off + 256:off + 384] \
                + tot[:, off + 384:off + 512]
            qk_ref[row:row + 1, :] = s4 + pltpu.roll(s4, ATT_HD, 1)

    return pl.pallas_call(
        body, in_specs=[pl.BlockSpec(memory_space=pltpu.VMEM)],
        out_specs=[pl.BlockSpec(memory_space=pltpu.VMEM)] * 2,
        out_shape=[_sds((1, SMALL_TOTAL)), _sds((2, 128))],
        compiler_params=_cp(), name="sum_small")(rows)


def _adamw_math(w, g, m, v):
    m = ADAM_B1 * m + (1.0 - ADAM_B1) * g
    v = ADAM_B2 * v + (1.0 - ADAM_B2) * (g * g)
    m_hat = m / (1.0 - ADAM_B1 ** ADAM_STEP)
    v_hat = v / (1.0 - ADAM_B2 ** ADAM_STEP)
    delta = -ADAM_LR * (m_hat / (jnp.sqrt(v_hat) + ADAM_EPS) + ADAM_WD * w)
    return delta, m, v


def _adamw_big(g, w, m, v, name):
    rows, cols = w.shape
    tr = 128

    def body(g_ref, w_ref, m_ref, v_ref, go_ref, d_ref, nm_ref, nv_ref):
        g = g_ref[...]
        go_ref[...] = g
        d_ref[...], nm_ref[...], nv_ref[...] = _adamw_math(w_ref[...], g, m_ref[...], v_ref[...])

    blk = pl.BlockSpec((tr, cols), lambda i: (i, 0))
    return pl.pallas_call(
        body, grid=(pl.cdiv(rows, tr),), in_specs=[blk] * 4, out_specs=[blk] * 4,
        out_shape=[_sds((rows, cols))] * 4, compiler_params=_cp(("parallel",)), name=name)(g, w, m, v)


def _adamw_rows(g, w, m, v, name):
    rows, cols = w.shape
    tr = 128

    def body(g_ref, w_ref, m_ref, v_ref, go_ref, d_ref, nm_ref, nv_ref, s_g, s_d, s_m, s_v):
        g = g_ref[...]
        s_g[...] = g
        s_d[...], s_m[...], s_v[...] = _adamw_math(w_ref[...], g, m_ref[...], v_ref[...])
        for i in range(tr):
            for scr, out in ((s_g, go_ref), (s_d, d_ref), (s_m, nm_ref), (s_v, nv_ref)):
                out[i] = scr[i:i + 1, :]

    blk = pl.BlockSpec((tr, cols), lambda i: (i, 0))
    oblk = pl.BlockSpec((tr, 1, cols), lambda i: (i, 0, 0))
    return pl.pallas_call(
        body, grid=(pl.cdiv(rows, tr),), in_specs=[blk] * 4, out_specs=[oblk] * 4,
        out_shape=[_sds((rows, 1, cols))] * 4, scratch_shapes=[pltpu.VMEM((tr, cols), F32)] * 4,
        compiler_params=_cp(("parallel",)), name=name)(g, w, m, v)


def _adamw_small(w, g, m, v, name):
    def body(w_ref, g_ref, m_ref, v_ref, d_ref, nm_ref, nv_ref):
        d_ref[...], nm_ref[...], nv_ref[...] = _adamw_math(w_ref[...], g_ref[...], m_ref[...], v_ref[...])

    vm = pl.BlockSpec(memory_space=pltpu.VMEM)
    return pl.pallas_call(body, in_specs=[vm] * 4, out_specs=[vm] * 3, out_shape=[_sds(w.shape)] * 3,
                          compiler_params=_cp(), name=name)(w, g, m, v)


WEIGHTS = ("norm_w", "w_in", "conv_w", "a_log", "dt_bias", "dn_norm_w", "q_norm_w", "k_norm_w", "rel_bias", "w_out")


def kernel(x, norm_w, w_in, conv_w, a_log, dt_bias, dn_norm_w, q_norm_w, k_norm_w, rel_bias, w_out, loss_target, m_norm_w, m_w_in, m_conv_w, m_a_log, m_dt_bias, m_dn_norm_w, m_q_norm_w, m_k_norm_w, m_rel_bias, m_w_out, v_norm_w, v_w_in, v_conv_w, v_a_log, v_dt_bias, v_dn_norm_w, v_q_norm_w, v_k_norm_w, v_rel_bias, v_w_out):
    xi, yi, _ = _position()
    chip = 2 * xi + yi
    w_loc = dict(norm_w=norm_w, w_in=w_in[0].T, conv_w=conv_w[0], a_log=a_log, dt_bias=dt_bias, dn_norm_w=dn_norm_w,
                 q_norm_w=q_norm_w, k_norm_w=k_norm_w, rel_bias=rel_bias, w_out=w_out[0])
    m_loc = dict(norm_w=m_norm_w, w_in=m_w_in[0].T, conv_w=m_conv_w[0], a_log=m_a_log, dt_bias=m_dt_bias,
                 dn_norm_w=m_dn_norm_w, q_norm_w=m_q_norm_w, k_norm_w=m_k_norm_w, rel_bias=m_rel_bias,
                 w_out=m_w_out[0])
    v_loc = dict(norm_w=v_norm_w, w_in=v_w_in[0].T, conv_w=v_conv_w[0], a_log=v_a_log, dt_bias=v_dt_bias,
                 dn_norm_w=v_dn_norm_w, q_norm_w=v_q_norm_w, k_norm_w=v_k_norm_w, rel_bias=v_rel_bias,
                 w_out=v_w_out[0])

    wt_pad = jnp.pad(w_loc["w_in"].astype(BF16), ((0, SHARD_PAD - D_SHARD), (0, 0)))
    g_in, g_out, g_conv, bias = _gather_weights(wt_pad, w_loc["w_out"].astype(BF16), w_loc["conv_w"], rel_bias,
                                                _bucket_tables())
    wt_full = _pack_w(g_in)
    w_out_full = g_out.reshape(D_MODEL, D_MODEL)
    conv_full = g_conv.transpose(1, 0, 2).reshape(CONV_W, 3 * D_DN)

    loss_local, last_kernel, grads = _local_step(x[0], loss_target[0], norm_w, wt_full, conv_full, a_log, dt_bias,
                                                 dn_norm_w, q_norm_w, k_norm_w, bias, w_out_full)
    grads["loss"] = loss_local

    u_in = _unpack_w(grads["w_in_t"], SHARD_PAD)
    u_out = grads["w_out"].reshape(N_CHIPS, D_MODEL // N_CHIPS, D_MODEL)
    win_out = u_out.shape[1] // 2
    h_in, hb_in, h_out, hb_out = _chip_sums(u_in, u_out, WIN, win_out)
    grad_x, d_nw8, grads["rel_bias"], r_in, r_out = last_kernel(chip_sums=(hb_in, hb_out))
    grads["norm_w"] = d_nw8[0:1, :]
    small = _pack_small(grads)
    full_in, full_out, r_small = _fill_parts(h_in, r_in, h_out, r_out, small, SHARD_PAD, u_out.shape[1])
    tot_small, qk = _sum_small(r_small.reshape(8, SMALL_TOTAL))

    def small_grad(name, n):
        off = _small_offset(name)
        return tot_small[:, off:off + n]

    loss = small_grad("loss", 1).reshape(())
    conv_all = small_grad("conv_w", CONV_W * 3 * D_DN).reshape(CONV_W, 3 * D_DN)
    g_small = dict(
        norm_w=small_grad("norm_w", D_MODEL),
        conv_w=lax.dynamic_slice_in_dim(conv_all, chip * (3 * D_DN // N_CHIPS), 3 * D_DN // N_CHIPS, axis=1),
        a_log=small_grad("a_log", DN_HEADS),
        dt_bias=small_grad("dt_bias", DN_HEADS),
        dn_norm_w=small_grad("dn_norm_w", DN_HD),
        q_norm_w=qk[0:1, 0:ATT_HD],
        k_norm_w=qk[1:2, 0:ATT_HD],
        rel_bias=small_grad("rel_bias", ATT_HEADS * N_BUCKETS).reshape(ATT_HEADS, N_BUCKETS),
    )

    out_g, out_d, out_m, out_v = {}, {}, {}, {}
    out_g["w_in"], out_d["w_in"], out_m["w_in"], out_v["w_in"] = _adamw_rows(
        full_in, w_loc["w_in"], m_loc["w_in"], v_loc["w_in"], "adamw_w_in")
    out_g["w_out"], out_d["w_out"], out_m["w_out"], out_v["w_out"] = _adamw_big(
        full_out, w_loc["w_out"], m_loc["w_out"], v_loc["w_out"], "adamw_w_out")
    for name in g_small:
        out_g[name] = g_small[name]
        out_d[name], out_m[name], out_v[name] = _adamw_small(w_loc[name], g_small[name], m_loc[name], v_loc[name],
                                                             "adamw_" + name)
    for d in (out_g, out_d, out_m, out_v):
        d["w_in"] = d["w_in"].transpose(1, 2, 0)
        for name in ("conv_w", "w_out"):
            d[name] = d[name][None]
    return (loss, grad_x[None], *[out_g[n] for n in WEIGHTS], *[out_d[n] for n in WEIGHTS],
            *[out_m[n] for n in WEIGHTS], *[out_v[n] for n in WEIGHTS])
```

```python
import functools
import math

import numpy as np
import jax
import jax.numpy as jnp
from jax import lax
from jax.experimental import pallas as pl
from jax.experimental.pallas import tpu as pltpu

F32 = jnp.float32
BF16 = jnp.bfloat16
HI = lax.Precision.HIGHEST

D_MODEL = 1024
D_DN = 512
DN_HEADS = 4
DN_HD = 128
CONV_W = 4
CHUNK = 64
D_ATT = 512
ATT_HEADS = 8
ATT_HD = 64
PATTERNS = ((128, 1), (512, 4), (2048, 16))
N_BUCKETS = 32
MAX_DISTANCE = 2048
D_IN = 4 * D_DN + 2 * DN_HEADS + 4 * D_ATT
D_IN_PAD = 4224
EPS = 1e-6
BLK = 128
NEG = -1e30
N_CHIPS = 4

ADAM_LR = 0.001
ADAM_B1 = 0.9
ADAM_B2 = 0.999
ADAM_EPS = 1e-08
ADAM_WD = 0.01
ADAM_STEP = 10

VMEM_LIMIT = 56 * 1024 * 1024

COL_Z = 3
COL_ATT_Q = 4
COL_ATT_K = 5
COL_ATT_V = 6
COL_GATE = 7
COL_BA_128 = 32


def _cp(sem=None):
    if sem is None:
        return pltpu.CompilerParams(vmem_limit_bytes=VMEM_LIMIT)
    return pltpu.CompilerParams(dimension_semantics=sem, vmem_limit_bytes=VMEM_LIMIT)


def _sds(shape, dtype=F32):
    return jax.ShapeDtypeStruct(shape, dtype)


def _mm(a, b):
    return jnp.dot(a.astype(BF16), b.astype(BF16), preferred_element_type=F32)


def _mm_nt(a, b):
    return lax.dot_general(a.astype(BF16), b.astype(BF16), (((1,), (1,)), ((), ())),
                           preferred_element_type=F32)


def _mm_tn(a, b):
    return lax.dot_general(a.astype(BF16), b.astype(BF16), (((0,), (0,)), ((), ())),
                           preferred_element_type=F32)


def _mmx(a, b):
    return jnp.dot(a, b, precision=HI, preferred_element_type=F32)


def _mmx_nt(a, b):
    return lax.dot_general(a, b, (((1,), (1,)), ((), ())), precision=HI, preferred_element_type=F32)


def _mmx_tn(a, b):
    return lax.dot_general(a, b, (((0,), (0,)), ((), ())), precision=HI, preferred_element_type=F32)


def _dot(a, b):
    return jnp.dot(a, b, preferred_element_type=F32)


def _dot_nt(a, b):
    return lax.dot_general(a, b, (((1,), (1,)), ((), ())), preferred_element_type=F32)


def _dot_tn(a, b):
    return lax.dot_general(a, b, (((0,), (0,)), ((), ())), preferred_element_type=F32)


def _split(a):
    hi = a.astype(BF16)
    return hi, (a - hi.astype(F32)).astype(BF16)


def _mm3(a_s, b_s):
    return _dot(a_s[0], b_s[0]) + _dot(a_s[0], b_s[1]) + _dot(a_s[1], b_s[0])


def _mm3_tn(a_s, b_s):
    return _dot_tn(a_s[0], b_s[0]) + _dot_tn(a_s[0], b_s[1]) + _dot_tn(a_s[1], b_s[0])


def _interleave(gens):
    live = list(gens)
    while live:
        nxt = []
        for g in live:
            try:
                next(g)
                nxt.append(g)
            except StopIteration:
                pass
        live = nxt


def _segsum(x, bd):
    hi = x.astype(BF16)
    r1 = x - hi.astype(F32)
    mid = r1.astype(BF16)
    lo = (r1 - mid.astype(F32)).astype(BF16)
    return (jnp.dot(hi, bd, preferred_element_type=F32) + jnp.dot(mid, bd, preferred_element_type=F32)
            + jnp.dot(lo, bd, preferred_element_type=F32))


def _compact_mat(seg, n=512):
    slot = 128 * seg // n
    src = np.arange(n)[:, None]
    dst = np.arange(128)[None, :]
    return jnp.asarray((src // seg == dst // slot).astype(np.float32), dtype=BF16)


def _seg_mean(x, cm, seg):
    hi, lo = _split(x)
    return (_dot(hi, cm) + _dot(lo, cm)) * (1.0 / seg)


def _seg_expand(c, cm, seg):
    hi, lo = _split(c)
    return (_dot_nt(hi, cm) + _dot_nt(lo, cm)) * (cm.shape[0] / (128.0 * seg))


def _seg_rstd(x, cm, seg):
    return _seg_expand(lax.rsqrt(_seg_mean(x * x, cm, seg) + EPS), cm, seg)


def _sigmoid(x):
    return 1.0 / (1.0 + jnp.exp(-x))


def _silu_grad(x, s):
    return s * (1.0 + x * (1.0 - s))


def _tri_incl():
    i = np.arange(CHUNK)
    return jnp.asarray((i[:, None] >= i[None, :]).astype(np.float32))


def _t5_bucket(dist):
    max_exact = N_BUCKETS // 2
    d = np.maximum(dist, 1).astype(np.float64)
    large = max_exact + (np.log(d / max_exact) / math.log(MAX_DISTANCE / max_exact)
                         * (N_BUCKETS - max_exact)).astype(np.int32)
    large = np.minimum(large, N_BUCKETS - 1)
    return np.where(dist < max_exact, dist, large).astype(np.int32)


def _bucket_tables():
    qi = np.arange(BLK)[:, None]
    kj = np.arange(2 * BLK)[None, :]
    step = qi - kj + BLK
    return jnp.asarray(np.stack([_t5_bucket(np.clip(step, 0, None) * r) for _, r in PATTERNS]))


def _in_proj(x, norm_w, wt_bf, conv_w8, alog_l, dtb_l):
    s = x.shape[0]
    tm = 512

    def body(x_ref, nw_ref, w_ref, cw_ref, al_ref, dt_ref, o_ref, q_ref, k_ref, v_ref, bg_ref, halo_ref):
        i = pl.program_id(0)

        @pl.when(i == 0)
        def _():
            halo_ref[...] = jnp.zeros_like(halo_ref)

        xv = x_ref[...]
        rstd = lax.rsqrt(jnp.mean(xv * xv, axis=-1, keepdims=True) + EPS)
        h = (xv * rstd * nw_ref[...]).astype(BF16)
        o_ref[...] = _dot_nt(h, w_ref[...])
        for c in range(12):
            lanes = slice(128 * c, 128 * c + 128)
            y = _conv_group(o_ref[:, lanes], halo_ref[:, lanes], cw_ref, c)
            sv = y * _sigmoid(y)
            if c < 8:
                rs = lax.rsqrt(jnp.sum(sv * sv, axis=1, keepdims=True) + EPS)
                n = sv * rs
                if c < 4:
                    q_ref[:, lanes] = n * (DN_HD ** -0.5)
                else:
                    k_ref[:, slice(128 * (c - 4), 128 * (c - 3))] = n
            else:
                v_ref[:, slice(128 * (c - 8), 128 * (c - 7))] = sv
        bg_ref[...] = _beta_g(o_ref[:, 128 * COL_BA_128:128 * COL_BA_128 + 128], al_ref[...], dt_ref[...])[0]
        halo_ref[...] = o_ref[tm - 8:tm, 0:3 * D_DN]

    rows = pl.BlockSpec((tm, 512), lambda i: (i, 0))
    return pl.pallas_call(
        body, grid=(s // tm,),
        in_specs=[pl.BlockSpec((tm, D_MODEL), lambda i: (i, 0)),
                  pl.BlockSpec((1, D_MODEL), lambda i: (0, 0)),
                  pl.BlockSpec((D_IN_PAD, D_MODEL), lambda i: (0, 0)),
                  pl.BlockSpec((8, 3 * D_DN), lambda i: (0, 0)),
                  pl.BlockSpec((1, 128), lambda i: (0, 0)), pl.BlockSpec((1, 128), lambda i: (0, 0))],
        out_specs=[pl.BlockSpec((tm, D_IN_PAD), lambda i: (i, 0)), rows, rows, rows,
                   pl.BlockSpec((tm, 128), lambda i: (i, 0))],
        out_shape=[_sds((s, D_IN_PAD))] + [_sds((s, 512))] * 3 + [_sds((s, 128))],
        scratch_shapes=[pltpu.VMEM((8, 3 * D_DN), F32)],
        compiler_params=_cp(("arbitrary",)), name="in_proj")(x, norm_w, wt_bf, conv_w8, alog_l, dtb_l)


def _conv_group(cur, halo, w_ref, c):
    rows = cur.shape[0]
    lanes = slice(128 * c, 128 * c + 128)
    xcat = jnp.concatenate([halo, cur], axis=0)
    y = cur * w_ref[CONV_W - 1:CONV_W, lanes]
    for k in range(1, CONV_W):
        sh = pltpu.roll(xcat, k, 0)[8:8 + rows]
        y = y + sh * w_ref[CONV_W - 1 - k:CONV_W - k, lanes]
    return y


def _beta_g(ba, alog_l, dtb_l):
    lane = lax.broadcasted_iota(jnp.int32, ba.shape, 1)
    sig_b = _sigmoid(ba)
    t = ba + dtb_l
    softplus = jnp.maximum(t, 0.0) + jnp.log(1.0 + jnp.exp(-jnp.abs(t)))
    nega = -jnp.exp(alog_l)
    g = nega * softplus
    out = jnp.where(lane < DN_HEADS, sig_b, jnp.where(lane < 2 * DN_HEADS, g, 0.0))
    return out, lane, sig_b, t, nega, g


CPS = 8
CPS_SCAN = 8
SPLIT_ITERS = 2


def _split3(a):
    hi = a.astype(BF16)
    r1 = a - hi.astype(F32)
    mid = r1.astype(BF16)
    return hi, mid, (r1 - mid.astype(F32)).astype(BF16)


def _lane_select():
    r = np.arange(128)
    return jnp.asarray((r[None, :, None] == np.arange(8)[:, None, None]) * np.ones((1, 1, 128)), dtype=BF16)


def _lane_bcast(a3, sel):
    return _dot(a3[0], sel) + _dot(a3[1], sel) + _dot(a3[2], sel)


def _rowsum_b(z, ones_b):
    hi, lo = _split(z)
    return _dot(hi, ones_b) + _dot(lo, ones_b)


def _chunk_cumsum(bg, bgt, tri):
    return _split3(bg), _split3(_mmx(tri, bg)), _mmx_nt(bgt, tri)


def _chunk_common(bg3, gc3, gc_row, h, sel_ref):
    gcc = _lane_bcast(gc3, sel_ref[DN_HEADS + h])
    beta = _dot(bg3[0], sel_ref[h]) + _dot(bg3[1], sel_ref[h])
    gcr = gc_row[DN_HEADS + h:DN_HEADS + h + 1, :]
    ii = lax.broadcasted_iota(jnp.int32, (CHUNK, CHUNK), 0)
    jj = lax.broadcasted_iota(jnp.int32, (CHUNK, CHUNK), 1)
    incl = ii >= jj
    strict = ii > jj
    decay = jnp.exp(jnp.where(incl, gcc[:, 0:CHUNK] - gcr, NEG))
    gl = gcc[CHUNK - 1:CHUNK, :]
    return gcc, beta, incl, strict, decay, gl


def _dn_prep(qn, kn, v, bg, bgt, tri, sel):
    s = qn.shape[0]
    nc = s // CHUNK

    def body(q_ref, k_ref, v_ref, bg_ref, bgt_ref, tri_ref, sel_ref,
             u_ref, w_ref, qd_ref, kt_ref, attn_ref, t_ref, gl_ref):
        tri_v = tri_ref[...]
        ii = lax.broadcasted_iota(jnp.int32, (CHUNK, CHUNK), 0)
        jj = lax.broadcasted_iota(jnp.int32, (CHUNK, CHUNK), 1)
        eye = (ii == jj).astype(F32)

        def chain(cc, h, bg3, gc3, gc_row):
            rows = slice(CHUNK * cc, CHUNK * cc + CHUNK)
            lanes = slice(128 * h, 128 * h + 128)
            gcc, beta, incl, strict, decay, gl = _chunk_common(bg3, gc3, gc_row, h, sel_ref)
            yield
            q = q_ref[rows, lanes]
            k = k_ref[rows, lanes]
            vv = v_ref[rows, lanes]
            kb = k * beta
            egc = jnp.exp(gcc)
            a_mat = jnp.where(strict, _mm_nt(kb, k) * decay, 0.0)
            attn_ref[cc, h] = jnp.where(incl, _mm_nt(q, k) * decay, 0.0)
            qd_ref[rows, lanes] = (q * egc).astype(BF16)
            kt_ref[rows, lanes] = (k * jnp.exp(gl - gcc)).astype(BF16)
            gl_ref[cc, h] = jnp.exp(gl)
            yield
            p = -a_mat
            t = eye + p
            for it in range(5):
                if it < SPLIT_ITERS:
                    ps = _split(p)
                    p = _mm3(ps, ps)
                    yield
                    t = t + _mm3(_split(t), _split(p))
                else:
                    p = _mm(p, p)
                    yield
                    t = t + _mm(t, p)
                yield
            t_ref[cc, h] = t
            ts = _split(t)
            u_ref[rows, lanes] = _mm3(ts, _split(vv * beta))
            w_ref[rows, lanes] = _mm3(ts, _split(kb * egc)).astype(BF16)

        gens = []
        for cc in range(CPS):
            bgv = bg_ref[CHUNK * cc:CHUNK * cc + CHUNK, :]
            bg3, gc3, gc_row = _chunk_cumsum(bgv, bgt_ref[cc], tri_v)
            gens += [chain(cc, h, bg3, gc3, gc_row) for h in range(DN_HEADS)]
        _interleave(gens)

    rows_step = CPS * CHUNK
    big = pl.BlockSpec((rows_step, 512), lambda n: (n, 0))
    sq = pl.BlockSpec((CPS, DN_HEADS, CHUNK, CHUNK), lambda n: (n, 0, 0, 0))
    return pl.pallas_call(
        body, grid=(nc // CPS,),
        in_specs=[big, big, big, pl.BlockSpec((rows_step, 128), lambda n: (n, 0)),
                  pl.BlockSpec((CPS, 8, CHUNK), lambda n: (n, 0, 0)),
                  pl.BlockSpec((CHUNK, CHUNK), lambda n: (0, 0)),
                  pl.BlockSpec((8, 128, 128), lambda n: (0, 0, 0))],
        out_specs=[big, big, big, big, sq, sq, pl.BlockSpec((CPS, DN_HEADS, 1, 128), lambda n: (n, 0, 0, 0))],
        out_shape=[_sds((s, 512))] + [_sds((s, 512), BF16)] * 3 + [_sds((nc, DN_HEADS, CHUNK, CHUNK))] * 2
        + [_sds((nc, DN_HEADS, 1, 128))],
        compiler_params=_cp(("parallel",)), name="dn_prep")(qn, kn, v, bg, bgt, tri, sel)


def _dn_scan(u, w, qd, kt, attn, gl):
    s = u.shape[0]
    nc = s // CHUNK

    def body(u_ref, w_ref, qd_ref, kt_ref, attn_ref, gl_ref, o_ref, vn_ref, sp_ref, st_ref):
        n = pl.program_id(0)

        @pl.when(n == 0)
        def _():
            st_ref[...] = jnp.zeros_like(st_ref)

        def chain(cc, h):
            rows = slice(CHUNK * cc, CHUNK * cc + CHUNK)
            lanes = slice(128 * h, 128 * h + 128)
            st = st_ref[h]
            sp_ref[cc, h] = st
            stb = st.astype(BF16)
            ws = _dot(w_ref[rows, lanes].astype(BF16), stb)
            qs = _dot(qd_ref[rows, lanes].astype(BF16), stb)
            yield
            vn = u_ref[rows, lanes] - ws
            vnb = vn.astype(BF16)
            vn_ref[rows, lanes] = vnb
            o_ref[rows, lanes] = qs + _dot(attn_ref[cc, h].astype(BF16), vnb)
            st_ref[h] = st * gl_ref[cc, h] + _dot_tn(kt_ref[rows, lanes].astype(BF16), vnb)

        for cc in range(CPS_SCAN):
            _interleave([chain(cc, h) for h in range(DN_HEADS)])

    big = pl.BlockSpec((CPS_SCAN * CHUNK, 512), lambda n: (n, 0))
    return pl.pallas_call(
        body, grid=(nc // CPS_SCAN,),
        in_specs=[big, big, big, big,
                  pl.BlockSpec((CPS_SCAN, DN_HEADS, CHUNK, CHUNK), lambda n: (n, 0, 0, 0)),
                  pl.BlockSpec((CPS_SCAN, DN_HEADS, 1, 128), lambda n: (n, 0, 0, 0))],
        out_specs=[big, big, pl.BlockSpec((CPS_SCAN, DN_HEADS, DN_HD, DN_HD), lambda n: (n, 0, 0, 0))],
        out_shape=[_sds((s, 512)), _sds((s, 512), BF16), _sds((nc, DN_HEADS, DN_HD, DN_HD))],
        scratch_shapes=[pltpu.VMEM((DN_HEADS, DN_HD, DN_HD), F32)],
        compiler_params=_cp(("arbitrary",)), name="dn_scan")(u, w, qd, kt, attn, gl)


R4 = PATTERNS[1][1]
R16 = PATTERNS[2][1]
TM = 512


def _pattern_spec(r, width=512):
    return pl.BlockSpec((r, TM // r, width), lambda i: (0, i, 0))


def _pattern_shape(s, r, dtype=F32, width=512):
    return _sds((r, s // r, width), dtype)


SLABS = pltpu.VMEM((4, TM, 128), F32)

HEAD_SLOT = 128 // ATT_HEADS


def _head_expand():
    src = np.arange(128)[:, None]
    dst = np.arange(512)[None, :]
    return jnp.asarray((src == (dst // ATT_HD) * HEAD_SLOT).astype(np.float32), dtype=BF16)


def _head_compact():
    src = np.arange(512)[:, None]
    dst = np.arange(128)[None, :]
    return jnp.asarray((src // ATT_HD == dst // HEAD_SLOT).astype(np.float32), dtype=BF16)


def _to_patterns(val, dsts, scr):
    for c in range(val.shape[1] // 128):
        lanes = slice(128 * c, 128 * c + 128)
        scr[c] = val[:, lanes]
        for dst_ref, r in dsts:
            for a in range(r):
                dst_ref[a, :, lanes] = scr[c, pl.ds(a, TM // r, stride=r), :].astype(dst_ref.dtype)


def _from_pattern(src_ref, r, scr):
    n_slab = src_ref.shape[2] // 128
    for c in range(n_slab):
        for a in range(r):
            scr[c, pl.ds(a, TM // r, stride=r), :] = src_ref[a, :, 128 * c:128 * c + 128].astype(F32)
    return jnp.concatenate([scr[c] for c in range(n_slab)], axis=1) if n_slab > 1 else scr[0]


def _att_pre(proj, qw_t, kw_t, bd64):
    s = proj.shape[0]

    def body(q_ref, k_ref, v_ref, qw_ref, kw_ref, bd_ref,
             q1_ref, k1_ref, v1_ref, q4_ref, k4_ref, v4_ref, q16_ref, k16_ref, v16_ref, scr):
        bd = bd_ref[...]
        q = q_ref[...]
        k = k_ref[...]
        qn = q * _seg_rstd(q, bd, ATT_HD) * qw_ref[...] * (ATT_HD ** -0.5)
        kn = k * _seg_rstd(k, bd, ATT_HD) * kw_ref[...]
        q1_ref[...] = qn.astype(BF16)
        k1_ref[...] = kn.astype(BF16)
        v1_ref[...] = v_ref[...].astype(BF16)
        _to_patterns(qn, ((q4_ref, R4), (q16_ref, R16)), scr)
        _to_patterns(kn, ((k4_ref, R4), (k16_ref, R16)), scr)
        _to_patterns(v_ref[...], ((v4_ref, R4), (v16_ref, R16)), scr)

    row = pl.BlockSpec((1, 512), lambda i: (0, 0))
    tok = pl.BlockSpec((TM, 512), lambda i: (i, 0))
    return pl.pallas_call(
        body, grid=(s // TM,),
        in_specs=[pl.BlockSpec((TM, 512), lambda i: (i, COL_ATT_Q)),
                  pl.BlockSpec((TM, 512), lambda i: (i, COL_ATT_K)),
                  pl.BlockSpec((TM, 512), lambda i: (i, COL_ATT_V)),
                  row, row, pl.BlockSpec((512, 128), lambda i: (0, 0))],
        out_specs=[tok] * 3 + [_pattern_spec(R4)] * 3 + [_pattern_spec(R16)] * 3,
        out_shape=[_sds((s, 512), BF16)] * 3 + [_pattern_shape(s, R4, BF16)] * 3 + [_pattern_shape(s, R16, BF16)] * 3,
        scratch_shapes=[SLABS],
        compiler_params=_cp(("parallel",)), name="att_pre")(proj, proj, proj, qw_t, kw_t, bd64)


def _bias_table(rb_ref, bk_ref, o_ref):
    for p in range(len(PATTERNS)):
        bk = bk_ref[p]
        for h in range(ATT_HEADS):
            acc = jnp.zeros((BLK, 2 * BLK), F32)
            for b in range(N_BUCKETS):
                acc = jnp.where(bk == b, rb_ref[h, b], acc)
            o_ref[p, h] = acc


def _bias_grad(ds_refs, bk_ref, o_ref):
    for h in range(ATT_HEADS):
        for b in range(N_BUCKETS):
            tot = jnp.zeros((), F32)
            for p, ds_ref in enumerate(ds_refs):
                tot = tot + jnp.sum(jnp.where(bk_ref[p] == b, ds_ref[h], 0.0))
            o_ref[h, b] = tot


QB_FWD = 2
QB_BWD = 4


def _att_masks(has_prev):
    qi = lax.broadcasted_iota(jnp.int32, (BLK, BLK), 0)
    kj = lax.broadcasted_iota(jnp.int32, (BLK, BLK), 1)
    lane = lax.broadcasted_iota(jnp.int32, (BLK, 2 * ATT_HD), 1)
    return jnp.logical_and(kj >= qi, has_prev), kj <= qi, lane < ATT_HD


def _head_lanes(h):
    half = h % 2
    return slice(ATT_HD * h, ATT_HD * h + ATT_HD), slice(ATT_HD * half, ATT_HD * half + ATT_HD)


def _att_scores(qm, kp2, kc2, bias_h, mask_prev, mask_cur):
    s_prev = jnp.where(mask_prev, _dot_nt(qm, kp2) + bias_h[:, :BLK], NEG)
    s_cur = jnp.where(mask_cur, _dot_nt(qm, kc2) + bias_h[:, BLK:], NEG)
    return s_prev, s_cur


def _att_fwd(q, k, v, bias, p_idx, r, name):
    QB = QB_FWD
    s = q.shape[0]
    nblk = s // BLK
    nseq = nblk // r

    def body(q_ref, kp_ref, kc_ref, vp_ref, vc_ref, b_ref, o_ref, lse_ref):
        j = pl.program_id(0)

        def head(h, rows, masks, q2, kp2, kc2, vp2, vc2):
            mask_prev, mask_cur, lo_half = masks
            out_l, pair_l = _head_lanes(h)
            sel = lo_half if h % 2 == 0 else jnp.logical_not(lo_half)
            qm = jnp.where(sel, q2, jnp.zeros_like(q2))
            s_prev, s_cur = _att_scores(qm, kp2, kc2, b_ref[0, h], mask_prev, mask_cur)
            yield
            m = jnp.maximum(jnp.max(s_prev, axis=1, keepdims=True), jnp.max(s_cur, axis=1, keepdims=True))
            p_prev = jnp.exp(s_prev - m)
            p_cur = jnp.exp(s_cur - m)
            l = jnp.sum(p_prev, axis=1, keepdims=True) + jnp.sum(p_cur, axis=1, keepdims=True)
            yield
            o2 = _dot(p_prev.astype(BF16), vp2) + _dot(p_cur.astype(BF16), vc2)
            o_ref[rows, out_l] = (o2 * (1.0 / l))[:, pair_l].astype(BF16)
            lse_ref[rows, HEAD_SLOT * h:HEAD_SLOT * h + HEAD_SLOT] = jnp.broadcast_to(m + jnp.log(l), (BLK, HEAD_SLOT))

        for sub in range(QB):
            rows = slice(BLK * sub, BLK * sub + BLK)
            before = slice(BLK * (sub - 1), BLK * sub)
            masks = _att_masks(((QB * j + sub) % nseq) != 0)
            gens = []
            for pp in range(ATT_HEADS // 2):
                lanes = slice(128 * pp, 128 * pp + 128)
                kp = kp_ref[:, lanes] if sub == 0 else kc_ref[before, lanes]
                vp = vp_ref[:, lanes] if sub == 0 else vc_ref[before, lanes]
                slabs = (q_ref[rows, lanes], kp, kc_ref[rows, lanes], vp, vc_ref[rows, lanes])
                gens += [head(2 * pp, rows, masks, *slabs), head(2 * pp + 1, rows, masks, *slabs)]
            _interleave(gens)

    cur = pl.BlockSpec((QB * BLK, 512), lambda j: (j, 0))
    prev = pl.BlockSpec((BLK, 512), lambda j: (jnp.maximum(QB * j - 1, 0), 0))
    return pl.pallas_call(
        body, grid=(nblk // QB,),
        in_specs=[cur, prev, cur, prev, cur,
                  pl.BlockSpec((1, ATT_HEADS, BLK, 2 * BLK), lambda j: (p_idx, 0, 0, 0))],
        out_specs=[cur, pl.BlockSpec((QB * BLK, 128), lambda j: (j, 0))],
        out_shape=[_sds((s, 512), BF16), _sds((s, 128))],
        compiler_params=_cp(("parallel",)), name=name)(q, k, k, v, v, bias)


def _post_fwd(o_dn, proj, o_pats, lse_pats, dnw_t, bd128):
    s = o_dn.shape[0]

    def body(o_ref, z_ref, gate_ref, o1_ref, o4_ref, o16_ref, s1_ref, s4_ref, s16_ref, wn_ref, bd_ref, ex_ref,
             mixed_ref, oatt_ref, l1_ref, l4_ref, l16_ref, scr_a, scr_b, scr_c, scr_d):
        o = o_ref[...]
        z = z_ref[...]
        rstd = _seg_rstd(o, bd_ref[...], DN_HD)
        y_dn = o * rstd * wn_ref[...] * (z * _sigmoid(z))
        mixed_ref[:, 0:512] = y_dn.astype(BF16)
        lses = (s1_ref[...], _from_pattern(s4_ref, R4, scr_a), _from_pattern(s16_ref, R16, scr_b))
        m = jnp.maximum(jnp.maximum(lses[0], lses[1]), lses[2])
        tot = jnp.exp(lses[0] - m) + jnp.exp(lses[1] - m) + jnp.exp(lses[2] - m)
        big_l = m + jnp.log(tot)
        l1_ref[...] = big_l
        _to_patterns(big_l, ((l4_ref, R4), (l16_ref, R16)), scr_a)
        ex = ex_ref[...]
        outs = (o1_ref[...], _from_pattern(o4_ref, R4, scr_c), _from_pattern(o16_ref, R16, scr_d))
        acc = jnp.zeros_like(o)
        for lse_p, o_p in zip(lses, outs):
            acc = acc + _lane_bcast(_split3(jnp.exp(lse_p - big_l)), ex) * o_p
        gate = gate_ref[...]
        oatt_ref[...] = acc
        mixed_ref[:, 512:1024] = (acc * (gate * _sigmoid(gate))).astype(BF16)

    blk = pl.BlockSpec((TM, 512), lambda i: (i, 0))
    cblk = pl.BlockSpec((TM, 128), lambda i: (i, 0))
    p4, p16 = _pattern_spec(R4), _pattern_spec(R16)
    c4, c16 = _pattern_spec(R4, 128), _pattern_spec(R16, 128)
    return pl.pallas_call(
        body, grid=(s // TM,),
        in_specs=[blk, pl.BlockSpec((TM, 512), lambda i: (i, COL_Z)),
                  pl.BlockSpec((TM, 512), lambda i: (i, COL_GATE)), blk, p4, p16, cblk, c4, c16,
                  pl.BlockSpec((1, 512), lambda i: (0, 0)), pl.BlockSpec((512, 128), lambda i: (0, 0)),
                  pl.BlockSpec((128, 512), lambda i: (0, 0))],
        out_specs=[pl.BlockSpec((TM, D_MODEL), lambda i: (i, 0)), blk, cblk, c4, c16],
        out_shape=[_sds((s, D_MODEL), BF16), _sds((s, 512)), _sds((s, 128)), _pattern_shape(s, R4, F32, 128),
                   _pattern_shape(s, R16, F32, 128)],
        scratch_shapes=[SLABS] * 4,
        compiler_params=_cp(("parallel",)), name="post_fwd")(o_dn, proj, proj, *o_pats, *lse_pats, dnw_t, bd128,
                                                              _head_expand())


def _out_post(x, mixed, w_out_bf, tgt, o_dn, proj, o_att, dnw_t, bd128):
    s = o_dn.shape[0]
    tm = TM

    def body(x_ref, m_ref, w_ref, t_ref, o_ref, z_ref, gate_ref, oatt_ref, wn_ref, bd128_ref, hc_ref,
             dy_ref, loss_ref, dw_ref, do_ref, dz_ref, dgate_ref, doatt_ref, do4_ref, do16_ref, delta_ref, dl4_ref,
             dl16_ref, dnw_ref, scr):
        i = pl.program_id(0)

        @pl.when(i == 0)
        def _():
            loss_ref[...] = jnp.zeros_like(loss_ref)
            dw_ref[...] = jnp.zeros_like(dw_ref)
            dnw_ref[...] = jnp.zeros_like(dnw_ref)

        err = x_ref[...] + _dot(m_ref[...], w_ref[...]) - t_ref[...]
        d_y = err * (1.0 / D_MODEL)
        dy_ref[...] = d_y
        part = 0.5 * jnp.sum(jnp.mean(err * err, axis=-1, keepdims=True), axis=0, keepdims=True)
        loss_ref[...] = loss_ref[...] + part
        dyb = d_y.astype(BF16)
        dw_ref[...] = dw_ref[...] + _dot_tn(m_ref[...], dyb)
        dm = _dot_nt(dyb, w_ref[...])

        bd128v = bd128_ref[...]
        o = o_ref[...]
        z = z_ref[...]
        wn = wn_ref[...]
        dy = dm[:, 0:512]
        rstd = _seg_rstd(o, bd128v, DN_HD)
        nrm = o * rstd
        sz = _sigmoid(z)
        dz_ref[...] = (dy * nrm * wn * _silu_grad(z, sz)).astype(BF16)
        dn = dy * z * sz
        gw = dn * wn
        do_ref[...] = rstd * (gw - nrm * _seg_expand(_seg_mean(gw * nrm, bd128v, DN_HD), bd128v, DN_HD))
        colsum = jnp.sum(dn * nrm, axis=0, keepdims=True)
        fold = colsum[:, 0:128] + colsum[:, 128:256] + colsum[:, 256:384] + colsum[:, 384:512]
        dnw_ref[...] = dnw_ref[...] + fold
        dya = dm[:, 512:1024]
        gate = gate_ref[...]
        oatt = oatt_ref[...]
        sg = _sigmoid(gate)
        dgate_ref[...] = (dya * oatt * _silu_grad(gate, sg)).astype(BF16)
        doa = dya * gate * sg
        doatt_ref[...] = doa.astype(BF16)
        delta = _segsum(doa * oatt, hc_ref[...])
        delta_ref[...] = delta
        _to_patterns(doa, ((do4_ref, R4), (do16_ref, R16)), scr)
        _to_patterns(delta, ((dl4_ref, R4), (dl16_ref, R16)), scr)

    wide = pl.BlockSpec((tm, D_MODEL), lambda i: (i, 0))
    full = pl.BlockSpec((D_MODEL, D_MODEL), lambda i: (0, 0))
    blk = pl.BlockSpec((tm, 512), lambda i: (i, 0))
    cblk = pl.BlockSpec((tm, 128), lambda i: (i, 0))
    p4, p16 = _pattern_spec(R4), _pattern_spec(R16)
    c4, c16 = _pattern_spec(R4, 128), _pattern_spec(R16, 128)
    return pl.pallas_call(
        body, grid=(s // tm,),
        in_specs=[wide, wide, full, wide, blk,
                  pl.BlockSpec((tm, 512), lambda i: (i, COL_Z)), pl.BlockSpec((tm, 512), lambda i: (i, COL_GATE)),
                  blk, pl.BlockSpec((1, 512), lambda i: (0, 0)), pl.BlockSpec((512, 128), lambda i: (0, 0)),
                  pl.BlockSpec((512, 128), lambda i: (0, 0))],
        out_specs=[wide, pl.BlockSpec((8, 128), lambda i: (0, 0)), full,
                   blk, blk, blk, blk, p4, p16, cblk, c4, c16, pl.BlockSpec((8, 128), lambda i: (0, 0))],
        out_shape=[_sds((s, D_MODEL)), _sds((8, 128)), _sds((D_MODEL, D_MODEL)), _sds((s, 512))]
        + [_sds((s, 512), BF16)] * 3 + [_pattern_shape(s, R4, BF16), _pattern_shape(s, R16, BF16), _sds((s, 128)),
                                        _pattern_shape(s, R4, F32, 128), _pattern_shape(s, R16, F32, 128),
                                        _sds((8, 128))],
        scratch_shapes=[SLABS],
        compiler_params=_cp(("arbitrary",)), name="out_post")(x, mixed, w_out_bf, tgt, o_dn, proj, proj, o_att, dnw_t,
                                                               bd128, _head_compact())


def _att_bwd(q, k, v, do, big_l, delta, bias, p_idx, r, name):
    QB = QB_BWD
    s = q.shape[0]
    nblk = s // BLK
    nseq = nblk // r
    nstep = nblk // QB

    def body(q_ref, kp_ref, kc_ref, vp_ref, vc_ref, do_ref, l_ref, dl_ref, b_ref,
             dq_ref, dk_ref, dv_ref, ds_ref, dkc_ref, dvc_ref):
        j = pl.program_id(0)

        @pl.when(j == 0)
        def _():
            dkc_ref[...] = jnp.zeros_like(dkc_ref)
            dvc_ref[...] = jnp.zeros_like(dvc_ref)
            ds_ref[...] = jnp.zeros_like(ds_ref)

        @pl.when(j < nstep)
        def _():
            def head(h, sub, masks, q2, do2, kp2, kc2, vp2, vc2):
                mask_prev, mask_cur, lo_half = masks
                rows = slice(BLK * sub, BLK * sub + BLK)
                out_l, pair_l = _head_lanes(h)
                sel = lo_half if h % 2 == 0 else jnp.logical_not(lo_half)
                qm = jnp.where(sel, q2, jnp.zeros_like(q2))
                dom = jnp.where(sel, do2, jnp.zeros_like(do2))
                s_prev, s_cur = _att_scores(qm, kp2, kc2, b_ref[0, h], mask_prev, mask_cur)
                dp_prev = _dot_nt(dom, vp2)
                dp_cur = _dot_nt(dom, vc2)
                yield
                lh = l_ref[rows, HEAD_SLOT * h:HEAD_SLOT * h + 1]
                dh = dl_ref[rows, HEAD_SLOT * h:HEAD_SLOT * h + 1]
                p_prev = jnp.exp(s_prev - lh)
                p_cur = jnp.exp(s_cur - lh)
                ds_prev = p_prev * (dp_prev - dh)
                ds_cur = p_cur * (dp_cur - dh)
                ds_ref[h, :, 0:BLK] = ds_ref[h, :, 0:BLK] + ds_prev
                ds_ref[h, :, BLK:2 * BLK] = ds_ref[h, :, BLK:2 * BLK] + ds_cur
                dsb_prev, dsb_cur = ds_prev.astype(BF16), ds_cur.astype(BF16)
                pb_prev, pb_cur = p_prev.astype(BF16), p_cur.astype(BF16)
                yield
                dq_ref[rows, out_l] = (_dot(dsb_prev, kp2) + _dot(dsb_cur, kc2))[:, pair_l].astype(BF16)
                dk_prev = _dot_tn(dsb_prev, q2)[:, pair_l]
                dv_prev = _dot_tn(pb_prev, do2)[:, pair_l]
                if sub == 0:
                    last = slice(BLK * (QB - 1), BLK * QB)
                    dk_ref[last, out_l] = (dkc_ref[last, out_l] + dk_prev).astype(BF16)
                    dv_ref[last, out_l] = (dvc_ref[last, out_l] + dv_prev).astype(BF16)
                else:
                    before = slice(BLK * (sub - 1), BLK * sub)
                    dkc_ref[before, out_l] = dkc_ref[before, out_l] + dk_prev
                    dvc_ref[before, out_l] = dvc_ref[before, out_l] + dv_prev
                yield
                dkc_ref[rows, out_l] = _dot_tn(dsb_cur, q2)[:, pair_l]
                dvc_ref[rows, out_l] = _dot_tn(pb_cur, do2)[:, pair_l]

            done = slice(0, BLK * (QB - 1))
            dk_ref[done, :] = dkc_ref[done, :].astype(BF16)
            dv_ref[done, :] = dvc_ref[done, :].astype(BF16)
            for sub in range(QB):
                rows = slice(BLK * sub, BLK * sub + BLK)
                before = slice(BLK * (sub - 1), BLK * sub)
                masks = _att_masks(((QB * j + sub) % nseq) != 0)
                gens = []
                for pp in range(ATT_HEADS // 2):
                    lanes = slice(128 * pp, 128 * pp + 128)
                    kp = kp_ref[:, lanes] if sub == 0 else kc_ref[before, lanes]
                    vp = vp_ref[:, lanes] if sub == 0 else vc_ref[before, lanes]
                    slabs = (q_ref[rows, lanes], do_ref[rows, lanes], kp, kc_ref[rows, lanes], vp,
                             vc_ref[rows, lanes])
                    gens += [head(2 * pp, sub, masks, *slabs), head(2 * pp + 1, sub, masks, *slabs)]
                _interleave(gens)

        @pl.when(j == nstep)
        def _():
            dk_ref[...] = dkc_ref[...].astype(BF16)
            dv_ref[...] = dvc_ref[...].astype(BF16)

    last_step = nstep - 1
    cur = pl.BlockSpec((QB * BLK, 512), lambda j: (jnp.minimum(j, last_step), 0))
    compact = pl.BlockSpec((QB * BLK, 128), lambda j: (jnp.minimum(j, last_step), 0))
    lag = pl.BlockSpec((QB * BLK, 512), lambda j: (jnp.clip(j - 1, 0, last_step), 0))
    prev = pl.BlockSpec((BLK, 512), lambda j: (jnp.clip(QB * j - 1, 0, nblk - 1), 0))
    return pl.pallas_call(
        body, grid=(nstep + 1,),
        in_specs=[cur, prev, cur, prev, cur, cur, compact, compact,
                  pl.BlockSpec((1, ATT_HEADS, BLK, 2 * BLK), lambda j: (p_idx, 0, 0, 0))],
        out_specs=[cur, lag, lag, pl.BlockSpec((ATT_HEADS, BLK, 2 * BLK), lambda j: (0, 0, 0))],
        out_shape=[_sds((s, 512), BF16)] * 3 + [_sds((ATT_HEADS, BLK, 2 * BLK))],
        scratch_shapes=[pltpu.VMEM((QB * BLK, 512), F32), pltpu.VMEM((QB * BLK, 512), F32)],
        compiler_params=_cp(("arbitrary",)), name=name)(q, k, k, v, v, do, big_l, delta, bias)


def _att_pre_bwd(dq_pats, dk_pats, dv_pats, proj, qw_t, kw_t, bd64):
    s = proj.shape[0]
    tm = TM

    def body(dq1_ref, dq4_ref, dq16_ref, dk1_ref, dk4_ref, dk16_ref, dv1_ref, dv4_ref, dv16_ref,
             q_ref, k_ref, qw_ref, kw_ref, bd_ref,
             dqr_ref, dkr_ref, dvr_ref, dqw_ref, dkw_ref, scr4, scr16):
        i = pl.program_id(0)

        @pl.when(i == 0)
        def _():
            dqw_ref[...] = jnp.zeros_like(dqw_ref)
            dkw_ref[...] = jnp.zeros_like(dkw_ref)

        bd = bd_ref[...]

        def total(d1_ref, d4_ref, d16_ref):
            return d1_ref[...] + _from_pattern(d4_ref, R4, scr4) + _from_pattern(d16_ref, R16, scr16)

        def one(d_refs, x_ref, w_ref, scale, dx_ref, dw_ref):
            dy = total(*d_refs) * scale
            x = x_ref[...]
            rstd = _seg_rstd(x, bd, ATT_HD)
            nrm = x * rstd
            dw_ref[...] = dw_ref[...] + jnp.sum(dy * nrm, axis=0, keepdims=True)
            g = dy * w_ref[...]
            dx_ref[...] = (rstd * (g - nrm * _seg_expand(_seg_mean(g * nrm, bd, ATT_HD), bd, ATT_HD))).astype(BF16)

        one((dq1_ref, dq4_ref, dq16_ref), q_ref, qw_ref, ATT_HD ** -0.5, dqr_ref, dqw_ref)
        one((dk1_ref, dk4_ref, dk16_ref), k_ref, kw_ref, 1.0, dkr_ref, dkw_ref)
        dvr_ref[...] = total(dv1_ref, dv4_ref, dv16_ref).astype(BF16)

    blk = pl.BlockSpec((tm, 512), lambda i: (i, 0))
    pats = [blk, _pattern_spec(R4), _pattern_spec(R16)]
    row = pl.BlockSpec((1, 512), lambda i: (0, 0))
    acc = pl.BlockSpec((8, 512), lambda i: (0, 0))
    return pl.pallas_call(
        body, grid=(s // tm,),
        in_specs=pats * 3 + [pl.BlockSpec((tm, 512), lambda i: (i, COL_ATT_Q)),
                             pl.BlockSpec((tm, 512), lambda i: (i, COL_ATT_K)), row, row,
                             pl.BlockSpec((512, 128), lambda i: (0, 0))],
        out_specs=[blk, blk, blk, acc, acc],
        out_shape=[_sds((s, 512), BF16)] * 3 + [_sds((8, 512))] * 2,
        scratch_shapes=[SLABS] * 2,
        compiler_params=_cp(("arbitrary",)), name="att_pre_bwd")(*dq_pats, *dk_pats, *dv_pats, proj, proj,
                                                                  qw_t, kw_t, bd64)


def _dn_scan_bwd(do, sp, qd, kt, w, vn, attn, gl):
    s = do.shape[0]
    nc = s // CHUNK

    def body(do_ref, sp_ref, qd_ref, kt_ref, w_ref, vn_ref, attn_ref, gl_ref,
             du_ref, dqd_ref, dkt_ref, dw_ref, dattn_ref, dgl_ref, ds_ref):
        n = pl.program_id(0)

        @pl.when(n == 0)
        def _():
            ds_ref[...] = jnp.zeros_like(ds_ref)

        def chain(cc, h):
            rows = slice(CHUNK * cc, CHUNK * cc + CHUNK)
            lanes = slice(128 * h, 128 * h + 128)
            dsn = ds_ref[h]
            st = sp_ref[cc, h]
            dsb, stb = dsn.astype(BF16), st.astype(BF16)
            dob = do_ref[rows, lanes].astype(BF16)
            vnb = vn_ref[rows, lanes].astype(BF16)
            dvn = _dot_tn(attn_ref[cc, h].astype(BF16), dob) + _dot(kt_ref[rows, lanes].astype(BF16), dsb)
            du_ref[rows, lanes] = dvn
            dqd_ref[rows, lanes] = _dot_nt(dob, stb)
            dattn_ref[cc, h] = _dot_nt(dob, vnb)
            dkt_ref[rows, lanes] = _dot_nt(vnb, dsb)
            tot = jnp.sum(jnp.sum(st * dsn, axis=1, keepdims=True), axis=0, keepdims=True)
            dgl_ref[cc, h] = jnp.broadcast_to(tot, (1, 128))
            qdo = _dot_tn(qd_ref[rows, lanes].astype(BF16), dob)
            yield
            dvb = dvn.astype(BF16)
            dw_ref[rows, lanes] = -_dot_nt(dvb, stb)
            ds_ref[h] = qdo + dsn * gl_ref[cc, h] - _dot_tn(w_ref[rows, lanes].astype(BF16), dvb)

        for cc in reversed(range(CPS_SCAN)):
            _interleave([chain(cc, h) for h in range(DN_HEADS)])

    nsteps = nc // CPS_SCAN
    big = pl.BlockSpec((CPS_SCAN * CHUNK, 512), lambda n: (nsteps - 1 - n, 0))
    sq = pl.BlockSpec((CPS_SCAN, DN_HEADS, CHUNK, CHUNK), lambda n: (nsteps - 1 - n, 0, 0, 0))
    glb = pl.BlockSpec((CPS_SCAN, DN_HEADS, 1, 128), lambda n: (nsteps - 1 - n, 0, 0, 0))
    return pl.pallas_call(
        body, grid=(nsteps,),
        in_specs=[big, pl.BlockSpec((CPS_SCAN, DN_HEADS, DN_HD, DN_HD), lambda n: (nsteps - 1 - n, 0, 0, 0)),
                  big, big, big, big, sq, glb],
        out_specs=[big, big, big, big, sq, glb],
        out_shape=[_sds((s, 512))] * 4 + [_sds((nc, DN_HEADS, CHUNK, CHUNK)), _sds((nc, DN_HEADS, 1, 128))],
        scratch_shapes=[pltpu.VMEM((DN_HEADS, DN_HD, DN_HD), F32)],
        compiler_params=_cp(("arbitrary",)), name="dn_scan_bwd")(do, sp, qd, kt, w, vn, attn, gl)


def _dn_prep_bwd(qn, kn, v, bg, bgt, tri, sel, t_inv, attn, u, w, du, dw, dqd, dkt, dattn, dgl):
    s = qn.shape[0]
    nc = s // CHUNK

    def body(q_ref, k_ref, v_ref, bg_ref, bgt_ref, tri_ref, sel_ref, t_ref, attn_ref, u_ref, w_ref,
             du_ref, dw_ref, dqd_ref, dkt_ref, dattn_ref, dgl_ref,
             dq_ref, dk_ref, dv_ref, dbg_ref):
        tri_v = tri_ref[...]
        lane = lax.broadcasted_iota(jnp.int32, (CHUNK, 128), 1)
        rowi = lax.broadcasted_iota(jnp.int32, (CHUNK, 128), 0)
        ones_b = jnp.ones((CHUNK, 128), BF16)
        ones_sq = jnp.ones((128, 128), BF16)
        parts = [[] for _ in range(CPS)]

        def chain(cc, h, bg3, gc3, gc_row):
            rows = slice(CHUNK * cc, CHUNK * cc + CHUNK)
            lanes = slice(128 * h, 128 * h + 128)
            gcc, beta, incl, strict, decay, gl = _chunk_common(bg3, gc3, gc_row, h, sel_ref)
            yield
            q = q_ref[rows, lanes]
            k = k_ref[rows, lanes]
            vv = v_ref[rows, lanes]
            ts = _split(t_ref[cc, h])
            egc = jnp.exp(gcc)
            kb = k * beta
            a_mat = jnp.where(strict, _mm_nt(kb, k) * decay, 0.0)
            dvb = _mm3_tn(ts, _split(du_ref[rows, lanes]))
            dkbg = _mm3_tn(ts, _split(dw_ref[rows, lanes]))
            yield
            d_a = jnp.where(strict, -(_mm_nt(dvb, u_ref[rows, lanes]) + _mm_nt(dkbg, w_ref[rows, lanes])), 0.0)
            d_m = d_a * decay
            dattn_m = jnp.where(incl, dattn_ref[cc, h], 0.0)
            dqk = dattn_m * decay
            e_hi, e_lo = _split(d_a * a_mat + dattn_m * attn_ref[cc, h])
            yield
            dkb = _mm(d_m, k)
            dk = _mm_tn(d_m, kb) + _mm_tn(dqk, q)
            dq = _mm(dqk, k)
            e_colsum = _dot_tn(e_hi, ones_b) + _dot_tn(e_lo, ones_b)
            e_rowsum = _dot(e_hi, ones_b) + _dot(e_lo, ones_b)
            dqd = dqd_ref[rows, lanes]
            dkt = dkt_ref[rows, lanes]
            sums = _rowsum_b(jnp.concatenate([dqd * q, dkt * k, dkbg * k, dkb * k, dvb * vv], axis=0), ones_sq)
            s_dqd, s_dkt, rk, s_dkb, s_dvb = (sums[CHUNK * n:CHUNK * n + CHUNK] for n in range(5))
            yield
            tail = jnp.exp(gl - gcc)
            r = s_dkt * tail
            dgl_tot = jnp.sum(r, axis=0, keepdims=True) + dgl_ref[cc, h] * jnp.exp(gl)
            dgc = e_rowsum - e_colsum + s_dqd * egc - r + rk * beta * egc
            dgc = dgc + jnp.where(rowi == CHUNK - 1, dgl_tot, 0.0)
            dq_ref[rows, lanes] = dq + dqd * egc
            dk_ref[rows, lanes] = dk + dkt * tail + dkbg * (beta * egc) + dkb * beta
            dv_ref[rows, lanes] = dvb * beta
            parts[cc].append((h, dgc, rk * egc + s_dkb + s_dvb))

        gens = []
        for cc in range(CPS):
            bgv = bg_ref[CHUNK * cc:CHUNK * cc + CHUNK, :]
            bg3, gc3, gc_row = _chunk_cumsum(bgv, bgt_ref[cc], tri_v)
            gens += [chain(cc, h, bg3, gc3, gc_row) for h in range(DN_HEADS)]
        _interleave(gens)
        for cc in range(CPS):
            dgc_mat = jnp.zeros((CHUNK, 128), F32)
            dbeta_mat = jnp.zeros((CHUNK, 128), F32)
            for h, dgc, dbeta in parts[cc]:
                dgc_mat = dgc_mat + jnp.where(lane == DN_HEADS + h, dgc, 0.0)
                dbeta_mat = dbeta_mat + jnp.where(lane == h, dbeta, 0.0)
            dbg_ref[CHUNK * cc:CHUNK * cc + CHUNK, :] = _mmx_tn(tri_v, dgc_mat) + dbeta_mat

    big = pl.BlockSpec((CPS * CHUNK, 512), lambda n: (n, 0))
    sq = pl.BlockSpec((CPS, DN_HEADS, CHUNK, CHUNK), lambda n: (n, 0, 0, 0))
    glb = pl.BlockSpec((CPS, DN_HEADS, 1, 128), lambda n: (n, 0, 0, 0))
    small = pl.BlockSpec((CPS * CHUNK, 128), lambda n: (n, 0))
    return pl.pallas_call(
        body, grid=(nc // CPS,),
        in_specs=[big, big, big, small, pl.BlockSpec((CPS, 8, CHUNK), lambda n: (n, 0, 0)),
                  pl.BlockSpec((CHUNK, CHUNK), lambda n: (0, 0)),
                  pl.BlockSpec((8, 128, 128), lambda n: (0, 0, 0)), sq, sq, big, big,
                  big, big, big, big, sq, glb],
        out_specs=[big, big, big, small],
        out_shape=[_sds((s, 512))] * 3 + [_sds((s, 128))],
        compiler_params=_cp(("parallel",)), name="dn_prep_bwd")(qn, kn, v, bg, bgt, tri, sel, t_inv, attn, u, w,
                                                                  du, dw, dqd, dkt, dattn, dgl)


def _dn_pre_bwd(dqn, dkn, dv, dbg, proj, conv_w8, alog_l, dtb_l):
    s = proj.shape[0]
    tr = 512
    nh = tr // 8

    def body(dq_ref, dk_ref, dv_ref, dbg_ref, u_ref, halo_ref, ba_ref, w_ref, al_ref, dt_ref,
             dy_ref, dba_ref, dsm_ref):
        i = pl.program_id(0)

        @pl.when(i == 0)
        def _():
            dsm_ref[...] = jnp.zeros_like(dsm_ref)

        keep = (i > 0).astype(F32)
        for c in range(12):
            lanes = slice(128 * c, 128 * c + 128)
            y = _conv_group(u_ref[:, lanes], halo_ref[:, lanes] * keep, w_ref, c)
            sg = _sigmoid(y)
            sv = y * sg
            if c < 8:
                rs = lax.rsqrt(jnp.sum(sv * sv, axis=1, keepdims=True) + EPS)
                n = sv * rs
                if c < 4:
                    dn = dq_ref[:, lanes] * (DN_HD ** -0.5)
                else:
                    dn = dk_ref[:, slice(128 * (c - 4), 128 * (c - 3))]
                dsv = rs * (dn - n * jnp.sum(dn * n, axis=1, keepdims=True))
            else:
                dsv = dv_ref[:, slice(128 * (c - 8), 128 * (c - 7))]
            dy_ref[:, lanes] = dsv * _silu_grad(y, sg)
        _, lane, sig_b, t, nega, g = _beta_g(ba_ref[...], al_ref[...], dt_ref[...])
        dbg = dbg_ref[...]
        da = dbg * nega * _sigmoid(t)
        is_b = lane < DN_HEADS
        is_a = jnp.logical_and(lane >= DN_HEADS, lane < 2 * DN_HEADS)
        dba_ref[...] = jnp.where(is_b, dbg * sig_b * (1.0 - sig_b), jnp.where(is_a, da, 0.0)).astype(BF16)
        d_alog = jnp.sum(jnp.where(is_a, dbg * g, 0.0), axis=0, keepdims=True)
        d_dtb = jnp.sum(jnp.where(is_a, da, 0.0), axis=0, keepdims=True)
        row = lax.broadcasted_iota(jnp.int32, (8, 128), 0)
        dsm_ref[...] = dsm_ref[...] + jnp.where(row == 0, d_alog, jnp.where(row == 1, d_dtb, 0.0))

    blk = pl.BlockSpec((tr, 512), lambda i: (i, 0))
    return pl.pallas_call(
        body, grid=(s // tr,),
        in_specs=[blk, blk, blk, pl.BlockSpec((tr, 128), lambda i: (i, 0)),
                  pl.BlockSpec((tr, 1536), lambda i: (i, 0)),
                  pl.BlockSpec((8, 1536), lambda i: (jnp.maximum(i * nh - 1, 0), 0)),
                  pl.BlockSpec((tr, 128), lambda i: (i, COL_BA_128)),
                  pl.BlockSpec((8, 1536), lambda i: (0, 0)),
                  pl.BlockSpec((1, 128), lambda i: (0, 0)), pl.BlockSpec((1, 128), lambda i: (0, 0))],
        out_specs=[pl.BlockSpec((tr, 1536), lambda i: (i, 0)), pl.BlockSpec((tr, 128), lambda i: (i, 0)),
                   pl.BlockSpec((8, 128), lambda i: (0, 0))],
        out_shape=[_sds((s, 1536)), _sds((s, 128), BF16), _sds((8, 128))],
        compiler_params=_cp(("arbitrary",)), name="dn_pre_bwd")(dqn, dkn, dv, dbg, proj, proj, proj, conv_w8,
                                                                 alog_l, dtb_l)


def _conv_bwd(dy, proj, conv_w8):
    s = dy.shape[0]
    tr = 512
    nh = tr // 8
    nblk = s // tr

    def body(dy_ref, dyn_ref, u_ref, halo_ref, w_ref, du_ref, dw_ref):
        i = pl.program_id(0)

        @pl.when(i == 0)
        def _():
            dw_ref[...] = jnp.zeros_like(dw_ref)

        keep_prev = (i > 0).astype(F32)
        keep_next = (i < nblk - 1).astype(F32)
        row = lax.broadcasted_iota(jnp.int32, (8, 128), 0)
        for c in range(12):
            lanes = slice(128 * c, 128 * c + 128)
            dyc = dy_ref[:, lanes]
            dcat = jnp.concatenate([dyc, dyn_ref[:, lanes] * keep_next], axis=0)
            xcat = jnp.concatenate([halo_ref[:, lanes] * keep_prev, u_ref[:, lanes]], axis=0)
            du = dyc * w_ref[CONV_W - 1:CONV_W, lanes]
            dwc = jnp.where(row == CONV_W - 1, jnp.sum(dyc * u_ref[:, lanes], axis=0, keepdims=True), 0.0)
            for k in range(1, CONV_W):
                du = du + pltpu.roll(dcat, tr + 8 - k, 0)[0:tr] * w_ref[CONV_W - 1 - k:CONV_W - k, lanes]
                ush = pltpu.roll(xcat, k, 0)[8:8 + tr]
                dwc = dwc + jnp.where(row == CONV_W - 1 - k, jnp.sum(dyc * ush, axis=0, keepdims=True), 0.0)
            du_ref[:, lanes] = du.astype(BF16)
            dw_ref[:, lanes] = dw_ref[:, lanes] + dwc

    return pl.pallas_call(
        body, grid=(nblk,),
        in_specs=[pl.BlockSpec((tr, 1536), lambda i: (i, 0)),
                  pl.BlockSpec((8, 1536), lambda i: (jnp.minimum((i + 1) * nh, s // 8 - 1), 0)),
                  pl.BlockSpec((tr, 1536), lambda i: (i, 0)),
                  pl.BlockSpec((8, 1536), lambda i: (jnp.maximum(i * nh - 1, 0), 0)),
                  pl.BlockSpec((8, 1536), lambda i: (0, 0))],
        out_specs=[pl.BlockSpec((tr, 1536), lambda i: (i, 0)), pl.BlockSpec((8, 1536), lambda i: (0, 0))],
        out_shape=[_sds((s, 1536), BF16), _sds((8, 1536))],
        compiler_params=_cp(("arbitrary",)), name="conv_bwd")(dy, dy, proj, proj, conv_w8)


PIECE_WIDTHS = (1536, 512, 512, 512, 512, 512, 128)


def _in_bwd_dx(pieces, w_bf, x, dy, norm_w, ds_accs, buckets, chip_sums=None):
    s = x.shape[0]
    tm = 512
    n_piece = len(PIECE_WIDTHS)
    n_step = s // tm
    n_arr = 0 if chip_sums is None else len(chip_sums)
    n_sent = 3 * n_arr

    def body(*refs):
        refs = list(refs)

        def take(n):
            return [refs.pop(0) for _ in range(n)]

        piece_refs = take(n_piece)
        w_ref, x_ref, dy_ref, nw_ref = take(4)
        ds_refs = take(len(ds_accs))
        bk_ref, = take(1)
        hb_refs = take(n_arr)
        dx_ref, dnw_ref, drb_ref = take(3)
        recv_refs = take(n_arr)
        i = pl.program_id(0)

        def copies():
            send_sems, recv_sems = refs
            xi, yi, ci = _position()
            out = []
            for j, (px, py) in enumerate(_other_chips(xi, yi)):
                for t, (src, dst) in enumerate(zip(hb_refs, recv_refs)):
                    k = n_arr * j + t
                    out.append(pltpu.make_async_remote_copy(
                        src_ref=src.at[2 * px + py], dst_ref=dst.at[j], send_sem=send_sems.at[k],
                        recv_sem=recv_sems.at[k], device_id=(px, py, ci), device_id_type=MESH_ID))
            return out

        @pl.when(i == 0)
        def _():
            dnw_ref[...] = jnp.zeros_like(dnw_ref)
            for cp in (copies() if n_sent else []):
                cp.start()

        @pl.when(i == 1)
        def _():
            _bias_grad(ds_refs, bk_ref, drb_ref)

        dp = jnp.concatenate([r[...] for r in piece_refs], axis=1)
        dh = _dot(dp, w_ref[...])
        xv = x_ref[...]
        rstd = lax.rsqrt(jnp.mean(xv * xv, axis=-1, keepdims=True) + EPS)
        xh = xv * rstd
        dnw_ref[...] = dnw_ref[...] + jnp.sum(dh * xh, axis=0, keepdims=True)
        g = dh * nw_ref[...]
        dx_ref[...] = rstd * (g - xh * jnp.mean(g * xh, axis=-1, keepdims=True)) + dy_ref[...]

        if n_sent:
            @pl.when(i == n_step - 1)
            def _():
                cps = copies()
                for cp in cps:
                    cp.wait_recv()
                for cp in cps:
                    cp.wait_send()

    def blk(n):
        return pl.BlockSpec((tm, n), lambda i: (i, 0))

    vm = pl.BlockSpec(memory_space=pltpu.VMEM)
    in_specs = [blk(n) for n in PIECE_WIDTHS] + [pl.BlockSpec((D_IN_PAD, D_MODEL), lambda i: (0, 0)), blk(D_MODEL),
                                                blk(D_MODEL), pl.BlockSpec((1, D_MODEL), lambda i: (0, 0))]
    in_specs += [vm] * (len(ds_accs) + 1)
    out_specs = [blk(D_MODEL), pl.BlockSpec((8, D_MODEL), lambda i: (0, 0)), pl.BlockSpec(memory_space=pltpu.SMEM)]
    out_shape = [_sds((s, D_MODEL)), _sds((8, D_MODEL)), _sds((ATT_HEADS, N_BUCKETS))]
    scratch = []
    extra = ()
    if n_sent:
        extra = tuple(chip_sums)
        in_specs += [ANY] * n_arr
        out_specs += [ANY] * n_arr
        out_shape += [_sds((3,) + a.shape[1:], a.dtype) for a in extra]
        scratch = [pltpu.SemaphoreType.DMA((n_sent,)), pltpu.SemaphoreType.DMA((n_sent,))]
    return pl.pallas_call(
        body, grid=(n_step,), in_specs=in_specs, out_specs=out_specs, out_shape=out_shape, scratch_shapes=scratch,
        compiler_params=_cp(("arbitrary",)), name="in_bwd_dx")(*pieces, w_bf, x, dy, norm_w, *ds_accs, buckets, *extra)


def _in_bwd_dw(pieces, x, norm_w):
    s = x.shape[0]
    tm = 512
    n_piece = len(PIECE_WIDTHS)

    def body(*refs):
        piece_refs = refs[:n_piece]
        x_ref, nw_ref, dw_ref = refs[n_piece:]
        i = pl.program_id(0)

        @pl.when(i == 0)
        def _():
            dw_ref[...] = jnp.zeros_like(dw_ref)

        xv = x_ref[...]
        rstd = lax.rsqrt(jnp.mean(xv * xv, axis=-1, keepdims=True) + EPS)
        h = (xv * rstd * nw_ref[...]).astype(BF16)
        at = 0
        for r, width in zip(piece_refs, PIECE_WIDTHS):
            dw_ref[at:at + width, :] = dw_ref[at:at + width, :] + _dot_tn(r[...], h)
            at += width

    return pl.pallas_call(
        body, grid=(s // tm,),
        in_specs=[pl.BlockSpec((tm, n), lambda i: (i, 0)) for n in PIECE_WIDTHS]
        + [pl.BlockSpec((tm, D_MODEL), lambda i: (i, 0)), pl.BlockSpec((1, D_MODEL), lambda i: (0, 0))],
        out_specs=pl.BlockSpec((D_IN_PAD, D_MODEL), lambda i: (0, 0)),
        out_shape=_sds((D_IN_PAD, D_MODEL)),
        compiler_params=_cp(("arbitrary",)), name="in_bwd_dw")(*pieces, x, norm_w)


def _flat(a):
    return a.reshape(-1, a.shape[-1])


def _as_pattern(a, r):
    return a if r == 1 else a.reshape(r, a.shape[0] // r, a.shape[1])


D_SHARD = D_IN // N_CHIPS
BA_START = 4 * D_DN
BA_PACKED = 4096
S1_HEAD = BA_START - D_SHARD
S1_BA = 2 * D_SHARD - BA_START


def _pack_w(g):
    n = g.shape[2]

    def body(g_ref, o_ref):
        g32 = g_ref.bitcast(jnp.uint32)
        o32 = o_ref.bitcast(jnp.uint32)
        full, head, ba1 = D_SHARD // 2, S1_HEAD // 2, S1_BA // 2
        ba2 = DN_HEADS - ba1
        pieces = [(0, 0, full), (1, 0, head), (2, ba2, full), (3, 0, full), (1, head, full), (2, 0, ba2)]
        at = 0
        for chip, lo, hi in pieces:
            o32[at:at + hi - lo, :] = g32[chip, lo:hi, :]
            at += hi - lo
        o32[at:D_IN_PAD // 2, :] = jnp.zeros((D_IN_PAD // 2 - at, n), jnp.uint32)

    vm = pl.BlockSpec(memory_space=pltpu.VMEM)
    return pl.pallas_call(body, in_specs=[vm], out_specs=vm, out_shape=_sds((D_IN_PAD, n), g.dtype),
                          compiler_params=_cp(), name="pack_w")(g)


def _unpack_w(p, rows):
    n = p.shape[1]
    tn = 256
    mid = BA_PACKED + S1_BA
    pieces = [(0, 0, (0, D_SHARD)), (1, 0, (D_SHARD, BA_START)), (1, S1_HEAD, (BA_PACKED, mid)),
              (2, 0, (mid, BA_PACKED + 2 * DN_HEADS)), (2, 2 * DN_HEADS - S1_BA, (BA_START, BA_START + S1_HEAD)),
              (3, 0, (BA_START + S1_HEAD, BA_PACKED))]

    def body(p_ref, o_ref):
        for chip, at, (lo, hi) in pieces:
            o_ref[chip, at:at + hi - lo, :] = p_ref[lo:hi, :]
        for chip in range(N_CHIPS):
            o_ref[chip, D_SHARD:rows, :] = jnp.zeros((rows - D_SHARD, tn), p.dtype)

    return pl.pallas_call(
        body, grid=(n // tn,),
        in_specs=[pl.BlockSpec((D_IN_PAD, tn), lambda j: (0, j))],
        out_specs=pl.BlockSpec((N_CHIPS, rows, tn), lambda j: (0, 0, j)),
        out_shape=_sds((N_CHIPS, rows, n), p.dtype),
        compiler_params=_cp(("parallel",)), name="unpack_w")(p)


def _lane_row(vec, offset):
    return jnp.pad(vec.reshape(1, -1), ((0, 0), (offset, 128 - offset - vec.shape[0])))


def _local_step(x, tgt, norm_w, w_bf, conv_w, a_log, dt_bias, dn_norm_w, q_norm_w, k_norm_w, bias, w_out_bf):
    s = x.shape[0]
    nc = s // CHUNK
    conv_w8 = jnp.pad(conv_w, ((0, 8 - CONV_W), (0, 0)))
    alog_l = _lane_row(a_log.reshape(-1), DN_HEADS)
    dtb_l = _lane_row(dt_bias.reshape(-1), DN_HEADS)
    dnw_t = jnp.tile(dn_norm_w.reshape(1, DN_HD), (1, DN_HEADS))
    qw_t = jnp.tile(q_norm_w.reshape(1, ATT_HD), (1, ATT_HEADS))
    kw_t = jnp.tile(k_norm_w.reshape(1, ATT_HD), (1, ATT_HEADS))
    bd128 = _compact_mat(DN_HD)
    bd64 = _compact_mat(ATT_HD)
    tri = _tri_incl()
    sel = _lane_select()
    buckets = _bucket_tables()

    proj, qn, kn, v_dn, bg = _in_proj(x, norm_w, w_bf, conv_w8, alog_l, dtb_l)
    bgt = bg[:, 0:8].reshape(nc, CHUNK, 8).transpose(0, 2, 1)
    u, w, qd, kt, attn, t_inv, gl = _dn_prep(qn, kn, v_dn, bg, bgt, tri, sel)
    o_dn, vn, sp = _dn_scan(u, w, qd, kt, attn, gl)
    q1, k1, v1, q4, k4, v4, q16, k16, v16 = _att_pre(proj, qw_t, kw_t, bd64)
    rs = [r for _, r in PATTERNS]
    qkv = [(q1, k1, v1), (_flat(q4), _flat(k4), _flat(v4)), (_flat(q16), _flat(k16), _flat(v16))]
    o_pats, lse_pats = [], []
    for p, r in enumerate(rs):
        o_p, lse_p = _att_fwd(*qkv[p], bias, p, r, "att_fwd_r%d" % r)
        o_pats.append(_as_pattern(o_p, r))
        lse_pats.append(_as_pattern(lse_p, r))
    mixed, o_att, l1, l4, l16 = _post_fwd(o_dn, proj, o_pats, lse_pats, dnw_t, bd128)

    (dy, loss_blk, d_w_out, do_dn, dz, dgate, do1, do4, do16, dl1, dl4, dl16,
     d_dnw) = _out_post(x, mixed, w_out_bf, tgt, o_dn, proj, o_att, dnw_t, bd128)
    side = [(do1, l1, dl1), (_flat(do4), _flat(l4), _flat(dl4)), (_flat(do16), _flat(l16), _flat(dl16))]
    dq_pats, dk_pats, dv_pats, ds_accs = [], [], [], []
    for p, r in enumerate(rs):
        dq_p, dk_p, dv_p, ds_p = _att_bwd(*qkv[p], *side[p], bias, p, r, "att_bwd_r%d" % r)
        dq_pats.append(_as_pattern(dq_p, r))
        dk_pats.append(_as_pattern(dk_p, r))
        dv_pats.append(_as_pattern(dv_p, r))
        ds_accs.append(ds_p)
    dq_att, dk_att, dv_att, d_qw, d_kw = _att_pre_bwd(dq_pats, dk_pats, dv_pats, proj, qw_t, kw_t, bd64)
    du, dqd, dkt, dw, dattn, dgl = _dn_scan_bwd(do_dn, sp, qd, kt, w, vn, attn, gl)
    dqn, dkn, dv_dn, dbg = _dn_prep_bwd(qn, kn, v_dn, bg, bgt, tri, sel, t_inv, attn, u, w, du, dw, dqd, dkt, dattn,
                                        dgl)
    dyc, dba, dsm = _dn_pre_bwd(dqn, dkn, dv_dn, dbg, proj, conv_w8, alog_l, dtb_l)
    d_qkv_dn, d_conv8 = _conv_bwd(dyc, proj, conv_w8)
    pieces = (d_qkv_dn, dz, dq_att, dk_att, dv_att, dgate, dba)
    d_w_in_t = _in_bwd_dw(pieces, x, norm_w)
    last = functools.partial(_in_bwd_dx, pieces, w_bf, x, dy, norm_w, ds_accs, buckets)

    grads = dict(
        w_in_t=d_w_in_t,
        conv_w=d_conv8[0:CONV_W, :],
        a_log=dsm[0:1, DN_HEADS:2 * DN_HEADS],
        dt_bias=dsm[1:2, DN_HEADS:2 * DN_HEADS],
        dn_norm_w=d_dnw[0:1, :],
        q_norm_w=d_qw[0:1, :].reshape(ATT_HEADS, ATT_HD),
        k_norm_w=d_kw[0:1, :].reshape(ATT_HEADS, ATT_HD),
        w_out=d_w_out,
    )
    return loss_blk[0, 0], last, grads


MESH_ID = pl.DeviceIdType.MESH
ANY = pl.BlockSpec(memory_space=pl.ANY)


def _position():
    return lax.axis_index("x"), lax.axis_index("y"), lax.axis_index("c")


def _other_chips(x, y):
    return [(1 - x, y), (x, 1 - y), (1 - x, 1 - y)]


SHARD_PAD = 1040
WIN = 528


def _row_split(n):
    return (n // 2) // 128 * 128


def _part(ref, cc):
    n = ref.shape[0]
    sp = _row_split(n)
    return ref.at[pl.ds(0, sp)] if cc == 0 else ref.at[pl.ds(sp, n - sp)]


def _halves(ref):
    n = ref.shape[0]
    sp = (n // 2) // 16 * 16
    return ref.at[pl.ds(0, sp)], ref.at[pl.ds(sp, n - sp)]


def _gather_weights(wt_s, w_out_s, conv_s, rel_bias, buckets):
    def body(a_ref, b_ref, c_ref, rb_ref, bk_ref, ga_ref, gb_ref, gc_ref, bias_ref, send_sems, recv_sems, loc_sems,
             a_vmem, b_vmem):
        x, y, c = _position()
        me = 2 * x + y
        sib = (x, y, 1 - c)
        big = ((a_ref, ga_ref), (b_ref, gb_ref))
        stage_in = [pltpu.make_async_copy(a_ref, a_vmem, loc_sems.at[0]),
                    pltpu.make_async_copy(b_ref, b_vmem, loc_sems.at[1])]
        local = [pltpu.make_async_copy(a_vmem, ga_ref.at[me], loc_sems.at[0]),
                 pltpu.make_async_copy(b_vmem, gb_ref.at[me], loc_sems.at[1]),
                 pltpu.make_async_copy(c_ref, gc_ref.at[me], loc_sems.at[2])]
        for cp in stage_in:
            cp.start()
        local[2].start()
        for cp in stage_in:
            cp.wait()
        for cp in local[:2]:
            cp.start()
        others = _other_chips(x, y)

        def exchange(cc):
            nbr = others[:2]
            dg = others[2]
            pending = []

            def copy(k, src, dst, to):
                cp = pltpu.make_async_remote_copy(src_ref=src, dst_ref=dst, send_sem=send_sems.at[k],
                                                  recv_sem=recv_sems.at[k], device_id=to, device_id_type=MESH_ID)
                pending.append(cp)
                cp.start()

            def landed(k, dst):
                pltpu.make_async_remote_copy(src_ref=dst, dst_ref=dst, send_sem=send_sems.at[k],
                                             recv_sem=recv_sems.at[k], device_id=sib, device_id_type=MESH_ID).wait_recv()

            def slot(dst, chip, c_part):
                return _part(dst.at[2 * chip[0] + chip[1]], c_part)

            for i, n in enumerate(nbr):
                for t, (src, dst) in enumerate(big):
                    copy(2 * i + t, _part(src, cc), _part(dst.at[me], cc), (*n, c))
            for j, chip in enumerate(others):
                copy(4 + j, c_ref, gc_ref.at[me], (*chip, c))
            _bias_table(rb_ref, bk_ref, bias_ref)
            for i, n in enumerate(nbr):
                for t, (_, dst) in enumerate(big):
                    got = slot(dst, n, cc)
                    landed(2 * i + t, got)
                    half = _halves(got)[1 - i]
                    copy(7 + 2 * (1 - i) + t, half, half, (*nbr[1 - i], c))
                    copy(11 + 2 * i + t, got, got, sib)
            for i, n in enumerate(nbr):
                for t, (_, dst) in enumerate(big):
                    half = _halves(slot(dst, dg, cc))[i]
                    landed(7 + 2 * i + t, half)
                    copy(15 + 2 * i + t, half, half, sib)
            for j, chip in enumerate(others):
                landed(4 + j, gc_ref.at[2 * chip[0] + chip[1]])
            for i, n in enumerate(nbr):
                for t, (_, dst) in enumerate(big):
                    landed(11 + 2 * i + t, slot(dst, n, 1 - cc))
                    landed(15 + 2 * i + t, _halves(slot(dst, dg, 1 - cc))[i])
            for cp in pending:
                cp.wait_send()

        for cc in (0, 1):
            pl.when(c == cc)(functools.partial(exchange, cc))
        for cp in local:
            cp.wait()

    srcs = (wt_s, w_out_s, conv_s)
    n_sem = 19
    return pl.pallas_call(
        body, in_specs=[ANY] * 3 + [pl.BlockSpec(memory_space=pltpu.SMEM), pl.BlockSpec(memory_space=pltpu.VMEM)],
        out_specs=[ANY] * 3 + [pl.BlockSpec(memory_space=pltpu.VMEM)],
        out_shape=[_sds((N_CHIPS,) + a.shape, a.dtype) for a in srcs]
        + [_sds((len(PATTERNS), ATT_HEADS, BLK, 2 * BLK))],
        scratch_shapes=[pltpu.SemaphoreType.DMA((n_sem,)), pltpu.SemaphoreType.DMA((n_sem,)),
                        pltpu.SemaphoreType.DMA((3,)), pltpu.VMEM(wt_s.shape, wt_s.dtype),
                        pltpu.VMEM(w_out_s.shape, w_out_s.dtype)],
        compiler_params=_cp(), name="gather_weights")(*srcs, rel_bias, buckets)


N_DEV = 8
PEER_FLIPS = [(dx, dy, dc) for dx in (0, 1) for dy in (0, 1) for dc in (0, 1)][1:]


def _small_copies(s_ref, rs_ref, send_sems, recv_sems):
    x, y, c = _position()
    dev = 4 * x + 2 * y + c
    sends, recvs = [], []
    for f, (dx, dy, dc) in enumerate(PEER_FLIPS):
        peer = (x ^ dx, y ^ dy, c ^ dc)
        sends.append(pltpu.make_async_remote_copy(
            src_ref=s_ref, dst_ref=rs_ref.at[dev], send_sem=send_sems.at[f], recv_sem=recv_sems.at[f],
            device_id=peer, device_id_type=MESH_ID))
        recvs.append(pltpu.make_async_remote_copy(
            src_ref=s_ref, dst_ref=rs_ref.at[4 * peer[0] + 2 * peer[1] + peer[2]], send_sem=send_sems.at[f],
            recv_sem=recv_sems.at[f], device_id=peer, device_id_type=MESH_ID))
    return sends, recvs


def _fill_parts(h_in, r_in, h_out, r_out, small, rows_in, rows_out):
    def body(ha_ref, ra_ref, hb_ref, rb_ref, s_ref, fa_ref, fb_ref, rs_ref, send_sems, recv_sems, loc_sems,
             small_send, small_recv, a_vmem, b_vmem, ra_vmem, rb_vmem):
        x, y, c = _position()
        sib = (x, y, 1 - c)
        chip = 2 * x + y
        pairs = ((a_vmem, fa_ref), (b_vmem, fb_ref))
        own_small = pltpu.make_async_copy(s_ref, rs_ref.at[4 * x + 2 * y + c], loc_sems.at[2])
        own_small.start()
        small_sends, small_recvs = _small_copies(s_ref, rs_ref, small_send, small_recv)
        for cp in small_sends:
            cp.start()
        stage_in = [pltpu.make_async_copy(ha_ref.at[chip], a_vmem, loc_sems.at[0]),
                    pltpu.make_async_copy(hb_ref.at[chip], b_vmem, loc_sems.at[1]),
                    pltpu.make_async_copy(ra_ref, ra_vmem, loc_sems.at[3]),
                    pltpu.make_async_copy(rb_ref, rb_vmem, loc_sems.at[4])]
        for cp in stage_in:
            cp.start()
        for cp in stage_in:
            cp.wait()
        for tot, recv in ((a_vmem, ra_vmem), (b_vmem, rb_vmem)):
            acc = tot[...]
            for j in range(3):
                acc = acc + recv[j].astype(F32)
            tot[...] = acc

        def fill(cc):
            mine = [_part(dst, cc) for _, dst in pairs]
            srcs = [src.at[pl.ds(0, m.shape[0])] for src, m in zip((a_vmem, b_vmem), mine)]
            local = [pltpu.make_async_copy(s, m, loc_sems.at[t]) for t, (s, m) in enumerate(zip(srcs, mine))]
            sends = [pltpu.make_async_remote_copy(src_ref=s, dst_ref=m, send_sem=send_sems.at[t],
                                                  recv_sem=recv_sems.at[t], device_id=sib, device_id_type=MESH_ID)
                     for t, (s, m) in enumerate(zip(srcs, mine))]
            for cp in local + sends:
                cp.start()
            for t, (_, dst) in enumerate(pairs):
                theirs = _part(dst, 1 - cc)
                pltpu.make_async_remote_copy(src_ref=theirs, dst_ref=theirs, send_sem=send_sems.at[t],
                                             recv_sem=recv_sems.at[t], device_id=sib, device_id_type=MESH_ID).wait_recv()
            for cp in sends:
                cp.wait_send()
            for cp in local:
                cp.wait()

        for cc in (0, 1):
            pl.when(c == cc)(functools.partial(fill, cc))
        for cp in small_recvs:
            cp.wait_recv()
        for cp in small_sends:
            cp.wait_send()
        own_small.wait()

    n_peer = len(PEER_FLIPS)
    return pl.pallas_call(
        body, in_specs=[ANY] * 5, out_specs=[ANY] * 3,
        out_shape=[_sds((rows_in, D_MODEL)), _sds((rows_out, D_MODEL)), _sds((N_DEV,) + small.shape, small.dtype)],
        scratch_shapes=[pltpu.SemaphoreType.DMA((2,)), pltpu.SemaphoreType.DMA((2,)), pltpu.SemaphoreType.DMA((5,)),
                        pltpu.SemaphoreType.DMA((n_peer,)), pltpu.SemaphoreType.DMA((n_peer,)),
                        pltpu.VMEM(h_in.shape[1:], F32), pltpu.VMEM(h_out.shape[1:], F32),
                        pltpu.VMEM(r_in.shape, r_in.dtype), pltpu.VMEM(r_out.shape, r_out.dtype)],
        compiler_params=_cp(), name="fill_parts")(h_in, r_in, h_out, r_out, small)


def _chip_sums(u_in, u_out, win_in, win_out):
    wins = (win_in, win_out)

    def body(a_ref, b_ref, ha_ref, hba_ref, hb_ref, hbb_ref, send_sems, recv_sems, loc_sems, out_sems,
             mine_a, theirs_a, sum_a, sumb_a, mine_b, theirs_b, sum_b, sumb_b):
        x, y, c = _position()
        sib = (x, y, 1 - c)
        groups = ((a_ref, mine_a, theirs_a, sum_a, sumb_a, ha_ref, hba_ref, win_in),
                  (b_ref, mine_b, theirs_b, sum_b, sumb_b, hb_ref, hbb_ref, win_out))
        loads, sends = [], []
        for t, (src, mine, theirs, _, _, _, _, win) in enumerate(groups):
            split = _row_split(src.shape[1])
            for k in range(N_CHIPS):
                n = N_CHIPS * t + k
                loads.append(pltpu.make_async_copy(
                    src.at[k, pl.ds(pl.multiple_of(c * split, split), win), :], mine.at[k], loc_sems.at[n]))
                sends.append(pltpu.make_async_remote_copy(
                    src_ref=src.at[k, pl.ds(pl.multiple_of((1 - c) * split, split), win), :], dst_ref=theirs.at[k],
                    send_sem=send_sems.at[n], recv_sem=recv_sems.at[n], device_id=sib, device_id_type=MESH_ID))
        for cp in sends + loads:
            cp.start()
        stores = []
        for t, (_, mine, theirs, tot, totb, h_out, hb_out, _) in enumerate(groups):
            for k in range(N_CHIPS):
                n = N_CHIPS * t + k
                loads[n].wait()
                sends[n].wait_recv()
                val = mine[k] + theirs[k]
                tot[k] = val
                totb[k] = val.astype(BF16)
                stores += [pltpu.make_async_copy(tot.at[k], h_out.at[k], out_sems.at[2 * n]),
                           pltpu.make_async_copy(totb.at[k], hb_out.at[k], out_sems.at[2 * n + 1])]
                stores[-2].start()
                stores[-1].start()
        for cp in sends:
            cp.wait_send()
        for cp in stores:
            cp.wait()

    shapes = [(N_CHIPS, w, D_MODEL) for w in wins]
    n_cp = 2 * N_CHIPS
    vmem = []
    for shp in shapes:
        vmem += [pltpu.VMEM(shp, F32), pltpu.VMEM(shp, F32), pltpu.VMEM(shp, F32), pltpu.VMEM(shp, BF16)]
    return pl.pallas_call(
        body, in_specs=[ANY] * 2, out_specs=[ANY] * 4,
        out_shape=[_sds(shapes[0]), _sds(shapes[0], BF16), _sds(shapes[1]), _sds(shapes[1], BF16)],
        scratch_shapes=[pltpu.SemaphoreType.DMA((n_cp,)), pltpu.SemaphoreType.DMA((n_cp,)),
                        pltpu.SemaphoreType.DMA((n_cp,)), pltpu.SemaphoreType.DMA((2 * n_cp,))] + vmem,
        compiler_params=_cp(), name="chip_sums")(u_in, u_out)


SMALL_LAYOUT = (("norm_w", 1024), ("conv_w", 6144), ("a_log", 128), ("dt_bias", 128), ("dn_norm_w", 128),
                ("q_norm_w", 512), ("k_norm_w", 512), ("rel_bias", 256), ("loss", 128))
SMALL_TOTAL = sum(n for _, n in SMALL_LAYOUT)


def _small_offset(name):
    off = 0
    for n, size in SMALL_LAYOUT:
        if n == name:
            return off
        off += size
    raise KeyError(name)


def _pack_small(grads):
    parts = []
    for name, size in SMALL_LAYOUT:
        flat = grads[name].reshape(1, -1)
        parts.append(jnp.pad(flat, ((0, 0), (0, size - flat.shape[1]))))
    return jnp.concatenate(parts, axis=1)


def _sum_small(rows):
    n_dev = rows.shape[0]
    q_off = _small_offset("q_norm_w")
    k_off = _small_offset("k_norm_w")

    def body(r_ref, tot_ref, qk_ref):
        tot = r_ref[0:1, :]
        for d in range(1, n_dev):
            tot = tot + r_ref[d:d + 1, :]
        tot_ref[...] = tot
        for row, off in ((0, q_off), (1, k_off)):
            s4 = tot[:, off:off + 128] + tot[:, off + 128:off + 256] + tot[:, off + 256:off + 384] \
                + tot[:, off + 384:off + 512]
            qk_ref[row:row + 1, :] = s4 + pltpu.roll(s4, ATT_HD, 1)

    return pl.pallas_call(
        body, in_specs=[pl.BlockSpec(memory_space=pltpu.VMEM)],
        out_specs=[pl.BlockSpec(memory_space=pltpu.VMEM)] * 2,
        out_shape=[_sds((1, SMALL_TOTAL)), _sds((2, 128))],
        compiler_params=_cp(), name="sum_small")(rows)


def _adamw_math(w, g, m, v):
    m = ADAM_B1 * m + (1.0 - ADAM_B1) * g
    v = ADAM_B2 * v + (1.0 - ADAM_B2) * (g * g)
    m_hat = m / (1.0 - ADAM_B1 ** ADAM_STEP)
    v_hat = v / (1.0 - ADAM_B2 ** ADAM_STEP)
    delta = -ADAM_LR * (m_hat / (jnp.sqrt(v_hat) + ADAM_EPS) + ADAM_WD * w)
    return delta, m, v


def _adamw_big(g, w, m, v, name):
    rows, cols = w.shape
    tr = 128

    def body(g_ref, w_ref, m_ref, v_ref, go_ref, d_ref, nm_ref, nv_ref):
        g = g_ref[...]
        go_ref[...] = g
        d_ref[...], nm_ref[...], nv_ref[...] = _adamw_math(w_ref[...], g, m_ref[...], v_ref[...])

    blk = pl.BlockSpec((tr, cols), lambda i: (i, 0))
    return pl.pallas_call(
        body, grid=(pl.cdiv(rows, tr),), in_specs=[blk] * 4, out_specs=[blk] * 4,
        out_shape=[_sds((rows, cols))] * 4, compiler_params=_cp(("parallel",)), name=name)(g, w, m, v)


def _adamw_rows(g, w, m, v, name):
    rows, cols = w.shape
    tr = 128

    def body(g_ref, w_ref, m_ref, v_ref, go_ref, d_ref, nm_ref, nv_ref, s_g, s_d, s_m, s_v):
        g = g_ref[...]
        s_g[...] = g
        s_d[...], s_m[...], s_v[...] = _adamw_math(w_ref[...], g, m_ref[...], v_ref[...])
        for i in range(tr):
            for scr, out in ((s_g, go_ref), (s_d, d_ref), (s_m, nm_ref), (s_v, nv_ref)):
                out[i] = scr[i:i + 1, :]

    blk = pl.BlockSpec((tr, cols), lambda i: (i, 0))
    oblk = pl.BlockSpec((tr, 1, cols), lambda i: (i, 0, 0))
    return pl.pallas_call(
        body, grid=(pl.cdiv(rows, tr),), in_specs=[blk] * 4, out_specs=[oblk] * 4,
        out_shape=[_sds((rows, 1, cols))] * 4, scratch_shapes=[pltpu.VMEM((tr, cols), F32)] * 4,
        compiler_params=_cp(("parallel",)), name=name)(g, w, m, v)


def _adamw_small(w, g, m, v, name):
    def body(w_ref, g_ref, m_ref, v_ref, d_ref, nm_ref, nv_ref):
        d_ref[...], nm_ref[...], nv_ref[...] = _adamw_math(w_ref[...], g_ref[...], m_ref[...], v_ref[...])

    vm = pl.BlockSpec(memory_space=pltpu.VMEM)
    return pl.pallas_call(body, in_specs=[vm] * 4, out_specs=[vm] * 3, out_shape=[_sds(w.shape)] * 3,
                          compiler_params=_cp(), name=name)(w, g, m, v)


WEIGHTS = ("norm_w", "w_in", "conv_w", "a_log", "dt_bias", "dn_norm_w", "q_norm_w", "k_norm_w", "rel_bias", "w_out")


def kernel(x, norm_w, w_in, conv_w, a_log, dt_bias, dn_norm_w, q_norm_w, k_norm_w, rel_bias, w_out, loss_target, m_norm_w, m_w_in, m_conv_w, m_a_log, m_dt_bias, m_dn_norm_w, m_q_norm_w, m_k_norm_w, m_rel_bias, m_w_out, v_norm_w, v_w_in, v_conv_w, v_a_log, v_dt_bias, v_dn_norm_w, v_q_norm_w, v_k_norm_w, v_rel_bias, v_w_out):
    xi, yi, _ = _position()
    chip = 2 * xi + yi
    w_loc = dict(norm_w=norm_w, w_in=w_in[0].T, conv_w=conv_w[0], a_log=a_log, dt_bias=dt_bias, dn_norm_w=dn_norm_w,
                 q_norm_w=q_norm_w, k_norm_w=k_norm_w, rel_bias=rel_bias, w_out=w_out[0])
    m_loc = dict(norm_w=m_norm_w, w_in=m_w_in[0].T, conv_w=m_conv_w[0], a_log=m_a_log, dt_bias=m_dt_bias,
                 dn_norm_w=m_dn_norm_w, q_norm_w=m_q_norm_w, k_norm_w=m_k_norm_w, rel_bias=m_rel_bias,
                 w_out=m_w_out[0])
    v_loc = dict(norm_w=v_norm_w, w_in=v_w_in[0].T, conv_w=v_conv_w[0], a_log=v_a_log, dt_bias=v_dt_bias,
                 dn_norm_w=v_dn_norm_w, q_norm_w=v_q_norm_w, k_norm_w=v_k_norm_w, rel_bias=v_rel_bias,
                 w_out=v_w_out[0])

    wt_pad = jnp.pad(w_loc["w_in"].astype(BF16), ((0, SHARD_PAD - D_SHARD), (0, 0)))
    g_in, g_out, g_conv, bias = _gather_weights(wt_pad, w_loc["w_out"].astype(BF16), w_loc["conv_w"], rel_bias,
                                                _bucket_tables())
    wt_full = _pack_w(g_in)
    w_out_full = g_out.reshape(D_MODEL, D_MODEL)
    conv_full = g_conv.transpose(1, 0, 2).reshape(CONV_W, 3 * D_DN)

    loss_local, last_kernel, grads = _local_step(x[0], loss_target[0], norm_w, wt_full, conv_full, a_log, dt_bias,
                                                 dn_norm_w, q_norm_w, k_norm_w, bias, w_out_full)
    grads["loss"] = loss_local

    u_in = _unpack_w(grads["w_in_t"], SHARD_PAD)
    u_out = grads["w_out"].reshape(N_CHIPS, D_MODEL // N_CHIPS, D_MODEL)
    win_out = u_out.shape[1] // 2
    h_in, hb_in, h_out, hb_out = _chip_sums(u_in, u_out, WIN, win_out)
    grad_x, d_nw8, grads["rel_bias"], r_in, r_out = last_kernel(chip_sums=(hb_in, hb_out))
    grads["norm_w"] = d_nw8[0:1, :]
    small = _pack_small(grads)
    full_in, full_out, r_small = _fill_parts(h_in, r_in, h_out, r_out, small, SHARD_PAD, u_out.shape[1])
    tot_small, qk = _sum_small(r_small.reshape(8, SMALL_TOTAL))

    def small_grad(name, n):
        off = _small_offset(name)
        return tot_small[:, off:off + n]

    loss = small_grad("loss", 1).reshape(())
    conv_all = small_grad("conv_w", CONV_W * 3 * D_DN).reshape(CONV_W, 3 * D_DN)
    g_small = dict(
        norm_w=small_grad("norm_w", D_MODEL),
        conv_w=lax.dynamic_slice_in_dim(conv_all, chip * (3 * D_DN // N_CHIPS), 3 * D_DN // N_CHIPS, axis=1),
        a_log=small_grad("a_log", DN_HEADS),
        dt_bias=small_grad("dt_bias", DN_HEADS),
        dn_norm_w=small_grad("dn_norm_w", DN_HD),
        q_norm_w=qk[0:1, 0:ATT_HD],
        k_norm_w=qk[1:2, 0:ATT_HD],
        rel_bias=small_grad("rel_bias", ATT_HEADS * N_BUCKETS).reshape(ATT_HEADS, N_BUCKETS),
    )

    out_g, out_d, out_m, out_v = {}, {}, {}, {}
    out_g["w_in"], out_d["w_in"], out_m["w_in"], out_v["w_in"] = _adamw_rows(
        full_in, w_loc["w_in"], m_loc["w_in"], v_loc["w_in"], "adamw_w_in")
    out_g["w_out"], out_d["w_out"], out_m["w_out"], out_v["w_out"] = _adamw_big(
        full_out, w_loc["w_out"], m_loc["w_out"], v_loc["w_out"], "adamw_w_out")
    for name in g_small:
        out_g[name] = g_small[name]
        out_d[name], out_m[name], out_v[name] = _adamw_small(w_loc[name], g_small[name], m_loc[name], v_loc[name],
                                                             "adamw_" + name)
    for d in (out_g, out_d, out_m, out_v):
        d["w_in"] = d["w_in"].transpose(1, 2, 0)
        for name in ("conv_w", "w_out"):
            d[name] = d[name][None]
    return (loss, grad_x[None], *[out_g[n] for n in WEIGHTS], *[out_d[n] for n in WEIGHTS],
            *[out_m[n] for n in WEIGHTS], *[out_v[n] for n in WEIGHTS])
```

```python
import functools
import math

import numpy as np
import jax
import jax.numpy as jnp
from jax import lax
from jax.experimental import pallas as pl
from jax.experimental.pallas import tpu as pltpu

F32 = jnp.float32
BF16 = jnp.bfloat16
HI = lax.Precision.HIGHEST

D_MODEL = 1024
D_DN = 512
DN_HEADS = 4
DN_HD = 128
CONV_W = 4
CHUNK = 64
D_ATT = 512
ATT_HEADS = 8
ATT_HD = 64
PATTERNS = ((128, 1), (512, 4), (2048, 16))
N_BUCKETS = 32
MAX_DISTANCE = 2048
D_IN = 4 * D_DN + 2 * DN_HEADS + 4 * D_ATT
D_IN_PAD = 4224
EPS = 1e-6
BLK = 128
NEG = -1e30
N_CHIPS = 4

ADAM_LR = 0.001
ADAM_B1 = 0.9
ADAM_B2 = 0.999
ADAM_EPS = 1e-08
ADAM_WD = 0.01
ADAM_STEP = 10

VMEM_LIMIT = 56 * 1024 * 1024

COL_Z = 3
COL_ATT_Q = 4
COL_ATT_K = 5
COL_ATT_V = 6
COL_GATE = 7
COL_BA_128 = 32


def _cp(sem=None):
    if sem is None:
        return pltpu.CompilerParams(vmem_limit_bytes=VMEM_LIMIT)
    return pltpu.CompilerParams(dimension_semantics=sem, vmem_limit_bytes=VMEM_LIMIT)


def _sds(shape, dtype=F32):
    return jax.ShapeDtypeStruct(shape, dtype)


def _mm(a, b):
    return jnp.dot(a.astype(BF16), b.astype(BF16), preferred_element_type=F32)


def _mm_nt(a, b):
    return lax.dot_general(a.astype(BF16), b.astype(BF16), (((1,), (1,)), ((), ())),
                           preferred_element_type=F32)


def _mm_tn(a, b):
    return lax.dot_general(a.astype(BF16), b.astype(BF16), (((0,), (0,)), ((), ())),
                           preferred_element_type=F32)


def _mmx(a, b):
    return jnp.dot(a, b, precision=HI, preferred_element_type=F32)


def _mmx_nt(a, b):
    return lax.dot_general(a, b, (((1,), (1,)), ((), ())), precision=HI, preferred_element_type=F32)


def _mmx_tn(a, b):
    return lax.dot_general(a, b, (((0,), (0,)), ((), ())), precision=HI, preferred_element_type=F32)


def _dot(a, b):
    return jnp.dot(a, b, preferred_element_type=F32)


def _dot_nt(a, b):
    return lax.dot_general(a, b, (((1,), (1,)), ((), ())), preferred_element_type=F32)


def _dot_tn(a, b):
    return lax.dot_general(a, b, (((0,), (0,)), ((), ())), preferred_element_type=F32)


def _split(a):
    hi = a.astype(BF16)
    return hi, (a - hi.astype(F32)).astype(BF16)


def _mm3(a_s, b_s):
    return _dot(a_s[0], b_s[0]) + _dot(a_s[0], b_s[1]) + _dot(a_s[1], b_s[0])


def _mm3_tn(a_s, b_s):
    return _dot_tn(a_s[0], b_s[0]) + _dot_tn(a_s[0], b_s[1]) + _dot_tn(a_s[1], b_s[0])


def _interleave(gens):
    live = list(gens)
    while live:
        nxt = []
        for g in live:
            try:
                next(g)
                nxt.append(g)
            except StopIteration:
                pass
        live = nxt


def _segsum(x, bd):
    hi = x.astype(BF16)
    r1 = x - hi.astype(F32)
    mid = r1.astype(BF16)
    lo = (r1 - mid.astype(F32)).astype(BF16)
    return (jnp.dot(hi, bd, preferred_element_type=F32) + jnp.dot(mid, bd, preferred_element_type=F32)
            + jnp.dot(lo, bd, preferred_element_type=F32))


def _compact_mat(seg, n=512):
    slot = 128 * seg // n
    src = np.arange(n)[:, None]
    dst = np.arange(128)[None, :]
    return jnp.asarray((src // seg == dst // slot).astype(np.float32), dtype=BF16)


def _seg_mean(x, cm, seg):
    hi, lo = _split(x)
    return (_dot(hi, cm) + _dot(lo, cm)) * (1.0 / seg)


def _seg_expand(c, cm, seg):
    hi, lo = _split(c)
    return (_dot_nt(hi, cm) + _dot_nt(lo, cm)) * (cm.shape[0] / (128.0 * seg))


def _seg_rstd(x, cm, seg):
    return _seg_expand(lax.rsqrt(_seg_mean(x * x, cm, seg) + EPS), cm, seg)


def _sigmoid(x):
    return 1.0 / (1.0 + jnp.exp(-x))


def _silu_grad(x, s):
    return s * (1.0 + x * (1.0 - s))


def _tri_incl():
    i = np.arange(CHUNK)
    return jnp.asarray((i[:, None] >= i[None, :]).astype(np.float32))


def _t5_bucket(dist):
    max_exact = N_BUCKETS // 2
    d = np.maximum(dist, 1).astype(np.float64)
    large = max_exact + (np.log(d / max_exact) / math.log(MAX_DISTANCE / max_exact)
                         * (N_BUCKETS - max_exact)).astype(np.int32)
    large = np.minimum(large, N_BUCKETS - 1)
    return np.where(dist < max_exact, dist, large).astype(np.int32)


def _bucket_tables():
    qi = np.arange(BLK)[:, None]
    kj = np.arange(2 * BLK)[None, :]
    step = qi - kj + BLK
    return jnp.asarray(np.stack([_t5_bucket(np.clip(step, 0, None) * r) for _, r in PATTERNS]))


def _in_proj(x, norm_w, wt_bf, conv_w8, alog_l, dtb_l):
    s = x.shape[0]
    tm = 512

    def body(x_ref, nw_ref, w_ref, cw_ref, al_ref, dt_ref, o_ref, q_ref, k_ref, v_ref, bg_ref, halo_ref):
        i = pl.program_id(0)

        @pl.when(i == 0)
        def _():
            halo_ref[...] = jnp.zeros_like(halo_ref)

        xv = x_ref[...]
        rstd = lax.rsqrt(jnp.mean(xv * xv, axis=-1, keepdims=True) + EPS)
        h = (xv * rstd * nw_ref[...]).astype(BF16)
        o_ref[...] = _dot_nt(h, w_ref[...])
        for c in range(12):
            lanes = slice(128 * c, 128 * c + 128)
            y = _conv_group(o_ref[:, lanes], halo_ref[:, lanes], cw_ref, c)
            sv = y * _sigmoid(y)
            if c < 8:
                rs = lax.rsqrt(jnp.sum(sv * sv, axis=1, keepdims=True) + EPS)
                n = sv * rs
                if c < 4:
                    q_ref[:, lanes] = n * (DN_HD ** -0.5)
                else:
                    k_ref[:, slice(128 * (c - 4), 128 * (c - 3))] = n
            else:
                v_ref[:, slice(128 * (c - 8), 128 * (c - 7))] = sv
        bg_ref[...] = _beta_g(o_ref[:, 128 * COL_BA_128:128 * COL_BA_128 + 128], al_ref[...], dt_ref[...])[0]
        halo_ref[...] = o_ref[tm - 8:tm, 0:3 * D_DN]

    rows = pl.BlockSpec((tm, 512), lambda i: (i, 0))
    return pl.pallas_call(
        body, grid=(s // tm,),
        in_specs=[pl.BlockSpec((tm, D_MODEL), lambda i: (i, 0)),
                  pl.BlockSpec((1, D_MODEL), lambda i: (0, 0)),
                  pl.BlockSpec((D_IN_PAD, D_MODEL), lambda i: (0, 0)),
                  pl.BlockSpec((8, 3 * D_DN), lambda i: (0, 0)),
                  pl.BlockSpec((1, 128), lambda i: (0, 0)), pl.BlockSpec((1, 128), lambda i: (0, 0))],
        out_specs=[pl.BlockSpec((tm, D_IN_PAD), lambda i: (i, 0)), rows, rows, rows,
                   pl.BlockSpec((tm, 128), lambda i: (i, 0))],
        out_shape=[_sds((s, D_IN_PAD))] + [_sds((s, 512))] * 3 + [_sds((s, 128))],
        scratch_shapes=[pltpu.VMEM((8, 3 * D_DN), F32)],
        compiler_params=_cp(("arbitrary",)), name="in_proj")(x, norm_w, wt_bf, conv_w8, alog_l, dtb_l)


def _conv_group(cur, halo, w_ref, c):
    rows = cur.shape[0]
    lanes = slice(128 * c, 128 * c + 128)
    xcat = jnp.concatenate([halo, cur], axis=0)
    y = cur * w_ref[CONV_W - 1:CONV_W, lanes]
    for k in range(1, CONV_W):
        sh = pltpu.roll(xcat, k, 0)[8:8 + rows]
        y = y + sh * w_ref[CONV_W - 1 - k:CONV_W - k, lanes]
    return y


def _beta_g(ba, alog_l, dtb_l):
    lane = lax.broadcasted_iota(jnp.int32, ba.shape, 1)
    sig_b = _sigmoid(ba)
    t = ba + dtb_l
    softplus = jnp.maximum(t, 0.0) + jnp.log(1.0 + jnp.exp(-jnp.abs(t)))
    nega = -jnp.exp(alog_l)
    g = nega * softplus
    out = jnp.where(lane < DN_HEADS, sig_b, jnp.where(lane < 2 * DN_HEADS, g, 0.0))
    return out, lane, sig_b, t, nega, g


CPS = 8
CPS_SCAN = 8
SPLIT_ITERS = 2


def _split3(a):
    hi = a.astype(BF16)
    r1 = a - hi.astype(F32)
    mid = r1.astype(BF16)
    return hi, mid, (r1 - mid.astype(F32)).astype(BF16)


def _lane_select():
    r = np.arange(128)
    return jnp.asarray((r[None, :, None] == np.arange(8)[:, None, None]) * np.ones((1, 1, 128)), dtype=BF16)


def _lane_bcast(a3, sel):
    return _dot(a3[0], sel) + _dot(a3[1], sel) + _dot(a3[2], sel)


def _rowsum_b(z, ones_b):
    hi, lo = _split(z)
    return _dot(hi, ones_b) + _dot(lo, ones_b)


def _chunk_cumsum(bg, bgt, tri):
    return _split3(bg), _split3(_mmx(tri, bg)), _mmx_nt(bgt, tri)


def _chunk_common(bg3, gc3, gc_row, h, sel_ref):
    gcc = _lane_bcast(gc3, sel_ref[DN_HEADS + h])
    beta = _dot(bg3[0], sel_ref[h]) + _dot(bg3[1], sel_ref[h])
    gcr = gc_row[DN_HEADS + h:DN_HEADS + h + 1, :]
    ii = lax.broadcasted_iota(jnp.int32, (CHUNK, CHUNK), 0)
    jj = lax.broadcasted_iota(jnp.int32, (CHUNK, CHUNK), 1)
    incl = ii >= jj
    strict = ii > jj
    decay = jnp.exp(jnp.where(incl, gcc[:, 0:CHUNK] - gcr, NEG))
    gl = gcc[CHUNK - 1:CHUNK, :]
    return gcc, beta, incl, strict, decay, gl


def _dn_prep(qn, kn, v, bg, bgt, tri, sel):
    s = qn.shape[0]
    nc = s // CHUNK

    def body(q_ref, k_ref, v_ref, bg_ref, bgt_ref, tri_ref, sel_ref,
             u_ref, w_ref, qd_ref, kt_ref, attn_ref, t_ref, gl_ref):
        tri_v = tri_ref[...]
        ii = lax.broadcasted_iota(jnp.int32, (CHUNK, CHUNK), 0)
        jj = lax.broadcasted_iota(jnp.int32, (CHUNK, CHUNK), 1)
        eye = (ii == jj).astype(F32)

        def chain(cc, h, bg3, gc3, gc_row):
            rows = slice(CHUNK * cc, CHUNK * cc + CHUNK)
            lanes = slice(128 * h, 128 * h + 128)
            gcc, beta, incl, strict, decay, gl = _chunk_common(bg3, gc3, gc_row, h, sel_ref)
            yield
            q = q_ref[rows, lanes]
            k = k_ref[rows, lanes]
            vv = v_ref[rows, lanes]
            kb = k * beta
            egc = jnp.exp(gcc)
            a_mat = jnp.where(strict, _mm_nt(kb, k) * decay, 0.0)
            attn_ref[cc, h] = jnp.where(incl, _mm_nt(q, k) * decay, 0.0)
            qd_ref[rows, lanes] = (q * egc).astype(BF16)
            kt_ref[rows, lanes] = (k * jnp.exp(gl - gcc)).astype(BF16)
            gl_ref[cc, h] = jnp.exp(gl)
            yield
            p = -a_mat
            t = eye + p
            for it in range(5):
                if it < SPLIT_ITERS:
                    ps = _split(p)
                    p = _mm3(ps, ps)
                    yield
                    t = t + _mm3(_split(t), _split(p))
                else:
                    p = _mm(p, p)
                    yield
                    t = t + _mm(t, p)
                yield
            t_ref[cc, h] = t
            ts = _split(t)
            u_ref[rows, lanes] = _mm3(ts, _split(vv * beta))
            w_ref[rows, lanes] = _mm3(ts, _split(kb * egc)).astype(BF16)

        gens = []
        for cc in range(CPS):
            bgv = bg_ref[CHUNK * cc:CHUNK * cc + CHUNK, :]
            bg3, gc3, gc_row = _chunk_cumsum(bgv, bgt_ref[cc], tri_v)
            gens += [chain(cc, h, bg3, gc3, gc_row) for h in range(DN_HEADS)]
        _interleave(gens)

    rows_step = CPS * CHUNK
    big = pl.BlockSpec((rows_step, 512), lambda n: (n, 0))
    sq = pl.BlockSpec((CPS, DN_HEADS, CHUNK, CHUNK), lambda n: (n, 0, 0, 0))
    return pl.pallas_call(
        body, grid=(nc // CPS,),
        in_specs=[big, big, big, pl.BlockSpec((rows_step, 128), lambda n: (n, 0)),
                  pl.BlockSpec((CPS, 8, CHUNK), lambda n: (n, 0, 0)),
                  pl.BlockSpec((CHUNK, CHUNK), lambda n: (0, 0)),
                  pl.BlockSpec((8, 128, 128), lambda n: (0, 0, 0))],
        out_specs=[big, big, big, big, sq, sq, pl.BlockSpec((CPS, DN_HEADS, 1, 128), lambda n: (n, 0, 0, 0))],
        out_shape=[_sds((s, 512))] + [_sds((s, 512), BF16)] * 3 + [_sds((nc, DN_HEADS, CHUNK, CHUNK))] * 2
        + [_sds((nc, DN_HEADS, 1, 128))],
        compiler_params=_cp(("parallel",)), name="dn_prep")(qn, kn, v, bg, bgt, tri, sel)


def _dn_scan(u, w, qd, kt, attn, gl):
    s = u.shape[0]
    nc = s // CHUNK

    def body(u_ref, w_ref, qd_ref, kt_ref, attn_ref, gl_ref, o_ref, vn_ref, sp_ref, st_ref):
        n = pl.program_id(0)

        @pl.when(n == 0)
        def _():
            st_ref[...] = jnp.zeros_like(st_ref)

        def chain(cc, h):
            rows = slice(CHUNK * cc, CHUNK * cc + CHUNK)
            lanes = slice(128 * h, 128 * h + 128)
            st = st_ref[h]
            sp_ref[cc, h] = st
            stb = st.astype(BF16)
            ws = _dot(w_ref[rows, lanes].astype(BF16), stb)
            qs = _dot(qd_ref[rows, lanes].astype(BF16), stb)
            yield
            vn = u_ref[rows, lanes] - ws
            vnb = vn.astype(BF16)
            vn_ref[rows, lanes] = vnb
            o_ref[rows, lanes] = qs + _dot(attn_ref[cc, h].astype(BF16), vnb)
            st_ref[h] = st * gl_ref[cc, h] + _dot_tn(kt_ref[rows, lanes].astype(BF16), vnb)

        for cc in range(CPS_SCAN):
            _interleave([chain(cc, h) for h in range(DN_HEADS)])

    big = pl.BlockSpec((CPS_SCAN * CHUNK, 512), lambda n: (n, 0))
    return pl.pallas_call(
        body, grid=(nc // CPS_SCAN,),
        in_specs=[big, big, big, big,
                  pl.BlockSpec((CPS_SCAN, DN_HEADS, CHUNK, CHUNK), lambda n: (n, 0, 0, 0)),
                  pl.BlockSpec((CPS_SCAN, DN_HEADS, 1, 128), lambda n: (n, 0, 0, 0))],
        out_specs=[big, big, pl.BlockSpec((CPS_SCAN, DN_HEADS, DN_HD, DN_HD), lambda n: (n, 0, 0, 0))],
        out_shape=[_sds((s, 512)), _sds((s, 512), BF16), _sds((nc, DN_HEADS, DN_HD, DN_HD))],
        scratch_shapes=[pltpu.VMEM((DN_HEADS, DN_HD, DN_HD), F32)],
        compiler_params=_cp(("arbitrary",)), name="dn_scan")(u, w, qd, kt, attn, gl)


R4 = PATTERNS[1][1]
R16 = PATTERNS[2][1]
TM = 512


def _pattern_spec(r, width=512):
    return pl.BlockSpec((r, TM // r, width), lambda i: (0, i, 0))


def _pattern_shape(s, r, dtype=F32, width=512):
    return _sds((r, s // r, width), dtype)


SLABS = pltpu.VMEM((4, TM, 128), F32)

HEAD_SLOT = 128 // ATT_HEADS


def _head_expand():
    src = np.arange(128)[:, None]
    dst = np.arange(512)[None, :]
    return jnp.asarray((src == (dst // ATT_HD) * HEAD_SLOT).astype(np.float32), dtype=BF16)


def _head_compact():
    src = np.arange(512)[:, None]
    dst = np.arange(128)[None, :]
    return jnp.asarray((src // ATT_HD == dst // HEAD_SLOT).astype(np.float32), dtype=BF16)


def _to_patterns(val, dsts, scr):
    for c in range(val.shape[1] // 128):
        lanes = slice(128 * c, 128 * c + 128)
        scr[c] = val[:, lanes]
        for dst_ref, r in dsts:
            for a in range(r):
                dst_ref[a, :, lanes] = scr[c, pl.ds(a, TM // r, stride=r), :].astype(dst_ref.dtype)


def _from_pattern(src_ref, r, scr):
    n_slab = src_ref.shape[2] // 128
    for c in range(n_slab):
        for a in range(r):
            scr[c, pl.ds(a, TM // r, stride=r), :] = src_ref[a, :, 128 * c:128 * c + 128].astype(F32)
    return jnp.concatenate([scr[c] for c in range(n_slab)], axis=1) if n_slab > 1 else scr[0]


def _att_pre(proj, qw_t, kw_t, bd64):
    s = proj.shape[0]

    def body(q_ref, k_ref, v_ref, qw_ref, kw_ref, bd_ref,
             q1_ref, k1_ref, v1_ref, q4_ref, k4_ref, v4_ref, q16_ref, k16_ref, v16_ref, scr):
        bd = bd_ref[...]
        q = q_ref[...]
        k = k_ref[...]
        qn = q * _seg_rstd(q, bd, ATT_HD) * qw_ref[...] * (ATT_HD ** -0.5)
        kn = k * _seg_rstd(k, bd, ATT_HD) * kw_ref[...]
        q1_ref[...] = qn.astype(BF16)
        k1_ref[...] = kn.astype(BF16)
        v1_ref[...] = v_ref[...].astype(BF16)
        _to_patterns(qn, ((q4_ref, R4), (q16_ref, R16)), scr)
        _to_patterns(kn, ((k4_ref, R4), (k16_ref, R16)), scr)
        _to_patterns(v_ref[...], ((v4_ref, R4), (v16_ref, R16)), scr)

    row = pl.BlockSpec((1, 512), lambda i: (0, 0))
    tok = pl.BlockSpec((TM, 512), lambda i: (i, 0))
    return pl.pallas_call(
        body, grid=(s // TM,),
        in_specs=[pl.BlockSpec((TM, 512), lambda i: (i, COL_ATT_Q)),
                  pl.BlockSpec((TM, 512), lambda i: (i, COL_ATT_K)),
                  pl.BlockSpec((TM, 512), lambda i: (i, COL_ATT_V)),
                  row, row, pl.BlockSpec((512, 128), lambda i: (0, 0))],
        out_specs=[tok] * 3 + [_pattern_spec(R4)] * 3 + [_pattern_spec(R16)] * 3,
        out_shape=[_sds((s, 512), BF16)] * 3 + [_pattern_shape(s, R4, BF16)] * 3 + [_pattern_shape(s, R16, BF16)] * 3,
        scratch_shapes=[SLABS],
        compiler_params=_cp(("parallel",)), name="att_pre")(proj, proj, proj, qw_t, kw_t, bd64)


def _bias_table(rb_ref, bk_ref, o_ref):
    for p in range(len(PATTERNS)):
        bk = bk_ref[p]
        for h in range(ATT_HEADS):
            acc = jnp.zeros((BLK, 2 * BLK), F32)
            for b in range(N_BUCKETS):
                acc = jnp.where(bk == b, rb_ref[h, b], acc)
            o_ref[p, h] = acc


def _bias_grad(ds_refs, bk_ref, o_ref):
    for h in range(ATT_HEADS):
        for b in range(N_BUCKETS):
            tot = jnp.zeros((), F32)
            for p, ds_ref in enumerate(ds_refs):
                tot = tot + jnp.sum(jnp.where(bk_ref[p] == b, ds_ref[h], 0.0))
            o_ref[h, b] = tot


QB_FWD = 2
QB_BWD = 4


def _att_masks(has_prev):
    qi = lax.broadcasted_iota(jnp.int32, (BLK, BLK), 0)
    kj = lax.broadcasted_iota(jnp.int32, (BLK, BLK), 1)
    lane = lax.broadcasted_iota(jnp.int32, (BLK, 2 * ATT_HD), 1)
    return jnp.logical_and(kj >= qi, has_prev), kj <= qi, lane < ATT_HD


def _head_lanes(h):
    half = h % 2
    return slice(ATT_HD * h, ATT_HD * h + ATT_HD), slice(ATT_HD * half, ATT_HD * half + ATT_HD)


def _att_scores(qm, kp2, kc2, bias_h, mask_prev, mask_cur):
    s_prev = jnp.where(mask_prev, _dot_nt(qm, kp2) + bias_h[:, :BLK], NEG)
    s_cur = jnp.where(mask_cur, _dot_nt(qm, kc2) + bias_h[:, BLK:], NEG)
    return s_prev, s_cur


def _att_fwd(q, k, v, bias, p_idx, r, name):
    QB = QB_FWD
    s = q.shape[0]
    nblk = s // BLK
    nseq = nblk // r

    def body(q_ref, kp_ref, kc_ref, vp_ref, vc_ref, b_ref, o_ref, lse_ref):
        j = pl.program_id(0)

        def head(h, rows, masks, q2, kp2, kc2, vp2, vc2):
            mask_prev, mask_cur, lo_half = masks
            out_l, pair_l = _head_lanes(h)
            sel = lo_half if h % 2 == 0 else jnp.logical_not(lo_half)
            qm = jnp.where(sel, q2, jnp.zeros_like(q2))
            s_prev, s_cur = _att_scores(qm, kp2, kc2, b_ref[0, h], mask_prev, mask_cur)
            yield
            m = jnp.maximum(jnp.max(s_prev, axis=1, keepdims=True), jnp.max(s_cur, axis=1, keepdims=True))
            p_prev = jnp.exp(s_prev - m)
            p_cur = jnp.exp(s_cur - m)
            l = jnp.sum(p_prev, axis=1, keepdims=True) + jnp.sum(p_cur, axis=1, keepdims=True)
            yield
            o2 = _dot(p_prev.astype(BF16), vp2) + _dot(p_cur.astype(BF16), vc2)
            o_ref[rows, out_l] = (o2 * (1.0 / l))[:, pair_l].astype(BF16)
            lse_ref[rows, HEAD_SLOT * h:HEAD_SLOT * h + HEAD_SLOT] = jnp.broadcast_to(m + jnp.log(l), (BLK, HEAD_SLOT))

        for sub in range(QB):
            rows = slice(BLK * sub, BLK * sub + BLK)
            before = slice(BLK * (sub - 1), BLK * sub)
            masks = _att_masks(((QB * j + sub) % nseq) != 0)
            gens = []
            for pp in range(ATT_HEADS // 2):
                lanes = slice(128 * pp, 128 * pp + 128)
                kp = kp_ref[:, lanes] if sub == 0 else kc_ref[before, lanes]
                vp = vp_ref[:, lanes] if sub == 0 else vc_ref[before, lanes]
                slabs = (q_ref[rows, lanes], kp, kc_ref[rows, lanes], vp, vc_ref[rows, lanes])
                gens += [head(2 * pp, rows, masks, *slabs), head(2 * pp + 1, rows, masks, *slabs)]
            _interleave(gens)

    cur = pl.BlockSpec((QB * BLK, 512), lambda j: (j, 0))
    prev = pl.BlockSpec((BLK, 512), lambda j: (jnp.maximum(QB * j - 1, 0), 0))
    return pl.pallas_call(
        body, grid=(nblk // QB,),
        in_specs=[cur, prev, cur, prev, cur,
                  pl.BlockSpec((1, ATT_HEADS, BLK, 2 * BLK), lambda j: (p_idx, 0, 0, 0))],
        out_specs=[cur, pl.BlockSpec((QB * BLK, 128), lambda j: (j, 0))],
        out_shape=[_sds((s, 512), BF16), _sds((s, 128))],
        compiler_params=_cp(("parallel",)), name=name)(q, k, k, v, v, bias)


def _post_fwd(o_dn, proj, o_pats, lse_pats, dnw_t, bd128):
    s = o_dn.shape[0]

    def body(o_ref, z_ref, gate_ref, o1_ref, o4_ref, o16_ref, s1_ref, s4_ref, s16_ref, wn_ref, bd_ref, ex_ref,
             mixed_ref, oatt_ref, l1_ref, l4_ref, l16_ref, scr_a, scr_b, scr_c, scr_d):
        o = o_ref[...]
        z = z_ref[...]
        rstd = _seg_rstd(o, bd_ref[...], DN_HD)
        y_dn = o * rstd * wn_ref[...] * (z * _sigmoid(z))
        mixed_ref[:, 0:512] = y_dn.astype(BF16)
        lses = (s1_ref[...], _from_pattern(s4_ref, R4, scr_a), _from_pattern(s16_ref, R16, scr_b))
        m = jnp.maximum(jnp.maximum(lses[0], lses[1]), lses[2])
        tot = jnp.exp(lses[0] - m) + jnp.exp(lses[1] - m) + jnp.exp(lses[2] - m)
        big_l = m + jnp.log(tot)
        l1_ref[...] = big_l
        _to_patterns(big_l, ((l4_ref, R4), (l16_ref, R16)), scr_a)
        ex = ex_ref[...]
        outs = (o1_ref[...], _from_pattern(o4_ref, R4, scr_c), _from_pattern(o16_ref, R16, scr_d))
        acc = jnp.zeros_like(o)
        for lse_p, o_p in zip(lses, outs):
            acc = acc + _lane_bcast(_split3(jnp.exp(lse_p - big_l)), ex) * o_p
        gate = gate_ref[...]
        oatt_ref[...] = acc
        mixed_ref[:, 512:1024] = (acc * (gate * _sigmoid(gate))).astype(BF16)

    blk = pl.BlockSpec((TM, 512), lambda i: (i, 0))
    cblk = pl.BlockSpec((TM, 128), lambda i: (i, 0))
    p4, p16 = _pattern_spec(R4), _pattern_spec(R16)
    c4, c16 = _pattern_spec(R4, 128), _pattern_spec(R16, 128)
    return pl.pallas_call(
        body, grid=(s // TM,),
        in_specs=[blk, pl.BlockSpec((TM, 512), lambda i: (i, COL_Z)),
                  pl.BlockSpec((TM, 512), lambda i: (i, COL_GATE)), blk, p4, p16, cblk, c4, c16,
                  pl.BlockSpec((1, 512), lambda i: (0, 0)), pl.BlockSpec((512, 128), lambda i: (0, 0)),
                  pl.BlockSpec((128, 512), lambda i: (0, 0))],
        out_specs=[pl.BlockSpec((TM, D_MODEL), lambda i: (i, 0)), blk, cblk, c4, c16],
        out_shape=[_sds((s, D_MODEL), BF16), _sds((s, 512)), _sds((s, 128)), _pattern_shape(s, R4, F32, 128),
                   _pattern_shape(s, R16, F32, 128)],
        scratch_shapes=[SLABS] * 4,
        compiler_params=_cp(("parallel",)), name="post_fwd")(o_dn, proj, proj, *o_pats, *lse_pats, dnw_t, bd128,
                                                              _head_expand())


def _out_post(x, mixed, w_out_bf, tgt, o_dn, proj, o_att, dnw_t, bd128):
    s = o_dn.shape[0]
    tm = TM

    def body(x_ref, m_ref, w_ref, t_ref, o_ref, z_ref, gate_ref, oatt_ref, wn_ref, bd128_ref, hc_ref,
             dy_ref, loss_ref, dw_ref, do_ref, dz_ref, dgate_ref, doatt_ref, do4_ref, do16_ref, delta_ref, dl4_ref,
             dl16_ref, dnw_ref, scr):
        i = pl.program_id(0)

        @pl.when(i == 0)
        def _():
            loss_ref[...] = jnp.zeros_like(loss_ref)
            dw_ref[...] = jnp.zeros_like(dw_ref)
            dnw_ref[...] = jnp.zeros_like(dnw_ref)

        err = x_ref[...] + _dot(m_ref[...], w_ref[...]) - t_ref[...]
        d_y = err * (1.0 / D_MODEL)
        dy_ref[...] = d_y
        part = 0.5 * jnp.sum(jnp.mean(err * err, axis=-1, keepdims=True), axis=0, keepdims=True)
        loss_ref[...] = loss_ref[...] + part
        dyb = d_y.astype(BF16)
        dw_ref[...] = dw_ref[...] + _dot_tn(m_ref[...], dyb)
        dm = _dot_nt(dyb, w_ref[...])

        bd128v = bd128_ref[...]
        o = o_ref[...]
        z = z_ref[...]
        wn = wn_ref[...]
        dy = dm[:, 0:512]
        rstd = _seg_rstd(o, bd128v, DN_HD)
        nrm = o * rstd
        sz = _sigmoid(z)
        dz_ref[...] = (dy * nrm * wn * _silu_grad(z, sz)).astype(BF16)
        dn = dy * z * sz
        gw = dn * wn
        do_ref[...] = rstd * (gw - nrm * _seg_expand(_seg_mean(gw * nrm, bd128v, DN_HD), bd128v, DN_HD))
        colsum = jnp.sum(dn * nrm, axis=0, keepdims=True)
        fold = colsum[:, 0:128] + colsum[:, 128:256] + colsum[:, 256:384] + colsum[:, 384:512]
        dnw_ref[...] = dnw_ref[...] + fold
        dya = dm[:, 512:1024]
        gate = gate_ref[...]
        oatt = oatt_ref[...]
        sg = _sigmoid(gate)
        dgate_ref[...] = (dya * oatt * _silu_grad(gate, sg)).astype(BF16)
        doa = dya * gate * sg
        doatt_ref[...] = doa.astype(BF16)
        delta = _segsum(doa * oatt, hc_ref[...])
        delta_ref[...] = delta
        _to_patterns(doa, ((do4_ref, R4), (do16_ref, R16)), scr)
        _to_patterns(delta, ((dl4_ref, R4), (dl16_ref, R16)), scr)

    wide = pl.BlockSpec((tm, D_MODEL), lambda i: (i, 0))
    full = pl.BlockSpec((D_MODEL, D_MODEL), lambda i: (0, 0))
    blk = pl.BlockSpec((tm, 512), lambda i: (i, 0))
    cblk = pl.BlockSpec((tm, 128), lambda i: (i, 0))
    p4, p16 = _pattern_spec(R4), _pattern_spec(R16)
    c4, c16 = _pattern_spec(R4, 128), _pattern_spec(R16, 128)
    return pl.pallas_call(
        body, grid=(s // tm,),
        in_specs=[wide, wide, full, wide, blk,
                  pl.BlockSpec((tm, 512), lambda i: (i, COL_Z)), pl.BlockSpec((tm, 512), lambda i: (i, COL_GATE)),
                  blk, pl.BlockSpec((1, 512), lambda i: (0, 0)), pl.BlockSpec((512, 128), lambda i: (0, 0)),
                  pl.BlockSpec((512, 128), lambda i: (0, 0))],
        out_specs=[wide, pl.BlockSpec((8, 128), lambda i: (0, 0)), full,
                   blk, blk, blk, blk, p4, p16, cblk, c4, c16, pl.BlockSpec((8, 128), lambda i: (0, 0))],
        out_shape=[_sds((s, D_MODEL)), _sds((8, 128)), _sds((D_MODEL, D_MODEL)), _sds((s, 512))]
        + [_sds((s, 512), BF16)] * 3 + [_pattern_shape(s, R4, BF16), _pattern_shape(s, R16, BF16), _sds((s, 128)),
                                        _pattern_shape(s, R4, F32, 128), _pattern_shape(s, R16, F32, 128),
                                        _sds((8, 128))],
        scratch_shapes=[SLABS],
        compiler_params=_cp(("arbitrary",)), name="out_post")(x, mixed, w_out_bf, tgt, o_dn, proj, proj, o_att, dnw_t,
                                                               bd128, _head_compact())


def _att_bwd(q, k, v, do, big_l, delta, bias, p_idx, r, name):
    QB = QB_BWD
    s = q.shape[0]
    nblk = s // BLK
    nseq = nblk // r
    nstep = nblk // QB

    def body(q_ref, kp_ref, kc_ref, vp_ref, vc_ref, do_ref, l_ref, dl_ref, b_ref,
             dq_ref, dk_ref, dv_ref, ds_ref, dkc_ref, dvc_ref):
        j = pl.program_id(0)

        @pl.when(j == 0)
        def _():
            dkc_ref[...] = jnp.zeros_like(dkc_ref)
            dvc_ref[...] = jnp.zeros_like(dvc_ref)
            ds_ref[...] = jnp.zeros_like(ds_ref)

        @pl.when(j < nstep)
        def _():
            def head(h, sub, masks, q2, do2, kp2, kc2, vp2, vc2):
                mask_prev, mask_cur, lo_half = masks
                rows = slice(BLK * sub, BLK * sub + BLK)
                out_l, pair_l = _head_lanes(h)
                sel = lo_half if h % 2 == 0 else jnp.logical_not(lo_half)
                qm = jnp.where(sel, q2, jnp.zeros_like(q2))
                dom = jnp.where(sel, do2, jnp.zeros_like(do2))
                s_prev, s_cur = _att_scores(qm, kp2, kc2, b_ref[0, h], mask_prev, mask_cur)
                dp_prev = _dot_nt(dom, vp2)
                dp_cur = _dot_nt(dom, vc2)
                yield
                lh = l_ref[rows, HEAD_SLOT * h:HEAD_SLOT * h + 1]
                dh = dl_ref[rows, HEAD_SLOT * h:HEAD_SLOT * h + 1]
                p_prev = jnp.exp(s_prev - lh)
                p_cur = jnp.exp(s_cur - lh)
                ds_prev = p_prev * (dp_prev - dh)
                ds_cur = p_cur * (dp_cur - dh)
                ds_ref[h, :, 0:BLK] = ds_ref[h, :, 0:BLK] + ds_prev
                ds_ref[h, :, BLK:2 * BLK] = ds_ref[h, :, BLK:2 * BLK] + ds_cur
                dsb_prev, dsb_cur = ds_prev.astype(BF16), ds_cur.astype(BF16)
                pb_prev, pb_cur = p_prev.astype(BF16), p_cur.astype(BF16)
                yield
                dq_ref[rows, out_l] = (_dot(dsb_prev, kp2) + _dot(dsb_cur, kc2))[:, pair_l].astype(BF16)
                dk_prev = _dot_tn(dsb_prev, q2)[:, pair_l]
                dv_prev = _dot_tn(pb_prev, do2)[:, pair_l]
                if sub == 0:
                    last = slice(BLK * (QB - 1), BLK * QB)
                    dk_ref[last, out_l] = (dkc_ref[last, out_l] + dk_prev).astype(BF16)
                    dv_ref[last, out_l] = (dvc_ref[last, out_l] + dv_prev).astype(BF16)
                else:
                    before = slice(BLK * (sub - 1), BLK * sub)
                    dkc_ref[before, out_l] = dkc_ref[before, out_l] + dk_prev
                    dvc_ref[before, out_l] = dvc_ref[before, out_l] + dv_prev
                yield
                dkc_ref[rows, out_l] = _dot_tn(dsb_cur, q2)[:, pair_l]
                dvc_ref[rows, out_l] = _dot_tn(pb_cur, do2)[:, pair_l]

            done = slice(0, BLK * (QB - 1))
            dk_ref[done, :] = dkc_ref[done, :].astype(BF16)
            dv_ref[done, :] = dvc_ref[done, :].astype(BF16)
            for sub in range(QB):
                rows = slice(BLK * sub, BLK * sub + BLK)
                before = slice(BLK * (sub - 1), BLK * sub)
                masks = _att_masks(((QB * j + sub) % nseq) != 0)
                gens = []
                for pp in range(ATT_HEADS // 2):
                    lanes = slice(128 * pp, 128 * pp + 128)
                    kp = kp_ref[:, lanes] if sub == 0 else kc_ref[before, lanes]
                    vp = vp_ref[:, lanes] if sub == 0 else vc_ref[before, lanes]
                    slabs = (q_ref[rows, lanes], do_ref[rows, lanes], kp, kc_ref[rows, lanes], vp,
                             vc_ref[rows, lanes])
                    gens += [head(2 * pp, sub, masks, *slabs), head(2 * pp + 1, sub, masks, *slabs)]
                _interleave(gens)

        @pl.when(j == nstep)
        def _():
            dk_ref[...] = dkc_ref[...].astype(BF16)
            dv_ref[...] = dvc_ref[...].astype(BF16)

    last_step = nstep - 1
    cur = pl.BlockSpec((QB * BLK, 512), lambda j: (jnp.minimum(j, last_step), 0))
    compact = pl.BlockSpec((QB * BLK, 128), lambda j: (jnp.minimum(j, last_step), 0))
    lag = pl.BlockSpec((QB * BLK, 512), lambda j: (jnp.clip(j - 1, 0, last_step), 0))
    prev = pl.BlockSpec((BLK, 512), lambda j: (jnp.clip(QB * j - 1, 0, nblk - 1), 0))
    return pl.pallas_call(
        body, grid=(nstep + 1,),
        in_specs=[cur, prev, cur, prev, cur, cur, compact, compact,
                  pl.BlockSpec((1, ATT_HEADS, BLK, 2 * BLK), lambda j: (p_idx, 0, 0, 0))],
        out_specs=[cur, lag, lag, pl.BlockSpec((ATT_HEADS, BLK, 2 * BLK), lambda j: (0, 0, 0))],
        out_shape=[_sds((s, 512), BF16)] * 3 + [_sds((ATT_HEADS, BLK, 2 * BLK))],
        scratch_shapes=[pltpu.VMEM((QB * BLK, 512), F32), pltpu.VMEM((QB * BLK, 512), F32)],
        compiler_params=_cp(("arbitrary",)), name=name)(q, k, k, v, v, do, big_l, delta, bias)


def _att_pre_bwd(dq_pats, dk_pats, dv_pats, proj, qw_t, kw_t, bd64):
    s = proj.shape[0]
    tm = TM

    def body(dq1_ref, dq4_ref, dq16_ref, dk1_ref, dk4_ref, dk16_ref, dv1_ref, dv4_ref, dv16_ref,
             q_ref, k_ref, qw_ref, kw_ref, bd_ref,
             dqr_ref, dkr_ref, dvr_ref, dqw_ref, dkw_ref, scr4, scr16):
        i = pl.program_id(0)

        @pl.when(i == 0)
        def _():
            dqw_ref[...] = jnp.zeros_like(dqw_ref)
            dkw_ref[...] = jnp.zeros_like(dkw_ref)

        bd = bd_ref[...]

        def total(d1_ref, d4_ref, d16_ref):
            return d1_ref[...] + _from_pattern(d4_ref, R4, scr4) + _from_pattern(d16_ref, R16, scr16)

        def one(d_refs, x_ref, w_ref, scale, dx_ref, dw_ref):
            dy = total(*d_refs) * scale
            x = x_ref[...]
            rstd = _seg_rstd(x, bd, ATT_HD)
            nrm = x * rstd
            dw_ref[...] = dw_ref[...] + jnp.sum(dy * nrm, axis=0, keepdims=True)
            g = dy * w_ref[...]
            dx_ref[...] = (rstd * (g - nrm * _seg_expand(_seg_mean(g * nrm, bd, ATT_HD), bd, ATT_HD))).astype(BF16)

        one((dq1_ref, dq4_ref, dq16_ref), q_ref, qw_ref, ATT_HD ** -0.5, dqr_ref, dqw_ref)
        one((dk1_ref, dk4_ref, dk16_ref), k_ref, kw_ref, 1.0, dkr_ref, dkw_ref)
        dvr_ref[...] = total(dv1_ref, dv4_ref, dv16_ref).astype(BF16)

    blk = pl.BlockSpec((tm, 512), lambda i: (i, 0))
    pats = [blk, _pattern_spec(R4), _pattern_spec(R16)]
    row = pl.BlockSpec((1, 512), lambda i: (0, 0))
    acc = pl.BlockSpec((8, 512), lambda i: (0, 0))
    return pl.pallas_call(
        body, grid=(s // tm,),
        in_specs=pats * 3 + [pl.BlockSpec((tm, 512), lambda i: (i, COL_ATT_Q)),
                             pl.BlockSpec((tm, 512), lambda i: (i, COL_ATT_K)), row, row,
                             pl.BlockSpec((512, 128), lambda i: (0, 0))],
        out_specs=[blk, blk, blk, acc, acc],
        out_shape=[_sds((s, 512), BF16)] * 3 + [_sds((8, 512))] * 2,
        scratch_shapes=[SLABS] * 2,
        compiler_params=_cp(("arbitrary",)), name="att_pre_bwd")(*dq_pats, *dk_pats, *dv_pats, proj, proj,
                                                                  qw_t, kw_t, bd64)


def _dn_scan_bwd(do, sp, qd, kt, w, vn, attn, gl):
    s = do.shape[0]
    nc = s // CHUNK

    def body(do_ref, sp_ref, qd_ref, kt_ref, w_ref, vn_ref, attn_ref, gl_ref,
             du_ref, dqd_ref, dkt_ref, dw_ref, dattn_ref, dgl_ref, ds_ref):
        n = pl.program_id(0)

        @pl.when(n == 0)
        def _():
            ds_ref[...] = jnp.zeros_like(ds_ref)

        def chain(cc, h):
            rows = slice(CHUNK * cc, CHUNK * cc + CHUNK)
            lanes = slice(128 * h, 128 * h + 128)
            dsn = ds_ref[h]
            st = sp_ref[cc, h]
            dsb, stb = dsn.astype(BF16), st.astype(BF16)
            dob = do_ref[rows, lanes].astype(BF16)
            vnb = vn_ref[rows, lanes].astype(BF16)
            dvn = _dot_tn(attn_ref[cc, h].astype(BF16), dob) + _dot(kt_ref[rows, lanes].astype(BF16), dsb)
            du_ref[rows, lanes] = dvn
            dqd_ref[rows, lanes] = _dot_nt(dob, stb)
            dattn_ref[cc, h] = _dot_nt(dob, vnb)
            dkt_ref[rows, lanes] = _dot_nt(vnb, dsb)
            tot = jnp.sum(jnp.sum(st * dsn, axis=1, keepdims=True), axis=0, keepdims=True)
            dgl_ref[cc, h] = jnp.broadcast_to(tot, (1, 128))
            qdo = _dot_tn(qd_ref[rows, lanes].astype(BF16), dob)
            yield
            dvb = dvn.astype(BF16)
            dw_ref[rows, lanes] = -_dot_nt(dvb, stb)
            ds_ref[h] = qdo + dsn * gl_ref[cc, h] - _dot_tn(w_ref[rows, lanes].astype(BF16), dvb)

        for cc in reversed(range(CPS_SCAN)):
            _interleave([chain(cc, h) for h in range(DN_HEADS)])

    nsteps = nc // CPS_SCAN
    big = pl.BlockSpec((CPS_SCAN * CHUNK, 512), lambda n: (nsteps - 1 - n, 0))
    sq = pl.BlockSpec((CPS_SCAN, DN_HEADS, CHUNK, CHUNK), lambda n: (nsteps - 1 - n, 0, 0, 0))
    glb = pl.BlockSpec((CPS_SCAN, DN_HEADS, 1, 128), lambda n: (nsteps - 1 - n, 0, 0, 0))
    return pl.pallas_call(
        body, grid=(nsteps,),
        in_specs=[big, pl.BlockSpec((CPS_SCAN, DN_HEADS, DN_HD, DN_HD), lambda n: (nsteps - 1 - n, 0, 0, 0)),
                  big, big, big, big, sq, glb],
        out_specs=[big, big, big, big, sq, glb],
        out_shape=[_sds((s, 512))] * 4 + [_sds((nc, DN_HEADS, CHUNK, CHUNK)), _sds((nc, DN_HEADS, 1, 128))],
        scratch_shapes=[pltpu.VMEM((DN_HEADS, DN_HD, DN_HD), F32)],
        compiler_params=_cp(("arbitrary",)), name="dn_scan_bwd")(do, sp, qd, kt, w, vn, attn, gl)


def _dn_prep_bwd(qn, kn, v, bg, bgt, tri, sel, t_inv, attn, u, w, du, dw, dqd, dkt, dattn, dgl):
    s = qn.shape[0]
    nc = s // CHUNK

    def body(q_ref, k_ref, v_ref, bg_ref, bgt_ref, tri_ref, sel_ref, t_ref, attn_ref, u_ref, w_ref,
             du_ref, dw_ref, dqd_ref, dkt_ref, dattn_ref, dgl_ref,
             dq_ref, dk_ref, dv_ref, dbg_ref):
        tri_v = tri_ref[...]
        lane = lax.broadcasted_iota(jnp.int32, (CHUNK, 128), 1)
        rowi = lax.broadcasted_iota(jnp.int32, (CHUNK, 128), 0)
        ones_b = jnp.ones((CHUNK, 128), BF16)
        ones_sq = jnp.ones((128, 128), BF16)
        parts = [[] for _ in range(CPS)]

        def chain(cc, h, bg3, gc3, gc_row):
            rows = slice(CHUNK * cc, CHUNK * cc + CHUNK)
            lanes = slice(128 * h, 128 * h + 128)
            gcc, beta, incl, strict, decay, gl = _chunk_common(bg3, gc3, gc_row, h, sel_ref)
            yield
            q = q_ref[rows, lanes]
            k = k_ref[rows, lanes]
            vv = v_ref[rows, lanes]
            ts = _split(t_ref[cc, h])
            egc = jnp.exp(gcc)
            kb = k * beta
            a_mat = jnp.where(strict, _mm_nt(kb, k) * decay, 0.0)
            dvb = _mm3_tn(ts, _split(du_ref[rows, lanes]))
            dkbg = _mm3_tn(ts, _split(dw_ref[rows, lanes]))
            yield
            d_a = jnp.where(strict, -(_mm_nt(dvb, u_ref[rows, lanes]) + _mm_nt(dkbg, w_ref[rows, lanes])), 0.0)
            d_m = d_a * decay
            dattn_m = jnp.where(incl, dattn_ref[cc, h], 0.0)
            dqk = dattn_m * decay
            e_hi, e_lo = _split(d_a * a_mat + dattn_m * attn_ref[cc, h])
            yield
            dkb = _mm(d_m, k)
            dk = _mm_tn(d_m, kb) + _mm_tn(dqk, q)
            dq = _mm(dqk, k)
            e_colsum = _dot_tn(e_hi, ones_b) + _dot_tn(e_lo, ones_b)
            e_rowsum = _dot(e_hi, ones_b) + _dot(e_lo, ones_b)
            dqd = dqd_ref[rows, lanes]
            dkt = dkt_ref[rows, lanes]
            sums = _rowsum_b(jnp.concatenate([dqd * q, dkt * k, dkbg * k, dkb * k, dvb * vv], axis=0), ones_sq)
            s_dqd, s_dkt, rk, s_dkb, s_dvb = (sums[CHUNK * n:CHUNK * n + CHUNK] for n in range(5))
            yield
            tail = jnp.exp(gl - gcc)
            r = s_dkt * tail
            dgl_tot = jnp.sum(r, axis=0, keepdims=True) + dgl_ref[cc, h] * jnp.exp(gl)
            dgc = e_rowsum - e_colsum + s_dqd * egc - r + rk * beta * egc
            dgc = dgc + jnp.where(rowi == CHUNK - 1, dgl_tot, 0.0)
            dq_ref[rows, lanes] = dq + dqd * egc
            dk_ref[rows, lanes] = dk + dkt * tail + dkbg * (beta * egc) + dkb * beta
            dv_ref[rows, lanes] = dvb * beta
            parts[cc].append((h, dgc, rk * egc + s_dkb + s_dvb))

        gens = []
        for cc in range(CPS):
            bgv = bg_ref[CHUNK * cc:CHUNK * cc + CHUNK, :]
            bg3, gc3, gc_row = _chunk_cumsum(bgv, bgt_ref[cc], tri_v)
            gens += [chain(cc, h, bg3, gc3, gc_row) for h in range(DN_HEADS)]
        _interleave(gens)
        for cc in range(CPS):
            dgc_mat = jnp.zeros((CHUNK, 128), F32)
            dbeta_mat = jnp.zeros((CHUNK, 128), F32)
            for h, dgc, dbeta in parts[cc]:
                dgc_mat = dgc_mat + jnp.where(lane == DN_HEADS + h, dgc, 0.0)
                dbeta_mat = dbeta_mat + jnp.where(lane == h, dbeta, 0.0)
            dbg_ref[CHUNK * cc:CHUNK * cc + CHUNK, :] = _mmx_tn(tri_v, dgc_mat) + dbeta_mat

    big = pl.BlockSpec((CPS * CHUNK, 512), lambda n: (n, 0))
    sq = pl.BlockSpec((CPS, DN_HEADS, CHUNK, CHUNK), lambda n: (n, 0, 0, 0))
    glb = pl.BlockSpec((CPS, DN_HEADS, 1, 128), lambda n: (n, 0, 0, 0))
    small = pl.BlockSpec((CPS * CHUNK, 128), lambda n: (n, 0))
    return pl.pallas_call(
        body, grid=(nc // CPS,),
        in_specs=[big, big, big, small, pl.BlockSpec((CPS, 8, CHUNK), lambda n: (n, 0, 0)),
                  pl.BlockSpec((CHUNK, CHUNK), lambda n: (0, 0)),
                  pl.BlockSpec((8, 128, 128), lambda n: (0, 0, 0)), sq, sq, big, big,
                  big, big, big, big, sq, glb],
        out_specs=[big, big, big, small],
        out_shape=[_sds((s, 512))] * 3 + [_sds((s, 128))],
        compiler_params=_cp(("parallel",)), name="dn_prep_bwd")(qn, kn, v, bg, bgt, tri, sel, t_inv, attn, u, w,
                                                                  du, dw, dqd, dkt, dattn, dgl)


def _dn_pre_bwd(dqn, dkn, dv, dbg, proj, conv_w8, alog_l, dtb_l):
    s = proj.shape[0]
    tr = 512
    nh = tr // 8
    nblk = s // tr

    def body(dq_ref, dk_ref, dv_ref, dbg_ref, u_ref, halo_ref, ba_ref, w_ref, al_ref, dt_ref,
             du_ref, dba_ref, dsm_ref, dw_ref, nxt_ref):
        i = pl.program_id(0)

        @pl.when(i == 0)
        def _():
            dsm_ref[...] = jnp.zeros_like(dsm_ref)
            dw_ref[...] = jnp.zeros_like(dw_ref)
            nxt_ref[...] = jnp.zeros_like(nxt_ref)

        keep = (i < nblk - 1).astype(F32)
        row8 = lax.broadcasted_iota(jnp.int32, (8, 128), 0)
        for c in range(12):
            lanes = slice(128 * c, 128 * c + 128)
            u = u_ref[:, lanes]
            xcat = jnp.concatenate([halo_ref[:, lanes] * keep, u], axis=0)
            shifted = [u] + [pltpu.roll(xcat, k, 0)[8:8 + tr] for k in range(1, CONV_W)]
            y = shifted[0] * w_ref[CONV_W - 1:CONV_W, lanes]
            for k in range(1, CONV_W):
                y = y + shifted[k] * w_ref[CONV_W - 1 - k:CONV_W - k, lanes]
            sg = _sigmoid(y)
            sv = y * sg
            if c < 8:
                rs = lax.rsqrt(jnp.sum(sv * sv, axis=1, keepdims=True) + EPS)
                n = sv * rs
                if c < 4:
                    dn = dq_ref[:, lanes] * (DN_HD ** -0.5)
                else:
                    dn = dk_ref[:, slice(128 * (c - 4), 128 * (c - 3))]
                dsv = rs * (dn - n * jnp.sum(dn * n, axis=1, keepdims=True))
            else:
                dsv = dv_ref[:, slice(128 * (c - 8), 128 * (c - 7))]
            dyc = dsv * _silu_grad(y, sg)
            dcat = jnp.concatenate([dyc, nxt_ref[:, lanes]], axis=0)
            du = dyc * w_ref[CONV_W - 1:CONV_W, lanes]
            dwc = jnp.zeros((8, 128), F32)
            for k in range(CONV_W):
                if k:
                    du = du + pltpu.roll(dcat, tr + 8 - k, 0)[0:tr] * w_ref[CONV_W - 1 - k:CONV_W - k, lanes]
                dwc = dwc + jnp.where(row8 == CONV_W - 1 - k, jnp.sum(dyc * shifted[k], axis=0, keepdims=True), 0.0)
            du_ref[:, lanes] = du.astype(BF16)
            dw_ref[:, lanes] = dw_ref[:, lanes] + dwc
            nxt_ref[:, lanes] = dyc[0:8]
        _, lane, sig_b, t, nega, g = _beta_g(ba_ref[...], al_ref[...], dt_ref[...])
        dbg = dbg_ref[...]
        da = dbg * nega * _sigmoid(t)
        is_b = lane < DN_HEADS
        is_a = jnp.logical_and(lane >= DN_HEADS, lane < 2 * DN_HEADS)
        dba_ref[...] = jnp.where(is_b, dbg * sig_b * (1.0 - sig_b), jnp.where(is_a, da, 0.0)).astype(BF16)
        d_alog = jnp.sum(jnp.where(is_a, dbg * g, 0.0), axis=0, keepdims=True)
        d_dtb = jnp.sum(jnp.where(is_a, da, 0.0), axis=0, keepdims=True)
        dsm_ref[...] = dsm_ref[...] + jnp.where(row8 == 0, d_alog, jnp.where(row8 == 1, d_dtb, 0.0))

    last = nblk - 1
    blk = pl.BlockSpec((tr, 512), lambda i: (last - i, 0))
    wide = pl.BlockSpec((tr, 1536), lambda i: (last - i, 0))
    return pl.pallas_call(
        body, grid=(nblk,),
        in_specs=[blk, blk, blk, pl.BlockSpec((tr, 128), lambda i: (last - i, 0)), wide,
                  pl.BlockSpec((8, 1536), lambda i: (jnp.maximum((last - i) * nh - 1, 0), 0)),
                  pl.BlockSpec((tr, 128), lambda i: (last - i, COL_BA_128)),
                  pl.BlockSpec((8, 1536), lambda i: (0, 0)),
                  pl.BlockSpec((1, 128), lambda i: (0, 0)), pl.BlockSpec((1, 128), lambda i: (0, 0))],
        out_specs=[wide, pl.BlockSpec((tr, 128), lambda i: (last - i, 0)),
                   pl.BlockSpec((8, 128), lambda i: (0, 0)), pl.BlockSpec((8, 1536), lambda i: (0, 0))],
        out_shape=[_sds((s, 1536), BF16), _sds((s, 128), BF16), _sds((8, 128)), _sds((8, 1536))],
        scratch_shapes=[pltpu.VMEM((8, 1536), F32)],
        compiler_params=_cp(("arbitrary",)), name="dn_pre_bwd")(dqn, dkn, dv, dbg, proj, proj, proj, conv_w8,
                                                                 alog_l, dtb_l)


PIECE_WIDTHS = (1536, 512, 512, 512, 512, 512, 128)


def _in_bwd_dx(pieces, w_bf, x, dy, norm_w, ds_accs, buckets, chip_sums=None):
    s = x.shape[0]
    tm = 512
    n_piece = len(PIECE_WIDTHS)
    n_step = s // tm
    n_arr = 0 if chip_sums is None else len(chip_sums)
    n_sent = 3 * n_arr

    def body(*refs):
        refs = list(refs)

        def take(n):
            return [refs.pop(0) for _ in range(n)]

        piece_refs = take(n_piece)
        w_ref, x_ref, dy_ref, nw_ref = take(4)
        ds_refs = take(len(ds_accs))
        bk_ref, = take(1)
        hb_refs = take(n_arr)
        dx_ref, dnw_ref, drb_ref = take(3)
        recv_refs = take(n_arr)
        i = pl.program_id(0)

        def copies():
            send_sems, recv_sems = refs
            xi, yi, ci = _position()
            out = []
            for j, (px, py) in enumerate(_other_chips(xi, yi)):
                for t, (src, dst) in enumerate(zip(hb_refs, recv_refs)):
                    k = n_arr * j + t
                    out.append(pltpu.make_async_remote_copy(
                        src_ref=src.at[2 * px + py], dst_ref=dst.at[j], send_sem=send_sems.at[k],
                        recv_sem=recv_sems.at[k], device_id=(px, py, ci), device_id_type=MESH_ID))
            return out

        @pl.when(i == 0)
        def _():
            dnw_ref[...] = jnp.zeros_like(dnw_ref)
            for cp in (copies() if n_sent else []):
                cp.start()

        @pl.when(i == 1)
        def _():
            _bias_grad(ds_refs, bk_ref, drb_ref)

        dp = jnp.concatenate([r[...] for r in piece_refs], axis=1)
        dh = _dot(dp, w_ref[...])
        xv = x_ref[...]
        rstd = lax.rsqrt(jnp.mean(xv * xv, axis=-1, keepdims=True) + EPS)
        xh = xv * rstd
        dnw_ref[...] = dnw_ref[...] + jnp.sum(dh * xh, axis=0, keepdims=True)
        g = dh * nw_ref[...]
        dx_ref[...] = rstd * (g - xh * jnp.mean(g * xh, axis=-1, keepdims=True)) + dy_ref[...]

        if n_sent:
            @pl.when(i == n_step - 1)
            def _():
                cps = copies()
                for cp in cps:
                    cp.wait_recv()
                for cp in cps:
                    cp.wait_send()

    def blk(n):
        return pl.BlockSpec((tm, n), lambda i: (i, 0))

    vm = pl.BlockSpec(memory_space=pltpu.VMEM)
    in_specs = [blk(n) for n in PIECE_WIDTHS] + [pl.BlockSpec((D_IN_PAD, D_MODEL), lambda i: (0, 0)), blk(D_MODEL),
                                                blk(D_MODEL), pl.BlockSpec((1, D_MODEL), lambda i: (0, 0))]
    in_specs += [vm] * (len(ds_accs) + 1)
    out_specs = [blk(D_MODEL), pl.BlockSpec((8, D_MODEL), lambda i: (0, 0)), pl.BlockSpec(memory_space=pltpu.SMEM)]
    out_shape = [_sds((s, D_MODEL)), _sds((8, D_MODEL)), _sds((ATT_HEADS, N_BUCKETS))]
    scratch = []
    extra = ()
    if n_sent:
        extra = tuple(chip_sums)
        in_specs += [ANY] * n_arr
        out_specs += [ANY] * n_arr
        out_shape += [_sds((3,) + a.shape[1:], a.dtype) for a in extra]
        scratch = [pltpu.SemaphoreType.DMA((n_sent,)), pltpu.SemaphoreType.DMA((n_sent,))]
    return pl.pallas_call(
        body, grid=(n_step,), in_specs=in_specs, out_specs=out_specs, out_shape=out_shape, scratch_shapes=scratch,
        compiler_params=_cp(("arbitrary",)), name="in_bwd_dx")(*pieces, w_bf, x, dy, norm_w, *ds_accs, buckets, *extra)


def _in_bwd_dw(pieces, x, norm_w):
    s = x.shape[0]
    tm = 512
    n_piece = len(PIECE_WIDTHS)

    def body(*refs):
        piece_refs = refs[:n_piece]
        x_ref, nw_ref, dw_ref = refs[n_piece:]
        i = pl.program_id(0)

        @pl.when(i == 0)
        def _():
            dw_ref[...] = jnp.zeros_like(dw_ref)

        xv = x_ref[...]
        rstd = lax.rsqrt(jnp.mean(xv * xv, axis=-1, keepdims=True) + EPS)
        h = (xv * rstd * nw_ref[...]).astype(BF16)
        at = 0
        for r, width in zip(piece_refs, PIECE_WIDTHS):
            dw_ref[at:at + width, :] = dw_ref[at:at + width, :] + _dot_tn(r[...], h)
            at += width

    return pl.pallas_call(
        body, grid=(s // tm,),
        in_specs=[pl.BlockSpec((tm, n), lambda i: (i, 0)) for n in PIECE_WIDTHS]
        + [pl.BlockSpec((tm, D_MODEL), lambda i: (i, 0)), pl.BlockSpec((1, D_MODEL), lambda i: (0, 0))],
        out_specs=pl.BlockSpec((D_IN_PAD, D_MODEL), lambda i: (0, 0)),
        out_shape=_sds((D_IN_PAD, D_MODEL)),
        compiler_params=_cp(("arbitrary",)), name="in_bwd_dw")(*pieces, x, norm_w)


def _flat(a):
    return a.reshape(-1, a.shape[-1])


def _as_pattern(a, r):
    return a if r == 1 else a.reshape(r, a.shape[0] // r, a.shape[1])


D_SHARD = D_IN // N_CHIPS
BA_START = 4 * D_DN
BA_PACKED = 4096
S1_HEAD = BA_START - D_SHARD
S1_BA = 2 * D_SHARD - BA_START


def _pack_w(g):
    n = g.shape[2]

    def body(g_ref, o_ref):
        g32 = g_ref.bitcast(jnp.uint32)
        o32 = o_ref.bitcast(jnp.uint32)
        full, head, ba1 = D_SHARD // 2, S1_HEAD // 2, S1_BA // 2
        ba2 = DN_HEADS - ba1
        pieces = [(0, 0, full), (1, 0, head), (2, ba2, full), (3, 0, full), (1, head, full), (2, 0, ba2)]
        at = 0
        for chip, lo, hi in pieces:
            o32[at:at + hi - lo, :] = g32[chip, lo:hi, :]
            at += hi - lo
        o32[at:D_IN_PAD // 2, :] = jnp.zeros((D_IN_PAD // 2 - at, n), jnp.uint32)

    vm = pl.BlockSpec(memory_space=pltpu.VMEM)
    return pl.pallas_call(body, in_specs=[vm], out_specs=vm, out_shape=_sds((D_IN_PAD, n), g.dtype),
                          compiler_params=_cp(), name="pack_w")(g)


def _unpack_w(p, rows):
    n = p.shape[1]
    tn = 256
    mid = BA_PACKED + S1_BA
    pieces = [(0, 0, (0, D_SHARD)), (1, 0, (D_SHARD, BA_START)), (1, S1_HEAD, (BA_PACKED, mid)),
              (2, 0, (mid, BA_PACKED + 2 * DN_HEADS)), (2, 2 * DN_HEADS - S1_BA, (BA_START, BA_START + S1_HEAD)),
              (3, 0, (BA_START + S1_HEAD, BA_PACKED))]

    def body(p_ref, o_ref):
        for chip, at, (lo, hi) in pieces:
            o_ref[chip, at:at + hi - lo, :] = p_ref[lo:hi, :]
        for chip in range(N_CHIPS):
            o_ref[chip, D_SHARD:rows, :] = jnp.zeros((rows - D_SHARD, tn), p.dtype)

    return pl.pallas_call(
        body, grid=(n // tn,),
        in_specs=[pl.BlockSpec((D_IN_PAD, tn), lambda j: (0, j))],
        out_specs=pl.BlockSpec((N_CHIPS, rows, tn), lambda j: (0, 0, j)),
        out_shape=_sds((N_CHIPS, rows, n), p.dtype),
        compiler_params=_cp(("parallel",)), name="unpack_w")(p)


def _lane_row(vec, offset):
    return jnp.pad(vec.reshape(1, -1), ((0, 0), (offset, 128 - offset - vec.shape[0])))


def _local_step(x, tgt, norm_w, w_bf, conv_w, a_log, dt_bias, dn_norm_w, q_norm_w, k_norm_w, bias, w_out_bf):
    s = x.shape[0]
    nc = s // CHUNK
    conv_w8 = jnp.pad(conv_w, ((0, 8 - CONV_W), (0, 0)))
    alog_l = _lane_row(a_log.reshape(-1), DN_HEADS)
    dtb_l = _lane_row(dt_bias.reshape(-1), DN_HEADS)
    dnw_t = jnp.tile(dn_norm_w.reshape(1, DN_HD), (1, DN_HEADS))
    qw_t = jnp.tile(q_norm_w.reshape(1, ATT_HD), (1, ATT_HEADS))
    kw_t = jnp.tile(k_norm_w.reshape(1, ATT_HD), (1, ATT_HEADS))
    bd128 = _compact_mat(DN_HD)
    bd64 = _compact_mat(ATT_HD)
    tri = _tri_incl()
    sel = _lane_select()
    buckets = _bucket_tables()

    proj, qn, kn, v_dn, bg = _in_proj(x, norm_w, w_bf, conv_w8, alog_l, dtb_l)
    bgt = bg[:, 0:8].reshape(nc, CHUNK, 8).transpose(0, 2, 1)
    u, w, qd, kt, attn, t_inv, gl = _dn_prep(qn, kn, v_dn, bg, bgt, tri, sel)
    o_dn, vn, sp = _dn_scan(u, w, qd, kt, attn, gl)
    q1, k1, v1, q4, k4, v4, q16, k16, v16 = _att_pre(proj, qw_t, kw_t, bd64)
    rs = [r for _, r in PATTERNS]
    qkv = [(q1, k1, v1), (_flat(q4), _flat(k4), _flat(v4)), (_flat(q16), _flat(k16), _flat(v16))]
    o_pats, lse_pats = [], []
    for p, r in enumerate(rs):
        o_p, lse_p = _att_fwd(*qkv[p], bias, p, r, "att_fwd_r%d" % r)
        o_pats.append(_as_pattern(o_p, r))
        lse_pats.append(_as_pattern(lse_p, r))
    mixed, o_att, l1, l4, l16 = _post_fwd(o_dn, proj, o_pats, lse_pats, dnw_t, bd128)

    (dy, loss_blk, d_w_out, do_dn, dz, dgate, do1, do4, do16, dl1, dl4, dl16,
     d_dnw) = _out_post(x, mixed, w_out_bf, tgt, o_dn, proj, o_att, dnw_t, bd128)
    side = [(do1, l1, dl1), (_flat(do4), _flat(l4), _flat(dl4)), (_flat(do16), _flat(l16), _flat(dl16))]
    dq_pats, dk_pats, dv_pats, ds_accs = [], [], [], []
    for p, r in enumerate(rs):
        dq_p, dk_p, dv_p, ds_p = _att_bwd(*qkv[p], *side[p], bias, p, r, "att_bwd_r%d" % r)
        dq_pats.append(_as_pattern(dq_p, r))
        dk_pats.append(_as_pattern(dk_p, r))
        dv_pats.append(_as_pattern(dv_p, r))
        ds_accs.append(ds_p)
    dq_att, dk_att, dv_att, d_qw, d_kw = _att_pre_bwd(dq_pats, dk_pats, dv_pats, proj, qw_t, kw_t, bd64)
    du, dqd, dkt, dw, dattn, dgl = _dn_scan_bwd(do_dn, sp, qd, kt, w, vn, attn, gl)
    dqn, dkn, dv_dn, dbg = _dn_prep_bwd(qn, kn, v_dn, bg, bgt, tri, sel, t_inv, attn, u, w, du, dw, dqd, dkt, dattn,
                                        dgl)
    d_qkv_dn, dba, dsm, d_conv8 = _dn_pre_bwd(dqn, dkn, dv_dn, dbg, proj, conv_w8, alog_l, dtb_l)
    pieces = (d_qkv_dn, dz, dq_att, dk_att, dv_att, dgate, dba)
    d_w_in_t = _in_bwd_dw(pieces, x, norm_w)
    last = functools.partial(_in_bwd_dx, pieces, w_bf, x, dy, norm_w, ds_accs, buckets)

    grads = dict(
        w_in_t=d_w_in_t,
        conv_w=d_conv8[0:CONV_W, :],
        a_log=dsm[0:1, DN_HEADS:2 * DN_HEADS],
        dt_bias=dsm[1:2, DN_HEADS:2 * DN_HEADS],
        dn_norm_w=d_dnw[0:1, :],
        q_norm_w=d_qw[0:1, :].reshape(ATT_HEADS, ATT_HD),
        k_norm_w=d_kw[0:1, :].reshape(ATT_HEADS, ATT_HD),
        w_out=d_w_out,
    )
    return loss_blk[0, 0], last, grads


MESH_ID = pl.DeviceIdType.MESH
ANY = pl.BlockSpec(memory_space=pl.ANY)


def _position():
    return lax.axis_index("x"), lax.axis_index("y"), lax.axis_index("c")


def _other_chips(x, y):
    return [(1 - x, y), (x, 1 - y), (1 - x, 1 - y)]


SHARD_PAD = 1040
WIN = 528


def _row_split(n):
    return (n // 2) // 128 * 128


def _part(ref, cc):
    n = ref.shape[0]
    sp = _row_split(n)
    return ref.at[pl.ds(0, sp)] if cc == 0 else ref.at[pl.ds(sp, n - sp)]


def _halves(ref):
    n = ref.shape[0]
    sp = (n // 2) // 16 * 16
    return ref.at[pl.ds(0, sp)], ref.at[pl.ds(sp, n - sp)]


def _gather_weights(wt_s, w_out_s, conv_s, rel_bias, buckets):
    def body(a_ref, b_ref, c_ref, rb_ref, bk_ref, ga_ref, gb_ref, gc_ref, bias_ref, send_sems, recv_sems, loc_sems,
             a_vmem, b_vmem):
        x, y, c = _position()
        me = 2 * x + y
        sib = (x, y, 1 - c)
        big = ((a_ref, ga_ref), (b_ref, gb_ref))
        stage_in = [pltpu.make_async_copy(a_ref, a_vmem, loc_sems.at[0]),
                    pltpu.make_async_copy(b_ref, b_vmem, loc_sems.at[1])]
        local = [pltpu.make_async_copy(a_vmem, ga_ref.at[me], loc_sems.at[0]),
                 pltpu.make_async_copy(b_vmem, gb_ref.at[me], loc_sems.at[1]),
                 pltpu.make_async_copy(c_ref, gc_ref.at[me], loc_sems.at[2])]
        for cp in stage_in:
            cp.start()
        local[2].start()
        for cp in stage_in:
            cp.wait()
        for cp in local[:2]:
            cp.start()
        others = _other_chips(x, y)

        def exchange(cc):
            nbr = others[:2]
            dg = others[2]
            pending = []

            def copy(k, src, dst, to):
                cp = pltpu.make_async_remote_copy(src_ref=src, dst_ref=dst, send_sem=send_sems.at[k],
                                                  recv_sem=recv_sems.at[k], device_id=to, device_id_type=MESH_ID)
                pending.append(cp)
                cp.start()

            def landed(k, dst):
                pltpu.make_async_remote_copy(src_ref=dst, dst_ref=dst, send_sem=send_sems.at[k],
                                             recv_sem=recv_sems.at[k], device_id=sib, device_id_type=MESH_ID).wait_recv()

            def slot(dst, chip, c_part):
                return _part(dst.at[2 * chip[0] + chip[1]], c_part)

            for i, n in enumerate(nbr):
                for t, (src, dst) in enumerate(big):
                    copy(2 * i + t, _part(src, cc), _part(dst.at[me], cc), (*n, c))
            for j, chip in enumerate(others):
                copy(4 + j, c_ref, gc_ref.at[me], (*chip, c))
            _bias_table(rb_ref, bk_ref, bias_ref)
            for i, n in enumerate(nbr):
                for t, (_, dst) in enumerate(big):
                    got = slot(dst, n, cc)
                    landed(2 * i + t, got)
                    half = _halves(got)[1 - i]
                    copy(7 + 2 * (1 - i) + t, half, half, (*nbr[1 - i], c))
                    copy(11 + 2 * i + t, got, got, sib)
            for i, n in enumerate(nbr):
                for t, (_, dst) in enumerate(big):
                    half = _halves(slot(dst, dg, cc))[i]
                    landed(7 + 2 * i + t, half)
                    copy(15 + 2 * i + t, half, half, sib)
            for j, chip in enumerate(others):
                landed(4 + j, gc_ref.at[2 * chip[0] + chip[1]])
            for i, n in enumerate(nbr):
                for t, (_, dst) in enumerate(big):
                    landed(11 + 2 * i + t, slot(dst, n, 1 - cc))
                    landed(15 + 2 * i + t, _halves(slot(dst, dg, 1 - cc))[i])
            for cp in pending:
                cp.wait_send()

        for cc in (0, 1):
            pl.when(c == cc)(functools.partial(exchange, cc))
        for cp in local:
            cp.wait()

    srcs = (wt_s, w_out_s, conv_s)
    n_sem = 19
    return pl.pallas_call(
        body, in_specs=[ANY] * 3 + [pl.BlockSpec(memory_space=pltpu.SMEM), pl.BlockSpec(memory_space=pltpu.VMEM)],
        out_specs=[ANY] * 3 + [pl.BlockSpec(memory_space=pltpu.VMEM)],
        out_shape=[_sds((N_CHIPS,) + a.shape, a.dtype) for a in srcs]
        + [_sds((len(PATTERNS), ATT_HEADS, BLK, 2 * BLK))],
        scratch_shapes=[pltpu.SemaphoreType.DMA((n_sem,)), pltpu.SemaphoreType.DMA((n_sem,)),
                        pltpu.SemaphoreType.DMA((3,)), pltpu.VMEM(wt_s.shape, wt_s.dtype),
                        pltpu.VMEM(w_out_s.shape, w_out_s.dtype)],
        compiler_params=_cp(), name="gather_weights")(*srcs, rel_bias, buckets)


N_DEV = 8
PEER_FLIPS = [(dx, dy, dc) for dx in (0, 1) for dy in (0, 1) for dc in (0, 1)][1:]


def _small_copies(s_ref, rs_ref, send_sems, recv_sems):
    x, y, c = _position()
    dev = 4 * x + 2 * y + c
    sends, recvs = [], []
    for f, (dx, dy, dc) in enumerate(PEER_FLIPS):
        peer = (x ^ dx, y ^ dy, c ^ dc)
        sends.append(pltpu.make_async_remote_copy(
            src_ref=s_ref, dst_ref=rs_ref.at[dev], send_sem=send_sems.at[f], recv_sem=recv_sems.at[f],
            device_id=peer, device_id_type=MESH_ID))
        recvs.append(pltpu.make_async_remote_copy(
            src_ref=s_ref, dst_ref=rs_ref.at[4 * peer[0] + 2 * peer[1] + peer[2]], send_sem=send_sems.at[f],
            recv_sem=recv_sems.at[f], device_id=peer, device_id_type=MESH_ID))
    return sends, recvs


def _fill_parts(h_in, r_in, h_out, r_out, small, rows_in, rows_out):
    def body(ha_ref, ra_ref, hb_ref, rb_ref, s_ref, fa_ref, fb_ref, rs_ref, send_sems, recv_sems, loc_sems,
             small_send, small_recv, a_vmem, b_vmem, ra_vmem, rb_vmem):
        x, y, c = _position()
        sib = (x, y, 1 - c)
        chip = 2 * x + y
        pairs = ((a_vmem, fa_ref), (b_vmem, fb_ref))
        own_small = pltpu.make_async_copy(s_ref, rs_ref.at[4 * x + 2 * y + c], loc_sems.at[2])
        own_small.start()
        small_sends, small_recvs = _small_copies(s_ref, rs_ref, small_send, small_recv)
        for cp in small_sends:
            cp.start()
        stage_in = [pltpu.make_async_copy(ha_ref.at[chip], a_vmem, loc_sems.at[0]),
                    pltpu.make_async_copy(hb_ref.at[chip], b_vmem, loc_sems.at[1]),
                    pltpu.make_async_copy(ra_ref, ra_vmem, loc_sems.at[3]),
                    pltpu.make_async_copy(rb_ref, rb_vmem, loc_sems.at[4])]
        for cp in stage_in:
            cp.start()
        for cp in stage_in:
            cp.wait()
        for tot, recv in ((a_vmem, ra_vmem), (b_vmem, rb_vmem)):
            acc = tot[...]
            for j in range(3):
                acc = acc + recv[j].astype(F32)
            tot[...] = acc

        def fill(cc):
            mine = [_part(dst, cc) for _, dst in pairs]
            srcs = [src.at[pl.ds(0, m.shape[0])] for src, m in zip((a_vmem, b_vmem), mine)]
            local = [pltpu.make_async_copy(s, m, loc_sems.at[t]) for t, (s, m) in enumerate(zip(srcs, mine))]
            sends = [pltpu.make_async_remote_copy(src_ref=s, dst_ref=m, send_sem=send_sems.at[t],
                                                  recv_sem=recv_sems.at[t], device_id=sib, device_id_type=MESH_ID)
                     for t, (s, m) in enumerate(zip(srcs, mine))]
            for cp in local + sends:
                cp.start()
            for t, (_, dst) in enumerate(pairs):
                theirs = _part(dst, 1 - cc)
                pltpu.make_async_remote_copy(src_ref=theirs, dst_ref=theirs, send_sem=send_sems.at[t],
                                             recv_sem=recv_sems.at[t], device_id=sib, device_id_type=MESH_ID).wait_recv()
            for cp in sends:
                cp.wait_send()
            for cp in local:
                cp.wait()

        for cc in (0, 1):
            pl.when(c == cc)(functools.partial(fill, cc))
        for cp in small_recvs:
            cp.wait_recv()
        for cp in small_sends:
            cp.wait_send()
        own_small.wait()

    n_peer = len(PEER_FLIPS)
    return pl.pallas_call(
        body, in_specs=[ANY] * 5, out_specs=[ANY] * 3,
        out_shape=[_sds((rows_in, D_MODEL)), _sds((rows_out, D_MODEL)), _sds((N_DEV,) + small.shape, small.dtype)],
        scratch_shapes=[pltpu.SemaphoreType.DMA((2,)), pltpu.SemaphoreType.DMA((2,)), pltpu.SemaphoreType.DMA((5,)),
                        pltpu.SemaphoreType.DMA((n_peer,)), pltpu.SemaphoreType.DMA((n_peer,)),
                        pltpu.VMEM(h_in.shape[1:], F32), pltpu.VMEM(h_out.shape[1:], F32),
                        pltpu.VMEM(r_in.shape, r_in.dtype), pltpu.VMEM(r_out.shape, r_out.dtype)],
        compiler_params=_cp(), name="fill_parts")(h_in, r_in, h_out, r_out, small)


def _chip_sums(u_in, u_out, win_in, win_out):
    wins = (win_in, win_out)

    def body(a_ref, b_ref, ha_ref, hba_ref, hb_ref, hbb_ref, send_sems, recv_sems, loc_sems, out_sems,
             mine_a, theirs_a, sum_a, sumb_a, mine_b, theirs_b, sum_b, sumb_b):
        x, y, c = _position()
        sib = (x, y, 1 - c)
        groups = ((a_ref, mine_a, theirs_a, sum_a, sumb_a, ha_ref, hba_ref, win_in),
                  (b_ref, mine_b, theirs_b, sum_b, sumb_b, hb_ref, hbb_ref, win_out))
        loads, sends = [], []
        for t, (src, mine, theirs, _, _, _, _, win) in enumerate(groups):
            split = _row_split(src.shape[1])
            for k in range(N_CHIPS):
                n = N_CHIPS * t + k
                loads.append(pltpu.make_async_copy(
                    src.at[k, pl.ds(pl.multiple_of(c * split, split), win), :], mine.at[k], loc_sems.at[n]))
                sends.append(pltpu.make_async_remote_copy(
                    src_ref=src.at[k, pl.ds(pl.multiple_of((1 - c) * split, split), win), :], dst_ref=theirs.at[k],
                    send_sem=send_sems.at[n], recv_sem=recv_sems.at[n], device_id=sib, device_id_type=MESH_ID))
        for cp in sends + loads:
            cp.start()
        stores = []
        for t, (_, mine, theirs, tot, totb, h_out, hb_out, _) in enumerate(groups):
            for k in range(N_CHIPS):
                n = N_CHIPS * t + k
                loads[n].wait()
                sends[n].wait_recv()
                val = mine[k] + theirs[k]
                tot[k] = val
                totb[k] = val.astype(BF16)
                stores += [pltpu.make_async_copy(tot.at[k], h_out.at[k], out_sems.at[2 * n]),
                           pltpu.make_async_copy(totb.at[k], hb_out.at[k], out_sems.at[2 * n + 1])]
                stores[-2].start()
                stores[-1].start()
        for cp in sends:
            cp.wait_send()
        for cp in stores:
            cp.wait()

    shapes = [(N_CHIPS, w, D_MODEL) for w in wins]
    n_cp = 2 * N_CHIPS
    vmem = []
    for shp in shapes:
        vmem += [pltpu.VMEM(shp, F32), pltpu.VMEM(shp, F32), pltpu.VMEM(shp, F32), pltpu.VMEM(shp, BF16)]
    return pl.pallas_call(
        body, in_specs=[ANY] * 2, out_specs=[ANY] * 4,
        out_shape=[_sds(shapes[0]), _sds(shapes[0], BF16), _sds(shapes[1]), _sds(shapes[1], BF16)],
        scratch_shapes=[pltpu.SemaphoreType.DMA((n_cp,)), pltpu.SemaphoreType.DMA((n_cp,)),
                        pltpu.SemaphoreType.DMA((n_cp,)), pltpu.SemaphoreType.DMA((2 * n_cp,))] + vmem,
        compiler_params=_cp(), name="chip_sums")(u_in, u_out)


SMALL_LAYOUT = (("norm_w", 1024), ("conv_w", 6144), ("a_log", 128), ("dt_bias", 128), ("dn_norm_w", 128),
                ("q_norm_w", 512), ("k_norm_w", 512), ("rel_bias", 256), ("loss", 128))
SMALL_TOTAL = sum(n for _, n in SMALL_LAYOUT)


def _small_offset(name):
    off = 0
    for n, size in SMALL_LAYOUT:
        if n == name:
            return off
        off += size
    raise KeyError(name)


def _pack_small(grads):
    parts = []
    for name, size in SMALL_LAYOUT:
        flat = grads[name].reshape(1, -1)
        parts.append(jnp.pad(flat, ((0, 0), (0, size - flat.shape[1]))))
    return jnp.concatenate(parts, axis=1)


def _sum_small(rows):
    n_dev = rows.shape[0]
    q_off = _small_offset("q_norm_w")
    k_off = _small_offset("k_norm_w")

    def body(r_ref, tot_ref, qk_ref):
        tot = r_ref[0:1, :]
        for d in range(1, n_dev):
            tot = tot + r_ref[d:d + 1, :]
        tot_ref[...] = tot
        for row, off in ((0, q_off), (1, k_off)):
            s4 = tot[:, off:off + 128] + tot[:, off + 128:off + 256] + tot[:, off + 256:off + 384] \
                + tot[:, off + 384:off + 512]
            qk_ref[row:row + 1, :] = s4 + pltpu.roll(s4, ATT_HD, 1)

    return pl.pallas_call(
        body, in_specs=[pl.BlockSpec(memory_space=pltpu.VMEM)],
        out_specs=[pl.BlockSpec(memory_space=pltpu.VMEM)] * 2,
        out_shape=[_sds((1, SMALL_TOTAL)), _sds((2, 128))],
        compiler_params=_cp(), name="sum_small")(rows)


def _adamw_math(w, g, m, v):
    m = ADAM_B1 * m + (1.0 - ADAM_B1) * g
    v = ADAM_B2 * v + (1.0 - ADAM_B2) * (g * g)
    m_hat = m / (1.0 - ADAM_B1 ** ADAM_STEP)
    v_hat = v / (1.0 - ADAM_B2 ** ADAM_STEP)
    delta = -ADAM_LR * (m_hat / (jnp.sqrt(v_hat) + ADAM_EPS) + ADAM_WD * w)
    return delta, m, v


def _adamw_big(g, w, m, v, name):
    rows, cols = w.shape
    tr = 128

    def body(g_ref, w_ref, m_ref, v_ref, go_ref, d_ref, nm_ref, nv_ref):
        g = g_ref[...]
        go_ref[...] = g
        d_ref[...], nm_ref[...], nv_ref[...] = _adamw_math(w_ref[...], g, m_ref[...], v_ref[...])

    blk = pl.BlockSpec((tr, cols), lambda i: (i, 0))
    return pl.pallas_call(
        body, grid=(pl.cdiv(rows, tr),), in_specs=[blk] * 4, out_specs=[blk] * 4,
        out_shape=[_sds((rows, cols))] * 4, compiler_params=_cp(("parallel",)), name=name)(g, w, m, v)


def _adamw_rows(g, w, m, v, name):
    rows, cols = w.shape
    tr = 128

    def body(g_ref, w_ref, m_ref, v_ref, go_ref, d_ref, nm_ref, nv_ref, s_g, s_d, s_m, s_v):
        g = g_ref[...]
        s_g[...] = g
        s_d[...], s_m[...], s_v[...] = _adamw_math(w_ref[...], g, m_ref[...], v_ref[...])
        for i in range(tr):
            for scr, out in ((s_g, go_ref), (s_d, d_ref), (s_m, nm_ref), (s_v, nv_ref)):
                out[i] = scr[i:i + 1, :]

    blk = pl.BlockSpec((tr, cols), lambda i: (i, 0))
    oblk = pl.BlockSpec((tr, 1, cols), lambda i: (i, 0, 0))
    return pl.pallas_call(
        body, grid=(pl.cdiv(rows, tr),), in_specs=[blk] * 4, out_specs=[oblk] * 4,
        out_shape=[_sds((rows, 1, cols))] * 4, scratch_shapes=[pltpu.VMEM((tr, cols), F32)] * 4,
        compiler_params=_cp(("parallel",)), name=name)(g, w, m, v)


def _adamw_small(w, g, m, v, name):
    def body(w_ref, g_ref, m_ref, v_ref, d_ref, nm_ref, nv_ref):
        d_ref[...], nm_ref[...], nv_ref[...] = _adamw_math(w_ref[...], g_ref[...], m_ref[...], v_ref[...])

    vm = pl.BlockSpec(memory_space=pltpu.VMEM)
    return pl.pallas_call(body, in_specs=[vm] * 4, out_specs=[vm] * 3, out_shape=[_sds(w.shape)] * 3,
                          compiler_params=_cp(), name=name)(w, g, m, v)


WEIGHTS = ("norm_w", "w_in", "conv_w", "a_log", "dt_bias", "dn_norm_w", "q_norm_w", "k_norm_w", "rel_bias", "w_out")


def kernel(x, norm_w, w_in, conv_w, a_log, dt_bias, dn_norm_w, q_norm_w, k_norm_w, rel_bias, w_out, loss_target, m_norm_w, m_w_in, m_conv_w, m_a_log, m_dt_bias, m_dn_norm_w, m_q_norm_w, m_k_norm_w, m_rel_bias, m_w_out, v_norm_w, v_w_in, v_conv_w, v_a_log, v_dt_bias, v_dn_norm_w, v_q_norm_w, v_k_norm_w, v_rel_bias, v_w_out):
    xi, yi, _ = _position()
    chip = 2 * xi + yi
    w_loc = dict(norm_w=norm_w, w_in=w_in[0].T, conv_w=conv_w[0], a_log=a_log, dt_bias=dt_bias, dn_norm_w=dn_norm_w,
                 q_norm_w=q_norm_w, k_norm_w=k_norm_w, rel_bias=rel_bias, w_out=w_out[0])
    m_loc = dict(norm_w=m_norm_w, w_in=m_w_in[0].T, conv_w=m_conv_w[0], a_log=m_a_log, dt_bias=m_dt_bias,
                 dn_norm_w=m_dn_norm_w, q_norm_w=m_q_norm_w, k_norm_w=m_k_norm_w, rel_bias=m_rel_bias,
                 w_out=m_w_out[0])
    v_loc = dict(norm_w=v_norm_w, w_in=v_w_in[0].T, conv_w=v_conv_w[0], a_log=v_a_log, dt_bias=v_dt_bias,
                 dn_norm_w=v_dn_norm_w, q_norm_w=v_q_norm_w, k_norm_w=v_k_norm_w, rel_bias=v_rel_bias,
                 w_out=v_w_out[0])

    wt_pad = jnp.pad(w_loc["w_in"].astype(BF16), ((0, SHARD_PAD - D_SHARD), (0, 0)))
    g_in, g_out, g_conv, bias = _gather_weights(wt_pad, w_loc["w_out"].astype(BF16), w_loc["conv_w"], rel_bias,
                                                _bucket_tables())
    wt_full = _pack_w(g_in)
    w_out_full = g_out.reshape(D_MODEL, D_MODEL)
    conv_full = g_conv.transpose(1, 0, 2).reshape(CONV_W, 3 * D_DN)

    loss_local, last_kernel, grads = _local_step(x[0], loss_target[0], norm_w, wt_full, conv_full, a_log, dt_bias,
                                                 dn_norm_w, q_norm_w, k_norm_w, bias, w_out_full)
    grads["loss"] = loss_local

    u_in = _unpack_w(grads["w_in_t"], SHARD_PAD)
    u_out = grads["w_out"].reshape(N_CHIPS, D_MODEL // N_CHIPS, D_MODEL)
    win_out = u_out.shape[1] // 2
    h_in, hb_in, h_out, hb_out = _chip_sums(u_in, u_out, WIN, win_out)
    grad_x, d_nw8, grads["rel_bias"], r_in, r_out = last_kernel(chip_sums=(hb_in, hb_out))
    grads["norm_w"] = d_nw8[0:1, :]
    small = _pack_small(grads)
    full_in, full_out, r_small = _fill_parts(h_in, r_in, h_out, r_out, small, SHARD_PAD, u_out.shape[1])
    tot_small, qk = _sum_small(r_small.reshape(8, SMALL_TOTAL))

    def small_grad(name, n):
        off = _small_offset(name)
        return tot_small[:, off:off + n]

    loss = small_grad("loss", 1).reshape(())
    conv_all = small_grad("conv_w", CONV_W * 3 * D_DN).reshape(CONV_W, 3 * D_DN)
    g_small = dict(
        norm_w=small_grad("norm_w", D_MODEL),
        conv_w=lax.dynamic_slice_in_dim(conv_all, chip * (3 * D_DN // N_CHIPS), 3 * D_DN // N_CHIPS, axis=1),
        a_log=small_grad("a_log", DN_HEADS),
        dt_bias=small_grad("dt_bias", DN_HEADS),
        dn_norm_w=small_grad("dn_norm_w", DN_HD),
        q_norm_w=qk[0:1, 0:ATT_HD],
        k_norm_w=qk[1:2, 0:ATT_HD],
        rel_bias=small_grad("rel_bias", ATT_HEADS * N_BUCKETS).reshape(ATT_HEADS, N_BUCKETS),
    )

    out_g, out_d, out_m, out_v = {}, {}, {}, {}
    out_g["w_in"], out_d["w_in"], out_m["w_in"], out_v["w_in"] = _adamw_rows(
        full_in, w_loc["w_in"], m_loc["w_in"], v_loc["w_in"], "adamw_w_in")
    out_g["w_out"], out_d["w_out"], out_m["w_out"], out_v["w_out"] = _adamw_big(
        full_out, w_loc["w_out"], m_loc["w_out"], v_loc["w_out"], "adamw_w_out")
    for name in g_small:
        out_g[name] = g_small[name]
        out_d[name], out_m[name], out_v[name] = _adamw_small(w_loc[name], g_small[name], m_loc[name], v_loc[name],
                                                             "adamw_" + name)
    for d in (out_g, out_d, out_m, out_v):
        d["w_in"] = d["w_in"].transpose(1, 2, 0)
        for name in ("conv_w", "w_out"):
            d[name] = d[name][None]
    return (loss, grad_x[None], *[out_g[n] for n in WEIGHTS], *[out_d[n] for n in WEIGHTS],
            *[out_m[n] for n in WEIGHTS], *[out_v[n] for n in WEIGHTS])
```

```python
import functools
import math

import numpy as np
import jax
import jax.numpy as jnp
from jax import lax
from jax.experimental import pallas as pl
from jax.experimental.pallas import tpu as pltpu

F32 = jnp.float32
BF16 = jnp.bfloat16
HI = lax.Precision.HIGHEST

D_MODEL = 1024
D_DN = 512
DN_HEADS = 4
DN_HD = 128
CONV_W = 4
CHUNK = 64
D_ATT = 512
ATT_HEADS = 8
ATT_HD = 64
PATTERNS = ((128, 1), (512, 4), (2048, 16))
N_BUCKETS = 32
MAX_DISTANCE = 2048
D_IN = 4 * D_DN + 2 * DN_HEADS + 4 * D_ATT
D_IN_PAD = 4224
EPS = 1e-6
BLK = 128
NEG = -1e30
N_CHIPS = 4

ADAM_LR = 0.001
ADAM_B1 = 0.9
ADAM_B2 = 0.999
ADAM_EPS = 1e-08
ADAM_WD = 0.01
ADAM_STEP = 10

VMEM_LIMIT = 56 * 1024 * 1024

COL_Z = 3
COL_ATT_Q = 4
COL_ATT_K = 5
COL_ATT_V = 6
COL_GATE = 7
COL_BA_128 = 32


def _cp(sem=None):
    if sem is None:
        return pltpu.CompilerParams(vmem_limit_bytes=VMEM_LIMIT)
    return pltpu.CompilerParams(dimension_semantics=sem, vmem_limit_bytes=VMEM_LIMIT)


def _sds(shape, dtype=F32):
    return jax.ShapeDtypeStruct(shape, dtype)


def _mm(a, b):
    return jnp.dot(a.astype(BF16), b.astype(BF16), preferred_element_type=F32)


def _mm_nt(a, b):
    return lax.dot_general(a.astype(BF16), b.astype(BF16), (((1,), (1,)), ((), ())),
                           preferred_element_type=F32)


def _mm_tn(a, b):
    return lax.dot_general(a.astype(BF16), b.astype(BF16), (((0,), (0,)), ((), ())),
                           preferred_element_type=F32)


def _mmx(a, b):
    return jnp.dot(a, b, precision=HI, preferred_element_type=F32)


def _mmx_nt(a, b):
    return lax.dot_general(a, b, (((1,), (1,)), ((), ())), precision=HI, preferred_element_type=F32)


def _mmx_tn(a, b):
    return lax.dot_general(a, b, (((0,), (0,)), ((), ())), precision=HI, preferred_element_type=F32)


def _dot(a, b):
    return jnp.dot(a, b, preferred_element_type=F32)


def _dot_nt(a, b):
    return lax.dot_general(a, b, (((1,), (1,)), ((), ())), preferred_element_type=F32)


def _dot_tn(a, b):
    return lax.dot_general(a, b, (((0,), (0,)), ((), ())), preferred_element_type=F32)


def _split(a):
    hi = a.astype(BF16)
    return hi, (a - hi.astype(F32)).astype(BF16)


def _mm3(a_s, b_s):
    return _dot(a_s[0], b_s[0]) + _dot(a_s[0], b_s[1]) + _dot(a_s[1], b_s[0])


def _mm3_tn(a_s, b_s):
    return _dot_tn(a_s[0], b_s[0]) + _dot_tn(a_s[0], b_s[1]) + _dot_tn(a_s[1], b_s[0])


def _interleave(gens):
    live = list(gens)
    while live:
        nxt = []
        for g in live:
            try:
                next(g)
                nxt.append(g)
            except StopIteration:
                pass
        live = nxt


def _segsum(x, bd):
    hi = x.astype(BF16)
    r1 = x - hi.astype(F32)
    mid = r1.astype(BF16)
    lo = (r1 - mid.astype(F32)).astype(BF16)
    return (jnp.dot(hi, bd, preferred_element_type=F32) + jnp.dot(mid, bd, preferred_element_type=F32)
            + jnp.dot(lo, bd, preferred_element_type=F32))


def _compact_mat(seg, n=512):
    slot = 128 * seg // n
    src = np.arange(n)[:, None]
    dst = np.arange(128)[None, :]
    return jnp.asarray((src // seg == dst // slot).astype(np.float32), dtype=BF16)


def _seg_mean(x, cm, seg):
    hi, lo = _split(x)
    return (_dot(hi, cm) + _dot(lo, cm)) * (1.0 / seg)


def _seg_expand(c, cm, seg):
    hi, lo = _split(c)
    return (_dot_nt(hi, cm) + _dot_nt(lo, cm)) * (cm.shape[0] / (128.0 * seg))


def _seg_rstd(x, cm, seg):
    return _seg_expand(lax.rsqrt(_seg_mean(x * x, cm, seg) + EPS), cm, seg)


def _sigmoid(x):
    return 1.0 / (1.0 + jnp.exp(-x))


def _silu_grad(x, s):
    return s * (1.0 + x * (1.0 - s))


def _tri_incl():
    i = np.arange(CHUNK)
    return jnp.asarray((i[:, None] >= i[None, :]).astype(np.float32))


def _t5_bucket(dist):
    max_exact = N_BUCKETS // 2
    d = np.maximum(dist, 1).astype(np.float64)
    large = max_exact + (np.log(d / max_exact) / math.log(MAX_DISTANCE / max_exact)
                         * (N_BUCKETS - max_exact)).astype(np.int32)
    large = np.minimum(large, N_BUCKETS - 1)
    return np.where(dist < max_exact, dist, large).astype(np.int32)


def _bucket_tables():
    qi = np.arange(BLK)[:, None]
    kj = np.arange(2 * BLK)[None, :]
    step = qi - kj + BLK
    return jnp.asarray(np.stack([_t5_bucket(np.clip(step, 0, None) * r) for _, r in PATTERNS]))


def _in_proj(x, norm_w, wt_bf, conv_w8, alog_l, dtb_l):
    s = x.shape[0]
    tm = 512

    def body(x_ref, nw_ref, w_ref, cw_ref, al_ref, dt_ref, o_ref, q_ref, k_ref, v_ref, bg_ref, halo_ref):
        i = pl.program_id(0)

        @pl.when(i == 0)
        def _():
            halo_ref[...] = jnp.zeros_like(halo_ref)

        xv = x_ref[...]
        rstd = lax.rsqrt(jnp.mean(xv * xv, axis=-1, keepdims=True) + EPS)
        h = (xv * rstd * nw_ref[...]).astype(BF16)
        o_ref[...] = _dot_nt(h, w_ref[...])
        for c in range(12):
            lanes = slice(128 * c, 128 * c + 128)
            y = _conv_group(o_ref[:, lanes], halo_ref[:, lanes], cw_ref, c)
            sv = y * _sigmoid(y)
            if c < 8:
                rs = lax.rsqrt(jnp.sum(sv * sv, axis=1, keepdims=True) + EPS)
                n = sv * rs
                if c < 4:
                    q_ref[:, lanes] = n * (DN_HD ** -0.5)
                else:
                    k_ref[:, slice(128 * (c - 4), 128 * (c - 3))] = n
            else:
                v_ref[:, slice(128 * (c - 8), 128 * (c - 7))] = sv
        bg_ref[...] = _beta_g(o_ref[:, 128 * COL_BA_128:128 * COL_BA_128 + 128], al_ref[...], dt_ref[...])[0]
        halo_ref[...] = o_ref[tm - 8:tm, 0:3 * D_DN]

    rows = pl.BlockSpec((tm, 512), lambda i: (i, 0))
    return pl.pallas_call(
        body, grid=(s // tm,),
        in_specs=[pl.BlockSpec((tm, D_MODEL), lambda i: (i, 0)),
                  pl.BlockSpec((1, D_MODEL), lambda i: (0, 0)),
                  pl.BlockSpec((D_IN_PAD, D_MODEL), lambda i: (0, 0)),
                  pl.BlockSpec((8, 3 * D_DN), lambda i: (0, 0)),
                  pl.BlockSpec((1, 128), lambda i: (0, 0)), pl.BlockSpec((1, 128), lambda i: (0, 0))],
        out_specs=[pl.BlockSpec((tm, D_IN_PAD), lambda i: (i, 0)), rows, rows, rows,
                   pl.BlockSpec((tm, 128), lambda i: (i, 0))],
        out_shape=[_sds((s, D_IN_PAD))] + [_sds((s, 512))] * 3 + [_sds((s, 128))],
        scratch_shapes=[pltpu.VMEM((8, 3 * D_DN), F32)],
        compiler_params=_cp(("arbitrary",)), name="in_proj")(x, norm_w, wt_bf, conv_w8, alog_l, dtb_l)


def _conv_group(cur, halo, w_ref, c):
    rows = cur.shape[0]
    lanes = slice(128 * c, 128 * c + 128)
    xcat = jnp.concatenate([halo, cur], axis=0)
    y = cur * w_ref[CONV_W - 1:CONV_W, lanes]
    for k in range(1, CONV_W):
        sh = pltpu.roll(xcat, k, 0)[8:8 + rows]
        y = y + sh * w_ref[CONV_W - 1 - k:CONV_W - k, lanes]
    return y


def _beta_g(ba, alog_l, dtb_l):
    lane = lax.broadcasted_iota(jnp.int32, ba.shape, 1)
    sig_b = _sigmoid(ba)
    t = ba + dtb_l
    softplus = jnp.maximum(t, 0.0) + jnp.log(1.0 + jnp.exp(-jnp.abs(t)))
    nega = -jnp.exp(alog_l)
    g = nega * softplus
    out = jnp.where(lane < DN_HEADS, sig_b, jnp.where(lane < 2 * DN_HEADS, g, 0.0))
    return out, lane, sig_b, t, nega, g


CPS = 8
CPS_SCAN = 16
SPLIT_ITERS = 2


def _split3(a):
    hi = a.astype(BF16)
    r1 = a - hi.astype(F32)
    mid = r1.astype(BF16)
    return hi, mid, (r1 - mid.astype(F32)).astype(BF16)


def _lane_select():
    r = np.arange(128)
    return jnp.asarray((r[None, :, None] == np.arange(8)[:, None, None]) * np.ones((1, 1, 128)), dtype=BF16)


def _lane_bcast(a3, sel):
    return _dot(a3[0], sel) + _dot(a3[1], sel) + _dot(a3[2], sel)


def _rowsum_b(z, ones_b):
    hi, lo = _split(z)
    return _dot(hi, ones_b) + _dot(lo, ones_b)


def _chunk_cumsum(bg, bgt, tri):
    return _split3(bg), _split3(_mmx(tri, bg)), _mmx_nt(bgt, tri)


def _chunk_common(bg3, gc3, gc_row, h, sel_ref):
    gcc = _lane_bcast(gc3, sel_ref[DN_HEADS + h])
    beta = _dot(bg3[0], sel_ref[h]) + _dot(bg3[1], sel_ref[h])
    gcr = gc_row[DN_HEADS + h:DN_HEADS + h + 1, :]
    ii = lax.broadcasted_iota(jnp.int32, (CHUNK, CHUNK), 0)
    jj = lax.broadcasted_iota(jnp.int32, (CHUNK, CHUNK), 1)
    incl = ii >= jj
    strict = ii > jj
    decay = jnp.exp(jnp.where(incl, gcc[:, 0:CHUNK] - gcr, NEG))
    gl = gcc[CHUNK - 1:CHUNK, :]
    return gcc, beta, incl, strict, decay, gl


def _dn_prep(qn, kn, v, bg, bgt, tri, sel):
    s = qn.shape[0]
    nc = s // CHUNK

    def body(q_ref, k_ref, v_ref, bg_ref, bgt_ref, tri_ref, sel_ref,
             u_ref, w_ref, qd_ref, kt_ref, attn_ref, t_ref, gl_ref):
        tri_v = tri_ref[...]
        ii = lax.broadcasted_iota(jnp.int32, (CHUNK, CHUNK), 0)
        jj = lax.broadcasted_iota(jnp.int32, (CHUNK, CHUNK), 1)
        eye = (ii == jj).astype(F32)

        def chain(cc, h, bg3, gc3, gc_row):
            rows = slice(CHUNK * cc, CHUNK * cc + CHUNK)
            lanes = slice(128 * h, 128 * h + 128)
            gcc, beta, incl, strict, decay, gl = _chunk_common(bg3, gc3, gc_row, h, sel_ref)
            yield
            q = q_ref[rows, lanes]
            k = k_ref[rows, lanes]
            vv = v_ref[rows, lanes]
            kb = k * beta
            egc = jnp.exp(gcc)
            a_mat = jnp.where(strict, _mm_nt(kb, k) * decay, 0.0)
            attn_ref[cc, h] = jnp.where(incl, _mm_nt(q, k) * decay, 0.0)
            qd_ref[rows, lanes] = (q * egc).astype(BF16)
            kt_ref[rows, lanes] = (k * jnp.exp(gl - gcc)).astype(BF16)
            gl_ref[cc, h] = jnp.exp(gl)
            yield
            p = -a_mat
            t = eye + p
            for it in range(5):
                if it < SPLIT_ITERS:
                    ps = _split(p)
                    p = _mm3(ps, ps)
                    yield
                    t = t + _mm3(_split(t), _split(p))
                else:
                    p = _mm(p, p)
                    yield
                    t = t + _mm(t, p)
                yield
            t_ref[cc, h] = t
            ts = _split(t)
            u_ref[rows, lanes] = _mm3(ts, _split(vv * beta))
            w_ref[rows, lanes] = _mm3(ts, _split(kb * egc)).astype(BF16)

        gens = []
        for cc in range(CPS):
            bgv = bg_ref[CHUNK * cc:CHUNK * cc + CHUNK, :]
            bg3, gc3, gc_row = _chunk_cumsum(bgv, bgt_ref[cc], tri_v)
            gens += [chain(cc, h, bg3, gc3, gc_row) for h in range(DN_HEADS)]
        _interleave(gens)

    rows_step = CPS * CHUNK
    big = pl.BlockSpec((rows_step, 512), lambda n: (n, 0))
    sq = pl.BlockSpec((CPS, DN_HEADS, CHUNK, CHUNK), lambda n: (n, 0, 0, 0))
    return pl.pallas_call(
        body, grid=(nc // CPS,),
        in_specs=[big, big, big, pl.BlockSpec((rows_step, 128), lambda n: (n, 0)),
                  pl.BlockSpec((CPS, 8, CHUNK), lambda n: (n, 0, 0)),
                  pl.BlockSpec((CHUNK, CHUNK), lambda n: (0, 0)),
                  pl.BlockSpec((8, 128, 128), lambda n: (0, 0, 0))],
        out_specs=[big, big, big, big, sq, sq, pl.BlockSpec((CPS, DN_HEADS, 1, 128), lambda n: (n, 0, 0, 0))],
        out_shape=[_sds((s, 512))] + [_sds((s, 512), BF16)] * 3 + [_sds((nc, DN_HEADS, CHUNK, CHUNK))] * 2
        + [_sds((nc, DN_HEADS, 1, 128))],
        compiler_params=_cp(("parallel",)), name="dn_prep")(qn, kn, v, bg, bgt, tri, sel)


def _dn_scan(u, w, qd, kt, attn, gl):
    s = u.shape[0]
    nc = s // CHUNK

    def body(u_ref, w_ref, qd_ref, kt_ref, attn_ref, gl_ref, o_ref, vn_ref, sp_ref, st_ref):
        n = pl.program_id(0)

        @pl.when(n == 0)
        def _():
            st_ref[...] = jnp.zeros_like(st_ref)

        def chain(cc, h):
            rows = slice(CHUNK * cc, CHUNK * cc + CHUNK)
            lanes = slice(128 * h, 128 * h + 128)
            st = st_ref[h]
            sp_ref[cc, h] = st
            stb = st.astype(BF16)
            ws = _dot(w_ref[rows, lanes].astype(BF16), stb)
            qs = _dot(qd_ref[rows, lanes].astype(BF16), stb)
            yield
            vn = u_ref[rows, lanes] - ws
            vnb = vn.astype(BF16)
            vn_ref[rows, lanes] = vnb
            o_ref[rows, lanes] = qs + _dot(attn_ref[cc, h].astype(BF16), vnb)
            st_ref[h] = st * gl_ref[cc, h] + _dot_tn(kt_ref[rows, lanes].astype(BF16), vnb)

        for cc in range(CPS_SCAN):
            _interleave([chain(cc, h) for h in range(DN_HEADS)])

    big = pl.BlockSpec((CPS_SCAN * CHUNK, 512), lambda n: (n, 0))
    return pl.pallas_call(
        body, grid=(nc // CPS_SCAN,),
        in_specs=[big, big, big, big,
                  pl.BlockSpec((CPS_SCAN, DN_HEADS, CHUNK, CHUNK), lambda n: (n, 0, 0, 0)),
                  pl.BlockSpec((CPS_SCAN, DN_HEADS, 1, 128), lambda n: (n, 0, 0, 0))],
        out_specs=[big, big, pl.BlockSpec((CPS_SCAN, DN_HEADS, DN_HD, DN_HD), lambda n: (n, 0, 0, 0))],
        out_shape=[_sds((s, 512)), _sds((s, 512), BF16), _sds((nc, DN_HEADS, DN_HD, DN_HD))],
        scratch_shapes=[pltpu.VMEM((DN_HEADS, DN_HD, DN_HD), F32)],
        compiler_params=_cp(("arbitrary",)), name="dn_scan")(u, w, qd, kt, attn, gl)


R4 = PATTERNS[1][1]
R16 = PATTERNS[2][1]
TM = 512


def _pattern_spec(r, width=512):
    return pl.BlockSpec((r, TM // r, width), lambda i: (0, i, 0))


def _pattern_shape(s, r, dtype=F32, width=512):
    return _sds((r, s // r, width), dtype)


SLABS = pltpu.VMEM((4, TM, 128), F32)

HEAD_SLOT = 128 // ATT_HEADS


def _head_expand():
    src = np.arange(128)[:, None]
    dst = np.arange(512)[None, :]
    return jnp.asarray((src == (dst // ATT_HD) * HEAD_SLOT).astype(np.float32), dtype=BF16)


def _head_compact():
    src = np.arange(512)[:, None]
    dst = np.arange(128)[None, :]
    return jnp.asarray((src // ATT_HD == dst // HEAD_SLOT).astype(np.float32), dtype=BF16)


def _to_patterns(val, dsts, scr):
    for c in range(val.shape[1] // 128):
        lanes = slice(128 * c, 128 * c + 128)
        scr[c] = val[:, lanes]
        for dst_ref, r in dsts:
            for a in range(r):
                dst_ref[a, :, lanes] = scr[c, pl.ds(a, TM // r, stride=r), :].astype(dst_ref.dtype)


def _from_pattern(src_ref, r, scr):
    n_slab = src_ref.shape[2] // 128
    for c in range(n_slab):
        for a in range(r):
            scr[c, pl.ds(a, TM // r, stride=r), :] = src_ref[a, :, 128 * c:128 * c + 128].astype(F32)
    return jnp.concatenate([scr[c] for c in range(n_slab)], axis=1) if n_slab > 1 else scr[0]


def _att_pre(proj, qw_t, kw_t, bd64):
    s = proj.shape[0]

    def body(q_ref, k_ref, v_ref, qw_ref, kw_ref, bd_ref,
             q1_ref, k1_ref, v1_ref, q4_ref, k4_ref, v4_ref, q16_ref, k16_ref, v16_ref, scr):
        bd = bd_ref[...]
        q = q_ref[...]
        k = k_ref[...]
        qn = q * _seg_rstd(q, bd, ATT_HD) * qw_ref[...] * (ATT_HD ** -0.5)
        kn = k * _seg_rstd(k, bd, ATT_HD) * kw_ref[...]
        q1_ref[...] = qn.astype(BF16)
        k1_ref[...] = kn.astype(BF16)
        v1_ref[...] = v_ref[...].astype(BF16)
        _to_patterns(qn, ((q4_ref, R4), (q16_ref, R16)), scr)
        _to_patterns(kn, ((k4_ref, R4), (k16_ref, R16)), scr)
        _to_patterns(v_ref[...], ((v4_ref, R4), (v16_ref, R16)), scr)

    row = pl.BlockSpec((1, 512), lambda i: (0, 0))
    tok = pl.BlockSpec((TM, 512), lambda i: (i, 0))
    return pl.pallas_call(
        body, grid=(s // TM,),
        in_specs=[pl.BlockSpec((TM, 512), lambda i: (i, COL_ATT_Q)),
                  pl.BlockSpec((TM, 512), lambda i: (i, COL_ATT_K)),
                  pl.BlockSpec((TM, 512), lambda i: (i, COL_ATT_V)),
                  row, row, pl.BlockSpec((512, 128), lambda i: (0, 0))],
        out_specs=[tok] * 3 + [_pattern_spec(R4)] * 3 + [_pattern_spec(R16)] * 3,
        out_shape=[_sds((s, 512), BF16)] * 3 + [_pattern_shape(s, R4, BF16)] * 3 + [_pattern_shape(s, R16, BF16)] * 3,
        scratch_shapes=[SLABS],
        compiler_params=_cp(("parallel",)), name="att_pre")(proj, proj, proj, qw_t, kw_t, bd64)


def _bias_table(rb_ref, bk_ref, o_ref):
    for p in range(len(PATTERNS)):
        bk = bk_ref[p]
        for h in range(ATT_HEADS):
            acc = jnp.zeros((BLK, 2 * BLK), F32)
            for b in range(N_BUCKETS):
                acc = jnp.where(bk == b, rb_ref[h, b], acc)
            o_ref[p, h] = acc


def _bias_grad(ds_refs, bk_ref, o_ref):
    for h in range(ATT_HEADS):
        for b in range(N_BUCKETS):
            tot = jnp.zeros((), F32)
            for p, ds_ref in enumerate(ds_refs):
                tot = tot + jnp.sum(jnp.where(bk_ref[p] == b, ds_ref[h], 0.0))
            o_ref[h, b] = tot


QB_FWD = 2
QB_BWD = 4


def _att_masks(has_prev):
    qi = lax.broadcasted_iota(jnp.int32, (BLK, BLK), 0)
    kj = lax.broadcasted_iota(jnp.int32, (BLK, BLK), 1)
    lane = lax.broadcasted_iota(jnp.int32, (BLK, 2 * ATT_HD), 1)
    return jnp.logical_and(kj >= qi, has_prev), kj <= qi, lane < ATT_HD


def _head_lanes(h):
    half = h % 2
    return slice(ATT_HD * h, ATT_HD * h + ATT_HD), slice(ATT_HD * half, ATT_HD * half + ATT_HD)


def _att_scores(qm, kp2, kc2, bias_h, mask_prev, mask_cur):
    s_prev = jnp.where(mask_prev, _dot_nt(qm, kp2) + bias_h[:, :BLK], NEG)
    s_cur = jnp.where(mask_cur, _dot_nt(qm, kc2) + bias_h[:, BLK:], NEG)
    return s_prev, s_cur


def _att_fwd(q, k, v, bias, p_idx, r, name):
    QB = QB_FWD
    s = q.shape[0]
    nblk = s // BLK
    nseq = nblk // r

    def body(q_ref, kp_ref, kc_ref, vp_ref, vc_ref, b_ref, o_ref, lse_ref):
        j = pl.program_id(0)

        def head(h, rows, masks, q2, kp2, kc2, vp2, vc2):
            mask_prev, mask_cur, lo_half = masks
            out_l, pair_l = _head_lanes(h)
            sel = lo_half if h % 2 == 0 else jnp.logical_not(lo_half)
            qm = jnp.where(sel, q2, jnp.zeros_like(q2))
            s_prev, s_cur = _att_scores(qm, kp2, kc2, b_ref[0, h], mask_prev, mask_cur)
            yield
            m = jnp.maximum(jnp.max(s_prev, axis=1, keepdims=True), jnp.max(s_cur, axis=1, keepdims=True))
            p_prev = jnp.exp(s_prev - m)
            p_cur = jnp.exp(s_cur - m)
            l = jnp.sum(p_prev, axis=1, keepdims=True) + jnp.sum(p_cur, axis=1, keepdims=True)
            yield
            o2 = _dot(p_prev.astype(BF16), vp2) + _dot(p_cur.astype(BF16), vc2)
            o_ref[rows, out_l] = (o2 * (1.0 / l))[:, pair_l].astype(BF16)
            lse_ref[rows, HEAD_SLOT * h:HEAD_SLOT * h + HEAD_SLOT] = jnp.broadcast_to(m + jnp.log(l), (BLK, HEAD_SLOT))

        for sub in range(QB):
            rows = slice(BLK * sub, BLK * sub + BLK)
            before = slice(BLK * (sub - 1), BLK * sub)
            masks = _att_masks(((QB * j + sub) % nseq) != 0)
            gens = []
            for pp in range(ATT_HEADS // 2):
                lanes = slice(128 * pp, 128 * pp + 128)
                kp = kp_ref[:, lanes] if sub == 0 else kc_ref[before, lanes]
                vp = vp_ref[:, lanes] if sub == 0 else vc_ref[before, lanes]
                slabs = (q_ref[rows, lanes], kp, kc_ref[rows, lanes], vp, vc_ref[rows, lanes])
                gens += [head(2 * pp, rows, masks, *slabs), head(2 * pp + 1, rows, masks, *slabs)]
            _interleave(gens)

    cur = pl.BlockSpec((QB * BLK, 512), lambda j: (j, 0))
    prev = pl.BlockSpec((BLK, 512), lambda j: (jnp.maximum(QB * j - 1, 0), 0))
    return pl.pallas_call(
        body, grid=(nblk // QB,),
        in_specs=[cur, prev, cur, prev, cur,
                  pl.BlockSpec((1, ATT_HEADS, BLK, 2 * BLK), lambda j: (p_idx, 0, 0, 0))],
        out_specs=[cur, pl.BlockSpec((QB * BLK, 128), lambda j: (j, 0))],
        out_shape=[_sds((s, 512), BF16), _sds((s, 128))],
        compiler_params=_cp(("parallel",)), name=name)(q, k, k, v, v, bias)


def _post_fwd(o_dn, proj, o_pats, lse_pats, dnw_t, bd128):
    s = o_dn.shape[0]

    def body(o_ref, z_ref, gate_ref, o1_ref, o4_ref, o16_ref, s1_ref, s4_ref, s16_ref, wn_ref, bd_ref, ex_ref,
             mixed_ref, oatt_ref, l1_ref, l4_ref, l16_ref, scr_a, scr_b, scr_c, scr_d):
        o = o_ref[...]
        z = z_ref[...]
        rstd = _seg_rstd(o, bd_ref[...], DN_HD)
        y_dn = o * rstd * wn_ref[...] * (z * _sigmoid(z))
        mixed_ref[:, 0:512] = y_dn.astype(BF16)
        lses = (s1_ref[...], _from_pattern(s4_ref, R4, scr_a), _from_pattern(s16_ref, R16, scr_b))
        m = jnp.maximum(jnp.maximum(lses[0], lses[1]), lses[2])
        tot = jnp.exp(lses[0] - m) + jnp.exp(lses[1] - m) + jnp.exp(lses[2] - m)
        big_l = m + jnp.log(tot)
        l1_ref[...] = big_l
        _to_patterns(big_l, ((l4_ref, R4), (l16_ref, R16)), scr_a)
        ex = ex_ref[...]
        outs = (o1_ref[...], _from_pattern(o4_ref, R4, scr_c), _from_pattern(o16_ref, R16, scr_d))
        acc = jnp.zeros_like(o)
        for lse_p, o_p in zip(lses, outs):
            acc = acc + _lane_bcast(_split3(jnp.exp(lse_p - big_l)), ex) * o_p
        gate = gate_ref[...]
        oatt_ref[...] = acc
        mixed_ref[:, 512:1024] = (acc * (gate * _sigmoid(gate))).astype(BF16)

    blk = pl.BlockSpec((TM, 512), lambda i: (i, 0))
    cblk = pl.BlockSpec((TM, 128), lambda i: (i, 0))
    p4, p16 = _pattern_spec(R4), _pattern_spec(R16)
    c4, c16 = _pattern_spec(R4, 128), _pattern_spec(R16, 128)
    return pl.pallas_call(
        body, grid=(s // TM,),
        in_specs=[blk, pl.BlockSpec((TM, 512), lambda i: (i, COL_Z)),
                  pl.BlockSpec((TM, 512), lambda i: (i, COL_GATE)), blk, p4, p16, cblk, c4, c16,
                  pl.BlockSpec((1, 512), lambda i: (0, 0)), pl.BlockSpec((512, 128), lambda i: (0, 0)),
                  pl.BlockSpec((128, 512), lambda i: (0, 0))],
        out_specs=[pl.BlockSpec((TM, D_MODEL), lambda i: (i, 0)), blk, cblk, c4, c16],
        out_shape=[_sds((s, D_MODEL), BF16), _sds((s, 512)), _sds((s, 128)), _pattern_shape(s, R4, F32, 128),
                   _pattern_shape(s, R16, F32, 128)],
        scratch_shapes=[SLABS] * 4,
        compiler_params=_cp(("parallel",)), name="post_fwd")(o_dn, proj, proj, *o_pats, *lse_pats, dnw_t, bd128,
                                                              _head_expand())


def _out_post(x, mixed, w_out_bf, tgt, o_dn, proj, o_att, dnw_t, bd128):
    s = o_dn.shape[0]
    tm = TM

    def body(x_ref, m_ref, w_ref, t_ref, o_ref, z_ref, gate_ref, oatt_ref, wn_ref, bd128_ref, hc_ref,
             dy_ref, loss_ref, dw_ref, do_ref, dz_ref, dgate_ref, doatt_ref, do4_ref, do16_ref, delta_ref, dl4_ref,
             dl16_ref, dnw_ref, scr):
        i = pl.program_id(0)

        @pl.when(i == 0)
        def _():
            loss_ref[...] = jnp.zeros_like(loss_ref)
            dw_ref[...] = jnp.zeros_like(dw_ref)
            dnw_ref[...] = jnp.zeros_like(dnw_ref)

        err = x_ref[...] + _dot(m_ref[...], w_ref[...]) - t_ref[...]
        d_y = err * (1.0 / D_MODEL)
        dy_ref[...] = d_y
        part = 0.5 * jnp.sum(jnp.mean(err * err, axis=-1, keepdims=True), axis=0, keepdims=True)
        loss_ref[...] = loss_ref[...] + part
        dyb = d_y.astype(BF16)
        dw_ref[...] = dw_ref[...] + _dot_tn(m_ref[...], dyb)
        dm = _dot_nt(dyb, w_ref[...])

        bd128v = bd128_ref[...]
        o = o_ref[...]
        z = z_ref[...]
        wn = wn_ref[...]
        dy = dm[:, 0:512]
        rstd = _seg_rstd(o, bd128v, DN_HD)
        nrm = o * rstd
        sz = _sigmoid(z)
        dz_ref[...] = (dy * nrm * wn * _silu_grad(z, sz)).astype(BF16)
        dn = dy * z * sz
        gw = dn * wn
        do_ref[...] = rstd * (gw - nrm * _seg_expand(_seg_mean(gw * nrm, bd128v, DN_HD), bd128v, DN_HD))
        colsum = jnp.sum(dn * nrm, axis=0, keepdims=True)
        fold = colsum[:, 0:128] + colsum[:, 128:256] + colsum[:, 256:384] + colsum[:, 384:512]
        dnw_ref[...] = dnw_ref[...] + fold
        dya = dm[:, 512:1024]
        gate = gate_ref[...]
        oatt = oatt_ref[...]
        sg = _sigmoid(gate)
        dgate_ref[...] = (dya * oatt * _silu_grad(gate, sg)).astype(BF16)
        doa = dya * gate * sg
        doatt_ref[...] = doa.astype(BF16)
        delta = _segsum(doa * oatt, hc_ref[...])
        delta_ref[...] = delta
        _to_patterns(doa, ((do4_ref, R4), (do16_ref, R16)), scr)
        _to_patterns(delta, ((dl4_ref, R4), (dl16_ref, R16)), scr)

    wide = pl.BlockSpec((tm, D_MODEL), lambda i: (i, 0))
    full = pl.BlockSpec((D_MODEL, D_MODEL), lambda i: (0, 0))
    blk = pl.BlockSpec((tm, 512), lambda i: (i, 0))
    cblk = pl.BlockSpec((tm, 128), lambda i: (i, 0))
    p4, p16 = _pattern_spec(R4), _pattern_spec(R16)
    c4, c16 = _pattern_spec(R4, 128), _pattern_spec(R16, 128)
    return pl.pallas_call(
        body, grid=(s // tm,),
        in_specs=[wide, wide, full, wide, blk,
                  pl.BlockSpec((tm, 512), lambda i: (i, COL_Z)), pl.BlockSpec((tm, 512), lambda i: (i, COL_GATE)),
                  blk, pl.BlockSpec((1, 512), lambda i: (0, 0)), pl.BlockSpec((512, 128), lambda i: (0, 0)),
                  pl.BlockSpec((512, 128), lambda i: (0, 0))],
        out_specs=[wide, pl.BlockSpec((8, 128), lambda i: (0, 0)), full,
                   blk, blk, blk, blk, p4, p16, cblk, c4, c16, pl.BlockSpec((8, 128), lambda i: (0, 0))],
        out_shape=[_sds((s, D_MODEL)), _sds((8, 128)), _sds((D_MODEL, D_MODEL)), _sds((s, 512))]
        + [_sds((s, 512), BF16)] * 3 + [_pattern_shape(s, R4, BF16), _pattern_shape(s, R16, BF16), _sds((s, 128)),
                                        _pattern_shape(s, R4, F32, 128), _pattern_shape(s, R16, F32, 128),
                                        _sds((8, 128))],
        scratch_shapes=[SLABS],
        compiler_params=_cp(("arbitrary",)), name="out_post")(x, mixed, w_out_bf, tgt, o_dn, proj, proj, o_att, dnw_t,
                                                               bd128, _head_compact())


def _att_bwd(q, k, v, do, big_l, delta, bias, p_idx, r, name):
    QB = QB_BWD
    s = q.shape[0]
    nblk = s // BLK
    nseq = nblk // r
    nstep = nblk // QB

    def body(q_ref, kp_ref, kc_ref, vp_ref, vc_ref, do_ref, l_ref, dl_ref, b_ref,
             dq_ref, dk_ref, dv_ref, ds_ref, dkc_ref, dvc_ref):
        j = pl.program_id(0)

        @pl.when(j == 0)
        def _():
            dkc_ref[...] = jnp.zeros_like(dkc_ref)
            dvc_ref[...] = jnp.zeros_like(dvc_ref)
            ds_ref[...] = jnp.zeros_like(ds_ref)

        @pl.when(j < nstep)
        def _():
            def head(h, sub, masks, q2, do2, kp2, kc2, vp2, vc2):
                mask_prev, mask_cur, lo_half = masks
                rows = slice(BLK * sub, BLK * sub + BLK)
                out_l, pair_l = _head_lanes(h)
                sel = lo_half if h % 2 == 0 else jnp.logical_not(lo_half)
                qm = jnp.where(sel, q2, jnp.zeros_like(q2))
                dom = jnp.where(sel, do2, jnp.zeros_like(do2))
                s_prev, s_cur = _att_scores(qm, kp2, kc2, b_ref[0, h], mask_prev, mask_cur)
                dp_prev = _dot_nt(dom, vp2)
                dp_cur = _dot_nt(dom, vc2)
                yield
                lh = l_ref[rows, HEAD_SLOT * h:HEAD_SLOT * h + 1]
                dh = dl_ref[rows, HEAD_SLOT * h:HEAD_SLOT * h + 1]
                p_prev = jnp.exp(s_prev - lh)
                p_cur = jnp.exp(s_cur - lh)
                ds_prev = p_prev * (dp_prev - dh)
                ds_cur = p_cur * (dp_cur - dh)
                ds_ref[h, :, 0:BLK] = ds_ref[h, :, 0:BLK] + ds_prev
                ds_ref[h, :, BLK:2 * BLK] = ds_ref[h, :, BLK:2 * BLK] + ds_cur
                dsb_prev, dsb_cur = ds_prev.astype(BF16), ds_cur.astype(BF16)
                pb_prev, pb_cur = p_prev.astype(BF16), p_cur.astype(BF16)
                yield
                dq_ref[rows, out_l] = (_dot(dsb_prev, kp2) + _dot(dsb_cur, kc2))[:, pair_l].astype(BF16)
                dk_prev = _dot_tn(dsb_prev, q2)[:, pair_l]
                dv_prev = _dot_tn(pb_prev, do2)[:, pair_l]
                if sub == 0:
                    last = slice(BLK * (QB - 1), BLK * QB)
                    dk_ref[last, out_l] = (dkc_ref[last, out_l] + dk_prev).astype(BF16)
                    dv_ref[last, out_l] = (dvc_ref[last, out_l] + dv_prev).astype(BF16)
                else:
                    before = slice(BLK * (sub - 1), BLK * sub)
                    dkc_ref[before, out_l] = dkc_ref[before, out_l] + dk_prev
                    dvc_ref[before, out_l] = dvc_ref[before, out_l] + dv_prev
                yield
                dkc_ref[rows, out_l] = _dot_tn(dsb_cur, q2)[:, pair_l]
                dvc_ref[rows, out_l] = _dot_tn(pb_cur, do2)[:, pair_l]

            done = slice(0, BLK * (QB - 1))
            dk_ref[done, :] = dkc_ref[done, :].astype(BF16)
            dv_ref[done, :] = dvc_ref[done, :].astype(BF16)
            for sub in range(QB):
                rows = slice(BLK * sub, BLK * sub + BLK)
                before = slice(BLK * (sub - 1), BLK * sub)
                masks = _att_masks(((QB * j + sub) % nseq) != 0)
                gens = []
                for pp in range(ATT_HEADS // 2):
                    lanes = slice(128 * pp, 128 * pp + 128)
                    kp = kp_ref[:, lanes] if sub == 0 else kc_ref[before, lanes]
                    vp = vp_ref[:, lanes] if sub == 0 else vc_ref[before, lanes]
                    slabs = (q_ref[rows, lanes], do_ref[rows, lanes], kp, kc_ref[rows, lanes], vp,
                             vc_ref[rows, lanes])
                    gens += [head(2 * pp, sub, masks, *slabs), head(2 * pp + 1, sub, masks, *slabs)]
                _interleave(gens)

        @pl.when(j == nstep)
        def _():
            dk_ref[...] = dkc_ref[...].astype(BF16)
            dv_ref[...] = dvc_ref[...].astype(BF16)

    last_step = nstep - 1
    cur = pl.BlockSpec((QB * BLK, 512), lambda j: (jnp.minimum(j, last_step), 0))
    compact = pl.BlockSpec((QB * BLK, 128), lambda j: (jnp.minimum(j, last_step), 0))
    lag = pl.BlockSpec((QB * BLK, 512), lambda j: (jnp.clip(j - 1, 0, last_step), 0))
    prev = pl.BlockSpec((BLK, 512), lambda j: (jnp.clip(QB * j - 1, 0, nblk - 1), 0))
    return pl.pallas_call(
        body, grid=(nstep + 1,),
        in_specs=[cur, prev, cur, prev, cur, cur, compact, compact,
                  pl.BlockSpec((1, ATT_HEADS, BLK, 2 * BLK), lambda j: (p_idx, 0, 0, 0))],
        out_specs=[cur, lag, lag, pl.BlockSpec((ATT_HEADS, BLK, 2 * BLK), lambda j: (0, 0, 0))],
        out_shape=[_sds((s, 512), BF16)] * 3 + [_sds((ATT_HEADS, BLK, 2 * BLK))],
        scratch_shapes=[pltpu.VMEM((QB * BLK, 512), F32), pltpu.VMEM((QB * BLK, 512), F32)],
        compiler_params=_cp(("arbitrary",)), name=name)(q, k, k, v, v, do, big_l, delta, bias)


def _att_pre_bwd(dq_pats, dk_pats, dv_pats, proj, qw_t, kw_t, bd64):
    s = proj.shape[0]
    tm = TM

    def body(dq1_ref, dq4_ref, dq16_ref, dk1_ref, dk4_ref, dk16_ref, dv1_ref, dv4_ref, dv16_ref,
             q_ref, k_ref, qw_ref, kw_ref, bd_ref,
             dqr_ref, dkr_ref, dvr_ref, dqw_ref, dkw_ref, scr4, scr16):
        i = pl.program_id(0)

        @pl.when(i == 0)
        def _():
            dqw_ref[...] = jnp.zeros_like(dqw_ref)
            dkw_ref[...] = jnp.zeros_like(dkw_ref)

        bd = bd_ref[...]

        def total(d1_ref, d4_ref, d16_ref):
            return d1_ref[...] + _from_pattern(d4_ref, R4, scr4) + _from_pattern(d16_ref, R16, scr16)

        def one(d_refs, x_ref, w_ref, scale, dx_ref, dw_ref):
            dy = total(*d_refs) * scale
            x = x_ref[...]
            rstd = _seg_rstd(x, bd, ATT_HD)
            nrm = x * rstd
            dw_ref[...] = dw_ref[...] + jnp.sum(dy * nrm, axis=0, keepdims=True)
            g = dy * w_ref[...]
            dx_ref[...] = (rstd * (g - nrm * _seg_expand(_seg_mean(g * nrm, bd, ATT_HD), bd, ATT_HD))).astype(BF16)

        one((dq1_ref, dq4_ref, dq16_ref), q_ref, qw_ref, ATT_HD ** -0.5, dqr_ref, dqw_ref)
        one((dk1_ref, dk4_ref, dk16_ref), k_ref, kw_ref, 1.0, dkr_ref, dkw_ref)
        dvr_ref[...] = total(dv1_ref, dv4_ref, dv16_ref).astype(BF16)

    blk = pl.BlockSpec((tm, 512), lambda i: (i, 0))
    pats = [blk, _pattern_spec(R4), _pattern_spec(R16)]
    row = pl.BlockSpec((1, 512), lambda i: (0, 0))
    acc = pl.BlockSpec((8, 512), lambda i: (0, 0))
    return pl.pallas_call(
        body, grid=(s // tm,),
        in_specs=pats * 3 + [pl.BlockSpec((tm, 512), lambda i: (i, COL_ATT_Q)),
                             pl.BlockSpec((tm, 512), lambda i: (i, COL_ATT_K)), row, row,
                             pl.BlockSpec((512, 128), lambda i: (0, 0))],
        out_specs=[blk, blk, blk, acc, acc],
        out_shape=[_sds((s, 512), BF16)] * 3 + [_sds((8, 512))] * 2,
        scratch_shapes=[SLABS] * 2,
        compiler_params=_cp(("arbitrary",)), name="att_pre_bwd")(*dq_pats, *dk_pats, *dv_pats, proj, proj,
                                                                  qw_t, kw_t, bd64)


def _dn_scan_bwd(do, sp, qd, kt, w, vn, attn, gl):
    s = do.shape[0]
    nc = s // CHUNK

    def body(do_ref, sp_ref, qd_ref, kt_ref, w_ref, vn_ref, attn_ref, gl_ref,
             du_ref, dqd_ref, dkt_ref, dw_ref, dattn_ref, dgl_ref, ds_ref):
        n = pl.program_id(0)

        @pl.when(n == 0)
        def _():
            ds_ref[...] = jnp.zeros_like(ds_ref)

        def chain(cc, h):
            rows = slice(CHUNK * cc, CHUNK * cc + CHUNK)
            lanes = slice(128 * h, 128 * h + 128)
            dsn = ds_ref[h]
            st = sp_ref[cc, h]
            dsb, stb = dsn.astype(BF16), st.astype(BF16)
            dob = do_ref[rows, lanes].astype(BF16)
            vnb = vn_ref[rows, lanes].astype(BF16)
            dvn = _dot_tn(attn_ref[cc, h].astype(BF16), dob) + _dot(kt_ref[rows, lanes].astype(BF16), dsb)
            du_ref[rows, lanes] = dvn
            dqd_ref[rows, lanes] = _dot_nt(dob, stb)
            dattn_ref[cc, h] = _dot_nt(dob, vnb)
            dkt_ref[rows, lanes] = _dot_nt(vnb, dsb)
            tot = jnp.sum(jnp.sum(st * dsn, axis=1, keepdims=True), axis=0, keepdims=True)
            dgl_ref[cc, h] = jnp.broadcast_to(tot, (1, 128))
            qdo = _dot_tn(qd_ref[rows, lanes].astype(BF16), dob)
            yield
            dvb = dvn.astype(BF16)
            dw_ref[rows, lanes] = -_dot_nt(dvb, stb)
            ds_ref[h] = qdo + dsn * gl_ref[cc, h] - _dot_tn(w_ref[rows, lanes].astype(BF16), dvb)

        for cc in reversed(range(CPS_SCAN)):
            _interleave([chain(cc, h) for h in range(DN_HEADS)])

    nsteps = nc // CPS_SCAN
    big = pl.BlockSpec((CPS_SCAN * CHUNK, 512), lambda n: (nsteps - 1 - n, 0))
    sq = pl.BlockSpec((CPS_SCAN, DN_HEADS, CHUNK, CHUNK), lambda n: (nsteps - 1 - n, 0, 0, 0))
    glb = pl.BlockSpec((CPS_SCAN, DN_HEADS, 1, 128), lambda n: (nsteps - 1 - n, 0, 0, 0))
    return pl.pallas_call(
        body, grid=(nsteps,),
        in_specs=[big, pl.BlockSpec((CPS_SCAN, DN_HEADS, DN_HD, DN_HD), lambda n: (nsteps - 1 - n, 0, 0, 0)),
                  big, big, big, big, sq, glb],
        out_specs=[big, big, big, big, sq, glb],
        out_shape=[_sds((s, 512))] * 4 + [_sds((nc, DN_HEADS, CHUNK, CHUNK)), _sds((nc, DN_HEADS, 1, 128))],
        scratch_shapes=[pltpu.VMEM((DN_HEADS, DN_HD, DN_HD), F32)],
        compiler_params=_cp(("arbitrary",)), name="dn_scan_bwd")(do, sp, qd, kt, w, vn, attn, gl)


def _dn_prep_bwd(qn, kn, v, bg, bgt, tri, sel, t_inv, attn, u, w, du, dw, dqd, dkt, dattn, dgl):
    s = qn.shape[0]
    nc = s // CHUNK

    def body(q_ref, k_ref, v_ref, bg_ref, bgt_ref, tri_ref, sel_ref, t_ref, attn_ref, u_ref, w_ref,
             du_ref, dw_ref, dqd_ref, dkt_ref, dattn_ref, dgl_ref,
             dq_ref, dk_ref, dv_ref, dbg_ref):
        tri_v = tri_ref[...]
        lane = lax.broadcasted_iota(jnp.int32, (CHUNK, 128), 1)
        rowi = lax.broadcasted_iota(jnp.int32, (CHUNK, 128), 0)
        ones_b = jnp.ones((CHUNK, 128), BF16)
        ones_sq = jnp.ones((128, 128), BF16)
        parts = [[] for _ in range(CPS)]

        def chain(cc, h, bg3, gc3, gc_row):
            rows = slice(CHUNK * cc, CHUNK * cc + CHUNK)
            lanes = slice(128 * h, 128 * h + 128)
            gcc, beta, incl, strict, decay, gl = _chunk_common(bg3, gc3, gc_row, h, sel_ref)
            yield
            q = q_ref[rows, lanes]
            k = k_ref[rows, lanes]
            vv = v_ref[rows, lanes]
            ts = _split(t_ref[cc, h])
            egc = jnp.exp(gcc)
            kb = k * beta
            a_mat = jnp.where(strict, _mm_nt(kb, k) * decay, 0.0)
            dvb = _mm3_tn(ts, _split(du_ref[rows, lanes]))
            dkbg = _mm3_tn(ts, _split(dw_ref[rows, lanes]))
            yield
            d_a = jnp.where(strict, -(_mm_nt(dvb, u_ref[rows, lanes]) + _mm_nt(dkbg, w_ref[rows, lanes])), 0.0)
            d_m = d_a * decay
            dattn_m = jnp.where(incl, dattn_ref[cc, h], 0.0)
            dqk = dattn_m * decay
            e_hi, e_lo = _split(d_a * a_mat + dattn_m * attn_ref[cc, h])
            yield
            dkb = _mm(d_m, k)
            dk = _mm_tn(d_m, kb) + _mm_tn(dqk, q)
            dq = _mm(dqk, k)
            e_colsum = _dot_tn(e_hi, ones_b) + _dot_tn(e_lo, ones_b)
            e_rowsum = _dot(e_hi, ones_b) + _dot(e_lo, ones_b)
            dqd = dqd_ref[rows, lanes]
            dkt = dkt_ref[rows, lanes]
            sums = _rowsum_b(jnp.concatenate([dqd * q, dkt * k, dkbg * k, dkb * k, dvb * vv], axis=0), ones_sq)
            s_dqd, s_dkt, rk, s_dkb, s_dvb = (sums[CHUNK * n:CHUNK * n + CHUNK] for n in range(5))
            yield
            tail = jnp.exp(gl - gcc)
            r = s_dkt * tail
            dgl_tot = jnp.sum(r, axis=0, keepdims=True) + dgl_ref[cc, h] * jnp.exp(gl)
            dgc = e_rowsum - e_colsum + s_dqd * egc - r + rk * beta * egc
            dgc = dgc + jnp.where(rowi == CHUNK - 1, dgl_tot, 0.0)
            dq_ref[rows, lanes] = dq + dqd * egc
            dk_ref[rows, lanes] = dk + dkt * tail + dkbg * (beta * egc) + dkb * beta
            dv_ref[rows, lanes] = dvb * beta
            parts[cc].append((h, dgc, rk * egc + s_dkb + s_dvb))

        gens = []
        for cc in range(CPS):
            bgv = bg_ref[CHUNK * cc:CHUNK * cc + CHUNK, :]
            bg3, gc3, gc_row = _chunk_cumsum(bgv, bgt_ref[cc], tri_v)
            gens += [chain(cc, h, bg3, gc3, gc_row) for h in range(DN_HEADS)]
        _interleave(gens)
        for cc in range(CPS):
            dgc_mat = jnp.zeros((CHUNK, 128), F32)
            dbeta_mat = jnp.zeros((CHUNK, 128), F32)
            for h, dgc, dbeta in parts[cc]:
                dgc_mat = dgc_mat + jnp.where(lane == DN_HEADS + h, dgc, 0.0)
                dbeta_mat = dbeta_mat + jnp.where(lane == h, dbeta, 0.0)
            dbg_ref[CHUNK * cc:CHUNK * cc + CHUNK, :] = _mmx_tn(tri_v, dgc_mat) + dbeta_mat

    big = pl.BlockSpec((CPS * CHUNK, 512), lambda n: (n, 0))
    sq = pl.BlockSpec((CPS, DN_HEADS, CHUNK, CHUNK), lambda n: (n, 0, 0, 0))
    glb = pl.BlockSpec((CPS, DN_HEADS, 1, 128), lambda n: (n, 0, 0, 0))
    small = pl.BlockSpec((CPS * CHUNK, 128), lambda n: (n, 0))
    return pl.pallas_call(
        body, grid=(nc // CPS,),
        in_specs=[big, big, big, small, pl.BlockSpec((CPS, 8, CHUNK), lambda n: (n, 0, 0)),
                  pl.BlockSpec((CHUNK, CHUNK), lambda n: (0, 0)),
                  pl.BlockSpec((8, 128, 128), lambda n: (0, 0, 0)), sq, sq, big, big,
                  big, big, big, big, sq, glb],
        out_specs=[big, big, big, small],
        out_shape=[_sds((s, 512))] * 3 + [_sds((s, 128))],
        compiler_params=_cp(("parallel",)), name="dn_prep_bwd")(qn, kn, v, bg, bgt, tri, sel, t_inv, attn, u, w,
                                                                  du, dw, dqd, dkt, dattn, dgl)


def _dn_pre_bwd(dqn, dkn, dv, dbg, proj, conv_w8, alog_l, dtb_l):
    s = proj.shape[0]
    tr = 512
    nh = tr // 8
    nblk = s // tr

    def body(dq_ref, dk_ref, dv_ref, dbg_ref, u_ref, halo_ref, ba_ref, w_ref, al_ref, dt_ref,
             du_ref, dba_ref, dsm_ref, dw_ref, nxt_ref):
        i = pl.program_id(0)

        @pl.when(i == 0)
        def _():
            dsm_ref[...] = jnp.zeros_like(dsm_ref)
            dw_ref[...] = jnp.zeros_like(dw_ref)
            nxt_ref[...] = jnp.zeros_like(nxt_ref)

        keep = (i < nblk - 1).astype(F32)
        row8 = lax.broadcasted_iota(jnp.int32, (8, 128), 0)
        for c in range(12):
            lanes = slice(128 * c, 128 * c + 128)
            u = u_ref[:, lanes]
            xcat = jnp.concatenate([halo_ref[:, lanes] * keep, u], axis=0)
            shifted = [u] + [pltpu.roll(xcat, k, 0)[8:8 + tr] for k in range(1, CONV_W)]
            y = shifted[0] * w_ref[CONV_W - 1:CONV_W, lanes]
            for k in range(1, CONV_W):
                y = y + shifted[k] * w_ref[CONV_W - 1 - k:CONV_W - k, lanes]
            sg = _sigmoid(y)
            sv = y * sg
            if c < 8:
                rs = lax.rsqrt(jnp.sum(sv * sv, axis=1, keepdims=True) + EPS)
                n = sv * rs
                if c < 4:
                    dn = dq_ref[:, lanes] * (DN_HD ** -0.5)
                else:
                    dn = dk_ref[:, slice(128 * (c - 4), 128 * (c - 3))]
                dsv = rs * (dn - n * jnp.sum(dn * n, axis=1, keepdims=True))
            else:
                dsv = dv_ref[:, slice(128 * (c - 8), 128 * (c - 7))]
            dyc = dsv * _silu_grad(y, sg)
            dcat = jnp.concatenate([dyc, nxt_ref[:, lanes]], axis=0)
            du = dyc * w_ref[CONV_W - 1:CONV_W, lanes]
            dwc = jnp.zeros((8, 128), F32)
            for k in range(CONV_W):
                if k:
                    du = du + pltpu.roll(dcat, tr + 8 - k, 0)[0:tr] * w_ref[CONV_W - 1 - k:CONV_W - k, lanes]
                dwc = dwc + jnp.where(row8 == CONV_W - 1 - k, jnp.sum(dyc * shifted[k], axis=0, keepdims=True), 0.0)
            du_ref[:, lanes] = du.astype(BF16)
            dw_ref[:, lanes] = dw_ref[:, lanes] + dwc
            nxt_ref[:, lanes] = dyc[0:8]
        _, lane, sig_b, t, nega, g = _beta_g(ba_ref[...], al_ref[...], dt_ref[...])
        dbg = dbg_ref[...]
        da = dbg * nega * _sigmoid(t)
        is_b = lane < DN_HEADS
        is_a = jnp.logical_and(lane >= DN_HEADS, lane < 2 * DN_HEADS)
        dba_ref[...] = jnp.where(is_b, dbg * sig_b * (1.0 - sig_b), jnp.where(is_a, da, 0.0)).astype(BF16)
        d_alog = jnp.sum(jnp.where(is_a, dbg * g, 0.0), axis=0, keepdims=True)
        d_dtb = jnp.sum(jnp.where(is_a, da, 0.0), axis=0, keepdims=True)
        dsm_ref[...] = dsm_ref[...] + jnp.where(row8 == 0, d_alog, jnp.where(row8 == 1, d_dtb, 0.0))

    last = nblk - 1
    blk = pl.BlockSpec((tr, 512), lambda i: (last - i, 0))
    wide = pl.BlockSpec((tr, 1536), lambda i: (last - i, 0))
    return pl.pallas_call(
        body, grid=(nblk,),
        in_specs=[blk, blk, blk, pl.BlockSpec((tr, 128), lambda i: (last - i, 0)), wide,
                  pl.BlockSpec((8, 1536), lambda i: (jnp.maximum((last - i) * nh - 1, 0), 0)),
                  pl.BlockSpec((tr, 128), lambda i: (last - i, COL_BA_128)),
                  pl.BlockSpec((8, 1536), lambda i: (0, 0)),
                  pl.BlockSpec((1, 128), lambda i: (0, 0)), pl.BlockSpec((1, 128), lambda i: (0, 0))],
        out_specs=[wide, pl.BlockSpec((tr, 128), lambda i: (last - i, 0)),
                   pl.BlockSpec((8, 128), lambda i: (0, 0)), pl.BlockSpec((8, 1536), lambda i: (0, 0))],
        out_shape=[_sds((s, 1536), BF16), _sds((s, 128), BF16), _sds((8, 128)), _sds((8, 1536))],
        scratch_shapes=[pltpu.VMEM((8, 1536), F32)],
        compiler_params=_cp(("arbitrary",)), name="dn_pre_bwd")(dqn, dkn, dv, dbg, proj, proj, proj, conv_w8,
                                                                 alog_l, dtb_l)


PIECE_WIDTHS = (1536, 512, 512, 512, 512, 512, 128)


def _in_bwd_dx(pieces, w_bf, x, dy, norm_w, ds_accs, buckets, chip_sums=None):
    s = x.shape[0]
    tm = 512
    n_piece = len(PIECE_WIDTHS)
    n_step = s // tm
    n_arr = 0 if chip_sums is None else len(chip_sums)
    n_sent = 3 * n_arr

    def body(*refs):
        refs = list(refs)

        def take(n):
            return [refs.pop(0) for _ in range(n)]

        piece_refs = take(n_piece)
        w_ref, x_ref, dy_ref, nw_ref = take(4)
        ds_refs = take(len(ds_accs))
        bk_ref, = take(1)
        hb_refs = take(n_arr)
        dx_ref, dnw_ref, drb_ref = take(3)
        recv_refs = take(n_arr)
        i = pl.program_id(0)

        def copies():
            send_sems, recv_sems = refs
            xi, yi, ci = _position()
            out = []
            for j, (px, py) in enumerate(_other_chips(xi, yi)):
                for t, (src, dst) in enumerate(zip(hb_refs, recv_refs)):
                    k = n_arr * j + t
                    out.append(pltpu.make_async_remote_copy(
                        src_ref=src.at[2 * px + py], dst_ref=dst.at[j], send_sem=send_sems.at[k],
                        recv_sem=recv_sems.at[k], device_id=(px, py, ci), device_id_type=MESH_ID))
            return out

        @pl.when(i == 0)
        def _():
            dnw_ref[...] = jnp.zeros_like(dnw_ref)
            for cp in (copies() if n_sent else []):
                cp.start()

        @pl.when(i == 1)
        def _():
            _bias_grad(ds_refs, bk_ref, drb_ref)

        dp = jnp.concatenate([r[...] for r in piece_refs], axis=1)
        dh = _dot(dp, w_ref[...])
        xv = x_ref[...]
        rstd = lax.rsqrt(jnp.mean(xv * xv, axis=-1, keepdims=True) + EPS)
        xh = xv * rstd
        dnw_ref[...] = dnw_ref[...] + jnp.sum(dh * xh, axis=0, keepdims=True)
        g = dh * nw_ref[...]
        dx_ref[...] = rstd * (g - xh * jnp.mean(g * xh, axis=-1, keepdims=True)) + dy_ref[...]

        if n_sent:
            @pl.when(i == n_step - 1)
            def _():
                cps = copies()
                for cp in cps:
                    cp.wait_recv()
                for cp in cps:
                    cp.wait_send()

    def blk(n):
        return pl.BlockSpec((tm, n), lambda i: (i, 0))

    vm = pl.BlockSpec(memory_space=pltpu.VMEM)
    in_specs = [blk(n) for n in PIECE_WIDTHS] + [pl.BlockSpec((D_IN_PAD, D_MODEL), lambda i: (0, 0)), blk(D_MODEL),
                                                blk(D_MODEL), pl.BlockSpec((1, D_MODEL), lambda i: (0, 0))]
    in_specs += [vm] * (len(ds_accs) + 1)
    out_specs = [blk(D_MODEL), pl.BlockSpec((8, D_MODEL), lambda i: (0, 0)), pl.BlockSpec(memory_space=pltpu.SMEM)]
    out_shape = [_sds((s, D_MODEL)), _sds((8, D_MODEL)), _sds((ATT_HEADS, N_BUCKETS))]
    scratch = []
    extra = ()
    if n_sent:
        extra = tuple(chip_sums)
        in_specs += [ANY] * n_arr
        out_specs += [ANY] * n_arr
        out_shape += [_sds((3,) + a.shape[1:], a.dtype) for a in extra]
        scratch = [pltpu.SemaphoreType.DMA((n_sent,)), pltpu.SemaphoreType.DMA((n_sent,))]
    return pl.pallas_call(
        body, grid=(n_step,), in_specs=in_specs, out_specs=out_specs, out_shape=out_shape, scratch_shapes=scratch,
        compiler_params=_cp(("arbitrary",)), name="in_bwd_dx")(*pieces, w_bf, x, dy, norm_w, *ds_accs, buckets, *extra)


def _in_bwd_dw(pieces, x, norm_w):
    s = x.shape[0]
    tm = 512
    n_piece = len(PIECE_WIDTHS)

    def body(*refs):
        piece_refs = refs[:n_piece]
        x_ref, nw_ref, dw_ref = refs[n_piece:]
        i = pl.program_id(0)

        @pl.when(i == 0)
        def _():
            dw_ref[...] = jnp.zeros_like(dw_ref)

        xv = x_ref[...]
        rstd = lax.rsqrt(jnp.mean(xv * xv, axis=-1, keepdims=True) + EPS)
        h = (xv * rstd * nw_ref[...]).astype(BF16)
        at = 0
        for r, width in zip(piece_refs, PIECE_WIDTHS):
            dw_ref[at:at + width, :] = dw_ref[at:at + width, :] + _dot_tn(r[...], h)
            at += width

    return pl.pallas_call(
        body, grid=(s // tm,),
        in_specs=[pl.BlockSpec((tm, n), lambda i: (i, 0)) for n in PIECE_WIDTHS]
        + [pl.BlockSpec((tm, D_MODEL), lambda i: (i, 0)), pl.BlockSpec((1, D_MODEL), lambda i: (0, 0))],
        out_specs=pl.BlockSpec((D_IN_PAD, D_MODEL), lambda i: (0, 0)),
        out_shape=_sds((D_IN_PAD, D_MODEL)),
        compiler_params=_cp(("arbitrary",)), name="in_bwd_dw")(*pieces, x, norm_w)


def _flat(a):
    return a.reshape(-1, a.shape[-1])


def _as_pattern(a, r):
    return a if r == 1 else a.reshape(r, a.shape[0] // r, a.shape[1])


D_SHARD = D_IN // N_CHIPS
BA_START = 4 * D_DN
BA_PACKED = 4096
S1_HEAD = BA_START - D_SHARD
S1_BA = 2 * D_SHARD - BA_START


def _pack_w(g):
    n = g.shape[2]

    def body(g_ref, o_ref):
        g32 = g_ref.bitcast(jnp.uint32)
        o32 = o_ref.bitcast(jnp.uint32)
        full, head, ba1 = D_SHARD // 2, S1_HEAD // 2, S1_BA // 2
        ba2 = DN_HEADS - ba1
        pieces = [(0, 0, full), (1, 0, head), (2, ba2, full), (3, 0, full), (1, head, full), (2, 0, ba2)]
        at = 0
        for chip, lo, hi in pieces:
            o32[at:at + hi - lo, :] = g32[chip, lo:hi, :]
            at += hi - lo
        o32[at:D_IN_PAD // 2, :] = jnp.zeros((D_IN_PAD // 2 - at, n), jnp.uint32)

    vm = pl.BlockSpec(memory_space=pltpu.VMEM)
    return pl.pallas_call(body, in_specs=[vm], out_specs=vm, out_shape=_sds((D_IN_PAD, n), g.dtype),
                          compiler_params=_cp(), name="pack_w")(g)


def _unpack_w(p, rows):
    n = p.shape[1]
    tn = 256
    mid = BA_PACKED + S1_BA
    pieces = [(0, 0, (0, D_SHARD)), (1, 0, (D_SHARD, BA_START)), (1, S1_HEAD, (BA_PACKED, mid)),
              (2, 0, (mid, BA_PACKED + 2 * DN_HEADS)), (2, 2 * DN_HEADS - S1_BA, (BA_START, BA_START + S1_HEAD)),
              (3, 0, (BA_START + S1_HEAD, BA_PACKED))]

    def body(p_ref, o_ref):
        for chip, at, (lo, hi) in pieces:
            o_ref[chip, at:at + hi - lo, :] = p_ref[lo:hi, :]
        for chip in range(N_CHIPS):
            o_ref[chip, D_SHARD:rows, :] = jnp.zeros((rows - D_SHARD, tn), p.dtype)

    return pl.pallas_call(
        body, grid=(n // tn,),
        in_specs=[pl.BlockSpec((D_IN_PAD, tn), lambda j: (0, j))],
        out_specs=pl.BlockSpec((N_CHIPS, rows, tn), lambda j: (0, 0, j)),
        out_shape=_sds((N_CHIPS, rows, n), p.dtype),
        compiler_params=_cp(("parallel",)), name="unpack_w")(p)


def _lane_row(vec, offset):
    return jnp.pad(vec.reshape(1, -1), ((0, 0), (offset, 128 - offset - vec.shape[0])))


def _local_step(x, tgt, norm_w, w_bf, conv_w, a_log, dt_bias, dn_norm_w, q_norm_w, k_norm_w, bias, w_out_bf):
    s = x.shape[0]
    nc = s // CHUNK
    conv_w8 = jnp.pad(conv_w, ((0, 8 - CONV_W), (0, 0)))
    alog_l = _lane_row(a_log.reshape(-1), DN_HEADS)
    dtb_l = _lane_row(dt_bias.reshape(-1), DN_HEADS)
    dnw_t = jnp.tile(dn_norm_w.reshape(1, DN_HD), (1, DN_HEADS))
    qw_t = jnp.tile(q_norm_w.reshape(1, ATT_HD), (1, ATT_HEADS))
    kw_t = jnp.tile(k_norm_w.reshape(1, ATT_HD), (1, ATT_HEADS))
    bd128 = _compact_mat(DN_HD)
    bd64 = _compact_mat(ATT_HD)
    tri = _tri_incl()
    sel = _lane_select()
    buckets = _bucket_tables()

    proj, qn, kn, v_dn, bg = _in_proj(x, norm_w, w_bf, conv_w8, alog_l, dtb_l)
    bgt = bg[:, 0:8].reshape(nc, CHUNK, 8).transpose(0, 2, 1)
    u, w, qd, kt, attn, t_inv, gl = _dn_prep(qn, kn, v_dn, bg, bgt, tri, sel)
    o_dn, vn, sp = _dn_scan(u, w, qd, kt, attn, gl)
    q1, k1, v1, q4, k4, v4, q16, k16, v16 = _att_pre(proj, qw_t, kw_t, bd64)
    rs = [r for _, r in PATTERNS]
    qkv = [(q1, k1, v1), (_flat(q4), _flat(k4), _flat(v4)), (_flat(q16), _flat(k16), _flat(v16))]
    o_pats, lse_pats = [], []
    for p, r in enumerate(rs):
        o_p, lse_p = _att_fwd(*qkv[p], bias, p, r, "att_fwd_r%d" % r)
        o_pats.append(_as_pattern(o_p, r))
        lse_pats.append(_as_pattern(lse_p, r))
    mixed, o_att, l1, l4, l16 = _post_fwd(o_dn, proj, o_pats, lse_pats, dnw_t, bd128)

    (dy, loss_blk, d_w_out, do_dn, dz, dgate, do1, do4, do16, dl1, dl4, dl16,
     d_dnw) = _out_post(x, mixed, w_out_bf, tgt, o_dn, proj, o_att, dnw_t, bd128)
    side = [(do1, l1, dl1), (_flat(do4), _flat(l4), _flat(dl4)), (_flat(do16), _flat(l16), _flat(dl16))]
    dq_pats, dk_pats, dv_pats, ds_accs = [], [], [], []
    for p, r in enumerate(rs):
        dq_p, dk_p, dv_p, ds_p = _att_bwd(*qkv[p], *side[p], bias, p, r, "att_bwd_r%d" % r)
        dq_pats.append(_as_pattern(dq_p, r))
        dk_pats.append(_as_pattern(dk_p, r))
        dv_pats.append(_as_pattern(dv_p, r))
        ds_accs.append(ds_p)
    dq_att, dk_att, dv_att, d_qw, d_kw = _att_pre_bwd(dq_pats, dk_pats, dv_pats, proj, qw_t, kw_t, bd64)
    du, dqd, dkt, dw, dattn, dgl = _dn_scan_bwd(do_dn, sp, qd, kt, w, vn, attn, gl)
    dqn, dkn, dv_dn, dbg = _dn_prep_bwd(qn, kn, v_dn, bg, bgt, tri, sel, t_inv, attn, u, w, du, dw, dqd, dkt, dattn,
                                        dgl)
    d_qkv_dn, dba, dsm, d_conv8 = _dn_pre_bwd(dqn, dkn, dv_dn, dbg, proj, conv_w8, alog_l, dtb_l)
    pieces = (d_qkv_dn, dz, dq_att, dk_att, dv_att, dgate, dba)
    d_w_in_t = _in_bwd_dw(pieces, x, norm_w)
    last = functools.partial(_in_bwd_dx, pieces, w_bf, x, dy, norm_w, ds_accs, buckets)

    grads = dict(
        w_in_t=d_w_in_t,
        conv_w=d_conv8[0:CONV_W, :],
        a_log=dsm[0:1, DN_HEADS:2 * DN_HEADS],
        dt_bias=dsm[1:2, DN_HEADS:2 * DN_HEADS],
        dn_norm_w=d_dnw[0:1, :],
        q_norm_w=d_qw[0:1, :].reshape(ATT_HEADS, ATT_HD),
        k_norm_w=d_kw[0:1, :].reshape(ATT_HEADS, ATT_HD),
        w_out=d_w_out,
    )
    return loss_blk[0, 0], last, grads


MESH_ID = pl.DeviceIdType.MESH
ANY = pl.BlockSpec(memory_space=pl.ANY)


def _position():
    return lax.axis_index("x"), lax.axis_index("y"), lax.axis_index("c")


def _other_chips(x, y):
    return [(1 - x, y), (x, 1 - y), (1 - x, 1 - y)]


SHARD_PAD = 1040
WIN = 528


def _row_split(n):
    return (n // 2) // 128 * 128


def _part(ref, cc):
    n = ref.shape[0]
    sp = _row_split(n)
    return ref.at[pl.ds(0, sp)] if cc == 0 else ref.at[pl.ds(sp, n - sp)]


def _halves(ref):
    n = ref.shape[0]
    sp = (n // 2) // 16 * 16
    return ref.at[pl.ds(0, sp)], ref.at[pl.ds(sp, n - sp)]


def _gather_weights(wt_s, w_out_s, conv_s, rel_bias, buckets):
    def body(a_ref, b_ref, c_ref, rb_ref, bk_ref, ga_ref, gb_ref, gc_ref, bias_ref, send_sems, recv_sems, loc_sems,
             a_vmem, b_vmem):
        x, y, c = _position()
        me = 2 * x + y
        sib = (x, y, 1 - c)
        big = ((a_ref, ga_ref), (b_ref, gb_ref))
        stage_in = [pltpu.make_async_copy(a_ref, a_vmem, loc_sems.at[0]),
                    pltpu.make_async_copy(b_ref, b_vmem, loc_sems.at[1])]
        local = [pltpu.make_async_copy(a_vmem, ga_ref.at[me], loc_sems.at[0]),
                 pltpu.make_async_copy(b_vmem, gb_ref.at[me], loc_sems.at[1]),
                 pltpu.make_async_copy(c_ref, gc_ref.at[me], loc_sems.at[2])]
        for cp in stage_in:
            cp.start()
        local[2].start()
        for cp in stage_in:
            cp.wait()
        for cp in local[:2]:
            cp.start()
        others = _other_chips(x, y)

        def exchange(cc):
            nbr = others[:2]
            dg = others[2]
            pending = []

            def copy(k, src, dst, to):
                cp = pltpu.make_async_remote_copy(src_ref=src, dst_ref=dst, send_sem=send_sems.at[k],
                                                  recv_sem=recv_sems.at[k], device_id=to, device_id_type=MESH_ID)
                pending.append(cp)
                cp.start()

            def landed(k, dst):
                pltpu.make_async_remote_copy(src_ref=dst, dst_ref=dst, send_sem=send_sems.at[k],
                                             recv_sem=recv_sems.at[k], device_id=sib, device_id_type=MESH_ID).wait_recv()

            def slot(dst, chip, c_part):
                return _part(dst.at[2 * chip[0] + chip[1]], c_part)

            for i, n in enumerate(nbr):
                for t, (src, dst) in enumerate(big):
                    copy(2 * i + t, _part(src, cc), _part(dst.at[me], cc), (*n, c))
            for j, chip in enumerate(others):
                copy(4 + j, c_ref, gc_ref.at[me], (*chip, c))
            _bias_table(rb_ref, bk_ref, bias_ref)
            for i, n in enumerate(nbr):
                for t, (_, dst) in enumerate(big):
                    got = slot(dst, n, cc)
                    landed(2 * i + t, got)
                    half = _halves(got)[1 - i]
                    copy(7 + 2 * (1 - i) + t, half, half, (*nbr[1 - i], c))
                    copy(11 + 2 * i + t, got, got, sib)
            for i, n in enumerate(nbr):
                for t, (_, dst) in enumerate(big):
                    half = _halves(slot(dst, dg, cc))[i]
                    landed(7 + 2 * i + t, half)
                    copy(15 + 2 * i + t, half, half, sib)
            for j, chip in enumerate(others):
                landed(4 + j, gc_ref.at[2 * chip[0] + chip[1]])
            for i, n in enumerate(nbr):
                for t, (_, dst) in enumerate(big):
                    landed(11 + 2 * i + t, slot(dst, n, 1 - cc))
                    landed(15 + 2 * i + t, _halves(slot(dst, dg, 1 - cc))[i])
            for cp in pending:
                cp.wait_send()

        for cc in (0, 1):
            pl.when(c == cc)(functools.partial(exchange, cc))
        for cp in local:
            cp.wait()

    srcs = (wt_s, w_out_s, conv_s)
    n_sem = 19
    return pl.pallas_call(
        body, in_specs=[ANY] * 3 + [pl.BlockSpec(memory_space=pltpu.SMEM), pl.BlockSpec(memory_space=pltpu.VMEM)],
        out_specs=[ANY] * 3 + [pl.BlockSpec(memory_space=pltpu.VMEM)],
        out_shape=[_sds((N_CHIPS,) + a.shape, a.dtype) for a in srcs]
        + [_sds((len(PATTERNS), ATT_HEADS, BLK, 2 * BLK))],
        scratch_shapes=[pltpu.SemaphoreType.DMA((n_sem,)), pltpu.SemaphoreType.DMA((n_sem,)),
                        pltpu.SemaphoreType.DMA((3,)), pltpu.VMEM(wt_s.shape, wt_s.dtype),
                        pltpu.VMEM(w_out_s.shape, w_out_s.dtype)],
        compiler_params=_cp(), name="gather_weights")(*srcs, rel_bias, buckets)


N_DEV = 8
PEER_FLIPS = [(dx, dy, dc) for dx in (0, 1) for dy in (0, 1) for dc in (0, 1)][1:]


def _small_copies(s_ref, rs_ref, send_sems, recv_sems):
    x, y, c = _position()
    dev = 4 * x + 2 * y + c
    sends, recvs = [], []
    for f, (dx, dy, dc) in enumerate(PEER_FLIPS):
        peer = (x ^ dx, y ^ dy, c ^ dc)
        sends.append(pltpu.make_async_remote_copy(
            src_ref=s_ref, dst_ref=rs_ref.at[dev], send_sem=send_sems.at[f], recv_sem=recv_sems.at[f],
            device_id=peer, device_id_type=MESH_ID))
        recvs.append(pltpu.make_async_remote_copy(
            src_ref=s_ref, dst_ref=rs_ref.at[4 * peer[0] + 2 * peer[1] + peer[2]], send_sem=send_sems.at[f],
            recv_sem=recv_sems.at[f], device_id=peer, device_id_type=MESH_ID))
    return sends, recvs


def _fill_parts(h_in, r_in, h_out, r_out, small, rows_in, rows_out):
    def body(ha_ref, ra_ref, hb_ref, rb_ref, s_ref, fa_ref, fb_ref, rs_ref, send_sems, recv_sems, loc_sems,
             small_send, small_recv, a_vmem, b_vmem, ra_vmem, rb_vmem):
        x, y, c = _position()
        sib = (x, y, 1 - c)
        chip = 2 * x + y
        pairs = ((a_vmem, fa_ref), (b_vmem, fb_ref))
        own_small = pltpu.make_async_copy(s_ref, rs_ref.at[4 * x + 2 * y + c], loc_sems.at[2])
        own_small.start()
        small_sends, small_recvs = _small_copies(s_ref, rs_ref, small_send, small_recv)
        for cp in small_sends:
            cp.start()
        stage_in = [pltpu.make_async_copy(ha_ref.at[chip], a_vmem, loc_sems.at[0]),
                    pltpu.make_async_copy(hb_ref.at[chip], b_vmem, loc_sems.at[1]),
                    pltpu.make_async_copy(ra_ref, ra_vmem, loc_sems.at[3]),
                    pltpu.make_async_copy(rb_ref, rb_vmem, loc_sems.at[4])]
        for cp in stage_in:
            cp.start()
        for cp in stage_in:
            cp.wait()
        for tot, recv in ((a_vmem, ra_vmem), (b_vmem, rb_vmem)):
            acc = tot[...]
            for j in range(3):
                acc = acc + recv[j].astype(F32)
            tot[...] = acc

        def fill(cc):
            mine = [_part(dst, cc) for _, dst in pairs]
            srcs = [src.at[pl.ds(0, m.shape[0])] for src, m in zip((a_vmem, b_vmem), mine)]
            local = [pltpu.make_async_copy(s, m, loc_sems.at[t]) for t, (s, m) in enumerate(zip(srcs, mine))]
            sends = [pltpu.make_async_remote_copy(src_ref=s, dst_ref=m, send_sem=send_sems.at[t],
                                                  recv_sem=recv_sems.at[t], device_id=sib, device_id_type=MESH_ID)
                     for t, (s, m) in enumerate(zip(srcs, mine))]
            for cp in local + sends:
                cp.start()
            for t, (_, dst) in enumerate(pairs):
                theirs = _part(dst, 1 - cc)
                pltpu.make_async_remote_copy(src_ref=theirs, dst_ref=theirs, send_sem=send_sems.at[t],
                                             recv_sem=recv_sems.at[t], device_id=sib, device_id_type=MESH_ID).wait_recv()
            for cp in sends:
                cp.wait_send()
            for cp in local:
                cp.wait()

        for cc in (0, 1):
            pl.when(c == cc)(functools.partial(fill, cc))
        for cp in small_recvs:
            cp.wait_recv()
        for cp in small_sends:
            cp.wait_send()
        own_small.wait()

    n_peer = len(PEER_FLIPS)
    return pl.pallas_call(
        body, in_specs=[ANY] * 5, out_specs=[ANY] * 3,
        out_shape=[_sds((rows_in, D_MODEL)), _sds((rows_out, D_MODEL)), _sds((N_DEV,) + small.shape, small.dtype)],
        scratch_shapes=[pltpu.SemaphoreType.DMA((2,)), pltpu.SemaphoreType.DMA((2,)), pltpu.SemaphoreType.DMA((5,)),
                        pltpu.SemaphoreType.DMA((n_peer,)), pltpu.SemaphoreType.DMA((n_peer,)),
                        pltpu.VMEM(h_in.shape[1:], F32), pltpu.VMEM(h_out.shape[1:], F32),
                        pltpu.VMEM(r_in.shape, r_in.dtype), pltpu.VMEM(r_out.shape, r_out.dtype)],
        compiler_params=_cp(), name="fill_parts")(h_in, r_in, h_out, r_out, small)


def _chip_sums(u_in, u_out, win_in, win_out):
    wins = (win_in, win_out)

    def body(a_ref, b_ref, ha_ref, hba_ref, hb_ref, hbb_ref, send_sems, recv_sems, loc_sems, out_sems,
             mine_a, theirs_a, sum_a, sumb_a, mine_b, theirs_b, sum_b, sumb_b):
        x, y, c = _position()
        sib = (x, y, 1 - c)
        groups = ((a_ref, mine_a, theirs_a, sum_a, sumb_a, ha_ref, hba_ref, win_in),
                  (b_ref, mine_b, theirs_b, sum_b, sumb_b, hb_ref, hbb_ref, win_out))
        loads, sends = [], []
        for t, (src, mine, theirs, _, _, _, _, win) in enumerate(groups):
            split = _row_split(src.shape[1])
            for k in range(N_CHIPS):
                n = N_CHIPS * t + k
                loads.append(pltpu.make_async_copy(
                    src.at[k, pl.ds(pl.multiple_of(c * split, split), win), :], mine.at[k], loc_sems.at[n]))
                sends.append(pltpu.make_async_remote_copy(
                    src_ref=src.at[k, pl.ds(pl.multiple_of((1 - c) * split, split), win), :], dst_ref=theirs.at[k],
                    send_sem=send_sems.at[n], recv_sem=recv_sems.at[n], device_id=sib, device_id_type=MESH_ID))
        for cp in sends + loads:
            cp.start()
        stores = []
        for t, (_, mine, theirs, tot, totb, h_out, hb_out, _) in enumerate(groups):
            for k in range(N_CHIPS):
                n = N_CHIPS * t + k
                loads[n].wait()
                sends[n].wait_recv()
                val = mine[k] + theirs[k]
                tot[k] = val
                totb[k] = val.astype(BF16)
                stores += [pltpu.make_async_copy(tot.at[k], h_out.at[k], out_sems.at[2 * n]),
                           pltpu.make_async_copy(totb.at[k], hb_out.at[k], out_sems.at[2 * n + 1])]
                stores[-2].start()
                stores[-1].start()
        for cp in sends:
            cp.wait_send()
        for cp in stores:
            cp.wait()

    shapes = [(N_CHIPS, w, D_MODEL) for w in wins]
    n_cp = 2 * N_CHIPS
    vmem = []
    for shp in shapes:
        vmem += [pltpu.VMEM(shp, F32), pltpu.VMEM(shp, F32), pltpu.VMEM(shp, F32), pltpu.VMEM(shp, BF16)]
    return pl.pallas_call(
        body, in_specs=[ANY] * 2, out_specs=[ANY] * 4,
        out_shape=[_sds(shapes[0]), _sds(shapes[0], BF16), _sds(shapes[1]), _sds(shapes[1], BF16)],
        scratch_shapes=[pltpu.SemaphoreType.DMA((n_cp,)), pltpu.SemaphoreType.DMA((n_cp,)),
                        pltpu.SemaphoreType.DMA((n_cp,)), pltpu.SemaphoreType.DMA((2 * n_cp,))] + vmem,
        compiler_params=_cp(), name="chip_sums")(u_in, u_out)


SMALL_LAYOUT = (("norm_w", 1024), ("conv_w", 6144), ("a_log", 128), ("dt_bias", 128), ("dn_norm_w", 128),
                ("q_norm_w", 512), ("k_norm_w", 512), ("rel_bias", 256), ("loss", 128))
SMALL_TOTAL = sum(n for _, n in SMALL_LAYOUT)


def _small_offset(name):
    off = 0
    for n, size in SMALL_LAYOUT:
        if n == name:
            return off
        off += size
    raise KeyError(name)


def _pack_small(grads):
    parts = []
    for name, size in SMALL_LAYOUT:
        flat = grads[name].reshape(1, -1)
        parts.append(jnp.pad(flat, ((0, 0), (0, size - flat.shape[1]))))
    return jnp.concatenate(parts, axis=1)


def _sum_small(rows):
    n_dev = rows.shape[0]
    q_off = _small_offset("q_norm_w")
    k_off = _small_offset("k_norm_w")

    def body(r_ref, tot_ref, qk_ref):
        tot = r_ref[0:1, :]
        for d in range(1, n_dev):
            tot = tot + r_ref[d:d + 1, :]
        tot_ref[...] = tot
        for row, off in ((0, q_off), (1, k_off)):
            s4 = tot[:, off:off + 128] + tot[:, off + 128:off + 256] + tot[:, off + 256:off + 384] \
                + tot[:, off + 384:off + 512]
            qk_ref[row:row + 1, :] = s4 + pltpu.roll(s4, ATT_HD, 1)

    return pl.pallas_call(
        body, in_specs=[pl.BlockSpec(memory_space=pltpu.VMEM)],
        out_specs=[pl.BlockSpec(memory_space=pltpu.VMEM)] * 2,
        out_shape=[_sds((1, SMALL_TOTAL)), _sds((2, 128))],
        compiler_params=_cp(), name="sum_small")(rows)


def _adamw_math(w, g, m, v):
    m = ADAM_B1 * m + (1.0 - ADAM_B1) * g
    v = ADAM_B2 * v + (1.0 - ADAM_B2) * (g * g)
    m_hat = m / (1.0 - ADAM_B1 ** ADAM_STEP)
    v_hat = v / (1.0 - ADAM_B2 ** ADAM_STEP)
    delta = -ADAM_LR * (m_hat / (jnp.sqrt(v_hat) + ADAM_EPS) + ADAM_WD * w)
    return delta, m, v


def _adamw_big(g, w, m, v, name):
    rows, cols = w.shape
    tr = 128

    def body(g_ref, w_ref, m_ref, v_ref, go_ref, d_ref, nm_ref, nv_ref):
        g = g_ref[...]
        go_ref[...] = g
        d_ref[...], nm_ref[...], nv_ref[...] = _adamw_math(w_ref[...], g, m_ref[...], v_ref[...])

    blk = pl.BlockSpec((tr, cols), lambda i: (i, 0))
    return pl.pallas_call(
        body, grid=(pl.cdiv(rows, tr),), in_specs=[blk] * 4, out_specs=[blk] * 4,
        out_shape=[_sds((rows, cols))] * 4, compiler_params=_cp(("parallel",)), name=name)(g, w, m, v)


def _adamw_rows(g, w, m, v, name):
    rows, cols = w.shape
    tr = 128

    def body(g_ref, w_ref, m_ref, v_ref, go_ref, d_ref, nm_ref, nv_ref, s_g, s_d, s_m, s_v):
        g = g_ref[...]
        s_g[...] = g
        s_d[...], s_m[...], s_v[...] = _adamw_math(w_ref[...], g, m_ref[...], v_ref[...])
        for i in range(tr):
            for scr, out in ((s_g, go_ref), (s_d, d_ref), (s_m, nm_ref), (s_v, nv_ref)):
                out[i] = scr[i:i + 1, :]

    blk = pl.BlockSpec((tr, cols), lambda i: (i, 0))
    oblk = pl.BlockSpec((tr, 1, cols), lambda i: (i, 0, 0))
    return pl.pallas_call(
        body, grid=(pl.cdiv(rows, tr),), in_specs=[blk] * 4, out_specs=[oblk] * 4,
        out_shape=[_sds((rows, 1, cols))] * 4, scratch_shapes=[pltpu.VMEM((tr, cols), F32)] * 4,
        compiler_params=_cp(("parallel",)), name=name)(g, w, m, v)


def _adamw_small(w, g, m, v, name):
    def body(w_ref, g_ref, m_ref, v_ref, d_ref, nm_ref, nv_ref):
        d_ref[...], nm_ref[...], nv_ref[...] = _adamw_math(w_ref[...], g_ref[...], m_ref[...], v_ref[...])

    vm = pl.BlockSpec(memory_space=pltpu.VMEM)
    return pl.pallas_call(body, in_specs=[vm] * 4, out_specs=[vm] * 3, out_shape=[_sds(w.shape)] * 3,
                          compiler_params=_cp(), name=name)(w, g, m, v)


WEIGHTS = ("norm_w", "w_in", "conv_w", "a_log", "dt_bias", "dn_norm_w", "q_norm_w", "k_norm_w", "rel_bias", "w_out")


def kernel(x, norm_w, w_in, conv_w, a_log, dt_bias, dn_norm_w, q_norm_w, k_norm_w, rel_bias, w_out, loss_target, m_norm_w, m_w_in, m_conv_w, m_a_log, m_dt_bias, m_dn_norm_w, m_q_norm_w, m_k_norm_w, m_rel_bias, m_w_out, v_norm_w, v_w_in, v_conv_w, v_a_log, v_dt_bias, v_dn_norm_w, v_q_norm_w, v_k_norm_w, v_rel_bias, v_w_out):
    xi, yi, _ = _position()
    chip = 2 * xi + yi
    w_loc = dict(norm_w=norm_w, w_in=w_in[0].T, conv_w=conv_w[0], a_log=a_log, dt_bias=dt_bias, dn_norm_w=dn_norm_w,
                 q_norm_w=q_norm_w, k_norm_w=k_norm_w, rel_bias=rel_bias, w_out=w_out[0])
    m_loc = dict(norm_w=m_norm_w, w_in=m_w_in[0].T, conv_w=m_conv_w[0], a_log=m_a_log, dt_bias=m_dt_bias,
                 dn_norm_w=m_dn_norm_w, q_norm_w=m_q_norm_w, k_norm_w=m_k_norm_w, rel_bias=m_rel_bias,
                 w_out=m_w_out[0])
    v_loc = dict(norm_w=v_norm_w, w_in=v_w_in[0].T, conv_w=v_conv_w[0], a_log=v_a_log, dt_bias=v_dt_bias,
                 dn_norm_w=v_dn_norm_w, q_norm_w=v_q_norm_w, k_norm_w=v_k_norm_w, rel_bias=v_rel_bias,
                 w_out=v_w_out[0])

    wt_pad = jnp.pad(w_loc["w_in"].astype(BF16), ((0, SHARD_PAD - D_SHARD), (0, 0)))
    g_in, g_out, g_conv, bias = _gather_weights(wt_pad, w_loc["w_out"].astype(BF16), w_loc["conv_w"], rel_bias,
                                                _bucket_tables())
    wt_full = _pack_w(g_in)
    w_out_full = g_out.reshape(D_MODEL, D_MODEL)
    conv_full = g_conv.transpose(1, 0, 2).reshape(CONV_W, 3 * D_DN)

    loss_local, last_kernel, grads = _local_step(x[0], loss_target[0], norm_w, wt_full, conv_full, a_log, dt_bias,
                                                 dn_norm_w, q_norm_w, k_norm_w, bias, w_out_full)
    grads["loss"] = loss_local

    u_in = _unpack_w(grads["w_in_t"], SHARD_PAD)
    u_out = grads["w_out"].reshape(N_CHIPS, D_MODEL // N_CHIPS, D_MODEL)
    win_out = u_out.shape[1] // 2
    h_in, hb_in, h_out, hb_out = _chip_sums(u_in, u_out, WIN, win_out)
    grad_x, d_nw8, grads["rel_bias"], r_in, r_out = last_kernel(chip_sums=(hb_in, hb_out))
    grads["norm_w"] = d_nw8[0:1, :]
    small = _pack_small(grads)
    full_in, full_out, r_small = _fill_parts(h_in, r_in, h_out, r_out, small, SHARD_PAD, u_out.shape[1])
    tot_small, qk = _sum_small(r_small.reshape(8, SMALL_TOTAL))

    def small_grad(name, n):
        off = _small_offset(name)
        return tot_small[:, off:off + n]

    loss = small_grad("loss", 1).reshape(())
    conv_all = small_grad("conv_w", CONV_W * 3 * D_DN).reshape(CONV_W, 3 * D_DN)
    g_small = dict(
        norm_w=small_grad("norm_w", D_MODEL),
        conv_w=lax.dynamic_slice_in_dim(conv_all, chip * (3 * D_DN // N_CHIPS), 3 * D_DN // N_CHIPS, axis=1),
        a_log=small_grad("a_log", DN_HEADS),
        dt_bias=small_grad("dt_bias", DN_HEADS),
        dn_norm_w=small_grad("dn_norm_w", DN_HD),
        q_norm_w=qk[0:1, 0:ATT_HD],
        k_norm_w=qk[1:2, 0:ATT_HD],
        rel_bias=small_grad("rel_bias", ATT_HEADS * N_BUCKETS).reshape(ATT_HEADS, N_BUCKETS),
    )

    out_g, out_d, out_m, out_v = {}, {}, {}, {}
    out_g["w_in"], out_d["w_in"], out_m["w_in"], out_v["w_in"] = _adamw_rows(
        full_in, w_loc["w_in"], m_loc["w_in"], v_loc["w_in"], "adamw_w_in")
    out_g["w_out"], out_d["w_out"], out_m["w_out"], out_v["w_out"] = _adamw_big(
        full_out, w_loc["w_out"], m_loc["w_out"], v_loc["w_out"], "adamw_w_out")
    for name in g_small:
        out_g[name] = g_small[name]
        out_d[name], out_m[name], out_v[name] = _adamw_small(w_loc[name], g_small[name], m_loc[name], v_loc[name],
                                                             "adamw_" + name)
    for d in (out_g, out_d, out_m, out_v):
        d["w_in"] = d["w_in"].transpose(1, 2, 0)
        for name in ("conv_w", "w_out"):
            d[name] = d[name][None]
    return (loss, grad_x[None], *[out_g[n] for n in WEIGHTS], *[out_d[n] for n in WEIGHTS],
            *[out_m[n] for n in WEIGHTS], *[out_v[n] for n in WEIGHTS])
```

```python
import functools
import math

import numpy as np
import jax
import jax.numpy as jnp
from jax import lax
from jax.experimental import pallas as pl
from jax.experimental.pallas import tpu as pltpu

F32 = jnp.float32
BF16 = jnp.bfloat16
HI = lax.Precision.HIGHEST

D_MODEL = 1024
D_DN = 512
DN_HEADS = 4
DN_HD = 128
CONV_W = 4
CHUNK = 64
D_ATT = 512
ATT_HEADS = 8
ATT_HD = 64
PATTERNS = ((128, 1), (512, 4), (2048, 16))
N_BUCKETS = 32
MAX_DISTANCE = 2048
D_IN = 4 * D_DN + 2 * DN_HEADS + 4 * D_ATT
D_IN_PAD = 4224
EPS = 1e-6
BLK = 128
NEG = -1e30
N_CHIPS = 4

ADAM_LR = 0.001
ADAM_B1 = 0.9
ADAM_B2 = 0.999
ADAM_EPS = 1e-08
ADAM_WD = 0.01
ADAM_STEP = 10

VMEM_LIMIT = 56 * 1024 * 1024

COL_Z = 3
COL_ATT_Q = 4
COL_ATT_K = 5
COL_ATT_V = 6
COL_GATE = 7
COL_BA_128 = 32


def _cp(sem=None):
    if sem is None:
        return pltpu.CompilerParams(vmem_limit_bytes=VMEM_LIMIT)
    return pltpu.CompilerParams(dimension_semantics=sem, vmem_limit_bytes=VMEM_LIMIT)


def _sds(shape, dtype=F32):
    return jax.ShapeDtypeStruct(shape, dtype)


def _mm(a, b):
    return jnp.dot(a.astype(BF16), b.astype(BF16), preferred_element_type=F32)


def _mm_nt(a, b):
    return lax.dot_general(a.astype(BF16), b.astype(BF16), (((1,), (1,)), ((), ())),
                           preferred_element_type=F32)


def _mm_tn(a, b):
    return lax.dot_general(a.astype(BF16), b.astype(BF16), (((0,), (0,)), ((), ())),
                           preferred_element_type=F32)


def _mmx(a, b):
    return jnp.dot(a, b, precision=HI, preferred_element_type=F32)


def _mmx_nt(a, b):
    return lax.dot_general(a, b, (((1,), (1,)), ((), ())), precision=HI, preferred_element_type=F32)


def _mmx_tn(a, b):
    return lax.dot_general(a, b, (((0,), (0,)), ((), ())), precision=HI, preferred_element_type=F32)


def _dot(a, b):
    return jnp.dot(a, b, preferred_element_type=F32)


def _dot_nt(a, b):
    return lax.dot_general(a, b, (((1,), (1,)), ((), ())), preferred_element_type=F32)


def _dot_tn(a, b):
    return lax.dot_general(a, b, (((0,), (0,)), ((), ())), preferred_element_type=F32)


def _split(a):
    hi = a.astype(BF16)
    return hi, (a - hi.astype(F32)).astype(BF16)


def _mm3(a_s, b_s):
    return _dot(a_s[0], b_s[0]) + _dot(a_s[0], b_s[1]) + _dot(a_s[1], b_s[0])


def _mm3_tn(a_s, b_s):
    return _dot_tn(a_s[0], b_s[0]) + _dot_tn(a_s[0], b_s[1]) + _dot_tn(a_s[1], b_s[0])


def _interleave(gens):
    live = list(gens)
    while live:
        nxt = []
        for g in live:
            try:
                next(g)
                nxt.append(g)
            except StopIteration:
                pass
        live = nxt


def _segsum(x, bd):
    hi = x.astype(BF16)
    r1 = x - hi.astype(F32)
    mid = r1.astype(BF16)
    lo = (r1 - mid.astype(F32)).astype(BF16)
    return (jnp.dot(hi, bd, preferred_element_type=F32) + jnp.dot(mid, bd, preferred_element_type=F32)
            + jnp.dot(lo, bd, preferred_element_type=F32))


def _compact_mat(seg, n=512):
    slot = 128 * seg // n
    src = np.arange(n)[:, None]
    dst = np.arange(128)[None, :]
    return jnp.asarray((src // seg == dst // slot).astype(np.float32), dtype=BF16)


def _seg_mean(x, cm, seg):
    hi, lo = _split(x)
    return (_dot(hi, cm) + _dot(lo, cm)) * (1.0 / seg)


def _seg_expand(c, cm, seg):
    hi, lo = _split(c)
    return (_dot_nt(hi, cm) + _dot_nt(lo, cm)) * (cm.shape[0] / (128.0 * seg))


def _seg_rstd(x, cm, seg):
    return _seg_expand(lax.rsqrt(_seg_mean(x * x, cm, seg) + EPS), cm, seg)


def _sigmoid(x):
    return 1.0 / (1.0 + jnp.exp(-x))


def _silu_grad(x, s):
    return s * (1.0 + x * (1.0 - s))


def _tri_incl():
    i = np.arange(CHUNK)
    return jnp.asarray((i[:, None] >= i[None, :]).astype(np.float32))


def _t5_bucket(dist):
    max_exact = N_BUCKETS // 2
    d = np.maximum(dist, 1).astype(np.float64)
    large = max_exact + (np.log(d / max_exact) / math.log(MAX_DISTANCE / max_exact)
                         * (N_BUCKETS - max_exact)).astype(np.int32)
    large = np.minimum(large, N_BUCKETS - 1)
    return np.where(dist < max_exact, dist, large).astype(np.int32)


def _bucket_tables():
    qi = np.arange(BLK)[:, None]
    kj = np.arange(2 * BLK)[None, :]
    step = qi - kj + BLK
    return jnp.asarray(np.stack([_t5_bucket(np.clip(step, 0, None) * r) for _, r in PATTERNS]))


def _in_proj(x, norm_w, wt_bf, conv_w8, alog_l, dtb_l):
    s = x.shape[0]
    tm = 512

    def body(x_ref, nw_ref, w_ref, cw_ref, al_ref, dt_ref, o_ref, q_ref, k_ref, v_ref, bg_ref, halo_ref):
        i = pl.program_id(0)

        @pl.when(i == 0)
        def _():
            halo_ref[...] = jnp.zeros_like(halo_ref)

        xv = x_ref[...]
        rstd = lax.rsqrt(jnp.mean(xv * xv, axis=-1, keepdims=True) + EPS)
        h = (xv * rstd * nw_ref[...]).astype(BF16)
        o_ref[...] = _dot_nt(h, w_ref[...])
        for c in range(12):
            lanes = slice(128 * c, 128 * c + 128)
            y = _conv_group(o_ref[:, lanes], halo_ref[:, lanes], cw_ref, c)
            sv = y * _sigmoid(y)
            if c < 8:
                rs = lax.rsqrt(jnp.sum(sv * sv, axis=1, keepdims=True) + EPS)
                n = sv * rs
                if c < 4:
                    q_ref[:, lanes] = n * (DN_HD ** -0.5)
                else:
                    k_ref[:, slice(128 * (c - 4), 128 * (c - 3))] = n
            else:
                v_ref[:, slice(128 * (c - 8), 128 * (c - 7))] = sv
        bg_ref[...] = _beta_g(o_ref[:, 128 * COL_BA_128:128 * COL_BA_128 + 128], al_ref[...], dt_ref[...])[0]
        halo_ref[...] = o_ref[tm - 8:tm, 0:3 * D_DN]

    rows = pl.BlockSpec((tm, 512), lambda i: (i, 0))
    return pl.pallas_call(
        body, grid=(s // tm,),
        in_specs=[pl.BlockSpec((tm, D_MODEL), lambda i: (i, 0)),
                  pl.BlockSpec((1, D_MODEL), lambda i: (0, 0)),
                  pl.BlockSpec((D_IN_PAD, D_MODEL), lambda i: (0, 0)),
                  pl.BlockSpec((8, 3 * D_DN), lambda i: (0, 0)),
                  pl.BlockSpec((1, 128), lambda i: (0, 0)), pl.BlockSpec((1, 128), lambda i: (0, 0))],
        out_specs=[pl.BlockSpec((tm, D_IN_PAD), lambda i: (i, 0)), rows, rows, rows,
                   pl.BlockSpec((tm, 128), lambda i: (i, 0))],
        out_shape=[_sds((s, D_IN_PAD))] + [_sds((s, 512))] * 3 + [_sds((s, 128))],
        scratch_shapes=[pltpu.VMEM((8, 3 * D_DN), F32)],
        compiler_params=_cp(("arbitrary",)), name="in_proj")(x, norm_w, wt_bf, conv_w8, alog_l, dtb_l)


def _conv_group(cur, halo, w_ref, c):
    rows = cur.shape[0]
    lanes = slice(128 * c, 128 * c + 128)
    xcat = jnp.concatenate([halo, cur], axis=0)
    y = cur * w_ref[CONV_W - 1:CONV_W, lanes]
    for k in range(1, CONV_W):
        sh = pltpu.roll(xcat, k, 0)[8:8 + rows]
        y = y + sh * w_ref[CONV_W - 1 - k:CONV_W - k, lanes]
    return y


def _beta_g(ba, alog_l, dtb_l):
    lane = lax.broadcasted_iota(jnp.int32, ba.shape, 1)
    sig_b = _sigmoid(ba)
    t = ba + dtb_l
    softplus = jnp.maximum(t, 0.0) + jnp.log(1.0 + jnp.exp(-jnp.abs(t)))
    nega = -jnp.exp(alog_l)
    g = nega * softplus
    out = jnp.where(lane < DN_HEADS, sig_b, jnp.where(lane < 2 * DN_HEADS, g, 0.0))
    return out, lane, sig_b, t, nega, g


CPS = 8
CPS_SCAN = 8
SPLIT_ITERS = 2


def _split3(a):
    hi = a.astype(BF16)
    r1 = a - hi.astype(F32)
    mid = r1.astype(BF16)
    return hi, mid, (r1 - mid.astype(F32)).astype(BF16)


def _lane_select():
    r = np.arange(128)
    return jnp.asarray((r[None, :, None] == np.arange(8)[:, None, None]) * np.ones((1, 1, 128)), dtype=BF16)


def _lane_bcast(a3, sel):
    return _dot(a3[0], sel) + _dot(a3[1], sel) + _dot(a3[2], sel)


def _rowsum_b(z, ones_b):
    hi, lo = _split(z)
    return _dot(hi, ones_b) + _dot(lo, ones_b)


def _chunk_cumsum(bg, bgt, tri):
    return _split3(bg), _split3(_mmx(tri, bg)), _mmx_nt(bgt, tri)


def _chunk_common(bg3, gc3, gc_row, h, sel_ref):
    gcc = _lane_bcast(gc3, sel_ref[DN_HEADS + h])
    beta = _dot(bg3[0], sel_ref[h]) + _dot(bg3[1], sel_ref[h])
    gcr = gc_row[DN_HEADS + h:DN_HEADS + h + 1, :]
    ii = lax.broadcasted_iota(jnp.int32, (CHUNK, CHUNK), 0)
    jj = lax.broadcasted_iota(jnp.int32, (CHUNK, CHUNK), 1)
    incl = ii >= jj
    strict = ii > jj
    decay = jnp.exp(jnp.where(incl, gcc[:, 0:CHUNK] - gcr, NEG))
    gl = gcc[CHUNK - 1:CHUNK, :]
    return gcc, beta, incl, strict, decay, gl


def _dn_prep(qn, kn, v, bg, bgt, tri, sel):
    s = qn.shape[0]
    nc = s // CHUNK

    def body(q_ref, k_ref, v_ref, bg_ref, bgt_ref, tri_ref, sel_ref,
             u_ref, w_ref, qd_ref, kt_ref, attn_ref, t_ref, gl_ref):
        tri_v = tri_ref[...]
        ii = lax.broadcasted_iota(jnp.int32, (CHUNK, CHUNK), 0)
        jj = lax.broadcasted_iota(jnp.int32, (CHUNK, CHUNK), 1)
        eye = (ii == jj).astype(F32)

        def chain(cc, h, bg3, gc3, gc_row):
            rows = slice(CHUNK * cc, CHUNK * cc + CHUNK)
            lanes = slice(128 * h, 128 * h + 128)
            gcc, beta, incl, strict, decay, gl = _chunk_common(bg3, gc3, gc_row, h, sel_ref)
            yield
            q = q_ref[rows, lanes]
            k = k_ref[rows, lanes]
            vv = v_ref[rows, lanes]
            kb = k * beta
            egc = jnp.exp(gcc)
            a_mat = jnp.where(strict, _mm_nt(kb, k) * decay, 0.0)
            attn_ref[cc, h] = jnp.where(incl, _mm_nt(q, k) * decay, 0.0)
            qd_ref[rows, lanes] = (q * egc).astype(BF16)
            kt_ref[rows, lanes] = (k * jnp.exp(gl - gcc)).astype(BF16)
            gl_ref[cc, h] = jnp.exp(gl)
            yield
            p = -a_mat
            t = eye + p
            for it in range(5):
                if it < SPLIT_ITERS:
                    ps = _split(p)
                    p = _mm3(ps, ps)
                    yield
                    t = t + _mm3(_split(t), _split(p))
                else:
                    p = _mm(p, p)
                    yield
                    t = t + _mm(t, p)
                yield
            t_ref[cc, h] = t
            ts = _split(t)
            u_ref[rows, lanes] = _mm3(ts, _split(vv * beta))
            w_ref[rows, lanes] = _mm3(ts, _split(kb * egc)).astype(BF16)

        gens = []
        for cc in range(CPS):
            bgv = bg_ref[CHUNK * cc:CHUNK * cc + CHUNK, :]
            bg3, gc3, gc_row = _chunk_cumsum(bgv, bgt_ref[cc], tri_v)
            gens += [chain(cc, h, bg3, gc3, gc_row) for h in range(DN_HEADS)]
        _interleave(gens)

    rows_step = CPS * CHUNK
    big = pl.BlockSpec((rows_step, 512), lambda n: (n, 0))
    sq = pl.BlockSpec((CPS, DN_HEADS, CHUNK, CHUNK), lambda n: (n, 0, 0, 0))
    return pl.pallas_call(
        body, grid=(nc // CPS,),
        in_specs=[big, big, big, pl.BlockSpec((rows_step, 128), lambda n: (n, 0)),
                  pl.BlockSpec((CPS, 8, CHUNK), lambda n: (n, 0, 0)),
                  pl.BlockSpec((CHUNK, CHUNK), lambda n: (0, 0)),
                  pl.BlockSpec((8, 128, 128), lambda n: (0, 0, 0))],
        out_specs=[big, big, big, big, sq, sq, pl.BlockSpec((CPS, DN_HEADS, 1, 128), lambda n: (n, 0, 0, 0))],
        out_shape=[_sds((s, 512))] + [_sds((s, 512), BF16)] * 3 + [_sds((nc, DN_HEADS, CHUNK, CHUNK))] * 2
        + [_sds((nc, DN_HEADS, 1, 128))],
        compiler_params=_cp(("parallel",)), name="dn_prep")(qn, kn, v, bg, bgt, tri, sel)


def _dn_scan(u, w, qd, kt, attn, gl):
    s = u.shape[0]
    nc = s // CHUNK

    def body(u_ref, w_ref, qd_ref, kt_ref, attn_ref, gl_ref, o_ref, vn_ref, sp_ref, st_ref):
        n = pl.program_id(0)

        @pl.when(n == 0)
        def _():
            st_ref[...] = jnp.zeros_like(st_ref)

        def chain(cc, h):
            rows = slice(CHUNK * cc, CHUNK * cc + CHUNK)
            lanes = slice(128 * h, 128 * h + 128)
            st = st_ref[h]
            sp_ref[cc, h] = st
            stb = st.astype(BF16)
            ws = _dot(w_ref[rows, lanes].astype(BF16), stb)
            qs = _dot(qd_ref[rows, lanes].astype(BF16), stb)
            yield
            vn = u_ref[rows, lanes] - ws
            vnb = vn.astype(BF16)
            vn_ref[rows, lanes] = vnb
            o_ref[rows, lanes] = qs + _dot(attn_ref[cc, h].astype(BF16), vnb)
            st_ref[h] = st * gl_ref[cc, h] + _dot_tn(kt_ref[rows, lanes].astype(BF16), vnb)

        for cc in range(CPS_SCAN):
            _interleave([chain(cc, h) for h in range(DN_HEADS)])

    big = pl.BlockSpec((CPS_SCAN * CHUNK, 512), lambda n: (n, 0))
    return pl.pallas_call(
        body, grid=(nc // CPS_SCAN,),
        in_specs=[big, big, big, big,
                  pl.BlockSpec((CPS_SCAN, DN_HEADS, CHUNK, CHUNK), lambda n: (n, 0, 0, 0)),
                  pl.BlockSpec((CPS_SCAN, DN_HEADS, 1, 128), lambda n: (n, 0, 0, 0))],
        out_specs=[big, big, pl.BlockSpec((CPS_SCAN, DN_HEADS, DN_HD, DN_HD), lambda n: (n, 0, 0, 0))],
        out_shape=[_sds((s, 512)), _sds((s, 512), BF16), _sds((nc, DN_HEADS, DN_HD, DN_HD))],
        scratch_shapes=[pltpu.VMEM((DN_HEADS, DN_HD, DN_HD), F32)],
        compiler_params=_cp(("arbitrary",)), name="dn_scan")(u, w, qd, kt, attn, gl)


R4 = PATTERNS[1][1]
R16 = PATTERNS[2][1]
TM = 256


def _pattern_spec(r, width=512):
    return pl.BlockSpec((r, TM // r, width), lambda i: (0, i, 0))


def _pattern_shape(s, r, dtype=F32, width=512):
    return _sds((r, s // r, width), dtype)


SLABS = pltpu.VMEM((4, TM, 128), F32)

HEAD_SLOT = 128 // ATT_HEADS


def _head_expand():
    src = np.arange(128)[:, None]
    dst = np.arange(512)[None, :]
    return jnp.asarray((src == (dst // ATT_HD) * HEAD_SLOT).astype(np.float32), dtype=BF16)


def _head_compact():
    src = np.arange(512)[:, None]
    dst = np.arange(128)[None, :]
    return jnp.asarray((src // ATT_HD == dst // HEAD_SLOT).astype(np.float32), dtype=BF16)


def _to_patterns(val, dsts, scr):
    for c in range(val.shape[1] // 128):
        lanes = slice(128 * c, 128 * c + 128)
        scr[c] = val[:, lanes]
        for dst_ref, r in dsts:
            for a in range(r):
                dst_ref[a, :, lanes] = scr[c, pl.ds(a, TM // r, stride=r), :].astype(dst_ref.dtype)


def _from_pattern(src_ref, r, scr):
    n_slab = src_ref.shape[2] // 128
    for c in range(n_slab):
        for a in range(r):
            scr[c, pl.ds(a, TM // r, stride=r), :] = src_ref[a, :, 128 * c:128 * c + 128].astype(F32)
    return jnp.concatenate([scr[c] for c in range(n_slab)], axis=1) if n_slab > 1 else scr[0]


def _att_pre(proj, qw_t, kw_t, bd64):
    s = proj.shape[0]

    def body(q_ref, k_ref, v_ref, qw_ref, kw_ref, bd_ref,
             q1_ref, k1_ref, v1_ref, q4_ref, k4_ref, v4_ref, q16_ref, k16_ref, v16_ref, scr):
        bd = bd_ref[...]
        q = q_ref[...]
        k = k_ref[...]
        qn = q * _seg_rstd(q, bd, ATT_HD) * qw_ref[...] * (ATT_HD ** -0.5)
        kn = k * _seg_rstd(k, bd, ATT_HD) * kw_ref[...]
        q1_ref[...] = qn.astype(BF16)
        k1_ref[...] = kn.astype(BF16)
        v1_ref[...] = v_ref[...].astype(BF16)
        _to_patterns(qn, ((q4_ref, R4), (q16_ref, R16)), scr)
        _to_patterns(kn, ((k4_ref, R4), (k16_ref, R16)), scr)
        _to_patterns(v_ref[...], ((v4_ref, R4), (v16_ref, R16)), scr)

    row = pl.BlockSpec((1, 512), lambda i: (0, 0))
    tok = pl.BlockSpec((TM, 512), lambda i: (i, 0))
    return pl.pallas_call(
        body, grid=(s // TM,),
        in_specs=[pl.BlockSpec((TM, 512), lambda i: (i, COL_ATT_Q)),
                  pl.BlockSpec((TM, 512), lambda i: (i, COL_ATT_K)),
                  pl.BlockSpec((TM, 512), lambda i: (i, COL_ATT_V)),
                  row, row, pl.BlockSpec((512, 128), lambda i: (0, 0))],
        out_specs=[tok] * 3 + [_pattern_spec(R4)] * 3 + [_pattern_spec(R16)] * 3,
        out_shape=[_sds((s, 512), BF16)] * 3 + [_pattern_shape(s, R4, BF16)] * 3 + [_pattern_shape(s, R16, BF16)] * 3,
        scratch_shapes=[SLABS],
        compiler_params=_cp(("parallel",)), name="att_pre")(proj, proj, proj, qw_t, kw_t, bd64)


def _bias_table(rb_ref, bk_ref, o_ref):
    for p in range(len(PATTERNS)):
        bk = bk_ref[p]
        for h in range(ATT_HEADS):
            acc = jnp.zeros((BLK, 2 * BLK), F32)
            for b in range(N_BUCKETS):
                acc = jnp.where(bk == b, rb_ref[h, b], acc)
            o_ref[p, h] = acc


def _bias_grad(ds_refs, bk_ref, o_ref):
    for h in range(ATT_HEADS):
        for b in range(N_BUCKETS):
            tot = jnp.zeros((), F32)
            for p, ds_ref in enumerate(ds_refs):
                tot = tot + jnp.sum(jnp.where(bk_ref[p] == b, ds_ref[h], 0.0))
            o_ref[h, b] = tot


QB_FWD = 2
QB_BWD = 4


def _att_masks(has_prev):
    qi = lax.broadcasted_iota(jnp.int32, (BLK, BLK), 0)
    kj = lax.broadcasted_iota(jnp.int32, (BLK, BLK), 1)
    lane = lax.broadcasted_iota(jnp.int32, (BLK, 2 * ATT_HD), 1)
    return jnp.logical_and(kj >= qi, has_prev), kj <= qi, lane < ATT_HD


def _head_lanes(h):
    half = h % 2
    return slice(ATT_HD * h, ATT_HD * h + ATT_HD), slice(ATT_HD * half, ATT_HD * half + ATT_HD)


def _att_scores(qm, kp2, kc2, bias_h, mask_prev, mask_cur):
    s_prev = jnp.where(mask_prev, _dot_nt(qm, kp2) + bias_h[:, :BLK], NEG)
    s_cur = jnp.where(mask_cur, _dot_nt(qm, kc2) + bias_h[:, BLK:], NEG)
    return s_prev, s_cur


def _att_fwd(q, k, v, bias, p_idx, r, name):
    QB = QB_FWD
    s = q.shape[0]
    nblk = s // BLK
    nseq = nblk // r

    def body(q_ref, kp_ref, kc_ref, vp_ref, vc_ref, b_ref, o_ref, lse_ref):
        j = pl.program_id(0)

        def head(h, rows, masks, q2, kp2, kc2, vp2, vc2):
            mask_prev, mask_cur, lo_half = masks
            out_l, pair_l = _head_lanes(h)
            sel = lo_half if h % 2 == 0 else jnp.logical_not(lo_half)
            qm = jnp.where(sel, q2, jnp.zeros_like(q2))
            s_prev, s_cur = _att_scores(qm, kp2, kc2, b_ref[0, h], mask_prev, mask_cur)
            yield
            m = jnp.maximum(jnp.max(s_prev, axis=1, keepdims=True), jnp.max(s_cur, axis=1, keepdims=True))
            p_prev = jnp.exp(s_prev - m)
            p_cur = jnp.exp(s_cur - m)
            l = jnp.sum(p_prev, axis=1, keepdims=True) + jnp.sum(p_cur, axis=1, keepdims=True)
            yield
            o2 = _dot(p_prev.astype(BF16), vp2) + _dot(p_cur.astype(BF16), vc2)
            o_ref[rows, out_l] = (o2 * (1.0 / l))[:, pair_l].astype(BF16)
            lse_ref[rows, HEAD_SLOT * h:HEAD_SLOT * h + HEAD_SLOT] = jnp.broadcast_to(m + jnp.log(l), (BLK, HEAD_SLOT))

        for sub in range(QB):
            rows = slice(BLK * sub, BLK * sub + BLK)
            before = slice(BLK * (sub - 1), BLK * sub)
            masks = _att_masks(((QB * j + sub) % nseq) != 0)
            gens = []
            for pp in range(ATT_HEADS // 2):
                lanes = slice(128 * pp, 128 * pp + 128)
                kp = kp_ref[:, lanes] if sub == 0 else kc_ref[before, lanes]
                vp = vp_ref[:, lanes] if sub == 0 else vc_ref[before, lanes]
                slabs = (q_ref[rows, lanes], kp, kc_ref[rows, lanes], vp, vc_ref[rows, lanes])
                gens += [head(2 * pp, rows, masks, *slabs), head(2 * pp + 1, rows, masks, *slabs)]
            _interleave(gens)

    cur = pl.BlockSpec((QB * BLK, 512), lambda j: (j, 0))
    prev = pl.BlockSpec((BLK, 512), lambda j: (jnp.maximum(QB * j - 1, 0), 0))
    return pl.pallas_call(
        body, grid=(nblk // QB,),
        in_specs=[cur, prev, cur, prev, cur,
                  pl.BlockSpec((1, ATT_HEADS, BLK, 2 * BLK), lambda j: (p_idx, 0, 0, 0))],
        out_specs=[cur, pl.BlockSpec((QB * BLK, 128), lambda j: (j, 0))],
        out_shape=[_sds((s, 512), BF16), _sds((s, 128))],
        compiler_params=_cp(("parallel",)), name=name)(q, k, k, v, v, bias)


def _post_fwd(o_dn, proj, o_pats, lse_pats, dnw_t, bd128):
    s = o_dn.shape[0]

    def body(o_ref, z_ref, gate_ref, o1_ref, o4_ref, o16_ref, s1_ref, s4_ref, s16_ref, wn_ref, bd_ref, ex_ref,
             mixed_ref, oatt_ref, l1_ref, l4_ref, l16_ref, scr_a, scr_b, scr_c, scr_d):
        o = o_ref[...]
        z = z_ref[...]
        rstd = _seg_rstd(o, bd_ref[...], DN_HD)
        y_dn = o * rstd * wn_ref[...] * (z * _sigmoid(z))
        mixed_ref[:, 0:512] = y_dn.astype(BF16)
        lses = (s1_ref[...], _from_pattern(s4_ref, R4, scr_a), _from_pattern(s16_ref, R16, scr_b))
        m = jnp.maximum(jnp.maximum(lses[0], lses[1]), lses[2])
        tot = jnp.exp(lses[0] - m) + jnp.exp(lses[1] - m) + jnp.exp(lses[2] - m)
        big_l = m + jnp.log(tot)
        l1_ref[...] = big_l
        _to_patterns(big_l, ((l4_ref, R4), (l16_ref, R16)), scr_a)
        ex = ex_ref[...]
        outs = (o1_ref[...], _from_pattern(o4_ref, R4, scr_c), _from_pattern(o16_ref, R16, scr_d))
        acc = jnp.zeros_like(o)
        for lse_p, o_p in zip(lses, outs):
            acc = acc + _lane_bcast(_split3(jnp.exp(lse_p - big_l)), ex) * o_p
        gate = gate_ref[...]
        oatt_ref[...] = acc
        mixed_ref[:, 512:1024] = (acc * (gate * _sigmoid(gate))).astype(BF16)

    blk = pl.BlockSpec((TM, 512), lambda i: (i, 0))
    cblk = pl.BlockSpec((TM, 128), lambda i: (i, 0))
    p4, p16 = _pattern_spec(R4), _pattern_spec(R16)
    c4, c16 = _pattern_spec(R4, 128), _pattern_spec(R16, 128)
    return pl.pallas_call(
        body, grid=(s // TM,),
        in_specs=[blk, pl.BlockSpec((TM, 512), lambda i: (i, COL_Z)),
                  pl.BlockSpec((TM, 512), lambda i: (i, COL_GATE)), blk, p4, p16, cblk, c4, c16,
                  pl.BlockSpec((1, 512), lambda i: (0, 0)), pl.BlockSpec((512, 128), lambda i: (0, 0)),
                  pl.BlockSpec((128, 512), lambda i: (0, 0))],
        out_specs=[pl.BlockSpec((TM, D_MODEL), lambda i: (i, 0)), blk, cblk, c4, c16],
        out_shape=[_sds((s, D_MODEL), BF16), _sds((s, 512)), _sds((s, 128)), _pattern_shape(s, R4, F32, 128),
                   _pattern_shape(s, R16, F32, 128)],
        scratch_shapes=[SLABS] * 4,
        compiler_params=_cp(("parallel",)), name="post_fwd")(o_dn, proj, proj, *o_pats, *lse_pats, dnw_t, bd128,
                                                              _head_expand())


def _out_post(x, mixed, w_out_bf, tgt, o_dn, proj, o_att, dnw_t, bd128):
    s = o_dn.shape[0]
    tm = TM

    def body(x_ref, m_ref, w_ref, t_ref, o_ref, z_ref, gate_ref, oatt_ref, wn_ref, bd128_ref, hc_ref,
             dy_ref, loss_ref, dw_ref, do_ref, dz_ref, dgate_ref, doatt_ref, do4_ref, do16_ref, delta_ref, dl4_ref,
             dl16_ref, dnw_ref, scr):
        i = pl.program_id(0)

        @pl.when(i == 0)
        def _():
            loss_ref[...] = jnp.zeros_like(loss_ref)
            dw_ref[...] = jnp.zeros_like(dw_ref)
            dnw_ref[...] = jnp.zeros_like(dnw_ref)

        err = x_ref[...] + _dot(m_ref[...], w_ref[...]) - t_ref[...]
        d_y = err * (1.0 / D_MODEL)
        dy_ref[...] = d_y
        part = 0.5 * jnp.sum(jnp.mean(err * err, axis=-1, keepdims=True), axis=0, keepdims=True)
        loss_ref[...] = loss_ref[...] + part
        dyb = d_y.astype(BF16)
        dw_ref[...] = dw_ref[...] + _dot_tn(m_ref[...], dyb)
        dm = _dot_nt(dyb, w_ref[...])

        bd128v = bd128_ref[...]
        o = o_ref[...]
        z = z_ref[...]
        wn = wn_ref[...]
        dy = dm[:, 0:512]
        rstd = _seg_rstd(o, bd128v, DN_HD)
        nrm = o * rstd
        sz = _sigmoid(z)
        dz_ref[...] = (dy * nrm * wn * _silu_grad(z, sz)).astype(BF16)
        dn = dy * z * sz
        gw = dn * wn
        do_ref[...] = rstd * (gw - nrm * _seg_expand(_seg_mean(gw * nrm, bd128v, DN_HD), bd128v, DN_HD))
        colsum = jnp.sum(dn * nrm, axis=0, keepdims=True)
        fold = colsum[:, 0:128] + colsum[:, 128:256] + colsum[:, 256:384] + colsum[:, 384:512]
        dnw_ref[...] = dnw_ref[...] + fold
        dya = dm[:, 512:1024]
        gate = gate_ref[...]
        oatt = oatt_ref[...]
        sg = _sigmoid(gate)
        dgate_ref[...] = (dya * oatt * _silu_grad(gate, sg)).astype(BF16)
        doa = dya * gate * sg
        doatt_ref[...] = doa.astype(BF16)
        delta = _segsum(doa * oatt, hc_ref[...])
        delta_ref[...] = delta
        _to_patterns(doa, ((do4_ref, R4), (do16_ref, R16)), scr)
        _to_patterns(delta, ((dl4_ref, R4), (dl16_ref, R16)), scr)

    wide = pl.BlockSpec((tm, D_MODEL), lambda i: (i, 0))
    full = pl.BlockSpec((D_MODEL, D_MODEL), lambda i: (0, 0))
    blk = pl.BlockSpec((tm, 512), lambda i: (i, 0))
    cblk = pl.BlockSpec((tm, 128), lambda i: (i, 0))
    p4, p16 = _pattern_spec(R4), _pattern_spec(R16)
    c4, c16 = _pattern_spec(R4, 128), _pattern_spec(R16, 128)
    return pl.pallas_call(
        body, grid=(s // tm,),
        in_specs=[wide, wide, full, wide, blk,
                  pl.BlockSpec((tm, 512), lambda i: (i, COL_Z)), pl.BlockSpec((tm, 512), lambda i: (i, COL_GATE)),
                  blk, pl.BlockSpec((1, 512), lambda i: (0, 0)), pl.BlockSpec((512, 128), lambda i: (0, 0)),
                  pl.BlockSpec((512, 128), lambda i: (0, 0))],
        out_specs=[wide, pl.BlockSpec((8, 128), lambda i: (0, 0)), full,
                   blk, blk, blk, blk, p4, p16, cblk, c4, c16, pl.BlockSpec((8, 128), lambda i: (0, 0))],
        out_shape=[_sds((s, D_MODEL)), _sds((8, 128)), _sds((D_MODEL, D_MODEL)), _sds((s, 512))]
        + [_sds((s, 512), BF16)] * 3 + [_pattern_shape(s, R4, BF16), _pattern_shape(s, R16, BF16), _sds((s, 128)),
                                        _pattern_shape(s, R4, F32, 128), _pattern_shape(s, R16, F32, 128),
                                        _sds((8, 128))],
        scratch_shapes=[SLABS],
        compiler_params=_cp(("arbitrary",)), name="out_post")(x, mixed, w_out_bf, tgt, o_dn, proj, proj, o_att, dnw_t,
                                                               bd128, _head_compact())


def _att_bwd(q, k, v, do, big_l, delta, bias, p_idx, r, name):
    QB = QB_BWD
    s = q.shape[0]
    nblk = s // BLK
    nseq = nblk // r
    nstep = nblk // QB

    def body(q_ref, kp_ref, kc_ref, vp_ref, vc_ref, do_ref, l_ref, dl_ref, b_ref,
             dq_ref, dk_ref, dv_ref, ds_ref, dkc_ref, dvc_ref):
        j = pl.program_id(0)

        @pl.when(j == 0)
        def _():
            dkc_ref[...] = jnp.zeros_like(dkc_ref)
            dvc_ref[...] = jnp.zeros_like(dvc_ref)
            ds_ref[...] = jnp.zeros_like(ds_ref)

        @pl.when(j < nstep)
        def _():
            def head(h, sub, masks, q2, do2, kp2, kc2, vp2, vc2):
                mask_prev, mask_cur, lo_half = masks
                rows = slice(BLK * sub, BLK * sub + BLK)
                out_l, pair_l = _head_lanes(h)
                sel = lo_half if h % 2 == 0 else jnp.logical_not(lo_half)
                qm = jnp.where(sel, q2, jnp.zeros_like(q2))
                dom = jnp.where(sel, do2, jnp.zeros_like(do2))
                s_prev, s_cur = _att_scores(qm, kp2, kc2, b_ref[0, h], mask_prev, mask_cur)
                dp_prev = _dot_nt(dom, vp2)
                dp_cur = _dot_nt(dom, vc2)
                yield
                lh = l_ref[rows, HEAD_SLOT * h:HEAD_SLOT * h + 1]
                dh = dl_ref[rows, HEAD_SLOT * h:HEAD_SLOT * h + 1]
                p_prev = jnp.exp(s_prev - lh)
                p_cur = jnp.exp(s_cur - lh)
                ds_prev = p_prev * (dp_prev - dh)
                ds_cur = p_cur * (dp_cur - dh)
                ds_ref[h, :, 0:BLK] = ds_ref[h, :, 0:BLK] + ds_prev
                ds_ref[h, :, BLK:2 * BLK] = ds_ref[h, :, BLK:2 * BLK] + ds_cur
                dsb_prev, dsb_cur = ds_prev.astype(BF16), ds_cur.astype(BF16)
                pb_prev, pb_cur = p_prev.astype(BF16), p_cur.astype(BF16)
                yield
                dq_ref[rows, out_l] = (_dot(dsb_prev, kp2) + _dot(dsb_cur, kc2))[:, pair_l].astype(BF16)
                dk_prev = _dot_tn(dsb_prev, q2)[:, pair_l]
                dv_prev = _dot_tn(pb_prev, do2)[:, pair_l]
                if sub == 0:
                    last = slice(BLK * (QB - 1), BLK * QB)
                    dk_ref[last, out_l] = (dkc_ref[last, out_l] + dk_prev).astype(BF16)
                    dv_ref[last, out_l] = (dvc_ref[last, out_l] + dv_prev).astype(BF16)
                else:
                    before = slice(BLK * (sub - 1), BLK * sub)
                    dkc_ref[before, out_l] = dkc_ref[before, out_l] + dk_prev
                    dvc_ref[before, out_l] = dvc_ref[before, out_l] + dv_prev
                yield
                dkc_ref[rows, out_l] = _dot_tn(dsb_cur, q2)[:, pair_l]
                dvc_ref[rows, out_l] = _dot_tn(pb_cur, do2)[:, pair_l]

            done = slice(0, BLK * (QB - 1))
            dk_ref[done, :] = dkc_ref[done, :].astype(BF16)
            dv_ref[done, :] = dvc_ref[done, :].astype(BF16)
            for sub in range(QB):
                rows = slice(BLK * sub, BLK * sub + BLK)
                before = slice(BLK * (sub - 1), BLK * sub)
                masks = _att_masks(((QB * j + sub) % nseq) != 0)
                gens = []
                for pp in range(ATT_HEADS // 2):
                    lanes = slice(128 * pp, 128 * pp + 128)
                    kp = kp_ref[:, lanes] if sub == 0 else kc_ref[before, lanes]
                    vp = vp_ref[:, lanes] if sub == 0 else vc_ref[before, lanes]
                    slabs = (q_ref[rows, lanes], do_ref[rows, lanes], kp, kc_ref[rows, lanes], vp,
                             vc_ref[rows, lanes])
                    gens += [head(2 * pp, sub, masks, *slabs), head(2 * pp + 1, sub, masks, *slabs)]
                _interleave(gens)

        @pl.when(j == nstep)
        def _():
            dk_ref[...] = dkc_ref[...].astype(BF16)
            dv_ref[...] = dvc_ref[...].astype(BF16)

    last_step = nstep - 1
    cur = pl.BlockSpec((QB * BLK, 512), lambda j: (jnp.minimum(j, last_step), 0))
    compact = pl.BlockSpec((QB * BLK, 128), lambda j: (jnp.minimum(j, last_step), 0))
    lag = pl.BlockSpec((QB * BLK, 512), lambda j: (jnp.clip(j - 1, 0, last_step), 0))
    prev = pl.BlockSpec((BLK, 512), lambda j: (jnp.clip(QB * j - 1, 0, nblk - 1), 0))
    return pl.pallas_call(
        body, grid=(nstep + 1,),
        in_specs=[cur, prev, cur, prev, cur, cur, compact, compact,
                  pl.BlockSpec((1, ATT_HEADS, BLK, 2 * BLK), lambda j: (p_idx, 0, 0, 0))],
        out_specs=[cur, lag, lag, pl.BlockSpec((ATT_HEADS, BLK, 2 * BLK), lambda j: (0, 0, 0))],
        out_shape=[_sds((s, 512), BF16)] * 3 + [_sds((ATT_HEADS, BLK, 2 * BLK))],
        scratch_shapes=[pltpu.VMEM((QB * BLK, 512), F32), pltpu.VMEM((QB * BLK, 512), F32)],
        compiler_params=_cp(("arbitrary",)), name=name)(q, k, k, v, v, do, big_l, delta, bias)


def _att_pre_bwd(dq_pats, dk_pats, dv_pats, proj, qw_t, kw_t, bd64):
    s = proj.shape[0]
    tm = TM

    def body(dq1_ref, dq4_ref, dq16_ref, dk1_ref, dk4_ref, dk16_ref, dv1_ref, dv4_ref, dv16_ref,
             q_ref, k_ref, qw_ref, kw_ref, bd_ref,
             dqr_ref, dkr_ref, dvr_ref, dqw_ref, dkw_ref, scr4, scr16):
        i = pl.program_id(0)

        @pl.when(i == 0)
        def _():
            dqw_ref[...] = jnp.zeros_like(dqw_ref)
            dkw_ref[...] = jnp.zeros_like(dkw_ref)

        bd = bd_ref[...]

        def total(d1_ref, d4_ref, d16_ref):
            return d1_ref[...] + _from_pattern(d4_ref, R4, scr4) + _from_pattern(d16_ref, R16, scr16)

        def one(d_refs, x_ref, w_ref, scale, dx_ref, dw_ref):
            dy = total(*d_refs) * scale
            x = x_ref[...]
            rstd = _seg_rstd(x, bd, ATT_HD)
            nrm = x * rstd
            dw_ref[...] = dw_ref[...] + jnp.sum(dy * nrm, axis=0, keepdims=True)
            g = dy * w_ref[...]
            dx_ref[...] = (rstd * (g - nrm * _seg_expand(_seg_mean(g * nrm, bd, ATT_HD), bd, ATT_HD))).astype(BF16)

        one((dq1_ref, dq4_ref, dq16_ref), q_ref, qw_ref, ATT_HD ** -0.5, dqr_ref, dqw_ref)
        one((dk1_ref, dk4_ref, dk16_ref), k_ref, kw_ref, 1.0, dkr_ref, dkw_ref)
        dvr_ref[...] = total(dv1_ref, dv4_ref, dv16_ref).astype(BF16)

    blk = pl.BlockSpec((tm, 512), lambda i: (i, 0))
    pats = [blk, _pattern_spec(R4), _pattern_spec(R16)]
    row = pl.BlockSpec((1, 512), lambda i: (0, 0))
    acc = pl.BlockSpec((8, 512), lambda i: (0, 0))
    return pl.pallas_call(
        body, grid=(s // tm,),
        in_specs=pats * 3 + [pl.BlockSpec((tm, 512), lambda i: (i, COL_ATT_Q)),
                             pl.BlockSpec((tm, 512), lambda i: (i, COL_ATT_K)), row, row,
                             pl.BlockSpec((512, 128), lambda i: (0, 0))],
        out_specs=[blk, blk, blk, acc, acc],
        out_shape=[_sds((s, 512), BF16)] * 3 + [_sds((8, 512))] * 2,
        scratch_shapes=[SLABS] * 2,
        compiler_params=_cp(("arbitrary",)), name="att_pre_bwd")(*dq_pats, *dk_pats, *dv_pats, proj, proj,
                                                                  qw_t, kw_t, bd64)


def _dn_scan_bwd(do, sp, qd, kt, w, vn, attn, gl):
    s = do.shape[0]
    nc = s // CHUNK

    def body(do_ref, sp_ref, qd_ref, kt_ref, w_ref, vn_ref, attn_ref, gl_ref,
             du_ref, dqd_ref, dkt_ref, dw_ref, dattn_ref, dgl_ref, ds_ref):
        n = pl.program_id(0)

        @pl.when(n == 0)
        def _():
            ds_ref[...] = jnp.zeros_like(ds_ref)

        def chain(cc, h):
            rows = slice(CHUNK * cc, CHUNK * cc + CHUNK)
            lanes = slice(128 * h, 128 * h + 128)
            dsn = ds_ref[h]
            st = sp_ref[cc, h]
            dsb, stb = dsn.astype(BF16), st.astype(BF16)
            dob = do_ref[rows, lanes].astype(BF16)
            vnb = vn_ref[rows, lanes].astype(BF16)
            dvn = _dot_tn(attn_ref[cc, h].astype(BF16), dob) + _dot(kt_ref[rows, lanes].astype(BF16), dsb)
            du_ref[rows, lanes] = dvn
            dqd_ref[rows, lanes] = _dot_nt(dob, stb)
            dattn_ref[cc, h] = _dot_nt(dob, vnb)
            dkt_ref[rows, lanes] = _dot_nt(vnb, dsb)
            tot = jnp.sum(jnp.sum(st * dsn, axis=1, keepdims=True), axis=0, keepdims=True)
            dgl_ref[cc, h] = jnp.broadcast_to(tot, (1, 128))
            qdo = _dot_tn(qd_ref[rows, lanes].astype(BF16), dob)
            yield
            dvb = dvn.astype(BF16)
            dw_ref[rows, lanes] = -_dot_nt(dvb, stb)
            ds_ref[h] = qdo + dsn * gl_ref[cc, h] - _dot_tn(w_ref[rows, lanes].astype(BF16), dvb)

        for cc in reversed(range(CPS_SCAN)):
            _interleave([chain(cc, h) for h in range(DN_HEADS)])

    nsteps = nc // CPS_SCAN
    big = pl.BlockSpec((CPS_SCAN * CHUNK, 512), lambda n: (nsteps - 1 - n, 0))
    sq = pl.BlockSpec((CPS_SCAN, DN_HEADS, CHUNK, CHUNK), lambda n: (nsteps - 1 - n, 0, 0, 0))
    glb = pl.BlockSpec((CPS_SCAN, DN_HEADS, 1, 128), lambda n: (nsteps - 1 - n, 0, 0, 0))
    return pl.pallas_call(
        body, grid=(nsteps,),
        in_specs=[big, pl.BlockSpec((CPS_SCAN, DN_HEADS, DN_HD, DN_HD), lambda n: (nsteps - 1 - n, 0, 0, 0)),
                  big, big, big, big, sq, glb],
        out_specs=[big, big, big, big, sq, glb],
        out_shape=[_sds((s, 512))] * 4 + [_sds((nc, DN_HEADS, CHUNK, CHUNK)), _sds((nc, DN_HEADS, 1, 128))],
        scratch_shapes=[pltpu.VMEM((DN_HEADS, DN_HD, DN_HD), F32)],
        compiler_params=_cp(("arbitrary",)), name="dn_scan_bwd")(do, sp, qd, kt, w, vn, attn, gl)


def _dn_prep_bwd(qn, kn, v, bg, bgt, tri, sel, t_inv, attn, u, w, du, dw, dqd, dkt, dattn, dgl):
    s = qn.shape[0]
    nc = s // CHUNK

    def body(q_ref, k_ref, v_ref, bg_ref, bgt_ref, tri_ref, sel_ref, t_ref, attn_ref, u_ref, w_ref,
             du_ref, dw_ref, dqd_ref, dkt_ref, dattn_ref, dgl_ref,
             dq_ref, dk_ref, dv_ref, dbg_ref):
        tri_v = tri_ref[...]
        lane = lax.broadcasted_iota(jnp.int32, (CHUNK, 128), 1)
        rowi = lax.broadcasted_iota(jnp.int32, (CHUNK, 128), 0)
        ones_b = jnp.ones((CHUNK, 128), BF16)
        ones_sq = jnp.ones((128, 128), BF16)
        parts = [[] for _ in range(CPS)]

        def chain(cc, h, bg3, gc3, gc_row):
            rows = slice(CHUNK * cc, CHUNK * cc + CHUNK)
            lanes = slice(128 * h, 128 * h + 128)
            gcc, beta, incl, strict, decay, gl = _chunk_common(bg3, gc3, gc_row, h, sel_ref)
            yield
            q = q_ref[rows, lanes]
            k = k_ref[rows, lanes]
            vv = v_ref[rows, lanes]
            ts = _split(t_ref[cc, h])
            egc = jnp.exp(gcc)
            kb = k * beta
            a_mat = jnp.where(strict, _mm_nt(kb, k) * decay, 0.0)
            dvb = _mm3_tn(ts, _split(du_ref[rows, lanes]))
            dkbg = _mm3_tn(ts, _split(dw_ref[rows, lanes]))
            yield
            d_a = jnp.where(strict, -(_mm_nt(dvb, u_ref[rows, lanes]) + _mm_nt(dkbg, w_ref[rows, lanes])), 0.0)
            d_m = d_a * decay
            dattn_m = jnp.where(incl, dattn_ref[cc, h], 0.0)
            dqk = dattn_m * decay
            e_hi, e_lo = _split(d_a * a_mat + dattn_m * attn_ref[cc, h])
            yield
            dkb = _mm(d_m, k)
            dk = _mm_tn(d_m, kb) + _mm_tn(dqk, q)
            dq = _mm(dqk, k)
            e_colsum = _dot_tn(e_hi, ones_b) + _dot_tn(e_lo, ones_b)
            e_rowsum = _dot(e_hi, ones_b) + _dot(e_lo, ones_b)
            dqd = dqd_ref[rows, lanes]
            dkt = dkt_ref[rows, lanes]
            sums = _rowsum_b(jnp.concatenate([dqd * q, dkt * k, dkbg * k, dkb * k, dvb * vv], axis=0), ones_sq)
            s_dqd, s_dkt, rk, s_dkb, s_dvb = (sums[CHUNK * n:CHUNK * n + CHUNK] for n in range(5))
            yield
            tail = jnp.exp(gl - gcc)
            r = s_dkt * tail
            dgl_tot = jnp.sum(r, axis=0, keepdims=True) + dgl_ref[cc, h] * jnp.exp(gl)
            dgc = e_rowsum - e_colsum + s_dqd * egc - r + rk * beta * egc
            dgc = dgc + jnp.where(rowi == CHUNK - 1, dgl_tot, 0.0)
            dq_ref[rows, lanes] = dq + dqd * egc
            dk_ref[rows, lanes] = dk + dkt * tail + dkbg * (beta * egc) + dkb * beta
            dv_ref[rows, lanes] = dvb * beta
            parts[cc].append((h, dgc, rk * egc + s_dkb + s_dvb))

        gens = []
        for cc in range(CPS):
            bgv = bg_ref[CHUNK * cc:CHUNK * cc + CHUNK, :]
            bg3, gc3, gc_row = _chunk_cumsum(bgv, bgt_ref[cc], tri_v)
            gens += [chain(cc, h, bg3, gc3, gc_row) for h in range(DN_HEADS)]
        _interleave(gens)
        for cc in range(CPS):
            dgc_mat = jnp.zeros((CHUNK, 128), F32)
            dbeta_mat = jnp.zeros((CHUNK, 128), F32)
            for h, dgc, dbeta in parts[cc]:
                dgc_mat = dgc_mat + jnp.where(lane == DN_HEADS + h, dgc, 0.0)
                dbeta_mat = dbeta_mat + jnp.where(lane == h, dbeta, 0.0)
            dbg_ref[CHUNK * cc:CHUNK * cc + CHUNK, :] = _mmx_tn(tri_v, dgc_mat) + dbeta_mat

    big = pl.BlockSpec((CPS * CHUNK, 512), lambda n: (n, 0))
    sq = pl.BlockSpec((CPS, DN_HEADS, CHUNK, CHUNK), lambda n: (n, 0, 0, 0))
    glb = pl.BlockSpec((CPS, DN_HEADS, 1, 128), lambda n: (n, 0, 0, 0))
    small = pl.BlockSpec((CPS * CHUNK, 128), lambda n: (n, 0))
    return pl.pallas_call(
        body, grid=(nc // CPS,),
        in_specs=[big, big, big, small, pl.BlockSpec((CPS, 8, CHUNK), lambda n: (n, 0, 0)),
                  pl.BlockSpec((CHUNK, CHUNK), lambda n: (0, 0)),
                  pl.BlockSpec((8, 128, 128), lambda n: (0, 0, 0)), sq, sq, big, big,
                  big, big, big, big, sq, glb],
        out_specs=[big, big, big, small],
        out_shape=[_sds((s, 512))] * 3 + [_sds((s, 128))],
        compiler_params=_cp(("parallel",)), name="dn_prep_bwd")(qn, kn, v, bg, bgt, tri, sel, t_inv, attn, u, w,
                                                                  du, dw, dqd, dkt, dattn, dgl)


def _dn_pre_bwd(dqn, dkn, dv, dbg, proj, conv_w8, alog_l, dtb_l):
    s = proj.shape[0]
    tr = 512
    nh = tr // 8
    nblk = s // tr

    def body(dq_ref, dk_ref, dv_ref, dbg_ref, u_ref, halo_ref, ba_ref, w_ref, al_ref, dt_ref,
             du_ref, dba_ref, dsm_ref, dw_ref, nxt_ref):
        i = pl.program_id(0)

        @pl.when(i == 0)
        def _():
            dsm_ref[...] = jnp.zeros_like(dsm_ref)
            dw_ref[...] = jnp.zeros_like(dw_ref)
            nxt_ref[...] = jnp.zeros_like(nxt_ref)

        keep = (i < nblk - 1).astype(F32)
        row8 = lax.broadcasted_iota(jnp.int32, (8, 128), 0)
        for c in range(12):
            lanes = slice(128 * c, 128 * c + 128)
            u = u_ref[:, lanes]
            xcat = jnp.concatenate([halo_ref[:, lanes] * keep, u], axis=0)
            shifted = [u] + [pltpu.roll(xcat, k, 0)[8:8 + tr] for k in range(1, CONV_W)]
            y = shifted[0] * w_ref[CONV_W - 1:CONV_W, lanes]
            for k in range(1, CONV_W):
                y = y + shifted[k] * w_ref[CONV_W - 1 - k:CONV_W - k, lanes]
            sg = _sigmoid(y)
            sv = y * sg
            if c < 8:
                rs = lax.rsqrt(jnp.sum(sv * sv, axis=1, keepdims=True) + EPS)
                n = sv * rs
                if c < 4:
                    dn = dq_ref[:, lanes] * (DN_HD ** -0.5)
                else:
                    dn = dk_ref[:, slice(128 * (c - 4), 128 * (c - 3))]
                dsv = rs * (dn - n * jnp.sum(dn * n, axis=1, keepdims=True))
            else:
                dsv = dv_ref[:, slice(128 * (c - 8), 128 * (c - 7))]
            dyc = dsv * _silu_grad(y, sg)
            dcat = jnp.concatenate([dyc, nxt_ref[:, lanes]], axis=0)
            du = dyc * w_ref[CONV_W - 1:CONV_W, lanes]
            dwc = jnp.zeros((8, 128), F32)
            for k in range(CONV_W):
                if k:
                    du = du + pltpu.roll(dcat, tr + 8 - k, 0)[0:tr] * w_ref[CONV_W - 1 - k:CONV_W - k, lanes]
                dwc = dwc + jnp.where(row8 == CONV_W - 1 - k, jnp.sum(dyc * shifted[k], axis=0, keepdims=True), 0.0)
            du_ref[:, lanes] = du.astype(BF16)
            dw_ref[:, lanes] = dw_ref[:, lanes] + dwc
            nxt_ref[:, lanes] = dyc[0:8]
        _, lane, sig_b, t, nega, g = _beta_g(ba_ref[...], al_ref[...], dt_ref[...])
        dbg = dbg_ref[...]
        da = dbg * nega * _sigmoid(t)
        is_b = lane < DN_HEADS
        is_a = jnp.logical_and(lane >= DN_HEADS, lane < 2 * DN_HEADS)
        dba_ref[...] = jnp.where(is_b, dbg * sig_b * (1.0 - sig_b), jnp.where(is_a, da, 0.0)).astype(BF16)
        d_alog = jnp.sum(jnp.where(is_a, dbg * g, 0.0), axis=0, keepdims=True)
        d_dtb = jnp.sum(jnp.where(is_a, da, 0.0), axis=0, keepdims=True)
        dsm_ref[...] = dsm_ref[...] + jnp.where(row8 == 0, d_alog, jnp.where(row8 == 1, d_dtb, 0.0))

    last = nblk - 1
    blk = pl.BlockSpec((tr, 512), lambda i: (last - i, 0))
    wide = pl.BlockSpec((tr, 1536), lambda i: (last - i, 0))
    return pl.pallas_call(
        body, grid=(nblk,),
        in_specs=[blk, blk, blk, pl.BlockSpec((tr, 128), lambda i: (last - i, 0)), wide,
                  pl.BlockSpec((8, 1536), lambda i: (jnp.maximum((last - i) * nh - 1, 0), 0)),
                  pl.BlockSpec((tr, 128), lambda i: (last - i, COL_BA_128)),
                  pl.BlockSpec((8, 1536), lambda i: (0, 0)),
                  pl.BlockSpec((1, 128), lambda i: (0, 0)), pl.BlockSpec((1, 128), lambda i: (0, 0))],
        out_specs=[wide, pl.BlockSpec((tr, 128), lambda i: (last - i, 0)),
                   pl.BlockSpec((8, 128), lambda i: (0, 0)), pl.BlockSpec((8, 1536), lambda i: (0, 0))],
        out_shape=[_sds((s, 1536), BF16), _sds((s, 128), BF16), _sds((8, 128)), _sds((8, 1536))],
        scratch_shapes=[pltpu.VMEM((8, 1536), F32)],
        compiler_params=_cp(("arbitrary",)), name="dn_pre_bwd")(dqn, dkn, dv, dbg, proj, proj, proj, conv_w8,
                                                                 alog_l, dtb_l)


PIECE_WIDTHS = (1536, 512, 512, 512, 512, 512, 128)


def _in_bwd_dx(pieces, w_bf, x, dy, norm_w, ds_accs, buckets, chip_sums=None):
    s = x.shape[0]
    tm = 512
    n_piece = len(PIECE_WIDTHS)
    n_step = s // tm
    n_arr = 0 if chip_sums is None else len(chip_sums)
    n_sent = 3 * n_arr

    def body(*refs):
        refs = list(refs)

        def take(n):
            return [refs.pop(0) for _ in range(n)]

        piece_refs = take(n_piece)
        w_ref, x_ref, dy_ref, nw_ref = take(4)
        ds_refs = take(len(ds_accs))
        bk_ref, = take(1)
        hb_refs = take(n_arr)
        dx_ref, dnw_ref, drb_ref = take(3)
        recv_refs = take(n_arr)
        i = pl.program_id(0)

        def copies():
            send_sems, recv_sems = refs
            xi, yi, ci = _position()
            out = []
            for j, (px, py) in enumerate(_other_chips(xi, yi)):
                for t, (src, dst) in enumerate(zip(hb_refs, recv_refs)):
                    k = n_arr * j + t
                    out.append(pltpu.make_async_remote_copy(
                        src_ref=src.at[2 * px + py], dst_ref=dst.at[j], send_sem=send_sems.at[k],
                        recv_sem=recv_sems.at[k], device_id=(px, py, ci), device_id_type=MESH_ID))
            return out

        @pl.when(i == 0)
        def _():
            dnw_ref[...] = jnp.zeros_like(dnw_ref)
            for cp in (copies() if n_sent else []):
                cp.start()

        @pl.when(i == 1)
        def _():
            _bias_grad(ds_refs, bk_ref, drb_ref)

        dp = jnp.concatenate([r[...] for r in piece_refs], axis=1)
        dh = _dot(dp, w_ref[...])
        xv = x_ref[...]
        rstd = lax.rsqrt(jnp.mean(xv * xv, axis=-1, keepdims=True) + EPS)
        xh = xv * rstd
        dnw_ref[...] = dnw_ref[...] + jnp.sum(dh * xh, axis=0, keepdims=True)
        g = dh * nw_ref[...]
        dx_ref[...] = rstd * (g - xh * jnp.mean(g * xh, axis=-1, keepdims=True)) + dy_ref[...]

        if n_sent:
            @pl.when(i == n_step - 1)
            def _():
                cps = copies()
                for cp in cps:
                    cp.wait_recv()
                for cp in cps:
                    cp.wait_send()

    def blk(n):
        return pl.BlockSpec((tm, n), lambda i: (i, 0))

    vm = pl.BlockSpec(memory_space=pltpu.VMEM)
    in_specs = [blk(n) for n in PIECE_WIDTHS] + [pl.BlockSpec((D_IN_PAD, D_MODEL), lambda i: (0, 0)), blk(D_MODEL),
                                                blk(D_MODEL), pl.BlockSpec((1, D_MODEL), lambda i: (0, 0))]
    in_specs += [vm] * (len(ds_accs) + 1)
    out_specs = [blk(D_MODEL), pl.BlockSpec((8, D_MODEL), lambda i: (0, 0)), pl.BlockSpec(memory_space=pltpu.SMEM)]
    out_shape = [_sds((s, D_MODEL)), _sds((8, D_MODEL)), _sds((ATT_HEADS, N_BUCKETS))]
    scratch = []
    extra = ()
    if n_sent:
        extra = tuple(chip_sums)
        in_specs += [ANY] * n_arr
        out_specs += [ANY] * n_arr
        out_shape += [_sds((3,) + a.shape[1:], a.dtype) for a in extra]
        scratch = [pltpu.SemaphoreType.DMA((n_sent,)), pltpu.SemaphoreType.DMA((n_sent,))]
    return pl.pallas_call(
        body, grid=(n_step,), in_specs=in_specs, out_specs=out_specs, out_shape=out_shape, scratch_shapes=scratch,
        compiler_params=_cp(("arbitrary",)), name="in_bwd_dx")(*pieces, w_bf, x, dy, norm_w, *ds_accs, buckets, *extra)


def _in_bwd_dw(pieces, x, norm_w):
    s = x.shape[0]
    tm = 512
    n_piece = len(PIECE_WIDTHS)

    def body(*refs):
        piece_refs = refs[:n_piece]
        x_ref, nw_ref, dw_ref = refs[n_piece:]
        i = pl.program_id(0)

        @pl.when(i == 0)
        def _():
            dw_ref[...] = jnp.zeros_like(dw_ref)

        xv = x_ref[...]
        rstd = lax.rsqrt(jnp.mean(xv * xv, axis=-1, keepdims=True) + EPS)
        h = (xv * rstd * nw_ref[...]).astype(BF16)
        at = 0
        for r, width in zip(piece_refs, PIECE_WIDTHS):
            dw_ref[at:at + width, :] = dw_ref[at:at + width, :] + _dot_tn(r[...], h)
            at += width

    return pl.pallas_call(
        body, grid=(s // tm,),
        in_specs=[pl.BlockSpec((tm, n), lambda i: (i, 0)) for n in PIECE_WIDTHS]
        + [pl.BlockSpec((tm, D_MODEL), lambda i: (i, 0)), pl.BlockSpec((1, D_MODEL), lambda i: (0, 0))],
        out_specs=pl.BlockSpec((D_IN_PAD, D_MODEL), lambda i: (0, 0)),
        out_shape=_sds((D_IN_PAD, D_MODEL)),
        compiler_params=_cp(("arbitrary",)), name="in_bwd_dw")(*pieces, x, norm_w)


def _flat(a):
    return a.reshape(-1, a.shape[-1])


def _as_pattern(a, r):
    return a if r == 1 else a.reshape(r, a.shape[0] // r, a.shape[1])


D_SHARD = D_IN // N_CHIPS
BA_START = 4 * D_DN
BA_PACKED = 4096
S1_HEAD = BA_START - D_SHARD
S1_BA = 2 * D_SHARD - BA_START


def _pack_w(g):
    n = g.shape[2]

    def body(g_ref, o_ref):
        g32 = g_ref.bitcast(jnp.uint32)
        o32 = o_ref.bitcast(jnp.uint32)
        full, head, ba1 = D_SHARD // 2, S1_HEAD // 2, S1_BA // 2
        ba2 = DN_HEADS - ba1
        pieces = [(0, 0, full), (1, 0, head), (2, ba2, full), (3, 0, full), (1, head, full), (2, 0, ba2)]
        at = 0
        for chip, lo, hi in pieces:
            o32[at:at + hi - lo, :] = g32[chip, lo:hi, :]
            at += hi - lo
        o32[at:D_IN_PAD // 2, :] = jnp.zeros((D_IN_PAD // 2 - at, n), jnp.uint32)

    vm = pl.BlockSpec(memory_space=pltpu.VMEM)
    return pl.pallas_call(body, in_specs=[vm], out_specs=vm, out_shape=_sds((D_IN_PAD, n), g.dtype),
                          compiler_params=_cp(), name="pack_w")(g)


def _unpack_w(p, rows):
    n = p.shape[1]
    tn = 256
    mid = BA_PACKED + S1_BA
    pieces = [(0, 0, (0, D_SHARD)), (1, 0, (D_SHARD, BA_START)), (1, S1_HEAD, (BA_PACKED, mid)),
              (2, 0, (mid, BA_PACKED + 2 * DN_HEADS)), (2, 2 * DN_HEADS - S1_BA, (BA_START, BA_START + S1_HEAD)),
              (3, 0, (BA_START + S1_HEAD, BA_PACKED))]

    def body(p_ref, o_ref):
        for chip, at, (lo, hi) in pieces:
            o_ref[chip, at:at + hi - lo, :] = p_ref[lo:hi, :]
        for chip in range(N_CHIPS):
            o_ref[chip, D_SHARD:rows, :] = jnp.zeros((rows - D_SHARD, tn), p.dtype)

    return pl.pallas_call(
        body, grid=(n // tn,),
        in_specs=[pl.BlockSpec((D_IN_PAD, tn), lambda j: (0, j))],
        out_specs=pl.BlockSpec((N_CHIPS, rows, tn), lambda j: (0, 0, j)),
        out_shape=_sds((N_CHIPS, rows, n), p.dtype),
        compiler_params=_cp(("parallel",)), name="unpack_w")(p)


def _lane_row(vec, offset):
    return jnp.pad(vec.reshape(1, -1), ((0, 0), (offset, 128 - offset - vec.shape[0])))


def _local_step(x, tgt, norm_w, w_bf, conv_w, a_log, dt_bias, dn_norm_w, q_norm_w, k_norm_w, bias, w_out_bf):
    s = x.shape[0]
    nc = s // CHUNK
    conv_w8 = jnp.pad(conv_w, ((0, 8 - CONV_W), (0, 0)))
    alog_l = _lane_row(a_log.reshape(-1), DN_HEADS)
    dtb_l = _lane_row(dt_bias.reshape(-1), DN_HEADS)
    dnw_t = jnp.tile(dn_norm_w.reshape(1, DN_HD), (1, DN_HEADS))
    qw_t = jnp.tile(q_norm_w.reshape(1, ATT_HD), (1, ATT_HEADS))
    kw_t = jnp.tile(k_norm_w.reshape(1, ATT_HD), (1, ATT_HEADS))
    bd128 = _compact_mat(DN_HD)
    bd64 = _compact_mat(ATT_HD)
    tri = _tri_incl()
    sel = _lane_select()
    buckets = _bucket_tables()

    proj, qn, kn, v_dn, bg = _in_proj(x, norm_w, w_bf, conv_w8, alog_l, dtb_l)
    bgt = bg[:, 0:8].reshape(nc, CHUNK, 8).transpose(0, 2, 1)
    u, w, qd, kt, attn, t_inv, gl = _dn_prep(qn, kn, v_dn, bg, bgt, tri, sel)
    o_dn, vn, sp = _dn_scan(u, w, qd, kt, attn, gl)
    q1, k1, v1, q4, k4, v4, q16, k16, v16 = _att_pre(proj, qw_t, kw_t, bd64)
    rs = [r for _, r in PATTERNS]
    qkv = [(q1, k1, v1), (_flat(q4), _flat(k4), _flat(v4)), (_flat(q16), _flat(k16), _flat(v16))]
    o_pats, lse_pats = [], []
    for p, r in enumerate(rs):
        o_p, lse_p = _att_fwd(*qkv[p], bias, p, r, "att_fwd_r%d" % r)
        o_pats.append(_as_pattern(o_p, r))
        lse_pats.append(_as_pattern(lse_p, r))
    mixed, o_att, l1, l4, l16 = _post_fwd(o_dn, proj, o_pats, lse_pats, dnw_t, bd128)

    (dy, loss_blk, d_w_out, do_dn, dz, dgate, do1, do4, do16, dl1, dl4, dl16,
     d_dnw) = _out_post(x, mixed, w_out_bf, tgt, o_dn, proj, o_att, dnw_t, bd128)
    side = [(do1, l1, dl1), (_flat(do4), _flat(l4), _flat(dl4)), (_flat(do16), _flat(l16), _flat(dl16))]
    dq_pats, dk_pats, dv_pats, ds_accs = [], [], [], []
    for p, r in enumerate(rs):
        dq_p, dk_p, dv_p, ds_p = _att_bwd(*qkv[p], *side[p], bias, p, r, "att_bwd_r%d" % r)
        dq_pats.append(_as_pattern(dq_p, r))
        dk_pats.append(_as_pattern(dk_p, r))
        dv_pats.append(_as_pattern(dv_p, r))
        ds_accs.append(ds_p)
    dq_att, dk_att, dv_att, d_qw, d_kw = _att_pre_bwd(dq_pats, dk_pats, dv_pats, proj, qw_t, kw_t, bd64)
    du, dqd, dkt, dw, dattn, dgl = _dn_scan_bwd(do_dn, sp, qd, kt, w, vn, attn, gl)
    dqn, dkn, dv_dn, dbg = _dn_prep_bwd(qn, kn, v_dn, bg, bgt, tri, sel, t_inv, attn, u, w, du, dw, dqd, dkt, dattn,
                                        dgl)
    d_qkv_dn, dba, dsm, d_conv8 = _dn_pre_bwd(dqn, dkn, dv_dn, dbg, proj, conv_w8, alog_l, dtb_l)
    pieces = (d_qkv_dn, dz, dq_att, dk_att, dv_att, dgate, dba)
    d_w_in_t = _in_bwd_dw(pieces, x, norm_w)
    last = functools.partial(_in_bwd_dx, pieces, w_bf, x, dy, norm_w, ds_accs, buckets)

    grads = dict(
        w_in_t=d_w_in_t,
        conv_w=d_conv8[0:CONV_W, :],
        a_log=dsm[0:1, DN_HEADS:2 * DN_HEADS],
        dt_bias=dsm[1:2, DN_HEADS:2 * DN_HEADS],
        dn_norm_w=d_dnw[0:1, :],
        q_norm_w=d_qw[0:1, :].reshape(ATT_HEADS, ATT_HD),
        k_norm_w=d_kw[0:1, :].reshape(ATT_HEADS, ATT_HD),
        w_out=d_w_out,
    )
    return loss_blk[0, 0], last, grads


MESH_ID = pl.DeviceIdType.MESH
ANY = pl.BlockSpec(memory_space=pl.ANY)


def _position():
    return lax.axis_index("x"), lax.axis_index("y"), lax.axis_index("c")


def _other_chips(x, y):
    return [(1 - x, y), (x, 1 - y), (1 - x, 1 - y)]


SHARD_PAD = 1040
WIN = 528


def _row_split(n):
    return (n // 2) // 128 * 128


def _part(ref, cc):
    n = ref.shape[0]
    sp = _row_split(n)
    return ref.at[pl.ds(0, sp)] if cc == 0 else ref.at[pl.ds(sp, n - sp)]


def _halves(ref):
    n = ref.shape[0]
    sp = (n // 2) // 16 * 16
    return ref.at[pl.ds(0, sp)], ref.at[pl.ds(sp, n - sp)]


def _gather_weights(wt_s, w_out_s, conv_s, rel_bias, buckets):
    def body(a_ref, b_ref, c_ref, rb_ref, bk_ref, ga_ref, gb_ref, gc_ref, bias_ref, send_sems, recv_sems, loc_sems,
             a_vmem, b_vmem):
        x, y, c = _position()
        me = 2 * x + y
        sib = (x, y, 1 - c)
        big = ((a_ref, ga_ref), (b_ref, gb_ref))
        stage_in = [pltpu.make_async_copy(a_ref, a_vmem, loc_sems.at[0]),
                    pltpu.make_async_copy(b_ref, b_vmem, loc_sems.at[1])]
        local = [pltpu.make_async_copy(a_vmem, ga_ref.at[me], loc_sems.at[0]),
                 pltpu.make_async_copy(b_vmem, gb_ref.at[me], loc_sems.at[1]),
                 pltpu.make_async_copy(c_ref, gc_ref.at[me], loc_sems.at[2])]
        for cp in stage_in:
            cp.start()
        local[2].start()
        for cp in stage_in:
            cp.wait()
        for cp in local[:2]:
            cp.start()
        others = _other_chips(x, y)

        def exchange(cc):
            nbr = others[:2]
            dg = others[2]
            pending = []

            def copy(k, src, dst, to):
                cp = pltpu.make_async_remote_copy(src_ref=src, dst_ref=dst, send_sem=send_sems.at[k],
                                                  recv_sem=recv_sems.at[k], device_id=to, device_id_type=MESH_ID)
                pending.append(cp)
                cp.start()

            def landed(k, dst):
                pltpu.make_async_remote_copy(src_ref=dst, dst_ref=dst, send_sem=send_sems.at[k],
                                             recv_sem=recv_sems.at[k], device_id=sib, device_id_type=MESH_ID).wait_recv()

            def slot(dst, chip, c_part):
                return _part(dst.at[2 * chip[0] + chip[1]], c_part)

            for i, n in enumerate(nbr):
                for t, (src, dst) in enumerate(big):
                    copy(2 * i + t, _part(src, cc), _part(dst.at[me], cc), (*n, c))
            for j, chip in enumerate(others):
                copy(4 + j, c_ref, gc_ref.at[me], (*chip, c))
            _bias_table(rb_ref, bk_ref, bias_ref)
            for i, n in enumerate(nbr):
                for t, (_, dst) in enumerate(big):
                    got = slot(dst, n, cc)
                    landed(2 * i + t, got)
                    half = _halves(got)[1 - i]
                    copy(7 + 2 * (1 - i) + t, half, half, (*nbr[1 - i], c))
                    copy(11 + 2 * i + t, got, got, sib)
            for i, n in enumerate(nbr):
                for t, (_, dst) in enumerate(big):
                    half = _halves(slot(dst, dg, cc))[i]
                    landed(7 + 2 * i + t, half)
                    copy(15 + 2 * i + t, half, half, sib)
            for j, chip in enumerate(others):
                landed(4 + j, gc_ref.at[2 * chip[0] + chip[1]])
            for i, n in enumerate(nbr):
                for t, (_, dst) in enumerate(big):
                    landed(11 + 2 * i + t, slot(dst, n, 1 - cc))
                    landed(15 + 2 * i + t, _halves(slot(dst, dg, 1 - cc))[i])
            for cp in pending:
                cp.wait_send()

        for cc in (0, 1):
            pl.when(c == cc)(functools.partial(exchange, cc))
        for cp in local:
            cp.wait()

    srcs = (wt_s, w_out_s, conv_s)
    n_sem = 19
    return pl.pallas_call(
        body, in_specs=[ANY] * 3 + [pl.BlockSpec(memory_space=pltpu.SMEM), pl.BlockSpec(memory_space=pltpu.VMEM)],
        out_specs=[ANY] * 3 + [pl.BlockSpec(memory_space=pltpu.VMEM)],
        out_shape=[_sds((N_CHIPS,) + a.shape, a.dtype) for a in srcs]
        + [_sds((len(PATTERNS), ATT_HEADS, BLK, 2 * BLK))],
        scratch_shapes=[pltpu.SemaphoreType.DMA((n_sem,)), pltpu.SemaphoreType.DMA((n_sem,)),
                        pltpu.SemaphoreType.DMA((3,)), pltpu.VMEM(wt_s.shape, wt_s.dtype),
                        pltpu.VMEM(w_out_s.shape, w_out_s.dtype)],
        compiler_params=_cp(), name="gather_weights")(*srcs, rel_bias, buckets)


N_DEV = 8
PEER_FLIPS = [(dx, dy, dc) for dx in (0, 1) for dy in (0, 1) for dc in (0, 1)][1:]


def _small_copies(s_ref, rs_ref, send_sems, recv_sems):
    x, y, c = _position()
    dev = 4 * x + 2 * y + c
    sends, recvs = [], []
    for f, (dx, dy, dc) in enumerate(PEER_FLIPS):
        peer = (x ^ dx, y ^ dy, c ^ dc)
        sends.append(pltpu.make_async_remote_copy(
            src_ref=s_ref, dst_ref=rs_ref.at[dev], send_sem=send_sems.at[f], recv_sem=recv_sems.at[f],
            device_id=peer, device_id_type=MESH_ID))
        recvs.append(pltpu.make_async_remote_copy(
            src_ref=s_ref, dst_ref=rs_ref.at[4 * peer[0] + 2 * peer[1] + peer[2]], send_sem=send_sems.at[f],
            recv_sem=recv_sems.at[f], device_id=peer, device_id_type=MESH_ID))
    return sends, recvs


def _fill_parts(h_in, r_in, h_out, r_out, small, rows_in, rows_out):
    def body(ha_ref, ra_ref, hb_ref, rb_ref, s_ref, fa_ref, fb_ref, rs_ref, send_sems, recv_sems, loc_sems,
             small_send, small_recv, a_vmem, b_vmem, ra_vmem, rb_vmem):
        x, y, c = _position()
        sib = (x, y, 1 - c)
        chip = 2 * x + y
        pairs = ((a_vmem, fa_ref), (b_vmem, fb_ref))
        own_small = pltpu.make_async_copy(s_ref, rs_ref.at[4 * x + 2 * y + c], loc_sems.at[2])
        own_small.start()
        small_sends, small_recvs = _small_copies(s_ref, rs_ref, small_send, small_recv)
        for cp in small_sends:
            cp.start()
        stage_in = [pltpu.make_async_copy(ha_ref.at[chip], a_vmem, loc_sems.at[0]),
                    pltpu.make_async_copy(hb_ref.at[chip], b_vmem, loc_sems.at[1]),
                    pltpu.make_async_copy(ra_ref, ra_vmem, loc_sems.at[3]),
                    pltpu.make_async_copy(rb_ref, rb_vmem, loc_sems.at[4])]
        for cp in stage_in:
            cp.start()
        for cp in stage_in:
            cp.wait()
        for tot, recv in ((a_vmem, ra_vmem), (b_vmem, rb_vmem)):
            acc = tot[...]
            for j in range(3):
                acc = acc + recv[j].astype(F32)
            tot[...] = acc

        def fill(cc):
            mine = [_part(dst, cc) for _, dst in pairs]
            srcs = [src.at[pl.ds(0, m.shape[0])] for src, m in zip((a_vmem, b_vmem), mine)]
            local = [pltpu.make_async_copy(s, m, loc_sems.at[t]) for t, (s, m) in enumerate(zip(srcs, mine))]
            sends = [pltpu.make_async_remote_copy(src_ref=s, dst_ref=m, send_sem=send_sems.at[t],
                                                  recv_sem=recv_sems.at[t], device_id=sib, device_id_type=MESH_ID)
                     for t, (s, m) in enumerate(zip(srcs, mine))]
            for cp in local + sends:
                cp.start()
            for t, (_, dst) in enumerate(pairs):
                theirs = _part(dst, 1 - cc)
                pltpu.make_async_remote_copy(src_ref=theirs, dst_ref=theirs, send_sem=send_sems.at[t],
                                             recv_sem=recv_sems.at[t], device_id=sib, device_id_type=MESH_ID).wait_recv()
            for cp in sends:
                cp.wait_send()
            for cp in local:
                cp.wait()

        for cc in (0, 1):
            pl.when(c == cc)(functools.partial(fill, cc))
        for cp in small_recvs:
            cp.wait_recv()
        for cp in small_sends:
            cp.wait_send()
        own_small.wait()

    n_peer = len(PEER_FLIPS)
    return pl.pallas_call(
        body, in_specs=[ANY] * 5, out_specs=[ANY] * 3,
        out_shape=[_sds((rows_in, D_MODEL)), _sds((rows_out, D_MODEL)), _sds((N_DEV,) + small.shape, small.dtype)],
        scratch_shapes=[pltpu.SemaphoreType.DMA((2,)), pltpu.SemaphoreType.DMA((2,)), pltpu.SemaphoreType.DMA((5,)),
                        pltpu.SemaphoreType.DMA((n_peer,)), pltpu.SemaphoreType.DMA((n_peer,)),
                        pltpu.VMEM(h_in.shape[1:], F32), pltpu.VMEM(h_out.shape[1:], F32),
                        pltpu.VMEM(r_in.shape, r_in.dtype), pltpu.VMEM(r_out.shape, r_out.dtype)],
        compiler_params=_cp(), name="fill_parts")(h_in, r_in, h_out, r_out, small)


def _chip_sums(u_in, u_out, win_in, win_out):
    wins = (win_in, win_out)

    def body(a_ref, b_ref, ha_ref, hba_ref, hb_ref, hbb_ref, send_sems, recv_sems, loc_sems, out_sems,
             mine_a, theirs_a, sum_a, sumb_a, mine_b, theirs_b, sum_b, sumb_b):
        x, y, c = _position()
        sib = (x, y, 1 - c)
        groups = ((a_ref, mine_a, theirs_a, sum_a, sumb_a, ha_ref, hba_ref, win_in),
                  (b_ref, mine_b, theirs_b, sum_b, sumb_b, hb_ref, hbb_ref, win_out))
        loads, sends = [], []
        for t, (src, mine, theirs, _, _, _, _, win) in enumerate(groups):
            split = _row_split(src.shape[1])
            for k in range(N_CHIPS):
                n = N_CHIPS * t + k
                loads.append(pltpu.make_async_copy(
                    src.at[k, pl.ds(pl.multiple_of(c * split, split), win), :], mine.at[k], loc_sems.at[n]))
                sends.append(pltpu.make_async_remote_copy(
                    src_ref=src.at[k, pl.ds(pl.multiple_of((1 - c) * split, split), win), :], dst_ref=theirs.at[k],
                    send_sem=send_sems.at[n], recv_sem=recv_sems.at[n], device_id=sib, device_id_type=MESH_ID))
        for cp in sends + loads:
            cp.start()
        stores = []
        for t, (_, mine, theirs, tot, totb, h_out, hb_out, _) in enumerate(groups):
            for k in range(N_CHIPS):
                n = N_CHIPS * t + k
                loads[n].wait()
                sends[n].wait_recv()
                val = mine[k] + theirs[k]
                tot[k] = val
                totb[k] = val.astype(BF16)
                stores += [pltpu.make_async_copy(tot.at[k], h_out.at[k], out_sems.at[2 * n]),
                           pltpu.make_async_copy(totb.at[k], hb_out.at[k], out_sems.at[2 * n + 1])]
                stores[-2].start()
                stores[-1].start()
        for cp in sends:
            cp.wait_send()
        for cp in stores:
            cp.wait()

    shapes = [(N_CHIPS, w, D_MODEL) for w in wins]
    n_cp = 2 * N_CHIPS
    vmem = []
    for shp in shapes:
        vmem += [pltpu.VMEM(shp, F32), pltpu.VMEM(shp, F32), pltpu.VMEM(shp, F32), pltpu.VMEM(shp, BF16)]
    return pl.pallas_call(
        body, in_specs=[ANY] * 2, out_specs=[ANY] * 4,
        out_shape=[_sds(shapes[0]), _sds(shapes[0], BF16), _sds(shapes[1]), _sds(shapes[1], BF16)],
        scratch_shapes=[pltpu.SemaphoreType.DMA((n_cp,)), pltpu.SemaphoreType.DMA((n_cp,)),
                        pltpu.SemaphoreType.DMA((n_cp,)), pltpu.SemaphoreType.DMA((2 * n_cp,))] + vmem,
        compiler_params=_cp(), name="chip_sums")(u_in, u_out)


SMALL_LAYOUT = (("norm_w", 1024), ("conv_w", 6144), ("a_log", 128), ("dt_bias", 128), ("dn_norm_w", 128),
                ("q_norm_w", 512), ("k_norm_w", 512), ("rel_bias", 256), ("loss", 128))
SMALL_TOTAL = sum(n for _, n in SMALL_LAYOUT)


def _small_offset(name):
    off = 0
    for n, size in SMALL_LAYOUT:
        if n == name:
            return off
        off += size
    raise KeyError(name)


def _pack_small(grads):
    parts = []
    for name, size in SMALL_LAYOUT:
        flat = grads[name].reshape(1, -1)
        parts.append(jnp.pad(flat, ((0, 0), (0, size - flat.shape[1]))))
    return jnp.concatenate(parts, axis=1)


def _sum_small(rows):
    n_dev = rows.shape[0]
    q_off = _small_offset("q_norm_w")
    k_off = _small_offset("k_norm_w")

    def body(r_ref, tot_ref, qk_ref):
        tot = r_ref[0:1, :]
        for d in range(1, n_dev):
            tot = tot + r_ref[d:d + 1, :]
        tot_ref[...] = tot
        for row, off in ((0, q_off), (1, k_off)):
            s4 = tot[:, off:off + 128] + tot[:, off + 128:off + 256] + tot[:, off + 256:off + 384] \
                + tot[:, off + 384:off + 512]
            qk_ref[row:row + 1, :] = s4 + pltpu.roll(s4, ATT_HD, 1)

    return pl.pallas_call(
        body, in_specs=[pl.BlockSpec(memory_space=pltpu.VMEM)],
        out_specs=[pl.BlockSpec(memory_space=pltpu.VMEM)] * 2,
        out_shape=[_sds((1, SMALL_TOTAL)), _sds((2, 128))],
        compiler_params=_cp(), name="sum_small")(rows)


def _adamw_math(w, g, m, v):
    m = ADAM_B1 * m + (1.0 - ADAM_B1) * g
    v = ADAM_B2 * v + (1.0 - ADAM_B2) * (g * g)
    m_hat = m / (1.0 - ADAM_B1 ** ADAM_STEP)
    v_hat = v / (1.0 - ADAM_B2 ** ADAM_STEP)
    delta = -ADAM_LR * (m_hat / (jnp.sqrt(v_hat) + ADAM_EPS) + ADAM_WD * w)
    return delta, m, v


def _adamw_big(g, w, m, v, name):
    rows, cols = w.shape
    tr = 128

    def body(g_ref, w_ref, m_ref, v_ref, go_ref, d_ref, nm_ref, nv_ref):
        g = g_ref[...]
        go_ref[...] = g
        d_ref[...], nm_ref[...], nv_ref[...] = _adamw_math(w_ref[...], g, m_ref[...], v_ref[...])

    blk = pl.BlockSpec((tr, cols), lambda i: (i, 0))
    return pl.pallas_call(
        body, grid=(pl.cdiv(rows, tr),), in_specs=[blk] * 4, out_specs=[blk] * 4,
        out_shape=[_sds((rows, cols))] * 4, compiler_params=_cp(("parallel",)), name=name)(g, w, m, v)


def _adamw_rows(g, w, m, v, name):
    rows, cols = w.shape
    tr = 128

    def body(g_ref, w_ref, m_ref, v_ref, go_ref, d_ref, nm_ref, nv_ref, s_g, s_d, s_m, s_v):
        g = g_ref[...]
        s_g[...] = g
        s_d[...], s_m[...], s_v[...] = _adamw_math(w_ref[...], g, m_ref[...], v_ref[...])
        for i in range(tr):
            for scr, out in ((s_g, go_ref), (s_d, d_ref), (s_m, nm_ref), (s_v, nv_ref)):
                out[i] = scr[i:i + 1, :]

    blk = pl.BlockSpec((tr, cols), lambda i: (i, 0))
    oblk = pl.BlockSpec((tr, 1, cols), lambda i: (i, 0, 0))
    return pl.pallas_call(
        body, grid=(pl.cdiv(rows, tr),), in_specs=[blk] * 4, out_specs=[oblk] * 4,
        out_shape=[_sds((rows, 1, cols))] * 4, scratch_shapes=[pltpu.VMEM((tr, cols), F32)] * 4,
        compiler_params=_cp(("parallel",)), name=name)(g, w, m, v)


def _adamw_small(w, g, m, v, name):
    def body(w_ref, g_ref, m_ref, v_ref, d_ref, nm_ref, nv_ref):
        d_ref[...], nm_ref[...], nv_ref[...] = _adamw_math(w_ref[...], g_ref[...], m_ref[...], v_ref[...])

    vm = pl.BlockSpec(memory_space=pltpu.VMEM)
    return pl.pallas_call(body, in_specs=[vm] * 4, out_specs=[vm] * 3, out_shape=[_sds(w.shape)] * 3,
                          compiler_params=_cp(), name=name)(w, g, m, v)


WEIGHTS = ("norm_w", "w_in", "conv_w", "a_log", "dt_bias", "dn_norm_w", "q_norm_w", "k_norm_w", "rel_bias", "w_out")


def kernel(x, norm_w, w_in, conv_w, a_log, dt_bias, dn_norm_w, q_norm_w, k_norm_w, rel_bias, w_out, loss_target, m_norm_w, m_w_in, m_conv_w, m_a_log, m_dt_bias, m_dn_norm_w, m_q_norm_w, m_k_norm_w, m_rel_bias, m_w_out, v_norm_w, v_w_in, v_conv_w, v_a_log, v_dt_bias, v_dn_norm_w, v_q_norm_w, v_k_norm_w, v_rel_bias, v_w_out):
    xi, yi, _ = _position()
    chip = 2 * xi + yi
    w_loc = dict(norm_w=norm_w, w_in=w_in[0].T, conv_w=conv_w[0], a_log=a_log, dt_bias=dt_bias, dn_norm_w=dn_norm_w,
                 q_norm_w=q_norm_w, k_norm_w=k_norm_w, rel_bias=rel_bias, w_out=w_out[0])
    m_loc = dict(norm_w=m_norm_w, w_in=m_w_in[0].T, conv_w=m_conv_w[0], a_log=m_a_log, dt_bias=m_dt_bias,
                 dn_norm_w=m_dn_norm_w, q_norm_w=m_q_norm_w, k_norm_w=m_k_norm_w, rel_bias=m_rel_bias,
                 w_out=m_w_out[0])
    v_loc = dict(norm_w=v_norm_w, w_in=v_w_in[0].T, conv_w=v_conv_w[0], a_log=v_a_log, dt_bias=v_dt_bias,
                 dn_norm_w=v_dn_norm_w, q_norm_w=v_q_norm_w, k_norm_w=v_k_norm_w, rel_bias=v_rel_bias,
                 w_out=v_w_out[0])

    wt_pad = jnp.pad(w_loc["w_in"].astype(BF16), ((0, SHARD_PAD - D_SHARD), (0, 0)))
    g_in, g_out, g_conv, bias = _gather_weights(wt_pad, w_loc["w_out"].astype(BF16), w_loc["conv_w"], rel_bias,
                                                _bucket_tables())
    wt_full = _pack_w(g_in)
    w_out_full = g_out.reshape(D_MODEL, D_MODEL)
    conv_full = g_conv.transpose(1, 0, 2).reshape(CONV_W, 3 * D_DN)

    loss_local, last_kernel, grads = _local_step(x[0], loss_target[0], norm_w, wt_full, conv_full, a_log, dt_bias,
                                                 dn_norm_w, q_norm_w, k_norm_w, bias, w_out_full)
    grads["loss"] = loss_local

    u_in = _unpack_w(grads["w_in_t"], SHARD_PAD)
    u_out = grads["w_out"].reshape(N_CHIPS, D_MODEL // N_CHIPS, D_MODEL)
    win_out = u_out.shape[1] // 2
    h_in, hb_in, h_out, hb_out = _chip_sums(u_in, u_out, WIN, win_out)
    grad_x, d_nw8, grads["rel_bias"], r_in, r_out = last_kernel(chip_sums=(hb_in, hb_out))
    grads["norm_w"] = d_nw8[0:1, :]
    small = _pack_small(grads)
    full_in, full_out, r_small = _fill_parts(h_in, r_in, h_out, r_out, small, SHARD_PAD, u_out.shape[1])
    tot_small, qk = _sum_small(r_small.reshape(8, SMALL_TOTAL))

    def small_grad(name, n):
        off = _small_offset(name)
        return tot_small[:, off:off + n]

    loss = small_grad("loss", 1).reshape(())
    conv_all = small_grad("conv_w", CONV_W * 3 * D_DN).reshape(CONV_W, 3 * D_DN)
    g_small = dict(
        norm_w=small_grad("norm_w", D_MODEL),
        conv_w=lax.dynamic_slice_in_dim(conv_all, chip * (3 * D_DN // N_CHIPS), 3 * D_DN // N_CHIPS, axis=1),
        a_log=small_grad("a_log", DN_HEADS),
        dt_bias=small_grad("dt_bias", DN_HEADS),
        dn_norm_w=small_grad("dn_norm_w", DN_HD),
        q_norm_w=qk[0:1, 0:ATT_HD],
        k_norm_w=qk[1:2, 0:ATT_HD],
        rel_bias=small_grad("rel_bias", ATT_HEADS * N_BUCKETS).reshape(ATT_HEADS, N_BUCKETS),
    )

    out_g, out_d, out_m, out_v = {}, {}, {}, {}
    out_g["w_in"], out_d["w_in"], out_m["w_in"], out_v["w_in"] = _adamw_rows(
        full_in, w_loc["w_in"], m_loc["w_in"], v_loc["w_in"], "adamw_w_in")
    out_g["w_out"], out_d["w_out"], out_m["w_out"], out_v["w_out"] = _adamw_big(
        full_out, w_loc["w_out"], m_loc["w_out"], v_loc["w_out"], "adamw_w_out")
    for name in g_small:
        out_g[name] = g_small[name]
        out_d[name], out_m[name], out_v[name] = _adamw_small(w_loc[name], g_small[name], m_loc[name], v_loc[name],
                                                             "adamw_" + name)
    for d in (out_g, out_d, out_m, out_v):
        d["w_in"] = d["w_in"].transpose(1, 2, 0)
        for name in ("conv_w", "w_out"):
            d[name] = d[name][None]
    return (loss, grad_x[None], *[out_g[n] for n in WEIGHTS], *[out_d[n] for n in WEIGHTS],
            *[out_m[n] for n in WEIGHTS], *[out_v[n] for n in WEIGHTS])
```
